```python
import math
import jax, jax.numpy as jnp
from jax import lax
import numpy as np

D_MODEL = 1024
BATCH = 8
SEQ = 4096
DEPTH = 1

CHUNK = 64
N_META = 16
DN_HEADS = 8
DN_DK = 128
DN_DV = 256
DN_CONV = 4
DN_QK = DN_HEADS * DN_DK
DN_V = DN_HEADS * DN_DV
SB_HEADS = 8
SB_DH = 128
SB_W = SB_HEADS * SB_DH
SB_BLOCK = 128
D_FF = -(-8 * D_MODEL // (3 * 256)) * 256
PROJ_WIDTH = 2 * DN_QK + 2 * DN_V + 2 * DN_HEADS + 3 * SB_W + 2 * D_MODEL
RMS_EPS = 1e-6
L2_EPS = 1e-6

kernel_name = 'hybrid_gdn_stickbreak_block'


def _split_points():
    widths = (DN_QK, DN_QK, DN_V, DN_V, DN_HEADS, DN_HEADS, SB_W, SB_W, SB_W, D_MODEL, D_MODEL)
    return [int(s) for s in np.cumsum(widths)[:-1]]


def rmsnorm(x, gain):
    xf = x.astype(jnp.float32)
    y = xf * lax.rsqrt(jnp.mean(xf * xf, axis=-1, keepdims=True) + RMS_EPS)
    return (y * gain.astype(jnp.float32)).astype(x.dtype)


def l2norm(x):
    xf = x.astype(jnp.float32)
    return xf * lax.rsqrt(jnp.sum(xf * xf, axis=-1, keepdims=True) + L2_EPS)


def causal_depthwise_conv(x, w):
    K, C = w.shape
    return lax.conv_general_dilated(
        x, w[:, None, :].astype(x.dtype), window_strides=(1,), padding=[(K - 1, 0)],
        dimension_numbers=('NWC', 'WIO', 'NWC'), feature_group_count=C)


def gated_delta_rule(q, k, v, g, beta):
    B, T, H, DK = q.shape
    DV = v.shape[-1]
    N = T // CHUNK
    f32 = jnp.float32

    def to_chunks(a):
        a = a.astype(f32).reshape((B, N, CHUNK, H) + a.shape[3:])
        return jnp.moveaxis(a, (1, 3), (0, 2))

    q = to_chunks(q) * (DK ** -0.5)
    k = to_chunks(k)
    v = to_chunks(v)
    beta = to_chunks(beta)
    g = jnp.cumsum(to_chunks(g), axis=-1)
    idx = jnp.arange(CHUNK)
    incl = idx[:, None] >= idx[None, :]
    strict = idx[:, None] > idx[None, :]
    decay = jnp.exp(jnp.where(incl, g[..., :, None] - g[..., None, :], -jnp.inf))
    kb = k * beta[..., None]
    lower = jnp.where(strict, jnp.einsum('nbhid,nbhjd->nbhij', kb, k) * decay, 0.0)
    eye = jnp.eye(CHUNK, dtype=f32)
    rhs = jnp.concatenate([v * beta[..., None], kb * jnp.exp(g)[..., None]], axis=-1)
    sol = lax.linalg.triangular_solve(eye + lower, rhs, left_side=True, lower=True)
    u, w = sol[..., :DV], sol[..., DV:]
    attn = jnp.einsum('nbhid,nbhjd->nbhij', q, k) * decay
    q_dec = q * jnp.exp(g)[..., None]
    k_dec = k * jnp.exp(g[..., -1:] - g)[..., None]
    g_last = jnp.exp(g[..., -1])

    def step(S, xs):
        u_c, w_c, attn_c, qd_c, kd_c, gl_c = xs
        v_new = u_c - jnp.einsum('bhcd,bhde->bhce', w_c, S)
        o = jnp.einsum('bhcd,bhde->bhce', qd_c, S) + jnp.einsum('bhij,bhje->bhie', attn_c, v_new)
        S = S * gl_c[..., None, None] + jnp.einsum('bhcd,bhce->bhde', kd_c, v_new)
        return S, o

    S0 = jnp.zeros((B, H, DK, DV), f32)
    _, o = lax.scan(step, S0, (u, w, attn, q_dec, k_dec, g_last))
    return jnp.moveaxis(o, (0, 2), (1, 3)).reshape(B, T, H, DV)


def stick_breaking_attention(q, k, v):
    B, T, H, D = q.shape
    nq = -(-T // SB_BLOCK)
    Tp = nq * SB_BLOCK
    pad = ((0, 0), (0, Tp - T), (0, 0), (0, 0))
    f32 = jnp.float32
    qh = jnp.pad(q.astype(f32), pad).reshape(B, nq, SB_BLOCK, H, D).transpose(1, 0, 3, 2, 4)
    kh = jnp.pad(k.astype(f32), pad).transpose(0, 2, 1, 3)
    vh = jnp.pad(v.astype(f32), pad).transpose(0, 2, 1, 3)
    key_pos = jnp.arange(Tp)
    scale = D ** -0.5

    def block(args):
        q_blk, start = args
        z = jnp.einsum('bhqd,bhkd->bhqk', q_blk, kh) * scale
        q_pos = start + jnp.arange(SB_BLOCK)
        visible = key_pos[None, :] < q_pos[:, None]
        log_keep = jnp.where(visible, jax.nn.log_sigmoid(-z), 0.0)
        log_w = jax.nn.log_sigmoid(z) + lax.cumsum(log_keep, axis=3, reverse=True) - log_keep
        w = jnp.where(visible, jnp.exp(log_w), 0.0)
        return jnp.einsum('bhqk,bhkd->bhqd', w, vh)

    o = lax.map(block, (qh, jnp.arange(nq) * SB_BLOCK))
    return o.transpose(1, 0, 3, 2, 4).reshape(B, Tp, H, D)[:, :T]


def hybrid_mixer(hn, w_in, conv_q, conv_k, conv_v, a_log, dt_bias, dn_gain, sbq_gain, sbk_gain,
                 w_branch_dn, w_branch_sb, w_out):
    B, T, _ = hn.shape
    proj = hn @ w_in
    (dq, dk, dv, dz, da, db, sq, sk, sv, gate_dn, gate_sb) = jnp.split(proj, _split_points(), axis=-1)

    def heads(a, d):
        return a.reshape(B, T, -1, d)

    q = l2norm(heads(jax.nn.silu(causal_depthwise_conv(dq, conv_q)), DN_DK))
    k = l2norm(heads(jax.nn.silu(causal_depthwise_conv(dk, conv_k)), DN_DK))
    v = heads(jax.nn.silu(causal_depthwise_conv(dv, conv_v)), DN_DV)
    g = -jnp.exp(a_log.astype(jnp.float32)) * jax.nn.softplus(da.astype(jnp.float32) + dt_bias.astype(jnp.float32))
    beta = jax.nn.sigmoid(db.astype(jnp.float32))
    pad_l = (-N_META) % CHUNK
    pad_r = (-(T + pad_l)) % CHUNK

    def chunk_pad(a):
        return jnp.pad(a, ((0, 0), (pad_l, pad_r)) + ((0, 0),) * (a.ndim - 2))

    o_dn = gated_delta_rule(chunk_pad(q), chunk_pad(k), chunk_pad(v), chunk_pad(g), chunk_pad(beta))
    o_dn = o_dn[:, pad_l:pad_l + T]
    o_dn = rmsnorm(o_dn, dn_gain) * jax.nn.silu(heads(dz, DN_DV).astype(jnp.float32))
    o_dn = o_dn.astype(hn.dtype).reshape(B, T, DN_V)

    qs = rmsnorm(heads(sq, SB_DH), sbq_gain)
    ks = rmsnorm(heads(sk, SB_DH), sbk_gain)
    o_sb = stick_breaking_attention(qs, ks, heads(sv, SB_DH)).astype(hn.dtype).reshape(B, T, SB_W)

    merged = jax.nn.sigmoid(gate_dn) * (o_dn @ w_branch_dn) + jax.nn.sigmoid(gate_sb) * (o_sb @ w_branch_sb)
    return merged @ w_out


def swiglu(hn, w_ffn_in, w_ffn_out):
    gate, up = jnp.split(hn @ w_ffn_in, 2, axis=-1)
    return (jax.nn.silu(gate) * up) @ w_ffn_out


def _fwd_setup_inputs(seed: int = 0) -> dict:
    key = jax.random.key(seed)
    ks = jax.random.split(key, 20)
    f32 = jnp.float32

    def nrm(k, shape, fan_in):
        return jax.random.normal(k, shape, f32) * (fan_in ** -0.5)

    def gain(k, n):
        return 1.0 + 0.02 * jax.random.normal(k, (DEPTH, n), f32)

    dt = jnp.exp(jax.random.uniform(ks[7], (DEPTH, DN_HEADS), f32) * (math.log(0.1) - math.log(1e-3)) + math.log(1e-3))
    return {
        'x': jax.random.normal(ks[0], (BATCH, SEQ, D_MODEL), f32),
        'meta_tokens': jax.random.normal(ks[1], (N_META, D_MODEL), f32),
        'norm_mix_gain': gain(ks[2], D_MODEL),
        'w_in': nrm(ks[3], (DEPTH, D_MODEL, PROJ_WIDTH), D_MODEL),
        'conv_q': nrm(ks[4], (DEPTH, DN_CONV, DN_QK), DN_CONV),
        'conv_k': nrm(ks[5], (DEPTH, DN_CONV, DN_QK), DN_CONV),
        'conv_v': nrm(ks[6], (DEPTH, DN_CONV, DN_V), DN_CONV),
        'dn_a_log': jnp.log(jax.random.uniform(ks[8], (DEPTH, DN_HEADS), f32, 1.0, 16.0)),
        'dn_dt_bias': dt + jnp.log(-jnp.expm1(-dt)),
        'dn_out_norm_gain': gain(ks[9], DN_DV),
        'sb_q_norm_gain': gain(ks[10], SB_DH),
        'sb_k_norm_gain': gain(ks[11], SB_DH),
        'w_branch_dn': nrm(ks[12], (DEPTH, DN_V, D_MODEL), DN_V),
        'w_branch_sb': nrm(ks[13], (DEPTH, SB_W, D_MODEL), SB_W),
        'w_out': nrm(ks[14], (DEPTH, D_MODEL, D_MODEL), D_MODEL),
        'norm_ffn_gain': gain(ks[15], D_MODEL),
        'w_ffn_in': nrm(ks[16], (DEPTH, D_MODEL, 2 * D_FF), D_MODEL),
        'w_ffn_out': nrm(ks[17], (DEPTH, D_FF, D_MODEL), D_FF),
    }


def _fwd_reference(x, meta_tokens, norm_mix_gain, w_in, conv_q, conv_k, conv_v, dn_a_log, dn_dt_bias,
              dn_out_norm_gain, sb_q_norm_gain, sb_k_norm_gain, w_branch_dn, w_branch_sb, w_out,
              norm_ffn_gain, w_ffn_in, w_ffn_out):
    B = x.shape[0]
    meta = jnp.broadcast_to(meta_tokens[None].astype(x.dtype), (B, N_META, D_MODEL))
    h = jnp.concatenate([meta, x], axis=1)
    for l in range(DEPTH):
        h = h + hybrid_mixer(rmsnorm(h, norm_mix_gain[l]), w_in[l], conv_q[l], conv_k[l], conv_v[l],
                             dn_a_log[l], dn_dt_bias[l], dn_out_norm_gain[l], sb_q_norm_gain[l],
                             sb_k_norm_gain[l], w_branch_dn[l], w_branch_sb[l], w_out[l])
        h = h + swiglu(rmsnorm(h, norm_ffn_gain[l]), w_ffn_in[l], w_ffn_out[l])
    return h[:, N_META:]


import jax as _jax
import jax.numpy as _jnp

TWIN_FORMAT = 'train_step'
FWD_PARAMS = ['x', 'meta_tokens', 'norm_mix_gain', 'w_in', 'conv_q', 'conv_k', 'conv_v', 'dn_a_log', 'dn_dt_bias', 'dn_out_norm_gain', 'sb_q_norm_gain', 'sb_k_norm_gain', 'w_branch_dn', 'w_branch_sb', 'w_out', 'norm_ffn_gain', 'w_ffn_in', 'w_ffn_out']
TWIN_WEIGHTS = ['meta_tokens', 'norm_mix_gain', 'w_in', 'conv_q', 'conv_k', 'conv_v', 'dn_a_log', 'dn_dt_bias', 'dn_out_norm_gain', 'sb_q_norm_gain', 'sb_k_norm_gain', 'w_branch_dn', 'w_branch_sb', 'w_out', 'norm_ffn_gain', 'w_ffn_in', 'w_ffn_out']
TWIN_DIFF_INPUT = 'x'
TWIN_INPUTS = ['x', 'meta_tokens', 'norm_mix_gain', 'w_in', 'conv_q', 'conv_k', 'conv_v', 'dn_a_log', 'dn_dt_bias', 'dn_out_norm_gain', 'sb_q_norm_gain', 'sb_k_norm_gain', 'w_branch_dn', 'w_branch_sb', 'w_out', 'norm_ffn_gain', 'w_ffn_in', 'w_ffn_out', 'loss_target', 'm_meta_tokens', 'm_norm_mix_gain', 'm_w_in', 'm_conv_q', 'm_conv_k', 'm_conv_v', 'm_dn_a_log', 'm_dn_dt_bias', 'm_dn_out_norm_gain', 'm_sb_q_norm_gain', 'm_sb_k_norm_gain', 'm_w_branch_dn', 'm_w_branch_sb', 'm_w_out', 'm_norm_ffn_gain', 'm_w_ffn_in', 'm_w_ffn_out', 'v_meta_tokens', 'v_norm_mix_gain', 'v_w_in', 'v_conv_q', 'v_conv_k', 'v_conv_v', 'v_dn_a_log', 'v_dn_dt_bias', 'v_dn_out_norm_gain', 'v_sb_q_norm_gain', 'v_sb_k_norm_gain', 'v_w_branch_dn', 'v_w_branch_sb', 'v_w_out', 'v_norm_ffn_gain', 'v_w_ffn_in', 'v_w_ffn_out']
TWIN_OUTPUTS = ['loss', 'grad_x', 'grad_meta_tokens', 'grad_norm_mix_gain', 'grad_w_in', 'grad_conv_q', 'grad_conv_k', 'grad_conv_v', 'grad_dn_a_log', 'grad_dn_dt_bias', 'grad_dn_out_norm_gain', 'grad_sb_q_norm_gain', 'grad_sb_k_norm_gain', 'grad_w_branch_dn', 'grad_w_branch_sb', 'grad_w_out', 'grad_norm_ffn_gain', 'grad_w_ffn_in', 'grad_w_ffn_out', 'delta_meta_tokens', 'delta_norm_mix_gain', 'delta_w_in', 'delta_conv_q', 'delta_conv_k', 'delta_conv_v', 'delta_dn_a_log', 'delta_dn_dt_bias', 'delta_dn_out_norm_gain', 'delta_sb_q_norm_gain', 'delta_sb_k_norm_gain', 'delta_w_branch_dn', 'delta_w_branch_sb', 'delta_w_out', 'delta_norm_ffn_gain', 'delta_w_ffn_in', 'delta_w_ffn_out', 'new_m_meta_tokens', 'new_m_norm_mix_gain', 'new_m_w_in', 'new_m_conv_q', 'new_m_conv_k', 'new_m_conv_v', 'new_m_dn_a_log', 'new_m_dn_dt_bias', 'new_m_dn_out_norm_gain', 'new_m_sb_q_norm_gain', 'new_m_sb_k_norm_gain', 'new_m_w_branch_dn', 'new_m_w_branch_sb', 'new_m_w_out', 'new_m_norm_ffn_gain', 'new_m_w_ffn_in', 'new_m_w_ffn_out', 'new_v_meta_tokens', 'new_v_norm_mix_gain', 'new_v_w_in', 'new_v_conv_q', 'new_v_conv_k', 'new_v_conv_v', 'new_v_dn_a_log', 'new_v_dn_dt_bias', 'new_v_dn_out_norm_gain', 'new_v_sb_q_norm_gain', 'new_v_sb_k_norm_gain', 'new_v_w_branch_dn', 'new_v_w_branch_sb', 'new_v_w_out', 'new_v_norm_ffn_gain', 'new_v_w_ffn_in', 'new_v_w_ffn_out']
TWIN_LEAF_KINDS = {'loss': 'loss', 'grad_x': 'grad_x', 'grad_meta_tokens': 'grad_w', 'grad_norm_mix_gain': 'grad_w', 'grad_w_in': 'grad_w', 'grad_conv_q': 'grad_w', 'grad_conv_k': 'grad_w', 'grad_conv_v': 'grad_w', 'grad_dn_a_log': 'grad_w', 'grad_dn_dt_bias': 'grad_w', 'grad_dn_out_norm_gain': 'grad_w', 'grad_sb_q_norm_gain': 'grad_w', 'grad_sb_k_norm_gain': 'grad_w', 'grad_w_branch_dn': 'grad_w', 'grad_w_branch_sb': 'grad_w', 'grad_w_out': 'grad_w', 'grad_norm_ffn_gain': 'grad_w', 'grad_w_ffn_in': 'grad_w', 'grad_w_ffn_out': 'grad_w', 'delta_meta_tokens': 'delta_w', 'delta_norm_mix_gain': 'delta_w', 'delta_w_in': 'delta_w', 'delta_conv_q': 'delta_w', 'delta_conv_k': 'delta_w', 'delta_conv_v': 'delta_w', 'delta_dn_a_log': 'delta_w', 'delta_dn_dt_bias': 'delta_w', 'delta_dn_out_norm_gain': 'delta_w', 'delta_sb_q_norm_gain': 'delta_w', 'delta_sb_k_norm_gain': 'delta_w', 'delta_w_branch_dn': 'delta_w', 'delta_w_branch_sb': 'delta_w', 'delta_w_out': 'delta_w', 'delta_norm_ffn_gain': 'delta_w', 'delta_w_ffn_in': 'delta_w', 'delta_w_ffn_out': 'delta_w', 'new_m_meta_tokens': 'new_m', 'new_m_norm_mix_gain': 'new_m', 'new_m_w_in': 'new_m', 'new_m_conv_q': 'new_m', 'new_m_conv_k': 'new_m', 'new_m_conv_v': 'new_m', 'new_m_dn_a_log': 'new_m', 'new_m_dn_dt_bias': 'new_m', 'new_m_dn_out_norm_gain': 'new_m', 'new_m_sb_q_norm_gain': 'new_m', 'new_m_sb_k_norm_gain': 'new_m', 'new_m_w_branch_dn': 'new_m', 'new_m_w_branch_sb': 'new_m', 'new_m_w_out': 'new_m', 'new_m_norm_ffn_gain': 'new_m', 'new_m_w_ffn_in': 'new_m', 'new_m_w_ffn_out': 'new_m', 'new_v_meta_tokens': 'new_v', 'new_v_norm_mix_gain': 'new_v', 'new_v_w_in': 'new_v', 'new_v_conv_q': 'new_v', 'new_v_conv_k': 'new_v', 'new_v_conv_v': 'new_v', 'new_v_dn_a_log': 'new_v', 'new_v_dn_dt_bias': 'new_v', 'new_v_dn_out_norm_gain': 'new_v', 'new_v_sb_q_norm_gain': 'new_v', 'new_v_sb_k_norm_gain': 'new_v', 'new_v_w_branch_dn': 'new_v', 'new_v_w_branch_sb': 'new_v', 'new_v_w_out': 'new_v', 'new_v_norm_ffn_gain': 'new_v', 'new_v_w_ffn_in': 'new_v', 'new_v_w_ffn_out': 'new_v'}


def _forward(args):
    return _fwd_reference(*[args[k] for k in FWD_PARAMS])


def _output_shape():
    out = _jax.eval_shape(lambda: _forward(_fwd_setup_inputs(0)))
    return out.shape, out.dtype

N_MICROBATCH = 1
ADAM_LR = 0.001
ADAM_B1 = 0.9
ADAM_B2 = 0.999
ADAM_EPS = 1e-08
ADAM_WD = 0.01
ADAM_STEP = 10
PER_EXAMPLE_BATCH_AXIS = {'x': 0, 'loss_target': 0}
SHARED_INPUTS = []
_WEIGHT_DTYPES = {'meta_tokens': _jnp.float32, 'norm_mix_gain': _jnp.float32, 'w_in': _jnp.float32, 'conv_q': _jnp.float32, 'conv_k': _jnp.float32, 'conv_v': _jnp.float32, 'dn_a_log': _jnp.float32, 'dn_dt_bias': _jnp.float32, 'dn_out_norm_gain': _jnp.float32, 'sb_q_norm_gain': _jnp.float32, 'sb_k_norm_gain': _jnp.float32, 'w_branch_dn': _jnp.float32, 'w_branch_sb': _jnp.float32, 'w_out': _jnp.float32, 'norm_ffn_gain': _jnp.float32, 'w_ffn_in': _jnp.float32, 'w_ffn_out': _jnp.float32}
MOMENT_SCALE = {'meta_tokens': 9.237102e-03, 'norm_mix_gain': 8.463995e+00, 'w_in': 1.036013e-01, 'conv_q': 9.430549e-02, 'conv_k': 9.849154e-02, 'conv_v': 1.826503e-01, 'dn_a_log': 2.708920e+00, 'dn_dt_bias': 2.625840e+00, 'dn_out_norm_gain': 1.398397e+01, 'sb_q_norm_gain': 4.633899e+00, 'sb_k_norm_gain': 4.622703e+00, 'w_branch_dn': 3.520362e-01, 'w_branch_sb': 1.840461e-01, 'w_out': 3.736660e-01, 'norm_ffn_gain': 2.486582e+01, 'w_ffn_in': 1.822878e-01, 'w_ffn_out': 2.555185e-01}


def _to_microbatches(a, axis):
    t = _jnp.moveaxis(a, axis, 0)
    t = t.reshape((N_MICROBATCH, t.shape[0] // N_MICROBATCH) + t.shape[1:])
    return _jnp.moveaxis(t, 1, axis + 1)


def setup_inputs(seed: int = 0) -> dict:
    inp = _fwd_setup_inputs(seed)
    key = _jax.random.fold_in(_jax.random.key(seed), 7919)
    shape, _ = _output_shape()
    out = dict(inp)
    out["loss_target"] = _jax.random.normal(_jax.random.fold_in(key, 0), shape, _jnp.float32)
    for i, name in enumerate(TWIN_WEIGHTS):
        w = inp[name].astype(_jnp.float32)
        if MOMENT_SCALE is None:
            s = _jnp.sqrt(_jnp.mean(_jnp.square(w)) + 1e-30)
        else:
            s = MOMENT_SCALE[name]
        km, kv = _jax.random.split(_jax.random.fold_in(key, i + 1))
        out[name] = w
        out["m_" + name] = s * _jax.random.normal(km, w.shape, _jnp.float32)
        out["v_" + name] = (s * s) * _jax.random.uniform(kv, w.shape, _jnp.float32, 0.5, 1.5)
    if N_MICROBATCH > 1:
        for name, axis in PER_EXAMPLE_BATCH_AXIS.items():
            out[name] = _to_microbatches(out[name], axis)
    return {'x': out['x'], 'meta_tokens': out['meta_tokens'], 'norm_mix_gain': out['norm_mix_gain'], 'w_in': out['w_in'], 'conv_q': out['conv_q'], 'conv_k': out['conv_k'], 'conv_v': out['conv_v'], 'dn_a_log': out['dn_a_log'], 'dn_dt_bias': out['dn_dt_bias'], 'dn_out_norm_gain': out['dn_out_norm_gain'], 'sb_q_norm_gain': out['sb_q_norm_gain'], 'sb_k_norm_gain': out['sb_k_norm_gain'], 'w_branch_dn': out['w_branch_dn'], 'w_branch_sb': out['w_branch_sb'], 'w_out': out['w_out'], 'norm_ffn_gain': out['norm_ffn_gain'], 'w_ffn_in': out['w_ffn_in'], 'w_ffn_out': out['w_ffn_out'], 'loss_target': out['loss_target'], 'm_meta_tokens': out['m_meta_tokens'], 'm_norm_mix_gain': out['m_norm_mix_gain'], 'm_w_in': out['m_w_in'], 'm_conv_q': out['m_conv_q'], 'm_conv_k': out['m_conv_k'], 'm_conv_v': out['m_conv_v'], 'm_dn_a_log': out['m_dn_a_log'], 'm_dn_dt_bias': out['m_dn_dt_bias'], 'm_dn_out_norm_gain': out['m_dn_out_norm_gain'], 'm_sb_q_norm_gain': out['m_sb_q_norm_gain'], 'm_sb_k_norm_gain': out['m_sb_k_norm_gain'], 'm_w_branch_dn': out['m_w_branch_dn'], 'm_w_branch_sb': out['m_w_branch_sb'], 'm_w_out': out['m_w_out'], 'm_norm_ffn_gain': out['m_norm_ffn_gain'], 'm_w_ffn_in': out['m_w_ffn_in'], 'm_w_ffn_out': out['m_w_ffn_out'], 'v_meta_tokens': out['v_meta_tokens'], 'v_norm_mix_gain': out['v_norm_mix_gain'], 'v_w_in': out['v_w_in'], 'v_conv_q': out['v_conv_q'], 'v_conv_k': out['v_conv_k'], 'v_conv_v': out['v_conv_v'], 'v_dn_a_log': out['v_dn_a_log'], 'v_dn_dt_bias': out['v_dn_dt_bias'], 'v_dn_out_norm_gain': out['v_dn_out_norm_gain'], 'v_sb_q_norm_gain': out['v_sb_q_norm_gain'], 'v_sb_k_norm_gain': out['v_sb_k_norm_gain'], 'v_w_branch_dn': out['v_w_branch_dn'], 'v_w_branch_sb': out['v_w_branch_sb'], 'v_w_out': out['v_w_out'], 'v_norm_ffn_gain': out['v_norm_ffn_gain'], 'v_w_ffn_in': out['v_w_ffn_in'], 'v_w_ffn_out': out['v_w_ffn_out']}


def _loss(weights, diff, rest, loss_target):
    with _jax.named_scope("forward"):
        args = {**rest, TWIN_DIFF_INPUT: diff, **{k: w.astype(_WEIGHT_DTYPES[k]) for k, w in weights.items()}}
        y = _forward(args)
    with _jax.named_scope("loss_head"):
        err = _jnp.square(y.astype(_jnp.float32) - loss_target)
        return 0.5 * _jnp.sum(_jnp.mean(err, axis=-1)) if err.ndim else 0.5 * err


def _adamw(w, g, m, v):
    m = ADAM_B1 * m + (1.0 - ADAM_B1) * g
    v = ADAM_B2 * v + (1.0 - ADAM_B2) * _jnp.square(g)
    m_hat = m / (1.0 - ADAM_B1 ** ADAM_STEP)
    v_hat = v / (1.0 - ADAM_B2 ** ADAM_STEP)
    delta = -ADAM_LR * (m_hat / (_jnp.sqrt(v_hat) + ADAM_EPS) + ADAM_WD * w)
    return delta, m, v


def reference(x, meta_tokens, norm_mix_gain, w_in, conv_q, conv_k, conv_v, dn_a_log, dn_dt_bias, dn_out_norm_gain, sb_q_norm_gain, sb_k_norm_gain, w_branch_dn, w_branch_sb, w_out, norm_ffn_gain, w_ffn_in, w_ffn_out, loss_target, m_meta_tokens, m_norm_mix_gain, m_w_in, m_conv_q, m_conv_k, m_conv_v, m_dn_a_log, m_dn_dt_bias, m_dn_out_norm_gain, m_sb_q_norm_gain, m_sb_k_norm_gain, m_w_branch_dn, m_w_branch_sb, m_w_out, m_norm_ffn_gain, m_w_ffn_in, m_w_ffn_out, v_meta_tokens, v_norm_mix_gain, v_w_in, v_conv_q, v_conv_k, v_conv_v, v_dn_a_log, v_dn_dt_bias, v_dn_out_norm_gain, v_sb_q_norm_gain, v_sb_k_norm_gain, v_w_branch_dn, v_w_branch_sb, v_w_out, v_norm_ffn_gain, v_w_ffn_in, v_w_ffn_out):
    given = dict(x=x, meta_tokens=meta_tokens, norm_mix_gain=norm_mix_gain, w_in=w_in, conv_q=conv_q, conv_k=conv_k, conv_v=conv_v, dn_a_log=dn_a_log, dn_dt_bias=dn_dt_bias, dn_out_norm_gain=dn_out_norm_gain, sb_q_norm_gain=sb_q_norm_gain, sb_k_norm_gain=sb_k_norm_gain, w_branch_dn=w_branch_dn, w_branch_sb=w_branch_sb, w_out=w_out, norm_ffn_gain=norm_ffn_gain, w_ffn_in=w_ffn_in, w_ffn_out=w_ffn_out, loss_target=loss_target, m_meta_tokens=m_meta_tokens, m_norm_mix_gain=m_norm_mix_gain, m_w_in=m_w_in, m_conv_q=m_conv_q, m_conv_k=m_conv_k, m_conv_v=m_conv_v, m_dn_a_log=m_dn_a_log, m_dn_dt_bias=m_dn_dt_bias, m_dn_out_norm_gain=m_dn_out_norm_gain, m_sb_q_norm_gain=m_sb_q_norm_gain, m_sb_k_norm_gain=m_sb_k_norm_gain, m_w_branch_dn=m_w_branch_dn, m_w_branch_sb=m_w_branch_sb, m_w_out=m_w_out, m_norm_ffn_gain=m_norm_ffn_gain, m_w_ffn_in=m_w_ffn_in, m_w_ffn_out=m_w_ffn_out, v_meta_tokens=v_meta_tokens, v_norm_mix_gain=v_norm_mix_gain, v_w_in=v_w_in, v_conv_q=v_conv_q, v_conv_k=v_conv_k, v_conv_v=v_conv_v, v_dn_a_log=v_dn_a_log, v_dn_dt_bias=v_dn_dt_bias, v_dn_out_norm_gain=v_dn_out_norm_gain, v_sb_q_norm_gain=v_sb_q_norm_gain, v_sb_k_norm_gain=v_sb_k_norm_gain, v_w_branch_dn=v_w_branch_dn, v_w_branch_sb=v_w_branch_sb, v_w_out=v_w_out, v_norm_ffn_gain=v_norm_ffn_gain, v_w_ffn_in=v_w_ffn_in, v_w_ffn_out=v_w_ffn_out)
    weights = {n: given[n] for n in TWIN_WEIGHTS}
    shared = {n: given[n] for n in SHARED_INPUTS}
    per_example = {n: given[n] for n in ['x']}
    grad_fn = _jax.value_and_grad(_loss, argnums=(0, 1))

    def one_microbatch(ex, loss_target):
        ex = dict(ex)
        diff = ex.pop(TWIN_DIFF_INPUT)
        return grad_fn(weights, diff, {**shared, **ex}, loss_target)

    if N_MICROBATCH == 1:
        loss, (grad_w, grad_x) = one_microbatch(per_example, given["loss_target"])
    else:
        def body(carry, xs):
            loss_sum, grad_sum = carry
            l_k, (gw_k, gx_k) = one_microbatch(xs[0], xs[1])
            with _jax.named_scope("update"):
                return (loss_sum + l_k, _jax.tree.map(_jnp.add, grad_sum, gw_k)), gx_k

        init = (_jnp.zeros((), _jnp.float32), _jax.tree.map(_jnp.zeros_like, weights))
        (loss, grad_w), grad_x = _jax.lax.scan(body, init, (per_example, given["loss_target"]))
    with _jax.named_scope("update"):
        delta_w, new_m, new_v = {}, {}, {}
        for n in TWIN_WEIGHTS:
            delta_w[n], new_m[n], new_v[n] = _adamw(weights[n], grad_w[n], given["m_" + n], given["v_" + n])
    return (loss, grad_x, *[grad_w[n] for n in TWIN_WEIGHTS], *[delta_w[n] for n in TWIN_WEIGHTS],
            *[new_m[n] for n in TWIN_WEIGHTS], *[new_v[n] for n in TWIN_WEIGHTS])
```

```python
import jax
import jax.numpy as jnp
from jax import lax
from jax.experimental import pallas as pl
from jax.experimental.pallas import tpu as pltpu

f32 = jnp.float32
bf16 = jnp.bfloat16

D_MODEL = 1024
N_META = 16
CHUNK = 64
HEADS = 8
DN_DK = 128
DN_DV = 256
DN_CONV = 4
DN_QK = HEADS * DN_DK
DN_V = HEADS * DN_DV
SB_DH = 128
SB_W = HEADS * SB_DH
SB_BLOCK = 128
D_FF = 2816
RMS_EPS = 1e-6
L2_EPS = 1e-6
ADAM_LR = 0.001
ADAM_B1 = 0.9
ADAM_B2 = 0.999
ADAM_EPS = 1e-08
ADAM_WD = 0.01
ADAM_STEP = 10

P0 = 112
LANE = 128
SUB = 8
VMEM_LIMIT = 48 * 1024 * 1024
N_CHIPS = 4
N_DEV = 8

C_DQ, C_DK, C_DV, C_DZ, C_SQ, C_SK, C_SV, C_GDN, C_GSB = 0, 1, 2, 4, 6, 7, 8, 9, 10
PROJ_BIG = 11 * 1024
AB_COL = 2 * DN_QK + 2 * DN_V


def _params(n_axes):
    return pltpu.CompilerParams(dimension_semantics=("arbitrary",) * n_axes, vmem_limit_bytes=VMEM_LIMIT)


def _tile(n, target, q=LANE):
    best = None
    for t in range(q, min(n, target) + 1, q):
        if n % t == 0:
            best = t
    return best if best is not None else n


def _dot(a, b):
    return jnp.dot(a.astype(bf16), b.astype(bf16), preferred_element_type=f32)


def _dot_nt(a, b):
    return lax.dot_general(a.astype(bf16), b.astype(bf16), (((1,), (1,)), ((), ())), preferred_element_type=f32)


def _dot_tn(a, b):
    return lax.dot_general(a.astype(bf16), b.astype(bf16), (((0,), (0,)), ((), ())), preferred_element_type=f32)


_HI = lax.Precision.HIGHEST


def _hdot(a, b):
    return jnp.dot(a, b, precision=_HI, preferred_element_type=f32)


def _hdot_nt(a, b):
    return lax.dot_general(a, b, (((1,), (1,)), ((), ())), precision=_HI, preferred_element_type=f32)


def _hdot_tn(a, b):
    return lax.dot_general(a, b, (((0,), (0,)), ((), ())), precision=_HI, preferred_element_type=f32)


def _sigmoid(x):
    e = jnp.exp(-jnp.abs(x))
    r = 1.0 / (1.0 + e)
    return jnp.where(x >= 0, r, e * r)


def _log1p_small(e):
    return jnp.where(e < 1e-3, e * (1.0 - e * (0.5 - e * (1.0 / 3.0))), jnp.log(1.0 + e))


def _rowsum(x):
    return jnp.sum(x, axis=1, keepdims=True)


def _allsum(x):
    return jnp.sum(jnp.sum(x, axis=1, keepdims=True), axis=0, keepdims=True)


def matmul(a, b, mode, name, residual=None, out_dtype=f32, tm_t=1408, tn_t=1024, tk_t=1408):
    if mode == "nn":
        (M, K), (K2, N) = a.shape, b.shape
    elif mode == "nt":
        (M, K), (N, K2) = a.shape, b.shape
    else:
        (K, M), (K2, N) = a.shape, b.shape
    assert K == K2, (a.shape, b.shape, mode)
    tm, tn, tk = _tile(M, tm_t), _tile(N, tn_t), _tile(K, tk_t)
    nk = K // tk
    if mode == "nn":
        a_spec = pl.BlockSpec((tm, tk), lambda i, j, k: (i, k))
        b_spec = pl.BlockSpec((tk, tn), lambda i, j, k: (k, j))
        dims = (((1,), (0,)), ((), ()))
    elif mode == "nt":
        a_spec = pl.BlockSpec((tm, tk), lambda i, j, k: (i, k))
        b_spec = pl.BlockSpec((tn, tk), lambda i, j, k: (j, k))
        dims = (((1,), (1,)), ((), ()))
    else:
        a_spec = pl.BlockSpec((tk, tm), lambda i, j, k: (k, i))
        b_spec = pl.BlockSpec((tk, tn), lambda i, j, k: (k, j))
        dims = (((0,), (0,)), ((), ()))
    o_spec = pl.BlockSpec((tm, tn), lambda i, j, k: (i, j))
    has_res = residual is not None

    def body(*refs):
        if has_res:
            a_ref, b_ref, r_ref, o_ref, acc_ref = refs
        else:
            a_ref, b_ref, o_ref, acc_ref = refs
        k = pl.program_id(2)

        @pl.when(k == 0)
        def _():
            acc_ref[...] = jnp.zeros_like(acc_ref)

        acc_ref[...] += lax.dot_general(a_ref[...].astype(bf16), b_ref[...].astype(bf16), dims,
                                        preferred_element_type=f32)

        @pl.when(k == nk - 1)
        def _():
            r = acc_ref[...]
            if has_res:
                r = r + r_ref[...]
            o_ref[...] = r.astype(out_dtype)

    ins = [a, b] + ([residual] if has_res else [])
    specs = [a_spec, b_spec] + ([o_spec] if has_res else [])
    return pl.pallas_call(
        body, name=name, grid=(M // tm, N // tn, nk), in_specs=specs, out_specs=o_spec,
        out_shape=jax.ShapeDtypeStruct((M, N), out_dtype),
        scratch_shapes=[pltpu.VMEM((tm, tn), f32)], compiler_params=_params(3),
    )(*ins)


def _row_tile(tp):
    return _tile(tp, 512)


def rms_fwd(h, gain, name):
    tp, d = h.shape
    rt = _row_tile(tp)

    def body(h_ref, g_ref, o_ref):
        x = h_ref[...]
        r = lax.rsqrt(jnp.mean(x * x, axis=-1, keepdims=True) + RMS_EPS)
        o_ref[...] = (x * r * g_ref[...]).astype(bf16)

    return pl.pallas_call(
        body, name=name, grid=(tp // rt,),
        in_specs=[pl.BlockSpec((rt, d), lambda i: (i, 0)), pl.BlockSpec((1, d), lambda i: (0, 0))],
        out_specs=pl.BlockSpec((rt, d), lambda i: (i, 0)),
        out_shape=jax.ShapeDtypeStruct((tp, d), bf16), compiler_params=_params(1),
    )(h, gain)


def rms_bwd(h, gain, dn, dres, name):
    tp, d = h.shape
    rt = _row_tile(tp)

    def body(h_ref, g_ref, dn_ref, dr_ref, dh_ref, dg_ref):
        i = pl.program_id(0)
        x = h_ref[...]
        r = lax.rsqrt(jnp.mean(x * x, axis=-1, keepdims=True) + RMS_EPS)
        xh = x * r
        dn_ = dn_ref[...]
        dxh = dn_ * g_ref[...]
        dh_ref[...] = r * (dxh - xh * jnp.mean(dxh * xh, axis=-1, keepdims=True)) + dr_ref[...]
        part = jnp.sum(dn_ * xh, axis=0, keepdims=True)

        @pl.when(i == 0)
        def _():
            dg_ref[...] = part

        @pl.when(i > 0)
        def _():
            dg_ref[...] += part

    row = pl.BlockSpec((rt, d), lambda i: (i, 0))
    vec = pl.BlockSpec((1, d), lambda i: (0, 0))
    return pl.pallas_call(
        body, name=name, grid=(tp // rt,), in_specs=[row, vec, row, row], out_specs=[row, vec],
        out_shape=[jax.ShapeDtypeStruct((tp, d), f32), jax.ShapeDtypeStruct((1, d), f32)],
        compiler_params=_params(1),
    )(h, gain, dn, dres)


def loss_head(y, target):
    tp, d = y.shape
    rt = P0 + N_META
    assert rt == SB_BLOCK and tp % rt == 0 and target.shape == (tp - rt, d)

    def body(y_ref, t_ref, dy_ref, l_ref):
        i = pl.program_id(0)

        @pl.when(i == 0)
        def _():
            dy_ref[...] = jnp.zeros_like(dy_ref)
            l_ref[...] = jnp.zeros_like(l_ref)

        @pl.when(i > 0)
        def _():
            err = y_ref[...] - t_ref[...]
            dy_ref[...] = err * (1.0 / d)
            l_ref[...] += jnp.broadcast_to(_allsum(err * err) * (0.5 / d), l_ref.shape)

    return pl.pallas_call(
        body, name="loss_head", grid=(tp // rt,),
        in_specs=[pl.BlockSpec((rt, d), lambda i: (i, 0)), pl.BlockSpec((rt, d), lambda i: (jnp.maximum(i - 1, 0), 0))],
        out_specs=[pl.BlockSpec((rt, d), lambda i: (i, 0)), pl.BlockSpec((1, LANE), lambda i: (0, 0))],
        out_shape=[jax.ShapeDtypeStruct((tp, d), f32), jax.ShapeDtypeStruct((1, LANE), f32)],
        compiler_params=_params(1),
    )(y, target)


def swiglu_fwd(u):
    tp = u.shape[0]
    rt, cb = _row_tile(tp), 256
    nb = D_FF // cb

    def body(g_ref, u_ref, o_ref):
        g = g_ref[...]
        o_ref[...] = (g * _sigmoid(g) * u_ref[...]).astype(bf16)

    return pl.pallas_call(
        body, name="swiglu_fwd", grid=(tp // rt, nb),
        in_specs=[pl.BlockSpec((rt, cb), lambda i, j: (i, j)), pl.BlockSpec((rt, cb), lambda i, j: (i, j + nb))],
        out_specs=pl.BlockSpec((rt, cb), lambda i, j: (i, j)),
        out_shape=jax.ShapeDtypeStruct((tp, D_FF), bf16), compiler_params=_params(2),
    )(u, u)


def swiglu_bwd(u, dact):
    tp = u.shape[0]
    rt, cb = _row_tile(tp), 256
    nb = D_FF // cb

    def body(g_ref, u_ref, da_ref, dg_ref, du_ref):
        g = g_ref[...]
        s = _sigmoid(g)
        da = da_ref[...]
        dg_ref[...] = (da * u_ref[...] * s * (1.0 + g * (1.0 - s))).astype(bf16)
        du_ref[...] = (da * g * s).astype(bf16)

    lo = pl.BlockSpec((rt, cb), lambda i, j: (i, j))
    hi = pl.BlockSpec((rt, cb), lambda i, j: (i, j + nb))
    dgate, dup = pl.pallas_call(
        body, name="swiglu_bwd", grid=(tp // rt, nb), in_specs=[lo, hi, lo], out_specs=[lo, lo],
        out_shape=[jax.ShapeDtypeStruct((tp, D_FF), bf16)] * 2, compiler_params=_params(2),
    )(u, u, dact)
    return dgate, dup


def merge_fwd(proj, ydn, ysb):
    tp = proj.shape[0]
    rt, d = _row_tile(tp), D_MODEL

    def body(gd_ref, gs_ref, yd_ref, ys_ref, o_ref):
        o_ref[...] = (_sigmoid(gd_ref[...]) * yd_ref[...] + _sigmoid(gs_ref[...]) * ys_ref[...]).astype(bf16)

    row = pl.BlockSpec((rt, d), lambda i: (i, 0))
    return pl.pallas_call(
        body, name="merge_fwd", grid=(tp // rt,),
        in_specs=[pl.BlockSpec((rt, d), lambda i: (i, C_GDN)), pl.BlockSpec((rt, d), lambda i: (i, C_GSB)), row, row],
        out_specs=row, out_shape=jax.ShapeDtypeStruct((tp, d), bf16), compiler_params=_params(1),
    )(proj, proj, ydn, ysb)


def merge_bwd(proj, ydn, ysb, dm):
    tp = proj.shape[0]
    rt, d = _row_tile(tp), D_MODEL

    def body(gd_ref, gs_ref, yd_ref, ys_ref, dm_ref, dyd_ref, dys_ref, dgd_ref, dgs_ref):
        dm_ = dm_ref[...]
        sd = _sigmoid(gd_ref[...])
        ss = _sigmoid(gs_ref[...])
        dyd_ref[...] = (dm_ * sd).astype(bf16)
        dys_ref[...] = (dm_ * ss).astype(bf16)
        dgd_ref[...] = (dm_ * yd_ref[...] * sd * (1.0 - sd)).astype(bf16)
        dgs_ref[...] = (dm_ * ys_ref[...] * ss * (1.0 - ss)).astype(bf16)

    row = pl.BlockSpec((rt, d), lambda i: (i, 0))
    return pl.pallas_call(
        body, name="merge_bwd", grid=(tp // rt,),
        in_specs=[pl.BlockSpec((rt, d), lambda i: (i, C_GDN)), pl.BlockSpec((rt, d), lambda i: (i, C_GSB)), row, row, row],
        out_specs=[row] * 4, out_shape=[jax.ShapeDtypeStruct((tp, d), bf16)] * 4, compiler_params=_params(1),
    )(proj, proj, ydn, ysb, dm)


def dn_out_fwd(o, proj, gain):
    tp = o.shape[0]
    rt, cb = _row_tile(tp), DN_DV
    zb = C_DZ * 1024 // cb

    def body(o_ref, z_ref, g_ref, y_ref):
        x = o_ref[...]
        r = lax.rsqrt(jnp.mean(x * x, axis=-1, keepdims=True) + RMS_EPS)
        z = z_ref[...]
        y_ref[...] = (x * r * g_ref[...] * (z * _sigmoid(z))).astype(bf16)

    blk = pl.BlockSpec((rt, cb), lambda i, j: (i, j))
    return pl.pallas_call(
        body, name="dn_out_fwd", grid=(tp // rt, HEADS),
        in_specs=[blk, pl.BlockSpec((rt, cb), lambda i, j: (i, j + zb)), pl.BlockSpec((1, cb), lambda i, j: (0, 0))],
        out_specs=blk, out_shape=jax.ShapeDtypeStruct((tp, DN_V), bf16), compiler_params=_params(2),
    )(o, proj, gain)


def dn_out_bwd(o, proj, gain, dy):
    tp = o.shape[0]
    rt, cb = _row_tile(tp), DN_DV
    zb = C_DZ * 1024 // cb

    def body(o_ref, z_ref, g_ref, dy_ref, do_ref, dz_ref, dg_ref):
        i, j = pl.program_id(0), pl.program_id(1)
        x = o_ref[...]
        r = lax.rsqrt(jnp.mean(x * x, axis=-1, keepdims=True) + RMS_EPS)
        xh = x * r
        z = z_ref[...]
        s = _sigmoid(z)
        dy_ = dy_ref[...]
        g = g_ref[...]
        drn = dy_ * (z * s)
        dz_ref[...] = (dy_ * xh * g * s * (1.0 + z * (1.0 - s))).astype(bf16)
        dxh = drn * g
        do_ref[...] = r * (dxh - xh * jnp.mean(dxh * xh, axis=-1, keepdims=True))
        part = jnp.sum(drn * xh, axis=0, keepdims=True)
        first = jnp.logical_and(i == 0, j == 0)

        @pl.when(first)
        def _():
            dg_ref[...] = part

        @pl.when(jnp.logical_not(first))
        def _():
            dg_ref[...] += part

    blk = pl.BlockSpec((rt, cb), lambda i, j: (i, j))
    vec = pl.BlockSpec((1, cb), lambda i, j: (0, 0))
    return pl.pallas_call(
        body, name="dn_out_bwd", grid=(tp // rt, HEADS),
        in_specs=[blk, pl.BlockSpec((rt, cb), lambda i, j: (i, j + zb)), vec, blk],
        out_specs=[blk, blk, vec],
        out_shape=[jax.ShapeDtypeStruct((tp, DN_V), f32), jax.ShapeDtypeStruct((tp, DN_V), bf16),
                   jax.ShapeDtypeStruct((1, cb), f32)],
        compiler_params=_params(2),
    )(o, proj, gain, dy)


def sb_prep_fwd(proj, gq, gk):
    tp = proj.shape[0]
    rt, cb = _row_tile(tp), SB_DH
    qb, kb, vb = (c * 1024 // cb for c in (C_SQ, C_SK, C_SV))

    def body(q_ref, k_ref, v_ref, gq_ref, gk_ref, qo_ref, ko_ref, vo_ref):
        for x_ref, g_ref, o_ref in ((q_ref, gq_ref, qo_ref), (k_ref, gk_ref, ko_ref)):
            x = x_ref[...]
            r = lax.rsqrt(jnp.mean(x * x, axis=-1, keepdims=True) + RMS_EPS)
            o_ref[...] = (x * r * g_ref[...]).astype(bf16)
        vo_ref[...] = v_ref[...].astype(bf16)

    blk = pl.BlockSpec((rt, cb), lambda i, j: (i, j))
    vec = pl.BlockSpec((1, cb), lambda i, j: (0, 0))
    return pl.pallas_call(
        body, name="sb_prep_fwd", grid=(tp // rt, HEADS),
        in_specs=[pl.BlockSpec((rt, cb), lambda i, j: (i, j + qb)), pl.BlockSpec((rt, cb), lambda i, j: (i, j + kb)),
                  pl.BlockSpec((rt, cb), lambda i, j: (i, j + vb)), vec, vec],
        out_specs=[blk] * 3, out_shape=[jax.ShapeDtypeStruct((tp, SB_W), bf16)] * 3, compiler_params=_params(2),
    )(proj, proj, proj, gq, gk)


def sb_prep_bwd(proj, gq, gk, dqs, dks):
    tp = proj.shape[0]
    rt, cb = _row_tile(tp), SB_DH
    qb, kb = (c * 1024 // cb for c in (C_SQ, C_SK))

    def body(q_ref, k_ref, gq_ref, gk_ref, dq_ref, dk_ref, dqo_ref, dko_ref, dgq_ref, dgk_ref):
        i, j = pl.program_id(0), pl.program_id(1)
        first = jnp.logical_and(i == 0, j == 0)
        for x_ref, g_ref, dn_ref, dx_ref, dg_ref in ((q_ref, gq_ref, dq_ref, dqo_ref, dgq_ref),
                                                     (k_ref, gk_ref, dk_ref, dko_ref, dgk_ref)):
            x = x_ref[...]
            r = lax.rsqrt(jnp.mean(x * x, axis=-1, keepdims=True) + RMS_EPS)
            xh = x * r
            dn_ = dn_ref[...]
            dxh = dn_ * g_ref[...]
            dx_ref[...] = (r * (dxh - xh * jnp.mean(dxh * xh, axis=-1, keepdims=True))).astype(bf16)
            part = jnp.sum(dn_ * xh, axis=0, keepdims=True)

            @pl.when(first)
            def _(dg_ref=dg_ref, part=part):
                dg_ref[...] = part

            @pl.when(jnp.logical_not(first))
            def _(dg_ref=dg_ref, part=part):
                dg_ref[...] += part

    blk = pl.BlockSpec((rt, cb), lambda i, j: (i, j))
    vec = pl.BlockSpec((1, cb), lambda i, j: (0, 0))
    return pl.pallas_call(
        body, name="sb_prep_bwd", grid=(tp // rt, HEADS),
        in_specs=[pl.BlockSpec((rt, cb), lambda i, j: (i, j + qb)), pl.BlockSpec((rt, cb), lambda i, j: (i, j + kb)),
                  vec, vec, blk, blk],
        out_specs=[blk, blk, vec, vec],
        out_shape=[jax.ShapeDtypeStruct((tp, SB_W), bf16)] * 2 + [jax.ShapeDtypeStruct((1, cb), f32)] * 2,
        compiler_params=_params(2),
    )(proj, proj, gq, gk, dqs, dks)


def cast_bf16(x, name):
    tp, d = x.shape
    rt = _row_tile(tp)
    blk = pl.BlockSpec((rt, d), lambda i: (i, 0))

    def body(x_ref, o_ref):
        o_ref[...] = x_ref[...].astype(bf16)

    return pl.pallas_call(body, name=name, grid=(tp // rt,), in_specs=[blk], out_specs=blk,
                          out_shape=jax.ShapeDtypeStruct((tp, d), bf16), compiler_params=_params(1))(x)


def _conv_taps(ext, rt):
    taps = []
    for k in range(DN_CONV):
        s = DN_CONV - 1 - k
        taps.append((pltpu.roll(ext, s, axis=0) if s else ext)[SUB:SUB + rt])
    return taps


def _conv_act(taps, w, l2):
    y = taps[0] * w[0:1]
    for k in range(1, DN_CONV):
        y = y + taps[k] * w[k:k + 1]
    s = _sigmoid(y)
    a = y * s
    if l2:
        n = lax.rsqrt(jnp.sum(a * a, axis=-1, keepdims=True) + L2_EPS)
        return y, s, a, n
    return y, s, a, None


def conv_fwd(proj, w8, col_blk, ncb, l2, name):
    tp = proj.shape[0]
    rt = _row_tile(tp)
    hb = rt // SUB

    def body(x_ref, h_ref, w_ref, o_ref):
        i = pl.program_id(1)
        halo = h_ref[...] * (i > 0).astype(f32)
        ext = jnp.concatenate([halo, x_ref[...]], axis=0)
        _, _, a, n = _conv_act(_conv_taps(ext, rt), w_ref[...], l2)
        o_ref[...] = a * n if l2 else a

    return pl.pallas_call(
        body, name=name, grid=(ncb, tp // rt),
        in_specs=[pl.BlockSpec((rt, LANE), lambda j, i: (i, j + col_blk)),
                  pl.BlockSpec((SUB, LANE), lambda j, i: (jnp.maximum(i * hb - 1, 0), j + col_blk)),
                  pl.BlockSpec((SUB, LANE), lambda j, i: (0, j))],
        out_specs=pl.BlockSpec((rt, LANE), lambda j, i: (i, j)),
        out_shape=jax.ShapeDtypeStruct((tp, ncb * LANE), f32), compiler_params=_params(2),
    )(proj, proj, w8)


def conv_bwd_act(proj, w8, dout, col_blk, ncb, l2, name):
    tp = proj.shape[0]
    rt = _row_tile(tp)
    hb = rt // SUB

    def body(x_ref, h_ref, w_ref, d_ref, dy_ref, dw_ref):
        i = pl.program_id(1)
        halo = h_ref[...] * (i > 0).astype(f32)
        ext = jnp.concatenate([halo, x_ref[...]], axis=0)
        taps = _conv_taps(ext, rt)
        y, s, a, n = _conv_act(taps, w_ref[...], l2)
        da = d_ref[...]
        if l2:
            out = a * n
            da = n * (da - out * jnp.sum(da * out, axis=-1, keepdims=True))
        dy = da * s * (1.0 + y * (1.0 - s))
        dy_ref[...] = dy
        rows = lax.broadcasted_iota(jnp.int32, (SUB, LANE), 0)
        part = jnp.zeros((SUB, LANE), f32)
        for k in range(DN_CONV):
            part = part + jnp.where(rows == k, jnp.sum(taps[k] * dy, axis=0, keepdims=True), 0.0)

        @pl.when(i == 0)
        def _():
            dw_ref[...] = part

        @pl.when(i > 0)
        def _():
            dw_ref[...] += part

    return pl.pallas_call(
        body, name=name, grid=(ncb, tp // rt),
        in_specs=[pl.BlockSpec((rt, LANE), lambda j, i: (i, j + col_blk)),
                  pl.BlockSpec((SUB, LANE), lambda j, i: (jnp.maximum(i * hb - 1, 0), j + col_blk)),
                  pl.BlockSpec((SUB, LANE), lambda j, i: (0, j)),
                  pl.BlockSpec((rt, LANE), lambda j, i: (i, j))],
        out_specs=[pl.BlockSpec((rt, LANE), lambda j, i: (i, j)), pl.BlockSpec((SUB, LANE), lambda j, i: (0, j))],
        out_shape=[jax.ShapeDtypeStruct((tp, ncb * LANE), f32), jax.ShapeDtypeStruct((SUB, ncb * LANE), f32)],
        compiler_params=_params(2),
    )(proj, proj, w8, dout)


def conv_bwd_in(dy, w8, name):
    tp, cols = dy.shape
    rt = _row_tile(tp)
    hb = rt // SUB
    nr = tp // rt
    last8 = tp // SUB - 1

    def body(d_ref, h_ref, w_ref, o_ref):
        i = pl.program_id(1)
        halo = h_ref[...] * (i < nr - 1).astype(f32)
        ext = jnp.concatenate([d_ref[...], halo], axis=0)
        w = w_ref[...]
        acc = None
        for k in range(DN_CONV):
            s = DN_CONV - 1 - k
            sh = (pltpu.roll(ext, rt + SUB - s, axis=0) if s else ext)[0:rt]
            term = sh * w[k:k + 1]
            acc = term if acc is None else acc + term
        o_ref[...] = acc.astype(bf16)

    return pl.pallas_call(
        body, name=name, grid=(cols // LANE, nr),
        in_specs=[pl.BlockSpec((rt, LANE), lambda j, i: (i, j)),
                  pl.BlockSpec((SUB, LANE), lambda j, i: (jnp.minimum((i + 1) * hb, last8), j)),
                  pl.BlockSpec((SUB, LANE), lambda j, i: (0, j))],
        out_specs=pl.BlockSpec((rt, LANE), lambda j, i: (i, j)),
        out_shape=jax.ShapeDtypeStruct((tp, cols), bf16), compiler_params=_params(2),
    )(dy, dy, w8)


def _ab_common(p, al, dtb, r0):
    rows = r0 + lax.broadcasted_iota(jnp.int32, p.shape, 0)
    mask = (rows >= P0).astype(f32)
    xx = p + dtb
    sp = jnp.maximum(xx, 0.0) + _log1p_small(jnp.exp(-jnp.abs(xx)))
    ea = jnp.exp(al)
    g = -ea * sp * mask
    beta = _sigmoid(p) * mask
    return g, beta, _sigmoid(xx), ea, mask


def ab_fwd(pab, al, dtb):
    tp = pab.shape[0]
    rt = _row_tile(tp)

    def body(p_ref, al_ref, dt_ref, g_ref, b_ref):
        i = pl.program_id(0)
        g, beta, _, _, _ = _ab_common(p_ref[...], al_ref[...], dt_ref[...], i * rt)
        for h in range(HEADS):
            g_ref[h] = jnp.broadcast_to(g[:, h:h + 1], (rt, LANE))
            b_ref[h] = jnp.broadcast_to(beta[:, HEADS + h:HEADS + h + 1], (rt, LANE))

    vec = pl.BlockSpec((1, LANE), lambda i: (0, 0))
    out = pl.BlockSpec((HEADS, rt, LANE), lambda i: (0, i, 0))
    return pl.pallas_call(
        body, name="ab_fwd", grid=(tp // rt,), in_specs=[pl.BlockSpec((rt, LANE), lambda i: (i, 0)), vec, vec],
        out_specs=[out, out], out_shape=[jax.ShapeDtypeStruct((HEADS, tp, LANE), f32)] * 2, compiler_params=_params(1),
    )(pab, al, dtb)


def ab_bwd(pab, al, dtb, dg, db):
    tp = pab.shape[0]
    rt = _row_tile(tp)

    def body(p_ref, al_ref, dt_ref, dg_ref, db_ref, dp_ref, dal_ref, ddt_ref):
        i = pl.program_id(0)
        g, beta, sx, ea, mask = _ab_common(p_ref[...], al_ref[...], dt_ref[...], i * rt)
        lanes = lax.broadcasted_iota(jnp.int32, (rt, LANE), 1)
        dgl = jnp.zeros((rt, LANE), f32)
        dbl = jnp.zeros((rt, LANE), f32)
        for h in range(HEADS):
            dgl = dgl + jnp.where(lanes == h, dg_ref[h], 0.0)
            dbl = dbl + jnp.where(lanes == HEADS + h, db_ref[h], 0.0)
        dxx = dgl * (-ea) * sx * mask
        dp_ref[...] = (dxx + dbl * beta * (1.0 - beta)).astype(bf16)
        pal = jnp.sum(dgl * g, axis=0, keepdims=True)
        pdt = jnp.sum(dxx, axis=0, keepdims=True)

        @pl.when(i == 0)
        def _():
            dal_ref[...] = pal
            ddt_ref[...] = pdt

        @pl.when(i > 0)
        def _():
            dal_ref[...] += pal
            ddt_ref[...] += pdt

    vec = pl.BlockSpec((1, LANE), lambda i: (0, 0))
    row = pl.BlockSpec((rt, LANE), lambda i: (i, 0))
    big = pl.BlockSpec((HEADS, rt, LANE), lambda i: (0, i, 0))
    return pl.pallas_call(
        body, name="ab_bwd", grid=(tp // rt,), in_specs=[row, vec, vec, big, big], out_specs=[row, vec, vec],
        out_shape=[jax.ShapeDtypeStruct((tp, LANE), bf16), jax.ShapeDtypeStruct((1, LANE), f32),
                   jax.ShapeDtypeStruct((1, LANE), f32)],
        compiler_params=_params(1),
    )(pab, al, dtb, dg, db)


class _Chunk:
    pass


def _gdn_chunk(q, k, v, gcol, bcol, grow8):
    C = CHUNK
    X = _Chunk()
    ri = lax.broadcasted_iota(jnp.int32, (C, C), 0)
    ci = lax.broadcasted_iota(jnp.int32, (C, C), 1)
    r2 = lax.broadcasted_iota(jnp.int32, (LANE, LANE), 0)
    c2 = lax.broadcasted_iota(jnp.int32, (LANE, LANE), 1)
    gam = _hdot((ri >= ci).astype(f32), gcol)
    gam_row = _hdot(grow8, (r2 <= c2).astype(f32))[0:1, 0:C]
    X.ri, X.ci = ri, ci
    X.Dm = jnp.where(ri >= ci, jnp.exp(jnp.minimum(gam[:, 0:C] - gam_row, 0.0)), 0.0)
    X.eg = jnp.exp(gam)
    gl = gam[C - 1:C, :]
    X.egl = jnp.exp(gl)
    X.kdec = jnp.exp(gl - gam)
    X.qs = q * (DN_DK ** -0.5)
    X.kb = k * bcol
    X.A = jnp.where(ri > ci, _dot_nt(X.kb, k) * X.Dm, 0.0)
    eye = (ri == ci).astype(f32)
    T = eye - X.A
    P = X.A
    for _ in range(5):
        P = _hdot(P, P)
        T = T + _hdot(T, P)
    X.T = T
    X.b2 = jnp.concatenate([bcol, bcol], axis=-1)
    X.u = _hdot(T, v * X.b2)
    X.w = _hdot(T, X.kb * X.eg)
    X.attn = _dot_nt(X.qs, k) * X.Dm
    X.qg = X.qs * X.eg
    X.kg = k * X.kdec
    return X


def gdn_fwd(q, k, v, gc, bc, grow):
    tp = q.shape[0]
    nc = tp // CHUNK

    def body(q_ref, k_ref, v_ref, gc_ref, bc_ref, gr_ref, o_ref, ss_ref, S_ref):
        c = pl.program_id(1)

        @pl.when(c == 0)
        def _():
            S_ref[...] = jnp.zeros_like(S_ref)

        X = _gdn_chunk(q_ref[...], k_ref[...], v_ref[...], gc_ref[0], bc_ref[0], gr_ref[0])
        S = S_ref[...]
        ss_ref[0, 0] = S
        vn = X.u - _dot(X.w, S)
        o_ref[...] = _dot(X.qg, S) + _dot(X.attn, vn)
        S_ref[...] = S * X.egl[:, 0:1] + _dot_tn(X.kg, vn)

    qk = pl.BlockSpec((CHUNK, DN_DK), lambda h, c: (c, h))
    vv = pl.BlockSpec((CHUNK, DN_DV), lambda h, c: (c, h))
    col = pl.BlockSpec((1, CHUNK, LANE), lambda h, c: (h, c, 0))
    row = pl.BlockSpec((1, SUB, LANE), lambda h, c: (h, c, 0))
    return pl.pallas_call(
        body, name="gdn_fwd", grid=(HEADS, nc), in_specs=[qk, qk, vv, col, col, row],
        out_specs=[vv, pl.BlockSpec((1, 1, DN_DK, DN_DV), lambda h, c: (h, c, 0, 0))],
        out_shape=[jax.ShapeDtypeStruct((tp, DN_V), f32), jax.ShapeDtypeStruct((HEADS, nc, DN_DK, DN_DV), f32)],
        scratch_shapes=[pltpu.VMEM((DN_DK, DN_DV), f32)], compiler_params=_params(2),
    )(q, k, v, gc, bc, grow)


def gdn_bwd(q, k, v, gc, bc, grow, states, do):
    tp = q.shape[0]
    nc = tp // CHUNK
    C = CHUNK

    def body(q_ref, k_ref, v_ref, gc_ref, bc_ref, gr_ref, ss_ref, do_ref, dq_ref, dk_ref, dv_ref, dg_ref, db_ref, dS_ref):
        c = pl.program_id(1)

        @pl.when(c == 0)
        def _():
            dS_ref[...] = jnp.zeros_like(dS_ref)

        k_, v_ = k_ref[...], v_ref[...]
        bcol = bc_ref[0]
        X = _gdn_chunk(q_ref[...], k_, v_, gc_ref[0], bcol, gr_ref[0])
        ri, ci = X.ri, X.ci
        S = ss_ref[0, 0]
        do_ = do_ref[...]
        dSn = dS_ref[...]
        vn = X.u - _dot(X.w, S)
        d_vn = _dot_tn(X.attn, do_) + _dot(X.kg, dSn)
        dattn = jnp.where(ri >= ci, _dot_nt(do_, vn), 0.0)
        d_qg = _dot_nt(do_, S)
        d_kg = _dot_nt(vn, dSn)
        dS_ref[...] = _dot_tn(X.qg, do_) + X.egl[:, 0:1] * dSn - _dot_tn(X.w, d_vn)
        dw = -_dot_nt(d_vn, S)
        dRu = _hdot_tn(X.T, d_vn)
        dRw = _hdot_tn(X.T, dw)
        dA = jnp.where(ri > ci, -(_hdot_nt(dRu, X.u) + _hdot_nt(dRw, X.w)), 0.0)
        dKK = dA * X.Dm
        dkb = _dot(dKK, k_) + dRw * X.eg
        dk = _dot_tn(dKK, X.kb)
        dQK = dattn * X.Dm
        dqs = _dot(dQK, k_) + d_qg * X.eg
        dk = dk + _dot_tn(dQK, X.qs)
        E = dA * X.A + dattn * X.attn
        dgam = _rowsum(E) - _hdot_tn(E, jnp.ones((C, LANE), f32))
        dv_ref[...] = dRu * X.b2
        dbeta = _rowsum(dRu * v_) + _rowsum(dkb * k_)
        dgam = dgam + _rowsum(dRw * (X.kb * X.eg)) + _rowsum(d_qg * X.qg)
        dk = dk + dkb * bcol + d_kg * X.kdec
        t = d_kg * X.kg
        dgam = dgam - _rowsum(t)
        dgl = _allsum(t) + X.egl[:, 0:1] * _allsum(S * dSn)
        rows = lax.broadcasted_iota(jnp.int32, (C, LANE), 0)
        dgam = dgam + jnp.where(rows == C - 1, dgl, 0.0)
        dq_ref[...] = dqs * (DN_DK ** -0.5)
        dk_ref[...] = dk
        dg_ref[0] = _hdot((ci >= ri).astype(f32), dgam)
        db_ref[0] = jnp.broadcast_to(dbeta, (C, LANE))

    rc = lambda c: nc - 1 - c
    qk = pl.BlockSpec((CHUNK, DN_DK), lambda h, c: (rc(c), h))
    vv = pl.BlockSpec((CHUNK, DN_DV), lambda h, c: (rc(c), h))
    col = pl.BlockSpec((1, CHUNK, LANE), lambda h, c: (h, rc(c), 0))
    row = pl.BlockSpec((1, SUB, LANE), lambda h, c: (h, rc(c), 0))
    st = pl.BlockSpec((1, 1, DN_DK, DN_DV), lambda h, c: (h, rc(c), 0, 0))
    return pl.pallas_call(
        body, name="gdn_bwd", grid=(HEADS, nc), in_specs=[qk, qk, vv, col, col, row, st, vv],
        out_specs=[qk, qk, vv, col, col],
        out_shape=[jax.ShapeDtypeStruct((tp, DN_QK), f32), jax.ShapeDtypeStruct((tp, DN_QK), f32),
                   jax.ShapeDtypeStruct((tp, DN_V), f32), jax.ShapeDtypeStruct((HEADS, tp, LANE), f32),
                   jax.ShapeDtypeStruct((HEADS, tp, LANE), f32)],
        scratch_shapes=[pltpu.VMEM((DN_DK, DN_DV), f32)], compiler_params=_params(2),
    )(q, k, v, gc, bc, grow, states, do)


def _cumsum_after(x, us):
    hi = x.astype(bf16)
    r1 = x - hi.astype(f32)
    mid = r1.astype(bf16)
    lo = (r1 - mid.astype(f32)).astype(bf16)
    return (jnp.dot(hi, us, preferred_element_type=f32) + jnp.dot(mid, us, preferred_element_type=f32)
            + jnp.dot(lo, us, preferred_element_type=f32))


def _sb_block(q, kj, i, j, cs, us, ri, ci):
    z = _dot_nt(q, kj) * (SB_DH ** -0.5)
    qpos = i * SB_BLOCK + ri
    kpos = j * SB_BLOCK + ci
    vis = jnp.logical_and(kpos < qpos, kpos >= P0)
    e = jnp.exp(-jnp.abs(z))
    l1p = _log1p_small(e)
    lsp = jnp.minimum(z, 0.0) - l1p
    lk = jnp.where(vis, -jnp.maximum(z, 0.0) - l1p, 0.0)
    w = jnp.where(vis, jnp.exp(lsp + _cumsum_after(lk, us) + cs), 0.0)
    return z, e, vis, lk, w


def sb_fwd(qs, ks, vs):
    tp = qs.shape[0]
    nq = tp // SB_BLOCK
    B = SB_BLOCK

    def body(q_ref, k_ref, v_ref, o_ref):
        i = pl.program_id(1)
        q = q_ref[...]
        ri = lax.broadcasted_iota(jnp.int32, (B, B), 0)
        ci = lax.broadcasted_iota(jnp.int32, (B, B), 1)
        us = (ri > ci).astype(bf16)

        def step(t, carry):
            acc, cs = carry
            j = i - t
            off = pl.multiple_of(j * B, B)
            _, _, _, lk, w = _sb_block(q, k_ref[pl.ds(off, B), :], i, j, cs, us, ri, ci)
            acc = acc + _dot(w, v_ref[pl.ds(off, B), :])
            return acc, cs + _rowsum(lk)

        acc, _ = lax.fori_loop(0, i + 1, step, (jnp.zeros((B, SB_DH), f32), jnp.zeros((B, 1), f32)))
        o_ref[...] = acc

    blk = pl.BlockSpec((B, SB_DH), lambda h, i: (i, h))
    full = pl.BlockSpec((tp, SB_DH), lambda h, i: (0, h))
    return pl.pallas_call(
        body, name="sb_fwd", grid=(HEADS, nq), in_specs=[blk, full, full], out_specs=blk,
        out_shape=jax.ShapeDtypeStruct((tp, SB_W), f32), compiler_params=_params(2),
    )(qs, ks, vs)


def sb_bwd(qs, ks, vs, o, do):
    tp = qs.shape[0]
    nq = tp // SB_BLOCK
    B = SB_BLOCK

    def body(q_ref, k_ref, v_ref, o_ref, do_ref, dq_ref, dk_ref, dv_ref):
        i = pl.program_id(1)

        @pl.when(i == 0)
        def _():
            dk_ref[...] = jnp.zeros_like(dk_ref)
            dv_ref[...] = jnp.zeros_like(dv_ref)

        q = q_ref[...]
        do_ = do_ref[...]
        dob = do_.astype(bf16)
        et = _rowsum(dob.astype(f32) * o_ref[...])
        ri = lax.broadcasted_iota(jnp.int32, (B, B), 0)
        ci = lax.broadcasted_iota(jnp.int32, (B, B), 1)
        us = (ri > ci).astype(bf16)

        def step(t, carry):
            dq, cs, ce = carry
            j = i - t
            off = pl.multiple_of(j * B, B)
            kj = k_ref[pl.ds(off, B), :]
            z, e, vis, lk, w = _sb_block(q, kj, i, j, cs, us, ri, ci)
            wb = w.astype(bf16)
            ee = _dot_nt(dob, v_ref[pl.ds(off, B), :]) * wb.astype(f32)
            f = et - (ee + _cumsum_after(ee, us) + ce)
            r = 1.0 / (1.0 + e)
            pos = z >= 0
            sp = jnp.where(pos, r, e * r)
            sn = jnp.where(pos, e * r, r)
            dz = (jnp.where(vis, ee * sn - f * sp, 0.0) * (SB_DH ** -0.5)).astype(bf16)
            dq = dq + _dot(dz, kj)
            dk_ref[pl.ds(off, B), :] += _dot_tn(dz, q)
            dv_ref[pl.ds(off, B), :] += _dot_tn(wb, dob)
            return dq, cs + _rowsum(lk), ce + _rowsum(ee)

        z0 = jnp.zeros((B, 1), f32)
        dq, _, _ = lax.fori_loop(0, i + 1, step, (jnp.zeros((B, SB_DH), f32), z0, z0))
        dq_ref[...] = dq

    blk = pl.BlockSpec((B, SB_DH), lambda h, i: (i, h))
    full = pl.BlockSpec((tp, SB_DH), lambda h, i: (0, h))
    return pl.pallas_call(
        body, name="sb_bwd", grid=(HEADS, nq), in_specs=[blk, full, full, blk, blk], out_specs=[blk, full, full],
        out_shape=[jax.ShapeDtypeStruct((tp, SB_W), f32)] * 3, compiler_params=_params(2),
    )(qs, ks, vs, o, do)


def adamw(w, g, m, v, name):
    r, c = w.shape
    rt = _tile(r, 128, SUB) if r % SUB == 0 else r
    blk = pl.BlockSpec((rt, c), lambda i: (i, 0))
    c1 =1.0 - ADAM_B1 ** ADAM_STEP
    c2 = 1.0 - ADAM_B2 ** ADAM_STEP

    def body(w_ref, g_ref, m_ref, v_ref, d_ref, mo_ref, vo_ref):
        g_ = g_ref[...]
        m_ = ADAM_B1 * m_ref[...] + (1.0 - ADAM_B1) * g_
        v_ = ADAM_B2 * v_ref[...] + (1.0 - ADAM_B2) * (g_ * g_)
        mo_ref[...] = m_
        vo_ref[...] = v_
        d_ref[...] = -ADAM_LR * ((m_ / c1) / (jnp.sqrt(v_ / c2) + ADAM_EPS) + ADAM_WD * w_ref[...])

    return pl.pallas_call(
        body, name=name, grid=(r // rt,), in_specs=[blk] * 4, out_specs=[blk] * 3,
        out_shape=[jax.ShapeDtypeStruct((r, c), f32)] * 3, compiler_params=_params(1),
    )(w, g, m, v)


def sum_slots(x, name):
    n, r, c = x.shape
    rt = _tile(r, 128, SUB) if r % SUB == 0 else r
    blk = pl.BlockSpec((n, rt, c), lambda i: (0, i, 0))

    def body(x_ref, o_ref):
        acc = x_ref[0]
        for s in range(1, n):
            acc = acc + x_ref[s]
        o_ref[...] = acc

    return pl.pallas_call(
        body, name=name, grid=(r // rt,), in_specs=[blk], out_specs=pl.BlockSpec((rt, c), lambda i: (i, 0)),
        out_shape=jax.ShapeDtypeStruct((r, c), f32), compiler_params=_params(1),
    )(x)


def add2(a, b, name):
    n, r, c = a.shape
    rt = _tile(r, 64, SUB) if r % SUB == 0 else r
    blk = pl.BlockSpec((n, rt, c), lambda i: (0, i, 0))

    def body(a_ref, b_ref, o_ref):
        o_ref[...] = a_ref[...] + b_ref[...]

    return pl.pallas_call(
        body, name=name, grid=(r // rt,), in_specs=[blk, blk], out_specs=blk,
        out_shape=jax.ShapeDtypeStruct((n, r, c), f32), compiler_params=_params(1),
    )(a, b)


_ANY = pl.BlockSpec(memory_space=pl.ANY)
_MESH = pl.DeviceIdType.MESH


def _coords():
    return lax.axis_index("x"), lax.axis_index("y"), lax.axis_index("c")


def _chip_peer(x, y, r):
    return x ^ (r >> 1), y ^ (r & 1)


def gather_chips(shards, name):
    n = len(shards)

    def body(*refs):
        src, dst = refs[:n], refs[n:2 * n]
        send, recv, loc = refs[2 * n:]
        x, y, c = _coords()
        me = 2 * x + y
        locs = [pltpu.make_async_copy(src[t], dst[t].at[me], loc.at[t]) for t in range(n)]
        for cp in locs:
            cp.start()
        outs = []
        for t in range(n):
            for r in range(1, N_CHIPS):
                px, py = _chip_peer(x, y, r)
                s = t * (N_CHIPS - 1) + r - 1
                outs.append(pltpu.make_async_remote_copy(src[t], dst[t].at[me], send.at[s], recv.at[s],
                                                         device_id=(px, py, c), device_id_type=_MESH))
        for cp in outs:
            cp.start()
        for t in range(n):
            for r in range(1, N_CHIPS):
                px, py = _chip_peer(x, y, r)
                s = t * (N_CHIPS - 1) + r - 1
                pltpu.make_async_remote_copy(src[t], dst[t].at[2 * px + py], send.at[s], recv.at[s],
                                             device_id=(px, py, c), device_id_type=_MESH).wait_recv()
        for cp in outs:
            cp.wait_send()
        for cp in locs:
            cp.wait()

    k = n * (N_CHIPS - 1)
    return pl.pallas_call(
        body, name=name, in_specs=[_ANY] * n, out_specs=[_ANY] * n,
        out_shape=[jax.ShapeDtypeStruct((N_CHIPS,) + s.shape, s.dtype) for s in shards],
        scratch_shapes=[pltpu.SemaphoreType.DMA((k,)), pltpu.SemaphoreType.DMA((k,)), pltpu.SemaphoreType.DMA((n,))],
    )(*shards)


def sibling_swap_halves(grads, name):
    n = len(grads)

    def body(*refs):
        src, dst = refs[:n], refs[n:2 * n]
        send, recv = refs[2 * n:]
        x, y, c = _coords()
        cps = []
        for t in range(n):
            for o in range(N_CHIPS):
                s = t * N_CHIPS + o
                cps.append(pltpu.make_async_remote_copy(src[t].at[o, 1 - c], dst[t].at[o], send.at[s], recv.at[s],
                                                        device_id=(x, y, 1 - c), device_id_type=_MESH))
        for cp in cps:
            cp.start()
        for cp in cps:
            cp.wait_recv()
        for cp in cps:
            cp.wait_send()

    k = n * N_CHIPS
    return pl.pallas_call(
        body, name=name, in_specs=[_ANY] * n, out_specs=[_ANY] * n,
        out_shape=[jax.ShapeDtypeStruct((N_CHIPS,) + g.shape[2:], g.dtype) for g in grads],
        scratch_shapes=[pltpu.SemaphoreType.DMA((k,)), pltpu.SemaphoreType.DMA((k,))],
    )(*grads)


def scatter_chips(parts, name):
    n = len(parts)

    def body(*refs):
        src, dst = refs[:n], refs[n:2 * n]
        send, recv, loc = refs[2 * n:]
        x, y, c = _coords()
        me = 2 * x + y
        locs = [pltpu.make_async_copy(src[t].at[me], dst[t].at[me], loc.at[t]) for t in range(n)]
        for cp in locs:
            cp.start()
        outs = []
        for t in range(n):
            for r in range(1, N_CHIPS):
                px, py = _chip_peer(x, y, r)
                s = t * (N_CHIPS - 1) + r - 1
                outs.append(pltpu.make_async_remote_copy(src[t].at[2 * px + py], dst[t].at[me], send.at[s], recv.at[s],
                                                         device_id=(px, py, c), device_id_type=_MESH))
        for cp in outs:
            cp.start()
        for t in range(n):
            for r in range(1, N_CHIPS):
                px, py = _chip_peer(x, y, r)
                s = t * (N_CHIPS - 1) + r - 1
                pltpu.make_async_remote_copy(src[t].at[me], dst[t].at[2 * px + py], send.at[s], recv.at[s],
                                             device_id=(px, py, c), device_id_type=_MESH).wait_recv()
        for cp in outs:
            cp.wait_send()
        for cp in locs:
            cp.wait()

    k = n * (N_CHIPS - 1)
    return pl.pallas_call(
        body, name=name, in_specs=[_ANY] * n, out_specs=[_ANY] * n,
        out_shape=[jax.ShapeDtypeStruct(p.shape, p.dtype) for p in parts],
        scratch_shapes=[pltpu.SemaphoreType.DMA((k,)), pltpu.SemaphoreType.DMA((k,)), pltpu.SemaphoreType.DMA((n,))],
    )(*parts)


def sibling_gather(halves, name):
    n = len(halves)

    def body(*refs):
        src, dst = refs[:n], refs[n:2 * n]
        send, recv, loc = refs[2 * n:]
        x, y, c = _coords()
        locs = [pltpu.make_async_copy(src[t], dst[t].at[c], loc.at[t]) for t in range(n)]
        for cp in locs:
            cp.start()
        outs = [pltpu.make_async_remote_copy(src[t], dst[t].at[c], send.at[t], recv.at[t],
                                             device_id=(x, y, 1 - c), device_id_type=_MESH) for t in range(n)]
        for cp in outs:
            cp.start()
        for t in range(n):
            pltpu.make_async_remote_copy(src[t], dst[t].at[1 - c], send.at[t], recv.at[t],
                                         device_id=(x, y, 1 - c), device_id_type=_MESH).wait_recv()
        for cp in outs:
            cp.wait_send()
        for cp in locs:
            cp.wait()

    return pl.pallas_call(
        body, name=name, in_specs=[_ANY] * n, out_specs=[_ANY] * n,
        out_shape=[jax.ShapeDtypeStruct((2,) + h.shape, h.dtype) for h in halves],
        scratch_shapes=[pltpu.SemaphoreType.DMA((n,)), pltpu.SemaphoreType.DMA((n,)), pltpu.SemaphoreType.DMA((n,))],
    )(*halves)


def gather_all(block, name):
    def body(src, dst, send, recv, loc):
        x, y, c = _coords()
        me = 4 * x + 2 * y + c
        mine = pltpu.make_async_copy(src, dst.at[me], loc)
        mine.start()
        outs = []
        for r in range(1, N_DEV):
            peer = (x ^ (r >> 2), y ^ ((r >> 1) & 1), c ^ (r & 1))
            outs.append(pltpu.make_async_remote_copy(src, dst.at[me], send.at[r - 1], recv.at[r - 1],
                                                     device_id=peer, device_id_type=_MESH))
        for cp in outs:
            cp.start()
        for r in range(1, N_DEV):
            px, py, pc = x ^ (r >> 2), y ^ ((r >> 1) & 1), c ^ (r & 1)
            pltpu.make_async_remote_copy(src, dst.at[4 * px + 2 * py + pc], send.at[r - 1], recv.at[r - 1],
                                         device_id=(px, py, pc), device_id_type=_MESH).wait_recv()
        for cp in outs:
            cp.wait_send()
        mine.wait()

    return pl.pallas_call(
        body, name=name, in_specs=[_ANY], out_specs=_ANY,
        out_shape=jax.ShapeDtypeStruct((N_DEV,) + block.shape, block.dtype),
        scratch_shapes=[pltpu.SemaphoreType.DMA((N_DEV - 1,)), pltpu.SemaphoreType.DMA((N_DEV - 1,)),
                        pltpu.SemaphoreType.DMA(())],
    )(block)


def _pad_lanes(v, n=LANE):
    return jnp.pad(v, ((0, 0), (0, n - v.shape[1])))


def _conv_w8(w):
    return jnp.pad(w, ((0, SUB - DN_CONV), (0, 0)))


def _row_layout(gc, tp):
    nc = tp // CHUNK
    g = gc[:, :, 0].reshape(HEADS, nc, 1, CHUNK)
    g = jnp.broadcast_to(g, (HEADS, nc, SUB, CHUNK))
    return jnp.pad(g, ((0, 0), (0, 0), (0, 0), (0, LANE - CHUNK))).reshape(HEADS, nc * SUB, LANE)


def _step(x, meta, W, target):
    seq = x.shape[0]
    tp = P0 + N_META + seq
    h0 = jnp.concatenate([jnp.zeros((P0, D_MODEL), f32), meta, x], axis=0)
    w_in = W["w_in"]
    w_big = jnp.concatenate([w_in[:, :AB_COL], w_in[:, AB_COL + 2 * HEADS:]], axis=1)
    w_ab = _pad_lanes(w_in[:, AB_COL:AB_COL + 2 * HEADS])
    cq8, ck8, cv8 = _conv_w8(W["conv_q"]), _conv_w8(W["conv_k"]), _conv_w8(W["conv_v"])
    al, dtb = _pad_lanes(W["dn_a_log"]), _pad_lanes(W["dn_dt_bias"])

    n1 = rms_fwd(h0, W["norm_mix_gain"], "rms1_fwd")
    proj = matmul(n1, w_big, "nn", "proj_fwd")
    pab = matmul(n1, w_ab, "nn", "pab_fwd")
    qn = conv_fwd(proj, cq8, C_DQ * 8, 8, True, "conv_q_fwd")
    kn = conv_fwd(proj, ck8, C_DK * 8, 8, True, "conv_k_fwd")
    va = conv_fwd(proj, cv8, C_DV * 8, 16, False, "conv_v_fwd")
    gc, bc = ab_fwd(pab, al, dtb)
    grow = _row_layout(gc, tp)
    o_dn, states = gdn_fwd(qn, kn, va, gc, bc, grow)
    on = dn_out_fwd(o_dn, proj, W["dn_out_norm_gain"])
    qs, ks, vs = sb_prep_fwd(proj, W["sb_q_norm_gain"], W["sb_k_norm_gain"])
    o_sb = sb_fwd(qs, ks, vs)
    o_sb16 = cast_bf16(o_sb, "o_sb_cast")
    ydn = matmul(on, W["w_branch_dn"], "nn", "ydn_fwd")
    ysb = matmul(o_sb16, W["w_branch_sb"], "nn", "ysb_fwd")
    merged = merge_fwd(proj, ydn, ysb)
    h1 = matmul(merged, W["w_out"], "nn", "wout_fwd", residual=h0)
    n2 = rms_fwd(h1, W["norm_ffn_gain"], "rms2_fwd")
    u = matmul(n2, W["w_ffn_in"], "nn", "ffn_in_fwd", tn_t=512)
    act = swiglu_fwd(u)
    y = matmul(act, W["w_ffn_out"], "nn", "ffn_out_fwd", residual=h1)
    dy, loss = loss_head(y, target)

    G = {}
    dy16 = cast_bf16(dy, "dy_cast")
    dact = matmul(dy16, W["w_ffn_out"], "nt", "ffn_out_dx", tn_t=1408)
    G["w_ffn_out"] = matmul(act, dy16, "tn", "ffn_out_dw", tm_t=1408)
    dgate, dup = swiglu_bwd(u, dact)
    du = jnp.concatenate([dgate, dup], axis=1)
    dn2 = matmul(du, W["w_ffn_in"], "nt", "ffn_in_dx", tk_t=512)
    G["w_ffn_in"] = matmul(n2, du, "tn", "ffn_in_dw", tn_t=512)
    dh1, G["norm_ffn_gain"] = rms_bwd(h1, W["norm_ffn_gain"], dn2, dy, "rms2_bwd")
    dh1_16 = cast_bf16(dh1, "dh1_cast")
    dmerged = matmul(dh1_16, W["w_out"], "nt", "wout_dx")
    G["w_out"] = matmul(merged, dh1_16, "tn", "wout_dw")
    dyd, dys, dgd, dgs = merge_bwd(proj, ydn, ysb, dmerged)
    don = matmul(dyd, W["w_branch_dn"], "nt", "ydn_dx")
    G["w_branch_dn"] = matmul(on, dyd, "tn", "ydn_dw")
    do_sb = matmul(dys, W["w_branch_sb"], "nt", "ysb_dx")
    G["w_branch_sb"] = matmul(o_sb16, dys, "tn", "ysb_dw")
    do_dn, dz, G["dn_out_norm_gain"] = dn_out_bwd(o_dn, proj, W["dn_out_norm_gain"], don)
    dqn, dkn, dva, dgc, dbc = gdn_bwd(qn, kn, va, gc, bc, grow, states, do_dn)
    dpab, dal, ddt = ab_bwd(pab, al, dtb, dgc, dbc)
    G["dn_a_log"], G["dn_dt_bias"] = dal[:, :HEADS], ddt[:, :HEADS]
    dyq, dcq = conv_bwd_act(proj, cq8, dqn, C_DQ * 8, 8, True, "conv_q_bwd")
    dyk, dck = conv_bwd_act(proj, ck8, dkn, C_DK * 8, 8, True, "conv_k_bwd")
    dyv, dcv = conv_bwd_act(proj, cv8, dva, C_DV * 8, 16, False, "conv_v_bwd")
    G["conv_q"], G["conv_k"], G["conv_v"] = dcq[:DN_CONV], dck[:DN_CONV], dcv[:DN_CONV]
    d_dq = conv_bwd_in(dyq, cq8, "conv_q_dx")
    d_dk = conv_bwd_in(dyk, ck8, "conv_k_dx")
    d_dv = conv_bwd_in(dyv, cv8, "conv_v_dx")
    dqs, dks, dvs = sb_bwd(qs, ks, vs, o_sb, do_sb)
    d_sq, d_sk, G["sb_q_norm_gain"], G["sb_k_norm_gain"] = sb_prep_bwd(
        proj, W["sb_q_norm_gain"], W["sb_k_norm_gain"], dqs, dks)
    d_sv = cast_bf16(dvs, "dvs_cast")
    dproj = jnp.concatenate([d_dq, d_dk, d_dv, dz, d_sq, d_sk, d_sv, dgd, dgs], axis=1)
    dn1 = matmul(dproj, w_big, "nt", "proj_dx", tk_t=1024)
    dn1 = matmul(dpab, w_ab, "nt", "pab_dx", residual=dn1)
    dw_big = matmul(n1, dproj, "tn", "proj_dw")
    dw_ab = matmul(n1, dpab, "tn", "pab_dw")
    G["w_in"] = jnp.concatenate([dw_big[:, :AB_COL], dw_ab[:, :2 * HEADS], dw_big[:, AB_COL:]], axis=1)
    dh0, G["norm_mix_gain"] = rms_bwd(h0, W["norm_mix_gain"], dn1, dh1, "rms1_bwd")
    G["meta_tokens"] = dh0[P0:P0 + N_META]
    return loss, dh0[P0 + N_META:], G


_BIG = ("w_in", "w_branch_dn", "w_branch_sb", "w_out", "w_ffn_in", "w_ffn_out")
_COL_SHARDED = ("w_in", "w_ffn_in", "meta_tokens", "conv_q", "conv_k", "conv_v")
_SMALL_REPL = ("norm_mix_gain", "norm_ffn_gain", "dn_a_log", "dn_dt_bias", "dn_out_norm_gain", "sb_q_norm_gain",
               "sb_k_norm_gain")
_SMALL_SHARD = ("meta_tokens", "conv_q", "conv_k", "conv_v")
_ORDER = ("meta_tokens", "norm_mix_gain", "w_in", "conv_q", "conv_k", "conv_v", "dn_a_log", "dn_dt_bias",
          "dn_out_norm_gain", "sb_q_norm_gain", "sb_k_norm_gain", "w_branch_dn", "w_branch_sb", "w_out",
          "norm_ffn_gain", "w_ffn_in", "w_ffn_out")


def _unshard(g4, name):
    if name in _COL_SHARDED:
        r, cs = g4.shape[1:]
        return jnp.transpose(g4, (1, 0, 2)).reshape(r, N_CHIPS * cs)
    return g4.reshape((-1,) + g4.shape[2:])


def _to_shards(full, name):
    if name in _COL_SHARDED:
        r, c = full.shape
        return jnp.transpose(full.reshape(r, N_CHIPS, c // N_CHIPS), (1, 0, 2))
    r, c = full.shape
    return full.reshape(N_CHIPS, r // N_CHIPS, c)


def _rows_1024(a):
    r, c = a.shape
    if c >= 1024:
        return a.reshape(r * (c // 1024), 1024)
    return jnp.pad(a, ((0, 0), (0, 1024 - c)))


def kernel(x, meta_tokens, norm_mix_gain, w_in, conv_q, conv_k, conv_v, dn_a_log, dn_dt_bias, dn_out_norm_gain, sb_q_norm_gain, sb_k_norm_gain, w_branch_dn, w_branch_sb, w_out, norm_ffn_gain, w_ffn_in, w_ffn_out, loss_target, m_meta_tokens, m_norm_mix_gain, m_w_in, m_conv_q, m_conv_k, m_conv_v, m_dn_a_log, m_dn_dt_bias, m_dn_out_norm_gain, m_sb_q_norm_gain, m_sb_k_norm_gain, m_w_branch_dn, m_w_branch_sb, m_w_out, m_norm_ffn_gain, m_w_ffn_in, m_w_ffn_out, v_meta_tokens, v_norm_mix_gain, v_w_in, v_conv_q, v_conv_k, v_conv_v, v_dn_a_log, v_dn_dt_bias, v_dn_out_norm_gain, v_sb_q_norm_gain, v_sb_k_norm_gain, v_w_branch_dn, v_w_branch_sb, v_w_out, v_norm_ffn_gain, v_w_ffn_in, v_w_ffn_out):
    Wl = dict(meta_tokens=meta_tokens, norm_mix_gain=norm_mix_gain, w_in=w_in[0], conv_q=conv_q[0], conv_k=conv_k[0],
              conv_v=conv_v[0], dn_a_log=dn_a_log, dn_dt_bias=dn_dt_bias, dn_out_norm_gain=dn_out_norm_gain,
              sb_q_norm_gain=sb_q_norm_gain, sb_k_norm_gain=sb_k_norm_gain, w_branch_dn=w_branch_dn[0],
              w_branch_sb=w_branch_sb[0], w_out=w_out[0], norm_ffn_gain=norm_ffn_gain, w_ffn_in=w_ffn_in[0],
              w_ffn_out=w_ffn_out[0])
    Ml = dict(meta_tokens=m_meta_tokens, norm_mix_gain=m_norm_mix_gain, w_in=m_w_in[0], conv_q=m_conv_q[0],
              conv_k=m_conv_k[0], conv_v=m_conv_v[0], dn_a_log=m_dn_a_log, dn_dt_bias=m_dn_dt_bias,
              dn_out_norm_gain=m_dn_out_norm_gain, sb_q_norm_gain=m_sb_q_norm_gain, sb_k_norm_gain=m_sb_k_norm_gain,
              w_branch_dn=m_w_branch_dn[0], w_branch_sb=m_w_branch_sb[0], w_out=m_w_out[0],
              norm_ffn_gain=m_norm_ffn_gain, w_ffn_in=m_w_ffn_in[0], w_ffn_out=m_w_ffn_out[0])
    Vl = dict(meta_tokens=v_meta_tokens, norm_mix_gain=v_norm_mix_gain, w_in=v_w_in[0], conv_q=v_conv_q[0],
              conv_k=v_conv_k[0], conv_v=v_conv_v[0], dn_a_log=v_dn_a_log, dn_dt_bias=v_dn_dt_bias,
              dn_out_norm_gain=v_dn_out_norm_gain, sb_q_norm_gain=v_sb_q_norm_gain, sb_k_norm_gain=v_sb_k_norm_gain,
              w_branch_dn=v_w_branch_dn[0], w_branch_sb=v_w_branch_sb[0], w_out=v_w_out[0],
              norm_ffn_gain=v_norm_ffn_gain, w_ffn_in=v_w_ffn_in[0], w_ffn_out=v_w_ffn_out[0])
    lead = {n: (1,) if (n in _BIG or n in ("conv_q", "conv_k", "conv_v")) else () for n in _ORDER}

    names = list(_BIG) + list(_SMALL_SHARD)
    shards = [Wl[n].astype(bf16) if n in _BIG else Wl[n] for n in names]
    gathered = gather_chips(shards, "gather_weights")
    W = dict(Wl)
    for n, g4 in zip(names, gathered):
        W[n] = _unshard(g4, n)

    loss, grad_x, G = _step(x[0], W["meta_tokens"], W, loss_target[0])

    c = lax.axis_index("c")
    g4 = [_to_shards(G[n], n) for n in _BIG]
    g42 = [g.reshape(N_CHIPS, 2, g.shape[1] // 2, g.shape[2]) for g in g4]
    from_sib = sibling_swap_halves(g42, "grad_sibling_swap")
    mine = [lax.dynamic_index_in_dim(g, c, axis=1, keepdims=False) for g in g42]
    chip_part = [add2(a, b, "grad_pair_add_%d" % t) for t, (a, b) in enumerate(zip(mine, from_sib))]
    slots = scatter_chips(chip_part, "grad_chip_exchange")
    halves = [sum_slots(s, "grad_chip_sum_%d" % t) for t, s in enumerate(slots)]
    both = sibling_gather(halves, "grad_sibling_gather")
    Gs = {n: b.reshape(b.shape[0] * b.shape[1], b.shape[2]) for n, b in zip(_BIG, both)}

    small_names = list(_SMALL_REPL) + list(_SMALL_SHARD)
    pieces = [_rows_1024(G[n]) for n in small_names] + [_rows_1024(loss)]
    counts = [p.shape[0] for p in pieces]
    pack = jnp.concatenate(pieces, axis=0)
    pad_rows = (-pack.shape[0]) % SUB
    pack = jnp.pad(pack, ((0, pad_rows), (0, 0)))
    total = sum_slots(gather_all(pack, "small_gather"), "small_sum")
    chip = 2 * lax.axis_index("x") + lax.axis_index("y")
    row = 0
    for n, cnt in zip(small_names, counts[:-1]):
        blk = total[row:row + cnt]
        row += cnt
        full_shape = G[n].shape
        if full_shape[1] >= 1024:
            blk = blk.reshape(full_shape)
        else:
            blk = blk[:, :full_shape[1]]
        if n in _SMALL_SHARD:
            cs = full_shape[1] // N_CHIPS
            blk = lax.dynamic_slice_in_dim(blk, chip * cs, cs, axis=1)
        Gs[n] = blk
    loss_out = total[row, 0]

    grads, deltas, new_m, new_v = [], [], [], []
    for n in _ORDER:
        d, m2, v2 = adamw(Wl[n], Gs[n], Ml[n], Vl[n], "adamw_" + n)
        shape = lead[n] + Wl[n].shape
        grads.append(Gs[n].reshape(shape))
        deltas.append(d.reshape(shape))
        new_m.append(m2.reshape(shape))
        new_v.append(v2.reshape(shape))
    return (loss_out, grad_x[None], *grads, *deltas, *new_m, *new_v)
```

```python
import jax
import jax.numpy as jnp
from jax import lax
from jax.experimental import pallas as pl
from jax.experimental.pallas import tpu as pltpu

f32 = jnp.float32
bf16 = jnp.bfloat16

D_MODEL = 1024
N_META = 16
CHUNK = 64
HEADS = 8
DN_DK = 128
DN_DV = 256
DN_CONV = 4
DN_QK = HEADS * DN_DK
DN_V = HEADS * DN_DV
SB_DH = 128
SB_W = HEADS * SB_DH
SB_BLOCK = 128
SB_GROUP = 4
GDN_HEADS_PER_STEP = 4
D_FF = 2816
RMS_EPS = 1e-6
L2_EPS = 1e-6
ADAM_LR = 0.001
ADAM_B1 = 0.9
ADAM_B2 = 0.999
ADAM_EPS = 1e-08
ADAM_WD = 0.01
ADAM_STEP = 10

P0 = 112
LANE = 128
SUB = 8
VMEM_LIMIT = 48 * 1024 * 1024
N_CHIPS = 4
N_DEV = 8

C_DQ, C_DK, C_DV, C_DZ, C_SQ, C_SK, C_SV, C_GDN, C_GSB = 0, 1, 2, 4, 6, 7, 8, 9, 10
PROJ_BIG = 11 * 1024
AB_COL = 2 * DN_QK + 2 * DN_V


def _params(n_axes):
    return pltpu.CompilerParams(dimension_semantics=("arbitrary",) * n_axes, vmem_limit_bytes=VMEM_LIMIT)


def _tile(n, target, q=LANE):
    best = None
    for t in range(q, min(n, target) + 1, q):
        if n % t == 0:
            best = t
    return best if best is not None else n


def _dot(a, b):
    return jnp.dot(a.astype(bf16), b.astype(bf16), preferred_element_type=f32)


def _dot_nt(a, b):
    return lax.dot_general(a.astype(bf16), b.astype(bf16), (((1,), (1,)), ((), ())), preferred_element_type=f32)


def _dot_tn(a, b):
    return lax.dot_general(a.astype(bf16), b.astype(bf16), (((0,), (0,)), ((), ())), preferred_element_type=f32)


_HI = lax.Precision.HIGHEST


def _hdot(a, b):
    return jnp.dot(a, b, precision=_HI, preferred_element_type=f32)


def _hdot_nt(a, b):
    return lax.dot_general(a, b, (((1,), (1,)), ((), ())), precision=_HI, preferred_element_type=f32)


def _hdot_tn(a, b):
    return lax.dot_general(a, b, (((0,), (0,)), ((), ())), precision=_HI, preferred_element_type=f32)


def _sigmoid(x):
    e = jnp.exp(-jnp.abs(x))
    r = 1.0 / (1.0 + e)
    return jnp.where(x >= 0, r, e * r)


def _log1p_small(e):
    return jnp.where(e < 1e-3, e * (1.0 - e * (0.5 - e * (1.0 / 3.0))), jnp.log(1.0 + e))


def _rowsum(x):
    return jnp.sum(x, axis=1, keepdims=True)


def _allsum(x):
    return jnp.sum(jnp.sum(x, axis=1, keepdims=True), axis=0, keepdims=True)


def matmul(a, b, mode, name, residual=None, out_dtype=f32, tm_t=1408, tn_t=1024, tk_t=1408):
    if mode == "nn":
        (M, K), (K2, N) = a.shape, b.shape
    elif mode == "nt":
        (M, K), (N, K2) = a.shape, b.shape
    else:
        (K, M), (K2, N) = a.shape, b.shape
    assert K == K2, (a.shape, b.shape, mode)
    tm, tn, tk = _tile(M, tm_t), _tile(N, tn_t), _tile(K, tk_t)
    nk = K // tk
    if mode == "nn":
        a_spec = pl.BlockSpec((tm, tk), lambda i, j, k: (i, k))
        b_spec = pl.BlockSpec((tk, tn), lambda i, j, k: (k, j))
        dims = (((1,), (0,)), ((), ()))
    elif mode == "nt":
        a_spec = pl.BlockSpec((tm, tk), lambda i, j, k: (i, k))
        b_spec = pl.BlockSpec((tn, tk), lambda i, j, k: (j, k))
        dims = (((1,), (1,)), ((), ()))
    else:
        a_spec = pl.BlockSpec((tk, tm), lambda i, j, k: (k, i))
        b_spec = pl.BlockSpec((tk, tn), lambda i, j, k: (k, j))
        dims = (((0,), (0,)), ((), ()))
    o_spec = pl.BlockSpec((tm, tn), lambda i, j, k: (i, j))
    has_res = residual is not None

    def body(*refs):
        if has_res:
            a_ref, b_ref, r_ref, o_ref, acc_ref = refs
        else:
            a_ref, b_ref, o_ref, acc_ref = refs
        k = pl.program_id(2)

        @pl.when(k == 0)
        def _():
            acc_ref[...] = jnp.zeros_like(acc_ref)

        acc_ref[...] += lax.dot_general(a_ref[...].astype(bf16), b_ref[...].astype(bf16), dims,
                                        preferred_element_type=f32)

        @pl.when(k == nk - 1)
        def _():
            r = acc_ref[...]
            if has_res:
                r = r + r_ref[...]
            o_ref[...] = r.astype(out_dtype)

    ins = [a, b] + ([residual] if has_res else [])
    specs = [a_spec, b_spec] + ([o_spec] if has_res else [])
    return pl.pallas_call(
        body, name=name, grid=(M // tm, N // tn, nk), in_specs=specs, out_specs=o_spec,
        out_shape=jax.ShapeDtypeStruct((M, N), out_dtype),
        scratch_shapes=[pltpu.VMEM((tm, tn), f32)], compiler_params=_params(3),
    )(*ins)


def _row_tile(tp):
    return _tile(tp, 512)


def rms_fwd(h, gain, name):
    tp, d = h.shape
    rt = _row_tile(tp)

    def body(h_ref, g_ref, o_ref):
        x = h_ref[...]
        r = lax.rsqrt(jnp.mean(x * x, axis=-1, keepdims=True) + RMS_EPS)
        o_ref[...] = (x * r * g_ref[...]).astype(bf16)

    return pl.pallas_call(
        body, name=name, grid=(tp // rt,),
        in_specs=[pl.BlockSpec((rt, d), lambda i: (i, 0)), pl.BlockSpec((1, d), lambda i: (0, 0))],
        out_specs=pl.BlockSpec((rt, d), lambda i: (i, 0)),
        out_shape=jax.ShapeDtypeStruct((tp, d), bf16), compiler_params=_params(1),
    )(h, gain)


def rms_bwd(h, gain, dn, dres, name):
    tp, d = h.shape
    rt = _row_tile(tp)

    def body(h_ref, g_ref, dn_ref, dr_ref, dh_ref, dg_ref):
        i = pl.program_id(0)
        x = h_ref[...]
        r = lax.rsqrt(jnp.mean(x * x, axis=-1, keepdims=True) + RMS_EPS)
        xh = x * r
        dn_ = dn_ref[...]
        dxh = dn_ * g_ref[...]
        dh_ref[...] = r * (dxh - xh * jnp.mean(dxh * xh, axis=-1, keepdims=True)) + dr_ref[...]
        part = jnp.sum(dn_ * xh, axis=0, keepdims=True)

        @pl.when(i == 0)
        def _():
            dg_ref[...] = part

        @pl.when(i > 0)
        def _():
            dg_ref[...] += part

    row = pl.BlockSpec((rt, d), lambda i: (i, 0))
    vec = pl.BlockSpec((1, d), lambda i: (0, 0))
    return pl.pallas_call(
        body, name=name, grid=(tp // rt,), in_specs=[row, vec, row, row], out_specs=[row, vec],
        out_shape=[jax.ShapeDtypeStruct((tp, d), f32), jax.ShapeDtypeStruct((1, d), f32)],
        compiler_params=_params(1),
    )(h, gain, dn, dres)


def loss_head(y, target):
    tp, d = y.shape
    rt = P0 + N_META
    assert rt == SB_BLOCK and tp % rt == 0 and target.shape == (tp - rt, d)

    def body(y_ref, t_ref, dy_ref, l_ref):
        i = pl.program_id(0)

        @pl.when(i == 0)
        def _():
            dy_ref[...] = jnp.zeros_like(dy_ref)
            l_ref[...] = jnp.zeros_like(l_ref)

        @pl.when(i > 0)
        def _():
            err = y_ref[...] - t_ref[...]
            dy_ref[...] = err * (1.0 / d)
            l_ref[...] += jnp.broadcast_to(_allsum(err * err) * (0.5 / d), l_ref.shape)

    return pl.pallas_call(
        body, name="loss_head", grid=(tp // rt,),
        in_specs=[pl.BlockSpec((rt, d), lambda i: (i, 0)), pl.BlockSpec((rt, d), lambda i: (jnp.maximum(i - 1, 0), 0))],
        out_specs=[pl.BlockSpec((rt, d), lambda i: (i, 0)), pl.BlockSpec((1, LANE), lambda i: (0, 0))],
        out_shape=[jax.ShapeDtypeStruct((tp, d), f32), jax.ShapeDtypeStruct((1, LANE), f32)],
        compiler_params=_params(1),
    )(y, target)


def swiglu_fwd(u):
    tp = u.shape[0]
    rt, cb = _row_tile(tp), 256
    nb = D_FF // cb

    def body(g_ref, u_ref, o_ref):
        g = g_ref[...]
        o_ref[...] = (g * _sigmoid(g) * u_ref[...]).astype(bf16)

    return pl.pallas_call(
        body, name="swiglu_fwd", grid=(tp // rt, nb),
        in_specs=[pl.BlockSpec((rt, cb), lambda i, j: (i, j)), pl.BlockSpec((rt, cb), lambda i, j: (i, j + nb))],
        out_specs=pl.BlockSpec((rt, cb), lambda i, j: (i, j)),
        out_shape=jax.ShapeDtypeStruct((tp, D_FF), bf16), compiler_params=_params(2),
    )(u, u)


def swiglu_bwd(u, dact):
    tp = u.shape[0]
    rt, cb = _row_tile(tp), 256
    nb = D_FF // cb

    def body(g_ref, u_ref, da_ref, dg_ref, du_ref):
        g = g_ref[...]
        s = _sigmoid(g)
        da = da_ref[...]
        dg_ref[...] = (da * u_ref[...] * s * (1.0 + g * (1.0 - s))).astype(bf16)
        du_ref[...] = (da * g * s).astype(bf16)

    lo = pl.BlockSpec((rt, cb), lambda i, j: (i, j))
    hi = pl.BlockSpec((rt, cb), lambda i, j: (i, j + nb))
    dgate, dup = pl.pallas_call(
        body, name="swiglu_bwd", grid=(tp // rt, nb), in_specs=[lo, hi, lo], out_specs=[lo, lo],
        out_shape=[jax.ShapeDtypeStruct((tp, D_FF), bf16)] * 2, compiler_params=_params(2),
    )(u, u, dact)
    return dgate, dup


def merge_fwd(proj, ydn, ysb):
    tp = proj.shape[0]
    rt, d = _row_tile(tp), D_MODEL

    def body(gd_ref, gs_ref, yd_ref, ys_ref, o_ref):
        o_ref[...] = (_sigmoid(gd_ref[...]) * yd_ref[...] + _sigmoid(gs_ref[...]) * ys_ref[...]).astype(bf16)

    row = pl.BlockSpec((rt, d), lambda i: (i, 0))
    return pl.pallas_call(
        body, name="merge_fwd", grid=(tp // rt,),
        in_specs=[pl.BlockSpec((rt, d), lambda i: (i, C_GDN)), pl.BlockSpec((rt, d), lambda i: (i, C_GSB)), row, row],
        out_specs=row, out_shape=jax.ShapeDtypeStruct((tp, d), bf16), compiler_params=_params(1),
    )(proj, proj, ydn, ysb)


def merge_bwd(proj, ydn, ysb, dm):
    tp = proj.shape[0]
    rt, d = _row_tile(tp), D_MODEL

    def body(gd_ref, gs_ref, yd_ref, ys_ref, dm_ref, dyd_ref, dys_ref, dgd_ref, dgs_ref):
        dm_ = dm_ref[...]
        sd = _sigmoid(gd_ref[...])
        ss = _sigmoid(gs_ref[...])
        dyd_ref[...] = (dm_ * sd).astype(bf16)
        dys_ref[...] = (dm_ * ss).astype(bf16)
        dgd_ref[...] = (dm_ * yd_ref[...] * sd * (1.0 - sd)).astype(bf16)
        dgs_ref[...] = (dm_ * ys_ref[...] * ss * (1.0 - ss)).astype(bf16)

    row = pl.BlockSpec((rt, d), lambda i: (i, 0))
    return pl.pallas_call(
        body, name="merge_bwd", grid=(tp // rt,),
        in_specs=[pl.BlockSpec((rt, d), lambda i: (i, C_GDN)), pl.BlockSpec((rt, d), lambda i: (i, C_GSB)), row, row, row],
        out_specs=[row] * 4, out_shape=[jax.ShapeDtypeStruct((tp, d), bf16)] * 4, compiler_params=_params(1),
    )(proj, proj, ydn, ysb, dm)


def dn_out_fwd(o, proj, gain):
    tp = o.shape[0]
    rt, cb = _row_tile(tp), DN_DV
    zb = C_DZ * 1024 // cb

    def body(o_ref, z_ref, g_ref, y_ref):
        x = o_ref[...]
        r = lax.rsqrt(jnp.mean(x * x, axis=-1, keepdims=True) + RMS_EPS)
        z = z_ref[...]
        y_ref[...] = (x * r * g_ref[...] * (z * _sigmoid(z))).astype(bf16)

    blk = pl.BlockSpec((rt, cb), lambda i, j: (i, j))
    return pl.pallas_call(
        body, name="dn_out_fwd", grid=(tp // rt, HEADS),
        in_specs=[blk, pl.BlockSpec((rt, cb), lambda i, j: (i, j + zb)), pl.BlockSpec((1, cb), lambda i, j: (0, 0))],
        out_specs=blk, out_shape=jax.ShapeDtypeStruct((tp, DN_V), bf16), compiler_params=_params(2),
    )(o, proj, gain)


def dn_out_bwd(o, proj, gain, dy):
    tp = o.shape[0]
    rt, cb = _row_tile(tp), DN_DV
    zb = C_DZ * 1024 // cb

    def body(o_ref, z_ref, g_ref, dy_ref, do_ref, dz_ref, dg_ref):
        i, j = pl.program_id(0), pl.program_id(1)
        x = o_ref[...]
        r = lax.rsqrt(jnp.mean(x * x, axis=-1, keepdims=True) + RMS_EPS)
        xh = x * r
        z = z_ref[...]
        s = _sigmoid(z)
        dy_ = dy_ref[...]
        g = g_ref[...]
        drn = dy_ * (z * s)
        dz_ref[...] = (dy_ * xh * g * s * (1.0 + z * (1.0 - s))).astype(bf16)
        dxh = drn * g
        do_ref[...] = r * (dxh - xh * jnp.mean(dxh * xh, axis=-1, keepdims=True))
        part = jnp.sum(drn * xh, axis=0, keepdims=True)
        first = jnp.logical_and(i == 0, j == 0)

        @pl.when(first)
        def _():
            dg_ref[...] = part

        @pl.when(jnp.logical_not(first))
        def _():
            dg_ref[...] += part

    blk = pl.BlockSpec((rt, cb), lambda i, j: (i, j))
    vec = pl.BlockSpec((1, cb), lambda i, j: (0, 0))
    return pl.pallas_call(
        body, name="dn_out_bwd", grid=(tp // rt, HEADS),
        in_specs=[blk, pl.BlockSpec((rt, cb), lambda i, j: (i, j + zb)), vec, blk],
        out_specs=[blk, blk, vec],
        out_shape=[jax.ShapeDtypeStruct((tp, DN_V), f32), jax.ShapeDtypeStruct((tp, DN_V), bf16),
                   jax.ShapeDtypeStruct((1, cb), f32)],
        compiler_params=_params(2),
    )(o, proj, gain, dy)


def sb_prep_fwd(proj, gq, gk):
    tp = proj.shape[0]
    rt, cb = _row_tile(tp), SB_DH
    qb, kb, vb = (c * 1024 // cb for c in (C_SQ, C_SK, C_SV))

    def body(q_ref, k_ref, v_ref, gq_ref, gk_ref, qo_ref, ko_ref, vo_ref):
        for x_ref, g_ref, o_ref in ((q_ref, gq_ref, qo_ref), (k_ref, gk_ref, ko_ref)):
            x = x_ref[...]
            r = lax.rsqrt(jnp.mean(x * x, axis=-1, keepdims=True) + RMS_EPS)
            o_ref[...] = (x * r * g_ref[...]).astype(bf16)
        vo_ref[...] = v_ref[...].astype(bf16)

    blk = pl.BlockSpec((rt, cb), lambda i, j: (i, j))
    vec = pl.BlockSpec((1, cb), lambda i, j: (0, 0))
    return pl.pallas_call(
        body, name="sb_prep_fwd", grid=(tp // rt, HEADS),
        in_specs=[pl.BlockSpec((rt, cb), lambda i, j: (i, j + qb)), pl.BlockSpec((rt, cb), lambda i, j: (i, j + kb)),
                  pl.BlockSpec((rt, cb), lambda i, j: (i, j + vb)), vec, vec],
        out_specs=[blk] * 3, out_shape=[jax.ShapeDtypeStruct((tp, SB_W), bf16)] * 3, compiler_params=_params(2),
    )(proj, proj, proj, gq, gk)


def sb_prep_bwd(proj, gq, gk, dqs, dks):
    tp = proj.shape[0]
    rt, cb = _row_tile(tp), SB_DH
    qb, kb = (c * 1024 // cb for c in (C_SQ, C_SK))

    def body(q_ref, k_ref, gq_ref, gk_ref, dq_ref, dk_ref, dqo_ref, dko_ref, dgq_ref, dgk_ref):
        i, j = pl.program_id(0), pl.program_id(1)
        first = jnp.logical_and(i == 0, j == 0)
        for x_ref, g_ref, dn_ref, dx_ref, dg_ref in ((q_ref, gq_ref, dq_ref, dqo_ref, dgq_ref),
                                                     (k_ref, gk_ref, dk_ref, dko_ref, dgk_ref)):
            x = x_ref[...]
            r = lax.rsqrt(jnp.mean(x * x, axis=-1, keepdims=True) + RMS_EPS)
            xh = x * r
            dn_ = dn_ref[...]
            dxh = dn_ * g_ref[...]
            dx_ref[...] = (r * (dxh - xh * jnp.mean(dxh * xh, axis=-1, keepdims=True))).astype(bf16)
            part = jnp.sum(dn_ * xh, axis=0, keepdims=True)

            @pl.when(first)
            def _(dg_ref=dg_ref, part=part):
                dg_ref[...] = part

            @pl.when(jnp.logical_not(first))
            def _(dg_ref=dg_ref, part=part):
                dg_ref[...] += part

    blk = pl.BlockSpec((rt, cb), lambda i, j: (i, j))
    vec = pl.BlockSpec((1, cb), lambda i, j: (0, 0))
    return pl.pallas_call(
        body, name="sb_prep_bwd", grid=(tp // rt, HEADS),
        in_specs=[pl.BlockSpec((rt, cb), lambda i, j: (i, j + qb)), pl.BlockSpec((rt, cb), lambda i, j: (i, j + kb)),
                  vec, vec, blk, blk],
        out_specs=[blk, blk, vec, vec],
        out_shape=[jax.ShapeDtypeStruct((tp, SB_W), bf16)] * 2 + [jax.ShapeDtypeStruct((1, cb), f32)] * 2,
        compiler_params=_params(2),
    )(proj, proj, gq, gk, dqs, dks)


def cast_bf16(x, name):
    tp, d = x.shape
    rt = _row_tile(tp)
    blk = pl.BlockSpec((rt, d), lambda i: (i, 0))

    def body(x_ref, o_ref):
        o_ref[...] = x_ref[...].astype(bf16)

    return pl.pallas_call(body, name=name, grid=(tp // rt,), in_specs=[blk], out_specs=blk,
                          out_shape=jax.ShapeDtypeStruct((tp, d), bf16), compiler_params=_params(1))(x)


def _conv_taps(ext, rt):
    taps = []
    for k in range(DN_CONV):
        s = DN_CONV - 1 - k
        taps.append((pltpu.roll(ext, s, axis=0) if s else ext)[SUB:SUB + rt])
    return taps


def _conv_act(taps, w, l2):
    y = taps[0] * w[0:1]
    for k in range(1, DN_CONV):
        y = y + taps[k] * w[k:k + 1]
    s = _sigmoid(y)
    a = y * s
    if l2:
        n = lax.rsqrt(jnp.sum(a * a, axis=-1, keepdims=True) + L2_EPS)
        return y, s, a, n
    return y, s, a, None


def conv_fwd(proj, w8, col_blk, ncb, l2, name):
    tp = proj.shape[0]
    rt = _row_tile(tp)
    hb = rt // SUB

    def body(x_ref, h_ref, w_ref, o_ref):
        i = pl.program_id(1)
        halo = h_ref[...] * (i > 0).astype(f32)
        ext = jnp.concatenate([halo, x_ref[...]], axis=0)
        _, _, a, n = _conv_act(_conv_taps(ext, rt), w_ref[...], l2)
        o_ref[...] = a * n if l2 else a

    return pl.pallas_call(
        body, name=name, grid=(ncb, tp // rt),
        in_specs=[pl.BlockSpec((rt, LANE), lambda j, i: (i, j + col_blk)),
                  pl.BlockSpec((SUB, LANE), lambda j, i: (jnp.maximum(i * hb - 1, 0), j + col_blk)),
                  pl.BlockSpec((SUB, LANE), lambda j, i: (0, j))],
        out_specs=pl.BlockSpec((rt, LANE), lambda j, i: (i, j)),
        out_shape=jax.ShapeDtypeStruct((tp, ncb * LANE), f32), compiler_params=_params(2),
    )(proj, proj, w8)


def conv_bwd_act(proj, w8, dout, col_blk, ncb, l2, name):
    tp = proj.shape[0]
    rt = _row_tile(tp)
    hb = rt // SUB

    def body(x_ref, h_ref, w_ref, d_ref, dy_ref, dw_ref):
        i = pl.program_id(1)
        halo = h_ref[...] * (i > 0).astype(f32)
        ext = jnp.concatenate([halo, x_ref[...]], axis=0)
        taps = _conv_taps(ext, rt)
        y, s, a, n = _conv_act(taps, w_ref[...], l2)
        da = d_ref[...]
        if l2:
            out = a * n
            da = n * (da - out * jnp.sum(da * out, axis=-1, keepdims=True))
        dy = da * s * (1.0 + y * (1.0 - s))
        dy_ref[...] = dy
        rows = lax.broadcasted_iota(jnp.int32, (SUB, LANE), 0)
        part = jnp.zeros((SUB, LANE), f32)
        for k in range(DN_CONV):
            part = part + jnp.where(rows == k, jnp.sum(taps[k] * dy, axis=0, keepdims=True), 0.0)

        @pl.when(i == 0)
        def _():
            dw_ref[...] = part

        @pl.when(i > 0)
        def _():
            dw_ref[...] += part

    return pl.pallas_call(
        body, name=name, grid=(ncb, tp // rt),
        in_specs=[pl.BlockSpec((rt, LANE), lambda j, i: (i, j + col_blk)),
                  pl.BlockSpec((SUB, LANE), lambda j, i: (jnp.maximum(i * hb - 1, 0), j + col_blk)),
                  pl.BlockSpec((SUB, LANE), lambda j, i: (0, j)),
                  pl.BlockSpec((rt, LANE), lambda j, i: (i, j))],
        out_specs=[pl.BlockSpec((rt, LANE), lambda j, i: (i, j)), pl.BlockSpec((SUB, LANE), lambda j, i: (0, j))],
        out_shape=[jax.ShapeDtypeStruct((tp, ncb * LANE), f32), jax.ShapeDtypeStruct((SUB, ncb * LANE), f32)],
        compiler_params=_params(2),
    )(proj, proj, w8, dout)


def conv_bwd_in(dy, w8, name):
    tp, cols = dy.shape
    rt = _row_tile(tp)
    hb = rt // SUB
    nr = tp // rt
    last8 = tp // SUB - 1

    def body(d_ref, h_ref, w_ref, o_ref):
        i = pl.program_id(1)
        halo = h_ref[...] * (i < nr - 1).astype(f32)
        ext = jnp.concatenate([d_ref[...], halo], axis=0)
        w = w_ref[...]
        acc = None
        for k in range(DN_CONV):
            s = DN_CONV - 1 - k
            sh = (pltpu.roll(ext, rt + SUB - s, axis=0) if s else ext)[0:rt]
            term = sh * w[k:k + 1]
            acc = term if acc is None else acc + term
        o_ref[...] = acc.astype(bf16)

    return pl.pallas_call(
        body, name=name, grid=(cols // LANE, nr),
        in_specs=[pl.BlockSpec((rt, LANE), lambda j, i: (i, j)),
                  pl.BlockSpec((SUB, LANE), lambda j, i: (jnp.minimum((i + 1) * hb, last8), j)),
                  pl.BlockSpec((SUB, LANE), lambda j, i: (0, j))],
        out_specs=pl.BlockSpec((rt, LANE), lambda j, i: (i, j)),
        out_shape=jax.ShapeDtypeStruct((tp, cols), bf16), compiler_params=_params(2),
    )(dy, dy, w8)


def _ab_common(p, al, dtb, r0):
    rows = r0 + lax.broadcasted_iota(jnp.int32, p.shape, 0)
    mask = (rows >= P0).astype(f32)
    xx = p + dtb
    sp = jnp.maximum(xx, 0.0) + _log1p_small(jnp.exp(-jnp.abs(xx)))
    ea = jnp.exp(al)
    g = -ea * sp * mask
    beta = _sigmoid(p) * mask
    return g, beta, _sigmoid(xx), ea, mask


def ab_fwd(pab, al, dtb):
    tp = pab.shape[0]
    rt = _row_tile(tp)

    def body(p_ref, al_ref, dt_ref, g_ref, b_ref):
        i = pl.program_id(0)
        g, beta, _, _, _ = _ab_common(p_ref[...], al_ref[...], dt_ref[...], i * rt)
        for h in range(HEADS):
            g_ref[h] = jnp.broadcast_to(g[:, h:h + 1], (rt, LANE))
            b_ref[h] = jnp.broadcast_to(beta[:, HEADS + h:HEADS + h + 1], (rt, LANE))

    vec = pl.BlockSpec((1, LANE), lambda i: (0, 0))
    out = pl.BlockSpec((HEADS, rt, LANE), lambda i: (0, i, 0))
    return pl.pallas_call(
        body, name="ab_fwd", grid=(tp // rt,), in_specs=[pl.BlockSpec((rt, LANE), lambda i: (i, 0)), vec, vec],
        out_specs=[out, out], out_shape=[jax.ShapeDtypeStruct((HEADS, tp, LANE), f32)] * 2, compiler_params=_params(1),
    )(pab, al, dtb)


def ab_bwd(pab, al, dtb, dg, db):
    tp = pab.shape[0]
    rt = _row_tile(tp)

    def body(p_ref, al_ref, dt_ref, dg_ref, db_ref, dp_ref, dal_ref, ddt_ref):
        i = pl.program_id(0)
        g, beta, sx, ea, mask = _ab_common(p_ref[...], al_ref[...], dt_ref[...], i * rt)
        lanes = lax.broadcasted_iota(jnp.int32, (rt, LANE), 1)
        dgl = jnp.zeros((rt, LANE), f32)
        dbl = jnp.zeros((rt, LANE), f32)
        for h in range(HEADS):
            dgl = dgl + jnp.where(lanes == h, dg_ref[h], 0.0)
            dbl = dbl + jnp.where(lanes == HEADS + h, db_ref[h], 0.0)
        dxx = dgl * (-ea) * sx * mask
        dp_ref[...] = (dxx + dbl * beta * (1.0 - beta)).astype(bf16)
        pal = jnp.sum(dgl * g, axis=0, keepdims=True)
        pdt = jnp.sum(dxx, axis=0, keepdims=True)

        @pl.when(i == 0)
        def _():
            dal_ref[...] = pal
            ddt_ref[...] = pdt

        @pl.when(i > 0)
        def _():
            dal_ref[...] += pal
            ddt_ref[...] += pdt

    vec = pl.BlockSpec((1, LANE), lambda i: (0, 0))
    row = pl.BlockSpec((rt, LANE), lambda i: (i, 0))
    big = pl.BlockSpec((HEADS, rt, LANE), lambda i: (0, i, 0))
    return pl.pallas_call(
        body, name="ab_bwd", grid=(tp // rt,), in_specs=[row, vec, vec, big, big], out_specs=[row, vec, vec],
        out_shape=[jax.ShapeDtypeStruct((tp, LANE), bf16), jax.ShapeDtypeStruct((1, LANE), f32),
                   jax.ShapeDtypeStruct((1, LANE), f32)],
        compiler_params=_params(1),
    )(pab, al, dtb, dg, db)


class _Chunk:
    pass


def _gdn_chunk(q, k, v, gcol, bcol, grow8):
    C = CHUNK
    R = range(len(q))
    X = _Chunk()
    ri = lax.broadcasted_iota(jnp.int32, (C, C), 0)
    ci = lax.broadcasted_iota(jnp.int32, (C, C), 1)
    r2 = lax.broadcasted_iota(jnp.int32, (LANE, LANE), 0)
    c2 = lax.broadcasted_iota(jnp.int32, (LANE, LANE), 1)
    lower = (ri >= ci).astype(f32)
    upper2 = (r2 <= c2).astype(f32)
    eye = (ri == ci).astype(f32)
    gam = [_hdot(lower, gcol[h]) for h in R]
    gam_row = [_hdot(grow8[h], upper2)[0:1, 0:C] for h in R]
    X.ri, X.ci = ri, ci
    X.Dm = [jnp.where(ri >= ci, jnp.exp(jnp.minimum(gam[h][:, 0:C] - gam_row[h], 0.0)), 0.0) for h in R]
    X.eg = [jnp.exp(gam[h]) for h in R]
    gl = [gam[h][C - 1:C, :] for h in R]
    X.egl = [jnp.exp(gl[h]) for h in R]
    X.kdec = [jnp.exp(gl[h] - gam[h]) for h in R]
    X.qs = [q[h] * (DN_DK ** -0.5) for h in R]
    X.kb = [k[h] * bcol[h] for h in R]
    kk = [_dot_nt(X.kb[h], k[h]) for h in R]
    qk = [_dot_nt(X.qs[h], k[h]) for h in R]
    X.A = [jnp.where(ri > ci, kk[h] * X.Dm[h], 0.0) for h in R]
    T = [eye - X.A[h] for h in R]
    P = list(X.A)
    for _ in range(5):
        P = [_hdot(P[h], P[h]) for h in R]
        T = [T[h] + _hdot(T[h], P[h]) for h in R]
    X.T = T
    X.b2 = [jnp.concatenate([bcol[h], bcol[h]], axis=-1) for h in R]
    X.u = [_hdot(T[h], v[h] * X.b2[h]) for h in R]
    X.w = [_hdot(T[h], X.kb[h] * X.eg[h]) for h in R]
    X.attn = [qk[h] * X.Dm[h] for h in R]
    X.qg = [X.qs[h] * X.eg[h] for h in R]
    X.kg = [k[h] * X.kdec[h] for h in R]
    return X


def gdn_fwd(q, k, v, gc, bc, grow):
    tp = q.shape[0]
    nc = tp // CHUNK
    hb = GDN_HEADS_PER_STEP

    def body(q_ref, k_ref, v_ref, gc_ref, bc_ref, gr_ref, o_ref, ss_ref, S_ref):
        c = pl.program_id(1)

        @pl.when(c == 0)
        def _():
            S_ref[...] = jnp.zeros_like(S_ref)

        R = range(hb)
        qc = [slice(h * DN_DK, (h + 1) * DN_DK) for h in R]
        vc = [slice(h * DN_DV, (h + 1) * DN_DV) for h in R]
        X = _gdn_chunk([q_ref[:, qc[h]] for h in R], [k_ref[:, qc[h]] for h in R], [v_ref[:, vc[h]] for h in R],
                       [gc_ref[h] for h in R], [bc_ref[h] for h in R], [gr_ref[h] for h in R])
        S = [S_ref[h] for h in R]
        for h in R:
            ss_ref[h, 0] = S[h]
        wS = [_dot(X.w[h], S[h]) for h in R]
        qS = [_dot(X.qg[h], S[h]) for h in R]
        vn = [X.u[h] - wS[h] for h in R]
        av = [_dot(X.attn[h], vn[h]) for h in R]
        kv = [_dot_tn(X.kg[h], vn[h]) for h in R]
        for h in R:
            o_ref[:, vc[h]] = qS[h] + av[h]
            S_ref[h] = S[h] * X.egl[h][:, 0:1] + kv[h]

    qk = pl.BlockSpec((CHUNK, hb * DN_DK), lambda g, c: (c, g))
    vv = pl.BlockSpec((CHUNK, hb * DN_DV), lambda g, c: (c, g))
    col = pl.BlockSpec((hb, CHUNK, LANE), lambda g, c: (g, c, 0))
    row = pl.BlockSpec((hb, SUB, LANE), lambda g, c: (g, c, 0))
    return pl.pallas_call(
        body, name="gdn_fwd", grid=(HEADS // hb, nc), in_specs=[qk, qk, vv, col, col, row],
        out_specs=[vv, pl.BlockSpec((hb, 1, DN_DK, DN_DV), lambda g, c: (g, c, 0, 0))],
        out_shape=[jax.ShapeDtypeStruct((tp, DN_V), f32), jax.ShapeDtypeStruct((HEADS, nc, DN_DK, DN_DV), f32)],
        scratch_shapes=[pltpu.VMEM((hb, DN_DK, DN_DV), f32)], compiler_params=_params(2),
    )(q, k, v, gc, bc, grow)


def gdn_bwd(q, k, v, gc, bc, grow, states, do):
    tp = q.shape[0]
    nc = tp // CHUNK
    C = CHUNK
    hb = GDN_HEADS_PER_STEP

    def body(q_ref, k_ref, v_ref, gc_ref, bc_ref, gr_ref, ss_ref, do_ref, dq_ref, dk_ref, dv_ref, dg_ref, db_ref, dS_ref):
        c = pl.program_id(1)

        @pl.when(c == 0)
        def _():
            dS_ref[...] = jnp.zeros_like(dS_ref)

        R = range(hb)
        qc = [slice(h * DN_DK, (h + 1) * DN_DK) for h in R]
        vc = [slice(h * DN_DV, (h + 1) * DN_DV) for h in R]
        k_ = [k_ref[:, qc[h]] for h in R]
        v_ = [v_ref[:, vc[h]] for h in R]
        bcol = [bc_ref[h] for h in R]
        X = _gdn_chunk([q_ref[:, qc[h]] for h in R], k_, v_, [gc_ref[h] for h in R], bcol, [gr_ref[h] for h in R])
        ri, ci = X.ri, X.ci
        S = [ss_ref[h, 0] for h in R]
        do_ = [do_ref[:, vc[h]] for h in R]
        dSn = [dS_ref[h] for h in R]
        wS = [_dot(X.w[h], S[h]) for h in R]
        ado = [_dot_tn(X.attn[h], do_[h]) for h in R]
        kdS = [_dot(X.kg[h], dSn[h]) for h in R]
        d_qg = [_dot_nt(do_[h], S[h]) for h in R]
        qdo = [_dot_tn(X.qg[h], do_[h]) for h in R]
        vn = [X.u[h] - wS[h] for h in R]
        d_vn = [ado[h] + kdS[h] for h in R]
        dovn = [_dot_nt(do_[h], vn[h]) for h in R]
        d_kg = [_dot_nt(vn[h], dSn[h]) for h in R]
        wdv = [_dot_tn(X.w[h], d_vn[h]) for h in R]
        dw = [-_dot_nt(d_vn[h], S[h]) for h in R]
        for h in R:
            dS_ref[h] = qdo[h] + X.egl[h][:, 0:1] * dSn[h] - wdv[h]
        dattn = [jnp.where(ri >= ci, dovn[h], 0.0) for h in R]
        dRu = [_hdot_tn(X.T[h], d_vn[h]) for h in R]
        dRw = [_hdot_tn(X.T[h], dw[h]) for h in R]
        dAu = [_hdot_nt(dRu[h], X.u[h]) for h in R]
        dAw = [_hdot_nt(dRw[h], X.w[h]) for h in R]
        dA = [jnp.where(ri > ci, -(dAu[h] + dAw[h]), 0.0) for h in R]
        dKK = [dA[h] * X.Dm[h] for h in R]
        dQK = [dattn[h] * X.Dm[h] for h in R]
        E = [dA[h] * X.A[h] + dattn[h] * X.attn[h] for h in R]
        dkb = [_dot(dKK[h], k_[h]) + dRw[h] * X.eg[h] for h in R]
        dk1 = [_dot_tn(dKK[h], X.kb[h]) for h in R]
        dqs = [_dot(dQK[h], k_[h]) + d_qg[h] * X.eg[h] for h in R]
        dk2 = [_dot_tn(dQK[h], X.qs[h]) for h in R]
        ones = jnp.ones((C, LANE), f32)
        colE = [_hdot_tn(E[h], ones) for h in R]
        rows = lax.broadcasted_iota(jnp.int32, (C, LANE), 0)
        upper = (ci >= ri).astype(f32)
        dgam = []
        for h in R:
            t = d_kg[h] * X.kg[h]
            dgl = _allsum(t) + X.egl[h][:, 0:1] * _allsum(S[h] * dSn[h])
            g = (_rowsum(E[h]) - colE[h] + _rowsum(dRw[h] * (X.kb[h] * X.eg[h])) + _rowsum(d_qg[h] * X.qg[h])
                 - _rowsum(t))
            dgam.append(g + jnp.where(rows == C - 1, dgl, 0.0))
        dg = [_hdot(upper, dgam[h]) for h in R]
        for h in R:
            dv_ref[:, vc[h]] = dRu[h] * X.b2[h]
            dbeta = _rowsum(dRu[h] * v_[h]) + _rowsum(dkb[h] * k_[h])
            dq_ref[:, qc[h]] = dqs[h] * (DN_DK ** -0.5)
            dk_ref[:, qc[h]] = dk1[h] + dk2[h] + dkb[h] * bcol[h] + d_kg[h] * X.kdec[h]
            dg_ref[h] = dg[h]
            db_ref[h] = jnp.broadcast_to(dbeta, (C, LANE))

    rc = lambda c: nc - 1 - c
    qk = pl.BlockSpec((CHUNK, hb * DN_DK), lambda g, c: (rc(c), g))
    vv = pl.BlockSpec((CHUNK, hb * DN_DV), lambda g, c: (rc(c), g))
    col = pl.BlockSpec((hb, CHUNK, LANE), lambda g, c: (g, rc(c), 0))
    row = pl.BlockSpec((hb, SUB, LANE), lambda g, c: (g, rc(c), 0))
    st = pl.BlockSpec((hb, 1, DN_DK, DN_DV), lambda g, c: (g, rc(c), 0, 0))
    return pl.pallas_call(
        body, name="gdn_bwd", grid=(HEADS // hb, nc), in_specs=[qk, qk, vv, col, col, row, st, vv],
        out_specs=[qk, qk, vv, col, col],
        out_shape=[jax.ShapeDtypeStruct((tp, DN_QK), f32), jax.ShapeDtypeStruct((tp, DN_QK), f32),
                   jax.ShapeDtypeStruct((tp, DN_V), f32), jax.ShapeDtypeStruct((HEADS, tp, LANE), f32),
                   jax.ShapeDtypeStruct((HEADS, tp, LANE), f32)],
        scratch_shapes=[pltpu.VMEM((hb, DN_DK, DN_DV), f32)], compiler_params=_params(2),
    )(q, k, v, gc, bc, grow, states, do)


def _cumsum_after(x, nb, us):
    B = SB_BLOCK
    hi = x.astype(bf16)
    lo = (x - hi.astype(f32)).astype(bf16)
    rows = [p[:, b * B:(b + 1) * B] for p in (hi, lo) for b in range(nb)]
    r = jnp.dot(jnp.concatenate(rows, axis=0), us, preferred_element_type=f32)
    out = [r[b * B:(b + 1) * B] + r[(nb + b) * B:(nb + b + 1) * B] for b in range(nb)]
    return out[0] if nb == 1 else jnp.concatenate(out, axis=1)


def _later_blocks(x, nb, carry):
    B = SB_BLOCK
    tot = [_rowsum(x[:, b * B:(b + 1) * B]) for b in range(nb)]
    offs = [None] * nb
    run = carry
    for b in range(nb - 1, -1, -1):
        offs[b] = jnp.broadcast_to(run, (B, B))
        run = run + tot[b]
    return (offs[0] if nb == 1 else jnp.concatenate(offs, axis=1)), run


def _sb_group(i, t):
    top = i - SB_GROUP * t
    jlo = jnp.maximum(top - SB_GROUP + 1, 0)
    rows = pl.ds(pl.multiple_of(jlo * SB_BLOCK, SB_BLOCK), SB_GROUP * SB_BLOCK)
    return jlo, rows, (top + 1) * SB_BLOCK


def _sb_weights(q, kcat, i, jlo, kend, cs, us):
    B, nb = SB_BLOCK, SB_GROUP
    z = _dot_nt(q, kcat) * (SB_DH ** -0.5)
    qpos = i * B + lax.broadcasted_iota(jnp.int32, (B, nb * B), 0)
    kpos = jlo * B + lax.broadcasted_iota(jnp.int32, (B, nb * B), 1)
    vis = jnp.logical_and(kpos < jnp.minimum(qpos, kend), kpos >= P0)
    e = jnp.exp(-jnp.abs(z))
    l1p = jnp.log(1.0 + e)
    lsp = jnp.minimum(z, 0.0) - l1p
    lk = jnp.where(vis, -jnp.maximum(z, 0.0) - l1p, 0.0)
    off, cs = _later_blocks(lk, nb, cs)
    w = jnp.where(vis, jnp.exp(lsp + _cumsum_after(lk, nb, us) + off), 0.0)
    return z, e, vis, w, cs


def sb_fwd(qs, ks, vs):
    tp = qs.shape[0]
    nq = tp // SB_BLOCK
    B, G = SB_BLOCK, SB_GROUP
    assert tp >= G * B

    def body(q_ref, k_ref, v_ref, o_ref):
        i = pl.program_id(1)
        q = q_ref[...]
        us = (lax.broadcasted_iota(jnp.int32, (B, B), 0) > lax.broadcasted_iota(jnp.int32, (B, B), 1)).astype(bf16)

        def step(t, carry):
            acc, cs = carry
            jlo, rows, kend = _sb_group(i, t)
            _, _, _, w, cs = _sb_weights(q, k_ref[rows, :], i, jlo, kend, cs, us)
            return acc + _dot(w, v_ref[rows, :]), cs

        acc, _ = lax.fori_loop(0, (i + G) // G, step, (jnp.zeros((B, SB_DH), f32), jnp.zeros((B, 1), f32)))
        o_ref[...] = acc

    blk = pl.BlockSpec((B, SB_DH), lambda h, i: (i, h))
    full = pl.BlockSpec((tp, SB_DH), lambda h, i: (0, h))
    return pl.pallas_call(
        body, name="sb_fwd", grid=(HEADS, nq), in_specs=[blk, full, full], out_specs=blk,
        out_shape=jax.ShapeDtypeStruct((tp, SB_W), f32), compiler_params=_params(2),
    )(qs, ks, vs)


def sb_bwd(qs, ks, vs, o, do):
    tp = qs.shape[0]
    nq = tp // SB_BLOCK
    B, G = SB_BLOCK, SB_GROUP
    assert tp >= G * B

    def body(q_ref, k_ref, v_ref, o_ref, do_ref, dq_ref, dk_ref, dv_ref):
        i = pl.program_id(1)

        @pl.when(i == 0)
        def _():
            dk_ref[...] = jnp.zeros_like(dk_ref)
            dv_ref[...] = jnp.zeros_like(dv_ref)

        q = q_ref[...]
        dob = do_ref[...].astype(bf16)
        et = _rowsum(dob.astype(f32) * o_ref[...])
        us = (lax.broadcasted_iota(jnp.int32, (B, B), 0) > lax.broadcasted_iota(jnp.int32, (B, B), 1)).astype(bf16)

        def step(t, carry):
            dq, cs, ce = carry
            jlo, rows, kend = _sb_group(i, t)
            kcat = k_ref[rows, :]
            dwv = _dot_nt(dob, v_ref[rows, :])
            z, e, vis, w, cs = _sb_weights(q, kcat, i, jlo, kend, cs, us)
            wb = w.astype(bf16)
            ee = dwv * wb.astype(f32)
            off, ce = _later_blocks(ee, G, ce)
            f = et - (ee + _cumsum_after(ee, G, us) + off)
            r = 1.0 / (1.0 + e)
            pos = z >= 0
            sp = jnp.where(pos, r, e * r)
            sn = jnp.where(pos, e * r, r)
            dz = (jnp.where(vis, ee * sn - f * sp, 0.0) * (SB_DH ** -0.5)).astype(bf16)
            dk_ref[rows, :] += _dot_tn(dz, q)
            dv_ref[rows, :] += _dot_tn(wb, dob)
            return dq + _dot(dz, kcat), cs, ce

        z0 = jnp.zeros((B, 1), f32)
        dq, _, _ = lax.fori_loop(0, (i + G) // G, step, (jnp.zeros((B, SB_DH), f32), z0, z0))
        dq_ref[...] = dq

    blk = pl.BlockSpec((B, SB_DH), lambda h, i: (i, h))
    full = pl.BlockSpec((tp, SB_DH), lambda h, i: (0, h))
    return pl.pallas_call(
        body, name="sb_bwd", grid=(HEADS, nq), in_specs=[blk, full, full, blk, blk], out_specs=[blk, full, full],
        out_shape=[jax.ShapeDtypeStruct((tp, SB_W), f32)] * 3, compiler_params=_params(2),
    )(qs, ks, vs, o, do)


def adamw(w, g, m, v, name):
    r, c = w.shape
    rt = _tile(r, 128, SUB) if r % SUB == 0 else r
    blk = pl.BlockSpec((rt, c), lambda i: (i, 0))
    c1 =1.0 - ADAM_B1 ** ADAM_STEP
    c2 = 1.0 - ADAM_B2 ** ADAM_STEP

    def body(w_ref, g_ref, m_ref, v_ref, d_ref, mo_ref, vo_ref):
        g_ = g_ref[...]
        m_ = ADAM_B1 * m_ref[...] + (1.0 - ADAM_B1) * g_
        v_ = ADAM_B2 * v_ref[...] + (1.0 - ADAM_B2) * (g_ * g_)
        mo_ref[...] = m_
        vo_ref[...] = v_
        d_ref[...] = -ADAM_LR * ((m_ / c1) / (jnp.sqrt(v_ / c2) + ADAM_EPS) + ADAM_WD * w_ref[...])

    return pl.pallas_call(
        body, name=name, grid=(r // rt,), in_specs=[blk] * 4, out_specs=[blk] * 3,
        out_shape=[jax.ShapeDtypeStruct((r, c), f32)] * 3, compiler_params=_params(1),
    )(w, g, m, v)


def sum_slots(x, name):
    n, r, c = x.shape
    rt = _tile(r, 128, SUB) if r % SUB == 0 else r
    blk = pl.BlockSpec((n, rt, c), lambda i: (0, i, 0))

    def body(x_ref, o_ref):
        acc = x_ref[0]
        for s in range(1, n):
            acc = acc + x_ref[s]
        o_ref[...] = acc

    return pl.pallas_call(
        body, name=name, grid=(r // rt,), in_specs=[blk], out_specs=pl.BlockSpec((rt, c), lambda i: (i, 0)),
        out_shape=jax.ShapeDtypeStruct((r, c), f32), compiler_params=_params(1),
    )(x)


def add2(a, b, name):
    n, r, c = a.shape
    rt = _tile(r, 64, SUB) if r % SUB == 0 else r
    blk = pl.BlockSpec((n, rt, c), lambda i: (0, i, 0))

    def body(a_ref, b_ref, o_ref):
        o_ref[...] = a_ref[...] + b_ref[...]

    return pl.pallas_call(
        body, name=name, grid=(r // rt,), in_specs=[blk, blk], out_specs=blk,
        out_shape=jax.ShapeDtypeStruct((n, r, c), f32), compiler_params=_params(1),
    )(a, b)


_ANY = pl.BlockSpec(memory_space=pl.ANY)
_MESH = pl.DeviceIdType.MESH


def _coords():
    return lax.axis_index("x"), lax.axis_index("y"), lax.axis_index("c")


def _chip_peer(x, y, r):
    return x ^ (r >> 1), y ^ (r & 1)


def gather_chips(shards, name):
    n = len(shards)

    def body(*refs):
        src, dst = refs[:n], refs[n:2 * n]
        send, recv, loc = refs[2 * n:]
        x, y, c = _coords()
        me = 2 * x + y
        locs = [pltpu.make_async_copy(src[t], dst[t].at[me], loc.at[t]) for t in range(n)]
        for cp in locs:
            cp.start()
        outs = []
        for t in range(n):
            for r in range(1, N_CHIPS):
                px, py = _chip_peer(x, y, r)
                s = t * (N_CHIPS - 1) + r - 1
                outs.append(pltpu.make_async_remote_copy(src[t], dst[t].at[me], send.at[s], recv.at[s],
                                                         device_id=(px, py, c), device_id_type=_MESH))
        for cp in outs:
            cp.start()
        for t in range(n):
            for r in range(1, N_CHIPS):
                px, py = _chip_peer(x, y, r)
                s = t * (N_CHIPS - 1) + r - 1
                pltpu.make_async_remote_copy(src[t], dst[t].at[2 * px + py], send.at[s], recv.at[s],
                                             device_id=(px, py, c), device_id_type=_MESH).wait_recv()
        for cp in outs:
            cp.wait_send()
        for cp in locs:
            cp.wait()

    k = n * (N_CHIPS - 1)
    return pl.pallas_call(
        body, name=name, in_specs=[_ANY] * n, out_specs=[_ANY] * n,
        out_shape=[jax.ShapeDtypeStruct((N_CHIPS,) + s.shape, s.dtype) for s in shards],
        scratch_shapes=[pltpu.SemaphoreType.DMA((k,)), pltpu.SemaphoreType.DMA((k,)), pltpu.SemaphoreType.DMA((n,))],
    )(*shards)


def sibling_swap_halves(grads, name):
    n = len(grads)

    def body(*refs):
        src, dst = refs[:n], refs[n:2 * n]
        send, recv = refs[2 * n:]
        x, y, c = _coords()
        cps = []
        for t in range(n):
            for o in range(N_CHIPS):
                s = t * N_CHIPS + o
                cps.append(pltpu.make_async_remote_copy(src[t].at[o, 1 - c], dst[t].at[o], send.at[s], recv.at[s],
                                                        device_id=(x, y, 1 - c), device_id_type=_MESH))
        for cp in cps:
            cp.start()
        for cp in cps:
            cp.wait_recv()
        for cp in cps:
            cp.wait_send()

    k = n * N_CHIPS
    return pl.pallas_call(
        body, name=name, in_specs=[_ANY] * n, out_specs=[_ANY] * n,
        out_shape=[jax.ShapeDtypeStruct((N_CHIPS,) + g.shape[2:], g.dtype) for g in grads],
        scratch_shapes=[pltpu.SemaphoreType.DMA((k,)), pltpu.SemaphoreType.DMA((k,))],
    )(*grads)


def scatter_chips(parts, name):
    n = len(parts)

    def body(*refs):
        src, dst = refs[:n], refs[n:2 * n]
        send, recv, loc = refs[2 * n:]
        x, y, c = _coords()
        me = 2 * x + y
        locs = [pltpu.make_async_copy(src[t].at[me], dst[t].at[me], loc.at[t]) for t in range(n)]
        for cp in locs:
            cp.start()
        outs = []
        for t in range(n):
            for r in range(1, N_CHIPS):
                px, py = _chip_peer(x, y, r)
                s = t * (N_CHIPS - 1) + r - 1
                outs.append(pltpu.make_async_remote_copy(src[t].at[2 * px + py], dst[t].at[me], send.at[s], recv.at[s],
                                                         device_id=(px, py, c), device_id_type=_MESH))
        for cp in outs:
            cp.start()
        for t in range(n):
            for r in range(1, N_CHIPS):
                px, py = _chip_peer(x, y, r)
                s = t * (N_CHIPS - 1) + r - 1
                pltpu.make_async_remote_copy(src[t].at[me], dst[t].at[2 * px + py], send.at[s], recv.at[s],
                                             device_id=(px, py, c), device_id_type=_MESH).wait_recv()
        for cp in outs:
            cp.wait_send()
        for cp in locs:
            cp.wait()

    k = n * (N_CHIPS - 1)
    return pl.pallas_call(
        body, name=name, in_specs=[_ANY] * n, out_specs=[_ANY] * n,
        out_shape=[jax.ShapeDtypeStruct(p.shape, p.dtype) for p in parts],
        scratch_shapes=[pltpu.SemaphoreType.DMA((k,)), pltpu.SemaphoreType.DMA((k,)), pltpu.SemaphoreType.DMA((n,))],
    )(*parts)


def sibling_gather(halves, name):
    n = len(halves)

    def body(*refs):
        src, dst = refs[:n], refs[n:2 * n]
        send, recv, loc = refs[2 * n:]
        x, y, c = _coords()
        locs = [pltpu.make_async_copy(src[t], dst[t].at[c], loc.at[t]) for t in range(n)]
        for cp in locs:
            cp.start()
        outs = [pltpu.make_async_remote_copy(src[t], dst[t].at[c], send.at[t], recv.at[t],
                                             device_id=(x, y, 1 - c), device_id_type=_MESH) for t in range(n)]
        for cp in outs:
            cp.start()
        for t in range(n):
            pltpu.make_async_remote_copy(src[t], dst[t].at[1 - c], send.at[t], recv.at[t],
                                         device_id=(x, y, 1 - c), device_id_type=_MESH).wait_recv()
        for cp in outs:
            cp.wait_send()
        for cp in locs:
            cp.wait()

    return pl.pallas_call(
        body, name=name, in_specs=[_ANY] * n, out_specs=[_ANY] * n,
        out_shape=[jax.ShapeDtypeStruct((2,) + h.shape, h.dtype) for h in halves],
        scratch_shapes=[pltpu.SemaphoreType.DMA((n,)), pltpu.SemaphoreType.DMA((n,)), pltpu.SemaphoreType.DMA((n,))],
    )(*halves)


def gather_all(block, name):
    def body(src, dst, send, recv, loc):
        x, y, c = _coords()
        me = 4 * x + 2 * y + c
        mine = pltpu.make_async_copy(src, dst.at[me], loc)
        mine.start()
        outs = []
        for r in range(1, N_DEV):
            peer = (x ^ (r >> 2), y ^ ((r >> 1) & 1), c ^ (r & 1))
            outs.append(pltpu.make_async_remote_copy(src, dst.at[me], send.at[r - 1], recv.at[r - 1],
                                                     device_id=peer, device_id_type=_MESH))
        for cp in outs:
            cp.start()
        for r in range(1, N_DEV):
            px, py, pc = x ^ (r >> 2), y ^ ((r >> 1) & 1), c ^ (r & 1)
            pltpu.make_async_remote_copy(src, dst.at[4 * px + 2 * py + pc], send.at[r - 1], recv.at[r - 1],
                                         device_id=(px, py, pc), device_id_type=_MESH).wait_recv()
        for cp in outs:
            cp.wait_send()
        mine.wait()

    return pl.pallas_call(
        body, name=name, in_specs=[_ANY], out_specs=_ANY,
        out_shape=jax.ShapeDtypeStruct((N_DEV,) + block.shape, block.dtype),
        scratch_shapes=[pltpu.SemaphoreType.DMA((N_DEV - 1,)), pltpu.SemaphoreType.DMA((N_DEV - 1,)),
                        pltpu.SemaphoreType.DMA(())],
    )(block)


def _pad_lanes(v, n=LANE):
    return jnp.pad(v, ((0, 0), (0, n - v.shape[1])))


def _conv_w8(w):
    return jnp.pad(w, ((0, SUB - DN_CONV), (0, 0)))


def _row_layout(gc, tp):
    nc = tp // CHUNK
    g = gc[:, :, 0].reshape(HEADS, nc, 1, CHUNK)
    g = jnp.broadcast_to(g, (HEADS, nc, SUB, CHUNK))
    return jnp.pad(g, ((0, 0), (0, 0), (0, 0), (0, LANE - CHUNK))).reshape(HEADS, nc * SUB, LANE)


def _step(x, meta, W, target):
    seq = x.shape[0]
    tp = P0 + N_META + seq
    h0 = jnp.concatenate([jnp.zeros((P0, D_MODEL), f32), meta, x], axis=0)
    w_in = W["w_in"]
    w_big = jnp.concatenate([w_in[:, :AB_COL], w_in[:, AB_COL + 2 * HEADS:]], axis=1)
    w_ab = _pad_lanes(w_in[:, AB_COL:AB_COL + 2 * HEADS])
    cq8, ck8, cv8 = _conv_w8(W["conv_q"]), _conv_w8(W["conv_k"]), _conv_w8(W["conv_v"])
    al, dtb = _pad_lanes(W["dn_a_log"]), _pad_lanes(W["dn_dt_bias"])

    n1 = rms_fwd(h0, W["norm_mix_gain"], "rms1_fwd")
    proj = matmul(n1, w_big, "nn", "proj_fwd")
    pab = matmul(n1, w_ab, "nn", "pab_fwd")
    qn = conv_fwd(proj, cq8, C_DQ * 8, 8, True, "conv_q_fwd")
    kn = conv_fwd(proj, ck8, C_DK * 8, 8, True, "conv_k_fwd")
    va = conv_fwd(proj, cv8, C_DV * 8, 16, False, "conv_v_fwd")
    gc, bc = ab_fwd(pab, al, dtb)
    grow = _row_layout(gc, tp)
    o_dn, states = gdn_fwd(qn, kn, va, gc, bc, grow)
    on = dn_out_fwd(o_dn, proj, W["dn_out_norm_gain"])
    qs, ks, vs = sb_prep_fwd(proj, W["sb_q_norm_gain"], W["sb_k_norm_gain"])
    o_sb = sb_fwd(qs, ks, vs)
    o_sb16 = cast_bf16(o_sb, "o_sb_cast")
    ydn = matmul(on, W["w_branch_dn"], "nn", "ydn_fwd")
    ysb = matmul(o_sb16, W["w_branch_sb"], "nn", "ysb_fwd")
    merged = merge_fwd(proj, ydn, ysb)
    h1 = matmul(merged, W["w_out"], "nn", "wout_fwd", residual=h0)
    n2 = rms_fwd(h1, W["norm_ffn_gain"], "rms2_fwd")
    u = matmul(n2, W["w_ffn_in"], "nn", "ffn_in_fwd", tn_t=512)
    act = swiglu_fwd(u)
    y = matmul(act, W["w_ffn_out"], "nn", "ffn_out_fwd", residual=h1)
    dy, loss = loss_head(y, target)

    G = {}
    dy16 = cast_bf16(dy, "dy_cast")
    dact = matmul(dy16, W["w_ffn_out"], "nt", "ffn_out_dx", tn_t=1408)
    G["w_ffn_out"] = matmul(act, dy16, "tn", "ffn_out_dw", tm_t=1408)
    dgate, dup = swiglu_bwd(u, dact)
    du = jnp.concatenate([dgate, dup], axis=1)
    dn2 = matmul(du, W["w_ffn_in"], "nt", "ffn_in_dx", tk_t=512)
    G["w_ffn_in"] = matmul(n2, du, "tn", "ffn_in_dw", tn_t=512)
    dh1, G["norm_ffn_gain"] = rms_bwd(h1, W["norm_ffn_gain"], dn2, dy, "rms2_bwd")
    dh1_16 = cast_bf16(dh1, "dh1_cast")
    dmerged = matmul(dh1_16, W["w_out"], "nt", "wout_dx")
    G["w_out"] = matmul(merged, dh1_16, "tn", "wout_dw")
    dyd, dys, dgd, dgs = merge_bwd(proj, ydn, ysb, dmerged)
    don = matmul(dyd, W["w_branch_dn"], "nt", "ydn_dx")
    G["w_branch_dn"] = matmul(on, dyd, "tn", "ydn_dw")
    do_sb = matmul(dys, W["w_branch_sb"], "nt", "ysb_dx")
    G["w_branch_sb"] = matmul(o_sb16, dys, "tn", "ysb_dw")
    do_dn, dz, G["dn_out_norm_gain"] = dn_out_bwd(o_dn, proj, W["dn_out_norm_gain"], don)
    dqn, dkn, dva, dgc, dbc = gdn_bwd(qn, kn, va, gc, bc, grow, states, do_dn)
    dpab, dal, ddt = ab_bwd(pab, al, dtb, dgc, dbc)
    G["dn_a_log"], G["dn_dt_bias"] = dal[:, :HEADS], ddt[:, :HEADS]
    dyq, dcq = conv_bwd_act(proj, cq8, dqn, C_DQ * 8, 8, True, "conv_q_bwd")
    dyk, dck = conv_bwd_act(proj, ck8, dkn, C_DK * 8, 8, True, "conv_k_bwd")
    dyv, dcv = conv_bwd_act(proj, cv8, dva, C_DV * 8, 16, False, "conv_v_bwd")
    G["conv_q"], G["conv_k"], G["conv_v"] = dcq[:DN_CONV], dck[:DN_CONV], dcv[:DN_CONV]
    d_dq = conv_bwd_in(dyq, cq8, "conv_q_dx")
    d_dk = conv_bwd_in(dyk, ck8, "conv_k_dx")
    d_dv = conv_bwd_in(dyv, cv8, "conv_v_dx")
    dqs, dks, dvs = sb_bwd(qs, ks, vs, o_sb, do_sb)
    d_sq, d_sk, G["sb_q_norm_gain"], G["sb_k_norm_gain"] = sb_prep_bwd(
        proj, W["sb_q_norm_gain"], W["sb_k_norm_gain"], dqs, dks)
    d_sv = cast_bf16(dvs, "dvs_cast")
    dproj = jnp.concatenate([d_dq, d_dk, d_dv, dz, d_sq, d_sk, d_sv, dgd, dgs], axis=1)
    dn1 = matmul(dproj, w_big, "nt", "proj_dx", tk_t=1024)
    dn1 = matmul(dpab, w_ab, "nt", "pab_dx", residual=dn1)
    dw_big = matmul(n1, dproj, "tn", "proj_dw")
    dw_ab = matmul(n1, dpab, "tn", "pab_dw")
    G["w_in"] = jnp.concatenate([dw_big[:, :AB_COL], dw_ab[:, :2 * HEADS], dw_big[:, AB_COL:]], axis=1)
    dh0, G["norm_mix_gain"] = rms_bwd(h0, W["norm_mix_gain"], dn1, dh1, "rms1_bwd")
    G["meta_tokens"] = dh0[P0:P0 + N_META]
    return loss, dh0[P0 + N_META:], G


_BIG = ("w_in", "w_branch_dn", "w_branch_sb", "w_out", "w_ffn_in", "w_ffn_out")
_COL_SHARDED = ("w_in", "w_ffn_in", "meta_tokens", "conv_q", "conv_k", "conv_v")
_SMALL_REPL = ("norm_mix_gain", "norm_ffn_gain", "dn_a_log", "dn_dt_bias", "dn_out_norm_gain", "sb_q_norm_gain",
               "sb_k_norm_gain")
_SMALL_SHARD = ("meta_tokens", "conv_q", "conv_k", "conv_v")
_ORDER = ("meta_tokens", "norm_mix_gain", "w_in", "conv_q", "conv_k", "conv_v", "dn_a_log", "dn_dt_bias",
          "dn_out_norm_gain", "sb_q_norm_gain", "sb_k_norm_gain", "w_branch_dn", "w_branch_sb", "w_out",
          "norm_ffn_gain", "w_ffn_in", "w_ffn_out")


def _unshard(g4, name):
    if name in _COL_SHARDED:
        r, cs = g4.shape[1:]
        return jnp.transpose(g4, (1, 0, 2)).reshape(r, N_CHIPS * cs)
    return g4.reshape((-1,) + g4.shape[2:])


def _to_shards(full, name):
    if name in _COL_SHARDED:
        r, c = full.shape
        return jnp.transpose(full.reshape(r, N_CHIPS, c // N_CHIPS), (1, 0, 2))
    r, c = full.shape
    return full.reshape(N_CHIPS, r // N_CHIPS, c)


def _rows_1024(a):
    r, c = a.shape
    if c >= 1024:
        return a.reshape(r * (c // 1024), 1024)
    return jnp.pad(a, ((0, 0), (0, 1024 - c)))


def kernel(x, meta_tokens, norm_mix_gain, w_in, conv_q, conv_k, conv_v, dn_a_log, dn_dt_bias, dn_out_norm_gain, sb_q_norm_gain, sb_k_norm_gain, w_branch_dn, w_branch_sb, w_out, norm_ffn_gain, w_ffn_in, w_ffn_out, loss_target, m_meta_tokens, m_norm_mix_gain, m_w_in, m_conv_q, m_conv_k, m_conv_v, m_dn_a_log, m_dn_dt_bias, m_dn_out_norm_gain, m_sb_q_norm_gain, m_sb_k_norm_gain, m_w_branch_dn, m_w_branch_sb, m_w_out, m_norm_ffn_gain, m_w_ffn_in, m_w_ffn_out, v_meta_tokens, v_norm_mix_gain, v_w_in, v_conv_q, v_conv_k, v_conv_v, v_dn_a_log, v_dn_dt_bias, v_dn_out_norm_gain, v_sb_q_norm_gain, v_sb_k_norm_gain, v_w_branch_dn, v_w_branch_sb, v_w_out, v_norm_ffn_gain, v_w_ffn_in, v_w_ffn_out):
    Wl = dict(meta_tokens=meta_tokens, norm_mix_gain=norm_mix_gain, w_in=w_in[0], conv_q=conv_q[0], conv_k=conv_k[0],
              conv_v=conv_v[0], dn_a_log=dn_a_log, dn_dt_bias=dn_dt_bias, dn_out_norm_gain=dn_out_norm_gain,
              sb_q_norm_gain=sb_q_norm_gain, sb_k_norm_gain=sb_k_norm_gain, w_branch_dn=w_branch_dn[0],
              w_branch_sb=w_branch_sb[0], w_out=w_out[0], norm_ffn_gain=norm_ffn_gain, w_ffn_in=w_ffn_in[0],
              w_ffn_out=w_ffn_out[0])
    Ml = dict(meta_tokens=m_meta_tokens, norm_mix_gain=m_norm_mix_gain, w_in=m_w_in[0], conv_q=m_conv_q[0],
              conv_k=m_conv_k[0], conv_v=m_conv_v[0], dn_a_log=m_dn_a_log, dn_dt_bias=m_dn_dt_bias,
              dn_out_norm_gain=m_dn_out_norm_gain, sb_q_norm_gain=m_sb_q_norm_gain, sb_k_norm_gain=m_sb_k_norm_gain,
              w_branch_dn=m_w_branch_dn[0], w_branch_sb=m_w_branch_sb[0], w_out=m_w_out[0],
              norm_ffn_gain=m_norm_ffn_gain, w_ffn_in=m_w_ffn_in[0], w_ffn_out=m_w_ffn_out[0])
    Vl = dict(meta_tokens=v_meta_tokens, norm_mix_gain=v_norm_mix_gain, w_in=v_w_in[0], conv_q=v_conv_q[0],
              conv_k=v_conv_k[0], conv_v=v_conv_v[0], dn_a_log=v_dn_a_log, dn_dt_bias=v_dn_dt_bias,
              dn_out_norm_gain=v_dn_out_norm_gain, sb_q_norm_gain=v_sb_q_norm_gain, sb_k_norm_gain=v_sb_k_norm_gain,
              w_branch_dn=v_w_branch_dn[0], w_branch_sb=v_w_branch_sb[0], w_out=v_w_out[0],
              norm_ffn_gain=v_norm_ffn_gain, w_ffn_in=v_w_ffn_in[0], w_ffn_out=v_w_ffn_out[0])
    lead = {n: (1,) if (n in _BIG or n in ("conv_q", "conv_k", "conv_v")) else () for n in _ORDER}

    names = list(_BIG) + list(_SMALL_SHARD)
    shards = [Wl[n].astype(bf16) if n in _BIG else Wl[n] for n in names]
    gathered = gather_chips(shards, "gather_weights")
    W = dict(Wl)
    for n, g4 in zip(names, gathered):
        W[n] = _unshard(g4, n)

    loss, grad_x, G = _step(x[0], W["meta_tokens"], W, loss_target[0])

    c = lax.axis_index("c")
    g4 = [_to_shards(G[n], n) for n in _BIG]
    g42 = [g.reshape(N_CHIPS, 2, g.shape[1] // 2, g.shape[2]) for g in g4]
    from_sib = sibling_swap_halves(g42, "grad_sibling_swap")
    mine = [lax.dynamic_index_in_dim(g, c, axis=1, keepdims=False) for g in g42]
    chip_part = [add2(a, b, "grad_pair_add_%d" % t) for t, (a, b) in enumerate(zip(mine, from_sib))]
    slots = scatter_chips(chip_part, "grad_chip_exchange")
    halves = [sum_slots(s, "grad_chip_sum_%d" % t) for t, s in enumerate(slots)]
    both = sibling_gather(halves, "grad_sibling_gather")
    Gs = {n: b.reshape(b.shape[0] * b.shape[1], b.shape[2]) for n, b in zip(_BIG, both)}

    small_names = list(_SMALL_REPL) + list(_SMALL_SHARD)
    pieces = [_rows_1024(G[n]) for n in small_names] + [_rows_1024(loss)]
    counts = [p.shape[0] for p in pieces]
    pack = jnp.concatenate(pieces, axis=0)
    pad_rows = (-pack.shape[0]) % SUB
    pack = jnp.pad(pack, ((0, pad_rows), (0, 0)))
    total = sum_slots(gather_all(pack, "small_gather"), "small_sum")
    chip = 2 * lax.axis_index("x") + lax.axis_index("y")
    row = 0
    for n, cnt in zip(small_names, counts[:-1]):
        blk = total[row:row + cnt]
        row += cnt
        full_shape = G[n].shape
        if full_shape[1] >= 1024:
            blk = blk.reshape(full_shape)
        else:
            blk = blk[:, :full_shape[1]]
        if n in _SMALL_SHARD:
            cs = full_shape[1] // N_CHIPS
            blk = lax.dynamic_slice_in_dim(blk, chip * cs, cs, axis=1)
        Gs[n] = blk
    loss_out = total[row, 0]

    grads, deltas, new_m, new_v = [], [], [], []
    for n in _ORDER:
        d, m2, v2 = adamw(Wl[n], Gs[n], Ml[n], Vl[n], "adamw_" + n)
        shape = lead[n] + Wl[n].shape
        grads.append(Gs[n].reshape(shape))
        deltas.append(d.reshape(shape))
        new_m.append(m2.reshape(shape))
        new_v.append(v2.reshape(shape))
    return (loss_out, grad_x[None], *grads, *deltas, *new_m, *new_v)
```

```python
import jax
import jax.numpy as jnp
from jax import lax
from jax.experimental import pallas as pl
from jax.experimental.pallas import tpu as pltpu

f32 = jnp.float32
bf16 = jnp.bfloat16

D_MODEL = 1024
N_META = 16
CHUNK = 64
HEADS = 8
DN_DK = 128
DN_DV = 256
DN_CONV = 4
DN_QK = HEADS * DN_DK
DN_V = HEADS * DN_DV
SB_DH = 128
SB_W = HEADS * SB_DH
SB_BLOCK = 128
SB_GROUP = 4
SB_HEADS_PER_STEP = 2
GDN_HEADS_PER_STEP = 8
CONV_W = 512
D_FF = 2816
RMS_EPS = 1e-6
L2_EPS = 1e-6
ADAM_LR = 0.001
ADAM_B1 = 0.9
ADAM_B2 = 0.999
ADAM_EPS = 1e-08
ADAM_WD = 0.01
ADAM_STEP = 10

P0 = 112
LANE = 128
SUB = 8
VMEM_LIMIT = 48 * 1024 * 1024
N_CHIPS = 4
N_DEV = 8

C_DQ, C_DK, C_DV, C_DZ, C_SQ, C_SK, C_SV, C_GDN, C_GSB = 0, 1, 2, 4, 6, 7, 8, 9, 10
PROJ_BIG = 11 * 1024
AB_COL = 2 * DN_QK + 2 * DN_V


def _params(n_axes):
    return pltpu.CompilerParams(dimension_semantics=("arbitrary",) * n_axes, vmem_limit_bytes=VMEM_LIMIT)


def _tile(n, target, q=LANE):
    best = None
    for t in range(q, min(n, target) + 1, q):
        if n % t == 0:
            best = t
    return best if best is not None else n


def _dot(a, b):
    return jnp.dot(a.astype(bf16), b.astype(bf16), preferred_element_type=f32)


def _dot_nt(a, b):
    return lax.dot_general(a.astype(bf16), b.astype(bf16), (((1,), (1,)), ((), ())), preferred_element_type=f32)


def _dot_tn(a, b):
    return lax.dot_general(a.astype(bf16), b.astype(bf16), (((0,), (0,)), ((), ())), preferred_element_type=f32)


_HI = lax.Precision.HIGHEST


def _hdot(a, b):
    return jnp.dot(a, b, precision=_HI, preferred_element_type=f32)


def _hdot_nt(a, b):
    return lax.dot_general(a, b, (((1,), (1,)), ((), ())), precision=_HI, preferred_element_type=f32)


def _hdot_tn(a, b):
    return lax.dot_general(a, b, (((0,), (0,)), ((), ())), precision=_HI, preferred_element_type=f32)


def _sigmoid(x):
    e = jnp.exp(-jnp.abs(x))
    r = 1.0 / (1.0 + e)
    return jnp.where(x >= 0, r, e * r)


def _log1p_small(e):
    return jnp.where(e < 1e-3, e * (1.0 - e * (0.5 - e * (1.0 / 3.0))), jnp.log(1.0 + e))


def _rowsum(x):
    return jnp.sum(x, axis=1, keepdims=True)


def _allsum(x):
    return jnp.sum(jnp.sum(x, axis=1, keepdims=True), axis=0, keepdims=True)


def matmul(a, b, mode, name, residual=None, out_dtype=f32, tm_t=1408, tn_t=1024, tk_t=1408):
    if mode == "nn":
        (M, K), (K2, N) = a.shape, b.shape
    elif mode == "nt":
        (M, K), (N, K2) = a.shape, b.shape
    else:
        (K, M), (K2, N) = a.shape, b.shape
    assert K == K2, (a.shape, b.shape, mode)
    tm, tn, tk = _tile(M, tm_t), _tile(N, tn_t), _tile(K, tk_t)
    nk = K // tk
    if mode == "nn":
        a_spec = pl.BlockSpec((tm, tk), lambda i, j, k: (i, k))
        b_spec = pl.BlockSpec((tk, tn), lambda i, j, k: (k, j))
        dims = (((1,), (0,)), ((), ()))
    elif mode == "nt":
        a_spec = pl.BlockSpec((tm, tk), lambda i, j, k: (i, k))
        b_spec = pl.BlockSpec((tn, tk), lambda i, j, k: (j, k))
        dims = (((1,), (1,)), ((), ()))
    else:
        a_spec = pl.BlockSpec((tk, tm), lambda i, j, k: (k, i))
        b_spec = pl.BlockSpec((tk, tn), lambda i, j, k: (k, j))
        dims = (((0,), (0,)), ((), ()))
    o_spec = pl.BlockSpec((tm, tn), lambda i, j, k: (i, j))
    has_res = residual is not None

    def body(*refs):
        if has_res:
            a_ref, b_ref, r_ref, o_ref, acc_ref = refs
        else:
            a_ref, b_ref, o_ref, acc_ref = refs
        k = pl.program_id(2)

        @pl.when(k == 0)
        def _():
            acc_ref[...] = jnp.zeros_like(acc_ref)

        acc_ref[...] += lax.dot_general(a_ref[...].astype(bf16), b_ref[...].astype(bf16), dims,
                                        preferred_element_type=f32)

        @pl.when(k == nk - 1)
        def _():
            r = acc_ref[...]
            if has_res:
                r = r + r_ref[...]
            o_ref[...] = r.astype(out_dtype)

    ins = [a, b] + ([residual] if has_res else [])
    specs = [a_spec, b_spec] + ([o_spec] if has_res else [])
    return pl.pallas_call(
        body, name=name, grid=(M // tm, N // tn, nk), in_specs=specs, out_specs=o_spec,
        out_shape=jax.ShapeDtypeStruct((M, N), out_dtype),
        scratch_shapes=[pltpu.VMEM((tm, tn), f32)], compiler_params=_params(3),
    )(*ins)


def _row_tile(tp):
    return _tile(tp, 512)


def rms_fwd(h, gain, name):
    tp, d = h.shape
    rt = _row_tile(tp)

    def body(h_ref, g_ref, o_ref):
        x = h_ref[...]
        r = lax.rsqrt(jnp.mean(x * x, axis=-1, keepdims=True) + RMS_EPS)
        o_ref[...] = (x * r * g_ref[...]).astype(bf16)

    return pl.pallas_call(
        body, name=name, grid=(tp // rt,),
        in_specs=[pl.BlockSpec((rt, d), lambda i: (i, 0)), pl.BlockSpec((1, d), lambda i: (0, 0))],
        out_specs=pl.BlockSpec((rt, d), lambda i: (i, 0)),
        out_shape=jax.ShapeDtypeStruct((tp, d), bf16), compiler_params=_params(1),
    )(h, gain)


def rms_bwd(h, gain, dn, dres, name):
    tp, d = h.shape
    rt = _row_tile(tp)

    def body(h_ref, g_ref, dn_ref, dr_ref, dh_ref, dg_ref):
        i = pl.program_id(0)
        x = h_ref[...]
        r = lax.rsqrt(jnp.mean(x * x, axis=-1, keepdims=True) + RMS_EPS)
        xh = x * r
        dn_ = dn_ref[...]
        dxh = dn_ * g_ref[...]
        dh_ref[...] = r * (dxh - xh * jnp.mean(dxh * xh, axis=-1, keepdims=True)) + dr_ref[...]
        part = jnp.sum(dn_ * xh, axis=0, keepdims=True)

        @pl.when(i == 0)
        def _():
            dg_ref[...] = part

        @pl.when(i > 0)
        def _():
            dg_ref[...] += part

    row = pl.BlockSpec((rt, d), lambda i: (i, 0))
    vec = pl.BlockSpec((1, d), lambda i: (0, 0))
    return pl.pallas_call(
        body, name=name, grid=(tp // rt,), in_specs=[row, vec, row, row], out_specs=[row, vec],
        out_shape=[jax.ShapeDtypeStruct((tp, d), f32), jax.ShapeDtypeStruct((1, d), f32)],
        compiler_params=_params(1),
    )(h, gain, dn, dres)


def loss_head(y, target):
    tp, d = y.shape
    rt = P0 + N_META
    assert rt == SB_BLOCK and tp % rt == 0 and target.shape == (tp - rt, d)

    def body(y_ref, t_ref, dy_ref, l_ref):
        i = pl.program_id(0)

        @pl.when(i == 0)
        def _():
            dy_ref[...] = jnp.zeros_like(dy_ref)
            l_ref[...] = jnp.zeros_like(l_ref)

        @pl.when(i > 0)
        def _():
            err = y_ref[...] - t_ref[...]
            dy_ref[...] = err * (1.0 / d)
            l_ref[...] += jnp.broadcast_to(_allsum(err * err) * (0.5 / d), l_ref.shape)

    return pl.pallas_call(
        body, name="loss_head", grid=(tp // rt,),
        in_specs=[pl.BlockSpec((rt, d), lambda i: (i, 0)), pl.BlockSpec((rt, d), lambda i: (jnp.maximum(i - 1, 0), 0))],
        out_specs=[pl.BlockSpec((rt, d), lambda i: (i, 0)), pl.BlockSpec((1, LANE), lambda i: (0, 0))],
        out_shape=[jax.ShapeDtypeStruct((tp, d), f32), jax.ShapeDtypeStruct((1, LANE), f32)],
        compiler_params=_params(1),
    )(y, target)


def swiglu_fwd(u):
    tp = u.shape[0]
    rt, cb = _row_tile(tp), D_FF // 2
    nb = D_FF // cb

    def body(g_ref, u_ref, o_ref):
        g = g_ref[...]
        o_ref[...] = (g * _sigmoid(g) * u_ref[...]).astype(bf16)

    return pl.pallas_call(
        body, name="swiglu_fwd", grid=(tp // rt, nb),
        in_specs=[pl.BlockSpec((rt, cb), lambda i, j: (i, j)), pl.BlockSpec((rt, cb), lambda i, j: (i, j + nb))],
        out_specs=pl.BlockSpec((rt, cb), lambda i, j: (i, j)),
        out_shape=jax.ShapeDtypeStruct((tp, D_FF), bf16), compiler_params=_params(2),
    )(u, u)


def swiglu_bwd(u, dact):
    tp = u.shape[0]
    rt, cb = _row_tile(tp), D_FF // 2
    nb = D_FF // cb

    def body(g_ref, u_ref, da_ref, dg_ref, du_ref):
        g = g_ref[...]
        s = _sigmoid(g)
        da = da_ref[...]
        dg_ref[...] = (da * u_ref[...] * s * (1.0 + g * (1.0 - s))).astype(bf16)
        du_ref[...] = (da * g * s).astype(bf16)

    lo = pl.BlockSpec((rt, cb), lambda i, j: (i, j))
    hi = pl.BlockSpec((rt, cb), lambda i, j: (i, j + nb))
    dgate, dup = pl.pallas_call(
        body, name="swiglu_bwd", grid=(tp // rt, nb), in_specs=[lo, hi, lo], out_specs=[lo, lo],
        out_shape=[jax.ShapeDtypeStruct((tp, D_FF), bf16)] * 2, compiler_params=_params(2),
    )(u, u, dact)
    return dgate, dup


def merge_fwd(proj, ydn, ysb):
    tp = proj.shape[0]
    rt, d = _row_tile(tp), D_MODEL

    def body(gd_ref, gs_ref, yd_ref, ys_ref, o_ref):
        o_ref[...] = (_sigmoid(gd_ref[...]) * yd_ref[...] + _sigmoid(gs_ref[...]) * ys_ref[...]).astype(bf16)

    row = pl.BlockSpec((rt, d), lambda i: (i, 0))
    return pl.pallas_call(
        body, name="merge_fwd", grid=(tp // rt,),
        in_specs=[pl.BlockSpec((rt, d), lambda i: (i, C_GDN)), pl.BlockSpec((rt, d), lambda i: (i, C_GSB)), row, row],
        out_specs=row, out_shape=jax.ShapeDtypeStruct((tp, d), bf16), compiler_params=_params(1),
    )(proj, proj, ydn, ysb)


def merge_bwd(proj, ydn, ysb, dm):
    tp = proj.shape[0]
    rt, d = _row_tile(tp), D_MODEL

    def body(gd_ref, gs_ref, yd_ref, ys_ref, dm_ref, dyd_ref, dys_ref, dgd_ref, dgs_ref):
        dm_ = dm_ref[...]
        sd = _sigmoid(gd_ref[...])
        ss = _sigmoid(gs_ref[...])
        dyd_ref[...] = (dm_ * sd).astype(bf16)
        dys_ref[...] = (dm_ * ss).astype(bf16)
        dgd_ref[...] = (dm_ * yd_ref[...] * sd * (1.0 - sd)).astype(bf16)
        dgs_ref[...] = (dm_ * ys_ref[...] * ss * (1.0 - ss)).astype(bf16)

    row = pl.BlockSpec((rt, d), lambda i: (i, 0))
    return pl.pallas_call(
        body, name="merge_bwd", grid=(tp // rt,),
        in_specs=[pl.BlockSpec((rt, d), lambda i: (i, C_GDN)), pl.BlockSpec((rt, d), lambda i: (i, C_GSB)), row, row, row],
        out_specs=[row] * 4, out_shape=[jax.ShapeDtypeStruct((tp, d), bf16)] * 4, compiler_params=_params(1),
    )(proj, proj, ydn, ysb, dm)


def dn_out_fwd(o, proj, gain):
    tp = o.shape[0]
    rt, cb, wide = _row_tile(tp), DN_DV, 1024
    zb = C_DZ * 1024 // wide

    def body(o_ref, z_ref, g_ref, y_ref):
        for s in range(wide // cb):
            sl = slice(s * cb, (s + 1) * cb)
            x = o_ref[:, sl]
            r = lax.rsqrt(jnp.mean(x * x, axis=-1, keepdims=True) + RMS_EPS)
            z = z_ref[:, sl]
            y_ref[:, sl] = (x * r * g_ref[...] * (z * _sigmoid(z))).astype(bf16)

    blk = pl.BlockSpec((rt, wide), lambda i, j: (i, j))
    return pl.pallas_call(
        body, name="dn_out_fwd", grid=(tp // rt, DN_V // wide),
        in_specs=[blk, pl.BlockSpec((rt, wide), lambda i, j: (i, j + zb)), pl.BlockSpec((1, cb), lambda i, j: (0, 0))],
        out_specs=blk, out_shape=jax.ShapeDtypeStruct((tp, DN_V), bf16), compiler_params=_params(2),
    )(o, proj, gain)


def dn_out_bwd(o, proj, gain, dy):
    tp = o.shape[0]
    rt, cb, wide = _row_tile(tp), DN_DV, 1024
    zb = C_DZ * 1024 // wide

    def body(o_ref, z_ref, g_ref, dy_ref, do_ref, dz_ref, dg_ref):
        i, j = pl.program_id(0), pl.program_id(1)
        g = g_ref[...]
        part = jnp.zeros((1, cb), f32)
        for hh in range(wide // cb):
            sl = slice(hh * cb, (hh + 1) * cb)
            x = o_ref[:, sl]
            r = lax.rsqrt(jnp.mean(x * x, axis=-1, keepdims=True) + RMS_EPS)
            xh = x * r
            z = z_ref[:, sl]
            s = _sigmoid(z)
            dy_ = dy_ref[:, sl]
            drn = dy_ * (z * s)
            dz_ref[:, sl] = (dy_ * xh * g * s * (1.0 + z * (1.0 - s))).astype(bf16)
            dxh = drn * g
            do_ref[:, sl] = r * (dxh - xh * jnp.mean(dxh * xh, axis=-1, keepdims=True))
            part = part + jnp.sum(drn * xh, axis=0, keepdims=True)
        first = jnp.logical_and(i == 0, j == 0)

        @pl.when(first)
        def _():
            dg_ref[...] = part

        @pl.when(jnp.logical_not(first))
        def _():
            dg_ref[...] += part

    blk = pl.BlockSpec((rt, wide), lambda i, j: (i, j))
    vec = pl.BlockSpec((1, cb), lambda i, j: (0, 0))
    return pl.pallas_call(
        body, name="dn_out_bwd", grid=(tp // rt, DN_V // wide),
        in_specs=[blk, pl.BlockSpec((rt, wide), lambda i, j: (i, j + zb)), vec, blk],
        out_specs=[blk, blk, vec],
        out_shape=[jax.ShapeDtypeStruct((tp, DN_V), f32), jax.ShapeDtypeStruct((tp, DN_V), bf16),
                   jax.ShapeDtypeStruct((1, cb), f32)],
        compiler_params=_params(2),
    )(o, proj, gain, dy)


def sb_prep_fwd(proj, gq, gk):
    tp = proj.shape[0]
    rt, cb = _row_tile(tp), SB_DH

    def body(q_ref, k_ref, v_ref, gq_ref, gk_ref, qo_ref, ko_ref, vo_ref):
        for x_ref, g_ref, o_ref in ((q_ref, gq_ref, qo_ref), (k_ref, gk_ref, ko_ref)):
            for h in range(HEADS):
                sl = slice(h * cb, (h + 1) * cb)
                x = x_ref[:, sl]
                r = lax.rsqrt(jnp.mean(x * x, axis=-1, keepdims=True) + RMS_EPS)
                o_ref[:, sl] = (x * r * g_ref[...]).astype(bf16)
        vo_ref[...] = v_ref[...].astype(bf16)

    blk = pl.BlockSpec((rt, SB_W), lambda i: (i, 0))
    vec = pl.BlockSpec((1, cb), lambda i: (0, 0))
    return pl.pallas_call(
        body, name="sb_prep_fwd", grid=(tp // rt,),
        in_specs=[pl.BlockSpec((rt, SB_W), lambda i: (i, C_SQ)), pl.BlockSpec((rt, SB_W), lambda i: (i, C_SK)),
                  pl.BlockSpec((rt, SB_W), lambda i: (i, C_SV)), vec, vec],
        out_specs=[blk] * 3, out_shape=[jax.ShapeDtypeStruct((tp, SB_W), bf16)] * 3, compiler_params=_params(1),
    )(proj, proj, proj, gq, gk)


def sb_prep_bwd(proj, gq, gk, dqs, dks):
    tp = proj.shape[0]
    rt, cb = _row_tile(tp), SB_DH

    def body(q_ref, k_ref, gq_ref, gk_ref, dq_ref, dk_ref, dqo_ref, dko_ref, dgq_ref, dgk_ref):
        first = pl.program_id(0) == 0
        for x_ref, g_ref, dn_ref, dx_ref, dg_ref in ((q_ref, gq_ref, dq_ref, dqo_ref, dgq_ref),
                                                     (k_ref, gk_ref, dk_ref, dko_ref, dgk_ref)):
            part = jnp.zeros((1, cb), f32)
            for h in range(HEADS):
                sl = slice(h * cb, (h + 1) * cb)
                x = x_ref[:, sl]
                r = lax.rsqrt(jnp.mean(x * x, axis=-1, keepdims=True) + RMS_EPS)
                xh = x * r
                dn_ = dn_ref[:, sl]
                dxh = dn_ * g_ref[...]
                dx_ref[:, sl] = (r * (dxh - xh * jnp.mean(dxh * xh, axis=-1, keepdims=True))).astype(bf16)
                part = part + jnp.sum(dn_ * xh, axis=0, keepdims=True)

            @pl.when(first)
            def _(dg_ref=dg_ref, part=part):
                dg_ref[...] = part

            @pl.when(jnp.logical_not(first))
            def _(dg_ref=dg_ref, part=part):
                dg_ref[...] += part

    blk = pl.BlockSpec((rt, SB_W), lambda i: (i, 0))
    vec = pl.BlockSpec((1, cb), lambda i: (0, 0))
    return pl.pallas_call(
        body, name="sb_prep_bwd", grid=(tp // rt,),
        in_specs=[pl.BlockSpec((rt, SB_W), lambda i: (i, C_SQ)), pl.BlockSpec((rt, SB_W), lambda i: (i, C_SK)),
                  vec, vec, blk, blk],
        out_specs=[blk, blk, vec, vec],
        out_shape=[jax.ShapeDtypeStruct((tp, SB_W), bf16)] * 2 + [jax.ShapeDtypeStruct((1, cb), f32)] * 2,
        compiler_params=_params(1),
    )(proj, proj, gq, gk, dqs, dks)


def cast_bf16(x, name):
    tp, d = x.shape
    rt = _row_tile(tp)
    blk = pl.BlockSpec((rt, d), lambda i: (i, 0))

    def body(x_ref, o_ref):
        o_ref[...] = x_ref[...].astype(bf16)

    return pl.pallas_call(body, name=name, grid=(tp // rt,), in_specs=[blk], out_specs=blk,
                          out_shape=jax.ShapeDtypeStruct((tp, d), bf16), compiler_params=_params(1))(x)


def _conv_taps(ext, rt):
    taps = []
    for k in range(DN_CONV):
        s = DN_CONV - 1 - k
        taps.append((pltpu.roll(ext, s, axis=0) if s else ext)[SUB:SUB + rt])
    return taps


def _conv_act(taps, w, l2):
    y = taps[0] * w[0:1]
    for k in range(1, DN_CONV):
        y = y + taps[k] * w[k:k + 1]
    s = _sigmoid(y)
    a = y * s
    if l2:
        n = lax.rsqrt(jnp.sum(a * a, axis=-1, keepdims=True) + L2_EPS)
        return y, s, a, n
    return y, s, a, None


def conv_fwd(proj, w8, col_blk, ncb, l2, name):
    tp = proj.shape[0]
    rt = _row_tile(tp)
    hb = rt // SUB
    cw = CONV_W
    cb0 = col_blk * LANE // cw

    def body(x_ref, h_ref, w_ref, o_ref):
        i = pl.program_id(1)
        first = (i > 0).astype(f32)
        for s in range(cw // LANE):
            sl = slice(s * LANE, (s + 1) * LANE)
            ext = jnp.concatenate([h_ref[:, sl] * first, x_ref[:, sl]], axis=0)
            _, _, a, n = _conv_act(_conv_taps(ext, rt), w_ref[:, sl], l2)
            o_ref[:, sl] = a * n if l2 else a

    return pl.pallas_call(
        body, name=name, grid=(ncb * LANE // cw, tp // rt),
        in_specs=[pl.BlockSpec((rt, cw), lambda j, i: (i, j + cb0)),
                  pl.BlockSpec((SUB, cw), lambda j, i: (jnp.maximum(i * hb - 1, 0), j + cb0)),
                  pl.BlockSpec((SUB, cw), lambda j, i: (0, j))],
        out_specs=pl.BlockSpec((rt, cw), lambda j, i: (i, j)),
        out_shape=jax.ShapeDtypeStruct((tp, ncb * LANE), f32), compiler_params=_params(2),
    )(proj, proj, w8)


def conv_bwd_act(proj, w8, dout, col_blk, ncb, l2, name):
    tp = proj.shape[0]
    rt = _row_tile(tp)
    hb = rt // SUB
    cw = CONV_W
    cb0 = col_blk * LANE // cw

    def body(x_ref, h_ref, w_ref, d_ref, dy_ref, dw_ref):
        i = pl.program_id(1)
        first = (i > 0).astype(f32)
        rows = lax.broadcasted_iota(jnp.int32, (SUB, LANE), 0)
        for s in range(cw // LANE):
            sl = slice(s * LANE, (s + 1) * LANE)
            ext = jnp.concatenate([h_ref[:, sl] * first, x_ref[:, sl]], axis=0)
            taps = _conv_taps(ext, rt)
            y, sg, a, n = _conv_act(taps, w_ref[:, sl], l2)
            da = d_ref[:, sl]
            if l2:
                out = a * n
                da = n * (da - out * jnp.sum(da * out, axis=-1, keepdims=True))
            dy = da * sg * (1.0 + y * (1.0 - sg))
            dy_ref[:, sl] = dy
            part = jnp.zeros((SUB, LANE), f32)
            for k in range(DN_CONV):
                part = part + jnp.where(rows == k, jnp.sum(taps[k] * dy, axis=0, keepdims=True), 0.0)

            @pl.when(i == 0)
            def _(sl=sl, part=part):
                dw_ref[:, sl] = part

            @pl.when(i > 0)
            def _(sl=sl, part=part):
                dw_ref[:, sl] += part

    return pl.pallas_call(
        body, name=name, grid=(ncb * LANE // cw, tp // rt),
        in_specs=[pl.BlockSpec((rt, cw), lambda j, i: (i, j + cb0)),
                  pl.BlockSpec((SUB, cw), lambda j, i: (jnp.maximum(i * hb - 1, 0), j + cb0)),
                  pl.BlockSpec((SUB, cw), lambda j, i: (0, j)),
                  pl.BlockSpec((rt, cw), lambda j, i: (i, j))],
        out_specs=[pl.BlockSpec((rt, cw), lambda j, i: (i, j)), pl.BlockSpec((SUB, cw), lambda j, i: (0, j))],
        out_shape=[jax.ShapeDtypeStruct((tp, ncb * LANE), f32), jax.ShapeDtypeStruct((SUB, ncb * LANE), f32)],
        compiler_params=_params(2),
    )(proj, proj, w8, dout)


def conv_bwd_in(dy, w8, name):
    tp, cols = dy.shape
    rt = _row_tile(tp)
    hb = rt // SUB
    nr = tp // rt
    last8 = tp // SUB - 1
    cw = CONV_W

    def body(d_ref, h_ref, w_ref, o_ref):
        i = pl.program_id(1)
        last = (i < nr - 1).astype(f32)
        for c0 in range(cw // LANE):
            sl = slice(c0 * LANE, (c0 + 1) * LANE)
            ext = jnp.concatenate([d_ref[:, sl], h_ref[:, sl] * last], axis=0)
            w = w_ref[:, sl]
            acc = None
            for k in range(DN_CONV):
                s = DN_CONV - 1 - k
                sh = (pltpu.roll(ext, rt + SUB - s, axis=0) if s else ext)[0:rt]
                term = sh * w[k:k + 1]
                acc = term if acc is None else acc + term
            o_ref[:, sl] = acc.astype(bf16)

    return pl.pallas_call(
        body, name=name, grid=(cols // cw, nr),
        in_specs=[pl.BlockSpec((rt, cw), lambda j, i: (i, j)),
                  pl.BlockSpec((SUB, cw), lambda j, i: (jnp.minimum((i + 1) * hb, last8), j)),
                  pl.BlockSpec((SUB, cw), lambda j, i: (0, j))],
        out_specs=pl.BlockSpec((rt, cw), lambda j, i: (i, j)),
        out_shape=jax.ShapeDtypeStruct((tp, cols), bf16), compiler_params=_params(2),
    )(dy, dy, w8)


def _ab_common(p, al, dtb, r0):
    rows = r0 + lax.broadcasted_iota(jnp.int32, p.shape, 0)
    mask = (rows >= P0).astype(f32)
    xx = p + dtb
    sp = jnp.maximum(xx, 0.0) + _log1p_small(jnp.exp(-jnp.abs(xx)))
    ea = jnp.exp(al)
    g = -ea * sp * mask
    beta = _sigmoid(p) * mask
    return g, beta, _sigmoid(xx), ea, mask


def ab_fwd(pab, al, dtb):
    tp = pab.shape[0]
    rt = _row_tile(tp)

    def body(p_ref, al_ref, dt_ref, g_ref, b_ref):
        i = pl.program_id(0)
        g, beta, _, _, _ = _ab_common(p_ref[...], al_ref[...], dt_ref[...], i * rt)
        for h in range(HEADS):
            g_ref[h] = jnp.broadcast_to(g[:, h:h + 1], (rt, LANE))
            b_ref[h] = jnp.broadcast_to(beta[:, HEADS + h:HEADS + h + 1], (rt, LANE))

    vec = pl.BlockSpec((1, LANE), lambda i: (0, 0))
    out = pl.BlockSpec((HEADS, rt, LANE), lambda i: (0, i, 0))
    return pl.pallas_call(
        body, name="ab_fwd", grid=(tp // rt,), in_specs=[pl.BlockSpec((rt, LANE), lambda i: (i, 0)), vec, vec],
        out_specs=[out, out], out_shape=[jax.ShapeDtypeStruct((HEADS, tp, LANE), f32)] * 2, compiler_params=_params(1),
    )(pab, al, dtb)


def ab_bwd(pab, al, dtb, dg, db):
    tp = pab.shape[0]
    rt = _row_tile(tp)

    def body(p_ref, al_ref, dt_ref, dg_ref, db_ref, dp_ref, dal_ref, ddt_ref):
        i = pl.program_id(0)
        g, beta, sx, ea, mask = _ab_common(p_ref[...], al_ref[...], dt_ref[...], i * rt)
        lanes = lax.broadcasted_iota(jnp.int32, (rt, LANE), 1)
        dgl = jnp.zeros((rt, LANE), f32)
        dbl = jnp.zeros((rt, LANE), f32)
        for h in range(HEADS):
            dgl = dgl + jnp.where(lanes == h, dg_ref[h], 0.0)
            dbl = dbl + jnp.where(lanes == HEADS + h, db_ref[h], 0.0)
        dxx = dgl * (-ea) * sx * mask
        dp_ref[...] = (dxx + dbl * beta * (1.0 - beta)).astype(bf16)
        pal = jnp.sum(dgl * g, axis=0, keepdims=True)
        pdt = jnp.sum(dxx, axis=0, keepdims=True)

        @pl.when(i == 0)
        def _():
            dal_ref[...] = pal
            ddt_ref[...] = pdt

        @pl.when(i > 0)
        def _():
            dal_ref[...] += pal
            ddt_ref[...] += pdt

    vec = pl.BlockSpec((1, LANE), lambda i: (0, 0))
    row = pl.BlockSpec((rt, LANE), lambda i: (i, 0))
    big = pl.BlockSpec((HEADS, rt, LANE), lambda i: (0, i, 0))
    return pl.pallas_call(
        body, name="ab_bwd", grid=(tp // rt,), in_specs=[row, vec, vec, big, big], out_specs=[row, vec, vec],
        out_shape=[jax.ShapeDtypeStruct((tp, LANE), bf16), jax.ShapeDtypeStruct((1, LANE), f32),
                   jax.ShapeDtypeStruct((1, LANE), f32)],
        compiler_params=_params(1),
    )(pab, al, dtb, dg, db)


class _Chunk:
    pass


def _gdn_chunk(q, k, v, gcol, bcol, grow8):
    C = CHUNK
    R = range(len(q))
    X = _Chunk()
    ri = lax.broadcasted_iota(jnp.int32, (C, C), 0)
    ci = lax.broadcasted_iota(jnp.int32, (C, C), 1)
    r2 = lax.broadcasted_iota(jnp.int32, (LANE, LANE), 0)
    c2 = lax.broadcasted_iota(jnp.int32, (LANE, LANE), 1)
    lower = (ri >= ci).astype(f32)
    upper2 = (r2 <= c2).astype(f32)
    eye = (ri == ci).astype(f32)
    gam = [_hdot(lower, gcol[h]) for h in R]
    gam_row = [_hdot(grow8[h], upper2)[0:1, 0:C] for h in R]
    X.ri, X.ci = ri, ci
    X.Dm = [jnp.where(ri >= ci, jnp.exp(jnp.minimum(gam[h][:, 0:C] - gam_row[h], 0.0)), 0.0) for h in R]
    X.eg = [jnp.exp(gam[h]) for h in R]
    gl = [gam[h][C - 1:C, :] for h in R]
    X.egl = [jnp.exp(gl[h]) for h in R]
    X.kdec = [jnp.exp(gl[h] - gam[h]) for h in R]
    X.qs = [q[h] * (DN_DK ** -0.5) for h in R]
    X.kb = [k[h] * bcol[h] for h in R]
    kk = [_dot_nt(X.kb[h], k[h]) for h in R]
    qk = [_dot_nt(X.qs[h], k[h]) for h in R]
    X.A = [jnp.where(ri > ci, kk[h] * X.Dm[h], 0.0) for h in R]
    T = [eye - X.A[h] for h in R]
    P = list(X.A)
    for _ in range(5):
        P = [_hdot(P[h], P[h]) for h in R]
        T = [T[h] + _hdot(T[h], P[h]) for h in R]
    X.T = T
    X.b2 = [jnp.concatenate([bcol[h], bcol[h]], axis=-1) for h in R]
    X.u = [_hdot(T[h], v[h] * X.b2[h]) for h in R]
    X.w = [_hdot(T[h], X.kb[h] * X.eg[h]) for h in R]
    X.attn = [qk[h] * X.Dm[h] for h in R]
    X.qg = [X.qs[h] * X.eg[h] for h in R]
    X.kg = [k[h] * X.kdec[h] for h in R]
    return X


def gdn_fwd(q, k, v, gc, bc, grow):
    tp = q.shape[0]
    nc = tp // CHUNK
    hb = GDN_HEADS_PER_STEP

    def body(q_ref, k_ref, v_ref, gc_ref, bc_ref, gr_ref, o_ref, ss_ref, S_ref):
        c = pl.program_id(1)

        @pl.when(c == 0)
        def _():
            S_ref[...] = jnp.zeros_like(S_ref)

        R = range(hb)
        qc = [slice(h * DN_DK, (h + 1) * DN_DK) for h in R]
        vc = [slice(h * DN_DV, (h + 1) * DN_DV) for h in R]
        X = _gdn_chunk([q_ref[:, qc[h]] for h in R], [k_ref[:, qc[h]] for h in R], [v_ref[:, vc[h]] for h in R],
                       [gc_ref[h] for h in R], [bc_ref[h] for h in R], [gr_ref[h] for h in R])
        S = [S_ref[h] for h in R]
        for h in R:
            ss_ref[h, 0] = S[h]
        wS = [_dot(X.w[h], S[h]) for h in R]
        qS = [_dot(X.qg[h], S[h]) for h in R]
        vn = [X.u[h] - wS[h] for h in R]
        av = [_dot(X.attn[h], vn[h]) for h in R]
        kv = [_dot_tn(X.kg[h], vn[h]) for h in R]
        for h in R:
            o_ref[:, vc[h]] = qS[h] + av[h]
            S_ref[h] = S[h] * X.egl[h][:, 0:1] + kv[h]

    qk = pl.BlockSpec((CHUNK, hb * DN_DK), lambda g, c: (c, g))
    vv = pl.BlockSpec((CHUNK, hb * DN_DV), lambda g, c: (c, g))
    col = pl.BlockSpec((hb, CHUNK, LANE), lambda g, c: (g, c, 0))
    row = pl.BlockSpec((hb, SUB, LANE), lambda g, c: (g, c, 0))
    return pl.pallas_call(
        body, name="gdn_fwd", grid=(HEADS // hb, nc), in_specs=[qk, qk, vv, col, col, row],
        out_specs=[vv, pl.BlockSpec((hb, 1, DN_DK, DN_DV), lambda g, c: (g, c, 0, 0))],
        out_shape=[jax.ShapeDtypeStruct((tp, DN_V), f32), jax.ShapeDtypeStruct((HEADS, nc, DN_DK, DN_DV), f32)],
        scratch_shapes=[pltpu.VMEM((hb, DN_DK, DN_DV), f32)], compiler_params=_params(2),
    )(q, k, v, gc, bc, grow)


def gdn_bwd(q, k, v, gc, bc, grow, states, do):
    tp = q.shape[0]
    nc = tp // CHUNK
    C = CHUNK
    hb = GDN_HEADS_PER_STEP

    def body(q_ref, k_ref, v_ref, gc_ref, bc_ref, gr_ref, ss_ref, do_ref, dq_ref, dk_ref, dv_ref, dg_ref, db_ref, dS_ref):
        c = pl.program_id(1)

        @pl.when(c == 0)
        def _():
            dS_ref[...] = jnp.zeros_like(dS_ref)

        R = range(hb)
        qc = [slice(h * DN_DK, (h + 1) * DN_DK) for h in R]
        vc = [slice(h * DN_DV, (h + 1) * DN_DV) for h in R]
        k_ = [k_ref[:, qc[h]] for h in R]
        v_ = [v_ref[:, vc[h]] for h in R]
        bcol = [bc_ref[h] for h in R]
        X = _gdn_chunk([q_ref[:, qc[h]] for h in R], k_, v_, [gc_ref[h] for h in R], bcol, [gr_ref[h] for h in R])
        ri, ci = X.ri, X.ci
        S = [ss_ref[h, 0] for h in R]
        do_ = [do_ref[:, vc[h]] for h in R]
        dSn = [dS_ref[h] for h in R]
        wS = [_dot(X.w[h], S[h]) for h in R]
        ado = [_dot_tn(X.attn[h], do_[h]) for h in R]
        kdS = [_dot(X.kg[h], dSn[h]) for h in R]
        d_qg = [_dot_nt(do_[h], S[h]) for h in R]
        qdo = [_dot_tn(X.qg[h], do_[h]) for h in R]
        vn = [X.u[h] - wS[h] for h in R]
        d_vn = [ado[h] + kdS[h] for h in R]
        dovn = [_dot_nt(do_[h], vn[h]) for h in R]
        d_kg = [_dot_nt(vn[h], dSn[h]) for h in R]
        wdv = [_dot_tn(X.w[h], d_vn[h]) for h in R]
        dw = [-_dot_nt(d_vn[h], S[h]) for h in R]
        for h in R:
            dS_ref[h] = qdo[h] + X.egl[h][:, 0:1] * dSn[h] - wdv[h]
        dattn = [jnp.where(ri >= ci, dovn[h], 0.0) for h in R]
        dRu = [_hdot_tn(X.T[h], d_vn[h]) for h in R]
        dRw = [_hdot_tn(X.T[h], dw[h]) for h in R]
        dAu = [_hdot_nt(dRu[h], X.u[h]) for h in R]
        dAw = [_hdot_nt(dRw[h], X.w[h]) for h in R]
        dA = [jnp.where(ri > ci, -(dAu[h] + dAw[h]), 0.0) for h in R]
        dKK = [dA[h] * X.Dm[h] for h in R]
        dQK = [dattn[h] * X.Dm[h] for h in R]
        E = [dA[h] * X.A[h] + dattn[h] * X.attn[h] for h in R]
        dkb = [_dot(dKK[h], k_[h]) + dRw[h] * X.eg[h] for h in R]
        dk1 = [_dot_tn(dKK[h], X.kb[h]) for h in R]
        dqs = [_dot(dQK[h], k_[h]) + d_qg[h] * X.eg[h] for h in R]
        dk2 = [_dot_tn(dQK[h], X.qs[h]) for h in R]
        ones = jnp.ones((C, LANE), f32)
        colE = [_hdot_tn(E[h], ones) for h in R]
        rows = lax.broadcasted_iota(jnp.int32, (C, LANE), 0)
        upper = (ci >= ri).astype(f32)
        dgam = []
        for h in R:
            t = d_kg[h] * X.kg[h]
            dgl = _allsum(t) + X.egl[h][:, 0:1] * _allsum(S[h] * dSn[h])
            g = (_rowsum(E[h]) - colE[h] + _rowsum(dRw[h] * (X.kb[h] * X.eg[h])) + _rowsum(d_qg[h] * X.qg[h])
                 - _rowsum(t))
            dgam.append(g + jnp.where(rows == C - 1, dgl, 0.0))
        dg = [_hdot(upper, dgam[h]) for h in R]
        for h in R:
            dv_ref[:, vc[h]] = dRu[h] * X.b2[h]
            dbeta = _rowsum(dRu[h] * v_[h]) + _rowsum(dkb[h] * k_[h])
            dq_ref[:, qc[h]] = dqs[h] * (DN_DK ** -0.5)
            dk_ref[:, qc[h]] = dk1[h] + dk2[h] + dkb[h] * bcol[h] + d_kg[h] * X.kdec[h]
            dg_ref[h] = dg[h]
            db_ref[h] = jnp.broadcast_to(dbeta, (C, LANE))

    rc = lambda c: nc - 1 - c
    qk = pl.BlockSpec((CHUNK, hb * DN_DK), lambda g, c: (rc(c), g))
    vv = pl.BlockSpec((CHUNK, hb * DN_DV), lambda g, c: (rc(c), g))
    col = pl.BlockSpec((hb, CHUNK, LANE), lambda g, c: (g, rc(c), 0))
    row = pl.BlockSpec((hb, SUB, LANE), lambda g, c: (g, rc(c), 0))
    st = pl.BlockSpec((hb, 1, DN_DK, DN_DV), lambda g, c: (g, rc(c), 0, 0))
    return pl.pallas_call(
        body, name="gdn_bwd", grid=(HEADS // hb, nc), in_specs=[qk, qk, vv, col, col, row, st, vv],
        out_specs=[qk, qk, vv, col, col],
        out_shape=[jax.ShapeDtypeStruct((tp, DN_QK), f32), jax.ShapeDtypeStruct((tp, DN_QK), f32),
                   jax.ShapeDtypeStruct((tp, DN_V), f32), jax.ShapeDtypeStruct((HEADS, tp, LANE), f32),
                   jax.ShapeDtypeStruct((HEADS, tp, LANE), f32)],
        scratch_shapes=[pltpu.VMEM((hb, DN_DK, DN_DV), f32)], compiler_params=_params(2),
    )(q, k, v, gc, bc, grow, states, do)


def _cumsum_after(x, nb, us):
    B = SB_BLOCK
    hi = x.astype(bf16)
    lo = (x - hi.astype(f32)).astype(bf16)
    rows = [p[:, b * B:(b + 1) * B] for p in (hi, lo) for b in range(nb)]
    r = jnp.dot(jnp.concatenate(rows, axis=0), us, preferred_element_type=f32)
    out = [r[b * B:(b + 1) * B] + r[(nb + b) * B:(nb + b + 1) * B] for b in range(nb)]
    return out[0] if nb == 1 else jnp.concatenate(out, axis=1)


def _later_blocks(x, nb, carry):
    B = SB_BLOCK
    tot = [_rowsum(x[:, b * B:(b + 1) * B]) for b in range(nb)]
    offs = [None] * nb
    run = carry
    for b in range(nb - 1, -1, -1):
        offs[b] = jnp.broadcast_to(run, (B, B))
        run = run + tot[b]
    return (offs[0] if nb == 1 else jnp.concatenate(offs, axis=1)), run


def _sb_group(i, t):
    top = i - SB_GROUP * t
    jlo = jnp.maximum(top - SB_GROUP + 1, 0)
    rows = pl.ds(pl.multiple_of(jlo * SB_BLOCK, SB_BLOCK), SB_GROUP * SB_BLOCK)
    return jlo, rows, (top + 1) * SB_BLOCK


def _sb_weights(q, kcat, i, jlo, kend, cs, us, masked):
    B, nb = SB_BLOCK, SB_GROUP
    R = range(len(q))
    z = [_dot_nt(q[h], kcat[h]) * (SB_DH ** -0.5) for h in R]
    e = [jnp.exp(-jnp.abs(z[h])) for h in R]
    l1p = [jnp.log(1.0 + e[h]) for h in R]
    lsp = [jnp.minimum(z[h], 0.0) - l1p[h] for h in R]
    lk = [-jnp.maximum(z[h], 0.0) - l1p[h] for h in R]
    vis = None
    if masked:
        qpos = i * B + lax.broadcasted_iota(jnp.int32, (B, nb * B), 0)
        kpos = jlo * B + lax.broadcasted_iota(jnp.int32, (B, nb * B), 1)
        vis = jnp.logical_and(kpos < jnp.minimum(qpos, kend), kpos >= P0)
        lk = [jnp.where(vis, lk[h], 0.0) for h in R]
    later = [_later_blocks(lk[h], nb, cs[h]) for h in R]
    cum = [_cumsum_after(lk[h], nb, us) for h in R]
    w = [jnp.exp(lsp[h] + cum[h] + later[h][0]) for h in R]
    if masked:
        w = [jnp.where(vis, w[h], 0.0) for h in R]
    return z, e, vis, w, [later[h][1] for h in R]


def _sb_loop(i, step, carry):
    trips = (i + SB_GROUP) // SB_GROUP
    carry = step(True)(0, carry)
    carry = lax.fori_loop(1, trips - 1, step(False), carry)
    return lax.fori_loop(jnp.maximum(trips - 1, 1), trips, step(True), carry)


def sb_fwd(qs, ks, vs):
    tp = qs.shape[0]
    nq = tp // SB_BLOCK
    B, G, hb = SB_BLOCK, SB_GROUP, SB_HEADS_PER_STEP
    assert tp >= G * B

    def body(q_ref, k_ref, v_ref, o_ref):
        i = pl.program_id(1)
        R = range(hb)
        hs = [slice(h * SB_DH, (h + 1) * SB_DH) for h in R]
        q = [q_ref[:, hs[h]] for h in R]
        us = (lax.broadcasted_iota(jnp.int32, (B, B), 0) > lax.broadcasted_iota(jnp.int32, (B, B), 1)).astype(bf16)

        def make_step(masked):
            def step(t, carry):
                acc, cs = carry
                jlo, rows, kend = _sb_group(i, t)
                _, _, _, w, cs = _sb_weights(q, [k_ref[rows, hs[h]] for h in R], i, jlo, kend, cs, us, masked)
                pv = [_dot(w[h], v_ref[rows, hs[h]]) for h in R]
                return tuple(acc[h] + pv[h] for h in R), tuple(cs)
            return step

        carry = (tuple(jnp.zeros((B, SB_DH), f32) for _ in R), tuple(jnp.zeros((B, 1), f32) for _ in R))
        acc, _ = _sb_loop(i, make_step, carry)
        for h in R:
            o_ref[:, hs[h]] = acc[h]

    blk = pl.BlockSpec((B, hb * SB_DH), lambda g, i: (i, g))
    full = pl.BlockSpec((tp, hb * SB_DH), lambda g, i: (0, g))
    return pl.pallas_call(
        body, name="sb_fwd", grid=(HEADS // hb, nq), in_specs=[blk, full, full], out_specs=blk,
        out_shape=jax.ShapeDtypeStruct((tp, SB_W), f32), compiler_params=_params(2),
    )(qs, ks, vs)


def sb_bwd(qs, ks, vs, o, do):
    tp = qs.shape[0]
    nq = tp // SB_BLOCK
    B, G, hb = SB_BLOCK, SB_GROUP, SB_HEADS_PER_STEP
    assert tp >= G * B

    def body(q_ref, k_ref, v_ref, o_ref, do_ref, dq_ref, dk_ref, dv_ref):
        i = pl.program_id(1)

        @pl.when(i == 0)
        def _():
            dk_ref[...] = jnp.zeros_like(dk_ref)
            dv_ref[...] = jnp.zeros_like(dv_ref)

        R = range(hb)
        hs = [slice(h * SB_DH, (h + 1) * SB_DH) for h in R]
        q = [q_ref[:, hs[h]] for h in R]
        dob = [do_ref[:, hs[h]].astype(bf16) for h in R]
        et = [_rowsum(dob[h].astype(f32) * o_ref[:, hs[h]]) for h in R]
        us = (lax.broadcasted_iota(jnp.int32, (B, B), 0) > lax.broadcasted_iota(jnp.int32, (B, B), 1)).astype(bf16)

        def make_step(masked):
            def step(t, carry):
                dq, cs, ce = carry
                jlo, rows, kend = _sb_group(i, t)
                kcat = [k_ref[rows, hs[h]] for h in R]
                dwv = [_dot_nt(dob[h], v_ref[rows, hs[h]]) for h in R]
                z, e, vis, w, cs = _sb_weights(q, kcat, i, jlo, kend, cs, us, masked)
                wb = [w[h].astype(bf16) for h in R]
                ee = [dwv[h] * wb[h].astype(f32) for h in R]
                later = [_later_blocks(ee[h], G, ce[h]) for h in R]
                cum = [_cumsum_after(ee[h], G, us) for h in R]
                dz = []
                for h in R:
                    f = et[h] - (ee[h] + cum[h] + later[h][0])
                    r = 1.0 / (1.0 + e[h])
                    pos = z[h] >= 0
                    sp = jnp.where(pos, r, e[h] * r)
                    sn = jnp.where(pos, e[h] * r, r)
                    d = ee[h] * sn - f * sp
                    if masked:
                        d = jnp.where(vis, d, 0.0)
                    dz.append((d * (SB_DH ** -0.5)).astype(bf16))
                dkj = [_dot_tn(dz[h], q[h]) for h in R]
                dvj = [_dot_tn(wb[h], dob[h]) for h in R]
                dqj = [_dot(dz[h], kcat[h]) for h in R]
                for h in R:
                    dk_ref[rows, hs[h]] += dkj[h]
                    dv_ref[rows, hs[h]] += dvj[h]
                return tuple(dq[h] + dqj[h] for h in R), tuple(cs), tuple(later[h][1] for h in R)
            return step

        z0 = tuple(jnp.zeros((B, 1), f32) for _ in R)
        dq, _, _ = _sb_loop(i, make_step, (tuple(jnp.zeros((B, SB_DH), f32) for _ in R), z0, z0))
        for h in R:
            dq_ref[:, hs[h]] = dq[h]

    blk = pl.BlockSpec((B, hb * SB_DH), lambda g, i: (i, g))
    full = pl.BlockSpec((tp, hb * SB_DH), lambda g, i: (0, g))
    return pl.pallas_call(
        body, name="sb_bwd", grid=(HEADS // hb, nq), in_specs=[blk, full, full, blk, blk], out_specs=[blk, full, full],
        out_shape=[jax.ShapeDtypeStruct((tp, SB_W), f32)] * 3, compiler_params=_params(2),
    )(qs, ks, vs, o, do)


def adamw(w, g, m, v, name):
    r, c = w.shape
    rt = _tile(r, 128, SUB) if r % SUB == 0 else r
    blk = pl.BlockSpec((rt, c), lambda i: (i, 0))
    c1 =1.0 - ADAM_B1 ** ADAM_STEP
    c2 = 1.0 - ADAM_B2 ** ADAM_STEP

    def body(w_ref, g_ref, m_ref, v_ref, d_ref, mo_ref, vo_ref):
        g_ = g_ref[...]
        m_ = ADAM_B1 * m_ref[...] + (1.0 - ADAM_B1) * g_
        v_ = ADAM_B2 * v_ref[...] + (1.0 - ADAM_B2) * (g_ * g_)
        mo_ref[...] = m_
        vo_ref[...] = v_
        d_ref[...] = -ADAM_LR * ((m_ / c1) / (jnp.sqrt(v_ / c2) + ADAM_EPS) + ADAM_WD * w_ref[...])

    return pl.pallas_call(
        body, name=name, grid=(r // rt,), in_specs=[blk] * 4, out_specs=[blk] * 3,
        out_shape=[jax.ShapeDtypeStruct((r, c), f32)] * 3, compiler_params=_params(1),
    )(w, g, m, v)


def sum_slots(x, name):
    n, r, c = x.shape
    rt = _tile(r, 128, SUB) if r % SUB == 0 else r
    blk = pl.BlockSpec((n, rt, c), lambda i: (0, i, 0))

    def body(x_ref, o_ref):
        acc = x_ref[0]
        for s in range(1, n):
            acc = acc + x_ref[s]
        o_ref[...] = acc

    return pl.pallas_call(
        body, name=name, grid=(r // rt,), in_specs=[blk], out_specs=pl.BlockSpec((rt, c), lambda i: (i, 0)),
        out_shape=jax.ShapeDtypeStruct((r, c), f32), compiler_params=_params(1),
    )(x)


def add2(a, b, name):
    n, r, c = a.shape
    rt = _tile(r, 64, SUB) if r % SUB == 0 else r
    blk = pl.BlockSpec((n, rt, c), lambda i: (0, i, 0))

    def body(a_ref, b_ref, o_ref):
        o_ref[...] = a_ref[...] + b_ref[...]

    return pl.pallas_call(
        body, name=name, grid=(r // rt,), in_specs=[blk, blk], out_specs=blk,
        out_shape=jax.ShapeDtypeStruct((n, r, c), f32), compiler_params=_params(1),
    )(a, b)


_ANY = pl.BlockSpec(memory_space=pl.ANY)
_MESH = pl.DeviceIdType.MESH


def _coords():
    return lax.axis_index("x"), lax.axis_index("y"), lax.axis_index("c")


def _chip_peer(x, y, r):
    return x ^ (r >> 1), y ^ (r & 1)


def gather_chips(big, small, name):
    nb, n = len(big), len(big) + len(small)
    shards = list(big) + list(small)
    kb = nb * (N_CHIPS - 1)

    def body(*refs):
        src, dst = refs[:n], refs[n:2 * n]
        send, recv, fsend, frecv = refs[2 * n:]
        x, y, c = _coords()
        me = 2 * x + y
        sib = (x, y, 1 - c)
        peers = [_chip_peer(x, y, r) for r in range(1, N_CHIPS)]

        def direct(t, j, slot):
            s = t * (N_CHIPS - 1) + j
            if t < nb:
                return pltpu.make_async_remote_copy(src[t].at[c], dst[t].at[slot, c], send.at[s], recv.at[s],
                                                    device_id=(*peers[j], c), device_id_type=_MESH)
            return pltpu.make_async_remote_copy(src[t], dst[t].at[slot], send.at[s], recv.at[s],
                                                device_id=(*peers[j], c), device_id_type=_MESH)

        def passed(t, j, half):
            s = t * (N_CHIPS - 1) + j
            px, py = peers[j]
            part = dst[t].at[2 * px + py, half]
            return pltpu.make_async_remote_copy(part, part, fsend.at[s], frecv.at[s], device_id=sib, device_id_type=_MESH)

        outs = [direct(t, j, me) for t in range(n) for j in range(N_CHIPS - 1)]
        for cp in outs:
            cp.start()
        fwd = []
        for t in range(nb):
            for j in range(N_CHIPS - 1):
                px, py = peers[j]
                direct(t, j, 2 * px + py).wait_recv()
                fwd.append(passed(t, j, c))
                fwd[-1].start()
        for t in range(nb, n):
            for j in range(N_CHIPS - 1):
                px, py = peers[j]
                direct(t, j, 2 * px + py).wait_recv()
        for t in range(nb):
            for j in range(N_CHIPS - 1):
                passed(t, j, 1 - c).wait_recv()
        for cp in outs + fwd:
            cp.wait_send()

    k = n * (N_CHIPS - 1)
    return pl.pallas_call(
        body, name=name, in_specs=[_ANY] * n, out_specs=[_ANY] * n,
        out_shape=[jax.ShapeDtypeStruct((N_CHIPS,) + s.shape, s.dtype) for s in shards],
        scratch_shapes=[pltpu.SemaphoreType.DMA((k,)), pltpu.SemaphoreType.DMA((k,)),
                        pltpu.SemaphoreType.DMA((kb,)), pltpu.SemaphoreType.DMA((kb,))],
    )(*shards)


def sibling_swap_halves(grads, name):
    n = len(grads)

    def body(*refs):
        src, dst = refs[:n], refs[n:2 * n]
        send, recv = refs[2 * n:]
        x, y, c = _coords()
        cps = []
        for t in range(n):
            for o in range(N_CHIPS):
                s = t * N_CHIPS + o
                cps.append(pltpu.make_async_remote_copy(src[t].at[o, 1 - c], dst[t].at[o], send.at[s], recv.at[s],
                                                        device_id=(x, y, 1 - c), device_id_type=_MESH))
        for cp in cps:
            cp.start()
        for cp in cps:
            cp.wait_recv()
        for cp in cps:
            cp.wait_send()

    k = n * N_CHIPS
    return pl.pallas_call(
        body, name=name, in_specs=[_ANY] * n, out_specs=[_ANY] * n,
        out_shape=[jax.ShapeDtypeStruct((N_CHIPS,) + g.shape[2:], g.dtype) for g in grads],
        scratch_shapes=[pltpu.SemaphoreType.DMA((k,)), pltpu.SemaphoreType.DMA((k,))],
    )(*grads)


def scatter_chips(parts, name):
    n = len(parts)

    def body(*refs):
        src, dst = refs[:n], refs[n:2 * n]
        send, recv = refs[2 * n:]
        x, y, c = _coords()
        me = 2 * x + y
        outs = []
        for t in range(n):
            for r in range(1, N_CHIPS):
                px, py = _chip_peer(x, y, r)
                s = t * (N_CHIPS - 1) + r - 1
                outs.append(pltpu.make_async_remote_copy(src[t].at[2 * px + py], dst[t].at[me], send.at[s], recv.at[s],
                                                         device_id=(px, py, c), device_id_type=_MESH))
        for cp in outs:
            cp.start()
        for t in range(n):
            for r in range(1, N_CHIPS):
                px, py = _chip_peer(x, y, r)
                s = t * (N_CHIPS - 1) + r - 1
                pltpu.make_async_remote_copy(src[t].at[me], dst[t].at[2 * px + py], send.at[s], recv.at[s],
                                             device_id=(px, py, c), device_id_type=_MESH).wait_recv()
        for cp in outs:
            cp.wait_send()

    k = n * (N_CHIPS - 1)
    return pl.pallas_call(
        body, name=name, in_specs=[_ANY] * n, out_specs=[_ANY] * n,
        out_shape=[jax.ShapeDtypeStruct(p.shape, p.dtype) for p in parts],
        scratch_shapes=[pltpu.SemaphoreType.DMA((k,)), pltpu.SemaphoreType.DMA((k,))],
    )(*parts)


def sibling_send(halves, name):
    n = len(halves)

    def body(*refs):
        src, dst = refs[:n], refs[n:2 * n]
        send, recv = refs[2 * n:]
        x, y, c = _coords()
        cps = [pltpu.make_async_remote_copy(src[t], dst[t], send.at[t], recv.at[t],
                                            device_id=(x, y, 1 - c), device_id_type=_MESH) for t in range(n)]
        for cp in cps:
            cp.start()
        for cp in cps:
            cp.wait_recv()
        for cp in cps:
            cp.wait_send()

    return pl.pallas_call(
        body, name=name, in_specs=[_ANY] * n, out_specs=[_ANY] * n,
        out_shape=[jax.ShapeDtypeStruct(h.shape, h.dtype) for h in halves],
        scratch_shapes=[pltpu.SemaphoreType.DMA((n,)), pltpu.SemaphoreType.DMA((n,))],
    )(*halves)


def gather_all(block, name):
    def body(src, dst, send, recv, loc):
        x, y, c = _coords()
        me = 4 * x + 2 * y + c
        mine = pltpu.make_async_copy(src, dst.at[me], loc)
        mine.start()
        outs = []
        for r in range(1, N_DEV):
            peer = (x ^ (r >> 2), y ^ ((r >> 1) & 1), c ^ (r & 1))
            outs.append(pltpu.make_async_remote_copy(src, dst.at[me], send.at[r - 1], recv.at[r - 1],
                                                     device_id=peer, device_id_type=_MESH))
        for cp in outs:
            cp.start()
        for r in range(1, N_DEV):
            px, py, pc = x ^ (r >> 2), y ^ ((r >> 1) & 1), c ^ (r & 1)
            pltpu.make_async_remote_copy(src, dst.at[4 * px + 2 * py + pc], send.at[r - 1], recv.at[r - 1],
                                         device_id=(px, py, pc), device_id_type=_MESH).wait_recv()
        for cp in outs:
            cp.wait_send()
        mine.wait()

    return pl.pallas_call(
        body, name=name, in_specs=[_ANY], out_specs=_ANY,
        out_shape=jax.ShapeDtypeStruct((N_DEV,) + block.shape, block.dtype),
        scratch_shapes=[pltpu.SemaphoreType.DMA((N_DEV - 1,)), pltpu.SemaphoreType.DMA((N_DEV - 1,)),
                        pltpu.SemaphoreType.DMA(())],
    )(block)


def _pad_lanes(v, n=LANE):
    return jnp.pad(v, ((0, 0), (0, n - v.shape[1])))


def _conv_w8(w):
    return jnp.pad(w, ((0, SUB - DN_CONV), (0, 0)))


def _row_layout(gc, tp):
    nc = tp // CHUNK
    g = gc[:, :, 0].reshape(HEADS, nc, 1, CHUNK)
    g = jnp.broadcast_to(g, (HEADS, nc, SUB, CHUNK))
    return jnp.pad(g, ((0, 0), (0, 0), (0, 0), (0, LANE - CHUNK))).reshape(HEADS, nc * SUB, LANE)


def _step(x, meta, W, target):
    seq = x.shape[0]
    tp = P0 + N_META + seq
    h0 = jnp.concatenate([jnp.zeros((P0, D_MODEL), f32), meta, x], axis=0)
    w_in = W["w_in"]
    w_big = jnp.concatenate([w_in[:, :AB_COL], w_in[:, AB_COL + 2 * HEADS:]], axis=1)
    w_ab = _pad_lanes(w_in[:, AB_COL:AB_COL + 2 * HEADS])
    cq8, ck8, cv8 = _conv_w8(W["conv_q"]), _conv_w8(W["conv_k"]), _conv_w8(W["conv_v"])
    al, dtb = _pad_lanes(W["dn_a_log"]), _pad_lanes(W["dn_dt_bias"])

    n1 = rms_fwd(h0, W["norm_mix_gain"], "rms1_fwd")
    proj = matmul(n1, w_big, "nn", "proj_fwd")
    pab = matmul(n1, w_ab, "nn", "pab_fwd")
    qn = conv_fwd(proj, cq8, C_DQ * 8, 8, True, "conv_q_fwd")
    kn = conv_fwd(proj, ck8, C_DK * 8, 8, True, "conv_k_fwd")
    va = conv_fwd(proj, cv8, C_DV * 8, 16, False, "conv_v_fwd")
    gc, bc = ab_fwd(pab, al, dtb)
    grow = _row_layout(gc, tp)
    o_dn, states = gdn_fwd(qn, kn, va, gc, bc, grow)
    on = dn_out_fwd(o_dn, proj, W["dn_out_norm_gain"])
    qs, ks, vs = sb_prep_fwd(proj, W["sb_q_norm_gain"], W["sb_k_norm_gain"])
    o_sb = sb_fwd(qs, ks, vs)
    o_sb16 = cast_bf16(o_sb, "o_sb_cast")
    ydn = matmul(on, W["w_branch_dn"], "nn", "ydn_fwd")
    ysb = matmul(o_sb16, W["w_branch_sb"], "nn", "ysb_fwd")
    merged = merge_fwd(proj, ydn, ysb)
    h1 = matmul(merged, W["w_out"], "nn", "wout_fwd", residual=h0)
    n2 = rms_fwd(h1, W["norm_ffn_gain"], "rms2_fwd")
    u = matmul(n2, W["w_ffn_in"], "nn", "ffn_in_fwd", tn_t=512)
    act = swiglu_fwd(u)
    y = matmul(act, W["w_ffn_out"], "nn", "ffn_out_fwd", residual=h1)
    dy, loss = loss_head(y, target)

    G = {}
    dy16 = cast_bf16(dy, "dy_cast")
    dact = matmul(dy16, W["w_ffn_out"], "nt", "ffn_out_dx", tn_t=1408)
    G["w_ffn_out"] = matmul(act, dy16, "tn", "ffn_out_dw", tm_t=1408)
    dgate, dup = swiglu_bwd(u, dact)
    du = jnp.concatenate([dgate, dup], axis=1)
    dn2 = matmul(du, W["w_ffn_in"], "nt", "ffn_in_dx", tk_t=512)
    G["w_ffn_in"] = matmul(n2, du, "tn", "ffn_in_dw", tn_t=512)
    dh1, G["norm_ffn_gain"] = rms_bwd(h1, W["norm_ffn_gain"], dn2, dy, "rms2_bwd")
    dh1_16 = cast_bf16(dh1, "dh1_cast")
    dmerged = matmul(dh1_16, W["w_out"], "nt", "wout_dx")
    G["w_out"] = matmul(merged, dh1_16, "tn", "wout_dw")
    dyd, dys, dgd, dgs = merge_bwd(proj, ydn, ysb, dmerged)
    don = matmul(dyd, W["w_branch_dn"], "nt", "ydn_dx")
    G["w_branch_dn"] = matmul(on, dyd, "tn", "ydn_dw")
    do_sb = matmul(dys, W["w_branch_sb"], "nt", "ysb_dx")
    G["w_branch_sb"] = matmul(o_sb16, dys, "tn", "ysb_dw")
    do_dn, dz, G["dn_out_norm_gain"] = dn_out_bwd(o_dn, proj, W["dn_out_norm_gain"], don)
    dqn, dkn, dva, dgc, dbc = gdn_bwd(qn, kn, va, gc, bc, grow, states, do_dn)
    dpab, dal, ddt = ab_bwd(pab, al, dtb, dgc, dbc)
    G["dn_a_log"], G["dn_dt_bias"] = dal[:, :HEADS], ddt[:, :HEADS]
    dyq, dcq = conv_bwd_act(proj, cq8, dqn, C_DQ * 8, 8, True, "conv_q_bwd")
    dyk, dck = conv_bwd_act(proj, ck8, dkn, C_DK * 8, 8, True, "conv_k_bwd")
    dyv, dcv = conv_bwd_act(proj, cv8, dva, C_DV * 8, 16, False, "conv_v_bwd")
    G["conv_q"], G["conv_k"], G["conv_v"] = dcq[:DN_CONV], dck[:DN_CONV], dcv[:DN_CONV]
    d_dq = conv_bwd_in(dyq, cq8, "conv_q_dx")
    d_dk = conv_bwd_in(dyk, ck8, "conv_k_dx")
    d_dv = conv_bwd_in(dyv, cv8, "conv_v_dx")
    dqs, dks, dvs = sb_bwd(qs, ks, vs, o_sb, do_sb)
    d_sq, d_sk, G["sb_q_norm_gain"], G["sb_k_norm_gain"] = sb_prep_bwd(
        proj, W["sb_q_norm_gain"], W["sb_k_norm_gain"], dqs, dks)
    d_sv = cast_bf16(dvs, "dvs_cast")
    dproj = jnp.concatenate([d_dq, d_dk, d_dv, dz, d_sq, d_sk, d_sv, dgd, dgs], axis=1)
    dn1 = matmul(dproj, w_big, "nt", "proj_dx", tk_t=1024)
    dn1 = matmul(dpab, w_ab, "nt", "pab_dx", residual=dn1)
    dw_big = matmul(n1, dproj, "tn", "proj_dw")
    dw_ab = matmul(n1, dpab, "tn", "pab_dw")
    G["w_in"] = jnp.concatenate([dw_big[:, :AB_COL], dw_ab[:, :2 * HEADS], dw_big[:, AB_COL:]], axis=1)
    dh0, G["norm_mix_gain"] = rms_bwd(h0, W["norm_mix_gain"], dn1, dh1, "rms1_bwd")
    G["meta_tokens"] = dh0[P0:P0 + N_META]
    return loss, dh0[P0 + N_META:], G


_BIG = ("w_in", "w_branch_dn", "w_branch_sb", "w_out", "w_ffn_in", "w_ffn_out")
_COL_SHARDED = ("w_in", "w_ffn_in", "meta_tokens", "conv_q", "conv_k", "conv_v")
_SMALL_REPL = ("norm_mix_gain", "norm_ffn_gain", "dn_a_log", "dn_dt_bias", "dn_out_norm_gain", "sb_q_norm_gain",
               "sb_k_norm_gain")
_SMALL_SHARD = ("meta_tokens", "conv_q", "conv_k", "conv_v")
_ORDER = ("meta_tokens", "norm_mix_gain", "w_in", "conv_q", "conv_k", "conv_v", "dn_a_log", "dn_dt_bias",
          "dn_out_norm_gain", "sb_q_norm_gain", "sb_k_norm_gain", "w_branch_dn", "w_branch_sb", "w_out",
          "norm_ffn_gain", "w_ffn_in", "w_ffn_out")


def _unshard(g4, name):
    if name in _COL_SHARDED:
        r, cs = g4.shape[1:]
        return jnp.transpose(g4, (1, 0, 2)).reshape(r, N_CHIPS * cs)
    return g4.reshape((-1,) + g4.shape[2:])


def _to_shards(full, name):
    if name in _COL_SHARDED:
        r, c = full.shape
        return jnp.transpose(full.reshape(r, N_CHIPS, c // N_CHIPS), (1, 0, 2))
    r, c = full.shape
    return full.reshape(N_CHIPS, r // N_CHIPS, c)


def _rows_1024(a):
    r, c = a.shape
    if c >= 1024:
        return a.reshape(r * (c // 1024), 1024)
    return jnp.pad(a, ((0, 0), (0, 1024 - c)))


def kernel(x, meta_tokens, norm_mix_gain, w_in, conv_q, conv_k, conv_v, dn_a_log, dn_dt_bias, dn_out_norm_gain, sb_q_norm_gain, sb_k_norm_gain, w_branch_dn, w_branch_sb, w_out, norm_ffn_gain, w_ffn_in, w_ffn_out, loss_target, m_meta_tokens, m_norm_mix_gain, m_w_in, m_conv_q, m_conv_k, m_conv_v, m_dn_a_log, m_dn_dt_bias, m_dn_out_norm_gain, m_sb_q_norm_gain, m_sb_k_norm_gain, m_w_branch_dn, m_w_branch_sb, m_w_out, m_norm_ffn_gain, m_w_ffn_in, m_w_ffn_out, v_meta_tokens, v_norm_mix_gain, v_w_in, v_conv_q, v_conv_k, v_conv_v, v_dn_a_log, v_dn_dt_bias, v_dn_out_norm_gain, v_sb_q_norm_gain, v_sb_k_norm_gain, v_w_branch_dn, v_w_branch_sb, v_w_out, v_norm_ffn_gain, v_w_ffn_in, v_w_ffn_out):
    Wl = dict(meta_tokens=meta_tokens, norm_mix_gain=norm_mix_gain, w_in=w_in[0], conv_q=conv_q[0], conv_k=conv_k[0],
              conv_v=conv_v[0], dn_a_log=dn_a_log, dn_dt_bias=dn_dt_bias, dn_out_norm_gain=dn_out_norm_gain,
              sb_q_norm_gain=sb_q_norm_gain, sb_k_norm_gain=sb_k_norm_gain, w_branch_dn=w_branch_dn[0],
              w_branch_sb=w_branch_sb[0], w_out=w_out[0], norm_ffn_gain=norm_ffn_gain, w_ffn_in=w_ffn_in[0],
              w_ffn_out=w_ffn_out[0])
    Ml = dict(meta_tokens=m_meta_tokens, norm_mix_gain=m_norm_mix_gain, w_in=m_w_in[0], conv_q=m_conv_q[0],
              conv_k=m_conv_k[0], conv_v=m_conv_v[0], dn_a_log=m_dn_a_log, dn_dt_bias=m_dn_dt_bias,
              dn_out_norm_gain=m_dn_out_norm_gain, sb_q_norm_gain=m_sb_q_norm_gain, sb_k_norm_gain=m_sb_k_norm_gain,
              w_branch_dn=m_w_branch_dn[0], w_branch_sb=m_w_branch_sb[0], w_out=m_w_out[0],
              norm_ffn_gain=m_norm_ffn_gain, w_ffn_in=m_w_ffn_in[0], w_ffn_out=m_w_ffn_out[0])
    Vl = dict(meta_tokens=v_meta_tokens, norm_mix_gain=v_norm_mix_gain, w_in=v_w_in[0], conv_q=v_conv_q[0],
              conv_k=v_conv_k[0], conv_v=v_conv_v[0], dn_a_log=v_dn_a_log, dn_dt_bias=v_dn_dt_bias,
              dn_out_norm_gain=v_dn_out_norm_gain, sb_q_norm_gain=v_sb_q_norm_gain, sb_k_norm_gain=v_sb_k_norm_gain,
              w_branch_dn=v_w_branch_dn[0], w_branch_sb=v_w_branch_sb[0], w_out=v_w_out[0],
              norm_ffn_gain=v_norm_ffn_gain, w_ffn_in=v_w_ffn_in[0], w_ffn_out=v_w_ffn_out[0])
    lead = {n: (1,) if (n in _BIG or n in ("conv_q", "conv_k", "conv_v")) else () for n in _ORDER}

    chip = 2 * lax.axis_index("x") + lax.axis_index("y")
    big = [Wl[n].astype(bf16) for n in _BIG]
    big = [b.reshape(2, b.shape[0] // 2, b.shape[1]) for b in big]
    small = [Wl[n] for n in _SMALL_SHARD]
    gathered = gather_chips(big, small, "gather_weights")
    W = dict(Wl)
    for n, own, g4 in zip(list(_BIG) + list(_SMALL_SHARD), big + small, gathered):
        g4 = lax.dynamic_update_slice(g4, own[None], (chip,) + (0,) * own.ndim)
        if n in _BIG:
            g4 = g4.reshape(N_CHIPS, 2 * g4.shape[2], g4.shape[3])
        W[n] = _unshard(g4, n)

    loss, grad_x, G = _step(x[0], W["meta_tokens"], W, loss_target[0])

    c = lax.axis_index("c")
    g4 = [_to_shards(G[n], n) for n in _BIG]
    g42 = [g.reshape(N_CHIPS, 2, g.shape[1] // 2, g.shape[2]) for g in g4]
    from_sib = sibling_swap_halves(g42, "grad_sibling_swap")
    mine = [lax.dynamic_index_in_dim(g, c, axis=1, keepdims=False) for g in g42]
    chip_part = [add2(a, b, "grad_pair_add_%d" % t) for t, (a, b) in enumerate(zip(mine, from_sib))]
    slots = scatter_chips(chip_part, "grad_chip_exchange")
    slots = [lax.dynamic_update_slice(s, lax.dynamic_index_in_dim(p, chip, axis=0, keepdims=True), (chip, 0, 0))
             for s, p in zip(slots, chip_part)]
    halves = [sum_slots(s, "grad_chip_sum_%d" % t) for t, s in enumerate(slots)]
    theirs = sibling_send(halves, "grad_sibling_send")
    Gs = {}
    for n, h, o in zip(_BIG, halves, theirs):
        Gs[n] = lax.dynamic_update_slice(jnp.concatenate([o, o], axis=0), h, (c * h.shape[0], 0))

    small_names = list(_SMALL_REPL) + list(_SMALL_SHARD)
    pieces = [_rows_1024(G[n]) for n in small_names] + [_rows_1024(loss)]
    counts = [p.shape[0] for p in pieces]
    pack = jnp.concatenate(pieces, axis=0)
    pad_rows = (-pack.shape[0]) % SUB
    pack = jnp.pad(pack, ((0, pad_rows), (0, 0)))
    total = sum_slots(gather_all(pack, "small_gather"), "small_sum")
    chip = 2 * lax.axis_index("x") + lax.axis_index("y")
    row = 0
    for n, cnt in zip(small_names, counts[:-1]):
        blk = total[row:row + cnt]
        row += cnt
        full_shape = G[n].shape
        if full_shape[1] >= 1024:
            blk = blk.reshape(full_shape)
        else:
            blk = blk[:, :full_shape[1]]
        if n in _SMALL_SHARD:
            cs = full_shape[1] // N_CHIPS
            blk = lax.dynamic_slice_in_dim(blk, chip * cs, cs, axis=1)
        Gs[n] = blk
    loss_out = total[row, 0]

    grads, deltas, new_m, new_v = [], [], [], []
    for n in _ORDER:
        d, m2, v2 = adamw(Wl[n], Gs[n], Ml[n], Vl[n], "adamw_" + n)
        shape = lead[n] + Wl[n].shape
        grads.append(Gs[n].reshape(shape))
        deltas.append(d.reshape(shape))
        new_m.append(m2.reshape(shape))
        new_v.append(v2.reshape(shape))
    return (loss_out, grad_x[None], *grads, *deltas, *new_m, *new_v)
```

```python
import jax
import jax.numpy as jnp
from jax import lax
from jax.experimental import pallas as pl
from jax.experimental.pallas import tpu as pltpu

f32 = jnp.float32
bf16 = jnp.bfloat16

D_MODEL = 1024
N_META = 16
CHUNK = 64
HEADS = 8
DN_DK = 128
DN_DV = 256
DN_CONV = 4
DN_QK = HEADS * DN_DK
DN_V = HEADS * DN_DV
SB_DH = 128
SB_W = HEADS * SB_DH
SB_BLOCK = 128
SB_GROUP = 4
SB_HEADS_PER_STEP = 2
GDN_HEADS_PER_STEP = 8
CONV_W = 512
D_FF = 2816
RMS_EPS = 1e-6
L2_EPS = 1e-6
ADAM_LR = 0.001
ADAM_B1 = 0.9
ADAM_B2 = 0.999
ADAM_EPS = 1e-08
ADAM_WD = 0.01
ADAM_STEP = 10

P0 = 112
LANE = 128
SUB = 8
VMEM_LIMIT = 48 * 1024 * 1024
N_CHIPS = 4
N_DEV = 8

C_DQ, C_DK, C_DV, C_DZ, C_SQ, C_SK, C_SV, C_GDN, C_GSB = 0, 1, 2, 4, 6, 7, 8, 9, 10
PROJ_BIG = 11 * 1024
AB_COL = 2 * DN_QK + 2 * DN_V


def _params(n_axes):
    return pltpu.CompilerParams(dimension_semantics=("arbitrary",) * n_axes, vmem_limit_bytes=VMEM_LIMIT)


def _tile(n, target, q=LANE):
    best = None
    for t in range(q, min(n, target) + 1, q):
        if n % t == 0:
            best = t
    return best if best is not None else n


def _dot(a, b):
    return jnp.dot(a.astype(bf16), b.astype(bf16), preferred_element_type=f32)


def _dot_nt(a, b):
    return lax.dot_general(a.astype(bf16), b.astype(bf16), (((1,), (1,)), ((), ())), preferred_element_type=f32)


def _dot_tn(a, b):
    return lax.dot_general(a.astype(bf16), b.astype(bf16), (((0,), (0,)), ((), ())), preferred_element_type=f32)


_HI = lax.Precision.HIGH


def _hdot(a, b):
    return jnp.dot(a, b, precision=_HI, preferred_element_type=f32)


def _hdot_nt(a, b):
    return lax.dot_general(a, b, (((1,), (1,)), ((), ())), precision=_HI, preferred_element_type=f32)


def _hdot_tn(a, b):
    return lax.dot_general(a, b, (((0,), (0,)), ((), ())), precision=_HI, preferred_element_type=f32)


def _sigmoid(x):
    e = jnp.exp(-jnp.abs(x))
    r = 1.0 / (1.0 + e)
    return jnp.where(x >= 0, r, e * r)


def _log1p_small(e):
    return jnp.where(e < 1e-3, e * (1.0 - e * (0.5 - e * (1.0 / 3.0))), jnp.log(1.0 + e))


def _rowsum(x):
    return jnp.sum(x, axis=1, keepdims=True)


def _allsum(x):
    return jnp.sum(jnp.sum(x, axis=1, keepdims=True), axis=0, keepdims=True)


def matmul(a, b, mode, name, residual=None, out_dtype=f32, tm_t=1408, tn_t=1024, tk_t=1408):
    if mode == "nn":
        (M, K), (K2, N) = a.shape, b.shape
    elif mode == "nt":
        (M, K), (N, K2) = a.shape, b.shape
    else:
        (K, M), (K2, N) = a.shape, b.shape
    assert K == K2, (a.shape, b.shape, mode)
    tm, tn, tk = _tile(M, tm_t), _tile(N, tn_t), _tile(K, tk_t)
    nk = K // tk
    if mode == "nn":
        a_spec = pl.BlockSpec((tm, tk), lambda i, j, k: (i, k))
        b_spec = pl.BlockSpec((tk, tn), lambda i, j, k: (k, j))
        dims = (((1,), (0,)), ((), ()))
    elif mode == "nt":
        a_spec = pl.BlockSpec((tm, tk), lambda i, j, k: (i, k))
        b_spec = pl.BlockSpec((tn, tk), lambda i, j, k: (j, k))
        dims = (((1,), (1,)), ((), ()))
    else:
        a_spec = pl.BlockSpec((tk, tm), lambda i, j, k: (k, i))
        b_spec = pl.BlockSpec((tk, tn), lambda i, j, k: (k, j))
        dims = (((0,), (0,)), ((), ()))
    o_spec = pl.BlockSpec((tm, tn), lambda i, j, k: (i, j))
    has_res = residual is not None

    def body(*refs):
        if has_res:
            a_ref, b_ref, r_ref, o_ref, acc_ref = refs
        else:
            a_ref, b_ref, o_ref, acc_ref = refs
        k = pl.program_id(2)

        @pl.when(k == 0)
        def _():
            acc_ref[...] = jnp.zeros_like(acc_ref)

        acc_ref[...] += lax.dot_general(a_ref[...].astype(bf16), b_ref[...].astype(bf16), dims,
                                        preferred_element_type=f32)

        @pl.when(k == nk - 1)
        def _():
            r = acc_ref[...]
            if has_res:
                r = r + r_ref[...]
            o_ref[...] = r.astype(out_dtype)

    ins = [a, b] + ([residual] if has_res else [])
    specs = [a_spec, b_spec] + ([o_spec] if has_res else [])
    return pl.pallas_call(
        body, name=name, grid=(M // tm, N // tn, nk), in_specs=specs, out_specs=o_spec,
        out_shape=jax.ShapeDtypeStruct((M, N), out_dtype),
        scratch_shapes=[pltpu.VMEM((tm, tn), f32)], compiler_params=_params(3),
    )(*ins)


def _row_tile(tp):
    return _tile(tp, 512)


def rms_fwd(h, gain, name):
    tp, d = h.shape
    rt = _row_tile(tp)

    def body(h_ref, g_ref, o_ref):
        x = h_ref[...]
        r = lax.rsqrt(jnp.mean(x * x, axis=-1, keepdims=True) + RMS_EPS)
        o_ref[...] = (x * r * g_ref[...]).astype(bf16)

    return pl.pallas_call(
        body, name=name, grid=(tp // rt,),
        in_specs=[pl.BlockSpec((rt, d), lambda i: (i, 0)), pl.BlockSpec((1, d), lambda i: (0, 0))],
        out_specs=pl.BlockSpec((rt, d), lambda i: (i, 0)),
        out_shape=jax.ShapeDtypeStruct((tp, d), bf16), compiler_params=_params(1),
    )(h, gain)


def rms_bwd(h, gain, dn, dres, name):
    tp, d = h.shape
    rt = _row_tile(tp)

    def body(h_ref, g_ref, dn_ref, dr_ref, dh_ref, dg_ref):
        i = pl.program_id(0)
        x = h_ref[...]
        r = lax.rsqrt(jnp.mean(x * x, axis=-1, keepdims=True) + RMS_EPS)
        xh = x * r
        dn_ = dn_ref[...]
        dxh = dn_ * g_ref[...]
        dh_ref[...] = r * (dxh - xh * jnp.mean(dxh * xh, axis=-1, keepdims=True)) + dr_ref[...]
        part = jnp.sum(dn_ * xh, axis=0, keepdims=True)

        @pl.when(i == 0)
        def _():
            dg_ref[...] = part

        @pl.when(i > 0)
        def _():
            dg_ref[...] += part

    row = pl.BlockSpec((rt, d), lambda i: (i, 0))
    vec = pl.BlockSpec((1, d), lambda i: (0, 0))
    return pl.pallas_call(
        body, name=name, grid=(tp // rt,), in_specs=[row, vec, row, row], out_specs=[row, vec],
        out_shape=[jax.ShapeDtypeStruct((tp, d), f32), jax.ShapeDtypeStruct((1, d), f32)],
        compiler_params=_params(1),
    )(h, gain, dn, dres)


def loss_head(y, target):
    tp, d = y.shape
    rt = P0 + N_META
    assert rt == SB_BLOCK and tp % rt == 0 and target.shape == (tp - rt, d)

    def body(y_ref, t_ref, dy_ref, l_ref):
        i = pl.program_id(0)

        @pl.when(i == 0)
        def _():
            dy_ref[...] = jnp.zeros_like(dy_ref)
            l_ref[...] = jnp.zeros_like(l_ref)

        @pl.when(i > 0)
        def _():
            err = y_ref[...] - t_ref[...]
            dy_ref[...] = err * (1.0 / d)
            l_ref[...] += jnp.broadcast_to(_allsum(err * err) * (0.5 / d), l_ref.shape)

    return pl.pallas_call(
        body, name="loss_head", grid=(tp // rt,),
        in_specs=[pl.BlockSpec((rt, d), lambda i: (i, 0)), pl.BlockSpec((rt, d), lambda i: (jnp.maximum(i - 1, 0), 0))],
        out_specs=[pl.BlockSpec((rt, d), lambda i: (i, 0)), pl.BlockSpec((1, LANE), lambda i: (0, 0))],
        out_shape=[jax.ShapeDtypeStruct((tp, d), f32), jax.ShapeDtypeStruct((1, LANE), f32)],
        compiler_params=_params(1),
    )(y, target)


def swiglu_fwd(u):
    tp = u.shape[0]
    rt, cb = _row_tile(tp), D_FF // 2
    nb = D_FF // cb

    def body(g_ref, u_ref, o_ref):
        g = g_ref[...]
        o_ref[...] = (g * _sigmoid(g) * u_ref[...]).astype(bf16)

    return pl.pallas_call(
        body, name="swiglu_fwd", grid=(tp // rt, nb),
        in_specs=[pl.BlockSpec((rt, cb), lambda i, j: (i, j)), pl.BlockSpec((rt, cb), lambda i, j: (i, j + nb))],
        out_specs=pl.BlockSpec((rt, cb), lambda i, j: (i, j)),
        out_shape=jax.ShapeDtypeStruct((tp, D_FF), bf16), compiler_params=_params(2),
    )(u, u)


def swiglu_bwd(u, dact):
    tp = u.shape[0]
    rt, cb = _row_tile(tp), D_FF // 2
    nb = D_FF // cb

    def body(g_ref, u_ref, da_ref, dg_ref, du_ref):
        g = g_ref[...]
        s = _sigmoid(g)
        da = da_ref[...]
        dg_ref[...] = (da * u_ref[...] * s * (1.0 + g * (1.0 - s))).astype(bf16)
        du_ref[...] = (da * g * s).astype(bf16)

    lo = pl.BlockSpec((rt, cb), lambda i, j: (i, j))
    hi = pl.BlockSpec((rt, cb), lambda i, j: (i, j + nb))
    dgate, dup = pl.pallas_call(
        body, name="swiglu_bwd", grid=(tp // rt, nb), in_specs=[lo, hi, lo], out_specs=[lo, lo],
        out_shape=[jax.ShapeDtypeStruct((tp, D_FF), bf16)] * 2, compiler_params=_params(2),
    )(u, u, dact)
    return dgate, dup


def merge_fwd(proj, ydn, ysb):
    tp = proj.shape[0]
    rt, d = _row_tile(tp), D_MODEL

    def body(gd_ref, gs_ref, yd_ref, ys_ref, o_ref):
        o_ref[...] = (_sigmoid(gd_ref[...]) * yd_ref[...] + _sigmoid(gs_ref[...]) * ys_ref[...]).astype(bf16)

    row = pl.BlockSpec((rt, d), lambda i: (i, 0))
    return pl.pallas_call(
        body, name="merge_fwd", grid=(tp // rt,),
        in_specs=[pl.BlockSpec((rt, d), lambda i: (i, C_GDN)), pl.BlockSpec((rt, d), lambda i: (i, C_GSB)), row, row],
        out_specs=row, out_shape=jax.ShapeDtypeStruct((tp, d), bf16), compiler_params=_params(1),
    )(proj, proj, ydn, ysb)


def merge_bwd(proj, ydn, ysb, dm):
    tp = proj.shape[0]
    rt, d = _row_tile(tp), D_MODEL

    def body(gd_ref, gs_ref, yd_ref, ys_ref, dm_ref, dyd_ref, dys_ref, dgd_ref, dgs_ref):
        dm_ = dm_ref[...]
        sd = _sigmoid(gd_ref[...])
        ss = _sigmoid(gs_ref[...])
        dyd_ref[...] = (dm_ * sd).astype(bf16)
        dys_ref[...] = (dm_ * ss).astype(bf16)
        dgd_ref[...] = (dm_ * yd_ref[...] * sd * (1.0 - sd)).astype(bf16)
        dgs_ref[...] = (dm_ * ys_ref[...] * ss * (1.0 - ss)).astype(bf16)

    row = pl.BlockSpec((rt, d), lambda i: (i, 0))
    return pl.pallas_call(
        body, name="merge_bwd", grid=(tp // rt,),
        in_specs=[pl.BlockSpec((rt, d), lambda i: (i, C_GDN)), pl.BlockSpec((rt, d), lambda i: (i, C_GSB)), row, row, row],
        out_specs=[row] * 4, out_shape=[jax.ShapeDtypeStruct((tp, d), bf16)] * 4, compiler_params=_params(1),
    )(proj, proj, ydn, ysb, dm)


def dn_out_fwd(o, proj, gain):
    tp = o.shape[0]
    rt, cb, wide = _row_tile(tp), DN_DV, 1024
    zb = C_DZ * 1024 // wide

    def body(o_ref, z_ref, g_ref, y_ref):
        for s in range(wide // cb):
            sl = slice(s * cb, (s + 1) * cb)
            x = o_ref[:, sl]
            r = lax.rsqrt(jnp.mean(x * x, axis=-1, keepdims=True) + RMS_EPS)
            z = z_ref[:, sl]
            y_ref[:, sl] = (x * r * g_ref[...] * (z * _sigmoid(z))).astype(bf16)

    blk = pl.BlockSpec((rt, wide), lambda i, j: (i, j))
    return pl.pallas_call(
        body, name="dn_out_fwd", grid=(tp // rt, DN_V // wide),
        in_specs=[blk, pl.BlockSpec((rt, wide), lambda i, j: (i, j + zb)), pl.BlockSpec((1, cb), lambda i, j: (0, 0))],
        out_specs=blk, out_shape=jax.ShapeDtypeStruct((tp, DN_V), bf16), compiler_params=_params(2),
    )(o, proj, gain)


def dn_out_bwd(o, proj, gain, dy):
    tp = o.shape[0]
    rt, cb, wide = _row_tile(tp), DN_DV, 1024
    zb = C_DZ * 1024 // wide

    def body(o_ref, z_ref, g_ref, dy_ref, do_ref, dz_ref, dg_ref):
        i, j = pl.program_id(0), pl.program_id(1)
        g = g_ref[...]
        part = jnp.zeros((1, cb), f32)
        for hh in range(wide // cb):
            sl = slice(hh * cb, (hh + 1) * cb)
            x = o_ref[:, sl]
            r = lax.rsqrt(jnp.mean(x * x, axis=-1, keepdims=True) + RMS_EPS)
            xh = x * r
            z = z_ref[:, sl]
            s = _sigmoid(z)
            dy_ = dy_ref[:, sl]
            drn = dy_ * (z * s)
            dz_ref[:, sl] = (dy_ * xh * g * s * (1.0 + z * (1.0 - s))).astype(bf16)
            dxh = drn * g
            do_ref[:, sl] = r * (dxh - xh * jnp.mean(dxh * xh, axis=-1, keepdims=True))
            part = part + jnp.sum(drn * xh, axis=0, keepdims=True)
        first = jnp.logical_and(i == 0, j == 0)

        @pl.when(first)
        def _():
            dg_ref[...] = part

        @pl.when(jnp.logical_not(first))
        def _():
            dg_ref[...] += part

    blk = pl.BlockSpec((rt, wide), lambda i, j: (i, j))
    vec = pl.BlockSpec((1, cb), lambda i, j: (0, 0))
    return pl.pallas_call(
        body, name="dn_out_bwd", grid=(tp // rt, DN_V // wide),
        in_specs=[blk, pl.BlockSpec((rt, wide), lambda i, j: (i, j + zb)), vec, blk],
        out_specs=[blk, blk, vec],
        out_shape=[jax.ShapeDtypeStruct((tp, DN_V), f32), jax.ShapeDtypeStruct((tp, DN_V), bf16),
                   jax.ShapeDtypeStruct((1, cb), f32)],
        compiler_params=_params(2),
    )(o, proj, gain, dy)


def sb_prep_fwd(proj, gq, gk):
    tp = proj.shape[0]
    rt, cb = _row_tile(tp), SB_DH

    def body(q_ref, k_ref, v_ref, gq_ref, gk_ref, qo_ref, ko_ref, vo_ref):
        for x_ref, g_ref, o_ref in ((q_ref, gq_ref, qo_ref), (k_ref, gk_ref, ko_ref)):
            for h in range(HEADS):
                sl = slice(h * cb, (h + 1) * cb)
                x = x_ref[:, sl]
                r = lax.rsqrt(jnp.mean(x * x, axis=-1, keepdims=True) + RMS_EPS)
                o_ref[:, sl] = (x * r * g_ref[...]).astype(bf16)
        vo_ref[...] = v_ref[...].astype(bf16)

    blk = pl.BlockSpec((rt, SB_W), lambda i: (i, 0))
    vec = pl.BlockSpec((1, cb), lambda i: (0, 0))
    return pl.pallas_call(
        body, name="sb_prep_fwd", grid=(tp // rt,),
        in_specs=[pl.BlockSpec((rt, SB_W), lambda i: (i, C_SQ)), pl.BlockSpec((rt, SB_W), lambda i: (i, C_SK)),
                  pl.BlockSpec((rt, SB_W), lambda i: (i, C_SV)), vec, vec],
        out_specs=[blk] * 3, out_shape=[jax.ShapeDtypeStruct((tp, SB_W), bf16)] * 3, compiler_params=_params(1),
    )(proj, proj, proj, gq, gk)


def sb_prep_bwd(proj, gq, gk, dqs, dks):
    tp = proj.shape[0]
    rt, cb = _row_tile(tp), SB_DH

    def body(q_ref, k_ref, gq_ref, gk_ref, dq_ref, dk_ref, dqo_ref, dko_ref, dgq_ref, dgk_ref):
        first = pl.program_id(0) == 0
        for x_ref, g_ref, dn_ref, dx_ref, dg_ref in ((q_ref, gq_ref, dq_ref, dqo_ref, dgq_ref),
                                                     (k_ref, gk_ref, dk_ref, dko_ref, dgk_ref)):
            part = jnp.zeros((1, cb), f32)
            for h in range(HEADS):
                sl = slice(h * cb, (h + 1) * cb)
                x = x_ref[:, sl]
                r = lax.rsqrt(jnp.mean(x * x, axis=-1, keepdims=True) + RMS_EPS)
                xh = x * r
                dn_ = dn_ref[:, sl]
                dxh = dn_ * g_ref[...]
                dx_ref[:, sl] = (r * (dxh - xh * jnp.mean(dxh * xh, axis=-1, keepdims=True))).astype(bf16)
                part = part + jnp.sum(dn_ * xh, axis=0, keepdims=True)

            @pl.when(first)
            def _(dg_ref=dg_ref, part=part):
                dg_ref[...] = part

            @pl.when(jnp.logical_not(first))
            def _(dg_ref=dg_ref, part=part):
                dg_ref[...] += part

    blk = pl.BlockSpec((rt, SB_W), lambda i: (i, 0))
    vec = pl.BlockSpec((1, cb), lambda i: (0, 0))
    return pl.pallas_call(
        body, name="sb_prep_bwd", grid=(tp // rt,),
        in_specs=[pl.BlockSpec((rt, SB_W), lambda i: (i, C_SQ)), pl.BlockSpec((rt, SB_W), lambda i: (i, C_SK)),
                  vec, vec, blk, blk],
        out_specs=[blk, blk, vec, vec],
        out_shape=[jax.ShapeDtypeStruct((tp, SB_W), bf16)] * 2 + [jax.ShapeDtypeStruct((1, cb), f32)] * 2,
        compiler_params=_params(1),
    )(proj, proj, gq, gk, dqs, dks)


def cast_bf16(x, name):
    tp, d = x.shape
    rt = _row_tile(tp)
    blk = pl.BlockSpec((rt, d), lambda i: (i, 0))

    def body(x_ref, o_ref):
        o_ref[...] = x_ref[...].astype(bf16)

    return pl.pallas_call(body, name=name, grid=(tp // rt,), in_specs=[blk], out_specs=blk,
                          out_shape=jax.ShapeDtypeStruct((tp, d), bf16), compiler_params=_params(1))(x)


def _conv_taps(ext, rt):
    taps = []
    for k in range(DN_CONV):
        s = DN_CONV - 1 - k
        taps.append((pltpu.roll(ext, s, axis=0) if s else ext)[SUB:SUB + rt])
    return taps


def _conv_act(taps, w, l2):
    y = taps[0] * w[0:1]
    for k in range(1, DN_CONV):
        y = y + taps[k] * w[k:k + 1]
    s = _sigmoid(y)
    a = y * s
    if l2:
        n = lax.rsqrt(jnp.sum(a * a, axis=-1, keepdims=True) + L2_EPS)
        return y, s, a, n
    return y, s, a, None


def conv_fwd(proj, w8, col_blk, ncb, l2, name):
    tp = proj.shape[0]
    rt = _row_tile(tp)
    hb = rt // SUB
    cw = CONV_W
    cb0 = col_blk * LANE // cw

    def body(x_ref, h_ref, w_ref, o_ref):
        i = pl.program_id(1)
        first = (i > 0).astype(f32)
        for s in range(cw // LANE):
            sl = slice(s * LANE, (s + 1) * LANE)
            ext = jnp.concatenate([h_ref[:, sl] * first, x_ref[:, sl]], axis=0)
            _, _, a, n = _conv_act(_conv_taps(ext, rt), w_ref[:, sl], l2)
            o_ref[:, sl] = a * n if l2 else a

    return pl.pallas_call(
        body, name=name, grid=(ncb * LANE // cw, tp // rt),
        in_specs=[pl.BlockSpec((rt, cw), lambda j, i: (i, j + cb0)),
                  pl.BlockSpec((SUB, cw), lambda j, i: (jnp.maximum(i * hb - 1, 0), j + cb0)),
                  pl.BlockSpec((SUB, cw), lambda j, i: (0, j))],
        out_specs=pl.BlockSpec((rt, cw), lambda j, i: (i, j)),
        out_shape=jax.ShapeDtypeStruct((tp, ncb * LANE), f32), compiler_params=_params(2),
    )(proj, proj, w8)


def conv_bwd_act(proj, w8, dout, col_blk, ncb, l2, name):
    tp = proj.shape[0]
    rt = _row_tile(tp)
    hb = rt // SUB
    cw = CONV_W
    cb0 = col_blk * LANE // cw

    def body(x_ref, h_ref, w_ref, d_ref, dy_ref, dw_ref):
        i = pl.program_id(1)
        first = (i > 0).astype(f32)
        rows = lax.broadcasted_iota(jnp.int32, (SUB, LANE), 0)
        for s in range(cw // LANE):
            sl = slice(s * LANE, (s + 1) * LANE)
            ext = jnp.concatenate([h_ref[:, sl] * first, x_ref[:, sl]], axis=0)
            taps = _conv_taps(ext, rt)
            y, sg, a, n = _conv_act(taps, w_ref[:, sl], l2)
            da = d_ref[:, sl]
            if l2:
                out = a * n
                da = n * (da - out * jnp.sum(da * out, axis=-1, keepdims=True))
            dy = da * sg * (1.0 + y * (1.0 - sg))
            dy_ref[:, sl] = dy
            part = jnp.zeros((SUB, LANE), f32)
            for k in range(DN_CONV):
                part = part + jnp.where(rows == k, jnp.sum(taps[k] * dy, axis=0, keepdims=True), 0.0)

            @pl.when(i == 0)
            def _(sl=sl, part=part):
                dw_ref[:, sl] = part

            @pl.when(i > 0)
            def _(sl=sl, part=part):
                dw_ref[:, sl] += part

    return pl.pallas_call(
        body, name=name, grid=(ncb * LANE // cw, tp // rt),
        in_specs=[pl.BlockSpec((rt, cw), lambda j, i: (i, j + cb0)),
                  pl.BlockSpec((SUB, cw), lambda j, i: (jnp.maximum(i * hb - 1, 0), j + cb0)),
                  pl.BlockSpec((SUB, cw), lambda j, i: (0, j)),
                  pl.BlockSpec((rt, cw), lambda j, i: (i, j))],
        out_specs=[pl.BlockSpec((rt, cw), lambda j, i: (i, j)), pl.BlockSpec((SUB, cw), lambda j, i: (0, j))],
        out_shape=[jax.ShapeDtypeStruct((tp, ncb * LANE), f32), jax.ShapeDtypeStruct((SUB, ncb * LANE), f32)],
        compiler_params=_params(2),
    )(proj, proj, w8, dout)


def conv_bwd_in(dy, w8, name):
    tp, cols = dy.shape
    rt = _row_tile(tp)
    hb = rt // SUB
    nr = tp // rt
    last8 = tp // SUB - 1
    cw = CONV_W

    def body(d_ref, h_ref, w_ref, o_ref):
        i = pl.program_id(1)
        last = (i < nr - 1).astype(f32)
        for c0 in range(cw // LANE):
            sl = slice(c0 * LANE, (c0 + 1) * LANE)
            ext = jnp.concatenate([d_ref[:, sl], h_ref[:, sl] * last], axis=0)
            w = w_ref[:, sl]
            acc = None
            for k in range(DN_CONV):
                s = DN_CONV - 1 - k
                sh = (pltpu.roll(ext, rt + SUB - s, axis=0) if s else ext)[0:rt]
                term = sh * w[k:k + 1]
                acc = term if acc is None else acc + term
            o_ref[:, sl] = acc.astype(bf16)

    return pl.pallas_call(
        body, name=name, grid=(cols // cw, nr),
        in_specs=[pl.BlockSpec((rt, cw), lambda j, i: (i, j)),
                  pl.BlockSpec((SUB, cw), lambda j, i: (jnp.minimum((i + 1) * hb, last8), j)),
                  pl.BlockSpec((SUB, cw), lambda j, i: (0, j))],
        out_specs=pl.BlockSpec((rt, cw), lambda j, i: (i, j)),
        out_shape=jax.ShapeDtypeStruct((tp, cols), bf16), compiler_params=_params(2),
    )(dy, dy, w8)


def _ab_common(p, al, dtb, r0):
    rows = r0 + lax.broadcasted_iota(jnp.int32, p.shape, 0)
    mask = (rows >= P0).astype(f32)
    xx = p + dtb
    sp = jnp.maximum(xx, 0.0) + _log1p_small(jnp.exp(-jnp.abs(xx)))
    ea = jnp.exp(al)
    g = -ea * sp * mask
    beta = _sigmoid(p) * mask
    return g, beta, _sigmoid(xx), ea, mask


def ab_fwd(pab, al, dtb):
    tp = pab.shape[0]
    rt = _row_tile(tp)

    def body(p_ref, al_ref, dt_ref, g_ref, b_ref):
        i = pl.program_id(0)
        g, beta, _, _, _ = _ab_common(p_ref[...], al_ref[...], dt_ref[...], i * rt)
        for h in range(HEADS):
            g_ref[h] = jnp.broadcast_to(g[:, h:h + 1], (rt, LANE))
            b_ref[h] = jnp.broadcast_to(beta[:, HEADS + h:HEADS + h + 1], (rt, LANE))

    vec = pl.BlockSpec((1, LANE), lambda i: (0, 0))
    out = pl.BlockSpec((HEADS, rt, LANE), lambda i: (0, i, 0))
    return pl.pallas_call(
        body, name="ab_fwd", grid=(tp // rt,), in_specs=[pl.BlockSpec((rt, LANE), lambda i: (i, 0)), vec, vec],
        out_specs=[out, out], out_shape=[jax.ShapeDtypeStruct((HEADS, tp, LANE), f32)] * 2, compiler_params=_params(1),
    )(pab, al, dtb)


def ab_bwd(pab, al, dtb, dg, db):
    tp = pab.shape[0]
    rt = _row_tile(tp)

    def body(p_ref, al_ref, dt_ref, dg_ref, db_ref, dp_ref, dal_ref, ddt_ref):
        i = pl.program_id(0)
        g, beta, sx, ea, mask = _ab_common(p_ref[...], al_ref[...], dt_ref[...], i * rt)
        lanes = lax.broadcasted_iota(jnp.int32, (rt, LANE), 1)
        dgl = jnp.zeros((rt, LANE), f32)
        dbl = jnp.zeros((rt, LANE), f32)
        for h in range(HEADS):
            dgl = dgl + jnp.where(lanes == h, dg_ref[h], 0.0)
            dbl = dbl + jnp.where(lanes == HEADS + h, db_ref[h], 0.0)
        dxx = dgl * (-ea) * sx * mask
        dp_ref[...] = (dxx + dbl * beta * (1.0 - beta)).astype(bf16)
        pal = jnp.sum(dgl * g, axis=0, keepdims=True)
        pdt = jnp.sum(dxx, axis=0, keepdims=True)

        @pl.when(i == 0)
        def _():
            dal_ref[...] = pal
            ddt_ref[...] = pdt

        @pl.when(i > 0)
        def _():
            dal_ref[...] += pal
            ddt_ref[...] += pdt

    vec = pl.BlockSpec((1, LANE), lambda i: (0, 0))
    row = pl.BlockSpec((rt, LANE), lambda i: (i, 0))
    big = pl.BlockSpec((HEADS, rt, LANE), lambda i: (0, i, 0))
    return pl.pallas_call(
        body, name="ab_bwd", grid=(tp // rt,), in_specs=[row, vec, vec, big, big], out_specs=[row, vec, vec],
        out_shape=[jax.ShapeDtypeStruct((tp, LANE), bf16), jax.ShapeDtypeStruct((1, LANE), f32),
                   jax.ShapeDtypeStruct((1, LANE), f32)],
        compiler_params=_params(1),
    )(pab, al, dtb, dg, db)


class _Chunk:
    pass


def _gdn_chunk(q, k, v, gcol, bcol, grow8):
    C = CHUNK
    R = range(len(q))
    X = _Chunk()
    ri = lax.broadcasted_iota(jnp.int32, (C, C), 0)
    ci = lax.broadcasted_iota(jnp.int32, (C, C), 1)
    r2 = lax.broadcasted_iota(jnp.int32, (LANE, LANE), 0)
    c2 = lax.broadcasted_iota(jnp.int32, (LANE, LANE), 1)
    lower = (ri >= ci).astype(f32)
    upper2 = (r2 <= c2).astype(f32)
    eye = (ri == ci).astype(f32)
    gam = [_hdot(lower, gcol[h]) for h in R]
    gam_row = [_hdot(grow8[h], upper2)[0:1, 0:C] for h in R]
    X.ri, X.ci = ri, ci
    X.Dm = [jnp.where(ri >= ci, jnp.exp(jnp.minimum(gam[h][:, 0:C] - gam_row[h], 0.0)), 0.0) for h in R]
    X.eg = [jnp.exp(gam[h]) for h in R]
    gl = [gam[h][C - 1:C, :] for h in R]
    X.egl = [jnp.exp(gl[h]) for h in R]
    X.kdec = [jnp.exp(gl[h] - gam[h]) for h in R]
    X.qs = [q[h] * (DN_DK ** -0.5) for h in R]
    X.kb = [k[h] * bcol[h] for h in R]
    kk = [_dot_nt(X.kb[h], k[h]) for h in R]
    qk = [_dot_nt(X.qs[h], k[h]) for h in R]
    X.A = [jnp.where(ri > ci, kk[h] * X.Dm[h], 0.0) for h in R]
    T = [eye - X.A[h] for h in R]
    P = list(X.A)
    for _ in range(5):
        P = [_hdot(P[h], P[h]) for h in R]
        T = [T[h] + _hdot(T[h], P[h]) for h in R]
    X.T = T
    X.b2 = [jnp.concatenate([bcol[h], bcol[h]], axis=-1) for h in R]
    X.u = [_hdot(T[h], v[h] * X.b2[h]) for h in R]
    X.w = [_hdot(T[h], X.kb[h] * X.eg[h]) for h in R]
    X.attn = [qk[h] * X.Dm[h] for h in R]
    X.qg = [X.qs[h] * X.eg[h] for h in R]
    X.kg = [k[h] * X.kdec[h] for h in R]
    return X


def gdn_fwd(q, k, v, gc, bc, grow):
    tp = q.shape[0]
    nc = tp // CHUNK
    hb = GDN_HEADS_PER_STEP

    def body(q_ref, k_ref, v_ref, gc_ref, bc_ref, gr_ref, o_ref, ss_ref, S_ref):
        c = pl.program_id(1)

        @pl.when(c == 0)
        def _():
            S_ref[...] = jnp.zeros_like(S_ref)

        R = range(hb)
        qc = [slice(h * DN_DK, (h + 1) * DN_DK) for h in R]
        vc = [slice(h * DN_DV, (h + 1) * DN_DV) for h in R]
        X = _gdn_chunk([q_ref[:, qc[h]] for h in R], [k_ref[:, qc[h]] for h in R], [v_ref[:, vc[h]] for h in R],
                       [gc_ref[h] for h in R], [bc_ref[h] for h in R], [gr_ref[h] for h in R])
        S = [S_ref[h] for h in R]
        for h in R:
            ss_ref[h, 0] = S[h]
        wS = [_dot(X.w[h], S[h]) for h in R]
        qS = [_dot(X.qg[h], S[h]) for h in R]
        vn = [X.u[h] - wS[h] for h in R]
        av = [_dot(X.attn[h], vn[h]) for h in R]
        kv = [_dot_tn(X.kg[h], vn[h]) for h in R]
        for h in R:
            o_ref[:, vc[h]] = qS[h] + av[h]
            S_ref[h] = S[h] * X.egl[h][:, 0:1] + kv[h]

    qk = pl.BlockSpec((CHUNK, hb * DN_DK), lambda g, c: (c, g))
    vv = pl.BlockSpec((CHUNK, hb * DN_DV), lambda g, c: (c, g))
    col = pl.BlockSpec((hb, CHUNK, LANE), lambda g, c: (g, c, 0))
    row = pl.BlockSpec((hb, SUB, LANE), lambda g, c: (g, c, 0))
    return pl.pallas_call(
        body, name="gdn_fwd", grid=(HEADS // hb, nc), in_specs=[qk, qk, vv, col, col, row],
        out_specs=[vv, pl.BlockSpec((hb, 1, DN_DK, DN_DV), lambda g, c: (g, c, 0, 0))],
        out_shape=[jax.ShapeDtypeStruct((tp, DN_V), f32), jax.ShapeDtypeStruct((HEADS, nc, DN_DK, DN_DV), f32)],
        scratch_shapes=[pltpu.VMEM((hb, DN_DK, DN_DV), f32)], compiler_params=_params(2),
    )(q, k, v, gc, bc, grow)


def gdn_bwd(q, k, v, gc, bc, grow, states, do):
    tp = q.shape[0]
    nc = tp // CHUNK
    C = CHUNK
    hb = GDN_HEADS_PER_STEP

    def body(q_ref, k_ref, v_ref, gc_ref, bc_ref, gr_ref, ss_ref, do_ref, dq_ref, dk_ref, dv_ref, dg_ref, db_ref, dS_ref):
        c = pl.program_id(1)

        @pl.when(c == 0)
        def _():
            dS_ref[...] = jnp.zeros_like(dS_ref)

        R = range(hb)
        qc = [slice(h * DN_DK, (h + 1) * DN_DK) for h in R]
        vc = [slice(h * DN_DV, (h + 1) * DN_DV) for h in R]
        k_ = [k_ref[:, qc[h]] for h in R]
        v_ = [v_ref[:, vc[h]] for h in R]
        bcol = [bc_ref[h] for h in R]
        X = _gdn_chunk([q_ref[:, qc[h]] for h in R], k_, v_, [gc_ref[h] for h in R], bcol, [gr_ref[h] for h in R])
        ri, ci = X.ri, X.ci
        S = [ss_ref[h, 0] for h in R]
        do_ = [do_ref[:, vc[h]] for h in R]
        dSn = [dS_ref[h] for h in R]
        wS = [_dot(X.w[h], S[h]) for h in R]
        ado = [_dot_tn(X.attn[h], do_[h]) for h in R]
        kdS = [_dot(X.kg[h], dSn[h]) for h in R]
        d_qg = [_dot_nt(do_[h], S[h]) for h in R]
        qdo = [_dot_tn(X.qg[h], do_[h]) for h in R]
        vn = [X.u[h] - wS[h] for h in R]
        d_vn = [ado[h] + kdS[h] for h in R]
        dovn = [_dot_nt(do_[h], vn[h]) for h in R]
        d_kg = [_dot_nt(vn[h], dSn[h]) for h in R]
        wdv = [_dot_tn(X.w[h], d_vn[h]) for h in R]
        dw = [-_dot_nt(d_vn[h], S[h]) for h in R]
        for h in R:
            dS_ref[h] = qdo[h] + X.egl[h][:, 0:1] * dSn[h] - wdv[h]
        dattn = [jnp.where(ri >= ci, dovn[h], 0.0) for h in R]
        dRu = [_hdot_tn(X.T[h], d_vn[h]) for h in R]
        dRw = [_hdot_tn(X.T[h], dw[h]) for h in R]
        dAu = [_hdot_nt(dRu[h], X.u[h]) for h in R]
        dAw = [_hdot_nt(dRw[h], X.w[h]) for h in R]
        dA = [jnp.where(ri > ci, -(dAu[h] + dAw[h]), 0.0) for h in R]
        dKK = [dA[h] * X.Dm[h] for h in R]
        dQK = [dattn[h] * X.Dm[h] for h in R]
        E = [dA[h] * X.A[h] + dattn[h] * X.attn[h] for h in R]
        dkb = [_dot(dKK[h], k_[h]) + dRw[h] * X.eg[h] for h in R]
        dk1 = [_dot_tn(dKK[h], X.kb[h]) for h in R]
        dqs = [_dot(dQK[h], k_[h]) + d_qg[h] * X.eg[h] for h in R]
        dk2 = [_dot_tn(dQK[h], X.qs[h]) for h in R]
        ones = jnp.ones((C, LANE), f32)
        colE = [_hdot_tn(E[h], ones) for h in R]
        rows = lax.broadcasted_iota(jnp.int32, (C, LANE), 0)
        upper = (ci >= ri).astype(f32)
        dgam = []
        for h in R:
            t = d_kg[h] * X.kg[h]
            dgl = _allsum(t) + X.egl[h][:, 0:1] * _allsum(S[h] * dSn[h])
            g = (_rowsum(E[h]) - colE[h] + _rowsum(dRw[h] * (X.kb[h] * X.eg[h])) + _rowsum(d_qg[h] * X.qg[h])
                 - _rowsum(t))
            dgam.append(g + jnp.where(rows == C - 1, dgl, 0.0))
        dg = [_hdot(upper, dgam[h]) for h in R]
        for h in R:
            dv_ref[:, vc[h]] = dRu[h] * X.b2[h]
            dbeta = _rowsum(dRu[h] * v_[h]) + _rowsum(dkb[h] * k_[h])
            dq_ref[:, qc[h]] = dqs[h] * (DN_DK ** -0.5)
            dk_ref[:, qc[h]] = dk1[h] + dk2[h] + dkb[h] * bcol[h] + d_kg[h] * X.kdec[h]
            dg_ref[h] = dg[h]
            db_ref[h] = jnp.broadcast_to(dbeta, (C, LANE))

    rc = lambda c: nc - 1 - c
    qk = pl.BlockSpec((CHUNK, hb * DN_DK), lambda g, c: (rc(c), g))
    vv = pl.BlockSpec((CHUNK, hb * DN_DV), lambda g, c: (rc(c), g))
    col = pl.BlockSpec((hb, CHUNK, LANE), lambda g, c: (g, rc(c), 0))
    row = pl.BlockSpec((hb, SUB, LANE), lambda g, c: (g, rc(c), 0))
    st = pl.BlockSpec((hb, 1, DN_DK, DN_DV), lambda g, c: (g, rc(c), 0, 0))
    return pl.pallas_call(
        body, name="gdn_bwd", grid=(HEADS // hb, nc), in_specs=[qk, qk, vv, col, col, row, st, vv],
        out_specs=[qk, qk, vv, col, col],
        out_shape=[jax.ShapeDtypeStruct((tp, DN_QK), f32), jax.ShapeDtypeStruct((tp, DN_QK), f32),
                   jax.ShapeDtypeStruct((tp, DN_V), f32), jax.ShapeDtypeStruct((HEADS, tp, LANE), f32),
                   jax.ShapeDtypeStruct((HEADS, tp, LANE), f32)],
        scratch_shapes=[pltpu.VMEM((hb, DN_DK, DN_DV), f32)], compiler_params=_params(2),
    )(q, k, v, gc, bc, grow, states, do)


def _cumsum_after(x, nb, us):
    B = SB_BLOCK
    hi = x.astype(bf16)
    lo = (x - hi.astype(f32)).astype(bf16)
    rows = [p[:, b * B:(b + 1) * B] for p in (hi, lo) for b in range(nb)]
    r = jnp.dot(jnp.concatenate(rows, axis=0), us, preferred_element_type=f32)
    out = [r[b * B:(b + 1) * B] + r[(nb + b) * B:(nb + b + 1) * B] for b in range(nb)]
    return out[0] if nb == 1 else jnp.concatenate(out, axis=1)


def _later_blocks(x, nb, carry):
    B = SB_BLOCK
    tot = [_rowsum(x[:, b * B:(b + 1) * B]) for b in range(nb)]
    offs = [None] * nb
    run = carry
    for b in range(nb - 1, -1, -1):
        offs[b] = jnp.broadcast_to(run, (B, B))
        run = run + tot[b]
    return (offs[0] if nb == 1 else jnp.concatenate(offs, axis=1)), run


def _sb_group(i, t):
    top = i - SB_GROUP * t
    jlo = jnp.maximum(top - SB_GROUP + 1, 0)
    rows = pl.ds(pl.multiple_of(jlo * SB_BLOCK, SB_BLOCK), SB_GROUP * SB_BLOCK)
    return jlo, rows, (top + 1) * SB_BLOCK


def _sb_weights(q, kcat, i, jlo, kend, cs, us, masked):
    B, nb = SB_BLOCK, SB_GROUP
    R = range(len(q))
    z = [_dot_nt(q[h], kcat[h]) * (SB_DH ** -0.5) for h in R]
    e = [jnp.exp(-jnp.abs(z[h])) for h in R]
    l1p = [jnp.log(1.0 + e[h]) for h in R]
    lsp = [jnp.minimum(z[h], 0.0) - l1p[h] for h in R]
    lk = [-jnp.maximum(z[h], 0.0) - l1p[h] for h in R]
    vis = None
    if masked:
        qpos = i * B + lax.broadcasted_iota(jnp.int32, (B, nb * B), 0)
        kpos = jlo * B + lax.broadcasted_iota(jnp.int32, (B, nb * B), 1)
        vis = jnp.logical_and(kpos < jnp.minimum(qpos, kend), kpos >= P0)
        lk = [jnp.where(vis, lk[h], 0.0) for h in R]
    later = [_later_blocks(lk[h], nb, cs[h]) for h in R]
    cum = [_cumsum_after(lk[h], nb, us) for h in R]
    w = [jnp.exp(lsp[h] + cum[h] + later[h][0]) for h in R]
    if masked:
        w = [jnp.where(vis, w[h], 0.0) for h in R]
    return z, e, vis, w, [later[h][1] for h in R]


def _sb_loop(i, step, carry):
    trips = (i + SB_GROUP) // SB_GROUP
    carry = step(True)(0, carry)
    carry = lax.fori_loop(1, trips - 1, step(False), carry)
    return lax.fori_loop(jnp.maximum(trips - 1, 1), trips, step(True), carry)


def _ride(rider, n_in, n_out, grid):
    n_rin = len(rider.ins) if rider else 0
    n_rout = len(rider.out_shapes) if rider else 0

    def split(refs):
        ins, rin = refs[:n_in], refs[n_in:n_in + n_rin]
        outs = refs[n_in + n_rin:n_in + n_rin + n_out]
        rout = refs[n_in + n_rin + n_out:n_in + n_rin + n_out + n_rout]
        return ins, outs, (rin, rout, refs[n_in + n_rin + n_out + n_rout:])

    def at(step, fn, r):
        if rider is None:
            return
        cond = None
        for a, g in enumerate(grid):
            c = pl.program_id(a) == (g - 1 if step == "last" else 0)
            cond = c if cond is None else jnp.logical_and(cond, c)

        @pl.when(cond)
        def _():
            fn(*r)

    first = lambda r: at("first", rider.start if rider else None, r)
    last = lambda r: at("last", rider.finish if rider else None, r)
    return split, first, last


def sb_fwd(qs, ks, vs, rider=None):
    tp = qs.shape[0]
    nq = tp // SB_BLOCK
    B, G, hb = SB_BLOCK, SB_GROUP, SB_HEADS_PER_STEP
    assert tp >= G * B
    grid = (HEADS // hb, nq)
    split, ride_first, ride_last = _ride(rider, 3, 1, grid)

    def body(*refs):
        (q_ref, k_ref, v_ref), (o_ref,), ride = split(refs)
        ride_first(ride)
        i = pl.program_id(1)
        R = range(hb)
        hs = [slice(h * SB_DH, (h + 1) * SB_DH) for h in R]
        q = [q_ref[:, hs[h]] for h in R]
        us = (lax.broadcasted_iota(jnp.int32, (B, B), 0) > lax.broadcasted_iota(jnp.int32, (B, B), 1)).astype(bf16)

        def make_step(masked):
            def step(t, carry):
                acc, cs = carry
                jlo, rows, kend = _sb_group(i, t)
                _, _, _, w, cs = _sb_weights(q, [k_ref[rows, hs[h]] for h in R], i, jlo, kend, cs, us, masked)
                pv = [_dot(w[h], v_ref[rows, hs[h]]) for h in R]
                return tuple(acc[h] + pv[h] for h in R), tuple(cs)
            return step

        carry = (tuple(jnp.zeros((B, SB_DH), f32) for _ in R), tuple(jnp.zeros((B, 1), f32) for _ in R))
        acc, _ = _sb_loop(i, make_step, carry)
        for h in R:
            o_ref[:, hs[h]] = acc[h]
        ride_last(ride)

    blk = pl.BlockSpec((B, hb * SB_DH), lambda g, i: (i, g))
    full = pl.BlockSpec((tp, hb * SB_DH), lambda g, i: (0, g))
    r_ins = rider.ins if rider else []
    r_outs = rider.out_shapes if rider else []
    res = pl.pallas_call(
        body, name="sb_fwd", grid=grid, in_specs=[blk, full, full] + [_ANY] * len(r_ins),
        out_specs=[blk] + [_ANY] * len(r_outs),
        out_shape=[jax.ShapeDtypeStruct((tp, SB_W), f32)] + list(r_outs),
        scratch_shapes=rider.scratch if rider else [], compiler_params=_params(2),
    )(qs, ks, vs, *r_ins)
    return res[0], res[1:]


def sb_bwd(qs, ks, vs, o, do, rider=None):
    tp = qs.shape[0]
    nq = tp // SB_BLOCK
    B, G, hb = SB_BLOCK, SB_GROUP, SB_HEADS_PER_STEP
    assert tp >= G * B
    grid = (HEADS // hb, nq)
    split, ride_first, ride_last = _ride(rider, 5, 3, grid)

    def body(*refs):
        (q_ref, k_ref, v_ref, o_ref, do_ref), (dq_ref, dk_ref, dv_ref), ride = split(refs)
        ride_first(ride)
        i = pl.program_id(1)

        @pl.when(i == 0)
        def _():
            dk_ref[...] = jnp.zeros_like(dk_ref)
            dv_ref[...] = jnp.zeros_like(dv_ref)

        R = range(hb)
        hs = [slice(h * SB_DH, (h + 1) * SB_DH) for h in R]
        q = [q_ref[:, hs[h]] for h in R]
        dob = [do_ref[:, hs[h]].astype(bf16) for h in R]
        et = [_rowsum(dob[h].astype(f32) * o_ref[:, hs[h]]) for h in R]
        us = (lax.broadcasted_iota(jnp.int32, (B, B), 0) > lax.broadcasted_iota(jnp.int32, (B, B), 1)).astype(bf16)

        def make_step(masked):
            def step(t, carry):
                dq, cs, ce = carry
                jlo, rows, kend = _sb_group(i, t)
                kcat = [k_ref[rows, hs[h]] for h in R]
                dwv = [_dot_nt(dob[h], v_ref[rows, hs[h]]) for h in R]
                z, e, vis, w, cs = _sb_weights(q, kcat, i, jlo, kend, cs, us, masked)
                wb = [w[h].astype(bf16) for h in R]
                ee = [dwv[h] * wb[h].astype(f32) for h in R]
                later = [_later_blocks(ee[h], G, ce[h]) for h in R]
                cum = [_cumsum_after(ee[h], G, us) for h in R]
                dz = []
                for h in R:
                    f = et[h] - (ee[h] + cum[h] + later[h][0])
                    r = 1.0 / (1.0 + e[h])
                    pos = z[h] >= 0
                    sp = jnp.where(pos, r, e[h] * r)
                    sn = jnp.where(pos, e[h] * r, r)
                    d = ee[h] * sn - f * sp
                    if masked:
                        d = jnp.where(vis, d, 0.0)
                    dz.append((d * (SB_DH ** -0.5)).astype(bf16))
                dkj = [_dot_tn(dz[h], q[h]) for h in R]
                dvj = [_dot_tn(wb[h], dob[h]) for h in R]
                dqj = [_dot(dz[h], kcat[h]) for h in R]
                for h in R:
                    dk_ref[rows, hs[h]] += dkj[h]
                    dv_ref[rows, hs[h]] += dvj[h]
                return tuple(dq[h] + dqj[h] for h in R), tuple(cs), tuple(later[h][1] for h in R)
            return step

        z0 = tuple(jnp.zeros((B, 1), f32) for _ in R)
        dq, _, _ = _sb_loop(i, make_step, (tuple(jnp.zeros((B, SB_DH), f32) for _ in R), z0, z0))
        for h in R:
            dq_ref[:, hs[h]] = dq[h]
        ride_last(ride)

    blk = pl.BlockSpec((B, hb * SB_DH), lambda g, i: (i, g))
    full = pl.BlockSpec((tp, hb * SB_DH), lambda g, i: (0, g))
    r_ins = rider.ins if rider else []
    r_outs = rider.out_shapes if rider else []
    res = pl.pallas_call(
        body, name="sb_bwd", grid=grid, in_specs=[blk, full, full, blk, blk] + [_ANY] * len(r_ins),
        out_specs=[blk, full, full] + [_ANY] * len(r_outs),
        out_shape=[jax.ShapeDtypeStruct((tp, SB_W), f32)] * 3 + list(r_outs),
        scratch_shapes=rider.scratch if rider else [], compiler_params=_params(2),
    )(qs, ks, vs, o, do, *r_ins)
    return res[:3], res[3:]


def adamw(w, g, m, v, name):
    r, c = w.shape
    rt = _tile(r, 128, SUB) if r % SUB == 0 else r
    blk = pl.BlockSpec((rt, c), lambda i: (i, 0))
    c1 =1.0 - ADAM_B1 ** ADAM_STEP
    c2 = 1.0 - ADAM_B2 ** ADAM_STEP

    def body(w_ref, g_ref, m_ref, v_ref, d_ref, mo_ref, vo_ref):
        g_ = g_ref[...]
        m_ = ADAM_B1 * m_ref[...] + (1.0 - ADAM_B1) * g_
        v_ = ADAM_B2 * v_ref[...] + (1.0 - ADAM_B2) * (g_ * g_)
        mo_ref[...] = m_
        vo_ref[...] = v_
        d_ref[...] = -ADAM_LR * ((m_ / c1) / (jnp.sqrt(v_ / c2) + ADAM_EPS) + ADAM_WD * w_ref[...])

    return pl.pallas_call(
        body, name=name, grid=(r // rt,), in_specs=[blk] * 4, out_specs=[blk] * 3,
        out_shape=[jax.ShapeDtypeStruct((r, c), f32)] * 3, compiler_params=_params(1),
    )(w, g, m, v)


def sum_slots(x, name):
    n, r, c = x.shape
    rt = _tile(r, 128, SUB) if r % SUB == 0 else r
    blk = pl.BlockSpec((n, rt, c), lambda i: (0, i, 0))

    def body(x_ref, o_ref):
        acc = x_ref[0]
        for s in range(1, n):
            acc = acc + x_ref[s]
        o_ref[...] = acc

    return pl.pallas_call(
        body, name=name, grid=(r // rt,), in_specs=[blk], out_specs=pl.BlockSpec((rt, c), lambda i: (i, 0)),
        out_shape=jax.ShapeDtypeStruct((r, c), f32), compiler_params=_params(1),
    )(x)


def add2(a, b, name):
    n, r, c = a.shape
    rt = _tile(r, 64, SUB) if r % SUB == 0 else r
    blk = pl.BlockSpec((n, rt, c), lambda i: (0, i, 0))

    def body(a_ref, b_ref, o_ref):
        o_ref[...] = a_ref[...] + b_ref[...]

    return pl.pallas_call(
        body, name=name, grid=(r // rt,), in_specs=[blk, blk], out_specs=blk,
        out_shape=jax.ShapeDtypeStruct((n, r, c), f32), compiler_params=_params(1),
    )(a, b)


_ANY = pl.BlockSpec(memory_space=pl.ANY)
_MESH = pl.DeviceIdType.MESH


def _coords():
    return lax.axis_index("x"), lax.axis_index("y"), lax.axis_index("c")


def _chip_peer(x, y, r):
    return x ^ (r >> 1), y ^ (r & 1)


class _Exchange:
    def __init__(self, ins, out_shapes, scratch, start, finish):
        self.ins, self.out_shapes, self.scratch, self.start, self.finish = ins, out_shapes, scratch, start, finish

    def split(self, refs):
        n, m = len(self.ins), len(self.out_shapes)
        return refs[:n], refs[n:n + m], refs[n + m:]


def run_exchange(ex, name):
    def body(*refs):
        ins, outs, sems = ex.split(refs)
        ex.start(ins, outs, sems)
        ex.finish(ins, outs, sems)

    return pl.pallas_call(body, name=name, in_specs=[_ANY] * len(ex.ins), out_specs=[_ANY] * len(ex.out_shapes),
                          out_shape=ex.out_shapes, scratch_shapes=ex.scratch)(*ex.ins)


def gather_chips(big, small):
    nb, n = len(big), len(big) + len(small)
    shards = list(big) + list(small)
    kb = nb * (N_CHIPS - 1)
    k = n * (N_CHIPS - 1)

    def copies(src, dst, sems):
        send, recv, fsend, frecv = sems
        x, y, c = _coords()
        sib = (x, y, 1 - c)
        peers = [_chip_peer(x, y, r) for r in range(1, N_CHIPS)]

        def direct(t, j, slot):
            s = t * (N_CHIPS - 1) + j
            if t < nb:
                return pltpu.make_async_remote_copy(src[t].at[c], dst[t].at[slot, c], send.at[s], recv.at[s],
                                                    device_id=(*peers[j], c), device_id_type=_MESH)
            return pltpu.make_async_remote_copy(src[t], dst[t].at[slot], send.at[s], recv.at[s],
                                                device_id=(*peers[j], c), device_id_type=_MESH)

        def passed(t, j, half):
            s = t * (N_CHIPS - 1) + j
            px, py = peers[j]
            part = dst[t].at[2 * px + py, half]
            return pltpu.make_async_remote_copy(part, part, fsend.at[s], frecv.at[s], device_id=sib, device_id_type=_MESH)

        return direct, passed, peers, 2 * x + y, c

    def start(src, dst, sems):
        direct, _, _, me, _ = copies(src, dst, sems)
        for t in range(n):
            for j in range(N_CHIPS - 1):
                direct(t, j, me).start()

    def finish(src, dst, sems):
        direct, passed, peers, me, c = copies(src, dst, sems)
        fwd = []
        for t in range(nb):
            for j in range(N_CHIPS - 1):
                px, py = peers[j]
                direct(t, j, 2 * px + py).wait_recv()
                fwd.append(passed(t, j, c))
                fwd[-1].start()
        for t in range(nb, n):
            for j in range(N_CHIPS - 1):
                px, py = peers[j]
                direct(t, j, 2 * px + py).wait_recv()
        for t in range(nb):
            for j in range(N_CHIPS - 1):
                passed(t, j, 1 - c).wait_recv()
        for t in range(n):
            for j in range(N_CHIPS - 1):
                direct(t, j, me).wait_send()
        for cp in fwd:
            cp.wait_send()

    return _Exchange(shards, [jax.ShapeDtypeStruct((N_CHIPS,) + s.shape, s.dtype) for s in shards],
                     [pltpu.SemaphoreType.DMA((k,)), pltpu.SemaphoreType.DMA((k,)),
                      pltpu.SemaphoreType.DMA((max(kb, 1),)), pltpu.SemaphoreType.DMA((max(kb, 1),))], start, finish)


def sibling_swap_halves(grads, name):
    n = len(grads)

    def body(*refs):
        src, dst = refs[:n], refs[n:2 * n]
        send, recv = refs[2 * n:]
        x, y, c = _coords()
        cps = []
        for t in range(n):
            for o in range(N_CHIPS):
                s = t * N_CHIPS + o
                cps.append(pltpu.make_async_remote_copy(src[t].at[o, 1 - c], dst[t].at[o], send.at[s], recv.at[s],
                                                        device_id=(x, y, 1 - c), device_id_type=_MESH))
        for cp in cps:
            cp.start()
        for cp in cps:
            cp.wait_recv()
        for cp in cps:
            cp.wait_send()

    k = n * N_CHIPS
    return pl.pallas_call(
        body, name=name, in_specs=[_ANY] * n, out_specs=[_ANY] * n,
        out_shape=[jax.ShapeDtypeStruct((N_CHIPS,) + g.shape[2:], g.dtype) for g in grads],
        scratch_shapes=[pltpu.SemaphoreType.DMA((k,)), pltpu.SemaphoreType.DMA((k,))],
    )(*grads)


def scatter_chips(parts):
    n = len(parts)
    k = n * (N_CHIPS - 1)

    def copy(src, dst, sems, t, r, landing):
        send, recv = sems
        x, y, c = _coords()
        me = 2 * x + y
        px, py = _chip_peer(x, y, r)
        peer = 2 * px + py
        s = t * (N_CHIPS - 1) + r - 1
        return pltpu.make_async_remote_copy(src[t].at[me if landing else peer], dst[t].at[peer if landing else me],
                                            send.at[s], recv.at[s], device_id=(px, py, c), device_id_type=_MESH)

    def start(src, dst, sems):
        for t in range(n):
            for r in range(1, N_CHIPS):
                copy(src, dst, sems, t, r, False).start()

    def finish(src, dst, sems):
        for t in range(n):
            for r in range(1, N_CHIPS):
                copy(src, dst, sems, t, r, True).wait_recv()
        for t in range(n):
            for r in range(1, N_CHIPS):
                copy(src, dst, sems, t, r, False).wait_send()

    return _Exchange(list(parts), [jax.ShapeDtypeStruct(p.shape, p.dtype) for p in parts],
                     [pltpu.SemaphoreType.DMA((k,)), pltpu.SemaphoreType.DMA((k,))], start, finish)


def sibling_send(halves, name):
    n = len(halves)

    def body(*refs):
        src, dst = refs[:n], refs[n:2 * n]
        send, recv = refs[2 * n:]
        x, y, c = _coords()
        cps = [pltpu.make_async_remote_copy(src[t], dst[t], send.at[t], recv.at[t],
                                            device_id=(x, y, 1 - c), device_id_type=_MESH) for t in range(n)]
        for cp in cps:
            cp.start()
        for cp in cps:
            cp.wait_recv()
        for cp in cps:
            cp.wait_send()

    return pl.pallas_call(
        body, name=name, in_specs=[_ANY] * n, out_specs=[_ANY] * n,
        out_shape=[jax.ShapeDtypeStruct(h.shape, h.dtype) for h in halves],
        scratch_shapes=[pltpu.SemaphoreType.DMA((n,)), pltpu.SemaphoreType.DMA((n,))],
    )(*halves)


def gather_all(block, name):
    def body(src, dst, send, recv, loc):
        x, y, c = _coords()
        me = 4 * x + 2 * y + c
        mine = pltpu.make_async_copy(src, dst.at[me], loc)
        mine.start()
        outs = []
        for r in range(1, N_DEV):
            peer = (x ^ (r >> 2), y ^ ((r >> 1) & 1), c ^ (r & 1))
            outs.append(pltpu.make_async_remote_copy(src, dst.at[me], send.at[r - 1], recv.at[r - 1],
                                                     device_id=peer, device_id_type=_MESH))
        for cp in outs:
            cp.start()
        for r in range(1, N_DEV):
            px, py, pc = x ^ (r >> 2), y ^ ((r >> 1) & 1), c ^ (r & 1)
            pltpu.make_async_remote_copy(src, dst.at[4 * px + 2 * py + pc], send.at[r - 1], recv.at[r - 1],
                                         device_id=(px, py, pc), device_id_type=_MESH).wait_recv()
        for cp in outs:
            cp.wait_send()
        mine.wait()

    return pl.pallas_call(
        body, name=name, in_specs=[_ANY], out_specs=_ANY,
        out_shape=jax.ShapeDtypeStruct((N_DEV,) + block.shape, block.dtype),
        scratch_shapes=[pltpu.SemaphoreType.DMA((N_DEV - 1,)), pltpu.SemaphoreType.DMA((N_DEV - 1,)),
                        pltpu.SemaphoreType.DMA(())],
    )(block)


def _pad_lanes(v, n=LANE):
    return jnp.pad(v, ((0, 0), (0, n - v.shape[1])))


def _conv_w8(w):
    return jnp.pad(w, ((0, SUB - DN_CONV), (0, 0)))


def _row_layout(gc, tp):
    nc = tp // CHUNK
    g = gc[:, :, 0].reshape(HEADS, nc, 1, CHUNK)
    g = jnp.broadcast_to(g, (HEADS, nc, SUB, CHUNK))
    return jnp.pad(g, ((0, 0), (0, 0), (0, 0), (0, LANE - CHUNK))).reshape(HEADS, nc * SUB, LANE)


def _step(x, meta, W, target, late_weights=None, early_grads=None):
    W = dict(W)
    seq = x.shape[0]
    tp = P0 + N_META + seq
    h0 = jnp.concatenate([jnp.zeros((P0, D_MODEL), f32), meta, x], axis=0)
    w_in = W["w_in"]
    w_big = jnp.concatenate([w_in[:, :AB_COL], w_in[:, AB_COL + 2 * HEADS:]], axis=1)
    w_ab = _pad_lanes(w_in[:, AB_COL:AB_COL + 2 * HEADS])
    cq8, ck8, cv8 = _conv_w8(W["conv_q"]), _conv_w8(W["conv_k"]), _conv_w8(W["conv_v"])
    al, dtb = _pad_lanes(W["dn_a_log"]), _pad_lanes(W["dn_dt_bias"])

    n1 = rms_fwd(h0, W["norm_mix_gain"], "rms1_fwd")
    proj = matmul(n1, w_big, "nn", "proj_fwd")
    pab = matmul(n1, w_ab, "nn", "pab_fwd")
    qn = conv_fwd(proj, cq8, C_DQ * 8, 8, True, "conv_q_fwd")
    kn = conv_fwd(proj, ck8, C_DK * 8, 8, True, "conv_k_fwd")
    va = conv_fwd(proj, cv8, C_DV * 8, 16, False, "conv_v_fwd")
    gc, bc = ab_fwd(pab, al, dtb)
    grow = _row_layout(gc, tp)
    o_dn, states = gdn_fwd(qn, kn, va, gc, bc, grow)
    on = dn_out_fwd(o_dn, proj, W["dn_out_norm_gain"])
    qs, ks, vs = sb_prep_fwd(proj, W["sb_q_norm_gain"], W["sb_k_norm_gain"])
    o_sb, arrived = sb_fwd(qs, ks, vs, rider=late_weights[0] if late_weights else None)
    if late_weights:
        W.update(late_weights[1](arrived))
    o_sb16 = cast_bf16(o_sb, "o_sb_cast")
    ydn = matmul(on, W["w_branch_dn"], "nn", "ydn_fwd")
    ysb = matmul(o_sb16, W["w_branch_sb"], "nn", "ysb_fwd")
    merged = merge_fwd(proj, ydn, ysb)
    h1 = matmul(merged, W["w_out"], "nn", "wout_fwd", residual=h0)
    n2 = rms_fwd(h1, W["norm_ffn_gain"], "rms2_fwd")
    u = matmul(n2, W["w_ffn_in"], "nn", "ffn_in_fwd", tn_t=512)
    act = swiglu_fwd(u)
    y = matmul(act, W["w_ffn_out"], "nn", "ffn_out_fwd", residual=h1)
    dy, loss = loss_head(y, target)

    G = {}
    dy16 = cast_bf16(dy, "dy_cast")
    dact = matmul(dy16, W["w_ffn_out"], "nt", "ffn_out_dx", tn_t=1408)
    G["w_ffn_out"] = matmul(act, dy16, "tn", "ffn_out_dw", tm_t=1408)
    dgate, dup = swiglu_bwd(u, dact)
    du = jnp.concatenate([dgate, dup], axis=1)
    dn2 = matmul(du, W["w_ffn_in"], "nt", "ffn_in_dx", tk_t=512)
    G["w_ffn_in"] = matmul(n2, du, "tn", "ffn_in_dw", tn_t=512)
    dh1, G["norm_ffn_gain"] = rms_bwd(h1, W["norm_ffn_gain"], dn2, dy, "rms2_bwd")
    dh1_16 = cast_bf16(dh1, "dh1_cast")
    dmerged = matmul(dh1_16, W["w_out"], "nt", "wout_dx")
    G["w_out"] = matmul(merged, dh1_16, "tn", "wout_dw")
    dyd, dys, dgd, dgs = merge_bwd(proj, ydn, ysb, dmerged)
    don = matmul(dyd, W["w_branch_dn"], "nt", "ydn_dx")
    G["w_branch_dn"] = matmul(on, dyd, "tn", "ydn_dw")
    do_sb = matmul(dys, W["w_branch_sb"], "nt", "ysb_dx")
    G["w_branch_sb"] = matmul(o_sb16, dys, "tn", "ysb_dw")
    do_dn, dz, G["dn_out_norm_gain"] = dn_out_bwd(o_dn, proj, W["dn_out_norm_gain"], don)
    dqn, dkn, dva, dgc, dbc = gdn_bwd(qn, kn, va, gc, bc, grow, states, do_dn)
    dpab, dal, ddt = ab_bwd(pab, al, dtb, dgc, dbc)
    G["dn_a_log"], G["dn_dt_bias"] = dal[:, :HEADS], ddt[:, :HEADS]
    dyq, dcq = conv_bwd_act(proj, cq8, dqn, C_DQ * 8, 8, True, "conv_q_bwd")
    dyk, dck = conv_bwd_act(proj, ck8, dkn, C_DK * 8, 8, True, "conv_k_bwd")
    dyv, dcv = conv_bwd_act(proj, cv8, dva, C_DV * 8, 16, False, "conv_v_bwd")
    G["conv_q"], G["conv_k"], G["conv_v"] = dcq[:DN_CONV], dck[:DN_CONV], dcv[:DN_CONV]
    d_dq = conv_bwd_in(dyq, cq8, "conv_q_dx")
    d_dk = conv_bwd_in(dyk, ck8, "conv_k_dx")
    d_dv = conv_bwd_in(dyv, cv8, "conv_v_dx")
    (dqs, dks, dvs), delivered = sb_bwd(qs, ks, vs, o_sb, do_sb, rider=early_grads[0](G) if early_grads else None)
    if early_grads:
        early_grads[1](delivered)
    d_sq, d_sk, G["sb_q_norm_gain"], G["sb_k_norm_gain"] = sb_prep_bwd(
        proj, W["sb_q_norm_gain"], W["sb_k_norm_gain"], dqs, dks)
    d_sv = cast_bf16(dvs, "dvs_cast")
    dproj = jnp.concatenate([d_dq, d_dk, d_dv, dz, d_sq, d_sk, d_sv, dgd, dgs], axis=1)
    dn1 = matmul(dproj, w_big, "nt", "proj_dx", tk_t=1024)
    dn1 = matmul(dpab, w_ab, "nt", "pab_dx", residual=dn1)
    dw_big = matmul(n1, dproj, "tn", "proj_dw")
    dw_ab = matmul(n1, dpab, "tn", "pab_dw")
    G["w_in"] = jnp.concatenate([dw_big[:, :AB_COL], dw_ab[:, :2 * HEADS], dw_big[:, AB_COL:]], axis=1)
    dh0, G["norm_mix_gain"] = rms_bwd(h0, W["norm_mix_gain"], dn1, dh1, "rms1_bwd")
    G["meta_tokens"] = dh0[P0:P0 + N_META]
    return loss, dh0[P0 + N_META:], G


_BIG = ("w_in", "w_branch_dn", "w_branch_sb", "w_out", "w_ffn_in", "w_ffn_out")
_COL_SHARDED = ("w_in", "w_ffn_in", "meta_tokens", "conv_q", "conv_k", "conv_v")
_SMALL_REPL = ("norm_mix_gain", "norm_ffn_gain", "dn_a_log", "dn_dt_bias", "dn_out_norm_gain", "sb_q_norm_gain",
               "sb_k_norm_gain")
_SMALL_SHARD = ("meta_tokens", "conv_q", "conv_k", "conv_v")
_ORDER = ("meta_tokens", "norm_mix_gain", "w_in", "conv_q", "conv_k", "conv_v", "dn_a_log", "dn_dt_bias",
          "dn_out_norm_gain", "sb_q_norm_gain", "sb_k_norm_gain", "w_branch_dn", "w_branch_sb", "w_out",
          "norm_ffn_gain", "w_ffn_in", "w_ffn_out")


def _unshard(g4, name):
    if name in _COL_SHARDED:
        r, cs = g4.shape[1:]
        return jnp.transpose(g4, (1, 0, 2)).reshape(r, N_CHIPS * cs)
    return g4.reshape((-1,) + g4.shape[2:])


def _to_shards(full, name):
    if name in _COL_SHARDED:
        r, c = full.shape
        return jnp.transpose(full.reshape(r, N_CHIPS, c // N_CHIPS), (1, 0, 2))
    r, c = full.shape
    return full.reshape(N_CHIPS, r // N_CHIPS, c)


def _rows_1024(a):
    r, c = a.shape
    if c >= 1024:
        return a.reshape(r * (c // 1024), 1024)
    return jnp.pad(a, ((0, 0), (0, 1024 - c)))


def kernel(x, meta_tokens, norm_mix_gain, w_in, conv_q, conv_k, conv_v, dn_a_log, dn_dt_bias, dn_out_norm_gain, sb_q_norm_gain, sb_k_norm_gain, w_branch_dn, w_branch_sb, w_out, norm_ffn_gain, w_ffn_in, w_ffn_out, loss_target, m_meta_tokens, m_norm_mix_gain, m_w_in, m_conv_q, m_conv_k, m_conv_v, m_dn_a_log, m_dn_dt_bias, m_dn_out_norm_gain, m_sb_q_norm_gain, m_sb_k_norm_gain, m_w_branch_dn, m_w_branch_sb, m_w_out, m_norm_ffn_gain, m_w_ffn_in, m_w_ffn_out, v_meta_tokens, v_norm_mix_gain, v_w_in, v_conv_q, v_conv_k, v_conv_v, v_dn_a_log, v_dn_dt_bias, v_dn_out_norm_gain, v_sb_q_norm_gain, v_sb_k_norm_gain, v_w_branch_dn, v_w_branch_sb, v_w_out, v_norm_ffn_gain, v_w_ffn_in, v_w_ffn_out):
    Wl = dict(meta_tokens=meta_tokens, norm_mix_gain=norm_mix_gain, w_in=w_in[0], conv_q=conv_q[0], conv_k=conv_k[0],
              conv_v=conv_v[0], dn_a_log=dn_a_log, dn_dt_bias=dn_dt_bias, dn_out_norm_gain=dn_out_norm_gain,
              sb_q_norm_gain=sb_q_norm_gain, sb_k_norm_gain=sb_k_norm_gain, w_branch_dn=w_branch_dn[0],
              w_branch_sb=w_branch_sb[0], w_out=w_out[0], norm_ffn_gain=norm_ffn_gain, w_ffn_in=w_ffn_in[0],
              w_ffn_out=w_ffn_out[0])
    Ml = dict(meta_tokens=m_meta_tokens, norm_mix_gain=m_norm_mix_gain, w_in=m_w_in[0], conv_q=m_conv_q[0],
              conv_k=m_conv_k[0], conv_v=m_conv_v[0], dn_a_log=m_dn_a_log, dn_dt_bias=m_dn_dt_bias,
              dn_out_norm_gain=m_dn_out_norm_gain, sb_q_norm_gain=m_sb_q_norm_gain, sb_k_norm_gain=m_sb_k_norm_gain,
              w_branch_dn=m_w_branch_dn[0], w_branch_sb=m_w_branch_sb[0], w_out=m_w_out[0],
              norm_ffn_gain=m_norm_ffn_gain, w_ffn_in=m_w_ffn_in[0], w_ffn_out=m_w_ffn_out[0])
    Vl = dict(meta_tokens=v_meta_tokens, norm_mix_gain=v_norm_mix_gain, w_in=v_w_in[0], conv_q=v_conv_q[0],
              conv_k=v_conv_k[0], conv_v=v_conv_v[0], dn_a_log=v_dn_a_log, dn_dt_bias=v_dn_dt_bias,
              dn_out_norm_gain=v_dn_out_norm_gain, sb_q_norm_gain=v_sb_q_norm_gain, sb_k_norm_gain=v_sb_k_norm_gain,
              w_branch_dn=v_w_branch_dn[0], w_branch_sb=v_w_branch_sb[0], w_out=v_w_out[0],
              norm_ffn_gain=v_norm_ffn_gain, w_ffn_in=v_w_ffn_in[0], w_ffn_out=v_w_ffn_out[0])
    lead = {n: (1,) if (n in _BIG or n in ("conv_q", "conv_k", "conv_v")) else () for n in _ORDER}

    chip = 2 * lax.axis_index("x") + lax.axis_index("y")
    c = lax.axis_index("c")
    halved = {n: Wl[n].astype(bf16).reshape(2, Wl[n].shape[0] // 2, Wl[n].shape[1]) for n in _BIG}

    def gathered_weights(names, owns, outs):
        res = {}
        for n, own, g4 in zip(names, owns, outs):
            g4 = lax.dynamic_update_slice(g4, own[None], (chip,) + (0,) * own.ndim)
            if n in _BIG:
                g4 = g4.reshape(N_CHIPS, 2 * g4.shape[2], g4.shape[3])
            res[n] = _unshard(g4, n)
        return res

    first = ["w_in"] + list(_SMALL_SHARD)
    first_own = [halved["w_in"]] + [Wl[n] for n in _SMALL_SHARD]
    W = dict(Wl)
    W.update(gathered_weights(first, first_own, run_exchange(gather_chips(first_own[:1], first_own[1:]), "gather_w_in")))
    late = [n for n in _BIG if n != "w_in"]
    late_own = [halved[n] for n in late]
    for n in late:
        del W[n]

    def pair_reduced(names, G, tag):
        g4 = [_to_shards(G[n], n) for n in names]
        g42 = [g.reshape(N_CHIPS, 2, g.shape[1] // 2, g.shape[2]) for g in g4]
        from_sib = sibling_swap_halves(g42, "grad_sibling_swap_" + tag)
        mine = [lax.dynamic_index_in_dim(g, c, axis=1, keepdims=False) for g in g42]
        return [add2(a, b, "grad_pair_add_%s%d" % (tag, t)) for t, (a, b) in enumerate(zip(mine, from_sib))]

    def chip_reduced(parts, slots, tag):
        slots = [lax.dynamic_update_slice(s, lax.dynamic_index_in_dim(p, chip, axis=0, keepdims=True), (chip, 0, 0))
                 for s, p in zip(slots, parts)]
        return [sum_slots(s, "grad_chip_sum_%s%d" % (tag, t)) for t, s in enumerate(slots)]

    early = {}

    def early_begin(G):
        early["parts"] = pair_reduced(late, G, "a")
        return scatter_chips(early["parts"])

    def early_done(slots):
        early["halves"] = chip_reduced(early["parts"], slots, "a")

    loss, grad_x, G = _step(x[0], W["meta_tokens"], W, loss_target[0],
                            late_weights=(gather_chips(late_own, []), lambda outs: gathered_weights(late, late_own, outs)),
                            early_grads=(early_begin, early_done))
    parts = pair_reduced(["w_in"], G, "b")
    halves = chip_reduced(parts, run_exchange(scatter_chips(parts), "grad_chip_exchange_w_in"), "b") + early["halves"]
    theirs = sibling_send(halves, "grad_sibling_send")
    Gs = {}
    for n, h, o in zip(["w_in"] + late, halves, theirs):
        Gs[n] = lax.dynamic_update_slice(jnp.concatenate([o, o], axis=0), h, (c * h.shape[0], 0))

    small_names = list(_SMALL_REPL) + list(_SMALL_SHARD)
    pieces = [_rows_1024(G[n]) for n in small_names] + [_rows_1024(loss)]
    counts = [p.shape[0] for p in pieces]
    pack = jnp.concatenate(pieces, axis=0)
    pad_rows = (-pack.shape[0]) % SUB
    pack = jnp.pad(pack, ((0, pad_rows), (0, 0)))
    total = sum_slots(gather_all(pack, "small_gather"), "small_sum")
    chip = 2 * lax.axis_index("x") + lax.axis_index("y")
    row = 0
    for n, cnt in zip(small_names, counts[:-1]):
        blk = total[row:row + cnt]
        row += cnt
        full_shape = G[n].shape
        if full_shape[1] >= 1024:
            blk = blk.reshape(full_shape)
        else:
            blk = blk[:, :full_shape[1]]
        if n in _SMALL_SHARD:
            cs = full_shape[1] // N_CHIPS
            blk = lax.dynamic_slice_in_dim(blk, chip * cs, cs, axis=1)
        Gs[n] = blk
    loss_out = total[row, 0]

    grads, deltas, new_m, new_v = [], [], [], []
    for n in _ORDER:
        d, m2, v2 = adamw(Wl[n], Gs[n], Ml[n], Vl[n], "adamw_" + n)
        shape = lead[n] + Wl[n].shape
        grads.append(Gs[n].reshape(shape))
        deltas.append(d.reshape(shape))
        new_m.append(m2.reshape(shape))
        new_v.append(v2.reshape(shape))
    return (loss_out, grad_x[None], *grads, *deltas, *new_m, *new_v)
```

```python
import jax
import jax.numpy as jnp
from jax import lax
from jax.experimental import pallas as pl
from jax.experimental.pallas import tpu as pltpu

f32 = jnp.float32
bf16 = jnp.bfloat16

D_MODEL = 1024
N_META = 16
CHUNK = 64
HEADS = 8
DN_DK = 128
DN_DV = 256
DN_CONV = 4
DN_QK = HEADS * DN_DK
DN_V = HEADS * DN_DV
SB_DH = 128
SB_W = HEADS * SB_DH
SB_BLOCK = 128
SB_GROUP = 4
SB_HEADS_PER_STEP = 2
GDN_HEADS_PER_STEP = 8
CONV_W = 512
D_FF = 2816
RMS_EPS = 1e-6
L2_EPS = 1e-6
ADAM_LR = 0.001
ADAM_B1 = 0.9
ADAM_B2 = 0.999
ADAM_EPS = 1e-08
ADAM_WD = 0.01
ADAM_STEP = 10

P0 = 112
LANE = 128
SUB = 8
VMEM_LIMIT = 48 * 1024 * 1024
N_CHIPS = 4
N_DEV = 8

C_DQ, C_DK, C_DV, C_DZ, C_SQ, C_SK, C_SV, C_GDN, C_GSB = 0, 1, 2, 4, 6, 7, 8, 9, 10
PROJ_BIG = 11 * 1024
AB_COL = 2 * DN_QK + 2 * DN_V


def _params(n_axes):
    return pltpu.CompilerParams(dimension_semantics=("arbitrary",) * n_axes, vmem_limit_bytes=VMEM_LIMIT)


def _tile(n, target, q=LANE):
    best = None
    for t in range(q, min(n, target) + 1, q):
        if n % t == 0:
            best = t
    return best if best is not None else n


def _dot(a, b):
    return jnp.dot(a.astype(bf16), b.astype(bf16), preferred_element_type=f32)


def _dot_nt(a, b):
    return lax.dot_general(a.astype(bf16), b.astype(bf16), (((1,), (1,)), ((), ())), preferred_element_type=f32)


def _dot_tn(a, b):
    return lax.dot_general(a.astype(bf16), b.astype(bf16), (((0,), (0,)), ((), ())), preferred_element_type=f32)


_HI = lax.Precision.HIGH


def _hdot(a, b):
    return jnp.dot(a, b, precision=_HI, preferred_element_type=f32)


def _hdot_nt(a, b):
    return lax.dot_general(a, b, (((1,), (1,)), ((), ())), precision=_HI, preferred_element_type=f32)


def _hdot_tn(a, b):
    return lax.dot_general(a, b, (((0,), (0,)), ((), ())), precision=_HI, preferred_element_type=f32)


def _sigmoid(x):
    e = jnp.exp(-jnp.abs(x))
    r = 1.0 / (1.0 + e)
    return jnp.where(x >= 0, r, e * r)


def _log1p_small(e):
    return jnp.where(e < 1e-3, e * (1.0 - e * (0.5 - e * (1.0 / 3.0))), jnp.log(1.0 + e))


def _rowsum(x):
    return jnp.sum(x, axis=1, keepdims=True)


def _allsum(x):
    return jnp.sum(jnp.sum(x, axis=1, keepdims=True), axis=0, keepdims=True)


def matmul(a, b, mode, name, residual=None, out_dtype=f32, tm_t=1408, tn_t=1024, tk_t=1408, rider=None):
    if mode == "nn":
        (M, K), (K2, N) = a.shape, b.shape
    elif mode == "nt":
        (M, K), (N, K2) = a.shape, b.shape
    else:
        (K, M), (K2, N) = a.shape, b.shape
    assert K == K2, (a.shape, b.shape, mode)
    tm, tn, tk = _tile(M, tm_t), _tile(N, tn_t), _tile(K, tk_t)
    nk = K // tk
    if mode == "nn":
        a_spec = pl.BlockSpec((tm, tk), lambda i, j, k: (i, k))
        b_spec = pl.BlockSpec((tk, tn), lambda i, j, k: (k, j))
        dims = (((1,), (0,)), ((), ()))
    elif mode == "nt":
        a_spec = pl.BlockSpec((tm, tk), lambda i, j, k: (i, k))
        b_spec = pl.BlockSpec((tn, tk), lambda i, j, k: (j, k))
        dims = (((1,), (1,)), ((), ()))
    else:
        a_spec = pl.BlockSpec((tk, tm), lambda i, j, k: (k, i))
        b_spec = pl.BlockSpec((tk, tn), lambda i, j, k: (k, j))
        dims = (((0,), (0,)), ((), ()))
    o_spec = pl.BlockSpec((tm, tn), lambda i, j, k: (i, j))
    has_res = residual is not None
    grid = (M // tm, N // tn, nk)
    split, ride_first, ride_last = _ride(rider, 3 if has_res else 2, 1, grid)

    def body(*refs):
        ins_, (o_ref,), (rin, rout, rest) = split(refs)
        a_ref, b_ref = ins_[:2]
        r_ref = ins_[2] if has_res else None
        acc_ref, ride = rest[0], (rin, rout, rest[1:])
        ride_first(ride)
        k = pl.program_id(2)

        @pl.when(k == 0)
        def _():
            acc_ref[...] = jnp.zeros_like(acc_ref)

        acc_ref[...] += lax.dot_general(a_ref[...].astype(bf16), b_ref[...].astype(bf16), dims,
                                        preferred_element_type=f32)

        @pl.when(k == nk - 1)
        def _():
            r = acc_ref[...]
            if has_res:
                r = r + r_ref[...]
            o_ref[...] = r.astype(out_dtype)

        ride_last(ride)

    ins = [a, b] + ([residual] if has_res else [])
    specs = [a_spec, b_spec] + ([o_spec] if has_res else [])
    r_ins = rider.ins if rider else []
    r_outs = rider.out_shapes if rider else []
    res = pl.pallas_call(
        body, name=name, grid=grid, in_specs=specs + [_ANY] * len(r_ins), out_specs=[o_spec] + [_ANY] * len(r_outs),
        out_shape=[jax.ShapeDtypeStruct((M, N), out_dtype)] + list(r_outs),
        scratch_shapes=[pltpu.VMEM((tm, tn), f32)] + (rider.scratch if rider else []), compiler_params=_params(3),
    )(*ins, *r_ins)
    return (res[0], res[1:]) if rider else res[0]


def _row_tile(tp):
    return _tile(tp, 512)


def rms_fwd(h, gain, name):
    tp, d = h.shape
    rt = _row_tile(tp)

    def body(h_ref, g_ref, o_ref):
        x = h_ref[...]
        r = lax.rsqrt(jnp.mean(x * x, axis=-1, keepdims=True) + RMS_EPS)
        o_ref[...] = (x * r * g_ref[...]).astype(bf16)

    return pl.pallas_call(
        body, name=name, grid=(tp // rt,),
        in_specs=[pl.BlockSpec((rt, d), lambda i: (i, 0)), pl.BlockSpec((1, d), lambda i: (0, 0))],
        out_specs=pl.BlockSpec((rt, d), lambda i: (i, 0)),
        out_shape=jax.ShapeDtypeStruct((tp, d), bf16), compiler_params=_params(1),
    )(h, gain)


def rms_bwd(h, gain, dn, dres, name):
    tp, d = h.shape
    rt = _row_tile(tp)

    def body(h_ref, g_ref, dn_ref, dr_ref, dh_ref, dg_ref):
        i = pl.program_id(0)
        x = h_ref[...]
        r = lax.rsqrt(jnp.mean(x * x, axis=-1, keepdims=True) + RMS_EPS)
        xh = x * r
        dn_ = dn_ref[...]
        dxh = dn_ * g_ref[...]
        dh_ref[...] = r * (dxh - xh * jnp.mean(dxh * xh, axis=-1, keepdims=True)) + dr_ref[...]
        part = jnp.sum(dn_ * xh, axis=0, keepdims=True)

        @pl.when(i == 0)
        def _():
            dg_ref[...] = part

        @pl.when(i > 0)
        def _():
            dg_ref[...] += part

    row = pl.BlockSpec((rt, d), lambda i: (i, 0))
    vec = pl.BlockSpec((1, d), lambda i: (0, 0))
    return pl.pallas_call(
        body, name=name, grid=(tp // rt,), in_specs=[row, vec, row, row], out_specs=[row, vec],
        out_shape=[jax.ShapeDtypeStruct((tp, d), f32), jax.ShapeDtypeStruct((1, d), f32)],
        compiler_params=_params(1),
    )(h, gain, dn, dres)


def loss_head(y, target):
    tp, d = y.shape
    rt = P0 + N_META
    assert rt == SB_BLOCK and tp % rt == 0 and target.shape == (tp - rt, d)

    def body(y_ref, t_ref, dy_ref, l_ref):
        i = pl.program_id(0)

        @pl.when(i == 0)
        def _():
            dy_ref[...] = jnp.zeros_like(dy_ref)
            l_ref[...] = jnp.zeros_like(l_ref)

        @pl.when(i > 0)
        def _():
            err = y_ref[...] - t_ref[...]
            dy_ref[...] = err * (1.0 / d)
            l_ref[...] += jnp.broadcast_to(_allsum(err * err) * (0.5 / d), l_ref.shape)

    return pl.pallas_call(
        body, name="loss_head", grid=(tp // rt,),
        in_specs=[pl.BlockSpec((rt, d), lambda i: (i, 0)), pl.BlockSpec((rt, d), lambda i: (jnp.maximum(i - 1, 0), 0))],
        out_specs=[pl.BlockSpec((rt, d), lambda i: (i, 0)), pl.BlockSpec((1, LANE), lambda i: (0, 0))],
        out_shape=[jax.ShapeDtypeStruct((tp, d), f32), jax.ShapeDtypeStruct((1, LANE), f32)],
        compiler_params=_params(1),
    )(y, target)


def swiglu_fwd(u):
    tp = u.shape[0]
    rt, cb = _row_tile(tp), D_FF // 2
    nb = D_FF // cb

    def body(g_ref, u_ref, o_ref):
        g = g_ref[...]
        o_ref[...] = (g * _sigmoid(g) * u_ref[...]).astype(bf16)

    return pl.pallas_call(
        body, name="swiglu_fwd", grid=(tp // rt, nb),
        in_specs=[pl.BlockSpec((rt, cb), lambda i, j: (i, j)), pl.BlockSpec((rt, cb), lambda i, j: (i, j + nb))],
        out_specs=pl.BlockSpec((rt, cb), lambda i, j: (i, j)),
        out_shape=jax.ShapeDtypeStruct((tp, D_FF), bf16), compiler_params=_params(2),
    )(u, u)


def swiglu_bwd(u, dact):
    tp = u.shape[0]
    rt, cb = _row_tile(tp), D_FF // 2
    nb = D_FF // cb

    def body(g_ref, u_ref, da_ref, dg_ref, du_ref):
        g = g_ref[...]
        s = _sigmoid(g)
        da = da_ref[...]
        dg_ref[...] = (da * u_ref[...] * s * (1.0 + g * (1.0 - s))).astype(bf16)
        du_ref[...] = (da * g * s).astype(bf16)

    lo = pl.BlockSpec((rt, cb), lambda i, j: (i, j))
    hi = pl.BlockSpec((rt, cb), lambda i, j: (i, j + nb))
    dgate, dup = pl.pallas_call(
        body, name="swiglu_bwd", grid=(tp // rt, nb), in_specs=[lo, hi, lo], out_specs=[lo, lo],
        out_shape=[jax.ShapeDtypeStruct((tp, D_FF), bf16)] * 2, compiler_params=_params(2),
    )(u, u, dact)
    return dgate, dup


def merge_fwd(proj, ydn, ysb):
    tp = proj.shape[0]
    rt, d = _row_tile(tp), D_MODEL

    def body(gd_ref, gs_ref, yd_ref, ys_ref, o_ref):
        o_ref[...] = (_sigmoid(gd_ref[...]) * yd_ref[...] + _sigmoid(gs_ref[...]) * ys_ref[...]).astype(bf16)

    row = pl.BlockSpec((rt, d), lambda i: (i, 0))
    return pl.pallas_call(
        body, name="merge_fwd", grid=(tp // rt,),
        in_specs=[pl.BlockSpec((rt, d), lambda i: (i, C_GDN)), pl.BlockSpec((rt, d), lambda i: (i, C_GSB)), row, row],
        out_specs=row, out_shape=jax.ShapeDtypeStruct((tp, d), bf16), compiler_params=_params(1),
    )(proj, proj, ydn, ysb)


def merge_bwd(proj, ydn, ysb, dm):
    tp = proj.shape[0]
    rt, d = _row_tile(tp), D_MODEL

    def body(gd_ref, gs_ref, yd_ref, ys_ref, dm_ref, dyd_ref, dys_ref, dgd_ref, dgs_ref):
        dm_ = dm_ref[...]
        sd = _sigmoid(gd_ref[...])
        ss = _sigmoid(gs_ref[...])
        dyd_ref[...] = (dm_ * sd).astype(bf16)
        dys_ref[...] = (dm_ * ss).astype(bf16)
        dgd_ref[...] = (dm_ * yd_ref[...] * sd * (1.0 - sd)).astype(bf16)
        dgs_ref[...] = (dm_ * ys_ref[...] * ss * (1.0 - ss)).astype(bf16)

    row = pl.BlockSpec((rt, d), lambda i: (i, 0))
    return pl.pallas_call(
        body, name="merge_bwd", grid=(tp // rt,),
        in_specs=[pl.BlockSpec((rt, d), lambda i: (i, C_GDN)), pl.BlockSpec((rt, d), lambda i: (i, C_GSB)), row, row, row],
        out_specs=[row] * 4, out_shape=[jax.ShapeDtypeStruct((tp, d), bf16)] * 4, compiler_params=_params(1),
    )(proj, proj, ydn, ysb, dm)


def dn_out_fwd(o, proj, gain):
    tp = o.shape[0]
    rt, cb, wide = _row_tile(tp), DN_DV, 1024
    zb = C_DZ * 1024 // wide

    def body(o_ref, z_ref, g_ref, y_ref):
        for s in range(wide // cb):
            sl = slice(s * cb, (s + 1) * cb)
            x = o_ref[:, sl]
            r = lax.rsqrt(jnp.mean(x * x, axis=-1, keepdims=True) + RMS_EPS)
            z = z_ref[:, sl]
            y_ref[:, sl] = (x * r * g_ref[...] * (z * _sigmoid(z))).astype(bf16)

    blk = pl.BlockSpec((rt, wide), lambda i, j: (i, j))
    return pl.pallas_call(
        body, name="dn_out_fwd", grid=(tp // rt, DN_V // wide),
        in_specs=[blk, pl.BlockSpec((rt, wide), lambda i, j: (i, j + zb)), pl.BlockSpec((1, cb), lambda i, j: (0, 0))],
        out_specs=blk, out_shape=jax.ShapeDtypeStruct((tp, DN_V), bf16), compiler_params=_params(2),
    )(o, proj, gain)


def dn_out_bwd(o, proj, gain, dy):
    tp = o.shape[0]
    rt, cb, wide = _row_tile(tp), DN_DV, 1024
    zb = C_DZ * 1024 // wide

    def body(o_ref, z_ref, g_ref, dy_ref, do_ref, dz_ref, dg_ref):
        i, j = pl.program_id(0), pl.program_id(1)
        g = g_ref[...]
        part = jnp.zeros((1, cb), f32)
        for hh in range(wide // cb):
            sl = slice(hh * cb, (hh + 1) * cb)
            x = o_ref[:, sl]
            r = lax.rsqrt(jnp.mean(x * x, axis=-1, keepdims=True) + RMS_EPS)
            xh = x * r
            z = z_ref[:, sl]
            s = _sigmoid(z)
            dy_ = dy_ref[:, sl]
            drn = dy_ * (z * s)
            dz_ref[:, sl] = (dy_ * xh * g * s * (1.0 + z * (1.0 - s))).astype(bf16)
            dxh = drn * g
            do_ref[:, sl] = r * (dxh - xh * jnp.mean(dxh * xh, axis=-1, keepdims=True))
            part = part + jnp.sum(drn * xh, axis=0, keepdims=True)
        first = jnp.logical_and(i == 0, j == 0)

        @pl.when(first)
        def _():
            dg_ref[...] = part

        @pl.when(jnp.logical_not(first))
        def _():
            dg_ref[...] += part

    blk = pl.BlockSpec((rt, wide), lambda i, j: (i, j))
    vec = pl.BlockSpec((1, cb), lambda i, j: (0, 0))
    return pl.pallas_call(
        body, name="dn_out_bwd", grid=(tp // rt, DN_V // wide),
        in_specs=[blk, pl.BlockSpec((rt, wide), lambda i, j: (i, j + zb)), vec, blk],
        out_specs=[blk, blk, vec],
        out_shape=[jax.ShapeDtypeStruct((tp, DN_V), f32), jax.ShapeDtypeStruct((tp, DN_V), bf16),
                   jax.ShapeDtypeStruct((1, cb), f32)],
        compiler_params=_params(2),
    )(o, proj, gain, dy)


def sb_prep_fwd(proj, gq, gk):
    tp = proj.shape[0]
    rt, cb = _row_tile(tp), SB_DH

    def body(q_ref, k_ref, v_ref, gq_ref, gk_ref, qo_ref, ko_ref, vo_ref):
        for x_ref, g_ref, o_ref in ((q_ref, gq_ref, qo_ref), (k_ref, gk_ref, ko_ref)):
            for h in range(HEADS):
                sl = slice(h * cb, (h + 1) * cb)
                x = x_ref[:, sl]
                r = lax.rsqrt(jnp.mean(x * x, axis=-1, keepdims=True) + RMS_EPS)
                o_ref[:, sl] = (x * r * g_ref[...]).astype(bf16)
        vo_ref[...] = v_ref[...].astype(bf16)

    blk = pl.BlockSpec((rt, SB_W), lambda i: (i, 0))
    vec = pl.BlockSpec((1, cb), lambda i: (0, 0))
    return pl.pallas_call(
        body, name="sb_prep_fwd", grid=(tp // rt,),
        in_specs=[pl.BlockSpec((rt, SB_W), lambda i: (i, C_SQ)), pl.BlockSpec((rt, SB_W), lambda i: (i, C_SK)),
                  pl.BlockSpec((rt, SB_W), lambda i: (i, C_SV)), vec, vec],
        out_specs=[blk] * 3, out_shape=[jax.ShapeDtypeStruct((tp, SB_W), bf16)] * 3, compiler_params=_params(1),
    )(proj, proj, proj, gq, gk)


def sb_prep_bwd(proj, gq, gk, dqs, dks):
    tp = proj.shape[0]
    rt, cb = _row_tile(tp), SB_DH

    def body(q_ref, k_ref, gq_ref, gk_ref, dq_ref, dk_ref, dqo_ref, dko_ref, dgq_ref, dgk_ref):
        first = pl.program_id(0) == 0
        for x_ref, g_ref, dn_ref, dx_ref, dg_ref in ((q_ref, gq_ref, dq_ref, dqo_ref, dgq_ref),
                                                     (k_ref, gk_ref, dk_ref, dko_ref, dgk_ref)):
            part = jnp.zeros((1, cb), f32)
            for h in range(HEADS):
                sl = slice(h * cb, (h + 1) * cb)
                x = x_ref[:, sl]
                r = lax.rsqrt(jnp.mean(x * x, axis=-1, keepdims=True) + RMS_EPS)
                xh = x * r
                dn_ = dn_ref[:, sl]
                dxh = dn_ * g_ref[...]
                dx_ref[:, sl] = (r * (dxh - xh * jnp.mean(dxh * xh, axis=-1, keepdims=True))).astype(bf16)
                part = part + jnp.sum(dn_ * xh, axis=0, keepdims=True)

            @pl.when(first)
            def _(dg_ref=dg_ref, part=part):
                dg_ref[...] = part

            @pl.when(jnp.logical_not(first))
            def _(dg_ref=dg_ref, part=part):
                dg_ref[...] += part

    blk = pl.BlockSpec((rt, SB_W), lambda i: (i, 0))
    vec = pl.BlockSpec((1, cb), lambda i: (0, 0))
    return pl.pallas_call(
        body, name="sb_prep_bwd", grid=(tp // rt,),
        in_specs=[pl.BlockSpec((rt, SB_W), lambda i: (i, C_SQ)), pl.BlockSpec((rt, SB_W), lambda i: (i, C_SK)),
                  vec, vec, blk, blk],
        out_specs=[blk, blk, vec, vec],
        out_shape=[jax.ShapeDtypeStruct((tp, SB_W), bf16)] * 2 + [jax.ShapeDtypeStruct((1, cb), f32)] * 2,
        compiler_params=_params(1),
    )(proj, proj, gq, gk, dqs, dks)


def cast_bf16(x, name):
    tp, d = x.shape
    rt = _row_tile(tp)
    blk = pl.BlockSpec((rt, d), lambda i: (i, 0))

    def body(x_ref, o_ref):
        o_ref[...] = x_ref[...].astype(bf16)

    return pl.pallas_call(body, name=name, grid=(tp // rt,), in_specs=[blk], out_specs=blk,
                          out_shape=jax.ShapeDtypeStruct((tp, d), bf16), compiler_params=_params(1))(x)


def _conv_taps(ext, rt):
    taps = []
    for k in range(DN_CONV):
        s = DN_CONV - 1 - k
        taps.append((pltpu.roll(ext, s, axis=0) if s else ext)[SUB:SUB + rt])
    return taps


def _conv_act(taps, w, l2):
    y = taps[0] * w[0:1]
    for k in range(1, DN_CONV):
        y = y + taps[k] * w[k:k + 1]
    s = _sigmoid(y)
    a = y * s
    if l2:
        n = lax.rsqrt(jnp.sum(a * a, axis=-1, keepdims=True) + L2_EPS)
        return y, s, a, n
    return y, s, a, None


def conv_fwd(proj, w8, col_blk, ncb, l2, name):
    tp = proj.shape[0]
    rt = _row_tile(tp)
    hb = rt // SUB
    cw = CONV_W
    cb0 = col_blk * LANE // cw

    def body(x_ref, h_ref, w_ref, o_ref):
        i = pl.program_id(1)
        first = (i > 0).astype(f32)
        for s in range(cw // LANE):
            sl = slice(s * LANE, (s + 1) * LANE)
            ext = jnp.concatenate([h_ref[:, sl] * first, x_ref[:, sl]], axis=0)
            _, _, a, n = _conv_act(_conv_taps(ext, rt), w_ref[:, sl], l2)
            o_ref[:, sl] = a * n if l2 else a

    return pl.pallas_call(
        body, name=name, grid=(ncb * LANE // cw, tp // rt),
        in_specs=[pl.BlockSpec((rt, cw), lambda j, i: (i, j + cb0)),
                  pl.BlockSpec((SUB, cw), lambda j, i: (jnp.maximum(i * hb - 1, 0), j + cb0)),
                  pl.BlockSpec((SUB, cw), lambda j, i: (0, j))],
        out_specs=pl.BlockSpec((rt, cw), lambda j, i: (i, j)),
        out_shape=jax.ShapeDtypeStruct((tp, ncb * LANE), f32), compiler_params=_params(2),
    )(proj, proj, w8)


def conv_bwd_act(proj, w8, dout, col_blk, ncb, l2, name):
    tp = proj.shape[0]
    rt = _row_tile(tp)
    hb = rt // SUB
    cw = CONV_W
    cb0 = col_blk * LANE // cw

    def body(x_ref, h_ref, w_ref, d_ref, dy_ref, dw_ref):
        i = pl.program_id(1)
        first = (i > 0).astype(f32)
        rows = lax.broadcasted_iota(jnp.int32, (SUB, LANE), 0)
        for s in range(cw // LANE):
            sl = slice(s * LANE, (s + 1) * LANE)
            ext = jnp.concatenate([h_ref[:, sl] * first, x_ref[:, sl]], axis=0)
            taps = _conv_taps(ext, rt)
            y, sg, a, n = _conv_act(taps, w_ref[:, sl], l2)
            da = d_ref[:, sl]
            if l2:
                out = a * n
                da = n * (da - out * jnp.sum(da * out, axis=-1, keepdims=True))
            dy = da * sg * (1.0 + y * (1.0 - sg))
            dy_ref[:, sl] = dy
            part = jnp.zeros((SUB, LANE), f32)
            for k in range(DN_CONV):
                part = part + jnp.where(rows == k, jnp.sum(taps[k] * dy, axis=0, keepdims=True), 0.0)

            @pl.when(i == 0)
            def _(sl=sl, part=part):
                dw_ref[:, sl] = part

            @pl.when(i > 0)
            def _(sl=sl, part=part):
                dw_ref[:, sl] += part

    return pl.pallas_call(
        body, name=name, grid=(ncb * LANE // cw, tp // rt),
        in_specs=[pl.BlockSpec((rt, cw), lambda j, i: (i, j + cb0)),
                  pl.BlockSpec((SUB, cw), lambda j, i: (jnp.maximum(i * hb - 1, 0), j + cb0)),
                  pl.BlockSpec((SUB, cw), lambda j, i: (0, j)),
                  pl.BlockSpec((rt, cw), lambda j, i: (i, j))],
        out_specs=[pl.BlockSpec((rt, cw), lambda j, i: (i, j)), pl.BlockSpec((SUB, cw), lambda j, i: (0, j))],
        out_shape=[jax.ShapeDtypeStruct((tp, ncb * LANE), f32), jax.ShapeDtypeStruct((SUB, ncb * LANE), f32)],
        compiler_params=_params(2),
    )(proj, proj, w8, dout)


def conv_bwd_in(dy, w8, name):
    tp, cols = dy.shape
    rt = _row_tile(tp)
    hb = rt // SUB
    nr = tp // rt
    last8 = tp // SUB - 1
    cw = CONV_W

    def body(d_ref, h_ref, w_ref, o_ref):
        i = pl.program_id(1)
        last = (i < nr - 1).astype(f32)
        for c0 in range(cw // LANE):
            sl = slice(c0 * LANE, (c0 + 1) * LANE)
            ext = jnp.concatenate([d_ref[:, sl], h_ref[:, sl] * last], axis=0)
            w = w_ref[:, sl]
            acc = None
            for k in range(DN_CONV):
                s = DN_CONV - 1 - k
                sh = (pltpu.roll(ext, rt + SUB - s, axis=0) if s else ext)[0:rt]
                term = sh * w[k:k + 1]
                acc = term if acc is None else acc + term
            o_ref[:, sl] = acc.astype(bf16)

    return pl.pallas_call(
        body, name=name, grid=(cols // cw, nr),
        in_specs=[pl.BlockSpec((rt, cw), lambda j, i: (i, j)),
                  pl.BlockSpec((SUB, cw), lambda j, i: (jnp.minimum((i + 1) * hb, last8), j)),
                  pl.BlockSpec((SUB, cw), lambda j, i: (0, j))],
        out_specs=pl.BlockSpec((rt, cw), lambda j, i: (i, j)),
        out_shape=jax.ShapeDtypeStruct((tp, cols), bf16), compiler_params=_params(2),
    )(dy, dy, w8)


def _ab_common(p, al, dtb, r0):
    rows = r0 + lax.broadcasted_iota(jnp.int32, p.shape, 0)
    mask = (rows >= P0).astype(f32)
    xx = p + dtb
    sp = jnp.maximum(xx, 0.0) + _log1p_small(jnp.exp(-jnp.abs(xx)))
    ea = jnp.exp(al)
    g = -ea * sp * mask
    beta = _sigmoid(p) * mask
    return g, beta, _sigmoid(xx), ea, mask


def ab_fwd(pab, al, dtb):
    tp = pab.shape[0]
    rt = _row_tile(tp)

    def body(p_ref, al_ref, dt_ref, g_ref, b_ref):
        i = pl.program_id(0)
        g, beta, _, _, _ = _ab_common(p_ref[...], al_ref[...], dt_ref[...], i * rt)
        for h in range(HEADS):
            g_ref[h] = jnp.broadcast_to(g[:, h:h + 1], (rt, LANE))
            b_ref[h] = jnp.broadcast_to(beta[:, HEADS + h:HEADS + h + 1], (rt, LANE))

    vec = pl.BlockSpec((1, LANE), lambda i: (0, 0))
    out = pl.BlockSpec((HEADS, rt, LANE), lambda i: (0, i, 0))
    return pl.pallas_call(
        body, name="ab_fwd", grid=(tp // rt,), in_specs=[pl.BlockSpec((rt, LANE), lambda i: (i, 0)), vec, vec],
        out_specs=[out, out], out_shape=[jax.ShapeDtypeStruct((HEADS, tp, LANE), f32)] * 2, compiler_params=_params(1),
    )(pab, al, dtb)


def ab_bwd(pab, al, dtb, dg, db):
    tp = pab.shape[0]
    rt = _row_tile(tp)

    def body(p_ref, al_ref, dt_ref, dg_ref, db_ref, dp_ref, dal_ref, ddt_ref):
        i = pl.program_id(0)
        g, beta, sx, ea, mask = _ab_common(p_ref[...], al_ref[...], dt_ref[...], i * rt)
        lanes = lax.broadcasted_iota(jnp.int32, (rt, LANE), 1)
        dgl = jnp.zeros((rt, LANE), f32)
        dbl = jnp.zeros((rt, LANE), f32)
        for h in range(HEADS):
            dgl = dgl + jnp.where(lanes == h, dg_ref[h], 0.0)
            dbl = dbl + jnp.where(lanes == HEADS + h, db_ref[h], 0.0)
        dxx = dgl * (-ea) * sx * mask
        dp_ref[...] = (dxx + dbl * beta * (1.0 - beta)).astype(bf16)
        pal = jnp.sum(dgl * g, axis=0, keepdims=True)
        pdt = jnp.sum(dxx, axis=0, keepdims=True)

        @pl.when(i == 0)
        def _():
            dal_ref[...] = pal
            ddt_ref[...] = pdt

        @pl.when(i > 0)
        def _():
            dal_ref[...] += pal
            ddt_ref[...] += pdt

    vec = pl.BlockSpec((1, LANE), lambda i: (0, 0))
    row = pl.BlockSpec((rt, LANE), lambda i: (i, 0))
    big = pl.BlockSpec((HEADS, rt, LANE), lambda i: (0, i, 0))
    return pl.pallas_call(
        body, name="ab_bwd", grid=(tp // rt,), in_specs=[row, vec, vec, big, big], out_specs=[row, vec, vec],
        out_shape=[jax.ShapeDtypeStruct((tp, LANE), bf16), jax.ShapeDtypeStruct((1, LANE), f32),
                   jax.ShapeDtypeStruct((1, LANE), f32)],
        compiler_params=_params(1),
    )(pab, al, dtb, dg, db)


class _Chunk:
    pass


def _gdn_chunk(q, k, v, gcol, bcol, grow8):
    C = CHUNK
    R = range(len(q))
    X = _Chunk()
    ri = lax.broadcasted_iota(jnp.int32, (C, C), 0)
    ci = lax.broadcasted_iota(jnp.int32, (C, C), 1)
    r2 = lax.broadcasted_iota(jnp.int32, (LANE, LANE), 0)
    c2 = lax.broadcasted_iota(jnp.int32, (LANE, LANE), 1)
    lower = (ri >= ci).astype(f32)
    upper2 = (r2 <= c2).astype(f32)
    eye = (ri == ci).astype(f32)
    gam = [_hdot(lower, gcol[h]) for h in R]
    gam_row = [_hdot(grow8[h], upper2)[0:1, 0:C] for h in R]
    X.ri, X.ci = ri, ci
    X.Dm = [jnp.where(ri >= ci, jnp.exp(jnp.minimum(gam[h][:, 0:C] - gam_row[h], 0.0)), 0.0) for h in R]
    X.eg = [jnp.exp(gam[h]) for h in R]
    gl = [gam[h][C - 1:C, :] for h in R]
    X.egl = [jnp.exp(gl[h]) for h in R]
    X.kdec = [jnp.exp(gl[h] - gam[h]) for h in R]
    X.qs = [q[h] * (DN_DK ** -0.5) for h in R]
    X.kb = [k[h] * bcol[h] for h in R]
    kk = [_dot_nt(X.kb[h], k[h]) for h in R]
    qk = [_dot_nt(X.qs[h], k[h]) for h in R]
    X.A = [jnp.where(ri > ci, kk[h] * X.Dm[h], 0.0) for h in R]
    T = [eye - X.A[h] for h in R]
    P = list(X.A)
    for _ in range(5):
        P = [_hdot(P[h], P[h]) for h in R]
        T = [T[h] + _hdot(T[h], P[h]) for h in R]
    X.T = T
    X.b2 = [jnp.concatenate([bcol[h], bcol[h]], axis=-1) for h in R]
    X.u = [_hdot(T[h], v[h] * X.b2[h]) for h in R]
    X.w = [_hdot(T[h], X.kb[h] * X.eg[h]) for h in R]
    X.attn = [qk[h] * X.Dm[h] for h in R]
    X.qg = [X.qs[h] * X.eg[h] for h in R]
    X.kg = [k[h] * X.kdec[h] for h in R]
    return X


def gdn_fwd(q, k, v, gc, bc, grow):
    tp = q.shape[0]
    nc = tp // CHUNK
    hb = GDN_HEADS_PER_STEP

    def body(q_ref, k_ref, v_ref, gc_ref, bc_ref, gr_ref, o_ref, ss_ref, S_ref):
        c = pl.program_id(1)

        @pl.when(c == 0)
        def _():
            S_ref[...] = jnp.zeros_like(S_ref)

        R = range(hb)
        qc = [slice(h * DN_DK, (h + 1) * DN_DK) for h in R]
        vc = [slice(h * DN_DV, (h + 1) * DN_DV) for h in R]
        X = _gdn_chunk([q_ref[:, qc[h]] for h in R], [k_ref[:, qc[h]] for h in R], [v_ref[:, vc[h]] for h in R],
                       [gc_ref[h] for h in R], [bc_ref[h] for h in R], [gr_ref[h] for h in R])
        S = [S_ref[h] for h in R]
        for h in R:
            ss_ref[h, 0] = S[h]
        wS = [_dot(X.w[h], S[h]) for h in R]
        qS = [_dot(X.qg[h], S[h]) for h in R]
        vn = [X.u[h] - wS[h] for h in R]
        av = [_dot(X.attn[h], vn[h]) for h in R]
        kv = [_dot_tn(X.kg[h], vn[h]) for h in R]
        for h in R:
            o_ref[:, vc[h]] = qS[h] + av[h]
            S_ref[h] = S[h] * X.egl[h][:, 0:1] + kv[h]

    qk = pl.BlockSpec((CHUNK, hb * DN_DK), lambda g, c: (c, g))
    vv = pl.BlockSpec((CHUNK, hb * DN_DV), lambda g, c: (c, g))
    col = pl.BlockSpec((hb, CHUNK, LANE), lambda g, c: (g, c, 0))
    row = pl.BlockSpec((hb, SUB, LANE), lambda g, c: (g, c, 0))
    return pl.pallas_call(
        body, name="gdn_fwd", grid=(HEADS // hb, nc), in_specs=[qk, qk, vv, col, col, row],
        out_specs=[vv, pl.BlockSpec((hb, 1, DN_DK, DN_DV), lambda g, c: (g, c, 0, 0))],
        out_shape=[jax.ShapeDtypeStruct((tp, DN_V), f32), jax.ShapeDtypeStruct((HEADS, nc, DN_DK, DN_DV), f32)],
        scratch_shapes=[pltpu.VMEM((hb, DN_DK, DN_DV), f32)], compiler_params=_params(2),
    )(q, k, v, gc, bc, grow)


def gdn_bwd(q, k, v, gc, bc, grow, states, do):
    tp = q.shape[0]
    nc = tp // CHUNK
    C = CHUNK
    hb = GDN_HEADS_PER_STEP

    def body(q_ref, k_ref, v_ref, gc_ref, bc_ref, gr_ref, ss_ref, do_ref, dq_ref, dk_ref, dv_ref, dg_ref, db_ref, dS_ref):
        c = pl.program_id(1)

        @pl.when(c == 0)
        def _():
            dS_ref[...] = jnp.zeros_like(dS_ref)

        R = range(hb)
        qc = [slice(h * DN_DK, (h + 1) * DN_DK) for h in R]
        vc = [slice(h * DN_DV, (h + 1) * DN_DV) for h in R]
        k_ = [k_ref[:, qc[h]] for h in R]
        v_ = [v_ref[:, vc[h]] for h in R]
        bcol = [bc_ref[h] for h in R]
        X = _gdn_chunk([q_ref[:, qc[h]] for h in R], k_, v_, [gc_ref[h] for h in R], bcol, [gr_ref[h] for h in R])
        ri, ci = X.ri, X.ci
        S = [ss_ref[h, 0] for h in R]
        do_ = [do_ref[:, vc[h]] for h in R]
        dSn = [dS_ref[h] for h in R]
        wS = [_dot(X.w[h], S[h]) for h in R]
        ado = [_dot_tn(X.attn[h], do_[h]) for h in R]
        kdS = [_dot(X.kg[h], dSn[h]) for h in R]
        d_qg = [_dot_nt(do_[h], S[h]) for h in R]
        qdo = [_dot_tn(X.qg[h], do_[h]) for h in R]
        vn = [X.u[h] - wS[h] for h in R]
        d_vn = [ado[h] + kdS[h] for h in R]
        dovn = [_dot_nt(do_[h], vn[h]) for h in R]
        d_kg = [_dot_nt(vn[h], dSn[h]) for h in R]
        wdv = [_dot_tn(X.w[h], d_vn[h]) for h in R]
        dw = [-_dot_nt(d_vn[h], S[h]) for h in R]
        for h in R:
            dS_ref[h] = qdo[h] + X.egl[h][:, 0:1] * dSn[h] - wdv[h]
        dattn = [jnp.where(ri >= ci, dovn[h], 0.0) for h in R]
        dRu = [_hdot_tn(X.T[h], d_vn[h]) for h in R]
        dRw = [_hdot_tn(X.T[h], dw[h]) for h in R]
        dAu = [_hdot_nt(dRu[h], X.u[h]) for h in R]
        dAw = [_hdot_nt(dRw[h], X.w[h]) for h in R]
        dA = [jnp.where(ri > ci, -(dAu[h] + dAw[h]), 0.0) for h in R]
        dKK = [dA[h] * X.Dm[h] for h in R]
        dQK = [dattn[h] * X.Dm[h] for h in R]
        E = [dA[h] * X.A[h] + dattn[h] * X.attn[h] for h in R]
        dkb = [_dot(dKK[h], k_[h]) + dRw[h] * X.eg[h] for h in R]
        dk1 = [_dot_tn(dKK[h], X.kb[h]) for h in R]
        dqs = [_dot(dQK[h], k_[h]) + d_qg[h] * X.eg[h] for h in R]
        dk2 = [_dot_tn(dQK[h], X.qs[h]) for h in R]
        ones = jnp.ones((C, LANE), f32)
        colE = [_hdot_tn(E[h], ones) for h in R]
        rows = lax.broadcasted_iota(jnp.int32, (C, LANE), 0)
        upper = (ci >= ri).astype(f32)
        dgam = []
        for h in R:
            t = d_kg[h] * X.kg[h]
            dgl = _allsum(t) + X.egl[h][:, 0:1] * _allsum(S[h] * dSn[h])
            g = (_rowsum(E[h]) - colE[h] + _rowsum(dRw[h] * (X.kb[h] * X.eg[h])) + _rowsum(d_qg[h] * X.qg[h])
                 - _rowsum(t))
            dgam.append(g + jnp.where(rows == C - 1, dgl, 0.0))
        dg = [_hdot(upper, dgam[h]) for h in R]
        for h in R:
            dv_ref[:, vc[h]] = dRu[h] * X.b2[h]
            dbeta = _rowsum(dRu[h] * v_[h]) + _rowsum(dkb[h] * k_[h])
            dq_ref[:, qc[h]] = dqs[h] * (DN_DK ** -0.5)
            dk_ref[:, qc[h]] = dk1[h] + dk2[h] + dkb[h] * bcol[h] + d_kg[h] * X.kdec[h]
            dg_ref[h] = dg[h]
            db_ref[h] = jnp.broadcast_to(dbeta, (C, LANE))

    rc = lambda c: nc - 1 - c
    qk = pl.BlockSpec((CHUNK, hb * DN_DK), lambda g, c: (rc(c), g))
    vv = pl.BlockSpec((CHUNK, hb * DN_DV), lambda g, c: (rc(c), g))
    col = pl.BlockSpec((hb, CHUNK, LANE), lambda g, c: (g, rc(c), 0))
    row = pl.BlockSpec((hb, SUB, LANE), lambda g, c: (g, rc(c), 0))
    st = pl.BlockSpec((hb, 1, DN_DK, DN_DV), lambda g, c: (g, rc(c), 0, 0))
    return pl.pallas_call(
        body, name="gdn_bwd", grid=(HEADS // hb, nc), in_specs=[qk, qk, vv, col, col, row, st, vv],
        out_specs=[qk, qk, vv, col, col],
        out_shape=[jax.ShapeDtypeStruct((tp, DN_QK), f32), jax.ShapeDtypeStruct((tp, DN_QK), f32),
                   jax.ShapeDtypeStruct((tp, DN_V), f32), jax.ShapeDtypeStruct((HEADS, tp, LANE), f32),
                   jax.ShapeDtypeStruct((HEADS, tp, LANE), f32)],
        scratch_shapes=[pltpu.VMEM((hb, DN_DK, DN_DV), f32)], compiler_params=_params(2),
    )(q, k, v, gc, bc, grow, states, do)


def _cumsum_after(x, nb, us):
    B = SB_BLOCK
    hi = x.astype(bf16)
    lo = (x - hi.astype(f32)).astype(bf16)
    rows = [p[:, b * B:(b + 1) * B] for p in (hi, lo) for b in range(nb)]
    r = jnp.dot(jnp.concatenate(rows, axis=0), us, preferred_element_type=f32)
    out = [r[b * B:(b + 1) * B] + r[(nb + b) * B:(nb + b + 1) * B] for b in range(nb)]
    return out[0] if nb == 1 else jnp.concatenate(out, axis=1)


def _later_blocks(x, nb, carry):
    B = SB_BLOCK
    tot = [_rowsum(x[:, b * B:(b + 1) * B]) for b in range(nb)]
    offs = [None] * nb
    run = carry
    for b in range(nb - 1, -1, -1):
        offs[b] = jnp.broadcast_to(run, (B, B))
        run = run + tot[b]
    return (offs[0] if nb == 1 else jnp.concatenate(offs, axis=1)), run


def _sb_group(i, t):
    top = i - SB_GROUP * t
    jlo = jnp.maximum(top - SB_GROUP + 1, 0)
    rows = pl.ds(pl.multiple_of(jlo * SB_BLOCK, SB_BLOCK), SB_GROUP * SB_BLOCK)
    return jlo, rows, (top + 1) * SB_BLOCK


def _sb_weights(q, kcat, i, jlo, kend, cs, us, masked):
    B, nb = SB_BLOCK, SB_GROUP
    R = range(len(q))
    z = [_dot_nt(q[h], kcat[h]) * (SB_DH ** -0.5) for h in R]
    e = [jnp.exp(-jnp.abs(z[h])) for h in R]
    l1p = [jnp.log(1.0 + e[h]) for h in R]
    lsp = [jnp.minimum(z[h], 0.0) - l1p[h] for h in R]
    lk = [lsp[h] - z[h] for h in R]
    vis = None
    if masked:
        qpos = i * B + lax.broadcasted_iota(jnp.int32, (B, nb * B), 0)
        kpos = jlo * B + lax.broadcasted_iota(jnp.int32, (B, nb * B), 1)
        vis = jnp.logical_and(kpos < jnp.minimum(qpos, kend), kpos >= P0)
        lk = [jnp.where(vis, lk[h], 0.0) for h in R]
    later = [_later_blocks(lk[h], nb, cs[h]) for h in R]
    cum = [_cumsum_after(lk[h], nb, us) for h in R]
    w = [jnp.exp(lsp[h] + cum[h] + later[h][0]) for h in R]
    if masked:
        w = [jnp.where(vis, w[h], 0.0) for h in R]
    return lsp, vis, w, [later[h][1] for h in R]


def _sb_loop(i, step, carry):
    trips = (i + SB_GROUP) // SB_GROUP
    carry = step(True)(0, carry)
    carry = lax.fori_loop(1, trips - 1, step(False), carry)
    return lax.fori_loop(jnp.maximum(trips - 1, 1), trips, step(True), carry)


def _ride(rider, n_in, n_out, grid):
    n_rin = len(rider.ins) if rider else 0
    n_rout = len(rider.out_shapes) if rider else 0

    def split(refs):
        ins, rin = refs[:n_in], refs[n_in:n_in + n_rin]
        outs = refs[n_in + n_rin:n_in + n_rin + n_out]
        rout = refs[n_in + n_rin + n_out:n_in + n_rin + n_out + n_rout]
        return ins, outs, (rin, rout, refs[n_in + n_rin + n_out + n_rout:])

    def at(step, fn, r):
        if rider is None:
            return
        cond = None
        for a, g in enumerate(grid):
            c = pl.program_id(a) == (g - 1 if step == "last" else 0)
            cond = c if cond is None else jnp.logical_and(cond, c)

        @pl.when(cond)
        def _():
            fn(*r)

    first = lambda r: at("first", rider.start if rider else None, r)
    last = lambda r: at("last", rider.finish if rider else None, r)
    return split, first, last


def sb_fwd(qs, ks, vs, rider=None):
    tp = qs.shape[0]
    nq = tp // SB_BLOCK
    B, G, hb = SB_BLOCK, SB_GROUP, SB_HEADS_PER_STEP
    assert tp >= G * B
    grid = (HEADS // hb, nq)
    split, ride_first, ride_last = _ride(rider, 3, 1, grid)

    def body(*refs):
        (q_ref, k_ref, v_ref), (o_ref,), ride = split(refs)
        ride_first(ride)
        i = pl.program_id(1)
        R = range(hb)
        hs = [slice(h * SB_DH, (h + 1) * SB_DH) for h in R]
        q = [q_ref[:, hs[h]] for h in R]
        us = (lax.broadcasted_iota(jnp.int32, (B, B), 0) > lax.broadcasted_iota(jnp.int32, (B, B), 1)).astype(bf16)

        def make_step(masked):
            def step(t, carry):
                acc, cs = carry
                jlo, rows, kend = _sb_group(i, t)
                _, _, w, cs = _sb_weights(q, [k_ref[rows, hs[h]] for h in R], i, jlo, kend, cs, us, masked)
                pv = [_dot(w[h], v_ref[rows, hs[h]]) for h in R]
                return tuple(acc[h] + pv[h] for h in R), tuple(cs)
            return step

        carry = (tuple(jnp.zeros((B, SB_DH), f32) for _ in R), tuple(jnp.zeros((B, 1), f32) for _ in R))
        acc, _ = _sb_loop(i, make_step, carry)
        for h in R:
            o_ref[:, hs[h]] = acc[h]
        ride_last(ride)

    blk = pl.BlockSpec((B, hb * SB_DH), lambda g, i: (i, g))
    full = pl.BlockSpec((tp, hb * SB_DH), lambda g, i: (0, g))
    r_ins = rider.ins if rider else []
    r_outs = rider.out_shapes if rider else []
    res = pl.pallas_call(
        body, name="sb_fwd", grid=grid, in_specs=[blk, full, full] + [_ANY] * len(r_ins),
        out_specs=[blk] + [_ANY] * len(r_outs),
        out_shape=[jax.ShapeDtypeStruct((tp, SB_W), f32)] + list(r_outs),
        scratch_shapes=rider.scratch if rider else [], compiler_params=_params(2),
    )(qs, ks, vs, *r_ins)
    return res[0], res[1:]


def sb_bwd(qs, ks, vs, o, do, rider=None):
    tp = qs.shape[0]
    nq = tp // SB_BLOCK
    B, G, hb = SB_BLOCK, SB_GROUP, SB_HEADS_PER_STEP
    assert tp >= G * B
    grid = (HEADS // hb, nq)
    split, ride_first, ride_last = _ride(rider, 5, 3, grid)

    def body(*refs):
        (q_ref, k_ref, v_ref, o_ref, do_ref), (dq_ref, dk_ref, dv_ref), ride = split(refs)
        ride_first(ride)
        i = pl.program_id(1)

        @pl.when(i == 0)
        def _():
            dk_ref[...] = jnp.zeros_like(dk_ref)
            dv_ref[...] = jnp.zeros_like(dv_ref)

        R = range(hb)
        hs = [slice(h * SB_DH, (h + 1) * SB_DH) for h in R]
        q = [q_ref[:, hs[h]] for h in R]
        dob = [do_ref[:, hs[h]].astype(bf16) for h in R]
        et = [_rowsum(dob[h].astype(f32) * o_ref[:, hs[h]]) for h in R]
        us = (lax.broadcasted_iota(jnp.int32, (B, B), 0) > lax.broadcasted_iota(jnp.int32, (B, B), 1)).astype(bf16)

        def make_step(masked):
            def step(t, carry):
                dq, cs, ce = carry
                jlo, rows, kend = _sb_group(i, t)
                kcat = [k_ref[rows, hs[h]] for h in R]
                dwv = [_dot_nt(dob[h], v_ref[rows, hs[h]]) for h in R]
                lsp, vis, w, cs = _sb_weights(q, kcat, i, jlo, kend, cs, us, masked)
                wb = [w[h].astype(bf16) for h in R]
                ee = [dwv[h] * wb[h].astype(f32) for h in R]
                later = [_later_blocks(ee[h], G, ce[h]) for h in R]
                cum = [_cumsum_after(ee[h], G, us) for h in R]
                dz = []
                for h in R:
                    d = ee[h] - jnp.exp(lsp[h]) * (et[h] - (cum[h] + later[h][0]))
                    if masked:
                        d = jnp.where(vis, d, 0.0)
                    dz.append((d * (SB_DH ** -0.5)).astype(bf16))
                dkj = [_dot_tn(dz[h], q[h]) for h in R]
                dvj = [_dot_tn(wb[h], dob[h]) for h in R]
                dqj = [_dot(dz[h], kcat[h]) for h in R]
                for h in R:
                    dk_ref[rows, hs[h]] += dkj[h]
                    dv_ref[rows, hs[h]] += dvj[h]
                return tuple(dq[h] + dqj[h] for h in R), tuple(cs), tuple(later[h][1] for h in R)
            return step

        z0 = tuple(jnp.zeros((B, 1), f32) for _ in R)
        dq, _, _ = _sb_loop(i, make_step, (tuple(jnp.zeros((B, SB_DH), f32) for _ in R), z0, z0))
        for h in R:
            dq_ref[:, hs[h]] = dq[h]
        ride_last(ride)

    blk = pl.BlockSpec((B, hb * SB_DH), lambda g, i: (i, g))
    full = pl.BlockSpec((tp, hb * SB_DH), lambda g, i: (0, g))
    r_ins = rider.ins if rider else []
    r_outs = rider.out_shapes if rider else []
    res = pl.pallas_call(
        body, name="sb_bwd", grid=grid, in_specs=[blk, full, full, blk, blk] + [_ANY] * len(r_ins),
        out_specs=[blk, full, full] + [_ANY] * len(r_outs),
        out_shape=[jax.ShapeDtypeStruct((tp, SB_W), f32)] * 3 + list(r_outs),
        scratch_shapes=rider.scratch if rider else [], compiler_params=_params(2),
    )(qs, ks, vs, o, do, *r_ins)
    return res[:3], res[3:]


def adamw(w, g, m, v, name):
    r, c = w.shape
    rt = _tile(r, 128, SUB) if r % SUB == 0 else r
    blk = pl.BlockSpec((rt, c), lambda i: (i, 0))
    c1 =1.0 - ADAM_B1 ** ADAM_STEP
    c2 = 1.0 - ADAM_B2 ** ADAM_STEP

    def body(w_ref, g_ref, m_ref, v_ref, d_ref, mo_ref, vo_ref):
        g_ = g_ref[...]
        m_ = ADAM_B1 * m_ref[...] + (1.0 - ADAM_B1) * g_
        v_ = ADAM_B2 * v_ref[...] + (1.0 - ADAM_B2) * (g_ * g_)
        mo_ref[...] = m_
        vo_ref[...] = v_
        d_ref[...] = -ADAM_LR * ((m_ / c1) / (jnp.sqrt(v_ / c2) + ADAM_EPS) + ADAM_WD * w_ref[...])

    return pl.pallas_call(
        body, name=name, grid=(r // rt,), in_specs=[blk] * 4, out_specs=[blk] * 3,
        out_shape=[jax.ShapeDtypeStruct((r, c), f32)] * 3, compiler_params=_params(1),
    )(w, g, m, v)


def sum_slots(x, name):
    n, r, c = x.shape
    rt = _tile(r, 128, SUB) if r % SUB == 0 else r
    blk = pl.BlockSpec((n, rt, c), lambda i: (0, i, 0))

    def body(x_ref, o_ref):
        acc = x_ref[0]
        for s in range(1, n):
            acc = acc + x_ref[s]
        o_ref[...] = acc

    return pl.pallas_call(
        body, name=name, grid=(r // rt,), in_specs=[blk], out_specs=pl.BlockSpec((rt, c), lambda i: (i, 0)),
        out_shape=jax.ShapeDtypeStruct((r, c), f32), compiler_params=_params(1),
    )(x)


def add2(a, b, name):
    n, r, c = a.shape
    rt = _tile(r, 64, SUB) if r % SUB == 0 else r
    blk = pl.BlockSpec((n, rt, c), lambda i: (0, i, 0))

    def body(a_ref, b_ref, o_ref):
        o_ref[...] = a_ref[...] + b_ref[...]

    return pl.pallas_call(
        body, name=name, grid=(r // rt,), in_specs=[blk, blk], out_specs=blk,
        out_shape=jax.ShapeDtypeStruct((n, r, c), f32), compiler_params=_params(1),
    )(a, b)


_ANY = pl.BlockSpec(memory_space=pl.ANY)
_MESH = pl.DeviceIdType.MESH


def _coords():
    return lax.axis_index("x"), lax.axis_index("y"), lax.axis_index("c")


def _chip_peer(x, y, r):
    return x ^ (r >> 1), y ^ (r & 1)


class _Exchange:
    def __init__(self, ins, out_shapes, scratch, start, finish):
        self.ins, self.out_shapes, self.scratch, self.start, self.finish = ins, out_shapes, scratch, start, finish

    def split(self, refs):
        n, m = len(self.ins), len(self.out_shapes)
        return refs[:n], refs[n:n + m], refs[n + m:]


def run_exchange(ex, name):
    def body(*refs):
        ins, outs, sems = ex.split(refs)
        ex.start(ins, outs, sems)
        ex.finish(ins, outs, sems)

    return pl.pallas_call(body, name=name, in_specs=[_ANY] * len(ex.ins), out_specs=[_ANY] * len(ex.out_shapes),
                          out_shape=ex.out_shapes, scratch_shapes=ex.scratch)(*ex.ins)


def gather_chips(big, small):
    nb, n = len(big), len(big) + len(small)
    shards = list(big) + list(small)
    kb = nb * (N_CHIPS - 1)
    k = n * (N_CHIPS - 1)

    def copies(src, dst, sems):
        send, recv, fsend, frecv = sems
        x, y, c = _coords()
        sib = (x, y, 1 - c)
        peers = [_chip_peer(x, y, r) for r in range(1, N_CHIPS)]

        def direct(t, j, slot):
            s = t * (N_CHIPS - 1) + j
            if t < nb:
                return pltpu.make_async_remote_copy(src[t].at[c], dst[t].at[slot, c], send.at[s], recv.at[s],
                                                    device_id=(*peers[j], c), device_id_type=_MESH)
            return pltpu.make_async_remote_copy(src[t], dst[t].at[slot], send.at[s], recv.at[s],
                                                device_id=(*peers[j], c), device_id_type=_MESH)

        def passed(t, j, half):
            s = t * (N_CHIPS - 1) + j
            px, py = peers[j]
            part = dst[t].at[2 * px + py, half]
            return pltpu.make_async_remote_copy(part, part, fsend.at[s], frecv.at[s], device_id=sib, device_id_type=_MESH)

        return direct, passed, peers, 2 * x + y, c

    def start(src, dst, sems):
        direct, _, _, me, _ = copies(src, dst, sems)
        for t in range(n):
            for j in range(N_CHIPS - 1):
                direct(t, j, me).start()

    def finish(src, dst, sems):
        direct, passed, peers, me, c = copies(src, dst, sems)
        fwd = []
        for t in range(nb):
            for j in range(N_CHIPS - 1):
                px, py = peers[j]
                direct(t, j, 2 * px + py).wait_recv()
                fwd.append(passed(t, j, c))
                fwd[-1].start()
        for t in range(nb, n):
            for j in range(N_CHIPS - 1):
                px, py = peers[j]
                direct(t, j, 2 * px + py).wait_recv()
        for t in range(nb):
            for j in range(N_CHIPS - 1):
                passed(t, j, 1 - c).wait_recv()
        for t in range(n):
            for j in range(N_CHIPS - 1):
                direct(t, j, me).wait_send()
        for cp in fwd:
            cp.wait_send()

    return _Exchange(shards, [jax.ShapeDtypeStruct((N_CHIPS,) + s.shape, s.dtype) for s in shards],
                     [pltpu.SemaphoreType.DMA((k,)), pltpu.SemaphoreType.DMA((k,)),
                      pltpu.SemaphoreType.DMA((max(kb, 1),)), pltpu.SemaphoreType.DMA((max(kb, 1),))], start, finish)


def sibling_swap_halves(grads, name):
    n = len(grads)

    def body(*refs):
        src, dst = refs[:n], refs[n:2 * n]
        send, recv = refs[2 * n:]
        x, y, c = _coords()
        cps = []
        for t in range(n):
            for o in range(N_CHIPS):
                s = t * N_CHIPS + o
                cps.append(pltpu.make_async_remote_copy(src[t].at[o, 1 - c], dst[t].at[o], send.at[s], recv.at[s],
                                                        device_id=(x, y, 1 - c), device_id_type=_MESH))
        for cp in cps:
            cp.start()
        for cp in cps:
            cp.wait_recv()
        for cp in cps:
            cp.wait_send()

    k = n * N_CHIPS
    return pl.pallas_call(
        body, name=name, in_specs=[_ANY] * n, out_specs=[_ANY] * n,
        out_shape=[jax.ShapeDtypeStruct((N_CHIPS,) + g.shape[2:], g.dtype) for g in grads],
        scratch_shapes=[pltpu.SemaphoreType.DMA((k,)), pltpu.SemaphoreType.DMA((k,))],
    )(*grads)


def scatter_chips(parts):
    n = len(parts)
    k = n * (N_CHIPS - 1)

    def copy(src, dst, sems, t, r, landing):
        send, recv = sems
        x, y, c = _coords()
        me = 2 * x + y
        px, py = _chip_peer(x, y, r)
        peer = 2 * px + py
        s = t * (N_CHIPS - 1) + r - 1
        return pltpu.make_async_remote_copy(src[t].at[me if landing else peer], dst[t].at[peer if landing else me],
                                            send.at[s], recv.at[s], device_id=(px, py, c), device_id_type=_MESH)

    def start(src, dst, sems):
        for t in range(n):
            for r in range(1, N_CHIPS):
                copy(src, dst, sems, t, r, False).start()

    def finish(src, dst, sems):
        for t in range(n):
            for r in range(1, N_CHIPS):
                copy(src, dst, sems, t, r, True).wait_recv()
        for t in range(n):
            for r in range(1, N_CHIPS):
                copy(src, dst, sems, t, r, False).wait_send()

    return _Exchange(list(parts), [jax.ShapeDtypeStruct(p.shape, p.dtype) for p in parts],
                     [pltpu.SemaphoreType.DMA((k,)), pltpu.SemaphoreType.DMA((k,))], start, finish)


def sibling_send(halves, name):
    n = len(halves)

    def body(*refs):
        src, dst = refs[:n], refs[n:2 * n]
        send, recv = refs[2 * n:]
        x, y, c = _coords()
        cps = [pltpu.make_async_remote_copy(src[t], dst[t], send.at[t], recv.at[t],
                                            device_id=(x, y, 1 - c), device_id_type=_MESH) for t in range(n)]
        for cp in cps:
            cp.start()
        for cp in cps:
            cp.wait_recv()
        for cp in cps:
            cp.wait_send()

    return pl.pallas_call(
        body, name=name, in_specs=[_ANY] * n, out_specs=[_ANY] * n,
        out_shape=[jax.ShapeDtypeStruct(h.shape, h.dtype) for h in halves],
        scratch_shapes=[pltpu.SemaphoreType.DMA((n,)), pltpu.SemaphoreType.DMA((n,))],
    )(*halves)


def gather_all(block, name):
    def body(src, dst, send, recv, loc):
        x, y, c = _coords()
        me = 4 * x + 2 * y + c
        mine = pltpu.make_async_copy(src, dst.at[me], loc)
        mine.start()
        outs = []
        for r in range(1, N_DEV):
            peer = (x ^ (r >> 2), y ^ ((r >> 1) & 1), c ^ (r & 1))
            outs.append(pltpu.make_async_remote_copy(src, dst.at[me], send.at[r - 1], recv.at[r - 1],
                                                     device_id=peer, device_id_type=_MESH))
        for cp in outs:
            cp.start()
        for r in range(1, N_DEV):
            px, py, pc = x ^ (r >> 2), y ^ ((r >> 1) & 1), c ^ (r & 1)
            pltpu.make_async_remote_copy(src, dst.at[4 * px + 2 * py + pc], send.at[r - 1], recv.at[r - 1],
                                         device_id=(px, py, pc), device_id_type=_MESH).wait_recv()
        for cp in outs:
            cp.wait_send()
        mine.wait()

    return pl.pallas_call(
        body, name=name, in_specs=[_ANY], out_specs=_ANY,
        out_shape=jax.ShapeDtypeStruct((N_DEV,) + block.shape, block.dtype),
        scratch_shapes=[pltpu.SemaphoreType.DMA((N_DEV - 1,)), pltpu.SemaphoreType.DMA((N_DEV - 1,)),
                        pltpu.SemaphoreType.DMA(())],
    )(block)


def _pad_lanes(v, n=LANE):
    return jnp.pad(v, ((0, 0), (0, n - v.shape[1])))


def _w_in_pieces():
    cs = (PROJ_BIG + 2 * HEADS) // N_CHIPS
    ab_end = AB_COL + 2 * HEADS
    out = []
    for o in range(N_CHIPS):
        lo, hi = o * cs, (o + 1) * cs
        cand = [("big", lo, min(hi, AB_COL), 0), ("ab", max(lo, AB_COL), min(hi, ab_end), AB_COL),
                ("big", max(lo, ab_end), hi, 2 * HEADS)]
        out.append([(s, a - off, b - off) for s, a, b, off in cand if a < b])
    return out


def _split_w_in(w4):
    big, ab = [], []
    for o, pieces in enumerate(_w_in_pieces()):
        at = 0
        for s, a, b in pieces:
            (big if s == "big" else ab).append(w4[o][:, at:at + b - a])
            at += b - a
    return jnp.concatenate(big, axis=1), _pad_lanes(jnp.concatenate(ab, axis=1))


def _join_w_in(big, ab):
    src = {"big": big, "ab": ab}
    return jnp.stack([jnp.concatenate([src[s][:, a:b] for s, a, b in pieces], axis=1) for pieces in _w_in_pieces()])


def _conv_w8(w):
    return jnp.pad(w, ((0, SUB - DN_CONV), (0, 0)))


def _row_layout(gc, tp):
    nc = tp // CHUNK
    g = gc[:, :, 0].reshape(HEADS, nc, 1, CHUNK)
    g = jnp.broadcast_to(g, (HEADS, nc, SUB, CHUNK))
    return jnp.pad(g, ((0, 0), (0, 0), (0, 0), (0, LANE - CHUNK))).reshape(HEADS, nc * SUB, LANE)


def _step(x, meta, W, target, late_weights=None, early_grads=None, last_grads=None):
    W = dict(W)
    seq = x.shape[0]
    tp = P0 + N_META + seq
    h0 = jnp.concatenate([jnp.zeros((P0, D_MODEL), f32), meta, x], axis=0)
    w_big, w_ab = _split_w_in(W["w_in"])
    cq8, ck8, cv8 = _conv_w8(W["conv_q"]), _conv_w8(W["conv_k"]), _conv_w8(W["conv_v"])
    al, dtb = _pad_lanes(W["dn_a_log"]), _pad_lanes(W["dn_dt_bias"])

    n1 = rms_fwd(h0, W["norm_mix_gain"], "rms1_fwd")
    proj = matmul(n1, w_big, "nn", "proj_fwd")
    pab = matmul(n1, w_ab, "nn", "pab_fwd")
    qn = conv_fwd(proj, cq8, C_DQ * 8, 8, True, "conv_q_fwd")
    kn = conv_fwd(proj, ck8, C_DK * 8, 8, True, "conv_k_fwd")
    va = conv_fwd(proj, cv8, C_DV * 8, 16, False, "conv_v_fwd")
    gc, bc = ab_fwd(pab, al, dtb)
    grow = _row_layout(gc, tp)
    o_dn, states = gdn_fwd(qn, kn, va, gc, bc, grow)
    on = dn_out_fwd(o_dn, proj, W["dn_out_norm_gain"])
    qs, ks, vs = sb_prep_fwd(proj, W["sb_q_norm_gain"], W["sb_k_norm_gain"])
    o_sb, arrived = sb_fwd(qs, ks, vs, rider=late_weights[0] if late_weights else None)
    if late_weights:
        W.update(late_weights[1](arrived))
    o_sb16 = cast_bf16(o_sb, "o_sb_cast")
    ydn = matmul(on, W["w_branch_dn"], "nn", "ydn_fwd")
    ysb = matmul(o_sb16, W["w_branch_sb"], "nn", "ysb_fwd")
    merged = merge_fwd(proj, ydn, ysb)
    h1 = matmul(merged, W["w_out"], "nn", "wout_fwd", residual=h0)
    n2 = rms_fwd(h1, W["norm_ffn_gain"], "rms2_fwd")
    u = matmul(n2, W["w_ffn_in"], "nn", "ffn_in_fwd", tn_t=512)
    act = swiglu_fwd(u)
    y = matmul(act, W["w_ffn_out"], "nn", "ffn_out_fwd", residual=h1)
    dy, loss = loss_head(y, target)

    G = {}
    dy16 = cast_bf16(dy, "dy_cast")
    dact = matmul(dy16, W["w_ffn_out"], "nt", "ffn_out_dx", tn_t=1408)
    G["w_ffn_out"] = matmul(act, dy16, "tn", "ffn_out_dw", tm_t=1408)
    dgate, dup = swiglu_bwd(u, dact)
    du = jnp.concatenate([dgate, dup], axis=1)
    dn2 = matmul(du, W["w_ffn_in"], "nt", "ffn_in_dx", tk_t=512)
    G["w_ffn_in"] = matmul(n2, du, "tn", "ffn_in_dw", tn_t=512)
    dh1, G["norm_ffn_gain"] = rms_bwd(h1, W["norm_ffn_gain"], dn2, dy, "rms2_bwd")
    dh1_16 = cast_bf16(dh1, "dh1_cast")
    dmerged = matmul(dh1_16, W["w_out"], "nt", "wout_dx")
    G["w_out"] = matmul(merged, dh1_16, "tn", "wout_dw")
    dyd, dys, dgd, dgs = merge_bwd(proj, ydn, ysb, dmerged)
    don = matmul(dyd, W["w_branch_dn"], "nt", "ydn_dx")
    G["w_branch_dn"] = matmul(on, dyd, "tn", "ydn_dw")
    do_sb = matmul(dys, W["w_branch_sb"], "nt", "ysb_dx")
    G["w_branch_sb"] = matmul(o_sb16, dys, "tn", "ysb_dw")
    do_dn, dz, G["dn_out_norm_gain"] = dn_out_bwd(o_dn, proj, W["dn_out_norm_gain"], don)
    dqn, dkn, dva, dgc, dbc = gdn_bwd(qn, kn, va, gc, bc, grow, states, do_dn)
    dpab, dal, ddt = ab_bwd(pab, al, dtb, dgc, dbc)
    G["dn_a_log"], G["dn_dt_bias"] = dal[:, :HEADS], ddt[:, :HEADS]
    dyq, dcq = conv_bwd_act(proj, cq8, dqn, C_DQ * 8, 8, True, "conv_q_bwd")
    dyk, dck = conv_bwd_act(proj, ck8, dkn, C_DK * 8, 8, True, "conv_k_bwd")
    dyv, dcv = conv_bwd_act(proj, cv8, dva, C_DV * 8, 16, False, "conv_v_bwd")
    G["conv_q"], G["conv_k"], G["conv_v"] = dcq[:DN_CONV], dck[:DN_CONV], dcv[:DN_CONV]
    d_dq = conv_bwd_in(dyq, cq8, "conv_q_dx")
    d_dk = conv_bwd_in(dyk, ck8, "conv_k_dx")
    d_dv = conv_bwd_in(dyv, cv8, "conv_v_dx")
    (dqs, dks, dvs), delivered = sb_bwd(qs, ks, vs, o_sb, do_sb, rider=early_grads[0](G) if early_grads else None)
    if early_grads:
        early_grads[1](delivered)
    d_sq, d_sk, G["sb_q_norm_gain"], G["sb_k_norm_gain"] = sb_prep_bwd(
        proj, W["sb_q_norm_gain"], W["sb_k_norm_gain"], dqs, dks)
    d_sv = cast_bf16(dvs, "dvs_cast")
    dproj = jnp.concatenate([d_dq, d_dk, d_dv, dz, d_sq, d_sk, d_sv, dgd, dgs], axis=1)
    dw_big = matmul(n1, dproj, "tn", "proj_dw")
    dw_ab = matmul(n1, dpab, "tn", "pab_dw")
    G["w_in"] = _join_w_in(dw_big, dw_ab)
    if last_grads:
        dn1, delivered = matmul(dproj, w_big, "nt", "proj_dx", tk_t=1024, rider=last_grads[0](G))
        last_grads[1](delivered)
    else:
        dn1 = matmul(dproj, w_big, "nt", "proj_dx", tk_t=1024)
    dn1 = matmul(dpab, w_ab, "nt", "pab_dx", residual=dn1)
    dh0, G["norm_mix_gain"] = rms_bwd(h0, W["norm_mix_gain"], dn1, dh1, "rms1_bwd")
    G["meta_tokens"] = dh0[P0:P0 + N_META]
    return loss, dh0[P0 + N_META:], G


_BIG = ("w_in", "w_branch_dn", "w_branch_sb", "w_out", "w_ffn_in", "w_ffn_out")
_COL_SHARDED = ("w_in", "w_ffn_in", "meta_tokens", "conv_q", "conv_k", "conv_v")
_SMALL_REPL = ("norm_mix_gain", "norm_ffn_gain", "dn_a_log", "dn_dt_bias", "dn_out_norm_gain", "sb_q_norm_gain",
               "sb_k_norm_gain")
_SMALL_SHARD = ("meta_tokens", "conv_q", "conv_k", "conv_v")
_ORDER = ("meta_tokens", "norm_mix_gain", "w_in", "conv_q", "conv_k", "conv_v", "dn_a_log", "dn_dt_bias",
          "dn_out_norm_gain", "sb_q_norm_gain", "sb_k_norm_gain", "w_branch_dn", "w_branch_sb", "w_out",
          "norm_ffn_gain", "w_ffn_in", "w_ffn_out")


def _unshard(g4, name):
    if name in _COL_SHARDED:
        r, cs = g4.shape[1:]
        return jnp.transpose(g4, (1, 0, 2)).reshape(r, N_CHIPS * cs)
    return g4.reshape((-1,) + g4.shape[2:])


def _to_shards(full, name):
    if full.ndim == 3:
        return full
    if name in _COL_SHARDED:
        r, c = full.shape
        return jnp.transpose(full.reshape(r, N_CHIPS, c // N_CHIPS), (1, 0, 2))
    r, c = full.shape
    return full.reshape(N_CHIPS, r // N_CHIPS, c)


def _rows_1024(a):
    r, c = a.shape
    if c >= 1024:
        return a.reshape(r * (c // 1024), 1024)
    return jnp.pad(a, ((0, 0), (0, 1024 - c)))


def kernel(x, meta_tokens, norm_mix_gain, w_in, conv_q, conv_k, conv_v, dn_a_log, dn_dt_bias, dn_out_norm_gain, sb_q_norm_gain, sb_k_norm_gain, w_branch_dn, w_branch_sb, w_out, norm_ffn_gain, w_ffn_in, w_ffn_out, loss_target, m_meta_tokens, m_norm_mix_gain, m_w_in, m_conv_q, m_conv_k, m_conv_v, m_dn_a_log, m_dn_dt_bias, m_dn_out_norm_gain, m_sb_q_norm_gain, m_sb_k_norm_gain, m_w_branch_dn, m_w_branch_sb, m_w_out, m_norm_ffn_gain, m_w_ffn_in, m_w_ffn_out, v_meta_tokens, v_norm_mix_gain, v_w_in, v_conv_q, v_conv_k, v_conv_v, v_dn_a_log, v_dn_dt_bias, v_dn_out_norm_gain, v_sb_q_norm_gain, v_sb_k_norm_gain, v_w_branch_dn, v_w_branch_sb, v_w_out, v_norm_ffn_gain, v_w_ffn_in, v_w_ffn_out):
    Wl = dict(meta_tokens=meta_tokens, norm_mix_gain=norm_mix_gain, w_in=w_in[0], conv_q=conv_q[0], conv_k=conv_k[0],
              conv_v=conv_v[0], dn_a_log=dn_a_log, dn_dt_bias=dn_dt_bias, dn_out_norm_gain=dn_out_norm_gain,
              sb_q_norm_gain=sb_q_norm_gain, sb_k_norm_gain=sb_k_norm_gain, w_branch_dn=w_branch_dn[0],
              w_branch_sb=w_branch_sb[0], w_out=w_out[0], norm_ffn_gain=norm_ffn_gain, w_ffn_in=w_ffn_in[0],
              w_ffn_out=w_ffn_out[0])
    Ml = dict(meta_tokens=m_meta_tokens, norm_mix_gain=m_norm_mix_gain, w_in=m_w_in[0], conv_q=m_conv_q[0],
              conv_k=m_conv_k[0], conv_v=m_conv_v[0], dn_a_log=m_dn_a_log, dn_dt_bias=m_dn_dt_bias,
              dn_out_norm_gain=m_dn_out_norm_gain, sb_q_norm_gain=m_sb_q_norm_gain, sb_k_norm_gain=m_sb_k_norm_gain,
              w_branch_dn=m_w_branch_dn[0], w_branch_sb=m_w_branch_sb[0], w_out=m_w_out[0],
              norm_ffn_gain=m_norm_ffn_gain, w_ffn_in=m_w_ffn_in[0], w_ffn_out=m_w_ffn_out[0])
    Vl = dict(meta_tokens=v_meta_tokens, norm_mix_gain=v_norm_mix_gain, w_in=v_w_in[0], conv_q=v_conv_q[0],
              conv_k=v_conv_k[0], conv_v=v_conv_v[0], dn_a_log=v_dn_a_log, dn_dt_bias=v_dn_dt_bias,
              dn_out_norm_gain=v_dn_out_norm_gain, sb_q_norm_gain=v_sb_q_norm_gain, sb_k_norm_gain=v_sb_k_norm_gain,
              w_branch_dn=v_w_branch_dn[0], w_branch_sb=v_w_branch_sb[0], w_out=v_w_out[0],
              norm_ffn_gain=v_norm_ffn_gain, w_ffn_in=v_w_ffn_in[0], w_ffn_out=v_w_ffn_out[0])
    lead = {n: (1,) if (n in _BIG or n in ("conv_q", "conv_k", "conv_v")) else () for n in _ORDER}

    chip = 2 * lax.axis_index("x") + lax.axis_index("y")
    c = lax.axis_index("c")
    halved = {n: Wl[n].astype(bf16).reshape(2, Wl[n].shape[0] // 2, Wl[n].shape[1]) for n in _BIG}

    def gathered_weights(names, owns, outs):
        res = {}
        for n, own, g4 in zip(names, owns, outs):
            g4 = lax.dynamic_update_slice(g4, own[None], (chip,) + (0,) * own.ndim)
            if n in _BIG:
                g4 = g4.reshape(N_CHIPS, 2 * g4.shape[2], g4.shape[3])
            res[n] = g4 if n == "w_in" else _unshard(g4, n)
        return res

    first = ["w_in"] + list(_SMALL_SHARD)
    first_own = [halved["w_in"]] + [Wl[n] for n in _SMALL_SHARD]
    W = dict(Wl)
    W.update(gathered_weights(first, first_own, run_exchange(gather_chips(first_own[:1], first_own[1:]), "gather_w_in")))
    late = [n for n in _BIG if n != "w_in"]
    late_own = [halved[n] for n in late]
    for n in late:
        del W[n]

    def pair_reduced(names, G, tag):
        g4 = [_to_shards(G[n], n) for n in names]
        g42 = [g.reshape(N_CHIPS, 2, g.shape[1] // 2, g.shape[2]) for g in g4]
        from_sib = sibling_swap_halves(g42, "grad_sibling_swap_" + tag)
        mine = [lax.dynamic_index_in_dim(g, c, axis=1, keepdims=False) for g in g42]
        return [add2(a, b, "grad_pair_add_%s%d" % (tag, t)) for t, (a, b) in enumerate(zip(mine, from_sib))]

    def chip_reduced(parts, slots, tag):
        slots = [lax.dynamic_update_slice(s, lax.dynamic_index_in_dim(p, chip, axis=0, keepdims=True), (chip, 0, 0))
                 for s, p in zip(slots, parts)]
        return [sum_slots(s, "grad_chip_sum_%s%d" % (tag, t)) for t, s in enumerate(slots)]

    def riding(names, tag):
        state = {}

        def begin(G):
            state["parts"] = pair_reduced(names, G, tag)
            return scatter_chips(state["parts"])

        def done(slots):
            state["halves"] = chip_reduced(state["parts"], slots, tag)

        return state, (begin, done)

    early, early_hooks = riding(late, "a")
    last, last_hooks = riding(["w_in"], "b")
    loss, grad_x, G = _step(x[0], W["meta_tokens"], W, loss_target[0],
                            late_weights=(gather_chips(late_own, []), lambda outs: gathered_weights(late, late_own, outs)),
                            early_grads=early_hooks, last_grads=last_hooks)
    halves = last["halves"] + early["halves"]
    theirs = sibling_send(halves, "grad_sibling_send")
    Gs = {}
    for n, h, o in zip(["w_in"] + late, halves, theirs):
        Gs[n] = lax.dynamic_update_slice(jnp.concatenate([o, o], axis=0), h, (c * h.shape[0], 0))

    small_names = list(_SMALL_REPL) + list(_SMALL_SHARD)
    pieces = [_rows_1024(G[n]) for n in small_names] + [_rows_1024(loss)]
    counts = [p.shape[0] for p in pieces]
    pack = jnp.concatenate(pieces, axis=0)
    pad_rows = (-pack.shape[0]) % SUB
    pack = jnp.pad(pack, ((0, pad_rows), (0, 0)))
    total = sum_slots(gather_all(pack, "small_gather"), "small_sum")
    chip = 2 * lax.axis_index("x") + lax.axis_index("y")
    row = 0
    for n, cnt in zip(small_names, counts[:-1]):
        blk = total[row:row + cnt]
        row += cnt
        full_shape = G[n].shape
        if full_shape[1] >= 1024:
            blk = blk.reshape(full_shape)
        else:
            blk = blk[:, :full_shape[1]]
        if n in _SMALL_SHARD:
            cs = full_shape[1] // N_CHIPS
            blk = lax.dynamic_slice_in_dim(blk, chip * cs, cs, axis=1)
        Gs[n] = blk
    loss_out = total[row, 0]

    grads, deltas, new_m, new_v = [], [], [], []
    for n in _ORDER:
        d, m2, v2 = adamw(Wl[n], Gs[n], Ml[n], Vl[n], "adamw_" + n)
        shape = lead[n] + Wl[n].shape
        grads.append(Gs[n].reshape(shape))
        deltas.append(d.reshape(shape))
        new_m.append(m2.reshape(shape))
        new_v.append(v2.reshape(shape))
    return (loss_out, grad_x[None], *grads, *deltas, *new_m, *new_v)
```

```python
import jax
import jax.numpy as jnp
from jax import lax
from jax.experimental import pallas as pl
from jax.experimental.pallas import tpu as pltpu

f32 = jnp.float32
bf16 = jnp.bfloat16

D_MODEL = 1024
N_META = 16
CHUNK = 64
HEADS = 8
DN_DK = 128
DN_DV = 256
DN_CONV = 4
DN_QK = HEADS * DN_DK
DN_V = HEADS * DN_DV
SB_DH = 128
SB_W = HEADS * SB_DH
SB_BLOCK = 128
SB_GROUP = 4
SB_HEADS_PER_STEP = 2
GDN_HEADS_PER_STEP = 8
CONV_W = 512
D_FF = 2816
RMS_EPS = 1e-6
L2_EPS = 1e-6
ADAM_LR = 0.001
ADAM_B1 = 0.9
ADAM_B2 = 0.999
ADAM_EPS = 1e-08
ADAM_WD = 0.01
ADAM_STEP = 10

P0 = 112
LANE = 128
SUB = 8
VMEM_LIMIT = 48 * 1024 * 1024
N_CHIPS = 4
N_DEV = 8

C_DQ, C_DK, C_DV, C_DZ, C_SQ, C_SK, C_SV, C_GDN, C_GSB = 0, 1, 2, 4, 6, 7, 8, 9, 10
PROJ_BIG = 11 * 1024
AB_COL = 2 * DN_QK + 2 * DN_V


def _params(n_axes):
    return pltpu.CompilerParams(dimension_semantics=("arbitrary",) * n_axes, vmem_limit_bytes=VMEM_LIMIT)


def _tile(n, target, q=LANE):
    best = None
    for t in range(q, min(n, target) + 1, q):
        if n % t == 0:
            best = t
    return best if best is not None else n


def _dot(a, b):
    return jnp.dot(a.astype(bf16), b.astype(bf16), preferred_element_type=f32)


def _dot_nt(a, b):
    return lax.dot_general(a.astype(bf16), b.astype(bf16), (((1,), (1,)), ((), ())), preferred_element_type=f32)


def _dot_tn(a, b):
    return lax.dot_general(a.astype(bf16), b.astype(bf16), (((0,), (0,)), ((), ())), preferred_element_type=f32)


_HI = lax.Precision.HIGH


def _hdot(a, b):
    return jnp.dot(a, b, precision=_HI, preferred_element_type=f32)


def _hdot_nt(a, b):
    return lax.dot_general(a, b, (((1,), (1,)), ((), ())), precision=_HI, preferred_element_type=f32)


def _hdot_tn(a, b):
    return lax.dot_general(a, b, (((0,), (0,)), ((), ())), precision=_HI, preferred_element_type=f32)


def _sigmoid(x):
    e = jnp.exp(-jnp.abs(x))
    r = 1.0 / (1.0 + e)
    return jnp.where(x >= 0, r, e * r)


def _log1p_small(e):
    return jnp.where(e < 1e-3, e * (1.0 - e * (0.5 - e * (1.0 / 3.0))), jnp.log(1.0 + e))


def _rowsum(x):
    return jnp.sum(x, axis=1, keepdims=True)


def _allsum(x):
    return jnp.sum(jnp.sum(x, axis=1, keepdims=True), axis=0, keepdims=True)


def matmul(a, b, mode, name, residual=None, out_dtype=f32, tm_t=1408, tn_t=1024, tk_t=1408, rider=None):
    if mode == "nn":
        (M, K), (K2, N) = a.shape, b.shape
    elif mode == "nt":
        (M, K), (N, K2) = a.shape, b.shape
    else:
        (K, M), (K2, N) = a.shape, b.shape
    assert K == K2, (a.shape, b.shape, mode)
    tm, tn, tk = _tile(M, tm_t), _tile(N, tn_t), _tile(K, tk_t)
    nk = K // tk
    if mode == "nn":
        a_spec = pl.BlockSpec((tm, tk), lambda i, j, k: (i, k))
        b_spec = pl.BlockSpec((tk, tn), lambda i, j, k: (k, j))
        dims = (((1,), (0,)), ((), ()))
    elif mode == "nt":
        a_spec = pl.BlockSpec((tm, tk), lambda i, j, k: (i, k))
        b_spec = pl.BlockSpec((tn, tk), lambda i, j, k: (j, k))
        dims = (((1,), (1,)), ((), ()))
    else:
        a_spec = pl.BlockSpec((tk, tm), lambda i, j, k: (k, i))
        b_spec = pl.BlockSpec((tk, tn), lambda i, j, k: (k, j))
        dims = (((0,), (0,)), ((), ()))
    o_spec = pl.BlockSpec((tm, tn), lambda i, j, k: (i, j))
    has_res = residual is not None
    grid = (M // tm, N // tn, nk)
    split, ride_first, ride_last = _ride(rider, 3 if has_res else 2, 1, grid)

    def body(*refs):
        ins_, (o_ref,), (rin, rout, rest) = split(refs)
        a_ref, b_ref = ins_[:2]
        r_ref = ins_[2] if has_res else None
        acc_ref, ride = rest[0], (rin, rout, rest[1:])
        ride_first(ride)
        k = pl.program_id(2)

        @pl.when(k == 0)
        def _():
            acc_ref[...] = jnp.zeros_like(acc_ref)

        acc_ref[...] += lax.dot_general(a_ref[...].astype(bf16), b_ref[...].astype(bf16), dims,
                                        preferred_element_type=f32)

        @pl.when(k == nk - 1)
        def _():
            r = acc_ref[...]
            if has_res:
                r = r + r_ref[...]
            o_ref[...] = r.astype(out_dtype)

        ride_last(ride)

    ins = [a, b] + ([residual] if has_res else [])
    specs = [a_spec, b_spec] + ([o_spec] if has_res else [])
    r_ins = rider.ins if rider else []
    r_outs = rider.out_shapes if rider else []
    res = pl.pallas_call(
        body, name=name, grid=grid, in_specs=specs + [_ANY] * len(r_ins), out_specs=[o_spec] + [_ANY] * len(r_outs),
        out_shape=[jax.ShapeDtypeStruct((M, N), out_dtype)] + list(r_outs),
        scratch_shapes=[pltpu.VMEM((tm, tn), f32)] + (rider.scratch if rider else []), compiler_params=_params(3),
    )(*ins, *r_ins)
    return (res[0], res[1:]) if rider else res[0]


def _row_tile(tp):
    return _tile(tp, 512)


def rms_fwd(h, gain, name):
    tp, d = h.shape
    rt = _row_tile(tp)

    def body(h_ref, g_ref, o_ref):
        x = h_ref[...]
        r = lax.rsqrt(jnp.mean(x * x, axis=-1, keepdims=True) + RMS_EPS)
        o_ref[...] = (x * r * g_ref[...]).astype(bf16)

    return pl.pallas_call(
        body, name=name, grid=(tp // rt,),
        in_specs=[pl.BlockSpec((rt, d), lambda i: (i, 0)), pl.BlockSpec((1, d), lambda i: (0, 0))],
        out_specs=pl.BlockSpec((rt, d), lambda i: (i, 0)),
        out_shape=jax.ShapeDtypeStruct((tp, d), bf16), compiler_params=_params(1),
    )(h, gain)


def rms_bwd(h, gain, dn, dres, name):
    tp, d = h.shape
    rt = _row_tile(tp)

    def body(h_ref, g_ref, dn_ref, dr_ref, dh_ref, dhb_ref, dg_ref):
        i = pl.program_id(0)
        x = h_ref[...]
        r = lax.rsqrt(jnp.mean(x * x, axis=-1, keepdims=True) + RMS_EPS)
        xh = x * r
        dn_ = dn_ref[...]
        dxh = dn_ * g_ref[...]
        dh = r * (dxh - xh * jnp.mean(dxh * xh, axis=-1, keepdims=True)) + dr_ref[...]
        dh_ref[...] = dh
        dhb_ref[...] = dh.astype(bf16)
        part = jnp.sum(dn_ * xh, axis=0, keepdims=True)

        @pl.when(i == 0)
        def _():
            dg_ref[...] = part

        @pl.when(i > 0)
        def _():
            dg_ref[...] += part

    row = pl.BlockSpec((rt, d), lambda i: (i, 0))
    vec = pl.BlockSpec((1, d), lambda i: (0, 0))
    return pl.pallas_call(
        body, name=name, grid=(tp // rt,), in_specs=[row, vec, row, row], out_specs=[row, row, vec],
        out_shape=[jax.ShapeDtypeStruct((tp, d), f32), jax.ShapeDtypeStruct((tp, d), bf16),
                   jax.ShapeDtypeStruct((1, d), f32)],
        compiler_params=_params(1),
    )(h, gain, dn, dres)


def loss_head(y, target):
    tp, d = y.shape
    rt = P0 + N_META
    assert rt == SB_BLOCK and tp % rt == 0 and target.shape == (tp - rt, d)

    def body(y_ref, t_ref, dy_ref, dyb_ref, l_ref):
        i = pl.program_id(0)

        @pl.when(i == 0)
        def _():
            dy_ref[...] = jnp.zeros_like(dy_ref)
            dyb_ref[...] = jnp.zeros_like(dyb_ref)
            l_ref[...] = jnp.zeros_like(l_ref)

        @pl.when(i > 0)
        def _():
            err = y_ref[...] - t_ref[...]
            dy = err * (1.0 / d)
            dy_ref[...] = dy
            dyb_ref[...] = dy.astype(bf16)
            l_ref[...] += jnp.broadcast_to(_allsum(err * err) * (0.5 / d), l_ref.shape)

    row = pl.BlockSpec((rt, d), lambda i: (i, 0))
    return pl.pallas_call(
        body, name="loss_head", grid=(tp // rt,),
        in_specs=[row, pl.BlockSpec((rt, d), lambda i: (jnp.maximum(i - 1, 0), 0))],
        out_specs=[row, row, pl.BlockSpec((1, LANE), lambda i: (0, 0))],
        out_shape=[jax.ShapeDtypeStruct((tp, d), f32), jax.ShapeDtypeStruct((tp, d), bf16),
                   jax.ShapeDtypeStruct((1, LANE), f32)],
        compiler_params=_params(1),
    )(y, target)


def swiglu_fwd(u):
    tp = u.shape[0]
    rt, cb = _row_tile(tp), D_FF // 2
    nb = D_FF // cb

    def body(g_ref, u_ref, o_ref):
        g = g_ref[...]
        o_ref[...] = (g * _sigmoid(g) * u_ref[...]).astype(bf16)

    return pl.pallas_call(
        body, name="swiglu_fwd", grid=(tp // rt, nb),
        in_specs=[pl.BlockSpec((rt, cb), lambda i, j: (i, j)), pl.BlockSpec((rt, cb), lambda i, j: (i, j + nb))],
        out_specs=pl.BlockSpec((rt, cb), lambda i, j: (i, j)),
        out_shape=jax.ShapeDtypeStruct((tp, D_FF), bf16), compiler_params=_params(2),
    )(u, u)


def swiglu_bwd(u, dact):
    tp = u.shape[0]
    rt, cb = _row_tile(tp), D_FF // 2
    nb = D_FF // cb

    def body(g_ref, u_ref, da_ref, dg_ref, du_ref):
        g = g_ref[...]
        s = _sigmoid(g)
        da = da_ref[...]
        dg_ref[...] = (da * u_ref[...] * s * (1.0 + g * (1.0 - s))).astype(bf16)
        du_ref[...] = (da * g * s).astype(bf16)

    lo = pl.BlockSpec((rt, cb), lambda i, j: (i, j))
    hi = pl.BlockSpec((rt, cb), lambda i, j: (i, j + nb))
    dgate, dup = pl.pallas_call(
        body, name="swiglu_bwd", grid=(tp // rt, nb), in_specs=[lo, hi, lo], out_specs=[lo, lo],
        out_shape=[jax.ShapeDtypeStruct((tp, D_FF), bf16)] * 2, compiler_params=_params(2),
    )(u, u, dact)
    return dgate, dup


def merge_fwd(proj, ydn, ysb):
    tp = proj.shape[0]
    rt, d = _row_tile(tp), D_MODEL

    def body(gd_ref, gs_ref, yd_ref, ys_ref, o_ref):
        o_ref[...] = (_sigmoid(gd_ref[...]) * yd_ref[...] + _sigmoid(gs_ref[...]) * ys_ref[...]).astype(bf16)

    row = pl.BlockSpec((rt, d), lambda i: (i, 0))
    return pl.pallas_call(
        body, name="merge_fwd", grid=(tp // rt,),
        in_specs=[pl.BlockSpec((rt, d), lambda i: (i, C_GDN)), pl.BlockSpec((rt, d), lambda i: (i, C_GSB)), row, row],
        out_specs=row, out_shape=jax.ShapeDtypeStruct((tp, d), bf16), compiler_params=_params(1),
    )(proj, proj, ydn, ysb)


def merge_bwd(proj, ydn, ysb, dm):
    tp = proj.shape[0]
    rt, d = _row_tile(tp), D_MODEL

    def body(gd_ref, gs_ref, yd_ref, ys_ref, dm_ref, dyd_ref, dys_ref, dgd_ref, dgs_ref):
        dm_ = dm_ref[...]
        sd = _sigmoid(gd_ref[...])
        ss = _sigmoid(gs_ref[...])
        dyd_ref[...] = (dm_ * sd).astype(bf16)
        dys_ref[...] = (dm_ * ss).astype(bf16)
        dgd_ref[...] = (dm_ * yd_ref[...] * sd * (1.0 - sd)).astype(bf16)
        dgs_ref[...] = (dm_ * ys_ref[...] * ss * (1.0 - ss)).astype(bf16)

    row = pl.BlockSpec((rt, d), lambda i: (i, 0))
    return pl.pallas_call(
        body, name="merge_bwd", grid=(tp // rt,),
        in_specs=[pl.BlockSpec((rt, d), lambda i: (i, C_GDN)), pl.BlockSpec((rt, d), lambda i: (i, C_GSB)), row, row, row],
        out_specs=[row] * 4, out_shape=[jax.ShapeDtypeStruct((tp, d), bf16)] * 4, compiler_params=_params(1),
    )(proj, proj, ydn, ysb, dm)


def dn_out_fwd(o, proj, gain):
    tp = o.shape[0]
    rt, cb, wide = _row_tile(tp), DN_DV, 1024
    zb = C_DZ * 1024 // wide

    def body(o_ref, z_ref, g_ref, y_ref):
        for s in range(wide // cb):
            sl = slice(s * cb, (s + 1) * cb)
            x = o_ref[:, sl]
            r = lax.rsqrt(jnp.mean(x * x, axis=-1, keepdims=True) + RMS_EPS)
            z = z_ref[:, sl]
            y_ref[:, sl] = (x * r * g_ref[...] * (z * _sigmoid(z))).astype(bf16)

    blk = pl.BlockSpec((rt, wide), lambda i, j: (i, j))
    return pl.pallas_call(
        body, name="dn_out_fwd", grid=(tp // rt, DN_V // wide),
        in_specs=[blk, pl.BlockSpec((rt, wide), lambda i, j: (i, j + zb)), pl.BlockSpec((1, cb), lambda i, j: (0, 0))],
        out_specs=blk, out_shape=jax.ShapeDtypeStruct((tp, DN_V), bf16), compiler_params=_params(2),
    )(o, proj, gain)


def dn_out_bwd(o, proj, gain, dy):
    tp = o.shape[0]
    rt, cb, wide = _row_tile(tp), DN_DV, 1024
    zb = C_DZ * 1024 // wide

    def body(o_ref, z_ref, g_ref, dy_ref, do_ref, dz_ref, dg_ref):
        i, j = pl.program_id(0), pl.program_id(1)
        g = g_ref[...]
        part = jnp.zeros((1, cb), f32)
        for hh in range(wide // cb):
            sl = slice(hh * cb, (hh + 1) * cb)
            x = o_ref[:, sl]
            r = lax.rsqrt(jnp.mean(x * x, axis=-1, keepdims=True) + RMS_EPS)
            xh = x * r
            z = z_ref[:, sl]
            s = _sigmoid(z)
            dy_ = dy_ref[:, sl]
            drn = dy_ * (z * s)
            dz_ref[:, sl] = (dy_ * xh * g * s * (1.0 + z * (1.0 - s))).astype(bf16)
            dxh = drn * g
            do_ref[:, sl] = r * (dxh - xh * jnp.mean(dxh * xh, axis=-1, keepdims=True))
            part = part + jnp.sum(drn * xh, axis=0, keepdims=True)
        first = jnp.logical_and(i == 0, j == 0)

        @pl.when(first)
        def _():
            dg_ref[...] = part

        @pl.when(jnp.logical_not(first))
        def _():
            dg_ref[...] += part

    blk = pl.BlockSpec((rt, wide), lambda i, j: (i, j))
    vec = pl.BlockSpec((1, cb), lambda i, j: (0, 0))
    return pl.pallas_call(
        body, name="dn_out_bwd", grid=(tp // rt, DN_V // wide),
        in_specs=[blk, pl.BlockSpec((rt, wide), lambda i, j: (i, j + zb)), vec, blk],
        out_specs=[blk, blk, vec],
        out_shape=[jax.ShapeDtypeStruct((tp, DN_V), f32), jax.ShapeDtypeStruct((tp, DN_V), bf16),
                   jax.ShapeDtypeStruct((1, cb), f32)],
        compiler_params=_params(2),
    )(o, proj, gain, dy)


def sb_prep_fwd(proj, gq, gk):
    tp = proj.shape[0]
    rt, cb = _row_tile(tp), SB_DH

    def body(q_ref, k_ref, v_ref, gq_ref, gk_ref, qo_ref, ko_ref, vo_ref):
        for x_ref, g_ref, o_ref in ((q_ref, gq_ref, qo_ref), (k_ref, gk_ref, ko_ref)):
            for h in range(HEADS):
                sl = slice(h * cb, (h + 1) * cb)
                x = x_ref[:, sl]
                r = lax.rsqrt(jnp.mean(x * x, axis=-1, keepdims=True) + RMS_EPS)
                o_ref[:, sl] = (x * r * g_ref[...]).astype(bf16)
        vo_ref[...] = v_ref[...].astype(bf16)

    blk = pl.BlockSpec((rt, SB_W), lambda i: (i, 0))
    vec = pl.BlockSpec((1, cb), lambda i: (0, 0))
    return pl.pallas_call(
        body, name="sb_prep_fwd", grid=(tp // rt,),
        in_specs=[pl.BlockSpec((rt, SB_W), lambda i: (i, C_SQ)), pl.BlockSpec((rt, SB_W), lambda i: (i, C_SK)),
                  pl.BlockSpec((rt, SB_W), lambda i: (i, C_SV)), vec, vec],
        out_specs=[blk] * 3, out_shape=[jax.ShapeDtypeStruct((tp, SB_W), bf16)] * 3, compiler_params=_params(1),
    )(proj, proj, proj, gq, gk)


def sb_prep_bwd(proj, gq, gk, dqs, dks, dvs):
    tp = proj.shape[0]
    rt, cb = _row_tile(tp), SB_DH

    def body(q_ref, k_ref, gq_ref, gk_ref, dq_ref, dk_ref, dv_ref, dqo_ref, dko_ref, dvo_ref, dgq_ref, dgk_ref):
        first = pl.program_id(0) == 0
        dvo_ref[...] = dv_ref[...].astype(bf16)
        for x_ref, g_ref, dn_ref, dx_ref, dg_ref in ((q_ref, gq_ref, dq_ref, dqo_ref, dgq_ref),
                                                     (k_ref, gk_ref, dk_ref, dko_ref, dgk_ref)):
            part = jnp.zeros((1, cb), f32)
            for h in range(HEADS):
                sl = slice(h * cb, (h + 1) * cb)
                x = x_ref[:, sl]
                r = lax.rsqrt(jnp.mean(x * x, axis=-1, keepdims=True) + RMS_EPS)
                xh = x * r
                dn_ = dn_ref[:, sl]
                dxh = dn_ * g_ref[...]
                dx_ref[:, sl] = (r * (dxh - xh * jnp.mean(dxh * xh, axis=-1, keepdims=True))).astype(bf16)
                part = part + jnp.sum(dn_ * xh, axis=0, keepdims=True)

            @pl.when(first)
            def _(dg_ref=dg_ref, part=part):
                dg_ref[...] = part

            @pl.when(jnp.logical_not(first))
            def _(dg_ref=dg_ref, part=part):
                dg_ref[...] += part

    blk = pl.BlockSpec((rt, SB_W), lambda i: (i, 0))
    vec = pl.BlockSpec((1, cb), lambda i: (0, 0))
    return pl.pallas_call(
        body, name="sb_prep_bwd", grid=(tp // rt,),
        in_specs=[pl.BlockSpec((rt, SB_W), lambda i: (i, C_SQ)), pl.BlockSpec((rt, SB_W), lambda i: (i, C_SK)),
                  vec, vec, blk, blk, blk],
        out_specs=[blk, blk, blk, vec, vec],
        out_shape=[jax.ShapeDtypeStruct((tp, SB_W), bf16)] * 3 + [jax.ShapeDtypeStruct((1, cb), f32)] * 2,
        compiler_params=_params(1),
    )(proj, proj, gq, gk, dqs, dks, dvs)


def _conv_taps(ext, rt):
    taps = []
    for k in range(DN_CONV):
        s = DN_CONV - 1 - k
        taps.append((pltpu.roll(ext, s, axis=0) if s else ext)[SUB:SUB + rt])
    return taps


def _conv_act(taps, w, l2):
    y = taps[0] * w[0:1]
    for k in range(1, DN_CONV):
        y = y + taps[k] * w[k:k + 1]
    s = _sigmoid(y)
    a = y * s
    if l2:
        n = lax.rsqrt(jnp.sum(a * a, axis=-1, keepdims=True) + L2_EPS)
        return y, s, a, n
    return y, s, a, None


def conv_fwd(proj, w8, col_blk, ncb, l2, name):
    tp = proj.shape[0]
    rt = _row_tile(tp)
    hb = rt // SUB
    cw = CONV_W
    cb0 = col_blk * LANE // cw

    def body(x_ref, h_ref, w_ref, o_ref):
        i = pl.program_id(1)
        first = (i > 0).astype(f32)
        for s in range(cw // LANE):
            sl = slice(s * LANE, (s + 1) * LANE)
            ext = jnp.concatenate([h_ref[:, sl] * first, x_ref[:, sl]], axis=0)
            _, _, a, n = _conv_act(_conv_taps(ext, rt), w_ref[:, sl], l2)
            o_ref[:, sl] = a * n if l2 else a

    return pl.pallas_call(
        body, name=name, grid=(ncb * LANE // cw, tp // rt),
        in_specs=[pl.BlockSpec((rt, cw), lambda j, i: (i, j + cb0)),
                  pl.BlockSpec((SUB, cw), lambda j, i: (jnp.maximum(i * hb - 1, 0), j + cb0)),
                  pl.BlockSpec((SUB, cw), lambda j, i: (0, j))],
        out_specs=pl.BlockSpec((rt, cw), lambda j, i: (i, j)),
        out_shape=jax.ShapeDtypeStruct((tp, ncb * LANE), f32), compiler_params=_params(2),
    )(proj, proj, w8)


def conv_bwd_act(proj, w8, dout, col_blk, ncb, l2, name):
    tp = proj.shape[0]
    rt = _row_tile(tp)
    hb = rt // SUB
    cw = CONV_W
    cb0 = col_blk * LANE // cw

    def body(x_ref, h_ref, w_ref, d_ref, dy_ref, dw_ref):
        i = pl.program_id(1)
        first = (i > 0).astype(f32)
        rows = lax.broadcasted_iota(jnp.int32, (SUB, LANE), 0)
        for s in range(cw // LANE):
            sl = slice(s * LANE, (s + 1) * LANE)
            ext = jnp.concatenate([h_ref[:, sl] * first, x_ref[:, sl]], axis=0)
            taps = _conv_taps(ext, rt)
            y, sg, a, n = _conv_act(taps, w_ref[:, sl], l2)
            da = d_ref[:, sl]
            if l2:
                out = a * n
                da = n * (da - out * jnp.sum(da * out, axis=-1, keepdims=True))
            dy = da * sg * (1.0 + y * (1.0 - sg))
            dy_ref[:, sl] = dy
            part = jnp.zeros((SUB, LANE), f32)
            for k in range(DN_CONV):
                part = part + jnp.where(rows == k, jnp.sum(taps[k] * dy, axis=0, keepdims=True), 0.0)

            @pl.when(i == 0)
            def _(sl=sl, part=part):
                dw_ref[:, sl] = part

            @pl.when(i > 0)
            def _(sl=sl, part=part):
                dw_ref[:, sl] += part

    return pl.pallas_call(
        body, name=name, grid=(ncb * LANE // cw, tp // rt),
        in_specs=[pl.BlockSpec((rt, cw), lambda j, i: (i, j + cb0)),
                  pl.BlockSpec((SUB, cw), lambda j, i: (jnp.maximum(i * hb - 1, 0), j + cb0)),
                  pl.BlockSpec((SUB, cw), lambda j, i: (0, j)),
                  pl.BlockSpec((rt, cw), lambda j, i: (i, j))],
        out_specs=[pl.BlockSpec((rt, cw), lambda j, i: (i, j)), pl.BlockSpec((SUB, cw), lambda j, i: (0, j))],
        out_shape=[jax.ShapeDtypeStruct((tp, ncb * LANE), f32), jax.ShapeDtypeStruct((SUB, ncb * LANE), f32)],
        compiler_params=_params(2),
    )(proj, proj, w8, dout)


def conv_bwd_in(dy, w8, name):
    tp, cols = dy.shape
    rt = _row_tile(tp)
    hb = rt // SUB
    nr = tp // rt
    last8 = tp // SUB - 1
    cw = CONV_W

    def body(d_ref, h_ref, w_ref, o_ref):
        i = pl.program_id(1)
        last = (i < nr - 1).astype(f32)
        for c0 in range(cw // LANE):
            sl = slice(c0 * LANE, (c0 + 1) * LANE)
            ext = jnp.concatenate([d_ref[:, sl], h_ref[:, sl] * last], axis=0)
            w = w_ref[:, sl]
            acc = None
            for k in range(DN_CONV):
                s = DN_CONV - 1 - k
                sh = (pltpu.roll(ext, rt + SUB - s, axis=0) if s else ext)[0:rt]
                term = sh * w[k:k + 1]
                acc = term if acc is None else acc + term
            o_ref[:, sl] = acc.astype(bf16)

    return pl.pallas_call(
        body, name=name, grid=(cols // cw, nr),
        in_specs=[pl.BlockSpec((rt, cw), lambda j, i: (i, j)),
                  pl.BlockSpec((SUB, cw), lambda j, i: (jnp.minimum((i + 1) * hb, last8), j)),
                  pl.BlockSpec((SUB, cw), lambda j, i: (0, j))],
        out_specs=pl.BlockSpec((rt, cw), lambda j, i: (i, j)),
        out_shape=jax.ShapeDtypeStruct((tp, cols), bf16), compiler_params=_params(2),
    )(dy, dy, w8)


def _ab_common(p, al, dtb, r0):
    rows = r0 + lax.broadcasted_iota(jnp.int32, p.shape, 0)
    mask = (rows >= P0).astype(f32)
    xx = p + dtb
    sp = jnp.maximum(xx, 0.0) + _log1p_small(jnp.exp(-jnp.abs(xx)))
    ea = jnp.exp(al)
    g = -ea * sp * mask
    beta = _sigmoid(p) * mask
    return g, beta, _sigmoid(xx), ea, mask


def ab_fwd(pab, al, dtb):
    tp = pab.shape[0]
    rt = _row_tile(tp)

    def body(p_ref, al_ref, dt_ref, g_ref, b_ref):
        i = pl.program_id(0)
        g, beta, _, _, _ = _ab_common(p_ref[...], al_ref[...], dt_ref[...], i * rt)
        for h in range(HEADS):
            g_ref[h] = jnp.broadcast_to(g[:, h:h + 1], (rt, LANE))
            b_ref[h] = jnp.broadcast_to(beta[:, HEADS + h:HEADS + h + 1], (rt, LANE))

    vec = pl.BlockSpec((1, LANE), lambda i: (0, 0))
    out = pl.BlockSpec((HEADS, rt, LANE), lambda i: (0, i, 0))
    return pl.pallas_call(
        body, name="ab_fwd", grid=(tp // rt,), in_specs=[pl.BlockSpec((rt, LANE), lambda i: (i, 0)), vec, vec],
        out_specs=[out, out], out_shape=[jax.ShapeDtypeStruct((HEADS, tp, LANE), f32)] * 2, compiler_params=_params(1),
    )(pab, al, dtb)


def ab_bwd(pab, al, dtb, dg, db):
    tp = pab.shape[0]
    rt = _row_tile(tp)

    def body(p_ref, al_ref, dt_ref, dg_ref, db_ref, dp_ref, dal_ref, ddt_ref):
        i = pl.program_id(0)
        g, beta, sx, ea, mask = _ab_common(p_ref[...], al_ref[...], dt_ref[...], i * rt)
        lanes = lax.broadcasted_iota(jnp.int32, (rt, LANE), 1)
        dgl = jnp.zeros((rt, LANE), f32)
        dbl = jnp.zeros((rt, LANE), f32)
        for h in range(HEADS):
            dgl = dgl + jnp.where(lanes == h, dg_ref[h], 0.0)
            dbl = dbl + jnp.where(lanes == HEADS + h, db_ref[h], 0.0)
        dxx = dgl * (-ea) * sx * mask
        dp_ref[...] = (dxx + dbl * beta * (1.0 - beta)).astype(bf16)
        pal = jnp.sum(dgl * g, axis=0, keepdims=True)
        pdt = jnp.sum(dxx, axis=0, keepdims=True)

        @pl.when(i == 0)
        def _():
            dal_ref[...] = pal
            ddt_ref[...] = pdt

        @pl.when(i > 0)
        def _():
            dal_ref[...] += pal
            ddt_ref[...] += pdt

    vec = pl.BlockSpec((1, LANE), lambda i: (0, 0))
    row = pl.BlockSpec((rt, LANE), lambda i: (i, 0))
    big = pl.BlockSpec((HEADS, rt, LANE), lambda i: (0, i, 0))
    return pl.pallas_call(
        body, name="ab_bwd", grid=(tp // rt,), in_specs=[row, vec, vec, big, big], out_specs=[row, vec, vec],
        out_shape=[jax.ShapeDtypeStruct((tp, LANE), bf16), jax.ShapeDtypeStruct((1, LANE), f32),
                   jax.ShapeDtypeStruct((1, LANE), f32)],
        compiler_params=_params(1),
    )(pab, al, dtb, dg, db)


class _Chunk:
    pass


def _gdn_chunk(q, k, v, gcol, bcol, grow8):
    C = CHUNK
    R = range(len(q))
    X = _Chunk()
    ri = lax.broadcasted_iota(jnp.int32, (C, C), 0)
    ci = lax.broadcasted_iota(jnp.int32, (C, C), 1)
    r2 = lax.broadcasted_iota(jnp.int32, (LANE, LANE), 0)
    c2 = lax.broadcasted_iota(jnp.int32, (LANE, LANE), 1)
    lower = (ri >= ci).astype(f32)
    upper2 = (r2 <= c2).astype(f32)
    eye = (ri == ci).astype(f32)
    gam = [_hdot(lower, gcol[h]) for h in R]
    gam_row = [_hdot(grow8[h], upper2)[0:1, 0:C] for h in R]
    X.ri, X.ci = ri, ci
    X.Dm = [jnp.where(ri >= ci, jnp.exp(jnp.minimum(gam[h][:, 0:C] - gam_row[h], 0.0)), 0.0) for h in R]
    X.eg = [jnp.exp(gam[h]) for h in R]
    gl = [gam[h][C - 1:C, :] for h in R]
    X.egl = [jnp.exp(gl[h]) for h in R]
    X.kdec = [jnp.exp(gl[h] - gam[h]) for h in R]
    X.qs = [q[h] * (DN_DK ** -0.5) for h in R]
    X.kb = [k[h] * bcol[h] for h in R]
    kk = [_dot_nt(X.kb[h], k[h]) for h in R]
    qk = [_dot_nt(X.qs[h], k[h]) for h in R]
    X.A = [jnp.where(ri > ci, kk[h] * X.Dm[h], 0.0) for h in R]
    T = [eye - X.A[h] for h in R]
    P = list(X.A)
    for _ in range(5):
        P = [_hdot(P[h], P[h]) for h in R]
        T = [T[h] + _hdot(T[h], P[h]) for h in R]
    X.T = T
    X.b2 = [jnp.concatenate([bcol[h], bcol[h]], axis=-1) for h in R]
    X.u = [_hdot(T[h], v[h] * X.b2[h]) for h in R]
    X.w = [_hdot(T[h], X.kb[h] * X.eg[h]) for h in R]
    X.attn = [qk[h] * X.Dm[h] for h in R]
    X.qg = [X.qs[h] * X.eg[h] for h in R]
    X.kg = [k[h] * X.kdec[h] for h in R]
    return X


def gdn_fwd(q, k, v, gc, bc, grow):
    tp = q.shape[0]
    nc = tp // CHUNK
    hb = GDN_HEADS_PER_STEP

    def body(q_ref, k_ref, v_ref, gc_ref, bc_ref, gr_ref, o_ref, ss_ref, S_ref):
        c = pl.program_id(1)

        @pl.when(c == 0)
        def _():
            S_ref[...] = jnp.zeros_like(S_ref)

        R = range(hb)
        qc = [slice(h * DN_DK, (h + 1) * DN_DK) for h in R]
        vc = [slice(h * DN_DV, (h + 1) * DN_DV) for h in R]
        X = _gdn_chunk([q_ref[:, qc[h]] for h in R], [k_ref[:, qc[h]] for h in R], [v_ref[:, vc[h]] for h in R],
                       [gc_ref[h] for h in R], [bc_ref[h] for h in R], [gr_ref[h] for h in R])
        S = [S_ref[h] for h in R]
        for h in R:
            ss_ref[h, 0] = S[h]
        wS = [_dot(X.w[h], S[h]) for h in R]
        qS = [_dot(X.qg[h], S[h]) for h in R]
        vn = [X.u[h] - wS[h] for h in R]
        av = [_dot(X.attn[h], vn[h]) for h in R]
        kv = [_dot_tn(X.kg[h], vn[h]) for h in R]
        for h in R:
            o_ref[:, vc[h]] = qS[h] + av[h]
            S_ref[h] = S[h] * X.egl[h][:, 0:1] + kv[h]

    qk = pl.BlockSpec((CHUNK, hb * DN_DK), lambda g, c: (c, g))
    vv = pl.BlockSpec((CHUNK, hb * DN_DV), lambda g, c: (c, g))
    col = pl.BlockSpec((hb, CHUNK, LANE), lambda g, c: (g, c, 0))
    row = pl.BlockSpec((hb, SUB, LANE), lambda g, c: (g, c, 0))
    return pl.pallas_call(
        body, name="gdn_fwd", grid=(HEADS // hb, nc), in_specs=[qk, qk, vv, col, col, row],
        out_specs=[vv, pl.BlockSpec((hb, 1, DN_DK, DN_DV), lambda g, c: (g, c, 0, 0))],
        out_shape=[jax.ShapeDtypeStruct((tp, DN_V), f32), jax.ShapeDtypeStruct((HEADS, nc, DN_DK, DN_DV), f32)],
        scratch_shapes=[pltpu.VMEM((hb, DN_DK, DN_DV), f32)], compiler_params=_params(2),
    )(q, k, v, gc, bc, grow)


def gdn_bwd(q, k, v, gc, bc, grow, states, do, rider=None):
    tp = q.shape[0]
    nc = tp // CHUNK
    C = CHUNK
    hb = GDN_HEADS_PER_STEP
    grid = (HEADS // hb, nc)
    split, ride_first, ride_last = _ride(rider, 8, 5, grid)

    def body(*refs):
        ((q_ref, k_ref, v_ref, gc_ref, bc_ref, gr_ref, ss_ref, do_ref), (dq_ref, dk_ref, dv_ref, dg_ref, db_ref),
         (rin, rout, rest)) = split(refs)
        dS_ref, ride = rest[0], (rin, rout, rest[1:])
        ride_first(ride)
        c = pl.program_id(1)

        @pl.when(c == 0)
        def _():
            dS_ref[...] = jnp.zeros_like(dS_ref)

        R = range(hb)
        qc = [slice(h * DN_DK, (h + 1) * DN_DK) for h in R]
        vc = [slice(h * DN_DV, (h + 1) * DN_DV) for h in R]
        k_ = [k_ref[:, qc[h]] for h in R]
        v_ = [v_ref[:, vc[h]] for h in R]
        bcol = [bc_ref[h] for h in R]
        X = _gdn_chunk([q_ref[:, qc[h]] for h in R], k_, v_, [gc_ref[h] for h in R], bcol, [gr_ref[h] for h in R])
        ri, ci = X.ri, X.ci
        S = [ss_ref[h, 0] for h in R]
        do_ = [do_ref[:, vc[h]] for h in R]
        dSn = [dS_ref[h] for h in R]
        wS = [_dot(X.w[h], S[h]) for h in R]
        ado = [_dot_tn(X.attn[h], do_[h]) for h in R]
        kdS = [_dot(X.kg[h], dSn[h]) for h in R]
        d_qg = [_dot_nt(do_[h], S[h]) for h in R]
        qdo = [_dot_tn(X.qg[h], do_[h]) for h in R]
        vn = [X.u[h] - wS[h] for h in R]
        d_vn = [ado[h] + kdS[h] for h in R]
        dovn = [_dot_nt(do_[h], vn[h]) for h in R]
        d_kg = [_dot_nt(vn[h], dSn[h]) for h in R]
        wdv = [_dot_tn(X.w[h], d_vn[h]) for h in R]
        dw = [-_dot_nt(d_vn[h], S[h]) for h in R]
        for h in R:
            dS_ref[h] = qdo[h] + X.egl[h][:, 0:1] * dSn[h] - wdv[h]
        dattn = [jnp.where(ri >= ci, dovn[h], 0.0) for h in R]
        dRu = [_hdot_tn(X.T[h], d_vn[h]) for h in R]
        dRw = [_hdot_tn(X.T[h], dw[h]) for h in R]
        dAu = [_hdot_nt(dRu[h], X.u[h]) for h in R]
        dAw = [_hdot_nt(dRw[h], X.w[h]) for h in R]
        dA = [jnp.where(ri > ci, -(dAu[h] + dAw[h]), 0.0) for h in R]
        dKK = [dA[h] * X.Dm[h] for h in R]
        dQK = [dattn[h] * X.Dm[h] for h in R]
        E = [dA[h] * X.A[h] + dattn[h] * X.attn[h] for h in R]
        dkb = [_dot(dKK[h], k_[h]) + dRw[h] * X.eg[h] for h in R]
        dk1 = [_dot_tn(dKK[h], X.kb[h]) for h in R]
        dqs = [_dot(dQK[h], k_[h]) + d_qg[h] * X.eg[h] for h in R]
        dk2 = [_dot_tn(dQK[h], X.qs[h]) for h in R]
        ones = jnp.ones((C, LANE), f32)
        colE = [_hdot_tn(E[h], ones) for h in R]
        rows = lax.broadcasted_iota(jnp.int32, (C, LANE), 0)
        upper = (ci >= ri).astype(f32)
        dgam = []
        for h in R:
            t = d_kg[h] * X.kg[h]
            dgl = _allsum(t) + X.egl[h][:, 0:1] * _allsum(S[h] * dSn[h])
            g = (_rowsum(E[h]) - colE[h] + _rowsum(dRw[h] * (X.kb[h] * X.eg[h])) + _rowsum(d_qg[h] * X.qg[h])
                 - _rowsum(t))
            dgam.append(g + jnp.where(rows == C - 1, dgl, 0.0))
        dg = [_hdot(upper, dgam[h]) for h in R]
        for h in R:
            dv_ref[:, vc[h]] = dRu[h] * X.b2[h]
            dbeta = _rowsum(dRu[h] * v_[h]) + _rowsum(dkb[h] * k_[h])
            dq_ref[:, qc[h]] = dqs[h] * (DN_DK ** -0.5)
            dk_ref[:, qc[h]] = dk1[h] + dk2[h] + dkb[h] * bcol[h] + d_kg[h] * X.kdec[h]
            dg_ref[h] = dg[h]
            db_ref[h] = jnp.broadcast_to(dbeta, (C, LANE))
        ride_last(ride)

    rc = lambda c: nc - 1 - c
    qk = pl.BlockSpec((CHUNK, hb * DN_DK), lambda g, c: (rc(c), g))
    vv = pl.BlockSpec((CHUNK, hb * DN_DV), lambda g, c: (rc(c), g))
    col = pl.BlockSpec((hb, CHUNK, LANE), lambda g, c: (g, rc(c), 0))
    row = pl.BlockSpec((hb, SUB, LANE), lambda g, c: (g, rc(c), 0))
    st = pl.BlockSpec((hb, 1, DN_DK, DN_DV), lambda g, c: (g, rc(c), 0, 0))
    r_ins = rider.ins if rider else []
    r_outs = rider.out_shapes if rider else []
    res = pl.pallas_call(
        body, name="gdn_bwd", grid=grid, in_specs=[qk, qk, vv, col, col, row, st, vv] + [_ANY] * len(r_ins),
        out_specs=[qk, qk, vv, col, col] + [_ANY] * len(r_outs),
        out_shape=[jax.ShapeDtypeStruct((tp, DN_QK), f32), jax.ShapeDtypeStruct((tp, DN_QK), f32),
                   jax.ShapeDtypeStruct((tp, DN_V), f32), jax.ShapeDtypeStruct((HEADS, tp, LANE), f32),
                   jax.ShapeDtypeStruct((HEADS, tp, LANE), f32)] + list(r_outs),
        scratch_shapes=[pltpu.VMEM((hb, DN_DK, DN_DV), f32)] + (rider.scratch if rider else []),
        compiler_params=_params(2),
    )(q, k, v, gc, bc, grow, states, do, *r_ins)
    return res[:5], res[5:]


def _cumsum_after(x, nb, us):
    B = SB_BLOCK
    hi = x.astype(bf16)
    lo = (x - hi.astype(f32)).astype(bf16)
    rows = [p[:, b * B:(b + 1) * B] for p in (hi, lo) for b in range(nb)]
    r = jnp.dot(jnp.concatenate(rows, axis=0), us, preferred_element_type=f32)
    out = [r[b * B:(b + 1) * B] + r[(nb + b) * B:(nb + b + 1) * B] for b in range(nb)]
    return out[0] if nb == 1 else jnp.concatenate(out, axis=1)


def _later_blocks(x, nb, carry):
    B = SB_BLOCK
    tot = [_rowsum(x[:, b * B:(b + 1) * B]) for b in range(nb)]
    offs = [None] * nb
    run = carry
    for b in range(nb - 1, -1, -1):
        offs[b] = jnp.broadcast_to(run, (B, B))
        run = run + tot[b]
    return (offs[0] if nb == 1 else jnp.concatenate(offs, axis=1)), run


def _sb_group(i, t):
    top = i - SB_GROUP * t
    jlo = jnp.maximum(top - SB_GROUP + 1, 0)
    rows = pl.ds(pl.multiple_of(jlo * SB_BLOCK, SB_BLOCK), SB_GROUP * SB_BLOCK)
    return jlo, rows, (top + 1) * SB_BLOCK


def _sb_weights(q, kcat, i, jlo, kend, cs, us, masked):
    B, nb = SB_BLOCK, SB_GROUP
    R = range(len(q))
    z = [_dot_nt(q[h], kcat[h]) * (SB_DH ** -0.5) for h in R]
    e = [jnp.exp(-jnp.abs(z[h])) for h in R]
    l1p = [jnp.log(1.0 + e[h]) for h in R]
    lsp = [jnp.minimum(z[h], 0.0) - l1p[h] for h in R]
    lk = [lsp[h] - z[h] for h in R]
    vis = None
    if masked:
        qpos = i * B + lax.broadcasted_iota(jnp.int32, (B, nb * B), 0)
        kpos = jlo * B + lax.broadcasted_iota(jnp.int32, (B, nb * B), 1)
        vis = jnp.logical_and(kpos < jnp.minimum(qpos, kend), kpos >= P0)
        lk = [jnp.where(vis, lk[h], 0.0) for h in R]
    later = [_later_blocks(lk[h], nb, cs[h]) for h in R]
    cum = [_cumsum_after(lk[h], nb, us) for h in R]
    w = [jnp.exp(lsp[h] + cum[h] + later[h][0]) for h in R]
    if masked:
        w = [jnp.where(vis, w[h], 0.0) for h in R]
    return lsp, vis, w, [later[h][1] for h in R]


def _sb_loop(i, step, carry):
    trips = (i + SB_GROUP) // SB_GROUP
    carry = step(True)(0, carry)
    carry = lax.fori_loop(1, trips - 1, step(False), carry)
    return lax.fori_loop(jnp.maximum(trips - 1, 1), trips, step(True), carry)


def _ride(rider, n_in, n_out, grid):
    n_rin = len(rider.ins) if rider else 0
    n_rout = len(rider.out_shapes) if rider else 0

    def split(refs):
        ins, rin = refs[:n_in], refs[n_in:n_in + n_rin]
        outs = refs[n_in + n_rin:n_in + n_rin + n_out]
        rout = refs[n_in + n_rin + n_out:n_in + n_rin + n_out + n_rout]
        return ins, outs, (rin, rout, refs[n_in + n_rin + n_out + n_rout:])

    def at(step, fn, r):
        if rider is None:
            return
        cond = None
        for a, g in enumerate(grid):
            c = pl.program_id(a) == (g - 1 if step == "last" else 0)
            cond = c if cond is None else jnp.logical_and(cond, c)

        @pl.when(cond)
        def _():
            fn(*r)

    first = lambda r: at("first", rider.start if rider else None, r)
    last = lambda r: at("last", rider.finish if rider else None, r)
    return split, first, last


def sb_fwd(qs, ks, vs, rider=None):
    tp = qs.shape[0]
    nq = tp // SB_BLOCK
    B, G, hb = SB_BLOCK, SB_GROUP, SB_HEADS_PER_STEP
    assert tp >= G * B
    grid = (HEADS // hb, nq)
    split, ride_first, ride_last = _ride(rider, 3, 2, grid)

    def body(*refs):
        (q_ref, k_ref, v_ref), (o_ref, ob_ref), ride = split(refs)
        ride_first(ride)
        i = pl.program_id(1)
        R = range(hb)
        hs = [slice(h * SB_DH, (h + 1) * SB_DH) for h in R]
        q = [q_ref[:, hs[h]] for h in R]
        us = (lax.broadcasted_iota(jnp.int32, (B, B), 0) > lax.broadcasted_iota(jnp.int32, (B, B), 1)).astype(bf16)

        def make_step(masked):
            def step(t, carry):
                acc, cs = carry
                jlo, rows, kend = _sb_group(i, t)
                _, _, w, cs = _sb_weights(q, [k_ref[rows, hs[h]] for h in R], i, jlo, kend, cs, us, masked)
                pv = [_dot(w[h], v_ref[rows, hs[h]]) for h in R]
                return tuple(acc[h] + pv[h] for h in R), tuple(cs)
            return step

        carry = (tuple(jnp.zeros((B, SB_DH), f32) for _ in R), tuple(jnp.zeros((B, 1), f32) for _ in R))
        acc, _ = _sb_loop(i, make_step, carry)
        for h in R:
            o_ref[:, hs[h]] = acc[h]
            ob_ref[:, hs[h]] = acc[h].astype(bf16)
        ride_last(ride)

    blk = pl.BlockSpec((B, hb * SB_DH), lambda g, i: (i, g))
    full = pl.BlockSpec((tp, hb * SB_DH), lambda g, i: (0, g))
    r_ins = rider.ins if rider else []
    r_outs = rider.out_shapes if rider else []
    res = pl.pallas_call(
        body, name="sb_fwd", grid=grid, in_specs=[blk, full, full] + [_ANY] * len(r_ins),
        out_specs=[blk, blk] + [_ANY] * len(r_outs),
        out_shape=[jax.ShapeDtypeStruct((tp, SB_W), f32), jax.ShapeDtypeStruct((tp, SB_W), bf16)] + list(r_outs),
        scratch_shapes=rider.scratch if rider else [], compiler_params=_params(2),
    )(qs, ks, vs, *r_ins)
    return res[0], res[1], res[2:]


def sb_bwd(qs, ks, vs, o, do, rider=None):
    tp = qs.shape[0]
    nq = tp // SB_BLOCK
    B, G, hb = SB_BLOCK, SB_GROUP, SB_HEADS_PER_STEP
    assert tp >= G * B
    grid = (HEADS // hb, nq)
    split, ride_first, ride_last = _ride(rider, 5, 3, grid)

    def body(*refs):
        (q_ref, k_ref, v_ref, o_ref, do_ref), (dq_ref, dk_ref, dv_ref), ride = split(refs)
        ride_first(ride)
        i = pl.program_id(1)

        @pl.when(i == 0)
        def _():
            dk_ref[...] = jnp.zeros_like(dk_ref)
            dv_ref[...] = jnp.zeros_like(dv_ref)

        R = range(hb)
        hs = [slice(h * SB_DH, (h + 1) * SB_DH) for h in R]
        q = [q_ref[:, hs[h]] for h in R]
        dob = [do_ref[:, hs[h]].astype(bf16) for h in R]
        et = [_rowsum(dob[h].astype(f32) * o_ref[:, hs[h]]) for h in R]
        us = (lax.broadcasted_iota(jnp.int32, (B, B), 0) > lax.broadcasted_iota(jnp.int32, (B, B), 1)).astype(bf16)

        def make_step(masked):
            def step(t, carry):
                dq, cs, ce = carry
                jlo, rows, kend = _sb_group(i, t)
                kcat = [k_ref[rows, hs[h]] for h in R]
                dwv = [_dot_nt(dob[h], v_ref[rows, hs[h]]) for h in R]
                lsp, vis, w, cs = _sb_weights(q, kcat, i, jlo, kend, cs, us, masked)
                wb = [w[h].astype(bf16) for h in R]
                ee = [dwv[h] * wb[h].astype(f32) for h in R]
                later = [_later_blocks(ee[h], G, ce[h]) for h in R]
                cum = [_cumsum_after(ee[h], G, us) for h in R]
                dz = []
                for h in R:
                    d = ee[h] - jnp.exp(lsp[h]) * (et[h] - (cum[h] + later[h][0]))
                    if masked:
                        d = jnp.where(vis, d, 0.0)
                    dz.append((d * (SB_DH ** -0.5)).astype(bf16))
                dkj = [_dot_tn(dz[h], q[h]) for h in R]
                dvj = [_dot_tn(wb[h], dob[h]) for h in R]
                dqj = [_dot(dz[h], kcat[h]) for h in R]
                for h in R:
                    dk_ref[rows, hs[h]] += dkj[h]
                    dv_ref[rows, hs[h]] += dvj[h]
                return tuple(dq[h] + dqj[h] for h in R), tuple(cs), tuple(later[h][1] for h in R)
            return step

        z0 = tuple(jnp.zeros((B, 1), f32) for _ in R)
        dq, _, _ = _sb_loop(i, make_step, (tuple(jnp.zeros((B, SB_DH), f32) for _ in R), z0, z0))
        for h in R:
            dq_ref[:, hs[h]] = dq[h]
        ride_last(ride)

    blk = pl.BlockSpec((B, hb * SB_DH), lambda g, i: (i, g))
    full = pl.BlockSpec((tp, hb * SB_DH), lambda g, i: (0, g))
    r_ins = rider.ins if rider else []
    r_outs = rider.out_shapes if rider else []
    res = pl.pallas_call(
        body, name="sb_bwd", grid=grid, in_specs=[blk, full, full, blk, blk] + [_ANY] * len(r_ins),
        out_specs=[blk, full, full] + [_ANY] * len(r_outs),
        out_shape=[jax.ShapeDtypeStruct((tp, SB_W), f32)] * 3 + list(r_outs),
        scratch_shapes=rider.scratch if rider else [], compiler_params=_params(2),
    )(qs, ks, vs, o, do, *r_ins)
    return res[:3], res[3:]


def adamw(w, g, m, v, name):
    r, c = w.shape
    rt = _tile(r, 128, SUB) if r % SUB == 0 else r
    blk = pl.BlockSpec((rt, c), lambda i: (i, 0))
    c1 =1.0 - ADAM_B1 ** ADAM_STEP
    c2 = 1.0 - ADAM_B2 ** ADAM_STEP

    def body(w_ref, g_ref, m_ref, v_ref, d_ref, mo_ref, vo_ref):
        g_ = g_ref[...]
        m_ = ADAM_B1 * m_ref[...] + (1.0 - ADAM_B1) * g_
        v_ = ADAM_B2 * v_ref[...] + (1.0 - ADAM_B2) * (g_ * g_)
        mo_ref[...] = m_
        vo_ref[...] = v_
        d_ref[...] = -ADAM_LR * ((m_ / c1) / (jnp.sqrt(v_ / c2) + ADAM_EPS) + ADAM_WD * w_ref[...])

    return pl.pallas_call(
        body, name=name, grid=(r // rt,), in_specs=[blk] * 4, out_specs=[blk] * 3,
        out_shape=[jax.ShapeDtypeStruct((r, c), f32)] * 3, compiler_params=_params(1),
    )(w, g, m, v)


def sum_slots(x, name):
    n, r, c = x.shape
    rt = _tile(r, 128, SUB) if r % SUB == 0 else r
    blk = pl.BlockSpec((n, rt, c), lambda i: (0, i, 0))

    def body(x_ref, o_ref):
        acc = x_ref[0].astype(f32)
        for s in range(1, n):
            acc = acc + x_ref[s].astype(f32)
        o_ref[...] = acc

    return pl.pallas_call(
        body, name=name, grid=(r // rt,), in_specs=[blk], out_specs=pl.BlockSpec((rt, c), lambda i: (i, 0)),
        out_shape=jax.ShapeDtypeStruct((r, c), f32), compiler_params=_params(1),
    )(x)


def add2(a, b, name, out_dtype=f32):
    n, r, c = a.shape
    rt = _tile(r, 64, SUB) if r % SUB == 0 else r
    blk = pl.BlockSpec((n, rt, c), lambda i: (0, i, 0))

    def body(a_ref, b_ref, o_ref):
        o_ref[...] = (a_ref[...] + b_ref[...]).astype(out_dtype)

    return pl.pallas_call(
        body, name=name, grid=(r // rt,), in_specs=[blk, blk], out_specs=blk,
        out_shape=jax.ShapeDtypeStruct((n, r, c), out_dtype), compiler_params=_params(1),
    )(a, b)


_ANY = pl.BlockSpec(memory_space=pl.ANY)
_MESH = pl.DeviceIdType.MESH


def _coords():
    return lax.axis_index("x"), lax.axis_index("y"), lax.axis_index("c")


def _chip_peer(x, y, r):
    return x ^ (r >> 1), y ^ (r & 1)


class _Exchange:
    def __init__(self, ins, out_shapes, scratch, start, finish):
        self.ins, self.out_shapes, self.scratch, self.start, self.finish = ins, out_shapes, scratch, start, finish

    def split(self, refs):
        n, m = len(self.ins), len(self.out_shapes)
        return refs[:n], refs[n:n + m], refs[n + m:]


def run_exchange(ex, name):
    def body(*refs):
        ins, outs, sems = ex.split(refs)
        ex.start(ins, outs, sems)
        ex.finish(ins, outs, sems)

    return pl.pallas_call(body, name=name, in_specs=[_ANY] * len(ex.ins), out_specs=[_ANY] * len(ex.out_shapes),
                          out_shape=ex.out_shapes, scratch_shapes=ex.scratch)(*ex.ins)


def gather_chips(big, small):
    nb, n = len(big), len(big) + len(small)
    shards = list(big) + list(small)
    kb = nb * (N_CHIPS - 1)
    k = n * (N_CHIPS - 1)

    def copies(src, dst, sems):
        send, recv, fsend, frecv = sems
        x, y, c = _coords()
        sib = (x, y, 1 - c)
        peers = [_chip_peer(x, y, r) for r in range(1, N_CHIPS)]

        def direct(t, j, slot):
            s = t * (N_CHIPS - 1) + j
            if t < nb:
                return pltpu.make_async_remote_copy(src[t].at[c], dst[t].at[slot, c], send.at[s], recv.at[s],
                                                    device_id=(*peers[j], c), device_id_type=_MESH)
            return pltpu.make_async_remote_copy(src[t], dst[t].at[slot], send.at[s], recv.at[s],
                                                device_id=(*peers[j], c), device_id_type=_MESH)

        def passed(t, j, half):
            s = t * (N_CHIPS - 1) + j
            px, py = peers[j]
            part = dst[t].at[2 * px + py, half]
            return pltpu.make_async_remote_copy(part, part, fsend.at[s], frecv.at[s], device_id=sib, device_id_type=_MESH)

        return direct, passed, peers, 2 * x + y, c

    def start(src, dst, sems):
        direct, _, _, me, _ = copies(src, dst, sems)
        for t in range(n):
            for j in range(N_CHIPS - 1):
                direct(t, j, me).start()

    def finish(src, dst, sems):
        direct, passed, peers, me, c = copies(src, dst, sems)
        fwd = []
        for t in range(nb):
            for j in range(N_CHIPS - 1):
                px, py = peers[j]
                direct(t, j, 2 * px + py).wait_recv()
                fwd.append(passed(t, j, c))
                fwd[-1].start()
        for t in range(nb, n):
            for j in range(N_CHIPS - 1):
                px, py = peers[j]
                direct(t, j, 2 * px + py).wait_recv()
        for t in range(nb):
            for j in range(N_CHIPS - 1):
                passed(t, j, 1 - c).wait_recv()
        for t in range(n):
            for j in range(N_CHIPS - 1):
                direct(t, j, me).wait_send()
        for cp in fwd:
            cp.wait_send()

    return _Exchange(shards, [jax.ShapeDtypeStruct((N_CHIPS,) + s.shape, s.dtype) for s in shards],
                     [pltpu.SemaphoreType.DMA((k,)), pltpu.SemaphoreType.DMA((k,)),
                      pltpu.SemaphoreType.DMA((max(kb, 1),)), pltpu.SemaphoreType.DMA((max(kb, 1),))], start, finish)


def sibling_swap(grads):
    n = len(grads)
    k = n * N_CHIPS

    def copies(src, dst, sems):
        send, recv = sems
        x, y, c = _coords()
        return [pltpu.make_async_remote_copy(src[t].at[o, 1 - c], dst[t].at[o], send.at[t * N_CHIPS + o],
                                             recv.at[t * N_CHIPS + o], device_id=(x, y, 1 - c), device_id_type=_MESH)
                for t in range(n) for o in range(N_CHIPS)]

    def start(src, dst, sems):
        for cp in copies(src, dst, sems):
            cp.start()

    def finish(src, dst, sems):
        cps = copies(src, dst, sems)
        for cp in cps:
            cp.wait_recv()
        for cp in cps:
            cp.wait_send()

    return _Exchange(list(grads), [jax.ShapeDtypeStruct((N_CHIPS,) + g.shape[2:], g.dtype) for g in grads],
                     [pltpu.SemaphoreType.DMA((k,)), pltpu.SemaphoreType.DMA((k,))], start, finish)


def scatter_chips(parts):
    n = len(parts)
    k = n * (N_CHIPS - 1)

    def copy(src, dst, sems, t, r, landing):
        send, recv = sems
        x, y, c = _coords()
        me = 2 * x + y
        px, py = _chip_peer(x, y, r)
        peer = 2 * px + py
        s = t * (N_CHIPS - 1) + r - 1
        return pltpu.make_async_remote_copy(src[t].at[me if landing else peer], dst[t].at[peer if landing else me],
                                            send.at[s], recv.at[s], device_id=(px, py, c), device_id_type=_MESH)

    def start(src, dst, sems):
        for t in range(n):
            for r in range(1, N_CHIPS):
                copy(src, dst, sems, t, r, False).start()

    def finish(src, dst, sems):
        for t in range(n):
            for r in range(1, N_CHIPS):
                copy(src, dst, sems, t, r, True).wait_recv()
        for t in range(n):
            for r in range(1, N_CHIPS):
                copy(src, dst, sems, t, r, False).wait_send()

    return _Exchange(list(parts), [jax.ShapeDtypeStruct(p.shape, p.dtype) for p in parts],
                     [pltpu.SemaphoreType.DMA((k,)), pltpu.SemaphoreType.DMA((k,))], start, finish)


def sibling_send(halves, name):
    n = len(halves)

    def body(*refs):
        src, dst = refs[:n], refs[n:2 * n]
        send, recv = refs[2 * n:]
        x, y, c = _coords()
        cps = [pltpu.make_async_remote_copy(src[t], dst[t], send.at[t], recv.at[t],
                                            device_id=(x, y, 1 - c), device_id_type=_MESH) for t in range(n)]
        for cp in cps:
            cp.start()
        for cp in cps:
            cp.wait_recv()
        for cp in cps:
            cp.wait_send()

    return pl.pallas_call(
        body, name=name, in_specs=[_ANY] * n, out_specs=[_ANY] * n,
        out_shape=[jax.ShapeDtypeStruct(h.shape, h.dtype) for h in halves],
        scratch_shapes=[pltpu.SemaphoreType.DMA((n,)), pltpu.SemaphoreType.DMA((n,))],
    )(*halves)


def gather_all(block, name):
    def body(src, dst, send, recv, loc):
        x, y, c = _coords()
        me = 4 * x + 2 * y + c
        mine = pltpu.make_async_copy(src, dst.at[me], loc)
        mine.start()
        outs = []
        for r in range(1, N_DEV):
            peer = (x ^ (r >> 2), y ^ ((r >> 1) & 1), c ^ (r & 1))
            outs.append(pltpu.make_async_remote_copy(src, dst.at[me], send.at[r - 1], recv.at[r - 1],
                                                     device_id=peer, device_id_type=_MESH))
        for cp in outs:
            cp.start()
        for r in range(1, N_DEV):
            px, py, pc = x ^ (r >> 2), y ^ ((r >> 1) & 1), c ^ (r & 1)
            pltpu.make_async_remote_copy(src, dst.at[4 * px + 2 * py + pc], send.at[r - 1], recv.at[r - 1],
                                         device_id=(px, py, pc), device_id_type=_MESH).wait_recv()
        for cp in outs:
            cp.wait_send()
        mine.wait()

    return pl.pallas_call(
        body, name=name, in_specs=[_ANY], out_specs=_ANY,
        out_shape=jax.ShapeDtypeStruct((N_DEV,) + block.shape, block.dtype),
        scratch_shapes=[pltpu.SemaphoreType.DMA((N_DEV - 1,)), pltpu.SemaphoreType.DMA((N_DEV - 1,)),
                        pltpu.SemaphoreType.DMA(())],
    )(block)


def _pad_lanes(v, n=LANE):
    return jnp.pad(v, ((0, 0), (0, n - v.shape[1])))


def _w_in_pieces():
    cs = (PROJ_BIG + 2 * HEADS) // N_CHIPS
    ab_end = AB_COL + 2 * HEADS
    out = []
    for o in range(N_CHIPS):
        lo, hi = o * cs, (o + 1) * cs
        cand = [("big", lo, min(hi, AB_COL), 0), ("ab", max(lo, AB_COL), min(hi, ab_end), AB_COL),
                ("big", max(lo, ab_end), hi, 2 * HEADS)]
        out.append([(s, a - off, b - off) for s, a, b, off in cand if a < b])
    return out


def _split_w_in(w4):
    big, ab = [], []
    for o, pieces in enumerate(_w_in_pieces()):
        at = 0
        for s, a, b in pieces:
            (big if s == "big" else ab).append(w4[o][:, at:at + b - a])
            at += b - a
    return jnp.concatenate(big, axis=1), _pad_lanes(jnp.concatenate(ab, axis=1))


def _join_w_in(big, ab):
    src = {"big": big, "ab": ab}
    return jnp.stack([jnp.concatenate([src[s][:, a:b] for s, a, b in pieces], axis=1) for pieces in _w_in_pieces()])


def _conv_w8(w):
    return jnp.pad(w, ((0, SUB - DN_CONV), (0, 0)))


def _row_layout(gc, tp):
    nc = tp // CHUNK
    g = gc[:, :, 0].reshape(HEADS, nc, 1, CHUNK)
    g = jnp.broadcast_to(g, (HEADS, nc, SUB, CHUNK))
    return jnp.pad(g, ((0, 0), (0, 0), (0, 0), (0, LANE - CHUNK))).reshape(HEADS, nc * SUB, LANE)


def _step(x, meta, W, target, late_weights=None, early_swap=None, early_grads=None, last_grads=None):
    W = dict(W)
    seq = x.shape[0]
    tp = P0 + N_META + seq
    h0 = jnp.concatenate([jnp.zeros((P0, D_MODEL), f32), meta, x], axis=0)
    w_big, w_ab = _split_w_in(W["w_in"])
    cq8, ck8, cv8 = _conv_w8(W["conv_q"]), _conv_w8(W["conv_k"]), _conv_w8(W["conv_v"])
    al, dtb = _pad_lanes(W["dn_a_log"]), _pad_lanes(W["dn_dt_bias"])

    n1 = rms_fwd(h0, W["norm_mix_gain"], "rms1_fwd")
    proj = matmul(n1, w_big, "nn", "proj_fwd")
    pab = matmul(n1, w_ab, "nn", "pab_fwd")
    qn = conv_fwd(proj, cq8, C_DQ * 8, 8, True, "conv_q_fwd")
    kn = conv_fwd(proj, ck8, C_DK * 8, 8, True, "conv_k_fwd")
    va = conv_fwd(proj, cv8, C_DV * 8, 16, False, "conv_v_fwd")
    gc, bc = ab_fwd(pab, al, dtb)
    grow = _row_layout(gc, tp)
    o_dn, states = gdn_fwd(qn, kn, va, gc, bc, grow)
    on = dn_out_fwd(o_dn, proj, W["dn_out_norm_gain"])
    qs, ks, vs = sb_prep_fwd(proj, W["sb_q_norm_gain"], W["sb_k_norm_gain"])
    o_sb, o_sb16, arrived = sb_fwd(qs, ks, vs, rider=late_weights[0] if late_weights else None)
    if late_weights:
        W.update(late_weights[1](arrived))
    ydn = matmul(on, W["w_branch_dn"], "nn", "ydn_fwd")
    ysb = matmul(o_sb16, W["w_branch_sb"], "nn", "ysb_fwd")
    merged = merge_fwd(proj, ydn, ysb)
    h1 = matmul(merged, W["w_out"], "nn", "wout_fwd", residual=h0)
    n2 = rms_fwd(h1, W["norm_ffn_gain"], "rms2_fwd")
    u = matmul(n2, W["w_ffn_in"], "nn", "ffn_in_fwd", tn_t=512)
    act = swiglu_fwd(u)
    y = matmul(act, W["w_ffn_out"], "nn", "ffn_out_fwd", residual=h1)
    dy, dy16, loss = loss_head(y, target)

    G = {}
    dact = matmul(dy16, W["w_ffn_out"], "nt", "ffn_out_dx", tn_t=1408)
    G["w_ffn_out"] = matmul(act, dy16, "tn", "ffn_out_dw", tm_t=1408)
    dgate, dup = swiglu_bwd(u, dact)
    du = jnp.concatenate([dgate, dup], axis=1)
    dn2 = matmul(du, W["w_ffn_in"], "nt", "ffn_in_dx", tk_t=512)
    G["w_ffn_in"] = matmul(n2, du, "tn", "ffn_in_dw", tn_t=512)
    dh1, dh1_16, G["norm_ffn_gain"] = rms_bwd(h1, W["norm_ffn_gain"], dn2, dy, "rms2_bwd")
    dmerged = matmul(dh1_16, W["w_out"], "nt", "wout_dx")
    G["w_out"] = matmul(merged, dh1_16, "tn", "wout_dw")
    dyd, dys, dgd, dgs = merge_bwd(proj, ydn, ysb, dmerged)
    don = matmul(dyd, W["w_branch_dn"], "nt", "ydn_dx")
    G["w_branch_dn"] = matmul(on, dyd, "tn", "ydn_dw")
    do_sb = matmul(dys, W["w_branch_sb"], "nt", "ysb_dx")
    G["w_branch_sb"] = matmul(o_sb16, dys, "tn", "ysb_dw")
    do_dn, dz, G["dn_out_norm_gain"] = dn_out_bwd(o_dn, proj, W["dn_out_norm_gain"], don)
    (dqn, dkn, dva, dgc, dbc), swapped = gdn_bwd(qn, kn, va, gc, bc, grow, states, do_dn,
                                                 rider=early_swap[0](G) if early_swap else None)
    if early_swap:
        early_swap[1](swapped)
    dpab, dal, ddt = ab_bwd(pab, al, dtb, dgc, dbc)
    G["dn_a_log"], G["dn_dt_bias"] = dal[:, :HEADS], ddt[:, :HEADS]
    dyq, dcq = conv_bwd_act(proj, cq8, dqn, C_DQ * 8, 8, True, "conv_q_bwd")
    dyk, dck = conv_bwd_act(proj, ck8, dkn, C_DK * 8, 8, True, "conv_k_bwd")
    dyv, dcv = conv_bwd_act(proj, cv8, dva, C_DV * 8, 16, False, "conv_v_bwd")
    G["conv_q"], G["conv_k"], G["conv_v"] = dcq[:DN_CONV], dck[:DN_CONV], dcv[:DN_CONV]
    d_dq = conv_bwd_in(dyq, cq8, "conv_q_dx")
    d_dk = conv_bwd_in(dyk, ck8, "conv_k_dx")
    d_dv = conv_bwd_in(dyv, cv8, "conv_v_dx")
    (dqs, dks, dvs), delivered = sb_bwd(qs, ks, vs, o_sb, do_sb, rider=early_grads[0](G) if early_grads else None)
    if early_grads:
        early_grads[1](delivered)
    d_sq, d_sk, d_sv, G["sb_q_norm_gain"], G["sb_k_norm_gain"] = sb_prep_bwd(
        proj, W["sb_q_norm_gain"], W["sb_k_norm_gain"], dqs, dks, dvs)
    dproj = jnp.concatenate([d_dq, d_dk, d_dv, dz, d_sq, d_sk, d_sv, dgd, dgs], axis=1)
    dw_big = matmul(n1, dproj, "tn", "proj_dw")
    dw_ab = matmul(n1, dpab, "tn", "pab_dw")
    G["w_in"] = _join_w_in(dw_big, dw_ab)
    if last_grads:
        dn1, delivered = matmul(dproj, w_big, "nt", "proj_dx", tk_t=1024, rider=last_grads[0](G))
        last_grads[1](delivered)
    else:
        dn1 = matmul(dproj, w_big, "nt", "proj_dx", tk_t=1024)
    dn1 = matmul(dpab, w_ab, "nt", "pab_dx", residual=dn1)
    dh0, _, G["norm_mix_gain"] = rms_bwd(h0, W["norm_mix_gain"], dn1, dh1, "rms1_bwd")
    G["meta_tokens"] = dh0[P0:P0 + N_META]
    return loss, dh0[P0 + N_META:], G


_BIG = ("w_in", "w_branch_dn", "w_branch_sb", "w_out", "w_ffn_in", "w_ffn_out")
_COL_SHARDED = ("w_in", "w_ffn_in", "meta_tokens", "conv_q", "conv_k", "conv_v")
_SMALL_REPL = ("norm_mix_gain", "norm_ffn_gain", "dn_a_log", "dn_dt_bias", "dn_out_norm_gain", "sb_q_norm_gain",
               "sb_k_norm_gain")
_SMALL_SHARD = ("meta_tokens", "conv_q", "conv_k", "conv_v")
_ORDER = ("meta_tokens", "norm_mix_gain", "w_in", "conv_q", "conv_k", "conv_v", "dn_a_log", "dn_dt_bias",
          "dn_out_norm_gain", "sb_q_norm_gain", "sb_k_norm_gain", "w_branch_dn", "w_branch_sb", "w_out",
          "norm_ffn_gain", "w_ffn_in", "w_ffn_out")


def _unshard(g4, name):
    if name in _COL_SHARDED:
        r, cs = g4.shape[1:]
        return jnp.transpose(g4, (1, 0, 2)).reshape(r, N_CHIPS * cs)
    return g4.reshape((-1,) + g4.shape[2:])


def _to_shards(full, name):
    if full.ndim == 3:
        return full
    if name in _COL_SHARDED:
        r, c = full.shape
        return jnp.transpose(full.reshape(r, N_CHIPS, c // N_CHIPS), (1, 0, 2))
    r, c = full.shape
    return full.reshape(N_CHIPS, r // N_CHIPS, c)


def _rows_1024(a):
    r, c = a.shape
    if c >= 1024:
        return a.reshape(r * (c // 1024), 1024)
    return jnp.pad(a, ((0, 0), (0, 1024 - c)))


def kernel(x, meta_tokens, norm_mix_gain, w_in, conv_q, conv_k, conv_v, dn_a_log, dn_dt_bias, dn_out_norm_gain, sb_q_norm_gain, sb_k_norm_gain, w_branch_dn, w_branch_sb, w_out, norm_ffn_gain, w_ffn_in, w_ffn_out, loss_target, m_meta_tokens, m_norm_mix_gain, m_w_in, m_conv_q, m_conv_k, m_conv_v, m_dn_a_log, m_dn_dt_bias, m_dn_out_norm_gain, m_sb_q_norm_gain, m_sb_k_norm_gain, m_w_branch_dn, m_w_branch_sb, m_w_out, m_norm_ffn_gain, m_w_ffn_in, m_w_ffn_out, v_meta_tokens, v_norm_mix_gain, v_w_in, v_conv_q, v_conv_k, v_conv_v, v_dn_a_log, v_dn_dt_bias, v_dn_out_norm_gain, v_sb_q_norm_gain, v_sb_k_norm_gain, v_w_branch_dn, v_w_branch_sb, v_w_out, v_norm_ffn_gain, v_w_ffn_in, v_w_ffn_out):
    Wl = dict(meta_tokens=meta_tokens, norm_mix_gain=norm_mix_gain, w_in=w_in[0], conv_q=conv_q[0], conv_k=conv_k[0],
              conv_v=conv_v[0], dn_a_log=dn_a_log, dn_dt_bias=dn_dt_bias, dn_out_norm_gain=dn_out_norm_gain,
              sb_q_norm_gain=sb_q_norm_gain, sb_k_norm_gain=sb_k_norm_gain, w_branch_dn=w_branch_dn[0],
              w_branch_sb=w_branch_sb[0], w_out=w_out[0], norm_ffn_gain=norm_ffn_gain, w_ffn_in=w_ffn_in[0],
              w_ffn_out=w_ffn_out[0])
    Ml = dict(meta_tokens=m_meta_tokens, norm_mix_gain=m_norm_mix_gain, w_in=m_w_in[0], conv_q=m_conv_q[0],
              conv_k=m_conv_k[0], conv_v=m_conv_v[0], dn_a_log=m_dn_a_log, dn_dt_bias=m_dn_dt_bias,
              dn_out_norm_gain=m_dn_out_norm_gain, sb_q_norm_gain=m_sb_q_norm_gain, sb_k_norm_gain=m_sb_k_norm_gain,
              w_branch_dn=m_w_branch_dn[0], w_branch_sb=m_w_branch_sb[0], w_out=m_w_out[0],
              norm_ffn_gain=m_norm_ffn_gain, w_ffn_in=m_w_ffn_in[0], w_ffn_out=m_w_ffn_out[0])
    Vl = dict(meta_tokens=v_meta_tokens, norm_mix_gain=v_norm_mix_gain, w_in=v_w_in[0], conv_q=v_conv_q[0],
              conv_k=v_conv_k[0], conv_v=v_conv_v[0], dn_a_log=v_dn_a_log, dn_dt_bias=v_dn_dt_bias,
              dn_out_norm_gain=v_dn_out_norm_gain, sb_q_norm_gain=v_sb_q_norm_gain, sb_k_norm_gain=v_sb_k_norm_gain,
              w_branch_dn=v_w_branch_dn[0], w_branch_sb=v_w_branch_sb[0], w_out=v_w_out[0],
              norm_ffn_gain=v_norm_ffn_gain, w_ffn_in=v_w_ffn_in[0], w_ffn_out=v_w_ffn_out[0])
    lead = {n: (1,) if (n in _BIG or n in ("conv_q", "conv_k", "conv_v")) else () for n in _ORDER}

    chip = 2 * lax.axis_index("x") + lax.axis_index("y")
    c = lax.axis_index("c")
    halved = {n: Wl[n].astype(bf16).reshape(2, Wl[n].shape[0] // 2, Wl[n].shape[1]) for n in _BIG}

    def gathered_weights(names, owns, outs):
        res = {}
        for n, own, g4 in zip(names, owns, outs):
            g4 = lax.dynamic_update_slice(g4, own[None], (chip,) + (0,) * own.ndim)
            if n in _BIG:
                g4 = g4.reshape(N_CHIPS, 2 * g4.shape[2], g4.shape[3])
            res[n] = g4 if n == "w_in" else _unshard(g4, n)
        return res

    first = ["w_in"] + list(_SMALL_SHARD)
    first_own = [halved["w_in"]] + [Wl[n] for n in _SMALL_SHARD]
    W = dict(Wl)
    W.update(gathered_weights(first, first_own, run_exchange(gather_chips(first_own[:1], first_own[1:]), "gather_w_in")))
    late = [n for n in _BIG if n != "w_in"]
    late_own = [halved[n] for n in late]
    for n in late:
        del W[n]

    def halves_of(names, G):
        g4 = [_to_shards(G[n], n) for n in names]
        return [g.reshape(N_CHIPS, 2, g.shape[1] // 2, g.shape[2]) for g in g4]

    def pair_added(g42, from_sib, tag, wire):
        mine = [lax.dynamic_index_in_dim(g, c, axis=1, keepdims=False) for g in g42]
        return [add2(a, b, "grad_pair_add_%s%d" % (tag, t), out_dtype=wire)
                for t, (a, b) in enumerate(zip(mine, from_sib))]

    def chip_reduced(parts, slots, tag):
        slots = [lax.dynamic_update_slice(s, lax.dynamic_index_in_dim(p, chip, axis=0, keepdims=True), (chip, 0, 0))
                 for s, p in zip(slots, parts)]
        return [sum_slots(s, "grad_chip_sum_%s%d" % (tag, t)) for t, s in enumerate(slots)]

    early, last = {}, {}

    def early_swap_begin(G):
        early["g42"] = halves_of(late, G)
        return sibling_swap(early["g42"])

    def early_begin(G):
        early["parts"] = pair_added(early["g42"], early["from_sib"], "a", f32)
        return scatter_chips(early["parts"])

    def last_begin(G):
        g42 = halves_of(["w_in"], G)
        last["parts"] = pair_added(g42, run_exchange(sibling_swap(g42), "grad_sibling_swap_b"), "b", bf16)
        return scatter_chips(last["parts"])

    loss, grad_x, G = _step(
        x[0], W["meta_tokens"], W, loss_target[0],
        late_weights=(gather_chips(late_own, []), lambda outs: gathered_weights(late, late_own, outs)),
        early_swap=(early_swap_begin, lambda outs: early.update(from_sib=outs)),
        early_grads=(early_begin, lambda slots: early.update(halves=chip_reduced(early["parts"], slots, "a"))),
        last_grads=(last_begin, lambda slots: last.update(halves=chip_reduced(last["parts"], slots, "b"))))
    halves = last["halves"] + early["halves"]
    theirs = sibling_send(halves, "grad_sibling_send")
    Gs = {}
    for n, h, o in zip(["w_in"] + late, halves, theirs):
        Gs[n] = lax.dynamic_update_slice(jnp.concatenate([o, o], axis=0), h, (c * h.shape[0], 0))

    small_names = list(_SMALL_REPL) + list(_SMALL_SHARD)
    pieces = [_rows_1024(G[n]) for n in small_names] + [_rows_1024(loss)]
    counts = [p.shape[0] for p in pieces]
    pack = jnp.concatenate(pieces, axis=0)
    pad_rows = (-pack.shape[0]) % SUB
    pack = jnp.pad(pack, ((0, pad_rows), (0, 0)))
    total = sum_slots(gather_all(pack, "small_gather"), "small_sum")
    chip = 2 * lax.axis_index("x") + lax.axis_index("y")
    row = 0
    for n, cnt in zip(small_names, counts[:-1]):
        blk = total[row:row + cnt]
        row += cnt
        full_shape = G[n].shape
        if full_shape[1] >= 1024:
            blk = blk.reshape(full_shape)
        else:
            blk = blk[:, :full_shape[1]]
        if n in _SMALL_SHARD:
            cs = full_shape[1] // N_CHIPS
            blk = lax.dynamic_slice_in_dim(blk, chip * cs, cs, axis=1)
        Gs[n] = blk
    loss_out = total[row, 0]

    grads, deltas, new_m, new_v = [], [], [], []
    for n in _ORDER:
        d, m2, v2 = adamw(Wl[n], Gs[n], Ml[n], Vl[n], "adamw_" + n)
        shape = lead[n] + Wl[n].shape
        grads.append(Gs[n].reshape(shape))
        deltas.append(d.reshape(shape))
        new_m.append(m2.reshape(shape))
        new_v.append(v2.reshape(shape))
    return (loss_out, grad_x[None], *grads, *deltas, *new_m, *new_v)
```

```python
import jax
import jax.numpy as jnp
from jax import lax
from jax.experimental import pallas as pl
from jax.experimental.pallas import tpu as pltpu

f32 = jnp.float32
bf16 = jnp.bfloat16

D_MODEL = 1024
N_META = 16
CHUNK = 64
HEADS = 8
DN_DK = 128
DN_DV = 256
DN_CONV = 4
DN_QK = HEADS * DN_DK
DN_V = HEADS * DN_DV
SB_DH = 128
SB_W = HEADS * SB_DH
SB_BLOCK = 128
SB_QB = 384
SB_GROUP = 4
SB_HEADS_PER_STEP = 2
GDN_HEADS_PER_STEP = 8
CONV_W = 512
D_FF = 2816
RMS_EPS = 1e-6
L2_EPS = 1e-6
ADAM_LR = 0.001
ADAM_B1 = 0.9
ADAM_B2 = 0.999
ADAM_EPS = 1e-08
ADAM_WD = 0.01
ADAM_STEP = 10

P0 = 112
LANE = 128
SUB = 8
VMEM_LIMIT = 48 * 1024 * 1024
N_CHIPS = 4
N_DEV = 8

C_DQ, C_DK, C_DV, C_DZ, C_SQ, C_SK, C_SV, C_GDN, C_GSB = 0, 1, 2, 4, 6, 7, 8, 9, 10
PROJ_BIG = 11 * 1024
AB_COL = 2 * DN_QK + 2 * DN_V


def _params(n_axes):
    return pltpu.CompilerParams(dimension_semantics=("arbitrary",) * n_axes, vmem_limit_bytes=VMEM_LIMIT)


def _tile(n, target, q=LANE):
    best = None
    for t in range(q, min(n, target) + 1, q):
        if n % t == 0:
            best = t
    return best if best is not None else n


def _dot(a, b):
    return jnp.dot(a.astype(bf16), b.astype(bf16), preferred_element_type=f32)


def _dot_nt(a, b):
    return lax.dot_general(a.astype(bf16), b.astype(bf16), (((1,), (1,)), ((), ())), preferred_element_type=f32)


def _dot_tn(a, b):
    return lax.dot_general(a.astype(bf16), b.astype(bf16), (((0,), (0,)), ((), ())), preferred_element_type=f32)


_HI = lax.Precision.HIGH


def _hdot(a, b):
    return jnp.dot(a, b, precision=_HI, preferred_element_type=f32)


def _hdot_nt(a, b):
    return lax.dot_general(a, b, (((1,), (1,)), ((), ())), precision=_HI, preferred_element_type=f32)


def _hdot_tn(a, b):
    return lax.dot_general(a, b, (((0,), (0,)), ((), ())), precision=_HI, preferred_element_type=f32)


def _sigmoid(x):
    e = jnp.exp(-jnp.abs(x))
    r = 1.0 / (1.0 + e)
    return jnp.where(x >= 0, r, e * r)


def _log1p_small(e):
    return jnp.where(e < 1e-3, e * (1.0 - e * (0.5 - e * (1.0 / 3.0))), jnp.log(1.0 + e))


def _rowsum(x):
    return jnp.sum(x, axis=1, keepdims=True)


def _allsum(x):
    return jnp.sum(jnp.sum(x, axis=1, keepdims=True), axis=0, keepdims=True)


def matmul(a, b, mode, name, residual=None, out_dtype=f32, tm_t=1408, tn_t=1024, tk_t=1408, rider=None):
    if mode == "nn":
        (M, K), (K2, N) = a.shape, b.shape
    elif mode == "nt":
        (M, K), (N, K2) = a.shape, b.shape
    else:
        (K, M), (K2, N) = a.shape, b.shape
    assert K == K2, (a.shape, b.shape, mode)
    tm, tn, tk = _tile(M, tm_t), _tile(N, tn_t), _tile(K, tk_t)
    nk = K // tk
    if mode == "nn":
        a_spec = pl.BlockSpec((tm, tk), lambda i, j, k: (i, k))
        b_spec = pl.BlockSpec((tk, tn), lambda i, j, k: (k, j))
        dims = (((1,), (0,)), ((), ()))
    elif mode == "nt":
        a_spec = pl.BlockSpec((tm, tk), lambda i, j, k: (i, k))
        b_spec = pl.BlockSpec((tn, tk), lambda i, j, k: (j, k))
        dims = (((1,), (1,)), ((), ()))
    else:
        a_spec = pl.BlockSpec((tk, tm), lambda i, j, k: (k, i))
        b_spec = pl.BlockSpec((tk, tn), lambda i, j, k: (k, j))
        dims = (((0,), (0,)), ((), ()))
    o_spec = pl.BlockSpec((tm, tn), lambda i, j, k: (i, j))
    has_res = residual is not None
    grid = (M // tm, N // tn, nk)
    split, ride_first, ride_last = _ride(rider, 3 if has_res else 2, 1, grid)

    def body(*refs):
        ins_, (o_ref,), (rin, rout, rest) = split(refs)
        a_ref, b_ref = ins_[:2]
        r_ref = ins_[2] if has_res else None
        acc_ref, ride = rest[0], (rin, rout, rest[1:])
        ride_first(ride)
        k = pl.program_id(2)

        @pl.when(k == 0)
        def _():
            acc_ref[...] = jnp.zeros_like(acc_ref)

        acc_ref[...] += lax.dot_general(a_ref[...].astype(bf16), b_ref[...].astype(bf16), dims,
                                        preferred_element_type=f32)

        @pl.when(k == nk - 1)
        def _():
            r = acc_ref[...]
            if has_res:
                r = r + r_ref[...]
            o_ref[...] = r.astype(out_dtype)

        ride_last(ride)

    ins = [a, b] + ([residual] if has_res else [])
    specs = [a_spec, b_spec] + ([o_spec] if has_res else [])
    r_ins = rider.ins if rider else []
    r_outs = rider.out_shapes if rider else []
    res = pl.pallas_call(
        body, name=name, grid=grid, in_specs=specs + [_ANY] * len(r_ins), out_specs=[o_spec] + [_ANY] * len(r_outs),
        out_shape=[jax.ShapeDtypeStruct((M, N), out_dtype)] + list(r_outs),
        scratch_shapes=[pltpu.VMEM((tm, tn), f32)] + (rider.scratch if rider else []), compiler_params=_params(3),
    )(*ins, *r_ins)
    return (res[0], res[1:]) if rider else res[0]


def _row_tile(tp):
    return _tile(tp, 512)


def rms_fwd(h, gain, name):
    tp, d = h.shape
    rt = _row_tile(tp)

    def body(h_ref, g_ref, o_ref):
        x = h_ref[...]
        r = lax.rsqrt(jnp.mean(x * x, axis=-1, keepdims=True) + RMS_EPS)
        o_ref[...] = (x * r * g_ref[...]).astype(bf16)

    return pl.pallas_call(
        body, name=name, grid=(tp // rt,),
        in_specs=[pl.BlockSpec((rt, d), lambda i: (i, 0)), pl.BlockSpec((1, d), lambda i: (0, 0))],
        out_specs=pl.BlockSpec((rt, d), lambda i: (i, 0)),
        out_shape=jax.ShapeDtypeStruct((tp, d), bf16), compiler_params=_params(1),
    )(h, gain)


def rms_bwd(h, gain, dn, dres, name):
    tp, d = h.shape
    rt = _row_tile(tp)

    def body(h_ref, g_ref, dn_ref, dr_ref, dh_ref, dhb_ref, dg_ref):
        i = pl.program_id(0)
        x = h_ref[...]
        r = lax.rsqrt(jnp.mean(x * x, axis=-1, keepdims=True) + RMS_EPS)
        xh = x * r
        dn_ = dn_ref[...]
        dxh = dn_ * g_ref[...]
        dh = r * (dxh - xh * jnp.mean(dxh * xh, axis=-1, keepdims=True)) + dr_ref[...]
        dh_ref[...] = dh
        dhb_ref[...] = dh.astype(bf16)
        part = jnp.sum(dn_ * xh, axis=0, keepdims=True)

        @pl.when(i == 0)
        def _():
            dg_ref[...] = part

        @pl.when(i > 0)
        def _():
            dg_ref[...] += part

    row = pl.BlockSpec((rt, d), lambda i: (i, 0))
    vec = pl.BlockSpec((1, d), lambda i: (0, 0))
    return pl.pallas_call(
        body, name=name, grid=(tp // rt,), in_specs=[row, vec, row, row], out_specs=[row, row, vec],
        out_shape=[jax.ShapeDtypeStruct((tp, d), f32), jax.ShapeDtypeStruct((tp, d), bf16),
                   jax.ShapeDtypeStruct((1, d), f32)],
        compiler_params=_params(1),
    )(h, gain, dn, dres)


def loss_head(y, target):
    tp, d = y.shape
    rt = P0 + N_META
    assert rt == SB_BLOCK and tp % rt == 0 and target.shape == (tp - rt, d)

    def body(y_ref, t_ref, dy_ref, dyb_ref, l_ref):
        i = pl.program_id(0)

        @pl.when(i == 0)
        def _():
            dy_ref[...] = jnp.zeros_like(dy_ref)
            dyb_ref[...] = jnp.zeros_like(dyb_ref)
            l_ref[...] = jnp.zeros_like(l_ref)

        @pl.when(i > 0)
        def _():
            err = y_ref[...] - t_ref[...]
            dy = err * (1.0 / d)
            dy_ref[...] = dy
            dyb_ref[...] = dy.astype(bf16)
            l_ref[...] += jnp.broadcast_to(_allsum(err * err) * (0.5 / d), l_ref.shape)

    row = pl.BlockSpec((rt, d), lambda i: (i, 0))
    return pl.pallas_call(
        body, name="loss_head", grid=(tp // rt,),
        in_specs=[row, pl.BlockSpec((rt, d), lambda i: (jnp.maximum(i - 1, 0), 0))],
        out_specs=[row, row, pl.BlockSpec((1, LANE), lambda i: (0, 0))],
        out_shape=[jax.ShapeDtypeStruct((tp, d), f32), jax.ShapeDtypeStruct((tp, d), bf16),
                   jax.ShapeDtypeStruct((1, LANE), f32)],
        compiler_params=_params(1),
    )(y, target)


def swiglu_fwd(u):
    tp = u.shape[0]
    rt, cb = _row_tile(tp), D_FF // 2
    nb = D_FF // cb

    def body(g_ref, u_ref, o_ref):
        g = g_ref[...]
        o_ref[...] = (g * _sigmoid(g) * u_ref[...]).astype(bf16)

    return pl.pallas_call(
        body, name="swiglu_fwd", grid=(tp // rt, nb),
        in_specs=[pl.BlockSpec((rt, cb), lambda i, j: (i, j)), pl.BlockSpec((rt, cb), lambda i, j: (i, j + nb))],
        out_specs=pl.BlockSpec((rt, cb), lambda i, j: (i, j)),
        out_shape=jax.ShapeDtypeStruct((tp, D_FF), bf16), compiler_params=_params(2),
    )(u, u)


def swiglu_bwd(u, dact):
    tp = u.shape[0]
    rt, cb = _row_tile(tp), D_FF // 2
    nb = D_FF // cb

    def body(g_ref, u_ref, da_ref, dg_ref, du_ref):
        g = g_ref[...]
        s = _sigmoid(g)
        da = da_ref[...]
        dg_ref[...] = (da * u_ref[...] * s * (1.0 + g * (1.0 - s))).astype(bf16)
        du_ref[...] = (da * g * s).astype(bf16)

    lo = pl.BlockSpec((rt, cb), lambda i, j: (i, j))
    hi = pl.BlockSpec((rt, cb), lambda i, j: (i, j + nb))
    dgate, dup = pl.pallas_call(
        body, name="swiglu_bwd", grid=(tp // rt, nb), in_specs=[lo, hi, lo], out_specs=[lo, lo],
        out_shape=[jax.ShapeDtypeStruct((tp, D_FF), bf16)] * 2, compiler_params=_params(2),
    )(u, u, dact)
    return dgate, dup


def merge_fwd(proj, ydn, ysb):
    tp = proj.shape[0]
    rt, d = _row_tile(tp), D_MODEL

    def body(gd_ref, gs_ref, yd_ref, ys_ref, o_ref):
        o_ref[...] = (_sigmoid(gd_ref[...]) * yd_ref[...] + _sigmoid(gs_ref[...]) * ys_ref[...]).astype(bf16)

    row = pl.BlockSpec((rt, d), lambda i: (i, 0))
    return pl.pallas_call(
        body, name="merge_fwd", grid=(tp // rt,),
        in_specs=[pl.BlockSpec((rt, d), lambda i: (i, C_GDN)), pl.BlockSpec((rt, d), lambda i: (i, C_GSB)), row, row],
        out_specs=row, out_shape=jax.ShapeDtypeStruct((tp, d), bf16), compiler_params=_params(1),
    )(proj, proj, ydn, ysb)


def merge_bwd(proj, ydn, ysb, dm):
    tp = proj.shape[0]
    rt, d = _row_tile(tp), D_MODEL

    def body(gd_ref, gs_ref, yd_ref, ys_ref, dm_ref, dyd_ref, dys_ref, dgd_ref, dgs_ref):
        dm_ = dm_ref[...]
        sd = _sigmoid(gd_ref[...])
        ss = _sigmoid(gs_ref[...])
        dyd_ref[...] = (dm_ * sd).astype(bf16)
        dys_ref[...] = (dm_ * ss).astype(bf16)
        dgd_ref[...] = (dm_ * yd_ref[...] * sd * (1.0 - sd)).astype(bf16)
        dgs_ref[...] = (dm_ * ys_ref[...] * ss * (1.0 - ss)).astype(bf16)

    row = pl.BlockSpec((rt, d), lambda i: (i, 0))
    return pl.pallas_call(
        body, name="merge_bwd", grid=(tp // rt,),
        in_specs=[pl.BlockSpec((rt, d), lambda i: (i, C_GDN)), pl.BlockSpec((rt, d), lambda i: (i, C_GSB)), row, row, row],
        out_specs=[row] * 4, out_shape=[jax.ShapeDtypeStruct((tp, d), bf16)] * 4, compiler_params=_params(1),
    )(proj, proj, ydn, ysb, dm)


def dn_out_fwd(o, proj, gain):
    tp = o.shape[0]
    rt, cb, wide = _row_tile(tp), DN_DV, 1024
    zb = C_DZ * 1024 // wide

    def body(o_ref, z_ref, g_ref, y_ref):
        for s in range(wide // cb):
            sl = slice(s * cb, (s + 1) * cb)
            x = o_ref[:, sl]
            r = lax.rsqrt(jnp.mean(x * x, axis=-1, keepdims=True) + RMS_EPS)
            z = z_ref[:, sl]
            y_ref[:, sl] = (x * r * g_ref[...] * (z * _sigmoid(z))).astype(bf16)

    blk = pl.BlockSpec((rt, wide), lambda i, j: (i, j))
    return pl.pallas_call(
        body, name="dn_out_fwd", grid=(tp // rt, DN_V // wide),
        in_specs=[blk, pl.BlockSpec((rt, wide), lambda i, j: (i, j + zb)), pl.BlockSpec((1, cb), lambda i, j: (0, 0))],
        out_specs=blk, out_shape=jax.ShapeDtypeStruct((tp, DN_V), bf16), compiler_params=_params(2),
    )(o, proj, gain)


def dn_out_bwd(o, proj, gain, dy):
    tp = o.shape[0]
    rt, cb, wide = _row_tile(tp), DN_DV, 1024
    zb = C_DZ * 1024 // wide

    def body(o_ref, z_ref, g_ref, dy_ref, do_ref, dz_ref, dg_ref):
        i, j = pl.program_id(0), pl.program_id(1)
        g = g_ref[...]
        part = jnp.zeros((1, cb), f32)
        for hh in range(wide // cb):
            sl = slice(hh * cb, (hh + 1) * cb)
            x = o_ref[:, sl]
            r = lax.rsqrt(jnp.mean(x * x, axis=-1, keepdims=True) + RMS_EPS)
            xh = x * r
            z = z_ref[:, sl]
            s = _sigmoid(z)
            dy_ = dy_ref[:, sl]
            drn = dy_ * (z * s)
            dz_ref[:, sl] = (dy_ * xh * g * s * (1.0 + z * (1.0 - s))).astype(bf16)
            dxh = drn * g
            do_ref[:, sl] = r * (dxh - xh * jnp.mean(dxh * xh, axis=-1, keepdims=True))
            part = part + jnp.sum(drn * xh, axis=0, keepdims=True)
        first = jnp.logical_and(i == 0, j == 0)

        @pl.when(first)
        def _():
            dg_ref[...] = part

        @pl.when(jnp.logical_not(first))
        def _():
            dg_ref[...] += part

    blk = pl.BlockSpec((rt, wide), lambda i, j: (i, j))
    vec = pl.BlockSpec((1, cb), lambda i, j: (0, 0))
    return pl.pallas_call(
        body, name="dn_out_bwd", grid=(tp // rt, DN_V // wide),
        in_specs=[blk, pl.BlockSpec((rt, wide), lambda i, j: (i, j + zb)), vec, blk],
        out_specs=[blk, blk, vec],
        out_shape=[jax.ShapeDtypeStruct((tp, DN_V), f32), jax.ShapeDtypeStruct((tp, DN_V), bf16),
                   jax.ShapeDtypeStruct((1, cb), f32)],
        compiler_params=_params(2),
    )(o, proj, gain, dy)


def sb_prep_fwd(proj, gq, gk):
    tp = proj.shape[0]
    rt, cb = _row_tile(tp), SB_DH

    def body(q_ref, k_ref, v_ref, gq_ref, gk_ref, qo_ref, ko_ref, vo_ref):
        for x_ref, g_ref, o_ref in ((q_ref, gq_ref, qo_ref), (k_ref, gk_ref, ko_ref)):
            for h in range(HEADS):
                sl = slice(h * cb, (h + 1) * cb)
                x = x_ref[:, sl]
                r = lax.rsqrt(jnp.mean(x * x, axis=-1, keepdims=True) + RMS_EPS)
                o_ref[:, sl] = (x * r * g_ref[...]).astype(bf16)
        vo_ref[...] = v_ref[...].astype(bf16)

    blk = pl.BlockSpec((rt, SB_W), lambda i: (i, 0))
    vec = pl.BlockSpec((1, cb), lambda i: (0, 0))
    return pl.pallas_call(
        body, name="sb_prep_fwd", grid=(tp // rt,),
        in_specs=[pl.BlockSpec((rt, SB_W), lambda i: (i, C_SQ)), pl.BlockSpec((rt, SB_W), lambda i: (i, C_SK)),
                  pl.BlockSpec((rt, SB_W), lambda i: (i, C_SV)), vec, vec],
        out_specs=[blk] * 3, out_shape=[jax.ShapeDtypeStruct((tp, SB_W), bf16)] * 3, compiler_params=_params(1),
    )(proj, proj, proj, gq, gk)


def sb_prep_bwd(proj, gq, gk, dqs, dks, dvs):
    tp = proj.shape[0]
    rt, cb = _row_tile(tp), SB_DH

    def body(q_ref, k_ref, gq_ref, gk_ref, dq_ref, dk_ref, dv_ref, dqo_ref, dko_ref, dvo_ref, dgq_ref, dgk_ref):
        first = pl.program_id(0) == 0
        dvo_ref[...] = dv_ref[...].astype(bf16)
        for x_ref, g_ref, dn_ref, dx_ref, dg_ref in ((q_ref, gq_ref, dq_ref, dqo_ref, dgq_ref),
                                                     (k_ref, gk_ref, dk_ref, dko_ref, dgk_ref)):
            part = jnp.zeros((1, cb), f32)
            for h in range(HEADS):
                sl = slice(h * cb, (h + 1) * cb)
                x = x_ref[:, sl]
                r = lax.rsqrt(jnp.mean(x * x, axis=-1, keepdims=True) + RMS_EPS)
                xh = x * r
                dn_ = dn_ref[:, sl]
                dxh = dn_ * g_ref[...]
                dx_ref[:, sl] = (r * (dxh - xh * jnp.mean(dxh * xh, axis=-1, keepdims=True))).astype(bf16)
                part = part + jnp.sum(dn_ * xh, axis=0, keepdims=True)

            @pl.when(first)
            def _(dg_ref=dg_ref, part=part):
                dg_ref[...] = part

            @pl.when(jnp.logical_not(first))
            def _(dg_ref=dg_ref, part=part):
                dg_ref[...] += part

    blk = pl.BlockSpec((rt, SB_W), lambda i: (i, 0))
    vec = pl.BlockSpec((1, cb), lambda i: (0, 0))
    return pl.pallas_call(
        body, name="sb_prep_bwd", grid=(tp // rt,),
        in_specs=[pl.BlockSpec((rt, SB_W), lambda i: (i, C_SQ)), pl.BlockSpec((rt, SB_W), lambda i: (i, C_SK)),
                  vec, vec, blk, blk, blk],
        out_specs=[blk, blk, blk, vec, vec],
        out_shape=[jax.ShapeDtypeStruct((tp, SB_W), bf16)] * 3 + [jax.ShapeDtypeStruct((1, cb), f32)] * 2,
        compiler_params=_params(1),
    )(proj, proj, gq, gk, dqs, dks, dvs)


def _conv_taps(ext, rt):
    taps = []
    for k in range(DN_CONV):
        s = DN_CONV - 1 - k
        taps.append((pltpu.roll(ext, s, axis=0) if s else ext)[SUB:SUB + rt])
    return taps


def _conv_act(taps, w, l2):
    y = taps[0] * w[0:1]
    for k in range(1, DN_CONV):
        y = y + taps[k] * w[k:k + 1]
    s = _sigmoid(y)
    a = y * s
    if l2:
        n = lax.rsqrt(jnp.sum(a * a, axis=-1, keepdims=True) + L2_EPS)
        return y, s, a, n
    return y, s, a, None


def conv_fwd(proj, w8, col_blk, ncb, l2, name):
    tp = proj.shape[0]
    rt = _row_tile(tp)
    hb = rt // SUB
    cw = CONV_W
    cb0 = col_blk * LANE // cw

    def body(x_ref, h_ref, w_ref, o_ref):
        i = pl.program_id(1)
        first = (i > 0).astype(f32)
        for s in range(cw // LANE):
            sl = slice(s * LANE, (s + 1) * LANE)
            ext = jnp.concatenate([h_ref[:, sl] * first, x_ref[:, sl]], axis=0)
            _, _, a, n = _conv_act(_conv_taps(ext, rt), w_ref[:, sl], l2)
            o_ref[:, sl] = a * n if l2 else a

    return pl.pallas_call(
        body, name=name, grid=(ncb * LANE // cw, tp // rt),
        in_specs=[pl.BlockSpec((rt, cw), lambda j, i: (i, j + cb0)),
                  pl.BlockSpec((SUB, cw), lambda j, i: (jnp.maximum(i * hb - 1, 0), j + cb0)),
                  pl.BlockSpec((SUB, cw), lambda j, i: (0, j))],
        out_specs=pl.BlockSpec((rt, cw), lambda j, i: (i, j)),
        out_shape=jax.ShapeDtypeStruct((tp, ncb * LANE), f32), compiler_params=_params(2),
    )(proj, proj, w8)


def conv_bwd_act(proj, w8, dout, col_blk, ncb, l2, name):
    tp = proj.shape[0]
    rt = _row_tile(tp)
    hb = rt // SUB
    cw = CONV_W
    cb0 = col_blk * LANE // cw

    def body(x_ref, h_ref, w_ref, d_ref, dy_ref, dw_ref):
        i = pl.program_id(1)
        first = (i > 0).astype(f32)
        rows = lax.broadcasted_iota(jnp.int32, (SUB, LANE), 0)
        for s in range(cw // LANE):
            sl = slice(s * LANE, (s + 1) * LANE)
            ext = jnp.concatenate([h_ref[:, sl] * first, x_ref[:, sl]], axis=0)
            taps = _conv_taps(ext, rt)
            y, sg, a, n = _conv_act(taps, w_ref[:, sl], l2)
            da = d_ref[:, sl]
            if l2:
                out = a * n
                da = n * (da - out * jnp.sum(da * out, axis=-1, keepdims=True))
            dy = da * sg * (1.0 + y * (1.0 - sg))
            dy_ref[:, sl] = dy
            part = jnp.zeros((SUB, LANE), f32)
            for k in range(DN_CONV):
                part = part + jnp.where(rows == k, jnp.sum(taps[k] * dy, axis=0, keepdims=True), 0.0)

            @pl.when(i == 0)
            def _(sl=sl, part=part):
                dw_ref[:, sl] = part

            @pl.when(i > 0)
            def _(sl=sl, part=part):
                dw_ref[:, sl] += part

    return pl.pallas_call(
        body, name=name, grid=(ncb * LANE // cw, tp // rt),
        in_specs=[pl.BlockSpec((rt, cw), lambda j, i: (i, j + cb0)),
                  pl.BlockSpec((SUB, cw), lambda j, i: (jnp.maximum(i * hb - 1, 0), j + cb0)),
                  pl.BlockSpec((SUB, cw), lambda j, i: (0, j)),
                  pl.BlockSpec((rt, cw), lambda j, i: (i, j))],
        out_specs=[pl.BlockSpec((rt, cw), lambda j, i: (i, j)), pl.BlockSpec((SUB, cw), lambda j, i: (0, j))],
        out_shape=[jax.ShapeDtypeStruct((tp, ncb * LANE), f32), jax.ShapeDtypeStruct((SUB, ncb * LANE), f32)],
        compiler_params=_params(2),
    )(proj, proj, w8, dout)


def conv_bwd_in(dy, w8, name):
    tp, cols = dy.shape
    rt = _row_tile(tp)
    hb = rt // SUB
    nr = tp // rt
    last8 = tp // SUB - 1
    cw = CONV_W

    def body(d_ref, h_ref, w_ref, o_ref):
        i = pl.program_id(1)
        last = (i < nr - 1).astype(f32)
        for c0 in range(cw // LANE):
            sl = slice(c0 * LANE, (c0 + 1) * LANE)
            ext = jnp.concatenate([d_ref[:, sl], h_ref[:, sl] * last], axis=0)
            w = w_ref[:, sl]
            acc = None
            for k in range(DN_CONV):
                s = DN_CONV - 1 - k
                sh = (pltpu.roll(ext, rt + SUB - s, axis=0) if s else ext)[0:rt]
                term = sh * w[k:k + 1]
                acc = term if acc is None else acc + term
            o_ref[:, sl] = acc.astype(bf16)

    return pl.pallas_call(
        body, name=name, grid=(cols // cw, nr),
        in_specs=[pl.BlockSpec((rt, cw), lambda j, i: (i, j)),
                  pl.BlockSpec((SUB, cw), lambda j, i: (jnp.minimum((i + 1) * hb, last8), j)),
                  pl.BlockSpec((SUB, cw), lambda j, i: (0, j))],
        out_specs=pl.BlockSpec((rt, cw), lambda j, i: (i, j)),
        out_shape=jax.ShapeDtypeStruct((tp, cols), bf16), compiler_params=_params(2),
    )(dy, dy, w8)


def _ab_common(p, al, dtb, r0):
    rows = r0 + lax.broadcasted_iota(jnp.int32, p.shape, 0)
    mask = (rows >= P0).astype(f32)
    xx = p + dtb
    sp = jnp.maximum(xx, 0.0) + _log1p_small(jnp.exp(-jnp.abs(xx)))
    ea = jnp.exp(al)
    g = -ea * sp * mask
    beta = _sigmoid(p) * mask
    return g, beta, _sigmoid(xx), ea, mask


def ab_fwd(pab, al, dtb):
    tp = pab.shape[0]
    rt = _row_tile(tp)

    def body(p_ref, al_ref, dt_ref, g_ref, b_ref):
        i = pl.program_id(0)
        g, beta, _, _, _ = _ab_common(p_ref[...], al_ref[...], dt_ref[...], i * rt)
        for h in range(HEADS):
            g_ref[h] = jnp.broadcast_to(g[:, h:h + 1], (rt, LANE))
            b_ref[h] = jnp.broadcast_to(beta[:, HEADS + h:HEADS + h + 1], (rt, LANE))

    vec = pl.BlockSpec((1, LANE), lambda i: (0, 0))
    out = pl.BlockSpec((HEADS, rt, LANE), lambda i: (0, i, 0))
    return pl.pallas_call(
        body, name="ab_fwd", grid=(tp // rt,), in_specs=[pl.BlockSpec((rt, LANE), lambda i: (i, 0)), vec, vec],
        out_specs=[out, out], out_shape=[jax.ShapeDtypeStruct((HEADS, tp, LANE), f32)] * 2, compiler_params=_params(1),
    )(pab, al, dtb)


def ab_bwd(pab, al, dtb, dg, db):
    tp = pab.shape[0]
    rt = _row_tile(tp)

    def body(p_ref, al_ref, dt_ref, dg_ref, db_ref, dp_ref, dal_ref, ddt_ref):
        i = pl.program_id(0)
        g, beta, sx, ea, mask = _ab_common(p_ref[...], al_ref[...], dt_ref[...], i * rt)
        lanes = lax.broadcasted_iota(jnp.int32, (rt, LANE), 1)
        dgl = jnp.zeros((rt, LANE), f32)
        dbl = jnp.zeros((rt, LANE), f32)
        for h in range(HEADS):
            dgl = dgl + jnp.where(lanes == h, dg_ref[h], 0.0)
            dbl = dbl + jnp.where(lanes == HEADS + h, db_ref[h], 0.0)
        dxx = dgl * (-ea) * sx * mask
        dp_ref[...] = (dxx + dbl * beta * (1.0 - beta)).astype(bf16)
        pal = jnp.sum(dgl * g, axis=0, keepdims=True)
        pdt = jnp.sum(dxx, axis=0, keepdims=True)

        @pl.when(i == 0)
        def _():
            dal_ref[...] = pal
            ddt_ref[...] = pdt

        @pl.when(i > 0)
        def _():
            dal_ref[...] += pal
            ddt_ref[...] += pdt

    vec = pl.BlockSpec((1, LANE), lambda i: (0, 0))
    row = pl.BlockSpec((rt, LANE), lambda i: (i, 0))
    big = pl.BlockSpec((HEADS, rt, LANE), lambda i: (0, i, 0))
    return pl.pallas_call(
        body, name="ab_bwd", grid=(tp // rt,), in_specs=[row, vec, vec, big, big], out_specs=[row, vec, vec],
        out_shape=[jax.ShapeDtypeStruct((tp, LANE), bf16), jax.ShapeDtypeStruct((1, LANE), f32),
                   jax.ShapeDtypeStruct((1, LANE), f32)],
        compiler_params=_params(1),
    )(pab, al, dtb, dg, db)


class _Chunk:
    pass


def _gdn_chunk(q, k, v, gcol, bcol, grow8):
    C = CHUNK
    R = range(len(q))
    X = _Chunk()
    ri = lax.broadcasted_iota(jnp.int32, (C, C), 0)
    ci = lax.broadcasted_iota(jnp.int32, (C, C), 1)
    r2 = lax.broadcasted_iota(jnp.int32, (LANE, LANE), 0)
    c2 = lax.broadcasted_iota(jnp.int32, (LANE, LANE), 1)
    lower = (ri >= ci).astype(f32)
    upper2 = (r2 <= c2).astype(f32)
    eye = (ri == ci).astype(f32)
    gam = [_hdot(lower, gcol[h]) for h in R]
    gam_row = [_hdot(grow8[h], upper2)[0:1, 0:C] for h in R]
    X.ri, X.ci = ri, ci
    X.Dm = [jnp.where(ri >= ci, jnp.exp(jnp.minimum(gam[h][:, 0:C] - gam_row[h], 0.0)), 0.0) for h in R]
    X.eg = [jnp.exp(gam[h]) for h in R]
    gl = [gam[h][C - 1:C, :] for h in R]
    X.egl = [jnp.exp(gl[h]) for h in R]
    X.kdec = [jnp.exp(gl[h] - gam[h]) for h in R]
    X.qs = [q[h] * (DN_DK ** -0.5) for h in R]
    X.kb = [k[h] * bcol[h] for h in R]
    kk = [_dot_nt(X.kb[h], k[h]) for h in R]
    qk = [_dot_nt(X.qs[h], k[h]) for h in R]
    X.A = [jnp.where(ri > ci, kk[h] * X.Dm[h], 0.0) for h in R]
    T = [eye - X.A[h] for h in R]
    P = list(X.A)
    for _ in range(5):
        P = [_hdot(P[h], P[h]) for h in R]
        T = [T[h] + _hdot(T[h], P[h]) for h in R]
    X.T = T
    X.b2 = [jnp.concatenate([bcol[h], bcol[h]], axis=-1) for h in R]
    X.u = [_hdot(T[h], v[h] * X.b2[h]) for h in R]
    X.w = [_hdot(T[h], X.kb[h] * X.eg[h]) for h in R]
    X.attn = [qk[h] * X.Dm[h] for h in R]
    X.qg = [X.qs[h] * X.eg[h] for h in R]
    X.kg = [k[h] * X.kdec[h] for h in R]
    return X


def gdn_fwd(q, k, v, gc, bc, grow):
    tp = q.shape[0]
    nc = tp // CHUNK
    hb = GDN_HEADS_PER_STEP

    def body(q_ref, k_ref, v_ref, gc_ref, bc_ref, gr_ref, o_ref, ss_ref, S_ref):
        c = pl.program_id(1)

        @pl.when(c == 0)
        def _():
            S_ref[...] = jnp.zeros_like(S_ref)

        R = range(hb)
        qc = [slice(h * DN_DK, (h + 1) * DN_DK) for h in R]
        vc = [slice(h * DN_DV, (h + 1) * DN_DV) for h in R]
        X = _gdn_chunk([q_ref[:, qc[h]] for h in R], [k_ref[:, qc[h]] for h in R], [v_ref[:, vc[h]] for h in R],
                       [gc_ref[h] for h in R], [bc_ref[h] for h in R], [gr_ref[h] for h in R])
        S = [S_ref[h] for h in R]
        for h in R:
            ss_ref[h, 0] = S[h]
        wS = [_dot(X.w[h], S[h]) for h in R]
        qS = [_dot(X.qg[h], S[h]) for h in R]
        vn = [X.u[h] - wS[h] for h in R]
        av = [_dot(X.attn[h], vn[h]) for h in R]
        kv = [_dot_tn(X.kg[h], vn[h]) for h in R]
        for h in R:
            o_ref[:, vc[h]] = qS[h] + av[h]
            S_ref[h] = S[h] * X.egl[h][:, 0:1] + kv[h]

    qk = pl.BlockSpec((CHUNK, hb * DN_DK), lambda g, c: (c, g))
    vv = pl.BlockSpec((CHUNK, hb * DN_DV), lambda g, c: (c, g))
    col = pl.BlockSpec((hb, CHUNK, LANE), lambda g, c: (g, c, 0))
    row = pl.BlockSpec((hb, SUB, LANE), lambda g, c: (g, c, 0))
    return pl.pallas_call(
        body, name="gdn_fwd", grid=(HEADS // hb, nc), in_specs=[qk, qk, vv, col, col, row],
        out_specs=[vv, pl.BlockSpec((hb, 1, DN_DK, DN_DV), lambda g, c: (g, c, 0, 0))],
        out_shape=[jax.ShapeDtypeStruct((tp, DN_V), f32), jax.ShapeDtypeStruct((HEADS, nc, DN_DK, DN_DV), f32)],
        scratch_shapes=[pltpu.VMEM((hb, DN_DK, DN_DV), f32)], compiler_params=_params(2),
    )(q, k, v, gc, bc, grow)


def gdn_bwd(q, k, v, gc, bc, grow, states, do, rider=None):
    tp = q.shape[0]
    nc = tp // CHUNK
    C = CHUNK
    hb = GDN_HEADS_PER_STEP
    grid = (HEADS // hb, nc)
    split, ride_first, ride_last = _ride(rider, 8, 5, grid)

    def body(*refs):
        ((q_ref, k_ref, v_ref, gc_ref, bc_ref, gr_ref, ss_ref, do_ref), (dq_ref, dk_ref, dv_ref, dg_ref, db_ref),
         (rin, rout, rest)) = split(refs)
        dS_ref, ride = rest[0], (rin, rout, rest[1:])
        ride_first(ride)
        c = pl.program_id(1)

        @pl.when(c == 0)
        def _():
            dS_ref[...] = jnp.zeros_like(dS_ref)

        R = range(hb)
        qc = [slice(h * DN_DK, (h + 1) * DN_DK) for h in R]
        vc = [slice(h * DN_DV, (h + 1) * DN_DV) for h in R]
        k_ = [k_ref[:, qc[h]] for h in R]
        v_ = [v_ref[:, vc[h]] for h in R]
        bcol = [bc_ref[h] for h in R]
        X = _gdn_chunk([q_ref[:, qc[h]] for h in R], k_, v_, [gc_ref[h] for h in R], bcol, [gr_ref[h] for h in R])
        ri, ci = X.ri, X.ci
        S = [ss_ref[h, 0] for h in R]
        do_ = [do_ref[:, vc[h]] for h in R]
        dSn = [dS_ref[h] for h in R]
        wS = [_dot(X.w[h], S[h]) for h in R]
        ado = [_dot_tn(X.attn[h], do_[h]) for h in R]
        kdS = [_dot(X.kg[h], dSn[h]) for h in R]
        d_qg = [_dot_nt(do_[h], S[h]) for h in R]
        qdo = [_dot_tn(X.qg[h], do_[h]) for h in R]
        vn = [X.u[h] - wS[h] for h in R]
        d_vn = [ado[h] + kdS[h] for h in R]
        dovn = [_dot_nt(do_[h], vn[h]) for h in R]
        d_kg = [_dot_nt(vn[h], dSn[h]) for h in R]
        wdv = [_dot_tn(X.w[h], d_vn[h]) for h in R]
        dw = [-_dot_nt(d_vn[h], S[h]) for h in R]
        for h in R:
            dS_ref[h] = qdo[h] + X.egl[h][:, 0:1] * dSn[h] - wdv[h]
        dattn = [jnp.where(ri >= ci, dovn[h], 0.0) for h in R]
        dRu = [_hdot_tn(X.T[h], d_vn[h]) for h in R]
        dRw = [_hdot_tn(X.T[h], dw[h]) for h in R]
        dAu = [_hdot_nt(dRu[h], X.u[h]) for h in R]
        dAw = [_hdot_nt(dRw[h], X.w[h]) for h in R]
        dA = [jnp.where(ri > ci, -(dAu[h] + dAw[h]), 0.0) for h in R]
        dKK = [dA[h] * X.Dm[h] for h in R]
        dQK = [dattn[h] * X.Dm[h] for h in R]
        E = [dA[h] * X.A[h] + dattn[h] * X.attn[h] for h in R]
        dkb = [_dot(dKK[h], k_[h]) + dRw[h] * X.eg[h] for h in R]
        dk1 = [_dot_tn(dKK[h], X.kb[h]) for h in R]
        dqs = [_dot(dQK[h], k_[h]) + d_qg[h] * X.eg[h] for h in R]
        dk2 = [_dot_tn(dQK[h], X.qs[h]) for h in R]
        ones = jnp.ones((C, LANE), f32)
        colE = [_hdot_tn(E[h], ones) for h in R]
        rows = lax.broadcasted_iota(jnp.int32, (C, LANE), 0)
        upper = (ci >= ri).astype(f32)
        dgam = []
        for h in R:
            t = d_kg[h] * X.kg[h]
            dgl = _allsum(t) + X.egl[h][:, 0:1] * _allsum(S[h] * dSn[h])
            g = (_rowsum(E[h]) - colE[h] + _rowsum(dRw[h] * (X.kb[h] * X.eg[h])) + _rowsum(d_qg[h] * X.qg[h])
                 - _rowsum(t))
            dgam.append(g + jnp.where(rows == C - 1, dgl, 0.0))
        dg = [_hdot(upper, dgam[h]) for h in R]
        for h in R:
            dv_ref[:, vc[h]] = dRu[h] * X.b2[h]
            dbeta = _rowsum(dRu[h] * v_[h]) + _rowsum(dkb[h] * k_[h])
            dq_ref[:, qc[h]] = dqs[h] * (DN_DK ** -0.5)
            dk_ref[:, qc[h]] = dk1[h] + dk2[h] + dkb[h] * bcol[h] + d_kg[h] * X.kdec[h]
            dg_ref[h] = dg[h]
            db_ref[h] = jnp.broadcast_to(dbeta, (C, LANE))
        ride_last(ride)

    rc = lambda c: nc - 1 - c
    qk = pl.BlockSpec((CHUNK, hb * DN_DK), lambda g, c: (rc(c), g))
    vv = pl.BlockSpec((CHUNK, hb * DN_DV), lambda g, c: (rc(c), g))
    col = pl.BlockSpec((hb, CHUNK, LANE), lambda g, c: (g, rc(c), 0))
    row = pl.BlockSpec((hb, SUB, LANE), lambda g, c: (g, rc(c), 0))
    st = pl.BlockSpec((hb, 1, DN_DK, DN_DV), lambda g, c: (g, rc(c), 0, 0))
    r_ins = rider.ins if rider else []
    r_outs = rider.out_shapes if rider else []
    res = pl.pallas_call(
        body, name="gdn_bwd", grid=grid, in_specs=[qk, qk, vv, col, col, row, st, vv] + [_ANY] * len(r_ins),
        out_specs=[qk, qk, vv, col, col] + [_ANY] * len(r_outs),
        out_shape=[jax.ShapeDtypeStruct((tp, DN_QK), f32), jax.ShapeDtypeStruct((tp, DN_QK), f32),
                   jax.ShapeDtypeStruct((tp, DN_V), f32), jax.ShapeDtypeStruct((HEADS, tp, LANE), f32),
                   jax.ShapeDtypeStruct((HEADS, tp, LANE), f32)] + list(r_outs),
        scratch_shapes=[pltpu.VMEM((hb, DN_DK, DN_DV), f32)] + (rider.scratch if rider else []),
        compiler_params=_params(2),
    )(q, k, v, gc, bc, grow, states, do, *r_ins)
    return res[:5], res[5:]


def _cumsum_after(x, nb, us):
    B, n = SB_BLOCK, x.shape[0]
    hi = x.astype(bf16)
    lo = (x - hi.astype(f32)).astype(bf16)
    rows = [p[:, b * B:(b + 1) * B] for p in (hi, lo) for b in range(nb)]
    r = jnp.dot(jnp.concatenate(rows, axis=0), us, preferred_element_type=f32)
    out = [r[b * n:(b + 1) * n] + r[(nb + b) * n:(nb + b + 1) * n] for b in range(nb)]
    return out[0] if nb == 1 else jnp.concatenate(out, axis=1)


def _later_blocks(x, nb, carry):
    B = SB_BLOCK
    tot = [_rowsum(x[:, b * B:(b + 1) * B]) for b in range(nb)]
    offs = [None] * nb
    run = carry
    for b in range(nb - 1, -1, -1):
        offs[b] = jnp.broadcast_to(run, (x.shape[0], B))
        run = run + tot[b]
    return (offs[0] if nb == 1 else jnp.concatenate(offs, axis=1)), run


def _sb_group(i, t):
    top = (i + 1) * (SB_QB // SB_BLOCK) - 1 - SB_GROUP * t
    jlo = jnp.maximum(top - SB_GROUP + 1, 0)
    rows = pl.ds(pl.multiple_of(jlo * SB_BLOCK, SB_BLOCK), SB_GROUP * SB_BLOCK)
    return jlo, rows, (top + 1) * SB_BLOCK


def _sb_weights(q, kcat, i, jlo, kend, cs, us, masked):
    B, nb = SB_BLOCK, SB_GROUP
    R = range(len(q))
    z = [_dot_nt(q[h], kcat[h]) * (SB_DH ** -0.5) for h in R]
    e = [jnp.exp(-jnp.abs(z[h])) for h in R]
    l1p = [jnp.log(1.0 + e[h]) for h in R]
    lsp = [jnp.minimum(z[h], 0.0) - l1p[h] for h in R]
    lk = [lsp[h] - z[h] for h in R]
    vis = None
    if masked:
        qpos = i * SB_QB + lax.broadcasted_iota(jnp.int32, (SB_QB, nb * B), 0)
        kpos = jlo * B + lax.broadcasted_iota(jnp.int32, (SB_QB, nb * B), 1)
        vis = jnp.logical_and(kpos < jnp.minimum(qpos, kend), kpos >= P0)
        lk = [jnp.where(vis, lk[h], 0.0) for h in R]
    later = [_later_blocks(lk[h], nb, cs[h]) for h in R]
    cum = [_cumsum_after(lk[h], nb, us) for h in R]
    w = [jnp.exp(lsp[h] + cum[h] + later[h][0]) for h in R]
    if masked:
        w = [jnp.where(vis, w[h], 0.0) for h in R]
    return lsp, vis, w, [later[h][1] for h in R]


def _sb_loop(i, step, carry):
    trips = ((i + 1) * (SB_QB // SB_BLOCK) - 1 + SB_GROUP) // SB_GROUP
    carry = step(True)(0, carry)
    carry = lax.fori_loop(1, trips - 1, step(False), carry)
    return lax.fori_loop(jnp.maximum(trips - 1, 1), trips, step(True), carry)


def _ride(rider, n_in, n_out, grid):
    n_rin = len(rider.ins) if rider else 0
    n_rout = len(rider.out_shapes) if rider else 0

    def split(refs):
        ins, rin = refs[:n_in], refs[n_in:n_in + n_rin]
        outs = refs[n_in + n_rin:n_in + n_rin + n_out]
        rout = refs[n_in + n_rin + n_out:n_in + n_rin + n_out + n_rout]
        return ins, outs, (rin, rout, refs[n_in + n_rin + n_out + n_rout:])

    def at(step, fn, r):
        if rider is None:
            return
        cond = None
        for a, g in enumerate(grid):
            c = pl.program_id(a) == (g - 1 if step == "last" else 0)
            cond = c if cond is None else jnp.logical_and(cond, c)

        @pl.when(cond)
        def _():
            fn(*r)

    first = lambda r: at("first", rider.start if rider else None, r)
    last = lambda r: at("last", rider.finish if rider else None, r)
    return split, first, last


def sb_fwd(qs, ks, vs, rider=None):
    tp = qs.shape[0]
    nq = tp // SB_QB
    B, G, hb, QB = SB_BLOCK, SB_GROUP, SB_HEADS_PER_STEP, SB_QB
    assert tp >= G * B and tp % QB == 0 and QB % B == 0 and G * B >= QB
    grid = (HEADS // hb, nq)
    split, ride_first, ride_last = _ride(rider, 3, 2, grid)

    def body(*refs):
        (q_ref, k_ref, v_ref), (o_ref, ob_ref), ride = split(refs)
        ride_first(ride)
        i = pl.program_id(1)
        R = range(hb)
        hs = [slice(h * SB_DH, (h + 1) * SB_DH) for h in R]
        q = [q_ref[:, hs[h]] for h in R]
        us = (lax.broadcasted_iota(jnp.int32, (B, B), 0) > lax.broadcasted_iota(jnp.int32, (B, B), 1)).astype(bf16)

        def make_step(masked):
            def step(t, carry):
                acc, cs = carry
                jlo, rows, kend = _sb_group(i, t)
                _, _, w, cs = _sb_weights(q, [k_ref[rows, hs[h]] for h in R], i, jlo, kend, cs, us, masked)
                pv = [_dot(w[h], v_ref[rows, hs[h]]) for h in R]
                return tuple(acc[h] + pv[h] for h in R), tuple(cs)
            return step

        carry = (tuple(jnp.zeros((QB, SB_DH), f32) for _ in R), tuple(jnp.zeros((QB, 1), f32) for _ in R))
        acc, _ = _sb_loop(i, make_step, carry)
        for h in R:
            o_ref[:, hs[h]] = acc[h]
            ob_ref[:, hs[h]] = acc[h].astype(bf16)
        ride_last(ride)

    blk = pl.BlockSpec((QB, hb * SB_DH), lambda g, i: (i, g))
    full = pl.BlockSpec((tp, hb * SB_DH), lambda g, i: (0, g))
    r_ins = rider.ins if rider else []
    r_outs = rider.out_shapes if rider else []
    res = pl.pallas_call(
        body, name="sb_fwd", grid=grid, in_specs=[blk, full, full] + [_ANY] * len(r_ins),
        out_specs=[blk, blk] + [_ANY] * len(r_outs),
        out_shape=[jax.ShapeDtypeStruct((tp, SB_W), f32), jax.ShapeDtypeStruct((tp, SB_W), bf16)] + list(r_outs),
        scratch_shapes=rider.scratch if rider else [], compiler_params=_params(2),
    )(qs, ks, vs, *r_ins)
    return res[0], res[1], res[2:]


def sb_bwd(qs, ks, vs, o, do, rider=None):
    tp = qs.shape[0]
    nq = tp // SB_QB
    B, G, hb, QB = SB_BLOCK, SB_GROUP, SB_HEADS_PER_STEP, SB_QB
    assert tp >= G * B and tp % QB == 0 and QB % B == 0 and G * B >= QB
    grid = (HEADS // hb, nq)
    split, ride_first, ride_last = _ride(rider, 5, 3, grid)

    def body(*refs):
        (q_ref, k_ref, v_ref, o_ref, do_ref), (dq_ref, dk_ref, dv_ref), ride = split(refs)
        ride_first(ride)
        i = pl.program_id(1)

        @pl.when(i == 0)
        def _():
            dk_ref[...] = jnp.zeros_like(dk_ref)
            dv_ref[...] = jnp.zeros_like(dv_ref)

        R = range(hb)
        hs = [slice(h * SB_DH, (h + 1) * SB_DH) for h in R]
        q = [q_ref[:, hs[h]] for h in R]
        dob = [do_ref[:, hs[h]].astype(bf16) for h in R]
        et = [_rowsum(dob[h].astype(f32) * o_ref[:, hs[h]]) for h in R]
        us = (lax.broadcasted_iota(jnp.int32, (B, B), 0) > lax.broadcasted_iota(jnp.int32, (B, B), 1)).astype(bf16)

        def make_step(masked):
            def step(t, carry):
                dq, cs, ce = carry
                jlo, rows, kend = _sb_group(i, t)
                kcat = [k_ref[rows, hs[h]] for h in R]
                dwv = [_dot_nt(dob[h], v_ref[rows, hs[h]]) for h in R]
                lsp, vis, w, cs = _sb_weights(q, kcat, i, jlo, kend, cs, us, masked)
                wb = [w[h].astype(bf16) for h in R]
                ee = [dwv[h] * wb[h].astype(f32) for h in R]
                later = [_later_blocks(ee[h], G, ce[h]) for h in R]
                cum = [_cumsum_after(ee[h], G, us) for h in R]
                dz = []
                for h in R:
                    d = ee[h] - jnp.exp(lsp[h]) * (et[h] - (cum[h] + later[h][0]))
                    if masked:
                        d = jnp.where(vis, d, 0.0)
                    dz.append((d * (SB_DH ** -0.5)).astype(bf16))
                dkj = [_dot_tn(dz[h], q[h]) for h in R]
                dvj = [_dot_tn(wb[h], dob[h]) for h in R]
                dqj = [_dot(dz[h], kcat[h]) for h in R]
                for h in R:
                    dk_ref[rows, hs[h]] += dkj[h]
                    dv_ref[rows, hs[h]] += dvj[h]
                return tuple(dq[h] + dqj[h] for h in R), tuple(cs), tuple(later[h][1] for h in R)
            return step

        z0 = tuple(jnp.zeros((QB, 1), f32) for _ in R)
        dq, _, _ = _sb_loop(i, make_step, (tuple(jnp.zeros((QB, SB_DH), f32) for _ in R), z0, z0))
        for h in R:
            dq_ref[:, hs[h]] = dq[h]
        ride_last(ride)

    blk = pl.BlockSpec((QB, hb * SB_DH), lambda g, i: (i, g))
    full = pl.BlockSpec((tp, hb * SB_DH), lambda g, i: (0, g))
    r_ins = rider.ins if rider else []
    r_outs = rider.out_shapes if rider else []
    res = pl.pallas_call(
        body, name="sb_bwd", grid=grid, in_specs=[blk, full, full, blk, blk] + [_ANY] * len(r_ins),
        out_specs=[blk, full, full] + [_ANY] * len(r_outs),
        out_shape=[jax.ShapeDtypeStruct((tp, SB_W), f32)] * 3 + list(r_outs),
        scratch_shapes=rider.scratch if rider else [], compiler_params=_params(2),
    )(qs, ks, vs, o, do, *r_ins)
    return res[:3], res[3:]


def adamw(w, g, m, v, name):
    r, c = w.shape
    rt = _tile(r, 128, SUB) if r % SUB == 0 else r
    blk = pl.BlockSpec((rt, c), lambda i: (i, 0))
    c1 =1.0 - ADAM_B1 ** ADAM_STEP
    c2 = 1.0 - ADAM_B2 ** ADAM_STEP

    def body(w_ref, g_ref, m_ref, v_ref, d_ref, mo_ref, vo_ref):
        g_ = g_ref[...]
        m_ = ADAM_B1 * m_ref[...] + (1.0 - ADAM_B1) * g_
        v_ = ADAM_B2 * v_ref[...] + (1.0 - ADAM_B2) * (g_ * g_)
        mo_ref[...] = m_
        vo_ref[...] = v_
        d_ref[...] = -ADAM_LR * ((m_ / c1) / (jnp.sqrt(v_ / c2) + ADAM_EPS) + ADAM_WD * w_ref[...])

    return pl.pallas_call(
        body, name=name, grid=(r // rt,), in_specs=[blk] * 4, out_specs=[blk] * 3,
        out_shape=[jax.ShapeDtypeStruct((r, c), f32)] * 3, compiler_params=_params(1),
    )(w, g, m, v)


def sum_slots(x, name):
    n, r, c = x.shape
    rt = _tile(r, 128, SUB) if r % SUB == 0 else r
    blk = pl.BlockSpec((n, rt, c), lambda i: (0, i, 0))

    def body(x_ref, o_ref):
        acc = x_ref[0].astype(f32)
        for s in range(1, n):
            acc = acc + x_ref[s].astype(f32)
        o_ref[...] = acc

    return pl.pallas_call(
        body, name=name, grid=(r // rt,), in_specs=[blk], out_specs=pl.BlockSpec((rt, c), lambda i: (i, 0)),
        out_shape=jax.ShapeDtypeStruct((r, c), f32), compiler_params=_params(1),
    )(x)


def add2(a, b, name, out_dtype=f32):
    n, r, c = a.shape
    rt = _tile(r, 64, SUB) if r % SUB == 0 else r
    blk = pl.BlockSpec((n, rt, c), lambda i: (0, i, 0))

    def body(a_ref, b_ref, o_ref):
        o_ref[...] = (a_ref[...] + b_ref[...]).astype(out_dtype)

    return pl.pallas_call(
        body, name=name, grid=(r // rt,), in_specs=[blk, blk], out_specs=blk,
        out_shape=jax.ShapeDtypeStruct((n, r, c), out_dtype), compiler_params=_params(1),
    )(a, b)


_ANY = pl.BlockSpec(memory_space=pl.ANY)
_MESH = pl.DeviceIdType.MESH


def _coords():
    return lax.axis_index("x"), lax.axis_index("y"), lax.axis_index("c")


def _chip_peer(x, y, r):
    return x ^ (r >> 1), y ^ (r & 1)


class _Exchange:
    def __init__(self, ins, out_shapes, scratch, start, finish):
        self.ins, self.out_shapes, self.scratch, self.start, self.finish = ins, out_shapes, scratch, start, finish

    def split(self, refs):
        n, m = len(self.ins), len(self.out_shapes)
        return refs[:n], refs[n:n + m], refs[n + m:]


def run_exchange(ex, name):
    def body(*refs):
        ins, outs, sems = ex.split(refs)
        ex.start(ins, outs, sems)
        ex.finish(ins, outs, sems)

    return pl.pallas_call(body, name=name, in_specs=[_ANY] * len(ex.ins), out_specs=[_ANY] * len(ex.out_shapes),
                          out_shape=ex.out_shapes, scratch_shapes=ex.scratch)(*ex.ins)


def gather_chips(big, small):
    nb, n = len(big), len(big) + len(small)
    shards = list(big) + list(small)
    kb = nb * (N_CHIPS - 1)
    k = n * (N_CHIPS - 1)

    def copies(src, dst, sems):
        send, recv, fsend, frecv = sems
        x, y, c = _coords()
        sib = (x, y, 1 - c)
        peers = [_chip_peer(x, y, r) for r in range(1, N_CHIPS)]

        def direct(t, j, slot):
            s = t * (N_CHIPS - 1) + j
            if t < nb:
                return pltpu.make_async_remote_copy(src[t].at[c], dst[t].at[slot, c], send.at[s], recv.at[s],
                                                    device_id=(*peers[j], c), device_id_type=_MESH)
            return pltpu.make_async_remote_copy(src[t], dst[t].at[slot], send.at[s], recv.at[s],
                                                device_id=(*peers[j], c), device_id_type=_MESH)

        def passed(t, j, half):
            s = t * (N_CHIPS - 1) + j
            px, py = peers[j]
            part = dst[t].at[2 * px + py, half]
            return pltpu.make_async_remote_copy(part, part, fsend.at[s], frecv.at[s], device_id=sib, device_id_type=_MESH)

        return direct, passed, peers, 2 * x + y, c

    def start(src, dst, sems):
        direct, _, _, me, _ = copies(src, dst, sems)
        for t in range(n):
            for j in range(N_CHIPS - 1):
                direct(t, j, me).start()

    def finish(src, dst, sems):
        direct, passed, peers, me, c = copies(src, dst, sems)
        fwd = []
        for t in range(nb):
            for j in range(N_CHIPS - 1):
                px, py = peers[j]
                direct(t, j, 2 * px + py).wait_recv()
                fwd.append(passed(t, j, c))
                fwd[-1].start()
        for t in range(nb, n):
            for j in range(N_CHIPS - 1):
                px, py = peers[j]
                direct(t, j, 2 * px + py).wait_recv()
        for t in range(nb):
            for j in range(N_CHIPS - 1):
                passed(t, j, 1 - c).wait_recv()
        for t in range(n):
            for j in range(N_CHIPS - 1):
                direct(t, j, me).wait_send()
        for cp in fwd:
            cp.wait_send()

    return _Exchange(shards, [jax.ShapeDtypeStruct((N_CHIPS,) + s.shape, s.dtype) for s in shards],
                     [pltpu.SemaphoreType.DMA((k,)), pltpu.SemaphoreType.DMA((k,)),
                      pltpu.SemaphoreType.DMA((max(kb, 1),)), pltpu.SemaphoreType.DMA((max(kb, 1),))], start, finish)


def sibling_swap(grads):
    n = len(grads)
    k = n * N_CHIPS

    def copies(src, dst, sems):
        send, recv = sems
        x, y, c = _coords()
        return [pltpu.make_async_remote_copy(src[t].at[o, 1 - c], dst[t].at[o], send.at[t * N_CHIPS + o],
                                             recv.at[t * N_CHIPS + o], device_id=(x, y, 1 - c), device_id_type=_MESH)
                for t in range(n) for o in range(N_CHIPS)]

    def start(src, dst, sems):
        for cp in copies(src, dst, sems):
            cp.start()

    def finish(src, dst, sems):
        cps = copies(src, dst, sems)
        for cp in cps:
            cp.wait_recv()
        for cp in cps:
            cp.wait_send()

    return _Exchange(list(grads), [jax.ShapeDtypeStruct((N_CHIPS,) + g.shape[2:], g.dtype) for g in grads],
                     [pltpu.SemaphoreType.DMA((k,)), pltpu.SemaphoreType.DMA((k,))], start, finish)


def scatter_chips(parts):
    n = len(parts)
    k = n * (N_CHIPS - 1)

    def copy(src, dst, sems, t, r, landing):
        send, recv = sems
        x, y, c = _coords()
        me = 2 * x + y
        px, py = _chip_peer(x, y, r)
        peer = 2 * px + py
        s = t * (N_CHIPS - 1) + r - 1
        return pltpu.make_async_remote_copy(src[t].at[me if landing else peer], dst[t].at[peer if landing else me],
                                            send.at[s], recv.at[s], device_id=(px, py, c), device_id_type=_MESH)

    def start(src, dst, sems):
        for t in range(n):
            for r in range(1, N_CHIPS):
                copy(src, dst, sems, t, r, False).start()

    def finish(src, dst, sems):
        for t in range(n):
            for r in range(1, N_CHIPS):
                copy(src, dst, sems, t, r, True).wait_recv()
        for t in range(n):
            for r in range(1, N_CHIPS):
                copy(src, dst, sems, t, r, False).wait_send()

    return _Exchange(list(parts), [jax.ShapeDtypeStruct(p.shape, p.dtype) for p in parts],
                     [pltpu.SemaphoreType.DMA((k,)), pltpu.SemaphoreType.DMA((k,))], start, finish)


def sibling_send(halves, name):
    n = len(halves)

    def body(*refs):
        src, dst = refs[:n], refs[n:2 * n]
        send, recv = refs[2 * n:]
        x, y, c = _coords()
        cps = [pltpu.make_async_remote_copy(src[t], dst[t], send.at[t], recv.at[t],
                                            device_id=(x, y, 1 - c), device_id_type=_MESH) for t in range(n)]
        for cp in cps:
            cp.start()
        for cp in cps:
            cp.wait_recv()
        for cp in cps:
            cp.wait_send()

    return pl.pallas_call(
        body, name=name, in_specs=[_ANY] * n, out_specs=[_ANY] * n,
        out_shape=[jax.ShapeDtypeStruct(h.shape, h.dtype) for h in halves],
        scratch_shapes=[pltpu.SemaphoreType.DMA((n,)), pltpu.SemaphoreType.DMA((n,))],
    )(*halves)


def gather_all(block, name):
    def body(src, dst, send, recv, loc):
        x, y, c = _coords()
        me = 4 * x + 2 * y + c
        mine = pltpu.make_async_copy(src, dst.at[me], loc)
        mine.start()
        outs = []
        for r in range(1, N_DEV):
            peer = (x ^ (r >> 2), y ^ ((r >> 1) & 1), c ^ (r & 1))
            outs.append(pltpu.make_async_remote_copy(src, dst.at[me], send.at[r - 1], recv.at[r - 1],
                                                     device_id=peer, device_id_type=_MESH))
        for cp in outs:
            cp.start()
        for r in range(1, N_DEV):
            px, py, pc = x ^ (r >> 2), y ^ ((r >> 1) & 1), c ^ (r & 1)
            pltpu.make_async_remote_copy(src, dst.at[4 * px + 2 * py + pc], send.at[r - 1], recv.at[r - 1],
                                         device_id=(px, py, pc), device_id_type=_MESH).wait_recv()
        for cp in outs:
            cp.wait_send()
        mine.wait()

    return pl.pallas_call(
        body, name=name, in_specs=[_ANY], out_specs=_ANY,
        out_shape=jax.ShapeDtypeStruct((N_DEV,) + block.shape, block.dtype),
        scratch_shapes=[pltpu.SemaphoreType.DMA((N_DEV - 1,)), pltpu.SemaphoreType.DMA((N_DEV - 1,)),
                        pltpu.SemaphoreType.DMA(())],
    )(block)


def _pad_lanes(v, n=LANE):
    return jnp.pad(v, ((0, 0), (0, n - v.shape[1])))


def _w_in_pieces():
    cs = (PROJ_BIG + 2 * HEADS) // N_CHIPS
    ab_end = AB_COL + 2 * HEADS
    out = []
    for o in range(N_CHIPS):
        lo, hi = o * cs, (o + 1) * cs
        cand = [("big", lo, min(hi, AB_COL), 0), ("ab", max(lo, AB_COL), min(hi, ab_end), AB_COL),
                ("big", max(lo, ab_end), hi, 2 * HEADS)]
        out.append([(s, a - off, b - off) for s, a, b, off in cand if a < b])
    return out


def _split_w_in(w4):
    big, ab = [], []
    for o, pieces in enumerate(_w_in_pieces()):
        at = 0
        for s, a, b in pieces:
            (big if s == "big" else ab).append(w4[o][:, at:at + b - a])
            at += b - a
    return jnp.concatenate(big, axis=1), _pad_lanes(jnp.concatenate(ab, axis=1))


def _join_w_in(big, ab):
    src = {"big": big, "ab": ab}
    return jnp.stack([jnp.concatenate([src[s][:, a:b] for s, a, b in pieces], axis=1) for pieces in _w_in_pieces()])


def _conv_w8(w):
    return jnp.pad(w, ((0, SUB - DN_CONV), (0, 0)))


def _row_layout(gc, tp):
    nc = tp // CHUNK
    g = gc[:, :, 0].reshape(HEADS, nc, 1, CHUNK)
    g = jnp.broadcast_to(g, (HEADS, nc, SUB, CHUNK))
    return jnp.pad(g, ((0, 0), (0, 0), (0, 0), (0, LANE - CHUNK))).reshape(HEADS, nc * SUB, LANE)


def _step(x, meta, W, target, late_weights=None, early_swap=None, early_grads=None, last_grads=None):
    W = dict(W)
    seq = x.shape[0]
    tp = P0 + N_META + seq
    h0 = jnp.concatenate([jnp.zeros((P0, D_MODEL), f32), meta, x], axis=0)
    w_big, w_ab = _split_w_in(W["w_in"])
    cq8, ck8, cv8 = _conv_w8(W["conv_q"]), _conv_w8(W["conv_k"]), _conv_w8(W["conv_v"])
    al, dtb = _pad_lanes(W["dn_a_log"]), _pad_lanes(W["dn_dt_bias"])

    n1 = rms_fwd(h0, W["norm_mix_gain"], "rms1_fwd")
    proj = matmul(n1, w_big, "nn", "proj_fwd")
    pab = matmul(n1, w_ab, "nn", "pab_fwd")
    qn = conv_fwd(proj, cq8, C_DQ * 8, 8, True, "conv_q_fwd")
    kn = conv_fwd(proj, ck8, C_DK * 8, 8, True, "conv_k_fwd")
    va = conv_fwd(proj, cv8, C_DV * 8, 16, False, "conv_v_fwd")
    gc, bc = ab_fwd(pab, al, dtb)
    grow = _row_layout(gc, tp)
    o_dn, states = gdn_fwd(qn, kn, va, gc, bc, grow)
    on = dn_out_fwd(o_dn, proj, W["dn_out_norm_gain"])
    qs, ks, vs = sb_prep_fwd(proj, W["sb_q_norm_gain"], W["sb_k_norm_gain"])
    o_sb, o_sb16, arrived = sb_fwd(qs, ks, vs, rider=late_weights[0] if late_weights else None)
    if late_weights:
        W.update(late_weights[1](arrived))
    ydn = matmul(on, W["w_branch_dn"], "nn", "ydn_fwd")
    ysb = matmul(o_sb16, W["w_branch_sb"], "nn", "ysb_fwd")
    merged = merge_fwd(proj, ydn, ysb)
    h1 = matmul(merged, W["w_out"], "nn", "wout_fwd", residual=h0)
    n2 = rms_fwd(h1, W["norm_ffn_gain"], "rms2_fwd")
    u = matmul(n2, W["w_ffn_in"], "nn", "ffn_in_fwd", tn_t=512)
    act = swiglu_fwd(u)
    y = matmul(act, W["w_ffn_out"], "nn", "ffn_out_fwd", residual=h1)
    dy, dy16, loss = loss_head(y, target)

    G = {}
    dact = matmul(dy16, W["w_ffn_out"], "nt", "ffn_out_dx", tn_t=1408)
    G["w_ffn_out"] = matmul(act, dy16, "tn", "ffn_out_dw", tm_t=1408)
    dgate, dup = swiglu_bwd(u, dact)
    du = jnp.concatenate([dgate, dup], axis=1)
    dn2 = matmul(du, W["w_ffn_in"], "nt", "ffn_in_dx", tk_t=512)
    G["w_ffn_in"] = matmul(n2, du, "tn", "ffn_in_dw", tn_t=512)
    dh1, dh1_16, G["norm_ffn_gain"] = rms_bwd(h1, W["norm_ffn_gain"], dn2, dy, "rms2_bwd")
    dmerged = matmul(dh1_16, W["w_out"], "nt", "wout_dx")
    G["w_out"] = matmul(merged, dh1_16, "tn", "wout_dw")
    dyd, dys, dgd, dgs = merge_bwd(proj, ydn, ysb, dmerged)
    don = matmul(dyd, W["w_branch_dn"], "nt", "ydn_dx")
    G["w_branch_dn"] = matmul(on, dyd, "tn", "ydn_dw")
    do_sb = matmul(dys, W["w_branch_sb"], "nt", "ysb_dx")
    G["w_branch_sb"] = matmul(o_sb16, dys, "tn", "ysb_dw")
    do_dn, dz, G["dn_out_norm_gain"] = dn_out_bwd(o_dn, proj, W["dn_out_norm_gain"], don)
    (dqn, dkn, dva, dgc, dbc), swapped = gdn_bwd(qn, kn, va, gc, bc, grow, states, do_dn,
                                                 rider=early_swap[0](G) if early_swap else None)
    if early_swap:
        early_swap[1](swapped)
    dpab, dal, ddt = ab_bwd(pab, al, dtb, dgc, dbc)
    G["dn_a_log"], G["dn_dt_bias"] = dal[:, :HEADS], ddt[:, :HEADS]
    dyq, dcq = conv_bwd_act(proj, cq8, dqn, C_DQ * 8, 8, True, "conv_q_bwd")
    dyk, dck = conv_bwd_act(proj, ck8, dkn, C_DK * 8, 8, True, "conv_k_bwd")
    dyv, dcv = conv_bwd_act(proj, cv8, dva, C_DV * 8, 16, False, "conv_v_bwd")
    G["conv_q"], G["conv_k"], G["conv_v"] = dcq[:DN_CONV], dck[:DN_CONV], dcv[:DN_CONV]
    d_dq = conv_bwd_in(dyq, cq8, "conv_q_dx")
    d_dk = conv_bwd_in(dyk, ck8, "conv_k_dx")
    d_dv = conv_bwd_in(dyv, cv8, "conv_v_dx")
    (dqs, dks, dvs), delivered = sb_bwd(qs, ks, vs, o_sb, do_sb, rider=early_grads[0](G) if early_grads else None)
    if early_grads:
        early_grads[1](delivered)
    d_sq, d_sk, d_sv, G["sb_q_norm_gain"], G["sb_k_norm_gain"] = sb_prep_bwd(
        proj, W["sb_q_norm_gain"], W["sb_k_norm_gain"], dqs, dks, dvs)
    dproj = jnp.concatenate([d_dq, d_dk, d_dv, dz, d_sq, d_sk, d_sv, dgd, dgs], axis=1)
    dw_big = matmul(n1, dproj, "tn", "proj_dw")
    dw_ab = matmul(n1, dpab, "tn", "pab_dw")
    G["w_in"] = _join_w_in(dw_big, dw_ab)
    if last_grads:
        dn1, delivered = matmul(dproj, w_big, "nt", "proj_dx", tk_t=1024, rider=last_grads[0](G))
        last_grads[1](delivered)
    else:
        dn1 = matmul(dproj, w_big, "nt", "proj_dx", tk_t=1024)
    dn1 = matmul(dpab, w_ab, "nt", "pab_dx", residual=dn1)
    dh0, _, G["norm_mix_gain"] = rms_bwd(h0, W["norm_mix_gain"], dn1, dh1, "rms1_bwd")
    G["meta_tokens"] = dh0[P0:P0 + N_META]
    return loss, dh0[P0 + N_META:], G


_BIG = ("w_in", "w_branch_dn", "w_branch_sb", "w_out", "w_ffn_in", "w_ffn_out")
_COL_SHARDED = ("w_in", "w_ffn_in", "meta_tokens", "conv_q", "conv_k", "conv_v")
_SMALL_REPL = ("norm_mix_gain", "norm_ffn_gain", "dn_a_log", "dn_dt_bias", "dn_out_norm_gain", "sb_q_norm_gain",
               "sb_k_norm_gain")
_SMALL_SHARD = ("meta_tokens", "conv_q", "conv_k", "conv_v")
_ORDER = ("meta_tokens", "norm_mix_gain", "w_in", "conv_q", "conv_k", "conv_v", "dn_a_log", "dn_dt_bias",
          "dn_out_norm_gain", "sb_q_norm_gain", "sb_k_norm_gain", "w_branch_dn", "w_branch_sb", "w_out",
          "norm_ffn_gain", "w_ffn_in", "w_ffn_out")


def _unshard(g4, name):
    if name in _COL_SHARDED:
        r, cs = g4.shape[1:]
        return jnp.transpose(g4, (1, 0, 2)).reshape(r, N_CHIPS * cs)
    return g4.reshape((-1,) + g4.shape[2:])


def _to_shards(full, name):
    if full.ndim == 3:
        return full
    if name in _COL_SHARDED:
        r, c = full.shape
        return jnp.transpose(full.reshape(r, N_CHIPS, c // N_CHIPS), (1, 0, 2))
    r, c = full.shape
    return full.reshape(N_CHIPS, r // N_CHIPS, c)


def _rows_1024(a):
    r, c = a.shape
    if c >= 1024:
        return a.reshape(r * (c // 1024), 1024)
    return jnp.pad(a, ((0, 0), (0, 1024 - c)))


def kernel(x, meta_tokens, norm_mix_gain, w_in, conv_q, conv_k, conv_v, dn_a_log, dn_dt_bias, dn_out_norm_gain, sb_q_norm_gain, sb_k_norm_gain, w_branch_dn, w_branch_sb, w_out, norm_ffn_gain, w_ffn_in, w_ffn_out, loss_target, m_meta_tokens, m_norm_mix_gain, m_w_in, m_conv_q, m_conv_k, m_conv_v, m_dn_a_log, m_dn_dt_bias, m_dn_out_norm_gain, m_sb_q_norm_gain, m_sb_k_norm_gain, m_w_branch_dn, m_w_branch_sb, m_w_out, m_norm_ffn_gain, m_w_ffn_in, m_w_ffn_out, v_meta_tokens, v_norm_mix_gain, v_w_in, v_conv_q, v_conv_k, v_conv_v, v_dn_a_log, v_dn_dt_bias, v_dn_out_norm_gain, v_sb_q_norm_gain, v_sb_k_norm_gain, v_w_branch_dn, v_w_branch_sb, v_w_out, v_norm_ffn_gain, v_w_ffn_in, v_w_ffn_out):
    Wl = dict(meta_tokens=meta_tokens, norm_mix_gain=norm_mix_gain, w_in=w_in[0], conv_q=conv_q[0], conv_k=conv_k[0],
              conv_v=conv_v[0], dn_a_log=dn_a_log, dn_dt_bias=dn_dt_bias, dn_out_norm_gain=dn_out_norm_gain,
              sb_q_norm_gain=sb_q_norm_gain, sb_k_norm_gain=sb_k_norm_gain, w_branch_dn=w_branch_dn[0],
              w_branch_sb=w_branch_sb[0], w_out=w_out[0], norm_ffn_gain=norm_ffn_gain, w_ffn_in=w_ffn_in[0],
              w_ffn_out=w_ffn_out[0])
    Ml = dict(meta_tokens=m_meta_tokens, norm_mix_gain=m_norm_mix_gain, w_in=m_w_in[0], conv_q=m_conv_q[0],
              conv_k=m_conv_k[0], conv_v=m_conv_v[0], dn_a_log=m_dn_a_log, dn_dt_bias=m_dn_dt_bias,
              dn_out_norm_gain=m_dn_out_norm_gain, sb_q_norm_gain=m_sb_q_norm_gain, sb_k_norm_gain=m_sb_k_norm_gain,
              w_branch_dn=m_w_branch_dn[0], w_branch_sb=m_w_branch_sb[0], w_out=m_w_out[0],
              norm_ffn_gain=m_norm_ffn_gain, w_ffn_in=m_w_ffn_in[0], w_ffn_out=m_w_ffn_out[0])
    Vl = dict(meta_tokens=v_meta_tokens, norm_mix_gain=v_norm_mix_gain, w_in=v_w_in[0], conv_q=v_conv_q[0],
              conv_k=v_conv_k[0], conv_v=v_conv_v[0], dn_a_log=v_dn_a_log, dn_dt_bias=v_dn_dt_bias,
              dn_out_norm_gain=v_dn_out_norm_gain, sb_q_norm_gain=v_sb_q_norm_gain, sb_k_norm_gain=v_sb_k_norm_gain,
              w_branch_dn=v_w_branch_dn[0], w_branch_sb=v_w_branch_sb[0], w_out=v_w_out[0],
              norm_ffn_gain=v_norm_ffn_gain, w_ffn_in=v_w_ffn_in[0], w_ffn_out=v_w_ffn_out[0])
    lead = {n: (1,) if (n in _BIG or n in ("conv_q", "conv_k", "conv_v")) else () for n in _ORDER}

    chip = 2 * lax.axis_index("x") + lax.axis_index("y")
    c = lax.axis_index("c")
    halved = {n: Wl[n].astype(bf16).reshape(2, Wl[n].shape[0] // 2, Wl[n].shape[1]) for n in _BIG}

    def gathered_weights(names, owns, outs):
        res = {}
        for n, own, g4 in zip(names, owns, outs):
            g4 = lax.dynamic_update_slice(g4, own[None], (chip,) + (0,) * own.ndim)
            if n in _BIG:
                g4 = g4.reshape(N_CHIPS, 2 * g4.shape[2], g4.shape[3])
            res[n] = g4 if n == "w_in" else _unshard(g4, n)
        return res

    first = ["w_in"] + list(_SMALL_SHARD)
    first_own = [halved["w_in"]] + [Wl[n] for n in _SMALL_SHARD]
    W = dict(Wl)
    W.update(gathered_weights(first, first_own, run_exchange(gather_chips(first_own[:1], first_own[1:]), "gather_w_in")))
    late = [n for n in _BIG if n != "w_in"]
    late_own = [halved[n] for n in late]
    for n in late:
        del W[n]

    def halves_of(names, G):
        g4 = [_to_shards(G[n], n) for n in names]
        return [g.reshape(N_CHIPS, 2, g.shape[1] // 2, g.shape[2]) for g in g4]

    def pair_added(g42, from_sib, tag, wire):
        mine = [lax.dynamic_index_in_dim(g, c, axis=1, keepdims=False) for g in g42]
        return [add2(a, b, "grad_pair_add_%s%d" % (tag, t), out_dtype=wire)
                for t, (a, b) in enumerate(zip(mine, from_sib))]

    def chip_reduced(parts, slots, tag):
        slots = [lax.dynamic_update_slice(s, lax.dynamic_index_in_dim(p, chip, axis=0, keepdims=True), (chip, 0, 0))
                 for s, p in zip(slots, parts)]
        return [sum_slots(s, "grad_chip_sum_%s%d" % (tag, t)) for t, s in enumerate(slots)]

    early, last = {}, {}

    def early_swap_begin(G):
        early["g42"] = halves_of(late, G)
        return sibling_swap(early["g42"])

    def early_begin(G):
        early["parts"] = pair_added(early["g42"], early["from_sib"], "a", f32)
        return scatter_chips(early["parts"])

    def last_begin(G):
        g42 = halves_of(["w_in"], G)
        last["parts"] = pair_added(g42, run_exchange(sibling_swap(g42), "grad_sibling_swap_b"), "b", bf16)
        return scatter_chips(last["parts"])

    loss, grad_x, G = _step(
        x[0], W["meta_tokens"], W, loss_target[0],
        late_weights=(gather_chips(late_own, []), lambda outs: gathered_weights(late, late_own, outs)),
        early_swap=(early_swap_begin, lambda outs: early.update(from_sib=outs)),
        early_grads=(early_begin, lambda slots: early.update(halves=chip_reduced(early["parts"], slots, "a"))),
        last_grads=(last_begin, lambda slots: last.update(halves=chip_reduced(last["parts"], slots, "b"))))
    halves = last["halves"] + early["halves"]
    theirs = sibling_send(halves, "grad_sibling_send")
    Gs = {}
    for n, h, o in zip(["w_in"] + late, halves, theirs):
        Gs[n] = lax.dynamic_update_slice(jnp.concatenate([o, o], axis=0), h, (c * h.shape[0], 0))

    small_names = list(_SMALL_REPL) + list(_SMALL_SHARD)
    pieces = [_rows_1024(G[n]) for n in small_names] + [_rows_1024(loss)]
    counts = [p.shape[0] for p in pieces]
    pack = jnp.concatenate(pieces, axis=0)
    pad_rows = (-pack.shape[0]) % SUB
    pack = jnp.pad(pack, ((0, pad_rows), (0, 0)))
    total = sum_slots(gather_all(pack, "small_gather"), "small_sum")
    chip = 2 * lax.axis_index("x") + lax.axis_index("y")
    row = 0
    for n, cnt in zip(small_names, counts[:-1]):
        blk = total[row:row + cnt]
        row += cnt
        full_shape = G[n].shape
        if full_shape[1] >= 1024:
            blk = blk.reshape(full_shape)
        else:
            blk = blk[:, :full_shape[1]]
        if n in _SMALL_SHARD:
            cs = full_shape[1] // N_CHIPS
            blk = lax.dynamic_slice_in_dim(blk, chip * cs, cs, axis=1)
        Gs[n] = blk
    loss_out = total[row, 0]

    grads, deltas, new_m, new_v = [], [], [], []
    for n in _ORDER:
        d, m2, v2 = adamw(Wl[n], Gs[n], Ml[n], Vl[n], "adamw_" + n)
        shape = lead[n] + Wl[n].shape
        grads.append(Gs[n].reshape(shape))
        deltas.append(d.reshape(shape))
        new_m.append(m2.reshape(shape))
        new_v.append(v2.reshape(shape))
    return (loss_out, grad_x[None], *grads, *deltas, *new_m, *new_v)
```

```python
import jax
import jax.numpy as jnp
from jax import lax
from jax.experimental import pallas as pl
from jax.experimental.pallas import tpu as pltpu

f32 = jnp.float32
bf16 = jnp.bfloat16

D_MODEL = 1024
N_META = 16
CHUNK = 64
HEADS = 8
DN_DK = 128
DN_DV = 256
DN_CONV = 4
DN_QK = HEADS * DN_DK
DN_V = HEADS * DN_DV
SB_DH = 128
SB_W = HEADS * SB_DH
SB_BLOCK = 128
SB_QB = 384
SB_GROUP = 4
SB_HEADS_PER_STEP = 2
GDN_HEADS_PER_STEP = 8
CONV_W = 512
D_FF = 2816
RMS_EPS = 1e-6
L2_EPS = 1e-6
ADAM_LR = 0.001
ADAM_B1 = 0.9
ADAM_B2 = 0.999
ADAM_EPS = 1e-08
ADAM_WD = 0.01
ADAM_STEP = 10

P0 = 112
LANE = 128
SUB = 8
VMEM_LIMIT = 48 * 1024 * 1024
N_CHIPS = 4
N_DEV = 8

C_DQ, C_DK, C_DV, C_DZ, C_SQ, C_SK, C_SV, C_GDN, C_GSB = 0, 1, 2, 4, 6, 7, 8, 9, 10
PROJ_BIG = 11 * 1024
AB_COL = 2 * DN_QK + 2 * DN_V


def _params(n_axes):
    return pltpu.CompilerParams(dimension_semantics=("arbitrary",) * n_axes, vmem_limit_bytes=VMEM_LIMIT)


def _tile(n, target, q=LANE):
    best = None
    for t in range(q, min(n, target) + 1, q):
        if n % t == 0:
            best = t
    return best if best is not None else n


def _dot(a, b):
    return jnp.dot(a.astype(bf16), b.astype(bf16), preferred_element_type=f32)


def _dot_nt(a, b):
    return lax.dot_general(a.astype(bf16), b.astype(bf16), (((1,), (1,)), ((), ())), preferred_element_type=f32)


def _dot_tn(a, b):
    return lax.dot_general(a.astype(bf16), b.astype(bf16), (((0,), (0,)), ((), ())), preferred_element_type=f32)


_HI = lax.Precision.HIGH


def _hdot(a, b):
    return jnp.dot(a, b, precision=_HI, preferred_element_type=f32)


def _hdot_nt(a, b):
    return lax.dot_general(a, b, (((1,), (1,)), ((), ())), precision=_HI, preferred_element_type=f32)


def _hdot_tn(a, b):
    return lax.dot_general(a, b, (((0,), (0,)), ((), ())), precision=_HI, preferred_element_type=f32)


def _sigmoid(x):
    e = jnp.exp(-jnp.abs(x))
    r = 1.0 / (1.0 + e)
    return jnp.where(x >= 0, r, e * r)


def _log1p_small(e):
    return jnp.where(e < 1e-3, e * (1.0 - e * (0.5 - e * (1.0 / 3.0))), jnp.log(1.0 + e))


def _rowsum(x):
    return jnp.sum(x, axis=1, keepdims=True)


def _allsum(x):
    return jnp.sum(jnp.sum(x, axis=1, keepdims=True), axis=0, keepdims=True)


def matmul(a, b, mode, name, residual=None, out_dtype=f32, tm_t=1408, tn_t=1024, tk_t=1408, rider=None):
    if mode == "nn":
        (M, K), (K2, N) = a.shape, b.shape
    elif mode == "nt":
        (M, K), (N, K2) = a.shape, b.shape
    else:
        (K, M), (K2, N) = a.shape, b.shape
    assert K == K2, (a.shape, b.shape, mode)
    tm, tn, tk = _tile(M, tm_t), _tile(N, tn_t), _tile(K, tk_t)
    nk = K // tk
    if mode == "nn":
        a_spec = pl.BlockSpec((tm, tk), lambda i, j, k: (i, k))
        b_spec = pl.BlockSpec((tk, tn), lambda i, j, k: (k, j))
        dims = (((1,), (0,)), ((), ()))
    elif mode == "nt":
        a_spec = pl.BlockSpec((tm, tk), lambda i, j, k: (i, k))
        b_spec = pl.BlockSpec((tn, tk), lambda i, j, k: (j, k))
        dims = (((1,), (1,)), ((), ()))
    else:
        a_spec = pl.BlockSpec((tk, tm), lambda i, j, k: (k, i))
        b_spec = pl.BlockSpec((tk, tn), lambda i, j, k: (k, j))
        dims = (((0,), (0,)), ((), ()))
    o_spec = pl.BlockSpec((tm, tn), lambda i, j, k: (i, j))
    has_res = residual is not None
    grid = (M // tm, N // tn, nk)
    split, ride_first, ride_last = _ride(rider, 3 if has_res else 2, 1, grid)

    def body(*refs):
        ins_, (o_ref,), (rin, rout, rest) = split(refs)
        a_ref, b_ref = ins_[:2]
        r_ref = ins_[2] if has_res else None
        acc_ref, ride = rest[0], (rin, rout, rest[1:])
        ride_first(ride)
        k = pl.program_id(2)

        @pl.when(k == 0)
        def _():
            acc_ref[...] = jnp.zeros_like(acc_ref)

        acc_ref[...] += lax.dot_general(a_ref[...].astype(bf16), b_ref[...].astype(bf16), dims,
                                        preferred_element_type=f32)

        @pl.when(k == nk - 1)
        def _():
            r = acc_ref[...]
            if has_res:
                r = r + r_ref[...]
            o_ref[...] = r.astype(out_dtype)

        ride_last(ride)

    ins = [a, b] + ([residual] if has_res else [])
    specs = [a_spec, b_spec] + ([o_spec] if has_res else [])
    r_ins = rider.ins if rider else []
    r_outs = rider.out_shapes if rider else []
    res = pl.pallas_call(
        body, name=name, grid=grid, in_specs=specs + [_ANY] * len(r_ins), out_specs=[o_spec] + [_ANY] * len(r_outs),
        out_shape=[jax.ShapeDtypeStruct((M, N), out_dtype)] + list(r_outs),
        scratch_shapes=[pltpu.VMEM((tm, tn), f32)] + (rider.scratch if rider else []), compiler_params=_params(3),
    )(*ins, *r_ins)
    return (res[0], res[1:]) if rider else res[0]


def _row_tile(tp):
    return _tile(tp, 512)


def rms_fwd(h, gain, name):
    tp, d = h.shape
    rt = _row_tile(tp)

    def body(h_ref, g_ref, o_ref):
        x = h_ref[...]
        r = lax.rsqrt(jnp.mean(x * x, axis=-1, keepdims=True) + RMS_EPS)
        o_ref[...] = (x * r * g_ref[...]).astype(bf16)

    return pl.pallas_call(
        body, name=name, grid=(tp // rt,),
        in_specs=[pl.BlockSpec((rt, d), lambda i: (i, 0)), pl.BlockSpec((1, d), lambda i: (0, 0))],
        out_specs=pl.BlockSpec((rt, d), lambda i: (i, 0)),
        out_shape=jax.ShapeDtypeStruct((tp, d), bf16), compiler_params=_params(1),
    )(h, gain)


def rms_bwd(h, gain, dn, dres, name):
    tp, d = h.shape
    rt = _row_tile(tp)

    def body(h_ref, g_ref, dn_ref, dr_ref, dh_ref, dhb_ref, dg_ref):
        i = pl.program_id(0)
        x = h_ref[...]
        r = lax.rsqrt(jnp.mean(x * x, axis=-1, keepdims=True) + RMS_EPS)
        xh = x * r
        dn_ = dn_ref[...]
        dxh = dn_ * g_ref[...]
        dh = r * (dxh - xh * jnp.mean(dxh * xh, axis=-1, keepdims=True)) + dr_ref[...]
        dh_ref[...] = dh
        dhb_ref[...] = dh.astype(bf16)
        part = jnp.sum(dn_ * xh, axis=0, keepdims=True)

        @pl.when(i == 0)
        def _():
            dg_ref[...] = part

        @pl.when(i > 0)
        def _():
            dg_ref[...] += part

    row = pl.BlockSpec((rt, d), lambda i: (i, 0))
    vec = pl.BlockSpec((1, d), lambda i: (0, 0))
    return pl.pallas_call(
        body, name=name, grid=(tp // rt,), in_specs=[row, vec, row, row], out_specs=[row, row, vec],
        out_shape=[jax.ShapeDtypeStruct((tp, d), f32), jax.ShapeDtypeStruct((tp, d), bf16),
                   jax.ShapeDtypeStruct((1, d), f32)],
        compiler_params=_params(1),
    )(h, gain, dn, dres)


def loss_head(y, target):
    tp, d = y.shape
    rt = P0 + N_META
    assert rt == SB_BLOCK and tp % rt == 0 and target.shape == (tp - rt, d)

    def body(y_ref, t_ref, dy_ref, dyb_ref, l_ref):
        i = pl.program_id(0)

        @pl.when(i == 0)
        def _():
            dy_ref[...] = jnp.zeros_like(dy_ref)
            dyb_ref[...] = jnp.zeros_like(dyb_ref)
            l_ref[...] = jnp.zeros_like(l_ref)

        @pl.when(i > 0)
        def _():
            err = y_ref[...] - t_ref[...]
            dy = err * (1.0 / d)
            dy_ref[...] = dy
            dyb_ref[...] = dy.astype(bf16)
            l_ref[...] += jnp.broadcast_to(_allsum(err * err) * (0.5 / d), l_ref.shape)

    row = pl.BlockSpec((rt, d), lambda i: (i, 0))
    return pl.pallas_call(
        body, name="loss_head", grid=(tp // rt,),
        in_specs=[row, pl.BlockSpec((rt, d), lambda i: (jnp.maximum(i - 1, 0), 0))],
        out_specs=[row, row, pl.BlockSpec((1, LANE), lambda i: (0, 0))],
        out_shape=[jax.ShapeDtypeStruct((tp, d), f32), jax.ShapeDtypeStruct((tp, d), bf16),
                   jax.ShapeDtypeStruct((1, LANE), f32)],
        compiler_params=_params(1),
    )(y, target)


FFN_BLK = D_FF // 2


def _ffn_interleave(w):
    r = w.shape[0]
    return jnp.transpose(w.reshape(r, 2, 2, FFN_BLK), (0, 2, 1, 3)).reshape(r, 2 * D_FF)


def swiglu_fwd(u):
    tp = u.shape[0]
    rt, cb = _row_tile(tp), FFN_BLK

    def body(gu_ref, o_ref):
        g = gu_ref[:, :cb]
        o_ref[...] = (g * _sigmoid(g) * gu_ref[:, cb:]).astype(bf16)

    return pl.pallas_call(
        body, name="swiglu_fwd", grid=(tp // rt, 2),
        in_specs=[pl.BlockSpec((rt, 2 * cb), lambda i, j: (i, j))], out_specs=pl.BlockSpec((rt, cb), lambda i, j: (i, j)),
        out_shape=jax.ShapeDtypeStruct((tp, D_FF), bf16), compiler_params=_params(2),
    )(u)


def swiglu_bwd(u, dact):
    tp = u.shape[0]
    rt, cb = _row_tile(tp), FFN_BLK

    def body(gu_ref, da_ref, o_ref):
        g = gu_ref[:, :cb]
        s = _sigmoid(g)
        da = da_ref[...]
        o_ref[:, :cb] = (da * gu_ref[:, cb:] * s * (1.0 + g * (1.0 - s))).astype(bf16)
        o_ref[:, cb:] = (da * g * s).astype(bf16)

    wide = pl.BlockSpec((rt, 2 * cb), lambda i, j: (i, j))
    return pl.pallas_call(
        body, name="swiglu_bwd", grid=(tp // rt, 2), in_specs=[wide, pl.BlockSpec((rt, cb), lambda i, j: (i, j))],
        out_specs=wide, out_shape=jax.ShapeDtypeStruct((tp, 2 * D_FF), bf16), compiler_params=_params(2),
    )(u, dact)


def merge_fwd(proj, ydn, ysb):
    tp = proj.shape[0]
    rt, d = _row_tile(tp), D_MODEL

    def body(gd_ref, gs_ref, yd_ref, ys_ref, o_ref):
        o_ref[...] = (_sigmoid(gd_ref[...]) * yd_ref[...] + _sigmoid(gs_ref[...]) * ys_ref[...]).astype(bf16)

    row = pl.BlockSpec((rt, d), lambda i: (i, 0))
    return pl.pallas_call(
        body, name="merge_fwd", grid=(tp // rt,),
        in_specs=[pl.BlockSpec((rt, d), lambda i: (i, C_GDN)), pl.BlockSpec((rt, d), lambda i: (i, C_GSB)), row, row],
        out_specs=row, out_shape=jax.ShapeDtypeStruct((tp, d), bf16), compiler_params=_params(1),
    )(proj, proj, ydn, ysb)


def merge_bwd(proj, ydn, ysb, dm):
    tp = proj.shape[0]
    rt, d = _row_tile(tp), D_MODEL

    def body(gd_ref, gs_ref, yd_ref, ys_ref, dm_ref, dyd_ref, dys_ref, dg_ref):
        dm_ = dm_ref[...]
        sd = _sigmoid(gd_ref[...])
        ss = _sigmoid(gs_ref[...])
        dyd_ref[...] = (dm_ * sd).astype(bf16)
        dys_ref[...] = (dm_ * ss).astype(bf16)
        dg_ref[:, :d] = (dm_ * yd_ref[...] * sd * (1.0 - sd)).astype(bf16)
        dg_ref[:, d:] = (dm_ * ys_ref[...] * ss * (1.0 - ss)).astype(bf16)

    row = pl.BlockSpec((rt, d), lambda i: (i, 0))
    return pl.pallas_call(
        body, name="merge_bwd", grid=(tp // rt,),
        in_specs=[pl.BlockSpec((rt, d), lambda i: (i, C_GDN)), pl.BlockSpec((rt, d), lambda i: (i, C_GSB)), row, row, row],
        out_specs=[row, row, pl.BlockSpec((rt, 2 * d), lambda i: (i, 0))],
        out_shape=[jax.ShapeDtypeStruct((tp, d), bf16)] * 2 + [jax.ShapeDtypeStruct((tp, 2 * d), bf16)],
        compiler_params=_params(1),
    )(proj, proj, ydn, ysb, dm)


def dn_out_fwd(o, proj, gain):
    tp = o.shape[0]
    rt, cb, wide = _row_tile(tp), DN_DV, 1024
    zb = C_DZ * 1024 // wide

    def body(o_ref, z_ref, g_ref, y_ref):
        for s in range(wide // cb):
            sl = slice(s * cb, (s + 1) * cb)
            x = o_ref[:, sl]
            r = lax.rsqrt(jnp.mean(x * x, axis=-1, keepdims=True) + RMS_EPS)
            z = z_ref[:, sl]
            y_ref[:, sl] = (x * r * g_ref[...] * (z * _sigmoid(z))).astype(bf16)

    blk = pl.BlockSpec((rt, wide), lambda i, j: (i, j))
    return pl.pallas_call(
        body, name="dn_out_fwd", grid=(tp // rt, DN_V // wide),
        in_specs=[blk, pl.BlockSpec((rt, wide), lambda i, j: (i, j + zb)), pl.BlockSpec((1, cb), lambda i, j: (0, 0))],
        out_specs=blk, out_shape=jax.ShapeDtypeStruct((tp, DN_V), bf16), compiler_params=_params(2),
    )(o, proj, gain)


def dn_out_bwd(o, proj, gain, dy):
    tp = o.shape[0]
    rt, cb, wide = _row_tile(tp), DN_DV, 1024
    zb = C_DZ * 1024 // wide

    def body(o_ref, z_ref, g_ref, dy_ref, do_ref, dz_ref, dg_ref):
        i, j = pl.program_id(0), pl.program_id(1)
        g = g_ref[...]
        part = jnp.zeros((1, cb), f32)
        for hh in range(wide // cb):
            sl = slice(hh * cb, (hh + 1) * cb)
            x = o_ref[:, sl]
            r = lax.rsqrt(jnp.mean(x * x, axis=-1, keepdims=True) + RMS_EPS)
            xh = x * r
            z = z_ref[:, sl]
            s = _sigmoid(z)
            dy_ = dy_ref[:, sl]
            drn = dy_ * (z * s)
            dz_ref[:, sl] = (dy_ * xh * g * s * (1.0 + z * (1.0 - s))).astype(bf16)
            dxh = drn * g
            do_ref[:, sl] = r * (dxh - xh * jnp.mean(dxh * xh, axis=-1, keepdims=True))
            part = part + jnp.sum(drn * xh, axis=0, keepdims=True)
        first = jnp.logical_and(i == 0, j == 0)

        @pl.when(first)
        def _():
            dg_ref[...] = part

        @pl.when(jnp.logical_not(first))
        def _():
            dg_ref[...] += part

    blk = pl.BlockSpec((rt, wide), lambda i, j: (i, j))
    vec = pl.BlockSpec((1, cb), lambda i, j: (0, 0))
    return pl.pallas_call(
        body, name="dn_out_bwd", grid=(tp // rt, DN_V // wide),
        in_specs=[blk, pl.BlockSpec((rt, wide), lambda i, j: (i, j + zb)), vec, blk],
        out_specs=[blk, blk, vec],
        out_shape=[jax.ShapeDtypeStruct((tp, DN_V), f32), jax.ShapeDtypeStruct((tp, DN_V), bf16),
                   jax.ShapeDtypeStruct((1, cb), f32)],
        compiler_params=_params(2),
    )(o, proj, gain, dy)


def sb_prep_fwd(proj, gq, gk):
    tp = proj.shape[0]
    rt, cb = _row_tile(tp), SB_DH

    def body(q_ref, k_ref, v_ref, gq_ref, gk_ref, qo_ref, ko_ref, vo_ref):
        for x_ref, g_ref, o_ref in ((q_ref, gq_ref, qo_ref), (k_ref, gk_ref, ko_ref)):
            for h in range(HEADS):
                sl = slice(h * cb, (h + 1) * cb)
                x = x_ref[:, sl]
                r = lax.rsqrt(jnp.mean(x * x, axis=-1, keepdims=True) + RMS_EPS)
                o_ref[:, sl] = (x * r * g_ref[...]).astype(bf16)
        vo_ref[...] = v_ref[...].astype(bf16)

    blk = pl.BlockSpec((rt, SB_W), lambda i: (i, 0))
    vec = pl.BlockSpec((1, cb), lambda i: (0, 0))
    return pl.pallas_call(
        body, name="sb_prep_fwd", grid=(tp // rt,),
        in_specs=[pl.BlockSpec((rt, SB_W), lambda i: (i, C_SQ)), pl.BlockSpec((rt, SB_W), lambda i: (i, C_SK)),
                  pl.BlockSpec((rt, SB_W), lambda i: (i, C_SV)), vec, vec],
        out_specs=[blk] * 3, out_shape=[jax.ShapeDtypeStruct((tp, SB_W), bf16)] * 3, compiler_params=_params(1),
    )(proj, proj, proj, gq, gk)


def sb_prep_bwd(proj, gq, gk, dqs, dks, dvs):
    tp = proj.shape[0]
    rt, cb = _row_tile(tp), SB_DH

    def body(q_ref, k_ref, gq_ref, gk_ref, dq_ref, dk_ref, dv_ref, do_ref, dgq_ref, dgk_ref):
        first = pl.program_id(0) == 0
        do_ref[:, 2 * SB_W:] = dv_ref[...].astype(bf16)
        for x_ref, g_ref, dn_ref, at, dg_ref in ((q_ref, gq_ref, dq_ref, 0, dgq_ref),
                                                 (k_ref, gk_ref, dk_ref, SB_W, dgk_ref)):
            part = jnp.zeros((1, cb), f32)
            for h in range(HEADS):
                sl = slice(h * cb, (h + 1) * cb)
                x = x_ref[:, sl]
                r = lax.rsqrt(jnp.mean(x * x, axis=-1, keepdims=True) + RMS_EPS)
                xh = x * r
                dn_ = dn_ref[:, sl]
                dxh = dn_ * g_ref[...]
                do_ref[:, at + h * cb:at + (h + 1) * cb] = (
                    r * (dxh - xh * jnp.mean(dxh * xh, axis=-1, keepdims=True))).astype(bf16)
                part = part + jnp.sum(dn_ * xh, axis=0, keepdims=True)

            @pl.when(first)
            def _(dg_ref=dg_ref, part=part):
                dg_ref[...] = part

            @pl.when(jnp.logical_not(first))
            def _(dg_ref=dg_ref, part=part):
                dg_ref[...] += part

    blk = pl.BlockSpec((rt, SB_W), lambda i: (i, 0))
    vec = pl.BlockSpec((1, cb), lambda i: (0, 0))
    return pl.pallas_call(
        body, name="sb_prep_bwd", grid=(tp // rt,),
        in_specs=[pl.BlockSpec((rt, SB_W), lambda i: (i, C_SQ)), pl.BlockSpec((rt, SB_W), lambda i: (i, C_SK)),
                  vec, vec, blk, blk, blk],
        out_specs=[pl.BlockSpec((rt, 3 * SB_W), lambda i: (i, 0)), vec, vec],
        out_shape=[jax.ShapeDtypeStruct((tp, 3 * SB_W), bf16)] + [jax.ShapeDtypeStruct((1, cb), f32)] * 2,
        compiler_params=_params(1),
    )(proj, proj, gq, gk, dqs, dks, dvs)


def _conv_taps(ext, rt):
    taps = []
    for k in range(DN_CONV):
        s = DN_CONV - 1 - k
        taps.append((pltpu.roll(ext, s, axis=0) if s else ext)[SUB:SUB + rt])
    return taps


def _conv_act(taps, w, l2):
    y = taps[0] * w[0:1]
    for k in range(1, DN_CONV):
        y = y + taps[k] * w[k:k + 1]
    s = _sigmoid(y)
    a = y * s
    if l2:
        n = lax.rsqrt(jnp.sum(a * a, axis=-1, keepdims=True) + L2_EPS)
        return y, s, a, n
    return y, s, a, None


def conv_fwd(proj, w8, col_blk, ncb, l2, name):
    tp = proj.shape[0]
    rt = _row_tile(tp)
    hb = rt // SUB
    cw = CONV_W
    cb0 = col_blk * LANE // cw

    def body(x_ref, h_ref, w_ref, o_ref):
        i = pl.program_id(1)
        first = (i > 0).astype(f32)
        for s in range(cw // LANE):
            sl = slice(s * LANE, (s + 1) * LANE)
            ext = jnp.concatenate([h_ref[:, sl] * first, x_ref[:, sl]], axis=0)
            _, _, a, n = _conv_act(_conv_taps(ext, rt), w_ref[:, sl], l2)
            o_ref[:, sl] = a * n if l2 else a

    return pl.pallas_call(
        body, name=name, grid=(ncb * LANE // cw, tp // rt),
        in_specs=[pl.BlockSpec((rt, cw), lambda j, i: (i, j + cb0)),
                  pl.BlockSpec((SUB, cw), lambda j, i: (jnp.maximum(i * hb - 1, 0), j + cb0)),
                  pl.BlockSpec((SUB, cw), lambda j, i: (0, j))],
        out_specs=pl.BlockSpec((rt, cw), lambda j, i: (i, j)),
        out_shape=jax.ShapeDtypeStruct((tp, ncb * LANE), f32), compiler_params=_params(2),
    )(proj, proj, w8)


def conv_bwd_act(proj, w8, dout, col_blk, ncb, l2, name):
    tp = proj.shape[0]
    rt = _row_tile(tp)
    hb = rt // SUB
    cw = CONV_W
    cb0 = col_blk * LANE // cw

    def body(x_ref, h_ref, w_ref, d_ref, dy_ref, dw_ref):
        i = pl.program_id(1)
        first = (i > 0).astype(f32)
        rows = lax.broadcasted_iota(jnp.int32, (SUB, LANE), 0)
        for s in range(cw // LANE):
            sl = slice(s * LANE, (s + 1) * LANE)
            ext = jnp.concatenate([h_ref[:, sl] * first, x_ref[:, sl]], axis=0)
            taps = _conv_taps(ext, rt)
            y, sg, a, n = _conv_act(taps, w_ref[:, sl], l2)
            da = d_ref[:, sl]
            if l2:
                out = a * n
                da = n * (da - out * jnp.sum(da * out, axis=-1, keepdims=True))
            dy = da * sg * (1.0 + y * (1.0 - sg))
            dy_ref[:, sl] = dy
            part = jnp.zeros((SUB, LANE), f32)
            for k in range(DN_CONV):
                part = part + jnp.where(rows == k, jnp.sum(taps[k] * dy, axis=0, keepdims=True), 0.0)

            @pl.when(i == 0)
            def _(sl=sl, part=part):
                dw_ref[:, sl] = part

            @pl.when(i > 0)
            def _(sl=sl, part=part):
                dw_ref[:, sl] += part

    return pl.pallas_call(
        body, name=name, grid=(ncb * LANE // cw, tp // rt),
        in_specs=[pl.BlockSpec((rt, cw), lambda j, i: (i, j + cb0)),
                  pl.BlockSpec((SUB, cw), lambda j, i: (jnp.maximum(i * hb - 1, 0), j + cb0)),
                  pl.BlockSpec((SUB, cw), lambda j, i: (0, j)),
                  pl.BlockSpec((rt, cw), lambda j, i: (i, j))],
        out_specs=[pl.BlockSpec((rt, cw), lambda j, i: (i, j)), pl.BlockSpec((SUB, cw), lambda j, i: (0, j))],
        out_shape=[jax.ShapeDtypeStruct((tp, ncb * LANE), f32), jax.ShapeDtypeStruct((SUB, ncb * LANE), f32)],
        compiler_params=_params(2),
    )(proj, proj, w8, dout)


def conv_bwd_in(dy, w8, name):
    tp, cols = dy.shape
    rt = _row_tile(tp)
    hb = rt // SUB
    nr = tp // rt
    last8 = tp // SUB - 1
    cw = CONV_W

    def body(d_ref, h_ref, w_ref, o_ref):
        i = pl.program_id(1)
        last = (i < nr - 1).astype(f32)
        for c0 in range(cw // LANE):
            sl = slice(c0 * LANE, (c0 + 1) * LANE)
            ext = jnp.concatenate([d_ref[:, sl], h_ref[:, sl] * last], axis=0)
            w = w_ref[:, sl]
            acc = None
            for k in range(DN_CONV):
                s = DN_CONV - 1 - k
                sh = (pltpu.roll(ext, rt + SUB - s, axis=0) if s else ext)[0:rt]
                term = sh * w[k:k + 1]
                acc = term if acc is None else acc + term
            o_ref[:, sl] = acc.astype(bf16)

    return pl.pallas_call(
        body, name=name, grid=(cols // cw, nr),
        in_specs=[pl.BlockSpec((rt, cw), lambda j, i: (i, j)),
                  pl.BlockSpec((SUB, cw), lambda j, i: (jnp.minimum((i + 1) * hb, last8), j)),
                  pl.BlockSpec((SUB, cw), lambda j, i: (0, j))],
        out_specs=pl.BlockSpec((rt, cw), lambda j, i: (i, j)),
        out_shape=jax.ShapeDtypeStruct((tp, cols), bf16), compiler_params=_params(2),
    )(dy, dy, w8)


def _ab_common(p, al, dtb, r0):
    rows = r0 + lax.broadcasted_iota(jnp.int32, p.shape, 0)
    mask = (rows >= P0).astype(f32)
    xx = p + dtb
    sp = jnp.maximum(xx, 0.0) + _log1p_small(jnp.exp(-jnp.abs(xx)))
    ea = jnp.exp(al)
    g = -ea * sp * mask
    beta = _sigmoid(p) * mask
    return g, beta, _sigmoid(xx), ea, mask


def ab_fwd(pab, al, dtb):
    tp = pab.shape[0]
    rt = _row_tile(tp)

    def body(p_ref, al_ref, dt_ref, g_ref, b_ref):
        i = pl.program_id(0)
        g, beta, _, _, _ = _ab_common(p_ref[...], al_ref[...], dt_ref[...], i * rt)
        for h in range(HEADS):
            g_ref[h] = jnp.broadcast_to(g[:, h:h + 1], (rt, LANE))
            b_ref[h] = jnp.broadcast_to(beta[:, HEADS + h:HEADS + h + 1], (rt, LANE))

    vec = pl.BlockSpec((1, LANE), lambda i: (0, 0))
    out = pl.BlockSpec((HEADS, rt, LANE), lambda i: (0, i, 0))
    return pl.pallas_call(
        body, name="ab_fwd", grid=(tp // rt,), in_specs=[pl.BlockSpec((rt, LANE), lambda i: (i, 0)), vec, vec],
        out_specs=[out, out], out_shape=[jax.ShapeDtypeStruct((HEADS, tp, LANE), f32)] * 2, compiler_params=_params(1),
    )(pab, al, dtb)


def ab_bwd(pab, al, dtb, dg, db):
    tp = pab.shape[0]
    rt = _row_tile(tp)

    def body(p_ref, al_ref, dt_ref, dg_ref, db_ref, dp_ref, dal_ref, ddt_ref):
        i = pl.program_id(0)
        g, beta, sx, ea, mask = _ab_common(p_ref[...], al_ref[...], dt_ref[...], i * rt)
        lanes = lax.broadcasted_iota(jnp.int32, (rt, LANE), 1)
        dgl = jnp.zeros((rt, LANE), f32)
        dbl = jnp.zeros((rt, LANE), f32)
        for h in range(HEADS):
            dgl = dgl + jnp.where(lanes == h, dg_ref[h], 0.0)
            dbl = dbl + jnp.where(lanes == HEADS + h, db_ref[h], 0.0)
        dxx = dgl * (-ea) * sx * mask
        dp_ref[...] = (dxx + dbl * beta * (1.0 - beta)).astype(bf16)
        pal = jnp.sum(dgl * g, axis=0, keepdims=True)
        pdt = jnp.sum(dxx, axis=0, keepdims=True)

        @pl.when(i == 0)
        def _():
            dal_ref[...] = pal
            ddt_ref[...] = pdt

        @pl.when(i > 0)
        def _():
            dal_ref[...] += pal
            ddt_ref[...] += pdt

    vec = pl.BlockSpec((1, LANE), lambda i: (0, 0))
    row = pl.BlockSpec((rt, LANE), lambda i: (i, 0))
    big = pl.BlockSpec((HEADS, rt, LANE), lambda i: (0, i, 0))
    return pl.pallas_call(
        body, name="ab_bwd", grid=(tp // rt,), in_specs=[row, vec, vec, big, big], out_specs=[row, vec, vec],
        out_shape=[jax.ShapeDtypeStruct((tp, LANE), bf16), jax.ShapeDtypeStruct((1, LANE), f32),
                   jax.ShapeDtypeStruct((1, LANE), f32)],
        compiler_params=_params(1),
    )(pab, al, dtb, dg, db)


class _Chunk:
    pass


def _gdn_chunk(q, k, v, gcol, bcol, grow8):
    C = CHUNK
    R = range(len(q))
    X = _Chunk()
    ri = lax.broadcasted_iota(jnp.int32, (C, C), 0)
    ci = lax.broadcasted_iota(jnp.int32, (C, C), 1)
    r2 = lax.broadcasted_iota(jnp.int32, (LANE, LANE), 0)
    c2 = lax.broadcasted_iota(jnp.int32, (LANE, LANE), 1)
    lower = (ri >= ci).astype(f32)
    upper2 = (r2 <= c2).astype(f32)
    eye = (ri == ci).astype(f32)
    gam = [_hdot(lower, gcol[h]) for h in R]
    gam_row = [_hdot(grow8[h], upper2)[0:1, 0:C] for h in R]
    X.ri, X.ci = ri, ci
    X.Dm = [jnp.where(ri >= ci, jnp.exp(jnp.minimum(gam[h][:, 0:C] - gam_row[h], 0.0)), 0.0) for h in R]
    X.eg = [jnp.exp(gam[h]) for h in R]
    gl = [gam[h][C - 1:C, :] for h in R]
    X.egl = [jnp.exp(gl[h]) for h in R]
    X.kdec = [jnp.exp(gl[h] - gam[h]) for h in R]
    X.qs = [q[h] * (DN_DK ** -0.5) for h in R]
    X.kb = [k[h] * bcol[h] for h in R]
    kk = [_dot_nt(X.kb[h], k[h]) for h in R]
    qk = [_dot_nt(X.qs[h], k[h]) for h in R]
    X.A = [jnp.where(ri > ci, kk[h] * X.Dm[h], 0.0) for h in R]
    T = [eye - X.A[h] for h in R]
    P = list(X.A)
    for _ in range(5):
        P = [_hdot(P[h], P[h]) for h in R]
        T = [T[h] + _hdot(T[h], P[h]) for h in R]
    X.T = T
    X.b2 = [jnp.concatenate([bcol[h], bcol[h]], axis=-1) for h in R]
    X.u = [_hdot(T[h], v[h] * X.b2[h]) for h in R]
    X.w = [_hdot(T[h], X.kb[h] * X.eg[h]) for h in R]
    X.attn = [qk[h] * X.Dm[h] for h in R]
    X.qg = [X.qs[h] * X.eg[h] for h in R]
    X.kg = [k[h] * X.kdec[h] for h in R]
    return X


def gdn_fwd(q, k, v, gc, bc, grow):
    tp = q.shape[0]
    nc = tp // CHUNK
    hb = GDN_HEADS_PER_STEP

    def body(q_ref, k_ref, v_ref, gc_ref, bc_ref, gr_ref, o_ref, ss_ref, S_ref):
        c = pl.program_id(1)

        @pl.when(c == 0)
        def _():
            S_ref[...] = jnp.zeros_like(S_ref)

        R = range(hb)
        qc = [slice(h * DN_DK, (h + 1) * DN_DK) for h in R]
        vc = [slice(h * DN_DV, (h + 1) * DN_DV) for h in R]
        X = _gdn_chunk([q_ref[:, qc[h]] for h in R], [k_ref[:, qc[h]] for h in R], [v_ref[:, vc[h]] for h in R],
                       [gc_ref[h] for h in R], [bc_ref[h] for h in R], [gr_ref[h] for h in R])
        S = [S_ref[h] for h in R]
        for h in R:
            ss_ref[h, 0] = S[h]
        wS = [_dot(X.w[h], S[h]) for h in R]
        qS = [_dot(X.qg[h], S[h]) for h in R]
        vn = [X.u[h] - wS[h] for h in R]
        av = [_dot(X.attn[h], vn[h]) for h in R]
        kv = [_dot_tn(X.kg[h], vn[h]) for h in R]
        for h in R:
            o_ref[:, vc[h]] = qS[h] + av[h]
            S_ref[h] = S[h] * X.egl[h][:, 0:1] + kv[h]

    qk = pl.BlockSpec((CHUNK, hb * DN_DK), lambda g, c: (c, g))
    vv = pl.BlockSpec((CHUNK, hb * DN_DV), lambda g, c: (c, g))
    col = pl.BlockSpec((hb, CHUNK, LANE), lambda g, c: (g, c, 0))
    row = pl.BlockSpec((hb, SUB, LANE), lambda g, c: (g, c, 0))
    return pl.pallas_call(
        body, name="gdn_fwd", grid=(HEADS // hb, nc), in_specs=[qk, qk, vv, col, col, row],
        out_specs=[vv, pl.BlockSpec((hb, 1, DN_DK, DN_DV), lambda g, c: (g, c, 0, 0))],
        out_shape=[jax.ShapeDtypeStruct((tp, DN_V), f32), jax.ShapeDtypeStruct((HEADS, nc, DN_DK, DN_DV), f32)],
        scratch_shapes=[pltpu.VMEM((hb, DN_DK, DN_DV), f32)], compiler_params=_params(2),
    )(q, k, v, gc, bc, grow)


def gdn_bwd(q, k, v, gc, bc, grow, states, do, rider=None):
    tp = q.shape[0]
    nc = tp // CHUNK
    C = CHUNK
    hb = GDN_HEADS_PER_STEP
    grid = (HEADS // hb, nc)
    split, ride_first, ride_last = _ride(rider, 8, 5, grid)

    def body(*refs):
        ((q_ref, k_ref, v_ref, gc_ref, bc_ref, gr_ref, ss_ref, do_ref), (dq_ref, dk_ref, dv_ref, dg_ref, db_ref),
         (rin, rout, rest)) = split(refs)
        dS_ref, ride = rest[0], (rin, rout, rest[1:])
        ride_first(ride)
        c = pl.program_id(1)

        @pl.when(c == 0)
        def _():
            dS_ref[...] = jnp.zeros_like(dS_ref)

        R = range(hb)
        qc = [slice(h * DN_DK, (h + 1) * DN_DK) for h in R]
        vc = [slice(h * DN_DV, (h + 1) * DN_DV) for h in R]
        k_ = [k_ref[:, qc[h]] for h in R]
        v_ = [v_ref[:, vc[h]] for h in R]
        bcol = [bc_ref[h] for h in R]
        X = _gdn_chunk([q_ref[:, qc[h]] for h in R], k_, v_, [gc_ref[h] for h in R], bcol, [gr_ref[h] for h in R])
        ri, ci = X.ri, X.ci
        S = [ss_ref[h, 0] for h in R]
        do_ = [do_ref[:, vc[h]] for h in R]
        dSn = [dS_ref[h] for h in R]
        wS = [_dot(X.w[h], S[h]) for h in R]
        ado = [_dot_tn(X.attn[h], do_[h]) for h in R]
        kdS = [_dot(X.kg[h], dSn[h]) for h in R]
        d_qg = [_dot_nt(do_[h], S[h]) for h in R]
        qdo = [_dot_tn(X.qg[h], do_[h]) for h in R]
        vn = [X.u[h] - wS[h] for h in R]
        d_vn = [ado[h] + kdS[h] for h in R]
        dovn = [_dot_nt(do_[h], vn[h]) for h in R]
        d_kg = [_dot_nt(vn[h], dSn[h]) for h in R]
        wdv = [_dot_tn(X.w[h], d_vn[h]) for h in R]
        dw = [-_dot_nt(d_vn[h], S[h]) for h in R]
        for h in R:
            dS_ref[h] = qdo[h] + X.egl[h][:, 0:1] * dSn[h] - wdv[h]
        dattn = [jnp.where(ri >= ci, dovn[h], 0.0) for h in R]
        dRu = [_hdot_tn(X.T[h], d_vn[h]) for h in R]
        dRw = [_hdot_tn(X.T[h], dw[h]) for h in R]
        dAu = [_hdot_nt(dRu[h], X.u[h]) for h in R]
        dAw = [_hdot_nt(dRw[h], X.w[h]) for h in R]
        dA = [jnp.where(ri > ci, -(dAu[h] + dAw[h]), 0.0) for h in R]
        dKK = [dA[h] * X.Dm[h] for h in R]
        dQK = [dattn[h] * X.Dm[h] for h in R]
        E = [dA[h] * X.A[h] + dattn[h] * X.attn[h] for h in R]
        dkb = [_dot(dKK[h], k_[h]) + dRw[h] * X.eg[h] for h in R]
        dk1 = [_dot_tn(dKK[h], X.kb[h]) for h in R]
        dqs = [_dot(dQK[h], k_[h]) + d_qg[h] * X.eg[h] for h in R]
        dk2 = [_dot_tn(dQK[h], X.qs[h]) for h in R]
        ones = jnp.ones((C, LANE), f32)
        colE = [_hdot_tn(E[h], ones) for h in R]
        rows = lax.broadcasted_iota(jnp.int32, (C, LANE), 0)
        upper = (ci >= ri).astype(f32)
        dgam = []
        for h in R:
            t = d_kg[h] * X.kg[h]
            dgl = _allsum(t) + X.egl[h][:, 0:1] * _allsum(S[h] * dSn[h])
            g = (_rowsum(E[h]) - colE[h] + _rowsum(dRw[h] * (X.kb[h] * X.eg[h])) + _rowsum(d_qg[h] * X.qg[h])
                 - _rowsum(t))
            dgam.append(g + jnp.where(rows == C - 1, dgl, 0.0))
        dg = [_hdot(upper, dgam[h]) for h in R]
        for h in R:
            dv_ref[:, vc[h]] = dRu[h] * X.b2[h]
            dbeta = _rowsum(dRu[h] * v_[h]) + _rowsum(dkb[h] * k_[h])
            dq_ref[:, qc[h]] = dqs[h] * (DN_DK ** -0.5)
            dk_ref[:, qc[h]] = dk1[h] + dk2[h] + dkb[h] * bcol[h] + d_kg[h] * X.kdec[h]
            dg_ref[h] = dg[h]
            db_ref[h] = jnp.broadcast_to(dbeta, (C, LANE))
        ride_last(ride)

    rc = lambda c: nc - 1 - c
    qk = pl.BlockSpec((CHUNK, hb * DN_DK), lambda g, c: (rc(c), g))
    vv = pl.BlockSpec((CHUNK, hb * DN_DV), lambda g, c: (rc(c), g))
    col = pl.BlockSpec((hb, CHUNK, LANE), lambda g, c: (g, rc(c), 0))
    row = pl.BlockSpec((hb, SUB, LANE), lambda g, c: (g, rc(c), 0))
    st = pl.BlockSpec((hb, 1, DN_DK, DN_DV), lambda g, c: (g, rc(c), 0, 0))
    r_ins = rider.ins if rider else []
    r_outs = rider.out_shapes if rider else []
    res = pl.pallas_call(
        body, name="gdn_bwd", grid=grid, in_specs=[qk, qk, vv, col, col, row, st, vv] + [_ANY] * len(r_ins),
        out_specs=[qk, qk, vv, col, col] + [_ANY] * len(r_outs),
        out_shape=[jax.ShapeDtypeStruct((tp, DN_QK), f32), jax.ShapeDtypeStruct((tp, DN_QK), f32),
                   jax.ShapeDtypeStruct((tp, DN_V), f32), jax.ShapeDtypeStruct((HEADS, tp, LANE), f32),
                   jax.ShapeDtypeStruct((HEADS, tp, LANE), f32)] + list(r_outs),
        scratch_shapes=[pltpu.VMEM((hb, DN_DK, DN_DV), f32)] + (rider.scratch if rider else []),
        compiler_params=_params(2),
    )(q, k, v, gc, bc, grow, states, do, *r_ins)
    return res[:5], res[5:]


def _cumsum_after(x, nb, us):
    B, n = SB_BLOCK, x.shape[0]
    hi = x.astype(bf16)
    lo = (x - hi.astype(f32)).astype(bf16)
    rows = [p[:, b * B:(b + 1) * B] for p in (hi, lo) for b in range(nb)]
    r = jnp.dot(jnp.concatenate(rows, axis=0), us, preferred_element_type=f32)
    out = [r[b * n:(b + 1) * n] + r[(nb + b) * n:(nb + b + 1) * n] for b in range(nb)]
    return out[0] if nb == 1 else jnp.concatenate(out, axis=1)


def _later_blocks(x, nb, carry):
    B = SB_BLOCK
    tot = [_rowsum(x[:, b * B:(b + 1) * B]) for b in range(nb)]
    offs = [None] * nb
    run = carry
    for b in range(nb - 1, -1, -1):
        offs[b] = jnp.broadcast_to(run, (x.shape[0], B))
        run = run + tot[b]
    return (offs[0] if nb == 1 else jnp.concatenate(offs, axis=1)), run


def _sb_group(i, t):
    top = (i + 1) * (SB_QB // SB_BLOCK) - 1 - SB_GROUP * t
    jlo = jnp.maximum(top - SB_GROUP + 1, 0)
    rows = pl.ds(pl.multiple_of(jlo * SB_BLOCK, SB_BLOCK), SB_GROUP * SB_BLOCK)
    return jlo, rows, (top + 1) * SB_BLOCK


def _sb_weights(q, kcat, i, jlo, kend, cs, us, masked):
    B, nb = SB_BLOCK, SB_GROUP
    R = range(len(q))
    z = [_dot_nt(q[h], kcat[h]) * (SB_DH ** -0.5) for h in R]
    e = [jnp.exp(-jnp.abs(z[h])) for h in R]
    l1p = [jnp.log(1.0 + e[h]) for h in R]
    lsp = [jnp.minimum(z[h], 0.0) - l1p[h] for h in R]
    lk = [lsp[h] - z[h] for h in R]
    vis = None
    if masked:
        qpos = i * SB_QB + lax.broadcasted_iota(jnp.int32, (SB_QB, nb * B), 0)
        kpos = jlo * B + lax.broadcasted_iota(jnp.int32, (SB_QB, nb * B), 1)
        vis = jnp.logical_and(kpos < jnp.minimum(qpos, kend), kpos >= P0)
        lk = [jnp.where(vis, lk[h], 0.0) for h in R]
    later = [_later_blocks(lk[h], nb, cs[h]) for h in R]
    cum = [_cumsum_after(lk[h], nb, us) for h in R]
    w = [jnp.exp(lsp[h] + cum[h] + later[h][0]) for h in R]
    if masked:
        w = [jnp.where(vis, w[h], 0.0) for h in R]
    return lsp, vis, w, [later[h][1] for h in R]


def _sb_loop(i, step, carry):
    trips = ((i + 1) * (SB_QB // SB_BLOCK) - 1 + SB_GROUP) // SB_GROUP
    carry = step(True)(0, carry)
    carry = lax.fori_loop(1, trips - 1, step(False), carry)
    return lax.fori_loop(jnp.maximum(trips - 1, 1), trips, step(True), carry)


def _ride(rider, n_in, n_out, grid):
    n_rin = len(rider.ins) if rider else 0
    n_rout = len(rider.out_shapes) if rider else 0

    def split(refs):
        ins, rin = refs[:n_in], refs[n_in:n_in + n_rin]
        outs = refs[n_in + n_rin:n_in + n_rin + n_out]
        rout = refs[n_in + n_rin + n_out:n_in + n_rin + n_out + n_rout]
        return ins, outs, (rin, rout, refs[n_in + n_rin + n_out + n_rout:])

    def at(step, fn, r):
        if rider is None:
            return
        cond = None
        for a, g in enumerate(grid):
            c = pl.program_id(a) == (g - 1 if step == "last" else 0)
            cond = c if cond is None else jnp.logical_and(cond, c)

        @pl.when(cond)
        def _():
            fn(*r)

    first = lambda r: at("first", rider.start if rider else None, r)
    last = lambda r: at("last", rider.finish if rider else None, r)
    return split, first, last


def sb_fwd(qs, ks, vs, rider=None):
    tp = qs.shape[0]
    nq = tp // SB_QB
    B, G, hb, QB = SB_BLOCK, SB_GROUP, SB_HEADS_PER_STEP, SB_QB
    assert tp >= G * B and tp % QB == 0 and QB % B == 0 and G * B >= QB
    grid = (HEADS // hb, nq)
    split, ride_first, ride_last = _ride(rider, 3, 2, grid)

    def body(*refs):
        (q_ref, k_ref, v_ref), (o_ref, ob_ref), ride = split(refs)
        ride_first(ride)
        i = pl.program_id(1)
        R = range(hb)
        hs = [slice(h * SB_DH, (h + 1) * SB_DH) for h in R]
        q = [q_ref[:, hs[h]] for h in R]
        us = (lax.broadcasted_iota(jnp.int32, (B, B), 0) > lax.broadcasted_iota(jnp.int32, (B, B), 1)).astype(bf16)

        def make_step(masked):
            def step(t, carry):
                acc, cs = carry
                jlo, rows, kend = _sb_group(i, t)
                _, _, w, cs = _sb_weights(q, [k_ref[rows, hs[h]] for h in R], i, jlo, kend, cs, us, masked)
                pv = [_dot(w[h], v_ref[rows, hs[h]]) for h in R]
                return tuple(acc[h] + pv[h] for h in R), tuple(cs)
            return step

        carry = (tuple(jnp.zeros((QB, SB_DH), f32) for _ in R), tuple(jnp.zeros((QB, 1), f32) for _ in R))
        acc, _ = _sb_loop(i, make_step, carry)
        for h in R:
            o_ref[:, hs[h]] = acc[h]
            ob_ref[:, hs[h]] = acc[h].astype(bf16)
        ride_last(ride)

    blk = pl.BlockSpec((QB, hb * SB_DH), lambda g, i: (i, g))
    full = pl.BlockSpec((tp, hb * SB_DH), lambda g, i: (0, g))
    r_ins = rider.ins if rider else []
    r_outs = rider.out_shapes if rider else []
    res = pl.pallas_call(
        body, name="sb_fwd", grid=grid, in_specs=[blk, full, full] + [_ANY] * len(r_ins),
        out_specs=[blk, blk] + [_ANY] * len(r_outs),
        out_shape=[jax.ShapeDtypeStruct((tp, SB_W), f32), jax.ShapeDtypeStruct((tp, SB_W), bf16)] + list(r_outs),
        scratch_shapes=rider.scratch if rider else [], compiler_params=_params(2),
    )(qs, ks, vs, *r_ins)
    return res[0], res[1], res[2:]


def sb_bwd(qs, ks, vs, o, do, rider=None):
    tp = qs.shape[0]
    nq = tp // SB_QB
    B, G, hb, QB = SB_BLOCK, SB_GROUP, SB_HEADS_PER_STEP, SB_QB
    assert tp >= G * B and tp % QB == 0 and QB % B == 0 and G * B >= QB
    grid = (HEADS // hb, nq)
    split, ride_first, ride_last = _ride(rider, 5, 3, grid)

    def body(*refs):
        (q_ref, k_ref, v_ref, o_ref, do_ref), (dq_ref, dk_ref, dv_ref), ride = split(refs)
        ride_first(ride)
        i = pl.program_id(1)

        @pl.when(i == 0)
        def _():
            dk_ref[...] = jnp.zeros_like(dk_ref)
            dv_ref[...] = jnp.zeros_like(dv_ref)

        R = range(hb)
        hs = [slice(h * SB_DH, (h + 1) * SB_DH) for h in R]
        q = [q_ref[:, hs[h]] for h in R]
        dob = [do_ref[:, hs[h]].astype(bf16) for h in R]
        et = [_rowsum(dob[h].astype(f32) * o_ref[:, hs[h]]) for h in R]
        us = (lax.broadcasted_iota(jnp.int32, (B, B), 0) > lax.broadcasted_iota(jnp.int32, (B, B), 1)).astype(bf16)

        def make_step(masked):
            def step(t, carry):
                dq, cs, ce = carry
                jlo, rows, kend = _sb_group(i, t)
                kcat = [k_ref[rows, hs[h]] for h in R]
                dwv = [_dot_nt(dob[h], v_ref[rows, hs[h]]) for h in R]
                lsp, vis, w, cs = _sb_weights(q, kcat, i, jlo, kend, cs, us, masked)
                wb = [w[h].astype(bf16) for h in R]
                ee = [dwv[h] * wb[h].astype(f32) for h in R]
                later = [_later_blocks(ee[h], G, ce[h]) for h in R]
                cum = [_cumsum_after(ee[h], G, us) for h in R]
                dz = []
                for h in R:
                    d = ee[h] - jnp.exp(lsp[h]) * (et[h] - (cum[h] + later[h][0]))
                    if masked:
                        d = jnp.where(vis, d, 0.0)
                    dz.append((d * (SB_DH ** -0.5)).astype(bf16))
                dkj = [_dot_tn(dz[h], q[h]) for h in R]
                dvj = [_dot_tn(wb[h], dob[h]) for h in R]
                dqj = [_dot(dz[h], kcat[h]) for h in R]
                for h in R:
                    dk_ref[rows, hs[h]] += dkj[h]
                    dv_ref[rows, hs[h]] += dvj[h]
                return tuple(dq[h] + dqj[h] for h in R), tuple(cs), tuple(later[h][1] for h in R)
            return step

        z0 = tuple(jnp.zeros((QB, 1), f32) for _ in R)
        dq, _, _ = _sb_loop(i, make_step, (tuple(jnp.zeros((QB, SB_DH), f32) for _ in R), z0, z0))
        for h in R:
            dq_ref[:, hs[h]] = dq[h]
        ride_last(ride)

    blk = pl.BlockSpec((QB, hb * SB_DH), lambda g, i: (i, g))
    full = pl.BlockSpec((tp, hb * SB_DH), lambda g, i: (0, g))
    r_ins = rider.ins if rider else []
    r_outs = rider.out_shapes if rider else []
    res = pl.pallas_call(
        body, name="sb_bwd", grid=grid, in_specs=[blk, full, full, blk, blk] + [_ANY] * len(r_ins),
        out_specs=[blk, full, full] + [_ANY] * len(r_outs),
        out_shape=[jax.ShapeDtypeStruct((tp, SB_W), f32)] * 3 + list(r_outs),
        scratch_shapes=rider.scratch if rider else [], compiler_params=_params(2),
    )(qs, ks, vs, o, do, *r_ins)
    return res[:3], res[3:]


def adamw(w, g, m, v, name):
    r, c = w.shape
    rt = _tile(r, 128, SUB) if r % SUB == 0 else r
    blk = pl.BlockSpec((rt, c), lambda i: (i, 0))
    c1 =1.0 - ADAM_B1 ** ADAM_STEP
    c2 = 1.0 - ADAM_B2 ** ADAM_STEP

    def body(w_ref, g_ref, m_ref, v_ref, d_ref, mo_ref, vo_ref):
        g_ = g_ref[...]
        m_ = ADAM_B1 * m_ref[...] + (1.0 - ADAM_B1) * g_
        v_ = ADAM_B2 * v_ref[...] + (1.0 - ADAM_B2) * (g_ * g_)
        mo_ref[...] = m_
        vo_ref[...] = v_
        d_ref[...] = -ADAM_LR * ((m_ / c1) / (jnp.sqrt(v_ / c2) + ADAM_EPS) + ADAM_WD * w_ref[...])

    return pl.pallas_call(
        body, name=name, grid=(r // rt,), in_specs=[blk] * 4, out_specs=[blk] * 3,
        out_shape=[jax.ShapeDtypeStruct((r, c), f32)] * 3, compiler_params=_params(1),
    )(w, g, m, v)


def sum_slots(x, name):
    n, r, c = x.shape
    rt = _tile(r, 128, SUB) if r % SUB == 0 else r
    blk = pl.BlockSpec((n, rt, c), lambda i: (0, i, 0))

    def body(x_ref, o_ref):
        acc = x_ref[0].astype(f32)
        for s in range(1, n):
            acc = acc + x_ref[s].astype(f32)
        o_ref[...] = acc

    return pl.pallas_call(
        body, name=name, grid=(r // rt,), in_specs=[blk], out_specs=pl.BlockSpec((rt, c), lambda i: (i, 0)),
        out_shape=jax.ShapeDtypeStruct((r, c), f32), compiler_params=_params(1),
    )(x)


def add2(a, b, name, out_dtype=f32):
    n, r, c = a.shape
    rt = _tile(r, 64, SUB) if r % SUB == 0 else r
    blk = pl.BlockSpec((n, rt, c), lambda i: (0, i, 0))

    def body(a_ref, b_ref, o_ref):
        o_ref[...] = (a_ref[...] + b_ref[...]).astype(out_dtype)

    return pl.pallas_call(
        body, name=name, grid=(r // rt,), in_specs=[blk, blk], out_specs=blk,
        out_shape=jax.ShapeDtypeStruct((n, r, c), out_dtype), compiler_params=_params(1),
    )(a, b)


_ANY = pl.BlockSpec(memory_space=pl.ANY)
_MESH = pl.DeviceIdType.MESH


def _coords():
    return lax.axis_index("x"), lax.axis_index("y"), lax.axis_index("c")


def _chip_peer(x, y, r):
    return x ^ (r >> 1), y ^ (r & 1)


class _Exchange:
    def __init__(self, ins, out_shapes, scratch, start, finish):
        self.ins, self.out_shapes, self.scratch, self.start, self.finish = ins, out_shapes, scratch, start, finish

    def split(self, refs):
        n, m = len(self.ins), len(self.out_shapes)
        return refs[:n], refs[n:n + m], refs[n + m:]


def run_exchange(ex, name):
    def body(*refs):
        ins, outs, sems = ex.split(refs)
        ex.start(ins, outs, sems)
        ex.finish(ins, outs, sems)

    return pl.pallas_call(body, name=name, in_specs=[_ANY] * len(ex.ins), out_specs=[_ANY] * len(ex.out_shapes),
                          out_shape=ex.out_shapes, scratch_shapes=ex.scratch)(*ex.ins)


def gather_chips(big, small):
    nb, n = len(big), len(big) + len(small)
    shards = list(big) + list(small)
    kb = nb * (N_CHIPS - 1)
    k = n * (N_CHIPS - 1)

    def copies(src, dst, sems):
        send, recv, fsend, frecv = sems
        x, y, c = _coords()
        sib = (x, y, 1 - c)
        peers = [_chip_peer(x, y, r) for r in range(1, N_CHIPS)]

        def direct(t, j, slot):
            s = t * (N_CHIPS - 1) + j
            if t < nb:
                return pltpu.make_async_remote_copy(src[t].at[c], dst[t].at[slot, c], send.at[s], recv.at[s],
                                                    device_id=(*peers[j], c), device_id_type=_MESH)
            return pltpu.make_async_remote_copy(src[t], dst[t].at[slot], send.at[s], recv.at[s],
                                                device_id=(*peers[j], c), device_id_type=_MESH)

        def passed(t, j, half):
            s = t * (N_CHIPS - 1) + j
            px, py = peers[j]
            part = dst[t].at[2 * px + py, half]
            return pltpu.make_async_remote_copy(part, part, fsend.at[s], frecv.at[s], device_id=sib, device_id_type=_MESH)

        return direct, passed, peers, 2 * x + y, c

    def start(src, dst, sems):
        direct, _, _, me, _ = copies(src, dst, sems)
        for t in range(n):
            for j in range(N_CHIPS - 1):
                direct(t, j, me).start()

    def finish(src, dst, sems):
        direct, passed, peers, me, c = copies(src, dst, sems)
        fwd = []
        for t in range(nb):
            for j in range(N_CHIPS - 1):
                px, py = peers[j]
                direct(t, j, 2 * px + py).wait_recv()
                fwd.append(passed(t, j, c))
                fwd[-1].start()
        for t in range(nb, n):
            for j in range(N_CHIPS - 1):
                px, py = peers[j]
                direct(t, j, 2 * px + py).wait_recv()
        for t in range(nb):
            for j in range(N_CHIPS - 1):
                passed(t, j, 1 - c).wait_recv()
        for t in range(n):
            for j in range(N_CHIPS - 1):
                direct(t, j, me).wait_send()
        for cp in fwd:
            cp.wait_send()

    return _Exchange(shards, [jax.ShapeDtypeStruct((N_CHIPS,) + s.shape, s.dtype) for s in shards],
                     [pltpu.SemaphoreType.DMA((k,)), pltpu.SemaphoreType.DMA((k,)),
                      pltpu.SemaphoreType.DMA((max(kb, 1),)), pltpu.SemaphoreType.DMA((max(kb, 1),))], start, finish)


def sibling_swap(grads):
    n = len(grads)
    k = n * N_CHIPS

    def copies(src, dst, sems):
        send, recv = sems
        x, y, c = _coords()
        return [pltpu.make_async_remote_copy(src[t].at[o, 1 - c], dst[t].at[o], send.at[t * N_CHIPS + o],
                                             recv.at[t * N_CHIPS + o], device_id=(x, y, 1 - c), device_id_type=_MESH)
                for t in range(n) for o in range(N_CHIPS)]

    def start(src, dst, sems):
        for cp in copies(src, dst, sems):
            cp.start()

    def finish(src, dst, sems):
        cps = copies(src, dst, sems)
        for cp in cps:
            cp.wait_recv()
        for cp in cps:
            cp.wait_send()

    return _Exchange(list(grads), [jax.ShapeDtypeStruct((N_CHIPS,) + g.shape[2:], g.dtype) for g in grads],
                     [pltpu.SemaphoreType.DMA((k,)), pltpu.SemaphoreType.DMA((k,))], start, finish)


def scatter_chips(parts):
    n = len(parts)
    k = n * (N_CHIPS - 1)

    def copy(src, dst, sems, t, r, landing):
        send, recv = sems
        x, y, c = _coords()
        me = 2 * x + y
        px, py = _chip_peer(x, y, r)
        peer = 2 * px + py
        s = t * (N_CHIPS - 1) + r - 1
        return pltpu.make_async_remote_copy(src[t].at[me if landing else peer], dst[t].at[peer if landing else me],
                                            send.at[s], recv.at[s], device_id=(px, py, c), device_id_type=_MESH)

    def start(src, dst, sems):
        for t in range(n):
            for r in range(1, N_CHIPS):
                copy(src, dst, sems, t, r, False).start()

    def finish(src, dst, sems):
        for t in range(n):
            for r in range(1, N_CHIPS):
                copy(src, dst, sems, t, r, True).wait_recv()
        for t in range(n):
            for r in range(1, N_CHIPS):
                copy(src, dst, sems, t, r, False).wait_send()

    return _Exchange(list(parts), [jax.ShapeDtypeStruct(p.shape, p.dtype) for p in parts],
                     [pltpu.SemaphoreType.DMA((k,)), pltpu.SemaphoreType.DMA((k,))], start, finish)


def sibling_send(halves, name):
    n = len(halves)

    def body(*refs):
        src, dst = refs[:n], refs[n:2 * n]
        send, recv = refs[2 * n:]
        x, y, c = _coords()
        cps = [pltpu.make_async_remote_copy(src[t], dst[t], send.at[t], recv.at[t],
                                            device_id=(x, y, 1 - c), device_id_type=_MESH) for t in range(n)]
        for cp in cps:
            cp.start()
        for cp in cps:
            cp.wait_recv()
        for cp in cps:
            cp.wait_send()

    return pl.pallas_call(
        body, name=name, in_specs=[_ANY] * n, out_specs=[_ANY] * n,
        out_shape=[jax.ShapeDtypeStruct(h.shape, h.dtype) for h in halves],
        scratch_shapes=[pltpu.SemaphoreType.DMA((n,)), pltpu.SemaphoreType.DMA((n,))],
    )(*halves)


def gather_all(block, name):
    def body(src, dst, send, recv, loc):
        x, y, c = _coords()
        me = 4 * x + 2 * y + c
        mine = pltpu.make_async_copy(src, dst.at[me], loc)
        mine.start()
        outs = []
        for r in range(1, N_DEV):
            peer = (x ^ (r >> 2), y ^ ((r >> 1) & 1), c ^ (r & 1))
            outs.append(pltpu.make_async_remote_copy(src, dst.at[me], send.at[r - 1], recv.at[r - 1],
                                                     device_id=peer, device_id_type=_MESH))
        for cp in outs:
            cp.start()
        for r in range(1, N_DEV):
            px, py, pc = x ^ (r >> 2), y ^ ((r >> 1) & 1), c ^ (r & 1)
            pltpu.make_async_remote_copy(src, dst.at[4 * px + 2 * py + pc], send.at[r - 1], recv.at[r - 1],
                                         device_id=(px, py, pc), device_id_type=_MESH).wait_recv()
        for cp in outs:
            cp.wait_send()
        mine.wait()

    return pl.pallas_call(
        body, name=name, in_specs=[_ANY], out_specs=_ANY,
        out_shape=jax.ShapeDtypeStruct((N_DEV,) + block.shape, block.dtype),
        scratch_shapes=[pltpu.SemaphoreType.DMA((N_DEV - 1,)), pltpu.SemaphoreType.DMA((N_DEV - 1,)),
                        pltpu.SemaphoreType.DMA(())],
    )(block)


def _pad_lanes(v, n=LANE):
    return jnp.pad(v, ((0, 0), (0, n - v.shape[1])))


def _w_in_pieces():
    cs = (PROJ_BIG + 2 * HEADS) // N_CHIPS
    ab_end = AB_COL + 2 * HEADS
    out = []
    for o in range(N_CHIPS):
        lo, hi = o * cs, (o + 1) * cs
        cand = [("big", lo, min(hi, AB_COL), 0), ("ab", max(lo, AB_COL), min(hi, ab_end), AB_COL),
                ("big", max(lo, ab_end), hi, 2 * HEADS)]
        out.append([(s, a - off, b - off) for s, a, b, off in cand if a < b])
    return out


def _split_w_in(w4):
    big, ab = [], []
    for o, pieces in enumerate(_w_in_pieces()):
        at = 0
        for s, a, b in pieces:
            (big if s == "big" else ab).append(w4[o][:, at:at + b - a])
            at += b - a
    return jnp.concatenate(big, axis=1), _pad_lanes(jnp.concatenate(ab, axis=1))


def _join_w_in(big, ab):
    src = {"big": big, "ab": ab}
    return jnp.stack([jnp.concatenate([src[s][:, a:b] for s, a, b in pieces], axis=1) for pieces in _w_in_pieces()])


def _conv_w8(w):
    return jnp.pad(w, ((0, SUB - DN_CONV), (0, 0)))


def _row_layout(gc, tp):
    nc = tp // CHUNK
    g = gc[:, :, 0].reshape(HEADS, nc, 1, CHUNK)
    g = jnp.broadcast_to(g, (HEADS, nc, SUB, CHUNK))
    return jnp.pad(g, ((0, 0), (0, 0), (0, 0), (0, LANE - CHUNK))).reshape(HEADS, nc * SUB, LANE)


def _step(x, meta, W, target, late_weights=None, early_swap=None, early_grads=None, last_grads=None):
    W = dict(W)
    seq = x.shape[0]
    tp = P0 + N_META + seq
    h0 = jnp.concatenate([jnp.zeros((P0, D_MODEL), f32), meta, x], axis=0)
    w_big, w_ab = _split_w_in(W["w_in"])
    cq8, ck8, cv8 = _conv_w8(W["conv_q"]), _conv_w8(W["conv_k"]), _conv_w8(W["conv_v"])
    al, dtb = _pad_lanes(W["dn_a_log"]), _pad_lanes(W["dn_dt_bias"])

    n1 = rms_fwd(h0, W["norm_mix_gain"], "rms1_fwd")
    proj = matmul(n1, w_big, "nn", "proj_fwd")
    pab = matmul(n1, w_ab, "nn", "pab_fwd")
    qn = conv_fwd(proj, cq8, C_DQ * 8, 8, True, "conv_q_fwd")
    kn = conv_fwd(proj, ck8, C_DK * 8, 8, True, "conv_k_fwd")
    va = conv_fwd(proj, cv8, C_DV * 8, 16, False, "conv_v_fwd")
    gc, bc = ab_fwd(pab, al, dtb)
    grow = _row_layout(gc, tp)
    o_dn, states = gdn_fwd(qn, kn, va, gc, bc, grow)
    on = dn_out_fwd(o_dn, proj, W["dn_out_norm_gain"])
    qs, ks, vs = sb_prep_fwd(proj, W["sb_q_norm_gain"], W["sb_k_norm_gain"])
    o_sb, o_sb16, arrived = sb_fwd(qs, ks, vs, rider=late_weights[0] if late_weights else None)
    if late_weights:
        W.update(late_weights[1](arrived))
    ydn = matmul(on, W["w_branch_dn"], "nn", "ydn_fwd")
    ysb = matmul(o_sb16, W["w_branch_sb"], "nn", "ysb_fwd")
    merged = merge_fwd(proj, ydn, ysb)
    h1 = matmul(merged, W["w_out"], "nn", "wout_fwd", residual=h0)
    n2 = rms_fwd(h1, W["norm_ffn_gain"], "rms2_fwd")
    w_ffn_in = _ffn_interleave(W["w_ffn_in"])
    u = matmul(n2, w_ffn_in, "nn", "ffn_in_fwd", tn_t=512)
    act = swiglu_fwd(u)
    y = matmul(act, W["w_ffn_out"], "nn", "ffn_out_fwd", residual=h1)
    dy, dy16, loss = loss_head(y, target)

    G = {}
    dact = matmul(dy16, W["w_ffn_out"], "nt", "ffn_out_dx", tn_t=1408)
    G["w_ffn_out"] = matmul(act, dy16, "tn", "ffn_out_dw", tm_t=1408)
    du = swiglu_bwd(u, dact)
    dn2 = matmul(du, w_ffn_in, "nt", "ffn_in_dx", tk_t=512)
    G["w_ffn_in"] = _ffn_interleave(matmul(n2, du, "tn", "ffn_in_dw", tn_t=512))
    dh1, dh1_16, G["norm_ffn_gain"] = rms_bwd(h1, W["norm_ffn_gain"], dn2, dy, "rms2_bwd")
    dmerged = matmul(dh1_16, W["w_out"], "nt", "wout_dx")
    G["w_out"] = matmul(merged, dh1_16, "tn", "wout_dw")
    dyd, dys, d_gates = merge_bwd(proj, ydn, ysb, dmerged)
    don = matmul(dyd, W["w_branch_dn"], "nt", "ydn_dx")
    G["w_branch_dn"] = matmul(on, dyd, "tn", "ydn_dw")
    do_sb = matmul(dys, W["w_branch_sb"], "nt", "ysb_dx")
    G["w_branch_sb"] = matmul(o_sb16, dys, "tn", "ysb_dw")
    do_dn, dz, G["dn_out_norm_gain"] = dn_out_bwd(o_dn, proj, W["dn_out_norm_gain"], don)
    (dqn, dkn, dva, dgc, dbc), swapped = gdn_bwd(qn, kn, va, gc, bc, grow, states, do_dn,
                                                 rider=early_swap[0](G) if early_swap else None)
    if early_swap:
        early_swap[1](swapped)
    dpab, dal, ddt = ab_bwd(pab, al, dtb, dgc, dbc)
    G["dn_a_log"], G["dn_dt_bias"] = dal[:, :HEADS], ddt[:, :HEADS]
    dyq, dcq = conv_bwd_act(proj, cq8, dqn, C_DQ * 8, 8, True, "conv_q_bwd")
    dyk, dck = conv_bwd_act(proj, ck8, dkn, C_DK * 8, 8, True, "conv_k_bwd")
    dyv, dcv = conv_bwd_act(proj, cv8, dva, C_DV * 8, 16, False, "conv_v_bwd")
    G["conv_q"], G["conv_k"], G["conv_v"] = dcq[:DN_CONV], dck[:DN_CONV], dcv[:DN_CONV]
    d_dq = conv_bwd_in(dyq, cq8, "conv_q_dx")
    d_dk = conv_bwd_in(dyk, ck8, "conv_k_dx")
    d_dv = conv_bwd_in(dyv, cv8, "conv_v_dx")
    (dqs, dks, dvs), delivered = sb_bwd(qs, ks, vs, o_sb, do_sb, rider=early_grads[0](G) if early_grads else None)
    if early_grads:
        early_grads[1](delivered)
    d_sb, G["sb_q_norm_gain"], G["sb_k_norm_gain"] = sb_prep_bwd(
        proj, W["sb_q_norm_gain"], W["sb_k_norm_gain"], dqs, dks, dvs)
    dproj = jnp.concatenate([d_dq, d_dk, d_dv, dz, d_sb, d_gates], axis=1)
    dw_big = matmul(n1, dproj, "tn", "proj_dw")
    dw_ab = matmul(n1, dpab, "tn", "pab_dw")
    G["w_in"] = _join_w_in(dw_big, dw_ab)
    if last_grads:
        dn1, delivered = matmul(dproj, w_big, "nt", "proj_dx", tk_t=1024, rider=last_grads[0](G))
        last_grads[1](delivered)
    else:
        dn1 = matmul(dproj, w_big, "nt", "proj_dx", tk_t=1024)
    dn1 = matmul(dpab, w_ab, "nt", "pab_dx", residual=dn1)
    dh0, _, G["norm_mix_gain"] = rms_bwd(h0, W["norm_mix_gain"], dn1, dh1, "rms1_bwd")
    G["meta_tokens"] = dh0[P0:P0 + N_META]
    return loss, dh0[P0 + N_META:], G


_BIG = ("w_in", "w_branch_dn", "w_branch_sb", "w_out", "w_ffn_in", "w_ffn_out")
_COL_SHARDED = ("w_in", "w_ffn_in", "meta_tokens", "conv_q", "conv_k", "conv_v")
_SMALL_REPL = ("norm_mix_gain", "norm_ffn_gain", "dn_a_log", "dn_dt_bias", "dn_out_norm_gain", "sb_q_norm_gain",
               "sb_k_norm_gain")
_SMALL_SHARD = ("meta_tokens", "conv_q", "conv_k", "conv_v")
_ORDER = ("meta_tokens", "norm_mix_gain", "w_in", "conv_q", "conv_k", "conv_v", "dn_a_log", "dn_dt_bias",
          "dn_out_norm_gain", "sb_q_norm_gain", "sb_k_norm_gain", "w_branch_dn", "w_branch_sb", "w_out",
          "norm_ffn_gain", "w_ffn_in", "w_ffn_out")


def _unshard(g4, name):
    if name in _COL_SHARDED:
        r, cs = g4.shape[1:]
        return jnp.transpose(g4, (1, 0, 2)).reshape(r, N_CHIPS * cs)
    return g4.reshape((-1,) + g4.shape[2:])


def _to_shards(full, name):
    if full.ndim == 3:
        return full
    if name in _COL_SHARDED:
        r, c = full.shape
        return jnp.transpose(full.reshape(r, N_CHIPS, c // N_CHIPS), (1, 0, 2))
    r, c = full.shape
    return full.reshape(N_CHIPS, r // N_CHIPS, c)


def _rows_1024(a):
    r, c = a.shape
    if c >= 1024:
        return a.reshape(r * (c // 1024), 1024)
    return jnp.pad(a, ((0, 0), (0, 1024 - c)))


def kernel(x, meta_tokens, norm_mix_gain, w_in, conv_q, conv_k, conv_v, dn_a_log, dn_dt_bias, dn_out_norm_gain, sb_q_norm_gain, sb_k_norm_gain, w_branch_dn, w_branch_sb, w_out, norm_ffn_gain, w_ffn_in, w_ffn_out, loss_target, m_meta_tokens, m_norm_mix_gain, m_w_in, m_conv_q, m_conv_k, m_conv_v, m_dn_a_log, m_dn_dt_bias, m_dn_out_norm_gain, m_sb_q_norm_gain, m_sb_k_norm_gain, m_w_branch_dn, m_w_branch_sb, m_w_out, m_norm_ffn_gain, m_w_ffn_in, m_w_ffn_out, v_meta_tokens, v_norm_mix_gain, v_w_in, v_conv_q, v_conv_k, v_conv_v, v_dn_a_log, v_dn_dt_bias, v_dn_out_norm_gain, v_sb_q_norm_gain, v_sb_k_norm_gain, v_w_branch_dn, v_w_branch_sb, v_w_out, v_norm_ffn_gain, v_w_ffn_in, v_w_ffn_out):
    Wl = dict(meta_tokens=meta_tokens, norm_mix_gain=norm_mix_gain, w_in=w_in[0], conv_q=conv_q[0], conv_k=conv_k[0],
              conv_v=conv_v[0], dn_a_log=dn_a_log, dn_dt_bias=dn_dt_bias, dn_out_norm_gain=dn_out_norm_gain,
              sb_q_norm_gain=sb_q_norm_gain, sb_k_norm_gain=sb_k_norm_gain, w_branch_dn=w_branch_dn[0],
              w_branch_sb=w_branch_sb[0], w_out=w_out[0], norm_ffn_gain=norm_ffn_gain, w_ffn_in=w_ffn_in[0],
              w_ffn_out=w_ffn_out[0])
    Ml = dict(meta_tokens=m_meta_tokens, norm_mix_gain=m_norm_mix_gain, w_in=m_w_in[0], conv_q=m_conv_q[0],
              conv_k=m_conv_k[0], conv_v=m_conv_v[0], dn_a_log=m_dn_a_log, dn_dt_bias=m_dn_dt_bias,
              dn_out_norm_gain=m_dn_out_norm_gain, sb_q_norm_gain=m_sb_q_norm_gain, sb_k_norm_gain=m_sb_k_norm_gain,
              w_branch_dn=m_w_branch_dn[0], w_branch_sb=m_w_branch_sb[0], w_out=m_w_out[0],
              norm_ffn_gain=m_norm_ffn_gain, w_ffn_in=m_w_ffn_in[0], w_ffn_out=m_w_ffn_out[0])
    Vl = dict(meta_tokens=v_meta_tokens, norm_mix_gain=v_norm_mix_gain, w_in=v_w_in[0], conv_q=v_conv_q[0],
              conv_k=v_conv_k[0], conv_v=v_conv_v[0], dn_a_log=v_dn_a_log, dn_dt_bias=v_dn_dt_bias,
              dn_out_norm_gain=v_dn_out_norm_gain, sb_q_norm_gain=v_sb_q_norm_gain, sb_k_norm_gain=v_sb_k_norm_gain,
              w_branch_dn=v_w_branch_dn[0], w_branch_sb=v_w_branch_sb[0], w_out=v_w_out[0],
              norm_ffn_gain=v_norm_ffn_gain, w_ffn_in=v_w_ffn_in[0], w_ffn_out=v_w_ffn_out[0])
    lead = {n: (1,) if (n in _BIG or n in ("conv_q", "conv_k", "conv_v")) else () for n in _ORDER}

    chip = 2 * lax.axis_index("x") + lax.axis_index("y")
    c = lax.axis_index("c")
    halved = {n: Wl[n].astype(bf16).reshape(2, Wl[n].shape[0] // 2, Wl[n].shape[1]) for n in _BIG}

    def gathered_weights(names, owns, outs):
        res = {}
        for n, own, g4 in zip(names, owns, outs):
            g4 = lax.dynamic_update_slice(g4, own[None], (chip,) + (0,) * own.ndim)
            if n in _BIG:
                g4 = g4.reshape(N_CHIPS, 2 * g4.shape[2], g4.shape[3])
            res[n] = g4 if n == "w_in" else _unshard(g4, n)
        return res

    first = ["w_in"] + list(_SMALL_SHARD)
    first_own = [halved["w_in"]] + [Wl[n] for n in _SMALL_SHARD]
    W = dict(Wl)
    W.update(gathered_weights(first, first_own, run_exchange(gather_chips(first_own[:1], first_own[1:]), "gather_w_in")))
    late = [n for n in _BIG if n != "w_in"]
    late_own = [halved[n] for n in late]
    for n in late:
        del W[n]

    def halves_of(names, G):
        g4 = [_to_shards(G[n], n) for n in names]
        return [g.reshape(N_CHIPS, 2, g.shape[1] // 2, g.shape[2]) for g in g4]

    def pair_added(g42, from_sib, tag, wire):
        mine = [lax.dynamic_index_in_dim(g, c, axis=1, keepdims=False) for g in g42]
        return [add2(a, b, "grad_pair_add_%s%d" % (tag, t), out_dtype=wire)
                for t, (a, b) in enumerate(zip(mine, from_sib))]

    def chip_reduced(parts, slots, tag):
        slots = [lax.dynamic_update_slice(s, lax.dynamic_index_in_dim(p, chip, axis=0, keepdims=True), (chip, 0, 0))
                 for s, p in zip(slots, parts)]
        return [sum_slots(s, "grad_chip_sum_%s%d" % (tag, t)) for t, s in enumerate(slots)]

    early, last = {}, {}

    def early_swap_begin(G):
        early["g42"] = halves_of(late, G)
        return sibling_swap(early["g42"])

    def early_begin(G):
        early["parts"] = pair_added(early["g42"], early["from_sib"], "a", f32)
        return scatter_chips(early["parts"])

    def last_begin(G):
        g42 = halves_of(["w_in"], G)
        last["parts"] = pair_added(g42, run_exchange(sibling_swap(g42), "grad_sibling_swap_b"), "b", bf16)
        return scatter_chips(last["parts"])

    loss, grad_x, G = _step(
        x[0], W["meta_tokens"], W, loss_target[0],
        late_weights=(gather_chips(late_own, []), lambda outs: gathered_weights(late, late_own, outs)),
        early_swap=(early_swap_begin, lambda outs: early.update(from_sib=outs)),
        early_grads=(early_begin, lambda slots: early.update(halves=chip_reduced(early["parts"], slots, "a"))),
        last_grads=(last_begin, lambda slots: last.update(halves=chip_reduced(last["parts"], slots, "b"))))
    halves = last["halves"] + early["halves"]
    theirs = sibling_send(halves, "grad_sibling_send")
    Gs = {}
    for n, h, o in zip(["w_in"] + late, halves, theirs):
        Gs[n] = lax.dynamic_update_slice(jnp.concatenate([o, o], axis=0), h, (c * h.shape[0], 0))

    small_names = list(_SMALL_REPL) + list(_SMALL_SHARD)
    pieces = [_rows_1024(G[n]) for n in small_names] + [_rows_1024(loss)]
    counts = [p.shape[0] for p in pieces]
    pack = jnp.concatenate(pieces, axis=0)
    pad_rows = (-pack.shape[0]) % SUB
    pack = jnp.pad(pack, ((0, pad_rows), (0, 0)))
    total = sum_slots(gather_all(pack, "small_gather"), "small_sum")
    chip = 2 * lax.axis_index("x") + lax.axis_index("y")
    row = 0
    for n, cnt in zip(small_names, counts[:-1]):
        blk = total[row:row + cnt]
        row += cnt
        full_shape = G[n].shape
        if full_shape[1] >= 1024:
            blk = blk.reshape(full_shape)
        else:
            blk = blk[:, :full_shape[1]]
        if n in _SMALL_SHARD:
            cs = full_shape[1] // N_CHIPS
            blk = lax.dynamic_slice_in_dim(blk, chip * cs, cs, axis=1)
        Gs[n] = blk
    loss_out = total[row, 0]

    grads, deltas, new_m, new_v = [], [], [], []
    for n in _ORDER:
        d, m2, v2 = adamw(Wl[n], Gs[n], Ml[n], Vl[n], "adamw_" + n)
        shape = lead[n] + Wl[n].shape
        grads.append(Gs[n].reshape(shape))
        deltas.append(d.reshape(shape))
        new_m.append(m2.reshape(shape))
        new_v.append(v2.reshape(shape))
    return (loss_out, grad_x[None], *grads, *deltas, *new_m, *new_v)
```

```python
import jax
import jax.numpy as jnp
from jax import lax
from jax.experimental import pallas as pl
from jax.experimental.pallas import tpu as pltpu

f32 = jnp.float32
bf16 = jnp.bfloat16

D_MODEL = 1024
N_META = 16
CHUNK = 64
HEADS = 8
DN_DK = 128
DN_DV = 256
DN_CONV = 4
DN_QK = HEADS * DN_DK
DN_V = HEADS * DN_DV
SB_DH = 128
SB_W = HEADS * SB_DH
SB_BLOCK = 128
SB_QB = 384
SB_GROUP = 4
SB_HEADS_PER_STEP = 2
SB_FWD_HEADS_PER_STEP = 4
GDN_HEADS_PER_STEP = 8
CONV_W = 512
D_FF = 2816
RMS_EPS = 1e-6
L2_EPS = 1e-6
ADAM_LR = 0.001
ADAM_B1 = 0.9
ADAM_B2 = 0.999
ADAM_EPS = 1e-08
ADAM_WD = 0.01
ADAM_STEP = 10

P0 = 112
LANE = 128
SUB = 8
VMEM_LIMIT = 48 * 1024 * 1024
N_CHIPS = 4
N_DEV = 8

C_DQ, C_DK, C_DV, C_DZ, C_SQ, C_SK, C_SV, C_GDN, C_GSB = 0, 1, 2, 4, 6, 7, 8, 9, 10
PROJ_BIG = 11 * 1024
AB_COL = 2 * DN_QK + 2 * DN_V


def _params(n_axes):
    return pltpu.CompilerParams(dimension_semantics=("arbitrary",) * n_axes, vmem_limit_bytes=VMEM_LIMIT)


def _tile(n, target, q=LANE):
    best = None
    for t in range(q, min(n, target) + 1, q):
        if n % t == 0:
            best = t
    return best if best is not None else n


def _dot(a, b):
    return jnp.dot(a.astype(bf16), b.astype(bf16), preferred_element_type=f32)


def _dot_nt(a, b):
    return lax.dot_general(a.astype(bf16), b.astype(bf16), (((1,), (1,)), ((), ())), preferred_element_type=f32)


def _dot_tn(a, b):
    return lax.dot_general(a.astype(bf16), b.astype(bf16), (((0,), (0,)), ((), ())), preferred_element_type=f32)


_HI = lax.Precision.HIGH


def _hdot(a, b):
    return jnp.dot(a, b, precision=_HI, preferred_element_type=f32)


def _hdot_nt(a, b):
    return lax.dot_general(a, b, (((1,), (1,)), ((), ())), precision=_HI, preferred_element_type=f32)


def _hdot_tn(a, b):
    return lax.dot_general(a, b, (((0,), (0,)), ((), ())), precision=_HI, preferred_element_type=f32)


def _sigmoid(x):
    e = jnp.exp(-jnp.abs(x))
    r = 1.0 / (1.0 + e)
    return jnp.where(x >= 0, r, e * r)


def _log1p_small(e):
    return jnp.where(e < 1e-3, e * (1.0 - e * (0.5 - e * (1.0 / 3.0))), jnp.log(1.0 + e))


def _rowsum(x):
    return jnp.sum(x, axis=1, keepdims=True)


def _allsum(x):
    return jnp.sum(jnp.sum(x, axis=1, keepdims=True), axis=0, keepdims=True)


def matmul(a, b, mode, name, residual=None, out_dtype=f32, tm_t=1408, tn_t=1024, tk_t=1408, rider=None):
    if mode == "nn":
        (M, K), (K2, N) = a.shape, b.shape
    elif mode == "nt":
        (M, K), (N, K2) = a.shape, b.shape
    else:
        (K, M), (K2, N) = a.shape, b.shape
    assert K == K2, (a.shape, b.shape, mode)
    tm, tn, tk = _tile(M, tm_t), _tile(N, tn_t), _tile(K, tk_t)
    nk = K // tk
    if mode == "nn":
        a_spec = pl.BlockSpec((tm, tk), lambda i, j, k: (i, k))
        b_spec = pl.BlockSpec((tk, tn), lambda i, j, k: (k, j))
        dims = (((1,), (0,)), ((), ()))
    elif mode == "nt":
        a_spec = pl.BlockSpec((tm, tk), lambda i, j, k: (i, k))
        b_spec = pl.BlockSpec((tn, tk), lambda i, j, k: (j, k))
        dims = (((1,), (1,)), ((), ()))
    else:
        a_spec = pl.BlockSpec((tk, tm), lambda i, j, k: (k, i))
        b_spec = pl.BlockSpec((tk, tn), lambda i, j, k: (k, j))
        dims = (((0,), (0,)), ((), ()))
    o_spec = pl.BlockSpec((tm, tn), lambda i, j, k: (i, j))
    has_res = residual is not None
    grid = (M // tm, N // tn, nk)
    split, ride_first, ride_last = _ride(rider, 3 if has_res else 2, 1, grid)

    def body(*refs):
        ins_, (o_ref,), (rin, rout, rest) = split(refs)
        a_ref, b_ref = ins_[:2]
        r_ref = ins_[2] if has_res else None
        acc_ref, ride = rest[0], (rin, rout, rest[1:])
        ride_first(ride)
        k = pl.program_id(2)

        @pl.when(k == 0)
        def _():
            acc_ref[...] = jnp.zeros_like(acc_ref)

        acc_ref[...] += lax.dot_general(a_ref[...].astype(bf16), b_ref[...].astype(bf16), dims,
                                        preferred_element_type=f32)

        @pl.when(k == nk - 1)
        def _():
            r = acc_ref[...]
            if has_res:
                r = r + r_ref[...]
            o_ref[...] = r.astype(out_dtype)

        ride_last(ride)

    ins = [a, b] + ([residual] if has_res else [])
    specs = [a_spec, b_spec] + ([o_spec] if has_res else [])
    r_ins = rider.ins if rider else []
    r_outs = rider.out_shapes if rider else []
    res = pl.pallas_call(
        body, name=name, grid=grid, in_specs=specs + [_ANY] * len(r_ins), out_specs=[o_spec] + [_ANY] * len(r_outs),
        out_shape=[jax.ShapeDtypeStruct((M, N), out_dtype)] + list(r_outs),
        scratch_shapes=[pltpu.VMEM((tm, tn), f32)] + (rider.scratch if rider else []), compiler_params=_params(3),
    )(*ins, *r_ins)
    return (res[0], res[1:]) if rider else res[0]


def _row_tile(tp):
    return _tile(tp, 512)


def rms_fwd(h, gain, name):
    tp, d = h.shape
    rt = _row_tile(tp)

    def body(h_ref, g_ref, o_ref):
        x = h_ref[...]
        r = lax.rsqrt(jnp.mean(x * x, axis=-1, keepdims=True) + RMS_EPS)
        o_ref[...] = (x * r * g_ref[...]).astype(bf16)

    return pl.pallas_call(
        body, name=name, grid=(tp // rt,),
        in_specs=[pl.BlockSpec((rt, d), lambda i: (i, 0)), pl.BlockSpec((1, d), lambda i: (0, 0))],
        out_specs=pl.BlockSpec((rt, d), lambda i: (i, 0)),
        out_shape=jax.ShapeDtypeStruct((tp, d), bf16), compiler_params=_params(1),
    )(h, gain)


def rms_bwd(h, gain, dn, dres, name):
    tp, d = h.shape
    rt = _row_tile(tp)

    def body(h_ref, g_ref, dn_ref, dr_ref, dh_ref, dhb_ref, dg_ref):
        i = pl.program_id(0)
        x = h_ref[...]
        r = lax.rsqrt(jnp.mean(x * x, axis=-1, keepdims=True) + RMS_EPS)
        xh = x * r
        dn_ = dn_ref[...]
        dxh = dn_ * g_ref[...]
        dh = r * (dxh - xh * jnp.mean(dxh * xh, axis=-1, keepdims=True)) + dr_ref[...]
        dh_ref[...] = dh
        dhb_ref[...] = dh.astype(bf16)
        part = jnp.sum(dn_ * xh, axis=0, keepdims=True)

        @pl.when(i == 0)
        def _():
            dg_ref[...] = part

        @pl.when(i > 0)
        def _():
            dg_ref[...] += part

    row = pl.BlockSpec((rt, d), lambda i: (i, 0))
    vec = pl.BlockSpec((1, d), lambda i: (0, 0))
    return pl.pallas_call(
        body, name=name, grid=(tp // rt,), in_specs=[row, vec, row, row], out_specs=[row, row, vec],
        out_shape=[jax.ShapeDtypeStruct((tp, d), f32), jax.ShapeDtypeStruct((tp, d), bf16),
                   jax.ShapeDtypeStruct((1, d), f32)],
        compiler_params=_params(1),
    )(h, gain, dn, dres)


def loss_head(y, target):
    tp, d = y.shape
    rt = P0 + N_META
    assert rt == SB_BLOCK and tp % rt == 0 and target.shape == (tp - rt, d)

    def body(y_ref, t_ref, dy_ref, dyb_ref, l_ref):
        i = pl.program_id(0)

        @pl.when(i == 0)
        def _():
            dy_ref[...] = jnp.zeros_like(dy_ref)
            dyb_ref[...] = jnp.zeros_like(dyb_ref)
            l_ref[...] = jnp.zeros_like(l_ref)

        @pl.when(i > 0)
        def _():
            err = y_ref[...] - t_ref[...]
            dy = err * (1.0 / d)
            dy_ref[...] = dy
            dyb_ref[...] = dy.astype(bf16)
            l_ref[...] += jnp.broadcast_to(_allsum(err * err) * (0.5 / d), l_ref.shape)

    row = pl.BlockSpec((rt, d), lambda i: (i, 0))
    return pl.pallas_call(
        body, name="loss_head", grid=(tp // rt,),
        in_specs=[row, pl.BlockSpec((rt, d), lambda i: (jnp.maximum(i - 1, 0), 0))],
        out_specs=[row, row, pl.BlockSpec((1, LANE), lambda i: (0, 0))],
        out_shape=[jax.ShapeDtypeStruct((tp, d), f32), jax.ShapeDtypeStruct((tp, d), bf16),
                   jax.ShapeDtypeStruct((1, LANE), f32)],
        compiler_params=_params(1),
    )(y, target)


def swiglu_fwd(u):
    tp = u.shape[0]
    rt, cb = _row_tile(tp), D_FF // 2
    nb = D_FF // cb

    def body(g_ref, u_ref, o_ref):
        g = g_ref[...]
        o_ref[...] = (g * _sigmoid(g) * u_ref[...]).astype(bf16)

    return pl.pallas_call(
        body, name="swiglu_fwd", grid=(tp // rt, nb),
        in_specs=[pl.BlockSpec((rt, cb), lambda i, j: (i, j)), pl.BlockSpec((rt, cb), lambda i, j: (i, j + nb))],
        out_specs=pl.BlockSpec((rt, cb), lambda i, j: (i, j)),
        out_shape=jax.ShapeDtypeStruct((tp, D_FF), bf16), compiler_params=_params(2),
    )(u, u)


def swiglu_bwd(u, dact):
    tp = u.shape[0]
    rt, cb = _row_tile(tp), D_FF // 2
    nb = D_FF // cb

    def body(g_ref, u_ref, da_ref, dg_ref, du_ref):
        g = g_ref[...]
        s = _sigmoid(g)
        da = da_ref[...]
        dg_ref[...] = (da * u_ref[...] * s * (1.0 + g * (1.0 - s))).astype(bf16)
        du_ref[...] = (da * g * s).astype(bf16)

    lo = pl.BlockSpec((rt, cb), lambda i, j: (i, j))
    hi = pl.BlockSpec((rt, cb), lambda i, j: (i, j + nb))
    dgate, dup = pl.pallas_call(
        body, name="swiglu_bwd", grid=(tp // rt, nb), in_specs=[lo, hi, lo], out_specs=[lo, lo],
        out_shape=[jax.ShapeDtypeStruct((tp, D_FF), bf16)] * 2, compiler_params=_params(2),
    )(u, u, dact)
    return dgate, dup


def merge_fwd(proj, ydn, ysb):
    tp = proj.shape[0]
    rt, d = _row_tile(tp), D_MODEL

    def body(gd_ref, gs_ref, yd_ref, ys_ref, o_ref):
        o_ref[...] = (_sigmoid(gd_ref[...]) * yd_ref[...] + _sigmoid(gs_ref[...]) * ys_ref[...]).astype(bf16)

    row = pl.BlockSpec((rt, d), lambda i: (i, 0))
    return pl.pallas_call(
        body, name="merge_fwd", grid=(tp // rt,),
        in_specs=[pl.BlockSpec((rt, d), lambda i: (i, C_GDN)), pl.BlockSpec((rt, d), lambda i: (i, C_GSB)), row, row],
        out_specs=row, out_shape=jax.ShapeDtypeStruct((tp, d), bf16), compiler_params=_params(1),
    )(proj, proj, ydn, ysb)


def merge_bwd(proj, ydn, ysb, dm):
    tp = proj.shape[0]
    rt, d = _row_tile(tp), D_MODEL

    def body(gd_ref, gs_ref, yd_ref, ys_ref, dm_ref, dyd_ref, dys_ref, dg_ref):
        dm_ = dm_ref[...]
        sd = _sigmoid(gd_ref[...])
        ss = _sigmoid(gs_ref[...])
        dyd_ref[...] = (dm_ * sd).astype(bf16)
        dys_ref[...] = (dm_ * ss).astype(bf16)
        dg_ref[:, :d] = (dm_ * yd_ref[...] * sd * (1.0 - sd)).astype(bf16)
        dg_ref[:, d:] = (dm_ * ys_ref[...] * ss * (1.0 - ss)).astype(bf16)

    row = pl.BlockSpec((rt, d), lambda i: (i, 0))
    return pl.pallas_call(
        body, name="merge_bwd", grid=(tp // rt,),
        in_specs=[pl.BlockSpec((rt, d), lambda i: (i, C_GDN)), pl.BlockSpec((rt, d), lambda i: (i, C_GSB)), row, row, row],
        out_specs=[row, row, pl.BlockSpec((rt, 2 * d), lambda i: (i, 0))],
        out_shape=[jax.ShapeDtypeStruct((tp, d), bf16)] * 2 + [jax.ShapeDtypeStruct((tp, 2 * d), bf16)],
        compiler_params=_params(1),
    )(proj, proj, ydn, ysb, dm)


def dn_out_fwd(o, proj, gain):
    tp = o.shape[0]
    rt, cb, wide = _row_tile(tp), DN_DV, 1024
    zb = C_DZ * 1024 // wide

    def body(o_ref, z_ref, g_ref, y_ref):
        for s in range(wide // cb):
            sl = slice(s * cb, (s + 1) * cb)
            x = o_ref[:, sl]
            r = lax.rsqrt(jnp.mean(x * x, axis=-1, keepdims=True) + RMS_EPS)
            z = z_ref[:, sl]
            y_ref[:, sl] = (x * r * g_ref[...] * (z * _sigmoid(z))).astype(bf16)

    blk = pl.BlockSpec((rt, wide), lambda i, j: (i, j))
    return pl.pallas_call(
        body, name="dn_out_fwd", grid=(tp // rt, DN_V // wide),
        in_specs=[blk, pl.BlockSpec((rt, wide), lambda i, j: (i, j + zb)), pl.BlockSpec((1, cb), lambda i, j: (0, 0))],
        out_specs=blk, out_shape=jax.ShapeDtypeStruct((tp, DN_V), bf16), compiler_params=_params(2),
    )(o, proj, gain)


def dn_out_bwd(o, proj, gain, dy):
    tp = o.shape[0]
    rt, cb, wide = _row_tile(tp), DN_DV, 1024
    zb = C_DZ * 1024 // wide

    def body(o_ref, z_ref, g_ref, dy_ref, do_ref, dz_ref, dg_ref):
        i, j = pl.program_id(0), pl.program_id(1)
        g = g_ref[...]
        part = jnp.zeros((1, cb), f32)
        for hh in range(wide // cb):
            sl = slice(hh * cb, (hh + 1) * cb)
            x = o_ref[:, sl]
            r = lax.rsqrt(jnp.mean(x * x, axis=-1, keepdims=True) + RMS_EPS)
            xh = x * r
            z = z_ref[:, sl]
            s = _sigmoid(z)
            dy_ = dy_ref[:, sl]
            drn = dy_ * (z * s)
            dz_ref[:, sl] = (dy_ * xh * g * s * (1.0 + z * (1.0 - s))).astype(bf16)
            dxh = drn * g
            do_ref[:, sl] = r * (dxh - xh * jnp.mean(dxh * xh, axis=-1, keepdims=True))
            part = part + jnp.sum(drn * xh, axis=0, keepdims=True)
        first = jnp.logical_and(i == 0, j == 0)

        @pl.when(first)
        def _():
            dg_ref[...] = part

        @pl.when(jnp.logical_not(first))
        def _():
            dg_ref[...] += part

    blk = pl.BlockSpec((rt, wide), lambda i, j: (i, j))
    vec = pl.BlockSpec((1, cb), lambda i, j: (0, 0))
    return pl.pallas_call(
        body, name="dn_out_bwd", grid=(tp // rt, DN_V // wide),
        in_specs=[blk, pl.BlockSpec((rt, wide), lambda i, j: (i, j + zb)), vec, blk],
        out_specs=[blk, blk, vec],
        out_shape=[jax.ShapeDtypeStruct((tp, DN_V), f32), jax.ShapeDtypeStruct((tp, DN_V), bf16),
                   jax.ShapeDtypeStruct((1, cb), f32)],
        compiler_params=_params(2),
    )(o, proj, gain, dy)


def sb_prep_fwd(proj, gq, gk):
    tp = proj.shape[0]
    rt, cb = _row_tile(tp), SB_DH

    def body(q_ref, k_ref, v_ref, gq_ref, gk_ref, qo_ref, ko_ref, vo_ref):
        for x_ref, g_ref, o_ref in ((q_ref, gq_ref, qo_ref), (k_ref, gk_ref, ko_ref)):
            for h in range(HEADS):
                sl = slice(h * cb, (h + 1) * cb)
                x = x_ref[:, sl]
                r = lax.rsqrt(jnp.mean(x * x, axis=-1, keepdims=True) + RMS_EPS)
                o_ref[:, sl] = (x * r * g_ref[...]).astype(bf16)
        vo_ref[...] = v_ref[...].astype(bf16)

    blk = pl.BlockSpec((rt, SB_W), lambda i: (i, 0))
    vec = pl.BlockSpec((1, cb), lambda i: (0, 0))
    return pl.pallas_call(
        body, name="sb_prep_fwd", grid=(tp // rt,),
        in_specs=[pl.BlockSpec((rt, SB_W), lambda i: (i, C_SQ)), pl.BlockSpec((rt, SB_W), lambda i: (i, C_SK)),
                  pl.BlockSpec((rt, SB_W), lambda i: (i, C_SV)), vec, vec],
        out_specs=[blk] * 3, out_shape=[jax.ShapeDtypeStruct((tp, SB_W), bf16)] * 3, compiler_params=_params(1),
    )(proj, proj, proj, gq, gk)


def sb_prep_bwd(proj, gq, gk, dqs, dks, dvs):
    tp = proj.shape[0]
    rt, cb = _row_tile(tp), SB_DH

    def body(q_ref, k_ref, gq_ref, gk_ref, dq_ref, dk_ref, dv_ref, do_ref, dgq_ref, dgk_ref):
        first = pl.program_id(0) == 0
        do_ref[:, 2 * SB_W:] = dv_ref[...].astype(bf16)
        for x_ref, g_ref, dn_ref, at, dg_ref in ((q_ref, gq_ref, dq_ref, 0, dgq_ref),
                                                 (k_ref, gk_ref, dk_ref, SB_W, dgk_ref)):
            part = jnp.zeros((1, cb), f32)
            for h in range(HEADS):
                sl = slice(h * cb, (h + 1) * cb)
                x = x_ref[:, sl]
                r = lax.rsqrt(jnp.mean(x * x, axis=-1, keepdims=True) + RMS_EPS)
                xh = x * r
                dn_ = dn_ref[:, sl]
                dxh = dn_ * g_ref[...]
                do_ref[:, at + h * cb:at + (h + 1) * cb] = (
                    r * (dxh - xh * jnp.mean(dxh * xh, axis=-1, keepdims=True))).astype(bf16)
                part = part + jnp.sum(dn_ * xh, axis=0, keepdims=True)

            @pl.when(first)
            def _(dg_ref=dg_ref, part=part):
                dg_ref[...] = part

            @pl.when(jnp.logical_not(first))
            def _(dg_ref=dg_ref, part=part):
                dg_ref[...] += part

    blk = pl.BlockSpec((rt, SB_W), lambda i: (i, 0))
    vec = pl.BlockSpec((1, cb), lambda i: (0, 0))
    return pl.pallas_call(
        body, name="sb_prep_bwd", grid=(tp // rt,),
        in_specs=[pl.BlockSpec((rt, SB_W), lambda i: (i, C_SQ)), pl.BlockSpec((rt, SB_W), lambda i: (i, C_SK)),
                  vec, vec, blk, blk, blk],
        out_specs=[pl.BlockSpec((rt, 3 * SB_W), lambda i: (i, 0)), vec, vec],
        out_shape=[jax.ShapeDtypeStruct((tp, 3 * SB_W), bf16)] + [jax.ShapeDtypeStruct((1, cb), f32)] * 2,
        compiler_params=_params(1),
    )(proj, proj, gq, gk, dqs, dks, dvs)


def _conv_taps(ext, rt):
    taps = []
    for k in range(DN_CONV):
        s = DN_CONV - 1 - k
        taps.append((pltpu.roll(ext, s, axis=0) if s else ext)[SUB:SUB + rt])
    return taps


def _conv_act(taps, w, l2):
    y = taps[0] * w[0:1]
    for k in range(1, DN_CONV):
        y = y + taps[k] * w[k:k + 1]
    s = _sigmoid(y)
    a = y * s
    if l2:
        n = lax.rsqrt(jnp.sum(a * a, axis=-1, keepdims=True) + L2_EPS)
        return y, s, a, n
    return y, s, a, None


def conv_fwd(proj, w8, col_blk, ncb, l2, name):
    tp = proj.shape[0]
    rt = _row_tile(tp)
    hb = rt // SUB
    cw = CONV_W
    cb0 = col_blk * LANE // cw

    def body(x_ref, h_ref, w_ref, o_ref):
        i = pl.program_id(1)
        first = (i > 0).astype(f32)
        for s in range(cw // LANE):
            sl = slice(s * LANE, (s + 1) * LANE)
            ext = jnp.concatenate([h_ref[:, sl] * first, x_ref[:, sl]], axis=0)
            _, _, a, n = _conv_act(_conv_taps(ext, rt), w_ref[:, sl], l2)
            o_ref[:, sl] = a * n if l2 else a

    return pl.pallas_call(
        body, name=name, grid=(ncb * LANE // cw, tp // rt),
        in_specs=[pl.BlockSpec((rt, cw), lambda j, i: (i, j + cb0)),
                  pl.BlockSpec((SUB, cw), lambda j, i: (jnp.maximum(i * hb - 1, 0), j + cb0)),
                  pl.BlockSpec((SUB, cw), lambda j, i: (0, j))],
        out_specs=pl.BlockSpec((rt, cw), lambda j, i: (i, j)),
        out_shape=jax.ShapeDtypeStruct((tp, ncb * LANE), f32), compiler_params=_params(2),
    )(proj, proj, w8)


def conv_bwd_act(proj, w8, dout, col_blk, ncb, l2, name):
    tp = proj.shape[0]
    rt = _row_tile(tp)
    hb = rt // SUB
    cw = CONV_W
    cb0 = col_blk * LANE // cw

    def body(x_ref, h_ref, w_ref, d_ref, dy_ref, dw_ref):
        i = pl.program_id(1)
        first = (i > 0).astype(f32)
        rows = lax.broadcasted_iota(jnp.int32, (SUB, LANE), 0)
        for s in range(cw // LANE):
            sl = slice(s * LANE, (s + 1) * LANE)
            ext = jnp.concatenate([h_ref[:, sl] * first, x_ref[:, sl]], axis=0)
            taps = _conv_taps(ext, rt)
            y, sg, a, n = _conv_act(taps, w_ref[:, sl], l2)
            da = d_ref[:, sl]
            if l2:
                out = a * n
                da = n * (da - out * jnp.sum(da * out, axis=-1, keepdims=True))
            dy = da * sg * (1.0 + y * (1.0 - sg))
            dy_ref[:, sl] = dy
            part = jnp.zeros((SUB, LANE), f32)
            for k in range(DN_CONV):
                part = part + jnp.where(rows == k, jnp.sum(taps[k] * dy, axis=0, keepdims=True), 0.0)

            @pl.when(i == 0)
            def _(sl=sl, part=part):
                dw_ref[:, sl] = part

            @pl.when(i > 0)
            def _(sl=sl, part=part):
                dw_ref[:, sl] += part

    return pl.pallas_call(
        body, name=name, grid=(ncb * LANE // cw, tp // rt),
        in_specs=[pl.BlockSpec((rt, cw), lambda j, i: (i, j + cb0)),
                  pl.BlockSpec((SUB, cw), lambda j, i: (jnp.maximum(i * hb - 1, 0), j + cb0)),
                  pl.BlockSpec((SUB, cw), lambda j, i: (0, j)),
                  pl.BlockSpec((rt, cw), lambda j, i: (i, j))],
        out_specs=[pl.BlockSpec((rt, cw), lambda j, i: (i, j)), pl.BlockSpec((SUB, cw), lambda j, i: (0, j))],
        out_shape=[jax.ShapeDtypeStruct((tp, ncb * LANE), f32), jax.ShapeDtypeStruct((SUB, ncb * LANE), f32)],
        compiler_params=_params(2),
    )(proj, proj, w8, dout)


def conv_bwd_in(dy, w8, name):
    tp, cols = dy.shape
    rt = _row_tile(tp)
    hb = rt // SUB
    nr = tp // rt
    last8 = tp // SUB - 1
    cw = CONV_W

    def body(d_ref, h_ref, w_ref, o_ref):
        i = pl.program_id(1)
        last = (i < nr - 1).astype(f32)
        for c0 in range(cw // LANE):
            sl = slice(c0 * LANE, (c0 + 1) * LANE)
            ext = jnp.concatenate([d_ref[:, sl], h_ref[:, sl] * last], axis=0)
            w = w_ref[:, sl]
            acc = None
            for k in range(DN_CONV):
                s = DN_CONV - 1 - k
                sh = (pltpu.roll(ext, rt + SUB - s, axis=0) if s else ext)[0:rt]
                term = sh * w[k:k + 1]
                acc = term if acc is None else acc + term
            o_ref[:, sl] = acc.astype(bf16)

    return pl.pallas_call(
        body, name=name, grid=(cols // cw, nr),
        in_specs=[pl.BlockSpec((rt, cw), lambda j, i: (i, j)),
                  pl.BlockSpec((SUB, cw), lambda j, i: (jnp.minimum((i + 1) * hb, last8), j)),
                  pl.BlockSpec((SUB, cw), lambda j, i: (0, j))],
        out_specs=pl.BlockSpec((rt, cw), lambda j, i: (i, j)),
        out_shape=jax.ShapeDtypeStruct((tp, cols), bf16), compiler_params=_params(2),
    )(dy, dy, w8)


def _ab_common(p, al, dtb, r0):
    rows = r0 + lax.broadcasted_iota(jnp.int32, p.shape, 0)
    mask = (rows >= P0).astype(f32)
    xx = p + dtb
    sp = jnp.maximum(xx, 0.0) + _log1p_small(jnp.exp(-jnp.abs(xx)))
    ea = jnp.exp(al)
    g = -ea * sp * mask
    beta = _sigmoid(p) * mask
    return g, beta, _sigmoid(xx), ea, mask


def _chunk_tri(rt, later):
    r = lax.broadcasted_iota(jnp.int32, (rt, rt), 0)
    c = lax.broadcasted_iota(jnp.int32, (rt, rt), 1)
    shift = CHUNK.bit_length() - 1
    same = jnp.right_shift(r, shift) == jnp.right_shift(c, shift)
    return jnp.logical_and(same, c >= r if later else c <= r).astype(f32)


def ab_fwd(pab, al, dtb):
    tp = pab.shape[0]
    rt = _row_tile(tp)
    assert rt % CHUNK == 0

    def body(p_ref, al_ref, dt_ref, g_ref, b_ref):
        i = pl.program_id(0)
        g, beta, _, _, _ = _ab_common(p_ref[...], al_ref[...], dt_ref[...], i * rt)
        gam = _hdot(_chunk_tri(rt, False), g)
        for h in range(HEADS):
            g_ref[h] = jnp.broadcast_to(gam[:, h:h + 1], (rt, LANE))
            b_ref[h] = jnp.broadcast_to(beta[:, HEADS + h:HEADS + h + 1], (rt, LANE))

    vec = pl.BlockSpec((1, LANE), lambda i: (0, 0))
    out = pl.BlockSpec((HEADS, rt, LANE), lambda i: (0, i, 0))
    return pl.pallas_call(
        body, name="ab_fwd", grid=(tp // rt,), in_specs=[pl.BlockSpec((rt, LANE), lambda i: (i, 0)), vec, vec],
        out_specs=[out, out], out_shape=[jax.ShapeDtypeStruct((HEADS, tp, LANE), f32)] * 2, compiler_params=_params(1),
    )(pab, al, dtb)


def ab_bwd(pab, al, dtb, dg, db):
    tp = pab.shape[0]
    rt = _row_tile(tp)

    def body(p_ref, al_ref, dt_ref, dg_ref, db_ref, dp_ref, dal_ref, ddt_ref):
        i = pl.program_id(0)
        g, beta, sx, ea, mask = _ab_common(p_ref[...], al_ref[...], dt_ref[...], i * rt)
        lanes = lax.broadcasted_iota(jnp.int32, (rt, LANE), 1)
        dgl = jnp.zeros((rt, LANE), f32)
        dbl = jnp.zeros((rt, LANE), f32)
        for h in range(HEADS):
            dgl = dgl + jnp.where(lanes == h, dg_ref[h], 0.0)
            dbl = dbl + jnp.where(lanes == HEADS + h, db_ref[h], 0.0)
        dgl = _hdot(_chunk_tri(rt, True), dgl)
        dxx = dgl * (-ea) * sx * mask
        dp_ref[...] = (dxx + dbl * beta * (1.0 - beta)).astype(bf16)
        pal = jnp.sum(dgl * g, axis=0, keepdims=True)
        pdt = jnp.sum(dxx, axis=0, keepdims=True)

        @pl.when(i == 0)
        def _():
            dal_ref[...] = pal
            ddt_ref[...] = pdt

        @pl.when(i > 0)
        def _():
            dal_ref[...] += pal
            ddt_ref[...] += pdt

    vec = pl.BlockSpec((1, LANE), lambda i: (0, 0))
    row = pl.BlockSpec((rt, LANE), lambda i: (i, 0))
    big = pl.BlockSpec((HEADS, rt, LANE), lambda i: (0, i, 0))
    return pl.pallas_call(
        body, name="ab_bwd", grid=(tp // rt,), in_specs=[row, vec, vec, big, big], out_specs=[row, vec, vec],
        out_shape=[jax.ShapeDtypeStruct((tp, LANE), bf16), jax.ShapeDtypeStruct((1, LANE), f32),
                   jax.ShapeDtypeStruct((1, LANE), f32)],
        compiler_params=_params(1),
    )(pab, al, dtb, dg, db)


class _Chunk:
    pass


def _gdn_chunk(q, k, v, gcol, bcol, grow8):
    C = CHUNK
    R = range(len(q))
    X = _Chunk()
    ri = lax.broadcasted_iota(jnp.int32, (C, C), 0)
    ci = lax.broadcasted_iota(jnp.int32, (C, C), 1)
    eye = (ri == ci).astype(f32)
    gam = list(gcol)
    gam_row = [grow8[h][0:1, 0:C] for h in R]
    X.ri, X.ci = ri, ci
    X.Dm = [jnp.where(ri >= ci, jnp.exp(jnp.minimum(gam[h][:, 0:C] - gam_row[h], 0.0)), 0.0) for h in R]
    X.eg = [jnp.exp(gam[h]) for h in R]
    gl = [gam[h][C - 1:C, :] for h in R]
    X.egl = [jnp.exp(gl[h]) for h in R]
    X.kdec = [jnp.exp(gl[h] - gam[h]) for h in R]
    X.qs = [q[h] * (DN_DK ** -0.5) for h in R]
    X.kb = [k[h] * bcol[h] for h in R]
    kk = [_dot_nt(X.kb[h], k[h]) for h in R]
    qk = [_dot_nt(X.qs[h], k[h]) for h in R]
    X.A = [jnp.where(ri > ci, kk[h] * X.Dm[h], 0.0) for h in R]
    T = [eye - X.A[h] for h in R]
    P = list(X.A)
    for _ in range(5):
        P = [_hdot(P[h], P[h]) for h in R]
        T = [T[h] + _hdot(T[h], P[h]) for h in R]
    X.T = T
    X.b2 = [jnp.concatenate([bcol[h], bcol[h]], axis=-1) for h in R]
    X.u = [_hdot(T[h], v[h] * X.b2[h]) for h in R]
    X.w = [_hdot(T[h], X.kb[h] * X.eg[h]) for h in R]
    X.attn = [qk[h] * X.Dm[h] for h in R]
    X.qg = [X.qs[h] * X.eg[h] for h in R]
    X.kg = [k[h] * X.kdec[h] for h in R]
    return X


def gdn_fwd(q, k, v, gc, bc, grow):
    tp = q.shape[0]
    nc = tp // CHUNK
    hb = GDN_HEADS_PER_STEP

    def body(q_ref, k_ref, v_ref, gc_ref, bc_ref, gr_ref, o_ref, ss_ref, S_ref):
        c = pl.program_id(1)

        @pl.when(c == 0)
        def _():
            S_ref[...] = jnp.zeros_like(S_ref)

        R = range(hb)
        qc = [slice(h * DN_DK, (h + 1) * DN_DK) for h in R]
        vc = [slice(h * DN_DV, (h + 1) * DN_DV) for h in R]
        X = _gdn_chunk([q_ref[:, qc[h]] for h in R], [k_ref[:, qc[h]] for h in R], [v_ref[:, vc[h]] for h in R],
                       [gc_ref[h] for h in R], [bc_ref[h] for h in R], [gr_ref[h] for h in R])
        S = [S_ref[h] for h in R]
        for h in R:
            ss_ref[h, 0] = S[h]
        wS = [_dot(X.w[h], S[h]) for h in R]
        qS = [_dot(X.qg[h], S[h]) for h in R]
        vn = [X.u[h] - wS[h] for h in R]
        av = [_dot(X.attn[h], vn[h]) for h in R]
        kv = [_dot_tn(X.kg[h], vn[h]) for h in R]
        for h in R:
            o_ref[:, vc[h]] = qS[h] + av[h]
            S_ref[h] = S[h] * X.egl[h][:, 0:1] + kv[h]

    qk = pl.BlockSpec((CHUNK, hb * DN_DK), lambda g, c: (c, g))
    vv = pl.BlockSpec((CHUNK, hb * DN_DV), lambda g, c: (c, g))
    col = pl.BlockSpec((hb, CHUNK, LANE), lambda g, c: (g, c, 0))
    row = pl.BlockSpec((hb, SUB, LANE), lambda g, c: (g, c, 0))
    return pl.pallas_call(
        body, name="gdn_fwd", grid=(HEADS // hb, nc), in_specs=[qk, qk, vv, col, col, row],
        out_specs=[vv, pl.BlockSpec((hb, 1, DN_DK, DN_DV), lambda g, c: (g, c, 0, 0))],
        out_shape=[jax.ShapeDtypeStruct((tp, DN_V), f32), jax.ShapeDtypeStruct((HEADS, nc, DN_DK, DN_DV), f32)],
        scratch_shapes=[pltpu.VMEM((hb, DN_DK, DN_DV), f32)], compiler_params=_params(2),
    )(q, k, v, gc, bc, grow)


def gdn_bwd(q, k, v, gc, bc, grow, states, do, rider=None):
    tp = q.shape[0]
    nc = tp // CHUNK
    C = CHUNK
    hb = GDN_HEADS_PER_STEP
    grid = (HEADS // hb, nc)
    split, ride_first, ride_last = _ride(rider, 8, 5, grid)

    def body(*refs):
        ((q_ref, k_ref, v_ref, gc_ref, bc_ref, gr_ref, ss_ref, do_ref), (dq_ref, dk_ref, dv_ref, dg_ref, db_ref),
         (rin, rout, rest)) = split(refs)
        dS_ref, ride = rest[0], (rin, rout, rest[1:])
        ride_first(ride)
        c = pl.program_id(1)

        @pl.when(c == 0)
        def _():
            dS_ref[...] = jnp.zeros_like(dS_ref)

        R = range(hb)
        qc = [slice(h * DN_DK, (h + 1) * DN_DK) for h in R]
        vc = [slice(h * DN_DV, (h + 1) * DN_DV) for h in R]
        k_ = [k_ref[:, qc[h]] for h in R]
        v_ = [v_ref[:, vc[h]] for h in R]
        bcol = [bc_ref[h] for h in R]
        X = _gdn_chunk([q_ref[:, qc[h]] for h in R], k_, v_, [gc_ref[h] for h in R], bcol, [gr_ref[h] for h in R])
        ri, ci = X.ri, X.ci
        S = [ss_ref[h, 0] for h in R]
        do_ = [do_ref[:, vc[h]] for h in R]
        dSn = [dS_ref[h] for h in R]
        wS = [_dot(X.w[h], S[h]) for h in R]
        ado = [_dot_tn(X.attn[h], do_[h]) for h in R]
        kdS = [_dot(X.kg[h], dSn[h]) for h in R]
        d_qg = [_dot_nt(do_[h], S[h]) for h in R]
        qdo = [_dot_tn(X.qg[h], do_[h]) for h in R]
        vn = [X.u[h] - wS[h] for h in R]
        d_vn = [ado[h] + kdS[h] for h in R]
        dovn = [_dot_nt(do_[h], vn[h]) for h in R]
        d_kg = [_dot_nt(vn[h], dSn[h]) for h in R]
        wdv = [_dot_tn(X.w[h], d_vn[h]) for h in R]
        dw = [-_dot_nt(d_vn[h], S[h]) for h in R]
        for h in R:
            dS_ref[h] = qdo[h] + X.egl[h][:, 0:1] * dSn[h] - wdv[h]
        dattn = [jnp.where(ri >= ci, dovn[h], 0.0) for h in R]
        dRu = [_hdot_tn(X.T[h], d_vn[h]) for h in R]
        dRw = [_hdot_tn(X.T[h], dw[h]) for h in R]
        dAu = [_hdot_nt(dRu[h], X.u[h]) for h in R]
        dAw = [_hdot_nt(dRw[h], X.w[h]) for h in R]
        dA = [jnp.where(ri > ci, -(dAu[h] + dAw[h]), 0.0) for h in R]
        dKK = [dA[h] * X.Dm[h] for h in R]
        dQK = [dattn[h] * X.Dm[h] for h in R]
        E = [dA[h] * X.A[h] + dattn[h] * X.attn[h] for h in R]
        dkb = [_dot(dKK[h], k_[h]) + dRw[h] * X.eg[h] for h in R]
        dk1 = [_dot_tn(dKK[h], X.kb[h]) for h in R]
        dqs = [_dot(dQK[h], k_[h]) + d_qg[h] * X.eg[h] for h in R]
        dk2 = [_dot_tn(dQK[h], X.qs[h]) for h in R]
        ones = jnp.ones((C, LANE), f32)
        colE = [_hdot_tn(E[h], ones) for h in R]
        rows = lax.broadcasted_iota(jnp.int32, (C, LANE), 0)
        dgam = []
        for h in R:
            t = d_kg[h] * X.kg[h]
            dgl = _allsum(t) + X.egl[h][:, 0:1] * _allsum(S[h] * dSn[h])
            g = (_rowsum(E[h]) - colE[h] + _rowsum(dRw[h] * (X.kb[h] * X.eg[h])) + _rowsum(d_qg[h] * X.qg[h])
                 - _rowsum(t))
            dgam.append(g + jnp.where(rows == C - 1, dgl, 0.0))
        for h in R:
            dv_ref[:, vc[h]] = dRu[h] * X.b2[h]
            dbeta = _rowsum(dRu[h] * v_[h]) + _rowsum(dkb[h] * k_[h])
            dq_ref[:, qc[h]] = dqs[h] * (DN_DK ** -0.5)
            dk_ref[:, qc[h]] = dk1[h] + dk2[h] + dkb[h] * bcol[h] + d_kg[h] * X.kdec[h]
            dg_ref[h] = dgam[h]
            db_ref[h] = jnp.broadcast_to(dbeta, (C, LANE))
        ride_last(ride)

    rc = lambda c: nc - 1 - c
    qk = pl.BlockSpec((CHUNK, hb * DN_DK), lambda g, c: (rc(c), g))
    vv = pl.BlockSpec((CHUNK, hb * DN_DV), lambda g, c: (rc(c), g))
    col = pl.BlockSpec((hb, CHUNK, LANE), lambda g, c: (g, rc(c), 0))
    row = pl.BlockSpec((hb, SUB, LANE), lambda g, c: (g, rc(c), 0))
    st = pl.BlockSpec((hb, 1, DN_DK, DN_DV), lambda g, c: (g, rc(c), 0, 0))
    r_ins = rider.ins if rider else []
    r_outs = rider.out_shapes if rider else []
    res = pl.pallas_call(
        body, name="gdn_bwd", grid=grid, in_specs=[qk, qk, vv, col, col, row, st, vv] + [_ANY] * len(r_ins),
        out_specs=[qk, qk, vv, col, col] + [_ANY] * len(r_outs),
        out_shape=[jax.ShapeDtypeStruct((tp, DN_QK), f32), jax.ShapeDtypeStruct((tp, DN_QK), f32),
                   jax.ShapeDtypeStruct((tp, DN_V), f32), jax.ShapeDtypeStruct((HEADS, tp, LANE), f32),
                   jax.ShapeDtypeStruct((HEADS, tp, LANE), f32)] + list(r_outs),
        scratch_shapes=[pltpu.VMEM((hb, DN_DK, DN_DV), f32)] + (rider.scratch if rider else []),
        compiler_params=_params(2),
    )(q, k, v, gc, bc, grow, states, do, *r_ins)
    return res[:5], res[5:]


def _cumsum_after(x, nb, us):
    B, n = SB_BLOCK, x.shape[0]
    hi = x.astype(bf16)
    lo = (x - hi.astype(f32)).astype(bf16)
    rows = [p[:, b * B:(b + 1) * B] for p in (hi, lo) for b in range(nb)]
    r = jnp.dot(jnp.concatenate(rows, axis=0), us, preferred_element_type=f32)
    out = [r[b * n:(b + 1) * n] + r[(nb + b) * n:(nb + b + 1) * n] for b in range(nb)]
    return out[0] if nb == 1 else jnp.concatenate(out, axis=1)


def _later_blocks(x, nb, carry):
    B = SB_BLOCK
    tot = [_rowsum(x[:, b * B:(b + 1) * B]) for b in range(nb)]
    offs = [None] * nb
    run = carry
    for b in range(nb - 1, -1, -1):
        offs[b] = jnp.broadcast_to(run, (x.shape[0], B))
        run = run + tot[b]
    return (offs[0] if nb == 1 else jnp.concatenate(offs, axis=1)), run


def _sb_group(i, t):
    top = (i + 1) * (SB_QB // SB_BLOCK) - 1 - SB_GROUP * t
    jlo = jnp.maximum(top - SB_GROUP + 1, 0)
    rows = pl.ds(pl.multiple_of(jlo * SB_BLOCK, SB_BLOCK), SB_GROUP * SB_BLOCK)
    return jlo, rows, (top + 1) * SB_BLOCK


def _sb_weights(q, kcat, i, jlo, kend, cs, us, masked):
    B, nb = SB_BLOCK, SB_GROUP
    R = range(len(q))
    z = [_dot_nt(q[h], kcat[h]) * (SB_DH ** -0.5) for h in R]
    e = [jnp.exp(-jnp.abs(z[h])) for h in R]
    l1p = [jnp.log(1.0 + e[h]) for h in R]
    lsp = [jnp.minimum(z[h], 0.0) - l1p[h] for h in R]
    lk = [lsp[h] - z[h] for h in R]
    vis = None
    if masked:
        qpos = i * SB_QB + lax.broadcasted_iota(jnp.int32, (SB_QB, nb * B), 0)
        kpos = jlo * B + lax.broadcasted_iota(jnp.int32, (SB_QB, nb * B), 1)
        vis = jnp.logical_and(kpos < jnp.minimum(qpos, kend), kpos >= P0)
        lk = [jnp.where(vis, lk[h], 0.0) for h in R]
    later = [_later_blocks(lk[h], nb, cs[h]) for h in R]
    cum = [_cumsum_after(lk[h], nb, us) for h in R]
    w = [jnp.exp(lsp[h] + cum[h] + later[h][0]) for h in R]
    if masked:
        w = [jnp.where(vis, w[h], 0.0) for h in R]
    return lsp, vis, w, [later[h][1] for h in R]


def _sb_loop(i, step, carry):
    trips = ((i + 1) * (SB_QB // SB_BLOCK) - 1 + SB_GROUP) // SB_GROUP
    carry = step(True)(0, carry)
    carry = lax.fori_loop(1, trips - 1, step(False), carry)
    return lax.fori_loop(jnp.maximum(trips - 1, 1), trips, step(True), carry)


def _ride(rider, n_in, n_out, grid):
    n_rin = len(rider.ins) if rider else 0
    n_rout = len(rider.out_shapes) if rider else 0

    def split(refs):
        ins, rin = refs[:n_in], refs[n_in:n_in + n_rin]
        outs = refs[n_in + n_rin:n_in + n_rin + n_out]
        rout = refs[n_in + n_rin + n_out:n_in + n_rin + n_out + n_rout]
        return ins, outs, (rin, rout, refs[n_in + n_rin + n_out + n_rout:])

    def at(step, fn, r):
        if rider is None:
            return
        cond = None
        for a, g in enumerate(grid):
            c = pl.program_id(a) == (g - 1 if step == "last" else 0)
            cond = c if cond is None else jnp.logical_and(cond, c)

        @pl.when(cond)
        def _():
            fn(*r)

    first = lambda r: at("first", rider.start if rider else None, r)
    last = lambda r: at("last", rider.finish if rider else None, r)
    return split, first, last


def sb_fwd(qs, ks, vs, rider=None):
    tp = qs.shape[0]
    nq = tp // SB_QB
    B, G, hb, QB = SB_BLOCK, SB_GROUP, SB_FWD_HEADS_PER_STEP, SB_QB
    assert tp >= G * B and tp % QB == 0 and QB % B == 0 and G * B >= QB
    grid = (HEADS // hb, nq)
    split, ride_first, ride_last = _ride(rider, 3, 2, grid)

    def body(*refs):
        (q_ref, k_ref, v_ref), (o_ref, ob_ref), ride = split(refs)
        ride_first(ride)
        i = pl.program_id(1)
        R = range(hb)
        hs = [slice(h * SB_DH, (h + 1) * SB_DH) for h in R]
        q = [q_ref[:, hs[h]] for h in R]
        us = (lax.broadcasted_iota(jnp.int32, (B, B), 0) > lax.broadcasted_iota(jnp.int32, (B, B), 1)).astype(bf16)

        def make_step(masked):
            def step(t, carry):
                acc, cs = carry
                jlo, rows, kend = _sb_group(i, t)
                _, _, w, cs = _sb_weights(q, [k_ref[rows, hs[h]] for h in R], i, jlo, kend, cs, us, masked)
                pv = [_dot(w[h], v_ref[rows, hs[h]]) for h in R]
                return tuple(acc[h] + pv[h] for h in R), tuple(cs)
            return step

        carry = (tuple(jnp.zeros((QB, SB_DH), f32) for _ in R), tuple(jnp.zeros((QB, 1), f32) for _ in R))
        acc, _ = _sb_loop(i, make_step, carry)
        for h in R:
            o_ref[:, hs[h]] = acc[h]
            ob_ref[:, hs[h]] = acc[h].astype(bf16)
        ride_last(ride)

    blk = pl.BlockSpec((QB, hb * SB_DH), lambda g, i: (i, g))
    full = pl.BlockSpec((tp, hb * SB_DH), lambda g, i: (0, g))
    r_ins = rider.ins if rider else []
    r_outs = rider.out_shapes if rider else []
    res = pl.pallas_call(
        body, name="sb_fwd", grid=grid, in_specs=[blk, full, full] + [_ANY] * len(r_ins),
        out_specs=[blk, blk] + [_ANY] * len(r_outs),
        out_shape=[jax.ShapeDtypeStruct((tp, SB_W), f32), jax.ShapeDtypeStruct((tp, SB_W), bf16)] + list(r_outs),
        scratch_shapes=rider.scratch if rider else [], compiler_params=_params(2),
    )(qs, ks, vs, *r_ins)
    return res[0], res[1], res[2:]


def sb_bwd(qs, ks, vs, o, do, rider=None):
    tp = qs.shape[0]
    nq = tp // SB_QB
    B, G, hb, QB = SB_BLOCK, SB_GROUP, SB_HEADS_PER_STEP, SB_QB
    assert tp >= G * B and tp % QB == 0 and QB % B == 0 and G * B >= QB
    grid = (HEADS // hb, nq)
    split, ride_first, ride_last = _ride(rider, 5, 3, grid)

    def body(*refs):
        (q_ref, k_ref, v_ref, o_ref, do_ref), (dq_ref, dk_ref, dv_ref), ride = split(refs)
        ride_first(ride)
        i = pl.program_id(1)

        @pl.when(i == 0)
        def _():
            dk_ref[...] = jnp.zeros_like(dk_ref)
            dv_ref[...] = jnp.zeros_like(dv_ref)

        R = range(hb)
        hs = [slice(h * SB_DH, (h + 1) * SB_DH) for h in R]
        q = [q_ref[:, hs[h]] for h in R]
        dob = [do_ref[:, hs[h]].astype(bf16) for h in R]
        et = [_rowsum(dob[h].astype(f32) * o_ref[:, hs[h]]) for h in R]
        us = (lax.broadcasted_iota(jnp.int32, (B, B), 0) > lax.broadcasted_iota(jnp.int32, (B, B), 1)).astype(bf16)

        def make_step(masked):
            def step(t, carry):
                dq, cs, ce = carry
                jlo, rows, kend = _sb_group(i, t)
                kcat = [k_ref[rows, hs[h]] for h in R]
                dwv = [_dot_nt(dob[h], v_ref[rows, hs[h]]) for h in R]
                lsp, vis, w, cs = _sb_weights(q, kcat, i, jlo, kend, cs, us, masked)
                wb = [w[h].astype(bf16) for h in R]
                ee = [dwv[h] * wb[h].astype(f32) for h in R]
                later = [_later_blocks(ee[h], G, ce[h]) for h in R]
                cum = [_cumsum_after(ee[h], G, us) for h in R]
                dz = []
                for h in R:
                    d = ee[h] - jnp.exp(lsp[h]) * (et[h] - (cum[h] + later[h][0]))
                    if masked:
                        d = jnp.where(vis, d, 0.0)
                    dz.append((d * (SB_DH ** -0.5)).astype(bf16))
                dkj = [_dot_tn(dz[h], q[h]) for h in R]
                dvj = [_dot_tn(wb[h], dob[h]) for h in R]
                dqj = [_dot(dz[h], kcat[h]) for h in R]
                for h in R:
                    dk_ref[rows, hs[h]] += dkj[h]
                    dv_ref[rows, hs[h]] += dvj[h]
                return tuple(dq[h] + dqj[h] for h in R), tuple(cs), tuple(later[h][1] for h in R)
            return step

        z0 = tuple(jnp.zeros((QB, 1), f32) for _ in R)
        dq, _, _ = _sb_loop(i, make_step, (tuple(jnp.zeros((QB, SB_DH), f32) for _ in R), z0, z0))
        for h in R:
            dq_ref[:, hs[h]] = dq[h]
        ride_last(ride)

    blk = pl.BlockSpec((QB, hb * SB_DH), lambda g, i: (i, g))
    full = pl.BlockSpec((tp, hb * SB_DH), lambda g, i: (0, g))
    r_ins = rider.ins if rider else []
    r_outs = rider.out_shapes if rider else []
    res = pl.pallas_call(
        body, name="sb_bwd", grid=grid, in_specs=[blk, full, full, blk, blk] + [_ANY] * len(r_ins),
        out_specs=[blk, full, full] + [_ANY] * len(r_outs),
        out_shape=[jax.ShapeDtypeStruct((tp, SB_W), f32)] * 3 + list(r_outs),
        scratch_shapes=rider.scratch if rider else [], compiler_params=_params(2),
    )(qs, ks, vs, o, do, *r_ins)
    return res[:3], res[3:]


def adamw(w, g, m, v, name):
    r, c = w.shape
    rt = _tile(r, 128, SUB) if r % SUB == 0 else r
    blk = pl.BlockSpec((rt, c), lambda i: (i, 0))
    c1 =1.0 - ADAM_B1 ** ADAM_STEP
    c2 = 1.0 - ADAM_B2 ** ADAM_STEP

    def body(w_ref, g_ref, m_ref, v_ref, d_ref, mo_ref, vo_ref):
        g_ = g_ref[...]
        m_ = ADAM_B1 * m_ref[...] + (1.0 - ADAM_B1) * g_
        v_ = ADAM_B2 * v_ref[...] + (1.0 - ADAM_B2) * (g_ * g_)
        mo_ref[...] = m_
        vo_ref[...] = v_
        d_ref[...] = -ADAM_LR * ((m_ / c1) / (jnp.sqrt(v_ / c2) + ADAM_EPS) + ADAM_WD * w_ref[...])

    return pl.pallas_call(
        body, name=name, grid=(r // rt,), in_specs=[blk] * 4, out_specs=[blk] * 3,
        out_shape=[jax.ShapeDtypeStruct((r, c), f32)] * 3, compiler_params=_params(1),
    )(w, g, m, v)


def sum_slots(x, name):
    n, r, c = x.shape
    rt = _tile(r, 128, SUB) if r % SUB == 0 else r
    blk = pl.BlockSpec((n, rt, c), lambda i: (0, i, 0))

    def body(x_ref, o_ref):
        acc = x_ref[0].astype(f32)
        for s in range(1, n):
            acc = acc + x_ref[s].astype(f32)
        o_ref[...] = acc

    return pl.pallas_call(
        body, name=name, grid=(r // rt,), in_specs=[blk], out_specs=pl.BlockSpec((rt, c), lambda i: (i, 0)),
        out_shape=jax.ShapeDtypeStruct((r, c), f32), compiler_params=_params(1),
    )(x)


def add2(a, b, name, out_dtype=f32):
    n, r, c = a.shape
    rt = _tile(r, 64, SUB) if r % SUB == 0 else r
    blk = pl.BlockSpec((n, rt, c), lambda i: (0, i, 0))

    def body(a_ref, b_ref, o_ref):
        o_ref[...] = (a_ref[...] + b_ref[...]).astype(out_dtype)

    return pl.pallas_call(
        body, name=name, grid=(r // rt,), in_specs=[blk, blk], out_specs=blk,
        out_shape=jax.ShapeDtypeStruct((n, r, c), out_dtype), compiler_params=_params(1),
    )(a, b)


_ANY = pl.BlockSpec(memory_space=pl.ANY)
_MESH = pl.DeviceIdType.MESH


def _coords():
    return lax.axis_index("x"), lax.axis_index("y"), lax.axis_index("c")


def _chip_peer(x, y, r):
    return x ^ (r >> 1), y ^ (r & 1)


class _Exchange:
    def __init__(self, ins, out_shapes, scratch, start, finish):
        self.ins, self.out_shapes, self.scratch, self.start, self.finish = ins, out_shapes, scratch, start, finish

    def split(self, refs):
        n, m = len(self.ins), len(self.out_shapes)
        return refs[:n], refs[n:n + m], refs[n + m:]


def run_exchange(ex, name):
    def body(*refs):
        ins, outs, sems = ex.split(refs)
        ex.start(ins, outs, sems)
        ex.finish(ins, outs, sems)

    return pl.pallas_call(body, name=name, in_specs=[_ANY] * len(ex.ins), out_specs=[_ANY] * len(ex.out_shapes),
                          out_shape=ex.out_shapes, scratch_shapes=ex.scratch)(*ex.ins)


def gather_chips(big, small):
    nb, n = len(big), len(big) + len(small)
    shards = list(big) + list(small)
    kb = nb * (N_CHIPS - 1)
    k = n * (N_CHIPS - 1)

    def copies(src, dst, sems):
        send, recv, fsend, frecv = sems
        x, y, c = _coords()
        sib = (x, y, 1 - c)
        peers = [_chip_peer(x, y, r) for r in range(1, N_CHIPS)]

        def direct(t, j, slot):
            s = t * (N_CHIPS - 1) + j
            if t < nb:
                return pltpu.make_async_remote_copy(src[t].at[c], dst[t].at[slot, c], send.at[s], recv.at[s],
                                                    device_id=(*peers[j], c), device_id_type=_MESH)
            return pltpu.make_async_remote_copy(src[t], dst[t].at[slot], send.at[s], recv.at[s],
                                                device_id=(*peers[j], c), device_id_type=_MESH)

        def passed(t, j, half):
            s = t * (N_CHIPS - 1) + j
            px, py = peers[j]
            part = dst[t].at[2 * px + py, half]
            return pltpu.make_async_remote_copy(part, part, fsend.at[s], frecv.at[s], device_id=sib, device_id_type=_MESH)

        return direct, passed, peers, 2 * x + y, c

    def start(src, dst, sems):
        direct, _, _, me, _ = copies(src, dst, sems)
        for t in range(n):
            for j in range(N_CHIPS - 1):
                direct(t, j, me).start()

    def finish(src, dst, sems):
        direct, passed, peers, me, c = copies(src, dst, sems)
        fwd = []
        for t in range(nb):
            for j in range(N_CHIPS - 1):
                px, py = peers[j]
                direct(t, j, 2 * px + py).wait_recv()
                fwd.append(passed(t, j, c))
                fwd[-1].start()
        for t in range(nb, n):
            for j in range(N_CHIPS - 1):
                px, py = peers[j]
                direct(t, j, 2 * px + py).wait_recv()
        for t in range(nb):
            for j in range(N_CHIPS - 1):
                passed(t, j, 1 - c).wait_recv()
        for t in range(n):
            for j in range(N_CHIPS - 1):
                direct(t, j, me).wait_send()
        for cp in fwd:
            cp.wait_send()

    return _Exchange(shards, [jax.ShapeDtypeStruct((N_CHIPS,) + s.shape, s.dtype) for s in shards],
                     [pltpu.SemaphoreType.DMA((k,)), pltpu.SemaphoreType.DMA((k,)),
                      pltpu.SemaphoreType.DMA((max(kb, 1),)), pltpu.SemaphoreType.DMA((max(kb, 1),))], start, finish)


def sibling_swap(grads):
    n = len(grads)
    k = n * N_CHIPS

    def copies(src, dst, sems):
        send, recv = sems
        x, y, c = _coords()
        return [pltpu.make_async_remote_copy(src[t].at[o, 1 - c], dst[t].at[o], send.at[t * N_CHIPS + o],
                                             recv.at[t * N_CHIPS + o], device_id=(x, y, 1 - c), device_id_type=_MESH)
                for t in range(n) for o in range(N_CHIPS)]

    def start(src, dst, sems):
        for cp in copies(src, dst, sems):
            cp.start()

    def finish(src, dst, sems):
        cps = copies(src, dst, sems)
        for cp in cps:
            cp.wait_recv()
        for cp in cps:
            cp.wait_send()

    return _Exchange(list(grads), [jax.ShapeDtypeStruct((N_CHIPS,) + g.shape[2:], g.dtype) for g in grads],
                     [pltpu.SemaphoreType.DMA((k,)), pltpu.SemaphoreType.DMA((k,))], start, finish)


def scatter_chips(parts):
    n = len(parts)
    k = n * (N_CHIPS - 1)

    def copy(src, dst, sems, t, r, landing):
        send, recv = sems
        x, y, c = _coords()
        me = 2 * x + y
        px, py = _chip_peer(x, y, r)
        peer = 2 * px + py
        s = t * (N_CHIPS - 1) + r - 1
        return pltpu.make_async_remote_copy(src[t].at[me if landing else peer], dst[t].at[peer if landing else me],
                                            send.at[s], recv.at[s], device_id=(px, py, c), device_id_type=_MESH)

    def start(src, dst, sems):
        for t in range(n):
            for r in range(1, N_CHIPS):
                copy(src, dst, sems, t, r, False).start()

    def finish(src, dst, sems):
        for t in range(n):
            for r in range(1, N_CHIPS):
                copy(src, dst, sems, t, r, True).wait_recv()
        for t in range(n):
            for r in range(1, N_CHIPS):
                copy(src, dst, sems, t, r, False).wait_send()

    return _Exchange(list(parts), [jax.ShapeDtypeStruct(p.shape, p.dtype) for p in parts],
                     [pltpu.SemaphoreType.DMA((k,)), pltpu.SemaphoreType.DMA((k,))], start, finish)


def sibling_send(halves, name):
    n = len(halves)

    def body(*refs):
        src, dst = refs[:n], refs[n:2 * n]
        send, recv = refs[2 * n:]
        x, y, c = _coords()
        cps = [pltpu.make_async_remote_copy(src[t], dst[t], send.at[t], recv.at[t],
                                            device_id=(x, y, 1 - c), device_id_type=_MESH) for t in range(n)]
        for cp in cps:
            cp.start()
        for cp in cps:
            cp.wait_recv()
        for cp in cps:
            cp.wait_send()

    return pl.pallas_call(
        body, name=name, in_specs=[_ANY] * n, out_specs=[_ANY] * n,
        out_shape=[jax.ShapeDtypeStruct(h.shape, h.dtype) for h in halves],
        scratch_shapes=[pltpu.SemaphoreType.DMA((n,)), pltpu.SemaphoreType.DMA((n,))],
    )(*halves)


def gather_all(block, name):
    def body(src, dst, send, recv, loc):
        x, y, c = _coords()
        me = 4 * x + 2 * y + c
        mine = pltpu.make_async_copy(src, dst.at[me], loc)
        mine.start()
        outs = []
        for r in range(1, N_DEV):
            peer = (x ^ (r >> 2), y ^ ((r >> 1) & 1), c ^ (r & 1))
            outs.append(pltpu.make_async_remote_copy(src, dst.at[me], send.at[r - 1], recv.at[r - 1],
                                                     device_id=peer, device_id_type=_MESH))
        for cp in outs:
            cp.start()
        for r in range(1, N_DEV):
            px, py, pc = x ^ (r >> 2), y ^ ((r >> 1) & 1), c ^ (r & 1)
            pltpu.make_async_remote_copy(src, dst.at[4 * px + 2 * py + pc], send.at[r - 1], recv.at[r - 1],
                                         device_id=(px, py, pc), device_id_type=_MESH).wait_recv()
        for cp in outs:
            cp.wait_send()
        mine.wait()

    return pl.pallas_call(
        body, name=name, in_specs=[_ANY], out_specs=_ANY,
        out_shape=jax.ShapeDtypeStruct((N_DEV,) + block.shape, block.dtype),
        scratch_shapes=[pltpu.SemaphoreType.DMA((N_DEV - 1,)), pltpu.SemaphoreType.DMA((N_DEV - 1,)),
                        pltpu.SemaphoreType.DMA(())],
    )(block)


def _pad_lanes(v, n=LANE):
    return jnp.pad(v, ((0, 0), (0, n - v.shape[1])))


def _w_in_pieces():
    cs = (PROJ_BIG + 2 * HEADS) // N_CHIPS
    ab_end = AB_COL + 2 * HEADS
    out = []
    for o in range(N_CHIPS):
        lo, hi = o * cs, (o + 1) * cs
        cand = [("big", lo, min(hi, AB_COL), 0), ("ab", max(lo, AB_COL), min(hi, ab_end), AB_COL),
                ("big", max(lo, ab_end), hi, 2 * HEADS)]
        out.append([(s, a - off, b - off) for s, a, b, off in cand if a < b])
    return out


def _split_w_in(w4):
    big, ab = [], []
    for o, pieces in enumerate(_w_in_pieces()):
        at = 0
        for s, a, b in pieces:
            (big if s == "big" else ab).append(w4[o][:, at:at + b - a])
            at += b - a
    return jnp.concatenate(big, axis=1), _pad_lanes(jnp.concatenate(ab, axis=1))


def _join_w_in(big, ab):
    src = {"big": big, "ab": ab}
    return jnp.stack([jnp.concatenate([src[s][:, a:b] for s, a, b in pieces], axis=1) for pieces in _w_in_pieces()])


def _conv_w8(w):
    return jnp.pad(w, ((0, SUB - DN_CONV), (0, 0)))


def _row_layout(gc, tp):
    nc = tp // CHUNK
    g = gc[:, :, 0].reshape(HEADS, nc, 1, CHUNK)
    g = jnp.broadcast_to(g, (HEADS, nc, SUB, CHUNK))
    return jnp.pad(g, ((0, 0), (0, 0), (0, 0), (0, LANE - CHUNK))).reshape(HEADS, nc * SUB, LANE)


def _step(x, meta, W, target, late_weights=None, early_swap=None, early_grads=None, last_grads=None):
    W = dict(W)
    seq = x.shape[0]
    tp = P0 + N_META + seq
    h0 = jnp.concatenate([jnp.zeros((P0, D_MODEL), f32), meta, x], axis=0)
    w_big, w_ab = _split_w_in(W["w_in"])
    cq8, ck8, cv8 = _conv_w8(W["conv_q"]), _conv_w8(W["conv_k"]), _conv_w8(W["conv_v"])
    al, dtb = _pad_lanes(W["dn_a_log"]), _pad_lanes(W["dn_dt_bias"])

    n1 = rms_fwd(h0, W["norm_mix_gain"], "rms1_fwd")
    proj = matmul(n1, w_big, "nn", "proj_fwd")
    pab = matmul(n1, w_ab, "nn", "pab_fwd")
    qn = conv_fwd(proj, cq8, C_DQ * 8, 8, True, "conv_q_fwd")
    kn = conv_fwd(proj, ck8, C_DK * 8, 8, True, "conv_k_fwd")
    va = conv_fwd(proj, cv8, C_DV * 8, 16, False, "conv_v_fwd")
    gc, bc = ab_fwd(pab, al, dtb)
    grow = _row_layout(gc, tp)
    o_dn, states = gdn_fwd(qn, kn, va, gc, bc, grow)
    on = dn_out_fwd(o_dn, proj, W["dn_out_norm_gain"])
    qs, ks, vs = sb_prep_fwd(proj, W["sb_q_norm_gain"], W["sb_k_norm_gain"])
    o_sb, o_sb16, arrived = sb_fwd(qs, ks, vs, rider=late_weights[0] if late_weights else None)
    if late_weights:
        W.update(late_weights[1](arrived))
    ydn = matmul(on, W["w_branch_dn"], "nn", "ydn_fwd")
    ysb = matmul(o_sb16, W["w_branch_sb"], "nn", "ysb_fwd")
    merged = merge_fwd(proj, ydn, ysb)
    h1 = matmul(merged, W["w_out"], "nn", "wout_fwd", residual=h0)
    n2 = rms_fwd(h1, W["norm_ffn_gain"], "rms2_fwd")
    u = matmul(n2, W["w_ffn_in"], "nn", "ffn_in_fwd", tn_t=512)
    act = swiglu_fwd(u)
    y = matmul(act, W["w_ffn_out"], "nn", "ffn_out_fwd", residual=h1)
    dy, dy16, loss = loss_head(y, target)

    G = {}
    dact = matmul(dy16, W["w_ffn_out"], "nt", "ffn_out_dx", tn_t=1408)
    G["w_ffn_out"] = matmul(act, dy16, "tn", "ffn_out_dw", tm_t=1408)
    dgate, dup = swiglu_bwd(u, dact)
    du = jnp.concatenate([dgate, dup], axis=1)
    dn2 = matmul(du, W["w_ffn_in"], "nt", "ffn_in_dx", tk_t=512)
    G["w_ffn_in"] = matmul(n2, du, "tn", "ffn_in_dw", tn_t=512)
    dh1, dh1_16, G["norm_ffn_gain"] = rms_bwd(h1, W["norm_ffn_gain"], dn2, dy, "rms2_bwd")
    dmerged = matmul(dh1_16, W["w_out"], "nt", "wout_dx")
    G["w_out"] = matmul(merged, dh1_16, "tn", "wout_dw")
    dyd, dys, d_gates = merge_bwd(proj, ydn, ysb, dmerged)
    don = matmul(dyd, W["w_branch_dn"], "nt", "ydn_dx")
    G["w_branch_dn"] = matmul(on, dyd, "tn", "ydn_dw")
    do_sb = matmul(dys, W["w_branch_sb"], "nt", "ysb_dx")
    G["w_branch_sb"] = matmul(o_sb16, dys, "tn", "ysb_dw")
    do_dn, dz, G["dn_out_norm_gain"] = dn_out_bwd(o_dn, proj, W["dn_out_norm_gain"], don)
    (dqn, dkn, dva, dgc, dbc), swapped = gdn_bwd(qn, kn, va, gc, bc, grow, states, do_dn,
                                                 rider=early_swap[0](G) if early_swap else None)
    if early_swap:
        early_swap[1](swapped)
    dpab, dal, ddt = ab_bwd(pab, al, dtb, dgc, dbc)
    G["dn_a_log"], G["dn_dt_bias"] = dal[:, :HEADS], ddt[:, :HEADS]
    dyq, dcq = conv_bwd_act(proj, cq8, dqn, C_DQ * 8, 8, True, "conv_q_bwd")
    dyk, dck = conv_bwd_act(proj, ck8, dkn, C_DK * 8, 8, True, "conv_k_bwd")
    dyv, dcv = conv_bwd_act(proj, cv8, dva, C_DV * 8, 16, False, "conv_v_bwd")
    G["conv_q"], G["conv_k"], G["conv_v"] = dcq[:DN_CONV], dck[:DN_CONV], dcv[:DN_CONV]
    d_dq = conv_bwd_in(dyq, cq8, "conv_q_dx")
    d_dk = conv_bwd_in(dyk, ck8, "conv_k_dx")
    d_dv = conv_bwd_in(dyv, cv8, "conv_v_dx")
    (dqs, dks, dvs), delivered = sb_bwd(qs, ks, vs, o_sb, do_sb, rider=early_grads[0](G) if early_grads else None)
    if early_grads:
        early_grads[1](delivered)
    d_sb, G["sb_q_norm_gain"], G["sb_k_norm_gain"] = sb_prep_bwd(
        proj, W["sb_q_norm_gain"], W["sb_k_norm_gain"], dqs, dks, dvs)
    dproj = jnp.concatenate([d_dq, d_dk, d_dv, dz, d_sb, d_gates], axis=1)
    dw_big = matmul(n1, dproj, "tn", "proj_dw")
    dw_ab = matmul(n1, dpab, "tn", "pab_dw")
    G["w_in"] = _join_w_in(dw_big, dw_ab)
    if last_grads:
        dn1, delivered = matmul(dproj, w_big, "nt", "proj_dx", tk_t=1024, rider=last_grads[0](G))
        last_grads[1](delivered)
    else:
        dn1 = matmul(dproj, w_big, "nt", "proj_dx", tk_t=1024)
    dn1 = matmul(dpab, w_ab, "nt", "pab_dx", residual=dn1)
    dh0, _, G["norm_mix_gain"] = rms_bwd(h0, W["norm_mix_gain"], dn1, dh1, "rms1_bwd")
    G["meta_tokens"] = dh0[P0:P0 + N_META]
    return loss, dh0[P0 + N_META:], G


_BIG = ("w_in", "w_branch_dn", "w_branch_sb", "w_out", "w_ffn_in", "w_ffn_out")
_COL_SHARDED = ("w_in", "w_ffn_in", "meta_tokens", "conv_q", "conv_k", "conv_v")
_SMALL_REPL = ("norm_mix_gain", "norm_ffn_gain", "dn_a_log", "dn_dt_bias", "dn_out_norm_gain", "sb_q_norm_gain",
               "sb_k_norm_gain")
_SMALL_SHARD = ("meta_tokens", "conv_q", "conv_k", "conv_v")
_ORDER = ("meta_tokens", "norm_mix_gain", "w_in", "conv_q", "conv_k", "conv_v", "dn_a_log", "dn_dt_bias",
          "dn_out_norm_gain", "sb_q_norm_gain", "sb_k_norm_gain", "w_branch_dn", "w_branch_sb", "w_out",
          "norm_ffn_gain", "w_ffn_in", "w_ffn_out")


def _unshard(g4, name):
    if name in _COL_SHARDED:
        r, cs = g4.shape[1:]
        return jnp.transpose(g4, (1, 0, 2)).reshape(r, N_CHIPS * cs)
    return g4.reshape((-1,) + g4.shape[2:])


def _to_shards(full, name):
    if full.ndim == 3:
        return full
    if name in _COL_SHARDED:
        r, c = full.shape
        return jnp.transpose(full.reshape(r, N_CHIPS, c // N_CHIPS), (1, 0, 2))
    r, c = full.shape
    return full.reshape(N_CHIPS, r // N_CHIPS, c)


def _rows_1024(a):
    r, c = a.shape
    if c >= 1024:
        return a.reshape(r * (c // 1024), 1024)
    return jnp.pad(a, ((0, 0), (0, 1024 - c)))


def kernel(x, meta_tokens, norm_mix_gain, w_in, conv_q, conv_k, conv_v, dn_a_log, dn_dt_bias, dn_out_norm_gain, sb_q_norm_gain, sb_k_norm_gain, w_branch_dn, w_branch_sb, w_out, norm_ffn_gain, w_ffn_in, w_ffn_out, loss_target, m_meta_tokens, m_norm_mix_gain, m_w_in, m_conv_q, m_conv_k, m_conv_v, m_dn_a_log, m_dn_dt_bias, m_dn_out_norm_gain, m_sb_q_norm_gain, m_sb_k_norm_gain, m_w_branch_dn, m_w_branch_sb, m_w_out, m_norm_ffn_gain, m_w_ffn_in, m_w_ffn_out, v_meta_tokens, v_norm_mix_gain, v_w_in, v_conv_q, v_conv_k, v_conv_v, v_dn_a_log, v_dn_dt_bias, v_dn_out_norm_gain, v_sb_q_norm_gain, v_sb_k_norm_gain, v_w_branch_dn, v_w_branch_sb, v_w_out, v_norm_ffn_gain, v_w_ffn_in, v_w_ffn_out):
    Wl = dict(meta_tokens=meta_tokens, norm_mix_gain=norm_mix_gain, w_in=w_in[0], conv_q=conv_q[0], conv_k=conv_k[0],
              conv_v=conv_v[0], dn_a_log=dn_a_log, dn_dt_bias=dn_dt_bias, dn_out_norm_gain=dn_out_norm_gain,
              sb_q_norm_gain=sb_q_norm_gain, sb_k_norm_gain=sb_k_norm_gain, w_branch_dn=w_branch_dn[0],
              w_branch_sb=w_branch_sb[0], w_out=w_out[0], norm_ffn_gain=norm_ffn_gain, w_ffn_in=w_ffn_in[0],
              w_ffn_out=w_ffn_out[0])
    Ml = dict(meta_tokens=m_meta_tokens, norm_mix_gain=m_norm_mix_gain, w_in=m_w_in[0], conv_q=m_conv_q[0],
              conv_k=m_conv_k[0], conv_v=m_conv_v[0], dn_a_log=m_dn_a_log, dn_dt_bias=m_dn_dt_bias,
              dn_out_norm_gain=m_dn_out_norm_gain, sb_q_norm_gain=m_sb_q_norm_gain, sb_k_norm_gain=m_sb_k_norm_gain,
              w_branch_dn=m_w_branch_dn[0], w_branch_sb=m_w_branch_sb[0], w_out=m_w_out[0],
              norm_ffn_gain=m_norm_ffn_gain, w_ffn_in=m_w_ffn_in[0], w_ffn_out=m_w_ffn_out[0])
    Vl = dict(meta_tokens=v_meta_tokens, norm_mix_gain=v_norm_mix_gain, w_in=v_w_in[0], conv_q=v_conv_q[0],
              conv_k=v_conv_k[0], conv_v=v_conv_v[0], dn_a_log=v_dn_a_log, dn_dt_bias=v_dn_dt_bias,
              dn_out_norm_gain=v_dn_out_norm_gain, sb_q_norm_gain=v_sb_q_norm_gain, sb_k_norm_gain=v_sb_k_norm_gain,
              w_branch_dn=v_w_branch_dn[0], w_branch_sb=v_w_branch_sb[0], w_out=v_w_out[0],
              norm_ffn_gain=v_norm_ffn_gain, w_ffn_in=v_w_ffn_in[0], w_ffn_out=v_w_ffn_out[0])
    lead = {n: (1,) if (n in _BIG or n in ("conv_q", "conv_k", "conv_v")) else () for n in _ORDER}

    chip = 2 * lax.axis_index("x") + lax.axis_index("y")
    c = lax.axis_index("c")
    halved = {n: Wl[n].astype(bf16).reshape(2, Wl[n].shape[0] // 2, Wl[n].shape[1]) for n in _BIG}

    def gathered_weights(names, owns, outs):
        res = {}
        for n, own, g4 in zip(names, owns, outs):
            g4 = lax.dynamic_update_slice(g4, own[None], (chip,) + (0,) * own.ndim)
            if n in _BIG:
                g4 = g4.reshape(N_CHIPS, 2 * g4.shape[2], g4.shape[3])
            res[n] = g4 if n == "w_in" else _unshard(g4, n)
        return res

    first = ["w_in"] + list(_SMALL_SHARD)
    first_own = [halved["w_in"]] + [Wl[n] for n in _SMALL_SHARD]
    W = dict(Wl)
    W.update(gathered_weights(first, first_own, run_exchange(gather_chips(first_own[:1], first_own[1:]), "gather_w_in")))
    late = [n for n in _BIG if n != "w_in"]
    late_own = [halved[n] for n in late]
    for n in late:
        del W[n]

    def halves_of(names, G):
        g4 = [_to_shards(G[n], n) for n in names]
        return [g.reshape(N_CHIPS, 2, g.shape[1] // 2, g.shape[2]) for g in g4]

    def pair_added(g42, from_sib, tag, wire):
        mine = [lax.dynamic_index_in_dim(g, c, axis=1, keepdims=False) for g in g42]
        return [add2(a, b, "grad_pair_add_%s%d" % (tag, t), out_dtype=wire)
                for t, (a, b) in enumerate(zip(mine, from_sib))]

    def chip_reduced(parts, slots, tag):
        slots = [lax.dynamic_update_slice(s, lax.dynamic_index_in_dim(p, chip, axis=0, keepdims=True), (chip, 0, 0))
                 for s, p in zip(slots, parts)]
        return [sum_slots(s, "grad_chip_sum_%s%d" % (tag, t)) for t, s in enumerate(slots)]

    early, last = {}, {}

    def early_swap_begin(G):
        early["g42"] = halves_of(late, G)
        return sibling_swap(early["g42"])

    def early_begin(G):
        early["parts"] = pair_added(early["g42"], early["from_sib"], "a", f32)
        return scatter_chips(early["parts"])

    def last_begin(G):
        g42 = halves_of(["w_in"], G)
        last["parts"] = pair_added(g42, run_exchange(sibling_swap(g42), "grad_sibling_swap_b"), "b", bf16)
        return scatter_chips(last["parts"])

    loss, grad_x, G = _step(
        x[0], W["meta_tokens"], W, loss_target[0],
        late_weights=(gather_chips(late_own, []), lambda outs: gathered_weights(late, late_own, outs)),
        early_swap=(early_swap_begin, lambda outs: early.update(from_sib=outs)),
        early_grads=(early_begin, lambda slots: early.update(halves=chip_reduced(early["parts"], slots, "a"))),
        last_grads=(last_begin, lambda slots: last.update(halves=chip_reduced(last["parts"], slots, "b"))))
    halves = last["halves"] + early["halves"]
    theirs = sibling_send(halves, "grad_sibling_send")
    Gs = {}
    for n, h, o in zip(["w_in"] + late, halves, theirs):
        Gs[n] = lax.dynamic_update_slice(jnp.concatenate([o, o], axis=0), h, (c * h.shape[0], 0))

    small_names = list(_SMALL_REPL) + list(_SMALL_SHARD)
    pieces = [_rows_1024(G[n]) for n in small_names] + [_rows_1024(loss)]
    counts = [p.shape[0] for p in pieces]
    pack = jnp.concatenate(pieces, axis=0)
    pad_rows = (-pack.shape[0]) % SUB
    pack = jnp.pad(pack, ((0, pad_rows), (0, 0)))
    total = sum_slots(gather_all(pack, "small_gather"), "small_sum")
    chip = 2 * lax.axis_index("x") + lax.axis_index("y")
    row = 0
    for n, cnt in zip(small_names, counts[:-1]):
        blk = total[row:row + cnt]
        row += cnt
        full_shape = G[n].shape
        if full_shape[1] >= 1024:
            blk = blk.reshape(full_shape)
        else:
            blk = blk[:, :full_shape[1]]
        if n in _SMALL_SHARD:
            cs = full_shape[1] // N_CHIPS
            blk = lax.dynamic_slice_in_dim(blk, chip * cs, cs, axis=1)
        Gs[n] = blk
    loss_out = total[row, 0]

    grads, deltas, new_m, new_v = [], [], [], []
    for n in _ORDER:
        d, m2, v2 = adamw(Wl[n], Gs[n], Ml[n], Vl[n], "adamw_" + n)
        shape = lead[n] + Wl[n].shape
        grads.append(Gs[n].reshape(shape))
        deltas.append(d.reshape(shape))
        new_m.append(m2.reshape(shape))
        new_v.append(v2.reshape(shape))
    return (loss_out, grad_x[None], *grads, *deltas, *new_m, *new_v)
```

```python
import jax
import jax.numpy as jnp
from jax import lax
from jax.experimental import pallas as pl
from jax.experimental.pallas import tpu as pltpu

f32 = jnp.float32
bf16 = jnp.bfloat16

D_MODEL = 1024
N_META = 16
CHUNK = 64
HEADS = 8
DN_DK = 128
DN_DV = 256
DN_CONV = 4
DN_QK = HEADS * DN_DK
DN_V = HEADS * DN_DV
SB_DH = 128
SB_W = HEADS * SB_DH
SB_BLOCK = 128
SB_QB = 384
SB_GROUP = 4
SB_HEADS_PER_STEP = 2
SB_FWD_HEADS_PER_STEP = 4
GDN_HEADS_PER_STEP = 8
CONV_W = 512
D_FF = 2816
RMS_EPS = 1e-6
L2_EPS = 1e-6
ADAM_LR = 0.001
ADAM_B1 = 0.9
ADAM_B2 = 0.999
ADAM_EPS = 1e-08
ADAM_WD = 0.01
ADAM_STEP = 10

P0 = 112
LANE = 128
SUB = 8
VMEM_LIMIT = 48 * 1024 * 1024
N_CHIPS = 4
N_DEV = 8

C_DQ, C_DK, C_DV, C_DZ, C_SQ, C_SK, C_SV, C_GDN, C_GSB = 0, 1, 2, 4, 6, 7, 8, 9, 10
PROJ_BIG = 11 * 1024
AB_COL = 2 * DN_QK + 2 * DN_V


def _params(n_axes):
    return pltpu.CompilerParams(dimension_semantics=("arbitrary",) * n_axes, vmem_limit_bytes=VMEM_LIMIT)


def _tile(n, target, q=LANE):
    best = None
    for t in range(q, min(n, target) + 1, q):
        if n % t == 0:
            best = t
    return best if best is not None else n


def _dot(a, b):
    return jnp.dot(a.astype(bf16), b.astype(bf16), preferred_element_type=f32)


def _dot_nt(a, b):
    return lax.dot_general(a.astype(bf16), b.astype(bf16), (((1,), (1,)), ((), ())), preferred_element_type=f32)


def _dot_tn(a, b):
    return lax.dot_general(a.astype(bf16), b.astype(bf16), (((0,), (0,)), ((), ())), preferred_element_type=f32)


_HI = lax.Precision.HIGH


def _hdot(a, b):
    return jnp.dot(a, b, precision=_HI, preferred_element_type=f32)


def _hdot_nt(a, b):
    return lax.dot_general(a, b, (((1,), (1,)), ((), ())), precision=_HI, preferred_element_type=f32)


def _hdot_tn(a, b):
    return lax.dot_general(a, b, (((0,), (0,)), ((), ())), precision=_HI, preferred_element_type=f32)


def _sigmoid(x):
    return 0.5 * jnp.tanh(0.5 * x) + 0.5


def _log1p_small(e):
    return jnp.where(e < 1e-3, e * (1.0 - e * (0.5 - e * (1.0 / 3.0))), jnp.log(1.0 + e))


def _rowsum(x):
    return jnp.sum(x, axis=1, keepdims=True)


def _allsum(x):
    return jnp.sum(jnp.sum(x, axis=1, keepdims=True), axis=0, keepdims=True)


def matmul(a, b, mode, name, residual=None, out_dtype=f32, tm_t=1408, tn_t=1024, tk_t=1408, rider=None):
    if mode == "nn":
        (M, K), (K2, N) = a.shape, b.shape
    elif mode == "nt":
        (M, K), (N, K2) = a.shape, b.shape
    else:
        (K, M), (K2, N) = a.shape, b.shape
    assert K == K2, (a.shape, b.shape, mode)
    tm, tn, tk = _tile(M, tm_t), _tile(N, tn_t), _tile(K, tk_t)
    nk = K // tk
    if mode == "nn":
        a_spec = pl.BlockSpec((tm, tk), lambda i, j, k: (i, k))
        b_spec = pl.BlockSpec((tk, tn), lambda i, j, k: (k, j))
        dims = (((1,), (0,)), ((), ()))
    elif mode == "nt":
        a_spec = pl.BlockSpec((tm, tk), lambda i, j, k: (i, k))
        b_spec = pl.BlockSpec((tn, tk), lambda i, j, k: (j, k))
        dims = (((1,), (1,)), ((), ()))
    else:
        a_spec = pl.BlockSpec((tk, tm), lambda i, j, k: (k, i))
        b_spec = pl.BlockSpec((tk, tn), lambda i, j, k: (k, j))
        dims = (((0,), (0,)), ((), ()))
    o_spec = pl.BlockSpec((tm, tn), lambda i, j, k: (i, j))
    has_res = residual is not None
    grid = (M // tm, N // tn, nk)
    split, ride_first, ride_last = _ride(rider, 3 if has_res else 2, 1, grid)

    def body(*refs):
        ins_, (o_ref,), (rin, rout, rest) = split(refs)
        a_ref, b_ref = ins_[:2]
        r_ref = ins_[2] if has_res else None
        acc_ref, ride = rest[0], (rin, rout, rest[1:])
        ride_first(ride)
        k = pl.program_id(2)

        @pl.when(k == 0)
        def _():
            acc_ref[...] = jnp.zeros_like(acc_ref)

        acc_ref[...] += lax.dot_general(a_ref[...].astype(bf16), b_ref[...].astype(bf16), dims,
                                        preferred_element_type=f32)

        @pl.when(k == nk - 1)
        def _():
            r = acc_ref[...]
            if has_res:
                r = r + r_ref[...]
            o_ref[...] = r.astype(out_dtype)

        ride_last(ride)

    ins = [a, b] + ([residual] if has_res else [])
    specs = [a_spec, b_spec] + ([o_spec] if has_res else [])
    r_ins = rider.ins if rider else []
    r_outs = rider.out_shapes if rider else []
    res = pl.pallas_call(
        body, name=name, grid=grid, in_specs=specs + [_ANY] * len(r_ins), out_specs=[o_spec] + [_ANY] * len(r_outs),
        out_shape=[jax.ShapeDtypeStruct((M, N), out_dtype)] + list(r_outs),
        scratch_shapes=[pltpu.VMEM((tm, tn), f32)] + (rider.scratch if rider else []), compiler_params=_params(3),
    )(*ins, *r_ins)
    return (res[0], res[1:]) if rider else res[0]


def _row_tile(tp):
    return _tile(tp, 512)


def rms_fwd(h, gain, name):
    tp, d = h.shape
    rt = _row_tile(tp)

    def body(h_ref, g_ref, o_ref):
        x = h_ref[...]
        r = lax.rsqrt(jnp.mean(x * x, axis=-1, keepdims=True) + RMS_EPS)
        o_ref[...] = (x * r * g_ref[...]).astype(bf16)

    return pl.pallas_call(
        body, name=name, grid=(tp // rt,),
        in_specs=[pl.BlockSpec((rt, d), lambda i: (i, 0)), pl.BlockSpec((1, d), lambda i: (0, 0))],
        out_specs=pl.BlockSpec((rt, d), lambda i: (i, 0)),
        out_shape=jax.ShapeDtypeStruct((tp, d), bf16), compiler_params=_params(1),
    )(h, gain)


def rms_bwd(h, gain, dn, dres, name):
    tp, d = h.shape
    rt = _row_tile(tp)

    def body(h_ref, g_ref, dn_ref, dr_ref, dh_ref, dhb_ref, dg_ref):
        i = pl.program_id(0)
        x = h_ref[...]
        r = lax.rsqrt(jnp.mean(x * x, axis=-1, keepdims=True) + RMS_EPS)
        xh = x * r
        dn_ = dn_ref[...]
        dxh = dn_ * g_ref[...]
        dh = r * (dxh - xh * jnp.mean(dxh * xh, axis=-1, keepdims=True)) + dr_ref[...]
        dh_ref[...] = dh
        dhb_ref[...] = dh.astype(bf16)
        part = jnp.sum(dn_ * xh, axis=0, keepdims=True)

        @pl.when(i == 0)
        def _():
            dg_ref[...] = part

        @pl.when(i > 0)
        def _():
            dg_ref[...] += part

    row = pl.BlockSpec((rt, d), lambda i: (i, 0))
    vec = pl.BlockSpec((1, d), lambda i: (0, 0))
    return pl.pallas_call(
        body, name=name, grid=(tp // rt,), in_specs=[row, vec, row, row], out_specs=[row, row, vec],
        out_shape=[jax.ShapeDtypeStruct((tp, d), f32), jax.ShapeDtypeStruct((tp, d), bf16),
                   jax.ShapeDtypeStruct((1, d), f32)],
        compiler_params=_params(1),
    )(h, gain, dn, dres)


def loss_head(y, target):
    tp, d = y.shape
    rt = P0 + N_META
    assert rt == SB_BLOCK and tp % rt == 0 and target.shape == (tp - rt, d)

    def body(y_ref, t_ref, dy_ref, dyb_ref, l_ref):
        i = pl.program_id(0)

        @pl.when(i == 0)
        def _():
            dy_ref[...] = jnp.zeros_like(dy_ref)
            dyb_ref[...] = jnp.zeros_like(dyb_ref)
            l_ref[...] = jnp.zeros_like(l_ref)

        @pl.when(i > 0)
        def _():
            err = y_ref[...] - t_ref[...]
            dy = err * (1.0 / d)
            dy_ref[...] = dy
            dyb_ref[...] = dy.astype(bf16)
            l_ref[...] += jnp.broadcast_to(_allsum(err * err) * (0.5 / d), l_ref.shape)

    row = pl.BlockSpec((rt, d), lambda i: (i, 0))
    return pl.pallas_call(
        body, name="loss_head", grid=(tp // rt,),
        in_specs=[row, pl.BlockSpec((rt, d), lambda i: (jnp.maximum(i - 1, 0), 0))],
        out_specs=[row, row, pl.BlockSpec((1, LANE), lambda i: (0, 0))],
        out_shape=[jax.ShapeDtypeStruct((tp, d), f32), jax.ShapeDtypeStruct((tp, d), bf16),
                   jax.ShapeDtypeStruct((1, LANE), f32)],
        compiler_params=_params(1),
    )(y, target)


def swiglu_fwd(u):
    tp = u.shape[0]
    rt, cb = _row_tile(tp), D_FF // 2
    nb = D_FF // cb

    def body(g_ref, u_ref, o_ref):
        g = g_ref[...]
        o_ref[...] = (g * _sigmoid(g) * u_ref[...]).astype(bf16)

    return pl.pallas_call(
        body, name="swiglu_fwd", grid=(tp // rt, nb),
        in_specs=[pl.BlockSpec((rt, cb), lambda i, j: (i, j)), pl.BlockSpec((rt, cb), lambda i, j: (i, j + nb))],
        out_specs=pl.BlockSpec((rt, cb), lambda i, j: (i, j)),
        out_shape=jax.ShapeDtypeStruct((tp, D_FF), bf16), compiler_params=_params(2),
    )(u, u)


def swiglu_bwd(u, dact):
    tp = u.shape[0]
    rt, cb = _row_tile(tp), D_FF // 2
    nb = D_FF // cb

    def body(g_ref, u_ref, da_ref, dg_ref, du_ref):
        g = g_ref[...]
        s = _sigmoid(g)
        da = da_ref[...]
        dg_ref[...] = (da * u_ref[...] * s * (1.0 + g * (1.0 - s))).astype(bf16)
        du_ref[...] = (da * g * s).astype(bf16)

    lo = pl.BlockSpec((rt, cb), lambda i, j: (i, j))
    hi = pl.BlockSpec((rt, cb), lambda i, j: (i, j + nb))
    dgate, dup = pl.pallas_call(
        body, name="swiglu_bwd", grid=(tp // rt, nb), in_specs=[lo, hi, lo], out_specs=[lo, lo],
        out_shape=[jax.ShapeDtypeStruct((tp, D_FF), bf16)] * 2, compiler_params=_params(2),
    )(u, u, dact)
    return dgate, dup


def merge_fwd(proj, ydn, ysb):
    tp = proj.shape[0]
    rt, d = _row_tile(tp), D_MODEL

    def body(gd_ref, gs_ref, yd_ref, ys_ref, o_ref):
        o_ref[...] = (_sigmoid(gd_ref[...]) * yd_ref[...] + _sigmoid(gs_ref[...]) * ys_ref[...]).astype(bf16)

    row = pl.BlockSpec((rt, d), lambda i: (i, 0))
    return pl.pallas_call(
        body, name="merge_fwd", grid=(tp // rt,),
        in_specs=[pl.BlockSpec((rt, d), lambda i: (i, C_GDN)), pl.BlockSpec((rt, d), lambda i: (i, C_GSB)), row, row],
        out_specs=row, out_shape=jax.ShapeDtypeStruct((tp, d), bf16), compiler_params=_params(1),
    )(proj, proj, ydn, ysb)


def merge_bwd(proj, ydn, ysb, dm):
    tp = proj.shape[0]
    rt, d = _row_tile(tp), D_MODEL

    def body(gd_ref, gs_ref, yd_ref, ys_ref, dm_ref, dyd_ref, dys_ref, dg_ref):
        dm_ = dm_ref[...]
        sd = _sigmoid(gd_ref[...])
        ss = _sigmoid(gs_ref[...])
        dyd_ref[...] = (dm_ * sd).astype(bf16)
        dys_ref[...] = (dm_ * ss).astype(bf16)
        dg_ref[:, :d] = (dm_ * yd_ref[...] * sd * (1.0 - sd)).astype(bf16)
        dg_ref[:, d:] = (dm_ * ys_ref[...] * ss * (1.0 - ss)).astype(bf16)

    row = pl.BlockSpec((rt, d), lambda i: (i, 0))
    return pl.pallas_call(
        body, name="merge_bwd", grid=(tp // rt,),
        in_specs=[pl.BlockSpec((rt, d), lambda i: (i, C_GDN)), pl.BlockSpec((rt, d), lambda i: (i, C_GSB)), row, row, row],
        out_specs=[row, row, pl.BlockSpec((rt, 2 * d), lambda i: (i, 0))],
        out_shape=[jax.ShapeDtypeStruct((tp, d), bf16)] * 2 + [jax.ShapeDtypeStruct((tp, 2 * d), bf16)],
        compiler_params=_params(1),
    )(proj, proj, ydn, ysb, dm)


def dn_out_fwd(o, proj, gain):
    tp = o.shape[0]
    rt, cb, wide = _row_tile(tp), DN_DV, 1024
    zb = C_DZ * 1024 // wide

    def body(o_ref, z_ref, g_ref, y_ref):
        for s in range(wide // cb):
            sl = slice(s * cb, (s + 1) * cb)
            x = o_ref[:, sl]
            r = lax.rsqrt(jnp.mean(x * x, axis=-1, keepdims=True) + RMS_EPS)
            z = z_ref[:, sl]
            y_ref[:, sl] = (x * r * g_ref[...] * (z * _sigmoid(z))).astype(bf16)

    blk = pl.BlockSpec((rt, wide), lambda i, j: (i, j))
    return pl.pallas_call(
        body, name="dn_out_fwd", grid=(tp // rt, DN_V // wide),
        in_specs=[blk, pl.BlockSpec((rt, wide), lambda i, j: (i, j + zb)), pl.BlockSpec((1, cb), lambda i, j: (0, 0))],
        out_specs=blk, out_shape=jax.ShapeDtypeStruct((tp, DN_V), bf16), compiler_params=_params(2),
    )(o, proj, gain)


def dn_out_bwd(o, proj, gain, dy):
    tp = o.shape[0]
    rt, cb, wide = _row_tile(tp), DN_DV, 1024
    zb = C_DZ * 1024 // wide

    def body(o_ref, z_ref, g_ref, dy_ref, do_ref, dz_ref, dg_ref):
        i, j = pl.program_id(0), pl.program_id(1)
        g = g_ref[...]
        part = jnp.zeros((1, cb), f32)
        for hh in range(wide // cb):
            sl = slice(hh * cb, (hh + 1) * cb)
            x = o_ref[:, sl]
            r = lax.rsqrt(jnp.mean(x * x, axis=-1, keepdims=True) + RMS_EPS)
            xh = x * r
            z = z_ref[:, sl]
            s = _sigmoid(z)
            dy_ = dy_ref[:, sl]
            drn = dy_ * (z * s)
            dz_ref[:, sl] = (dy_ * xh * g * s * (1.0 + z * (1.0 - s))).astype(bf16)
            dxh = drn * g
            do_ref[:, sl] = r * (dxh - xh * jnp.mean(dxh * xh, axis=-1, keepdims=True))
            part = part + jnp.sum(drn * xh, axis=0, keepdims=True)
        first = jnp.logical_and(i == 0, j == 0)

        @pl.when(first)
        def _():
            dg_ref[...] = part

        @pl.when(jnp.logical_not(first))
        def _():
            dg_ref[...] += part

    blk = pl.BlockSpec((rt, wide), lambda i, j: (i, j))
    vec = pl.BlockSpec((1, cb), lambda i, j: (0, 0))
    return pl.pallas_call(
        body, name="dn_out_bwd", grid=(tp // rt, DN_V // wide),
        in_specs=[blk, pl.BlockSpec((rt, wide), lambda i, j: (i, j + zb)), vec, blk],
        out_specs=[blk, blk, vec],
        out_shape=[jax.ShapeDtypeStruct((tp, DN_V), f32), jax.ShapeDtypeStruct((tp, DN_V), bf16),
                   jax.ShapeDtypeStruct((1, cb), f32)],
        compiler_params=_params(2),
    )(o, proj, gain, dy)


def sb_prep_fwd(proj, gq, gk):
    tp = proj.shape[0]
    rt, cb = _row_tile(tp), SB_DH

    def body(q_ref, k_ref, v_ref, gq_ref, gk_ref, qo_ref, ko_ref, vo_ref):
        for x_ref, g_ref, o_ref in ((q_ref, gq_ref, qo_ref), (k_ref, gk_ref, ko_ref)):
            for h in range(HEADS):
                sl = slice(h * cb, (h + 1) * cb)
                x = x_ref[:, sl]
                r = lax.rsqrt(jnp.mean(x * x, axis=-1, keepdims=True) + RMS_EPS)
                o_ref[:, sl] = (x * r * g_ref[...]).astype(bf16)
        vo_ref[...] = v_ref[...].astype(bf16)

    blk = pl.BlockSpec((rt, SB_W), lambda i: (i, 0))
    vec = pl.BlockSpec((1, cb), lambda i: (0, 0))
    return pl.pallas_call(
        body, name="sb_prep_fwd", grid=(tp // rt,),
        in_specs=[pl.BlockSpec((rt, SB_W), lambda i: (i, C_SQ)), pl.BlockSpec((rt, SB_W), lambda i: (i, C_SK)),
                  pl.BlockSpec((rt, SB_W), lambda i: (i, C_SV)), vec, vec],
        out_specs=[blk] * 3, out_shape=[jax.ShapeDtypeStruct((tp, SB_W), bf16)] * 3, compiler_params=_params(1),
    )(proj, proj, proj, gq, gk)


def sb_prep_bwd(proj, gq, gk, dqs, dks, dvs):
    tp = proj.shape[0]
    rt, cb = _row_tile(tp), SB_DH

    def body(q_ref, k_ref, gq_ref, gk_ref, dq_ref, dk_ref, dv_ref, do_ref, dgq_ref, dgk_ref):
        first = pl.program_id(0) == 0
        do_ref[:, 2 * SB_W:] = dv_ref[...].astype(bf16)
        for x_ref, g_ref, dn_ref, at, dg_ref in ((q_ref, gq_ref, dq_ref, 0, dgq_ref),
                                                 (k_ref, gk_ref, dk_ref, SB_W, dgk_ref)):
            part = jnp.zeros((1, cb), f32)
            for h in range(HEADS):
                sl = slice(h * cb, (h + 1) * cb)
                x = x_ref[:, sl]
                r = lax.rsqrt(jnp.mean(x * x, axis=-1, keepdims=True) + RMS_EPS)
                xh = x * r
                dn_ = dn_ref[:, sl]
                dxh = dn_ * g_ref[...]
                do_ref[:, at + h * cb:at + (h + 1) * cb] = (
                    r * (dxh - xh * jnp.mean(dxh * xh, axis=-1, keepdims=True))).astype(bf16)
                part = part + jnp.sum(dn_ * xh, axis=0, keepdims=True)

            @pl.when(first)
            def _(dg_ref=dg_ref, part=part):
                dg_ref[...] = part

            @pl.when(jnp.logical_not(first))
            def _(dg_ref=dg_ref, part=part):
                dg_ref[...] += part

    blk = pl.BlockSpec((rt, SB_W), lambda i: (i, 0))
    vec = pl.BlockSpec((1, cb), lambda i: (0, 0))
    return pl.pallas_call(
        body, name="sb_prep_bwd", grid=(tp // rt,),
        in_specs=[pl.BlockSpec((rt, SB_W), lambda i: (i, C_SQ)), pl.BlockSpec((rt, SB_W), lambda i: (i, C_SK)),
                  vec, vec, blk, blk, blk],
        out_specs=[pl.BlockSpec((rt, 3 * SB_W), lambda i: (i, 0)), vec, vec],
        out_shape=[jax.ShapeDtypeStruct((tp, 3 * SB_W), bf16)] + [jax.ShapeDtypeStruct((1, cb), f32)] * 2,
        compiler_params=_params(1),
    )(proj, proj, gq, gk, dqs, dks, dvs)


def _conv_taps(ext, rt):
    taps = []
    for k in range(DN_CONV):
        s = DN_CONV - 1 - k
        taps.append((pltpu.roll(ext, s, axis=0) if s else ext)[SUB:SUB + rt])
    return taps


def _conv_act(taps, w, l2):
    y = taps[0] * w[0:1]
    for k in range(1, DN_CONV):
        y = y + taps[k] * w[k:k + 1]
    s = _sigmoid(y)
    a = y * s
    if l2:
        n = lax.rsqrt(jnp.sum(a * a, axis=-1, keepdims=True) + L2_EPS)
        return y, s, a, n
    return y, s, a, None


def conv_fwd(proj, w8, col_blk, ncb, l2, name):
    tp = proj.shape[0]
    rt = _row_tile(tp)
    hb = rt // SUB
    cw = CONV_W
    cb0 = col_blk * LANE // cw

    def body(x_ref, h_ref, w_ref, o_ref):
        i = pl.program_id(1)
        first = (i > 0).astype(f32)
        for s in range(cw // LANE):
            sl = slice(s * LANE, (s + 1) * LANE)
            ext = jnp.concatenate([h_ref[:, sl] * first, x_ref[:, sl]], axis=0)
            _, _, a, n = _conv_act(_conv_taps(ext, rt), w_ref[:, sl], l2)
            o_ref[:, sl] = a * n if l2 else a

    return pl.pallas_call(
        body, name=name, grid=(ncb * LANE // cw, tp // rt),
        in_specs=[pl.BlockSpec((rt, cw), lambda j, i: (i, j + cb0)),
                  pl.BlockSpec((SUB, cw), lambda j, i: (jnp.maximum(i * hb - 1, 0), j + cb0)),
                  pl.BlockSpec((SUB, cw), lambda j, i: (0, j))],
        out_specs=pl.BlockSpec((rt, cw), lambda j, i: (i, j)),
        out_shape=jax.ShapeDtypeStruct((tp, ncb * LANE), f32), compiler_params=_params(2),
    )(proj, proj, w8)


def conv_bwd_act(proj, w8, dout, col_blk, ncb, l2, name):
    tp = proj.shape[0]
    rt = _row_tile(tp)
    hb = rt // SUB
    cw = CONV_W
    cb0 = col_blk * LANE // cw

    def body(x_ref, h_ref, w_ref, d_ref, dy_ref, dw_ref):
        i = pl.program_id(1)
        first = (i > 0).astype(f32)
        rows = lax.broadcasted_iota(jnp.int32, (SUB, LANE), 0)
        for s in range(cw // LANE):
            sl = slice(s * LANE, (s + 1) * LANE)
            ext = jnp.concatenate([h_ref[:, sl] * first, x_ref[:, sl]], axis=0)
            taps = _conv_taps(ext, rt)
            y, sg, a, n = _conv_act(taps, w_ref[:, sl], l2)
            da = d_ref[:, sl]
            if l2:
                out = a * n
                da = n * (da - out * jnp.sum(da * out, axis=-1, keepdims=True))
            dy = da * sg * (1.0 + y * (1.0 - sg))
            dy_ref[:, sl] = dy
            part = jnp.zeros((SUB, LANE), f32)
            for k in range(DN_CONV):
                part = part + jnp.where(rows == k, jnp.sum(taps[k] * dy, axis=0, keepdims=True), 0.0)

            @pl.when(i == 0)
            def _(sl=sl, part=part):
                dw_ref[:, sl] = part

            @pl.when(i > 0)
            def _(sl=sl, part=part):
                dw_ref[:, sl] += part

    return pl.pallas_call(
        body, name=name, grid=(ncb * LANE // cw, tp // rt),
        in_specs=[pl.BlockSpec((rt, cw), lambda j, i: (i, j + cb0)),
                  pl.BlockSpec((SUB, cw), lambda j, i: (jnp.maximum(i * hb - 1, 0), j + cb0)),
                  pl.BlockSpec((SUB, cw), lambda j, i: (0, j)),
                  pl.BlockSpec((rt, cw), lambda j, i: (i, j))],
        out_specs=[pl.BlockSpec((rt, cw), lambda j, i: (i, j)), pl.BlockSpec((SUB, cw), lambda j, i: (0, j))],
        out_shape=[jax.ShapeDtypeStruct((tp, ncb * LANE), f32), jax.ShapeDtypeStruct((SUB, ncb * LANE), f32)],
        compiler_params=_params(2),
    )(proj, proj, w8, dout)


def conv_bwd_in(dy, w8, name):
    tp, cols = dy.shape
    rt = _row_tile(tp)
    hb = rt // SUB
    nr = tp // rt
    last8 = tp // SUB - 1
    cw = CONV_W

    def body(d_ref, h_ref, w_ref, o_ref):
        i = pl.program_id(1)
        last = (i < nr - 1).astype(f32)
        for c0 in range(cw // LANE):
            sl = slice(c0 * LANE, (c0 + 1) * LANE)
            ext = jnp.concatenate([d_ref[:, sl], h_ref[:, sl] * last], axis=0)
            w = w_ref[:, sl]
            acc = None
            for k in range(DN_CONV):
                s = DN_CONV - 1 - k
                sh = (pltpu.roll(ext, rt + SUB - s, axis=0) if s else ext)[0:rt]
                term = sh * w[k:k + 1]
                acc = term if acc is None else acc + term
            o_ref[:, sl] = acc.astype(bf16)

    return pl.pallas_call(
        body, name=name, grid=(cols // cw, nr),
        in_specs=[pl.BlockSpec((rt, cw), lambda j, i: (i, j)),
                  pl.BlockSpec((SUB, cw), lambda j, i: (jnp.minimum((i + 1) * hb, last8), j)),
                  pl.BlockSpec((SUB, cw), lambda j, i: (0, j))],
        out_specs=pl.BlockSpec((rt, cw), lambda j, i: (i, j)),
        out_shape=jax.ShapeDtypeStruct((tp, cols), bf16), compiler_params=_params(2),
    )(dy, dy, w8)


def _ab_common(p, al, dtb, r0):
    rows = r0 + lax.broadcasted_iota(jnp.int32, p.shape, 0)
    mask = (rows >= P0).astype(f32)
    xx = p + dtb
    sp = jnp.maximum(xx, 0.0) + _log1p_small(jnp.exp(-jnp.abs(xx)))
    ea = jnp.exp(al)
    g = -ea * sp * mask
    beta = _sigmoid(p) * mask
    return g, beta, _sigmoid(xx), ea, mask


def _chunk_tri(rt, later):
    r = lax.broadcasted_iota(jnp.int32, (rt, rt), 0)
    c = lax.broadcasted_iota(jnp.int32, (rt, rt), 1)
    shift = CHUNK.bit_length() - 1
    same = jnp.right_shift(r, shift) == jnp.right_shift(c, shift)
    return jnp.logical_and(same, c >= r if later else c <= r).astype(f32)


def ab_fwd(pab, al, dtb):
    tp = pab.shape[0]
    rt = _row_tile(tp)
    assert rt % CHUNK == 0

    def body(p_ref, al_ref, dt_ref, g_ref, b_ref):
        i = pl.program_id(0)
        g, beta, _, _, _ = _ab_common(p_ref[...], al_ref[...], dt_ref[...], i * rt)
        gam = _hdot(_chunk_tri(rt, False), g)
        for h in range(HEADS):
            g_ref[h] = jnp.broadcast_to(gam[:, h:h + 1], (rt, LANE))
            b_ref[h] = jnp.broadcast_to(beta[:, HEADS + h:HEADS + h + 1], (rt, LANE))

    vec = pl.BlockSpec((1, LANE), lambda i: (0, 0))
    out = pl.BlockSpec((HEADS, rt, LANE), lambda i: (0, i, 0))
    return pl.pallas_call(
        body, name="ab_fwd", grid=(tp // rt,), in_specs=[pl.BlockSpec((rt, LANE), lambda i: (i, 0)), vec, vec],
        out_specs=[out, out], out_shape=[jax.ShapeDtypeStruct((HEADS, tp, LANE), f32)] * 2, compiler_params=_params(1),
    )(pab, al, dtb)


def ab_bwd(pab, al, dtb, dg, db):
    tp = pab.shape[0]
    rt = _row_tile(tp)

    def body(p_ref, al_ref, dt_ref, dg_ref, db_ref, dp_ref, dal_ref, ddt_ref):
        i = pl.program_id(0)
        g, beta, sx, ea, mask = _ab_common(p_ref[...], al_ref[...], dt_ref[...], i * rt)
        lanes = lax.broadcasted_iota(jnp.int32, (rt, LANE), 1)
        dgl = jnp.zeros((rt, LANE), f32)
        dbl = jnp.zeros((rt, LANE), f32)
        for h in range(HEADS):
            dgl = dgl + jnp.where(lanes == h, dg_ref[h], 0.0)
            dbl = dbl + jnp.where(lanes == HEADS + h, db_ref[h], 0.0)
        dgl = _hdot(_chunk_tri(rt, True), dgl)
        dxx = dgl * (-ea) * sx * mask
        dp_ref[...] = (dxx + dbl * beta * (1.0 - beta)).astype(bf16)
        pal = jnp.sum(dgl * g, axis=0, keepdims=True)
        pdt = jnp.sum(dxx, axis=0, keepdims=True)

        @pl.when(i == 0)
        def _():
            dal_ref[...] = pal
            ddt_ref[...] = pdt

        @pl.when(i > 0)
        def _():
            dal_ref[...] += pal
            ddt_ref[...] += pdt

    vec = pl.BlockSpec((1, LANE), lambda i: (0, 0))
    row = pl.BlockSpec((rt, LANE), lambda i: (i, 0))
    big = pl.BlockSpec((HEADS, rt, LANE), lambda i: (0, i, 0))
    return pl.pallas_call(
        body, name="ab_bwd", grid=(tp // rt,), in_specs=[row, vec, vec, big, big], out_specs=[row, vec, vec],
        out_shape=[jax.ShapeDtypeStruct((tp, LANE), bf16), jax.ShapeDtypeStruct((1, LANE), f32),
                   jax.ShapeDtypeStruct((1, LANE), f32)],
        compiler_params=_params(1),
    )(pab, al, dtb, dg, db)


class _Chunk:
    pass


def _gdn_chunk(q, k, v, gcol, bcol, grow8):
    C = CHUNK
    R = range(len(q))
    X = _Chunk()
    ri = lax.broadcasted_iota(jnp.int32, (C, C), 0)
    ci = lax.broadcasted_iota(jnp.int32, (C, C), 1)
    eye = (ri == ci).astype(f32)
    gam = list(gcol)
    gam_row = [grow8[h][0:1, 0:C] for h in R]
    X.ri, X.ci = ri, ci
    X.Dm = [jnp.where(ri >= ci, jnp.exp(jnp.minimum(gam[h][:, 0:C] - gam_row[h], 0.0)), 0.0) for h in R]
    X.eg = [jnp.exp(gam[h]) for h in R]
    gl = [gam[h][C - 1:C, :] for h in R]
    X.egl = [jnp.exp(gl[h]) for h in R]
    X.kdec = [jnp.exp(gl[h] - gam[h]) for h in R]
    X.qs = [q[h] * (DN_DK ** -0.5) for h in R]
    X.kb = [k[h] * bcol[h] for h in R]
    kk = [_dot_nt(X.kb[h], k[h]) for h in R]
    qk = [_dot_nt(X.qs[h], k[h]) for h in R]
    X.A = [jnp.where(ri > ci, kk[h] * X.Dm[h], 0.0) for h in R]
    T = [eye - X.A[h] for h in R]
    P = list(X.A)
    for _ in range(5):
        P = [_hdot(P[h], P[h]) for h in R]
        T = [T[h] + _hdot(T[h], P[h]) for h in R]
    X.T = T
    X.b2 = [jnp.concatenate([bcol[h], bcol[h]], axis=-1) for h in R]
    X.u = [_hdot(T[h], v[h] * X.b2[h]) for h in R]
    X.w = [_hdot(T[h], X.kb[h] * X.eg[h]) for h in R]
    X.attn = [qk[h] * X.Dm[h] for h in R]
    X.qg = [X.qs[h] * X.eg[h] for h in R]
    X.kg = [k[h] * X.kdec[h] for h in R]
    return X


def gdn_fwd(q, k, v, gc, bc, grow):
    tp = q.shape[0]
    nc = tp // CHUNK
    hb = GDN_HEADS_PER_STEP

    def body(q_ref, k_ref, v_ref, gc_ref, bc_ref, gr_ref, o_ref, ss_ref, S_ref):
        c = pl.program_id(1)

        @pl.when(c == 0)
        def _():
            S_ref[...] = jnp.zeros_like(S_ref)

        R = range(hb)
        qc = [slice(h * DN_DK, (h + 1) * DN_DK) for h in R]
        vc = [slice(h * DN_DV, (h + 1) * DN_DV) for h in R]
        X = _gdn_chunk([q_ref[:, qc[h]] for h in R], [k_ref[:, qc[h]] for h in R], [v_ref[:, vc[h]] for h in R],
                       [gc_ref[h] for h in R], [bc_ref[h] for h in R], [gr_ref[h] for h in R])
        S = [S_ref[h] for h in R]
        for h in R:
            ss_ref[h, 0] = S[h]
        wS = [_dot(X.w[h], S[h]) for h in R]
        qS = [_dot(X.qg[h], S[h]) for h in R]
        vn = [X.u[h] - wS[h] for h in R]
        av = [_dot(X.attn[h], vn[h]) for h in R]
        kv = [_dot_tn(X.kg[h], vn[h]) for h in R]
        for h in R:
            o_ref[:, vc[h]] = qS[h] + av[h]
            S_ref[h] = S[h] * X.egl[h][:, 0:1] + kv[h]

    qk = pl.BlockSpec((CHUNK, hb * DN_DK), lambda g, c: (c, g))
    vv = pl.BlockSpec((CHUNK, hb * DN_DV), lambda g, c: (c, g))
    col = pl.BlockSpec((hb, CHUNK, LANE), lambda g, c: (g, c, 0))
    row = pl.BlockSpec((hb, SUB, LANE), lambda g, c: (g, c, 0))
    return pl.pallas_call(
        body, name="gdn_fwd", grid=(HEADS // hb, nc), in_specs=[qk, qk, vv, col, col, row],
        out_specs=[vv, pl.BlockSpec((hb, 1, DN_DK, DN_DV), lambda g, c: (g, c, 0, 0))],
        out_shape=[jax.ShapeDtypeStruct((tp, DN_V), f32), jax.ShapeDtypeStruct((HEADS, nc, DN_DK, DN_DV), f32)],
        scratch_shapes=[pltpu.VMEM((hb, DN_DK, DN_DV), f32)], compiler_params=_params(2),
    )(q, k, v, gc, bc, grow)


def gdn_bwd(q, k, v, gc, bc, grow, states, do, rider=None):
    tp = q.shape[0]
    nc = tp // CHUNK
    C = CHUNK
    hb = GDN_HEADS_PER_STEP
    grid = (HEADS // hb, nc)
    split, ride_first, ride_last = _ride(rider, 8, 5, grid)

    def body(*refs):
        ((q_ref, k_ref, v_ref, gc_ref, bc_ref, gr_ref, ss_ref, do_ref), (dq_ref, dk_ref, dv_ref, dg_ref, db_ref),
         (rin, rout, rest)) = split(refs)
        dS_ref, ride = rest[0], (rin, rout, rest[1:])
        ride_first(ride)
        c = pl.program_id(1)

        @pl.when(c == 0)
        def _():
            dS_ref[...] = jnp.zeros_like(dS_ref)

        R = range(hb)
        qc = [slice(h * DN_DK, (h + 1) * DN_DK) for h in R]
        vc = [slice(h * DN_DV, (h + 1) * DN_DV) for h in R]
        k_ = [k_ref[:, qc[h]] for h in R]
        v_ = [v_ref[:, vc[h]] for h in R]
        bcol = [bc_ref[h] for h in R]
        X = _gdn_chunk([q_ref[:, qc[h]] for h in R], k_, v_, [gc_ref[h] for h in R], bcol, [gr_ref[h] for h in R])
        ri, ci = X.ri, X.ci
        S = [ss_ref[h, 0] for h in R]
        do_ = [do_ref[:, vc[h]] for h in R]
        dSn = [dS_ref[h] for h in R]
        wS = [_dot(X.w[h], S[h]) for h in R]
        ado = [_dot_tn(X.attn[h], do_[h]) for h in R]
        kdS = [_dot(X.kg[h], dSn[h]) for h in R]
        d_qg = [_dot_nt(do_[h], S[h]) for h in R]
        qdo = [_dot_tn(X.qg[h], do_[h]) for h in R]
        vn = [X.u[h] - wS[h] for h in R]
        d_vn = [ado[h] + kdS[h] for h in R]
        dovn = [_dot_nt(do_[h], vn[h]) for h in R]
        d_kg = [_dot_nt(vn[h], dSn[h]) for h in R]
        wdv = [_dot_tn(X.w[h], d_vn[h]) for h in R]
        dw = [-_dot_nt(d_vn[h], S[h]) for h in R]
        for h in R:
            dS_ref[h] = qdo[h] + X.egl[h][:, 0:1] * dSn[h] - wdv[h]
        dattn = [jnp.where(ri >= ci, dovn[h], 0.0) for h in R]
        dRu = [_hdot_tn(X.T[h], d_vn[h]) for h in R]
        dRw = [_hdot_tn(X.T[h], dw[h]) for h in R]
        dAu = [_hdot_nt(dRu[h], X.u[h]) for h in R]
        dAw = [_hdot_nt(dRw[h], X.w[h]) for h in R]
        dA = [jnp.where(ri > ci, -(dAu[h] + dAw[h]), 0.0) for h in R]
        dKK = [dA[h] * X.Dm[h] for h in R]
        dQK = [dattn[h] * X.Dm[h] for h in R]
        E = [dA[h] * X.A[h] + dattn[h] * X.attn[h] for h in R]
        dkb = [_dot(dKK[h], k_[h]) + dRw[h] * X.eg[h] for h in R]
        dk1 = [_dot_tn(dKK[h], X.kb[h]) for h in R]
        dqs = [_dot(dQK[h], k_[h]) + d_qg[h] * X.eg[h] for h in R]
        dk2 = [_dot_tn(dQK[h], X.qs[h]) for h in R]
        ones = jnp.ones((C, LANE), f32)
        colE = [_hdot_tn(E[h], ones) for h in R]
        rows = lax.broadcasted_iota(jnp.int32, (C, LANE), 0)
        dgam = []
        for h in R:
            t = d_kg[h] * X.kg[h]
            dgl = _allsum(t) + X.egl[h][:, 0:1] * _allsum(S[h] * dSn[h])
            g = (_rowsum(E[h]) - colE[h] + _rowsum(dRw[h] * (X.kb[h] * X.eg[h])) + _rowsum(d_qg[h] * X.qg[h])
                 - _rowsum(t))
            dgam.append(g + jnp.where(rows == C - 1, dgl, 0.0))
        for h in R:
            dv_ref[:, vc[h]] = dRu[h] * X.b2[h]
            dbeta = _rowsum(dRu[h] * v_[h]) + _rowsum(dkb[h] * k_[h])
            dq_ref[:, qc[h]] = dqs[h] * (DN_DK ** -0.5)
            dk_ref[:, qc[h]] = dk1[h] + dk2[h] + dkb[h] * bcol[h] + d_kg[h] * X.kdec[h]
            dg_ref[h] = dgam[h]
            db_ref[h] = jnp.broadcast_to(dbeta, (C, LANE))
        ride_last(ride)

    rc = lambda c: nc - 1 - c
    qk = pl.BlockSpec((CHUNK, hb * DN_DK), lambda g, c: (rc(c), g))
    vv = pl.BlockSpec((CHUNK, hb * DN_DV), lambda g, c: (rc(c), g))
    col = pl.BlockSpec((hb, CHUNK, LANE), lambda g, c: (g, rc(c), 0))
    row = pl.BlockSpec((hb, SUB, LANE), lambda g, c: (g, rc(c), 0))
    st = pl.BlockSpec((hb, 1, DN_DK, DN_DV), lambda g, c: (g, rc(c), 0, 0))
    r_ins = rider.ins if rider else []
    r_outs = rider.out_shapes if rider else []
    res = pl.pallas_call(
        body, name="gdn_bwd", grid=grid, in_specs=[qk, qk, vv, col, col, row, st, vv] + [_ANY] * len(r_ins),
        out_specs=[qk, qk, vv, col, col] + [_ANY] * len(r_outs),
        out_shape=[jax.ShapeDtypeStruct((tp, DN_QK), f32), jax.ShapeDtypeStruct((tp, DN_QK), f32),
                   jax.ShapeDtypeStruct((tp, DN_V), f32), jax.ShapeDtypeStruct((HEADS, tp, LANE), f32),
                   jax.ShapeDtypeStruct((HEADS, tp, LANE), f32)] + list(r_outs),
        scratch_shapes=[pltpu.VMEM((hb, DN_DK, DN_DV), f32)] + (rider.scratch if rider else []),
        compiler_params=_params(2),
    )(q, k, v, gc, bc, grow, states, do, *r_ins)
    return res[:5], res[5:]


def _cumsum_after(x, nb, us):
    B, n = SB_BLOCK, x.shape[0]
    hi = x.astype(bf16)
    lo = (x - hi.astype(f32)).astype(bf16)
    rows = [p[:, b * B:(b + 1) * B] for p in (hi, lo) for b in range(nb)]
    r = jnp.dot(jnp.concatenate(rows, axis=0), us, preferred_element_type=f32)
    out = [r[b * n:(b + 1) * n] + r[(nb + b) * n:(nb + b + 1) * n] for b in range(nb)]
    return out[0] if nb == 1 else jnp.concatenate(out, axis=1)


def _later_blocks(x, nb, carry):
    B = SB_BLOCK
    tot = [_rowsum(x[:, b * B:(b + 1) * B]) for b in range(nb)]
    offs = [None] * nb
    run = carry
    for b in range(nb - 1, -1, -1):
        offs[b] = jnp.broadcast_to(run, (x.shape[0], B))
        run = run + tot[b]
    return (offs[0] if nb == 1 else jnp.concatenate(offs, axis=1)), run


def _sb_group(i, t):
    top = (i + 1) * (SB_QB // SB_BLOCK) - 1 - SB_GROUP * t
    jlo = jnp.maximum(top - SB_GROUP + 1, 0)
    rows = pl.ds(pl.multiple_of(jlo * SB_BLOCK, SB_BLOCK), SB_GROUP * SB_BLOCK)
    return jlo, rows, (top + 1) * SB_BLOCK


def _sb_weights(q, kcat, i, jlo, kend, cs, us, masked):
    B, nb = SB_BLOCK, SB_GROUP
    R = range(len(q))
    z = [_dot_nt(q[h], kcat[h]) * (SB_DH ** -0.5) for h in R]
    e = [jnp.exp(-jnp.abs(z[h])) for h in R]
    l1p = [jnp.log(1.0 + e[h]) for h in R]
    lsp = [jnp.minimum(z[h], 0.0) - l1p[h] for h in R]
    lk = [lsp[h] - z[h] for h in R]
    vis = None
    if masked:
        qpos = i * SB_QB + lax.broadcasted_iota(jnp.int32, (SB_QB, nb * B), 0)
        kpos = jlo * B + lax.broadcasted_iota(jnp.int32, (SB_QB, nb * B), 1)
        vis = jnp.logical_and(kpos < jnp.minimum(qpos, kend), kpos >= P0)
        lk = [jnp.where(vis, lk[h], 0.0) for h in R]
    later = [_later_blocks(lk[h], nb, cs[h]) for h in R]
    cum = [_cumsum_after(lk[h], nb, us) for h in R]
    w = [jnp.exp(lsp[h] + cum[h] + later[h][0]) for h in R]
    if masked:
        w = [jnp.where(vis, w[h], 0.0) for h in R]
    return lsp, vis, w, [later[h][1] for h in R]


def _sb_loop(i, step, carry):
    trips = ((i + 1) * (SB_QB // SB_BLOCK) - 1 + SB_GROUP) // SB_GROUP
    carry = step(True)(0, carry)
    carry = lax.fori_loop(1, trips - 1, step(False), carry)
    return lax.fori_loop(jnp.maximum(trips - 1, 1), trips, step(True), carry)


def _ride(rider, n_in, n_out, grid):
    n_rin = len(rider.ins) if rider else 0
    n_rout = len(rider.out_shapes) if rider else 0

    def split(refs):
        ins, rin = refs[:n_in], refs[n_in:n_in + n_rin]
        outs = refs[n_in + n_rin:n_in + n_rin + n_out]
        rout = refs[n_in + n_rin + n_out:n_in + n_rin + n_out + n_rout]
        return ins, outs, (rin, rout, refs[n_in + n_rin + n_out + n_rout:])

    def at(step, fn, r):
        if rider is None:
            return
        cond = None
        for a, g in enumerate(grid):
            c = pl.program_id(a) == (g - 1 if step == "last" else 0)
            cond = c if cond is None else jnp.logical_and(cond, c)

        @pl.when(cond)
        def _():
            fn(*r)

    first = lambda r: at("first", rider.start if rider else None, r)
    last = lambda r: at("last", rider.finish if rider else None, r)
    return split, first, last


def sb_fwd(qs, ks, vs, rider=None):
    tp = qs.shape[0]
    nq = tp // SB_QB
    B, G, hb, QB = SB_BLOCK, SB_GROUP, SB_FWD_HEADS_PER_STEP, SB_QB
    assert tp >= G * B and tp % QB == 0 and QB % B == 0 and G * B >= QB
    grid = (HEADS // hb, nq)
    split, ride_first, ride_last = _ride(rider, 3, 2, grid)

    def body(*refs):
        (q_ref, k_ref, v_ref), (o_ref, ob_ref), ride = split(refs)
        ride_first(ride)
        i = pl.program_id(1)
        R = range(hb)
        hs = [slice(h * SB_DH, (h + 1) * SB_DH) for h in R]
        q = [q_ref[:, hs[h]] for h in R]
        us = (lax.broadcasted_iota(jnp.int32, (B, B), 0) > lax.broadcasted_iota(jnp.int32, (B, B), 1)).astype(bf16)

        def make_step(masked):
            def step(t, carry):
                acc, cs = carry
                jlo, rows, kend = _sb_group(i, t)
                _, _, w, cs = _sb_weights(q, [k_ref[rows, hs[h]] for h in R], i, jlo, kend, cs, us, masked)
                pv = [_dot(w[h], v_ref[rows, hs[h]]) for h in R]
                return tuple(acc[h] + pv[h] for h in R), tuple(cs)
            return step

        carry = (tuple(jnp.zeros((QB, SB_DH), f32) for _ in R), tuple(jnp.zeros((QB, 1), f32) for _ in R))
        acc, _ = _sb_loop(i, make_step, carry)
        for h in R:
            o_ref[:, hs[h]] = acc[h]
            ob_ref[:, hs[h]] = acc[h].astype(bf16)
        ride_last(ride)

    blk = pl.BlockSpec((QB, hb * SB_DH), lambda g, i: (i, g))
    full = pl.BlockSpec((tp, hb * SB_DH), lambda g, i: (0, g))
    r_ins = rider.ins if rider else []
    r_outs = rider.out_shapes if rider else []
    res = pl.pallas_call(
        body, name="sb_fwd", grid=grid, in_specs=[blk, full, full] + [_ANY] * len(r_ins),
        out_specs=[blk, blk] + [_ANY] * len(r_outs),
        out_shape=[jax.ShapeDtypeStruct((tp, SB_W), f32), jax.ShapeDtypeStruct((tp, SB_W), bf16)] + list(r_outs),
        scratch_shapes=rider.scratch if rider else [], compiler_params=_params(2),
    )(qs, ks, vs, *r_ins)
    return res[0], res[1], res[2:]


def sb_bwd(qs, ks, vs, o, do, rider=None):
    tp = qs.shape[0]
    nq = tp // SB_QB
    B, G, hb, QB = SB_BLOCK, SB_GROUP, SB_HEADS_PER_STEP, SB_QB
    assert tp >= G * B and tp % QB == 0 and QB % B == 0 and G * B >= QB
    grid = (HEADS // hb, nq)
    split, ride_first, ride_last = _ride(rider, 5, 3, grid)

    def body(*refs):
        (q_ref, k_ref, v_ref, o_ref, do_ref), (dq_ref, dk_ref, dv_ref), ride = split(refs)
        ride_first(ride)
        i = pl.program_id(1)

        @pl.when(i == 0)
        def _():
            dk_ref[...] = jnp.zeros_like(dk_ref)
            dv_ref[...] = jnp.zeros_like(dv_ref)

        R = range(hb)
        hs = [slice(h * SB_DH, (h + 1) * SB_DH) for h in R]
        q = [q_ref[:, hs[h]] for h in R]
        dob = [do_ref[:, hs[h]].astype(bf16) for h in R]
        et = [_rowsum(dob[h].astype(f32) * o_ref[:, hs[h]]) for h in R]
        us = (lax.broadcasted_iota(jnp.int32, (B, B), 0) > lax.broadcasted_iota(jnp.int32, (B, B), 1)).astype(bf16)

        def make_step(masked):
            def step(t, carry):
                dq, cs, ce = carry
                jlo, rows, kend = _sb_group(i, t)
                kcat = [k_ref[rows, hs[h]] for h in R]
                dwv = [_dot_nt(dob[h], v_ref[rows, hs[h]]) for h in R]
                lsp, vis, w, cs = _sb_weights(q, kcat, i, jlo, kend, cs, us, masked)
                wb = [w[h].astype(bf16) for h in R]
                ee = [dwv[h] * wb[h].astype(f32) for h in R]
                later = [_later_blocks(ee[h], G, ce[h]) for h in R]
                cum = [_cumsum_after(ee[h], G, us) for h in R]
                dz = []
                for h in R:
                    d = ee[h] - jnp.exp(lsp[h]) * (et[h] - (cum[h] + later[h][0]))
                    if masked:
                        d = jnp.where(vis, d, 0.0)
                    dz.append((d * (SB_DH ** -0.5)).astype(bf16))
                dkj = [_dot_tn(dz[h], q[h]) for h in R]
                dvj = [_dot_tn(wb[h], dob[h]) for h in R]
                dqj = [_dot(dz[h], kcat[h]) for h in R]
                for h in R:
                    dk_ref[rows, hs[h]] += dkj[h]
                    dv_ref[rows, hs[h]] += dvj[h]
                return tuple(dq[h] + dqj[h] for h in R), tuple(cs), tuple(later[h][1] for h in R)
            return step

        z0 = tuple(jnp.zeros((QB, 1), f32) for _ in R)
        dq, _, _ = _sb_loop(i, make_step, (tuple(jnp.zeros((QB, SB_DH), f32) for _ in R), z0, z0))
        for h in R:
            dq_ref[:, hs[h]] = dq[h]
        ride_last(ride)

    blk = pl.BlockSpec((QB, hb * SB_DH), lambda g, i: (i, g))
    full = pl.BlockSpec((tp, hb * SB_DH), lambda g, i: (0, g))
    r_ins = rider.ins if rider else []
    r_outs = rider.out_shapes if rider else []
    res = pl.pallas_call(
        body, name="sb_bwd", grid=grid, in_specs=[blk, full, full, blk, blk] + [_ANY] * len(r_ins),
        out_specs=[blk, full, full] + [_ANY] * len(r_outs),
        out_shape=[jax.ShapeDtypeStruct((tp, SB_W), f32)] * 3 + list(r_outs),
        scratch_shapes=rider.scratch if rider else [], compiler_params=_params(2),
    )(qs, ks, vs, o, do, *r_ins)
    return res[:3], res[3:]


def adamw(w, g, m, v, name):
    r, c = w.shape
    rt = _tile(r, 128, SUB) if r % SUB == 0 else r
    blk = pl.BlockSpec((rt, c), lambda i: (i, 0))
    c1 =1.0 - ADAM_B1 ** ADAM_STEP
    c2 = 1.0 - ADAM_B2 ** ADAM_STEP

    def body(w_ref, g_ref, m_ref, v_ref, d_ref, mo_ref, vo_ref):
        g_ = g_ref[...]
        m_ = ADAM_B1 * m_ref[...] + (1.0 - ADAM_B1) * g_
        v_ = ADAM_B2 * v_ref[...] + (1.0 - ADAM_B2) * (g_ * g_)
        mo_ref[...] = m_
        vo_ref[...] = v_
        d_ref[...] = -ADAM_LR * ((m_ / c1) / (jnp.sqrt(v_ / c2) + ADAM_EPS) + ADAM_WD * w_ref[...])

    return pl.pallas_call(
        body, name=name, grid=(r // rt,), in_specs=[blk] * 4, out_specs=[blk] * 3,
        out_shape=[jax.ShapeDtypeStruct((r, c), f32)] * 3, compiler_params=_params(1),
    )(w, g, m, v)


def sum_slots(x, name):
    n, r, c = x.shape
    rt = _tile(r, 128, SUB) if r % SUB == 0 else r
    blk = pl.BlockSpec((n, rt, c), lambda i: (0, i, 0))

    def body(x_ref, o_ref):
        acc = x_ref[0].astype(f32)
        for s in range(1, n):
            acc = acc + x_ref[s].astype(f32)
        o_ref[...] = acc

    return pl.pallas_call(
        body, name=name, grid=(r // rt,), in_specs=[blk], out_specs=pl.BlockSpec((rt, c), lambda i: (i, 0)),
        out_shape=jax.ShapeDtypeStruct((r, c), f32), compiler_params=_params(1),
    )(x)


def add2(a, b, name, out_dtype=f32):
    n, r, c = a.shape
    rt = _tile(r, 64, SUB) if r % SUB == 0 else r
    blk = pl.BlockSpec((n, rt, c), lambda i: (0, i, 0))

    def body(a_ref, b_ref, o_ref):
        o_ref[...] = (a_ref[...] + b_ref[...]).astype(out_dtype)

    return pl.pallas_call(
        body, name=name, grid=(r // rt,), in_specs=[blk, blk], out_specs=blk,
        out_shape=jax.ShapeDtypeStruct((n, r, c), out_dtype), compiler_params=_params(1),
    )(a, b)


_ANY = pl.BlockSpec(memory_space=pl.ANY)
_MESH = pl.DeviceIdType.MESH


def _coords():
    return lax.axis_index("x"), lax.axis_index("y"), lax.axis_index("c")


def _chip_peer(x, y, r):
    return x ^ (r >> 1), y ^ (r & 1)


class _Exchange:
    def __init__(self, ins, out_shapes, scratch, start, finish):
        self.ins, self.out_shapes, self.scratch, self.start, self.finish = ins, out_shapes, scratch, start, finish

    def split(self, refs):
        n, m = len(self.ins), len(self.out_shapes)
        return refs[:n], refs[n:n + m], refs[n + m:]


def run_exchange(ex, name):
    def body(*refs):
        ins, outs, sems = ex.split(refs)
        ex.start(ins, outs, sems)
        ex.finish(ins, outs, sems)

    return pl.pallas_call(body, name=name, in_specs=[_ANY] * len(ex.ins), out_specs=[_ANY] * len(ex.out_shapes),
                          out_shape=ex.out_shapes, scratch_shapes=ex.scratch)(*ex.ins)


def gather_chips(big, small):
    nb, n = len(big), len(big) + len(small)
    shards = list(big) + list(small)
    kb = nb * (N_CHIPS - 1)
    k = n * (N_CHIPS - 1)

    def copies(src, dst, sems):
        send, recv, fsend, frecv = sems
        x, y, c = _coords()
        sib = (x, y, 1 - c)
        peers = [_chip_peer(x, y, r) for r in range(1, N_CHIPS)]

        def direct(t, j, slot):
            s = t * (N_CHIPS - 1) + j
            if t < nb:
                return pltpu.make_async_remote_copy(src[t].at[c], dst[t].at[slot, c], send.at[s], recv.at[s],
                                                    device_id=(*peers[j], c), device_id_type=_MESH)
            return pltpu.make_async_remote_copy(src[t], dst[t].at[slot], send.at[s], recv.at[s],
                                                device_id=(*peers[j], c), device_id_type=_MESH)

        def passed(t, j, half):
            s = t * (N_CHIPS - 1) + j
            px, py = peers[j]
            part = dst[t].at[2 * px + py, half]
            return pltpu.make_async_remote_copy(part, part, fsend.at[s], frecv.at[s], device_id=sib, device_id_type=_MESH)

        return direct, passed, peers, 2 * x + y, c

    def start(src, dst, sems):
        direct, _, _, me, _ = copies(src, dst, sems)
        for t in range(n):
            for j in range(N_CHIPS - 1):
                direct(t, j, me).start()

    def finish(src, dst, sems):
        direct, passed, peers, me, c = copies(src, dst, sems)
        fwd = []
        for t in range(nb):
            for j in range(N_CHIPS - 1):
                px, py = peers[j]
                direct(t, j, 2 * px + py).wait_recv()
                fwd.append(passed(t, j, c))
                fwd[-1].start()
        for t in range(nb, n):
            for j in range(N_CHIPS - 1):
                px, py = peers[j]
                direct(t, j, 2 * px + py).wait_recv()
        for t in range(nb):
            for j in range(N_CHIPS - 1):
                passed(t, j, 1 - c).wait_recv()
        for t in range(n):
            for j in range(N_CHIPS - 1):
                direct(t, j, me).wait_send()
        for cp in fwd:
            cp.wait_send()

    return _Exchange(shards, [jax.ShapeDtypeStruct((N_CHIPS,) + s.shape, s.dtype) for s in shards],
                     [pltpu.SemaphoreType.DMA((k,)), pltpu.SemaphoreType.DMA((k,)),
                      pltpu.SemaphoreType.DMA((max(kb, 1),)), pltpu.SemaphoreType.DMA((max(kb, 1),))], start, finish)


def sibling_swap(grads):
    n = len(grads)
    k = n * N_CHIPS

    def copies(src, dst, sems):
        send, recv = sems
        x, y, c = _coords()
        return [pltpu.make_async_remote_copy(src[t].at[o, 1 - c], dst[t].at[o], send.at[t * N_CHIPS + o],
                                             recv.at[t * N_CHIPS + o], device_id=(x, y, 1 - c), device_id_type=_MESH)
                for t in range(n) for o in range(N_CHIPS)]

    def start(src, dst, sems):
        for cp in copies(src, dst, sems):
            cp.start()

    def finish(src, dst, sems):
        cps = copies(src, dst, sems)
        for cp in cps:
            cp.wait_recv()
        for cp in cps:
            cp.wait_send()

    return _Exchange(list(grads), [jax.ShapeDtypeStruct((N_CHIPS,) + g.shape[2:], g.dtype) for g in grads],
                     [pltpu.SemaphoreType.DMA((k,)), pltpu.SemaphoreType.DMA((k,))], start, finish)


def scatter_chips(parts):
    n = len(parts)
    k = n * (N_CHIPS - 1)

    def copy(src, dst, sems, t, r, landing):
        send, recv = sems
        x, y, c = _coords()
        me = 2 * x + y
        px, py = _chip_peer(x, y, r)
        peer = 2 * px + py
        s = t * (N_CHIPS - 1) + r - 1
        return pltpu.make_async_remote_copy(src[t].at[me if landing else peer], dst[t].at[peer if landing else me],
                                            send.at[s], recv.at[s], device_id=(px, py, c), device_id_type=_MESH)

    def start(src, dst, sems):
        for t in range(n):
            for r in range(1, N_CHIPS):
                copy(src, dst, sems, t, r, False).start()

    def finish(src, dst, sems):
        for t in range(n):
            for r in range(1, N_CHIPS):
                copy(src, dst, sems, t, r, True).wait_recv()
        for t in range(n):
            for r in range(1, N_CHIPS):
                copy(src, dst, sems, t, r, False).wait_send()

    return _Exchange(list(parts), [jax.ShapeDtypeStruct(p.shape, p.dtype) for p in parts],
                     [pltpu.SemaphoreType.DMA((k,)), pltpu.SemaphoreType.DMA((k,))], start, finish)


def sibling_send(halves, name):
    n = len(halves)

    def body(*refs):
        src, dst = refs[:n], refs[n:2 * n]
        send, recv = refs[2 * n:]
        x, y, c = _coords()
        cps = [pltpu.make_async_remote_copy(src[t], dst[t], send.at[t], recv.at[t],
                                            device_id=(x, y, 1 - c), device_id_type=_MESH) for t in range(n)]
        for cp in cps:
            cp.start()
        for cp in cps:
            cp.wait_recv()
        for cp in cps:
            cp.wait_send()

    return pl.pallas_call(
        body, name=name, in_specs=[_ANY] * n, out_specs=[_ANY] * n,
        out_shape=[jax.ShapeDtypeStruct(h.shape, h.dtype) for h in halves],
        scratch_shapes=[pltpu.SemaphoreType.DMA((n,)), pltpu.SemaphoreType.DMA((n,))],
    )(*halves)


def gather_all(block, name):
    def body(src, dst, send, recv, loc):
        x, y, c = _coords()
        me = 4 * x + 2 * y + c
        mine = pltpu.make_async_copy(src, dst.at[me], loc)
        mine.start()
        outs = []
        for r in range(1, N_DEV):
            peer = (x ^ (r >> 2), y ^ ((r >> 1) & 1), c ^ (r & 1))
            outs.append(pltpu.make_async_remote_copy(src, dst.at[me], send.at[r - 1], recv.at[r - 1],
                                                     device_id=peer, device_id_type=_MESH))
        for cp in outs:
            cp.start()
        for r in range(1, N_DEV):
            px, py, pc = x ^ (r >> 2), y ^ ((r >> 1) & 1), c ^ (r & 1)
            pltpu.make_async_remote_copy(src, dst.at[4 * px + 2 * py + pc], send.at[r - 1], recv.at[r - 1],
                                         device_id=(px, py, pc), device_id_type=_MESH).wait_recv()
        for cp in outs:
            cp.wait_send()
        mine.wait()

    return pl.pallas_call(
        body, name=name, in_specs=[_ANY], out_specs=_ANY,
        out_shape=jax.ShapeDtypeStruct((N_DEV,) + block.shape, block.dtype),
        scratch_shapes=[pltpu.SemaphoreType.DMA((N_DEV - 1,)), pltpu.SemaphoreType.DMA((N_DEV - 1,)),
                        pltpu.SemaphoreType.DMA(())],
    )(block)


def _pad_lanes(v, n=LANE):
    return jnp.pad(v, ((0, 0), (0, n - v.shape[1])))


def _w_in_pieces():
    cs = (PROJ_BIG + 2 * HEADS) // N_CHIPS
    ab_end = AB_COL + 2 * HEADS
    out = []
    for o in range(N_CHIPS):
        lo, hi = o * cs, (o + 1) * cs
        cand = [("big", lo, min(hi, AB_COL), 0), ("ab", max(lo, AB_COL), min(hi, ab_end), AB_COL),
                ("big", max(lo, ab_end), hi, 2 * HEADS)]
        out.append([(s, a - off, b - off) for s, a, b, off in cand if a < b])
    return out


def _split_w_in(w4):
    big, ab = [], []
    for o, pieces in enumerate(_w_in_pieces()):
        at = 0
        for s, a, b in pieces:
            (big if s == "big" else ab).append(w4[o][:, at:at + b - a])
            at += b - a
    return jnp.concatenate(big, axis=1), _pad_lanes(jnp.concatenate(ab, axis=1))


def _join_w_in(big, ab):
    src = {"big": big, "ab": ab}
    return jnp.stack([jnp.concatenate([src[s][:, a:b] for s, a, b in pieces], axis=1) for pieces in _w_in_pieces()])


def _conv_w8(w):
    return jnp.pad(w, ((0, SUB - DN_CONV), (0, 0)))


def _row_layout(gc, tp):
    nc = tp // CHUNK
    g = gc[:, :, 0].reshape(HEADS, nc, 1, CHUNK)
    g = jnp.broadcast_to(g, (HEADS, nc, SUB, CHUNK))
    return jnp.pad(g, ((0, 0), (0, 0), (0, 0), (0, LANE - CHUNK))).reshape(HEADS, nc * SUB, LANE)


def _step(x, meta, W, target, late_weights=None, early_swap=None, early_grads=None, last_grads=None):
    W = dict(W)
    seq = x.shape[0]
    tp = P0 + N_META + seq
    h0 = jnp.concatenate([jnp.zeros((P0, D_MODEL), f32), meta, x], axis=0)
    w_big, w_ab = _split_w_in(W["w_in"])
    cq8, ck8, cv8 = _conv_w8(W["conv_q"]), _conv_w8(W["conv_k"]), _conv_w8(W["conv_v"])
    al, dtb = _pad_lanes(W["dn_a_log"]), _pad_lanes(W["dn_dt_bias"])

    n1 = rms_fwd(h0, W["norm_mix_gain"], "rms1_fwd")
    proj = matmul(n1, w_big, "nn", "proj_fwd")
    pab = matmul(n1, w_ab, "nn", "pab_fwd")
    qn = conv_fwd(proj, cq8, C_DQ * 8, 8, True, "conv_q_fwd")
    kn = conv_fwd(proj, ck8, C_DK * 8, 8, True, "conv_k_fwd")
    va = conv_fwd(proj, cv8, C_DV * 8, 16, False, "conv_v_fwd")
    gc, bc = ab_fwd(pab, al, dtb)
    grow = _row_layout(gc, tp)
    o_dn, states = gdn_fwd(qn, kn, va, gc, bc, grow)
    on = dn_out_fwd(o_dn, proj, W["dn_out_norm_gain"])
    qs, ks, vs = sb_prep_fwd(proj, W["sb_q_norm_gain"], W["sb_k_norm_gain"])
    o_sb, o_sb16, arrived = sb_fwd(qs, ks, vs, rider=late_weights[0] if late_weights else None)
    if late_weights:
        W.update(late_weights[1](arrived))
    ydn = matmul(on, W["w_branch_dn"], "nn", "ydn_fwd")
    ysb = matmul(o_sb16, W["w_branch_sb"], "nn", "ysb_fwd")
    merged = merge_fwd(proj, ydn, ysb)
    h1 = matmul(merged, W["w_out"], "nn", "wout_fwd", residual=h0)
    n2 = rms_fwd(h1, W["norm_ffn_gain"], "rms2_fwd")
    u = matmul(n2, W["w_ffn_in"], "nn", "ffn_in_fwd", tn_t=512)
    act = swiglu_fwd(u)
    y = matmul(act, W["w_ffn_out"], "nn", "ffn_out_fwd", residual=h1)
    dy, dy16, loss = loss_head(y, target)

    G = {}
    dact = matmul(dy16, W["w_ffn_out"], "nt", "ffn_out_dx", tn_t=1408)
    G["w_ffn_out"] = matmul(act, dy16, "tn", "ffn_out_dw", tm_t=1408)
    dgate, dup = swiglu_bwd(u, dact)
    du = jnp.concatenate([dgate, dup], axis=1)
    dn2 = matmul(du, W["w_ffn_in"], "nt", "ffn_in_dx", tk_t=512)
    G["w_ffn_in"] = matmul(n2, du, "tn", "ffn_in_dw", tn_t=512)
    dh1, dh1_16, G["norm_ffn_gain"] = rms_bwd(h1, W["norm_ffn_gain"], dn2, dy, "rms2_bwd")
    dmerged = matmul(dh1_16, W["w_out"], "nt", "wout_dx")
    G["w_out"] = matmul(merged, dh1_16, "tn", "wout_dw")
    dyd, dys, d_gates = merge_bwd(proj, ydn, ysb, dmerged)
    don = matmul(dyd, W["w_branch_dn"], "nt", "ydn_dx")
    G["w_branch_dn"] = matmul(on, dyd, "tn", "ydn_dw")
    do_sb = matmul(dys, W["w_branch_sb"], "nt", "ysb_dx")
    G["w_branch_sb"] = matmul(o_sb16, dys, "tn", "ysb_dw")
    do_dn, dz, G["dn_out_norm_gain"] = dn_out_bwd(o_dn, proj, W["dn_out_norm_gain"], don)
    (dqn, dkn, dva, dgc, dbc), swapped = gdn_bwd(qn, kn, va, gc, bc, grow, states, do_dn,
                                                 rider=early_swap[0](G) if early_swap else None)
    if early_swap:
        early_swap[1](swapped)
    dpab, dal, ddt = ab_bwd(pab, al, dtb, dgc, dbc)
    G["dn_a_log"], G["dn_dt_bias"] = dal[:, :HEADS], ddt[:, :HEADS]
    dyq, dcq = conv_bwd_act(proj, cq8, dqn, C_DQ * 8, 8, True, "conv_q_bwd")
    dyk, dck = conv_bwd_act(proj, ck8, dkn, C_DK * 8, 8, True, "conv_k_bwd")
    dyv, dcv = conv_bwd_act(proj, cv8, dva, C_DV * 8, 16, False, "conv_v_bwd")
    G["conv_q"], G["conv_k"], G["conv_v"] = dcq[:DN_CONV], dck[:DN_CONV], dcv[:DN_CONV]
    d_dq = conv_bwd_in(dyq, cq8, "conv_q_dx")
    d_dk = conv_bwd_in(dyk, ck8, "conv_k_dx")
    d_dv = conv_bwd_in(dyv, cv8, "conv_v_dx")
    (dqs, dks, dvs), delivered = sb_bwd(qs, ks, vs, o_sb, do_sb, rider=early_grads[0](G) if early_grads else None)
    if early_grads:
        early_grads[1](delivered)
    d_sb, G["sb_q_norm_gain"], G["sb_k_norm_gain"] = sb_prep_bwd(
        proj, W["sb_q_norm_gain"], W["sb_k_norm_gain"], dqs, dks, dvs)
    dproj = jnp.concatenate([d_dq, d_dk, d_dv, dz, d_sb, d_gates], axis=1)
    dw_big = matmul(n1, dproj, "tn", "proj_dw")
    dw_ab = matmul(n1, dpab, "tn", "pab_dw")
    G["w_in"] = _join_w_in(dw_big, dw_ab)
    if last_grads:
        dn1, delivered = matmul(dproj, w_big, "nt", "proj_dx", tk_t=1024, rider=last_grads[0](G))
        last_grads[1](delivered)
    else:
        dn1 = matmul(dproj, w_big, "nt", "proj_dx", tk_t=1024)
    dn1 = matmul(dpab, w_ab, "nt", "pab_dx", residual=dn1)
    dh0, _, G["norm_mix_gain"] = rms_bwd(h0, W["norm_mix_gain"], dn1, dh1, "rms1_bwd")
    G["meta_tokens"] = dh0[P0:P0 + N_META]
    return loss, dh0[P0 + N_META:], G


_BIG = ("w_in", "w_branch_dn", "w_branch_sb", "w_out", "w_ffn_in", "w_ffn_out")
_COL_SHARDED = ("w_in", "w_ffn_in", "meta_tokens", "conv_q", "conv_k", "conv_v")
_SMALL_REPL = ("norm_mix_gain", "norm_ffn_gain", "dn_a_log", "dn_dt_bias", "dn_out_norm_gain", "sb_q_norm_gain",
               "sb_k_norm_gain")
_SMALL_SHARD = ("meta_tokens", "conv_q", "conv_k", "conv_v")
_ORDER = ("meta_tokens", "norm_mix_gain", "w_in", "conv_q", "conv_k", "conv_v", "dn_a_log", "dn_dt_bias",
          "dn_out_norm_gain", "sb_q_norm_gain", "sb_k_norm_gain", "w_branch_dn", "w_branch_sb", "w_out",
          "norm_ffn_gain", "w_ffn_in", "w_ffn_out")


def _unshard(g4, name):
    if name in _COL_SHARDED:
        r, cs = g4.shape[1:]
        return jnp.transpose(g4, (1, 0, 2)).reshape(r, N_CHIPS * cs)
    return g4.reshape((-1,) + g4.shape[2:])


def _to_shards(full, name):
    if full.ndim == 3:
        return full
    if name in _COL_SHARDED:
        r, c = full.shape
        return jnp.transpose(full.reshape(r, N_CHIPS, c // N_CHIPS), (1, 0, 2))
    r, c = full.shape
    return full.reshape(N_CHIPS, r // N_CHIPS, c)


def _rows_1024(a):
    r, c = a.shape
    if c >= 1024:
        return a.reshape(r * (c // 1024), 1024)
    return jnp.pad(a, ((0, 0), (0, 1024 - c)))


def kernel(x, meta_tokens, norm_mix_gain, w_in, conv_q, conv_k, conv_v, dn_a_log, dn_dt_bias, dn_out_norm_gain, sb_q_norm_gain, sb_k_norm_gain, w_branch_dn, w_branch_sb, w_out, norm_ffn_gain, w_ffn_in, w_ffn_out, loss_target, m_meta_tokens, m_norm_mix_gain, m_w_in, m_conv_q, m_conv_k, m_conv_v, m_dn_a_log, m_dn_dt_bias, m_dn_out_norm_gain, m_sb_q_norm_gain, m_sb_k_norm_gain, m_w_branch_dn, m_w_branch_sb, m_w_out, m_norm_ffn_gain, m_w_ffn_in, m_w_ffn_out, v_meta_tokens, v_norm_mix_gain, v_w_in, v_conv_q, v_conv_k, v_conv_v, v_dn_a_log, v_dn_dt_bias, v_dn_out_norm_gain, v_sb_q_norm_gain, v_sb_k_norm_gain, v_w_branch_dn, v_w_branch_sb, v_w_out, v_norm_ffn_gain, v_w_ffn_in, v_w_ffn_out):
    Wl = dict(meta_tokens=meta_tokens, norm_mix_gain=norm_mix_gain, w_in=w_in[0], conv_q=conv_q[0], conv_k=conv_k[0],
              conv_v=conv_v[0], dn_a_log=dn_a_log, dn_dt_bias=dn_dt_bias, dn_out_norm_gain=dn_out_norm_gain,
              sb_q_norm_gain=sb_q_norm_gain, sb_k_norm_gain=sb_k_norm_gain, w_branch_dn=w_branch_dn[0],
              w_branch_sb=w_branch_sb[0], w_out=w_out[0], norm_ffn_gain=norm_ffn_gain, w_ffn_in=w_ffn_in[0],
              w_ffn_out=w_ffn_out[0])
    Ml = dict(meta_tokens=m_meta_tokens, norm_mix_gain=m_norm_mix_gain, w_in=m_w_in[0], conv_q=m_conv_q[0],
              conv_k=m_conv_k[0], conv_v=m_conv_v[0], dn_a_log=m_dn_a_log, dn_dt_bias=m_dn_dt_bias,
              dn_out_norm_gain=m_dn_out_norm_gain, sb_q_norm_gain=m_sb_q_norm_gain, sb_k_norm_gain=m_sb_k_norm_gain,
              w_branch_dn=m_w_branch_dn[0], w_branch_sb=m_w_branch_sb[0], w_out=m_w_out[0],
              norm_ffn_gain=m_norm_ffn_gain, w_ffn_in=m_w_ffn_in[0], w_ffn_out=m_w_ffn_out[0])
    Vl = dict(meta_tokens=v_meta_tokens, norm_mix_gain=v_norm_mix_gain, w_in=v_w_in[0], conv_q=v_conv_q[0],
              conv_k=v_conv_k[0], conv_v=v_conv_v[0], dn_a_log=v_dn_a_log, dn_dt_bias=v_dn_dt_bias,
              dn_out_norm_gain=v_dn_out_norm_gain, sb_q_norm_gain=v_sb_q_norm_gain, sb_k_norm_gain=v_sb_k_norm_gain,
              w_branch_dn=v_w_branch_dn[0], w_branch_sb=v_w_branch_sb[0], w_out=v_w_out[0],
              norm_ffn_gain=v_norm_ffn_gain, w_ffn_in=v_w_ffn_in[0], w_ffn_out=v_w_ffn_out[0])
    lead = {n: (1,) if (n in _BIG or n in ("conv_q", "conv_k", "conv_v")) else () for n in _ORDER}

    chip = 2 * lax.axis_index("x") + lax.axis_index("y")
    c = lax.axis_index("c")
    halved = {n: Wl[n].astype(bf16).reshape(2, Wl[n].shape[0] // 2, Wl[n].shape[1]) for n in _BIG}

    def gathered_weights(names, owns, outs):
        res = {}
        for n, own, g4 in zip(names, owns, outs):
            g4 = lax.dynamic_update_slice(g4, own[None], (chip,) + (0,) * own.ndim)
            if n in _BIG:
                g4 = g4.reshape(N_CHIPS, 2 * g4.shape[2], g4.shape[3])
            res[n] = g4 if n == "w_in" else _unshard(g4, n)
        return res

    first = ["w_in"] + list(_SMALL_SHARD)
    first_own = [halved["w_in"]] + [Wl[n] for n in _SMALL_SHARD]
    W = dict(Wl)
    W.update(gathered_weights(first, first_own, run_exchange(gather_chips(first_own[:1], first_own[1:]), "gather_w_in")))
    late = [n for n in _BIG if n != "w_in"]
    late_own = [halved[n] for n in late]
    for n in late:
        del W[n]

    def halves_of(names, G):
        g4 = [_to_shards(G[n], n) for n in names]
        return [g.reshape(N_CHIPS, 2, g.shape[1] // 2, g.shape[2]) for g in g4]

    def pair_added(g42, from_sib, tag, wire):
        mine = [lax.dynamic_index_in_dim(g, c, axis=1, keepdims=False) for g in g42]
        return [add2(a, b, "grad_pair_add_%s%d" % (tag, t), out_dtype=wire)
                for t, (a, b) in enumerate(zip(mine, from_sib))]

    def chip_reduced(parts, slots, tag):
        slots = [lax.dynamic_update_slice(s, lax.dynamic_index_in_dim(p, chip, axis=0, keepdims=True), (chip, 0, 0))
                 for s, p in zip(slots, parts)]
        return [sum_slots(s, "grad_chip_sum_%s%d" % (tag, t)) for t, s in enumerate(slots)]

    early, last = {}, {}

    def early_swap_begin(G):
        early["g42"] = halves_of(late, G)
        return sibling_swap(early["g42"])

    def early_begin(G):
        early["parts"] = pair_added(early["g42"], early["from_sib"], "a", f32)
        return scatter_chips(early["parts"])

    def last_begin(G):
        g42 = halves_of(["w_in"], G)
        last["parts"] = pair_added(g42, run_exchange(sibling_swap(g42), "grad_sibling_swap_b"), "b", bf16)
        return scatter_chips(last["parts"])

    loss, grad_x, G = _step(
        x[0], W["meta_tokens"], W, loss_target[0],
        late_weights=(gather_chips(late_own, []), lambda outs: gathered_weights(late, late_own, outs)),
        early_swap=(early_swap_begin, lambda outs: early.update(from_sib=outs)),
        early_grads=(early_begin, lambda slots: early.update(halves=chip_reduced(early["parts"], slots, "a"))),
        last_grads=(last_begin, lambda slots: last.update(halves=chip_reduced(last["parts"], slots, "b"))))
    halves = last["halves"] + early["halves"]
    theirs = sibling_send(halves, "grad_sibling_send")
    Gs = {}
    for n, h, o in zip(["w_in"] + late, halves, theirs):
        Gs[n] = lax.dynamic_update_slice(jnp.concatenate([o, o], axis=0), h, (c * h.shape[0], 0))

    small_names = list(_SMALL_REPL) + list(_SMALL_SHARD)
    pieces = [_rows_1024(G[n]) for n in small_names] + [_rows_1024(loss)]
    counts = [p.shape[0] for p in pieces]
    pack = jnp.concatenate(pieces, axis=0)
    pad_rows = (-pack.shape[0]) % SUB
    pack = jnp.pad(pack, ((0, pad_rows), (0, 0)))
    total = sum_slots(gather_all(pack, "small_gather"), "small_sum")
    chip = 2 * lax.axis_index("x") + lax.axis_index("y")
    row = 0
    for n, cnt in zip(small_names, counts[:-1]):
        blk = total[row:row + cnt]
        row += cnt
        full_shape = G[n].shape
        if full_shape[1] >= 1024:
            blk = blk.reshape(full_shape)
        else:
            blk = blk[:, :full_shape[1]]
        if n in _SMALL_SHARD:
            cs = full_shape[1] // N_CHIPS
            blk = lax.dynamic_slice_in_dim(blk, chip * cs, cs, axis=1)
        Gs[n] = blk
    loss_out = total[row, 0]

    grads, deltas, new_m, new_v = [], [], [], []
    for n in _ORDER:
        d, m2, v2 = adamw(Wl[n], Gs[n], Ml[n], Vl[n], "adamw_" + n)
        shape = lead[n] + Wl[n].shape
        grads.append(Gs[n].reshape(shape))
        deltas.append(d.reshape(shape))
        new_m.append(m2.reshape(shape))
        new_v.append(v2.reshape(shape))
    return (loss_out, grad_x[None], *grads, *deltas, *new_m, *new_v)
```

```python
import jax
import jax.numpy as jnp
from jax import lax
from jax.experimental import pallas as pl
from jax.experimental.pallas import tpu as pltpu

f32 = jnp.float32
bf16 = jnp.bfloat16

D_MODEL = 1024
N_META = 16
CHUNK = 64
HEADS = 8
DN_DK = 128
DN_DV = 256
DN_CONV = 4
DN_QK = HEADS * DN_DK
DN_V = HEADS * DN_DV
SB_DH = 128
SB_W = HEADS * SB_DH
SB_BLOCK = 128
SB_QB = 384
SB_GROUP = 4
SB_HEADS_PER_STEP = 2
SB_FWD_HEADS_PER_STEP = 4
GDN_HEADS_PER_STEP = 8
CONV_W = 1024
D_FF = 2816
RMS_EPS = 1e-6
L2_EPS = 1e-6
ADAM_LR = 0.001
ADAM_B1 = 0.9
ADAM_B2 = 0.999
ADAM_EPS = 1e-08
ADAM_WD = 0.01
ADAM_STEP = 10

P0 = 112
LANE = 128
SUB = 8
VMEM_LIMIT = 48 * 1024 * 1024
N_CHIPS = 4
N_DEV = 8

C_DQ, C_DK, C_DV, C_DZ, C_SQ, C_SK, C_SV, C_GDN, C_GSB = 0, 1, 2, 4, 6, 7, 8, 9, 10
PROJ_BIG = 11 * 1024
AB_COL = 2 * DN_QK + 2 * DN_V


def _params(n_axes):
    return pltpu.CompilerParams(dimension_semantics=("arbitrary",) * n_axes, vmem_limit_bytes=VMEM_LIMIT)


def _tile(n, target, q=LANE):
    best = None
    for t in range(q, min(n, target) + 1, q):
        if n % t == 0:
            best = t
    return best if best is not None else n


def _dot(a, b):
    return jnp.dot(a.astype(bf16), b.astype(bf16), preferred_element_type=f32)


def _dot_nt(a, b):
    return lax.dot_general(a.astype(bf16), b.astype(bf16), (((1,), (1,)), ((), ())), preferred_element_type=f32)


def _dot_tn(a, b):
    return lax.dot_general(a.astype(bf16), b.astype(bf16), (((0,), (0,)), ((), ())), preferred_element_type=f32)


_HI = lax.Precision.HIGH


def _hdot(a, b):
    return jnp.dot(a, b, precision=_HI, preferred_element_type=f32)


def _hdot_nt(a, b):
    return lax.dot_general(a, b, (((1,), (1,)), ((), ())), precision=_HI, preferred_element_type=f32)


def _hdot_tn(a, b):
    return lax.dot_general(a, b, (((0,), (0,)), ((), ())), precision=_HI, preferred_element_type=f32)


def _sigmoid(x):
    return 0.5 * jnp.tanh(0.5 * x) + 0.5


def _log1p_small(e):
    return jnp.where(e < 1e-3, e * (1.0 - e * (0.5 - e * (1.0 / 3.0))), jnp.log(1.0 + e))


def _rowsum(x):
    return jnp.sum(x, axis=1, keepdims=True)


def _allsum(x):
    return jnp.sum(jnp.sum(x, axis=1, keepdims=True), axis=0, keepdims=True)


def matmul(a, b, mode, name, residual=None, out_dtype=f32, tm_t=1408, tn_t=1024, tk_t=1408, rider=None):
    if mode == "nn":
        (M, K), (K2, N) = a.shape, b.shape
    elif mode == "nt":
        (M, K), (N, K2) = a.shape, b.shape
    else:
        (K, M), (K2, N) = a.shape, b.shape
    assert K == K2, (a.shape, b.shape, mode)
    tm, tn, tk = _tile(M, tm_t), _tile(N, tn_t), _tile(K, tk_t)
    nk = K // tk
    if mode == "nn":
        a_spec = pl.BlockSpec((tm, tk), lambda i, j, k: (i, k))
        b_spec = pl.BlockSpec((tk, tn), lambda i, j, k: (k, j))
        dims = (((1,), (0,)), ((), ()))
    elif mode == "nt":
        a_spec = pl.BlockSpec((tm, tk), lambda i, j, k: (i, k))
        b_spec = pl.BlockSpec((tn, tk), lambda i, j, k: (j, k))
        dims = (((1,), (1,)), ((), ()))
    else:
        a_spec = pl.BlockSpec((tk, tm), lambda i, j, k: (k, i))
        b_spec = pl.BlockSpec((tk, tn), lambda i, j, k: (k, j))
        dims = (((0,), (0,)), ((), ()))
    o_spec = pl.BlockSpec((tm, tn), lambda i, j, k: (i, j))
    has_res = residual is not None
    grid = (M // tm, N // tn, nk)
    split, ride_first, ride_last = _ride(rider, 3 if has_res else 2, 1, grid)

    def body(*refs):
        ins_, (o_ref,), (rin, rout, rest) = split(refs)
        a_ref, b_ref = ins_[:2]
        r_ref = ins_[2] if has_res else None
        acc_ref, ride = rest[0], (rin, rout, rest[1:])
        ride_first(ride)
        k = pl.program_id(2)

        @pl.when(k == 0)
        def _():
            acc_ref[...] = jnp.zeros_like(acc_ref)

        acc_ref[...] += lax.dot_general(a_ref[...].astype(bf16), b_ref[...].astype(bf16), dims,
                                        preferred_element_type=f32)

        @pl.when(k == nk - 1)
        def _():
            r = acc_ref[...]
            if has_res:
                r = r + r_ref[...]
            o_ref[...] = r.astype(out_dtype)

        ride_last(ride)

    ins = [a, b] + ([residual] if has_res else [])
    specs = [a_spec, b_spec] + ([o_spec] if has_res else [])
    r_ins = rider.ins if rider else []
    r_outs = rider.out_shapes if rider else []
    res = pl.pallas_call(
        body, name=name, grid=grid, in_specs=specs + [_ANY] * len(r_ins), out_specs=[o_spec] + [_ANY] * len(r_outs),
        out_shape=[jax.ShapeDtypeStruct((M, N), out_dtype)] + list(r_outs),
        scratch_shapes=[pltpu.VMEM((tm, tn), f32)] + (rider.scratch if rider else []), compiler_params=_params(3),
    )(*ins, *r_ins)
    return (res[0], res[1:]) if rider else res[0]


def _row_tile(tp):
    return _tile(tp, 512)


def rms_fwd(h, gain, name):
    tp, d = h.shape
    rt = _row_tile(tp)

    def body(h_ref, g_ref, o_ref):
        x = h_ref[...]
        r = lax.rsqrt(jnp.mean(x * x, axis=-1, keepdims=True) + RMS_EPS)
        o_ref[...] = (x * r * g_ref[...]).astype(bf16)

    return pl.pallas_call(
        body, name=name, grid=(tp // rt,),
        in_specs=[pl.BlockSpec((rt, d), lambda i: (i, 0)), pl.BlockSpec((1, d), lambda i: (0, 0))],
        out_specs=pl.BlockSpec((rt, d), lambda i: (i, 0)),
        out_shape=jax.ShapeDtypeStruct((tp, d), bf16), compiler_params=_params(1),
    )(h, gain)


def rms_bwd(h, gain, dn, dres, name):
    tp, d = h.shape
    rt = _row_tile(tp)

    def body(h_ref, g_ref, dn_ref, dr_ref, dh_ref, dhb_ref, dg_ref):
        i = pl.program_id(0)
        x = h_ref[...]
        r = lax.rsqrt(jnp.mean(x * x, axis=-1, keepdims=True) + RMS_EPS)
        xh = x * r
        dn_ = dn_ref[...]
        dxh = dn_ * g_ref[...]
        dh = r * (dxh - xh * jnp.mean(dxh * xh, axis=-1, keepdims=True)) + dr_ref[...]
        dh_ref[...] = dh
        dhb_ref[...] = dh.astype(bf16)
        part = jnp.sum(dn_ * xh, axis=0, keepdims=True)

        @pl.when(i == 0)
        def _():
            dg_ref[...] = part

        @pl.when(i > 0)
        def _():
            dg_ref[...] += part

    row = pl.BlockSpec((rt, d), lambda i: (i, 0))
    vec = pl.BlockSpec((1, d), lambda i: (0, 0))
    return pl.pallas_call(
        body, name=name, grid=(tp // rt,), in_specs=[row, vec, row, row], out_specs=[row, row, vec],
        out_shape=[jax.ShapeDtypeStruct((tp, d), f32), jax.ShapeDtypeStruct((tp, d), bf16),
                   jax.ShapeDtypeStruct((1, d), f32)],
        compiler_params=_params(1),
    )(h, gain, dn, dres)


def loss_head(y, target):
    tp, d = y.shape
    lead = P0 + N_META
    rt = _row_tile(tp)
    ns = rt // lead
    assert lead == SB_BLOCK and rt % lead == 0 and target.shape == (tp - lead, d)
    last = target.shape[0] // lead - 1

    def body(*refs):
        y_ref, t_refs = refs[0], refs[1:1 + ns]
        dy_ref, dyb_ref, l_ref = refs[1 + ns:]
        i = pl.program_id(0)

        @pl.when(i == 0)
        def _():
            l_ref[...] = jnp.zeros_like(l_ref)

        part = jnp.zeros((1, 1), f32)
        for s in range(ns):
            rows = slice(s * lead, (s + 1) * lead)
            err = y_ref[rows, :] - t_refs[s][...]
            if s == 0:
                err = err * (i > 0).astype(f32)
            dy = err * (1.0 / d)
            dy_ref[rows, :] = dy
            dyb_ref[rows, :] = dy.astype(bf16)
            part = part + _allsum(err * err)
        l_ref[...] += jnp.broadcast_to(part * (0.5 / d), l_ref.shape)

    row = pl.BlockSpec((rt, d), lambda i: (i, 0))
    t_specs = [pl.BlockSpec((lead, d), lambda i, s=s: (jnp.clip(ns * i + s - 1, 0, last), 0)) for s in range(ns)]
    return pl.pallas_call(
        body, name="loss_head", grid=(tp // rt,), in_specs=[row] + t_specs,
        out_specs=[row, row, pl.BlockSpec((1, LANE), lambda i: (0, 0))],
        out_shape=[jax.ShapeDtypeStruct((tp, d), f32), jax.ShapeDtypeStruct((tp, d), bf16),
                   jax.ShapeDtypeStruct((1, LANE), f32)],
        compiler_params=_params(1),
    )(y, *([target] * ns))


def swiglu_fwd(u):
    tp = u.shape[0]
    rt, cb = _row_tile(tp), D_FF // 2
    nb = D_FF // cb

    def body(g_ref, u_ref, o_ref):
        g = g_ref[...]
        o_ref[...] = (g * _sigmoid(g) * u_ref[...]).astype(bf16)

    return pl.pallas_call(
        body, name="swiglu_fwd", grid=(tp // rt, nb),
        in_specs=[pl.BlockSpec((rt, cb), lambda i, j: (i, j)), pl.BlockSpec((rt, cb), lambda i, j: (i, j + nb))],
        out_specs=pl.BlockSpec((rt, cb), lambda i, j: (i, j)),
        out_shape=jax.ShapeDtypeStruct((tp, D_FF), bf16), compiler_params=_params(2),
    )(u, u)


def swiglu_bwd(u, dact):
    tp = u.shape[0]
    rt, cb = _row_tile(tp), D_FF // 2
    nb = D_FF // cb

    def body(g_ref, u_ref, da_ref, dg_ref, du_ref):
        g = g_ref[...]
        s = _sigmoid(g)
        da = da_ref[...]
        dg_ref[...] = (da * u_ref[...] * s * (1.0 + g * (1.0 - s))).astype(bf16)
        du_ref[...] = (da * g * s).astype(bf16)

    lo = pl.BlockSpec((rt, cb), lambda i, j: (i, j))
    hi = pl.BlockSpec((rt, cb), lambda i, j: (i, j + nb))
    dgate, dup = pl.pallas_call(
        body, name="swiglu_bwd", grid=(tp // rt, nb), in_specs=[lo, hi, lo], out_specs=[lo, lo],
        out_shape=[jax.ShapeDtypeStruct((tp, D_FF), bf16)] * 2, compiler_params=_params(2),
    )(u, u, dact)
    return dgate, dup


def merge_fwd(proj, ydn, ysb):
    tp = proj.shape[0]
    rt, d = _row_tile(tp), D_MODEL

    def body(gd_ref, gs_ref, yd_ref, ys_ref, o_ref):
        o_ref[...] = (_sigmoid(gd_ref[...]) * yd_ref[...] + _sigmoid(gs_ref[...]) * ys_ref[...]).astype(bf16)

    row = pl.BlockSpec((rt, d), lambda i: (i, 0))
    return pl.pallas_call(
        body, name="merge_fwd", grid=(tp // rt,),
        in_specs=[pl.BlockSpec((rt, d), lambda i: (i, C_GDN)), pl.BlockSpec((rt, d), lambda i: (i, C_GSB)), row, row],
        out_specs=row, out_shape=jax.ShapeDtypeStruct((tp, d), bf16), compiler_params=_params(1),
    )(proj, proj, ydn, ysb)


def merge_bwd(proj, ydn, ysb, dm):
    tp = proj.shape[0]
    rt, d = _row_tile(tp), D_MODEL

    def body(gd_ref, gs_ref, yd_ref, ys_ref, dm_ref, dyd_ref, dys_ref, dg_ref):
        dm_ = dm_ref[...]
        sd = _sigmoid(gd_ref[...])
        ss = _sigmoid(gs_ref[...])
        dyd_ref[...] = (dm_ * sd).astype(bf16)
        dys_ref[...] = (dm_ * ss).astype(bf16)
        dg_ref[:, :d] = (dm_ * yd_ref[...] * sd * (1.0 - sd)).astype(bf16)
        dg_ref[:, d:] = (dm_ * ys_ref[...] * ss * (1.0 - ss)).astype(bf16)

    row = pl.BlockSpec((rt, d), lambda i: (i, 0))
    return pl.pallas_call(
        body, name="merge_bwd", grid=(tp // rt,),
        in_specs=[pl.BlockSpec((rt, d), lambda i: (i, C_GDN)), pl.BlockSpec((rt, d), lambda i: (i, C_GSB)), row, row, row],
        out_specs=[row, row, pl.BlockSpec((rt, 2 * d), lambda i: (i, 0))],
        out_shape=[jax.ShapeDtypeStruct((tp, d), bf16)] * 2 + [jax.ShapeDtypeStruct((tp, 2 * d), bf16)],
        compiler_params=_params(1),
    )(proj, proj, ydn, ysb, dm)


def dn_out_fwd(o, proj, gain):
    tp = o.shape[0]
    rt, cb, wide = _row_tile(tp), DN_DV, 1024
    zb = C_DZ * 1024 // wide

    def body(o_ref, z_ref, g_ref, y_ref):
        for s in range(wide // cb):
            sl = slice(s * cb, (s + 1) * cb)
            x = o_ref[:, sl]
            r = lax.rsqrt(jnp.mean(x * x, axis=-1, keepdims=True) + RMS_EPS)
            z = z_ref[:, sl]
            y_ref[:, sl] = (x * r * g_ref[...] * (z * _sigmoid(z))).astype(bf16)

    blk = pl.BlockSpec((rt, wide), lambda i, j: (i, j))
    return pl.pallas_call(
        body, name="dn_out_fwd", grid=(tp // rt, DN_V // wide),
        in_specs=[blk, pl.BlockSpec((rt, wide), lambda i, j: (i, j + zb)), pl.BlockSpec((1, cb), lambda i, j: (0, 0))],
        out_specs=blk, out_shape=jax.ShapeDtypeStruct((tp, DN_V), bf16), compiler_params=_params(2),
    )(o, proj, gain)


def dn_out_bwd(o, proj, gain, dy):
    tp = o.shape[0]
    rt, cb, wide = _row_tile(tp), DN_DV, 1024
    zb = C_DZ * 1024 // wide

    def body(o_ref, z_ref, g_ref, dy_ref, do_ref, dz_ref, dg_ref):
        i, j = pl.program_id(0), pl.program_id(1)
        g = g_ref[...]
        part = jnp.zeros((1, cb), f32)
        for hh in range(wide // cb):
            sl = slice(hh * cb, (hh + 1) * cb)
            x = o_ref[:, sl]
            r = lax.rsqrt(jnp.mean(x * x, axis=-1, keepdims=True) + RMS_EPS)
            xh = x * r
            z = z_ref[:, sl]
            s = _sigmoid(z)
            dy_ = dy_ref[:, sl]
            drn = dy_ * (z * s)
            dz_ref[:, sl] = (dy_ * xh * g * s * (1.0 + z * (1.0 - s))).astype(bf16)
            dxh = drn * g
            do_ref[:, sl] = r * (dxh - xh * jnp.mean(dxh * xh, axis=-1, keepdims=True))
            part = part + jnp.sum(drn * xh, axis=0, keepdims=True)
        first = jnp.logical_and(i == 0, j == 0)

        @pl.when(first)
        def _():
            dg_ref[...] = part

        @pl.when(jnp.logical_not(first))
        def _():
            dg_ref[...] += part

    blk = pl.BlockSpec((rt, wide), lambda i, j: (i, j))
    vec = pl.BlockSpec((1, cb), lambda i, j: (0, 0))
    return pl.pallas_call(
        body, name="dn_out_bwd", grid=(tp // rt, DN_V // wide),
        in_specs=[blk, pl.BlockSpec((rt, wide), lambda i, j: (i, j + zb)), vec, blk],
        out_specs=[blk, blk, vec],
        out_shape=[jax.ShapeDtypeStruct((tp, DN_V), f32), jax.ShapeDtypeStruct((tp, DN_V), bf16),
                   jax.ShapeDtypeStruct((1, cb), f32)],
        compiler_params=_params(2),
    )(o, proj, gain, dy)


def sb_prep_fwd(proj, gq, gk):
    tp = proj.shape[0]
    rt, cb = _row_tile(tp), SB_DH

    def body(q_ref, k_ref, v_ref, gq_ref, gk_ref, qo_ref, ko_ref, vo_ref):
        for x_ref, g_ref, o_ref in ((q_ref, gq_ref, qo_ref), (k_ref, gk_ref, ko_ref)):
            for h in range(HEADS):
                sl = slice(h * cb, (h + 1) * cb)
                x = x_ref[:, sl]
                r = lax.rsqrt(jnp.mean(x * x, axis=-1, keepdims=True) + RMS_EPS)
                o_ref[:, sl] = (x * r * g_ref[...]).astype(bf16)
        vo_ref[...] = v_ref[...].astype(bf16)

    blk = pl.BlockSpec((rt, SB_W), lambda i: (i, 0))
    vec = pl.BlockSpec((1, cb), lambda i: (0, 0))
    return pl.pallas_call(
        body, name="sb_prep_fwd", grid=(tp // rt,),
        in_specs=[pl.BlockSpec((rt, SB_W), lambda i: (i, C_SQ)), pl.BlockSpec((rt, SB_W), lambda i: (i, C_SK)),
                  pl.BlockSpec((rt, SB_W), lambda i: (i, C_SV)), vec, vec],
        out_specs=[blk] * 3, out_shape=[jax.ShapeDtypeStruct((tp, SB_W), bf16)] * 3, compiler_params=_params(1),
    )(proj, proj, proj, gq, gk)


def sb_prep_bwd(proj, gq, gk, dqs, dks, dvs):
    tp = proj.shape[0]
    rt, cb = _row_tile(tp), SB_DH

    def body(q_ref, k_ref, gq_ref, gk_ref, dq_ref, dk_ref, dv_ref, do_ref, dgq_ref, dgk_ref):
        first = pl.program_id(0) == 0
        do_ref[:, 2 * SB_W:] = dv_ref[...].astype(bf16)
        for x_ref, g_ref, dn_ref, at, dg_ref in ((q_ref, gq_ref, dq_ref, 0, dgq_ref),
                                                 (k_ref, gk_ref, dk_ref, SB_W, dgk_ref)):
            part = jnp.zeros((1, cb), f32)
            for h in range(HEADS):
                sl = slice(h * cb, (h + 1) * cb)
                x = x_ref[:, sl]
                r = lax.rsqrt(jnp.mean(x * x, axis=-1, keepdims=True) + RMS_EPS)
                xh = x * r
                dn_ = dn_ref[:, sl]
                dxh = dn_ * g_ref[...]
                do_ref[:, at + h * cb:at + (h + 1) * cb] = (
                    r * (dxh - xh * jnp.mean(dxh * xh, axis=-1, keepdims=True))).astype(bf16)
                part = part + jnp.sum(dn_ * xh, axis=0, keepdims=True)

            @pl.when(first)
            def _(dg_ref=dg_ref, part=part):
                dg_ref[...] = part

            @pl.when(jnp.logical_not(first))
            def _(dg_ref=dg_ref, part=part):
                dg_ref[...] += part

    blk = pl.BlockSpec((rt, SB_W), lambda i: (i, 0))
    vec = pl.BlockSpec((1, cb), lambda i: (0, 0))
    return pl.pallas_call(
        body, name="sb_prep_bwd", grid=(tp // rt,),
        in_specs=[pl.BlockSpec((rt, SB_W), lambda i: (i, C_SQ)), pl.BlockSpec((rt, SB_W), lambda i: (i, C_SK)),
                  vec, vec, blk, blk, blk],
        out_specs=[pl.BlockSpec((rt, 3 * SB_W), lambda i: (i, 0)), vec, vec],
        out_shape=[jax.ShapeDtypeStruct((tp, 3 * SB_W), bf16)] + [jax.ShapeDtypeStruct((1, cb), f32)] * 2,
        compiler_params=_params(1),
    )(proj, proj, gq, gk, dqs, dks, dvs)


def _conv_taps(ext, rt):
    taps = []
    for k in range(DN_CONV):
        s = DN_CONV - 1 - k
        taps.append((pltpu.roll(ext, s, axis=0) if s else ext)[SUB:SUB + rt])
    return taps


def _conv_act(taps, w, l2):
    y = taps[0] * w[0:1]
    for k in range(1, DN_CONV):
        y = y + taps[k] * w[k:k + 1]
    s = _sigmoid(y)
    a = y * s
    if l2:
        n = lax.rsqrt(jnp.sum(a * a, axis=-1, keepdims=True) + L2_EPS)
        return y, s, a, n
    return y, s, a, None


def conv_fwd(proj, w8, col_blk, ncb, l2, name):
    tp = proj.shape[0]
    rt = _row_tile(tp)
    hb = rt // SUB
    cw = CONV_W
    cb0 = col_blk * LANE // cw

    def body(x_ref, h_ref, w_ref, o_ref):
        i = pl.program_id(1)
        first = (i > 0).astype(f32)
        for s in range(cw // LANE):
            sl = slice(s * LANE, (s + 1) * LANE)
            ext = jnp.concatenate([h_ref[:, sl] * first, x_ref[:, sl]], axis=0)
            _, _, a, n = _conv_act(_conv_taps(ext, rt), w_ref[:, sl], l2)
            o_ref[:, sl] = a * n if l2 else a

    return pl.pallas_call(
        body, name=name, grid=(ncb * LANE // cw, tp // rt),
        in_specs=[pl.BlockSpec((rt, cw), lambda j, i: (i, j + cb0)),
                  pl.BlockSpec((SUB, cw), lambda j, i: (jnp.maximum(i * hb - 1, 0), j + cb0)),
                  pl.BlockSpec((SUB, cw), lambda j, i: (0, j))],
        out_specs=pl.BlockSpec((rt, cw), lambda j, i: (i, j)),
        out_shape=jax.ShapeDtypeStruct((tp, ncb * LANE), f32), compiler_params=_params(2),
    )(proj, proj, w8)


def conv_bwd_act(proj, w8, dout, col_blk, ncb, l2, name):
    tp = proj.shape[0]
    rt = _row_tile(tp)
    hb = rt // SUB
    cw = CONV_W
    cb0 = col_blk * LANE // cw

    def body(x_ref, h_ref, w_ref, d_ref, dy_ref, dw_ref):
        i = pl.program_id(1)
        first = (i > 0).astype(f32)
        rows = lax.broadcasted_iota(jnp.int32, (SUB, LANE), 0)
        for s in range(cw // LANE):
            sl = slice(s * LANE, (s + 1) * LANE)
            ext = jnp.concatenate([h_ref[:, sl] * first, x_ref[:, sl]], axis=0)
            taps = _conv_taps(ext, rt)
            y, sg, a, n = _conv_act(taps, w_ref[:, sl], l2)
            da = d_ref[:, sl]
            if l2:
                out = a * n
                da = n * (da - out * jnp.sum(da * out, axis=-1, keepdims=True))
            dy = da * sg * (1.0 + y * (1.0 - sg))
            dy_ref[:, sl] = dy
            part = jnp.zeros((SUB, LANE), f32)
            for k in range(DN_CONV):
                part = part + jnp.where(rows == k, jnp.sum(taps[k] * dy, axis=0, keepdims=True), 0.0)

            @pl.when(i == 0)
            def _(sl=sl, part=part):
                dw_ref[:, sl] = part

            @pl.when(i > 0)
            def _(sl=sl, part=part):
                dw_ref[:, sl] += part

    return pl.pallas_call(
        body, name=name, grid=(ncb * LANE // cw, tp // rt),
        in_specs=[pl.BlockSpec((rt, cw), lambda j, i: (i, j + cb0)),
                  pl.BlockSpec((SUB, cw), lambda j, i: (jnp.maximum(i * hb - 1, 0), j + cb0)),
                  pl.BlockSpec((SUB, cw), lambda j, i: (0, j)),
                  pl.BlockSpec((rt, cw), lambda j, i: (i, j))],
        out_specs=[pl.BlockSpec((rt, cw), lambda j, i: (i, j)), pl.BlockSpec((SUB, cw), lambda j, i: (0, j))],
        out_shape=[jax.ShapeDtypeStruct((tp, ncb * LANE), f32), jax.ShapeDtypeStruct((SUB, ncb * LANE), f32)],
        compiler_params=_params(2),
    )(proj, proj, w8, dout)


def conv_bwd_in(dy, w8, name):
    tp, cols = dy.shape
    rt = _row_tile(tp)
    hb = rt // SUB
    nr = tp // rt
    last8 = tp // SUB - 1
    cw = CONV_W

    def body(d_ref, h_ref, w_ref, o_ref):
        i = pl.program_id(1)
        last = (i < nr - 1).astype(f32)
        for c0 in range(cw // LANE):
            sl = slice(c0 * LANE, (c0 + 1) * LANE)
            ext = jnp.concatenate([d_ref[:, sl], h_ref[:, sl] * last], axis=0)
            w = w_ref[:, sl]
            acc = None
            for k in range(DN_CONV):
                s = DN_CONV - 1 - k
                sh = (pltpu.roll(ext, rt + SUB - s, axis=0) if s else ext)[0:rt]
                term = sh * w[k:k + 1]
                acc = term if acc is None else acc + term
            o_ref[:, sl] = acc.astype(bf16)

    return pl.pallas_call(
        body, name=name, grid=(cols // cw, nr),
        in_specs=[pl.BlockSpec((rt, cw), lambda j, i: (i, j)),
                  pl.BlockSpec((SUB, cw), lambda j, i: (jnp.minimum((i + 1) * hb, last8), j)),
                  pl.BlockSpec((SUB, cw), lambda j, i: (0, j))],
        out_specs=pl.BlockSpec((rt, cw), lambda j, i: (i, j)),
        out_shape=jax.ShapeDtypeStruct((tp, cols), bf16), compiler_params=_params(2),
    )(dy, dy, w8)


def _ab_common(p, al, dtb, r0):
    rows = r0 + lax.broadcasted_iota(jnp.int32, p.shape, 0)
    mask = (rows >= P0).astype(f32)
    xx = p + dtb
    sp = jnp.maximum(xx, 0.0) + _log1p_small(jnp.exp(-jnp.abs(xx)))
    ea = jnp.exp(al)
    g = -ea * sp * mask
    beta = _sigmoid(p) * mask
    return g, beta, _sigmoid(xx), ea, mask


def _chunk_tri(rt, later):
    r = lax.broadcasted_iota(jnp.int32, (rt, rt), 0)
    c = lax.broadcasted_iota(jnp.int32, (rt, rt), 1)
    shift = CHUNK.bit_length() - 1
    same = jnp.right_shift(r, shift) == jnp.right_shift(c, shift)
    return jnp.logical_and(same, c >= r if later else c <= r).astype(f32)


def ab_fwd(pab, al, dtb):
    tp = pab.shape[0]
    rt = _row_tile(tp)
    assert rt % CHUNK == 0

    def body(p_ref, al_ref, dt_ref, g_ref, b_ref):
        i = pl.program_id(0)
        g, beta, _, _, _ = _ab_common(p_ref[...], al_ref[...], dt_ref[...], i * rt)
        gam = _hdot(_chunk_tri(rt, False), g)
        for h in range(HEADS):
            g_ref[h] = jnp.broadcast_to(gam[:, h:h + 1], (rt, LANE))
            b_ref[h] = jnp.broadcast_to(beta[:, HEADS + h:HEADS + h + 1], (rt, LANE))

    vec = pl.BlockSpec((1, LANE), lambda i: (0, 0))
    out = pl.BlockSpec((HEADS, rt, LANE), lambda i: (0, i, 0))
    return pl.pallas_call(
        body, name="ab_fwd", grid=(tp // rt,), in_specs=[pl.BlockSpec((rt, LANE), lambda i: (i, 0)), vec, vec],
        out_specs=[out, out], out_shape=[jax.ShapeDtypeStruct((HEADS, tp, LANE), f32)] * 2, compiler_params=_params(1),
    )(pab, al, dtb)


def ab_bwd(pab, al, dtb, dg, db):
    tp = pab.shape[0]
    rt = _row_tile(tp)

    def body(p_ref, al_ref, dt_ref, dg_ref, db_ref, dp_ref, dal_ref, ddt_ref):
        i = pl.program_id(0)
        g, beta, sx, ea, mask = _ab_common(p_ref[...], al_ref[...], dt_ref[...], i * rt)
        lanes = lax.broadcasted_iota(jnp.int32, (rt, LANE), 1)
        dgl = jnp.zeros((rt, LANE), f32)
        dbl = jnp.zeros((rt, LANE), f32)
        for h in range(HEADS):
            dgl = dgl + jnp.where(lanes == h, dg_ref[h], 0.0)
            dbl = dbl + jnp.where(lanes == HEADS + h, db_ref[h], 0.0)
        dgl = _hdot(_chunk_tri(rt, True), dgl)
        dxx = dgl * (-ea) * sx * mask
        dp_ref[...] = (dxx + dbl * beta * (1.0 - beta)).astype(bf16)
        pal = jnp.sum(dgl * g, axis=0, keepdims=True)
        pdt = jnp.sum(dxx, axis=0, keepdims=True)

        @pl.when(i == 0)
        def _():
            dal_ref[...] = pal
            ddt_ref[...] = pdt

        @pl.when(i > 0)
        def _():
            dal_ref[...] += pal
            ddt_ref[...] += pdt

    vec = pl.BlockSpec((1, LANE), lambda i: (0, 0))
    row = pl.BlockSpec((rt, LANE), lambda i: (i, 0))
    big = pl.BlockSpec((HEADS, rt, LANE), lambda i: (0, i, 0))
    return pl.pallas_call(
        body, name="ab_bwd", grid=(tp // rt,), in_specs=[row, vec, vec, big, big], out_specs=[row, vec, vec],
        out_shape=[jax.ShapeDtypeStruct((tp, LANE), bf16), jax.ShapeDtypeStruct((1, LANE), f32),
                   jax.ShapeDtypeStruct((1, LANE), f32)],
        compiler_params=_params(1),
    )(pab, al, dtb, dg, db)


class _Chunk:
    pass


def _gdn_chunk(q, k, v, gcol, bcol, grow8):
    C = CHUNK
    R = range(len(q))
    X = _Chunk()
    ri = lax.broadcasted_iota(jnp.int32, (C, C), 0)
    ci = lax.broadcasted_iota(jnp.int32, (C, C), 1)
    eye = (ri == ci).astype(f32)
    gam = list(gcol)
    gam_row = [grow8[h][0:1, 0:C] for h in R]
    X.ri, X.ci = ri, ci
    X.Dm = [jnp.where(ri >= ci, jnp.exp(jnp.minimum(gam[h][:, 0:C] - gam_row[h], 0.0)), 0.0) for h in R]
    X.eg = [jnp.exp(gam[h]) for h in R]
    gl = [gam[h][C - 1:C, :] for h in R]
    X.egl = [jnp.exp(gl[h]) for h in R]
    X.kdec = [jnp.exp(gl[h] - gam[h]) for h in R]
    X.qs = [q[h] * (DN_DK ** -0.5) for h in R]
    X.kb = [k[h] * bcol[h] for h in R]
    kk = [_dot_nt(X.kb[h], k[h]) for h in R]
    qk = [_dot_nt(X.qs[h], k[h]) for h in R]
    X.A = [jnp.where(ri > ci, kk[h] * X.Dm[h], 0.0) for h in R]
    T = [eye - X.A[h] for h in R]
    P = list(X.A)
    for _ in range(5):
        P = [_hdot(P[h], P[h]) for h in R]
        T = [T[h] + _hdot(T[h], P[h]) for h in R]
    X.T = T
    X.b2 = [jnp.concatenate([bcol[h], bcol[h]], axis=-1) for h in R]
    X.u = [_hdot(T[h], v[h] * X.b2[h]) for h in R]
    X.w = [_hdot(T[h], X.kb[h] * X.eg[h]) for h in R]
    X.attn = [qk[h] * X.Dm[h] for h in R]
    X.qg = [X.qs[h] * X.eg[h] for h in R]
    X.kg = [k[h] * X.kdec[h] for h in R]
    return X


def gdn_fwd(q, k, v, gc, bc, grow):
    tp = q.shape[0]
    nc = tp // CHUNK
    hb = GDN_HEADS_PER_STEP

    def body(q_ref, k_ref, v_ref, gc_ref, bc_ref, gr_ref, o_ref, ss_ref, S_ref):
        c = pl.program_id(1)

        @pl.when(c == 0)
        def _():
            S_ref[...] = jnp.zeros_like(S_ref)

        R = range(hb)
        qc = [slice(h * DN_DK, (h + 1) * DN_DK) for h in R]
        vc = [slice(h * DN_DV, (h + 1) * DN_DV) for h in R]
        X = _gdn_chunk([q_ref[:, qc[h]] for h in R], [k_ref[:, qc[h]] for h in R], [v_ref[:, vc[h]] for h in R],
                       [gc_ref[h] for h in R], [bc_ref[h] for h in R], [gr_ref[h] for h in R])
        S = [S_ref[h] for h in R]
        for h in R:
            ss_ref[h, 0] = S[h]
        wS = [_dot(X.w[h], S[h]) for h in R]
        qS = [_dot(X.qg[h], S[h]) for h in R]
        vn = [X.u[h] - wS[h] for h in R]
        av = [_dot(X.attn[h], vn[h]) for h in R]
        kv = [_dot_tn(X.kg[h], vn[h]) for h in R]
        for h in R:
            o_ref[:, vc[h]] = qS[h] + av[h]
            S_ref[h] = S[h] * X.egl[h][:, 0:1] + kv[h]

    qk = pl.BlockSpec((CHUNK, hb * DN_DK), lambda g, c: (c, g))
    vv = pl.BlockSpec((CHUNK, hb * DN_DV), lambda g, c: (c, g))
    col = pl.BlockSpec((hb, CHUNK, LANE), lambda g, c: (g, c, 0))
    row = pl.BlockSpec((hb, SUB, LANE), lambda g, c: (g, c, 0))
    return pl.pallas_call(
        body, name="gdn_fwd", grid=(HEADS // hb, nc), in_specs=[qk, qk, vv, col, col, row],
        out_specs=[vv, pl.BlockSpec((hb, 1, DN_DK, DN_DV), lambda g, c: (g, c, 0, 0))],
        out_shape=[jax.ShapeDtypeStruct((tp, DN_V), f32), jax.ShapeDtypeStruct((HEADS, nc, DN_DK, DN_DV), f32)],
        scratch_shapes=[pltpu.VMEM((hb, DN_DK, DN_DV), f32)], compiler_params=_params(2),
    )(q, k, v, gc, bc, grow)


def gdn_bwd(q, k, v, gc, bc, grow, states, do, rider=None):
    tp = q.shape[0]
    nc = tp // CHUNK
    C = CHUNK
    hb = GDN_HEADS_PER_STEP
    grid = (HEADS // hb, nc)
    split, ride_first, ride_last = _ride(rider, 8, 5, grid)

    def body(*refs):
        ((q_ref, k_ref, v_ref, gc_ref, bc_ref, gr_ref, ss_ref, do_ref), (dq_ref, dk_ref, dv_ref, dg_ref, db_ref),
         (rin, rout, rest)) = split(refs)
        dS_ref, ride = rest[0], (rin, rout, rest[1:])
        ride_first(ride)
        c = pl.program_id(1)

        @pl.when(c == 0)
        def _():
            dS_ref[...] = jnp.zeros_like(dS_ref)

        R = range(hb)
        qc = [slice(h * DN_DK, (h + 1) * DN_DK) for h in R]
        vc = [slice(h * DN_DV, (h + 1) * DN_DV) for h in R]
        k_ = [k_ref[:, qc[h]] for h in R]
        v_ = [v_ref[:, vc[h]] for h in R]
        bcol = [bc_ref[h] for h in R]
        X = _gdn_chunk([q_ref[:, qc[h]] for h in R], k_, v_, [gc_ref[h] for h in R], bcol, [gr_ref[h] for h in R])
        ri, ci = X.ri, X.ci
        S = [ss_ref[h, 0] for h in R]
        do_ = [do_ref[:, vc[h]] for h in R]
        dSn = [dS_ref[h] for h in R]
        wS = [_dot(X.w[h], S[h]) for h in R]
        ado = [_dot_tn(X.attn[h], do_[h]) for h in R]
        kdS = [_dot(X.kg[h], dSn[h]) for h in R]
        d_qg = [_dot_nt(do_[h], S[h]) for h in R]
        qdo = [_dot_tn(X.qg[h], do_[h]) for h in R]
        vn = [X.u[h] - wS[h] for h in R]
        d_vn = [ado[h] + kdS[h] for h in R]
        dovn = [_dot_nt(do_[h], vn[h]) for h in R]
        d_kg = [_dot_nt(vn[h], dSn[h]) for h in R]
        wdv = [_dot_tn(X.w[h], d_vn[h]) for h in R]
        dw = [-_dot_nt(d_vn[h], S[h]) for h in R]
        for h in R:
            dS_ref[h] = qdo[h] + X.egl[h][:, 0:1] * dSn[h] - wdv[h]
        dattn = [jnp.where(ri >= ci, dovn[h], 0.0) for h in R]
        dRu = [_hdot_tn(X.T[h], d_vn[h]) for h in R]
        dRw = [_hdot_tn(X.T[h], dw[h]) for h in R]
        dAu = [_hdot_nt(dRu[h], X.u[h]) for h in R]
        dAw = [_hdot_nt(dRw[h], X.w[h]) for h in R]
        dA = [jnp.where(ri > ci, -(dAu[h] + dAw[h]), 0.0) for h in R]
        dKK = [dA[h] * X.Dm[h] for h in R]
        dQK = [dattn[h] * X.Dm[h] for h in R]
        E = [dA[h] * X.A[h] + dattn[h] * X.attn[h] for h in R]
        dkb = [_dot(dKK[h], k_[h]) + dRw[h] * X.eg[h] for h in R]
        dk1 = [_dot_tn(dKK[h], X.kb[h]) for h in R]
        dqs = [_dot(dQK[h], k_[h]) + d_qg[h] * X.eg[h] for h in R]
        dk2 = [_dot_tn(dQK[h], X.qs[h]) for h in R]
        ones = jnp.ones((C, LANE), f32)
        colE = [_hdot_tn(E[h], ones) for h in R]
        rows = lax.broadcasted_iota(jnp.int32, (C, LANE), 0)
        dgam = []
        for h in R:
            t = d_kg[h] * X.kg[h]
            dgl = _allsum(t) + X.egl[h][:, 0:1] * _allsum(S[h] * dSn[h])
            g = (_rowsum(E[h]) - colE[h] + _rowsum(dRw[h] * (X.kb[h] * X.eg[h])) + _rowsum(d_qg[h] * X.qg[h])
                 - _rowsum(t))
            dgam.append(g + jnp.where(rows == C - 1, dgl, 0.0))
        for h in R:
            dv_ref[:, vc[h]] = dRu[h] * X.b2[h]
            dbeta = _rowsum(dRu[h] * v_[h]) + _rowsum(dkb[h] * k_[h])
            dq_ref[:, qc[h]] = dqs[h] * (DN_DK ** -0.5)
            dk_ref[:, qc[h]] = dk1[h] + dk2[h] + dkb[h] * bcol[h] + d_kg[h] * X.kdec[h]
            dg_ref[h] = dgam[h]
            db_ref[h] = jnp.broadcast_to(dbeta, (C, LANE))
        ride_last(ride)

    rc = lambda c: nc - 1 - c
    qk = pl.BlockSpec((CHUNK, hb * DN_DK), lambda g, c: (rc(c), g))
    vv = pl.BlockSpec((CHUNK, hb * DN_DV), lambda g, c: (rc(c), g))
    col = pl.BlockSpec((hb, CHUNK, LANE), lambda g, c: (g, rc(c), 0))
    row = pl.BlockSpec((hb, SUB, LANE), lambda g, c: (g, rc(c), 0))
    st = pl.BlockSpec((hb, 1, DN_DK, DN_DV), lambda g, c: (g, rc(c), 0, 0))
    r_ins = rider.ins if rider else []
    r_outs = rider.out_shapes if rider else []
    res = pl.pallas_call(
        body, name="gdn_bwd", grid=grid, in_specs=[qk, qk, vv, col, col, row, st, vv] + [_ANY] * len(r_ins),
        out_specs=[qk, qk, vv, col, col] + [_ANY] * len(r_outs),
        out_shape=[jax.ShapeDtypeStruct((tp, DN_QK), f32), jax.ShapeDtypeStruct((tp, DN_QK), f32),
                   jax.ShapeDtypeStruct((tp, DN_V), f32), jax.ShapeDtypeStruct((HEADS, tp, LANE), f32),
                   jax.ShapeDtypeStruct((HEADS, tp, LANE), f32)] + list(r_outs),
        scratch_shapes=[pltpu.VMEM((hb, DN_DK, DN_DV), f32)] + (rider.scratch if rider else []),
        compiler_params=_params(2),
    )(q, k, v, gc, bc, grow, states, do, *r_ins)
    return res[:5], res[5:]


def _cumsum_after(x, nb, us):
    B, n = SB_BLOCK, x.shape[0]
    hi = x.astype(bf16)
    lo = (x - hi.astype(f32)).astype(bf16)
    rows = [p[:, b * B:(b + 1) * B] for p in (hi, lo) for b in range(nb)]
    r = jnp.dot(jnp.concatenate(rows, axis=0), us, preferred_element_type=f32)
    out = [r[b * n:(b + 1) * n] + r[(nb + b) * n:(nb + b + 1) * n] for b in range(nb)]
    return out[0] if nb == 1 else jnp.concatenate(out, axis=1)


def _later_blocks(x, nb, carry):
    B = SB_BLOCK
    tot = [_rowsum(x[:, b * B:(b + 1) * B]) for b in range(nb)]
    offs = [None] * nb
    run = carry
    for b in range(nb - 1, -1, -1):
        offs[b] = jnp.broadcast_to(run, (x.shape[0], B))
        run = run + tot[b]
    return (offs[0] if nb == 1 else jnp.concatenate(offs, axis=1)), run


def _sb_group(i, t):
    top = (i + 1) * (SB_QB // SB_BLOCK) - 1 - SB_GROUP * t
    jlo = jnp.maximum(top - SB_GROUP + 1, 0)
    rows = pl.ds(pl.multiple_of(jlo * SB_BLOCK, SB_BLOCK), SB_GROUP * SB_BLOCK)
    return jlo, rows, (top + 1) * SB_BLOCK


def _sb_weights(q, kcat, i, jlo, kend, cs, us, masked):
    B, nb = SB_BLOCK, SB_GROUP
    R = range(len(q))
    z = [_dot_nt(q[h], kcat[h]) * (SB_DH ** -0.5) for h in R]
    e = [jnp.exp(-jnp.abs(z[h])) for h in R]
    l1p = [jnp.log(1.0 + e[h]) for h in R]
    lsp = [jnp.minimum(z[h], 0.0) - l1p[h] for h in R]
    lk = [lsp[h] - z[h] for h in R]
    vis = None
    if masked:
        qpos = i * SB_QB + lax.broadcasted_iota(jnp.int32, (SB_QB, nb * B), 0)
        kpos = jlo * B + lax.broadcasted_iota(jnp.int32, (SB_QB, nb * B), 1)
        vis = jnp.logical_and(kpos < jnp.minimum(qpos, kend), kpos >= P0)
        lk = [jnp.where(vis, lk[h], 0.0) for h in R]
    later = [_later_blocks(lk[h], nb, cs[h]) for h in R]
    cum = [_cumsum_after(lk[h], nb, us) for h in R]
    w = [jnp.exp(lsp[h] + cum[h] + later[h][0]) for h in R]
    if masked:
        w = [jnp.where(vis, w[h], 0.0) for h in R]
    return lsp, vis, w, [later[h][1] for h in R]


def _sb_loop(i, step, carry):
    trips = ((i + 1) * (SB_QB // SB_BLOCK) - 1 + SB_GROUP) // SB_GROUP
    carry = step(True)(0, carry)
    carry = lax.fori_loop(1, trips - 1, step(False), carry)
    return lax.fori_loop(jnp.maximum(trips - 1, 1), trips, step(True), carry)


def _ride(rider, n_in, n_out, grid):
    n_rin = len(rider.ins) if rider else 0
    n_rout = len(rider.out_shapes) if rider else 0

    def split(refs):
        ins, rin = refs[:n_in], refs[n_in:n_in + n_rin]
        outs = refs[n_in + n_rin:n_in + n_rin + n_out]
        rout = refs[n_in + n_rin + n_out:n_in + n_rin + n_out + n_rout]
        return ins, outs, (rin, rout, refs[n_in + n_rin + n_out + n_rout:])

    def at(step, fn, r):
        if rider is None:
            return
        cond = None
        for a, g in enumerate(grid):
            c = pl.program_id(a) == (g - 1 if step == "last" else 0)
            cond = c if cond is None else jnp.logical_and(cond, c)

        @pl.when(cond)
        def _():
            fn(*r)

    first = lambda r: at("first", rider.start if rider else None, r)
    last = lambda r: at("last", rider.finish if rider else None, r)
    return split, first, last


def sb_fwd(qs, ks, vs, rider=None):
    tp = qs.shape[0]
    nq = tp // SB_QB
    B, G, hb, QB = SB_BLOCK, SB_GROUP, SB_FWD_HEADS_PER_STEP, SB_QB
    assert tp >= G * B and tp % QB == 0 and QB % B == 0 and G * B >= QB
    grid = (HEADS // hb, nq)
    split, ride_first, ride_last = _ride(rider, 3, 2, grid)

    def body(*refs):
        (q_ref, k_ref, v_ref), (o_ref, ob_ref), ride = split(refs)
        ride_first(ride)
        i = pl.program_id(1)
        R = range(hb)
        hs = [slice(h * SB_DH, (h + 1) * SB_DH) for h in R]
        q = [q_ref[:, hs[h]] for h in R]
        us = (lax.broadcasted_iota(jnp.int32, (B, B), 0) > lax.broadcasted_iota(jnp.int32, (B, B), 1)).astype(bf16)

        def make_step(masked):
            def step(t, carry):
                acc, cs = carry
                jlo, rows, kend = _sb_group(i, t)
                _, _, w, cs = _sb_weights(q, [k_ref[rows, hs[h]] for h in R], i, jlo, kend, cs, us, masked)
                pv = [_dot(w[h], v_ref[rows, hs[h]]) for h in R]
                return tuple(acc[h] + pv[h] for h in R), tuple(cs)
            return step

        carry = (tuple(jnp.zeros((QB, SB_DH), f32) for _ in R), tuple(jnp.zeros((QB, 1), f32) for _ in R))
        acc, _ = _sb_loop(i, make_step, carry)
        for h in R:
            o_ref[:, hs[h]] = acc[h]
            ob_ref[:, hs[h]] = acc[h].astype(bf16)
        ride_last(ride)

    blk = pl.BlockSpec((QB, hb * SB_DH), lambda g, i: (i, g))
    full = pl.BlockSpec((tp, hb * SB_DH), lambda g, i: (0, g))
    r_ins = rider.ins if rider else []
    r_outs = rider.out_shapes if rider else []
    res = pl.pallas_call(
        body, name="sb_fwd", grid=grid, in_specs=[blk, full, full] + [_ANY] * len(r_ins),
        out_specs=[blk, blk] + [_ANY] * len(r_outs),
        out_shape=[jax.ShapeDtypeStruct((tp, SB_W), f32), jax.ShapeDtypeStruct((tp, SB_W), bf16)] + list(r_outs),
        scratch_shapes=rider.scratch if rider else [], compiler_params=_params(2),
    )(qs, ks, vs, *r_ins)
    return res[0], res[1], res[2:]


def sb_bwd(qs, ks, vs, o, do, rider=None):
    tp = qs.shape[0]
    nq = tp // SB_QB
    B, G, hb, QB = SB_BLOCK, SB_GROUP, SB_HEADS_PER_STEP, SB_QB
    assert tp >= G * B and tp % QB == 0 and QB % B == 0 and G * B >= QB
    grid = (HEADS // hb, nq)
    split, ride_first, ride_last = _ride(rider, 5, 3, grid)

    def body(*refs):
        (q_ref, k_ref, v_ref, o_ref, do_ref), (dq_ref, dk_ref, dv_ref), ride = split(refs)
        ride_first(ride)
        i = pl.program_id(1)

        @pl.when(i == 0)
        def _():
            dk_ref[...] = jnp.zeros_like(dk_ref)
            dv_ref[...] = jnp.zeros_like(dv_ref)

        R = range(hb)
        hs = [slice(h * SB_DH, (h + 1) * SB_DH) for h in R]
        q = [q_ref[:, hs[h]] for h in R]
        dob = [do_ref[:, hs[h]].astype(bf16) for h in R]
        et = [_rowsum(dob[h].astype(f32) * o_ref[:, hs[h]]) for h in R]
        us = (lax.broadcasted_iota(jnp.int32, (B, B), 0) > lax.broadcasted_iota(jnp.int32, (B, B), 1)).astype(bf16)

        def make_step(masked):
            def step(t, carry):
                dq, cs, ce = carry
                jlo, rows, kend = _sb_group(i, t)
                kcat = [k_ref[rows, hs[h]] for h in R]
                dwv = [_dot_nt(dob[h], v_ref[rows, hs[h]]) for h in R]
                lsp, vis, w, cs = _sb_weights(q, kcat, i, jlo, kend, cs, us, masked)
                wb = [w[h].astype(bf16) for h in R]
                ee = [dwv[h] * wb[h].astype(f32) for h in R]
                later = [_later_blocks(ee[h], G, ce[h]) for h in R]
                cum = [_cumsum_after(ee[h], G, us) for h in R]
                dz = []
                for h in R:
                    d = ee[h] - jnp.exp(lsp[h]) * (et[h] - (cum[h] + later[h][0]))
                    if masked:
                        d = jnp.where(vis, d, 0.0)
                    dz.append((d * (SB_DH ** -0.5)).astype(bf16))
                dkj = [_dot_tn(dz[h], q[h]) for h in R]
                dvj = [_dot_tn(wb[h], dob[h]) for h in R]
                dqj = [_dot(dz[h], kcat[h]) for h in R]
                for h in R:
                    dk_ref[rows, hs[h]] += dkj[h]
                    dv_ref[rows, hs[h]] += dvj[h]
                return tuple(dq[h] + dqj[h] for h in R), tuple(cs), tuple(later[h][1] for h in R)
            return step

        z0 = tuple(jnp.zeros((QB, 1), f32) for _ in R)
        dq, _, _ = _sb_loop(i, make_step, (tuple(jnp.zeros((QB, SB_DH), f32) for _ in R), z0, z0))
        for h in R:
            dq_ref[:, hs[h]] = dq[h]
        ride_last(ride)

    blk = pl.BlockSpec((QB, hb * SB_DH), lambda g, i: (i, g))
    full = pl.BlockSpec((tp, hb * SB_DH), lambda g, i: (0, g))
    r_ins = rider.ins if rider else []
    r_outs = rider.out_shapes if rider else []
    res = pl.pallas_call(
        body, name="sb_bwd", grid=grid, in_specs=[blk, full, full, blk, blk] + [_ANY] * len(r_ins),
        out_specs=[blk, full, full] + [_ANY] * len(r_outs),
        out_shape=[jax.ShapeDtypeStruct((tp, SB_W), f32)] * 3 + list(r_outs),
        scratch_shapes=rider.scratch if rider else [], compiler_params=_params(2),
    )(qs, ks, vs, o, do, *r_ins)
    return res[:3], res[3:]


def adamw(w, g, m, v, name):
    r, c = w.shape
    rt = _tile(r, 128, SUB) if r % SUB == 0 else r
    blk = pl.BlockSpec((rt, c), lambda i: (i, 0))
    c1 =1.0 - ADAM_B1 ** ADAM_STEP
    c2 = 1.0 - ADAM_B2 ** ADAM_STEP

    def body(w_ref, g_ref, m_ref, v_ref, d_ref, mo_ref, vo_ref):
        g_ = g_ref[...]
        m_ = ADAM_B1 * m_ref[...] + (1.0 - ADAM_B1) * g_
        v_ = ADAM_B2 * v_ref[...] + (1.0 - ADAM_B2) * (g_ * g_)
        mo_ref[...] = m_
        vo_ref[...] = v_
        d_ref[...] = -ADAM_LR * ((m_ / c1) / (jnp.sqrt(v_ / c2) + ADAM_EPS) + ADAM_WD * w_ref[...])

    return pl.pallas_call(
        body, name=name, grid=(r // rt,), in_specs=[blk] * 4, out_specs=[blk] * 3,
        out_shape=[jax.ShapeDtypeStruct((r, c), f32)] * 3, compiler_params=_params(1),
    )(w, g, m, v)


def sum_slots(x, name):
    n, r, c = x.shape
    rt = _tile(r, 128, SUB) if r % SUB == 0 else r
    blk = pl.BlockSpec((n, rt, c), lambda i: (0, i, 0))

    def body(x_ref, o_ref):
        acc = x_ref[0].astype(f32)
        for s in range(1, n):
            acc = acc + x_ref[s].astype(f32)
        o_ref[...] = acc

    return pl.pallas_call(
        body, name=name, grid=(r // rt,), in_specs=[blk], out_specs=pl.BlockSpec((rt, c), lambda i: (i, 0)),
        out_shape=jax.ShapeDtypeStruct((r, c), f32), compiler_params=_params(1),
    )(x)


def add2(a, b, name, out_dtype=f32):
    n, r, c = a.shape
    rt = _tile(r, 64, SUB) if r % SUB == 0 else r
    blk = pl.BlockSpec((n, rt, c), lambda i: (0, i, 0))

    def body(a_ref, b_ref, o_ref):
        o_ref[...] = (a_ref[...] + b_ref[...]).astype(out_dtype)

    return pl.pallas_call(
        body, name=name, grid=(r // rt,), in_specs=[blk, blk], out_specs=blk,
        out_shape=jax.ShapeDtypeStruct((n, r, c), out_dtype), compiler_params=_params(1),
    )(a, b)


_ANY = pl.BlockSpec(memory_space=pl.ANY)
_MESH = pl.DeviceIdType.MESH


def _coords():
    return lax.axis_index("x"), lax.axis_index("y"), lax.axis_index("c")


def _chip_peer(x, y, r):
    return x ^ (r >> 1), y ^ (r & 1)


class _Exchange:
    def __init__(self, ins, out_shapes, scratch, start, finish):
        self.ins, self.out_shapes, self.scratch, self.start, self.finish = ins, out_shapes, scratch, start, finish

    def split(self, refs):
        n, m = len(self.ins), len(self.out_shapes)
        return refs[:n], refs[n:n + m], refs[n + m:]


def run_exchange(ex, name):
    def body(*refs):
        ins, outs, sems = ex.split(refs)
        ex.start(ins, outs, sems)
        ex.finish(ins, outs, sems)

    return pl.pallas_call(body, name=name, in_specs=[_ANY] * len(ex.ins), out_specs=[_ANY] * len(ex.out_shapes),
                          out_shape=ex.out_shapes, scratch_shapes=ex.scratch)(*ex.ins)


def gather_chips(big, small):
    nb, n = len(big), len(big) + len(small)
    shards = list(big) + list(small)
    kb = nb * (N_CHIPS - 1)
    k = n * (N_CHIPS - 1)

    def copies(src, dst, sems):
        send, recv, fsend, frecv = sems
        x, y, c = _coords()
        sib = (x, y, 1 - c)
        peers = [_chip_peer(x, y, r) for r in range(1, N_CHIPS)]

        def direct(t, j, slot):
            s = t * (N_CHIPS - 1) + j
            if t < nb:
                return pltpu.make_async_remote_copy(src[t].at[c], dst[t].at[slot, c], send.at[s], recv.at[s],
                                                    device_id=(*peers[j], c), device_id_type=_MESH)
            return pltpu.make_async_remote_copy(src[t], dst[t].at[slot], send.at[s], recv.at[s],
                                                device_id=(*peers[j], c), device_id_type=_MESH)

        def passed(t, j, half):
            s = t * (N_CHIPS - 1) + j
            px, py = peers[j]
            part = dst[t].at[2 * px + py, half]
            return pltpu.make_async_remote_copy(part, part, fsend.at[s], frecv.at[s], device_id=sib, device_id_type=_MESH)

        return direct, passed, peers, 2 * x + y, c

    def start(src, dst, sems):
        direct, _, _, me, _ = copies(src, dst, sems)
        for t in range(n):
            for j in range(N_CHIPS - 1):
                direct(t, j, me).start()

    def finish(src, dst, sems):
        direct, passed, peers, me, c = copies(src, dst, sems)
        fwd = []
        for t in range(nb):
            for j in range(N_CHIPS - 1):
                px, py = peers[j]
                direct(t, j, 2 * px + py).wait_recv()
                fwd.append(passed(t, j, c))
                fwd[-1].start()
        for t in range(nb, n):
            for j in range(N_CHIPS - 1):
                px, py = peers[j]
                direct(t, j, 2 * px + py).wait_recv()
        for t in range(nb):
            for j in range(N_CHIPS - 1):
                passed(t, j, 1 - c).wait_recv()
        for t in range(n):
            for j in range(N_CHIPS - 1):
                direct(t, j, me).wait_send()
        for cp in fwd:
            cp.wait_send()

    return _Exchange(shards, [jax.ShapeDtypeStruct((N_CHIPS,) + s.shape, s.dtype) for s in shards],
                     [pltpu.SemaphoreType.DMA((k,)), pltpu.SemaphoreType.DMA((k,)),
                      pltpu.SemaphoreType.DMA((max(kb, 1),)), pltpu.SemaphoreType.DMA((max(kb, 1),))], start, finish)


def sibling_swap(grads):
    n = len(grads)
    k = n * N_CHIPS

    def copies(src, dst, sems):
        send, recv = sems
        x, y, c = _coords()
        return [pltpu.make_async_remote_copy(src[t].at[o, 1 - c], dst[t].at[o], send.at[t * N_CHIPS + o],
                                             recv.at[t * N_CHIPS + o], device_id=(x, y, 1 - c), device_id_type=_MESH)
                for t in range(n) for o in range(N_CHIPS)]

    def start(src, dst, sems):
        for cp in copies(src, dst, sems):
            cp.start()

    def finish(src, dst, sems):
        cps = copies(src, dst, sems)
        for cp in cps:
            cp.wait_recv()
        for cp in cps:
            cp.wait_send()

    return _Exchange(list(grads), [jax.ShapeDtypeStruct((N_CHIPS,) + g.shape[2:], g.dtype) for g in grads],
                     [pltpu.SemaphoreType.DMA((k,)), pltpu.SemaphoreType.DMA((k,))], start, finish)


def scatter_chips(parts):
    n = len(parts)
    k = n * (N_CHIPS - 1)

    def copy(src, dst, sems, t, r, landing):
        send, recv = sems
        x, y, c = _coords()
        me = 2 * x + y
        px, py = _chip_peer(x, y, r)
        peer = 2 * px + py
        s = t * (N_CHIPS - 1) + r - 1
        return pltpu.make_async_remote_copy(src[t].at[me if landing else peer], dst[t].at[peer if landing else me],
                                            send.at[s], recv.at[s], device_id=(px, py, c), device_id_type=_MESH)

    def start(src, dst, sems):
        for t in range(n):
            for r in range(1, N_CHIPS):
                copy(src, dst, sems, t, r, False).start()

    def finish(src, dst, sems):
        for t in range(n):
            for r in range(1, N_CHIPS):
                copy(src, dst, sems, t, r, True).wait_recv()
        for t in range(n):
            for r in range(1, N_CHIPS):
                copy(src, dst, sems, t, r, False).wait_send()

    return _Exchange(list(parts), [jax.ShapeDtypeStruct(p.shape, p.dtype) for p in parts],
                     [pltpu.SemaphoreType.DMA((k,)), pltpu.SemaphoreType.DMA((k,))], start, finish)


def sibling_send(halves, name):
    n = len(halves)

    def body(*refs):
        src, dst = refs[:n], refs[n:2 * n]
        send, recv = refs[2 * n:]
        x, y, c = _coords()
        cps = [pltpu.make_async_remote_copy(src[t], dst[t], send.at[t], recv.at[t],
                                            device_id=(x, y, 1 - c), device_id_type=_MESH) for t in range(n)]
        for cp in cps:
            cp.start()
        for cp in cps:
            cp.wait_recv()
        for cp in cps:
            cp.wait_send()

    return pl.pallas_call(
        body, name=name, in_specs=[_ANY] * n, out_specs=[_ANY] * n,
        out_shape=[jax.ShapeDtypeStruct(h.shape, h.dtype) for h in halves],
        scratch_shapes=[pltpu.SemaphoreType.DMA((n,)), pltpu.SemaphoreType.DMA((n,))],
    )(*halves)


def gather_all(block, name):
    def body(src, dst, send, recv, loc):
        x, y, c = _coords()
        me = 4 * x + 2 * y + c
        mine = pltpu.make_async_copy(src, dst.at[me], loc)
        mine.start()
        outs = []
        for r in range(1, N_DEV):
            peer = (x ^ (r >> 2), y ^ ((r >> 1) & 1), c ^ (r & 1))
            outs.append(pltpu.make_async_remote_copy(src, dst.at[me], send.at[r - 1], recv.at[r - 1],
                                                     device_id=peer, device_id_type=_MESH))
        for cp in outs:
            cp.start()
        for r in range(1, N_DEV):
            px, py, pc = x ^ (r >> 2), y ^ ((r >> 1) & 1), c ^ (r & 1)
            pltpu.make_async_remote_copy(src, dst.at[4 * px + 2 * py + pc], send.at[r - 1], recv.at[r - 1],
                                         device_id=(px, py, pc), device_id_type=_MESH).wait_recv()
        for cp in outs:
            cp.wait_send()
        mine.wait()

    return pl.pallas_call(
        body, name=name, in_specs=[_ANY], out_specs=_ANY,
        out_shape=jax.ShapeDtypeStruct((N_DEV,) + block.shape, block.dtype),
        scratch_shapes=[pltpu.SemaphoreType.DMA((N_DEV - 1,)), pltpu.SemaphoreType.DMA((N_DEV - 1,)),
                        pltpu.SemaphoreType.DMA(())],
    )(block)


def _pad_lanes(v, n=LANE):
    return jnp.pad(v, ((0, 0), (0, n - v.shape[1])))


def _w_in_pieces():
    cs = (PROJ_BIG + 2 * HEADS) // N_CHIPS
    ab_end = AB_COL + 2 * HEADS
    out = []
    for o in range(N_CHIPS):
        lo, hi = o * cs, (o + 1) * cs
        cand = [("big", lo, min(hi, AB_COL), 0), ("ab", max(lo, AB_COL), min(hi, ab_end), AB_COL),
                ("big", max(lo, ab_end), hi, 2 * HEADS)]
        out.append([(s, a - off, b - off) for s, a, b, off in cand if a < b])
    return out


def _split_w_in(w4):
    big, ab = [], []
    for o, pieces in enumerate(_w_in_pieces()):
        at = 0
        for s, a, b in pieces:
            (big if s == "big" else ab).append(w4[o][:, at:at + b - a])
            at += b - a
    return jnp.concatenate(big, axis=1), _pad_lanes(jnp.concatenate(ab, axis=1))


def _join_w_in(big, ab):
    src = {"big": big, "ab": ab}
    return jnp.stack([jnp.concatenate([src[s][:, a:b] for s, a, b in pieces], axis=1) for pieces in _w_in_pieces()])


def _conv_w8(w):
    return jnp.pad(w, ((0, SUB - DN_CONV), (0, 0)))


def _row_layout(gc, tp):
    nc = tp // CHUNK
    g = gc[:, :, 0].reshape(HEADS, nc, 1, CHUNK)
    g = jnp.broadcast_to(g, (HEADS, nc, SUB, CHUNK))
    return jnp.pad(g, ((0, 0), (0, 0), (0, 0), (0, LANE - CHUNK))).reshape(HEADS, nc * SUB, LANE)


def _step(x, meta, W, target, late_weights=None, early_swap=None, early_grads=None, last_grads=None):
    W = dict(W)
    seq = x.shape[0]
    tp = P0 + N_META + seq
    h0 = jnp.concatenate([jnp.zeros((P0, D_MODEL), f32), meta, x], axis=0)
    w_big, w_ab = _split_w_in(W["w_in"])
    cq8, ck8, cv8 = _conv_w8(W["conv_q"]), _conv_w8(W["conv_k"]), _conv_w8(W["conv_v"])
    al, dtb = _pad_lanes(W["dn_a_log"]), _pad_lanes(W["dn_dt_bias"])

    n1 = rms_fwd(h0, W["norm_mix_gain"], "rms1_fwd")
    proj = matmul(n1, w_big, "nn", "proj_fwd")
    pab = matmul(n1, w_ab, "nn", "pab_fwd")
    qn = conv_fwd(proj, cq8, C_DQ * 8, 8, True, "conv_q_fwd")
    kn = conv_fwd(proj, ck8, C_DK * 8, 8, True, "conv_k_fwd")
    va = conv_fwd(proj, cv8, C_DV * 8, 16, False, "conv_v_fwd")
    gc, bc = ab_fwd(pab, al, dtb)
    grow = _row_layout(gc, tp)
    o_dn, states = gdn_fwd(qn, kn, va, gc, bc, grow)
    on = dn_out_fwd(o_dn, proj, W["dn_out_norm_gain"])
    qs, ks, vs = sb_prep_fwd(proj, W["sb_q_norm_gain"], W["sb_k_norm_gain"])
    o_sb, o_sb16, arrived = sb_fwd(qs, ks, vs, rider=late_weights[0] if late_weights else None)
    if late_weights:
        W.update(late_weights[1](arrived))
    ydn = matmul(on, W["w_branch_dn"], "nn", "ydn_fwd")
    ysb = matmul(o_sb16, W["w_branch_sb"], "nn", "ysb_fwd")
    merged = merge_fwd(proj, ydn, ysb)
    h1 = matmul(merged, W["w_out"], "nn", "wout_fwd", residual=h0)
    n2 = rms_fwd(h1, W["norm_ffn_gain"], "rms2_fwd")
    u = matmul(n2, W["w_ffn_in"], "nn", "ffn_in_fwd", tn_t=512)
    act = swiglu_fwd(u)
    y = matmul(act, W["w_ffn_out"], "nn", "ffn_out_fwd", residual=h1)
    dy, dy16, loss = loss_head(y, target)

    G = {}
    dact = matmul(dy16, W["w_ffn_out"], "nt", "ffn_out_dx", tn_t=1408)
    G["w_ffn_out"] = matmul(act, dy16, "tn", "ffn_out_dw", tm_t=1408)
    dgate, dup = swiglu_bwd(u, dact)
    du = jnp.concatenate([dgate, dup], axis=1)
    dn2 = matmul(du, W["w_ffn_in"], "nt", "ffn_in_dx", tk_t=512)
    G["w_ffn_in"] = matmul(n2, du, "tn", "ffn_in_dw", tn_t=512)
    dh1, dh1_16, G["norm_ffn_gain"] = rms_bwd(h1, W["norm_ffn_gain"], dn2, dy, "rms2_bwd")
    dmerged = matmul(dh1_16, W["w_out"], "nt", "wout_dx")
    G["w_out"] = matmul(merged, dh1_16, "tn", "wout_dw")
    dyd, dys, d_gates = merge_bwd(proj, ydn, ysb, dmerged)
    don = matmul(dyd, W["w_branch_dn"], "nt", "ydn_dx")
    G["w_branch_dn"] = matmul(on, dyd, "tn", "ydn_dw")
    do_sb = matmul(dys, W["w_branch_sb"], "nt", "ysb_dx")
    G["w_branch_sb"] = matmul(o_sb16, dys, "tn", "ysb_dw")
    do_dn, dz, G["dn_out_norm_gain"] = dn_out_bwd(o_dn, proj, W["dn_out_norm_gain"], don)
    (dqn, dkn, dva, dgc, dbc), swapped = gdn_bwd(qn, kn, va, gc, bc, grow, states, do_dn,
                                                 rider=early_swap[0](G) if early_swap else None)
    if early_swap:
        early_swap[1](swapped)
    dpab, dal, ddt = ab_bwd(pab, al, dtb, dgc, dbc)
    G["dn_a_log"], G["dn_dt_bias"] = dal[:, :HEADS], ddt[:, :HEADS]
    dyq, dcq = conv_bwd_act(proj, cq8, dqn, C_DQ * 8, 8, True, "conv_q_bwd")
    dyk, dck = conv_bwd_act(proj, ck8, dkn, C_DK * 8, 8, True, "conv_k_bwd")
    dyv, dcv = conv_bwd_act(proj, cv8, dva, C_DV * 8, 16, False, "conv_v_bwd")
    G["conv_q"], G["conv_k"], G["conv_v"] = dcq[:DN_CONV], dck[:DN_CONV], dcv[:DN_CONV]
    d_dq = conv_bwd_in(dyq, cq8, "conv_q_dx")
    d_dk = conv_bwd_in(dyk, ck8, "conv_k_dx")
    d_dv = conv_bwd_in(dyv, cv8, "conv_v_dx")
    (dqs, dks, dvs), delivered = sb_bwd(qs, ks, vs, o_sb, do_sb, rider=early_grads[0](G) if early_grads else None)
    if early_grads:
        early_grads[1](delivered)
    d_sb, G["sb_q_norm_gain"], G["sb_k_norm_gain"] = sb_prep_bwd(
        proj, W["sb_q_norm_gain"], W["sb_k_norm_gain"], dqs, dks, dvs)
    dproj = jnp.concatenate([d_dq, d_dk, d_dv, dz, d_sb, d_gates], axis=1)
    dw_big = matmul(n1, dproj, "tn", "proj_dw")
    dw_ab = matmul(n1, dpab, "tn", "pab_dw")
    G["w_in"] = _join_w_in(dw_big, dw_ab)
    if last_grads:
        dn1, delivered = matmul(dproj, w_big, "nt", "proj_dx", tk_t=1024, rider=last_grads[0](G))
        last_grads[1](delivered)
    else:
        dn1 = matmul(dproj, w_big, "nt", "proj_dx", tk_t=1024)
    dn1 = matmul(dpab, w_ab, "nt", "pab_dx", residual=dn1)
    dh0, _, G["norm_mix_gain"] = rms_bwd(h0, W["norm_mix_gain"], dn1, dh1, "rms1_bwd")
    G["meta_tokens"] = dh0[P0:P0 + N_META]
    return loss, dh0[P0 + N_META:], G


_BIG = ("w_in", "w_branch_dn", "w_branch_sb", "w_out", "w_ffn_in", "w_ffn_out")
_COL_SHARDED = ("w_in", "w_ffn_in", "meta_tokens", "conv_q", "conv_k", "conv_v")
_SMALL_REPL = ("norm_mix_gain", "norm_ffn_gain", "dn_a_log", "dn_dt_bias", "dn_out_norm_gain", "sb_q_norm_gain",
               "sb_k_norm_gain")
_SMALL_SHARD = ("meta_tokens", "conv_q", "conv_k", "conv_v")
_ORDER = ("meta_tokens", "norm_mix_gain", "w_in", "conv_q", "conv_k", "conv_v", "dn_a_log", "dn_dt_bias",
          "dn_out_norm_gain", "sb_q_norm_gain", "sb_k_norm_gain", "w_branch_dn", "w_branch_sb", "w_out",
          "norm_ffn_gain", "w_ffn_in", "w_ffn_out")


def _unshard(g4, name):
    if name in _COL_SHARDED:
        r, cs = g4.shape[1:]
        return jnp.transpose(g4, (1, 0, 2)).reshape(r, N_CHIPS * cs)
    return g4.reshape((-1,) + g4.shape[2:])


def _to_shards(full, name):
    if full.ndim == 3:
        return full
    if name in _COL_SHARDED:
        r, c = full.shape
        return jnp.transpose(full.reshape(r, N_CHIPS, c // N_CHIPS), (1, 0, 2))
    r, c = full.shape
    return full.reshape(N_CHIPS, r // N_CHIPS, c)


def _rows_1024(a):
    r, c = a.shape
    if c >= 1024:
        return a.reshape(r * (c // 1024), 1024)
    return jnp.pad(a, ((0, 0), (0, 1024 - c)))


def kernel(x, meta_tokens, norm_mix_gain, w_in, conv_q, conv_k, conv_v, dn_a_log, dn_dt_bias, dn_out_norm_gain, sb_q_norm_gain, sb_k_norm_gain, w_branch_dn, w_branch_sb, w_out, norm_ffn_gain, w_ffn_in, w_ffn_out, loss_target, m_meta_tokens, m_norm_mix_gain, m_w_in, m_conv_q, m_conv_k, m_conv_v, m_dn_a_log, m_dn_dt_bias, m_dn_out_norm_gain, m_sb_q_norm_gain, m_sb_k_norm_gain, m_w_branch_dn, m_w_branch_sb, m_w_out, m_norm_ffn_gain, m_w_ffn_in, m_w_ffn_out, v_meta_tokens, v_norm_mix_gain, v_w_in, v_conv_q, v_conv_k, v_conv_v, v_dn_a_log, v_dn_dt_bias, v_dn_out_norm_gain, v_sb_q_norm_gain, v_sb_k_norm_gain, v_w_branch_dn, v_w_branch_sb, v_w_out, v_norm_ffn_gain, v_w_ffn_in, v_w_ffn_out):
    Wl = dict(meta_tokens=meta_tokens, norm_mix_gain=norm_mix_gain, w_in=w_in[0], conv_q=conv_q[0], conv_k=conv_k[0],
              conv_v=conv_v[0], dn_a_log=dn_a_log, dn_dt_bias=dn_dt_bias, dn_out_norm_gain=dn_out_norm_gain,
              sb_q_norm_gain=sb_q_norm_gain, sb_k_norm_gain=sb_k_norm_gain, w_branch_dn=w_branch_dn[0],
              w_branch_sb=w_branch_sb[0], w_out=w_out[0], norm_ffn_gain=norm_ffn_gain, w_ffn_in=w_ffn_in[0],
              w_ffn_out=w_ffn_out[0])
    Ml = dict(meta_tokens=m_meta_tokens, norm_mix_gain=m_norm_mix_gain, w_in=m_w_in[0], conv_q=m_conv_q[0],
              conv_k=m_conv_k[0], conv_v=m_conv_v[0], dn_a_log=m_dn_a_log, dn_dt_bias=m_dn_dt_bias,
              dn_out_norm_gain=m_dn_out_norm_gain, sb_q_norm_gain=m_sb_q_norm_gain, sb_k_norm_gain=m_sb_k_norm_gain,
              w_branch_dn=m_w_branch_dn[0], w_branch_sb=m_w_branch_sb[0], w_out=m_w_out[0],
              norm_ffn_gain=m_norm_ffn_gain, w_ffn_in=m_w_ffn_in[0], w_ffn_out=m_w_ffn_out[0])
    Vl = dict(meta_tokens=v_meta_tokens, norm_mix_gain=v_norm_mix_gain, w_in=v_w_in[0], conv_q=v_conv_q[0],
              conv_k=v_conv_k[0], conv_v=v_conv_v[0], dn_a_log=v_dn_a_log, dn_dt_bias=v_dn_dt_bias,
              dn_out_norm_gain=v_dn_out_norm_gain, sb_q_norm_gain=v_sb_q_norm_gain, sb_k_norm_gain=v_sb_k_norm_gain,
              w_branch_dn=v_w_branch_dn[0], w_branch_sb=v_w_branch_sb[0], w_out=v_w_out[0],
              norm_ffn_gain=v_norm_ffn_gain, w_ffn_in=v_w_ffn_in[0], w_ffn_out=v_w_ffn_out[0])
    lead = {n: (1,) if (n in _BIG or n in ("conv_q", "conv_k", "conv_v")) else () for n in _ORDER}

    chip = 2 * lax.axis_index("x") + lax.axis_index("y")
    c = lax.axis_index("c")
    halved = {n: Wl[n].astype(bf16).reshape(2, Wl[n].shape[0] // 2, Wl[n].shape[1]) for n in _BIG}

    def gathered_weights(names, owns, outs):
        res = {}
        for n, own, g4 in zip(names, owns, outs):
            g4 = lax.dynamic_update_slice(g4, own[None], (chip,) + (0,) * own.ndim)
            if n in _BIG:
                g4 = g4.reshape(N_CHIPS, 2 * g4.shape[2], g4.shape[3])
            res[n] = g4 if n == "w_in" else _unshard(g4, n)
        return res

    first = ["w_in"] + list(_SMALL_SHARD)
    first_own = [halved["w_in"]] + [Wl[n] for n in _SMALL_SHARD]
    W = dict(Wl)
    W.update(gathered_weights(first, first_own, run_exchange(gather_chips(first_own[:1], first_own[1:]), "gather_w_in")))
    late = [n for n in _BIG if n != "w_in"]
    late_own = [halved[n] for n in late]
    for n in late:
        del W[n]

    def halves_of(names, G):
        g4 = [_to_shards(G[n], n) for n in names]
        return [g.reshape(N_CHIPS, 2, g.shape[1] // 2, g.shape[2]) for g in g4]

    def pair_added(g42, from_sib, tag, wire):
        mine = [lax.dynamic_index_in_dim(g, c, axis=1, keepdims=False) for g in g42]
        return [add2(a, b, "grad_pair_add_%s%d" % (tag, t), out_dtype=wire)
                for t, (a, b) in enumerate(zip(mine, from_sib))]

    def chip_reduced(parts, slots, tag):
        slots = [lax.dynamic_update_slice(s, lax.dynamic_index_in_dim(p, chip, axis=0, keepdims=True), (chip, 0, 0))
                 for s, p in zip(slots, parts)]
        return [sum_slots(s, "grad_chip_sum_%s%d" % (tag, t)) for t, s in enumerate(slots)]

    early, last = {}, {}

    def early_swap_begin(G):
        early["g42"] = halves_of(late, G)
        return sibling_swap(early["g42"])

    def early_begin(G):
        early["parts"] = pair_added(early["g42"], early["from_sib"], "a", f32)
        return scatter_chips(early["parts"])

    def last_begin(G):
        g42 = halves_of(["w_in"], G)
        last["parts"] = pair_added(g42, run_exchange(sibling_swap(g42), "grad_sibling_swap_b"), "b", bf16)
        return scatter_chips(last["parts"])

    loss, grad_x, G = _step(
        x[0], W["meta_tokens"], W, loss_target[0],
        late_weights=(gather_chips(late_own, []), lambda outs: gathered_weights(late, late_own, outs)),
        early_swap=(early_swap_begin, lambda outs: early.update(from_sib=outs)),
        early_grads=(early_begin, lambda slots: early.update(halves=chip_reduced(early["parts"], slots, "a"))),
        last_grads=(last_begin, lambda slots: last.update(halves=chip_reduced(last["parts"], slots, "b"))))
    halves = last["halves"] + early["halves"]
    theirs = sibling_send(halves, "grad_sibling_send")
    Gs = {}
    for n, h, o in zip(["w_in"] + late, halves, theirs):
        Gs[n] = lax.dynamic_update_slice(jnp.concatenate([o, o], axis=0), h, (c * h.shape[0], 0))

    small_names = list(_SMALL_REPL) + list(_SMALL_SHARD)
    pieces = [_rows_1024(G[n]) for n in small_names] + [_rows_1024(loss)]
    counts = [p.shape[0] for p in pieces]
    pack = jnp.concatenate(pieces, axis=0)
    pad_rows = (-pack.shape[0]) % SUB
    pack = jnp.pad(pack, ((0, pad_rows), (0, 0)))
    total = sum_slots(gather_all(pack, "small_gather"), "small_sum")
    chip = 2 * lax.axis_index("x") + lax.axis_index("y")
    row = 0
    for n, cnt in zip(small_names, counts[:-1]):
        blk = total[row:row + cnt]
        row += cnt
        full_shape = G[n].shape
        if full_shape[1] >= 1024:
            blk = blk.reshape(full_shape)
        else:
            blk = blk[:, :full_shape[1]]
        if n in _SMALL_SHARD:
            cs = full_shape[1] // N_CHIPS
            blk = lax.dynamic_slice_in_dim(blk, chip * cs, cs, axis=1)
        Gs[n] = blk
    loss_out = total[row, 0]

    grads, deltas, new_m, new_v = [], [], [], []
    for n in _ORDER:
        d, m2, v2 = adamw(Wl[n], Gs[n], Ml[n], Vl[n], "adamw_" + n)
        shape = lead[n] + Wl[n].shape
        grads.append(Gs[n].reshape(shape))
        deltas.append(d.reshape(shape))
        new_m.append(m2.reshape(shape))
        new_v.append(v2.reshape(shape))
    return (loss_out, grad_x[None], *grads, *deltas, *new_m, *new_v)
```

```python
import jax
import jax.numpy as jnp
from jax import lax
from jax.experimental import pallas as pl
from jax.experimental.pallas import tpu as pltpu

f32 = jnp.float32
bf16 = jnp.bfloat16

D_MODEL = 1024
N_META = 16
CHUNK = 64
HEADS = 8
DN_DK = 128
DN_DV = 256
DN_CONV = 4
DN_QK = HEADS * DN_DK
DN_V = HEADS * DN_DV
SB_DH = 128
SB_W = HEADS * SB_DH
SB_BLOCK = 128
SB_QB = 384
SB_GROUP = 4
SB_HEADS_PER_STEP = 2
SB_FWD_HEADS_PER_STEP = 4
GDN_HEADS_PER_STEP = 8
CONV_W = 1024
D_FF = 2816
RMS_EPS = 1e-6
L2_EPS = 1e-6
ADAM_LR = 0.001
ADAM_B1 = 0.9
ADAM_B2 = 0.999
ADAM_EPS = 1e-08
ADAM_WD = 0.01
ADAM_STEP = 10

P0 = 112
LANE = 128
SUB = 8
VMEM_LIMIT = 48 * 1024 * 1024
N_CHIPS = 4
N_DEV = 8

C_DQ, C_DK, C_DV, C_DZ, C_SQ, C_SK, C_SV, C_GDN, C_GSB = 0, 1, 2, 4, 6, 7, 8, 9, 10
PROJ_BIG = 11 * 1024
AB_COL = 2 * DN_QK + 2 * DN_V


def _params(n_axes):
    return pltpu.CompilerParams(dimension_semantics=("arbitrary",) * n_axes, vmem_limit_bytes=VMEM_LIMIT)


def _tile(n, target, q=LANE):
    best = None
    for t in range(q, min(n, target) + 1, q):
        if n % t == 0:
            best = t
    return best if best is not None else n


def _dot(a, b):
    return jnp.dot(a.astype(bf16), b.astype(bf16), preferred_element_type=f32)


def _dot_nt(a, b):
    return lax.dot_general(a.astype(bf16), b.astype(bf16), (((1,), (1,)), ((), ())), preferred_element_type=f32)


def _dot_tn(a, b):
    return lax.dot_general(a.astype(bf16), b.astype(bf16), (((0,), (0,)), ((), ())), preferred_element_type=f32)


_HI = lax.Precision.HIGH


def _hdot(a, b):
    return jnp.dot(a, b, precision=_HI, preferred_element_type=f32)


def _hdot_nt(a, b):
    return lax.dot_general(a, b, (((1,), (1,)), ((), ())), precision=_HI, preferred_element_type=f32)


def _hdot_tn(a, b):
    return lax.dot_general(a, b, (((0,), (0,)), ((), ())), precision=_HI, preferred_element_type=f32)


def _sigmoid(x):
    return 0.5 * jnp.tanh(0.5 * x) + 0.5


def _log1p_small(e):
    return jnp.where(e < 1e-3, e * (1.0 - e * (0.5 - e * (1.0 / 3.0))), jnp.log(1.0 + e))


def _rowsum(x):
    return jnp.sum(x, axis=1, keepdims=True)


def _allsum(x):
    return jnp.sum(jnp.sum(x, axis=1, keepdims=True), axis=0, keepdims=True)


def matmul(a, b, mode, name, residual=None, out_dtype=f32, tm_t=1408, tn_t=1024, tk_t=1408, rider=None):
    if mode == "nn":
        (M, K), (K2, N) = a.shape, b.shape
    elif mode == "nt":
        (M, K), (N, K2) = a.shape, b.shape
    else:
        (K, M), (K2, N) = a.shape, b.shape
    assert K == K2, (a.shape, b.shape, mode)
    tm, tn, tk = _tile(M, tm_t), _tile(N, tn_t), _tile(K, tk_t)
    nk = K // tk
    if mode == "nn":
        a_spec = pl.BlockSpec((tm, tk), lambda i, j, k: (i, k))
        b_spec = pl.BlockSpec((tk, tn), lambda i, j, k: (k, j))
        dims = (((1,), (0,)), ((), ()))
    elif mode == "nt":
        a_spec = pl.BlockSpec((tm, tk), lambda i, j, k: (i, k))
        b_spec = pl.BlockSpec((tn, tk), lambda i, j, k: (j, k))
        dims = (((1,), (1,)), ((), ()))
    else:
        a_spec = pl.BlockSpec((tk, tm), lambda i, j, k: (k, i))
        b_spec = pl.BlockSpec((tk, tn), lambda i, j, k: (k, j))
        dims = (((0,), (0,)), ((), ()))
    o_spec = pl.BlockSpec((tm, tn), lambda i, j, k: (i, j))
    has_res = residual is not None
    grid = (M // tm, N // tn, nk)
    split, ride_first, ride_last = _ride(rider, 3 if has_res else 2, 1, grid)

    def body(*refs):
        ins_, (o_ref,), (rin, rout, rest) = split(refs)
        a_ref, b_ref = ins_[:2]
        r_ref = ins_[2] if has_res else None
        acc_ref, ride = rest[0], (rin, rout, rest[1:])
        ride_first(ride)
        k = pl.program_id(2)

        @pl.when(k == 0)
        def _():
            acc_ref[...] = jnp.zeros_like(acc_ref)

        acc_ref[...] += lax.dot_general(a_ref[...].astype(bf16), b_ref[...].astype(bf16), dims,
                                        preferred_element_type=f32)

        @pl.when(k == nk - 1)
        def _():
            r = acc_ref[...]
            if has_res:
                r = r + r_ref[...]
            o_ref[...] = r.astype(out_dtype)

        ride_last(ride)

    ins = [a, b] + ([residual] if has_res else [])
    specs = [a_spec, b_spec] + ([o_spec] if has_res else [])
    r_ins = rider.ins if rider else []
    r_outs = rider.out_shapes if rider else []
    res = pl.pallas_call(
        body, name=name, grid=grid, in_specs=specs + [_ANY] * len(r_ins), out_specs=[o_spec] + [_ANY] * len(r_outs),
        out_shape=[jax.ShapeDtypeStruct((M, N), out_dtype)] + list(r_outs),
        scratch_shapes=[pltpu.VMEM((tm, tn), f32)] + (rider.scratch if rider else []), compiler_params=_params(3),
    )(*ins, *r_ins)
    return (res[0], res[1:]) if rider else res[0]


def _row_tile(tp):
    return _tile(tp, 512)


def rms_fwd(h, gain, name):
    tp, d = h.shape
    rt = _row_tile(tp)

    def body(h_ref, g_ref, o_ref):
        x = h_ref[...]
        r = lax.rsqrt(jnp.mean(x * x, axis=-1, keepdims=True) + RMS_EPS)
        o_ref[...] = (x * r * g_ref[...]).astype(bf16)

    return pl.pallas_call(
        body, name=name, grid=(tp // rt,),
        in_specs=[pl.BlockSpec((rt, d), lambda i: (i, 0)), pl.BlockSpec((1, d), lambda i: (0, 0))],
        out_specs=pl.BlockSpec((rt, d), lambda i: (i, 0)),
        out_shape=jax.ShapeDtypeStruct((tp, d), bf16), compiler_params=_params(1),
    )(h, gain)


def rms_bwd(h, gain, dn, dres, name):
    tp, d = h.shape
    rt = _row_tile(tp)

    def body(h_ref, g_ref, dn_ref, dr_ref, dh_ref, dhb_ref, dg_ref):
        i = pl.program_id(0)
        x = h_ref[...]
        r = lax.rsqrt(jnp.mean(x * x, axis=-1, keepdims=True) + RMS_EPS)
        xh = x * r
        dn_ = dn_ref[...]
        dxh = dn_ * g_ref[...]
        dh = r * (dxh - xh * jnp.mean(dxh * xh, axis=-1, keepdims=True)) + dr_ref[...]
        dh_ref[...] = dh
        dhb_ref[...] = dh.astype(bf16)
        part = jnp.sum(dn_ * xh, axis=0, keepdims=True)

        @pl.when(i == 0)
        def _():
            dg_ref[...] = part

        @pl.when(i > 0)
        def _():
            dg_ref[...] += part

    row = pl.BlockSpec((rt, d), lambda i: (i, 0))
    vec = pl.BlockSpec((1, d), lambda i: (0, 0))
    return pl.pallas_call(
        body, name=name, grid=(tp // rt,), in_specs=[row, vec, row, row], out_specs=[row, row, vec],
        out_shape=[jax.ShapeDtypeStruct((tp, d), f32), jax.ShapeDtypeStruct((tp, d), bf16),
                   jax.ShapeDtypeStruct((1, d), f32)],
        compiler_params=_params(1),
    )(h, gain, dn, dres)


def loss_head(y, target):
    tp, d = y.shape
    lead = P0 + N_META
    rt = _row_tile(tp)
    ns = rt // lead
    assert lead == SB_BLOCK and rt % lead == 0 and target.shape == (tp - lead, d)
    last = target.shape[0] // lead - 1

    def body(*refs):
        y_ref, t_refs = refs[0], refs[1:1 + ns]
        dy_ref, dyb_ref, l_ref = refs[1 + ns:]
        i = pl.program_id(0)

        @pl.when(i == 0)
        def _():
            l_ref[...] = jnp.zeros_like(l_ref)

        part = jnp.zeros((1, 1), f32)
        for s in range(ns):
            rows = slice(s * lead, (s + 1) * lead)
            err = y_ref[rows, :] - t_refs[s][...]
            if s == 0:
                err = err * (i > 0).astype(f32)
            dy = err * (1.0 / d)
            dy_ref[rows, :] = dy
            dyb_ref[rows, :] = dy.astype(bf16)
            part = part + _allsum(err * err)
        l_ref[...] += jnp.broadcast_to(part * (0.5 / d), l_ref.shape)

    row = pl.BlockSpec((rt, d), lambda i: (i, 0))
    t_specs = [pl.BlockSpec((lead, d), lambda i, s=s: (jnp.clip(ns * i + s - 1, 0, last), 0)) for s in range(ns)]
    return pl.pallas_call(
        body, name="loss_head", grid=(tp // rt,), in_specs=[row] + t_specs,
        out_specs=[row, row, pl.BlockSpec((1, LANE), lambda i: (0, 0))],
        out_shape=[jax.ShapeDtypeStruct((tp, d), f32), jax.ShapeDtypeStruct((tp, d), bf16),
                   jax.ShapeDtypeStruct((1, LANE), f32)],
        compiler_params=_params(1),
    )(y, *([target] * ns))


def swiglu_fwd(u):
    tp = u.shape[0]
    rt, cb = _row_tile(tp), D_FF // 2
    nb = D_FF // cb

    def body(g_ref, u_ref, o_ref):
        g = g_ref[...]
        o_ref[...] = (g * _sigmoid(g) * u_ref[...]).astype(bf16)

    return pl.pallas_call(
        body, name="swiglu_fwd", grid=(tp // rt, nb),
        in_specs=[pl.BlockSpec((rt, cb), lambda i, j: (i, j)), pl.BlockSpec((rt, cb), lambda i, j: (i, j + nb))],
        out_specs=pl.BlockSpec((rt, cb), lambda i, j: (i, j)),
        out_shape=jax.ShapeDtypeStruct((tp, D_FF), bf16), compiler_params=_params(2),
    )(u, u)


def swiglu_bwd(u, dact):
    tp = u.shape[0]
    rt, cb = _row_tile(tp), D_FF // 2
    nb = D_FF // cb

    def body(g_ref, u_ref, da_ref, dg_ref, du_ref):
        g = g_ref[...]
        s = _sigmoid(g)
        da = da_ref[...]
        dg_ref[...] = (da * u_ref[...] * s * (1.0 + g * (1.0 - s))).astype(bf16)
        du_ref[...] = (da * g * s).astype(bf16)

    lo = pl.BlockSpec((rt, cb), lambda i, j: (i, j))
    hi = pl.BlockSpec((rt, cb), lambda i, j: (i, j + nb))
    dgate, dup = pl.pallas_call(
        body, name="swiglu_bwd", grid=(tp // rt, nb), in_specs=[lo, hi, lo], out_specs=[lo, lo],
        out_shape=[jax.ShapeDtypeStruct((tp, D_FF), bf16)] * 2, compiler_params=_params(2),
    )(u, u, dact)
    return dgate, dup


def merge_fwd(proj, ydn, ysb):
    tp = proj.shape[0]
    rt, d = _row_tile(tp), D_MODEL

    def body(gd_ref, gs_ref, yd_ref, ys_ref, o_ref):
        o_ref[...] = (_sigmoid(gd_ref[...]) * yd_ref[...] + _sigmoid(gs_ref[...]) * ys_ref[...]).astype(bf16)

    row = pl.BlockSpec((rt, d), lambda i: (i, 0))
    return pl.pallas_call(
        body, name="merge_fwd", grid=(tp // rt,),
        in_specs=[pl.BlockSpec((rt, d), lambda i: (i, C_GDN)), pl.BlockSpec((rt, d), lambda i: (i, C_GSB)), row, row],
        out_specs=row, out_shape=jax.ShapeDtypeStruct((tp, d), bf16), compiler_params=_params(1),
    )(proj, proj, ydn, ysb)


def merge_bwd(proj, ydn, ysb, dm):
    tp = proj.shape[0]
    rt, d = _row_tile(tp), D_MODEL

    def body(gd_ref, gs_ref, yd_ref, ys_ref, dm_ref, dyd_ref, dys_ref, dg_ref):
        dm_ = dm_ref[...]
        sd = _sigmoid(gd_ref[...])
        ss = _sigmoid(gs_ref[...])
        dyd_ref[...] = (dm_ * sd).astype(bf16)
        dys_ref[...] = (dm_ * ss).astype(bf16)
        dg_ref[:, :d] = (dm_ * yd_ref[...] * sd * (1.0 - sd)).astype(bf16)
        dg_ref[:, d:] = (dm_ * ys_ref[...] * ss * (1.0 - ss)).astype(bf16)

    row = pl.BlockSpec((rt, d), lambda i: (i, 0))
    return pl.pallas_call(
        body, name="merge_bwd", grid=(tp // rt,),
        in_specs=[pl.BlockSpec((rt, d), lambda i: (i, C_GDN)), pl.BlockSpec((rt, d), lambda i: (i, C_GSB)), row, row, row],
        out_specs=[row, row, pl.BlockSpec((rt, 2 * d), lambda i: (i, 0))],
        out_shape=[jax.ShapeDtypeStruct((tp, d), bf16)] * 2 + [jax.ShapeDtypeStruct((tp, 2 * d), bf16)],
        compiler_params=_params(1),
    )(proj, proj, ydn, ysb, dm)


def dn_out_fwd(o, proj, gain):
    tp = o.shape[0]
    rt, cb, wide = _row_tile(tp), DN_DV, 1024
    zb = C_DZ * 1024 // wide

    def body(o_ref, z_ref, g_ref, y_ref):
        for s in range(wide // cb):
            sl = slice(s * cb, (s + 1) * cb)
            x = o_ref[:, sl]
            r = lax.rsqrt(jnp.mean(x * x, axis=-1, keepdims=True) + RMS_EPS)
            z = z_ref[:, sl]
            y_ref[:, sl] = (x * r * g_ref[...] * (z * _sigmoid(z))).astype(bf16)

    blk = pl.BlockSpec((rt, wide), lambda i, j: (i, j))
    return pl.pallas_call(
        body, name="dn_out_fwd", grid=(tp // rt, DN_V // wide),
        in_specs=[blk, pl.BlockSpec((rt, wide), lambda i, j: (i, j + zb)), pl.BlockSpec((1, cb), lambda i, j: (0, 0))],
        out_specs=blk, out_shape=jax.ShapeDtypeStruct((tp, DN_V), bf16), compiler_params=_params(2),
    )(o, proj, gain)


def dn_out_bwd(o, proj, gain, dy):
    tp = o.shape[0]
    rt, cb, wide = _row_tile(tp), DN_DV, 1024
    zb = C_DZ * 1024 // wide

    def body(o_ref, z_ref, g_ref, dy_ref, do_ref, dz_ref, dg_ref):
        i, j = pl.program_id(0), pl.program_id(1)
        g = g_ref[...]
        part = jnp.zeros((1, cb), f32)
        for hh in range(wide // cb):
            sl = slice(hh * cb, (hh + 1) * cb)
            x = o_ref[:, sl]
            r = lax.rsqrt(jnp.mean(x * x, axis=-1, keepdims=True) + RMS_EPS)
            xh = x * r
            z = z_ref[:, sl]
            s = _sigmoid(z)
            dy_ = dy_ref[:, sl]
            drn = dy_ * (z * s)
            dz_ref[:, sl] = (dy_ * xh * g * s * (1.0 + z * (1.0 - s))).astype(bf16)
            dxh = drn * g
            do_ref[:, sl] = r * (dxh - xh * jnp.mean(dxh * xh, axis=-1, keepdims=True))
            part = part + jnp.sum(drn * xh, axis=0, keepdims=True)
        first = jnp.logical_and(i == 0, j == 0)

        @pl.when(first)
        def _():
            dg_ref[...] = part

        @pl.when(jnp.logical_not(first))
        def _():
            dg_ref[...] += part

    blk = pl.BlockSpec((rt, wide), lambda i, j: (i, j))
    vec = pl.BlockSpec((1, cb), lambda i, j: (0, 0))
    return pl.pallas_call(
        body, name="dn_out_bwd", grid=(tp // rt, DN_V // wide),
        in_specs=[blk, pl.BlockSpec((rt, wide), lambda i, j: (i, j + zb)), vec, blk],
        out_specs=[blk, blk, vec],
        out_shape=[jax.ShapeDtypeStruct((tp, DN_V), f32), jax.ShapeDtypeStruct((tp, DN_V), bf16),
                   jax.ShapeDtypeStruct((1, cb), f32)],
        compiler_params=_params(2),
    )(o, proj, gain, dy)


def sb_prep_fwd(proj, gq, gk):
    tp = proj.shape[0]
    rt, cb = _row_tile(tp), SB_DH

    def body(q_ref, k_ref, v_ref, gq_ref, gk_ref, qo_ref, ko_ref, vo_ref):
        for x_ref, g_ref, o_ref in ((q_ref, gq_ref, qo_ref), (k_ref, gk_ref, ko_ref)):
            for h in range(HEADS):
                sl = slice(h * cb, (h + 1) * cb)
                x = x_ref[:, sl]
                r = lax.rsqrt(jnp.mean(x * x, axis=-1, keepdims=True) + RMS_EPS)
                o_ref[:, sl] = (x * r * g_ref[...]).astype(bf16)
        vo_ref[...] = v_ref[...].astype(bf16)

    blk = pl.BlockSpec((rt, SB_W), lambda i: (i, 0))
    vec = pl.BlockSpec((1, cb), lambda i: (0, 0))
    return pl.pallas_call(
        body, name="sb_prep_fwd", grid=(tp // rt,),
        in_specs=[pl.BlockSpec((rt, SB_W), lambda i: (i, C_SQ)), pl.BlockSpec((rt, SB_W), lambda i: (i, C_SK)),
                  pl.BlockSpec((rt, SB_W), lambda i: (i, C_SV)), vec, vec],
        out_specs=[blk] * 3, out_shape=[jax.ShapeDtypeStruct((tp, SB_W), bf16)] * 3, compiler_params=_params(1),
    )(proj, proj, proj, gq, gk)


def sb_prep_bwd(proj, gq, gk, dqs, dks, dvs):
    tp = proj.shape[0]
    rt, cb = _row_tile(tp), SB_DH

    def body(q_ref, k_ref, gq_ref, gk_ref, dq_ref, dk_ref, dv_ref, do_ref, dgq_ref, dgk_ref):
        first = pl.program_id(0) == 0
        do_ref[:, 2 * SB_W:] = dv_ref[...].astype(bf16)
        for x_ref, g_ref, dn_ref, at, dg_ref, mul in ((q_ref, gq_ref, dq_ref, 0, dgq_ref, None),
                                                      (k_ref, gk_ref, dk_ref, SB_W, dgk_ref, SB_DH ** -0.5)):
            part = jnp.zeros((1, cb), f32)
            for h in range(HEADS):
                sl = slice(h * cb, (h + 1) * cb)
                x = x_ref[:, sl]
                r = lax.rsqrt(jnp.mean(x * x, axis=-1, keepdims=True) + RMS_EPS)
                xh = x * r
                dn_ = dn_ref[:, sl] if mul is None else dn_ref[:, sl] * mul
                dxh = dn_ * g_ref[...]
                do_ref[:, at + h * cb:at + (h + 1) * cb] = (
                    r * (dxh - xh * jnp.mean(dxh * xh, axis=-1, keepdims=True))).astype(bf16)
                part = part + jnp.sum(dn_ * xh, axis=0, keepdims=True)

            @pl.when(first)
            def _(dg_ref=dg_ref, part=part):
                dg_ref[...] = part

            @pl.when(jnp.logical_not(first))
            def _(dg_ref=dg_ref, part=part):
                dg_ref[...] += part

    blk = pl.BlockSpec((rt, SB_W), lambda i: (i, 0))
    vec = pl.BlockSpec((1, cb), lambda i: (0, 0))
    return pl.pallas_call(
        body, name="sb_prep_bwd", grid=(tp // rt,),
        in_specs=[pl.BlockSpec((rt, SB_W), lambda i: (i, C_SQ)), pl.BlockSpec((rt, SB_W), lambda i: (i, C_SK)),
                  vec, vec, blk, blk, blk],
        out_specs=[pl.BlockSpec((rt, 3 * SB_W), lambda i: (i, 0)), vec, vec],
        out_shape=[jax.ShapeDtypeStruct((tp, 3 * SB_W), bf16)] + [jax.ShapeDtypeStruct((1, cb), f32)] * 2,
        compiler_params=_params(1),
    )(proj, proj, gq, gk, dqs, dks, dvs)


def _conv_taps(ext, rt):
    taps = []
    for k in range(DN_CONV):
        s = DN_CONV - 1 - k
        taps.append((pltpu.roll(ext, s, axis=0) if s else ext)[SUB:SUB + rt])
    return taps


def _conv_act(taps, w, l2):
    y = taps[0] * w[0:1]
    for k in range(1, DN_CONV):
        y = y + taps[k] * w[k:k + 1]
    s = _sigmoid(y)
    a = y * s
    if l2:
        n = lax.rsqrt(jnp.sum(a * a, axis=-1, keepdims=True) + L2_EPS)
        return y, s, a, n
    return y, s, a, None


def conv_fwd(proj, w8, col_blk, ncb, l2, name):
    tp = proj.shape[0]
    rt = _row_tile(tp)
    hb = rt // SUB
    cw = CONV_W
    cb0 = col_blk * LANE // cw

    def body(x_ref, h_ref, w_ref, o_ref):
        i = pl.program_id(1)
        first = (i > 0).astype(f32)
        for s in range(cw // LANE):
            sl = slice(s * LANE, (s + 1) * LANE)
            ext = jnp.concatenate([h_ref[:, sl] * first, x_ref[:, sl]], axis=0)
            _, _, a, n = _conv_act(_conv_taps(ext, rt), w_ref[:, sl], l2)
            o_ref[:, sl] = a * n if l2 else a

    return pl.pallas_call(
        body, name=name, grid=(ncb * LANE // cw, tp // rt),
        in_specs=[pl.BlockSpec((rt, cw), lambda j, i: (i, j + cb0)),
                  pl.BlockSpec((SUB, cw), lambda j, i: (jnp.maximum(i * hb - 1, 0), j + cb0)),
                  pl.BlockSpec((SUB, cw), lambda j, i: (0, j))],
        out_specs=pl.BlockSpec((rt, cw), lambda j, i: (i, j)),
        out_shape=jax.ShapeDtypeStruct((tp, ncb * LANE), f32), compiler_params=_params(2),
    )(proj, proj, w8)


def conv_bwd_act(proj, w8, dout, col_blk, ncb, l2, name):
    tp = proj.shape[0]
    rt = _row_tile(tp)
    hb = rt // SUB
    cw = CONV_W
    cb0 = col_blk * LANE // cw

    def body(x_ref, h_ref, w_ref, d_ref, dy_ref, dw_ref):
        i = pl.program_id(1)
        first = (i > 0).astype(f32)
        rows = lax.broadcasted_iota(jnp.int32, (SUB, LANE), 0)
        for s in range(cw // LANE):
            sl = slice(s * LANE, (s + 1) * LANE)
            ext = jnp.concatenate([h_ref[:, sl] * first, x_ref[:, sl]], axis=0)
            taps = _conv_taps(ext, rt)
            y, sg, a, n = _conv_act(taps, w_ref[:, sl], l2)
            da = d_ref[:, sl]
            if l2:
                out = a * n
                da = n * (da - out * jnp.sum(da * out, axis=-1, keepdims=True))
            dy = da * sg * (1.0 + y * (1.0 - sg))
            dy_ref[:, sl] = dy
            part = jnp.zeros((SUB, LANE), f32)
            for k in range(DN_CONV):
                part = part + jnp.where(rows == k, jnp.sum(taps[k] * dy, axis=0, keepdims=True), 0.0)

            @pl.when(i == 0)
            def _(sl=sl, part=part):
                dw_ref[:, sl] = part

            @pl.when(i > 0)
            def _(sl=sl, part=part):
                dw_ref[:, sl] += part

    return pl.pallas_call(
        body, name=name, grid=(ncb * LANE // cw, tp // rt),
        in_specs=[pl.BlockSpec((rt, cw), lambda j, i: (i, j + cb0)),
                  pl.BlockSpec((SUB, cw), lambda j, i: (jnp.maximum(i * hb - 1, 0), j + cb0)),
                  pl.BlockSpec((SUB, cw), lambda j, i: (0, j)),
                  pl.BlockSpec((rt, cw), lambda j, i: (i, j))],
        out_specs=[pl.BlockSpec((rt, cw), lambda j, i: (i, j)), pl.BlockSpec((SUB, cw), lambda j, i: (0, j))],
        out_shape=[jax.ShapeDtypeStruct((tp, ncb * LANE), f32), jax.ShapeDtypeStruct((SUB, ncb * LANE), f32)],
        compiler_params=_params(2),
    )(proj, proj, w8, dout)


def conv_bwd_in(dy, w8, name):
    tp, cols = dy.shape
    rt = _row_tile(tp)
    hb = rt // SUB
    nr = tp // rt
    last8 = tp // SUB - 1
    cw = CONV_W

    def body(d_ref, h_ref, w_ref, o_ref):
        i = pl.program_id(1)
        last = (i < nr - 1).astype(f32)
        for c0 in range(cw // LANE):
            sl = slice(c0 * LANE, (c0 + 1) * LANE)
            ext = jnp.concatenate([d_ref[:, sl], h_ref[:, sl] * last], axis=0)
            w = w_ref[:, sl]
            acc = None
            for k in range(DN_CONV):
                s = DN_CONV - 1 - k
                sh = (pltpu.roll(ext, rt + SUB - s, axis=0) if s else ext)[0:rt]
                term = sh * w[k:k + 1]
                acc = term if acc is None else acc + term
            o_ref[:, sl] = acc.astype(bf16)

    return pl.pallas_call(
        body, name=name, grid=(cols // cw, nr),
        in_specs=[pl.BlockSpec((rt, cw), lambda j, i: (i, j)),
                  pl.BlockSpec((SUB, cw), lambda j, i: (jnp.minimum((i + 1) * hb, last8), j)),
                  pl.BlockSpec((SUB, cw), lambda j, i: (0, j))],
        out_specs=pl.BlockSpec((rt, cw), lambda j, i: (i, j)),
        out_shape=jax.ShapeDtypeStruct((tp, cols), bf16), compiler_params=_params(2),
    )(dy, dy, w8)


def _ab_common(p, al, dtb, r0):
    rows = r0 + lax.broadcasted_iota(jnp.int32, p.shape, 0)
    mask = (rows >= P0).astype(f32)
    xx = p + dtb
    sp = jnp.maximum(xx, 0.0) + _log1p_small(jnp.exp(-jnp.abs(xx)))
    ea = jnp.exp(al)
    g = -ea * sp * mask
    beta = _sigmoid(p) * mask
    return g, beta, _sigmoid(xx), ea, mask


def _chunk_tri(rt, later):
    r = lax.broadcasted_iota(jnp.int32, (rt, rt), 0)
    c = lax.broadcasted_iota(jnp.int32, (rt, rt), 1)
    shift = CHUNK.bit_length() - 1
    same = jnp.right_shift(r, shift) == jnp.right_shift(c, shift)
    return jnp.logical_and(same, c >= r if later else c <= r).astype(f32)


def ab_fwd(pab, al, dtb):
    tp = pab.shape[0]
    rt = _row_tile(tp)
    assert rt % CHUNK == 0

    def body(p_ref, al_ref, dt_ref, g_ref, b_ref):
        i = pl.program_id(0)
        g, beta, _, _, _ = _ab_common(p_ref[...], al_ref[...], dt_ref[...], i * rt)
        gam = _hdot(_chunk_tri(rt, False), g)
        for h in range(HEADS):
            g_ref[h] = jnp.broadcast_to(gam[:, h:h + 1], (rt, LANE))
            b_ref[h] = jnp.broadcast_to(beta[:, HEADS + h:HEADS + h + 1], (rt, LANE))

    vec = pl.BlockSpec((1, LANE), lambda i: (0, 0))
    out = pl.BlockSpec((HEADS, rt, LANE), lambda i: (0, i, 0))
    return pl.pallas_call(
        body, name="ab_fwd", grid=(tp // rt,), in_specs=[pl.BlockSpec((rt, LANE), lambda i: (i, 0)), vec, vec],
        out_specs=[out, out], out_shape=[jax.ShapeDtypeStruct((HEADS, tp, LANE), f32)] * 2, compiler_params=_params(1),
    )(pab, al, dtb)


def ab_bwd(pab, al, dtb, dg, db):
    tp = pab.shape[0]
    rt = _row_tile(tp)

    def body(p_ref, al_ref, dt_ref, dg_ref, db_ref, dp_ref, dal_ref, ddt_ref):
        i = pl.program_id(0)
        g, beta, sx, ea, mask = _ab_common(p_ref[...], al_ref[...], dt_ref[...], i * rt)
        lanes = lax.broadcasted_iota(jnp.int32, (rt, LANE), 1)
        dgl = jnp.zeros((rt, LANE), f32)
        dbl = jnp.zeros((rt, LANE), f32)
        for h in range(HEADS):
            dgl = dgl + jnp.where(lanes == h, dg_ref[h], 0.0)
            dbl = dbl + jnp.where(lanes == HEADS + h, db_ref[h], 0.0)
        dgl = _hdot(_chunk_tri(rt, True), dgl)
        dxx = dgl * (-ea) * sx * mask
        dp_ref[...] = (dxx + dbl * beta * (1.0 - beta)).astype(bf16)
        pal = jnp.sum(dgl * g, axis=0, keepdims=True)
        pdt = jnp.sum(dxx, axis=0, keepdims=True)

        @pl.when(i == 0)
        def _():
            dal_ref[...] = pal
            ddt_ref[...] = pdt

        @pl.when(i > 0)
        def _():
            dal_ref[...] += pal
            ddt_ref[...] += pdt

    vec = pl.BlockSpec((1, LANE), lambda i: (0, 0))
    row = pl.BlockSpec((rt, LANE), lambda i: (i, 0))
    big = pl.BlockSpec((HEADS, rt, LANE), lambda i: (0, i, 0))
    return pl.pallas_call(
        body, name="ab_bwd", grid=(tp // rt,), in_specs=[row, vec, vec, big, big], out_specs=[row, vec, vec],
        out_shape=[jax.ShapeDtypeStruct((tp, LANE), bf16), jax.ShapeDtypeStruct((1, LANE), f32),
                   jax.ShapeDtypeStruct((1, LANE), f32)],
        compiler_params=_params(1),
    )(pab, al, dtb, dg, db)


class _Chunk:
    pass


def _gdn_chunk(q, k, v, gcol, bcol, grow8):
    C = CHUNK
    R = range(len(q))
    X = _Chunk()
    ri = lax.broadcasted_iota(jnp.int32, (C, C), 0)
    ci = lax.broadcasted_iota(jnp.int32, (C, C), 1)
    eye = (ri == ci).astype(f32)
    gam = list(gcol)
    gam_row = [grow8[h][0:1, 0:C] for h in R]
    X.ri, X.ci = ri, ci
    X.Dm = [jnp.where(ri >= ci, jnp.exp(jnp.minimum(gam[h][:, 0:C] - gam_row[h], 0.0)), 0.0) for h in R]
    X.eg = [jnp.exp(gam[h]) for h in R]
    gl = [gam[h][C - 1:C, :] for h in R]
    X.egl = [jnp.exp(gl[h]) for h in R]
    X.kdec = [jnp.exp(gl[h] - gam[h]) for h in R]
    X.qs = [q[h] * (DN_DK ** -0.5) for h in R]
    X.kb = [k[h] * bcol[h] for h in R]
    kk = [_dot_nt(X.kb[h], k[h]) for h in R]
    qk = [_dot_nt(X.qs[h], k[h]) for h in R]
    X.A = [jnp.where(ri > ci, kk[h] * X.Dm[h], 0.0) for h in R]
    T = [eye - X.A[h] for h in R]
    P = list(X.A)
    for _ in range(5):
        P = [_hdot(P[h], P[h]) for h in R]
        T = [T[h] + _hdot(T[h], P[h]) for h in R]
    X.T = T
    X.b2 = [jnp.concatenate([bcol[h], bcol[h]], axis=-1) for h in R]
    X.u = [_hdot(T[h], v[h] * X.b2[h]) for h in R]
    X.w = [_hdot(T[h], X.kb[h] * X.eg[h]) for h in R]
    X.attn = [qk[h] * X.Dm[h] for h in R]
    X.qg = [X.qs[h] * X.eg[h] for h in R]
    X.kg = [k[h] * X.kdec[h] for h in R]
    return X


def gdn_fwd(q, k, v, gc, bc, grow):
    tp = q.shape[0]
    nc = tp // CHUNK
    hb = GDN_HEADS_PER_STEP

    def body(q_ref, k_ref, v_ref, gc_ref, bc_ref, gr_ref, o_ref, ss_ref, S_ref):
        c = pl.program_id(1)

        @pl.when(c == 0)
        def _():
            S_ref[...] = jnp.zeros_like(S_ref)

        R = range(hb)
        qc = [slice(h * DN_DK, (h + 1) * DN_DK) for h in R]
        vc = [slice(h * DN_DV, (h + 1) * DN_DV) for h in R]
        X = _gdn_chunk([q_ref[:, qc[h]] for h in R], [k_ref[:, qc[h]] for h in R], [v_ref[:, vc[h]] for h in R],
                       [gc_ref[h] for h in R], [bc_ref[h] for h in R], [gr_ref[h] for h in R])
        S = [S_ref[h] for h in R]
        for h in R:
            ss_ref[h, 0] = S[h]
        wS = [_dot(X.w[h], S[h]) for h in R]
        qS = [_dot(X.qg[h], S[h]) for h in R]
        vn = [X.u[h] - wS[h] for h in R]
        av = [_dot(X.attn[h], vn[h]) for h in R]
        kv = [_dot_tn(X.kg[h], vn[h]) for h in R]
        for h in R:
            o_ref[:, vc[h]] = qS[h] + av[h]
            S_ref[h] = S[h] * X.egl[h][:, 0:1] + kv[h]

    qk = pl.BlockSpec((CHUNK, hb * DN_DK), lambda g, c: (c, g))
    vv = pl.BlockSpec((CHUNK, hb * DN_DV), lambda g, c: (c, g))
    col = pl.BlockSpec((hb, CHUNK, LANE), lambda g, c: (g, c, 0))
    row = pl.BlockSpec((hb, SUB, LANE), lambda g, c: (g, c, 0))
    return pl.pallas_call(
        body, name="gdn_fwd", grid=(HEADS // hb, nc), in_specs=[qk, qk, vv, col, col, row],
        out_specs=[vv, pl.BlockSpec((hb, 1, DN_DK, DN_DV), lambda g, c: (g, c, 0, 0))],
        out_shape=[jax.ShapeDtypeStruct((tp, DN_V), f32), jax.ShapeDtypeStruct((HEADS, nc, DN_DK, DN_DV), f32)],
        scratch_shapes=[pltpu.VMEM((hb, DN_DK, DN_DV), f32)], compiler_params=_params(2),
    )(q, k, v, gc, bc, grow)


def gdn_bwd(q, k, v, gc, bc, grow, states, do, rider=None):
    tp = q.shape[0]
    nc = tp // CHUNK
    C = CHUNK
    hb = GDN_HEADS_PER_STEP
    grid = (HEADS // hb, nc)
    split, ride_first, ride_last = _ride(rider, 8, 5, grid)

    def body(*refs):
        ((q_ref, k_ref, v_ref, gc_ref, bc_ref, gr_ref, ss_ref, do_ref), (dq_ref, dk_ref, dv_ref, dg_ref, db_ref),
         (rin, rout, rest)) = split(refs)
        dS_ref, ride = rest[0], (rin, rout, rest[1:])
        ride_first(ride)
        c = pl.program_id(1)

        @pl.when(c == 0)
        def _():
            dS_ref[...] = jnp.zeros_like(dS_ref)

        R = range(hb)
        qc = [slice(h * DN_DK, (h + 1) * DN_DK) for h in R]
        vc = [slice(h * DN_DV, (h + 1) * DN_DV) for h in R]
        k_ = [k_ref[:, qc[h]] for h in R]
        v_ = [v_ref[:, vc[h]] for h in R]
        bcol = [bc_ref[h] for h in R]
        X = _gdn_chunk([q_ref[:, qc[h]] for h in R], k_, v_, [gc_ref[h] for h in R], bcol, [gr_ref[h] for h in R])
        ri, ci = X.ri, X.ci
        S = [ss_ref[h, 0] for h in R]
        do_ = [do_ref[:, vc[h]] for h in R]
        dSn = [dS_ref[h] for h in R]
        wS = [_dot(X.w[h], S[h]) for h in R]
        ado = [_dot_tn(X.attn[h], do_[h]) for h in R]
        kdS = [_dot(X.kg[h], dSn[h]) for h in R]
        d_qg = [_dot_nt(do_[h], S[h]) for h in R]
        qdo = [_dot_tn(X.qg[h], do_[h]) for h in R]
        vn = [X.u[h] - wS[h] for h in R]
        d_vn = [ado[h] + kdS[h] for h in R]
        dovn = [_dot_nt(do_[h], vn[h]) for h in R]
        d_kg = [_dot_nt(vn[h], dSn[h]) for h in R]
        wdv = [_dot_tn(X.w[h], d_vn[h]) for h in R]
        dw = [-_dot_nt(d_vn[h], S[h]) for h in R]
        for h in R:
            dS_ref[h] = qdo[h] + X.egl[h][:, 0:1] * dSn[h] - wdv[h]
        dattn = [jnp.where(ri >= ci, dovn[h], 0.0) for h in R]
        dRu = [_hdot_tn(X.T[h], d_vn[h]) for h in R]
        dRw = [_hdot_tn(X.T[h], dw[h]) for h in R]
        dAu = [_hdot_nt(dRu[h], X.u[h]) for h in R]
        dAw = [_hdot_nt(dRw[h], X.w[h]) for h in R]
        dA = [jnp.where(ri > ci, -(dAu[h] + dAw[h]), 0.0) for h in R]
        dKK = [dA[h] * X.Dm[h] for h in R]
        dQK = [dattn[h] * X.Dm[h] for h in R]
        E = [dA[h] * X.A[h] + dattn[h] * X.attn[h] for h in R]
        dkb = [_dot(dKK[h], k_[h]) + dRw[h] * X.eg[h] for h in R]
        dk1 = [_dot_tn(dKK[h], X.kb[h]) for h in R]
        dqs = [_dot(dQK[h], k_[h]) + d_qg[h] * X.eg[h] for h in R]
        dk2 = [_dot_tn(dQK[h], X.qs[h]) for h in R]
        ones = jnp.ones((C, LANE), f32)
        colE = [_hdot_tn(E[h], ones) for h in R]
        rows = lax.broadcasted_iota(jnp.int32, (C, LANE), 0)
        dgam = []
        for h in R:
            t = d_kg[h] * X.kg[h]
            dgl = _allsum(t) + X.egl[h][:, 0:1] * _allsum(S[h] * dSn[h])
            g = (_rowsum(E[h]) - colE[h] + _rowsum(dRw[h] * (X.kb[h] * X.eg[h])) + _rowsum(d_qg[h] * X.qg[h])
                 - _rowsum(t))
            dgam.append(g + jnp.where(rows == C - 1, dgl, 0.0))
        for h in R:
            dv_ref[:, vc[h]] = dRu[h] * X.b2[h]
            dbeta = _rowsum(dRu[h] * v_[h]) + _rowsum(dkb[h] * k_[h])
            dq_ref[:, qc[h]] = dqs[h] * (DN_DK ** -0.5)
            dk_ref[:, qc[h]] = dk1[h] + dk2[h] + dkb[h] * bcol[h] + d_kg[h] * X.kdec[h]
            dg_ref[h] = dgam[h]
            db_ref[h] = jnp.broadcast_to(dbeta, (C, LANE))
        ride_last(ride)

    rc = lambda c: nc - 1 - c
    qk = pl.BlockSpec((CHUNK, hb * DN_DK), lambda g, c: (rc(c), g))
    vv = pl.BlockSpec((CHUNK, hb * DN_DV), lambda g, c: (rc(c), g))
    col = pl.BlockSpec((hb, CHUNK, LANE), lambda g, c: (g, rc(c), 0))
    row = pl.BlockSpec((hb, SUB, LANE), lambda g, c: (g, rc(c), 0))
    st = pl.BlockSpec((hb, 1, DN_DK, DN_DV), lambda g, c: (g, rc(c), 0, 0))
    r_ins = rider.ins if rider else []
    r_outs = rider.out_shapes if rider else []
    res = pl.pallas_call(
        body, name="gdn_bwd", grid=grid, in_specs=[qk, qk, vv, col, col, row, st, vv] + [_ANY] * len(r_ins),
        out_specs=[qk, qk, vv, col, col] + [_ANY] * len(r_outs),
        out_shape=[jax.ShapeDtypeStruct((tp, DN_QK), f32), jax.ShapeDtypeStruct((tp, DN_QK), f32),
                   jax.ShapeDtypeStruct((tp, DN_V), f32), jax.ShapeDtypeStruct((HEADS, tp, LANE), f32),
                   jax.ShapeDtypeStruct((HEADS, tp, LANE), f32)] + list(r_outs),
        scratch_shapes=[pltpu.VMEM((hb, DN_DK, DN_DV), f32)] + (rider.scratch if rider else []),
        compiler_params=_params(2),
    )(q, k, v, gc, bc, grow, states, do, *r_ins)
    return res[:5], res[5:]


def _cumsum_after(x, nb, us, pieces=2):
    B, n = SB_BLOCK, x.shape[0]
    hi = x.astype(bf16)
    parts = (hi, (x - hi.astype(f32)).astype(bf16)) if pieces == 2 else (hi,)
    rows = [p[:, b * B:(b + 1) * B] for p in parts for b in range(nb)]
    r = jnp.dot(jnp.concatenate(rows, axis=0), us, preferred_element_type=f32)
    out = [r[b * n:(b + 1) * n] for b in range(nb)]
    if pieces == 2:
        out = [out[b] + r[(nb + b) * n:(nb + b + 1) * n] for b in range(nb)]
    return out[0] if nb == 1 else jnp.concatenate(out, axis=1)


def _later_blocks(x, nb, carry):
    B = SB_BLOCK
    tot = [_rowsum(x[:, b * B:(b + 1) * B]) for b in range(nb)]
    offs = [None] * nb
    run = carry
    for b in range(nb - 1, -1, -1):
        offs[b] = jnp.broadcast_to(run, (x.shape[0], B))
        run = run + tot[b]
    return (offs[0] if nb == 1 else jnp.concatenate(offs, axis=1)), run


def _sb_group(i, t):
    top = (i + 1) * (SB_QB // SB_BLOCK) - 1 - SB_GROUP * t
    jlo = jnp.maximum(top - SB_GROUP + 1, 0)
    rows = pl.ds(pl.multiple_of(jlo * SB_BLOCK, SB_BLOCK), SB_GROUP * SB_BLOCK)
    return jlo, rows, (top + 1) * SB_BLOCK


def _sb_weights(q, kcat, i, jlo, kend, cs, us, masked):
    B, nb = SB_BLOCK, SB_GROUP
    R = range(len(q))
    z = [_dot_nt(q[h], kcat[h]) * (SB_DH ** -0.5) for h in R]
    e = [jnp.exp(-jnp.abs(z[h])) for h in R]
    l1p = [jnp.log(1.0 + e[h]) for h in R]
    lsp = [jnp.minimum(z[h], 0.0) - l1p[h] for h in R]
    lk = [lsp[h] - z[h] for h in R]
    vis = None
    if masked:
        qpos = i * SB_QB + lax.broadcasted_iota(jnp.int32, (SB_QB, nb * B), 0)
        kpos = jlo * B + lax.broadcasted_iota(jnp.int32, (SB_QB, nb * B), 1)
        vis = jnp.logical_and(kpos < jnp.minimum(qpos, kend), kpos >= P0)
        lk = [jnp.where(vis, lk[h], 0.0) for h in R]
    later = [_later_blocks(lk[h], nb, cs[h]) for h in R]
    cum = [_cumsum_after(lk[h], nb, us) for h in R]
    w = [jnp.exp(lsp[h] + cum[h] + later[h][0]) for h in R]
    if masked:
        w = [jnp.where(vis, w[h], 0.0) for h in R]
    return lsp, vis, w, [later[h][1] for h in R]


def _sb_loop(i, step, carry):
    trips = ((i + 1) * (SB_QB // SB_BLOCK) - 1 + SB_GROUP) // SB_GROUP
    carry = step(True)(0, carry)
    carry = lax.fori_loop(1, trips - 1, step(False), carry)
    return lax.fori_loop(jnp.maximum(trips - 1, 1), trips, step(True), carry)


def _ride(rider, n_in, n_out, grid):
    n_rin = len(rider.ins) if rider else 0
    n_rout = len(rider.out_shapes) if rider else 0

    def split(refs):
        ins, rin = refs[:n_in], refs[n_in:n_in + n_rin]
        outs = refs[n_in + n_rin:n_in + n_rin + n_out]
        rout = refs[n_in + n_rin + n_out:n_in + n_rin + n_out + n_rout]
        return ins, outs, (rin, rout, refs[n_in + n_rin + n_out + n_rout:])

    def at(step, fn, r):
        if rider is None:
            return
        cond = None
        for a, g in enumerate(grid):
            c = pl.program_id(a) == (g - 1 if step == "last" else 0)
            cond = c if cond is None else jnp.logical_and(cond, c)

        @pl.when(cond)
        def _():
            fn(*r)

    first = lambda r: at("first", rider.start if rider else None, r)
    last = lambda r: at("last", rider.finish if rider else None, r)
    return split, first, last


def sb_fwd(qs, ks, vs, rider=None):
    tp = qs.shape[0]
    nq = tp // SB_QB
    B, G, hb, QB = SB_BLOCK, SB_GROUP, SB_FWD_HEADS_PER_STEP, SB_QB
    assert tp >= G * B and tp % QB == 0 and QB % B == 0 and G * B >= QB
    grid = (HEADS // hb, nq)
    split, ride_first, ride_last = _ride(rider, 3, 2, grid)

    def body(*refs):
        (q_ref, k_ref, v_ref), (o_ref, ob_ref), ride = split(refs)
        ride_first(ride)
        i = pl.program_id(1)
        R = range(hb)
        hs = [slice(h * SB_DH, (h + 1) * SB_DH) for h in R]
        q = [q_ref[:, hs[h]] for h in R]
        us = (lax.broadcasted_iota(jnp.int32, (B, B), 0) > lax.broadcasted_iota(jnp.int32, (B, B), 1)).astype(bf16)

        def make_step(masked):
            def step(t, carry):
                acc, cs = carry
                jlo, rows, kend = _sb_group(i, t)
                _, _, w, cs = _sb_weights(q, [k_ref[rows, hs[h]] for h in R], i, jlo, kend, cs, us, masked)
                pv = [_dot(w[h], v_ref[rows, hs[h]]) for h in R]
                return tuple(acc[h] + pv[h] for h in R), tuple(cs)
            return step

        carry = (tuple(jnp.zeros((QB, SB_DH), f32) for _ in R), tuple(jnp.zeros((QB, 1), f32) for _ in R))
        acc, _ = _sb_loop(i, make_step, carry)
        for h in R:
            o_ref[:, hs[h]] = acc[h]
            ob_ref[:, hs[h]] = acc[h].astype(bf16)
        ride_last(ride)

    blk = pl.BlockSpec((QB, hb * SB_DH), lambda g, i: (i, g))
    full = pl.BlockSpec((tp, hb * SB_DH), lambda g, i: (0, g))
    r_ins = rider.ins if rider else []
    r_outs = rider.out_shapes if rider else []
    res = pl.pallas_call(
        body, name="sb_fwd", grid=grid, in_specs=[blk, full, full] + [_ANY] * len(r_ins),
        out_specs=[blk, blk] + [_ANY] * len(r_outs),
        out_shape=[jax.ShapeDtypeStruct((tp, SB_W), f32), jax.ShapeDtypeStruct((tp, SB_W), bf16)] + list(r_outs),
        scratch_shapes=rider.scratch if rider else [], compiler_params=_params(2),
    )(qs, ks, vs, *r_ins)
    return res[0], res[1], res[2:]


def sb_bwd(qs, ks, vs, o, do, rider=None):
    tp = qs.shape[0]
    nq = tp // SB_QB
    B, G, hb, QB = SB_BLOCK, SB_GROUP, SB_HEADS_PER_STEP, SB_QB
    assert tp >= G * B and tp % QB == 0 and QB % B == 0 and G * B >= QB
    grid = (HEADS // hb, nq)
    split, ride_first, ride_last = _ride(rider, 5, 3, grid)

    def body(*refs):
        (q_ref, k_ref, v_ref, o_ref, do_ref), (dq_ref, dk_ref, dv_ref), ride = split(refs)
        ride_first(ride)
        i = pl.program_id(1)

        @pl.when(i == 0)
        def _():
            dk_ref[...] = jnp.zeros_like(dk_ref)
            dv_ref[...] = jnp.zeros_like(dv_ref)

        R = range(hb)
        hs = [slice(h * SB_DH, (h + 1) * SB_DH) for h in R]
        q = [q_ref[:, hs[h]] for h in R]
        dob = [do_ref[:, hs[h]].astype(bf16) for h in R]
        et = [_rowsum(dob[h].astype(f32) * o_ref[:, hs[h]]) for h in R]
        us = (lax.broadcasted_iota(jnp.int32, (B, B), 0) > lax.broadcasted_iota(jnp.int32, (B, B), 1)).astype(bf16)

        def make_step(masked):
            def step(t, carry):
                dq, cs, ce = carry
                jlo, rows, kend = _sb_group(i, t)
                kcat = [k_ref[rows, hs[h]] for h in R]
                dwv = [_dot_nt(dob[h], v_ref[rows, hs[h]]) for h in R]
                lsp, vis, w, cs = _sb_weights(q, kcat, i, jlo, kend, cs, us, masked)
                wb = [w[h].astype(bf16) for h in R]
                ee = [dwv[h] * wb[h].astype(f32) for h in R]
                later = [_later_blocks(ee[h], G, ce[h]) for h in R]
                cum = [_cumsum_after(ee[h], G, us) for h in R]
                dz = []
                for h in R:
                    d = ee[h] - jnp.exp(lsp[h]) * (et[h] - (cum[h] + later[h][0]))
                    if masked:
                        d = jnp.where(vis, d, 0.0)
                    dz.append(d.astype(bf16))
                dkj = [_dot_tn(dz[h], q[h]) for h in R]
                dvj = [_dot_tn(wb[h], dob[h]) for h in R]
                dqj = [_dot(dz[h], kcat[h]) for h in R]
                for h in R:
                    dk_ref[rows, hs[h]] += dkj[h]
                    dv_ref[rows, hs[h]] += dvj[h]
                return tuple(dq[h] + dqj[h] for h in R), tuple(cs), tuple(later[h][1] for h in R)
            return step

        z0 = tuple(jnp.zeros((QB, 1), f32) for _ in R)
        dq, _, _ = _sb_loop(i, make_step, (tuple(jnp.zeros((QB, SB_DH), f32) for _ in R), z0, z0))
        for h in R:
            dq_ref[:, hs[h]] = dq[h] * (SB_DH ** -0.5)
        ride_last(ride)

    blk = pl.BlockSpec((QB, hb * SB_DH), lambda g, i: (i, g))
    full = pl.BlockSpec((tp, hb * SB_DH), lambda g, i: (0, g))
    r_ins = rider.ins if rider else []
    r_outs = rider.out_shapes if rider else []
    res = pl.pallas_call(
        body, name="sb_bwd", grid=grid, in_specs=[blk, full, full, blk, blk] + [_ANY] * len(r_ins),
        out_specs=[blk, full, full] + [_ANY] * len(r_outs),
        out_shape=[jax.ShapeDtypeStruct((tp, SB_W), f32)] * 3 + list(r_outs),
        scratch_shapes=rider.scratch if rider else [], compiler_params=_params(2),
    )(qs, ks, vs, o, do, *r_ins)
    return res[:3], res[3:]


def adamw(w, g, m, v, name):
    r, c = w.shape
    rt = _tile(r, 128, SUB) if r % SUB == 0 else r
    blk = pl.BlockSpec((rt, c), lambda i: (i, 0))
    c1 =1.0 - ADAM_B1 ** ADAM_STEP
    c2 = 1.0 - ADAM_B2 ** ADAM_STEP

    def body(w_ref, g_ref, m_ref, v_ref, d_ref, mo_ref, vo_ref):
        g_ = g_ref[...]
        m_ = ADAM_B1 * m_ref[...] + (1.0 - ADAM_B1) * g_
        v_ = ADAM_B2 * v_ref[...] + (1.0 - ADAM_B2) * (g_ * g_)
        mo_ref[...] = m_
        vo_ref[...] = v_
        d_ref[...] = -ADAM_LR * ((m_ / c1) / (jnp.sqrt(v_ / c2) + ADAM_EPS) + ADAM_WD * w_ref[...])

    return pl.pallas_call(
        body, name=name, grid=(r // rt,), in_specs=[blk] * 4, out_specs=[blk] * 3,
        out_shape=[jax.ShapeDtypeStruct((r, c), f32)] * 3, compiler_params=_params(1),
    )(w, g, m, v)


def sum_slots(x, name):
    n, r, c = x.shape
    rt = _tile(r, 128, SUB) if r % SUB == 0 else r
    blk = pl.BlockSpec((n, rt, c), lambda i: (0, i, 0))

    def body(x_ref, o_ref):
        acc = x_ref[0].astype(f32)
        for s in range(1, n):
            acc = acc + x_ref[s].astype(f32)
        o_ref[...] = acc

    return pl.pallas_call(
        body, name=name, grid=(r // rt,), in_specs=[blk], out_specs=pl.BlockSpec((rt, c), lambda i: (i, 0)),
        out_shape=jax.ShapeDtypeStruct((r, c), f32), compiler_params=_params(1),
    )(x)


def add2(a, b, name, out_dtype=f32):
    n, r, c = a.shape
    rt = _tile(r, 64, SUB) if r % SUB == 0 else r
    blk = pl.BlockSpec((n, rt, c), lambda i: (0, i, 0))

    def body(a_ref, b_ref, o_ref):
        o_ref[...] = (a_ref[...] + b_ref[...]).astype(out_dtype)

    return pl.pallas_call(
        body, name=name, grid=(r // rt,), in_specs=[blk, blk], out_specs=blk,
        out_shape=jax.ShapeDtypeStruct((n, r, c), out_dtype), compiler_params=_params(1),
    )(a, b)


_ANY = pl.BlockSpec(memory_space=pl.ANY)
_MESH = pl.DeviceIdType.MESH


def _coords():
    return lax.axis_index("x"), lax.axis_index("y"), lax.axis_index("c")


def _chip_peer(x, y, r):
    return x ^ (r >> 1), y ^ (r & 1)


class _Exchange:
    def __init__(self, ins, out_shapes, scratch, start, finish):
        self.ins, self.out_shapes, self.scratch, self.start, self.finish = ins, out_shapes, scratch, start, finish

    def split(self, refs):
        n, m = len(self.ins), len(self.out_shapes)
        return refs[:n], refs[n:n + m], refs[n + m:]


def run_exchange(ex, name):
    def body(*refs):
        ins, outs, sems = ex.split(refs)
        ex.start(ins, outs, sems)
        ex.finish(ins, outs, sems)

    return pl.pallas_call(body, name=name, in_specs=[_ANY] * len(ex.ins), out_specs=[_ANY] * len(ex.out_shapes),
                          out_shape=ex.out_shapes, scratch_shapes=ex.scratch)(*ex.ins)


def gather_chips(big, small):
    nb, n = len(big), len(big) + len(small)
    shards = list(big) + list(small)
    kb = nb * (N_CHIPS - 1)
    k = n * (N_CHIPS - 1)

    def copies(src, dst, sems):
        send, recv, fsend, frecv = sems
        x, y, c = _coords()
        sib = (x, y, 1 - c)
        peers = [_chip_peer(x, y, r) for r in range(1, N_CHIPS)]

        def direct(t, j, slot):
            s = t * (N_CHIPS - 1) + j
            if t < nb:
                return pltpu.make_async_remote_copy(src[t].at[c], dst[t].at[slot, c], send.at[s], recv.at[s],
                                                    device_id=(*peers[j], c), device_id_type=_MESH)
            return pltpu.make_async_remote_copy(src[t], dst[t].at[slot], send.at[s], recv.at[s],
                                                device_id=(*peers[j], c), device_id_type=_MESH)

        def passed(t, j, half):
            s = t * (N_CHIPS - 1) + j
            px, py = peers[j]
            part = dst[t].at[2 * px + py, half]
            return pltpu.make_async_remote_copy(part, part, fsend.at[s], frecv.at[s], device_id=sib, device_id_type=_MESH)

        return direct, passed, peers, 2 * x + y, c

    def start(src, dst, sems):
        direct, _, _, me, _ = copies(src, dst, sems)
        for t in range(n):
            for j in range(N_CHIPS - 1):
                direct(t, j, me).start()

    def finish(src, dst, sems):
        direct, passed, peers, me, c = copies(src, dst, sems)
        fwd = []
        for t in range(nb):
            for j in range(N_CHIPS - 1):
                px, py = peers[j]
                direct(t, j, 2 * px + py).wait_recv()
                fwd.append(passed(t, j, c))
                fwd[-1].start()
        for t in range(nb, n):
            for j in range(N_CHIPS - 1):
                px, py = peers[j]
                direct(t, j, 2 * px + py).wait_recv()
        for t in range(nb):
            for j in range(N_CHIPS - 1):
                passed(t, j, 1 - c).wait_recv()
        for t in range(n):
            for j in range(N_CHIPS - 1):
                direct(t, j, me).wait_send()
        for cp in fwd:
            cp.wait_send()

    return _Exchange(shards, [jax.ShapeDtypeStruct((N_CHIPS,) + s.shape, s.dtype) for s in shards],
                     [pltpu.SemaphoreType.DMA((k,)), pltpu.SemaphoreType.DMA((k,)),
                      pltpu.SemaphoreType.DMA((max(kb, 1),)), pltpu.SemaphoreType.DMA((max(kb, 1),))], start, finish)


def sibling_swap(grads):
    pairs = [(t, o) for t, g in enumerate(grads) for o in range(g.shape[0])]
    k = len(pairs)

    def copies(src, dst, sems):
        send, recv = sems
        x, y, c = _coords()
        return [pltpu.make_async_remote_copy(src[t].at[o, 1 - c], dst[t].at[o], send.at[s], recv.at[s],
                                             device_id=(x, y, 1 - c), device_id_type=_MESH)
                for s, (t, o) in enumerate(pairs)]

    def start(src, dst, sems):
        for cp in copies(src, dst, sems):
            cp.start()

    def finish(src, dst, sems):
        cps = copies(src, dst, sems)
        for cp in cps:
            cp.wait_recv()
        for cp in cps:
            cp.wait_send()

    return _Exchange(list(grads), [jax.ShapeDtypeStruct((g.shape[0],) + g.shape[2:], g.dtype) for g in grads],
                     [pltpu.SemaphoreType.DMA((k,)), pltpu.SemaphoreType.DMA((k,))], start, finish)


def scatter_chips(parts):
    n = len(parts)
    k = n * (N_CHIPS - 1)

    def copy(src, dst, sems, t, r, landing):
        send, recv = sems
        x, y, c = _coords()
        me = 2 * x + y
        px, py = _chip_peer(x, y, r)
        peer = 2 * px + py
        s = t * (N_CHIPS - 1) + r - 1
        return pltpu.make_async_remote_copy(src[t].at[me if landing else peer], dst[t].at[peer if landing else me],
                                            send.at[s], recv.at[s], device_id=(px, py, c), device_id_type=_MESH)

    def start(src, dst, sems):
        for t in range(n):
            for r in range(1, N_CHIPS):
                copy(src, dst, sems, t, r, False).start()

    def finish(src, dst, sems):
        for t in range(n):
            for r in range(1, N_CHIPS):
                copy(src, dst, sems, t, r, True).wait_recv()
        for t in range(n):
            for r in range(1, N_CHIPS):
                copy(src, dst, sems, t, r, False).wait_send()

    return _Exchange(list(parts), [jax.ShapeDtypeStruct(p.shape, p.dtype) for p in parts],
                     [pltpu.SemaphoreType.DMA((k,)), pltpu.SemaphoreType.DMA((k,))], start, finish)


def sibling_send(halves, name):
    n = len(halves)

    def body(*refs):
        src, dst = refs[:n], refs[n:2 * n]
        send, recv = refs[2 * n:]
        x, y, c = _coords()
        cps = [pltpu.make_async_remote_copy(src[t], dst[t], send.at[t], recv.at[t],
                                            device_id=(x, y, 1 - c), device_id_type=_MESH) for t in range(n)]
        for cp in cps:
            cp.start()
        for cp in cps:
            cp.wait_recv()
        for cp in cps:
            cp.wait_send()

    return pl.pallas_call(
        body, name=name, in_specs=[_ANY] * n, out_specs=[_ANY] * n,
        out_shape=[jax.ShapeDtypeStruct(h.shape, h.dtype) for h in halves],
        scratch_shapes=[pltpu.SemaphoreType.DMA((n,)), pltpu.SemaphoreType.DMA((n,))],
    )(*halves)


def gather_all(block, name):
    def body(src, dst, send, recv, loc):
        x, y, c = _coords()
        me = 4 * x + 2 * y + c
        mine = pltpu.make_async_copy(src, dst.at[me], loc)
        mine.start()
        outs = []
        for r in range(1, N_DEV):
            peer = (x ^ (r >> 2), y ^ ((r >> 1) & 1), c ^ (r & 1))
            outs.append(pltpu.make_async_remote_copy(src, dst.at[me], send.at[r - 1], recv.at[r - 1],
                                                     device_id=peer, device_id_type=_MESH))
        for cp in outs:
            cp.start()
        for r in range(1, N_DEV):
            px, py, pc = x ^ (r >> 2), y ^ ((r >> 1) & 1), c ^ (r & 1)
            pltpu.make_async_remote_copy(src, dst.at[4 * px + 2 * py + pc], send.at[r - 1], recv.at[r - 1],
                                         device_id=(px, py, pc), device_id_type=_MESH).wait_recv()
        for cp in outs:
            cp.wait_send()
        mine.wait()

    return pl.pallas_call(
        body, name=name, in_specs=[_ANY], out_specs=_ANY,
        out_shape=jax.ShapeDtypeStruct((N_DEV,) + block.shape, block.dtype),
        scratch_shapes=[pltpu.SemaphoreType.DMA((N_DEV - 1,)), pltpu.SemaphoreType.DMA((N_DEV - 1,)),
                        pltpu.SemaphoreType.DMA(())],
    )(block)


def _pad_lanes(v, n=LANE):
    return jnp.pad(v, ((0, 0), (0, n - v.shape[1])))


def _w_in_pieces():
    cs = (PROJ_BIG + 2 * HEADS) // N_CHIPS
    ab_end = AB_COL + 2 * HEADS
    out = []
    for o in range(N_CHIPS):
        lo, hi = o * cs, (o + 1) * cs
        cand = [("big", lo, min(hi, AB_COL), 0), ("ab", max(lo, AB_COL), min(hi, ab_end), AB_COL),
                ("big", max(lo, ab_end), hi, 2 * HEADS)]
        out.append([(s, a - off, b - off) for s, a, b, off in cand if a < b])
    return out


def _split_w_in(w4):
    big, ab = [], []
    for o, pieces in enumerate(_w_in_pieces()):
        at = 0
        for s, a, b in pieces:
            (big if s == "big" else ab).append(w4[o][:, at:at + b - a])
            at += b - a
    return jnp.concatenate(big, axis=1), _pad_lanes(jnp.concatenate(ab, axis=1))


def _join_w_in(big, ab):
    src = {"big": big, "ab": ab}
    return jnp.stack([jnp.concatenate([src[s][:, a:b] for s, a, b in pieces], axis=1) for pieces in _w_in_pieces()])


def _conv_w8(w):
    return jnp.pad(w, ((0, SUB - DN_CONV), (0, 0)))


def _row_layout(gc, tp):
    nc = tp // CHUNK
    g = gc[:, :, 0].reshape(HEADS, nc, 1, CHUNK)
    g = jnp.broadcast_to(g, (HEADS, nc, SUB, CHUNK))
    return jnp.pad(g, ((0, 0), (0, 0), (0, 0), (0, LANE - CHUNK))).reshape(HEADS, nc * SUB, LANE)


def _step(x, meta, W, target, late_weights=None, early_swap=None, early_grads=None, last_grads=None):
    W = dict(W)
    seq = x.shape[0]
    tp = P0 + N_META + seq
    h0 = jnp.concatenate([jnp.zeros((P0, D_MODEL), f32), meta, x], axis=0)
    w_big, w_ab = _split_w_in(W["w_in"])
    cq8, ck8, cv8 = _conv_w8(W["conv_q"]), _conv_w8(W["conv_k"]), _conv_w8(W["conv_v"])
    al, dtb = _pad_lanes(W["dn_a_log"]), _pad_lanes(W["dn_dt_bias"])

    n1 = rms_fwd(h0, W["norm_mix_gain"], "rms1_fwd")
    proj = matmul(n1, w_big, "nn", "proj_fwd")
    pab = matmul(n1, w_ab, "nn", "pab_fwd")
    qn = conv_fwd(proj, cq8, C_DQ * 8, 8, True, "conv_q_fwd")
    kn = conv_fwd(proj, ck8, C_DK * 8, 8, True, "conv_k_fwd")
    va = conv_fwd(proj, cv8, C_DV * 8, 16, False, "conv_v_fwd")
    gc, bc = ab_fwd(pab, al, dtb)
    grow = _row_layout(gc, tp)
    o_dn, states = gdn_fwd(qn, kn, va, gc, bc, grow)
    on = dn_out_fwd(o_dn, proj, W["dn_out_norm_gain"])
    qs, ks, vs = sb_prep_fwd(proj, W["sb_q_norm_gain"], W["sb_k_norm_gain"])
    o_sb, o_sb16, arrived = sb_fwd(qs, ks, vs, rider=late_weights[0] if late_weights else None)
    if late_weights:
        W.update(late_weights[1](arrived))
    ydn = matmul(on, W["w_branch_dn"], "nn", "ydn_fwd")
    ysb = matmul(o_sb16, W["w_branch_sb"], "nn", "ysb_fwd")
    merged = merge_fwd(proj, ydn, ysb)
    h1 = matmul(merged, W["w_out"], "nn", "wout_fwd", residual=h0)
    n2 = rms_fwd(h1, W["norm_ffn_gain"], "rms2_fwd")
    u = matmul(n2, W["w_ffn_in"], "nn", "ffn_in_fwd", tn_t=512)
    act = swiglu_fwd(u)
    y = matmul(act, W["w_ffn_out"], "nn", "ffn_out_fwd", residual=h1)
    dy, dy16, loss = loss_head(y, target)

    G = {}
    dact = matmul(dy16, W["w_ffn_out"], "nt", "ffn_out_dx", tn_t=1408)
    G["w_ffn_out"] = matmul(act, dy16, "tn", "ffn_out_dw", tm_t=1408)
    dgate, dup = swiglu_bwd(u, dact)
    du = jnp.concatenate([dgate, dup], axis=1)
    dn2 = matmul(du, W["w_ffn_in"], "nt", "ffn_in_dx", tk_t=512)
    G["w_ffn_in"] = matmul(n2, du, "tn", "ffn_in_dw", tn_t=512)
    dh1, dh1_16, G["norm_ffn_gain"] = rms_bwd(h1, W["norm_ffn_gain"], dn2, dy, "rms2_bwd")
    dmerged = matmul(dh1_16, W["w_out"], "nt", "wout_dx")
    G["w_out"] = matmul(merged, dh1_16, "tn", "wout_dw")
    dyd, dys, d_gates = merge_bwd(proj, ydn, ysb, dmerged)
    don = matmul(dyd, W["w_branch_dn"], "nt", "ydn_dx")
    G["w_branch_dn"] = matmul(on, dyd, "tn", "ydn_dw")
    do_sb = matmul(dys, W["w_branch_sb"], "nt", "ysb_dx")
    G["w_branch_sb"] = matmul(o_sb16, dys, "tn", "ysb_dw")
    do_dn, dz, G["dn_out_norm_gain"] = dn_out_bwd(o_dn, proj, W["dn_out_norm_gain"], don)
    (dqn, dkn, dva, dgc, dbc), swapped = gdn_bwd(qn, kn, va, gc, bc, grow, states, do_dn,
                                                 rider=early_swap[0](G) if early_swap else None)
    if early_swap:
        early_swap[1](swapped)
    dpab, dal, ddt = ab_bwd(pab, al, dtb, dgc, dbc)
    G["dn_a_log"], G["dn_dt_bias"] = dal[:, :HEADS], ddt[:, :HEADS]
    dyq, dcq = conv_bwd_act(proj, cq8, dqn, C_DQ * 8, 8, True, "conv_q_bwd")
    dyk, dck = conv_bwd_act(proj, ck8, dkn, C_DK * 8, 8, True, "conv_k_bwd")
    dyv, dcv = conv_bwd_act(proj, cv8, dva, C_DV * 8, 16, False, "conv_v_bwd")
    G["conv_q"], G["conv_k"], G["conv_v"] = dcq[:DN_CONV], dck[:DN_CONV], dcv[:DN_CONV]
    d_dq = conv_bwd_in(dyq, cq8, "conv_q_dx")
    d_dk = conv_bwd_in(dyk, ck8, "conv_k_dx")
    d_dv = conv_bwd_in(dyv, cv8, "conv_v_dx")
    (dqs, dks, dvs), delivered = sb_bwd(qs, ks, vs, o_sb, do_sb, rider=early_grads[0](G) if early_grads else None)
    if early_grads:
        early_grads[1](delivered)
    d_sb, G["sb_q_norm_gain"], G["sb_k_norm_gain"] = sb_prep_bwd(
        proj, W["sb_q_norm_gain"], W["sb_k_norm_gain"], dqs, dks, dvs)
    dproj = jnp.concatenate([d_dq, d_dk, d_dv, dz, d_sb, d_gates], axis=1)
    dw_big = matmul(n1, dproj, "tn", "proj_dw")
    dw_ab = matmul(n1, dpab, "tn", "pab_dw")
    G["w_in"] = (dw_big, dw_ab)
    if last_grads:
        dn1, delivered = matmul(dproj, w_big, "nt", "proj_dx", tk_t=1024, rider=last_grads[0](G))
        last_grads[1](delivered)
    else:
        dn1 = matmul(dproj, w_big, "nt", "proj_dx", tk_t=1024)
    dn1 = matmul(dpab, w_ab, "nt", "pab_dx", residual=dn1)
    dh0, _, G["norm_mix_gain"] = rms_bwd(h0, W["norm_mix_gain"], dn1, dh1, "rms1_bwd")
    G["meta_tokens"] = dh0[P0:P0 + N_META]
    return loss, dh0[P0 + N_META:], G


_BIG = ("w_in", "w_branch_dn", "w_branch_sb", "w_out", "w_ffn_in", "w_ffn_out")
_COL_SHARDED = ("w_in", "w_ffn_in", "meta_tokens", "conv_q", "conv_k", "conv_v")
_SMALL_REPL = ("norm_mix_gain", "norm_ffn_gain", "dn_a_log", "dn_dt_bias", "dn_out_norm_gain", "sb_q_norm_gain",
               "sb_k_norm_gain")
_SMALL_SHARD = ("meta_tokens", "conv_q", "conv_k", "conv_v")
_ORDER = ("meta_tokens", "norm_mix_gain", "w_in", "conv_q", "conv_k", "conv_v", "dn_a_log", "dn_dt_bias",
          "dn_out_norm_gain", "sb_q_norm_gain", "sb_k_norm_gain", "w_branch_dn", "w_branch_sb", "w_out",
          "norm_ffn_gain", "w_ffn_in", "w_ffn_out")


def _unshard(g4, name):
    if name in _COL_SHARDED:
        r, cs = g4.shape[1:]
        return jnp.transpose(g4, (1, 0, 2)).reshape(r, N_CHIPS * cs)
    return g4.reshape((-1,) + g4.shape[2:])


def _to_shards(full, name):
    if full.ndim == 3:
        return full
    if name in _COL_SHARDED:
        r, c = full.shape
        return jnp.transpose(full.reshape(r, N_CHIPS, c // N_CHIPS), (1, 0, 2))
    r, c = full.shape
    return full.reshape(N_CHIPS, r // N_CHIPS, c)


def _rows_1024(a):
    r, c = a.shape
    if c >= 1024:
        return a.reshape(r * (c // 1024), 1024)
    return jnp.pad(a, ((0, 0), (0, 1024 - c)))


def kernel(x, meta_tokens, norm_mix_gain, w_in, conv_q, conv_k, conv_v, dn_a_log, dn_dt_bias, dn_out_norm_gain, sb_q_norm_gain, sb_k_norm_gain, w_branch_dn, w_branch_sb, w_out, norm_ffn_gain, w_ffn_in, w_ffn_out, loss_target, m_meta_tokens, m_norm_mix_gain, m_w_in, m_conv_q, m_conv_k, m_conv_v, m_dn_a_log, m_dn_dt_bias, m_dn_out_norm_gain, m_sb_q_norm_gain, m_sb_k_norm_gain, m_w_branch_dn, m_w_branch_sb, m_w_out, m_norm_ffn_gain, m_w_ffn_in, m_w_ffn_out, v_meta_tokens, v_norm_mix_gain, v_w_in, v_conv_q, v_conv_k, v_conv_v, v_dn_a_log, v_dn_dt_bias, v_dn_out_norm_gain, v_sb_q_norm_gain, v_sb_k_norm_gain, v_w_branch_dn, v_w_branch_sb, v_w_out, v_norm_ffn_gain, v_w_ffn_in, v_w_ffn_out):
    Wl = dict(meta_tokens=meta_tokens, norm_mix_gain=norm_mix_gain, w_in=w_in[0], conv_q=conv_q[0], conv_k=conv_k[0],
              conv_v=conv_v[0], dn_a_log=dn_a_log, dn_dt_bias=dn_dt_bias, dn_out_norm_gain=dn_out_norm_gain,
              sb_q_norm_gain=sb_q_norm_gain, sb_k_norm_gain=sb_k_norm_gain, w_branch_dn=w_branch_dn[0],
              w_branch_sb=w_branch_sb[0], w_out=w_out[0], norm_ffn_gain=norm_ffn_gain, w_ffn_in=w_ffn_in[0],
              w_ffn_out=w_ffn_out[0])
    Ml = dict(meta_tokens=m_meta_tokens, norm_mix_gain=m_norm_mix_gain, w_in=m_w_in[0], conv_q=m_conv_q[0],
              conv_k=m_conv_k[0], conv_v=m_conv_v[0], dn_a_log=m_dn_a_log, dn_dt_bias=m_dn_dt_bias,
              dn_out_norm_gain=m_dn_out_norm_gain, sb_q_norm_gain=m_sb_q_norm_gain, sb_k_norm_gain=m_sb_k_norm_gain,
              w_branch_dn=m_w_branch_dn[0], w_branch_sb=m_w_branch_sb[0], w_out=m_w_out[0],
              norm_ffn_gain=m_norm_ffn_gain, w_ffn_in=m_w_ffn_in[0], w_ffn_out=m_w_ffn_out[0])
    Vl = dict(meta_tokens=v_meta_tokens, norm_mix_gain=v_norm_mix_gain, w_in=v_w_in[0], conv_q=v_conv_q[0],
              conv_k=v_conv_k[0], conv_v=v_conv_v[0], dn_a_log=v_dn_a_log, dn_dt_bias=v_dn_dt_bias,
              dn_out_norm_gain=v_dn_out_norm_gain, sb_q_norm_gain=v_sb_q_norm_gain, sb_k_norm_gain=v_sb_k_norm_gain,
              w_branch_dn=v_w_branch_dn[0], w_branch_sb=v_w_branch_sb[0], w_out=v_w_out[0],
              norm_ffn_gain=v_norm_ffn_gain, w_ffn_in=v_w_ffn_in[0], w_ffn_out=v_w_ffn_out[0])
    lead = {n: (1,) if (n in _BIG or n in ("conv_q", "conv_k", "conv_v")) else () for n in _ORDER}

    chip = 2 * lax.axis_index("x") + lax.axis_index("y")
    c = lax.axis_index("c")
    halved = {n: Wl[n].astype(bf16).reshape(2, Wl[n].shape[0] // 2, Wl[n].shape[1]) for n in _BIG}

    def gathered_weights(names, owns, outs):
        res = {}
        for n, own, g4 in zip(names, owns, outs):
            g4 = lax.dynamic_update_slice(g4, own[None], (chip,) + (0,) * own.ndim)
            if n in _BIG:
                g4 = g4.reshape(N_CHIPS, 2 * g4.shape[2], g4.shape[3])
            res[n] = g4 if n == "w_in" else _unshard(g4, n)
        return res

    first = ["w_in"] + list(_SMALL_SHARD)
    first_own = [halved["w_in"]] + [Wl[n] for n in _SMALL_SHARD]
    W = dict(Wl)
    W.update(gathered_weights(first, first_own, run_exchange(gather_chips(first_own[:1], first_own[1:]), "gather_w_in")))
    late = [n for n in _BIG if n != "w_in"]
    late_own = [halved[n] for n in late]
    for n in late:
        del W[n]

    def halves_of(names, G):
        g4 = [_to_shards(G[n], n) for n in names]
        return [g.reshape(N_CHIPS, 2, g.shape[1] // 2, g.shape[2]) for g in g4]

    def pair_added(g42, from_sib, tag, wire):
        mine = [lax.dynamic_index_in_dim(g, c, axis=1, keepdims=False) for g in g42]
        return [add2(a, b, "grad_pair_add_%s%d" % (tag, t), out_dtype=wire)
                for t, (a, b) in enumerate(zip(mine, from_sib))]

    def chip_reduced(parts, slots, tag):
        slots = [lax.dynamic_update_slice(s, lax.dynamic_index_in_dim(p, chip, axis=0, keepdims=True), (chip, 0, 0))
                 for s, p in zip(slots, parts)]
        return [sum_slots(s, "grad_chip_sum_%s%d" % (tag, t)) for t, s in enumerate(slots)]

    early, last = {}, {}

    def early_swap_begin(G):
        early["g42"] = halves_of(late, G)
        return sibling_swap(early["g42"])

    def early_begin(G):
        early["parts"] = pair_added(early["g42"], early["from_sib"], "a", f32)
        return scatter_chips(early["parts"])

    def last_begin(G):
        g2 = [g.reshape(1, 2, g.shape[0] // 2, g.shape[1]) for g in G["w_in"]]
        added = pair_added(g2, run_exchange(sibling_swap(g2), "grad_sibling_swap_b"), "b", bf16)
        last["parts"] = [_join_w_in(added[0][0], added[1][0])]
        return scatter_chips(last["parts"])

    loss, grad_x, G = _step(
        x[0], W["meta_tokens"], W, loss_target[0],
        late_weights=(gather_chips(late_own, []), lambda outs: gathered_weights(late, late_own, outs)),
        early_swap=(early_swap_begin, lambda outs: early.update(from_sib=outs)),
        early_grads=(early_begin, lambda slots: early.update(halves=chip_reduced(early["parts"], slots, "a"))),
        last_grads=(last_begin, lambda slots: last.update(halves=chip_reduced(last["parts"], slots, "b"))))
    halves = last["halves"] + early["halves"]
    theirs = sibling_send(halves, "grad_sibling_send")
    Gs = {}
    for n, h, o in zip(["w_in"] + late, halves, theirs):
        Gs[n] = lax.dynamic_update_slice(jnp.concatenate([o, o], axis=0), h, (c * h.shape[0], 0))

    small_names = list(_SMALL_REPL) + list(_SMALL_SHARD)
    pieces = [_rows_1024(G[n]) for n in small_names] + [_rows_1024(loss)]
    counts = [p.shape[0] for p in pieces]
    pack = jnp.concatenate(pieces, axis=0)
    pad_rows = (-pack.shape[0]) % SUB
    pack = jnp.pad(pack, ((0, pad_rows), (0, 0)))
    total = sum_slots(gather_all(pack, "small_gather"), "small_sum")
    chip = 2 * lax.axis_index("x") + lax.axis_index("y")
    row = 0
    for n, cnt in zip(small_names, counts[:-1]):
        blk = total[row:row + cnt]
        row += cnt
        full_shape = G[n].shape
        if full_shape[1] >= 1024:
            blk = blk.reshape(full_shape)
        else:
            blk = blk[:, :full_shape[1]]
        if n in _SMALL_SHARD:
            cs = full_shape[1] // N_CHIPS
            blk = lax.dynamic_slice_in_dim(blk, chip * cs, cs, axis=1)
        Gs[n] = blk
    loss_out = total[row, 0]

    grads, deltas, new_m, new_v = [], [], [], []
    for n in _ORDER:
        d, m2, v2 = adamw(Wl[n], Gs[n], Ml[n], Vl[n], "adamw_" + n)
        shape = lead[n] + Wl[n].shape
        grads.append(Gs[n].reshape(shape))
        deltas.append(d.reshape(shape))
        new_m.append(m2.reshape(shape))
        new_v.append(v2.reshape(shape))
    return (loss_out, grad_x[None], *grads, *deltas, *new_m, *new_v)
```

```python
import jax
import jax.numpy as jnp
from jax import lax
from jax.experimental import pallas as pl
from jax.experimental.pallas import tpu as pltpu

f32 = jnp.float32
bf16 = jnp.bfloat16

D_MODEL = 1024
N_META = 16
CHUNK = 64
HEADS = 8
DN_DK = 128
DN_DV = 256
DN_CONV = 4
DN_QK = HEADS * DN_DK
DN_V = HEADS * DN_DV
SB_DH = 128
SB_W = HEADS * SB_DH
SB_BLOCK = 128
SB_QB = 384
SB_GROUP = 4
SB_HEADS_PER_STEP = 2
SB_FWD_HEADS_PER_STEP = 4
GDN_HEADS_PER_STEP = 8
CONV_W = 1024
D_FF = 2816
RMS_EPS = 1e-6
L2_EPS = 1e-6
ADAM_LR = 0.001
ADAM_B1 = 0.9
ADAM_B2 = 0.999
ADAM_EPS = 1e-08
ADAM_WD = 0.01
ADAM_STEP = 10

P0 = 112
LANE = 128
SUB = 8
VMEM_LIMIT = 48 * 1024 * 1024
N_CHIPS = 4
N_DEV = 8

C_DQ, C_DK, C_DV, C_DZ, C_SQ, C_SK, C_SV, C_GDN, C_GSB = 0, 1, 2, 4, 6, 7, 8, 9, 10
PROJ_BIG = 11 * 1024
AB_COL = 2 * DN_QK + 2 * DN_V


def _params(n_axes):
    return pltpu.CompilerParams(dimension_semantics=("arbitrary",) * n_axes, vmem_limit_bytes=VMEM_LIMIT)


def _tile(n, target, q=LANE):
    best = None
    for t in range(q, min(n, target) + 1, q):
        if n % t == 0:
            best = t
    return best if best is not None else n


def _dot(a, b):
    return jnp.dot(a.astype(bf16), b.astype(bf16), preferred_element_type=f32)


def _dot_nt(a, b):
    return lax.dot_general(a.astype(bf16), b.astype(bf16), (((1,), (1,)), ((), ())), preferred_element_type=f32)


def _dot_tn(a, b):
    return lax.dot_general(a.astype(bf16), b.astype(bf16), (((0,), (0,)), ((), ())), preferred_element_type=f32)


_HI = lax.Precision.HIGH


def _hdot(a, b):
    return jnp.dot(a, b, precision=_HI, preferred_element_type=f32)


def _hdot_nt(a, b):
    return lax.dot_general(a, b, (((1,), (1,)), ((), ())), precision=_HI, preferred_element_type=f32)


def _hdot_tn(a, b):
    return lax.dot_general(a, b, (((0,), (0,)), ((), ())), precision=_HI, preferred_element_type=f32)


def _sigmoid(x):
    return 0.5 * jnp.tanh(0.5 * x) + 0.5


def _log1p_small(e):
    return jnp.where(e < 1e-3, e * (1.0 - e * (0.5 - e * (1.0 / 3.0))), jnp.log(1.0 + e))


def _rowsum(x):
    return jnp.sum(x, axis=1, keepdims=True)


def _allsum(x):
    return jnp.sum(jnp.sum(x, axis=1, keepdims=True), axis=0, keepdims=True)


def matmul(a, b, mode, name, residual=None, out_dtype=f32, tm_t=1408, tn_t=1024, tk_t=1408, rider=None):
    if mode == "nn":
        (M, K), (K2, N) = a.shape, b.shape
    elif mode == "nt":
        (M, K), (N, K2) = a.shape, b.shape
    else:
        (K, M), (K2, N) = a.shape, b.shape
    assert K == K2, (a.shape, b.shape, mode)
    tm, tn, tk = _tile(M, tm_t), _tile(N, tn_t), _tile(K, tk_t)
    nk = K // tk
    if mode == "nn":
        a_spec = pl.BlockSpec((tm, tk), lambda i, j, k: (i, k))
        b_spec = pl.BlockSpec((tk, tn), lambda i, j, k: (k, j))
        dims = (((1,), (0,)), ((), ()))
    elif mode == "nt":
        a_spec = pl.BlockSpec((tm, tk), lambda i, j, k: (i, k))
        b_spec = pl.BlockSpec((tn, tk), lambda i, j, k: (j, k))
        dims = (((1,), (1,)), ((), ()))
    else:
        a_spec = pl.BlockSpec((tk, tm), lambda i, j, k: (k, i))
        b_spec = pl.BlockSpec((tk, tn), lambda i, j, k: (k, j))
        dims = (((0,), (0,)), ((), ()))
    o_spec = pl.BlockSpec((tm, tn), lambda i, j, k: (i, j))
    has_res = residual is not None
    grid = (M // tm, N // tn, nk)
    split, ride_first, ride_last = _ride(rider, 3 if has_res else 2, 1, grid)

    def body(*refs):
        ins_, (o_ref,), (rin, rout, rest) = split(refs)
        a_ref, b_ref = ins_[:2]
        r_ref = ins_[2] if has_res else None
        acc_ref, ride = rest[0], (rin, rout, rest[1:])
        ride_first(ride)
        k = pl.program_id(2)

        @pl.when(k == 0)
        def _():
            acc_ref[...] = jnp.zeros_like(acc_ref)

        acc_ref[...] += lax.dot_general(a_ref[...].astype(bf16), b_ref[...].astype(bf16), dims,
                                        preferred_element_type=f32)

        @pl.when(k == nk - 1)
        def _():
            r = acc_ref[...]
            if has_res:
                r = r + r_ref[...]
            o_ref[...] = r.astype(out_dtype)

        ride_last(ride)

    ins = [a, b] + ([residual] if has_res else [])
    specs = [a_spec, b_spec] + ([o_spec] if has_res else [])
    r_ins = rider.ins if rider else []
    r_outs = rider.out_shapes if rider else []
    res = pl.pallas_call(
        body, name=name, grid=grid, in_specs=specs + [_ANY] * len(r_ins), out_specs=[o_spec] + [_ANY] * len(r_outs),
        out_shape=[jax.ShapeDtypeStruct((M, N), out_dtype)] + list(r_outs),
        scratch_shapes=[pltpu.VMEM((tm, tn), f32)] + (rider.scratch if rider else []), compiler_params=_params(3),
    )(*ins, *r_ins)
    return (res[0], res[1:]) if rider else res[0]


def _row_tile(tp):
    return _tile(tp, 512)


def rms_fwd(h, gain, name):
    tp, d = h.shape
    rt = _row_tile(tp)

    def body(h_ref, g_ref, o_ref):
        x = h_ref[...]
        r = lax.rsqrt(jnp.mean(x * x, axis=-1, keepdims=True) + RMS_EPS)
        o_ref[...] = (x * r * g_ref[...]).astype(bf16)

    return pl.pallas_call(
        body, name=name, grid=(tp // rt,),
        in_specs=[pl.BlockSpec((rt, d), lambda i: (i, 0)), pl.BlockSpec((1, d), lambda i: (0, 0))],
        out_specs=pl.BlockSpec((rt, d), lambda i: (i, 0)),
        out_shape=jax.ShapeDtypeStruct((tp, d), bf16), compiler_params=_params(1),
    )(h, gain)


def rms_bwd(h, gain, dn, dres, name):
    tp, d = h.shape
    rt = _row_tile(tp)

    def body(h_ref, g_ref, dn_ref, dr_ref, dh_ref, dhb_ref, dg_ref):
        i = pl.program_id(0)
        x = h_ref[...]
        r = lax.rsqrt(jnp.mean(x * x, axis=-1, keepdims=True) + RMS_EPS)
        xh = x * r
        dn_ = dn_ref[...]
        dxh = dn_ * g_ref[...]
        dh = r * (dxh - xh * jnp.mean(dxh * xh, axis=-1, keepdims=True)) + dr_ref[...]
        dh_ref[...] = dh
        dhb_ref[...] = dh.astype(bf16)
        part = jnp.sum(dn_ * xh, axis=0, keepdims=True)

        @pl.when(i == 0)
        def _():
            dg_ref[...] = part

        @pl.when(i > 0)
        def _():
            dg_ref[...] += part

    row = pl.BlockSpec((rt, d), lambda i: (i, 0))
    vec = pl.BlockSpec((1, d), lambda i: (0, 0))
    return pl.pallas_call(
        body, name=name, grid=(tp // rt,), in_specs=[row, vec, row, row], out_specs=[row, row, vec],
        out_shape=[jax.ShapeDtypeStruct((tp, d), f32), jax.ShapeDtypeStruct((tp, d), bf16),
                   jax.ShapeDtypeStruct((1, d), f32)],
        compiler_params=_params(1),
    )(h, gain, dn, dres)


def loss_head(y, target):
    tp, d = y.shape
    lead = P0 + N_META
    rt = _row_tile(tp)
    ns = rt // lead
    assert lead == SB_BLOCK and rt % lead == 0 and target.shape == (tp - lead, d)
    last = target.shape[0] // lead - 1

    def body(*refs):
        y_ref, t_refs = refs[0], refs[1:1 + ns]
        dy_ref, dyb_ref, l_ref = refs[1 + ns:]
        i = pl.program_id(0)

        @pl.when(i == 0)
        def _():
            l_ref[...] = jnp.zeros_like(l_ref)

        part = jnp.zeros((1, 1), f32)
        for s in range(ns):
            rows = slice(s * lead, (s + 1) * lead)
            err = y_ref[rows, :] - t_refs[s][...]
            if s == 0:
                err = err * (i > 0).astype(f32)
            dy = err * (1.0 / d)
            dy_ref[rows, :] = dy
            dyb_ref[rows, :] = dy.astype(bf16)
            part = part + _allsum(err * err)
        l_ref[...] += jnp.broadcast_to(part * (0.5 / d), l_ref.shape)

    row = pl.BlockSpec((rt, d), lambda i: (i, 0))
    t_specs = [pl.BlockSpec((lead, d), lambda i, s=s: (jnp.clip(ns * i + s - 1, 0, last), 0)) for s in range(ns)]
    return pl.pallas_call(
        body, name="loss_head", grid=(tp // rt,), in_specs=[row] + t_specs,
        out_specs=[row, row, pl.BlockSpec((1, LANE), lambda i: (0, 0))],
        out_shape=[jax.ShapeDtypeStruct((tp, d), f32), jax.ShapeDtypeStruct((tp, d), bf16),
                   jax.ShapeDtypeStruct((1, LANE), f32)],
        compiler_params=_params(1),
    )(y, *([target] * ns))


def swiglu_fwd(u):
    tp = u.shape[0]
    rt, cb = _row_tile(tp), D_FF // 2
    nb = D_FF // cb

    def body(g_ref, u_ref, o_ref):
        g = g_ref[...]
        o_ref[...] = (g * _sigmoid(g) * u_ref[...]).astype(bf16)

    return pl.pallas_call(
        body, name="swiglu_fwd", grid=(tp // rt, nb),
        in_specs=[pl.BlockSpec((rt, cb), lambda i, j: (i, j)), pl.BlockSpec((rt, cb), lambda i, j: (i, j + nb))],
        out_specs=pl.BlockSpec((rt, cb), lambda i, j: (i, j)),
        out_shape=jax.ShapeDtypeStruct((tp, D_FF), bf16), compiler_params=_params(2),
    )(u, u)


def swiglu_bwd(u, dact):
    tp = u.shape[0]
    rt, cb = _row_tile(tp), D_FF // 2
    nb = D_FF // cb

    def body(g_ref, u_ref, da_ref, dg_ref, du_ref):
        g = g_ref[...]
        s = _sigmoid(g)
        da = da_ref[...]
        dg_ref[...] = (da * u_ref[...] * s * (1.0 + g * (1.0 - s))).astype(bf16)
        du_ref[...] = (da * g * s).astype(bf16)

    lo = pl.BlockSpec((rt, cb), lambda i, j: (i, j))
    hi = pl.BlockSpec((rt, cb), lambda i, j: (i, j + nb))
    dgate, dup = pl.pallas_call(
        body, name="swiglu_bwd", grid=(tp // rt, nb), in_specs=[lo, hi, lo], out_specs=[lo, lo],
        out_shape=[jax.ShapeDtypeStruct((tp, D_FF), bf16)] * 2, compiler_params=_params(2),
    )(u, u, dact)
    return dgate, dup


def merge_fwd(proj, ydn, ysb):
    tp = proj.shape[0]
    rt, d = _row_tile(tp), D_MODEL

    def body(gd_ref, gs_ref, yd_ref, ys_ref, o_ref):
        o_ref[...] = (_sigmoid(gd_ref[...]) * yd_ref[...] + _sigmoid(gs_ref[...]) * ys_ref[...]).astype(bf16)

    row = pl.BlockSpec((rt, d), lambda i: (i, 0))
    return pl.pallas_call(
        body, name="merge_fwd", grid=(tp // rt,),
        in_specs=[pl.BlockSpec((rt, d), lambda i: (i, C_GDN)), pl.BlockSpec((rt, d), lambda i: (i, C_GSB)), row, row],
        out_specs=row, out_shape=jax.ShapeDtypeStruct((tp, d), bf16), compiler_params=_params(1),
    )(proj, proj, ydn, ysb)


def merge_bwd(proj, ydn, ysb, dm):
    tp = proj.shape[0]
    rt, d = _row_tile(tp), D_MODEL

    def body(gd_ref, gs_ref, yd_ref, ys_ref, dm_ref, dyd_ref, dys_ref, dg_ref):
        dm_ = dm_ref[...]
        sd = _sigmoid(gd_ref[...])
        ss = _sigmoid(gs_ref[...])
        dyd_ref[...] = (dm_ * sd).astype(bf16)
        dys_ref[...] = (dm_ * ss).astype(bf16)
        dg_ref[:, :d] = (dm_ * yd_ref[...] * sd * (1.0 - sd)).astype(bf16)
        dg_ref[:, d:] = (dm_ * ys_ref[...] * ss * (1.0 - ss)).astype(bf16)

    row = pl.BlockSpec((rt, d), lambda i: (i, 0))
    return pl.pallas_call(
        body, name="merge_bwd", grid=(tp // rt,),
        in_specs=[pl.BlockSpec((rt, d), lambda i: (i, C_GDN)), pl.BlockSpec((rt, d), lambda i: (i, C_GSB)), row, row, row],
        out_specs=[row, row, pl.BlockSpec((rt, 2 * d), lambda i: (i, 0))],
        out_shape=[jax.ShapeDtypeStruct((tp, d), bf16)] * 2 + [jax.ShapeDtypeStruct((tp, 2 * d), bf16)],
        compiler_params=_params(1),
    )(proj, proj, ydn, ysb, dm)


def dn_out_fwd(o, proj, gain):
    tp = o.shape[0]
    rt, cb, wide = _row_tile(tp), DN_DV, 1024
    zb = C_DZ * 1024 // wide

    def body(o_ref, z_ref, g_ref, y_ref):
        for s in range(wide // cb):
            sl = slice(s * cb, (s + 1) * cb)
            x = o_ref[:, sl]
            r = lax.rsqrt(jnp.mean(x * x, axis=-1, keepdims=True) + RMS_EPS)
            z = z_ref[:, sl]
            y_ref[:, sl] = (x * r * g_ref[...] * (z * _sigmoid(z))).astype(bf16)

    blk = pl.BlockSpec((rt, wide), lambda i, j: (i, j))
    return pl.pallas_call(
        body, name="dn_out_fwd", grid=(tp // rt, DN_V // wide),
        in_specs=[blk, pl.BlockSpec((rt, wide), lambda i, j: (i, j + zb)), pl.BlockSpec((1, cb), lambda i, j: (0, 0))],
        out_specs=blk, out_shape=jax.ShapeDtypeStruct((tp, DN_V), bf16), compiler_params=_params(2),
    )(o, proj, gain)


def dn_out_bwd(o, proj, gain, dy):
    tp = o.shape[0]
    rt, cb, wide = _row_tile(tp), DN_DV, 1024
    zb = C_DZ * 1024 // wide

    def body(o_ref, z_ref, g_ref, dy_ref, do_ref, dz_ref, dg_ref):
        i, j = pl.program_id(0), pl.program_id(1)
        g = g_ref[...]
        part = jnp.zeros((1, cb), f32)
        for hh in range(wide // cb):
            sl = slice(hh * cb, (hh + 1) * cb)
            x = o_ref[:, sl]
            r = lax.rsqrt(jnp.mean(x * x, axis=-1, keepdims=True) + RMS_EPS)
            xh = x * r
            z = z_ref[:, sl]
            s = _sigmoid(z)
            dy_ = dy_ref[:, sl]
            drn = dy_ * (z * s)
            dz_ref[:, sl] = (dy_ * xh * g * s * (1.0 + z * (1.0 - s))).astype(bf16)
            dxh = drn * g
            do_ref[:, sl] = r * (dxh - xh * jnp.mean(dxh * xh, axis=-1, keepdims=True))
            part = part + jnp.sum(drn * xh, axis=0, keepdims=True)
        first = jnp.logical_and(i == 0, j == 0)

        @pl.when(first)
        def _():
            dg_ref[...] = part

        @pl.when(jnp.logical_not(first))
        def _():
            dg_ref[...] += part

    blk = pl.BlockSpec((rt, wide), lambda i, j: (i, j))
    vec = pl.BlockSpec((1, cb), lambda i, j: (0, 0))
    return pl.pallas_call(
        body, name="dn_out_bwd", grid=(tp // rt, DN_V // wide),
        in_specs=[blk, pl.BlockSpec((rt, wide), lambda i, j: (i, j + zb)), vec, blk],
        out_specs=[blk, pl.BlockSpec((rt, wide), lambda i, j: (i, j + zb)), vec],
        out_shape=[jax.ShapeDtypeStruct((tp, DN_V), f32), jax.ShapeDtypeStruct((tp, PROJ_BIG), bf16),
                   jax.ShapeDtypeStruct((1, cb), f32)],
        compiler_params=_params(2),
    )(o, proj, gain, dy)


def sb_prep_fwd(proj, gq, gk):
    tp = proj.shape[0]
    rt, cb = _row_tile(tp), SB_DH

    def body(q_ref, k_ref, v_ref, gq_ref, gk_ref, qo_ref, ko_ref, vo_ref):
        for x_ref, g_ref, o_ref in ((q_ref, gq_ref, qo_ref), (k_ref, gk_ref, ko_ref)):
            for h in range(HEADS):
                sl = slice(h * cb, (h + 1) * cb)
                x = x_ref[:, sl]
                r = lax.rsqrt(jnp.mean(x * x, axis=-1, keepdims=True) + RMS_EPS)
                o_ref[:, sl] = (x * r * g_ref[...]).astype(bf16)
        vo_ref[...] = v_ref[...].astype(bf16)

    blk = pl.BlockSpec((rt, SB_W), lambda i: (i, 0))
    vec = pl.BlockSpec((1, cb), lambda i: (0, 0))
    return pl.pallas_call(
        body, name="sb_prep_fwd", grid=(tp // rt,),
        in_specs=[pl.BlockSpec((rt, SB_W), lambda i: (i, C_SQ)), pl.BlockSpec((rt, SB_W), lambda i: (i, C_SK)),
                  pl.BlockSpec((rt, SB_W), lambda i: (i, C_SV)), vec, vec],
        out_specs=[blk] * 3, out_shape=[jax.ShapeDtypeStruct((tp, SB_W), bf16)] * 3, compiler_params=_params(1),
    )(proj, proj, proj, gq, gk)


def sb_prep_bwd(proj, gq, gk, dqs, dks, dvs, into):
    tp = proj.shape[0]
    rt, cb = _row_tile(tp), SB_DH
    assert (C_SQ * 1024) % (3 * SB_W) == 0 and (C_SQ + 1, C_SQ + 2) == (C_SK, C_SV)

    def body(q_ref, k_ref, gq_ref, gk_ref, dq_ref, dk_ref, dv_ref, into_ref, do_ref, dgq_ref, dgk_ref):
        first = pl.program_id(0) == 0
        do_ref[:, 2 * SB_W:] = dv_ref[...].astype(bf16)
        for x_ref, g_ref, dn_ref, at, dg_ref, mul in ((q_ref, gq_ref, dq_ref, 0, dgq_ref, None),
                                                      (k_ref, gk_ref, dk_ref, SB_W, dgk_ref, SB_DH ** -0.5)):
            part = jnp.zeros((1, cb), f32)
            for h in range(HEADS):
                sl = slice(h * cb, (h + 1) * cb)
                x = x_ref[:, sl]
                r = lax.rsqrt(jnp.mean(x * x, axis=-1, keepdims=True) + RMS_EPS)
                xh = x * r
                dn_ = dn_ref[:, sl] if mul is None else dn_ref[:, sl] * mul
                dxh = dn_ * g_ref[...]
                do_ref[:, at + h * cb:at + (h + 1) * cb] = (
                    r * (dxh - xh * jnp.mean(dxh * xh, axis=-1, keepdims=True))).astype(bf16)
                part = part + jnp.sum(dn_ * xh, axis=0, keepdims=True)

            @pl.when(first)
            def _(dg_ref=dg_ref, part=part):
                dg_ref[...] = part

            @pl.when(jnp.logical_not(first))
            def _(dg_ref=dg_ref, part=part):
                dg_ref[...] += part

    blk = pl.BlockSpec((rt, SB_W), lambda i: (i, 0))
    vec = pl.BlockSpec((1, cb), lambda i: (0, 0))
    return pl.pallas_call(
        body, name="sb_prep_bwd", grid=(tp // rt,),
        in_specs=[pl.BlockSpec((rt, SB_W), lambda i: (i, C_SQ)), pl.BlockSpec((rt, SB_W), lambda i: (i, C_SK)),
                  vec, vec, blk, blk, blk, pl.BlockSpec(memory_space=pl.ANY)],
        out_specs=[pl.BlockSpec((rt, 3 * SB_W), lambda i: (i, C_SQ * 1024 // (3 * SB_W))), vec, vec],
        out_shape=[jax.ShapeDtypeStruct(into.shape, into.dtype)] + [jax.ShapeDtypeStruct((1, cb), f32)] * 2,
        input_output_aliases={7: 0}, compiler_params=_params(1),
    )(proj, proj, gq, gk, dqs, dks, dvs, into)


def _conv_taps(ext, rt):
    taps = []
    for k in range(DN_CONV):
        s = DN_CONV - 1 - k
        taps.append((pltpu.roll(ext, s, axis=0) if s else ext)[SUB:SUB + rt])
    return taps


def _conv_act(taps, w, l2):
    y = taps[0] * w[0:1]
    for k in range(1, DN_CONV):
        y = y + taps[k] * w[k:k + 1]
    s = _sigmoid(y)
    a = y * s
    if l2:
        n = lax.rsqrt(jnp.sum(a * a, axis=-1, keepdims=True) + L2_EPS)
        return y, s, a, n
    return y, s, a, None


def conv_fwd(proj, w8, col_blk, ncb, l2, name):
    tp = proj.shape[0]
    rt = _row_tile(tp)
    hb = rt // SUB
    cw = CONV_W
    cb0 = col_blk * LANE // cw

    def body(x_ref, h_ref, w_ref, o_ref):
        i = pl.program_id(1)
        first = (i > 0).astype(f32)
        for s in range(cw // LANE):
            sl = slice(s * LANE, (s + 1) * LANE)
            ext = jnp.concatenate([h_ref[:, sl] * first, x_ref[:, sl]], axis=0)
            _, _, a, n = _conv_act(_conv_taps(ext, rt), w_ref[:, sl], l2)
            o_ref[:, sl] = a * n if l2 else a

    return pl.pallas_call(
        body, name=name, grid=(ncb * LANE // cw, tp // rt),
        in_specs=[pl.BlockSpec((rt, cw), lambda j, i: (i, j + cb0)),
                  pl.BlockSpec((SUB, cw), lambda j, i: (jnp.maximum(i * hb - 1, 0), j + cb0)),
                  pl.BlockSpec((SUB, cw), lambda j, i: (0, j))],
        out_specs=pl.BlockSpec((rt, cw), lambda j, i: (i, j)),
        out_shape=jax.ShapeDtypeStruct((tp, ncb * LANE), f32), compiler_params=_params(2),
    )(proj, proj, w8)


def conv_bwd_act(proj, w8, dout, col_blk, ncb, l2, name):
    tp = proj.shape[0]
    rt = _row_tile(tp)
    hb = rt // SUB
    cw = CONV_W
    cb0 = col_blk * LANE // cw

    def body(x_ref, h_ref, w_ref, d_ref, dy_ref, dw_ref):
        i = pl.program_id(1)
        first = (i > 0).astype(f32)
        rows = lax.broadcasted_iota(jnp.int32, (SUB, LANE), 0)
        for s in range(cw // LANE):
            sl = slice(s * LANE, (s + 1) * LANE)
            ext = jnp.concatenate([h_ref[:, sl] * first, x_ref[:, sl]], axis=0)
            taps = _conv_taps(ext, rt)
            y, sg, a, n = _conv_act(taps, w_ref[:, sl], l2)
            da = d_ref[:, sl]
            if l2:
                out = a * n
                da = n * (da - out * jnp.sum(da * out, axis=-1, keepdims=True))
            dy = da * sg * (1.0 + y * (1.0 - sg))
            dy_ref[:, sl] = dy
            part = jnp.zeros((SUB, LANE), f32)
            for k in range(DN_CONV):
                part = part + jnp.where(rows == k, jnp.sum(taps[k] * dy, axis=0, keepdims=True), 0.0)

            @pl.when(i == 0)
            def _(sl=sl, part=part):
                dw_ref[:, sl] = part

            @pl.when(i > 0)
            def _(sl=sl, part=part):
                dw_ref[:, sl] += part

    return pl.pallas_call(
        body, name=name, grid=(ncb * LANE // cw, tp // rt),
        in_specs=[pl.BlockSpec((rt, cw), lambda j, i: (i, j + cb0)),
                  pl.BlockSpec((SUB, cw), lambda j, i: (jnp.maximum(i * hb - 1, 0), j + cb0)),
                  pl.BlockSpec((SUB, cw), lambda j, i: (0, j)),
                  pl.BlockSpec((rt, cw), lambda j, i: (i, j))],
        out_specs=[pl.BlockSpec((rt, cw), lambda j, i: (i, j)), pl.BlockSpec((SUB, cw), lambda j, i: (0, j))],
        out_shape=[jax.ShapeDtypeStruct((tp, ncb * LANE), f32), jax.ShapeDtypeStruct((SUB, ncb * LANE), f32)],
        compiler_params=_params(2),
    )(proj, proj, w8, dout)


def conv_bwd_in(dy, w8, into, col_blk, name):
    tp, cols = dy.shape
    rt = _row_tile(tp)
    hb = rt // SUB
    nr = tp // rt
    last8 = tp // SUB - 1
    cw = CONV_W
    cb0 = col_blk * LANE // cw

    def body(d_ref, h_ref, w_ref, into_ref, o_ref):
        i = pl.program_id(1)
        last = (i < nr - 1).astype(f32)
        for c0 in range(cw // LANE):
            sl = slice(c0 * LANE, (c0 + 1) * LANE)
            ext = jnp.concatenate([d_ref[:, sl], h_ref[:, sl] * last], axis=0)
            w = w_ref[:, sl]
            acc = None
            for k in range(DN_CONV):
                s = DN_CONV - 1 - k
                sh = (pltpu.roll(ext, rt + SUB - s, axis=0) if s else ext)[0:rt]
                term = sh * w[k:k + 1]
                acc = term if acc is None else acc + term
            o_ref[:, sl] = acc.astype(bf16)

    return pl.pallas_call(
        body, name=name, grid=(cols // cw, nr),
        in_specs=[pl.BlockSpec((rt, cw), lambda j, i: (i, j)),
                  pl.BlockSpec((SUB, cw), lambda j, i: (jnp.minimum((i + 1) * hb, last8), j)),
                  pl.BlockSpec((SUB, cw), lambda j, i: (0, j)), pl.BlockSpec(memory_space=pl.ANY)],
        out_specs=pl.BlockSpec((rt, cw), lambda j, i: (i, j + cb0)),
        out_shape=jax.ShapeDtypeStruct(into.shape, into.dtype), input_output_aliases={3: 0},
        compiler_params=_params(2),
    )(dy, dy, w8, into)


def _ab_common(p, al, dtb, r0):
    rows = r0 + lax.broadcasted_iota(jnp.int32, p.shape, 0)
    mask = (rows >= P0).astype(f32)
    xx = p + dtb
    sp = jnp.maximum(xx, 0.0) + _log1p_small(jnp.exp(-jnp.abs(xx)))
    ea = jnp.exp(al)
    g = -ea * sp * mask
    beta = _sigmoid(p) * mask
    return g, beta, _sigmoid(xx), ea, mask


def _chunk_tri(rt, later):
    r = lax.broadcasted_iota(jnp.int32, (rt, rt), 0)
    c = lax.broadcasted_iota(jnp.int32, (rt, rt), 1)
    shift = CHUNK.bit_length() - 1
    same = jnp.right_shift(r, shift) == jnp.right_shift(c, shift)
    return jnp.logical_and(same, c >= r if later else c <= r).astype(f32)


def ab_fwd(pab, al, dtb):
    tp = pab.shape[0]
    rt = _row_tile(tp)
    assert rt % CHUNK == 0

    def body(p_ref, al_ref, dt_ref, g_ref, b_ref):
        i = pl.program_id(0)
        g, beta, _, _, _ = _ab_common(p_ref[...], al_ref[...], dt_ref[...], i * rt)
        gam = _hdot(_chunk_tri(rt, False), g)
        for h in range(HEADS):
            g_ref[h] = jnp.broadcast_to(gam[:, h:h + 1], (rt, LANE))
            b_ref[h] = jnp.broadcast_to(beta[:, HEADS + h:HEADS + h + 1], (rt, LANE))

    vec = pl.BlockSpec((1, LANE), lambda i: (0, 0))
    out = pl.BlockSpec((HEADS, rt, LANE), lambda i: (0, i, 0))
    return pl.pallas_call(
        body, name="ab_fwd", grid=(tp // rt,), in_specs=[pl.BlockSpec((rt, LANE), lambda i: (i, 0)), vec, vec],
        out_specs=[out, out], out_shape=[jax.ShapeDtypeStruct((HEADS, tp, LANE), f32)] * 2, compiler_params=_params(1),
    )(pab, al, dtb)


def ab_bwd(pab, al, dtb, dg, db):
    tp = pab.shape[0]
    rt = _row_tile(tp)

    def body(p_ref, al_ref, dt_ref, dg_ref, db_ref, dp_ref, dal_ref, ddt_ref):
        i = pl.program_id(0)
        g, beta, sx, ea, mask = _ab_common(p_ref[...], al_ref[...], dt_ref[...], i * rt)
        lanes = lax.broadcasted_iota(jnp.int32, (rt, LANE), 1)
        dgl = jnp.zeros((rt, LANE), f32)
        dbl = jnp.zeros((rt, LANE), f32)
        for h in range(HEADS):
            dgl = dgl + jnp.where(lanes == h, dg_ref[h], 0.0)
            dbl = dbl + jnp.where(lanes == HEADS + h, db_ref[h], 0.0)
        dgl = _hdot(_chunk_tri(rt, True), dgl)
        dxx = dgl * (-ea) * sx * mask
        dp_ref[...] = (dxx + dbl * beta * (1.0 - beta)).astype(bf16)
        pal = jnp.sum(dgl * g, axis=0, keepdims=True)
        pdt = jnp.sum(dxx, axis=0, keepdims=True)

        @pl.when(i == 0)
        def _():
            dal_ref[...] = pal
            ddt_ref[...] = pdt

        @pl.when(i > 0)
        def _():
            dal_ref[...] += pal
            ddt_ref[...] += pdt

    vec = pl.BlockSpec((1, LANE), lambda i: (0, 0))
    row = pl.BlockSpec((rt, LANE), lambda i: (i, 0))
    big = pl.BlockSpec((HEADS, rt, LANE), lambda i: (0, i, 0))
    return pl.pallas_call(
        body, name="ab_bwd", grid=(tp // rt,), in_specs=[row, vec, vec, big, big], out_specs=[row, vec, vec],
        out_shape=[jax.ShapeDtypeStruct((tp, LANE), bf16), jax.ShapeDtypeStruct((1, LANE), f32),
                   jax.ShapeDtypeStruct((1, LANE), f32)],
        compiler_params=_params(1),
    )(pab, al, dtb, dg, db)


class _Chunk:
    pass


def _gdn_chunk(q, k, v, gcol, bcol, grow8):
    C = CHUNK
    R = range(len(q))
    X = _Chunk()
    ri = lax.broadcasted_iota(jnp.int32, (C, C), 0)
    ci = lax.broadcasted_iota(jnp.int32, (C, C), 1)
    eye = (ri == ci).astype(f32)
    gam = list(gcol)
    gam_row = [grow8[h][0:1, 0:C] for h in R]
    X.ri, X.ci = ri, ci
    X.Dm = [jnp.where(ri >= ci, jnp.exp(jnp.minimum(gam[h][:, 0:C] - gam_row[h], 0.0)), 0.0) for h in R]
    X.eg = [jnp.exp(gam[h]) for h in R]
    gl = [gam[h][C - 1:C, :] for h in R]
    X.egl = [jnp.exp(gl[h]) for h in R]
    X.kdec = [jnp.exp(gl[h] - gam[h]) for h in R]
    X.qs = [q[h] * (DN_DK ** -0.5) for h in R]
    X.kb = [k[h] * bcol[h] for h in R]
    kk = [_dot_nt(X.kb[h], k[h]) for h in R]
    qk = [_dot_nt(X.qs[h], k[h]) for h in R]
    X.A = [jnp.where(ri > ci, kk[h] * X.Dm[h], 0.0) for h in R]
    T = [eye - X.A[h] for h in R]
    P = list(X.A)
    for _ in range(5):
        P = [_hdot(P[h], P[h]) for h in R]
        T = [T[h] + _hdot(T[h], P[h]) for h in R]
    X.T = T
    X.b2 = [jnp.concatenate([bcol[h], bcol[h]], axis=-1) for h in R]
    X.u = [_hdot(T[h], v[h] * X.b2[h]) for h in R]
    X.w = [_hdot(T[h], X.kb[h] * X.eg[h]) for h in R]
    X.attn = [qk[h] * X.Dm[h] for h in R]
    X.qg = [X.qs[h] * X.eg[h] for h in R]
    X.kg = [k[h] * X.kdec[h] for h in R]
    return X


def gdn_fwd(q, k, v, gc, bc, grow):
    tp = q.shape[0]
    nc = tp // CHUNK
    hb = GDN_HEADS_PER_STEP

    def body(q_ref, k_ref, v_ref, gc_ref, bc_ref, gr_ref, o_ref, ss_ref, S_ref):
        c = pl.program_id(1)

        @pl.when(c == 0)
        def _():
            S_ref[...] = jnp.zeros_like(S_ref)

        R = range(hb)
        qc = [slice(h * DN_DK, (h + 1) * DN_DK) for h in R]
        vc = [slice(h * DN_DV, (h + 1) * DN_DV) for h in R]
        X = _gdn_chunk([q_ref[:, qc[h]] for h in R], [k_ref[:, qc[h]] for h in R], [v_ref[:, vc[h]] for h in R],
                       [gc_ref[h] for h in R], [bc_ref[h] for h in R], [gr_ref[h] for h in R])
        S = [S_ref[h] for h in R]
        for h in R:
            ss_ref[h, 0] = S[h]
        wS = [_dot(X.w[h], S[h]) for h in R]
        qS = [_dot(X.qg[h], S[h]) for h in R]
        vn = [X.u[h] - wS[h] for h in R]
        av = [_dot(X.attn[h], vn[h]) for h in R]
        kv = [_dot_tn(X.kg[h], vn[h]) for h in R]
        for h in R:
            o_ref[:, vc[h]] = qS[h] + av[h]
            S_ref[h] = S[h] * X.egl[h][:, 0:1] + kv[h]

    qk = pl.BlockSpec((CHUNK, hb * DN_DK), lambda g, c: (c, g))
    vv = pl.BlockSpec((CHUNK, hb * DN_DV), lambda g, c: (c, g))
    col = pl.BlockSpec((hb, CHUNK, LANE), lambda g, c: (g, c, 0))
    row = pl.BlockSpec((hb, SUB, LANE), lambda g, c: (g, c, 0))
    return pl.pallas_call(
        body, name="gdn_fwd", grid=(HEADS // hb, nc), in_specs=[qk, qk, vv, col, col, row],
        out_specs=[vv, pl.BlockSpec((hb, 1, DN_DK, DN_DV), lambda g, c: (g, c, 0, 0))],
        out_shape=[jax.ShapeDtypeStruct((tp, DN_V), f32), jax.ShapeDtypeStruct((HEADS, nc, DN_DK, DN_DV), f32)],
        scratch_shapes=[pltpu.VMEM((hb, DN_DK, DN_DV), f32)], compiler_params=_params(2),
    )(q, k, v, gc, bc, grow)


def gdn_bwd(q, k, v, gc, bc, grow, states, do, rider=None):
    tp = q.shape[0]
    nc = tp // CHUNK
    C = CHUNK
    hb = GDN_HEADS_PER_STEP
    grid = (HEADS // hb, nc)
    split, ride_first, ride_last = _ride(rider, 8, 5, grid)

    def body(*refs):
        ((q_ref, k_ref, v_ref, gc_ref, bc_ref, gr_ref, ss_ref, do_ref), (dq_ref, dk_ref, dv_ref, dg_ref, db_ref),
         (rin, rout, rest)) = split(refs)
        dS_ref, ride = rest[0], (rin, rout, rest[1:])
        ride_first(ride)
        c = pl.program_id(1)

        @pl.when(c == 0)
        def _():
            dS_ref[...] = jnp.zeros_like(dS_ref)

        R = range(hb)
        qc = [slice(h * DN_DK, (h + 1) * DN_DK) for h in R]
        vc = [slice(h * DN_DV, (h + 1) * DN_DV) for h in R]
        k_ = [k_ref[:, qc[h]] for h in R]
        v_ = [v_ref[:, vc[h]] for h in R]
        bcol = [bc_ref[h] for h in R]
        X = _gdn_chunk([q_ref[:, qc[h]] for h in R], k_, v_, [gc_ref[h] for h in R], bcol, [gr_ref[h] for h in R])
        ri, ci = X.ri, X.ci
        S = [ss_ref[h, 0] for h in R]
        do_ = [do_ref[:, vc[h]] for h in R]
        dSn = [dS_ref[h] for h in R]
        wS = [_dot(X.w[h], S[h]) for h in R]
        ado = [_dot_tn(X.attn[h], do_[h]) for h in R]
        kdS = [_dot(X.kg[h], dSn[h]) for h in R]
        d_qg = [_dot_nt(do_[h], S[h]) for h in R]
        qdo = [_dot_tn(X.qg[h], do_[h]) for h in R]
        vn = [X.u[h] - wS[h] for h in R]
        d_vn = [ado[h] + kdS[h] for h in R]
        dovn = [_dot_nt(do_[h], vn[h]) for h in R]
        d_kg = [_dot_nt(vn[h], dSn[h]) for h in R]
        wdv = [_dot_tn(X.w[h], d_vn[h]) for h in R]
        dw = [-_dot_nt(d_vn[h], S[h]) for h in R]
        for h in R:
            dS_ref[h] = qdo[h] + X.egl[h][:, 0:1] * dSn[h] - wdv[h]
        dattn = [jnp.where(ri >= ci, dovn[h], 0.0) for h in R]
        dRu = [_hdot_tn(X.T[h], d_vn[h]) for h in R]
        dRw = [_hdot_tn(X.T[h], dw[h]) for h in R]
        dAu = [_hdot_nt(dRu[h], X.u[h]) for h in R]
        dAw = [_hdot_nt(dRw[h], X.w[h]) for h in R]
        dA = [jnp.where(ri > ci, -(dAu[h] + dAw[h]), 0.0) for h in R]
        dKK = [dA[h] * X.Dm[h] for h in R]
        dQK = [dattn[h] * X.Dm[h] for h in R]
        E = [dA[h] * X.A[h] + dattn[h] * X.attn[h] for h in R]
        dkb = [_dot(dKK[h], k_[h]) + dRw[h] * X.eg[h] for h in R]
        dk1 = [_dot_tn(dKK[h], X.kb[h]) for h in R]
        dqs = [_dot(dQK[h], k_[h]) + d_qg[h] * X.eg[h] for h in R]
        dk2 = [_dot_tn(dQK[h], X.qs[h]) for h in R]
        ones = jnp.ones((C, LANE), f32)
        colE = [_hdot_tn(E[h], ones) for h in R]
        rows = lax.broadcasted_iota(jnp.int32, (C, LANE), 0)
        dgam = []
        for h in R:
            t = d_kg[h] * X.kg[h]
            dgl = _allsum(t) + X.egl[h][:, 0:1] * _allsum(S[h] * dSn[h])
            g = (_rowsum(E[h]) - colE[h] + _rowsum(dRw[h] * (X.kb[h] * X.eg[h])) + _rowsum(d_qg[h] * X.qg[h])
                 - _rowsum(t))
            dgam.append(g + jnp.where(rows == C - 1, dgl, 0.0))
        for h in R:
            dv_ref[:, vc[h]] = dRu[h] * X.b2[h]
            dbeta = _rowsum(dRu[h] * v_[h]) + _rowsum(dkb[h] * k_[h])
            dq_ref[:, qc[h]] = dqs[h] * (DN_DK ** -0.5)
            dk_ref[:, qc[h]] = dk1[h] + dk2[h] + dkb[h] * bcol[h] + d_kg[h] * X.kdec[h]
            dg_ref[h] = dgam[h]
            db_ref[h] = jnp.broadcast_to(dbeta, (C, LANE))
        ride_last(ride)

    rc = lambda c: nc - 1 - c
    qk = pl.BlockSpec((CHUNK, hb * DN_DK), lambda g, c: (rc(c), g))
    vv = pl.BlockSpec((CHUNK, hb * DN_DV), lambda g, c: (rc(c), g))
    col = pl.BlockSpec((hb, CHUNK, LANE), lambda g, c: (g, rc(c), 0))
    row = pl.BlockSpec((hb, SUB, LANE), lambda g, c: (g, rc(c), 0))
    st = pl.BlockSpec((hb, 1, DN_DK, DN_DV), lambda g, c: (g, rc(c), 0, 0))
    r_ins = rider.ins if rider else []
    r_outs = rider.out_shapes if rider else []
    res = pl.pallas_call(
        body, name="gdn_bwd", grid=grid, in_specs=[qk, qk, vv, col, col, row, st, vv] + [_ANY] * len(r_ins),
        out_specs=[qk, qk, vv, col, col] + [_ANY] * len(r_outs),
        out_shape=[jax.ShapeDtypeStruct((tp, DN_QK), f32), jax.ShapeDtypeStruct((tp, DN_QK), f32),
                   jax.ShapeDtypeStruct((tp, DN_V), f32), jax.ShapeDtypeStruct((HEADS, tp, LANE), f32),
                   jax.ShapeDtypeStruct((HEADS, tp, LANE), f32)] + list(r_outs),
        scratch_shapes=[pltpu.VMEM((hb, DN_DK, DN_DV), f32)] + (rider.scratch if rider else []),
        compiler_params=_params(2),
    )(q, k, v, gc, bc, grow, states, do, *r_ins)
    return res[:5], res[5:]


def _cumsum_after(x, nb, us, pieces=2):
    B, n = SB_BLOCK, x.shape[0]
    hi = x.astype(bf16)
    parts = (hi, (x - hi.astype(f32)).astype(bf16)) if pieces == 2 else (hi,)
    rows = [p[:, b * B:(b + 1) * B] for p in parts for b in range(nb)]
    r = jnp.dot(jnp.concatenate(rows, axis=0), us, preferred_element_type=f32)
    out = [r[b * n:(b + 1) * n] for b in range(nb)]
    if pieces == 2:
        out = [out[b] + r[(nb + b) * n:(nb + b + 1) * n] for b in range(nb)]
    return out[0] if nb == 1 else jnp.concatenate(out, axis=1)


def _later_blocks(x, nb, carry):
    B = SB_BLOCK
    tot = [_rowsum(x[:, b * B:(b + 1) * B]) for b in range(nb)]
    offs = [None] * nb
    run = carry
    for b in range(nb - 1, -1, -1):
        offs[b] = jnp.broadcast_to(run, (x.shape[0], B))
        run = run + tot[b]
    return (offs[0] if nb == 1 else jnp.concatenate(offs, axis=1)), run


def _sb_group(i, t):
    top = (i + 1) * (SB_QB // SB_BLOCK) - 1 - SB_GROUP * t
    jlo = jnp.maximum(top - SB_GROUP + 1, 0)
    rows = pl.ds(pl.multiple_of(jlo * SB_BLOCK, SB_BLOCK), SB_GROUP * SB_BLOCK)
    return jlo, rows, (top + 1) * SB_BLOCK


def _sb_weights(q, kcat, i, jlo, kend, cs, us, masked):
    B, nb = SB_BLOCK, SB_GROUP
    R = range(len(q))
    z = [_dot_nt(q[h], kcat[h]) * (SB_DH ** -0.5) for h in R]
    e = [jnp.exp(-jnp.abs(z[h])) for h in R]
    l1p = [jnp.log(1.0 + e[h]) for h in R]
    lsp = [jnp.minimum(z[h], 0.0) - l1p[h] for h in R]
    lk = [lsp[h] - z[h] for h in R]
    vis = None
    if masked:
        qpos = i * SB_QB + lax.broadcasted_iota(jnp.int32, (SB_QB, nb * B), 0)
        kpos = jlo * B + lax.broadcasted_iota(jnp.int32, (SB_QB, nb * B), 1)
        vis = jnp.logical_and(kpos < jnp.minimum(qpos, kend), kpos >= P0)
        lk = [jnp.where(vis, lk[h], 0.0) for h in R]
    later = [_later_blocks(lk[h], nb, cs[h]) for h in R]
    cum = [_cumsum_after(lk[h], nb, us) for h in R]
    w = [jnp.exp(lsp[h] + cum[h] + later[h][0]) for h in R]
    if masked:
        w = [jnp.where(vis, w[h], 0.0) for h in R]
    return lsp, vis, w, [later[h][1] for h in R]


def _sb_loop(i, step, carry):
    trips = ((i + 1) * (SB_QB // SB_BLOCK) - 1 + SB_GROUP) // SB_GROUP
    carry = step(True)(0, carry)
    carry = lax.fori_loop(1, trips - 1, step(False), carry)
    return lax.fori_loop(jnp.maximum(trips - 1, 1), trips, step(True), carry)


def _ride(rider, n_in, n_out, grid):
    n_rin = len(rider.ins) if rider else 0
    n_rout = len(rider.out_shapes) if rider else 0

    def split(refs):
        ins, rin = refs[:n_in], refs[n_in:n_in + n_rin]
        outs = refs[n_in + n_rin:n_in + n_rin + n_out]
        rout = refs[n_in + n_rin + n_out:n_in + n_rin + n_out + n_rout]
        return ins, outs, (rin, rout, refs[n_in + n_rin + n_out + n_rout:])

    def at(step, fn, r):
        if rider is None:
            return
        cond = None
        for a, g in enumerate(grid):
            c = pl.program_id(a) == (g - 1 if step == "last" else 0)
            cond = c if cond is None else jnp.logical_and(cond, c)

        @pl.when(cond)
        def _():
            fn(*r)

    first = lambda r: at("first", rider.start if rider else None, r)
    last = lambda r: at("last", rider.finish if rider else None, r)
    return split, first, last


def sb_fwd(qs, ks, vs, rider=None):
    tp = qs.shape[0]
    nq = tp // SB_QB
    B, G, hb, QB = SB_BLOCK, SB_GROUP, SB_FWD_HEADS_PER_STEP, SB_QB
    assert tp >= G * B and tp % QB == 0 and QB % B == 0 and G * B >= QB
    grid = (HEADS // hb, nq)
    split, ride_first, ride_last = _ride(rider, 3, 2, grid)

    def body(*refs):
        (q_ref, k_ref, v_ref), (o_ref, ob_ref), ride = split(refs)
        ride_first(ride)
        i = pl.program_id(1)
        R = range(hb)
        hs = [slice(h * SB_DH, (h + 1) * SB_DH) for h in R]
        q = [q_ref[:, hs[h]] for h in R]
        us = (lax.broadcasted_iota(jnp.int32, (B, B), 0) > lax.broadcasted_iota(jnp.int32, (B, B), 1)).astype(bf16)

        def make_step(masked):
            def step(t, carry):
                acc, cs = carry
                jlo, rows, kend = _sb_group(i, t)
                _, _, w, cs = _sb_weights(q, [k_ref[rows, hs[h]] for h in R], i, jlo, kend, cs, us, masked)
                pv = [_dot(w[h], v_ref[rows, hs[h]]) for h in R]
                return tuple(acc[h] + pv[h] for h in R), tuple(cs)
            return step

        carry = (tuple(jnp.zeros((QB, SB_DH), f32) for _ in R), tuple(jnp.zeros((QB, 1), f32) for _ in R))
        acc, _ = _sb_loop(i, make_step, carry)
        for h in R:
            o_ref[:, hs[h]] = acc[h]
            ob_ref[:, hs[h]] = acc[h].astype(bf16)
        ride_last(ride)

    blk = pl.BlockSpec((QB, hb * SB_DH), lambda g, i: (i, g))
    full = pl.BlockSpec((tp, hb * SB_DH), lambda g, i: (0, g))
    r_ins = rider.ins if rider else []
    r_outs = rider.out_shapes if rider else []
    res = pl.pallas_call(
        body, name="sb_fwd", grid=grid, in_specs=[blk, full, full] + [_ANY] * len(r_ins),
        out_specs=[blk, blk] + [_ANY] * len(r_outs),
        out_shape=[jax.ShapeDtypeStruct((tp, SB_W), f32), jax.ShapeDtypeStruct((tp, SB_W), bf16)] + list(r_outs),
        scratch_shapes=rider.scratch if rider else [], compiler_params=_params(2),
    )(qs, ks, vs, *r_ins)
    return res[0], res[1], res[2:]


def sb_bwd(qs, ks, vs, o, do, rider=None):
    tp = qs.shape[0]
    nq = tp // SB_QB
    B, G, hb, QB = SB_BLOCK, SB_GROUP, SB_HEADS_PER_STEP, SB_QB
    assert tp >= G * B and tp % QB == 0 and QB % B == 0 and G * B >= QB
    grid = (HEADS // hb, nq)
    split, ride_first, ride_last = _ride(rider, 5, 3, grid)

    def body(*refs):
        (q_ref, k_ref, v_ref, o_ref, do_ref), (dq_ref, dk_ref, dv_ref), ride = split(refs)
        ride_first(ride)
        i = pl.program_id(1)

        @pl.when(i == 0)
        def _():
            dk_ref[...] = jnp.zeros_like(dk_ref)
            dv_ref[...] = jnp.zeros_like(dv_ref)

        R = range(hb)
        hs = [slice(h * SB_DH, (h + 1) * SB_DH) for h in R]
        q = [q_ref[:, hs[h]] for h in R]
        dob = [do_ref[:, hs[h]].astype(bf16) for h in R]
        et = [_rowsum(dob[h].astype(f32) * o_ref[:, hs[h]]) for h in R]
        us = (lax.broadcasted_iota(jnp.int32, (B, B), 0) > lax.broadcasted_iota(jnp.int32, (B, B), 1)).astype(bf16)

        def make_step(masked):
            def step(t, carry):
                dq, cs, ce = carry
                jlo, rows, kend = _sb_group(i, t)
                kcat = [k_ref[rows, hs[h]] for h in R]
                dwv = [_dot_nt(dob[h], v_ref[rows, hs[h]]) for h in R]
                lsp, vis, w, cs = _sb_weights(q, kcat, i, jlo, kend, cs, us, masked)
                wb = [w[h].astype(bf16) for h in R]
                ee = [dwv[h] * wb[h].astype(f32) for h in R]
                later = [_later_blocks(ee[h], G, ce[h]) for h in R]
                cum = [_cumsum_after(ee[h], G, us) for h in R]
                dz = []
                for h in R:
                    d = ee[h] - jnp.exp(lsp[h]) * (et[h] - (cum[h] + later[h][0]))
                    if masked:
                        d = jnp.where(vis, d, 0.0)
                    dz.append(d.astype(bf16))
                dkj = [_dot_tn(dz[h], q[h]) for h in R]
                dvj = [_dot_tn(wb[h], dob[h]) for h in R]
                dqj = [_dot(dz[h], kcat[h]) for h in R]
                for h in R:
                    dk_ref[rows, hs[h]] += dkj[h]
                    dv_ref[rows, hs[h]] += dvj[h]
                return tuple(dq[h] + dqj[h] for h in R), tuple(cs), tuple(later[h][1] for h in R)
            return step

        z0 = tuple(jnp.zeros((QB, 1), f32) for _ in R)
        dq, _, _ = _sb_loop(i, make_step, (tuple(jnp.zeros((QB, SB_DH), f32) for _ in R), z0, z0))
        for h in R:
            dq_ref[:, hs[h]] = dq[h] * (SB_DH ** -0.5)
        ride_last(ride)

    blk = pl.BlockSpec((QB, hb * SB_DH), lambda g, i: (i, g))
    full = pl.BlockSpec((tp, hb * SB_DH), lambda g, i: (0, g))
    r_ins = rider.ins if rider else []
    r_outs = rider.out_shapes if rider else []
    res = pl.pallas_call(
        body, name="sb_bwd", grid=grid, in_specs=[blk, full, full, blk, blk] + [_ANY] * len(r_ins),
        out_specs=[blk, full, full] + [_ANY] * len(r_outs),
        out_shape=[jax.ShapeDtypeStruct((tp, SB_W), f32)] * 3 + list(r_outs),
        scratch_shapes=rider.scratch if rider else [], compiler_params=_params(2),
    )(qs, ks, vs, o, do, *r_ins)
    return res[:3], res[3:]


def adamw(w, g, m, v, name):
    r, c = w.shape
    rt = _tile(r, 128, SUB) if r % SUB == 0 else r
    blk = pl.BlockSpec((rt, c), lambda i: (i, 0))
    c1 =1.0 - ADAM_B1 ** ADAM_STEP
    c2 = 1.0 - ADAM_B2 ** ADAM_STEP

    def body(w_ref, g_ref, m_ref, v_ref, d_ref, mo_ref, vo_ref):
        g_ = g_ref[...]
        m_ = ADAM_B1 * m_ref[...] + (1.0 - ADAM_B1) * g_
        v_ = ADAM_B2 * v_ref[...] + (1.0 - ADAM_B2) * (g_ * g_)
        mo_ref[...] = m_
        vo_ref[...] = v_
        d_ref[...] = -ADAM_LR * ((m_ / c1) / (jnp.sqrt(v_ / c2) + ADAM_EPS) + ADAM_WD * w_ref[...])

    return pl.pallas_call(
        body, name=name, grid=(r // rt,), in_specs=[blk] * 4, out_specs=[blk] * 3,
        out_shape=[jax.ShapeDtypeStruct((r, c), f32)] * 3, compiler_params=_params(1),
    )(w, g, m, v)


def sum_slots(x, name):
    n, r, c = x.shape
    rt = _tile(r, 128, SUB) if r % SUB == 0 else r
    blk = pl.BlockSpec((n, rt, c), lambda i: (0, i, 0))

    def body(x_ref, o_ref):
        acc = x_ref[0].astype(f32)
        for s in range(1, n):
            acc = acc + x_ref[s].astype(f32)
        o_ref[...] = acc

    return pl.pallas_call(
        body, name=name, grid=(r // rt,), in_specs=[blk], out_specs=pl.BlockSpec((rt, c), lambda i: (i, 0)),
        out_shape=jax.ShapeDtypeStruct((r, c), f32), compiler_params=_params(1),
    )(x)


def add2(a, b, name, out_dtype=f32):
    n, r, c = a.shape
    rt = _tile(r, 64, SUB) if r % SUB == 0 else r
    blk = pl.BlockSpec((n, rt, c), lambda i: (0, i, 0))

    def body(a_ref, b_ref, o_ref):
        o_ref[...] = (a_ref[...] + b_ref[...]).astype(out_dtype)

    return pl.pallas_call(
        body, name=name, grid=(r // rt,), in_specs=[blk, blk], out_specs=blk,
        out_shape=jax.ShapeDtypeStruct((n, r, c), out_dtype), compiler_params=_params(1),
    )(a, b)


_ANY = pl.BlockSpec(memory_space=pl.ANY)
_MESH = pl.DeviceIdType.MESH


def _coords():
    return lax.axis_index("x"), lax.axis_index("y"), lax.axis_index("c")


def _chip_peer(x, y, r):
    return x ^ (r >> 1), y ^ (r & 1)


class _Exchange:
    def __init__(self, ins, out_shapes, scratch, start, finish):
        self.ins, self.out_shapes, self.scratch, self.start, self.finish = ins, out_shapes, scratch, start, finish

    def split(self, refs):
        n, m = len(self.ins), len(self.out_shapes)
        return refs[:n], refs[n:n + m], refs[n + m:]


def run_exchange(ex, name):
    def body(*refs):
        ins, outs, sems = ex.split(refs)
        ex.start(ins, outs, sems)
        ex.finish(ins, outs, sems)

    return pl.pallas_call(body, name=name, in_specs=[_ANY] * len(ex.ins), out_specs=[_ANY] * len(ex.out_shapes),
                          out_shape=ex.out_shapes, scratch_shapes=ex.scratch)(*ex.ins)


def gather_chips(big, small):
    nb, n = len(big), len(big) + len(small)
    shards = list(big) + list(small)
    kb = nb * (N_CHIPS - 1)
    k = n * (N_CHIPS - 1)

    def copies(src, dst, sems):
        send, recv, fsend, frecv = sems
        x, y, c = _coords()
        sib = (x, y, 1 - c)
        peers = [_chip_peer(x, y, r) for r in range(1, N_CHIPS)]

        def direct(t, j, slot):
            s = t * (N_CHIPS - 1) + j
            if t < nb:
                return pltpu.make_async_remote_copy(src[t].at[c], dst[t].at[slot, c], send.at[s], recv.at[s],
                                                    device_id=(*peers[j], c), device_id_type=_MESH)
            return pltpu.make_async_remote_copy(src[t], dst[t].at[slot], send.at[s], recv.at[s],
                                                device_id=(*peers[j], c), device_id_type=_MESH)

        def passed(t, j, half):
            s = t * (N_CHIPS - 1) + j
            px, py = peers[j]
            part = dst[t].at[2 * px + py, half]
            return pltpu.make_async_remote_copy(part, part, fsend.at[s], frecv.at[s], device_id=sib, device_id_type=_MESH)

        return direct, passed, peers, 2 * x + y, c

    def start(src, dst, sems):
        direct, _, _, me, _ = copies(src, dst, sems)
        for t in range(n):
            for j in range(N_CHIPS - 1):
                direct(t, j, me).start()

    def finish(src, dst, sems):
        direct, passed, peers, me, c = copies(src, dst, sems)
        fwd = []
        for t in range(nb):
            for j in range(N_CHIPS - 1):
                px, py = peers[j]
                direct(t, j, 2 * px + py).wait_recv()
                fwd.append(passed(t, j, c))
                fwd[-1].start()
        for t in range(nb, n):
            for j in range(N_CHIPS - 1):
                px, py = peers[j]
                direct(t, j, 2 * px + py).wait_recv()
        for t in range(nb):
            for j in range(N_CHIPS - 1):
                passed(t, j, 1 - c).wait_recv()
        for t in range(n):
            for j in range(N_CHIPS - 1):
                direct(t, j, me).wait_send()
        for cp in fwd:
            cp.wait_send()

    return _Exchange(shards, [jax.ShapeDtypeStruct((N_CHIPS,) + s.shape, s.dtype) for s in shards],
                     [pltpu.SemaphoreType.DMA((k,)), pltpu.SemaphoreType.DMA((k,)),
                      pltpu.SemaphoreType.DMA((max(kb, 1),)), pltpu.SemaphoreType.DMA((max(kb, 1),))], start, finish)


def sibling_swap(grads):
    pairs = [(t, o) for t, g in enumerate(grads) for o in range(g.shape[0])]
    k = len(pairs)

    def copies(src, dst, sems):
        send, recv = sems
        x, y, c = _coords()
        return [pltpu.make_async_remote_copy(src[t].at[o, 1 - c], dst[t].at[o], send.at[s], recv.at[s],
                                             device_id=(x, y, 1 - c), device_id_type=_MESH)
                for s, (t, o) in enumerate(pairs)]

    def start(src, dst, sems):
        for cp in copies(src, dst, sems):
            cp.start()

    def finish(src, dst, sems):
        cps = copies(src, dst, sems)
        for cp in cps:
            cp.wait_recv()
        for cp in cps:
            cp.wait_send()

    return _Exchange(list(grads), [jax.ShapeDtypeStruct((g.shape[0],) + g.shape[2:], g.dtype) for g in grads],
                     [pltpu.SemaphoreType.DMA((k,)), pltpu.SemaphoreType.DMA((k,))], start, finish)


def scatter_chips(parts):
    n = len(parts)
    k = n * (N_CHIPS - 1)

    def copy(src, dst, sems, t, r, landing):
        send, recv = sems
        x, y, c = _coords()
        me = 2 * x + y
        px, py = _chip_peer(x, y, r)
        peer = 2 * px + py
        s = t * (N_CHIPS - 1) + r - 1
        return pltpu.make_async_remote_copy(src[t].at[me if landing else peer], dst[t].at[peer if landing else me],
                                            send.at[s], recv.at[s], device_id=(px, py, c), device_id_type=_MESH)

    def start(src, dst, sems):
        for t in range(n):
            for r in range(1, N_CHIPS):
                copy(src, dst, sems, t, r, False).start()

    def finish(src, dst, sems):
        for t in range(n):
            for r in range(1, N_CHIPS):
                copy(src, dst, sems, t, r, True).wait_recv()
        for t in range(n):
            for r in range(1, N_CHIPS):
                copy(src, dst, sems, t, r, False).wait_send()

    return _Exchange(list(parts), [jax.ShapeDtypeStruct(p.shape, p.dtype) for p in parts],
                     [pltpu.SemaphoreType.DMA((k,)), pltpu.SemaphoreType.DMA((k,))], start, finish)


def sibling_send(halves, name):
    n = len(halves)

    def body(*refs):
        src, dst = refs[:n], refs[n:2 * n]
        send, recv = refs[2 * n:]
        x, y, c = _coords()
        cps = [pltpu.make_async_remote_copy(src[t], dst[t], send.at[t], recv.at[t],
                                            device_id=(x, y, 1 - c), device_id_type=_MESH) for t in range(n)]
        for cp in cps:
            cp.start()
        for cp in cps:
            cp.wait_recv()
        for cp in cps:
            cp.wait_send()

    return pl.pallas_call(
        body, name=name, in_specs=[_ANY] * n, out_specs=[_ANY] * n,
        out_shape=[jax.ShapeDtypeStruct(h.shape, h.dtype) for h in halves],
        scratch_shapes=[pltpu.SemaphoreType.DMA((n,)), pltpu.SemaphoreType.DMA((n,))],
    )(*halves)


def gather_all(block, name):
    def body(src, dst, send, recv, loc):
        x, y, c = _coords()
        me = 4 * x + 2 * y + c
        mine = pltpu.make_async_copy(src, dst.at[me], loc)
        mine.start()
        outs = []
        for r in range(1, N_DEV):
            peer = (x ^ (r >> 2), y ^ ((r >> 1) & 1), c ^ (r & 1))
            outs.append(pltpu.make_async_remote_copy(src, dst.at[me], send.at[r - 1], recv.at[r - 1],
                                                     device_id=peer, device_id_type=_MESH))
        for cp in outs:
            cp.start()
        for r in range(1, N_DEV):
            px, py, pc = x ^ (r >> 2), y ^ ((r >> 1) & 1), c ^ (r & 1)
            pltpu.make_async_remote_copy(src, dst.at[4 * px + 2 * py + pc], send.at[r - 1], recv.at[r - 1],
                                         device_id=(px, py, pc), device_id_type=_MESH).wait_recv()
        for cp in outs:
            cp.wait_send()
        mine.wait()

    return pl.pallas_call(
        body, name=name, in_specs=[_ANY], out_specs=_ANY,
        out_shape=jax.ShapeDtypeStruct((N_DEV,) + block.shape, block.dtype),
        scratch_shapes=[pltpu.SemaphoreType.DMA((N_DEV - 1,)), pltpu.SemaphoreType.DMA((N_DEV - 1,)),
                        pltpu.SemaphoreType.DMA(())],
    )(block)


def _pad_lanes(v, n=LANE):
    return jnp.pad(v, ((0, 0), (0, n - v.shape[1])))


def _w_in_pieces():
    cs = (PROJ_BIG + 2 * HEADS) // N_CHIPS
    ab_end = AB_COL + 2 * HEADS
    out = []
    for o in range(N_CHIPS):
        lo, hi = o * cs, (o + 1) * cs
        cand = [("big", lo, min(hi, AB_COL), 0), ("ab", max(lo, AB_COL), min(hi, ab_end), AB_COL),
                ("big", max(lo, ab_end), hi, 2 * HEADS)]
        out.append([(s, a - off, b - off) for s, a, b, off in cand if a < b])
    return out


def _split_w_in(w4):
    big, ab = [], []
    for o, pieces in enumerate(_w_in_pieces()):
        at = 0
        for s, a, b in pieces:
            (big if s == "big" else ab).append(w4[o][:, at:at + b - a])
            at += b - a
    return jnp.concatenate(big, axis=1), _pad_lanes(jnp.concatenate(ab, axis=1))


def _join_w_in(big, ab):
    src = {"big": big, "ab": ab}
    return jnp.stack([jnp.concatenate([src[s][:, a:b] for s, a, b in pieces], axis=1) for pieces in _w_in_pieces()])


def _conv_w8(w):
    return jnp.pad(w, ((0, SUB - DN_CONV), (0, 0)))


def _row_layout(gc, tp):
    nc = tp // CHUNK
    g = gc[:, :, 0].reshape(HEADS, nc, 1, CHUNK)
    g = jnp.broadcast_to(g, (HEADS, nc, SUB, CHUNK))
    return jnp.pad(g, ((0, 0), (0, 0), (0, 0), (0, LANE - CHUNK))).reshape(HEADS, nc * SUB, LANE)


def _step(x, meta, W, target, late_weights=None, early_swap=None, early_grads=None, last_grads=None):
    W = dict(W)
    seq = x.shape[0]
    tp = P0 + N_META + seq
    h0 = jnp.concatenate([jnp.zeros((P0, D_MODEL), f32), meta, x], axis=0)
    w_big, w_ab = _split_w_in(W["w_in"])
    cq8, ck8, cv8 = _conv_w8(W["conv_q"]), _conv_w8(W["conv_k"]), _conv_w8(W["conv_v"])
    al, dtb = _pad_lanes(W["dn_a_log"]), _pad_lanes(W["dn_dt_bias"])

    n1 = rms_fwd(h0, W["norm_mix_gain"], "rms1_fwd")
    proj = matmul(n1, w_big, "nn", "proj_fwd")
    pab = matmul(n1, w_ab, "nn", "pab_fwd")
    qn = conv_fwd(proj, cq8, C_DQ * 8, 8, True, "conv_q_fwd")
    kn = conv_fwd(proj, ck8, C_DK * 8, 8, True, "conv_k_fwd")
    va = conv_fwd(proj, cv8, C_DV * 8, 16, False, "conv_v_fwd")
    gc, bc = ab_fwd(pab, al, dtb)
    grow = _row_layout(gc, tp)
    o_dn, states = gdn_fwd(qn, kn, va, gc, bc, grow)
    on = dn_out_fwd(o_dn, proj, W["dn_out_norm_gain"])
    qs, ks, vs = sb_prep_fwd(proj, W["sb_q_norm_gain"], W["sb_k_norm_gain"])
    o_sb, o_sb16, arrived = sb_fwd(qs, ks, vs, rider=late_weights[0] if late_weights else None)
    if late_weights:
        W.update(late_weights[1](arrived))
    ydn = matmul(on, W["w_branch_dn"], "nn", "ydn_fwd")
    ysb = matmul(o_sb16, W["w_branch_sb"], "nn", "ysb_fwd")
    merged = merge_fwd(proj, ydn, ysb)
    h1 = matmul(merged, W["w_out"], "nn", "wout_fwd", residual=h0)
    n2 = rms_fwd(h1, W["norm_ffn_gain"], "rms2_fwd")
    u = matmul(n2, W["w_ffn_in"], "nn", "ffn_in_fwd", tn_t=512)
    act = swiglu_fwd(u)
    y = matmul(act, W["w_ffn_out"], "nn", "ffn_out_fwd", residual=h1)
    dy, dy16, loss = loss_head(y, target)

    G = {}
    dact = matmul(dy16, W["w_ffn_out"], "nt", "ffn_out_dx", tn_t=1408)
    G["w_ffn_out"] = matmul(act, dy16, "tn", "ffn_out_dw", tm_t=1408)
    dgate, dup = swiglu_bwd(u, dact)
    du = jnp.concatenate([dgate, dup], axis=1)
    dn2 = matmul(du, W["w_ffn_in"], "nt", "ffn_in_dx", tk_t=512)
    G["w_ffn_in"] = matmul(n2, du, "tn", "ffn_in_dw", tn_t=512)
    dh1, dh1_16, G["norm_ffn_gain"] = rms_bwd(h1, W["norm_ffn_gain"], dn2, dy, "rms2_bwd")
    dmerged = matmul(dh1_16, W["w_out"], "nt", "wout_dx")
    G["w_out"] = matmul(merged, dh1_16, "tn", "wout_dw")
    dyd, dys, d_gates = merge_bwd(proj, ydn, ysb, dmerged)
    don = matmul(dyd, W["w_branch_dn"], "nt", "ydn_dx")
    G["w_branch_dn"] = matmul(on, dyd, "tn", "ydn_dw")
    do_sb = matmul(dys, W["w_branch_sb"], "nt", "ysb_dx")
    G["w_branch_sb"] = matmul(o_sb16, dys, "tn", "ysb_dw")
    do_dn, dproj, G["dn_out_norm_gain"] = dn_out_bwd(o_dn, proj, W["dn_out_norm_gain"], don)
    (dqn, dkn, dva, dgc, dbc), swapped = gdn_bwd(qn, kn, va, gc, bc, grow, states, do_dn,
                                                 rider=early_swap[0](G) if early_swap else None)
    if early_swap:
        early_swap[1](swapped)
    dpab, dal, ddt = ab_bwd(pab, al, dtb, dgc, dbc)
    G["dn_a_log"], G["dn_dt_bias"] = dal[:, :HEADS], ddt[:, :HEADS]
    dyq, dcq = conv_bwd_act(proj, cq8, dqn, C_DQ * 8, 8, True, "conv_q_bwd")
    dyk, dck = conv_bwd_act(proj, ck8, dkn, C_DK * 8, 8, True, "conv_k_bwd")
    dyv, dcv = conv_bwd_act(proj, cv8, dva, C_DV * 8, 16, False, "conv_v_bwd")
    G["conv_q"], G["conv_k"], G["conv_v"] = dcq[:DN_CONV], dck[:DN_CONV], dcv[:DN_CONV]
    dproj = conv_bwd_in(dyq, cq8, dproj, C_DQ * 8, "conv_q_dx")
    dproj = conv_bwd_in(dyk, ck8, dproj, C_DK * 8, "conv_k_dx")
    dproj = conv_bwd_in(dyv, cv8, dproj, C_DV * 8, "conv_v_dx")
    (dqs, dks, dvs), delivered = sb_bwd(qs, ks, vs, o_sb, do_sb, rider=early_grads[0](G) if early_grads else None)
    if early_grads:
        early_grads[1](delivered)
    dproj, G["sb_q_norm_gain"], G["sb_k_norm_gain"] = sb_prep_bwd(
        proj, W["sb_q_norm_gain"], W["sb_k_norm_gain"], dqs, dks, dvs, dproj)
    dproj = lax.dynamic_update_slice(dproj, d_gates, (0, C_GDN * 1024))
    dw_big = matmul(n1, dproj, "tn", "proj_dw")
    dw_ab = matmul(n1, dpab, "tn", "pab_dw")
    G["w_in"] = (dw_big, dw_ab)
    if last_grads:
        dn1, delivered = matmul(dproj, w_big, "nt", "proj_dx", tk_t=1024, rider=last_grads[0](G))
        last_grads[1](delivered)
    else:
        dn1 = matmul(dproj, w_big, "nt", "proj_dx", tk_t=1024)
    dn1 = matmul(dpab, w_ab, "nt", "pab_dx", residual=dn1)
    dh0, _, G["norm_mix_gain"] = rms_bwd(h0, W["norm_mix_gain"], dn1, dh1, "rms1_bwd")
    G["meta_tokens"] = dh0[P0:P0 + N_META]
    return loss, dh0[P0 + N_META:], G


_BIG = ("w_in", "w_branch_dn", "w_branch_sb", "w_out", "w_ffn_in", "w_ffn_out")
_COL_SHARDED = ("w_in", "w_ffn_in", "meta_tokens", "conv_q", "conv_k", "conv_v")
_SMALL_REPL = ("norm_mix_gain", "norm_ffn_gain", "dn_a_log", "dn_dt_bias", "dn_out_norm_gain", "sb_q_norm_gain",
               "sb_k_norm_gain")
_SMALL_SHARD = ("meta_tokens", "conv_q", "conv_k", "conv_v")
_ORDER = ("meta_tokens", "norm_mix_gain", "w_in", "conv_q", "conv_k", "conv_v", "dn_a_log", "dn_dt_bias",
          "dn_out_norm_gain", "sb_q_norm_gain", "sb_k_norm_gain", "w_branch_dn", "w_branch_sb", "w_out",
          "norm_ffn_gain", "w_ffn_in", "w_ffn_out")


def _unshard(g4, name):
    if name in _COL_SHARDED:
        r, cs = g4.shape[1:]
        return jnp.transpose(g4, (1, 0, 2)).reshape(r, N_CHIPS * cs)
    return g4.reshape((-1,) + g4.shape[2:])


def _to_shards(full, name):
    if full.ndim == 3:
        return full
    if name in _COL_SHARDED:
        r, c = full.shape
        return jnp.transpose(full.reshape(r, N_CHIPS, c // N_CHIPS), (1, 0, 2))
    r, c = full.shape
    return full.reshape(N_CHIPS, r // N_CHIPS, c)


def _rows_1024(a):
    r, c = a.shape
    if c >= 1024:
        return a.reshape(r * (c // 1024), 1024)
    return jnp.pad(a, ((0, 0), (0, 1024 - c)))


def kernel(x, meta_tokens, norm_mix_gain, w_in, conv_q, conv_k, conv_v, dn_a_log, dn_dt_bias, dn_out_norm_gain, sb_q_norm_gain, sb_k_norm_gain, w_branch_dn, w_branch_sb, w_out, norm_ffn_gain, w_ffn_in, w_ffn_out, loss_target, m_meta_tokens, m_norm_mix_gain, m_w_in, m_conv_q, m_conv_k, m_conv_v, m_dn_a_log, m_dn_dt_bias, m_dn_out_norm_gain, m_sb_q_norm_gain, m_sb_k_norm_gain, m_w_branch_dn, m_w_branch_sb, m_w_out, m_norm_ffn_gain, m_w_ffn_in, m_w_ffn_out, v_meta_tokens, v_norm_mix_gain, v_w_in, v_conv_q, v_conv_k, v_conv_v, v_dn_a_log, v_dn_dt_bias, v_dn_out_norm_gain, v_sb_q_norm_gain, v_sb_k_norm_gain, v_w_branch_dn, v_w_branch_sb, v_w_out, v_norm_ffn_gain, v_w_ffn_in, v_w_ffn_out):
    Wl = dict(meta_tokens=meta_tokens, norm_mix_gain=norm_mix_gain, w_in=w_in[0], conv_q=conv_q[0], conv_k=conv_k[0],
              conv_v=conv_v[0], dn_a_log=dn_a_log, dn_dt_bias=dn_dt_bias, dn_out_norm_gain=dn_out_norm_gain,
              sb_q_norm_gain=sb_q_norm_gain, sb_k_norm_gain=sb_k_norm_gain, w_branch_dn=w_branch_dn[0],
              w_branch_sb=w_branch_sb[0], w_out=w_out[0], norm_ffn_gain=norm_ffn_gain, w_ffn_in=w_ffn_in[0],
              w_ffn_out=w_ffn_out[0])
    Ml = dict(meta_tokens=m_meta_tokens, norm_mix_gain=m_norm_mix_gain, w_in=m_w_in[0], conv_q=m_conv_q[0],
              conv_k=m_conv_k[0], conv_v=m_conv_v[0], dn_a_log=m_dn_a_log, dn_dt_bias=m_dn_dt_bias,
              dn_out_norm_gain=m_dn_out_norm_gain, sb_q_norm_gain=m_sb_q_norm_gain, sb_k_norm_gain=m_sb_k_norm_gain,
              w_branch_dn=m_w_branch_dn[0], w_branch_sb=m_w_branch_sb[0], w_out=m_w_out[0],
              norm_ffn_gain=m_norm_ffn_gain, w_ffn_in=m_w_ffn_in[0], w_ffn_out=m_w_ffn_out[0])
    Vl = dict(meta_tokens=v_meta_tokens, norm_mix_gain=v_norm_mix_gain, w_in=v_w_in[0], conv_q=v_conv_q[0],
              conv_k=v_conv_k[0], conv_v=v_conv_v[0], dn_a_log=v_dn_a_log, dn_dt_bias=v_dn_dt_bias,
              dn_out_norm_gain=v_dn_out_norm_gain, sb_q_norm_gain=v_sb_q_norm_gain, sb_k_norm_gain=v_sb_k_norm_gain,
              w_branch_dn=v_w_branch_dn[0], w_branch_sb=v_w_branch_sb[0], w_out=v_w_out[0],
              norm_ffn_gain=v_norm_ffn_gain, w_ffn_in=v_w_ffn_in[0], w_ffn_out=v_w_ffn_out[0])
    lead = {n: (1,) if (n in _BIG or n in ("conv_q", "conv_k", "conv_v")) else () for n in _ORDER}

    chip = 2 * lax.axis_index("x") + lax.axis_index("y")
    c = lax.axis_index("c")
    halved = {n: Wl[n].astype(bf16).reshape(2, Wl[n].shape[0] // 2, Wl[n].shape[1]) for n in _BIG}

    def gathered_weights(names, owns, outs):
        res = {}
        for n, own, g4 in zip(names, owns, outs):
            g4 = lax.dynamic_update_slice(g4, own[None], (chip,) + (0,) * own.ndim)
            if n in _BIG:
                g4 = g4.reshape(N_CHIPS, 2 * g4.shape[2], g4.shape[3])
            res[n] = g4 if n == "w_in" else _unshard(g4, n)
        return res

    first = ["w_in"] + list(_SMALL_SHARD)
    first_own = [halved["w_in"]] + [Wl[n] for n in _SMALL_SHARD]
    W = dict(Wl)
    W.update(gathered_weights(first, first_own, run_exchange(gather_chips(first_own[:1], first_own[1:]), "gather_w_in")))
    late = [n for n in _BIG if n != "w_in"]
    late_own = [halved[n] for n in late]
    for n in late:
        del W[n]

    def halves_of(names, G):
        g4 = [_to_shards(G[n], n) for n in names]
        return [g.reshape(N_CHIPS, 2, g.shape[1] // 2, g.shape[2]) for g in g4]

    def pair_added(g42, from_sib, tag, wire):
        mine = [lax.dynamic_index_in_dim(g, c, axis=1, keepdims=False) for g in g42]
        return [add2(a, b, "grad_pair_add_%s%d" % (tag, t), out_dtype=wire)
                for t, (a, b) in enumerate(zip(mine, from_sib))]

    def chip_reduced(parts, slots, tag):
        slots = [lax.dynamic_update_slice(s, lax.dynamic_index_in_dim(p, chip, axis=0, keepdims=True), (chip, 0, 0))
                 for s, p in zip(slots, parts)]
        return [sum_slots(s, "grad_chip_sum_%s%d" % (tag, t)) for t, s in enumerate(slots)]

    early, last = {}, {}

    def early_swap_begin(G):
        early["g42"] = halves_of(late, G)
        return sibling_swap(early["g42"])

    def early_begin(G):
        early["parts"] = pair_added(early["g42"], early["from_sib"], "a", f32)
        return scatter_chips(early["parts"])

    def last_begin(G):
        g2 = [g.reshape(1, 2, g.shape[0] // 2, g.shape[1]) for g in G["w_in"]]
        added = pair_added(g2, run_exchange(sibling_swap(g2), "grad_sibling_swap_b"), "b", bf16)
        last["parts"] = [_join_w_in(added[0][0], added[1][0])]
        return scatter_chips(last["parts"])

    loss, grad_x, G = _step(
        x[0], W["meta_tokens"], W, loss_target[0],
        late_weights=(gather_chips(late_own, []), lambda outs: gathered_weights(late, late_own, outs)),
        early_swap=(early_swap_begin, lambda outs: early.update(from_sib=outs)),
        early_grads=(early_begin, lambda slots: early.update(halves=chip_reduced(early["parts"], slots, "a"))),
        last_grads=(last_begin, lambda slots: last.update(halves=chip_reduced(last["parts"], slots, "b"))))
    halves = last["halves"] + early["halves"]
    theirs = sibling_send(halves, "grad_sibling_send")
    Gs = {}
    for n, h, o in zip(["w_in"] + late, halves, theirs):
        Gs[n] = lax.dynamic_update_slice(jnp.concatenate([o, o], axis=0), h, (c * h.shape[0], 0))

    small_names = list(_SMALL_REPL) + list(_SMALL_SHARD)
    pieces = [_rows_1024(G[n]) for n in small_names] + [_rows_1024(loss)]
    counts = [p.shape[0] for p in pieces]
    pack = jnp.concatenate(pieces, axis=0)
    pad_rows = (-pack.shape[0]) % SUB
    pack = jnp.pad(pack, ((0, pad_rows), (0, 0)))
    total = sum_slots(gather_all(pack, "small_gather"), "small_sum")
    chip = 2 * lax.axis_index("x") + lax.axis_index("y")
    row = 0
    for n, cnt in zip(small_names, counts[:-1]):
        blk = total[row:row + cnt]
        row += cnt
        full_shape = G[n].shape
        if full_shape[1] >= 1024:
            blk = blk.reshape(full_shape)
        else:
            blk = blk[:, :full_shape[1]]
        if n in _SMALL_SHARD:
            cs = full_shape[1] // N_CHIPS
            blk = lax.dynamic_slice_in_dim(blk, chip * cs, cs, axis=1)
        Gs[n] = blk
    loss_out = total[row, 0]

    grads, deltas, new_m, new_v = [], [], [], []
    for n in _ORDER:
        d, m2, v2 = adamw(Wl[n], Gs[n], Ml[n], Vl[n], "adamw_" + n)
        shape = lead[n] + Wl[n].shape
        grads.append(Gs[n].reshape(shape))
        deltas.append(d.reshape(shape))
        new_m.append(m2.reshape(shape))
        new_v.append(v2.reshape(shape))
    return (loss_out, grad_x[None], *grads, *deltas, *new_m, *new_v)
```

```python
import jax
import jax.numpy as jnp
from jax import lax
from jax.experimental import pallas as pl
from jax.experimental.pallas import tpu as pltpu

f32 = jnp.float32
bf16 = jnp.bfloat16

D_MODEL = 1024
N_META = 16
CHUNK = 64
HEADS = 8
DN_DK = 128
DN_DV = 256
DN_CONV = 4
DN_QK = HEADS * DN_DK
DN_V = HEADS * DN_DV
SB_DH = 128
SB_W = HEADS * SB_DH
SB_BLOCK = 128
SB_QB = 384
SB_GROUP = 4
SB_HEADS_PER_STEP = 2
SB_FWD_HEADS_PER_STEP = 4
GDN_HEADS_PER_STEP = 8
CONV_W = 1024
D_FF = 2816
RMS_EPS = 1e-6
L2_EPS = 1e-6
ADAM_LR = 0.001
ADAM_B1 = 0.9
ADAM_B2 = 0.999
ADAM_EPS = 1e-08
ADAM_WD = 0.01
ADAM_STEP = 10

P0 = 112
LANE = 128
SUB = 8
VMEM_LIMIT = 48 * 1024 * 1024
N_CHIPS = 4
N_DEV = 8

C_DQ, C_DK, C_DV, C_DZ, C_SQ, C_SK, C_SV, C_GDN, C_GSB = 0, 1, 2, 4, 6, 7, 8, 9, 10
PROJ_BIG = 11 * 1024
AB_COL = 2 * DN_QK + 2 * DN_V


def _params(n_axes):
    return pltpu.CompilerParams(dimension_semantics=("arbitrary",) * n_axes, vmem_limit_bytes=VMEM_LIMIT)


def _tile(n, target, q=LANE):
    best = None
    for t in range(q, min(n, target) + 1, q):
        if n % t == 0:
            best = t
    return best if best is not None else n


def _dot(a, b):
    return jnp.dot(a.astype(bf16), b.astype(bf16), preferred_element_type=f32)


def _dot_nt(a, b):
    return lax.dot_general(a.astype(bf16), b.astype(bf16), (((1,), (1,)), ((), ())), preferred_element_type=f32)


def _dot_tn(a, b):
    return lax.dot_general(a.astype(bf16), b.astype(bf16), (((0,), (0,)), ((), ())), preferred_element_type=f32)


_HI = lax.Precision.HIGH


def _hdot(a, b):
    return jnp.dot(a, b, precision=_HI, preferred_element_type=f32)


def _hdot_nt(a, b):
    return lax.dot_general(a, b, (((1,), (1,)), ((), ())), precision=_HI, preferred_element_type=f32)


def _hdot_tn(a, b):
    return lax.dot_general(a, b, (((0,), (0,)), ((), ())), precision=_HI, preferred_element_type=f32)


def _sigmoid(x):
    return 0.5 * jnp.tanh(0.5 * x) + 0.5


def _log1p_small(e):
    return jnp.where(e < 1e-3, e * (1.0 - e * (0.5 - e * (1.0 / 3.0))), jnp.log(1.0 + e))


def _rowsum(x):
    return jnp.sum(x, axis=1, keepdims=True)


def _allsum(x):
    return jnp.sum(jnp.sum(x, axis=1, keepdims=True), axis=0, keepdims=True)


def matmul(a, b, mode, name, residual=None, out_dtype=f32, tm_t=1408, tn_t=1024, tk_t=1408, rider=None):
    if mode == "nn":
        (M, K), (K2, N) = a.shape, b.shape
    elif mode == "nt":
        (M, K), (N, K2) = a.shape, b.shape
    else:
        (K, M), (K2, N) = a.shape, b.shape
    assert K == K2, (a.shape, b.shape, mode)
    tm, tn, tk = _tile(M, tm_t), _tile(N, tn_t), _tile(K, tk_t)
    nk = K // tk
    if mode == "nn":
        a_spec = pl.BlockSpec((tm, tk), lambda i, j, k: (i, k))
        b_spec = pl.BlockSpec((tk, tn), lambda i, j, k: (k, j))
        dims = (((1,), (0,)), ((), ()))
    elif mode == "nt":
        a_spec = pl.BlockSpec((tm, tk), lambda i, j, k: (i, k))
        b_spec = pl.BlockSpec((tn, tk), lambda i, j, k: (j, k))
        dims = (((1,), (1,)), ((), ()))
    else:
        a_spec = pl.BlockSpec((tk, tm), lambda i, j, k: (k, i))
        b_spec = pl.BlockSpec((tk, tn), lambda i, j, k: (k, j))
        dims = (((0,), (0,)), ((), ()))
    o_spec = pl.BlockSpec((tm, tn), lambda i, j, k: (i, j))
    has_res = residual is not None
    grid = (M // tm, N // tn, nk)
    split, ride_first, ride_last = _ride(rider, 3 if has_res else 2, 1, grid)

    def body(*refs):
        ins_, (o_ref,), (rin, rout, rest) = split(refs)
        a_ref, b_ref = ins_[:2]
        r_ref = ins_[2] if has_res else None
        acc_ref, ride = rest[0], (rin, rout, rest[1:])
        ride_first(ride)
        k = pl.program_id(2)

        @pl.when(k == 0)
        def _():
            acc_ref[...] = jnp.zeros_like(acc_ref)

        acc_ref[...] += lax.dot_general(a_ref[...].astype(bf16), b_ref[...].astype(bf16), dims,
                                        preferred_element_type=f32)

        @pl.when(k == nk - 1)
        def _():
            r = acc_ref[...]
            if has_res:
                r = r + r_ref[...]
            o_ref[...] = r.astype(out_dtype)

        ride_last(ride)

    ins = [a, b] + ([residual] if has_res else [])
    specs = [a_spec, b_spec] + ([o_spec] if has_res else [])
    r_ins = rider.ins if rider else []
    r_outs = rider.out_shapes if rider else []
    res = pl.pallas_call(
        body, name=name, grid=grid, in_specs=specs + [_ANY] * len(r_ins), out_specs=[o_spec] + [_ANY] * len(r_outs),
        out_shape=[jax.ShapeDtypeStruct((M, N), out_dtype)] + list(r_outs),
        scratch_shapes=[pltpu.VMEM((tm, tn), f32)] + (rider.scratch if rider else []), compiler_params=_params(3),
    )(*ins, *r_ins)
    return (res[0], res[1:]) if rider else res[0]


def _row_tile(tp):
    return _tile(tp, 512)


def rms_fwd(h, gain, name):
    tp, d = h.shape
    rt = _row_tile(tp)

    def body(h_ref, g_ref, o_ref):
        x = h_ref[...]
        r = lax.rsqrt(jnp.mean(x * x, axis=-1, keepdims=True) + RMS_EPS)
        o_ref[...] = (x * r * g_ref[...]).astype(bf16)

    return pl.pallas_call(
        body, name=name, grid=(tp // rt,),
        in_specs=[pl.BlockSpec((rt, d), lambda i: (i, 0)), pl.BlockSpec((1, d), lambda i: (0, 0))],
        out_specs=pl.BlockSpec((rt, d), lambda i: (i, 0)),
        out_shape=jax.ShapeDtypeStruct((tp, d), bf16), compiler_params=_params(1),
    )(h, gain)


def rms_bwd(h, gain, dn, dres, name):
    tp, d = h.shape
    rt = _row_tile(tp)

    def body(h_ref, g_ref, dn_ref, dr_ref, dh_ref, dhb_ref, dg_ref):
        i = pl.program_id(0)
        x = h_ref[...]
        r = lax.rsqrt(jnp.mean(x * x, axis=-1, keepdims=True) + RMS_EPS)
        xh = x * r
        dn_ = dn_ref[...]
        dxh = dn_ * g_ref[...]
        dh = r * (dxh - xh * jnp.mean(dxh * xh, axis=-1, keepdims=True)) + dr_ref[...]
        dh_ref[...] = dh
        dhb_ref[...] = dh.astype(bf16)
        part = jnp.sum(dn_ * xh, axis=0, keepdims=True)

        @pl.when(i == 0)
        def _():
            dg_ref[...] = part

        @pl.when(i > 0)
        def _():
            dg_ref[...] += part

    row = pl.BlockSpec((rt, d), lambda i: (i, 0))
    vec = pl.BlockSpec((1, d), lambda i: (0, 0))
    return pl.pallas_call(
        body, name=name, grid=(tp // rt,), in_specs=[row, vec, row, row], out_specs=[row, row, vec],
        out_shape=[jax.ShapeDtypeStruct((tp, d), f32), jax.ShapeDtypeStruct((tp, d), bf16),
                   jax.ShapeDtypeStruct((1, d), f32)],
        compiler_params=_params(1),
    )(h, gain, dn, dres)


def loss_head(y, target):
    tp, d = y.shape
    lead = P0 + N_META
    rt = _row_tile(tp)
    ns = rt // lead
    assert lead == SB_BLOCK and rt % lead == 0 and target.shape == (tp - lead, d)
    last = target.shape[0] // lead - 1

    def body(*refs):
        y_ref, t_refs = refs[0], refs[1:1 + ns]
        dy_ref, dyb_ref, l_ref = refs[1 + ns:]
        i = pl.program_id(0)

        @pl.when(i == 0)
        def _():
            l_ref[...] = jnp.zeros_like(l_ref)

        part = jnp.zeros((1, 1), f32)
        for s in range(ns):
            rows = slice(s * lead, (s + 1) * lead)
            err = y_ref[rows, :] - t_refs[s][...]
            if s == 0:
                err = err * (i > 0).astype(f32)
            dy = err * (1.0 / d)
            dy_ref[rows, :] = dy
            dyb_ref[rows, :] = dy.astype(bf16)
            part = part + _allsum(err * err)
        l_ref[...] += jnp.broadcast_to(part * (0.5 / d), l_ref.shape)

    row = pl.BlockSpec((rt, d), lambda i: (i, 0))
    t_specs = [pl.BlockSpec((lead, d), lambda i, s=s: (jnp.clip(ns * i + s - 1, 0, last), 0)) for s in range(ns)]
    return pl.pallas_call(
        body, name="loss_head", grid=(tp // rt,), in_specs=[row] + t_specs,
        out_specs=[row, row, pl.BlockSpec((1, LANE), lambda i: (0, 0))],
        out_shape=[jax.ShapeDtypeStruct((tp, d), f32), jax.ShapeDtypeStruct((tp, d), bf16),
                   jax.ShapeDtypeStruct((1, LANE), f32)],
        compiler_params=_params(1),
    )(y, *([target] * ns))


def swiglu_fwd(u):
    tp = u.shape[0]
    rt, cb = _row_tile(tp), D_FF // 2
    nb = D_FF // cb

    def body(g_ref, u_ref, o_ref):
        g = g_ref[...]
        o_ref[...] = (g * _sigmoid(g) * u_ref[...]).astype(bf16)

    return pl.pallas_call(
        body, name="swiglu_fwd", grid=(tp // rt, nb),
        in_specs=[pl.BlockSpec((rt, cb), lambda i, j: (i, j)), pl.BlockSpec((rt, cb), lambda i, j: (i, j + nb))],
        out_specs=pl.BlockSpec((rt, cb), lambda i, j: (i, j)),
        out_shape=jax.ShapeDtypeStruct((tp, D_FF), bf16), compiler_params=_params(2),
    )(u, u)


def swiglu_bwd(u, dact):
    tp = u.shape[0]
    rt, cb = _row_tile(tp), D_FF // 2
    nb = D_FF // cb

    def body(g_ref, u_ref, da_ref, dg_ref, du_ref):
        g = g_ref[...]
        s = _sigmoid(g)
        da = da_ref[...]
        dg_ref[...] = (da * u_ref[...] * s * (1.0 + g * (1.0 - s))).astype(bf16)
        du_ref[...] = (da * g * s).astype(bf16)

    lo = pl.BlockSpec((rt, cb), lambda i, j: (i, j))
    hi = pl.BlockSpec((rt, cb), lambda i, j: (i, j + nb))
    dgate, dup = pl.pallas_call(
        body, name="swiglu_bwd", grid=(tp // rt, nb), in_specs=[lo, hi, lo], out_specs=[lo, lo],
        out_shape=[jax.ShapeDtypeStruct((tp, D_FF), bf16)] * 2, compiler_params=_params(2),
    )(u, u, dact)
    return dgate, dup


def merge_fwd(proj, ydn, ysb):
    tp = proj.shape[0]
    rt, d = _row_tile(tp), D_MODEL

    def body(gd_ref, gs_ref, yd_ref, ys_ref, o_ref):
        o_ref[...] = (_sigmoid(gd_ref[...]) * yd_ref[...] + _sigmoid(gs_ref[...]) * ys_ref[...]).astype(bf16)

    row = pl.BlockSpec((rt, d), lambda i: (i, 0))
    return pl.pallas_call(
        body, name="merge_fwd", grid=(tp // rt,),
        in_specs=[pl.BlockSpec((rt, d), lambda i: (i, C_GDN)), pl.BlockSpec((rt, d), lambda i: (i, C_GSB)), row, row],
        out_specs=row, out_shape=jax.ShapeDtypeStruct((tp, d), bf16), compiler_params=_params(1),
    )(proj, proj, ydn, ysb)


def merge_bwd(proj, ydn, ysb, dm):
    tp = proj.shape[0]
    rt, d = _row_tile(tp), D_MODEL

    def body(gd_ref, gs_ref, yd_ref, ys_ref, dm_ref, dyd_ref, dys_ref, dg_ref):
        dm_ = dm_ref[...]
        sd = _sigmoid(gd_ref[...])
        ss = _sigmoid(gs_ref[...])
        dyd_ref[...] = (dm_ * sd).astype(bf16)
        dys_ref[...] = (dm_ * ss).astype(bf16)
        dg_ref[:, :d] = (dm_ * yd_ref[...] * sd * (1.0 - sd)).astype(bf16)
        dg_ref[:, d:] = (dm_ * ys_ref[...] * ss * (1.0 - ss)).astype(bf16)

    row = pl.BlockSpec((rt, d), lambda i: (i, 0))
    return pl.pallas_call(
        body, name="merge_bwd", grid=(tp // rt,),
        in_specs=[pl.BlockSpec((rt, d), lambda i: (i, C_GDN)), pl.BlockSpec((rt, d), lambda i: (i, C_GSB)), row, row, row],
        out_specs=[row, row, pl.BlockSpec((rt, 2 * d), lambda i: (i, 0))],
        out_shape=[jax.ShapeDtypeStruct((tp, d), bf16)] * 2 + [jax.ShapeDtypeStruct((tp, 2 * d), bf16)],
        compiler_params=_params(1),
    )(proj, proj, ydn, ysb, dm)


def dn_out_fwd(o, proj, gain):
    tp = o.shape[0]
    rt, cb, wide = _row_tile(tp), DN_DV, 1024
    zb = C_DZ * 1024 // wide

    def body(o_ref, z_ref, g_ref, y_ref):
        for s in range(wide // cb):
            sl = slice(s * cb, (s + 1) * cb)
            x = o_ref[:, sl]
            r = lax.rsqrt(jnp.mean(x * x, axis=-1, keepdims=True) + RMS_EPS)
            z = z_ref[:, sl]
            y_ref[:, sl] = (x * r * g_ref[...] * (z * _sigmoid(z))).astype(bf16)

    blk = pl.BlockSpec((rt, wide), lambda i, j: (i, j))
    return pl.pallas_call(
        body, name="dn_out_fwd", grid=(tp // rt, DN_V // wide),
        in_specs=[blk, pl.BlockSpec((rt, wide), lambda i, j: (i, j + zb)), pl.BlockSpec((1, cb), lambda i, j: (0, 0))],
        out_specs=blk, out_shape=jax.ShapeDtypeStruct((tp, DN_V), bf16), compiler_params=_params(2),
    )(o, proj, gain)


def dn_out_bwd(o, proj, gain, dy):
    tp = o.shape[0]
    rt, cb, wide = _row_tile(tp), DN_DV, 1024
    zb = C_DZ * 1024 // wide

    def body(o_ref, z_ref, g_ref, dy_ref, do_ref, dz_ref, dg_ref):
        i, j = pl.program_id(0), pl.program_id(1)
        g = g_ref[...]
        part = jnp.zeros((1, cb), f32)
        for hh in range(wide // cb):
            sl = slice(hh * cb, (hh + 1) * cb)
            x = o_ref[:, sl]
            r = lax.rsqrt(jnp.mean(x * x, axis=-1, keepdims=True) + RMS_EPS)
            xh = x * r
            z = z_ref[:, sl]
            s = _sigmoid(z)
            dy_ = dy_ref[:, sl]
            drn = dy_ * (z * s)
            dz_ref[:, sl] = (dy_ * xh * g * s * (1.0 + z * (1.0 - s))).astype(bf16)
            dxh = drn * g
            do_ref[:, sl] = r * (dxh - xh * jnp.mean(dxh * xh, axis=-1, keepdims=True))
            part = part + jnp.sum(drn * xh, axis=0, keepdims=True)
        first = jnp.logical_and(i == 0, j == 0)

        @pl.when(first)
        def _():
            dg_ref[...] = part

        @pl.when(jnp.logical_not(first))
        def _():
            dg_ref[...] += part

    blk = pl.BlockSpec((rt, wide), lambda i, j: (i, j))
    vec = pl.BlockSpec((1, cb), lambda i, j: (0, 0))
    return pl.pallas_call(
        body, name="dn_out_bwd", grid=(tp // rt, DN_V // wide),
        in_specs=[blk, pl.BlockSpec((rt, wide), lambda i, j: (i, j + zb)), vec, blk],
        out_specs=[blk, pl.BlockSpec((rt, wide), lambda i, j: (i, j + zb)), vec],
        out_shape=[jax.ShapeDtypeStruct((tp, DN_V), f32), jax.ShapeDtypeStruct((tp, PROJ_BIG), bf16),
                   jax.ShapeDtypeStruct((1, cb), f32)],
        compiler_params=_params(2),
    )(o, proj, gain, dy)


def sb_prep_fwd(proj, gq, gk):
    tp = proj.shape[0]
    rt, cb = _row_tile(tp), SB_DH

    def body(q_ref, k_ref, v_ref, gq_ref, gk_ref, qo_ref, ko_ref, vo_ref):
        for x_ref, g_ref, o_ref in ((q_ref, gq_ref, qo_ref), (k_ref, gk_ref, ko_ref)):
            for h in range(HEADS):
                sl = slice(h * cb, (h + 1) * cb)
                x = x_ref[:, sl]
                r = lax.rsqrt(jnp.mean(x * x, axis=-1, keepdims=True) + RMS_EPS)
                o_ref[:, sl] = (x * r * g_ref[...]).astype(bf16)
        vo_ref[...] = v_ref[...].astype(bf16)

    blk = pl.BlockSpec((rt, SB_W), lambda i: (i, 0))
    vec = pl.BlockSpec((1, cb), lambda i: (0, 0))
    return pl.pallas_call(
        body, name="sb_prep_fwd", grid=(tp // rt,),
        in_specs=[pl.BlockSpec((rt, SB_W), lambda i: (i, C_SQ)), pl.BlockSpec((rt, SB_W), lambda i: (i, C_SK)),
                  pl.BlockSpec((rt, SB_W), lambda i: (i, C_SV)), vec, vec],
        out_specs=[blk] * 3, out_shape=[jax.ShapeDtypeStruct((tp, SB_W), bf16)] * 3, compiler_params=_params(1),
    )(proj, proj, proj, gq, gk)


def sb_prep_bwd(proj, gq, gk, dqs, dks, dvs, into):
    tp = proj.shape[0]
    rt, cb = _row_tile(tp), SB_DH
    assert (C_SQ * 1024) % (3 * SB_W) == 0 and (C_SQ + 1, C_SQ + 2) == (C_SK, C_SV)

    def body(q_ref, k_ref, gq_ref, gk_ref, dq_ref, dk_ref, dv_ref, into_ref, do_ref, dgq_ref, dgk_ref):
        first = pl.program_id(0) == 0
        do_ref[:, 2 * SB_W:] = dv_ref[...].astype(bf16)
        for x_ref, g_ref, dn_ref, at, dg_ref, mul in ((q_ref, gq_ref, dq_ref, 0, dgq_ref, None),
                                                      (k_ref, gk_ref, dk_ref, SB_W, dgk_ref, SB_DH ** -0.5)):
            part = jnp.zeros((1, cb), f32)
            for h in range(HEADS):
                sl = slice(h * cb, (h + 1) * cb)
                x = x_ref[:, sl]
                r = lax.rsqrt(jnp.mean(x * x, axis=-1, keepdims=True) + RMS_EPS)
                xh = x * r
                dn_ = dn_ref[:, sl] if mul is None else dn_ref[:, sl] * mul
                dxh = dn_ * g_ref[...]
                do_ref[:, at + h * cb:at + (h + 1) * cb] = (
                    r * (dxh - xh * jnp.mean(dxh * xh, axis=-1, keepdims=True))).astype(bf16)
                part = part + jnp.sum(dn_ * xh, axis=0, keepdims=True)

            @pl.when(first)
            def _(dg_ref=dg_ref, part=part):
                dg_ref[...] = part

            @pl.when(jnp.logical_not(first))
            def _(dg_ref=dg_ref, part=part):
                dg_ref[...] += part

    blk = pl.BlockSpec((rt, SB_W), lambda i: (i, 0))
    vec = pl.BlockSpec((1, cb), lambda i: (0, 0))
    return pl.pallas_call(
        body, name="sb_prep_bwd", grid=(tp // rt,),
        in_specs=[pl.BlockSpec((rt, SB_W), lambda i: (i, C_SQ)), pl.BlockSpec((rt, SB_W), lambda i: (i, C_SK)),
                  vec, vec, blk, blk, blk, pl.BlockSpec(memory_space=pl.ANY)],
        out_specs=[pl.BlockSpec((rt, 3 * SB_W), lambda i: (i, C_SQ * 1024 // (3 * SB_W))), vec, vec],
        out_shape=[jax.ShapeDtypeStruct(into.shape, into.dtype)] + [jax.ShapeDtypeStruct((1, cb), f32)] * 2,
        input_output_aliases={7: 0}, compiler_params=_params(1),
    )(proj, proj, gq, gk, dqs, dks, dvs, into)


def _conv_taps(ext, rt):
    taps = []
    for k in range(DN_CONV):
        s = DN_CONV - 1 - k
        taps.append((pltpu.roll(ext, s, axis=0) if s else ext)[SUB:SUB + rt])
    return taps


def _conv_act(taps, w, l2):
    y = taps[0] * w[0:1]
    for k in range(1, DN_CONV):
        y = y + taps[k] * w[k:k + 1]
    s = _sigmoid(y)
    a = y * s
    if l2:
        n = lax.rsqrt(jnp.sum(a * a, axis=-1, keepdims=True) + L2_EPS)
        return y, s, a, n
    return y, s, a, None


def conv_fwd(proj, w8, col_blk, ncb, l2, name):
    tp = proj.shape[0]
    rt = _row_tile(tp)
    hb = rt // SUB
    cw = CONV_W
    cb0 = col_blk * LANE // cw

    def body(x_ref, h_ref, w_ref, o_ref):
        i = pl.program_id(1)
        first = (i > 0).astype(f32)
        for s in range(cw // LANE):
            sl = slice(s * LANE, (s + 1) * LANE)
            ext = jnp.concatenate([h_ref[:, sl] * first, x_ref[:, sl]], axis=0)
            _, _, a, n = _conv_act(_conv_taps(ext, rt), w_ref[:, sl], l2)
            o_ref[:, sl] = a * n if l2 else a

    return pl.pallas_call(
        body, name=name, grid=(ncb * LANE // cw, tp // rt),
        in_specs=[pl.BlockSpec((rt, cw), lambda j, i: (i, j + cb0)),
                  pl.BlockSpec((SUB, cw), lambda j, i: (jnp.maximum(i * hb - 1, 0), j + cb0)),
                  pl.BlockSpec((SUB, cw), lambda j, i: (0, j))],
        out_specs=pl.BlockSpec((rt, cw), lambda j, i: (i, j)),
        out_shape=jax.ShapeDtypeStruct((tp, ncb * LANE), f32), compiler_params=_params(2),
    )(proj, proj, w8)


def conv_bwd_act(proj, w8, dout, col_blk, ncb, l2, name):
    tp = proj.shape[0]
    rt = _row_tile(tp)
    hb = rt // SUB
    cw = CONV_W
    cb0 = col_blk * LANE // cw

    def body(x_ref, h_ref, w_ref, d_ref, dy_ref, dw_ref):
        i = pl.program_id(1)
        first = (i > 0).astype(f32)
        rows = lax.broadcasted_iota(jnp.int32, (SUB, LANE), 0)
        for s in range(cw // LANE):
            sl = slice(s * LANE, (s + 1) * LANE)
            ext = jnp.concatenate([h_ref[:, sl] * first, x_ref[:, sl]], axis=0)
            taps = _conv_taps(ext, rt)
            y, sg, a, n = _conv_act(taps, w_ref[:, sl], l2)
            da = d_ref[:, sl]
            if l2:
                out = a * n
                da = n * (da - out * jnp.sum(da * out, axis=-1, keepdims=True))
            dy = da * sg * (1.0 + y * (1.0 - sg))
            dy_ref[:, sl] = dy
            part = jnp.zeros((SUB, LANE), f32)
            for k in range(DN_CONV):
                part = part + jnp.where(rows == k, jnp.sum(taps[k] * dy, axis=0, keepdims=True), 0.0)

            @pl.when(i == 0)
            def _(sl=sl, part=part):
                dw_ref[:, sl] = part

            @pl.when(i > 0)
            def _(sl=sl, part=part):
                dw_ref[:, sl] += part

    return pl.pallas_call(
        body, name=name, grid=(ncb * LANE // cw, tp // rt),
        in_specs=[pl.BlockSpec((rt, cw), lambda j, i: (i, j + cb0)),
                  pl.BlockSpec((SUB, cw), lambda j, i: (jnp.maximum(i * hb - 1, 0), j + cb0)),
                  pl.BlockSpec((SUB, cw), lambda j, i: (0, j)),
                  pl.BlockSpec((rt, cw), lambda j, i: (i, j))],
        out_specs=[pl.BlockSpec((rt, cw), lambda j, i: (i, j)), pl.BlockSpec((SUB, cw), lambda j, i: (0, j))],
        out_shape=[jax.ShapeDtypeStruct((tp, ncb * LANE), f32), jax.ShapeDtypeStruct((SUB, ncb * LANE), f32)],
        compiler_params=_params(2),
    )(proj, proj, w8, dout)


def conv_bwd_in(dy, w8, into, col_blk, name):
    tp, cols = dy.shape
    rt = _row_tile(tp)
    hb = rt // SUB
    nr = tp // rt
    last8 = tp // SUB - 1
    cw = CONV_W
    cb0 = col_blk * LANE // cw

    def body(d_ref, h_ref, w_ref, into_ref, o_ref):
        i = pl.program_id(1)
        last = (i < nr - 1).astype(f32)
        for c0 in range(cw // LANE):
            sl = slice(c0 * LANE, (c0 + 1) * LANE)
            ext = jnp.concatenate([d_ref[:, sl], h_ref[:, sl] * last], axis=0)
            w = w_ref[:, sl]
            acc = None
            for k in range(DN_CONV):
                s = DN_CONV - 1 - k
                sh = (pltpu.roll(ext, rt + SUB - s, axis=0) if s else ext)[0:rt]
                term = sh * w[k:k + 1]
                acc = term if acc is None else acc + term
            o_ref[:, sl] = acc.astype(bf16)

    return pl.pallas_call(
        body, name=name, grid=(cols // cw, nr),
        in_specs=[pl.BlockSpec((rt, cw), lambda j, i: (i, j)),
                  pl.BlockSpec((SUB, cw), lambda j, i: (jnp.minimum((i + 1) * hb, last8), j)),
                  pl.BlockSpec((SUB, cw), lambda j, i: (0, j)), pl.BlockSpec(memory_space=pl.ANY)],
        out_specs=pl.BlockSpec((rt, cw), lambda j, i: (i, j + cb0)),
        out_shape=jax.ShapeDtypeStruct(into.shape, into.dtype), input_output_aliases={3: 0},
        compiler_params=_params(2),
    )(dy, dy, w8, into)


def _ab_common(p, al, dtb, r0):
    rows = r0 + lax.broadcasted_iota(jnp.int32, p.shape, 0)
    mask = (rows >= P0).astype(f32)
    xx = p + dtb
    sp = jnp.maximum(xx, 0.0) + _log1p_small(jnp.exp(-jnp.abs(xx)))
    ea = jnp.exp(al)
    g = -ea * sp * mask
    beta = _sigmoid(p) * mask
    return g, beta, _sigmoid(xx), ea, mask


def _chunk_tri(rt, later):
    r = lax.broadcasted_iota(jnp.int32, (rt, rt), 0)
    c = lax.broadcasted_iota(jnp.int32, (rt, rt), 1)
    shift = CHUNK.bit_length() - 1
    same = jnp.right_shift(r, shift) == jnp.right_shift(c, shift)
    return jnp.logical_and(same, c >= r if later else c <= r).astype(f32)


def ab_fwd(pab, al, dtb):
    tp = pab.shape[0]
    rt = _row_tile(tp)
    assert rt % CHUNK == 0

    def body(p_ref, al_ref, dt_ref, g_ref, b_ref):
        i = pl.program_id(0)
        g, beta, _, _, _ = _ab_common(p_ref[...], al_ref[...], dt_ref[...], i * rt)
        gam = _hdot(_chunk_tri(rt, False), g)
        for h in range(HEADS):
            g_ref[h] = jnp.broadcast_to(gam[:, h:h + 1], (rt, LANE))
            b_ref[h] = jnp.broadcast_to(beta[:, HEADS + h:HEADS + h + 1], (rt, LANE))

    vec = pl.BlockSpec((1, LANE), lambda i: (0, 0))
    out = pl.BlockSpec((HEADS, rt, LANE), lambda i: (0, i, 0))
    return pl.pallas_call(
        body, name="ab_fwd", grid=(tp // rt,), in_specs=[pl.BlockSpec((rt, LANE), lambda i: (i, 0)), vec, vec],
        out_specs=[out, out], out_shape=[jax.ShapeDtypeStruct((HEADS, tp, LANE), f32)] * 2, compiler_params=_params(1),
    )(pab, al, dtb)


def ab_bwd(pab, al, dtb, dg, db):
    tp = pab.shape[0]
    rt = _row_tile(tp)

    def body(p_ref, al_ref, dt_ref, dg_ref, db_ref, dp_ref, dal_ref, ddt_ref):
        i = pl.program_id(0)
        g, beta, sx, ea, mask = _ab_common(p_ref[...], al_ref[...], dt_ref[...], i * rt)
        lanes = lax.broadcasted_iota(jnp.int32, (rt, LANE), 1)
        dgl = jnp.zeros((rt, LANE), f32)
        dbl = jnp.zeros((rt, LANE), f32)
        for h in range(HEADS):
            dgl = dgl + jnp.where(lanes == h, dg_ref[h], 0.0)
            dbl = dbl + jnp.where(lanes == HEADS + h, db_ref[h], 0.0)
        dgl = _hdot(_chunk_tri(rt, True), dgl)
        dxx = dgl * (-ea) * sx * mask
        dp_ref[...] = (dxx + dbl * beta * (1.0 - beta)).astype(bf16)
        pal = jnp.sum(dgl * g, axis=0, keepdims=True)
        pdt = jnp.sum(dxx, axis=0, keepdims=True)

        @pl.when(i == 0)
        def _():
            dal_ref[...] = pal
            ddt_ref[...] = pdt

        @pl.when(i > 0)
        def _():
            dal_ref[...] += pal
            ddt_ref[...] += pdt

    vec = pl.BlockSpec((1, LANE), lambda i: (0, 0))
    row = pl.BlockSpec((rt, LANE), lambda i: (i, 0))
    big = pl.BlockSpec((HEADS, rt, LANE), lambda i: (0, i, 0))
    return pl.pallas_call(
        body, name="ab_bwd", grid=(tp // rt,), in_specs=[row, vec, vec, big, big], out_specs=[row, vec, vec],
        out_shape=[jax.ShapeDtypeStruct((tp, LANE), bf16), jax.ShapeDtypeStruct((1, LANE), f32),
                   jax.ShapeDtypeStruct((1, LANE), f32)],
        compiler_params=_params(1),
    )(pab, al, dtb, dg, db)


class _Chunk:
    pass


def _gdn_chunk(q, k, v, gcol, bcol, grow8):
    C = CHUNK
    R = range(len(q))
    X = _Chunk()
    ri = lax.broadcasted_iota(jnp.int32, (C, C), 0)
    ci = lax.broadcasted_iota(jnp.int32, (C, C), 1)
    eye = (ri == ci).astype(f32)
    gam = list(gcol)
    gam_row = [grow8[h][0:1, 0:C] for h in R]
    X.ri, X.ci = ri, ci
    X.Dm = [jnp.where(ri >= ci, jnp.exp(jnp.minimum(gam[h][:, 0:C] - gam_row[h], 0.0)), 0.0) for h in R]
    X.eg = [jnp.exp(gam[h]) for h in R]
    gl = [gam[h][C - 1:C, :] for h in R]
    X.egl = [jnp.exp(gl[h]) for h in R]
    X.kdec = [jnp.exp(gl[h] - gam[h]) for h in R]
    X.qs = [q[h] * (DN_DK ** -0.5) for h in R]
    X.kb = [k[h] * bcol[h] for h in R]
    kk = [_dot_nt(X.kb[h], k[h]) for h in R]
    qk = [_dot_nt(X.qs[h], k[h]) for h in R]
    X.A = [jnp.where(ri > ci, kk[h] * X.Dm[h], 0.0) for h in R]
    assert C == 64
    b16 = jnp.right_shift(ri, 4) == jnp.right_shift(ci, 4)
    b32 = jnp.right_shift(ri, 5) == jnp.right_shift(ci, 5)
    P = [jnp.where(b16, X.A[h], 0.0) for h in R]
    T = [eye - P[h] for h in R]
    for _ in range(3):
        P = [_hdot(P[h], P[h]) for h in R]
        T = [T[h] + _hdot(T[h], P[h]) for h in R]
    for off in (jnp.logical_and(b32, jnp.logical_not(b16)), jnp.logical_not(b32)):
        AT = [_hdot(jnp.where(off, X.A[h], 0.0), T[h]) for h in R]
        T = [T[h] - _hdot(T[h], AT[h]) for h in R]
    X.T = T
    X.b2 = [jnp.concatenate([bcol[h], bcol[h]], axis=-1) for h in R]
    X.u = [_hdot(T[h], v[h] * X.b2[h]) for h in R]
    X.w = [_hdot(T[h], X.kb[h] * X.eg[h]) for h in R]
    X.attn = [qk[h] * X.Dm[h] for h in R]
    X.qg = [X.qs[h] * X.eg[h] for h in R]
    X.kg = [k[h] * X.kdec[h] for h in R]
    return X


def gdn_fwd(q, k, v, gc, bc, grow):
    tp = q.shape[0]
    nc = tp // CHUNK
    hb = GDN_HEADS_PER_STEP

    def body(q_ref, k_ref, v_ref, gc_ref, bc_ref, gr_ref, o_ref, ss_ref, S_ref):
        c = pl.program_id(1)

        @pl.when(c == 0)
        def _():
            S_ref[...] = jnp.zeros_like(S_ref)

        R = range(hb)
        qc = [slice(h * DN_DK, (h + 1) * DN_DK) for h in R]
        vc = [slice(h * DN_DV, (h + 1) * DN_DV) for h in R]
        X = _gdn_chunk([q_ref[:, qc[h]] for h in R], [k_ref[:, qc[h]] for h in R], [v_ref[:, vc[h]] for h in R],
                       [gc_ref[h] for h in R], [bc_ref[h] for h in R], [gr_ref[h] for h in R])
        S = [S_ref[h] for h in R]
        for h in R:
            ss_ref[h, 0] = S[h]
        wS = [_dot(X.w[h], S[h]) for h in R]
        qS = [_dot(X.qg[h], S[h]) for h in R]
        vn = [X.u[h] - wS[h] for h in R]
        av = [_dot(X.attn[h], vn[h]) for h in R]
        kv = [_dot_tn(X.kg[h], vn[h]) for h in R]
        for h in R:
            o_ref[:, vc[h]] = qS[h] + av[h]
            S_ref[h] = S[h] * X.egl[h][:, 0:1] + kv[h]

    qk = pl.BlockSpec((CHUNK, hb * DN_DK), lambda g, c: (c, g))
    vv = pl.BlockSpec((CHUNK, hb * DN_DV), lambda g, c: (c, g))
    col = pl.BlockSpec((hb, CHUNK, LANE), lambda g, c: (g, c, 0))
    row = pl.BlockSpec((hb, SUB, LANE), lambda g, c: (g, c, 0))
    return pl.pallas_call(
        body, name="gdn_fwd", grid=(HEADS // hb, nc), in_specs=[qk, qk, vv, col, col, row],
        out_specs=[vv, pl.BlockSpec((hb, 1, DN_DK, DN_DV), lambda g, c: (g, c, 0, 0))],
        out_shape=[jax.ShapeDtypeStruct((tp, DN_V), f32), jax.ShapeDtypeStruct((HEADS, nc, DN_DK, DN_DV), f32)],
        scratch_shapes=[pltpu.VMEM((hb, DN_DK, DN_DV), f32)], compiler_params=_params(2),
    )(q, k, v, gc, bc, grow)


def gdn_bwd(q, k, v, gc, bc, grow, states, do, rider=None):
    tp = q.shape[0]
    nc = tp // CHUNK
    C = CHUNK
    hb = GDN_HEADS_PER_STEP
    grid = (HEADS // hb, nc)
    split, ride_first, ride_last = _ride(rider, 8, 5, grid)

    def body(*refs):
        ((q_ref, k_ref, v_ref, gc_ref, bc_ref, gr_ref, ss_ref, do_ref), (dq_ref, dk_ref, dv_ref, dg_ref, db_ref),
         (rin, rout, rest)) = split(refs)
        dS_ref, ride = rest[0], (rin, rout, rest[1:])
        ride_first(ride)
        c = pl.program_id(1)

        @pl.when(c == 0)
        def _():
            dS_ref[...] = jnp.zeros_like(dS_ref)

        R = range(hb)
        qc = [slice(h * DN_DK, (h + 1) * DN_DK) for h in R]
        vc = [slice(h * DN_DV, (h + 1) * DN_DV) for h in R]
        k_ = [k_ref[:, qc[h]] for h in R]
        v_ = [v_ref[:, vc[h]] for h in R]
        bcol = [bc_ref[h] for h in R]
        X = _gdn_chunk([q_ref[:, qc[h]] for h in R], k_, v_, [gc_ref[h] for h in R], bcol, [gr_ref[h] for h in R])
        ri, ci = X.ri, X.ci
        S = [ss_ref[h, 0] for h in R]
        do_ = [do_ref[:, vc[h]] for h in R]
        dSn = [dS_ref[h] for h in R]
        wS = [_dot(X.w[h], S[h]) for h in R]
        ado = [_dot_tn(X.attn[h], do_[h]) for h in R]
        kdS = [_dot(X.kg[h], dSn[h]) for h in R]
        d_qg = [_dot_nt(do_[h], S[h]) for h in R]
        qdo = [_dot_tn(X.qg[h], do_[h]) for h in R]
        vn = [X.u[h] - wS[h] for h in R]
        d_vn = [ado[h] + kdS[h] for h in R]
        dovn = [_dot_nt(do_[h], vn[h]) for h in R]
        d_kg = [_dot_nt(vn[h], dSn[h]) for h in R]
        wdv = [_dot_tn(X.w[h], d_vn[h]) for h in R]
        dw = [-_dot_nt(d_vn[h], S[h]) for h in R]
        for h in R:
            dS_ref[h] = qdo[h] + X.egl[h][:, 0:1] * dSn[h] - wdv[h]
        dattn = [jnp.where(ri >= ci, dovn[h], 0.0) for h in R]
        dRu = [_hdot_tn(X.T[h], d_vn[h]) for h in R]
        dRw = [_hdot_tn(X.T[h], dw[h]) for h in R]
        dAu = [_hdot_nt(dRu[h], X.u[h]) for h in R]
        dAw = [_hdot_nt(dRw[h], X.w[h]) for h in R]
        dA = [jnp.where(ri > ci, -(dAu[h] + dAw[h]), 0.0) for h in R]
        dKK = [dA[h] * X.Dm[h] for h in R]
        dQK = [dattn[h] * X.Dm[h] for h in R]
        E = [dA[h] * X.A[h] + dattn[h] * X.attn[h] for h in R]
        dkb = [_dot(dKK[h], k_[h]) + dRw[h] * X.eg[h] for h in R]
        dk1 = [_dot_tn(dKK[h], X.kb[h]) for h in R]
        dqs = [_dot(dQK[h], k_[h]) + d_qg[h] * X.eg[h] for h in R]
        dk2 = [_dot_tn(dQK[h], X.qs[h]) for h in R]
        ones = jnp.ones((C, LANE), f32)
        colE = [_hdot_tn(E[h], ones) for h in R]
        rows = lax.broadcasted_iota(jnp.int32, (C, LANE), 0)
        dgam = []
        for h in R:
            t = d_kg[h] * X.kg[h]
            dgl = _allsum(t) + X.egl[h][:, 0:1] * _allsum(S[h] * dSn[h])
            g = (_rowsum(E[h]) - colE[h] + _rowsum(dRw[h] * (X.kb[h] * X.eg[h])) + _rowsum(d_qg[h] * X.qg[h])
                 - _rowsum(t))
            dgam.append(g + jnp.where(rows == C - 1, dgl, 0.0))
        for h in R:
            dv_ref[:, vc[h]] = dRu[h] * X.b2[h]
            dbeta = _rowsum(dRu[h] * v_[h]) + _rowsum(dkb[h] * k_[h])
            dq_ref[:, qc[h]] = dqs[h] * (DN_DK ** -0.5)
            dk_ref[:, qc[h]] = dk1[h] + dk2[h] + dkb[h] * bcol[h] + d_kg[h] * X.kdec[h]
            dg_ref[h] = dgam[h]
            db_ref[h] = jnp.broadcast_to(dbeta, (C, LANE))
        ride_last(ride)

    rc = lambda c: nc - 1 - c
    qk = pl.BlockSpec((CHUNK, hb * DN_DK), lambda g, c: (rc(c), g))
    vv = pl.BlockSpec((CHUNK, hb * DN_DV), lambda g, c: (rc(c), g))
    col = pl.BlockSpec((hb, CHUNK, LANE), lambda g, c: (g, rc(c), 0))
    row = pl.BlockSpec((hb, SUB, LANE), lambda g, c: (g, rc(c), 0))
    st = pl.BlockSpec((hb, 1, DN_DK, DN_DV), lambda g, c: (g, rc(c), 0, 0))
    r_ins = rider.ins if rider else []
    r_outs = rider.out_shapes if rider else []
    res = pl.pallas_call(
        body, name="gdn_bwd", grid=grid, in_specs=[qk, qk, vv, col, col, row, st, vv] + [_ANY] * len(r_ins),
        out_specs=[qk, qk, vv, col, col] + [_ANY] * len(r_outs),
        out_shape=[jax.ShapeDtypeStruct((tp, DN_QK), f32), jax.ShapeDtypeStruct((tp, DN_QK), f32),
                   jax.ShapeDtypeStruct((tp, DN_V), f32), jax.ShapeDtypeStruct((HEADS, tp, LANE), f32),
                   jax.ShapeDtypeStruct((HEADS, tp, LANE), f32)] + list(r_outs),
        scratch_shapes=[pltpu.VMEM((hb, DN_DK, DN_DV), f32)] + (rider.scratch if rider else []),
        compiler_params=_params(2),
    )(q, k, v, gc, bc, grow, states, do, *r_ins)
    return res[:5], res[5:]


def _cumsum_after(x, nb, us, pieces=2):
    B, n = SB_BLOCK, x.shape[0]
    hi = x.astype(bf16)
    parts = (hi, (x - hi.astype(f32)).astype(bf16)) if pieces == 2 else (hi,)
    rows = [p[:, b * B:(b + 1) * B] for p in parts for b in range(nb)]
    r = jnp.dot(jnp.concatenate(rows, axis=0), us, preferred_element_type=f32)
    out = [r[b * n:(b + 1) * n] for b in range(nb)]
    if pieces == 2:
        out = [out[b] + r[(nb + b) * n:(nb + b + 1) * n] for b in range(nb)]
    return out[0] if nb == 1 else jnp.concatenate(out, axis=1)


def _later_blocks(x, nb, carry):
    B = SB_BLOCK
    tot = [_rowsum(x[:, b * B:(b + 1) * B]) for b in range(nb)]
    offs = [None] * nb
    run = carry
    for b in range(nb - 1, -1, -1):
        offs[b] = jnp.broadcast_to(run, (x.shape[0], B))
        run = run + tot[b]
    return (offs[0] if nb == 1 else jnp.concatenate(offs, axis=1)), run


def _sb_group(i, t):
    top = (i + 1) * (SB_QB // SB_BLOCK) - 1 - SB_GROUP * t
    jlo = jnp.maximum(top - SB_GROUP + 1, 0)
    rows = pl.ds(pl.multiple_of(jlo * SB_BLOCK, SB_BLOCK), SB_GROUP * SB_BLOCK)
    return jlo, rows, (top + 1) * SB_BLOCK


def _sb_weights(q, kcat, i, jlo, kend, cs, us, masked):
    B, nb = SB_BLOCK, SB_GROUP
    R = range(len(q))
    z = [_dot_nt(q[h], kcat[h]) * (SB_DH ** -0.5) for h in R]
    e = [jnp.exp(-jnp.abs(z[h])) for h in R]
    l1p = [jnp.log(1.0 + e[h]) for h in R]
    lsp = [jnp.minimum(z[h], 0.0) - l1p[h] for h in R]
    lk = [lsp[h] - z[h] for h in R]
    vis = None
    if masked:
        qpos = i * SB_QB + lax.broadcasted_iota(jnp.int32, (SB_QB, nb * B), 0)
        kpos = jlo * B + lax.broadcasted_iota(jnp.int32, (SB_QB, nb * B), 1)
        vis = jnp.logical_and(kpos < jnp.minimum(qpos, kend), kpos >= P0)
        lk = [jnp.where(vis, lk[h], 0.0) for h in R]
    later = [_later_blocks(lk[h], nb, cs[h]) for h in R]
    cum = [_cumsum_after(lk[h], nb, us) for h in R]
    w = [jnp.exp(lsp[h] + cum[h] + later[h][0]) for h in R]
    if masked:
        w = [jnp.where(vis, w[h], 0.0) for h in R]
    return lsp, vis, w, [later[h][1] for h in R]


def _sb_loop(i, step, carry):
    trips = ((i + 1) * (SB_QB // SB_BLOCK) - 1 + SB_GROUP) // SB_GROUP
    carry = step(True)(0, carry)
    carry = lax.fori_loop(1, trips - 1, step(False), carry)
    return lax.fori_loop(jnp.maximum(trips - 1, 1), trips, step(True), carry)


def _ride(rider, n_in, n_out, grid):
    n_rin = len(rider.ins) if rider else 0
    n_rout = len(rider.out_shapes) if rider else 0

    def split(refs):
        ins, rin = refs[:n_in], refs[n_in:n_in + n_rin]
        outs = refs[n_in + n_rin:n_in + n_rin + n_out]
        rout = refs[n_in + n_rin + n_out:n_in + n_rin + n_out + n_rout]
        return ins, outs, (rin, rout, refs[n_in + n_rin + n_out + n_rout:])

    def at(step, fn, r):
        if rider is None:
            return
        cond = None
        for a, g in enumerate(grid):
            c = pl.program_id(a) == (g - 1 if step == "last" else 0)
            cond = c if cond is None else jnp.logical_and(cond, c)

        @pl.when(cond)
        def _():
            fn(*r)

    first = lambda r: at("first", rider.start if rider else None, r)
    last = lambda r: at("last", rider.finish if rider else None, r)
    return split, first, last


def sb_fwd(qs, ks, vs, rider=None):
    tp = qs.shape[0]
    nq = tp // SB_QB
    B, G, hb, QB = SB_BLOCK, SB_GROUP, SB_FWD_HEADS_PER_STEP, SB_QB
    assert tp >= G * B and tp % QB == 0 and QB % B == 0 and G * B >= QB
    grid = (HEADS // hb, nq)
    split, ride_first, ride_last = _ride(rider, 3, 2, grid)

    def body(*refs):
        (q_ref, k_ref, v_ref), (o_ref, ob_ref), ride = split(refs)
        ride_first(ride)
        i = pl.program_id(1)
        R = range(hb)
        hs = [slice(h * SB_DH, (h + 1) * SB_DH) for h in R]
        q = [q_ref[:, hs[h]] for h in R]
        us = (lax.broadcasted_iota(jnp.int32, (B, B), 0) > lax.broadcasted_iota(jnp.int32, (B, B), 1)).astype(bf16)

        def make_step(masked):
            def step(t, carry):
                acc, cs = carry
                jlo, rows, kend = _sb_group(i, t)
                _, _, w, cs = _sb_weights(q, [k_ref[rows, hs[h]] for h in R], i, jlo, kend, cs, us, masked)
                pv = [_dot(w[h], v_ref[rows, hs[h]]) for h in R]
                return tuple(acc[h] + pv[h] for h in R), tuple(cs)
            return step

        carry = (tuple(jnp.zeros((QB, SB_DH), f32) for _ in R), tuple(jnp.zeros((QB, 1), f32) for _ in R))
        acc, _ = _sb_loop(i, make_step, carry)
        for h in R:
            o_ref[:, hs[h]] = acc[h]
            ob_ref[:, hs[h]] = acc[h].astype(bf16)
        ride_last(ride)

    blk = pl.BlockSpec((QB, hb * SB_DH), lambda g, i: (i, g))
    full = pl.BlockSpec((tp, hb * SB_DH), lambda g, i: (0, g))
    r_ins = rider.ins if rider else []
    r_outs = rider.out_shapes if rider else []
    res = pl.pallas_call(
        body, name="sb_fwd", grid=grid, in_specs=[blk, full, full] + [_ANY] * len(r_ins),
        out_specs=[blk, blk] + [_ANY] * len(r_outs),
        out_shape=[jax.ShapeDtypeStruct((tp, SB_W), f32), jax.ShapeDtypeStruct((tp, SB_W), bf16)] + list(r_outs),
        scratch_shapes=rider.scratch if rider else [], compiler_params=_params(2),
    )(qs, ks, vs, *r_ins)
    return res[0], res[1], res[2:]


def sb_bwd(qs, ks, vs, o, do, rider=None):
    tp = qs.shape[0]
    nq = tp // SB_QB
    B, G, hb, QB = SB_BLOCK, SB_GROUP, SB_HEADS_PER_STEP, SB_QB
    assert tp >= G * B and tp % QB == 0 and QB % B == 0 and G * B >= QB
    grid = (HEADS // hb, nq)
    split, ride_first, ride_last = _ride(rider, 5, 3, grid)

    def body(*refs):
        (q_ref, k_ref, v_ref, o_ref, do_ref), (dq_ref, dk_ref, dv_ref), ride = split(refs)
        ride_first(ride)
        i = pl.program_id(1)

        @pl.when(i == 0)
        def _():
            dk_ref[...] = jnp.zeros_like(dk_ref)
            dv_ref[...] = jnp.zeros_like(dv_ref)

        R = range(hb)
        hs = [slice(h * SB_DH, (h + 1) * SB_DH) for h in R]
        q = [q_ref[:, hs[h]] for h in R]
        dob = [do_ref[:, hs[h]].astype(bf16) for h in R]
        et = [_rowsum(dob[h].astype(f32) * o_ref[:, hs[h]]) for h in R]
        us = (lax.broadcasted_iota(jnp.int32, (B, B), 0) > lax.broadcasted_iota(jnp.int32, (B, B), 1)).astype(bf16)

        def make_step(masked):
            def step(t, carry):
                dq, cs, ce = carry
                jlo, rows, kend = _sb_group(i, t)
                kcat = [k_ref[rows, hs[h]] for h in R]
                dwv = [_dot_nt(dob[h], v_ref[rows, hs[h]]) for h in R]
                lsp, vis, w, cs = _sb_weights(q, kcat, i, jlo, kend, cs, us, masked)
                wb = [w[h].astype(bf16) for h in R]
                ee = [dwv[h] * wb[h].astype(f32) for h in R]
                later = [_later_blocks(ee[h], G, ce[h]) for h in R]
                cum = [_cumsum_after(ee[h], G, us) for h in R]
                dz = []
                for h in R:
                    d = ee[h] - jnp.exp(lsp[h]) * (et[h] - (cum[h] + later[h][0]))
                    if masked:
                        d = jnp.where(vis, d, 0.0)
                    dz.append(d.astype(bf16))
                dkj = [_dot_tn(dz[h], q[h]) for h in R]
                dvj = [_dot_tn(wb[h], dob[h]) for h in R]
                dqj = [_dot(dz[h], kcat[h]) for h in R]
                for h in R:
                    dk_ref[rows, hs[h]] += dkj[h]
                    dv_ref[rows, hs[h]] += dvj[h]
                return tuple(dq[h] + dqj[h] for h in R), tuple(cs), tuple(later[h][1] for h in R)
            return step

        z0 = tuple(jnp.zeros((QB, 1), f32) for _ in R)
        dq, _, _ = _sb_loop(i, make_step, (tuple(jnp.zeros((QB, SB_DH), f32) for _ in R), z0, z0))
        for h in R:
            dq_ref[:, hs[h]] = dq[h] * (SB_DH ** -0.5)
        ride_last(ride)

    blk = pl.BlockSpec((QB, hb * SB_DH), lambda g, i: (i, g))
    full = pl.BlockSpec((tp, hb * SB_DH), lambda g, i: (0, g))
    r_ins = rider.ins if rider else []
    r_outs = rider.out_shapes if rider else []
    res = pl.pallas_call(
        body, name="sb_bwd", grid=grid, in_specs=[blk, full, full, blk, blk] + [_ANY] * len(r_ins),
        out_specs=[blk, full, full] + [_ANY] * len(r_outs),
        out_shape=[jax.ShapeDtypeStruct((tp, SB_W), f32)] * 3 + list(r_outs),
        scratch_shapes=rider.scratch if rider else [], compiler_params=_params(2),
    )(qs, ks, vs, o, do, *r_ins)
    return res[:3], res[3:]


def adamw(w, g, m, v, name):
    r, c = w.shape
    rt = _tile(r, 128, SUB) if r % SUB == 0 else r
    blk = pl.BlockSpec((rt, c), lambda i: (i, 0))
    c1 =1.0 - ADAM_B1 ** ADAM_STEP
    c2 = 1.0 - ADAM_B2 ** ADAM_STEP

    def body(w_ref, g_ref, m_ref, v_ref, d_ref, mo_ref, vo_ref):
        g_ = g_ref[...]
        m_ = ADAM_B1 * m_ref[...] + (1.0 - ADAM_B1) * g_
        v_ = ADAM_B2 * v_ref[...] + (1.0 - ADAM_B2) * (g_ * g_)
        mo_ref[...] = m_
        vo_ref[...] = v_
        d_ref[...] = -ADAM_LR * ((m_ / c1) / (jnp.sqrt(v_ / c2) + ADAM_EPS) + ADAM_WD * w_ref[...])

    return pl.pallas_call(
        body, name=name, grid=(r // rt,), in_specs=[blk] * 4, out_specs=[blk] * 3,
        out_shape=[jax.ShapeDtypeStruct((r, c), f32)] * 3, compiler_params=_params(1),
    )(w, g, m, v)


def sum_slots(x, name):
    n, r, c = x.shape
    rt = _tile(r, 128, SUB) if r % SUB == 0 else r
    blk = pl.BlockSpec((n, rt, c), lambda i: (0, i, 0))

    def body(x_ref, o_ref):
        acc = x_ref[0].astype(f32)
        for s in range(1, n):
            acc = acc + x_ref[s].astype(f32)
        o_ref[...] = acc

    return pl.pallas_call(
        body, name=name, grid=(r // rt,), in_specs=[blk], out_specs=pl.BlockSpec((rt, c), lambda i: (i, 0)),
        out_shape=jax.ShapeDtypeStruct((r, c), f32), compiler_params=_params(1),
    )(x)


def add2(a, b, name, out_dtype=f32):
    n, r, c = a.shape
    rt = _tile(r, 64, SUB) if r % SUB == 0 else r
    blk = pl.BlockSpec((n, rt, c), lambda i: (0, i, 0))

    def body(a_ref, b_ref, o_ref):
        o_ref[...] = (a_ref[...] + b_ref[...]).astype(out_dtype)

    return pl.pallas_call(
        body, name=name, grid=(r // rt,), in_specs=[blk, blk], out_specs=blk,
        out_shape=jax.ShapeDtypeStruct((n, r, c), out_dtype), compiler_params=_params(1),
    )(a, b)


_ANY = pl.BlockSpec(memory_space=pl.ANY)
_MESH = pl.DeviceIdType.MESH


def _coords():
    return lax.axis_index("x"), lax.axis_index("y"), lax.axis_index("c")


def _chip_peer(x, y, r):
    return x ^ (r >> 1), y ^ (r & 1)


class _Exchange:
    def __init__(self, ins, out_shapes, scratch, start, finish):
        self.ins, self.out_shapes, self.scratch, self.start, self.finish = ins, out_shapes, scratch, start, finish

    def split(self, refs):
        n, m = len(self.ins), len(self.out_shapes)
        return refs[:n], refs[n:n + m], refs[n + m:]


def run_exchange(ex, name):
    def body(*refs):
        ins, outs, sems = ex.split(refs)
        ex.start(ins, outs, sems)
        ex.finish(ins, outs, sems)

    return pl.pallas_call(body, name=name, in_specs=[_ANY] * len(ex.ins), out_specs=[_ANY] * len(ex.out_shapes),
                          out_shape=ex.out_shapes, scratch_shapes=ex.scratch)(*ex.ins)


def gather_chips(big, small):
    nb, n = len(big), len(big) + len(small)
    shards = list(big) + list(small)
    kb = nb * (N_CHIPS - 1)
    k = n * (N_CHIPS - 1)

    def copies(src, dst, sems):
        send, recv, fsend, frecv = sems
        x, y, c = _coords()
        sib = (x, y, 1 - c)
        peers = [_chip_peer(x, y, r) for r in range(1, N_CHIPS)]

        def direct(t, j, slot):
            s = t * (N_CHIPS - 1) + j
            if t < nb:
                return pltpu.make_async_remote_copy(src[t].at[c], dst[t].at[slot, c], send.at[s], recv.at[s],
                                                    device_id=(*peers[j], c), device_id_type=_MESH)
            return pltpu.make_async_remote_copy(src[t], dst[t].at[slot], send.at[s], recv.at[s],
                                                device_id=(*peers[j], c), device_id_type=_MESH)

        def passed(t, j, half):
            s = t * (N_CHIPS - 1) + j
            px, py = peers[j]
            part = dst[t].at[2 * px + py, half]
            return pltpu.make_async_remote_copy(part, part, fsend.at[s], frecv.at[s], device_id=sib, device_id_type=_MESH)

        return direct, passed, peers, 2 * x + y, c

    def start(src, dst, sems):
        direct, _, _, me, _ = copies(src, dst, sems)
        for t in range(n):
            for j in range(N_CHIPS - 1):
                direct(t, j, me).start()

    def finish(src, dst, sems):
        direct, passed, peers, me, c = copies(src, dst, sems)
        fwd = []
        for t in range(nb):
            for j in range(N_CHIPS - 1):
                px, py = peers[j]
                direct(t, j, 2 * px + py).wait_recv()
                fwd.append(passed(t, j, c))
                fwd[-1].start()
        for t in range(nb, n):
            for j in range(N_CHIPS - 1):
                px, py = peers[j]
                direct(t, j, 2 * px + py).wait_recv()
        for t in range(nb):
            for j in range(N_CHIPS - 1):
                passed(t, j, 1 - c).wait_recv()
        for t in range(n):
            for j in range(N_CHIPS - 1):
                direct(t, j, me).wait_send()
        for cp in fwd:
            cp.wait_send()

    return _Exchange(shards, [jax.ShapeDtypeStruct((N_CHIPS,) + s.shape, s.dtype) for s in shards],
                     [pltpu.SemaphoreType.DMA((k,)), pltpu.SemaphoreType.DMA((k,)),
                      pltpu.SemaphoreType.DMA((max(kb, 1),)), pltpu.SemaphoreType.DMA((max(kb, 1),))], start, finish)


def sibling_swap(grads):
    pairs = [(t, o) for t, g in enumerate(grads) for o in range(g.shape[0])]
    k = len(pairs)

    def copies(src, dst, sems):
        send, recv = sems
        x, y, c = _coords()
        return [pltpu.make_async_remote_copy(src[t].at[o, 1 - c], dst[t].at[o], send.at[s], recv.at[s],
                                             device_id=(x, y, 1 - c), device_id_type=_MESH)
                for s, (t, o) in enumerate(pairs)]

    def start(src, dst, sems):
        for cp in copies(src, dst, sems):
            cp.start()

    def finish(src, dst, sems):
        cps = copies(src, dst, sems)
        for cp in cps:
            cp.wait_recv()
        for cp in cps:
            cp.wait_send()

    return _Exchange(list(grads), [jax.ShapeDtypeStruct((g.shape[0],) + g.shape[2:], g.dtype) for g in grads],
                     [pltpu.SemaphoreType.DMA((k,)), pltpu.SemaphoreType.DMA((k,))], start, finish)


def scatter_chips(parts):
    n = len(parts)
    k = n * (N_CHIPS - 1)

    def copy(src, dst, sems, t, r, landing):
        send, recv = sems
        x, y, c = _coords()
        me = 2 * x + y
        px, py = _chip_peer(x, y, r)
        peer = 2 * px + py
        s = t * (N_CHIPS - 1) + r - 1
        return pltpu.make_async_remote_copy(src[t].at[me if landing else peer], dst[t].at[peer if landing else me],
                                            send.at[s], recv.at[s], device_id=(px, py, c), device_id_type=_MESH)

    def start(src, dst, sems):
        for t in range(n):
            for r in range(1, N_CHIPS):
                copy(src, dst, sems, t, r, False).start()

    def finish(src, dst, sems):
        for t in range(n):
            for r in range(1, N_CHIPS):
                copy(src, dst, sems, t, r, True).wait_recv()
        for t in range(n):
            for r in range(1, N_CHIPS):
                copy(src, dst, sems, t, r, False).wait_send()

    return _Exchange(list(parts), [jax.ShapeDtypeStruct(p.shape, p.dtype) for p in parts],
                     [pltpu.SemaphoreType.DMA((k,)), pltpu.SemaphoreType.DMA((k,))], start, finish)


def sibling_send(halves, name):
    n = len(halves)

    def body(*refs):
        src, dst = refs[:n], refs[n:2 * n]
        send, recv = refs[2 * n:]
        x, y, c = _coords()
        cps = [pltpu.make_async_remote_copy(src[t], dst[t], send.at[t], recv.at[t],
                                            device_id=(x, y, 1 - c), device_id_type=_MESH) for t in range(n)]
        for cp in cps:
            cp.start()
        for cp in cps:
            cp.wait_recv()
        for cp in cps:
            cp.wait_send()

    return pl.pallas_call(
        body, name=name, in_specs=[_ANY] * n, out_specs=[_ANY] * n,
        out_shape=[jax.ShapeDtypeStruct(h.shape, h.dtype) for h in halves],
        scratch_shapes=[pltpu.SemaphoreType.DMA((n,)), pltpu.SemaphoreType.DMA((n,))],
    )(*halves)


def gather_all(block, name):
    def body(src, dst, send, recv, loc):
        x, y, c = _coords()
        me = 4 * x + 2 * y + c
        mine = pltpu.make_async_copy(src, dst.at[me], loc)
        mine.start()
        outs = []
        for r in range(1, N_DEV):
            peer = (x ^ (r >> 2), y ^ ((r >> 1) & 1), c ^ (r & 1))
            outs.append(pltpu.make_async_remote_copy(src, dst.at[me], send.at[r - 1], recv.at[r - 1],
                                                     device_id=peer, device_id_type=_MESH))
        for cp in outs:
            cp.start()
        for r in range(1, N_DEV):
            px, py, pc = x ^ (r >> 2), y ^ ((r >> 1) & 1), c ^ (r & 1)
            pltpu.make_async_remote_copy(src, dst.at[4 * px + 2 * py + pc], send.at[r - 1], recv.at[r - 1],
                                         device_id=(px, py, pc), device_id_type=_MESH).wait_recv()
        for cp in outs:
            cp.wait_send()
        mine.wait()

    return pl.pallas_call(
        body, name=name, in_specs=[_ANY], out_specs=_ANY,
        out_shape=jax.ShapeDtypeStruct((N_DEV,) + block.shape, block.dtype),
        scratch_shapes=[pltpu.SemaphoreType.DMA((N_DEV - 1,)), pltpu.SemaphoreType.DMA((N_DEV - 1,)),
                        pltpu.SemaphoreType.DMA(())],
    )(block)


def _pad_lanes(v, n=LANE):
    return jnp.pad(v, ((0, 0), (0, n - v.shape[1])))


def _w_in_pieces():
    cs = (PROJ_BIG + 2 * HEADS) // N_CHIPS
    ab_end = AB_COL + 2 * HEADS
    out = []
    for o in range(N_CHIPS):
        lo, hi = o * cs, (o + 1) * cs
        cand = [("big", lo, min(hi, AB_COL), 0), ("ab", max(lo, AB_COL), min(hi, ab_end), AB_COL),
                ("big", max(lo, ab_end), hi, 2 * HEADS)]
        out.append([(s, a - off, b - off) for s, a, b, off in cand if a < b])
    return out


def _split_w_in(w4):
    big, ab = [], []
    for o, pieces in enumerate(_w_in_pieces()):
        at = 0
        for s, a, b in pieces:
            (big if s == "big" else ab).append(w4[o][:, at:at + b - a])
            at += b - a
    return jnp.concatenate(big, axis=1), _pad_lanes(jnp.concatenate(ab, axis=1))


def _join_w_in(big, ab):
    src = {"big": big, "ab": ab}
    return jnp.stack([jnp.concatenate([src[s][:, a:b] for s, a, b in pieces], axis=1) for pieces in _w_in_pieces()])


def _conv_w8(w):
    return jnp.pad(w, ((0, SUB - DN_CONV), (0, 0)))


def _row_layout(gc, tp):
    nc = tp // CHUNK
    g = gc[:, :, 0].reshape(HEADS, nc, 1, CHUNK)
    g = jnp.broadcast_to(g, (HEADS, nc, SUB, CHUNK))
    return jnp.pad(g, ((0, 0), (0, 0), (0, 0), (0, LANE - CHUNK))).reshape(HEADS, nc * SUB, LANE)


def _step(x, meta, W, target, late_weights=None, early_swap=None, early_grads=None, last_grads=None):
    W = dict(W)
    seq = x.shape[0]
    tp = P0 + N_META + seq
    h0 = jnp.concatenate([jnp.zeros((P0, D_MODEL), f32), meta, x], axis=0)
    w_big, w_ab = _split_w_in(W["w_in"])
    cq8, ck8, cv8 = _conv_w8(W["conv_q"]), _conv_w8(W["conv_k"]), _conv_w8(W["conv_v"])
    al, dtb = _pad_lanes(W["dn_a_log"]), _pad_lanes(W["dn_dt_bias"])

    n1 = rms_fwd(h0, W["norm_mix_gain"], "rms1_fwd")
    proj = matmul(n1, w_big, "nn", "proj_fwd")
    pab = matmul(n1, w_ab, "nn", "pab_fwd")
    qn = conv_fwd(proj, cq8, C_DQ * 8, 8, True, "conv_q_fwd")
    kn = conv_fwd(proj, ck8, C_DK * 8, 8, True, "conv_k_fwd")
    va = conv_fwd(proj, cv8, C_DV * 8, 16, False, "conv_v_fwd")
    gc, bc = ab_fwd(pab, al, dtb)
    grow = _row_layout(gc, tp)
    o_dn, states = gdn_fwd(qn, kn, va, gc, bc, grow)
    on = dn_out_fwd(o_dn, proj, W["dn_out_norm_gain"])
    qs, ks, vs = sb_prep_fwd(proj, W["sb_q_norm_gain"], W["sb_k_norm_gain"])
    o_sb, o_sb16, arrived = sb_fwd(qs, ks, vs, rider=late_weights[0] if late_weights else None)
    if late_weights:
        W.update(late_weights[1](arrived))
    ydn = matmul(on, W["w_branch_dn"], "nn", "ydn_fwd")
    ysb = matmul(o_sb16, W["w_branch_sb"], "nn", "ysb_fwd")
    merged = merge_fwd(proj, ydn, ysb)
    h1 = matmul(merged, W["w_out"], "nn", "wout_fwd", residual=h0)
    n2 = rms_fwd(h1, W["norm_ffn_gain"], "rms2_fwd")
    u = matmul(n2, W["w_ffn_in"], "nn", "ffn_in_fwd", tn_t=512)
    act = swiglu_fwd(u)
    y = matmul(act, W["w_ffn_out"], "nn", "ffn_out_fwd", residual=h1)
    dy, dy16, loss = loss_head(y, target)

    G = {}
    dact = matmul(dy16, W["w_ffn_out"], "nt", "ffn_out_dx", tn_t=1408)
    G["w_ffn_out"] = matmul(act, dy16, "tn", "ffn_out_dw", tm_t=1408)
    dgate, dup = swiglu_bwd(u, dact)
    du = jnp.concatenate([dgate, dup], axis=1)
    dn2 = matmul(du, W["w_ffn_in"], "nt", "ffn_in_dx", tk_t=512)
    G["w_ffn_in"] = matmul(n2, du, "tn", "ffn_in_dw", tn_t=512)
    dh1, dh1_16, G["norm_ffn_gain"] = rms_bwd(h1, W["norm_ffn_gain"], dn2, dy, "rms2_bwd")
    dmerged = matmul(dh1_16, W["w_out"], "nt", "wout_dx")
    G["w_out"] = matmul(merged, dh1_16, "tn", "wout_dw")
    dyd, dys, d_gates = merge_bwd(proj, ydn, ysb, dmerged)
    don = matmul(dyd, W["w_branch_dn"], "nt", "ydn_dx")
    G["w_branch_dn"] = matmul(on, dyd, "tn", "ydn_dw")
    do_sb = matmul(dys, W["w_branch_sb"], "nt", "ysb_dx")
    G["w_branch_sb"] = matmul(o_sb16, dys, "tn", "ysb_dw")
    do_dn, dproj, G["dn_out_norm_gain"] = dn_out_bwd(o_dn, proj, W["dn_out_norm_gain"], don)
    (dqn, dkn, dva, dgc, dbc), swapped = gdn_bwd(qn, kn, va, gc, bc, grow, states, do_dn,
                                                 rider=early_swap[0](G) if early_swap else None)
    if early_swap:
        early_swap[1](swapped)
    dpab, dal, ddt = ab_bwd(pab, al, dtb, dgc, dbc)
    G["dn_a_log"], G["dn_dt_bias"] = dal[:, :HEADS], ddt[:, :HEADS]
    dyq, dcq = conv_bwd_act(proj, cq8, dqn, C_DQ * 8, 8, True, "conv_q_bwd")
    dyk, dck = conv_bwd_act(proj, ck8, dkn, C_DK * 8, 8, True, "conv_k_bwd")
    dyv, dcv = conv_bwd_act(proj, cv8, dva, C_DV * 8, 16, False, "conv_v_bwd")
    G["conv_q"], G["conv_k"], G["conv_v"] = dcq[:DN_CONV], dck[:DN_CONV], dcv[:DN_CONV]
    dproj = conv_bwd_in(dyq, cq8, dproj, C_DQ * 8, "conv_q_dx")
    dproj = conv_bwd_in(dyk, ck8, dproj, C_DK * 8, "conv_k_dx")
    dproj = conv_bwd_in(dyv, cv8, dproj, C_DV * 8, "conv_v_dx")
    (dqs, dks, dvs), delivered = sb_bwd(qs, ks, vs, o_sb, do_sb, rider=early_grads[0](G) if early_grads else None)
    if early_grads:
        early_grads[1](delivered)
    dproj, G["sb_q_norm_gain"], G["sb_k_norm_gain"] = sb_prep_bwd(
        proj, W["sb_q_norm_gain"], W["sb_k_norm_gain"], dqs, dks, dvs, dproj)
    dproj = lax.dynamic_update_slice(dproj, d_gates, (0, C_GDN * 1024))
    dw_big = matmul(n1, dproj, "tn", "proj_dw")
    dw_ab = matmul(n1, dpab, "tn", "pab_dw")
    G["w_in"] = (dw_big, dw_ab)
    if last_grads:
        dn1, delivered = matmul(dproj, w_big, "nt", "proj_dx", tk_t=1024, rider=last_grads[0](G))
        last_grads[1](delivered)
    else:
        dn1 = matmul(dproj, w_big, "nt", "proj_dx", tk_t=1024)
    dn1 = matmul(dpab, w_ab, "nt", "pab_dx", residual=dn1)
    dh0, _, G["norm_mix_gain"] = rms_bwd(h0, W["norm_mix_gain"], dn1, dh1, "rms1_bwd")
    G["meta_tokens"] = dh0[P0:P0 + N_META]
    return loss, dh0[P0 + N_META:], G


_BIG = ("w_in", "w_branch_dn", "w_branch_sb", "w_out", "w_ffn_in", "w_ffn_out")
_COL_SHARDED = ("w_in", "w_ffn_in", "meta_tokens", "conv_q", "conv_k", "conv_v")
_SMALL_REPL = ("norm_mix_gain", "norm_ffn_gain", "dn_a_log", "dn_dt_bias", "dn_out_norm_gain", "sb_q_norm_gain",
               "sb_k_norm_gain")
_SMALL_SHARD = ("meta_tokens", "conv_q", "conv_k", "conv_v")
_ORDER = ("meta_tokens", "norm_mix_gain", "w_in", "conv_q", "conv_k", "conv_v", "dn_a_log", "dn_dt_bias",
          "dn_out_norm_gain", "sb_q_norm_gain", "sb_k_norm_gain", "w_branch_dn", "w_branch_sb", "w_out",
          "norm_ffn_gain", "w_ffn_in", "w_ffn_out")


def _unshard(g4, name):
    if name in _COL_SHARDED:
        r, cs = g4.shape[1:]
        return jnp.transpose(g4, (1, 0, 2)).reshape(r, N_CHIPS * cs)
    return g4.reshape((-1,) + g4.shape[2:])


def _to_shards(full, name):
    if name in _COL_SHARDED:
        r, c = full.shape
        return jnp.transpose(full.reshape(r, N_CHIPS, c // N_CHIPS), (1, 0, 2))
    r, c = full.shape
    return full.reshape(N_CHIPS, r // N_CHIPS, c)


def _rows_1024(a):
    r, c = a.shape
    if c >= 1024:
        return a.reshape(r * (c // 1024), 1024)
    return jnp.pad(a, ((0, 0), (0, 1024 - c)))


def kernel(x, meta_tokens, norm_mix_gain, w_in, conv_q, conv_k, conv_v, dn_a_log, dn_dt_bias, dn_out_norm_gain, sb_q_norm_gain, sb_k_norm_gain, w_branch_dn, w_branch_sb, w_out, norm_ffn_gain, w_ffn_in, w_ffn_out, loss_target, m_meta_tokens, m_norm_mix_gain, m_w_in, m_conv_q, m_conv_k, m_conv_v, m_dn_a_log, m_dn_dt_bias, m_dn_out_norm_gain, m_sb_q_norm_gain, m_sb_k_norm_gain, m_w_branch_dn, m_w_branch_sb, m_w_out, m_norm_ffn_gain, m_w_ffn_in, m_w_ffn_out, v_meta_tokens, v_norm_mix_gain, v_w_in, v_conv_q, v_conv_k, v_conv_v, v_dn_a_log, v_dn_dt_bias, v_dn_out_norm_gain, v_sb_q_norm_gain, v_sb_k_norm_gain, v_w_branch_dn, v_w_branch_sb, v_w_out, v_norm_ffn_gain, v_w_ffn_in, v_w_ffn_out):
    Wl = dict(meta_tokens=meta_tokens, norm_mix_gain=norm_mix_gain, w_in=w_in[0], conv_q=conv_q[0], conv_k=conv_k[0],
              conv_v=conv_v[0], dn_a_log=dn_a_log, dn_dt_bias=dn_dt_bias, dn_out_norm_gain=dn_out_norm_gain,
              sb_q_norm_gain=sb_q_norm_gain, sb_k_norm_gain=sb_k_norm_gain, w_branch_dn=w_branch_dn[0],
              w_branch_sb=w_branch_sb[0], w_out=w_out[0], norm_ffn_gain=norm_ffn_gain, w_ffn_in=w_ffn_in[0],
              w_ffn_out=w_ffn_out[0])
    Ml = dict(meta_tokens=m_meta_tokens, norm_mix_gain=m_norm_mix_gain, w_in=m_w_in[0], conv_q=m_conv_q[0],
              conv_k=m_conv_k[0], conv_v=m_conv_v[0], dn_a_log=m_dn_a_log, dn_dt_bias=m_dn_dt_bias,
              dn_out_norm_gain=m_dn_out_norm_gain, sb_q_norm_gain=m_sb_q_norm_gain, sb_k_norm_gain=m_sb_k_norm_gain,
              w_branch_dn=m_w_branch_dn[0], w_branch_sb=m_w_branch_sb[0], w_out=m_w_out[0],
              norm_ffn_gain=m_norm_ffn_gain, w_ffn_in=m_w_ffn_in[0], w_ffn_out=m_w_ffn_out[0])
    Vl = dict(meta_tokens=v_meta_tokens, norm_mix_gain=v_norm_mix_gain, w_in=v_w_in[0], conv_q=v_conv_q[0],
              conv_k=v_conv_k[0], conv_v=v_conv_v[0], dn_a_log=v_dn_a_log, dn_dt_bias=v_dn_dt_bias,
              dn_out_norm_gain=v_dn_out_norm_gain, sb_q_norm_gain=v_sb_q_norm_gain, sb_k_norm_gain=v_sb_k_norm_gain,
              w_branch_dn=v_w_branch_dn[0], w_branch_sb=v_w_branch_sb[0], w_out=v_w_out[0],
              norm_ffn_gain=v_norm_ffn_gain, w_ffn_in=v_w_ffn_in[0], w_ffn_out=v_w_ffn_out[0])
    lead = {n: (1,) if (n in _BIG or n in ("conv_q", "conv_k", "conv_v")) else () for n in _ORDER}

    chip = 2 * lax.axis_index("x") + lax.axis_index("y")
    c = lax.axis_index("c")
    halved = {n: Wl[n].astype(bf16).reshape(2, Wl[n].shape[0] // 2, Wl[n].shape[1]) for n in _BIG}

    def gathered_weights(names, owns, outs):
        res = {}
        for n, own, g4 in zip(names, owns, outs):
            g4 = lax.dynamic_update_slice(g4, own[None], (chip,) + (0,) * own.ndim)
            if n in _BIG:
                g4 = g4.reshape(N_CHIPS, 2 * g4.shape[2], g4.shape[3])
            res[n] = g4 if n == "w_in" else _unshard(g4, n)
        return res

    first = ["w_in"] + list(_SMALL_SHARD)
    first_own = [halved["w_in"]] + [Wl[n] for n in _SMALL_SHARD]
    W = dict(Wl)
    W.update(gathered_weights(first, first_own, run_exchange(gather_chips(first_own[:1], first_own[1:]), "gather_w_in")))
    late = [n for n in _BIG if n != "w_in"]
    late_own = [halved[n] for n in late]
    for n in late:
        del W[n]

    def halves_of(names, G):
        g4 = [_to_shards(G[n], n) for n in names]
        return [g.reshape(N_CHIPS, 2, g.shape[1] // 2, g.shape[2]) for g in g4]

    def pair_added(g42, from_sib, tag, wire):
        mine = [lax.dynamic_index_in_dim(g, c, axis=1, keepdims=False) for g in g42]
        return [add2(a, b, "grad_pair_add_%s%d" % (tag, t), out_dtype=wire)
                for t, (a, b) in enumerate(zip(mine, from_sib))]

    def chip_reduced(parts, slots, tag):
        slots = [lax.dynamic_update_slice(s, lax.dynamic_index_in_dim(p, chip, axis=0, keepdims=True), (chip, 0, 0))
                 for s, p in zip(slots, parts)]
        return [sum_slots(s, "grad_chip_sum_%s%d" % (tag, t)) for t, s in enumerate(slots)]

    early, last = {}, {}

    def early_swap_begin(G):
        early["g42"] = halves_of(late, G)
        return sibling_swap(early["g42"])

    def early_begin(G):
        early["parts"] = pair_added(early["g42"], early["from_sib"], "a", f32)
        return scatter_chips(early["parts"])

    def last_begin(G):
        g2 = [g.reshape(1, 2, g.shape[0] // 2, g.shape[1]) for g in G["w_in"]]
        added = pair_added(g2, run_exchange(sibling_swap(g2), "grad_sibling_swap_b"), "b", bf16)
        last["parts"] = [_join_w_in(added[0][0], added[1][0])]
        return scatter_chips(last["parts"])

    loss, grad_x, G = _step(
        x[0], W["meta_tokens"], W, loss_target[0],
        late_weights=(gather_chips(late_own, []), lambda outs: gathered_weights(late, late_own, outs)),
        early_swap=(early_swap_begin, lambda outs: early.update(from_sib=outs)),
        early_grads=(early_begin, lambda slots: early.update(halves=chip_reduced(early["parts"], slots, "a"))),
        last_grads=(last_begin, lambda slots: last.update(halves=chip_reduced(last["parts"], slots, "b"))))
    halves = last["halves"] + early["halves"]
    theirs = sibling_send(halves, "grad_sibling_send")
    Gs = {}
    for n, h, o in zip(["w_in"] + late, halves, theirs):
        Gs[n] = lax.dynamic_update_slice(jnp.concatenate([o, o], axis=0), h, (c * h.shape[0], 0))

    small_names = list(_SMALL_REPL) + list(_SMALL_SHARD)
    pieces = [_rows_1024(G[n]) for n in small_names] + [_rows_1024(loss)]
    counts = [p.shape[0] for p in pieces]
    pack = jnp.concatenate(pieces, axis=0)
    pad_rows = (-pack.shape[0]) % SUB
    pack = jnp.pad(pack, ((0, pad_rows), (0, 0)))
    total = sum_slots(gather_all(pack, "small_gather"), "small_sum")
    chip = 2 * lax.axis_index("x") + lax.axis_index("y")
    row = 0
    for n, cnt in zip(small_names, counts[:-1]):
        blk = total[row:row + cnt]
        row += cnt
        full_shape = G[n].shape
        if full_shape[1] >= 1024:
            blk = blk.reshape(full_shape)
        else:
            blk = blk[:, :full_shape[1]]
        if n in _SMALL_SHARD:
            cs = full_shape[1] // N_CHIPS
            blk = lax.dynamic_slice_in_dim(blk, chip * cs, cs, axis=1)
        Gs[n] = blk
    loss_out = total[row, 0]

    grads, deltas, new_m, new_v = [], [], [], []
    for n in _ORDER:
        d, m2, v2 = adamw(Wl[n], Gs[n], Ml[n], Vl[n], "adamw_" + n)
        shape = lead[n] + Wl[n].shape
        grads.append(Gs[n].reshape(shape))
        deltas.append(d.reshape(shape))
        new_m.append(m2.reshape(shape))
        new_v.append(v2.reshape(shape))
    return (loss_out, grad_x[None], *grads, *deltas, *new_m, *new_v)
```

```python
import jax
import jax.numpy as jnp
from jax import lax
from jax.experimental import pallas as pl
from jax.experimental.pallas import tpu as pltpu

f32 = jnp.float32
bf16 = jnp.bfloat16

D_MODEL = 1024
N_META = 16
CHUNK = 64
HEADS = 8
DN_DK = 128
DN_DV = 256
DN_CONV = 4
DN_QK = HEADS * DN_DK
DN_V = HEADS * DN_DV
SB_DH = 128
SB_W = HEADS * SB_DH
SB_BLOCK = 128
SB_QB = 384
SB_GROUP = 4
SB_HEADS_PER_STEP = 2
SB_FWD_HEADS_PER_STEP = 4
GDN_HEADS_PER_STEP = 8
CONV_W = 2048
D_FF = 2816
RMS_EPS = 1e-6
L2_EPS = 1e-6
ADAM_LR = 0.001
ADAM_B1 = 0.9
ADAM_B2 = 0.999
ADAM_EPS = 1e-08
ADAM_WD = 0.01
ADAM_STEP = 10

P0 = 112
LANE = 128
SUB = 8
VMEM_LIMIT = 48 * 1024 * 1024
N_CHIPS = 4
N_DEV = 8

C_DQ, C_DK, C_DV, C_DZ, C_SQ, C_SK, C_SV, C_GDN, C_GSB = 0, 1, 2, 4, 6, 7, 8, 9, 10
PROJ_BIG = 11 * 1024
AB_COL = 2 * DN_QK + 2 * DN_V


def _params(n_axes):
    return pltpu.CompilerParams(dimension_semantics=("arbitrary",) * n_axes, vmem_limit_bytes=VMEM_LIMIT)


def _tile(n, target, q=LANE):
    best = None
    for t in range(q, min(n, target) + 1, q):
        if n % t == 0:
            best = t
    return best if best is not None else n


def _dot(a, b):
    return jnp.dot(a.astype(bf16), b.astype(bf16), preferred_element_type=f32)


def _dot_nt(a, b):
    return lax.dot_general(a.astype(bf16), b.astype(bf16), (((1,), (1,)), ((), ())), preferred_element_type=f32)


def _dot_tn(a, b):
    return lax.dot_general(a.astype(bf16), b.astype(bf16), (((0,), (0,)), ((), ())), preferred_element_type=f32)


_HI = lax.Precision.HIGH


def _hdot(a, b):
    return jnp.dot(a, b, precision=_HI, preferred_element_type=f32)


def _hdot_nt(a, b):
    return lax.dot_general(a, b, (((1,), (1,)), ((), ())), precision=_HI, preferred_element_type=f32)


def _hdot_tn(a, b):
    return lax.dot_general(a, b, (((0,), (0,)), ((), ())), precision=_HI, preferred_element_type=f32)


def _sigmoid(x):
    return 0.5 * jnp.tanh(0.5 * x) + 0.5


def _log1p_small(e):
    return jnp.where(e < 1e-3, e * (1.0 - e * (0.5 - e * (1.0 / 3.0))), jnp.log(1.0 + e))


def _rowsum(x):
    return jnp.sum(x, axis=1, keepdims=True)


def _allsum(x):
    return jnp.sum(jnp.sum(x, axis=1, keepdims=True), axis=0, keepdims=True)


def matmul(a, b, mode, name, residual=None, out_dtype=f32, tm_t=1408, tn_t=1024, tk_t=1408, rider=None):
    if mode == "nn":
        (M, K), (K2, N) = a.shape, b.shape
    elif mode == "nt":
        (M, K), (N, K2) = a.shape, b.shape
    else:
        (K, M), (K2, N) = a.shape, b.shape
    assert K == K2, (a.shape, b.shape, mode)
    tm, tn, tk = _tile(M, tm_t), _tile(N, tn_t), _tile(K, tk_t)
    nk = K // tk
    if mode == "nn":
        a_spec = pl.BlockSpec((tm, tk), lambda i, j, k: (i, k))
        b_spec = pl.BlockSpec((tk, tn), lambda i, j, k: (k, j))
        dims = (((1,), (0,)), ((), ()))
    elif mode == "nt":
        a_spec = pl.BlockSpec((tm, tk), lambda i, j, k: (i, k))
        b_spec = pl.BlockSpec((tn, tk), lambda i, j, k: (j, k))
        dims = (((1,), (1,)), ((), ()))
    else:
        a_spec = pl.BlockSpec((tk, tm), lambda i, j, k: (k, i))
        b_spec = pl.BlockSpec((tk, tn), lambda i, j, k: (k, j))
        dims = (((0,), (0,)), ((), ()))
    o_spec = pl.BlockSpec((tm, tn), lambda i, j, k: (i, j))
    has_res = residual is not None
    grid = (M // tm, N // tn, nk)
    split, ride_first, ride_last = _ride(rider, 3 if has_res else 2, 1, grid)

    def body(*refs):
        ins_, (o_ref,), (rin, rout, rest) = split(refs)
        a_ref, b_ref = ins_[:2]
        r_ref = ins_[2] if has_res else None
        acc_ref, ride = rest[0], (rin, rout, rest[1:])
        ride_first(ride)
        k = pl.program_id(2)

        @pl.when(k == 0)
        def _():
            acc_ref[...] = jnp.zeros_like(acc_ref)

        acc_ref[...] += lax.dot_general(a_ref[...].astype(bf16), b_ref[...].astype(bf16), dims,
                                        preferred_element_type=f32)

        @pl.when(k == nk - 1)
        def _():
            r = acc_ref[...]
            if has_res:
                r = r + r_ref[...]
            o_ref[...] = r.astype(out_dtype)

        ride_last(ride)

    ins = [a, b] + ([residual] if has_res else [])
    specs = [a_spec, b_spec] + ([o_spec] if has_res else [])
    r_ins = rider.ins if rider else []
    r_outs = rider.out_shapes if rider else []
    res = pl.pallas_call(
        body, name=name, grid=grid, in_specs=specs + [_ANY] * len(r_ins), out_specs=[o_spec] + [_ANY] * len(r_outs),
        out_shape=[jax.ShapeDtypeStruct((M, N), out_dtype)] + list(r_outs),
        scratch_shapes=[pltpu.VMEM((tm, tn), f32)] + (rider.scratch if rider else []), compiler_params=_params(3),
    )(*ins, *r_ins)
    return (res[0], res[1:]) if rider else res[0]


def _row_tile(tp):
    return _tile(tp, 512)


def rms_fwd(h, gain, name):
    tp, d = h.shape
    rt = _row_tile(tp)

    def body(h_ref, g_ref, o_ref):
        x = h_ref[...]
        r = lax.rsqrt(jnp.mean(x * x, axis=-1, keepdims=True) + RMS_EPS)
        o_ref[...] = (x * r * g_ref[...]).astype(bf16)

    return pl.pallas_call(
        body, name=name, grid=(tp // rt,),
        in_specs=[pl.BlockSpec((rt, d), lambda i: (i, 0)), pl.BlockSpec((1, d), lambda i: (0, 0))],
        out_specs=pl.BlockSpec((rt, d), lambda i: (i, 0)),
        out_shape=jax.ShapeDtypeStruct((tp, d), bf16), compiler_params=_params(1),
    )(h, gain)


def rms_bwd(h, gain, dn, dres, name):
    tp, d = h.shape
    rt = _row_tile(tp)

    def body(h_ref, g_ref, dn_ref, dr_ref, dh_ref, dhb_ref, dg_ref):
        i = pl.program_id(0)
        x = h_ref[...]
        r = lax.rsqrt(jnp.mean(x * x, axis=-1, keepdims=True) + RMS_EPS)
        xh = x * r
        dn_ = dn_ref[...]
        dxh = dn_ * g_ref[...]
        dh = r * (dxh - xh * jnp.mean(dxh * xh, axis=-1, keepdims=True)) + dr_ref[...]
        dh_ref[...] = dh
        dhb_ref[...] = dh.astype(bf16)
        part = jnp.sum(dn_ * xh, axis=0, keepdims=True)

        @pl.when(i == 0)
        def _():
            dg_ref[...] = part

        @pl.when(i > 0)
        def _():
            dg_ref[...] += part

    row = pl.BlockSpec((rt, d), lambda i: (i, 0))
    vec = pl.BlockSpec((1, d), lambda i: (0, 0))
    return pl.pallas_call(
        body, name=name, grid=(tp // rt,), in_specs=[row, vec, row, row], out_specs=[row, row, vec],
        out_shape=[jax.ShapeDtypeStruct((tp, d), f32), jax.ShapeDtypeStruct((tp, d), bf16),
                   jax.ShapeDtypeStruct((1, d), f32)],
        compiler_params=_params(1),
    )(h, gain, dn, dres)


def loss_head(y, target):
    tp, d = y.shape
    lead = P0 + N_META
    rt = _row_tile(tp)
    ns = rt // lead
    assert lead == SB_BLOCK and rt % lead == 0 and target.shape == (tp - lead, d)
    last = target.shape[0] // lead - 1

    def body(*refs):
        y_ref, t_refs = refs[0], refs[1:1 + ns]
        dy_ref, dyb_ref, l_ref = refs[1 + ns:]
        i = pl.program_id(0)

        @pl.when(i == 0)
        def _():
            l_ref[...] = jnp.zeros_like(l_ref)

        part = jnp.zeros((1, 1), f32)
        for s in range(ns):
            rows = slice(s * lead, (s + 1) * lead)
            err = y_ref[rows, :] - t_refs[s][...]
            if s == 0:
                err = err * (i > 0).astype(f32)
            dy = err * (1.0 / d)
            dy_ref[rows, :] = dy
            dyb_ref[rows, :] = dy.astype(bf16)
            part = part + _allsum(err * err)
        l_ref[...] += jnp.broadcast_to(part * (0.5 / d), l_ref.shape)

    row = pl.BlockSpec((rt, d), lambda i: (i, 0))
    t_specs = [pl.BlockSpec((lead, d), lambda i, s=s: (jnp.clip(ns * i + s - 1, 0, last), 0)) for s in range(ns)]
    return pl.pallas_call(
        body, name="loss_head", grid=(tp // rt,), in_specs=[row] + t_specs,
        out_specs=[row, row, pl.BlockSpec((1, LANE), lambda i: (0, 0))],
        out_shape=[jax.ShapeDtypeStruct((tp, d), f32), jax.ShapeDtypeStruct((tp, d), bf16),
                   jax.ShapeDtypeStruct((1, LANE), f32)],
        compiler_params=_params(1),
    )(y, *([target] * ns))


def swiglu_fwd(u):
    tp = u.shape[0]
    rt, cb = _row_tile(tp), D_FF // 2
    nb = D_FF // cb

    def body(g_ref, u_ref, o_ref):
        g = g_ref[...]
        o_ref[...] = (g * _sigmoid(g) * u_ref[...]).astype(bf16)

    return pl.pallas_call(
        body, name="swiglu_fwd", grid=(tp // rt, nb),
        in_specs=[pl.BlockSpec((rt, cb), lambda i, j: (i, j)), pl.BlockSpec((rt, cb), lambda i, j: (i, j + nb))],
        out_specs=pl.BlockSpec((rt, cb), lambda i, j: (i, j)),
        out_shape=jax.ShapeDtypeStruct((tp, D_FF), bf16), compiler_params=_params(2),
    )(u, u)


def swiglu_bwd(u, dact):
    tp = u.shape[0]
    rt, cb = _row_tile(tp), D_FF // 2
    nb = D_FF // cb

    def body(g_ref, u_ref, da_ref, dg_ref, du_ref):
        g = g_ref[...]
        s = _sigmoid(g)
        da = da_ref[...]
        dg_ref[...] = (da * u_ref[...] * s * (1.0 + g * (1.0 - s))).astype(bf16)
        du_ref[...] = (da * g * s).astype(bf16)

    lo = pl.BlockSpec((rt, cb), lambda i, j: (i, j))
    hi = pl.BlockSpec((rt, cb), lambda i, j: (i, j + nb))
    dgate, dup = pl.pallas_call(
        body, name="swiglu_bwd", grid=(tp // rt, nb), in_specs=[lo, hi, lo], out_specs=[lo, lo],
        out_shape=[jax.ShapeDtypeStruct((tp, D_FF), bf16)] * 2, compiler_params=_params(2),
    )(u, u, dact)
    return dgate, dup


def merge_fwd(proj, ydn, ysb):
    tp = proj.shape[0]
    rt, d = _row_tile(tp), D_MODEL

    def body(gd_ref, gs_ref, yd_ref, ys_ref, o_ref):
        o_ref[...] = (_sigmoid(gd_ref[...]) * yd_ref[...] + _sigmoid(gs_ref[...]) * ys_ref[...]).astype(bf16)

    row = pl.BlockSpec((rt, d), lambda i: (i, 0))
    return pl.pallas_call(
        body, name="merge_fwd", grid=(tp // rt,),
        in_specs=[pl.BlockSpec((rt, d), lambda i: (i, C_GDN)), pl.BlockSpec((rt, d), lambda i: (i, C_GSB)), row, row],
        out_specs=row, out_shape=jax.ShapeDtypeStruct((tp, d), bf16), compiler_params=_params(1),
    )(proj, proj, ydn, ysb)


def merge_bwd(proj, ydn, ysb, dm):
    tp = proj.shape[0]
    rt, d = _row_tile(tp), D_MODEL

    def body(gd_ref, gs_ref, yd_ref, ys_ref, dm_ref, dyd_ref, dys_ref, dg_ref):
        dm_ = dm_ref[...]
        sd = _sigmoid(gd_ref[...])
        ss = _sigmoid(gs_ref[...])
        dyd_ref[...] = (dm_ * sd).astype(bf16)
        dys_ref[...] = (dm_ * ss).astype(bf16)
        dg_ref[:, :d] = (dm_ * yd_ref[...] * sd * (1.0 - sd)).astype(bf16)
        dg_ref[:, d:] = (dm_ * ys_ref[...] * ss * (1.0 - ss)).astype(bf16)

    row = pl.BlockSpec((rt, d), lambda i: (i, 0))
    return pl.pallas_call(
        body, name="merge_bwd", grid=(tp // rt,),
        in_specs=[pl.BlockSpec((rt, d), lambda i: (i, C_GDN)), pl.BlockSpec((rt, d), lambda i: (i, C_GSB)), row, row, row],
        out_specs=[row, row, pl.BlockSpec((rt, 2 * d), lambda i: (i, 0))],
        out_shape=[jax.ShapeDtypeStruct((tp, d), bf16)] * 2 + [jax.ShapeDtypeStruct((tp, 2 * d), bf16)],
        compiler_params=_params(1),
    )(proj, proj, ydn, ysb, dm)


def dn_out_fwd(o, proj, gain):
    tp = o.shape[0]
    rt, cb, wide = _row_tile(tp), DN_DV, 1024
    zb = C_DZ * 1024 // wide

    def body(o_ref, z_ref, g_ref, y_ref):
        for s in range(wide // cb):
            sl = slice(s * cb, (s + 1) * cb)
            x = o_ref[:, sl]
            r = lax.rsqrt(jnp.mean(x * x, axis=-1, keepdims=True) + RMS_EPS)
            z = z_ref[:, sl]
            y_ref[:, sl] = (x * r * g_ref[...] * (z * _sigmoid(z))).astype(bf16)

    blk = pl.BlockSpec((rt, wide), lambda i, j: (i, j))
    return pl.pallas_call(
        body, name="dn_out_fwd", grid=(tp // rt, DN_V // wide),
        in_specs=[blk, pl.BlockSpec((rt, wide), lambda i, j: (i, j + zb)), pl.BlockSpec((1, cb), lambda i, j: (0, 0))],
        out_specs=blk, out_shape=jax.ShapeDtypeStruct((tp, DN_V), bf16), compiler_params=_params(2),
    )(o, proj, gain)


def dn_out_bwd(o, proj, gain, dy):
    tp = o.shape[0]
    rt, cb, wide = _row_tile(tp), DN_DV, 1024
    zb = C_DZ * 1024 // wide

    def body(o_ref, z_ref, g_ref, dy_ref, do_ref, dz_ref, dg_ref):
        i, j = pl.program_id(0), pl.program_id(1)
        g = g_ref[...]
        part = jnp.zeros((1, cb), f32)
        for hh in range(wide // cb):
            sl = slice(hh * cb, (hh + 1) * cb)
            x = o_ref[:, sl]
            r = lax.rsqrt(jnp.mean(x * x, axis=-1, keepdims=True) + RMS_EPS)
            xh = x * r
            z = z_ref[:, sl]
            s = _sigmoid(z)
            dy_ = dy_ref[:, sl]
            drn = dy_ * (z * s)
            dz_ref[:, sl] = (dy_ * xh * g * s * (1.0 + z * (1.0 - s))).astype(bf16)
            dxh = drn * g
            do_ref[:, sl] = r * (dxh - xh * jnp.mean(dxh * xh, axis=-1, keepdims=True))
            part = part + jnp.sum(drn * xh, axis=0, keepdims=True)
        first = jnp.logical_and(i == 0, j == 0)

        @pl.when(first)
        def _():
            dg_ref[...] = part

        @pl.when(jnp.logical_not(first))
        def _():
            dg_ref[...] += part

    blk = pl.BlockSpec((rt, wide), lambda i, j: (i, j))
    vec = pl.BlockSpec((1, cb), lambda i, j: (0, 0))
    return pl.pallas_call(
        body, name="dn_out_bwd", grid=(tp // rt, DN_V // wide),
        in_specs=[blk, pl.BlockSpec((rt, wide), lambda i, j: (i, j + zb)), vec, blk],
        out_specs=[blk, pl.BlockSpec((rt, wide), lambda i, j: (i, j + zb)), vec],
        out_shape=[jax.ShapeDtypeStruct((tp, DN_V), f32), jax.ShapeDtypeStruct((tp, PROJ_BIG), bf16),
                   jax.ShapeDtypeStruct((1, cb), f32)],
        compiler_params=_params(2),
    )(o, proj, gain, dy)


def sb_prep_fwd(proj, gq, gk):
    tp = proj.shape[0]
    rt, cb = _row_tile(tp), SB_DH

    def body(q_ref, k_ref, v_ref, gq_ref, gk_ref, qo_ref, ko_ref, vo_ref):
        for x_ref, g_ref, o_ref in ((q_ref, gq_ref, qo_ref), (k_ref, gk_ref, ko_ref)):
            for h in range(HEADS):
                sl = slice(h * cb, (h + 1) * cb)
                x = x_ref[:, sl]
                r = lax.rsqrt(jnp.mean(x * x, axis=-1, keepdims=True) + RMS_EPS)
                o_ref[:, sl] = (x * r * g_ref[...]).astype(bf16)
        vo_ref[...] = v_ref[...].astype(bf16)

    blk = pl.BlockSpec((rt, SB_W), lambda i: (i, 0))
    vec = pl.BlockSpec((1, cb), lambda i: (0, 0))
    return pl.pallas_call(
        body, name="sb_prep_fwd", grid=(tp // rt,),
        in_specs=[pl.BlockSpec((rt, SB_W), lambda i: (i, C_SQ)), pl.BlockSpec((rt, SB_W), lambda i: (i, C_SK)),
                  pl.BlockSpec((rt, SB_W), lambda i: (i, C_SV)), vec, vec],
        out_specs=[blk] * 3, out_shape=[jax.ShapeDtypeStruct((tp, SB_W), bf16)] * 3, compiler_params=_params(1),
    )(proj, proj, proj, gq, gk)


def sb_prep_bwd(proj, gq, gk, dqs, dks, dvs, into):
    tp = proj.shape[0]
    rt, cb = _row_tile(tp), SB_DH
    assert (C_SQ * 1024) % (3 * SB_W) == 0 and (C_SQ + 1, C_SQ + 2) == (C_SK, C_SV)

    def body(q_ref, k_ref, gq_ref, gk_ref, dq_ref, dk_ref, dv_ref, into_ref, do_ref, dgq_ref, dgk_ref):
        first = pl.program_id(0) == 0
        do_ref[:, 2 * SB_W:] = dv_ref[...].astype(bf16)
        for x_ref, g_ref, dn_ref, at, dg_ref, mul in ((q_ref, gq_ref, dq_ref, 0, dgq_ref, None),
                                                      (k_ref, gk_ref, dk_ref, SB_W, dgk_ref, SB_DH ** -0.5)):
            part = jnp.zeros((1, cb), f32)
            for h in range(HEADS):
                sl = slice(h * cb, (h + 1) * cb)
                x = x_ref[:, sl]
                r = lax.rsqrt(jnp.mean(x * x, axis=-1, keepdims=True) + RMS_EPS)
                xh = x * r
                dn_ = dn_ref[:, sl] if mul is None else dn_ref[:, sl] * mul
                dxh = dn_ * g_ref[...]
                do_ref[:, at + h * cb:at + (h + 1) * cb] = (
                    r * (dxh - xh * jnp.mean(dxh * xh, axis=-1, keepdims=True))).astype(bf16)
                part = part + jnp.sum(dn_ * xh, axis=0, keepdims=True)

            @pl.when(first)
            def _(dg_ref=dg_ref, part=part):
                dg_ref[...] = part

            @pl.when(jnp.logical_not(first))
            def _(dg_ref=dg_ref, part=part):
                dg_ref[...] += part

    blk = pl.BlockSpec((rt, SB_W), lambda i: (i, 0))
    vec = pl.BlockSpec((1, cb), lambda i: (0, 0))
    return pl.pallas_call(
        body, name="sb_prep_bwd", grid=(tp // rt,),
        in_specs=[pl.BlockSpec((rt, SB_W), lambda i: (i, C_SQ)), pl.BlockSpec((rt, SB_W), lambda i: (i, C_SK)),
                  vec, vec, blk, blk, blk, pl.BlockSpec(memory_space=pl.ANY)],
        out_specs=[pl.BlockSpec((rt, 3 * SB_W), lambda i: (i, C_SQ * 1024 // (3 * SB_W))), vec, vec],
        out_shape=[jax.ShapeDtypeStruct(into.shape, into.dtype)] + [jax.ShapeDtypeStruct((1, cb), f32)] * 2,
        input_output_aliases={7: 0}, compiler_params=_params(1),
    )(proj, proj, gq, gk, dqs, dks, dvs, into)


def _conv_taps(ext, rt):
    taps = []
    for k in range(DN_CONV):
        s = DN_CONV - 1 - k
        taps.append((pltpu.roll(ext, s, axis=0) if s else ext)[SUB:SUB + rt])
    return taps


def _conv_act(taps, w, l2):
    y = taps[0] * w[0:1]
    for k in range(1, DN_CONV):
        y = y + taps[k] * w[k:k + 1]
    s = _sigmoid(y)
    a = y * s
    if l2:
        n = lax.rsqrt(jnp.sum(a * a, axis=-1, keepdims=True) + L2_EPS)
        return y, s, a, n
    return y, s, a, None


def conv_fwd(proj, w8, col_blk, ncb, l2, name):
    tp = proj.shape[0]
    rt = _row_tile(tp)
    hb = rt // SUB
    cw = min(CONV_W, ncb * LANE)
    cb0 = col_blk * LANE // cw

    def body(x_ref, h_ref, w_ref, o_ref):
        i = pl.program_id(1)
        first = (i > 0).astype(f32)
        for s in range(cw // LANE):
            sl = slice(s * LANE, (s + 1) * LANE)
            ext = jnp.concatenate([h_ref[:, sl] * first, x_ref[:, sl]], axis=0)
            _, _, a, n = _conv_act(_conv_taps(ext, rt), w_ref[:, sl], l2)
            o_ref[:, sl] = a * n if l2 else a

    return pl.pallas_call(
        body, name=name, grid=(ncb * LANE // cw, tp // rt),
        in_specs=[pl.BlockSpec((rt, cw), lambda j, i: (i, j + cb0)),
                  pl.BlockSpec((SUB, cw), lambda j, i: (jnp.maximum(i * hb - 1, 0), j + cb0)),
                  pl.BlockSpec((SUB, cw), lambda j, i: (0, j))],
        out_specs=pl.BlockSpec((rt, cw), lambda j, i: (i, j)),
        out_shape=jax.ShapeDtypeStruct((tp, ncb * LANE), f32), compiler_params=_params(2),
    )(proj, proj, w8)


def conv_bwd_act(proj, w8, dout, col_blk, ncb, l2, name):
    tp = proj.shape[0]
    rt = _row_tile(tp)
    hb = rt // SUB
    cw = min(CONV_W, ncb * LANE)
    cb0 = col_blk * LANE // cw

    def body(x_ref, h_ref, w_ref, d_ref, dy_ref, dw_ref):
        i = pl.program_id(1)
        first = (i > 0).astype(f32)
        rows = lax.broadcasted_iota(jnp.int32, (SUB, LANE), 0)
        for s in range(cw // LANE):
            sl = slice(s * LANE, (s + 1) * LANE)
            ext = jnp.concatenate([h_ref[:, sl] * first, x_ref[:, sl]], axis=0)
            taps = _conv_taps(ext, rt)
            y, sg, a, n = _conv_act(taps, w_ref[:, sl], l2)
            da = d_ref[:, sl]
            if l2:
                out = a * n
                da = n * (da - out * jnp.sum(da * out, axis=-1, keepdims=True))
            dy = da * sg * (1.0 + y * (1.0 - sg))
            dy_ref[:, sl] = dy
            part = jnp.zeros((SUB, LANE), f32)
            for k in range(DN_CONV):
                part = part + jnp.where(rows == k, jnp.sum(taps[k] * dy, axis=0, keepdims=True), 0.0)

            @pl.when(i == 0)
            def _(sl=sl, part=part):
                dw_ref[:, sl] = part

            @pl.when(i > 0)
            def _(sl=sl, part=part):
                dw_ref[:, sl] += part

    return pl.pallas_call(
        body, name=name, grid=(ncb * LANE // cw, tp // rt),
        in_specs=[pl.BlockSpec((rt, cw), lambda j, i: (i, j + cb0)),
                  pl.BlockSpec((SUB, cw), lambda j, i: (jnp.maximum(i * hb - 1, 0), j + cb0)),
                  pl.BlockSpec((SUB, cw), lambda j, i: (0, j)),
                  pl.BlockSpec((rt, cw), lambda j, i: (i, j))],
        out_specs=[pl.BlockSpec((rt, cw), lambda j, i: (i, j)), pl.BlockSpec((SUB, cw), lambda j, i: (0, j))],
        out_shape=[jax.ShapeDtypeStruct((tp, ncb * LANE), f32), jax.ShapeDtypeStruct((SUB, ncb * LANE), f32)],
        compiler_params=_params(2),
    )(proj, proj, w8, dout)


def conv_bwd_in(dy, w8, into, col_blk, name):
    tp, cols = dy.shape
    rt = _row_tile(tp)
    hb = rt // SUB
    nr = tp // rt
    last8 = tp // SUB - 1
    cw = min(CONV_W, cols)
    cb0 = col_blk * LANE // cw

    def body(d_ref, h_ref, w_ref, into_ref, o_ref):
        i = pl.program_id(1)
        last = (i < nr - 1).astype(f32)
        for c0 in range(cw // LANE):
            sl = slice(c0 * LANE, (c0 + 1) * LANE)
            ext = jnp.concatenate([d_ref[:, sl], h_ref[:, sl] * last], axis=0)
            w = w_ref[:, sl]
            acc = None
            for k in range(DN_CONV):
                s = DN_CONV - 1 - k
                sh = (pltpu.roll(ext, rt + SUB - s, axis=0) if s else ext)[0:rt]
                term = sh * w[k:k + 1]
                acc = term if acc is None else acc + term
            o_ref[:, sl] = acc.astype(bf16)

    return pl.pallas_call(
        body, name=name, grid=(cols // cw, nr),
        in_specs=[pl.BlockSpec((rt, cw), lambda j, i: (i, j)),
                  pl.BlockSpec((SUB, cw), lambda j, i: (jnp.minimum((i + 1) * hb, last8), j)),
                  pl.BlockSpec((SUB, cw), lambda j, i: (0, j)), pl.BlockSpec(memory_space=pl.ANY)],
        out_specs=pl.BlockSpec((rt, cw), lambda j, i: (i, j + cb0)),
        out_shape=jax.ShapeDtypeStruct(into.shape, into.dtype), input_output_aliases={3: 0},
        compiler_params=_params(2),
    )(dy, dy, w8, into)


def _ab_common(p, al, dtb, r0):
    rows = r0 + lax.broadcasted_iota(jnp.int32, p.shape, 0)
    mask = (rows >= P0).astype(f32)
    xx = p + dtb
    sp = jnp.maximum(xx, 0.0) + _log1p_small(jnp.exp(-jnp.abs(xx)))
    ea = jnp.exp(al)
    g = -ea * sp * mask
    beta = _sigmoid(p) * mask
    return g, beta, _sigmoid(xx), ea, mask


def _chunk_tri(rt, later):
    r = lax.broadcasted_iota(jnp.int32, (rt, rt), 0)
    c = lax.broadcasted_iota(jnp.int32, (rt, rt), 1)
    shift = CHUNK.bit_length() - 1
    same = jnp.right_shift(r, shift) == jnp.right_shift(c, shift)
    return jnp.logical_and(same, c >= r if later else c <= r).astype(f32)


def ab_fwd(pab, al, dtb):
    tp = pab.shape[0]
    rt = _row_tile(tp)
    assert rt % CHUNK == 0

    def body(p_ref, al_ref, dt_ref, g_ref, b_ref):
        i = pl.program_id(0)
        g, beta, _, _, _ = _ab_common(p_ref[...], al_ref[...], dt_ref[...], i * rt)
        gam = _hdot(_chunk_tri(rt, False), g)
        for h in range(HEADS):
            g_ref[h] = jnp.broadcast_to(gam[:, h:h + 1], (rt, LANE))
            b_ref[h] = jnp.broadcast_to(beta[:, HEADS + h:HEADS + h + 1], (rt, LANE))

    vec = pl.BlockSpec((1, LANE), lambda i: (0, 0))
    out = pl.BlockSpec((HEADS, rt, LANE), lambda i: (0, i, 0))
    return pl.pallas_call(
        body, name="ab_fwd", grid=(tp // rt,), in_specs=[pl.BlockSpec((rt, LANE), lambda i: (i, 0)), vec, vec],
        out_specs=[out, out], out_shape=[jax.ShapeDtypeStruct((HEADS, tp, LANE), f32)] * 2, compiler_params=_params(1),
    )(pab, al, dtb)


def ab_bwd(pab, al, dtb, dg, db):
    tp = pab.shape[0]
    rt = _row_tile(tp)

    def body(p_ref, al_ref, dt_ref, dg_ref, db_ref, dp_ref, dal_ref, ddt_ref):
        i = pl.program_id(0)
        g, beta, sx, ea, mask = _ab_common(p_ref[...], al_ref[...], dt_ref[...], i * rt)
        lanes = lax.broadcasted_iota(jnp.int32, (rt, LANE), 1)
        dgl = jnp.zeros((rt, LANE), f32)
        dbl = jnp.zeros((rt, LANE), f32)
        for h in range(HEADS):
            dgl = dgl + jnp.where(lanes == h, dg_ref[h], 0.0)
            dbl = dbl + jnp.where(lanes == HEADS + h, db_ref[h], 0.0)
        dgl = _hdot(_chunk_tri(rt, True), dgl)
        dxx = dgl * (-ea) * sx * mask
        dp_ref[...] = (dxx + dbl * beta * (1.0 - beta)).astype(bf16)
        pal = jnp.sum(dgl * g, axis=0, keepdims=True)
        pdt = jnp.sum(dxx, axis=0, keepdims=True)

        @pl.when(i == 0)
        def _():
            dal_ref[...] = pal
            ddt_ref[...] = pdt

        @pl.when(i > 0)
        def _():
            dal_ref[...] += pal
            ddt_ref[...] += pdt

    vec = pl.BlockSpec((1, LANE), lambda i: (0, 0))
    row = pl.BlockSpec((rt, LANE), lambda i: (i, 0))
    big = pl.BlockSpec((HEADS, rt, LANE), lambda i: (0, i, 0))
    return pl.pallas_call(
        body, name="ab_bwd", grid=(tp // rt,), in_specs=[row, vec, vec, big, big], out_specs=[row, vec, vec],
        out_shape=[jax.ShapeDtypeStruct((tp, LANE), bf16), jax.ShapeDtypeStruct((1, LANE), f32),
                   jax.ShapeDtypeStruct((1, LANE), f32)],
        compiler_params=_params(1),
    )(pab, al, dtb, dg, db)


class _Chunk:
    pass


def _gdn_chunk(q, k, v, gcol, bcol, grow8):
    C = CHUNK
    R = range(len(q))
    X = _Chunk()
    ri = lax.broadcasted_iota(jnp.int32, (C, C), 0)
    ci = lax.broadcasted_iota(jnp.int32, (C, C), 1)
    eye = (ri == ci).astype(f32)
    gam = list(gcol)
    gam_row = [grow8[h][0:1, 0:C] for h in R]
    X.ri, X.ci = ri, ci
    X.Dm = [jnp.where(ri >= ci, jnp.exp(jnp.minimum(gam[h][:, 0:C] - gam_row[h], 0.0)), 0.0) for h in R]
    X.eg = [jnp.exp(gam[h]) for h in R]
    gl = [gam[h][C - 1:C, :] for h in R]
    X.egl = [jnp.exp(gl[h]) for h in R]
    X.kdec = [jnp.exp(gl[h] - gam[h]) for h in R]
    X.qs = [q[h] * (DN_DK ** -0.5) for h in R]
    X.kb = [k[h] * bcol[h] for h in R]
    kk = [_dot_nt(X.kb[h], k[h]) for h in R]
    qk = [_dot_nt(X.qs[h], k[h]) for h in R]
    X.A = [jnp.where(ri > ci, kk[h] * X.Dm[h], 0.0) for h in R]
    assert C == 64
    b16 = jnp.right_shift(ri, 4) == jnp.right_shift(ci, 4)
    b32 = jnp.right_shift(ri, 5) == jnp.right_shift(ci, 5)
    P = [jnp.where(b16, X.A[h], 0.0) for h in R]
    T = [eye - P[h] for h in R]
    for _ in range(3):
        P = [_hdot(P[h], P[h]) for h in R]
        T = [T[h] + _hdot(T[h], P[h]) for h in R]
    for off in (jnp.logical_and(b32, jnp.logical_not(b16)), jnp.logical_not(b32)):
        AT = [_hdot(jnp.where(off, X.A[h], 0.0), T[h]) for h in R]
        T = [T[h] - _hdot(T[h], AT[h]) for h in R]
    X.T = T
    X.b2 = [jnp.concatenate([bcol[h], bcol[h]], axis=-1) for h in R]
    X.u = [_hdot(T[h], v[h] * X.b2[h]) for h in R]
    X.w = [_hdot(T[h], X.kb[h] * X.eg[h]) for h in R]
    X.attn = [qk[h] * X.Dm[h] for h in R]
    X.qg = [X.qs[h] * X.eg[h] for h in R]
    X.kg = [k[h] * X.kdec[h] for h in R]
    return X


def gdn_fwd(q, k, v, gc, bc, grow):
    tp = q.shape[0]
    nc = tp // CHUNK
    hb = GDN_HEADS_PER_STEP

    def body(q_ref, k_ref, v_ref, gc_ref, bc_ref, gr_ref, o_ref, ss_ref, S_ref):
        c = pl.program_id(1)

        @pl.when(c == 0)
        def _():
            S_ref[...] = jnp.zeros_like(S_ref)

        R = range(hb)
        qc = [slice(h * DN_DK, (h + 1) * DN_DK) for h in R]
        vc = [slice(h * DN_DV, (h + 1) * DN_DV) for h in R]
        X = _gdn_chunk([q_ref[:, qc[h]] for h in R], [k_ref[:, qc[h]] for h in R], [v_ref[:, vc[h]] for h in R],
                       [gc_ref[h] for h in R], [bc_ref[h] for h in R], [gr_ref[h] for h in R])
        S = [S_ref[h] for h in R]
        for h in R:
            ss_ref[h, 0] = S[h]
        wS = [_dot(X.w[h], S[h]) for h in R]
        qS = [_dot(X.qg[h], S[h]) for h in R]
        vn = [X.u[h] - wS[h] for h in R]
        av = [_dot(X.attn[h], vn[h]) for h in R]
        kv = [_dot_tn(X.kg[h], vn[h]) for h in R]
        for h in R:
            o_ref[:, vc[h]] = qS[h] + av[h]
            S_ref[h] = S[h] * X.egl[h][:, 0:1] + kv[h]

    qk = pl.BlockSpec((CHUNK, hb * DN_DK), lambda g, c: (c, g))
    vv = pl.BlockSpec((CHUNK, hb * DN_DV), lambda g, c: (c, g))
    col = pl.BlockSpec((hb, CHUNK, LANE), lambda g, c: (g, c, 0))
    row = pl.BlockSpec((hb, SUB, LANE), lambda g, c: (g, c, 0))
    return pl.pallas_call(
        body, name="gdn_fwd", grid=(HEADS // hb, nc), in_specs=[qk, qk, vv, col, col, row],
        out_specs=[vv, pl.BlockSpec((hb, 1, DN_DK, DN_DV), lambda g, c: (g, c, 0, 0))],
        out_shape=[jax.ShapeDtypeStruct((tp, DN_V), f32), jax.ShapeDtypeStruct((HEADS, nc, DN_DK, DN_DV), f32)],
        scratch_shapes=[pltpu.VMEM((hb, DN_DK, DN_DV), f32)], compiler_params=_params(2),
    )(q, k, v, gc, bc, grow)


def gdn_bwd(q, k, v, gc, bc, grow, states, do, rider=None):
    tp = q.shape[0]
    nc = tp // CHUNK
    C = CHUNK
    hb = GDN_HEADS_PER_STEP
    grid = (HEADS // hb, nc)
    split, ride_first, ride_last = _ride(rider, 8, 5, grid)

    def body(*refs):
        ((q_ref, k_ref, v_ref, gc_ref, bc_ref, gr_ref, ss_ref, do_ref), (dq_ref, dk_ref, dv_ref, dg_ref, db_ref),
         (rin, rout, rest)) = split(refs)
        dS_ref, ride = rest[0], (rin, rout, rest[1:])
        ride_first(ride)
        c = pl.program_id(1)

        @pl.when(c == 0)
        def _():
            dS_ref[...] = jnp.zeros_like(dS_ref)

        R = range(hb)
        qc = [slice(h * DN_DK, (h + 1) * DN_DK) for h in R]
        vc = [slice(h * DN_DV, (h + 1) * DN_DV) for h in R]
        k_ = [k_ref[:, qc[h]] for h in R]
        v_ = [v_ref[:, vc[h]] for h in R]
        bcol = [bc_ref[h] for h in R]
        X = _gdn_chunk([q_ref[:, qc[h]] for h in R], k_, v_, [gc_ref[h] for h in R], bcol, [gr_ref[h] for h in R])
        ri, ci = X.ri, X.ci
        S = [ss_ref[h, 0] for h in R]
        do_ = [do_ref[:, vc[h]] for h in R]
        dSn = [dS_ref[h] for h in R]
        wS = [_dot(X.w[h], S[h]) for h in R]
        ado = [_dot_tn(X.attn[h], do_[h]) for h in R]
        kdS = [_dot(X.kg[h], dSn[h]) for h in R]
        d_qg = [_dot_nt(do_[h], S[h]) for h in R]
        qdo = [_dot_tn(X.qg[h], do_[h]) for h in R]
        vn = [X.u[h] - wS[h] for h in R]
        d_vn = [ado[h] + kdS[h] for h in R]
        dovn = [_dot_nt(do_[h], vn[h]) for h in R]
        d_kg = [_dot_nt(vn[h], dSn[h]) for h in R]
        wdv = [_dot_tn(X.w[h], d_vn[h]) for h in R]
        dw = [-_dot_nt(d_vn[h], S[h]) for h in R]
        for h in R:
            dS_ref[h] = qdo[h] + X.egl[h][:, 0:1] * dSn[h] - wdv[h]
        dattn = [jnp.where(ri >= ci, dovn[h], 0.0) for h in R]
        dRu = [_hdot_tn(X.T[h], d_vn[h]) for h in R]
        dRw = [_hdot_tn(X.T[h], dw[h]) for h in R]
        dAu = [_hdot_nt(dRu[h], X.u[h]) for h in R]
        dAw = [_hdot_nt(dRw[h], X.w[h]) for h in R]
        dA = [jnp.where(ri > ci, -(dAu[h] + dAw[h]), 0.0) for h in R]
        dKK = [dA[h] * X.Dm[h] for h in R]
        dQK = [dattn[h] * X.Dm[h] for h in R]
        E = [dA[h] * X.A[h] + dattn[h] * X.attn[h] for h in R]
        dkb = [_dot(dKK[h], k_[h]) + dRw[h] * X.eg[h] for h in R]
        dk1 = [_dot_tn(dKK[h], X.kb[h]) for h in R]
        dqs = [_dot(dQK[h], k_[h]) + d_qg[h] * X.eg[h] for h in R]
        dk2 = [_dot_tn(dQK[h], X.qs[h]) for h in R]
        ones = jnp.ones((C, LANE), f32)
        colE = [_hdot_tn(E[h], ones) for h in R]
        rows = lax.broadcasted_iota(jnp.int32, (C, LANE), 0)
        dgam = []
        for h in R:
            t = d_kg[h] * X.kg[h]
            dgl = _allsum(t) + X.egl[h][:, 0:1] * _allsum(S[h] * dSn[h])
            g = (_rowsum(E[h]) - colE[h] + _rowsum(dRw[h] * (X.kb[h] * X.eg[h])) + _rowsum(d_qg[h] * X.qg[h])
                 - _rowsum(t))
            dgam.append(g + jnp.where(rows == C - 1, dgl, 0.0))
        for h in R:
            dv_ref[:, vc[h]] = dRu[h] * X.b2[h]
            dbeta = _rowsum(dRu[h] * v_[h]) + _rowsum(dkb[h] * k_[h])
            dq_ref[:, qc[h]] = dqs[h] * (DN_DK ** -0.5)
            dk_ref[:, qc[h]] = dk1[h] + dk2[h] + dkb[h] * bcol[h] + d_kg[h] * X.kdec[h]
            dg_ref[h] = dgam[h]
            db_ref[h] = jnp.broadcast_to(dbeta, (C, LANE))
        ride_last(ride)

    rc = lambda c: nc - 1 - c
    qk = pl.BlockSpec((CHUNK, hb * DN_DK), lambda g, c: (rc(c), g))
    vv = pl.BlockSpec((CHUNK, hb * DN_DV), lambda g, c: (rc(c), g))
    col = pl.BlockSpec((hb, CHUNK, LANE), lambda g, c: (g, rc(c), 0))
    row = pl.BlockSpec((hb, SUB, LANE), lambda g, c: (g, rc(c), 0))
    st = pl.BlockSpec((hb, 1, DN_DK, DN_DV), lambda g, c: (g, rc(c), 0, 0))
    r_ins = rider.ins if rider else []
    r_outs = rider.out_shapes if rider else []
    res = pl.pallas_call(
        body, name="gdn_bwd", grid=grid, in_specs=[qk, qk, vv, col, col, row, st, vv] + [_ANY] * len(r_ins),
        out_specs=[qk, qk, vv, col, col] + [_ANY] * len(r_outs),
        out_shape=[jax.ShapeDtypeStruct((tp, DN_QK), f32), jax.ShapeDtypeStruct((tp, DN_QK), f32),
                   jax.ShapeDtypeStruct((tp, DN_V), f32), jax.ShapeDtypeStruct((HEADS, tp, LANE), f32),
                   jax.ShapeDtypeStruct((HEADS, tp, LANE), f32)] + list(r_outs),
        scratch_shapes=[pltpu.VMEM((hb, DN_DK, DN_DV), f32)] + (rider.scratch if rider else []),
        compiler_params=_params(2),
    )(q, k, v, gc, bc, grow, states, do, *r_ins)
    return res[:5], res[5:]


def _cumsum_after(x, nb, us, pieces=2):
    B, n = SB_BLOCK, x.shape[0]
    hi = x.astype(bf16)
    parts = (hi, (x - hi.astype(f32)).astype(bf16)) if pieces == 2 else (hi,)
    rows = [p[:, b * B:(b + 1) * B] for p in parts for b in range(nb)]
    r = jnp.dot(jnp.concatenate(rows, axis=0), us, preferred_element_type=f32)
    out = [r[b * n:(b + 1) * n] for b in range(nb)]
    if pieces == 2:
        out = [out[b] + r[(nb + b) * n:(nb + b + 1) * n] for b in range(nb)]
    return out[0] if nb == 1 else jnp.concatenate(out, axis=1)


def _later_blocks(x, nb, carry):
    B = SB_BLOCK
    tot = [_rowsum(x[:, b * B:(b + 1) * B]) for b in range(nb)]
    offs = [None] * nb
    run = carry
    for b in range(nb - 1, -1, -1):
        offs[b] = jnp.broadcast_to(run, (x.shape[0], B))
        run = run + tot[b]
    return (offs[0] if nb == 1 else jnp.concatenate(offs, axis=1)), run


def _sb_group(i, t):
    top = (i + 1) * (SB_QB // SB_BLOCK) - 1 - SB_GROUP * t
    jlo = jnp.maximum(top - SB_GROUP + 1, 0)
    rows = pl.ds(pl.multiple_of(jlo * SB_BLOCK, SB_BLOCK), SB_GROUP * SB_BLOCK)
    return jlo, rows, (top + 1) * SB_BLOCK


def _sb_weights(q, kcat, i, jlo, kend, cs, us, masked):
    B, nb = SB_BLOCK, SB_GROUP
    R = range(len(q))
    z = [_dot_nt(q[h], kcat[h]) * (SB_DH ** -0.5) for h in R]
    e = [jnp.exp(-jnp.abs(z[h])) for h in R]
    l1p = [jnp.log(1.0 + e[h]) for h in R]
    lsp = [jnp.minimum(z[h], 0.0) - l1p[h] for h in R]
    lk = [lsp[h] - z[h] for h in R]
    vis = None
    if masked:
        qpos = i * SB_QB + lax.broadcasted_iota(jnp.int32, (SB_QB, nb * B), 0)
        kpos = jlo * B + lax.broadcasted_iota(jnp.int32, (SB_QB, nb * B), 1)
        vis = jnp.logical_and(kpos < jnp.minimum(qpos, kend), kpos >= P0)
        lk = [jnp.where(vis, lk[h], 0.0) for h in R]
    later = [_later_blocks(lk[h], nb, cs[h]) for h in R]
    cum = [_cumsum_after(lk[h], nb, us) for h in R]
    w = [jnp.exp(lsp[h] + cum[h] + later[h][0]) for h in R]
    if masked:
        w = [jnp.where(vis, w[h], 0.0) for h in R]
    return lsp, vis, w, [later[h][1] for h in R]


def _sb_loop(i, step, carry):
    trips = ((i + 1) * (SB_QB // SB_BLOCK) - 1 + SB_GROUP) // SB_GROUP
    carry = step(True)(0, carry)
    carry = lax.fori_loop(1, trips - 1, step(False), carry)
    return lax.fori_loop(jnp.maximum(trips - 1, 1), trips, step(True), carry)


def _ride(rider, n_in, n_out, grid):
    n_rin = len(rider.ins) if rider else 0
    n_rout = len(rider.out_shapes) if rider else 0

    def split(refs):
        ins, rin = refs[:n_in], refs[n_in:n_in + n_rin]
        outs = refs[n_in + n_rin:n_in + n_rin + n_out]
        rout = refs[n_in + n_rin + n_out:n_in + n_rin + n_out + n_rout]
        return ins, outs, (rin, rout, refs[n_in + n_rin + n_out + n_rout:])

    def at(step, fn, r):
        if rider is None:
            return
        cond = None
        for a, g in enumerate(grid):
            c = pl.program_id(a) == (g - 1 if step == "last" else 0)
            cond = c if cond is None else jnp.logical_and(cond, c)

        @pl.when(cond)
        def _():
            fn(*r)

    first = lambda r: at("first", rider.start if rider else None, r)
    last = lambda r: at("last", rider.finish if rider else None, r)
    return split, first, last


def sb_fwd(qs, ks, vs, rider=None):
    tp = qs.shape[0]
    nq = tp // SB_QB
    B, G, hb, QB = SB_BLOCK, SB_GROUP, SB_FWD_HEADS_PER_STEP, SB_QB
    assert tp >= G * B and tp % QB == 0 and QB % B == 0 and G * B >= QB
    grid = (HEADS // hb, nq)
    split, ride_first, ride_last = _ride(rider, 3, 2, grid)

    def body(*refs):
        (q_ref, k_ref, v_ref), (o_ref, ob_ref), ride = split(refs)
        ride_first(ride)
        i = pl.program_id(1)
        R = range(hb)
        hs = [slice(h * SB_DH, (h + 1) * SB_DH) for h in R]
        q = [q_ref[:, hs[h]] for h in R]
        us = (lax.broadcasted_iota(jnp.int32, (B, B), 0) > lax.broadcasted_iota(jnp.int32, (B, B), 1)).astype(bf16)

        def make_step(masked):
            def step(t, carry):
                acc, cs = carry
                jlo, rows, kend = _sb_group(i, t)
                _, _, w, cs = _sb_weights(q, [k_ref[rows, hs[h]] for h in R], i, jlo, kend, cs, us, masked)
                pv = [_dot(w[h], v_ref[rows, hs[h]]) for h in R]
                return tuple(acc[h] + pv[h] for h in R), tuple(cs)
            return step

        carry = (tuple(jnp.zeros((QB, SB_DH), f32) for _ in R), tuple(jnp.zeros((QB, 1), f32) for _ in R))
        acc, _ = _sb_loop(i, make_step, carry)
        for h in R:
            o_ref[:, hs[h]] = acc[h]
            ob_ref[:, hs[h]] = acc[h].astype(bf16)
        ride_last(ride)

    blk = pl.BlockSpec((QB, hb * SB_DH), lambda g, i: (i, g))
    full = pl.BlockSpec((tp, hb * SB_DH), lambda g, i: (0, g))
    r_ins = rider.ins if rider else []
    r_outs = rider.out_shapes if rider else []
    res = pl.pallas_call(
        body, name="sb_fwd", grid=grid, in_specs=[blk, full, full] + [_ANY] * len(r_ins),
        out_specs=[blk, blk] + [_ANY] * len(r_outs),
        out_shape=[jax.ShapeDtypeStruct((tp, SB_W), f32), jax.ShapeDtypeStruct((tp, SB_W), bf16)] + list(r_outs),
        scratch_shapes=rider.scratch if rider else [], compiler_params=_params(2),
    )(qs, ks, vs, *r_ins)
    return res[0], res[1], res[2:]


def sb_bwd(qs, ks, vs, o, do, rider=None):
    tp = qs.shape[0]
    nq = tp // SB_QB
    B, G, hb, QB = SB_BLOCK, SB_GROUP, SB_HEADS_PER_STEP, SB_QB
    assert tp >= G * B and tp % QB == 0 and QB % B == 0 and G * B >= QB
    grid = (HEADS // hb, nq)
    split, ride_first, ride_last = _ride(rider, 5, 3, grid)

    def body(*refs):
        (q_ref, k_ref, v_ref, o_ref, do_ref), (dq_ref, dk_ref, dv_ref), ride = split(refs)
        ride_first(ride)
        i = pl.program_id(1)

        @pl.when(i == 0)
        def _():
            dk_ref[...] = jnp.zeros_like(dk_ref)
            dv_ref[...] = jnp.zeros_like(dv_ref)

        R = range(hb)
        hs = [slice(h * SB_DH, (h + 1) * SB_DH) for h in R]
        q = [q_ref[:, hs[h]] for h in R]
        dob = [do_ref[:, hs[h]].astype(bf16) for h in R]
        et = [_rowsum(dob[h].astype(f32) * o_ref[:, hs[h]]) for h in R]
        us = (lax.broadcasted_iota(jnp.int32, (B, B), 0) > lax.broadcasted_iota(jnp.int32, (B, B), 1)).astype(bf16)

        def make_step(masked):
            def step(t, carry):
                dq, cs, ce = carry
                jlo, rows, kend = _sb_group(i, t)
                kcat = [k_ref[rows, hs[h]] for h in R]
                dwv = [_dot_nt(dob[h], v_ref[rows, hs[h]]) for h in R]
                lsp, vis, w, cs = _sb_weights(q, kcat, i, jlo, kend, cs, us, masked)
                wb = [w[h].astype(bf16) for h in R]
                ee = [dwv[h] * wb[h].astype(f32) for h in R]
                later = [_later_blocks(ee[h], G, ce[h]) for h in R]
                cum = [_cumsum_after(ee[h], G, us) for h in R]
                dz = []
                for h in R:
                    d = ee[h] - jnp.exp(lsp[h]) * (et[h] - (cum[h] + later[h][0]))
                    if masked:
                        d = jnp.where(vis, d, 0.0)
                    dz.append(d.astype(bf16))
                dkj = [_dot_tn(dz[h], q[h]) for h in R]
                dvj = [_dot_tn(wb[h], dob[h]) for h in R]
                dqj = [_dot(dz[h], kcat[h]) for h in R]
                for h in R:
                    dk_ref[rows, hs[h]] += dkj[h]
                    dv_ref[rows, hs[h]] += dvj[h]
                return tuple(dq[h] + dqj[h] for h in R), tuple(cs), tuple(later[h][1] for h in R)
            return step

        z0 = tuple(jnp.zeros((QB, 1), f32) for _ in R)
        dq, _, _ = _sb_loop(i, make_step, (tuple(jnp.zeros((QB, SB_DH), f32) for _ in R), z0, z0))
        for h in R:
            dq_ref[:, hs[h]] = dq[h] * (SB_DH ** -0.5)
        ride_last(ride)

    blk = pl.BlockSpec((QB, hb * SB_DH), lambda g, i: (i, g))
    full = pl.BlockSpec((tp, hb * SB_DH), lambda g, i: (0, g))
    r_ins = rider.ins if rider else []
    r_outs = rider.out_shapes if rider else []
    res = pl.pallas_call(
        body, name="sb_bwd", grid=grid, in_specs=[blk, full, full, blk, blk] + [_ANY] * len(r_ins),
        out_specs=[blk, full, full] + [_ANY] * len(r_outs),
        out_shape=[jax.ShapeDtypeStruct((tp, SB_W), f32)] * 3 + list(r_outs),
        scratch_shapes=rider.scratch if rider else [], compiler_params=_params(2),
    )(qs, ks, vs, o, do, *r_ins)
    return res[:3], res[3:]


def adamw(w, g, m, v, name, rider=None):
    r, c = w.shape
    rt = _tile(r, 128, SUB) if r % SUB == 0 else r
    blk = pl.BlockSpec((rt, c), lambda i: (i, 0))
    c1 = 1.0 - ADAM_B1 ** ADAM_STEP
    c2 = 1.0 - ADAM_B2 ** ADAM_STEP
    grid = (r // rt,)
    split, ride_first, ride_last = _ride(rider, 4, 3, grid)

    def body(*refs):
        (w_ref, g_ref, m_ref, v_ref), (d_ref, mo_ref, vo_ref), ride = split(refs)
        ride_first(ride)
        g_ = g_ref[...]
        m_ = ADAM_B1 * m_ref[...] + (1.0 - ADAM_B1) * g_
        v_ = ADAM_B2 * v_ref[...] + (1.0 - ADAM_B2) * (g_ * g_)
        mo_ref[...] = m_
        vo_ref[...] = v_
        d_ref[...] = -ADAM_LR * ((m_ / c1) / (jnp.sqrt(v_ / c2) + ADAM_EPS) + ADAM_WD * w_ref[...])
        ride_last(ride)

    r_ins = rider.ins if rider else []
    r_outs = rider.out_shapes if rider else []
    res = pl.pallas_call(
        body, name=name, grid=grid, in_specs=[blk] * 4 + [_ANY] * len(r_ins), out_specs=[blk] * 3 + [_ANY] * len(r_outs),
        out_shape=[jax.ShapeDtypeStruct((r, c), f32)] * 3 + list(r_outs),
        scratch_shapes=rider.scratch if rider else [], compiler_params=_params(1),
    )(w, g, m, v, *r_ins)
    return res[:3], res[3:]


def sum_slots(x, name):
    n, r, c = x.shape
    rt = _tile(r, 128, SUB) if r % SUB == 0 else r
    blk = pl.BlockSpec((n, rt, c), lambda i: (0, i, 0))

    def body(x_ref, o_ref):
        acc = x_ref[0].astype(f32)
        for s in range(1, n):
            acc = acc + x_ref[s].astype(f32)
        o_ref[...] = acc

    return pl.pallas_call(
        body, name=name, grid=(r // rt,), in_specs=[blk], out_specs=pl.BlockSpec((rt, c), lambda i: (i, 0)),
        out_shape=jax.ShapeDtypeStruct((r, c), f32), compiler_params=_params(1),
    )(x)


def add2(a, b, name, out_dtype=f32):
    n, r, c = a.shape
    rt = _tile(r, 64, SUB) if r % SUB == 0 else r
    blk = pl.BlockSpec((n, rt, c), lambda i: (0, i, 0))

    def body(a_ref, b_ref, o_ref):
        o_ref[...] = (a_ref[...] + b_ref[...]).astype(out_dtype)

    return pl.pallas_call(
        body, name=name, grid=(r // rt,), in_specs=[blk, blk], out_specs=blk,
        out_shape=jax.ShapeDtypeStruct((n, r, c), out_dtype), compiler_params=_params(1),
    )(a, b)


_ANY = pl.BlockSpec(memory_space=pl.ANY)
_MESH = pl.DeviceIdType.MESH


def _coords():
    return lax.axis_index("x"), lax.axis_index("y"), lax.axis_index("c")


def _chip_peer(x, y, r):
    return x ^ (r >> 1), y ^ (r & 1)


class _Exchange:
    def __init__(self, ins, out_shapes, scratch, start, finish):
        self.ins, self.out_shapes, self.scratch, self.start, self.finish = ins, out_shapes, scratch, start, finish

    def split(self, refs):
        n, m = len(self.ins), len(self.out_shapes)
        return refs[:n], refs[n:n + m], refs[n + m:]


def run_exchange(ex, name):
    def body(*refs):
        ins, outs, sems = ex.split(refs)
        ex.start(ins, outs, sems)
        ex.finish(ins, outs, sems)

    return pl.pallas_call(body, name=name, in_specs=[_ANY] * len(ex.ins), out_specs=[_ANY] * len(ex.out_shapes),
                          out_shape=ex.out_shapes, scratch_shapes=ex.scratch)(*ex.ins)


def gather_chips(big, small):
    nb, n = len(big), len(big) + len(small)
    shards = list(big) + list(small)
    kb = nb * (N_CHIPS - 1)
    k = n * (N_CHIPS - 1)

    def copies(src, dst, sems):
        send, recv, fsend, frecv = sems
        x, y, c = _coords()
        sib = (x, y, 1 - c)
        peers = [_chip_peer(x, y, r) for r in range(1, N_CHIPS)]

        def direct(t, j, slot):
            s = t * (N_CHIPS - 1) + j
            if t < nb:
                return pltpu.make_async_remote_copy(src[t].at[c], dst[t].at[slot, c], send.at[s], recv.at[s],
                                                    device_id=(*peers[j], c), device_id_type=_MESH)
            return pltpu.make_async_remote_copy(src[t], dst[t].at[slot], send.at[s], recv.at[s],
                                                device_id=(*peers[j], c), device_id_type=_MESH)

        def passed(t, j, half):
            s = t * (N_CHIPS - 1) + j
            px, py = peers[j]
            part = dst[t].at[2 * px + py, half]
            return pltpu.make_async_remote_copy(part, part, fsend.at[s], frecv.at[s], device_id=sib, device_id_type=_MESH)

        return direct, passed, peers, 2 * x + y, c

    def start(src, dst, sems):
        direct, _, _, me, _ = copies(src, dst, sems)
        for t in range(n):
            for j in range(N_CHIPS - 1):
                direct(t, j, me).start()

    def finish(src, dst, sems):
        direct, passed, peers, me, c = copies(src, dst, sems)
        fwd = []
        for t in range(nb):
            for j in range(N_CHIPS - 1):
                px, py = peers[j]
                direct(t, j, 2 * px + py).wait_recv()
                fwd.append(passed(t, j, c))
                fwd[-1].start()
        for t in range(nb, n):
            for j in range(N_CHIPS - 1):
                px, py = peers[j]
                direct(t, j, 2 * px + py).wait_recv()
        for t in range(nb):
            for j in range(N_CHIPS - 1):
                passed(t, j, 1 - c).wait_recv()
        for t in range(n):
            for j in range(N_CHIPS - 1):
                direct(t, j, me).wait_send()
        for cp in fwd:
            cp.wait_send()

    return _Exchange(shards, [jax.ShapeDtypeStruct((N_CHIPS,) + s.shape, s.dtype) for s in shards],
                     [pltpu.SemaphoreType.DMA((k,)), pltpu.SemaphoreType.DMA((k,)),
                      pltpu.SemaphoreType.DMA((max(kb, 1),)), pltpu.SemaphoreType.DMA((max(kb, 1),))], start, finish)


def sibling_swap(grads):
    pairs = [(t, o) for t, g in enumerate(grads) for o in range(g.shape[0])]
    k = len(pairs)

    def copies(src, dst, sems):
        send, recv = sems
        x, y, c = _coords()
        return [pltpu.make_async_remote_copy(src[t].at[o, 1 - c], dst[t].at[o], send.at[s], recv.at[s],
                                             device_id=(x, y, 1 - c), device_id_type=_MESH)
                for s, (t, o) in enumerate(pairs)]

    def start(src, dst, sems):
        for cp in copies(src, dst, sems):
            cp.start()

    def finish(src, dst, sems):
        cps = copies(src, dst, sems)
        for cp in cps:
            cp.wait_recv()
        for cp in cps:
            cp.wait_send()

    return _Exchange(list(grads), [jax.ShapeDtypeStruct((g.shape[0],) + g.shape[2:], g.dtype) for g in grads],
                     [pltpu.SemaphoreType.DMA((k,)), pltpu.SemaphoreType.DMA((k,))], start, finish)


def scatter_chips(parts):
    n = len(parts)
    k = n * (N_CHIPS - 1)

    def copy(src, dst, sems, t, r, landing):
        send, recv = sems
        x, y, c = _coords()
        me = 2 * x + y
        px, py = _chip_peer(x, y, r)
        peer = 2 * px + py
        s = t * (N_CHIPS - 1) + r - 1
        return pltpu.make_async_remote_copy(src[t].at[me if landing else peer], dst[t].at[peer if landing else me],
                                            send.at[s], recv.at[s], device_id=(px, py, c), device_id_type=_MESH)

    def start(src, dst, sems):
        for t in range(n):
            for r in range(1, N_CHIPS):
                copy(src, dst, sems, t, r, False).start()

    def finish(src, dst, sems):
        for t in range(n):
            for r in range(1, N_CHIPS):
                copy(src, dst, sems, t, r, True).wait_recv()
        for t in range(n):
            for r in range(1, N_CHIPS):
                copy(src, dst, sems, t, r, False).wait_send()

    return _Exchange(list(parts), [jax.ShapeDtypeStruct(p.shape, p.dtype) for p in parts],
                     [pltpu.SemaphoreType.DMA((k,)), pltpu.SemaphoreType.DMA((k,))], start, finish)


def sibling_send(halves, name):
    n = len(halves)

    def body(*refs):
        src, dst = refs[:n], refs[n:2 * n]
        send, recv = refs[2 * n:]
        x, y, c = _coords()
        cps = [pltpu.make_async_remote_copy(src[t], dst[t], send.at[t], recv.at[t],
                                            device_id=(x, y, 1 - c), device_id_type=_MESH) for t in range(n)]
        for cp in cps:
            cp.start()
        for cp in cps:
            cp.wait_recv()
        for cp in cps:
            cp.wait_send()

    return pl.pallas_call(
        body, name=name, in_specs=[_ANY] * n, out_specs=[_ANY] * n,
        out_shape=[jax.ShapeDtypeStruct(h.shape, h.dtype) for h in halves],
        scratch_shapes=[pltpu.SemaphoreType.DMA((n,)), pltpu.SemaphoreType.DMA((n,))],
    )(*halves)


def gather_all(block):
    def copies(src, dst, sems, landing):
        send, recv, loc = sems
        x, y, c = _coords()
        me = 4 * x + 2 * y + c
        mine = pltpu.make_async_copy(src[0], dst[0].at[me], loc)
        remote = []
        for r in range(1, N_DEV):
            px, py, pc = x ^ (r >> 2), y ^ ((r >> 1) & 1), c ^ (r & 1)
            slot = 4 * px + 2 * py + pc if landing else me
            remote.append(pltpu.make_async_remote_copy(src[0], dst[0].at[slot], send.at[r - 1], recv.at[r - 1],
                                                       device_id=(px, py, pc), device_id_type=_MESH))
        return mine, remote

    def start(src, dst, sems):
        mine, outs = copies(src, dst, sems, False)
        mine.start()
        for cp in outs:
            cp.start()

    def finish(src, dst, sems):
        mine, lands = copies(src, dst, sems, True)
        for cp in lands:
            cp.wait_recv()
        for cp in lands:
            cp.wait_send()
        mine.wait()

    return _Exchange([block], [jax.ShapeDtypeStruct((N_DEV,) + block.shape, block.dtype)],
                     [pltpu.SemaphoreType.DMA((N_DEV - 1,)), pltpu.SemaphoreType.DMA((N_DEV - 1,)),
                      pltpu.SemaphoreType.DMA(())], start, finish)


def _pad_lanes(v, n=LANE):
    return jnp.pad(v, ((0, 0), (0, n - v.shape[1])))


def _w_in_pieces():
    cs = (PROJ_BIG + 2 * HEADS) // N_CHIPS
    ab_end = AB_COL + 2 * HEADS
    out = []
    for o in range(N_CHIPS):
        lo, hi = o * cs, (o + 1) * cs
        cand = [("big", lo, min(hi, AB_COL), 0), ("ab", max(lo, AB_COL), min(hi, ab_end), AB_COL),
                ("big", max(lo, ab_end), hi, 2 * HEADS)]
        out.append([(s, a - off, b - off) for s, a, b, off in cand if a < b])
    return out


def _split_w_in(w4):
    big, ab = [], []
    for o, pieces in enumerate(_w_in_pieces()):
        at = 0
        for s, a, b in pieces:
            (big if s == "big" else ab).append(w4[o][:, at:at + b - a])
            at += b - a
    return jnp.concatenate(big, axis=1), _pad_lanes(jnp.concatenate(ab, axis=1))


def _join_w_in(big, ab):
    src = {"big": big, "ab": ab}
    return jnp.stack([jnp.concatenate([src[s][:, a:b] for s, a, b in pieces], axis=1) for pieces in _w_in_pieces()])


def _conv_w8(w):
    return jnp.pad(w, ((0, SUB - DN_CONV), (0, 0)))


def _row_layout(gc, tp):
    nc = tp // CHUNK
    g = gc[:, :, 0].reshape(HEADS, nc, 1, CHUNK)
    g = jnp.broadcast_to(g, (HEADS, nc, SUB, CHUNK))
    return jnp.pad(g, ((0, 0), (0, 0), (0, 0), (0, LANE - CHUNK))).reshape(HEADS, nc * SUB, LANE)


def _step(x, meta, W, target, late_weights=None, early_swap=None, early_grads=None, last_grads=None):
    W = dict(W)
    seq = x.shape[0]
    tp = P0 + N_META + seq
    h0 = jnp.concatenate([jnp.zeros((P0, D_MODEL), f32), meta, x], axis=0)
    w_big, w_ab = _split_w_in(W["w_in"])
    cq8, ck8, cv8 = _conv_w8(W["conv_q"]), _conv_w8(W["conv_k"]), _conv_w8(W["conv_v"])
    al, dtb = _pad_lanes(W["dn_a_log"]), _pad_lanes(W["dn_dt_bias"])

    n1 = rms_fwd(h0, W["norm_mix_gain"], "rms1_fwd")
    proj = matmul(n1, w_big, "nn", "proj_fwd")
    pab = matmul(n1, w_ab, "nn", "pab_fwd")
    qn = conv_fwd(proj, cq8, C_DQ * 8, 8, True, "conv_q_fwd")
    kn = conv_fwd(proj, ck8, C_DK * 8, 8, True, "conv_k_fwd")
    va = conv_fwd(proj, cv8, C_DV * 8, 16, False, "conv_v_fwd")
    gc, bc = ab_fwd(pab, al, dtb)
    grow = _row_layout(gc, tp)
    o_dn, states = gdn_fwd(qn, kn, va, gc, bc, grow)
    on = dn_out_fwd(o_dn, proj, W["dn_out_norm_gain"])
    qs, ks, vs = sb_prep_fwd(proj, W["sb_q_norm_gain"], W["sb_k_norm_gain"])
    o_sb, o_sb16, arrived = sb_fwd(qs, ks, vs, rider=late_weights[0] if late_weights else None)
    if late_weights:
        W.update(late_weights[1](arrived))
    ydn = matmul(on, W["w_branch_dn"], "nn", "ydn_fwd")
    ysb = matmul(o_sb16, W["w_branch_sb"], "nn", "ysb_fwd")
    merged = merge_fwd(proj, ydn, ysb)
    h1 = matmul(merged, W["w_out"], "nn", "wout_fwd", residual=h0)
    n2 = rms_fwd(h1, W["norm_ffn_gain"], "rms2_fwd")
    u = matmul(n2, W["w_ffn_in"], "nn", "ffn_in_fwd", tn_t=512)
    act = swiglu_fwd(u)
    y = matmul(act, W["w_ffn_out"], "nn", "ffn_out_fwd", residual=h1)
    dy, dy16, loss = loss_head(y, target)

    G = {}
    dact = matmul(dy16, W["w_ffn_out"], "nt", "ffn_out_dx", tn_t=1408)
    G["w_ffn_out"] = matmul(act, dy16, "tn", "ffn_out_dw", tm_t=1408)
    dgate, dup = swiglu_bwd(u, dact)
    du = jnp.concatenate([dgate, dup], axis=1)
    dn2 = matmul(du, W["w_ffn_in"], "nt", "ffn_in_dx", tk_t=512)
    G["w_ffn_in"] = matmul(n2, du, "tn", "ffn_in_dw", tn_t=512)
    dh1, dh1_16, G["norm_ffn_gain"] = rms_bwd(h1, W["norm_ffn_gain"], dn2, dy, "rms2_bwd")
    dmerged = matmul(dh1_16, W["w_out"], "nt", "wout_dx")
    G["w_out"] = matmul(merged, dh1_16, "tn", "wout_dw")
    dyd, dys, d_gates = merge_bwd(proj, ydn, ysb, dmerged)
    don = matmul(dyd, W["w_branch_dn"], "nt", "ydn_dx")
    G["w_branch_dn"] = matmul(on, dyd, "tn", "ydn_dw")
    do_sb = matmul(dys, W["w_branch_sb"], "nt", "ysb_dx")
    G["w_branch_sb"] = matmul(o_sb16, dys, "tn", "ysb_dw")
    do_dn, dproj, G["dn_out_norm_gain"] = dn_out_bwd(o_dn, proj, W["dn_out_norm_gain"], don)
    (dqn, dkn, dva, dgc, dbc), swapped = gdn_bwd(qn, kn, va, gc, bc, grow, states, do_dn,
                                                 rider=early_swap[0](G) if early_swap else None)
    if early_swap:
        early_swap[1](swapped)
    dpab, dal, ddt = ab_bwd(pab, al, dtb, dgc, dbc)
    G["dn_a_log"], G["dn_dt_bias"] = dal[:, :HEADS], ddt[:, :HEADS]
    dyq, dcq = conv_bwd_act(proj, cq8, dqn, C_DQ * 8, 8, True, "conv_q_bwd")
    dyk, dck = conv_bwd_act(proj, ck8, dkn, C_DK * 8, 8, True, "conv_k_bwd")
    dyv, dcv = conv_bwd_act(proj, cv8, dva, C_DV * 8, 16, False, "conv_v_bwd")
    G["conv_q"], G["conv_k"], G["conv_v"] = dcq[:DN_CONV], dck[:DN_CONV], dcv[:DN_CONV]
    dproj = conv_bwd_in(dyq, cq8, dproj, C_DQ * 8, "conv_q_dx")
    dproj = conv_bwd_in(dyk, ck8, dproj, C_DK * 8, "conv_k_dx")
    dproj = conv_bwd_in(dyv, cv8, dproj, C_DV * 8, "conv_v_dx")
    (dqs, dks, dvs), delivered = sb_bwd(qs, ks, vs, o_sb, do_sb, rider=early_grads[0](G) if early_grads else None)
    if early_grads:
        early_grads[1](delivered)
    dproj, G["sb_q_norm_gain"], G["sb_k_norm_gain"] = sb_prep_bwd(
        proj, W["sb_q_norm_gain"], W["sb_k_norm_gain"], dqs, dks, dvs, dproj)
    dproj = lax.dynamic_update_slice(dproj, d_gates, (0, C_GDN * 1024))
    dw_big = matmul(n1, dproj, "tn", "proj_dw")
    dw_ab = matmul(n1, dpab, "tn", "pab_dw")
    G["w_in"] = (dw_big, dw_ab)
    if last_grads:
        dn1, delivered = matmul(dproj, w_big, "nt", "proj_dx", tk_t=1024, rider=last_grads[0](G))
        last_grads[1](delivered)
    else:
        dn1 = matmul(dproj, w_big, "nt", "proj_dx", tk_t=1024)
    dn1 = matmul(dpab, w_ab, "nt", "pab_dx", residual=dn1)
    dh0, _, G["norm_mix_gain"] = rms_bwd(h0, W["norm_mix_gain"], dn1, dh1, "rms1_bwd")
    G["meta_tokens"] = dh0[P0:P0 + N_META]
    return loss, dh0[P0 + N_META:], G


_BIG = ("w_in", "w_branch_dn", "w_branch_sb", "w_out", "w_ffn_in", "w_ffn_out")
_COL_SHARDED = ("w_in", "w_ffn_in", "meta_tokens", "conv_q", "conv_k", "conv_v")
_SMALL_REPL = ("norm_mix_gain", "norm_ffn_gain", "dn_a_log", "dn_dt_bias", "dn_out_norm_gain", "sb_q_norm_gain",
               "sb_k_norm_gain")
_SMALL_SHARD = ("meta_tokens", "conv_q", "conv_k", "conv_v")
_ORDER = ("meta_tokens", "norm_mix_gain", "w_in", "conv_q", "conv_k", "conv_v", "dn_a_log", "dn_dt_bias",
          "dn_out_norm_gain", "sb_q_norm_gain", "sb_k_norm_gain", "w_branch_dn", "w_branch_sb", "w_out",
          "norm_ffn_gain", "w_ffn_in", "w_ffn_out")


def _unshard(g4, name):
    if name in _COL_SHARDED:
        r, cs = g4.shape[1:]
        return jnp.transpose(g4, (1, 0, 2)).reshape(r, N_CHIPS * cs)
    return g4.reshape((-1,) + g4.shape[2:])


def _to_shards(full, name):
    if name in _COL_SHARDED:
        r, c = full.shape
        return jnp.transpose(full.reshape(r, N_CHIPS, c // N_CHIPS), (1, 0, 2))
    r, c = full.shape
    return full.reshape(N_CHIPS, r // N_CHIPS, c)


def _rows_1024(a):
    r, c = a.shape
    if c >= 1024:
        return a.reshape(r * (c // 1024), 1024)
    return jnp.pad(a, ((0, 0), (0, 1024 - c)))


def kernel(x, meta_tokens, norm_mix_gain, w_in, conv_q, conv_k, conv_v, dn_a_log, dn_dt_bias, dn_out_norm_gain, sb_q_norm_gain, sb_k_norm_gain, w_branch_dn, w_branch_sb, w_out, norm_ffn_gain, w_ffn_in, w_ffn_out, loss_target, m_meta_tokens, m_norm_mix_gain, m_w_in, m_conv_q, m_conv_k, m_conv_v, m_dn_a_log, m_dn_dt_bias, m_dn_out_norm_gain, m_sb_q_norm_gain, m_sb_k_norm_gain, m_w_branch_dn, m_w_branch_sb, m_w_out, m_norm_ffn_gain, m_w_ffn_in, m_w_ffn_out, v_meta_tokens, v_norm_mix_gain, v_w_in, v_conv_q, v_conv_k, v_conv_v, v_dn_a_log, v_dn_dt_bias, v_dn_out_norm_gain, v_sb_q_norm_gain, v_sb_k_norm_gain, v_w_branch_dn, v_w_branch_sb, v_w_out, v_norm_ffn_gain, v_w_ffn_in, v_w_ffn_out):
    Wl = dict(meta_tokens=meta_tokens, norm_mix_gain=norm_mix_gain, w_in=w_in[0], conv_q=conv_q[0], conv_k=conv_k[0],
              conv_v=conv_v[0], dn_a_log=dn_a_log, dn_dt_bias=dn_dt_bias, dn_out_norm_gain=dn_out_norm_gain,
              sb_q_norm_gain=sb_q_norm_gain, sb_k_norm_gain=sb_k_norm_gain, w_branch_dn=w_branch_dn[0],
              w_branch_sb=w_branch_sb[0], w_out=w_out[0], norm_ffn_gain=norm_ffn_gain, w_ffn_in=w_ffn_in[0],
              w_ffn_out=w_ffn_out[0])
    Ml = dict(meta_tokens=m_meta_tokens, norm_mix_gain=m_norm_mix_gain, w_in=m_w_in[0], conv_q=m_conv_q[0],
              conv_k=m_conv_k[0], conv_v=m_conv_v[0], dn_a_log=m_dn_a_log, dn_dt_bias=m_dn_dt_bias,
              dn_out_norm_gain=m_dn_out_norm_gain, sb_q_norm_gain=m_sb_q_norm_gain, sb_k_norm_gain=m_sb_k_norm_gain,
              w_branch_dn=m_w_branch_dn[0], w_branch_sb=m_w_branch_sb[0], w_out=m_w_out[0],
              norm_ffn_gain=m_norm_ffn_gain, w_ffn_in=m_w_ffn_in[0], w_ffn_out=m_w_ffn_out[0])
    Vl = dict(meta_tokens=v_meta_tokens, norm_mix_gain=v_norm_mix_gain, w_in=v_w_in[0], conv_q=v_conv_q[0],
              conv_k=v_conv_k[0], conv_v=v_conv_v[0], dn_a_log=v_dn_a_log, dn_dt_bias=v_dn_dt_bias,
              dn_out_norm_gain=v_dn_out_norm_gain, sb_q_norm_gain=v_sb_q_norm_gain, sb_k_norm_gain=v_sb_k_norm_gain,
              w_branch_dn=v_w_branch_dn[0], w_branch_sb=v_w_branch_sb[0], w_out=v_w_out[0],
              norm_ffn_gain=v_norm_ffn_gain, w_ffn_in=v_w_ffn_in[0], w_ffn_out=v_w_ffn_out[0])
    lead = {n: (1,) if (n in _BIG or n in ("conv_q", "conv_k", "conv_v")) else () for n in _ORDER}

    chip = 2 * lax.axis_index("x") + lax.axis_index("y")
    c = lax.axis_index("c")
    halved = {n: Wl[n].astype(bf16).reshape(2, Wl[n].shape[0] // 2, Wl[n].shape[1]) for n in _BIG}

    def gathered_weights(names, owns, outs):
        res = {}
        for n, own, g4 in zip(names, owns, outs):
            g4 = lax.dynamic_update_slice(g4, own[None], (chip,) + (0,) * own.ndim)
            if n in _BIG:
                g4 = g4.reshape(N_CHIPS, 2 * g4.shape[2], g4.shape[3])
            res[n] = g4 if n == "w_in" else _unshard(g4, n)
        return res

    first = ["w_in"] + list(_SMALL_SHARD)
    first_own = [halved["w_in"]] + [Wl[n] for n in _SMALL_SHARD]
    W = dict(Wl)
    W.update(gathered_weights(first, first_own, run_exchange(gather_chips(first_own[:1], first_own[1:]), "gather_w_in")))
    late = [n for n in _BIG if n != "w_in"]
    late_own = [halved[n] for n in late]
    for n in late:
        del W[n]

    def halves_of(names, G):
        g4 = [_to_shards(G[n], n) for n in names]
        return [g.reshape(N_CHIPS, 2, g.shape[1] // 2, g.shape[2]) for g in g4]

    def pair_added(g42, from_sib, tag, wire):
        mine = [lax.dynamic_index_in_dim(g, c, axis=1, keepdims=False) for g in g42]
        return [add2(a, b, "grad_pair_add_%s%d" % (tag, t), out_dtype=wire)
                for t, (a, b) in enumerate(zip(mine, from_sib))]

    def chip_reduced(parts, slots, tag):
        slots = [lax.dynamic_update_slice(s, lax.dynamic_index_in_dim(p, chip, axis=0, keepdims=True), (chip, 0, 0))
                 for s, p in zip(slots, parts)]
        return [sum_slots(s, "grad_chip_sum_%s%d" % (tag, t)) for t, s in enumerate(slots)]

    early, last = {}, {}

    def early_swap_begin(G):
        early["g42"] = halves_of(late, G)
        return sibling_swap(early["g42"])

    def early_begin(G):
        early["parts"] = pair_added(early["g42"], early["from_sib"], "a", f32)
        return scatter_chips(early["parts"])

    def last_begin(G):
        g2 = [g.reshape(1, 2, g.shape[0] // 2, g.shape[1]) for g in G["w_in"]]
        added = pair_added(g2, run_exchange(sibling_swap(g2), "grad_sibling_swap_b"), "b", bf16)
        last["parts"] = [_join_w_in(added[0][0], added[1][0])]
        return scatter_chips(last["parts"])

    loss, grad_x, G = _step(
        x[0], W["meta_tokens"], W, loss_target[0],
        late_weights=(gather_chips(late_own, []), lambda outs: gathered_weights(late, late_own, outs)),
        early_swap=(early_swap_begin, lambda outs: early.update(from_sib=outs)),
        early_grads=(early_begin, lambda slots: early.update(halves=chip_reduced(early["parts"], slots, "a"))),
        last_grads=(last_begin, lambda slots: last.update(halves=chip_reduced(last["parts"], slots, "b"))))
    halves = last["halves"] + early["halves"]
    theirs = sibling_send(halves, "grad_sibling_send")
    Gs = {}
    for n, h, o in zip(["w_in"] + late, halves, theirs):
        Gs[n] = lax.dynamic_update_slice(jnp.concatenate([o, o], axis=0), h, (c * h.shape[0], 0))

    small_names = list(_SMALL_REPL) + list(_SMALL_SHARD)
    pieces = [_rows_1024(G[n]) for n in small_names] + [_rows_1024(loss)]
    counts = [p.shape[0] for p in pieces]
    pack = jnp.concatenate(pieces, axis=0)
    pad_rows = (-pack.shape[0]) % SUB
    pack = jnp.pad(pack, ((0, pad_rows), (0, 0)))
    adam = {"w_in": adamw(Wl["w_in"], Gs["w_in"], Ml["w_in"], Vl["w_in"], "adamw_w_in", rider=gather_all(pack))}
    total = sum_slots(adam["w_in"][1][0], "small_sum")
    row = 0
    for n, cnt in zip(small_names, counts[:-1]):
        blk = total[row:row + cnt]
        row += cnt
        full_shape = G[n].shape
        if full_shape[1] >= 1024:
            blk = blk.reshape(full_shape)
        else:
            blk = blk[:, :full_shape[1]]
        if n in _SMALL_SHARD:
            cs = full_shape[1] // N_CHIPS
            blk = lax.dynamic_slice_in_dim(blk, chip * cs, cs, axis=1)
        Gs[n] = blk
    loss_out = total[row, 0]

    grads, deltas, new_m, new_v = [], [], [], []
    for n in _ORDER:
        d, m2, v2 = (adam[n] if n in adam else adamw(Wl[n], Gs[n], Ml[n], Vl[n], "adamw_" + n))[0]
        shape = lead[n] + Wl[n].shape
        grads.append(Gs[n].reshape(shape))
        deltas.append(d.reshape(shape))
        new_m.append(m2.reshape(shape))
        new_v.append(v2.reshape(shape))
    return (loss_out, grad_x[None], *grads, *deltas, *new_m, *new_v)
```

```python
import jax
import jax.numpy as jnp
from jax import lax
from jax.experimental import pallas as pl
from jax.experimental.pallas import tpu as pltpu

f32 = jnp.float32
bf16 = jnp.bfloat16

D_MODEL = 1024
N_META = 16
CHUNK = 64
HEADS = 8
DN_DK = 128
DN_DV = 256
DN_CONV = 4
DN_QK = HEADS * DN_DK
DN_V = HEADS * DN_DV
SB_DH = 128
SB_W = HEADS * SB_DH
SB_BLOCK = 128
SB_QB = 384
SB_GROUP = 4
SB_HEADS_PER_STEP = 2
SB_FWD_HEADS_PER_STEP = 4
GDN_HEADS_PER_STEP = 8
CONV_W = 2048
D_FF = 2816
RMS_EPS = 1e-6
L2_EPS = 1e-6
ADAM_LR = 0.001
ADAM_B1 = 0.9
ADAM_B2 = 0.999
ADAM_EPS = 1e-08
ADAM_WD = 0.01
ADAM_STEP = 10

P0 = 112
LANE = 128
SUB = 8
VMEM_LIMIT = 48 * 1024 * 1024
N_CHIPS = 4
N_DEV = 8

C_DQ, C_DK, C_DV, C_DZ, C_SQ, C_SK, C_SV, C_GDN, C_GSB = 0, 1, 2, 4, 6, 7, 8, 9, 10
PROJ_BIG = 11 * 1024
AB_COL = 2 * DN_QK + 2 * DN_V


def _params(n_axes):
    return pltpu.CompilerParams(dimension_semantics=("arbitrary",) * n_axes, vmem_limit_bytes=VMEM_LIMIT)


def _tile(n, target, q=LANE):
    best = None
    for t in range(q, min(n, target) + 1, q):
        if n % t == 0:
            best = t
    return best if best is not None else n


def _dot(a, b):
    return jnp.dot(a.astype(bf16), b.astype(bf16), preferred_element_type=f32)


def _dot_nt(a, b):
    return lax.dot_general(a.astype(bf16), b.astype(bf16), (((1,), (1,)), ((), ())), preferred_element_type=f32)


def _dot_tn(a, b):
    return lax.dot_general(a.astype(bf16), b.astype(bf16), (((0,), (0,)), ((), ())), preferred_element_type=f32)


_HI = lax.Precision.HIGH


def _hdot(a, b):
    return jnp.dot(a, b, precision=_HI, preferred_element_type=f32)


def _hdot_nt(a, b):
    return lax.dot_general(a, b, (((1,), (1,)), ((), ())), precision=_HI, preferred_element_type=f32)


def _hdot_tn(a, b):
    return lax.dot_general(a, b, (((0,), (0,)), ((), ())), precision=_HI, preferred_element_type=f32)


def _sigmoid(x):
    return 0.5 * jnp.tanh(0.5 * x) + 0.5


def _log1p_small(e):
    return jnp.where(e < 1e-3, e * (1.0 - e * (0.5 - e * (1.0 / 3.0))), jnp.log(1.0 + e))


def _rowsum(x):
    return jnp.sum(x, axis=1, keepdims=True)


def _allsum(x):
    return jnp.sum(jnp.sum(x, axis=1, keepdims=True), axis=0, keepdims=True)


def matmul(a, b, mode, name, residual=None, out_dtype=f32, tm_t=1408, tn_t=1024, tk_t=1408, rider=None):
    if mode == "nn":
        (M, K), (K2, N) = a.shape, b.shape
    elif mode == "nt":
        (M, K), (N, K2) = a.shape, b.shape
    else:
        (K, M), (K2, N) = a.shape, b.shape
    assert K == K2, (a.shape, b.shape, mode)
    tm, tn, tk = _tile(M, tm_t), _tile(N, tn_t), _tile(K, tk_t)
    nk = K // tk
    if mode == "nn":
        a_spec = pl.BlockSpec((tm, tk), lambda i, j, k: (i, k))
        b_spec = pl.BlockSpec((tk, tn), lambda i, j, k: (k, j))
        dims = (((1,), (0,)), ((), ()))
    elif mode == "nt":
        a_spec = pl.BlockSpec((tm, tk), lambda i, j, k: (i, k))
        b_spec = pl.BlockSpec((tn, tk), lambda i, j, k: (j, k))
        dims = (((1,), (1,)), ((), ()))
    else:
        a_spec = pl.BlockSpec((tk, tm), lambda i, j, k: (k, i))
        b_spec = pl.BlockSpec((tk, tn), lambda i, j, k: (k, j))
        dims = (((0,), (0,)), ((), ()))
    o_spec = pl.BlockSpec((tm, tn), lambda i, j, k: (i, j))
    has_res = residual is not None
    grid = (M // tm, N // tn, nk)
    split, ride_first, ride_last = _ride(rider, 3 if has_res else 2, 1, grid)

    def body(*refs):
        ins_, (o_ref,), (rin, rout, rest) = split(refs)
        a_ref, b_ref = ins_[:2]
        r_ref = ins_[2] if has_res else None
        acc_ref, ride = rest[0], (rin, rout, rest[1:])
        ride_first(ride)
        k = pl.program_id(2)

        @pl.when(k == 0)
        def _():
            acc_ref[...] = jnp.zeros_like(acc_ref)

        acc_ref[...] += lax.dot_general(a_ref[...].astype(bf16), b_ref[...].astype(bf16), dims,
                                        preferred_element_type=f32)

        @pl.when(k == nk - 1)
        def _():
            r = acc_ref[...]
            if has_res:
                r = r + r_ref[...]
            o_ref[...] = r.astype(out_dtype)

        ride_last(ride)

    ins = [a, b] + ([residual] if has_res else [])
    specs = [a_spec, b_spec] + ([o_spec] if has_res else [])
    r_ins = rider.ins if rider else []
    r_outs = rider.out_shapes if rider else []
    res = pl.pallas_call(
        body, name=name, grid=grid, in_specs=specs + [_ANY] * len(r_ins), out_specs=[o_spec] + [_ANY] * len(r_outs),
        out_shape=[jax.ShapeDtypeStruct((M, N), out_dtype)] + list(r_outs),
        scratch_shapes=[pltpu.VMEM((tm, tn), f32)] + (rider.scratch if rider else []), compiler_params=_params(3),
    )(*ins, *r_ins)
    return (res[0], res[1:]) if rider else res[0]


def _row_tile(tp):
    return _tile(tp, 512)


def rms_fwd(h, gain, name):
    tp, d = h.shape
    rt = _row_tile(tp)

    def body(h_ref, g_ref, o_ref):
        x = h_ref[...]
        r = lax.rsqrt(jnp.mean(x * x, axis=-1, keepdims=True) + RMS_EPS)
        o_ref[...] = (x * r * g_ref[...]).astype(bf16)

    return pl.pallas_call(
        body, name=name, grid=(tp // rt,),
        in_specs=[pl.BlockSpec((rt, d), lambda i: (i, 0)), pl.BlockSpec((1, d), lambda i: (0, 0))],
        out_specs=pl.BlockSpec((rt, d), lambda i: (i, 0)),
        out_shape=jax.ShapeDtypeStruct((tp, d), bf16), compiler_params=_params(1),
    )(h, gain)


def rms_bwd(h, gain, dn, dres, name):
    tp, d = h.shape
    rt = _row_tile(tp)

    def body(h_ref, g_ref, dn_ref, dr_ref, dh_ref, dhb_ref, dg_ref):
        i = pl.program_id(0)
        x = h_ref[...]
        r = lax.rsqrt(jnp.mean(x * x, axis=-1, keepdims=True) + RMS_EPS)
        xh = x * r
        dn_ = dn_ref[...]
        dxh = dn_ * g_ref[...]
        dh = r * (dxh - xh * jnp.mean(dxh * xh, axis=-1, keepdims=True)) + dr_ref[...]
        dh_ref[...] = dh
        dhb_ref[...] = dh.astype(bf16)
        part = jnp.sum(dn_ * xh, axis=0, keepdims=True)

        @pl.when(i == 0)
        def _():
            dg_ref[...] = part

        @pl.when(i > 0)
        def _():
            dg_ref[...] += part

    row = pl.BlockSpec((rt, d), lambda i: (i, 0))
    vec = pl.BlockSpec((1, d), lambda i: (0, 0))
    return pl.pallas_call(
        body, name=name, grid=(tp // rt,), in_specs=[row, vec, row, row], out_specs=[row, row, vec],
        out_shape=[jax.ShapeDtypeStruct((tp, d), f32), jax.ShapeDtypeStruct((tp, d), bf16),
                   jax.ShapeDtypeStruct((1, d), f32)],
        compiler_params=_params(1),
    )(h, gain, dn, dres)


def loss_head(y, target):
    tp, d = y.shape
    lead = P0 + N_META
    rt = _row_tile(tp)
    ns = rt // lead
    assert lead == SB_BLOCK and rt % lead == 0 and target.shape == (tp - lead, d)
    last = target.shape[0] // lead - 1

    def body(*refs):
        y_ref, t_refs = refs[0], refs[1:1 + ns]
        dy_ref, dyb_ref, l_ref = refs[1 + ns:]
        i = pl.program_id(0)

        @pl.when(i == 0)
        def _():
            l_ref[...] = jnp.zeros_like(l_ref)

        part = jnp.zeros((1, 1), f32)
        for s in range(ns):
            rows = slice(s * lead, (s + 1) * lead)
            err = y_ref[rows, :] - t_refs[s][...]
            if s == 0:
                err = err * (i > 0).astype(f32)
            dy = err * (1.0 / d)
            dy_ref[rows, :] = dy
            dyb_ref[rows, :] = dy.astype(bf16)
            part = part + _allsum(err * err)
        l_ref[...] += jnp.broadcast_to(part * (0.5 / d), l_ref.shape)

    row = pl.BlockSpec((rt, d), lambda i: (i, 0))
    t_specs = [pl.BlockSpec((lead, d), lambda i, s=s: (jnp.clip(ns * i + s - 1, 0, last), 0)) for s in range(ns)]
    return pl.pallas_call(
        body, name="loss_head", grid=(tp // rt,), in_specs=[row] + t_specs,
        out_specs=[row, row, pl.BlockSpec((1, LANE), lambda i: (0, 0))],
        out_shape=[jax.ShapeDtypeStruct((tp, d), f32), jax.ShapeDtypeStruct((tp, d), bf16),
                   jax.ShapeDtypeStruct((1, LANE), f32)],
        compiler_params=_params(1),
    )(y, *([target] * ns))


def swiglu_fwd(u):
    tp = u.shape[0]
    rt, cb = _row_tile(tp), D_FF // 2
    nb = D_FF // cb

    def body(g_ref, u_ref, o_ref):
        g = g_ref[...]
        o_ref[...] = (g * _sigmoid(g) * u_ref[...]).astype(bf16)

    return pl.pallas_call(
        body, name="swiglu_fwd", grid=(tp // rt, nb),
        in_specs=[pl.BlockSpec((rt, cb), lambda i, j: (i, j)), pl.BlockSpec((rt, cb), lambda i, j: (i, j + nb))],
        out_specs=pl.BlockSpec((rt, cb), lambda i, j: (i, j)),
        out_shape=jax.ShapeDtypeStruct((tp, D_FF), bf16), compiler_params=_params(2),
    )(u, u)


def swiglu_bwd(u, dact):
    tp = u.shape[0]
    rt, cb = _row_tile(tp), D_FF // 2
    nb = D_FF // cb

    def body(g_ref, u_ref, da_ref, dg_ref, du_ref):
        g = g_ref[...]
        s = _sigmoid(g)
        da = da_ref[...]
        dg_ref[...] = (da * u_ref[...] * s * (1.0 + g * (1.0 - s))).astype(bf16)
        du_ref[...] = (da * g * s).astype(bf16)

    lo = pl.BlockSpec((rt, cb), lambda i, j: (i, j))
    hi = pl.BlockSpec((rt, cb), lambda i, j: (i, j + nb))
    dgate, dup = pl.pallas_call(
        body, name="swiglu_bwd", grid=(tp // rt, nb), in_specs=[lo, hi, lo], out_specs=[lo, lo],
        out_shape=[jax.ShapeDtypeStruct((tp, D_FF), bf16)] * 2, compiler_params=_params(2),
    )(u, u, dact)
    return dgate, dup


def merge_fwd(proj, ydn, ysb):
    tp = proj.shape[0]
    rt, d = _row_tile(tp), D_MODEL

    def body(gd_ref, gs_ref, yd_ref, ys_ref, o_ref):
        o_ref[...] = (_sigmoid(gd_ref[...]) * yd_ref[...] + _sigmoid(gs_ref[...]) * ys_ref[...]).astype(bf16)

    row = pl.BlockSpec((rt, d), lambda i: (i, 0))
    return pl.pallas_call(
        body, name="merge_fwd", grid=(tp // rt,),
        in_specs=[pl.BlockSpec((rt, d), lambda i: (i, C_GDN)), pl.BlockSpec((rt, d), lambda i: (i, C_GSB)), row, row],
        out_specs=row, out_shape=jax.ShapeDtypeStruct((tp, d), bf16), compiler_params=_params(1),
    )(proj, proj, ydn, ysb)


def merge_bwd(proj, ydn, ysb, dm):
    tp = proj.shape[0]
    rt, d = _row_tile(tp), D_MODEL

    def body(gd_ref, gs_ref, yd_ref, ys_ref, dm_ref, dyd_ref, dys_ref, dg_ref):
        dm_ = dm_ref[...]
        sd = _sigmoid(gd_ref[...])
        ss = _sigmoid(gs_ref[...])
        dyd_ref[...] = (dm_ * sd).astype(bf16)
        dys_ref[...] = (dm_ * ss).astype(bf16)
        dg_ref[:, :d] = (dm_ * yd_ref[...] * sd * (1.0 - sd)).astype(bf16)
        dg_ref[:, d:] = (dm_ * ys_ref[...] * ss * (1.0 - ss)).astype(bf16)

    row = pl.BlockSpec((rt, d), lambda i: (i, 0))
    return pl.pallas_call(
        body, name="merge_bwd", grid=(tp // rt,),
        in_specs=[pl.BlockSpec((rt, d), lambda i: (i, C_GDN)), pl.BlockSpec((rt, d), lambda i: (i, C_GSB)), row, row, row],
        out_specs=[row, row, pl.BlockSpec((rt, 2 * d), lambda i: (i, 0))],
        out_shape=[jax.ShapeDtypeStruct((tp, d), bf16)] * 2 + [jax.ShapeDtypeStruct((tp, 2 * d), bf16)],
        compiler_params=_params(1),
    )(proj, proj, ydn, ysb, dm)


def dn_out_fwd(o, proj, gain):
    tp = o.shape[0]
    rt, cb, wide = _row_tile(tp), DN_DV, 1024
    zb = C_DZ * 1024 // wide

    def body(o_ref, z_ref, g_ref, y_ref):
        for s in range(wide // cb):
            sl = slice(s * cb, (s + 1) * cb)
            x = o_ref[:, sl]
            r = lax.rsqrt(jnp.mean(x * x, axis=-1, keepdims=True) + RMS_EPS)
            z = z_ref[:, sl]
            y_ref[:, sl] = (x * r * g_ref[...] * (z * _sigmoid(z))).astype(bf16)

    blk = pl.BlockSpec((rt, wide), lambda i, j: (i, j))
    return pl.pallas_call(
        body, name="dn_out_fwd", grid=(tp // rt, DN_V // wide),
        in_specs=[blk, pl.BlockSpec((rt, wide), lambda i, j: (i, j + zb)), pl.BlockSpec((1, cb), lambda i, j: (0, 0))],
        out_specs=blk, out_shape=jax.ShapeDtypeStruct((tp, DN_V), bf16), compiler_params=_params(2),
    )(o, proj, gain)


def dn_out_bwd(o, proj, gain, dy):
    tp = o.shape[0]
    rt, cb, wide = _row_tile(tp), DN_DV, 1024
    zb = C_DZ * 1024 // wide

    def body(o_ref, z_ref, g_ref, dy_ref, do_ref, dz_ref, dg_ref):
        i, j = pl.program_id(0), pl.program_id(1)
        g = g_ref[...]
        part = jnp.zeros((1, cb), f32)
        for hh in range(wide // cb):
            sl = slice(hh * cb, (hh + 1) * cb)
            x = o_ref[:, sl]
            r = lax.rsqrt(jnp.mean(x * x, axis=-1, keepdims=True) + RMS_EPS)
            xh = x * r
            z = z_ref[:, sl]
            s = _sigmoid(z)
            dy_ = dy_ref[:, sl]
            drn = dy_ * (z * s)
            dz_ref[:, sl] = (dy_ * xh * g * s * (1.0 + z * (1.0 - s))).astype(bf16)
            dxh = drn * g
            do_ref[:, sl] = r * (dxh - xh * jnp.mean(dxh * xh, axis=-1, keepdims=True))
            part = part + jnp.sum(drn * xh, axis=0, keepdims=True)
        first = jnp.logical_and(i == 0, j == 0)

        @pl.when(first)
        def _():
            dg_ref[...] = part

        @pl.when(jnp.logical_not(first))
        def _():
            dg_ref[...] += part

    blk = pl.BlockSpec((rt, wide), lambda i, j: (i, j))
    vec = pl.BlockSpec((1, cb), lambda i, j: (0, 0))
    return pl.pallas_call(
        body, name="dn_out_bwd", grid=(tp // rt, DN_V // wide),
        in_specs=[blk, pl.BlockSpec((rt, wide), lambda i, j: (i, j + zb)), vec, blk],
        out_specs=[blk, pl.BlockSpec((rt, wide), lambda i, j: (i, j + zb)), vec],
        out_shape=[jax.ShapeDtypeStruct((tp, DN_V), f32), jax.ShapeDtypeStruct((tp, PROJ_BIG), bf16),
                   jax.ShapeDtypeStruct((1, cb), f32)],
        compiler_params=_params(2),
    )(o, proj, gain, dy)


def sb_prep_fwd(proj, gq, gk):
    tp = proj.shape[0]
    rt, cb = _row_tile(tp), SB_DH

    def body(q_ref, k_ref, v_ref, gq_ref, gk_ref, qo_ref, ko_ref, vo_ref):
        for x_ref, g_ref, o_ref in ((q_ref, gq_ref, qo_ref), (k_ref, gk_ref, ko_ref)):
            for h in range(HEADS):
                sl = slice(h * cb, (h + 1) * cb)
                x = x_ref[:, sl]
                r = lax.rsqrt(jnp.mean(x * x, axis=-1, keepdims=True) + RMS_EPS)
                o_ref[:, sl] = (x * r * g_ref[...]).astype(bf16)
        vo_ref[...] = v_ref[...].astype(bf16)

    blk = pl.BlockSpec((rt, SB_W), lambda i: (i, 0))
    vec = pl.BlockSpec((1, cb), lambda i: (0, 0))
    return pl.pallas_call(
        body, name="sb_prep_fwd", grid=(tp // rt,),
        in_specs=[pl.BlockSpec((rt, SB_W), lambda i: (i, C_SQ)), pl.BlockSpec((rt, SB_W), lambda i: (i, C_SK)),
                  pl.BlockSpec((rt, SB_W), lambda i: (i, C_SV)), vec, vec],
        out_specs=[blk] * 3, out_shape=[jax.ShapeDtypeStruct((tp, SB_W), bf16)] * 3, compiler_params=_params(1),
    )(proj, proj, proj, gq, gk)


def sb_prep_bwd(proj, gq, gk, dqs, dks, dvs, into):
    tp = proj.shape[0]
    rt, cb = _row_tile(tp), SB_DH
    assert (C_SQ * 1024) % (3 * SB_W) == 0 and (C_SQ + 1, C_SQ + 2) == (C_SK, C_SV)

    def body(q_ref, k_ref, gq_ref, gk_ref, dq_ref, dk_ref, dv_ref, into_ref, do_ref, dgq_ref, dgk_ref):
        first = pl.program_id(0) == 0
        do_ref[:, 2 * SB_W:] = dv_ref[...].astype(bf16)
        for x_ref, g_ref, dn_ref, at, dg_ref, mul in ((q_ref, gq_ref, dq_ref, 0, dgq_ref, None),
                                                      (k_ref, gk_ref, dk_ref, SB_W, dgk_ref, SB_DH ** -0.5)):
            part = jnp.zeros((1, cb), f32)
            for h in range(HEADS):
                sl = slice(h * cb, (h + 1) * cb)
                x = x_ref[:, sl]
                r = lax.rsqrt(jnp.mean(x * x, axis=-1, keepdims=True) + RMS_EPS)
                xh = x * r
                dn_ = dn_ref[:, sl] if mul is None else dn_ref[:, sl] * mul
                dxh = dn_ * g_ref[...]
                do_ref[:, at + h * cb:at + (h + 1) * cb] = (
                    r * (dxh - xh * jnp.mean(dxh * xh, axis=-1, keepdims=True))).astype(bf16)
                part = part + jnp.sum(dn_ * xh, axis=0, keepdims=True)

            @pl.when(first)
            def _(dg_ref=dg_ref, part=part):
                dg_ref[...] = part

            @pl.when(jnp.logical_not(first))
            def _(dg_ref=dg_ref, part=part):
                dg_ref[...] += part

    blk = pl.BlockSpec((rt, SB_W), lambda i: (i, 0))
    vec = pl.BlockSpec((1, cb), lambda i: (0, 0))
    return pl.pallas_call(
        body, name="sb_prep_bwd", grid=(tp // rt,),
        in_specs=[pl.BlockSpec((rt, SB_W), lambda i: (i, C_SQ)), pl.BlockSpec((rt, SB_W), lambda i: (i, C_SK)),
                  vec, vec, blk, blk, blk, pl.BlockSpec(memory_space=pl.ANY)],
        out_specs=[pl.BlockSpec((rt, 3 * SB_W), lambda i: (i, C_SQ * 1024 // (3 * SB_W))), vec, vec],
        out_shape=[jax.ShapeDtypeStruct(into.shape, into.dtype)] + [jax.ShapeDtypeStruct((1, cb), f32)] * 2,
        input_output_aliases={7: 0}, compiler_params=_params(1),
    )(proj, proj, gq, gk, dqs, dks, dvs, into)


def _conv_taps(ext, rt):
    taps = []
    for k in range(DN_CONV):
        s = DN_CONV - 1 - k
        taps.append((pltpu.roll(ext, s, axis=0) if s else ext)[SUB:SUB + rt])
    return taps


def _conv_act(taps, w, l2):
    y = taps[0] * w[0:1]
    for k in range(1, DN_CONV):
        y = y + taps[k] * w[k:k + 1]
    s = _sigmoid(y)
    a = y * s
    if l2:
        n = lax.rsqrt(jnp.sum(a * a, axis=-1, keepdims=True) + L2_EPS)
        return y, s, a, n
    return y, s, a, None


def conv_fwd(proj, w8, col_blk, ncb, l2, name):
    tp = proj.shape[0]
    rt = _row_tile(tp)
    hb = rt // SUB
    cw = min(CONV_W, ncb * LANE)
    cb0 = col_blk * LANE // cw

    def body(x_ref, h_ref, w_ref, o_ref):
        i = pl.program_id(1)
        first = (i > 0).astype(f32)
        for s in range(cw // LANE):
            sl = slice(s * LANE, (s + 1) * LANE)
            ext = jnp.concatenate([h_ref[:, sl] * first, x_ref[:, sl]], axis=0)
            _, _, a, n = _conv_act(_conv_taps(ext, rt), w_ref[:, sl], l2)
            o_ref[:, sl] = a * n if l2 else a

    return pl.pallas_call(
        body, name=name, grid=(ncb * LANE // cw, tp // rt),
        in_specs=[pl.BlockSpec((rt, cw), lambda j, i: (i, j + cb0)),
                  pl.BlockSpec((SUB, cw), lambda j, i: (jnp.maximum(i * hb - 1, 0), j + cb0)),
                  pl.BlockSpec((SUB, cw), lambda j, i: (0, j))],
        out_specs=pl.BlockSpec((rt, cw), lambda j, i: (i, j)),
        out_shape=jax.ShapeDtypeStruct((tp, ncb * LANE), f32), compiler_params=_params(2),
    )(proj, proj, w8)


def conv_bwd(proj, w8, dout, into, col_blk, ncb, l2, name):
    tp = proj.shape[0]
    rt = _row_tile(tp)
    hb = rt // SUB
    nr = tp // rt
    last8 = tp // SUB - 1
    cw = min(CONV_W, ncb * LANE)
    cb0 = col_blk * LANE // cw
    n = rt + SUB

    def body(x_ref, xb_ref, xf_ref, w_ref, d_ref, df_ref, into_ref, o_ref, dw_ref):
        i = pl.program_id(1)
        first = (i > 0).astype(f32)
        last = (i < nr - 1).astype(f32)
        rows = lax.broadcasted_iota(jnp.int32, (SUB, LANE), 0)
        for s in range(cw // LANE):
            sl = slice(s * LANE, (s + 1) * LANE)
            ext = jnp.concatenate([xb_ref[:, sl] * first, x_ref[:, sl], xf_ref[:, sl] * last], axis=0)
            taps = _conv_taps(ext, n)
            w = w_ref[:, sl]
            y, sg, a, nrm = _conv_act(taps, w, l2)
            da = jnp.concatenate([d_ref[:, sl], df_ref[:, sl] * last], axis=0)
            if l2:
                out = a * nrm
                da = nrm * (da - out * jnp.sum(da * out, axis=-1, keepdims=True))
            dy = da * sg * (1.0 + y * (1.0 - sg))
            part = jnp.zeros((SUB, LANE), f32)
            for k in range(DN_CONV):
                part = part + jnp.where(rows == k, jnp.sum(taps[k][0:rt] * dy[0:rt], axis=0, keepdims=True), 0.0)

            @pl.when(i == 0)
            def _(sl=sl, part=part):
                dw_ref[:, sl] = part

            @pl.when(i > 0)
            def _(sl=sl, part=part):
                dw_ref[:, sl] += part

            acc = None
            for k in range(DN_CONV):
                up = DN_CONV - 1 - k
                term = (pltpu.roll(dy, n - up, axis=0) if up else dy)[0:rt] * w[k:k + 1]
                acc = term if acc is None else acc + term
            o_ref[:, sl] = acc.astype(bf16)

    after = lambda j, i: (jnp.minimum((i + 1) * hb, last8), j)
    return pl.pallas_call(
        body, name=name, grid=(ncb * LANE // cw, nr),
        in_specs=[pl.BlockSpec((rt, cw), lambda j, i: (i, j + cb0)),
                  pl.BlockSpec((SUB, cw), lambda j, i: (jnp.maximum(i * hb - 1, 0), j + cb0)),
                  pl.BlockSpec((SUB, cw), lambda j, i: (jnp.minimum((i + 1) * hb, last8), j + cb0)),
                  pl.BlockSpec((SUB, cw), lambda j, i: (0, j)),
                  pl.BlockSpec((rt, cw), lambda j, i: (i, j)), pl.BlockSpec((SUB, cw), after),
                  pl.BlockSpec(memory_space=pl.ANY)],
        out_specs=[pl.BlockSpec((rt, cw), lambda j, i: (i, j + cb0)), pl.BlockSpec((SUB, cw), lambda j, i: (0, j))],
        out_shape=[jax.ShapeDtypeStruct(into.shape, into.dtype), jax.ShapeDtypeStruct((SUB, ncb * LANE), f32)],
        input_output_aliases={6: 0}, compiler_params=_params(2),
    )(proj, proj, proj, w8, dout, dout, into)


def _ab_common(p, al, dtb, r0):
    rows = r0 + lax.broadcasted_iota(jnp.int32, p.shape, 0)
    mask = (rows >= P0).astype(f32)
    xx = p + dtb
    sp = jnp.maximum(xx, 0.0) + _log1p_small(jnp.exp(-jnp.abs(xx)))
    ea = jnp.exp(al)
    g = -ea * sp * mask
    beta = _sigmoid(p) * mask
    return g, beta, _sigmoid(xx), ea, mask


def _chunk_tri(rt, later):
    r = lax.broadcasted_iota(jnp.int32, (rt, rt), 0)
    c = lax.broadcasted_iota(jnp.int32, (rt, rt), 1)
    shift = CHUNK.bit_length() - 1
    same = jnp.right_shift(r, shift) == jnp.right_shift(c, shift)
    return jnp.logical_and(same, c >= r if later else c <= r).astype(f32)


def ab_fwd(pab, al, dtb):
    tp = pab.shape[0]
    rt = _row_tile(tp)
    assert rt % CHUNK == 0

    def body(p_ref, al_ref, dt_ref, g_ref, b_ref):
        i = pl.program_id(0)
        g, beta, _, _, _ = _ab_common(p_ref[...], al_ref[...], dt_ref[...], i * rt)
        gam = _hdot(_chunk_tri(rt, False), g)
        for h in range(HEADS):
            g_ref[h] = jnp.broadcast_to(gam[:, h:h + 1], (rt, LANE))
            b_ref[h] = jnp.broadcast_to(beta[:, HEADS + h:HEADS + h + 1], (rt, LANE))

    vec = pl.BlockSpec((1, LANE), lambda i: (0, 0))
    out = pl.BlockSpec((HEADS, rt, LANE), lambda i: (0, i, 0))
    return pl.pallas_call(
        body, name="ab_fwd", grid=(tp // rt,), in_specs=[pl.BlockSpec((rt, LANE), lambda i: (i, 0)), vec, vec],
        out_specs=[out, out], out_shape=[jax.ShapeDtypeStruct((HEADS, tp, LANE), f32)] * 2, compiler_params=_params(1),
    )(pab, al, dtb)


def ab_bwd(pab, al, dtb, dg, db):
    tp = pab.shape[0]
    rt = _row_tile(tp)

    def body(p_ref, al_ref, dt_ref, dg_ref, db_ref, dp_ref, dal_ref, ddt_ref):
        i = pl.program_id(0)
        g, beta, sx, ea, mask = _ab_common(p_ref[...], al_ref[...], dt_ref[...], i * rt)
        lanes = lax.broadcasted_iota(jnp.int32, (rt, LANE), 1)
        dgl = jnp.zeros((rt, LANE), f32)
        dbl = jnp.zeros((rt, LANE), f32)
        for h in range(HEADS):
            dgl = dgl + jnp.where(lanes == h, dg_ref[h], 0.0)
            dbl = dbl + jnp.where(lanes == HEADS + h, db_ref[h], 0.0)
        dgl = _hdot(_chunk_tri(rt, True), dgl)
        dxx = dgl * (-ea) * sx * mask
        dp_ref[...] = (dxx + dbl * beta * (1.0 - beta)).astype(bf16)
        pal = jnp.sum(dgl * g, axis=0, keepdims=True)
        pdt = jnp.sum(dxx, axis=0, keepdims=True)

        @pl.when(i == 0)
        def _():
            dal_ref[...] = pal
            ddt_ref[...] = pdt

        @pl.when(i > 0)
        def _():
            dal_ref[...] += pal
            ddt_ref[...] += pdt

    vec = pl.BlockSpec((1, LANE), lambda i: (0, 0))
    row = pl.BlockSpec((rt, LANE), lambda i: (i, 0))
    big = pl.BlockSpec((HEADS, rt, LANE), lambda i: (0, i, 0))
    return pl.pallas_call(
        body, name="ab_bwd", grid=(tp // rt,), in_specs=[row, vec, vec, big, big], out_specs=[row, vec, vec],
        out_shape=[jax.ShapeDtypeStruct((tp, LANE), bf16), jax.ShapeDtypeStruct((1, LANE), f32),
                   jax.ShapeDtypeStruct((1, LANE), f32)],
        compiler_params=_params(1),
    )(pab, al, dtb, dg, db)


class _Chunk:
    pass


def _gdn_chunk(q, k, v, gcol, bcol, grow8):
    C = CHUNK
    R = range(len(q))
    X = _Chunk()
    ri = lax.broadcasted_iota(jnp.int32, (C, C), 0)
    ci = lax.broadcasted_iota(jnp.int32, (C, C), 1)
    eye = (ri == ci).astype(f32)
    gam = list(gcol)
    gam_row = [grow8[h][0:1, 0:C] for h in R]
    X.ri, X.ci = ri, ci
    X.Dm = [jnp.where(ri >= ci, jnp.exp(jnp.minimum(gam[h][:, 0:C] - gam_row[h], 0.0)), 0.0) for h in R]
    X.eg = [jnp.exp(gam[h]) for h in R]
    gl = [gam[h][C - 1:C, :] for h in R]
    X.egl = [jnp.exp(gl[h]) for h in R]
    X.kdec = [jnp.exp(gl[h] - gam[h]) for h in R]
    X.qs = [q[h] * (DN_DK ** -0.5) for h in R]
    X.kb = [k[h] * bcol[h] for h in R]
    kk = [_dot_nt(X.kb[h], k[h]) for h in R]
    qk = [_dot_nt(X.qs[h], k[h]) for h in R]
    X.A = [jnp.where(ri > ci, kk[h] * X.Dm[h], 0.0) for h in R]
    assert C == 64
    b16 = jnp.right_shift(ri, 4) == jnp.right_shift(ci, 4)
    b32 = jnp.right_shift(ri, 5) == jnp.right_shift(ci, 5)
    P = [jnp.where(b16, X.A[h], 0.0) for h in R]
    T = [eye - P[h] for h in R]
    for _ in range(3):
        P = [_hdot(P[h], P[h]) for h in R]
        T = [T[h] + _hdot(T[h], P[h]) for h in R]
    for off in (jnp.logical_and(b32, jnp.logical_not(b16)), jnp.logical_not(b32)):
        AT = [_hdot(jnp.where(off, X.A[h], 0.0), T[h]) for h in R]
        T = [T[h] - _hdot(T[h], AT[h]) for h in R]
    X.T = T
    X.b2 = [jnp.concatenate([bcol[h], bcol[h]], axis=-1) for h in R]
    X.u = [_hdot(T[h], v[h] * X.b2[h]) for h in R]
    X.w = [_hdot(T[h], X.kb[h] * X.eg[h]) for h in R]
    X.attn = [qk[h] * X.Dm[h] for h in R]
    X.qg = [X.qs[h] * X.eg[h] for h in R]
    X.kg = [k[h] * X.kdec[h] for h in R]
    return X


def gdn_fwd(q, k, v, gc, bc, grow):
    tp = q.shape[0]
    nc = tp // CHUNK
    hb = GDN_HEADS_PER_STEP

    def body(q_ref, k_ref, v_ref, gc_ref, bc_ref, gr_ref, o_ref, ss_ref, S_ref):
        c = pl.program_id(1)

        @pl.when(c == 0)
        def _():
            S_ref[...] = jnp.zeros_like(S_ref)

        R = range(hb)
        qc = [slice(h * DN_DK, (h + 1) * DN_DK) for h in R]
        vc = [slice(h * DN_DV, (h + 1) * DN_DV) for h in R]
        X = _gdn_chunk([q_ref[:, qc[h]] for h in R], [k_ref[:, qc[h]] for h in R], [v_ref[:, vc[h]] for h in R],
                       [gc_ref[h] for h in R], [bc_ref[h] for h in R], [gr_ref[h] for h in R])
        S = [S_ref[h] for h in R]
        for h in R:
            ss_ref[h, 0] = S[h]
        wS = [_dot(X.w[h], S[h]) for h in R]
        qS = [_dot(X.qg[h], S[h]) for h in R]
        vn = [X.u[h] - wS[h] for h in R]
        av = [_dot(X.attn[h], vn[h]) for h in R]
        kv = [_dot_tn(X.kg[h], vn[h]) for h in R]
        for h in R:
            o_ref[:, vc[h]] = qS[h] + av[h]
            S_ref[h] = S[h] * X.egl[h][:, 0:1] + kv[h]

    qk = pl.BlockSpec((CHUNK, hb * DN_DK), lambda g, c: (c, g))
    vv = pl.BlockSpec((CHUNK, hb * DN_DV), lambda g, c: (c, g))
    col = pl.BlockSpec((hb, CHUNK, LANE), lambda g, c: (g, c, 0))
    row = pl.BlockSpec((hb, SUB, LANE), lambda g, c: (g, c, 0))
    return pl.pallas_call(
        body, name="gdn_fwd", grid=(HEADS // hb, nc), in_specs=[qk, qk, vv, col, col, row],
        out_specs=[vv, pl.BlockSpec((hb, 1, DN_DK, DN_DV), lambda g, c: (g, c, 0, 0))],
        out_shape=[jax.ShapeDtypeStruct((tp, DN_V), f32), jax.ShapeDtypeStruct((HEADS, nc, DN_DK, DN_DV), f32)],
        scratch_shapes=[pltpu.VMEM((hb, DN_DK, DN_DV), f32)], compiler_params=_params(2),
    )(q, k, v, gc, bc, grow)


def gdn_bwd(q, k, v, gc, bc, grow, states, do, rider=None):
    tp = q.shape[0]
    nc = tp // CHUNK
    C = CHUNK
    hb = GDN_HEADS_PER_STEP
    grid = (HEADS // hb, nc)
    split, ride_first, ride_last = _ride(rider, 8, 5, grid)

    def body(*refs):
        ((q_ref, k_ref, v_ref, gc_ref, bc_ref, gr_ref, ss_ref, do_ref), (dq_ref, dk_ref, dv_ref, dg_ref, db_ref),
         (rin, rout, rest)) = split(refs)
        dS_ref, ride = rest[0], (rin, rout, rest[1:])
        ride_first(ride)
        c = pl.program_id(1)

        @pl.when(c == 0)
        def _():
            dS_ref[...] = jnp.zeros_like(dS_ref)

        R = range(hb)
        qc = [slice(h * DN_DK, (h + 1) * DN_DK) for h in R]
        vc = [slice(h * DN_DV, (h + 1) * DN_DV) for h in R]
        k_ = [k_ref[:, qc[h]] for h in R]
        v_ = [v_ref[:, vc[h]] for h in R]
        bcol = [bc_ref[h] for h in R]
        X = _gdn_chunk([q_ref[:, qc[h]] for h in R], k_, v_, [gc_ref[h] for h in R], bcol, [gr_ref[h] for h in R])
        ri, ci = X.ri, X.ci
        S = [ss_ref[h, 0] for h in R]
        do_ = [do_ref[:, vc[h]] for h in R]
        dSn = [dS_ref[h] for h in R]
        wS = [_dot(X.w[h], S[h]) for h in R]
        ado = [_dot_tn(X.attn[h], do_[h]) for h in R]
        kdS = [_dot(X.kg[h], dSn[h]) for h in R]
        d_qg = [_dot_nt(do_[h], S[h]) for h in R]
        qdo = [_dot_tn(X.qg[h], do_[h]) for h in R]
        vn = [X.u[h] - wS[h] for h in R]
        d_vn = [ado[h] + kdS[h] for h in R]
        dovn = [_dot_nt(do_[h], vn[h]) for h in R]
        d_kg = [_dot_nt(vn[h], dSn[h]) for h in R]
        wdv = [_dot_tn(X.w[h], d_vn[h]) for h in R]
        dw = [-_dot_nt(d_vn[h], S[h]) for h in R]
        for h in R:
            dS_ref[h] = qdo[h] + X.egl[h][:, 0:1] * dSn[h] - wdv[h]
        dattn = [jnp.where(ri >= ci, dovn[h], 0.0) for h in R]
        dRu = [_hdot_tn(X.T[h], d_vn[h]) for h in R]
        dRw = [_hdot_tn(X.T[h], dw[h]) for h in R]
        dAu = [_hdot_nt(dRu[h], X.u[h]) for h in R]
        dAw = [_hdot_nt(dRw[h], X.w[h]) for h in R]
        dA = [jnp.where(ri > ci, -(dAu[h] + dAw[h]), 0.0) for h in R]
        dKK = [dA[h] * X.Dm[h] for h in R]
        dQK = [dattn[h] * X.Dm[h] for h in R]
        E = [dA[h] * X.A[h] + dattn[h] * X.attn[h] for h in R]
        dkb = [_dot(dKK[h], k_[h]) + dRw[h] * X.eg[h] for h in R]
        dk1 = [_dot_tn(dKK[h], X.kb[h]) for h in R]
        dqs = [_dot(dQK[h], k_[h]) + d_qg[h] * X.eg[h] for h in R]
        dk2 = [_dot_tn(dQK[h], X.qs[h]) for h in R]
        ones = jnp.ones((C, LANE), f32)
        colE = [_hdot_tn(E[h], ones) for h in R]
        rows = lax.broadcasted_iota(jnp.int32, (C, LANE), 0)
        dgam = []
        for h in R:
            t = d_kg[h] * X.kg[h]
            dgl = _allsum(t) + X.egl[h][:, 0:1] * _allsum(S[h] * dSn[h])
            g = (_rowsum(E[h]) - colE[h] + _rowsum(dRw[h] * (X.kb[h] * X.eg[h])) + _rowsum(d_qg[h] * X.qg[h])
                 - _rowsum(t))
            dgam.append(g + jnp.where(rows == C - 1, dgl, 0.0))
        for h in R:
            dv_ref[:, vc[h]] = dRu[h] * X.b2[h]
            dbeta = _rowsum(dRu[h] * v_[h]) + _rowsum(dkb[h] * k_[h])
            dq_ref[:, qc[h]] = dqs[h] * (DN_DK ** -0.5)
            dk_ref[:, qc[h]] = dk1[h] + dk2[h] + dkb[h] * bcol[h] + d_kg[h] * X.kdec[h]
            dg_ref[h] = dgam[h]
            db_ref[h] = jnp.broadcast_to(dbeta, (C, LANE))
        ride_last(ride)

    rc = lambda c: nc - 1 - c
    qk = pl.BlockSpec((CHUNK, hb * DN_DK), lambda g, c: (rc(c), g))
    vv = pl.BlockSpec((CHUNK, hb * DN_DV), lambda g, c: (rc(c), g))
    col = pl.BlockSpec((hb, CHUNK, LANE), lambda g, c: (g, rc(c), 0))
    row = pl.BlockSpec((hb, SUB, LANE), lambda g, c: (g, rc(c), 0))
    st = pl.BlockSpec((hb, 1, DN_DK, DN_DV), lambda g, c: (g, rc(c), 0, 0))
    r_ins = rider.ins if rider else []
    r_outs = rider.out_shapes if rider else []
    res = pl.pallas_call(
        body, name="gdn_bwd", grid=grid, in_specs=[qk, qk, vv, col, col, row, st, vv] + [_ANY] * len(r_ins),
        out_specs=[qk, qk, vv, col, col] + [_ANY] * len(r_outs),
        out_shape=[jax.ShapeDtypeStruct((tp, DN_QK), f32), jax.ShapeDtypeStruct((tp, DN_QK), f32),
                   jax.ShapeDtypeStruct((tp, DN_V), f32), jax.ShapeDtypeStruct((HEADS, tp, LANE), f32),
                   jax.ShapeDtypeStruct((HEADS, tp, LANE), f32)] + list(r_outs),
        scratch_shapes=[pltpu.VMEM((hb, DN_DK, DN_DV), f32)] + (rider.scratch if rider else []),
        compiler_params=_params(2),
    )(q, k, v, gc, bc, grow, states, do, *r_ins)
    return res[:5], res[5:]


def _cumsum_after(x, nb, us, pieces=2):
    B, n = SB_BLOCK, x.shape[0]
    hi = x.astype(bf16)
    parts = (hi, (x - hi.astype(f32)).astype(bf16)) if pieces == 2 else (hi,)
    rows = [p[:, b * B:(b + 1) * B] for p in parts for b in range(nb)]
    r = jnp.dot(jnp.concatenate(rows, axis=0), us, preferred_element_type=f32)
    out = [r[b * n:(b + 1) * n] for b in range(nb)]
    if pieces == 2:
        out = [out[b] + r[(nb + b) * n:(nb + b + 1) * n] for b in range(nb)]
    return out[0] if nb == 1 else jnp.concatenate(out, axis=1)


def _later_blocks(x, nb, carry):
    B = SB_BLOCK
    tot = [_rowsum(x[:, b * B:(b + 1) * B]) for b in range(nb)]
    offs = [None] * nb
    run = carry
    for b in range(nb - 1, -1, -1):
        offs[b] = jnp.broadcast_to(run, (x.shape[0], B))
        run = run + tot[b]
    return (offs[0] if nb == 1 else jnp.concatenate(offs, axis=1)), run


def _sb_group(i, t):
    top = (i + 1) * (SB_QB // SB_BLOCK) - 1 - SB_GROUP * t
    jlo = jnp.maximum(top - SB_GROUP + 1, 0)
    rows = pl.ds(pl.multiple_of(jlo * SB_BLOCK, SB_BLOCK), SB_GROUP * SB_BLOCK)
    return jlo, rows, (top + 1) * SB_BLOCK


def _sb_weights(q, kcat, i, jlo, kend, cs, us, masked):
    B, nb = SB_BLOCK, SB_GROUP
    R = range(len(q))
    z = [_dot_nt(q[h], kcat[h]) * (SB_DH ** -0.5) for h in R]
    e = [jnp.exp(-jnp.abs(z[h])) for h in R]
    l1p = [jnp.log(1.0 + e[h]) for h in R]
    lsp = [jnp.minimum(z[h], 0.0) - l1p[h] for h in R]
    lk = [lsp[h] - z[h] for h in R]
    vis = None
    if masked:
        qpos = i * SB_QB + lax.broadcasted_iota(jnp.int32, (SB_QB, nb * B), 0)
        kpos = jlo * B + lax.broadcasted_iota(jnp.int32, (SB_QB, nb * B), 1)
        vis = jnp.logical_and(kpos < jnp.minimum(qpos, kend), kpos >= P0)
        lk = [jnp.where(vis, lk[h], 0.0) for h in R]
    later = [_later_blocks(lk[h], nb, cs[h]) for h in R]
    cum = [_cumsum_after(lk[h], nb, us) for h in R]
    w = [jnp.exp(lsp[h] + cum[h] + later[h][0]) for h in R]
    if masked:
        w = [jnp.where(vis, w[h], 0.0) for h in R]
    return lsp, vis, w, [later[h][1] for h in R]


def _sb_loop(i, step, carry):
    trips = ((i + 1) * (SB_QB // SB_BLOCK) - 1 + SB_GROUP) // SB_GROUP
    carry = step(True)(0, carry)
    carry = lax.fori_loop(1, trips - 1, step(False), carry)
    return lax.fori_loop(jnp.maximum(trips - 1, 1), trips, step(True), carry)


def _ride(rider, n_in, n_out, grid):
    n_rin = len(rider.ins) if rider else 0
    n_rout = len(rider.out_shapes) if rider else 0

    def split(refs):
        ins, rin = refs[:n_in], refs[n_in:n_in + n_rin]
        outs = refs[n_in + n_rin:n_in + n_rin + n_out]
        rout = refs[n_in + n_rin + n_out:n_in + n_rin + n_out + n_rout]
        return ins, outs, (rin, rout, refs[n_in + n_rin + n_out + n_rout:])

    def at(step, fn, r):
        if rider is None:
            return
        cond = None
        for a, g in enumerate(grid):
            c = pl.program_id(a) == (g - 1 if step == "last" else 0)
            cond = c if cond is None else jnp.logical_and(cond, c)

        @pl.when(cond)
        def _():
            fn(*r)

    first = lambda r: at("first", rider.start if rider else None, r)
    last = lambda r: at("last", rider.finish if rider else None, r)
    return split, first, last


def sb_fwd(qs, ks, vs, rider=None):
    tp = qs.shape[0]
    nq = tp // SB_QB
    B, G, hb, QB = SB_BLOCK, SB_GROUP, SB_FWD_HEADS_PER_STEP, SB_QB
    assert tp >= G * B and tp % QB == 0 and QB % B == 0 and G * B >= QB
    grid = (HEADS // hb, nq)
    split, ride_first, ride_last = _ride(rider, 3, 2, grid)

    def body(*refs):
        (q_ref, k_ref, v_ref), (o_ref, ob_ref), ride = split(refs)
        ride_first(ride)
        i = pl.program_id(1)
        R = range(hb)
        hs = [slice(h * SB_DH, (h + 1) * SB_DH) for h in R]
        q = [q_ref[:, hs[h]] for h in R]
        us = (lax.broadcasted_iota(jnp.int32, (B, B), 0) > lax.broadcasted_iota(jnp.int32, (B, B), 1)).astype(bf16)

        def make_step(masked):
            def step(t, carry):
                acc, cs = carry
                jlo, rows, kend = _sb_group(i, t)
                _, _, w, cs = _sb_weights(q, [k_ref[rows, hs[h]] for h in R], i, jlo, kend, cs, us, masked)
                pv = [_dot(w[h], v_ref[rows, hs[h]]) for h in R]
                return tuple(acc[h] + pv[h] for h in R), tuple(cs)
            return step

        carry = (tuple(jnp.zeros((QB, SB_DH), f32) for _ in R), tuple(jnp.zeros((QB, 1), f32) for _ in R))
        acc, _ = _sb_loop(i, make_step, carry)
        for h in R:
            o_ref[:, hs[h]] = acc[h]
            ob_ref[:, hs[h]] = acc[h].astype(bf16)
        ride_last(ride)

    blk = pl.BlockSpec((QB, hb * SB_DH), lambda g, i: (i, g))
    full = pl.BlockSpec((tp, hb * SB_DH), lambda g, i: (0, g))
    r_ins = rider.ins if rider else []
    r_outs = rider.out_shapes if rider else []
    res = pl.pallas_call(
        body, name="sb_fwd", grid=grid, in_specs=[blk, full, full] + [_ANY] * len(r_ins),
        out_specs=[blk, blk] + [_ANY] * len(r_outs),
        out_shape=[jax.ShapeDtypeStruct((tp, SB_W), f32), jax.ShapeDtypeStruct((tp, SB_W), bf16)] + list(r_outs),
        scratch_shapes=rider.scratch if rider else [], compiler_params=_params(2),
    )(qs, ks, vs, *r_ins)
    return res[0], res[1], res[2:]


def sb_bwd(qs, ks, vs, o, do, rider=None):
    tp = qs.shape[0]
    nq = tp // SB_QB
    B, G, hb, QB = SB_BLOCK, SB_GROUP, SB_HEADS_PER_STEP, SB_QB
    assert tp >= G * B and tp % QB == 0 and QB % B == 0 and G * B >= QB
    grid = (HEADS // hb, nq)
    split, ride_first, ride_last = _ride(rider, 5, 3, grid)

    def body(*refs):
        (q_ref, k_ref, v_ref, o_ref, do_ref), (dq_ref, dk_ref, dv_ref), ride = split(refs)
        ride_first(ride)
        i = pl.program_id(1)

        @pl.when(i == 0)
        def _():
            dk_ref[...] = jnp.zeros_like(dk_ref)
            dv_ref[...] = jnp.zeros_like(dv_ref)

        R = range(hb)
        hs = [slice(h * SB_DH, (h + 1) * SB_DH) for h in R]
        q = [q_ref[:, hs[h]] for h in R]
        dob = [do_ref[:, hs[h]].astype(bf16) for h in R]
        et = [_rowsum(dob[h].astype(f32) * o_ref[:, hs[h]]) for h in R]
        us = (lax.broadcasted_iota(jnp.int32, (B, B), 0) > lax.broadcasted_iota(jnp.int32, (B, B), 1)).astype(bf16)

        def make_step(masked):
            def step(t, carry):
                dq, cs, ce = carry
                jlo, rows, kend = _sb_group(i, t)
                kcat = [k_ref[rows, hs[h]] for h in R]
                dwv = [_dot_nt(dob[h], v_ref[rows, hs[h]]) for h in R]
                lsp, vis, w, cs = _sb_weights(q, kcat, i, jlo, kend, cs, us, masked)
                wb = [w[h].astype(bf16) for h in R]
                ee = [dwv[h] * wb[h].astype(f32) for h in R]
                later = [_later_blocks(ee[h], G, ce[h]) for h in R]
                cum = [_cumsum_after(ee[h], G, us) for h in R]
                dz = []
                for h in R:
                    d = ee[h] - jnp.exp(lsp[h]) * (et[h] - (cum[h] + later[h][0]))
                    if masked:
                        d = jnp.where(vis, d, 0.0)
                    dz.append(d.astype(bf16))
                dkj = [_dot_tn(dz[h], q[h]) for h in R]
                dvj = [_dot_tn(wb[h], dob[h]) for h in R]
                dqj = [_dot(dz[h], kcat[h]) for h in R]
                for h in R:
                    dk_ref[rows, hs[h]] += dkj[h]
                    dv_ref[rows, hs[h]] += dvj[h]
                return tuple(dq[h] + dqj[h] for h in R), tuple(cs), tuple(later[h][1] for h in R)
            return step

        z0 = tuple(jnp.zeros((QB, 1), f32) for _ in R)
        dq, _, _ = _sb_loop(i, make_step, (tuple(jnp.zeros((QB, SB_DH), f32) for _ in R), z0, z0))
        for h in R:
            dq_ref[:, hs[h]] = dq[h] * (SB_DH ** -0.5)
        ride_last(ride)

    blk = pl.BlockSpec((QB, hb * SB_DH), lambda g, i: (i, g))
    full = pl.BlockSpec((tp, hb * SB_DH), lambda g, i: (0, g))
    r_ins = rider.ins if rider else []
    r_outs = rider.out_shapes if rider else []
    res = pl.pallas_call(
        body, name="sb_bwd", grid=grid, in_specs=[blk, full, full, blk, blk] + [_ANY] * len(r_ins),
        out_specs=[blk, full, full] + [_ANY] * len(r_outs),
        out_shape=[jax.ShapeDtypeStruct((tp, SB_W), f32)] * 3 + list(r_outs),
        scratch_shapes=rider.scratch if rider else [], compiler_params=_params(2),
    )(qs, ks, vs, o, do, *r_ins)
    return res[:3], res[3:]


def adamw(w, g, m, v, name, rider=None):
    r, c = w.shape
    rt = _tile(r, 128, SUB) if r % SUB == 0 else r
    blk = pl.BlockSpec((rt, c), lambda i: (i, 0))
    c1 = 1.0 - ADAM_B1 ** ADAM_STEP
    c2 = 1.0 - ADAM_B2 ** ADAM_STEP
    grid = (r // rt,)
    split, ride_first, ride_last = _ride(rider, 4, 3, grid)

    def body(*refs):
        (w_ref, g_ref, m_ref, v_ref), (d_ref, mo_ref, vo_ref), ride = split(refs)
        ride_first(ride)
        g_ = g_ref[...]
        m_ = ADAM_B1 * m_ref[...] + (1.0 - ADAM_B1) * g_
        v_ = ADAM_B2 * v_ref[...] + (1.0 - ADAM_B2) * (g_ * g_)
        mo_ref[...] = m_
        vo_ref[...] = v_
        d_ref[...] = -ADAM_LR * ((m_ / c1) / (jnp.sqrt(v_ / c2) + ADAM_EPS) + ADAM_WD * w_ref[...])
        ride_last(ride)

    r_ins = rider.ins if rider else []
    r_outs = rider.out_shapes if rider else []
    res = pl.pallas_call(
        body, name=name, grid=grid, in_specs=[blk] * 4 + [_ANY] * len(r_ins), out_specs=[blk] * 3 + [_ANY] * len(r_outs),
        out_shape=[jax.ShapeDtypeStruct((r, c), f32)] * 3 + list(r_outs),
        scratch_shapes=rider.scratch if rider else [], compiler_params=_params(1),
    )(w, g, m, v, *r_ins)
    return res[:3], res[3:]


def sum_slots(x, name):
    n, r, c = x.shape
    rt = _tile(r, 128, SUB) if r % SUB == 0 else r
    blk = pl.BlockSpec((n, rt, c), lambda i: (0, i, 0))

    def body(x_ref, o_ref):
        acc = x_ref[0].astype(f32)
        for s in range(1, n):
            acc = acc + x_ref[s].astype(f32)
        o_ref[...] = acc

    return pl.pallas_call(
        body, name=name, grid=(r // rt,), in_specs=[blk], out_specs=pl.BlockSpec((rt, c), lambda i: (i, 0)),
        out_shape=jax.ShapeDtypeStruct((r, c), f32), compiler_params=_params(1),
    )(x)


def add2(a, b, name, out_dtype=f32):
    n, r, c = a.shape
    rt = _tile(r, 64, SUB) if r % SUB == 0 else r
    blk = pl.BlockSpec((n, rt, c), lambda i: (0, i, 0))

    def body(a_ref, b_ref, o_ref):
        o_ref[...] = (a_ref[...] + b_ref[...]).astype(out_dtype)

    return pl.pallas_call(
        body, name=name, grid=(r // rt,), in_specs=[blk, blk], out_specs=blk,
        out_shape=jax.ShapeDtypeStruct((n, r, c), out_dtype), compiler_params=_params(1),
    )(a, b)


_ANY = pl.BlockSpec(memory_space=pl.ANY)
_MESH = pl.DeviceIdType.MESH


def _coords():
    return lax.axis_index("x"), lax.axis_index("y"), lax.axis_index("c")


def _chip_peer(x, y, r):
    return x ^ (r >> 1), y ^ (r & 1)


class _Exchange:
    def __init__(self, ins, out_shapes, scratch, start, finish):
        self.ins, self.out_shapes, self.scratch, self.start, self.finish = ins, out_shapes, scratch, start, finish

    def split(self, refs):
        n, m = len(self.ins), len(self.out_shapes)
        return refs[:n], refs[n:n + m], refs[n + m:]


def run_exchange(ex, name):
    def body(*refs):
        ins, outs, sems = ex.split(refs)
        ex.start(ins, outs, sems)
        ex.finish(ins, outs, sems)

    return pl.pallas_call(body, name=name, in_specs=[_ANY] * len(ex.ins), out_specs=[_ANY] * len(ex.out_shapes),
                          out_shape=ex.out_shapes, scratch_shapes=ex.scratch)(*ex.ins)


def gather_chips(big, small):
    nb, n = len(big), len(big) + len(small)
    shards = list(big) + list(small)
    kb = nb * (N_CHIPS - 1)
    k = n * (N_CHIPS - 1)

    def copies(src, dst, sems):
        send, recv, fsend, frecv = sems
        x, y, c = _coords()
        sib = (x, y, 1 - c)
        peers = [_chip_peer(x, y, r) for r in range(1, N_CHIPS)]

        def direct(t, j, slot):
            s = t * (N_CHIPS - 1) + j
            if t < nb:
                return pltpu.make_async_remote_copy(src[t].at[c], dst[t].at[slot, c], send.at[s], recv.at[s],
                                                    device_id=(*peers[j], c), device_id_type=_MESH)
            return pltpu.make_async_remote_copy(src[t], dst[t].at[slot], send.at[s], recv.at[s],
                                                device_id=(*peers[j], c), device_id_type=_MESH)

        def passed(t, j, half):
            s = t * (N_CHIPS - 1) + j
            px, py = peers[j]
            part = dst[t].at[2 * px + py, half]
            return pltpu.make_async_remote_copy(part, part, fsend.at[s], frecv.at[s], device_id=sib, device_id_type=_MESH)

        return direct, passed, peers, 2 * x + y, c

    def start(src, dst, sems):
        direct, _, _, me, _ = copies(src, dst, sems)
        for t in range(n):
            for j in range(N_CHIPS - 1):
                direct(t, j, me).start()

    def finish(src, dst, sems):
        direct, passed, peers, me, c = copies(src, dst, sems)
        fwd = []
        for t in range(nb):
            for j in range(N_CHIPS - 1):
                px, py = peers[j]
                direct(t, j, 2 * px + py).wait_recv()
                fwd.append(passed(t, j, c))
                fwd[-1].start()
        for t in range(nb, n):
            for j in range(N_CHIPS - 1):
                px, py = peers[j]
                direct(t, j, 2 * px + py).wait_recv()
        for t in range(nb):
            for j in range(N_CHIPS - 1):
                passed(t, j, 1 - c).wait_recv()
        for t in range(n):
            for j in range(N_CHIPS - 1):
                direct(t, j, me).wait_send()
        for cp in fwd:
            cp.wait_send()

    return _Exchange(shards, [jax.ShapeDtypeStruct((N_CHIPS,) + s.shape, s.dtype) for s in shards],
                     [pltpu.SemaphoreType.DMA((k,)), pltpu.SemaphoreType.DMA((k,)),
                      pltpu.SemaphoreType.DMA((max(kb, 1),)), pltpu.SemaphoreType.DMA((max(kb, 1),))], start, finish)


def sibling_swap(grads):
    pairs = [(t, o) for t, g in enumerate(grads) for o in range(g.shape[0])]
    k = len(pairs)

    def copies(src, dst, sems):
        send, recv = sems
        x, y, c = _coords()
        return [pltpu.make_async_remote_copy(src[t].at[o, 1 - c], dst[t].at[o], send.at[s], recv.at[s],
                                             device_id=(x, y, 1 - c), device_id_type=_MESH)
                for s, (t, o) in enumerate(pairs)]

    def start(src, dst, sems):
        for cp in copies(src, dst, sems):
            cp.start()

    def finish(src, dst, sems):
        cps = copies(src, dst, sems)
        for cp in cps:
            cp.wait_recv()
        for cp in cps:
            cp.wait_send()

    return _Exchange(list(grads), [jax.ShapeDtypeStruct((g.shape[0],) + g.shape[2:], g.dtype) for g in grads],
                     [pltpu.SemaphoreType.DMA((k,)), pltpu.SemaphoreType.DMA((k,))], start, finish)


def scatter_chips(parts):
    n = len(parts)
    k = n * (N_CHIPS - 1)

    def copy(src, dst, sems, t, r, landing):
        send, recv = sems
        x, y, c = _coords()
        me = 2 * x + y
        px, py = _chip_peer(x, y, r)
        peer = 2 * px + py
        s = t * (N_CHIPS - 1) + r - 1
        return pltpu.make_async_remote_copy(src[t].at[me if landing else peer], dst[t].at[peer if landing else me],
                                            send.at[s], recv.at[s], device_id=(px, py, c), device_id_type=_MESH)

    def start(src, dst, sems):
        for t in range(n):
            for r in range(1, N_CHIPS):
                copy(src, dst, sems, t, r, False).start()

    def finish(src, dst, sems):
        for t in range(n):
            for r in range(1, N_CHIPS):
                copy(src, dst, sems, t, r, True).wait_recv()
        for t in range(n):
            for r in range(1, N_CHIPS):
                copy(src, dst, sems, t, r, False).wait_send()

    return _Exchange(list(parts), [jax.ShapeDtypeStruct(p.shape, p.dtype) for p in parts],
                     [pltpu.SemaphoreType.DMA((k,)), pltpu.SemaphoreType.DMA((k,))], start, finish)


def sibling_send(halves, name):
    n = len(halves)

    def body(*refs):
        src, dst = refs[:n], refs[n:2 * n]
        send, recv = refs[2 * n:]
        x, y, c = _coords()
        cps = [pltpu.make_async_remote_copy(src[t], dst[t], send.at[t], recv.at[t],
                                            device_id=(x, y, 1 - c), device_id_type=_MESH) for t in range(n)]
        for cp in cps:
            cp.start()
        for cp in cps:
            cp.wait_recv()
        for cp in cps:
            cp.wait_send()

    return pl.pallas_call(
        body, name=name, in_specs=[_ANY] * n, out_specs=[_ANY] * n,
        out_shape=[jax.ShapeDtypeStruct(h.shape, h.dtype) for h in halves],
        scratch_shapes=[pltpu.SemaphoreType.DMA((n,)), pltpu.SemaphoreType.DMA((n,))],
    )(*halves)


def gather_all(block):
    def copies(src, dst, sems, landing):
        send, recv, loc = sems
        x, y, c = _coords()
        me = 4 * x + 2 * y + c
        mine = pltpu.make_async_copy(src[0], dst[0].at[me], loc)
        remote = []
        for r in range(1, N_DEV):
            px, py, pc = x ^ (r >> 2), y ^ ((r >> 1) & 1), c ^ (r & 1)
            slot = 4 * px + 2 * py + pc if landing else me
            remote.append(pltpu.make_async_remote_copy(src[0], dst[0].at[slot], send.at[r - 1], recv.at[r - 1],
                                                       device_id=(px, py, pc), device_id_type=_MESH))
        return mine, remote

    def start(src, dst, sems):
        mine, outs = copies(src, dst, sems, False)
        mine.start()
        for cp in outs:
            cp.start()

    def finish(src, dst, sems):
        mine, lands = copies(src, dst, sems, True)
        for cp in lands:
            cp.wait_recv()
        for cp in lands:
            cp.wait_send()
        mine.wait()

    return _Exchange([block], [jax.ShapeDtypeStruct((N_DEV,) + block.shape, block.dtype)],
                     [pltpu.SemaphoreType.DMA((N_DEV - 1,)), pltpu.SemaphoreType.DMA((N_DEV - 1,)),
                      pltpu.SemaphoreType.DMA(())], start, finish)


def _pad_lanes(v, n=LANE):
    return jnp.pad(v, ((0, 0), (0, n - v.shape[1])))


def _w_in_pieces():
    cs = (PROJ_BIG + 2 * HEADS) // N_CHIPS
    ab_end = AB_COL + 2 * HEADS
    out = []
    for o in range(N_CHIPS):
        lo, hi = o * cs, (o + 1) * cs
        cand = [("big", lo, min(hi, AB_COL), 0), ("ab", max(lo, AB_COL), min(hi, ab_end), AB_COL),
                ("big", max(lo, ab_end), hi, 2 * HEADS)]
        out.append([(s, a - off, b - off) for s, a, b, off in cand if a < b])
    return out


def _split_w_in(w4):
    big, ab = [], []
    for o, pieces in enumerate(_w_in_pieces()):
        at = 0
        for s, a, b in pieces:
            (big if s == "big" else ab).append(w4[o][:, at:at + b - a])
            at += b - a
    return jnp.concatenate(big, axis=1), _pad_lanes(jnp.concatenate(ab, axis=1))


def _join_w_in(big, ab):
    src = {"big": big, "ab": ab}
    return jnp.stack([jnp.concatenate([src[s][:, a:b] for s, a, b in pieces], axis=1) for pieces in _w_in_pieces()])


def _conv_w8(w):
    return jnp.pad(w, ((0, SUB - DN_CONV), (0, 0)))


def _row_layout(gc, tp):
    nc = tp // CHUNK
    g = gc[:, :, 0].reshape(HEADS, nc, 1, CHUNK)
    g = jnp.broadcast_to(g, (HEADS, nc, SUB, CHUNK))
    return jnp.pad(g, ((0, 0), (0, 0), (0, 0), (0, LANE - CHUNK))).reshape(HEADS, nc * SUB, LANE)


def _step(x, meta, W, target, late_weights=None, early_swap=None, early_grads=None, last_grads=None):
    W = dict(W)
    seq = x.shape[0]
    tp = P0 + N_META + seq
    h0 = jnp.concatenate([jnp.zeros((P0, D_MODEL), f32), meta, x], axis=0)
    w_big, w_ab = _split_w_in(W["w_in"])
    cq8, ck8, cv8 = _conv_w8(W["conv_q"]), _conv_w8(W["conv_k"]), _conv_w8(W["conv_v"])
    al, dtb = _pad_lanes(W["dn_a_log"]), _pad_lanes(W["dn_dt_bias"])

    n1 = rms_fwd(h0, W["norm_mix_gain"], "rms1_fwd")
    proj = matmul(n1, w_big, "nn", "proj_fwd")
    pab = matmul(n1, w_ab, "nn", "pab_fwd")
    qn = conv_fwd(proj, cq8, C_DQ * 8, 8, True, "conv_q_fwd")
    kn = conv_fwd(proj, ck8, C_DK * 8, 8, True, "conv_k_fwd")
    va = conv_fwd(proj, cv8, C_DV * 8, 16, False, "conv_v_fwd")
    gc, bc = ab_fwd(pab, al, dtb)
    grow = _row_layout(gc, tp)
    o_dn, states = gdn_fwd(qn, kn, va, gc, bc, grow)
    on = dn_out_fwd(o_dn, proj, W["dn_out_norm_gain"])
    qs, ks, vs = sb_prep_fwd(proj, W["sb_q_norm_gain"], W["sb_k_norm_gain"])
    o_sb, o_sb16, arrived = sb_fwd(qs, ks, vs, rider=late_weights[0] if late_weights else None)
    if late_weights:
        W.update(late_weights[1](arrived))
    ydn = matmul(on, W["w_branch_dn"], "nn", "ydn_fwd")
    ysb = matmul(o_sb16, W["w_branch_sb"], "nn", "ysb_fwd")
    merged = merge_fwd(proj, ydn, ysb)
    h1 = matmul(merged, W["w_out"], "nn", "wout_fwd", residual=h0)
    n2 = rms_fwd(h1, W["norm_ffn_gain"], "rms2_fwd")
    u = matmul(n2, W["w_ffn_in"], "nn", "ffn_in_fwd", tn_t=512)
    act = swiglu_fwd(u)
    y = matmul(act, W["w_ffn_out"], "nn", "ffn_out_fwd", residual=h1)
    dy, dy16, loss = loss_head(y, target)

    G = {}
    dact = matmul(dy16, W["w_ffn_out"], "nt", "ffn_out_dx", tn_t=1408)
    G["w_ffn_out"] = matmul(act, dy16, "tn", "ffn_out_dw", tm_t=1408)
    dgate, dup = swiglu_bwd(u, dact)
    du = jnp.concatenate([dgate, dup], axis=1)
    dn2 = matmul(du, W["w_ffn_in"], "nt", "ffn_in_dx", tk_t=512)
    G["w_ffn_in"] = matmul(n2, du, "tn", "ffn_in_dw", tn_t=512)
    dh1, dh1_16, G["norm_ffn_gain"] = rms_bwd(h1, W["norm_ffn_gain"], dn2, dy, "rms2_bwd")
    dmerged = matmul(dh1_16, W["w_out"], "nt", "wout_dx")
    G["w_out"] = matmul(merged, dh1_16, "tn", "wout_dw")
    dyd, dys, d_gates = merge_bwd(proj, ydn, ysb, dmerged)
    don = matmul(dyd, W["w_branch_dn"], "nt", "ydn_dx")
    G["w_branch_dn"] = matmul(on, dyd, "tn", "ydn_dw")
    do_sb = matmul(dys, W["w_branch_sb"], "nt", "ysb_dx")
    G["w_branch_sb"] = matmul(o_sb16, dys, "tn", "ysb_dw")
    do_dn, dproj, G["dn_out_norm_gain"] = dn_out_bwd(o_dn, proj, W["dn_out_norm_gain"], don)
    (dqn, dkn, dva, dgc, dbc), swapped = gdn_bwd(qn, kn, va, gc, bc, grow, states, do_dn,
                                                 rider=early_swap[0](G) if early_swap else None)
    if early_swap:
        early_swap[1](swapped)
    dpab, dal, ddt = ab_bwd(pab, al, dtb, dgc, dbc)
    G["dn_a_log"], G["dn_dt_bias"] = dal[:, :HEADS], ddt[:, :HEADS]
    dproj, dcq = conv_bwd(proj, cq8, dqn, dproj, C_DQ * 8, 8, True, "conv_q_bwd")
    dproj, dck = conv_bwd(proj, ck8, dkn, dproj, C_DK * 8, 8, True, "conv_k_bwd")
    dproj, dcv = conv_bwd(proj, cv8, dva, dproj, C_DV * 8, 16, False, "conv_v_bwd")
    G["conv_q"], G["conv_k"], G["conv_v"] = dcq[:DN_CONV], dck[:DN_CONV], dcv[:DN_CONV]
    (dqs, dks, dvs), delivered = sb_bwd(qs, ks, vs, o_sb, do_sb, rider=early_grads[0](G) if early_grads else None)
    if early_grads:
        early_grads[1](delivered)
    dproj, G["sb_q_norm_gain"], G["sb_k_norm_gain"] = sb_prep_bwd(
        proj, W["sb_q_norm_gain"], W["sb_k_norm_gain"], dqs, dks, dvs, dproj)
    dproj = lax.dynamic_update_slice(dproj, d_gates, (0, C_GDN * 1024))
    dw_big = matmul(n1, dproj, "tn", "proj_dw")
    dw_ab = matmul(n1, dpab, "tn", "pab_dw")
    G["w_in"] = (dw_big, dw_ab)
    if last_grads:
        dn1, delivered = matmul(dproj, w_big, "nt", "proj_dx", tk_t=1024, rider=last_grads[0](G))
        last_grads[1](delivered)
    else:
        dn1 = matmul(dproj, w_big, "nt", "proj_dx", tk_t=1024)
    dn1 = matmul(dpab, w_ab, "nt", "pab_dx", residual=dn1)
    dh0, _, G["norm_mix_gain"] = rms_bwd(h0, W["norm_mix_gain"], dn1, dh1, "rms1_bwd")
    G["meta_tokens"] = dh0[P0:P0 + N_META]
    return loss, dh0[P0 + N_META:], G


_BIG = ("w_in", "w_branch_dn", "w_branch_sb", "w_out", "w_ffn_in", "w_ffn_out")
_COL_SHARDED = ("w_in", "w_ffn_in", "meta_tokens", "conv_q", "conv_k", "conv_v")
_SMALL_REPL = ("norm_mix_gain", "norm_ffn_gain", "dn_a_log", "dn_dt_bias", "dn_out_norm_gain", "sb_q_norm_gain",
               "sb_k_norm_gain")
_SMALL_SHARD = ("meta_tokens", "conv_q", "conv_k", "conv_v")
_ORDER = ("meta_tokens", "norm_mix_gain", "w_in", "conv_q", "conv_k", "conv_v", "dn_a_log", "dn_dt_bias",
          "dn_out_norm_gain", "sb_q_norm_gain", "sb_k_norm_gain", "w_branch_dn", "w_branch_sb", "w_out",
          "norm_ffn_gain", "w_ffn_in", "w_ffn_out")


def _unshard(g4, name):
    if name in _COL_SHARDED:
        r, cs = g4.shape[1:]
        return jnp.transpose(g4, (1, 0, 2)).reshape(r, N_CHIPS * cs)
    return g4.reshape((-1,) + g4.shape[2:])


def _to_shards(full, name):
    if name in _COL_SHARDED:
        r, c = full.shape
        return jnp.transpose(full.reshape(r, N_CHIPS, c // N_CHIPS), (1, 0, 2))
    r, c = full.shape
    return full.reshape(N_CHIPS, r // N_CHIPS, c)


def _rows_1024(a):
    r, c = a.shape
    if c >= 1024:
        return a.reshape(r * (c // 1024), 1024)
    return jnp.pad(a, ((0, 0), (0, 1024 - c)))


def kernel(x, meta_tokens, norm_mix_gain, w_in, conv_q, conv_k, conv_v, dn_a_log, dn_dt_bias, dn_out_norm_gain, sb_q_norm_gain, sb_k_norm_gain, w_branch_dn, w_branch_sb, w_out, norm_ffn_gain, w_ffn_in, w_ffn_out, loss_target, m_meta_tokens, m_norm_mix_gain, m_w_in, m_conv_q, m_conv_k, m_conv_v, m_dn_a_log, m_dn_dt_bias, m_dn_out_norm_gain, m_sb_q_norm_gain, m_sb_k_norm_gain, m_w_branch_dn, m_w_branch_sb, m_w_out, m_norm_ffn_gain, m_w_ffn_in, m_w_ffn_out, v_meta_tokens, v_norm_mix_gain, v_w_in, v_conv_q, v_conv_k, v_conv_v, v_dn_a_log, v_dn_dt_bias, v_dn_out_norm_gain, v_sb_q_norm_gain, v_sb_k_norm_gain, v_w_branch_dn, v_w_branch_sb, v_w_out, v_norm_ffn_gain, v_w_ffn_in, v_w_ffn_out):
    Wl = dict(meta_tokens=meta_tokens, norm_mix_gain=norm_mix_gain, w_in=w_in[0], conv_q=conv_q[0], conv_k=conv_k[0],
              conv_v=conv_v[0], dn_a_log=dn_a_log, dn_dt_bias=dn_dt_bias, dn_out_norm_gain=dn_out_norm_gain,
              sb_q_norm_gain=sb_q_norm_gain, sb_k_norm_gain=sb_k_norm_gain, w_branch_dn=w_branch_dn[0],
              w_branch_sb=w_branch_sb[0], w_out=w_out[0], norm_ffn_gain=norm_ffn_gain, w_ffn_in=w_ffn_in[0],
              w_ffn_out=w_ffn_out[0])
    Ml = dict(meta_tokens=m_meta_tokens, norm_mix_gain=m_norm_mix_gain, w_in=m_w_in[0], conv_q=m_conv_q[0],
              conv_k=m_conv_k[0], conv_v=m_conv_v[0], dn_a_log=m_dn_a_log, dn_dt_bias=m_dn_dt_bias,
              dn_out_norm_gain=m_dn_out_norm_gain, sb_q_norm_gain=m_sb_q_norm_gain, sb_k_norm_gain=m_sb_k_norm_gain,
              w_branch_dn=m_w_branch_dn[0], w_branch_sb=m_w_branch_sb[0], w_out=m_w_out[0],
              norm_ffn_gain=m_norm_ffn_gain, w_ffn_in=m_w_ffn_in[0], w_ffn_out=m_w_ffn_out[0])
    Vl = dict(meta_tokens=v_meta_tokens, norm_mix_gain=v_norm_mix_gain, w_in=v_w_in[0], conv_q=v_conv_q[0],
              conv_k=v_conv_k[0], conv_v=v_conv_v[0], dn_a_log=v_dn_a_log, dn_dt_bias=v_dn_dt_bias,
              dn_out_norm_gain=v_dn_out_norm_gain, sb_q_norm_gain=v_sb_q_norm_gain, sb_k_norm_gain=v_sb_k_norm_gain,
              w_branch_dn=v_w_branch_dn[0], w_branch_sb=v_w_branch_sb[0], w_out=v_w_out[0],
              norm_ffn_gain=v_norm_ffn_gain, w_ffn_in=v_w_ffn_in[0], w_ffn_out=v_w_ffn_out[0])
    lead = {n: (1,) if (n in _BIG or n in ("conv_q", "conv_k", "conv_v")) else () for n in _ORDER}

    chip = 2 * lax.axis_index("x") + lax.axis_index("y")
    c = lax.axis_index("c")
    halved = {n: Wl[n].astype(bf16).reshape(2, Wl[n].shape[0] // 2, Wl[n].shape[1]) for n in _BIG}

    def gathered_weights(names, owns, outs):
        res = {}
        for n, own, g4 in zip(names, owns, outs):
            g4 = lax.dynamic_update_slice(g4, own[None], (chip,) + (0,) * own.ndim)
            if n in _BIG:
                g4 = g4.reshape(N_CHIPS, 2 * g4.shape[2], g4.shape[3])
            res[n] = g4 if n == "w_in" else _unshard(g4, n)
        return res

    first = ["w_in"] + list(_SMALL_SHARD)
    first_own = [halved["w_in"]] + [Wl[n] for n in _SMALL_SHARD]
    W = dict(Wl)
    W.update(gathered_weights(first, first_own, run_exchange(gather_chips(first_own[:1], first_own[1:]), "gather_w_in")))
    late = [n for n in _BIG if n != "w_in"]
    late_own = [halved[n] for n in late]
    for n in late:
        del W[n]

    def halves_of(names, G):
        g4 = [_to_shards(G[n], n) for n in names]
        return [g.reshape(N_CHIPS, 2, g.shape[1] // 2, g.shape[2]) for g in g4]

    def pair_added(g42, from_sib, tag, wire):
        mine = [lax.dynamic_index_in_dim(g, c, axis=1, keepdims=False) for g in g42]
        return [add2(a, b, "grad_pair_add_%s%d" % (tag, t), out_dtype=wire)
                for t, (a, b) in enumerate(zip(mine, from_sib))]

    def chip_reduced(parts, slots, tag):
        slots = [lax.dynamic_update_slice(s, lax.dynamic_index_in_dim(p, chip, axis=0, keepdims=True), (chip, 0, 0))
                 for s, p in zip(slots, parts)]
        return [sum_slots(s, "grad_chip_sum_%s%d" % (tag, t)) for t, s in enumerate(slots)]

    early, last = {}, {}

    def early_swap_begin(G):
        early["g42"] = halves_of(late, G)
        return sibling_swap(early["g42"])

    def early_begin(G):
        early["parts"] = pair_added(early["g42"], early["from_sib"], "a", f32)
        return scatter_chips(early["parts"])

    def last_begin(G):
        g2 = [g.reshape(1, 2, g.shape[0] // 2, g.shape[1]) for g in G["w_in"]]
        added = pair_added(g2, run_exchange(sibling_swap(g2), "grad_sibling_swap_b"), "b", bf16)
        last["parts"] = [_join_w_in(added[0][0], added[1][0])]
        return scatter_chips(last["parts"])

    loss, grad_x, G = _step(
        x[0], W["meta_tokens"], W, loss_target[0],
        late_weights=(gather_chips(late_own, []), lambda outs: gathered_weights(late, late_own, outs)),
        early_swap=(early_swap_begin, lambda outs: early.update(from_sib=outs)),
        early_grads=(early_begin, lambda slots: early.update(halves=chip_reduced(early["parts"], slots, "a"))),
        last_grads=(last_begin, lambda slots: last.update(halves=chip_reduced(last["parts"], slots, "b"))))
    halves = last["halves"] + early["halves"]
    theirs = sibling_send(halves, "grad_sibling_send")
    Gs = {}
    for n, h, o in zip(["w_in"] + late, halves, theirs):
        Gs[n] = lax.dynamic_update_slice(jnp.concatenate([o, o], axis=0), h, (c * h.shape[0], 0))

    small_names = list(_SMALL_REPL) + list(_SMALL_SHARD)
    pieces = [_rows_1024(G[n]) for n in small_names] + [_rows_1024(loss)]
    counts = [p.shape[0] for p in pieces]
    pack = jnp.concatenate(pieces, axis=0)
    pad_rows = (-pack.shape[0]) % SUB
    pack = jnp.pad(pack, ((0, pad_rows), (0, 0)))
    adam = {"w_in": adamw(Wl["w_in"], Gs["w_in"], Ml["w_in"], Vl["w_in"], "adamw_w_in", rider=gather_all(pack))}
    total = sum_slots(adam["w_in"][1][0], "small_sum")
    row = 0
    for n, cnt in zip(small_names, counts[:-1]):
        blk = total[row:row + cnt]
        row += cnt
        full_shape = G[n].shape
        if full_shape[1] >= 1024:
            blk = blk.reshape(full_shape)
        else:
            blk = blk[:, :full_shape[1]]
        if n in _SMALL_SHARD:
            cs = full_shape[1] // N_CHIPS
            blk = lax.dynamic_slice_in_dim(blk, chip * cs, cs, axis=1)
        Gs[n] = blk
    loss_out = total[row, 0]

    grads, deltas, new_m, new_v = [], [], [], []
    for n in _ORDER:
        d, m2, v2 = (adam[n] if n in adam else adamw(Wl[n], Gs[n], Ml[n], Vl[n], "adamw_" + n))[0]
        shape = lead[n] + Wl[n].shape
        grads.append(Gs[n].reshape(shape))
        deltas.append(d.reshape(shape))
        new_m.append(m2.reshape(shape))
        new_v.append(v2.reshape(shape))
    return (loss_out, grad_x[None], *grads, *deltas, *new_m, *new_v)
```

```python
import jax
import jax.numpy as jnp
from jax import lax
from jax.experimental import pallas as pl
from jax.experimental.pallas import tpu as pltpu

f32 = jnp.float32
bf16 = jnp.bfloat16

D_MODEL = 1024
N_META = 16
CHUNK = 64
HEADS = 8
DN_DK = 128
DN_DV = 256
DN_CONV = 4
DN_QK = HEADS * DN_DK
DN_V = HEADS * DN_DV
SB_DH = 128
SB_W = HEADS * SB_DH
SB_BLOCK = 128
SB_QB = 384
SB_GROUP = 4
SB_HEADS_PER_STEP = 2
SB_FWD_HEADS_PER_STEP = 4
GDN_HEADS_PER_STEP = 8
CONV_W = 2048
D_FF = 2816
RMS_EPS = 1e-6
L2_EPS = 1e-6
ADAM_LR = 0.001
ADAM_B1 = 0.9
ADAM_B2 = 0.999
ADAM_EPS = 1e-08
ADAM_WD = 0.01
ADAM_STEP = 10

P0 = 112
LANE = 128
SUB = 8
VMEM_LIMIT = 48 * 1024 * 1024
N_CHIPS = 4
N_DEV = 8

C_DQ, C_DK, C_DV, C_DZ, C_SQ, C_SK, C_SV, C_GDN, C_GSB = 0, 1, 2, 4, 6, 7, 8, 9, 10
PROJ_BIG = 11 * 1024
AB_COL = 2 * DN_QK + 2 * DN_V


def _params(n_axes):
    return pltpu.CompilerParams(dimension_semantics=("arbitrary",) * n_axes, vmem_limit_bytes=VMEM_LIMIT)


def _tile(n, target, q=LANE):
    best = None
    for t in range(q, min(n, target) + 1, q):
        if n % t == 0:
            best = t
    return best if best is not None else n


def _dot(a, b):
    return jnp.dot(a.astype(bf16), b.astype(bf16), preferred_element_type=f32)


def _dot_nt(a, b):
    return lax.dot_general(a.astype(bf16), b.astype(bf16), (((1,), (1,)), ((), ())), preferred_element_type=f32)


def _dot_tn(a, b):
    return lax.dot_general(a.astype(bf16), b.astype(bf16), (((0,), (0,)), ((), ())), preferred_element_type=f32)


_HI = lax.Precision.HIGH


def _hdot(a, b):
    return jnp.dot(a, b, precision=_HI, preferred_element_type=f32)


def _hdot_nt(a, b):
    return lax.dot_general(a, b, (((1,), (1,)), ((), ())), precision=_HI, preferred_element_type=f32)


def _hdot_tn(a, b):
    return lax.dot_general(a, b, (((0,), (0,)), ((), ())), precision=_HI, preferred_element_type=f32)


def _sigmoid(x):
    return 0.5 * jnp.tanh(0.5 * x) + 0.5


def _log1p_small(e):
    return jnp.where(e < 1e-3, e * (1.0 - e * (0.5 - e * (1.0 / 3.0))), jnp.log(1.0 + e))


def _rowsum(x):
    return jnp.sum(x, axis=1, keepdims=True)


def _allsum(x):
    return jnp.sum(jnp.sum(x, axis=1, keepdims=True), axis=0, keepdims=True)


def matmul(a, b, mode, name, residual=None, out_dtype=f32, tm_t=1408, tn_t=1024, tk_t=1408, rider=None):
    if mode == "nn":
        (M, K), (K2, N) = a.shape, b.shape
    elif mode == "nt":
        (M, K), (N, K2) = a.shape, b.shape
    else:
        (K, M), (K2, N) = a.shape, b.shape
    assert K == K2, (a.shape, b.shape, mode)
    tm, tn, tk = _tile(M, tm_t), _tile(N, tn_t), _tile(K, tk_t)
    nk = K // tk
    if mode == "nn":
        a_spec = pl.BlockSpec((tm, tk), lambda i, j, k: (i, k))
        b_spec = pl.BlockSpec((tk, tn), lambda i, j, k: (k, j))
        dims = (((1,), (0,)), ((), ()))
    elif mode == "nt":
        a_spec = pl.BlockSpec((tm, tk), lambda i, j, k: (i, k))
        b_spec = pl.BlockSpec((tn, tk), lambda i, j, k: (j, k))
        dims = (((1,), (1,)), ((), ()))
    else:
        a_spec = pl.BlockSpec((tk, tm), lambda i, j, k: (k, i))
        b_spec = pl.BlockSpec((tk, tn), lambda i, j, k: (k, j))
        dims = (((0,), (0,)), ((), ()))
    o_spec = pl.BlockSpec((tm, tn), lambda i, j, k: (i, j))
    has_res = residual is not None
    grid = (M // tm, N // tn, nk)
    split, ride_first, ride_last = _ride(rider, 3 if has_res else 2, 1, grid)

    def body(*refs):
        ins_, (o_ref,), (rin, rout, rest) = split(refs)
        a_ref, b_ref = ins_[:2]
        r_ref = ins_[2] if has_res else None
        acc_ref, ride = rest[0], (rin, rout, rest[1:])
        ride_first(ride)
        k = pl.program_id(2)

        @pl.when(k == 0)
        def _():
            acc_ref[...] = jnp.zeros_like(acc_ref)

        acc_ref[...] += lax.dot_general(a_ref[...].astype(bf16), b_ref[...].astype(bf16), dims,
                                        preferred_element_type=f32)

        @pl.when(k == nk - 1)
        def _():
            r = acc_ref[...]
            if has_res:
                r = r + r_ref[...]
            o_ref[...] = r.astype(out_dtype)

        ride_last(ride)

    ins = [a, b] + ([residual] if has_res else [])
    specs = [a_spec, b_spec] + ([o_spec] if has_res else [])
    r_ins = rider.ins if rider else []
    r_outs = rider.out_shapes if rider else []
    res = pl.pallas_call(
        body, name=name, grid=grid, in_specs=specs + [_ANY] * len(r_ins), out_specs=[o_spec] + [_ANY] * len(r_outs),
        out_shape=[jax.ShapeDtypeStruct((M, N), out_dtype)] + list(r_outs),
        scratch_shapes=[pltpu.VMEM((tm, tn), f32)] + (rider.scratch if rider else []), compiler_params=_params(3),
    )(*ins, *r_ins)
    return (res[0], res[1:]) if rider else res[0]


def _row_tile(tp):
    return _tile(tp, 512)


def rms_fwd(h, gain, name):
    tp, d = h.shape
    rt = _row_tile(tp)

    def body(h_ref, g_ref, o_ref):
        x = h_ref[...]
        r = lax.rsqrt(jnp.mean(x * x, axis=-1, keepdims=True) + RMS_EPS)
        o_ref[...] = (x * r * g_ref[...]).astype(bf16)

    return pl.pallas_call(
        body, name=name, grid=(tp // rt,),
        in_specs=[pl.BlockSpec((rt, d), lambda i: (i, 0)), pl.BlockSpec((1, d), lambda i: (0, 0))],
        out_specs=pl.BlockSpec((rt, d), lambda i: (i, 0)),
        out_shape=jax.ShapeDtypeStruct((tp, d), bf16), compiler_params=_params(1),
    )(h, gain)


def rms_bwd(h, gain, dn, dres, name, rider=None):
    tp, d = h.shape
    rt = _row_tile(tp)
    grid = (tp // rt,)
    split, ride_first, ride_last = _ride(rider, 4, 3, grid)

    def body(*refs):
        (h_ref, g_ref, dn_ref, dr_ref), (dh_ref, dhb_ref, dg_ref), ride = split(refs)
        ride_first(ride)
        i = pl.program_id(0)
        x = h_ref[...]
        r = lax.rsqrt(jnp.mean(x * x, axis=-1, keepdims=True) + RMS_EPS)
        xh = x * r
        dn_ = dn_ref[...]
        dxh = dn_ * g_ref[...]
        dh = r * (dxh - xh * jnp.mean(dxh * xh, axis=-1, keepdims=True)) + dr_ref[...]
        dh_ref[...] = dh
        dhb_ref[...] = dh.astype(bf16)
        part = jnp.sum(dn_ * xh, axis=0, keepdims=True)

        @pl.when(i == 0)
        def _():
            dg_ref[...] = part

        @pl.when(i > 0)
        def _():
            dg_ref[...] += part

        ride_last(ride)

    row = pl.BlockSpec((rt, d), lambda i: (i, 0))
    vec = pl.BlockSpec((1, d), lambda i: (0, 0))
    r_ins = rider.ins if rider else []
    r_outs = rider.out_shapes if rider else []
    res = pl.pallas_call(
        body, name=name, grid=grid, in_specs=[row, vec, row, row] + [_ANY] * len(r_ins),
        out_specs=[row, row, vec] + [_ANY] * len(r_outs),
        out_shape=[jax.ShapeDtypeStruct((tp, d), f32), jax.ShapeDtypeStruct((tp, d), bf16),
                   jax.ShapeDtypeStruct((1, d), f32)] + list(r_outs),
        scratch_shapes=rider.scratch if rider else [], compiler_params=_params(1),
    )(h, gain, dn, dres, *r_ins)
    return res[:3], res[3:]


def loss_head(y, target):
    tp, d = y.shape
    lead = P0 + N_META
    rt = _row_tile(tp)
    ns = rt // lead
    assert lead == SB_BLOCK and rt % lead == 0 and target.shape == (tp - lead, d)
    last = target.shape[0] // lead - 1

    def body(*refs):
        y_ref, t_refs = refs[0], refs[1:1 + ns]
        dy_ref, dyb_ref, l_ref = refs[1 + ns:]
        i = pl.program_id(0)

        @pl.when(i == 0)
        def _():
            l_ref[...] = jnp.zeros_like(l_ref)

        part = jnp.zeros((1, 1), f32)
        for s in range(ns):
            rows = slice(s * lead, (s + 1) * lead)
            err = y_ref[rows, :] - t_refs[s][...]
            if s == 0:
                err = err * (i > 0).astype(f32)
            dy = err * (1.0 / d)
            dy_ref[rows, :] = dy
            dyb_ref[rows, :] = dy.astype(bf16)
            part = part + _allsum(err * err)
        l_ref[...] += jnp.broadcast_to(part * (0.5 / d), l_ref.shape)

    row = pl.BlockSpec((rt, d), lambda i: (i, 0))
    t_specs = [pl.BlockSpec((lead, d), lambda i, s=s: (jnp.clip(ns * i + s - 1, 0, last), 0)) for s in range(ns)]
    return pl.pallas_call(
        body, name="loss_head", grid=(tp // rt,), in_specs=[row] + t_specs,
        out_specs=[row, row, pl.BlockSpec((1, LANE), lambda i: (0, 0))],
        out_shape=[jax.ShapeDtypeStruct((tp, d), f32), jax.ShapeDtypeStruct((tp, d), bf16),
                   jax.ShapeDtypeStruct((1, LANE), f32)],
        compiler_params=_params(1),
    )(y, *([target] * ns))


def swiglu_fwd(u):
    tp = u.shape[0]
    rt, cb = _row_tile(tp), D_FF // 2
    nb = D_FF // cb

    def body(g_ref, u_ref, o_ref):
        g = g_ref[...]
        o_ref[...] = (g * _sigmoid(g) * u_ref[...]).astype(bf16)

    return pl.pallas_call(
        body, name="swiglu_fwd", grid=(tp // rt, nb),
        in_specs=[pl.BlockSpec((rt, cb), lambda i, j: (i, j)), pl.BlockSpec((rt, cb), lambda i, j: (i, j + nb))],
        out_specs=pl.BlockSpec((rt, cb), lambda i, j: (i, j)),
        out_shape=jax.ShapeDtypeStruct((tp, D_FF), bf16), compiler_params=_params(2),
    )(u, u)


def swiglu_bwd(u, dact):
    tp = u.shape[0]
    rt, cb = _row_tile(tp), D_FF // 2
    nb = D_FF // cb

    def body(g_ref, u_ref, da_ref, dg_ref, du_ref):
        g = g_ref[...]
        s = _sigmoid(g)
        da = da_ref[...]
        dg_ref[...] = (da * u_ref[...] * s * (1.0 + g * (1.0 - s))).astype(bf16)
        du_ref[...] = (da * g * s).astype(bf16)

    lo = pl.BlockSpec((rt, cb), lambda i, j: (i, j))
    hi = pl.BlockSpec((rt, cb), lambda i, j: (i, j + nb))
    dgate, dup = pl.pallas_call(
        body, name="swiglu_bwd", grid=(tp // rt, nb), in_specs=[lo, hi, lo], out_specs=[lo, lo],
        out_shape=[jax.ShapeDtypeStruct((tp, D_FF), bf16)] * 2, compiler_params=_params(2),
    )(u, u, dact)
    return dgate, dup


def merge_fwd(proj, ydn, ysb):
    tp = proj.shape[0]
    rt, d = _row_tile(tp), D_MODEL

    def body(gd_ref, gs_ref, yd_ref, ys_ref, o_ref):
        o_ref[...] = (_sigmoid(gd_ref[...]) * yd_ref[...] + _sigmoid(gs_ref[...]) * ys_ref[...]).astype(bf16)

    row = pl.BlockSpec((rt, d), lambda i: (i, 0))
    return pl.pallas_call(
        body, name="merge_fwd", grid=(tp // rt,),
        in_specs=[pl.BlockSpec((rt, d), lambda i: (i, C_GDN)), pl.BlockSpec((rt, d), lambda i: (i, C_GSB)), row, row],
        out_specs=row, out_shape=jax.ShapeDtypeStruct((tp, d), bf16), compiler_params=_params(1),
    )(proj, proj, ydn, ysb)


def merge_bwd(proj, ydn, ysb, dm):
    tp = proj.shape[0]
    rt, d = _row_tile(tp), D_MODEL

    def body(gd_ref, gs_ref, yd_ref, ys_ref, dm_ref, dyd_ref, dys_ref, dg_ref):
        dm_ = dm_ref[...]
        sd = _sigmoid(gd_ref[...])
        ss = _sigmoid(gs_ref[...])
        dyd_ref[...] = (dm_ * sd).astype(bf16)
        dys_ref[...] = (dm_ * ss).astype(bf16)
        dg_ref[:, :d] = (dm_ * yd_ref[...] * sd * (1.0 - sd)).astype(bf16)
        dg_ref[:, d:] = (dm_ * ys_ref[...] * ss * (1.0 - ss)).astype(bf16)

    row = pl.BlockSpec((rt, d), lambda i: (i, 0))
    return pl.pallas_call(
        body, name="merge_bwd", grid=(tp // rt,),
        in_specs=[pl.BlockSpec((rt, d), lambda i: (i, C_GDN)), pl.BlockSpec((rt, d), lambda i: (i, C_GSB)), row, row, row],
        out_specs=[row, row, pl.BlockSpec((rt, 2 * d), lambda i: (i, 0))],
        out_shape=[jax.ShapeDtypeStruct((tp, d), bf16)] * 2 + [jax.ShapeDtypeStruct((tp, 2 * d), bf16)],
        compiler_params=_params(1),
    )(proj, proj, ydn, ysb, dm)


def dn_out_fwd(o, proj, gain):
    tp = o.shape[0]
    rt, cb, wide = _row_tile(tp), DN_DV, 1024
    zb = C_DZ * 1024 // wide

    def body(o_ref, z_ref, g_ref, y_ref):
        for s in range(wide // cb):
            sl = slice(s * cb, (s + 1) * cb)
            x = o_ref[:, sl]
            r = lax.rsqrt(jnp.mean(x * x, axis=-1, keepdims=True) + RMS_EPS)
            z = z_ref[:, sl]
            y_ref[:, sl] = (x * r * g_ref[...] * (z * _sigmoid(z))).astype(bf16)

    blk = pl.BlockSpec((rt, wide), lambda i, j: (i, j))
    return pl.pallas_call(
        body, name="dn_out_fwd", grid=(tp // rt, DN_V // wide),
        in_specs=[blk, pl.BlockSpec((rt, wide), lambda i, j: (i, j + zb)), pl.BlockSpec((1, cb), lambda i, j: (0, 0))],
        out_specs=blk, out_shape=jax.ShapeDtypeStruct((tp, DN_V), bf16), compiler_params=_params(2),
    )(o, proj, gain)


def dn_out_bwd(o, proj, gain, dy):
    tp = o.shape[0]
    rt, cb, wide = _row_tile(tp), DN_DV, 1024
    zb = C_DZ * 1024 // wide

    def body(o_ref, z_ref, g_ref, dy_ref, do_ref, dz_ref, dg_ref):
        i, j = pl.program_id(0), pl.program_id(1)
        g = g_ref[...]
        part = jnp.zeros((1, cb), f32)
        for hh in range(wide // cb):
            sl = slice(hh * cb, (hh + 1) * cb)
            x = o_ref[:, sl]
            r = lax.rsqrt(jnp.mean(x * x, axis=-1, keepdims=True) + RMS_EPS)
            xh = x * r
            z = z_ref[:, sl]
            s = _sigmoid(z)
            dy_ = dy_ref[:, sl]
            drn = dy_ * (z * s)
            dz_ref[:, sl] = (dy_ * xh * g * s * (1.0 + z * (1.0 - s))).astype(bf16)
            dxh = drn * g
            do_ref[:, sl] = r * (dxh - xh * jnp.mean(dxh * xh, axis=-1, keepdims=True))
            part = part + jnp.sum(drn * xh, axis=0, keepdims=True)
        first = jnp.logical_and(i == 0, j == 0)

        @pl.when(first)
        def _():
            dg_ref[...] = part

        @pl.when(jnp.logical_not(first))
        def _():
            dg_ref[...] += part

    blk = pl.BlockSpec((rt, wide), lambda i, j: (i, j))
    vec = pl.BlockSpec((1, cb), lambda i, j: (0, 0))
    return pl.pallas_call(
        body, name="dn_out_bwd", grid=(tp // rt, DN_V // wide),
        in_specs=[blk, pl.BlockSpec((rt, wide), lambda i, j: (i, j + zb)), vec, blk],
        out_specs=[blk, pl.BlockSpec((rt, wide), lambda i, j: (i, j + zb)), vec],
        out_shape=[jax.ShapeDtypeStruct((tp, DN_V), f32), jax.ShapeDtypeStruct((tp, PROJ_BIG), bf16),
                   jax.ShapeDtypeStruct((1, cb), f32)],
        compiler_params=_params(2),
    )(o, proj, gain, dy)


def sb_prep_fwd(proj, gq, gk):
    tp = proj.shape[0]
    rt, cb = _row_tile(tp), SB_DH

    def body(q_ref, k_ref, v_ref, gq_ref, gk_ref, qo_ref, ko_ref, vo_ref):
        for x_ref, g_ref, o_ref in ((q_ref, gq_ref, qo_ref), (k_ref, gk_ref, ko_ref)):
            for h in range(HEADS):
                sl = slice(h * cb, (h + 1) * cb)
                x = x_ref[:, sl]
                r = lax.rsqrt(jnp.mean(x * x, axis=-1, keepdims=True) + RMS_EPS)
                o_ref[:, sl] = (x * r * g_ref[...]).astype(bf16)
        vo_ref[...] = v_ref[...].astype(bf16)

    blk = pl.BlockSpec((rt, SB_W), lambda i: (i, 0))
    vec = pl.BlockSpec((1, cb), lambda i: (0, 0))
    return pl.pallas_call(
        body, name="sb_prep_fwd", grid=(tp // rt,),
        in_specs=[pl.BlockSpec((rt, SB_W), lambda i: (i, C_SQ)), pl.BlockSpec((rt, SB_W), lambda i: (i, C_SK)),
                  pl.BlockSpec((rt, SB_W), lambda i: (i, C_SV)), vec, vec],
        out_specs=[blk] * 3, out_shape=[jax.ShapeDtypeStruct((tp, SB_W), bf16)] * 3, compiler_params=_params(1),
    )(proj, proj, proj, gq, gk)


def sb_prep_bwd(proj, gq, gk, dqs, dks, dvs, into):
    tp = proj.shape[0]
    rt, cb = _row_tile(tp), SB_DH
    assert (C_SQ * 1024) % (3 * SB_W) == 0 and (C_SQ + 1, C_SQ + 2) == (C_SK, C_SV)

    def body(q_ref, k_ref, gq_ref, gk_ref, dq_ref, dk_ref, dv_ref, into_ref, do_ref, dgq_ref, dgk_ref):
        first = pl.program_id(0) == 0
        do_ref[:, 2 * SB_W:] = dv_ref[...].astype(bf16)
        for x_ref, g_ref, dn_ref, at, dg_ref, mul in ((q_ref, gq_ref, dq_ref, 0, dgq_ref, None),
                                                      (k_ref, gk_ref, dk_ref, SB_W, dgk_ref, SB_DH ** -0.5)):
            part = jnp.zeros((1, cb), f32)
            for h in range(HEADS):
                sl = slice(h * cb, (h + 1) * cb)
                x = x_ref[:, sl]
                r = lax.rsqrt(jnp.mean(x * x, axis=-1, keepdims=True) + RMS_EPS)
                xh = x * r
                dn_ = dn_ref[:, sl] if mul is None else dn_ref[:, sl] * mul
                dxh = dn_ * g_ref[...]
                do_ref[:, at + h * cb:at + (h + 1) * cb] = (
                    r * (dxh - xh * jnp.mean(dxh * xh, axis=-1, keepdims=True))).astype(bf16)
                part = part + jnp.sum(dn_ * xh, axis=0, keepdims=True)

            @pl.when(first)
            def _(dg_ref=dg_ref, part=part):
                dg_ref[...] = part

            @pl.when(jnp.logical_not(first))
            def _(dg_ref=dg_ref, part=part):
                dg_ref[...] += part

    blk = pl.BlockSpec((rt, SB_W), lambda i: (i, 0))
    vec = pl.BlockSpec((1, cb), lambda i: (0, 0))
    return pl.pallas_call(
        body, name="sb_prep_bwd", grid=(tp // rt,),
        in_specs=[pl.BlockSpec((rt, SB_W), lambda i: (i, C_SQ)), pl.BlockSpec((rt, SB_W), lambda i: (i, C_SK)),
                  vec, vec, blk, blk, blk, pl.BlockSpec(memory_space=pl.ANY)],
        out_specs=[pl.BlockSpec((rt, 3 * SB_W), lambda i: (i, C_SQ * 1024 // (3 * SB_W))), vec, vec],
        out_shape=[jax.ShapeDtypeStruct(into.shape, into.dtype)] + [jax.ShapeDtypeStruct((1, cb), f32)] * 2,
        input_output_aliases={7: 0}, compiler_params=_params(1),
    )(proj, proj, gq, gk, dqs, dks, dvs, into)


def _conv_taps(ext, rt):
    taps = []
    for k in range(DN_CONV):
        s = DN_CONV - 1 - k
        taps.append((pltpu.roll(ext, s, axis=0) if s else ext)[SUB:SUB + rt])
    return taps


def _conv_act(taps, w, l2):
    y = taps[0] * w[0:1]
    for k in range(1, DN_CONV):
        y = y + taps[k] * w[k:k + 1]
    s = _sigmoid(y)
    a = y * s
    if l2:
        n = lax.rsqrt(jnp.sum(a * a, axis=-1, keepdims=True) + L2_EPS)
        return y, s, a, n
    return y, s, a, None


def conv_fwd(proj, w8, col_blk, ncb, l2, name):
    tp = proj.shape[0]
    rt = _row_tile(tp)
    hb = rt // SUB
    cw = min(CONV_W, ncb * LANE)
    cb0 = col_blk * LANE // cw

    def body(x_ref, h_ref, w_ref, o_ref):
        i = pl.program_id(1)
        first = (i > 0).astype(f32)
        for s in range(cw // LANE):
            sl = slice(s * LANE, (s + 1) * LANE)
            ext = jnp.concatenate([h_ref[:, sl] * first, x_ref[:, sl]], axis=0)
            _, _, a, n = _conv_act(_conv_taps(ext, rt), w_ref[:, sl], l2)
            o_ref[:, sl] = a * n if l2 else a

    return pl.pallas_call(
        body, name=name, grid=(ncb * LANE // cw, tp // rt),
        in_specs=[pl.BlockSpec((rt, cw), lambda j, i: (i, j + cb0)),
                  pl.BlockSpec((SUB, cw), lambda j, i: (jnp.maximum(i * hb - 1, 0), j + cb0)),
                  pl.BlockSpec((SUB, cw), lambda j, i: (0, j))],
        out_specs=pl.BlockSpec((rt, cw), lambda j, i: (i, j)),
        out_shape=jax.ShapeDtypeStruct((tp, ncb * LANE), f32), compiler_params=_params(2),
    )(proj, proj, w8)


def conv_bwd(proj, w8, dout, into, col_blk, ncb, l2, name):
    tp = proj.shape[0]
    rt = _row_tile(tp)
    hb = rt // SUB
    nr = tp // rt
    last8 = tp // SUB - 1
    cw = min(CONV_W, ncb * LANE)
    cb0 = col_blk * LANE // cw
    n = rt + SUB

    def body(x_ref, xb_ref, xf_ref, w_ref, d_ref, df_ref, into_ref, o_ref, dw_ref):
        i = pl.program_id(1)
        first = (i > 0).astype(f32)
        last = (i < nr - 1).astype(f32)
        rows = lax.broadcasted_iota(jnp.int32, (SUB, LANE), 0)
        for s in range(cw // LANE):
            sl = slice(s * LANE, (s + 1) * LANE)
            ext = jnp.concatenate([xb_ref[:, sl] * first, x_ref[:, sl], xf_ref[:, sl] * last], axis=0)
            taps = _conv_taps(ext, n)
            w = w_ref[:, sl]
            y, sg, a, nrm = _conv_act(taps, w, l2)
            da = jnp.concatenate([d_ref[:, sl], df_ref[:, sl] * last], axis=0)
            if l2:
                out = a * nrm
                da = nrm * (da - out * jnp.sum(da * out, axis=-1, keepdims=True))
            dy = da * sg * (1.0 + y * (1.0 - sg))
            part = jnp.zeros((SUB, LANE), f32)
            for k in range(DN_CONV):
                part = part + jnp.where(rows == k, jnp.sum(taps[k][0:rt] * dy[0:rt], axis=0, keepdims=True), 0.0)

            @pl.when(i == 0)
            def _(sl=sl, part=part):
                dw_ref[:, sl] = part

            @pl.when(i > 0)
            def _(sl=sl, part=part):
                dw_ref[:, sl] += part

            acc = None
            for k in range(DN_CONV):
                up = DN_CONV - 1 - k
                term = (pltpu.roll(dy, n - up, axis=0) if up else dy)[0:rt] * w[k:k + 1]
                acc = term if acc is None else acc + term
            o_ref[:, sl] = acc.astype(bf16)

    after = lambda j, i: (jnp.minimum((i + 1) * hb, last8), j)
    return pl.pallas_call(
        body, name=name, grid=(ncb * LANE // cw, nr),
        in_specs=[pl.BlockSpec((rt, cw), lambda j, i: (i, j + cb0)),
                  pl.BlockSpec((SUB, cw), lambda j, i: (jnp.maximum(i * hb - 1, 0), j + cb0)),
                  pl.BlockSpec((SUB, cw), lambda j, i: (jnp.minimum((i + 1) * hb, last8), j + cb0)),
                  pl.BlockSpec((SUB, cw), lambda j, i: (0, j)),
                  pl.BlockSpec((rt, cw), lambda j, i: (i, j)), pl.BlockSpec((SUB, cw), after),
                  pl.BlockSpec(memory_space=pl.ANY)],
        out_specs=[pl.BlockSpec((rt, cw), lambda j, i: (i, j + cb0)), pl.BlockSpec((SUB, cw), lambda j, i: (0, j))],
        out_shape=[jax.ShapeDtypeStruct(into.shape, into.dtype), jax.ShapeDtypeStruct((SUB, ncb * LANE), f32)],
        input_output_aliases={6: 0}, compiler_params=_params(2),
    )(proj, proj, proj, w8, dout, dout, into)


def _ab_common(p, al, dtb, r0):
    rows = r0 + lax.broadcasted_iota(jnp.int32, p.shape, 0)
    mask = (rows >= P0).astype(f32)
    xx = p + dtb
    sp = jnp.maximum(xx, 0.0) + _log1p_small(jnp.exp(-jnp.abs(xx)))
    ea = jnp.exp(al)
    g = -ea * sp * mask
    beta = _sigmoid(p) * mask
    return g, beta, _sigmoid(xx), ea, mask


def _chunk_tri(rt, later):
    r = lax.broadcasted_iota(jnp.int32, (rt, rt), 0)
    c = lax.broadcasted_iota(jnp.int32, (rt, rt), 1)
    shift = CHUNK.bit_length() - 1
    same = jnp.right_shift(r, shift) == jnp.right_shift(c, shift)
    return jnp.logical_and(same, c >= r if later else c <= r).astype(f32)


def ab_fwd(pab, al, dtb):
    tp = pab.shape[0]
    rt = _row_tile(tp)
    assert rt % CHUNK == 0

    def body(p_ref, al_ref, dt_ref, g_ref, b_ref):
        i = pl.program_id(0)
        g, beta, _, _, _ = _ab_common(p_ref[...], al_ref[...], dt_ref[...], i * rt)
        gam = _hdot(_chunk_tri(rt, False), g)
        for h in range(HEADS):
            g_ref[h] = jnp.broadcast_to(gam[:, h:h + 1], (rt, LANE))
            b_ref[h] = jnp.broadcast_to(beta[:, HEADS + h:HEADS + h + 1], (rt, LANE))

    vec = pl.BlockSpec((1, LANE), lambda i: (0, 0))
    out = pl.BlockSpec((HEADS, rt, LANE), lambda i: (0, i, 0))
    return pl.pallas_call(
        body, name="ab_fwd", grid=(tp // rt,), in_specs=[pl.BlockSpec((rt, LANE), lambda i: (i, 0)), vec, vec],
        out_specs=[out, out], out_shape=[jax.ShapeDtypeStruct((HEADS, tp, LANE), f32)] * 2, compiler_params=_params(1),
    )(pab, al, dtb)


def ab_bwd(pab, al, dtb, dg, db):
    tp = pab.shape[0]
    rt = _row_tile(tp)

    def body(p_ref, al_ref, dt_ref, dg_ref, db_ref, dp_ref, dal_ref, ddt_ref):
        i = pl.program_id(0)
        g, beta, sx, ea, mask = _ab_common(p_ref[...], al_ref[...], dt_ref[...], i * rt)
        lanes = lax.broadcasted_iota(jnp.int32, (rt, LANE), 1)
        dgl = jnp.zeros((rt, LANE), f32)
        dbl = jnp.zeros((rt, LANE), f32)
        for h in range(HEADS):
            dgl = dgl + jnp.where(lanes == h, dg_ref[h], 0.0)
            dbl = dbl + jnp.where(lanes == HEADS + h, db_ref[h], 0.0)
        dgl = _hdot(_chunk_tri(rt, True), dgl)
        dxx = dgl * (-ea) * sx * mask
        dp_ref[...] = (dxx + dbl * beta * (1.0 - beta)).astype(bf16)
        pal = jnp.sum(dgl * g, axis=0, keepdims=True)
        pdt = jnp.sum(dxx, axis=0, keepdims=True)

        @pl.when(i == 0)
        def _():
            dal_ref[...] = pal
            ddt_ref[...] = pdt

        @pl.when(i > 0)
        def _():
            dal_ref[...] += pal
            ddt_ref[...] += pdt

    vec = pl.BlockSpec((1, LANE), lambda i: (0, 0))
    row = pl.BlockSpec((rt, LANE), lambda i: (i, 0))
    big = pl.BlockSpec((HEADS, rt, LANE), lambda i: (0, i, 0))
    return pl.pallas_call(
        body, name="ab_bwd", grid=(tp // rt,), in_specs=[row, vec, vec, big, big], out_specs=[row, vec, vec],
        out_shape=[jax.ShapeDtypeStruct((tp, LANE), bf16), jax.ShapeDtypeStruct((1, LANE), f32),
                   jax.ShapeDtypeStruct((1, LANE), f32)],
        compiler_params=_params(1),
    )(pab, al, dtb, dg, db)


class _Chunk:
    pass


def _gdn_chunk(q, k, v, gcol, bcol, grow8):
    C = CHUNK
    R = range(len(q))
    X = _Chunk()
    ri = lax.broadcasted_iota(jnp.int32, (C, C), 0)
    ci = lax.broadcasted_iota(jnp.int32, (C, C), 1)
    eye = (ri == ci).astype(f32)
    gam = list(gcol)
    gam_row = [grow8[h][0:1, 0:C] for h in R]
    X.ri, X.ci = ri, ci
    X.Dm = [jnp.where(ri >= ci, jnp.exp(jnp.minimum(gam[h][:, 0:C] - gam_row[h], 0.0)), 0.0) for h in R]
    X.eg = [jnp.exp(gam[h]) for h in R]
    gl = [gam[h][C - 1:C, :] for h in R]
    X.egl = [jnp.exp(gl[h]) for h in R]
    X.kdec = [jnp.exp(gl[h] - gam[h]) for h in R]
    X.qs = [q[h] * (DN_DK ** -0.5) for h in R]
    X.kb = [k[h] * bcol[h] for h in R]
    kk = [_dot_nt(X.kb[h], k[h]) for h in R]
    qk = [_dot_nt(X.qs[h], k[h]) for h in R]
    X.A = [jnp.where(ri > ci, kk[h] * X.Dm[h], 0.0) for h in R]
    assert C == 64
    b16 = jnp.right_shift(ri, 4) == jnp.right_shift(ci, 4)
    b32 = jnp.right_shift(ri, 5) == jnp.right_shift(ci, 5)
    P = [jnp.where(b16, X.A[h], 0.0) for h in R]
    T = [eye - P[h] for h in R]
    for _ in range(3):
        P = [_hdot(P[h], P[h]) for h in R]
        T = [T[h] + _hdot(T[h], P[h]) for h in R]
    for off in (jnp.logical_and(b32, jnp.logical_not(b16)), jnp.logical_not(b32)):
        AT = [_hdot(jnp.where(off, X.A[h], 0.0), T[h]) for h in R]
        T = [T[h] - _hdot(T[h], AT[h]) for h in R]
    X.T = T
    X.b2 = [jnp.concatenate([bcol[h], bcol[h]], axis=-1) for h in R]
    X.u = [_hdot(T[h], v[h] * X.b2[h]) for h in R]
    X.w = [_hdot(T[h], X.kb[h] * X.eg[h]) for h in R]
    X.attn = [qk[h] * X.Dm[h] for h in R]
    X.qg = [X.qs[h] * X.eg[h] for h in R]
    X.kg = [k[h] * X.kdec[h] for h in R]
    return X


def gdn_fwd(q, k, v, gc, bc, grow):
    tp = q.shape[0]
    nc = tp // CHUNK
    hb = GDN_HEADS_PER_STEP

    def body(q_ref, k_ref, v_ref, gc_ref, bc_ref, gr_ref, o_ref, ss_ref, S_ref):
        c = pl.program_id(1)

        @pl.when(c == 0)
        def _():
            S_ref[...] = jnp.zeros_like(S_ref)

        R = range(hb)
        qc = [slice(h * DN_DK, (h + 1) * DN_DK) for h in R]
        vc = [slice(h * DN_DV, (h + 1) * DN_DV) for h in R]
        X = _gdn_chunk([q_ref[:, qc[h]] for h in R], [k_ref[:, qc[h]] for h in R], [v_ref[:, vc[h]] for h in R],
                       [gc_ref[h] for h in R], [bc_ref[h] for h in R], [gr_ref[h] for h in R])
        S = [S_ref[h] for h in R]
        for h in R:
            ss_ref[h, 0] = S[h]
        wS = [_dot(X.w[h], S[h]) for h in R]
        qS = [_dot(X.qg[h], S[h]) for h in R]
        vn = [X.u[h] - wS[h] for h in R]
        av = [_dot(X.attn[h], vn[h]) for h in R]
        kv = [_dot_tn(X.kg[h], vn[h]) for h in R]
        for h in R:
            o_ref[:, vc[h]] = qS[h] + av[h]
            S_ref[h] = S[h] * X.egl[h][:, 0:1] + kv[h]

    qk = pl.BlockSpec((CHUNK, hb * DN_DK), lambda g, c: (c, g))
    vv = pl.BlockSpec((CHUNK, hb * DN_DV), lambda g, c: (c, g))
    col = pl.BlockSpec((hb, CHUNK, LANE), lambda g, c: (g, c, 0))
    row = pl.BlockSpec((hb, SUB, LANE), lambda g, c: (g, c, 0))
    return pl.pallas_call(
        body, name="gdn_fwd", grid=(HEADS // hb, nc), in_specs=[qk, qk, vv, col, col, row],
        out_specs=[vv, pl.BlockSpec((hb, 1, DN_DK, DN_DV), lambda g, c: (g, c, 0, 0))],
        out_shape=[jax.ShapeDtypeStruct((tp, DN_V), f32), jax.ShapeDtypeStruct((HEADS, nc, DN_DK, DN_DV), f32)],
        scratch_shapes=[pltpu.VMEM((hb, DN_DK, DN_DV), f32)], compiler_params=_params(2),
    )(q, k, v, gc, bc, grow)


def gdn_bwd(q, k, v, gc, bc, grow, states, do, rider=None):
    tp = q.shape[0]
    nc = tp // CHUNK
    C = CHUNK
    hb = GDN_HEADS_PER_STEP
    grid = (HEADS // hb, nc)
    split, ride_first, ride_last = _ride(rider, 8, 5, grid)

    def body(*refs):
        ((q_ref, k_ref, v_ref, gc_ref, bc_ref, gr_ref, ss_ref, do_ref), (dq_ref, dk_ref, dv_ref, dg_ref, db_ref),
         (rin, rout, rest)) = split(refs)
        dS_ref, ride = rest[0], (rin, rout, rest[1:])
        ride_first(ride)
        c = pl.program_id(1)

        @pl.when(c == 0)
        def _():
            dS_ref[...] = jnp.zeros_like(dS_ref)

        R = range(hb)
        qc = [slice(h * DN_DK, (h + 1) * DN_DK) for h in R]
        vc = [slice(h * DN_DV, (h + 1) * DN_DV) for h in R]
        k_ = [k_ref[:, qc[h]] for h in R]
        v_ = [v_ref[:, vc[h]] for h in R]
        bcol = [bc_ref[h] for h in R]
        X = _gdn_chunk([q_ref[:, qc[h]] for h in R], k_, v_, [gc_ref[h] for h in R], bcol, [gr_ref[h] for h in R])
        ri, ci = X.ri, X.ci
        S = [ss_ref[h, 0] for h in R]
        do_ = [do_ref[:, vc[h]] for h in R]
        dSn = [dS_ref[h] for h in R]
        wS = [_dot(X.w[h], S[h]) for h in R]
        ado = [_dot_tn(X.attn[h], do_[h]) for h in R]
        kdS = [_dot(X.kg[h], dSn[h]) for h in R]
        d_qg = [_dot_nt(do_[h], S[h]) for h in R]
        qdo = [_dot_tn(X.qg[h], do_[h]) for h in R]
        vn = [X.u[h] - wS[h] for h in R]
        d_vn = [ado[h] + kdS[h] for h in R]
        dovn = [_dot_nt(do_[h], vn[h]) for h in R]
        d_kg = [_dot_nt(vn[h], dSn[h]) for h in R]
        wdv = [_dot_tn(X.w[h], d_vn[h]) for h in R]
        dw = [-_dot_nt(d_vn[h], S[h]) for h in R]
        for h in R:
            dS_ref[h] = qdo[h] + X.egl[h][:, 0:1] * dSn[h] - wdv[h]
        dattn = [jnp.where(ri >= ci, dovn[h], 0.0) for h in R]
        dRu = [_hdot_tn(X.T[h], d_vn[h]) for h in R]
        dRw = [_hdot_tn(X.T[h], dw[h]) for h in R]
        dAu = [_hdot_nt(dRu[h], X.u[h]) for h in R]
        dAw = [_hdot_nt(dRw[h], X.w[h]) for h in R]
        dA = [jnp.where(ri > ci, -(dAu[h] + dAw[h]), 0.0) for h in R]
        dKK = [dA[h] * X.Dm[h] for h in R]
        dQK = [dattn[h] * X.Dm[h] for h in R]
        E = [dA[h] * X.A[h] + dattn[h] * X.attn[h] for h in R]
        dkb = [_dot(dKK[h], k_[h]) + dRw[h] * X.eg[h] for h in R]
        dk1 = [_dot_tn(dKK[h], X.kb[h]) for h in R]
        dqs = [_dot(dQK[h], k_[h]) + d_qg[h] * X.eg[h] for h in R]
        dk2 = [_dot_tn(dQK[h], X.qs[h]) for h in R]
        ones = jnp.ones((C, LANE), f32)
        colE = [_hdot_tn(E[h], ones) for h in R]
        rows = lax.broadcasted_iota(jnp.int32, (C, LANE), 0)
        dgam = []
        for h in R:
            t = d_kg[h] * X.kg[h]
            dgl = _allsum(t) + X.egl[h][:, 0:1] * _allsum(S[h] * dSn[h])
            g = (_rowsum(E[h]) - colE[h] + _rowsum(dRw[h] * (X.kb[h] * X.eg[h])) + _rowsum(d_qg[h] * X.qg[h])
                 - _rowsum(t))
            dgam.append(g + jnp.where(rows == C - 1, dgl, 0.0))
        for h in R:
            dv_ref[:, vc[h]] = dRu[h] * X.b2[h]
            dbeta = _rowsum(dRu[h] * v_[h]) + _rowsum(dkb[h] * k_[h])
            dq_ref[:, qc[h]] = dqs[h] * (DN_DK ** -0.5)
            dk_ref[:, qc[h]] = dk1[h] + dk2[h] + dkb[h] * bcol[h] + d_kg[h] * X.kdec[h]
            dg_ref[h] = dgam[h]
            db_ref[h] = jnp.broadcast_to(dbeta, (C, LANE))
        ride_last(ride)

    rc = lambda c: nc - 1 - c
    qk = pl.BlockSpec((CHUNK, hb * DN_DK), lambda g, c: (rc(c), g))
    vv = pl.BlockSpec((CHUNK, hb * DN_DV), lambda g, c: (rc(c), g))
    col = pl.BlockSpec((hb, CHUNK, LANE), lambda g, c: (g, rc(c), 0))
    row = pl.BlockSpec((hb, SUB, LANE), lambda g, c: (g, rc(c), 0))
    st = pl.BlockSpec((hb, 1, DN_DK, DN_DV), lambda g, c: (g, rc(c), 0, 0))
    r_ins = rider.ins if rider else []
    r_outs = rider.out_shapes if rider else []
    res = pl.pallas_call(
        body, name="gdn_bwd", grid=grid, in_specs=[qk, qk, vv, col, col, row, st, vv] + [_ANY] * len(r_ins),
        out_specs=[qk, qk, vv, col, col] + [_ANY] * len(r_outs),
        out_shape=[jax.ShapeDtypeStruct((tp, DN_QK), f32), jax.ShapeDtypeStruct((tp, DN_QK), f32),
                   jax.ShapeDtypeStruct((tp, DN_V), f32), jax.ShapeDtypeStruct((HEADS, tp, LANE), f32),
                   jax.ShapeDtypeStruct((HEADS, tp, LANE), f32)] + list(r_outs),
        scratch_shapes=[pltpu.VMEM((hb, DN_DK, DN_DV), f32)] + (rider.scratch if rider else []),
        compiler_params=_params(2),
    )(q, k, v, gc, bc, grow, states, do, *r_ins)
    return res[:5], res[5:]


def _cumsum_after(x, nb, us, pieces=2):
    B, n = SB_BLOCK, x.shape[0]
    hi = x.astype(bf16)
    parts = (hi, (x - hi.astype(f32)).astype(bf16)) if pieces == 2 else (hi,)
    rows = [p[:, b * B:(b + 1) * B] for p in parts for b in range(nb)]
    r = jnp.dot(jnp.concatenate(rows, axis=0), us, preferred_element_type=f32)
    out = [r[b * n:(b + 1) * n] for b in range(nb)]
    if pieces == 2:
        out = [out[b] + r[(nb + b) * n:(nb + b + 1) * n] for b in range(nb)]
    return out[0] if nb == 1 else jnp.concatenate(out, axis=1)


def _later_blocks(x, nb, carry):
    B = SB_BLOCK
    tot = [_rowsum(x[:, b * B:(b + 1) * B]) for b in range(nb)]
    offs = [None] * nb
    run = carry
    for b in range(nb - 1, -1, -1):
        offs[b] = jnp.broadcast_to(run, (x.shape[0], B))
        run = run + tot[b]
    return (offs[0] if nb == 1 else jnp.concatenate(offs, axis=1)), run


def _sb_group(i, t):
    top = (i + 1) * (SB_QB // SB_BLOCK) - 1 - SB_GROUP * t
    jlo = jnp.maximum(top - SB_GROUP + 1, 0)
    rows = pl.ds(pl.multiple_of(jlo * SB_BLOCK, SB_BLOCK), SB_GROUP * SB_BLOCK)
    return jlo, rows, (top + 1) * SB_BLOCK


def _sb_weights(q, kcat, i, jlo, kend, cs, us, masked):
    B, nb = SB_BLOCK, SB_GROUP
    R = range(len(q))
    z = [_dot_nt(q[h], kcat[h]) * (SB_DH ** -0.5) for h in R]
    e = [jnp.exp(-jnp.abs(z[h])) for h in R]
    l1p = [jnp.log(1.0 + e[h]) for h in R]
    lsp = [jnp.minimum(z[h], 0.0) - l1p[h] for h in R]
    lk = [lsp[h] - z[h] for h in R]
    vis = None
    if masked:
        qpos = i * SB_QB + lax.broadcasted_iota(jnp.int32, (SB_QB, nb * B), 0)
        kpos = jlo * B + lax.broadcasted_iota(jnp.int32, (SB_QB, nb * B), 1)
        vis = jnp.logical_and(kpos < jnp.minimum(qpos, kend), kpos >= P0)
        lk = [jnp.where(vis, lk[h], 0.0) for h in R]
    later = [_later_blocks(lk[h], nb, cs[h]) for h in R]
    cum = [_cumsum_after(lk[h], nb, us) for h in R]
    w = [jnp.exp(lsp[h] + cum[h] + later[h][0]) for h in R]
    if masked:
        w = [jnp.where(vis, w[h], 0.0) for h in R]
    return lsp, vis, w, [later[h][1] for h in R]


def _sb_loop(i, step, carry):
    trips = ((i + 1) * (SB_QB // SB_BLOCK) - 1 + SB_GROUP) // SB_GROUP
    carry = step(True)(0, carry)
    carry = lax.fori_loop(1, trips - 1, step(False), carry)
    return lax.fori_loop(jnp.maximum(trips - 1, 1), trips, step(True), carry)


def _ride(rider, n_in, n_out, grid):
    n_rin = len(rider.ins) if rider else 0
    n_rout = len(rider.out_shapes) if rider else 0

    def split(refs):
        ins, rin = refs[:n_in], refs[n_in:n_in + n_rin]
        outs = refs[n_in + n_rin:n_in + n_rin + n_out]
        rout = refs[n_in + n_rin + n_out:n_in + n_rin + n_out + n_rout]
        return ins, outs, (rin, rout, refs[n_in + n_rin + n_out + n_rout:])

    def at(step, fn, r):
        if rider is None:
            return
        cond = None
        for a, g in enumerate(grid):
            c = pl.program_id(a) == (g - 1 if step == "last" else 0)
            cond = c if cond is None else jnp.logical_and(cond, c)

        @pl.when(cond)
        def _():
            fn(*r)

    first = lambda r: at("first", rider.start if rider else None, r)
    last = lambda r: at("last", rider.finish if rider else None, r)
    return split, first, last


def sb_fwd(qs, ks, vs, rider=None):
    tp = qs.shape[0]
    nq = tp // SB_QB
    B, G, hb, QB = SB_BLOCK, SB_GROUP, SB_FWD_HEADS_PER_STEP, SB_QB
    assert tp >= G * B and tp % QB == 0 and QB % B == 0 and G * B >= QB
    grid = (HEADS // hb, nq)
    split, ride_first, ride_last = _ride(rider, 3, 2, grid)

    def body(*refs):
        (q_ref, k_ref, v_ref), (o_ref, ob_ref), ride = split(refs)
        ride_first(ride)
        i = pl.program_id(1)
        R = range(hb)
        hs = [slice(h * SB_DH, (h + 1) * SB_DH) for h in R]
        q = [q_ref[:, hs[h]] for h in R]
        us = (lax.broadcasted_iota(jnp.int32, (B, B), 0) > lax.broadcasted_iota(jnp.int32, (B, B), 1)).astype(bf16)

        def make_step(masked):
            def step(t, carry):
                acc, cs = carry
                jlo, rows, kend = _sb_group(i, t)
                _, _, w, cs = _sb_weights(q, [k_ref[rows, hs[h]] for h in R], i, jlo, kend, cs, us, masked)
                pv = [_dot(w[h], v_ref[rows, hs[h]]) for h in R]
                return tuple(acc[h] + pv[h] for h in R), tuple(cs)
            return step

        carry = (tuple(jnp.zeros((QB, SB_DH), f32) for _ in R), tuple(jnp.zeros((QB, 1), f32) for _ in R))
        acc, _ = _sb_loop(i, make_step, carry)
        for h in R:
            o_ref[:, hs[h]] = acc[h]
            ob_ref[:, hs[h]] = acc[h].astype(bf16)
        ride_last(ride)

    blk = pl.BlockSpec((QB, hb * SB_DH), lambda g, i: (i, g))
    full = pl.BlockSpec((tp, hb * SB_DH), lambda g, i: (0, g))
    r_ins = rider.ins if rider else []
    r_outs = rider.out_shapes if rider else []
    res = pl.pallas_call(
        body, name="sb_fwd", grid=grid, in_specs=[blk, full, full] + [_ANY] * len(r_ins),
        out_specs=[blk, blk] + [_ANY] * len(r_outs),
        out_shape=[jax.ShapeDtypeStruct((tp, SB_W), f32), jax.ShapeDtypeStruct((tp, SB_W), bf16)] + list(r_outs),
        scratch_shapes=rider.scratch if rider else [], compiler_params=_params(2),
    )(qs, ks, vs, *r_ins)
    return res[0], res[1], res[2:]


def sb_bwd(qs, ks, vs, o, do, rider=None):
    tp = qs.shape[0]
    nq = tp // SB_QB
    B, G, hb, QB = SB_BLOCK, SB_GROUP, SB_HEADS_PER_STEP, SB_QB
    assert tp >= G * B and tp % QB == 0 and QB % B == 0 and G * B >= QB
    grid = (HEADS // hb, nq)
    split, ride_first, ride_last = _ride(rider, 5, 3, grid)

    def body(*refs):
        (q_ref, k_ref, v_ref, o_ref, do_ref), (dq_ref, dk_ref, dv_ref), ride = split(refs)
        ride_first(ride)
        i = pl.program_id(1)

        @pl.when(i == 0)
        def _():
            dk_ref[...] = jnp.zeros_like(dk_ref)
            dv_ref[...] = jnp.zeros_like(dv_ref)

        R = range(hb)
        hs = [slice(h * SB_DH, (h + 1) * SB_DH) for h in R]
        q = [q_ref[:, hs[h]] for h in R]
        dob = [do_ref[:, hs[h]].astype(bf16) for h in R]
        et = [_rowsum(dob[h].astype(f32) * o_ref[:, hs[h]]) for h in R]
        us = (lax.broadcasted_iota(jnp.int32, (B, B), 0) > lax.broadcasted_iota(jnp.int32, (B, B), 1)).astype(bf16)

        def make_step(masked):
            def step(t, carry):
                dq, cs, ce = carry
                jlo, rows, kend = _sb_group(i, t)
                kcat = [k_ref[rows, hs[h]] for h in R]
                dwv = [_dot_nt(dob[h], v_ref[rows, hs[h]]) for h in R]
                lsp, vis, w, cs = _sb_weights(q, kcat, i, jlo, kend, cs, us, masked)
                wb = [w[h].astype(bf16) for h in R]
                ee = [dwv[h] * wb[h].astype(f32) for h in R]
                later = [_later_blocks(ee[h], G, ce[h]) for h in R]
                cum = [_cumsum_after(ee[h], G, us) for h in R]
                dz = []
                for h in R:
                    d = ee[h] - jnp.exp(lsp[h]) * (et[h] - (cum[h] + later[h][0]))
                    if masked:
                        d = jnp.where(vis, d, 0.0)
                    dz.append(d.astype(bf16))
                dkj = [_dot_tn(dz[h], q[h]) for h in R]
                dvj = [_dot_tn(wb[h], dob[h]) for h in R]
                dqj = [_dot(dz[h], kcat[h]) for h in R]
                for h in R:
                    dk_ref[rows, hs[h]] += dkj[h]
                    dv_ref[rows, hs[h]] += dvj[h]
                return tuple(dq[h] + dqj[h] for h in R), tuple(cs), tuple(later[h][1] for h in R)
            return step

        z0 = tuple(jnp.zeros((QB, 1), f32) for _ in R)
        dq, _, _ = _sb_loop(i, make_step, (tuple(jnp.zeros((QB, SB_DH), f32) for _ in R), z0, z0))
        for h in R:
            dq_ref[:, hs[h]] = dq[h] * (SB_DH ** -0.5)
        ride_last(ride)

    blk = pl.BlockSpec((QB, hb * SB_DH), lambda g, i: (i, g))
    full = pl.BlockSpec((tp, hb * SB_DH), lambda g, i: (0, g))
    r_ins = rider.ins if rider else []
    r_outs = rider.out_shapes if rider else []
    res = pl.pallas_call(
        body, name="sb_bwd", grid=grid, in_specs=[blk, full, full, blk, blk] + [_ANY] * len(r_ins),
        out_specs=[blk, full, full] + [_ANY] * len(r_outs),
        out_shape=[jax.ShapeDtypeStruct((tp, SB_W), f32)] * 3 + list(r_outs),
        scratch_shapes=rider.scratch if rider else [], compiler_params=_params(2),
    )(qs, ks, vs, o, do, *r_ins)
    return res[:3], res[3:]


def adamw(w, g, m, v, name, rider=None):
    r, c = w.shape
    rt = _tile(r, 128, SUB) if r % SUB == 0 else r
    blk = pl.BlockSpec((rt, c), lambda i: (i, 0))
    c1 = 1.0 - ADAM_B1 ** ADAM_STEP
    c2 = 1.0 - ADAM_B2 ** ADAM_STEP
    grid = (r // rt,)
    split, ride_first, ride_last = _ride(rider, 4, 3, grid)

    def body(*refs):
        (w_ref, g_ref, m_ref, v_ref), (d_ref, mo_ref, vo_ref), ride = split(refs)
        ride_first(ride)
        g_ = g_ref[...]
        m_ = ADAM_B1 * m_ref[...] + (1.0 - ADAM_B1) * g_
        v_ = ADAM_B2 * v_ref[...] + (1.0 - ADAM_B2) * (g_ * g_)
        mo_ref[...] = m_
        vo_ref[...] = v_
        d_ref[...] = -ADAM_LR * ((m_ / c1) / (jnp.sqrt(v_ / c2) + ADAM_EPS) + ADAM_WD * w_ref[...])
        ride_last(ride)

    r_ins = rider.ins if rider else []
    r_outs = rider.out_shapes if rider else []
    res = pl.pallas_call(
        body, name=name, grid=grid, in_specs=[blk] * 4 + [_ANY] * len(r_ins), out_specs=[blk] * 3 + [_ANY] * len(r_outs),
        out_shape=[jax.ShapeDtypeStruct((r, c), f32)] * 3 + list(r_outs),
        scratch_shapes=rider.scratch if rider else [], compiler_params=_params(1),
    )(w, g, m, v, *r_ins)
    return res[:3], res[3:]


def sum_slots(x, name):
    n, r, c = x.shape
    rt = _tile(r, 128, SUB) if r % SUB == 0 else r
    blk = pl.BlockSpec((n, rt, c), lambda i: (0, i, 0))

    def body(x_ref, o_ref):
        acc = x_ref[0].astype(f32)
        for s in range(1, n):
            acc = acc + x_ref[s].astype(f32)
        o_ref[...] = acc

    return pl.pallas_call(
        body, name=name, grid=(r // rt,), in_specs=[blk], out_specs=pl.BlockSpec((rt, c), lambda i: (i, 0)),
        out_shape=jax.ShapeDtypeStruct((r, c), f32), compiler_params=_params(1),
    )(x)


def add2(a, b, name, out_dtype=f32):
    n, r, c = a.shape
    rt = _tile(r, 64, SUB) if r % SUB == 0 else r
    blk = pl.BlockSpec((n, rt, c), lambda i: (0, i, 0))

    def body(a_ref, b_ref, o_ref):
        o_ref[...] = (a_ref[...] + b_ref[...]).astype(out_dtype)

    return pl.pallas_call(
        body, name=name, grid=(r // rt,), in_specs=[blk, blk], out_specs=blk,
        out_shape=jax.ShapeDtypeStruct((n, r, c), out_dtype), compiler_params=_params(1),
    )(a, b)


_ANY = pl.BlockSpec(memory_space=pl.ANY)
_MESH = pl.DeviceIdType.MESH


def _coords():
    return lax.axis_index("x"), lax.axis_index("y"), lax.axis_index("c")


def _chip_peer(x, y, r):
    return x ^ (r >> 1), y ^ (r & 1)


class _Exchange:
    def __init__(self, ins, out_shapes, scratch, start, finish):
        self.ins, self.out_shapes, self.scratch, self.start, self.finish = ins, out_shapes, scratch, start, finish

    def split(self, refs):
        n, m = len(self.ins), len(self.out_shapes)
        return refs[:n], refs[n:n + m], refs[n + m:]


def run_exchange(ex, name):
    def body(*refs):
        ins, outs, sems = ex.split(refs)
        ex.start(ins, outs, sems)
        ex.finish(ins, outs, sems)

    return pl.pallas_call(body, name=name, in_specs=[_ANY] * len(ex.ins), out_specs=[_ANY] * len(ex.out_shapes),
                          out_shape=ex.out_shapes, scratch_shapes=ex.scratch)(*ex.ins)


def gather_chips(big, small):
    nb, n = len(big), len(big) + len(small)
    shards = list(big) + list(small)
    kb = nb * (N_CHIPS - 1)
    k = n * (N_CHIPS - 1)

    def copies(src, dst, sems):
        send, recv, fsend, frecv = sems
        x, y, c = _coords()
        sib = (x, y, 1 - c)
        peers = [_chip_peer(x, y, r) for r in range(1, N_CHIPS)]

        def direct(t, j, slot):
            s = t * (N_CHIPS - 1) + j
            if t < nb:
                return pltpu.make_async_remote_copy(src[t].at[c], dst[t].at[slot, c], send.at[s], recv.at[s],
                                                    device_id=(*peers[j], c), device_id_type=_MESH)
            return pltpu.make_async_remote_copy(src[t], dst[t].at[slot], send.at[s], recv.at[s],
                                                device_id=(*peers[j], c), device_id_type=_MESH)

        def passed(t, j, half):
            s = t * (N_CHIPS - 1) + j
            px, py = peers[j]
            part = dst[t].at[2 * px + py, half]
            return pltpu.make_async_remote_copy(part, part, fsend.at[s], frecv.at[s], device_id=sib, device_id_type=_MESH)

        return direct, passed, peers, 2 * x + y, c

    def start(src, dst, sems):
        direct, _, _, me, _ = copies(src, dst, sems)
        for t in range(n):
            for j in range(N_CHIPS - 1):
                direct(t, j, me).start()

    def finish(src, dst, sems):
        direct, passed, peers, me, c = copies(src, dst, sems)
        fwd = []
        for t in range(nb):
            for j in range(N_CHIPS - 1):
                px, py = peers[j]
                direct(t, j, 2 * px + py).wait_recv()
                fwd.append(passed(t, j, c))
                fwd[-1].start()
        for t in range(nb, n):
            for j in range(N_CHIPS - 1):
                px, py = peers[j]
                direct(t, j, 2 * px + py).wait_recv()
        for t in range(nb):
            for j in range(N_CHIPS - 1):
                passed(t, j, 1 - c).wait_recv()
        for t in range(n):
            for j in range(N_CHIPS - 1):
                direct(t, j, me).wait_send()
        for cp in fwd:
            cp.wait_send()

    return _Exchange(shards, [jax.ShapeDtypeStruct((N_CHIPS,) + s.shape, s.dtype) for s in shards],
                     [pltpu.SemaphoreType.DMA((k,)), pltpu.SemaphoreType.DMA((k,)),
                      pltpu.SemaphoreType.DMA((max(kb, 1),)), pltpu.SemaphoreType.DMA((max(kb, 1),))], start, finish)


def sibling_swap(grads):
    pairs = [(t, o) for t, g in enumerate(grads) for o in range(g.shape[0])]
    k = len(pairs)

    def copies(src, dst, sems):
        send, recv = sems
        x, y, c = _coords()
        return [pltpu.make_async_remote_copy(src[t].at[o, 1 - c], dst[t].at[o], send.at[s], recv.at[s],
                                             device_id=(x, y, 1 - c), device_id_type=_MESH)
                for s, (t, o) in enumerate(pairs)]

    def start(src, dst, sems):
        for cp in copies(src, dst, sems):
            cp.start()

    def finish(src, dst, sems):
        cps = copies(src, dst, sems)
        for cp in cps:
            cp.wait_recv()
        for cp in cps:
            cp.wait_send()

    return _Exchange(list(grads), [jax.ShapeDtypeStruct((g.shape[0],) + g.shape[2:], g.dtype) for g in grads],
                     [pltpu.SemaphoreType.DMA((k,)), pltpu.SemaphoreType.DMA((k,))], start, finish)


def scatter_chips(parts):
    n = len(parts)
    k = n * (N_CHIPS - 1)

    def copy(src, dst, sems, t, r, landing):
        send, recv = sems
        x, y, c = _coords()
        me = 2 * x + y
        px, py = _chip_peer(x, y, r)
        peer = 2 * px + py
        s = t * (N_CHIPS - 1) + r - 1
        return pltpu.make_async_remote_copy(src[t].at[me if landing else peer], dst[t].at[peer if landing else me],
                                            send.at[s], recv.at[s], device_id=(px, py, c), device_id_type=_MESH)

    def start(src, dst, sems):
        for t in range(n):
            for r in range(1, N_CHIPS):
                copy(src, dst, sems, t, r, False).start()

    def finish(src, dst, sems):
        for t in range(n):
            for r in range(1, N_CHIPS):
                copy(src, dst, sems, t, r, True).wait_recv()
        for t in range(n):
            for r in range(1, N_CHIPS):
                copy(src, dst, sems, t, r, False).wait_send()

    return _Exchange(list(parts), [jax.ShapeDtypeStruct(p.shape, p.dtype) for p in parts],
                     [pltpu.SemaphoreType.DMA((k,)), pltpu.SemaphoreType.DMA((k,))], start, finish)


def sibling_send(halves):
    n = len(halves)

    def copies(src, dst, sems):
        send, recv = sems
        x, y, c = _coords()
        return [pltpu.make_async_remote_copy(src[t], dst[t], send.at[t], recv.at[t],
                                             device_id=(x, y, 1 - c), device_id_type=_MESH) for t in range(n)]

    def start(src, dst, sems):
        for cp in copies(src, dst, sems):
            cp.start()

    def finish(src, dst, sems):
        cps = copies(src, dst, sems)
        for cp in cps:
            cp.wait_recv()
        for cp in cps:
            cp.wait_send()

    return _Exchange(list(halves), [jax.ShapeDtypeStruct(h.shape, h.dtype) for h in halves],
                     [pltpu.SemaphoreType.DMA((n,)), pltpu.SemaphoreType.DMA((n,))], start, finish)


def gather_all(block):
    def copies(src, dst, sems, landing):
        send, recv, loc = sems
        x, y, c = _coords()
        me = 4 * x + 2 * y + c
        mine = pltpu.make_async_copy(src[0], dst[0].at[me], loc)
        remote = []
        for r in range(1, N_DEV):
            px, py, pc = x ^ (r >> 2), y ^ ((r >> 1) & 1), c ^ (r & 1)
            slot = 4 * px + 2 * py + pc if landing else me
            remote.append(pltpu.make_async_remote_copy(src[0], dst[0].at[slot], send.at[r - 1], recv.at[r - 1],
                                                       device_id=(px, py, pc), device_id_type=_MESH))
        return mine, remote

    def start(src, dst, sems):
        mine, outs = copies(src, dst, sems, False)
        mine.start()
        for cp in outs:
            cp.start()

    def finish(src, dst, sems):
        mine, lands = copies(src, dst, sems, True)
        for cp in lands:
            cp.wait_recv()
        for cp in lands:
            cp.wait_send()
        mine.wait()

    return _Exchange([block], [jax.ShapeDtypeStruct((N_DEV,) + block.shape, block.dtype)],
                     [pltpu.SemaphoreType.DMA((N_DEV - 1,)), pltpu.SemaphoreType.DMA((N_DEV - 1,)),
                      pltpu.SemaphoreType.DMA(())], start, finish)


def _pad_lanes(v, n=LANE):
    return jnp.pad(v, ((0, 0), (0, n - v.shape[1])))


def _w_in_pieces():
    cs = (PROJ_BIG + 2 * HEADS) // N_CHIPS
    ab_end = AB_COL + 2 * HEADS
    out = []
    for o in range(N_CHIPS):
        lo, hi = o * cs, (o + 1) * cs
        cand = [("big", lo, min(hi, AB_COL), 0), ("ab", max(lo, AB_COL), min(hi, ab_end), AB_COL),
                ("big", max(lo, ab_end), hi, 2 * HEADS)]
        out.append([(s, a - off, b - off) for s, a, b, off in cand if a < b])
    return out


def _split_w_in(w4):
    big, ab = [], []
    for o, pieces in enumerate(_w_in_pieces()):
        at = 0
        for s, a, b in pieces:
            (big if s == "big" else ab).append(w4[o][:, at:at + b - a])
            at += b - a
    return jnp.concatenate(big, axis=1), _pad_lanes(jnp.concatenate(ab, axis=1))


def _join_w_in(big, ab):
    src = {"big": big, "ab": ab}
    return jnp.stack([jnp.concatenate([src[s][:, a:b] for s, a, b in pieces], axis=1) for pieces in _w_in_pieces()])


def _conv_w8(w):
    return jnp.pad(w, ((0, SUB - DN_CONV), (0, 0)))


def _row_layout(gc, tp):
    nc = tp // CHUNK
    g = gc[:, :, 0].reshape(HEADS, nc, 1, CHUNK)
    g = jnp.broadcast_to(g, (HEADS, nc, SUB, CHUNK))
    return jnp.pad(g, ((0, 0), (0, 0), (0, 0), (0, LANE - CHUNK))).reshape(HEADS, nc * SUB, LANE)


def _step(x, meta, W, target, late_weights=None, early_swap=None, early_grads=None, last_grads=None, final_send=None):
    W = dict(W)
    seq = x.shape[0]
    tp = P0 + N_META + seq
    h0 = jnp.concatenate([jnp.zeros((P0, D_MODEL), f32), meta, x], axis=0)
    w_big, w_ab = _split_w_in(W["w_in"])
    cq8, ck8, cv8 = _conv_w8(W["conv_q"]), _conv_w8(W["conv_k"]), _conv_w8(W["conv_v"])
    al, dtb = _pad_lanes(W["dn_a_log"]), _pad_lanes(W["dn_dt_bias"])

    n1 = rms_fwd(h0, W["norm_mix_gain"], "rms1_fwd")
    proj = matmul(n1, w_big, "nn", "proj_fwd")
    pab = matmul(n1, w_ab, "nn", "pab_fwd")
    qn = conv_fwd(proj, cq8, C_DQ * 8, 8, True, "conv_q_fwd")
    kn = conv_fwd(proj, ck8, C_DK * 8, 8, True, "conv_k_fwd")
    va = conv_fwd(proj, cv8, C_DV * 8, 16, False, "conv_v_fwd")
    gc, bc = ab_fwd(pab, al, dtb)
    grow = _row_layout(gc, tp)
    o_dn, states = gdn_fwd(qn, kn, va, gc, bc, grow)
    on = dn_out_fwd(o_dn, proj, W["dn_out_norm_gain"])
    qs, ks, vs = sb_prep_fwd(proj, W["sb_q_norm_gain"], W["sb_k_norm_gain"])
    o_sb, o_sb16, arrived = sb_fwd(qs, ks, vs, rider=late_weights[0] if late_weights else None)
    if late_weights:
        W.update(late_weights[1](arrived))
    ydn = matmul(on, W["w_branch_dn"], "nn", "ydn_fwd")
    ysb = matmul(o_sb16, W["w_branch_sb"], "nn", "ysb_fwd")
    merged = merge_fwd(proj, ydn, ysb)
    h1 = matmul(merged, W["w_out"], "nn", "wout_fwd", residual=h0)
    n2 = rms_fwd(h1, W["norm_ffn_gain"], "rms2_fwd")
    u = matmul(n2, W["w_ffn_in"], "nn", "ffn_in_fwd", tn_t=512)
    act = swiglu_fwd(u)
    y = matmul(act, W["w_ffn_out"], "nn", "ffn_out_fwd", residual=h1)
    dy, dy16, loss = loss_head(y, target)

    G = {}
    dact = matmul(dy16, W["w_ffn_out"], "nt", "ffn_out_dx", tn_t=1408)
    G["w_ffn_out"] = matmul(act, dy16, "tn", "ffn_out_dw", tm_t=1408)
    dgate, dup = swiglu_bwd(u, dact)
    du = jnp.concatenate([dgate, dup], axis=1)
    dn2 = matmul(du, W["w_ffn_in"], "nt", "ffn_in_dx", tk_t=512)
    G["w_ffn_in"] = matmul(n2, du, "tn", "ffn_in_dw", tn_t=512)
    (dh1, dh1_16, G["norm_ffn_gain"]), _ = rms_bwd(h1, W["norm_ffn_gain"], dn2, dy, "rms2_bwd")
    dmerged = matmul(dh1_16, W["w_out"], "nt", "wout_dx")
    G["w_out"] = matmul(merged, dh1_16, "tn", "wout_dw")
    dyd, dys, d_gates = merge_bwd(proj, ydn, ysb, dmerged)
    don = matmul(dyd, W["w_branch_dn"], "nt", "ydn_dx")
    G["w_branch_dn"] = matmul(on, dyd, "tn", "ydn_dw")
    do_sb = matmul(dys, W["w_branch_sb"], "nt", "ysb_dx")
    G["w_branch_sb"] = matmul(o_sb16, dys, "tn", "ysb_dw")
    do_dn, dproj, G["dn_out_norm_gain"] = dn_out_bwd(o_dn, proj, W["dn_out_norm_gain"], don)
    (dqn, dkn, dva, dgc, dbc), swapped = gdn_bwd(qn, kn, va, gc, bc, grow, states, do_dn,
                                                 rider=early_swap[0](G) if early_swap else None)
    if early_swap:
        early_swap[1](swapped)
    dpab, dal, ddt = ab_bwd(pab, al, dtb, dgc, dbc)
    G["dn_a_log"], G["dn_dt_bias"] = dal[:, :HEADS], ddt[:, :HEADS]
    dproj, dcq = conv_bwd(proj, cq8, dqn, dproj, C_DQ * 8, 8, True, "conv_q_bwd")
    dproj, dck = conv_bwd(proj, ck8, dkn, dproj, C_DK * 8, 8, True, "conv_k_bwd")
    dproj, dcv = conv_bwd(proj, cv8, dva, dproj, C_DV * 8, 16, False, "conv_v_bwd")
    G["conv_q"], G["conv_k"], G["conv_v"] = dcq[:DN_CONV], dck[:DN_CONV], dcv[:DN_CONV]
    (dqs, dks, dvs), delivered = sb_bwd(qs, ks, vs, o_sb, do_sb, rider=early_grads[0](G) if early_grads else None)
    if early_grads:
        early_grads[1](delivered)
    dproj, G["sb_q_norm_gain"], G["sb_k_norm_gain"] = sb_prep_bwd(
        proj, W["sb_q_norm_gain"], W["sb_k_norm_gain"], dqs, dks, dvs, dproj)
    dproj = lax.dynamic_update_slice(dproj, d_gates, (0, C_GDN * 1024))
    dw_big = matmul(n1, dproj, "tn", "proj_dw")
    dw_ab = matmul(n1, dpab, "tn", "pab_dw")
    G["w_in"] = (dw_big, dw_ab)
    if last_grads:
        dn1, delivered = matmul(dproj, w_big, "nt", "proj_dx", tk_t=1024, rider=last_grads[0](G))
        last_grads[1](delivered)
    else:
        dn1 = matmul(dproj, w_big, "nt", "proj_dx", tk_t=1024)
    dn1 = matmul(dpab, w_ab, "nt", "pab_dx", residual=dn1)
    (dh0, _, G["norm_mix_gain"]), sent = rms_bwd(h0, W["norm_mix_gain"], dn1, dh1, "rms1_bwd",
                                                 rider=final_send[0]() if final_send else None)
    if final_send:
        final_send[1](sent)
    G["meta_tokens"] = dh0[P0:P0 + N_META]
    return loss, dh0[P0 + N_META:], G


_BIG = ("w_in", "w_branch_dn", "w_branch_sb", "w_out", "w_ffn_in", "w_ffn_out")
_COL_SHARDED = ("w_in", "w_ffn_in", "meta_tokens", "conv_q", "conv_k", "conv_v")
_SMALL_REPL = ("norm_mix_gain", "norm_ffn_gain", "dn_a_log", "dn_dt_bias", "dn_out_norm_gain", "sb_q_norm_gain",
               "sb_k_norm_gain")
_SMALL_SHARD = ("meta_tokens", "conv_q", "conv_k", "conv_v")
_ORDER = ("meta_tokens", "norm_mix_gain", "w_in", "conv_q", "conv_k", "conv_v", "dn_a_log", "dn_dt_bias",
          "dn_out_norm_gain", "sb_q_norm_gain", "sb_k_norm_gain", "w_branch_dn", "w_branch_sb", "w_out",
          "norm_ffn_gain", "w_ffn_in", "w_ffn_out")


def _unshard(g4, name):
    if name in _COL_SHARDED:
        r, cs = g4.shape[1:]
        return jnp.transpose(g4, (1, 0, 2)).reshape(r, N_CHIPS * cs)
    return g4.reshape((-1,) + g4.shape[2:])


def _to_shards(full, name):
    if name in _COL_SHARDED:
        r, c = full.shape
        return jnp.transpose(full.reshape(r, N_CHIPS, c // N_CHIPS), (1, 0, 2))
    r, c = full.shape
    return full.reshape(N_CHIPS, r // N_CHIPS, c)


def _rows_1024(a):
    r, c = a.shape
    if c >= 1024:
        return a.reshape(r * (c // 1024), 1024)
    return jnp.pad(a, ((0, 0), (0, 1024 - c)))


def kernel(x, meta_tokens, norm_mix_gain, w_in, conv_q, conv_k, conv_v, dn_a_log, dn_dt_bias, dn_out_norm_gain, sb_q_norm_gain, sb_k_norm_gain, w_branch_dn, w_branch_sb, w_out, norm_ffn_gain, w_ffn_in, w_ffn_out, loss_target, m_meta_tokens, m_norm_mix_gain, m_w_in, m_conv_q, m_conv_k, m_conv_v, m_dn_a_log, m_dn_dt_bias, m_dn_out_norm_gain, m_sb_q_norm_gain, m_sb_k_norm_gain, m_w_branch_dn, m_w_branch_sb, m_w_out, m_norm_ffn_gain, m_w_ffn_in, m_w_ffn_out, v_meta_tokens, v_norm_mix_gain, v_w_in, v_conv_q, v_conv_k, v_conv_v, v_dn_a_log, v_dn_dt_bias, v_dn_out_norm_gain, v_sb_q_norm_gain, v_sb_k_norm_gain, v_w_branch_dn, v_w_branch_sb, v_w_out, v_norm_ffn_gain, v_w_ffn_in, v_w_ffn_out):
    Wl = dict(meta_tokens=meta_tokens, norm_mix_gain=norm_mix_gain, w_in=w_in[0], conv_q=conv_q[0], conv_k=conv_k[0],
              conv_v=conv_v[0], dn_a_log=dn_a_log, dn_dt_bias=dn_dt_bias, dn_out_norm_gain=dn_out_norm_gain,
              sb_q_norm_gain=sb_q_norm_gain, sb_k_norm_gain=sb_k_norm_gain, w_branch_dn=w_branch_dn[0],
              w_branch_sb=w_branch_sb[0], w_out=w_out[0], norm_ffn_gain=norm_ffn_gain, w_ffn_in=w_ffn_in[0],
              w_ffn_out=w_ffn_out[0])
    Ml = dict(meta_tokens=m_meta_tokens, norm_mix_gain=m_norm_mix_gain, w_in=m_w_in[0], conv_q=m_conv_q[0],
              conv_k=m_conv_k[0], conv_v=m_conv_v[0], dn_a_log=m_dn_a_log, dn_dt_bias=m_dn_dt_bias,
              dn_out_norm_gain=m_dn_out_norm_gain, sb_q_norm_gain=m_sb_q_norm_gain, sb_k_norm_gain=m_sb_k_norm_gain,
              w_branch_dn=m_w_branch_dn[0], w_branch_sb=m_w_branch_sb[0], w_out=m_w_out[0],
              norm_ffn_gain=m_norm_ffn_gain, w_ffn_in=m_w_ffn_in[0], w_ffn_out=m_w_ffn_out[0])
    Vl = dict(meta_tokens=v_meta_tokens, norm_mix_gain=v_norm_mix_gain, w_in=v_w_in[0], conv_q=v_conv_q[0],
              conv_k=v_conv_k[0], conv_v=v_conv_v[0], dn_a_log=v_dn_a_log, dn_dt_bias=v_dn_dt_bias,
              dn_out_norm_gain=v_dn_out_norm_gain, sb_q_norm_gain=v_sb_q_norm_gain, sb_k_norm_gain=v_sb_k_norm_gain,
              w_branch_dn=v_w_branch_dn[0], w_branch_sb=v_w_branch_sb[0], w_out=v_w_out[0],
              norm_ffn_gain=v_norm_ffn_gain, w_ffn_in=v_w_ffn_in[0], w_ffn_out=v_w_ffn_out[0])
    lead = {n: (1,) if (n in _BIG or n in ("conv_q", "conv_k", "conv_v")) else () for n in _ORDER}

    chip = 2 * lax.axis_index("x") + lax.axis_index("y")
    c = lax.axis_index("c")
    halved = {n: Wl[n].astype(bf16).reshape(2, Wl[n].shape[0] // 2, Wl[n].shape[1]) for n in _BIG}

    def gathered_weights(names, owns, outs):
        res = {}
        for n, own, g4 in zip(names, owns, outs):
            g4 = lax.dynamic_update_slice(g4, own[None], (chip,) + (0,) * own.ndim)
            if n in _BIG:
                g4 = g4.reshape(N_CHIPS, 2 * g4.shape[2], g4.shape[3])
            res[n] = g4 if n == "w_in" else _unshard(g4, n)
        return res

    first = ["w_in"] + list(_SMALL_SHARD)
    first_own = [halved["w_in"]] + [Wl[n] for n in _SMALL_SHARD]
    W = dict(Wl)
    W.update(gathered_weights(first, first_own, run_exchange(gather_chips(first_own[:1], first_own[1:]), "gather_w_in")))
    late = [n for n in _BIG if n != "w_in"]
    late_own = [halved[n] for n in late]
    for n in late:
        del W[n]

    def halves_of(names, G):
        g4 = [_to_shards(G[n], n) for n in names]
        return [g.reshape(N_CHIPS, 2, g.shape[1] // 2, g.shape[2]) for g in g4]

    def pair_added(g42, from_sib, tag, wire):
        mine = [lax.dynamic_index_in_dim(g, c, axis=1, keepdims=False) for g in g42]
        return [add2(a, b, "grad_pair_add_%s%d" % (tag, t), out_dtype=wire)
                for t, (a, b) in enumerate(zip(mine, from_sib))]

    def chip_reduced(parts, slots, tag):
        slots = [lax.dynamic_update_slice(s, lax.dynamic_index_in_dim(p, chip, axis=0, keepdims=True), (chip, 0, 0))
                 for s, p in zip(slots, parts)]
        return [sum_slots(s, "grad_chip_sum_%s%d" % (tag, t)) for t, s in enumerate(slots)]

    early, last = {}, {}

    def early_swap_begin(G):
        early["g42"] = halves_of(late, G)
        return sibling_swap(early["g42"])

    def early_begin(G):
        early["parts"] = pair_added(early["g42"], early["from_sib"], "a", f32)
        return scatter_chips(early["parts"])

    def last_begin(G):
        g2 = [g.reshape(1, 2, g.shape[0] // 2, g.shape[1]) for g in G["w_in"]]
        added = pair_added(g2, run_exchange(sibling_swap(g2), "grad_sibling_swap_b"), "b", bf16)
        last["parts"] = [_join_w_in(added[0][0], added[1][0])]
        return scatter_chips(last["parts"])

    loss, grad_x, G = _step(
        x[0], W["meta_tokens"], W, loss_target[0],
        late_weights=(gather_chips(late_own, []), lambda outs: gathered_weights(late, late_own, outs)),
        early_swap=(early_swap_begin, lambda outs: early.update(from_sib=outs)),
        early_grads=(early_begin, lambda slots: early.update(halves=chip_reduced(early["parts"], slots, "a"))),
        last_grads=(last_begin, lambda slots: last.update(halves=chip_reduced(last["parts"], slots, "b"))),
        final_send=(lambda: sibling_send(last["halves"] + early["halves"]), lambda outs: last.update(theirs=outs)))
    Gs = {}
    for n, h, o in zip(["w_in"] + late, last["halves"] + early["halves"], last["theirs"]):
        Gs[n] = lax.dynamic_update_slice(jnp.concatenate([o, o], axis=0), h, (c * h.shape[0], 0))

    small_names = list(_SMALL_REPL) + list(_SMALL_SHARD)
    pieces = [_rows_1024(G[n]) for n in small_names] + [_rows_1024(loss)]
    counts = [p.shape[0] for p in pieces]
    pack = jnp.concatenate(pieces, axis=0)
    pad_rows = (-pack.shape[0]) % SUB
    pack = jnp.pad(pack, ((0, pad_rows), (0, 0)))
    adam = {"w_in": adamw(Wl["w_in"], Gs["w_in"], Ml["w_in"], Vl["w_in"], "adamw_w_in", rider=gather_all(pack))}
    total = sum_slots(adam["w_in"][1][0], "small_sum")
    row = 0
    for n, cnt in zip(small_names, counts[:-1]):
        blk = total[row:row + cnt]
        row += cnt
        full_shape = G[n].shape
        if full_shape[1] >= 1024:
            blk = blk.reshape(full_shape)
        else:
            blk = blk[:, :full_shape[1]]
        if n in _SMALL_SHARD:
            cs = full_shape[1] // N_CHIPS
            blk = lax.dynamic_slice_in_dim(blk, chip * cs, cs, axis=1)
        Gs[n] = blk
    loss_out = total[row, 0]

    grads, deltas, new_m, new_v = [], [], [], []
    for n in _ORDER:
        d, m2, v2 = (adam[n] if n in adam else adamw(Wl[n], Gs[n], Ml[n], Vl[n], "adamw_" + n))[0]
        shape = lead[n] + Wl[n].shape
        grads.append(Gs[n].reshape(shape))
        deltas.append(d.reshape(shape))
        new_m.append(m2.reshape(shape))
        new_v.append(v2.reshape(shape))
    return (loss_out, grad_x[None], *grads, *deltas, *new_m, *new_v)
```

```python
import jax
import jax.numpy as jnp
from jax import lax
from jax.experimental import pallas as pl
from jax.experimental.pallas import tpu as pltpu

f32 = jnp.float32
bf16 = jnp.bfloat16

D_MODEL = 1024
N_META = 16
CHUNK = 64
HEADS = 8
DN_DK = 128
DN_DV = 256
DN_CONV = 4
DN_QK = HEADS * DN_DK
DN_V = HEADS * DN_DV
SB_DH = 128
SB_W = HEADS * SB_DH
SB_BLOCK = 128
SB_QB = 384
SB_GROUP = 4
SB_HEADS_PER_STEP = 2
SB_FWD_HEADS_PER_STEP = 4
GDN_HEADS_PER_STEP = 8
CONV_W = 2048
D_FF = 2816
RMS_EPS = 1e-6
L2_EPS = 1e-6
ADAM_LR = 0.001
ADAM_B1 = 0.9
ADAM_B2 = 0.999
ADAM_EPS = 1e-08
ADAM_WD = 0.01
ADAM_STEP = 10

P0 = 112
LANE = 128
SUB = 8
VMEM_LIMIT = 48 * 1024 * 1024
N_CHIPS = 4
N_DEV = 8

C_DQ, C_DK, C_DV, C_DZ, C_SQ, C_SK, C_SV, C_GDN, C_GSB = 0, 1, 2, 4, 6, 7, 8, 9, 10
PROJ_BIG = 11 * 1024
AB_COL = 2 * DN_QK + 2 * DN_V


def _params(n_axes):
    return pltpu.CompilerParams(dimension_semantics=("arbitrary",) * n_axes, vmem_limit_bytes=VMEM_LIMIT)


def _tile(n, target, q=LANE):
    best = None
    for t in range(q, min(n, target) + 1, q):
        if n % t == 0:
            best = t
    return best if best is not None else n


def _dot(a, b):
    return jnp.dot(a.astype(bf16), b.astype(bf16), preferred_element_type=f32)


def _dot_nt(a, b):
    return lax.dot_general(a.astype(bf16), b.astype(bf16), (((1,), (1,)), ((), ())), preferred_element_type=f32)


def _dot_tn(a, b):
    return lax.dot_general(a.astype(bf16), b.astype(bf16), (((0,), (0,)), ((), ())), preferred_element_type=f32)


_HI = lax.Precision.HIGH


def _hdot(a, b):
    return jnp.dot(a, b, precision=_HI, preferred_element_type=f32)


def _hdot_nt(a, b):
    return lax.dot_general(a, b, (((1,), (1,)), ((), ())), precision=_HI, preferred_element_type=f32)


def _hdot_tn(a, b):
    return lax.dot_general(a, b, (((0,), (0,)), ((), ())), precision=_HI, preferred_element_type=f32)


def _sigmoid(x):
    return 0.5 * jnp.tanh(0.5 * x) + 0.5


def _log1p_small(e):
    return jnp.where(e < 1e-3, e * (1.0 - e * (0.5 - e * (1.0 / 3.0))), jnp.log(1.0 + e))


def _rowsum(x):
    return jnp.sum(x, axis=1, keepdims=True)


def _allsum(x):
    return jnp.sum(jnp.sum(x, axis=1, keepdims=True), axis=0, keepdims=True)


def matmul(a, b, mode, name, residual=None, out_dtype=f32, tm_t=1408, tn_t=1024, tk_t=1408, rider=None):
    if mode == "nn":
        (M, K), (K2, N) = a.shape, b.shape
    elif mode == "nt":
        (M, K), (N, K2) = a.shape, b.shape
    else:
        (K, M), (K2, N) = a.shape, b.shape
    assert K == K2, (a.shape, b.shape, mode)
    tm, tn, tk = _tile(M, tm_t), _tile(N, tn_t), _tile(K, tk_t)
    nk = K // tk
    if mode == "nn":
        a_spec = pl.BlockSpec((tm, tk), lambda i, j, k: (i, k))
        b_spec = pl.BlockSpec((tk, tn), lambda i, j, k: (k, j))
        dims = (((1,), (0,)), ((), ()))
    elif mode == "nt":
        a_spec = pl.BlockSpec((tm, tk), lambda i, j, k: (i, k))
        b_spec = pl.BlockSpec((tn, tk), lambda i, j, k: (j, k))
        dims = (((1,), (1,)), ((), ()))
    else:
        a_spec = pl.BlockSpec((tk, tm), lambda i, j, k: (k, i))
        b_spec = pl.BlockSpec((tk, tn), lambda i, j, k: (k, j))
        dims = (((0,), (0,)), ((), ()))
    o_spec = pl.BlockSpec((tm, tn), lambda i, j, k: (i, j))
    has_res = residual is not None
    grid = (M // tm, N // tn, nk)
    split, ride_first, ride_last = _ride(rider, 3 if has_res else 2, 1, grid)

    def body(*refs):
        ins_, (o_ref,), (rin, rout, rest) = split(refs)
        a_ref, b_ref = ins_[:2]
        r_ref = ins_[2] if has_res else None
        acc_ref, ride = rest[0], (rin, rout, rest[1:])
        ride_first(ride)
        k = pl.program_id(2)

        @pl.when(k == 0)
        def _():
            acc_ref[...] = jnp.zeros_like(acc_ref)

        acc_ref[...] += lax.dot_general(a_ref[...].astype(bf16), b_ref[...].astype(bf16), dims,
                                        preferred_element_type=f32)

        @pl.when(k == nk - 1)
        def _():
            r = acc_ref[...]
            if has_res:
                r = r + r_ref[...]
            o_ref[...] = r.astype(out_dtype)

        ride_last(ride)

    ins = [a, b] + ([residual] if has_res else [])
    specs = [a_spec, b_spec] + ([o_spec] if has_res else [])
    r_ins = rider.ins if rider else []
    r_outs = rider.out_shapes if rider else []
    res = pl.pallas_call(
        body, name=name, grid=grid, in_specs=specs + [_ANY] * len(r_ins), out_specs=[o_spec] + [_ANY] * len(r_outs),
        out_shape=[jax.ShapeDtypeStruct((M, N), out_dtype)] + list(r_outs),
        scratch_shapes=[pltpu.VMEM((tm, tn), f32)] + (rider.scratch if rider else []), compiler_params=_params(3),
    )(*ins, *r_ins)
    return (res[0], res[1:]) if rider else res[0]


def _row_tile(tp):
    return _tile(tp, 512)


def rms_fwd(h, gain, name):
    tp, d = h.shape
    rt = _row_tile(tp)

    def body(h_ref, g_ref, o_ref):
        x = h_ref[...]
        r = lax.rsqrt(jnp.mean(x * x, axis=-1, keepdims=True) + RMS_EPS)
        o_ref[...] = (x * r * g_ref[...]).astype(bf16)

    return pl.pallas_call(
        body, name=name, grid=(tp // rt,),
        in_specs=[pl.BlockSpec((rt, d), lambda i: (i, 0)), pl.BlockSpec((1, d), lambda i: (0, 0))],
        out_specs=pl.BlockSpec((rt, d), lambda i: (i, 0)),
        out_shape=jax.ShapeDtypeStruct((tp, d), bf16), compiler_params=_params(1),
    )(h, gain)


def rms_bwd(h, gain, dn, dres, name, rider=None):
    tp, d = h.shape
    rt = _row_tile(tp)
    grid = (tp // rt,)
    split, ride_first, ride_last = _ride(rider, 4, 3, grid)

    def body(*refs):
        (h_ref, g_ref, dn_ref, dr_ref), (dh_ref, dhb_ref, dg_ref), ride = split(refs)
        ride_first(ride)
        i = pl.program_id(0)
        x = h_ref[...]
        r = lax.rsqrt(jnp.mean(x * x, axis=-1, keepdims=True) + RMS_EPS)
        xh = x * r
        dn_ = dn_ref[...]
        dxh = dn_ * g_ref[...]
        dh = r * (dxh - xh * jnp.mean(dxh * xh, axis=-1, keepdims=True)) + dr_ref[...]
        dh_ref[...] = dh
        dhb_ref[...] = dh.astype(bf16)
        part = jnp.sum(dn_ * xh, axis=0, keepdims=True)

        @pl.when(i == 0)
        def _():
            dg_ref[...] = part

        @pl.when(i > 0)
        def _():
            dg_ref[...] += part

        ride_last(ride)

    row = pl.BlockSpec((rt, d), lambda i: (i, 0))
    vec = pl.BlockSpec((1, d), lambda i: (0, 0))
    r_ins = rider.ins if rider else []
    r_outs = rider.out_shapes if rider else []
    res = pl.pallas_call(
        body, name=name, grid=grid, in_specs=[row, vec, row, row] + [_ANY] * len(r_ins),
        out_specs=[row, row, vec] + [_ANY] * len(r_outs),
        out_shape=[jax.ShapeDtypeStruct((tp, d), f32), jax.ShapeDtypeStruct((tp, d), bf16),
                   jax.ShapeDtypeStruct((1, d), f32)] + list(r_outs),
        scratch_shapes=rider.scratch if rider else [], compiler_params=_params(1),
    )(h, gain, dn, dres, *r_ins)
    return res[:3], res[3:]


def loss_head(y, target):
    tp, d = y.shape
    lead = P0 + N_META
    rt = _row_tile(tp)
    ns = rt // lead
    assert lead == SB_BLOCK and rt % lead == 0 and target.shape == (tp - lead, d)
    last = target.shape[0] // lead - 1

    def body(*refs):
        y_ref, t_refs = refs[0], refs[1:1 + ns]
        dy_ref, dyb_ref, l_ref = refs[1 + ns:]
        i = pl.program_id(0)

        @pl.when(i == 0)
        def _():
            l_ref[...] = jnp.zeros_like(l_ref)

        part = jnp.zeros((1, 1), f32)
        for s in range(ns):
            rows = slice(s * lead, (s + 1) * lead)
            err = y_ref[rows, :] - t_refs[s][...]
            if s == 0:
                err = err * (i > 0).astype(f32)
            dy = err * (1.0 / d)
            dy_ref[rows, :] = dy
            dyb_ref[rows, :] = dy.astype(bf16)
            part = part + _allsum(err * err)
        l_ref[...] += jnp.broadcast_to(part * (0.5 / d), l_ref.shape)

    row = pl.BlockSpec((rt, d), lambda i: (i, 0))
    t_specs = [pl.BlockSpec((lead, d), lambda i, s=s: (jnp.clip(ns * i + s - 1, 0, last), 0)) for s in range(ns)]
    return pl.pallas_call(
        body, name="loss_head", grid=(tp // rt,), in_specs=[row] + t_specs,
        out_specs=[row, row, pl.BlockSpec((1, LANE), lambda i: (0, 0))],
        out_shape=[jax.ShapeDtypeStruct((tp, d), f32), jax.ShapeDtypeStruct((tp, d), bf16),
                   jax.ShapeDtypeStruct((1, LANE), f32)],
        compiler_params=_params(1),
    )(y, *([target] * ns))


def swiglu_fwd(u):
    tp = u.shape[0]
    rt, cb = _row_tile(tp), D_FF // 2
    nb = D_FF // cb

    def body(g_ref, u_ref, o_ref):
        g = g_ref[...]
        o_ref[...] = (g * _sigmoid(g) * u_ref[...]).astype(bf16)

    return pl.pallas_call(
        body, name="swiglu_fwd", grid=(tp // rt, nb),
        in_specs=[pl.BlockSpec((rt, cb), lambda i, j: (i, j)), pl.BlockSpec((rt, cb), lambda i, j: (i, j + nb))],
        out_specs=pl.BlockSpec((rt, cb), lambda i, j: (i, j)),
        out_shape=jax.ShapeDtypeStruct((tp, D_FF), bf16), compiler_params=_params(2),
    )(u, u)


def swiglu_bwd(u, dact):
    tp = u.shape[0]
    rt, cb = _row_tile(tp), D_FF // 2
    nb = D_FF // cb

    def body(g_ref, u_ref, da_ref, dg_ref, du_ref):
        g = g_ref[...]
        s = _sigmoid(g)
        da = da_ref[...]
        dg_ref[...] = (da * u_ref[...] * s * (1.0 + g * (1.0 - s))).astype(bf16)
        du_ref[...] = (da * g * s).astype(bf16)

    lo = pl.BlockSpec((rt, cb), lambda i, j: (i, j))
    hi = pl.BlockSpec((rt, cb), lambda i, j: (i, j + nb))
    dgate, dup = pl.pallas_call(
        body, name="swiglu_bwd", grid=(tp // rt, nb), in_specs=[lo, hi, lo], out_specs=[lo, lo],
        out_shape=[jax.ShapeDtypeStruct((tp, D_FF), bf16)] * 2, compiler_params=_params(2),
    )(u, u, dact)
    return dgate, dup


def merge_fwd(proj, ydn, ysb):
    tp = proj.shape[0]
    rt, d = _row_tile(tp), D_MODEL

    def body(gd_ref, gs_ref, yd_ref, ys_ref, o_ref):
        o_ref[...] = (_sigmoid(gd_ref[...]) * yd_ref[...] + _sigmoid(gs_ref[...]) * ys_ref[...]).astype(bf16)

    row = pl.BlockSpec((rt, d), lambda i: (i, 0))
    return pl.pallas_call(
        body, name="merge_fwd", grid=(tp // rt,),
        in_specs=[pl.BlockSpec((rt, d), lambda i: (i, C_GDN)), pl.BlockSpec((rt, d), lambda i: (i, C_GSB)), row, row],
        out_specs=row, out_shape=jax.ShapeDtypeStruct((tp, d), bf16), compiler_params=_params(1),
    )(proj, proj, ydn, ysb)


def merge_bwd(proj, ydn, ysb, dm):
    tp = proj.shape[0]
    rt, d = _row_tile(tp), D_MODEL

    def body(gd_ref, gs_ref, yd_ref, ys_ref, dm_ref, dyd_ref, dys_ref, dg_ref):
        dm_ = dm_ref[...]
        sd = _sigmoid(gd_ref[...])
        ss = _sigmoid(gs_ref[...])
        dyd_ref[...] = (dm_ * sd).astype(bf16)
        dys_ref[...] = (dm_ * ss).astype(bf16)
        dg_ref[:, :d] = (dm_ * yd_ref[...] * sd * (1.0 - sd)).astype(bf16)
        dg_ref[:, d:] = (dm_ * ys_ref[...] * ss * (1.0 - ss)).astype(bf16)

    row = pl.BlockSpec((rt, d), lambda i: (i, 0))
    return pl.pallas_call(
        body, name="merge_bwd", grid=(tp // rt,),
        in_specs=[pl.BlockSpec((rt, d), lambda i: (i, C_GDN)), pl.BlockSpec((rt, d), lambda i: (i, C_GSB)), row, row, row],
        out_specs=[row, row, pl.BlockSpec((rt, 2 * d), lambda i: (i, 0))],
        out_shape=[jax.ShapeDtypeStruct((tp, d), bf16)] * 2 + [jax.ShapeDtypeStruct((tp, 2 * d), bf16)],
        compiler_params=_params(1),
    )(proj, proj, ydn, ysb, dm)


def dn_out_fwd(o, proj, gain):
    tp = o.shape[0]
    rt, cb, wide = _row_tile(tp), DN_DV, 1024
    zb = C_DZ * 1024 // wide

    def body(o_ref, z_ref, g_ref, y_ref):
        for s in range(wide // cb):
            sl = slice(s * cb, (s + 1) * cb)
            x = o_ref[:, sl]
            r = lax.rsqrt(jnp.mean(x * x, axis=-1, keepdims=True) + RMS_EPS)
            z = z_ref[:, sl]
            y_ref[:, sl] = (x * r * g_ref[...] * (z * _sigmoid(z))).astype(bf16)

    blk = pl.BlockSpec((rt, wide), lambda i, j: (i, j))
    return pl.pallas_call(
        body, name="dn_out_fwd", grid=(tp // rt, DN_V // wide),
        in_specs=[blk, pl.BlockSpec((rt, wide), lambda i, j: (i, j + zb)), pl.BlockSpec((1, cb), lambda i, j: (0, 0))],
        out_specs=blk, out_shape=jax.ShapeDtypeStruct((tp, DN_V), bf16), compiler_params=_params(2),
    )(o, proj, gain)


def dn_out_bwd(o, proj, gain, dy):
    tp = o.shape[0]
    rt, cb, wide = _row_tile(tp), DN_DV, 1024
    zb = C_DZ * 1024 // wide

    def body(o_ref, z_ref, g_ref, dy_ref, do_ref, dz_ref, dg_ref):
        i, j = pl.program_id(0), pl.program_id(1)
        g = g_ref[...]
        part = jnp.zeros((1, cb), f32)
        for hh in range(wide // cb):
            sl = slice(hh * cb, (hh + 1) * cb)
            x = o_ref[:, sl]
            r = lax.rsqrt(jnp.mean(x * x, axis=-1, keepdims=True) + RMS_EPS)
            xh = x * r
            z = z_ref[:, sl]
            s = _sigmoid(z)
            dy_ = dy_ref[:, sl]
            drn = dy_ * (z * s)
            dz_ref[:, sl] = (dy_ * xh * g * s * (1.0 + z * (1.0 - s))).astype(bf16)
            dxh = drn * g
            do_ref[:, sl] = r * (dxh - xh * jnp.mean(dxh * xh, axis=-1, keepdims=True))
            part = part + jnp.sum(drn * xh, axis=0, keepdims=True)
        first = jnp.logical_and(i == 0, j == 0)

        @pl.when(first)
        def _():
            dg_ref[...] = part

        @pl.when(jnp.logical_not(first))
        def _():
            dg_ref[...] += part

    blk = pl.BlockSpec((rt, wide), lambda i, j: (i, j))
    vec = pl.BlockSpec((1, cb), lambda i, j: (0, 0))
    return pl.pallas_call(
        body, name="dn_out_bwd", grid=(tp // rt, DN_V // wide),
        in_specs=[blk, pl.BlockSpec((rt, wide), lambda i, j: (i, j + zb)), vec, blk],
        out_specs=[blk, pl.BlockSpec((rt, wide), lambda i, j: (i, j + zb)), vec],
        out_shape=[jax.ShapeDtypeStruct((tp, DN_V), f32), jax.ShapeDtypeStruct((tp, PROJ_BIG), bf16),
                   jax.ShapeDtypeStruct((1, cb), f32)],
        compiler_params=_params(2),
    )(o, proj, gain, dy)


def sb_prep_fwd(proj, gq, gk):
    tp = proj.shape[0]
    rt, cb = _row_tile(tp), SB_DH

    def body(q_ref, k_ref, v_ref, gq_ref, gk_ref, qo_ref, ko_ref, vo_ref):
        for x_ref, g_ref, o_ref in ((q_ref, gq_ref, qo_ref), (k_ref, gk_ref, ko_ref)):
            for h in range(HEADS):
                sl = slice(h * cb, (h + 1) * cb)
                x = x_ref[:, sl]
                r = lax.rsqrt(jnp.mean(x * x, axis=-1, keepdims=True) + RMS_EPS)
                o_ref[:, sl] = (x * r * g_ref[...]).astype(bf16)
        vo_ref[...] = v_ref[...].astype(bf16)

    blk = pl.BlockSpec((rt, SB_W), lambda i: (i, 0))
    vec = pl.BlockSpec((1, cb), lambda i: (0, 0))
    return pl.pallas_call(
        body, name="sb_prep_fwd", grid=(tp // rt,),
        in_specs=[pl.BlockSpec((rt, SB_W), lambda i: (i, C_SQ)), pl.BlockSpec((rt, SB_W), lambda i: (i, C_SK)),
                  pl.BlockSpec((rt, SB_W), lambda i: (i, C_SV)), vec, vec],
        out_specs=[blk] * 3, out_shape=[jax.ShapeDtypeStruct((tp, SB_W), bf16)] * 3, compiler_params=_params(1),
    )(proj, proj, proj, gq, gk)


def sb_prep_bwd(proj, gq, gk, dqs, dks, dvs, into):
    tp = proj.shape[0]
    rt, cb = _row_tile(tp), SB_DH
    assert (C_SQ * 1024) % (3 * SB_W) == 0 and (C_SQ + 1, C_SQ + 2) == (C_SK, C_SV)

    def body(q_ref, k_ref, gq_ref, gk_ref, dq_ref, dk_ref, dv_ref, into_ref, do_ref, dgq_ref, dgk_ref):
        first = pl.program_id(0) == 0
        do_ref[:, 2 * SB_W:] = dv_ref[...].astype(bf16)
        for x_ref, g_ref, dn_ref, at, dg_ref, mul in ((q_ref, gq_ref, dq_ref, 0, dgq_ref, None),
                                                      (k_ref, gk_ref, dk_ref, SB_W, dgk_ref, SB_DH ** -0.5)):
            part = jnp.zeros((1, cb), f32)
            for h in range(HEADS):
                sl = slice(h * cb, (h + 1) * cb)
                x = x_ref[:, sl]
                r = lax.rsqrt(jnp.mean(x * x, axis=-1, keepdims=True) + RMS_EPS)
                xh = x * r
                dn_ = dn_ref[:, sl] if mul is None else dn_ref[:, sl] * mul
                dxh = dn_ * g_ref[...]
                do_ref[:, at + h * cb:at + (h + 1) * cb] = (
                    r * (dxh - xh * jnp.mean(dxh * xh, axis=-1, keepdims=True))).astype(bf16)
                part = part + jnp.sum(dn_ * xh, axis=0, keepdims=True)

            @pl.when(first)
            def _(dg_ref=dg_ref, part=part):
                dg_ref[...] = part

            @pl.when(jnp.logical_not(first))
            def _(dg_ref=dg_ref, part=part):
                dg_ref[...] += part

    blk = pl.BlockSpec((rt, SB_W), lambda i: (i, 0))
    vec = pl.BlockSpec((1, cb), lambda i: (0, 0))
    return pl.pallas_call(
        body, name="sb_prep_bwd", grid=(tp // rt,),
        in_specs=[pl.BlockSpec((rt, SB_W), lambda i: (i, C_SQ)), pl.BlockSpec((rt, SB_W), lambda i: (i, C_SK)),
                  vec, vec, blk, blk, blk, pl.BlockSpec(memory_space=pl.ANY)],
        out_specs=[pl.BlockSpec((rt, 3 * SB_W), lambda i: (i, C_SQ * 1024 // (3 * SB_W))), vec, vec],
        out_shape=[jax.ShapeDtypeStruct(into.shape, into.dtype)] + [jax.ShapeDtypeStruct((1, cb), f32)] * 2,
        input_output_aliases={7: 0}, compiler_params=_params(1),
    )(proj, proj, gq, gk, dqs, dks, dvs, into)


def _conv_taps(ext, rt):
    taps = []
    for k in range(DN_CONV):
        s = DN_CONV - 1 - k
        taps.append((pltpu.roll(ext, s, axis=0) if s else ext)[SUB:SUB + rt])
    return taps


def _conv_act(taps, w, l2):
    y = taps[0] * w[0:1]
    for k in range(1, DN_CONV):
        y = y + taps[k] * w[k:k + 1]
    s = _sigmoid(y)
    a = y * s
    if l2:
        n = lax.rsqrt(jnp.sum(a * a, axis=-1, keepdims=True) + L2_EPS)
        return y, s, a, n
    return y, s, a, None


def conv_fwd(proj, w8, col_blk, ncb, l2, name):
    tp = proj.shape[0]
    rt = _row_tile(tp)
    hb = rt // SUB
    cw = min(CONV_W, ncb * LANE)
    cb0 = col_blk * LANE // cw

    def body(x_ref, h_ref, w_ref, o_ref):
        i = pl.program_id(1)
        first = (i > 0).astype(f32)
        for s in range(cw // LANE):
            sl = slice(s * LANE, (s + 1) * LANE)
            ext = jnp.concatenate([h_ref[:, sl] * first, x_ref[:, sl]], axis=0)
            _, _, a, n = _conv_act(_conv_taps(ext, rt), w_ref[:, sl], l2)
            o_ref[:, sl] = a * n if l2 else a

    return pl.pallas_call(
        body, name=name, grid=(ncb * LANE // cw, tp // rt),
        in_specs=[pl.BlockSpec((rt, cw), lambda j, i: (i, j + cb0)),
                  pl.BlockSpec((SUB, cw), lambda j, i: (jnp.maximum(i * hb - 1, 0), j + cb0)),
                  pl.BlockSpec((SUB, cw), lambda j, i: (0, j))],
        out_specs=pl.BlockSpec((rt, cw), lambda j, i: (i, j)),
        out_shape=jax.ShapeDtypeStruct((tp, ncb * LANE), f32), compiler_params=_params(2),
    )(proj, proj, w8)


def conv_bwd(proj, w8, dout, into, col_blk, ncb, l2, name):
    tp = proj.shape[0]
    rt = _row_tile(tp)
    hb = rt // SUB
    nr = tp // rt
    last8 = tp // SUB - 1
    cw = min(CONV_W, ncb * LANE)
    cb0 = col_blk * LANE // cw
    n = rt + SUB

    def body(x_ref, xb_ref, xf_ref, w_ref, d_ref, df_ref, into_ref, o_ref, dw_ref):
        i = pl.program_id(1)
        first = (i > 0).astype(f32)
        last = (i < nr - 1).astype(f32)
        rows = lax.broadcasted_iota(jnp.int32, (SUB, LANE), 0)
        for s in range(cw // LANE):
            sl = slice(s * LANE, (s + 1) * LANE)
            ext = jnp.concatenate([xb_ref[:, sl] * first, x_ref[:, sl], xf_ref[:, sl] * last], axis=0)
            taps = _conv_taps(ext, n)
            w = w_ref[:, sl]
            y, sg, a, nrm = _conv_act(taps, w, l2)
            da = jnp.concatenate([d_ref[:, sl], df_ref[:, sl] * last], axis=0)
            if l2:
                out = a * nrm
                da = nrm * (da - out * jnp.sum(da * out, axis=-1, keepdims=True))
            dy = da * sg * (1.0 + y * (1.0 - sg))
            part = jnp.zeros((SUB, LANE), f32)
            for k in range(DN_CONV):
                part = part + jnp.where(rows == k, jnp.sum(taps[k][0:rt] * dy[0:rt], axis=0, keepdims=True), 0.0)

            @pl.when(i == 0)
            def _(sl=sl, part=part):
                dw_ref[:, sl] = part

            @pl.when(i > 0)
            def _(sl=sl, part=part):
                dw_ref[:, sl] += part

            acc = None
            for k in range(DN_CONV):
                up = DN_CONV - 1 - k
                term = (pltpu.roll(dy, n - up, axis=0) if up else dy)[0:rt] * w[k:k + 1]
                acc = term if acc is None else acc + term
            o_ref[:, sl] = acc.astype(bf16)

    after = lambda j, i: (jnp.minimum((i + 1) * hb, last8), j)
    return pl.pallas_call(
        body, name=name, grid=(ncb * LANE // cw, nr),
        in_specs=[pl.BlockSpec((rt, cw), lambda j, i: (i, j + cb0)),
                  pl.BlockSpec((SUB, cw), lambda j, i: (jnp.maximum(i * hb - 1, 0), j + cb0)),
                  pl.BlockSpec((SUB, cw), lambda j, i: (jnp.minimum((i + 1) * hb, last8), j + cb0)),
                  pl.BlockSpec((SUB, cw), lambda j, i: (0, j)),
                  pl.BlockSpec((rt, cw), lambda j, i: (i, j)), pl.BlockSpec((SUB, cw), after),
                  pl.BlockSpec(memory_space=pl.ANY)],
        out_specs=[pl.BlockSpec((rt, cw), lambda j, i: (i, j + cb0)), pl.BlockSpec((SUB, cw), lambda j, i: (0, j))],
        out_shape=[jax.ShapeDtypeStruct(into.shape, into.dtype), jax.ShapeDtypeStruct((SUB, ncb * LANE), f32)],
        input_output_aliases={6: 0}, compiler_params=_params(2),
    )(proj, proj, proj, w8, dout, dout, into)


def _ab_common(p, al, dtb, r0):
    rows = r0 + lax.broadcasted_iota(jnp.int32, p.shape, 0)
    mask = (rows >= P0).astype(f32)
    xx = p + dtb
    sp = jnp.maximum(xx, 0.0) + _log1p_small(jnp.exp(-jnp.abs(xx)))
    ea = jnp.exp(al)
    g = -ea * sp * mask
    beta = _sigmoid(p) * mask
    return g, beta, _sigmoid(xx), ea, mask


def _chunk_tri(rt, later):
    r = lax.broadcasted_iota(jnp.int32, (rt, rt), 0)
    c = lax.broadcasted_iota(jnp.int32, (rt, rt), 1)
    shift = CHUNK.bit_length() - 1
    same = jnp.right_shift(r, shift) == jnp.right_shift(c, shift)
    return jnp.logical_and(same, c >= r if later else c <= r).astype(f32)


def ab_fwd(pab, al, dtb):
    tp = pab.shape[0]
    rt = _row_tile(tp)
    assert rt % CHUNK == 0

    def body(p_ref, al_ref, dt_ref, g_ref, b_ref):
        i = pl.program_id(0)
        g, beta, _, _, _ = _ab_common(p_ref[...], al_ref[...], dt_ref[...], i * rt)
        gam = _hdot(_chunk_tri(rt, False), g)
        for h in range(HEADS):
            g_ref[h] = jnp.broadcast_to(gam[:, h:h + 1], (rt, LANE))
            b_ref[h] = jnp.broadcast_to(beta[:, HEADS + h:HEADS + h + 1], (rt, LANE))

    vec = pl.BlockSpec((1, LANE), lambda i: (0, 0))
    out = pl.BlockSpec((HEADS, rt, LANE), lambda i: (0, i, 0))
    return pl.pallas_call(
        body, name="ab_fwd", grid=(tp // rt,), in_specs=[pl.BlockSpec((rt, LANE), lambda i: (i, 0)), vec, vec],
        out_specs=[out, out], out_shape=[jax.ShapeDtypeStruct((HEADS, tp, LANE), f32)] * 2, compiler_params=_params(1),
    )(pab, al, dtb)


def ab_bwd(pab, al, dtb, dg, db):
    tp = pab.shape[0]
    rt = _row_tile(tp)

    def body(p_ref, al_ref, dt_ref, dg_ref, db_ref, dp_ref, dal_ref, ddt_ref):
        i = pl.program_id(0)
        g, beta, sx, ea, mask = _ab_common(p_ref[...], al_ref[...], dt_ref[...], i * rt)
        lanes = lax.broadcasted_iota(jnp.int32, (rt, LANE), 1)
        dgl = jnp.zeros((rt, LANE), f32)
        dbl = jnp.zeros((rt, LANE), f32)
        for h in range(HEADS):
            dgl = dgl + jnp.where(lanes == h, dg_ref[h], 0.0)
            dbl = dbl + jnp.where(lanes == HEADS + h, db_ref[h], 0.0)
        dgl = _hdot(_chunk_tri(rt, True), dgl)
        dxx = dgl * (-ea) * sx * mask
        dp_ref[...] = (dxx + dbl * beta * (1.0 - beta)).astype(bf16)
        pal = jnp.sum(dgl * g, axis=0, keepdims=True)
        pdt = jnp.sum(dxx, axis=0, keepdims=True)

        @pl.when(i == 0)
        def _():
            dal_ref[...] = pal
            ddt_ref[...] = pdt

        @pl.when(i > 0)
        def _():
            dal_ref[...] += pal
            ddt_ref[...] += pdt

    vec = pl.BlockSpec((1, LANE), lambda i: (0, 0))
    row = pl.BlockSpec((rt, LANE), lambda i: (i, 0))
    big = pl.BlockSpec((HEADS, rt, LANE), lambda i: (0, i, 0))
    return pl.pallas_call(
        body, name="ab_bwd", grid=(tp // rt,), in_specs=[row, vec, vec, big, big], out_specs=[row, vec, vec],
        out_shape=[jax.ShapeDtypeStruct((tp, LANE), bf16), jax.ShapeDtypeStruct((1, LANE), f32),
                   jax.ShapeDtypeStruct((1, LANE), f32)],
        compiler_params=_params(1),
    )(pab, al, dtb, dg, db)


class _Chunk:
    pass


def _gdn_chunk(q, k, v, gcol, bcol, grow8):
    C = CHUNK
    R = range(len(q))
    X = _Chunk()
    ri = lax.broadcasted_iota(jnp.int32, (C, C), 0)
    ci = lax.broadcasted_iota(jnp.int32, (C, C), 1)
    eye = (ri == ci).astype(f32)
    gam = list(gcol)
    gam_row = [grow8[h][0:1, 0:C] for h in R]
    X.ri, X.ci = ri, ci
    X.Dm = [jnp.where(ri >= ci, jnp.exp(jnp.minimum(gam[h][:, 0:C] - gam_row[h], 0.0)), 0.0) for h in R]
    X.eg = [jnp.exp(gam[h]) for h in R]
    gl = [gam[h][C - 1:C, :] for h in R]
    X.egl = [jnp.exp(gl[h]) for h in R]
    X.kdec = [jnp.exp(gl[h] - gam[h]) for h in R]
    X.qs = [q[h] * (DN_DK ** -0.5) for h in R]
    X.kb = [k[h] * bcol[h] for h in R]
    kk = [_dot_nt(X.kb[h], k[h]) for h in R]
    qk = [_dot_nt(X.qs[h], k[h]) for h in R]
    X.A = [jnp.where(ri > ci, kk[h] * X.Dm[h], 0.0) for h in R]
    assert C == 64
    b16 = jnp.right_shift(ri, 4) == jnp.right_shift(ci, 4)
    b32 = jnp.right_shift(ri, 5) == jnp.right_shift(ci, 5)
    P = [jnp.where(b16, X.A[h], 0.0) for h in R]
    T = [eye - P[h] for h in R]
    for _ in range(3):
        P = [_hdot(P[h], P[h]) for h in R]
        T = [T[h] + _hdot(T[h], P[h]) for h in R]
    for off in (jnp.logical_and(b32, jnp.logical_not(b16)), jnp.logical_not(b32)):
        AT = [_hdot(jnp.where(off, X.A[h], 0.0), T[h]) for h in R]
        T = [T[h] - _hdot(T[h], AT[h]) for h in R]
    X.T = T
    X.b2 = [jnp.concatenate([bcol[h], bcol[h]], axis=-1) for h in R]
    X.u = [_hdot(T[h], v[h] * X.b2[h]) for h in R]
    X.w = [_hdot(T[h], X.kb[h] * X.eg[h]) for h in R]
    X.attn = [qk[h] * X.Dm[h] for h in R]
    X.qg = [X.qs[h] * X.eg[h] for h in R]
    X.kg = [k[h] * X.kdec[h] for h in R]
    return X


def gdn_fwd(q, k, v, gc, bc, grow):
    tp = q.shape[0]
    nc = tp // CHUNK
    hb = GDN_HEADS_PER_STEP

    def body(q_ref, k_ref, v_ref, gc_ref, bc_ref, gr_ref, o_ref, ss_ref, S_ref):
        c = pl.program_id(1)

        @pl.when(c == 0)
        def _():
            S_ref[...] = jnp.zeros_like(S_ref)

        R = range(hb)
        qc = [slice(h * DN_DK, (h + 1) * DN_DK) for h in R]
        vc = [slice(h * DN_DV, (h + 1) * DN_DV) for h in R]
        X = _gdn_chunk([q_ref[:, qc[h]] for h in R], [k_ref[:, qc[h]] for h in R], [v_ref[:, vc[h]] for h in R],
                       [gc_ref[h] for h in R], [bc_ref[h] for h in R], [gr_ref[h] for h in R])
        S = [S_ref[h] for h in R]
        for h in R:
            ss_ref[h, 0] = S[h]
        wS = [_dot(X.w[h], S[h]) for h in R]
        qS = [_dot(X.qg[h], S[h]) for h in R]
        vn = [X.u[h] - wS[h] for h in R]
        av = [_dot(X.attn[h], vn[h]) for h in R]
        kv = [_dot_tn(X.kg[h], vn[h]) for h in R]
        for h in R:
            o_ref[:, vc[h]] = qS[h] + av[h]
            S_ref[h] = S[h] * X.egl[h][:, 0:1] + kv[h]

    qk = pl.BlockSpec((CHUNK, hb * DN_DK), lambda g, c: (c, g))
    vv = pl.BlockSpec((CHUNK, hb * DN_DV), lambda g, c: (c, g))
    col = pl.BlockSpec((hb, CHUNK, LANE), lambda g, c: (g, c, 0))
    row = pl.BlockSpec((hb, SUB, LANE), lambda g, c: (g, c, 0))
    return pl.pallas_call(
        body, name="gdn_fwd", grid=(HEADS // hb, nc), in_specs=[qk, qk, vv, col, col, row],
        out_specs=[vv, pl.BlockSpec((hb, 1, DN_DK, DN_DV), lambda g, c: (g, c, 0, 0))],
        out_shape=[jax.ShapeDtypeStruct((tp, DN_V), f32), jax.ShapeDtypeStruct((HEADS, nc, DN_DK, DN_DV), f32)],
        scratch_shapes=[pltpu.VMEM((hb, DN_DK, DN_DV), f32)], compiler_params=_params(2),
    )(q, k, v, gc, bc, grow)


def gdn_bwd(q, k, v, gc, bc, grow, states, do, rider=None):
    tp = q.shape[0]
    nc = tp // CHUNK
    C = CHUNK
    hb = GDN_HEADS_PER_STEP
    grid = (HEADS // hb, nc)
    split, ride_first, ride_last = _ride(rider, 8, 5, grid)

    def body(*refs):
        ((q_ref, k_ref, v_ref, gc_ref, bc_ref, gr_ref, ss_ref, do_ref), (dq_ref, dk_ref, dv_ref, dg_ref, db_ref),
         (rin, rout, rest)) = split(refs)
        dS_ref, ride = rest[0], (rin, rout, rest[1:])
        ride_first(ride)
        c = pl.program_id(1)

        @pl.when(c == 0)
        def _():
            dS_ref[...] = jnp.zeros_like(dS_ref)

        R = range(hb)
        qc = [slice(h * DN_DK, (h + 1) * DN_DK) for h in R]
        vc = [slice(h * DN_DV, (h + 1) * DN_DV) for h in R]
        k_ = [k_ref[:, qc[h]] for h in R]
        v_ = [v_ref[:, vc[h]] for h in R]
        bcol = [bc_ref[h] for h in R]
        X = _gdn_chunk([q_ref[:, qc[h]] for h in R], k_, v_, [gc_ref[h] for h in R], bcol, [gr_ref[h] for h in R])
        ri, ci = X.ri, X.ci
        S = [ss_ref[h, 0] for h in R]
        do_ = [do_ref[:, vc[h]] for h in R]
        dSn = [dS_ref[h] for h in R]
        wS = [_dot(X.w[h], S[h]) for h in R]
        ado = [_dot_tn(X.attn[h], do_[h]) for h in R]
        kdS = [_dot(X.kg[h], dSn[h]) for h in R]
        d_qg = [_dot_nt(do_[h], S[h]) for h in R]
        qdo = [_dot_tn(X.qg[h], do_[h]) for h in R]
        vn = [X.u[h] - wS[h] for h in R]
        d_vn = [ado[h] + kdS[h] for h in R]
        dovn = [_dot_nt(do_[h], vn[h]) for h in R]
        d_kg = [_dot_nt(vn[h], dSn[h]) for h in R]
        wdv = [_dot_tn(X.w[h], d_vn[h]) for h in R]
        dw = [-_dot_nt(d_vn[h], S[h]) for h in R]
        for h in R:
            dS_ref[h] = qdo[h] + X.egl[h][:, 0:1] * dSn[h] - wdv[h]
        dattn = [jnp.where(ri >= ci, dovn[h], 0.0) for h in R]
        dRu = [_hdot_tn(X.T[h], d_vn[h]) for h in R]
        dRw = [_hdot_tn(X.T[h], dw[h]) for h in R]
        dAu = [_hdot_nt(dRu[h], X.u[h]) for h in R]
        dAw = [_hdot_nt(dRw[h], X.w[h]) for h in R]
        dA = [jnp.where(ri > ci, -(dAu[h] + dAw[h]), 0.0) for h in R]
        dKK = [dA[h] * X.Dm[h] for h in R]
        dQK = [dattn[h] * X.Dm[h] for h in R]
        E = [dA[h] * X.A[h] + dattn[h] * X.attn[h] for h in R]
        dkb = [_dot(dKK[h], k_[h]) + dRw[h] * X.eg[h] for h in R]
        dk1 = [_dot_tn(dKK[h], X.kb[h]) for h in R]
        dqs = [_dot(dQK[h], k_[h]) + d_qg[h] * X.eg[h] for h in R]
        dk2 = [_dot_tn(dQK[h], X.qs[h]) for h in R]
        ones = jnp.ones((C, LANE), f32)
        colE = [_hdot_tn(E[h], ones) for h in R]
        rows = lax.broadcasted_iota(jnp.int32, (C, LANE), 0)
        dgam = []
        for h in R:
            t = d_kg[h] * X.kg[h]
            dgl = _allsum(t) + X.egl[h][:, 0:1] * _allsum(S[h] * dSn[h])
            g = (_rowsum(E[h]) - colE[h] + _rowsum(dRw[h] * (X.kb[h] * X.eg[h])) + _rowsum(d_qg[h] * X.qg[h])
                 - _rowsum(t))
            dgam.append(g + jnp.where(rows == C - 1, dgl, 0.0))
        for h in R:
            dv_ref[:, vc[h]] = dRu[h] * X.b2[h]
            dbeta = _rowsum(dRu[h] * v_[h]) + _rowsum(dkb[h] * k_[h])
            dq_ref[:, qc[h]] = dqs[h] * (DN_DK ** -0.5)
            dk_ref[:, qc[h]] = dk1[h] + dk2[h] + dkb[h] * bcol[h] + d_kg[h] * X.kdec[h]
            dg_ref[h] = dgam[h]
            db_ref[h] = jnp.broadcast_to(dbeta, (C, LANE))
        ride_last(ride)

    rc = lambda c: nc - 1 - c
    qk = pl.BlockSpec((CHUNK, hb * DN_DK), lambda g, c: (rc(c), g))
    vv = pl.BlockSpec((CHUNK, hb * DN_DV), lambda g, c: (rc(c), g))
    col = pl.BlockSpec((hb, CHUNK, LANE), lambda g, c: (g, rc(c), 0))
    row = pl.BlockSpec((hb, SUB, LANE), lambda g, c: (g, rc(c), 0))
    st = pl.BlockSpec((hb, 1, DN_DK, DN_DV), lambda g, c: (g, rc(c), 0, 0))
    r_ins = rider.ins if rider else []
    r_outs = rider.out_shapes if rider else []
    res = pl.pallas_call(
        body, name="gdn_bwd", grid=grid, in_specs=[qk, qk, vv, col, col, row, st, vv] + [_ANY] * len(r_ins),
        out_specs=[qk, qk, vv, col, col] + [_ANY] * len(r_outs),
        out_shape=[jax.ShapeDtypeStruct((tp, DN_QK), f32), jax.ShapeDtypeStruct((tp, DN_QK), f32),
                   jax.ShapeDtypeStruct((tp, DN_V), f32), jax.ShapeDtypeStruct((HEADS, tp, LANE), f32),
                   jax.ShapeDtypeStruct((HEADS, tp, LANE), f32)] + list(r_outs),
        scratch_shapes=[pltpu.VMEM((hb, DN_DK, DN_DV), f32)] + (rider.scratch if rider else []),
        compiler_params=_params(2),
    )(q, k, v, gc, bc, grow, states, do, *r_ins)
    return res[:5], res[5:]


def _cumsum_after(x, nb, us, pieces=2):
    B, n = SB_BLOCK, x.shape[0]
    hi = x.astype(bf16)
    parts = (hi, (x - hi.astype(f32)).astype(bf16)) if pieces == 2 else (hi,)
    rows = [p[:, b * B:(b + 1) * B] for p in parts for b in range(nb)]
    r = jnp.dot(jnp.concatenate(rows, axis=0), us, preferred_element_type=f32)
    out = [r[b * n:(b + 1) * n] for b in range(nb)]
    if pieces == 2:
        out = [out[b] + r[(nb + b) * n:(nb + b + 1) * n] for b in range(nb)]
    return out[0] if nb == 1 else jnp.concatenate(out, axis=1)


def _later_blocks(x, nb, carry):
    B = SB_BLOCK
    tot = [_rowsum(x[:, b * B:(b + 1) * B]) for b in range(nb)]
    offs = [None] * nb
    run = carry
    for b in range(nb - 1, -1, -1):
        offs[b] = jnp.broadcast_to(run, (x.shape[0], B))
        run = run + tot[b]
    return (offs[0] if nb == 1 else jnp.concatenate(offs, axis=1)), run


def _sb_group(i, t):
    top = (i + 1) * (SB_QB // SB_BLOCK) - 1 - SB_GROUP * t
    jlo = jnp.maximum(top - SB_GROUP + 1, 0)
    rows = pl.ds(pl.multiple_of(jlo * SB_BLOCK, SB_BLOCK), SB_GROUP * SB_BLOCK)
    return jlo, rows, (top + 1) * SB_BLOCK


def _sb_weights(q, kcat, i, jlo, kend, cs, us, masked):
    B, nb = SB_BLOCK, SB_GROUP
    R = range(len(q))
    z = [_dot_nt(q[h], kcat[h]) * (SB_DH ** -0.5) for h in R]
    e = [jnp.exp(-jnp.abs(z[h])) for h in R]
    l1p = [jnp.log(1.0 + e[h]) for h in R]
    lsp = [jnp.minimum(z[h], 0.0) - l1p[h] for h in R]
    lk = [lsp[h] - z[h] for h in R]
    vis = None
    if masked:
        qpos = i * SB_QB + lax.broadcasted_iota(jnp.int32, (SB_QB, nb * B), 0)
        kpos = jlo * B + lax.broadcasted_iota(jnp.int32, (SB_QB, nb * B), 1)
        vis = jnp.logical_and(kpos < jnp.minimum(qpos, kend), kpos >= P0)
        lk = [jnp.where(vis, lk[h], 0.0) for h in R]
    later = [_later_blocks(lk[h], nb, cs[h]) for h in R]
    cum = [_cumsum_after(lk[h], nb, us) for h in R]
    w = [jnp.exp(lsp[h] + cum[h] + later[h][0]) for h in R]
    if masked:
        w = [jnp.where(vis, w[h], 0.0) for h in R]
    return lsp, vis, w, [later[h][1] for h in R]


def _sb_loop(i, step, carry):
    trips = ((i + 1) * (SB_QB // SB_BLOCK) - 1 + SB_GROUP) // SB_GROUP
    carry = step(True)(0, carry)
    carry = lax.fori_loop(1, trips - 1, step(False), carry)
    return lax.fori_loop(jnp.maximum(trips - 1, 1), trips, step(True), carry)


def _ride(rider, n_in, n_out, grid):
    n_rin = len(rider.ins) if rider else 0
    n_rout = len(rider.out_shapes) if rider else 0

    def split(refs):
        ins, rin = refs[:n_in], refs[n_in:n_in + n_rin]
        outs = refs[n_in + n_rin:n_in + n_rin + n_out]
        rout = refs[n_in + n_rin + n_out:n_in + n_rin + n_out + n_rout]
        return ins, outs, (rin, rout, refs[n_in + n_rin + n_out + n_rout:])

    def at(step, fn, r):
        if rider is None:
            return
        cond = None
        for a, g in enumerate(grid):
            c = pl.program_id(a) == (g - 1 if step == "last" else 0)
            cond = c if cond is None else jnp.logical_and(cond, c)

        @pl.when(cond)
        def _():
            fn(*r)

    first = lambda r: at("first", rider.start if rider else None, r)
    last = lambda r: at("last", rider.finish if rider else None, r)
    return split, first, last


def sb_fwd(qs, ks, vs, rider=None):
    tp = qs.shape[0]
    nq = tp // SB_QB
    B, G, hb, QB = SB_BLOCK, SB_GROUP, SB_FWD_HEADS_PER_STEP, SB_QB
    assert tp >= G * B and tp % QB == 0 and QB % B == 0 and G * B >= QB
    grid = (HEADS // hb, nq)
    split, ride_first, ride_last = _ride(rider, 3, 2, grid)

    def body(*refs):
        (q_ref, k_ref, v_ref), (o_ref, ob_ref), ride = split(refs)
        ride_first(ride)
        i = pl.program_id(1)
        R = range(hb)
        hs = [slice(h * SB_DH, (h + 1) * SB_DH) for h in R]
        q = [q_ref[:, hs[h]] for h in R]
        us = (lax.broadcasted_iota(jnp.int32, (B, B), 0) > lax.broadcasted_iota(jnp.int32, (B, B), 1)).astype(bf16)

        def make_step(masked):
            def step(t, carry):
                acc, cs = carry
                jlo, rows, kend = _sb_group(i, t)
                _, _, w, cs = _sb_weights(q, [k_ref[rows, hs[h]] for h in R], i, jlo, kend, cs, us, masked)
                pv = [_dot(w[h], v_ref[rows, hs[h]]) for h in R]
                return tuple(acc[h] + pv[h] for h in R), tuple(cs)
            return step

        carry = (tuple(jnp.zeros((QB, SB_DH), f32) for _ in R), tuple(jnp.zeros((QB, 1), f32) for _ in R))
        acc, _ = _sb_loop(i, make_step, carry)
        for h in R:
            o_ref[:, hs[h]] = acc[h]
            ob_ref[:, hs[h]] = acc[h].astype(bf16)
        ride_last(ride)

    blk = pl.BlockSpec((QB, hb * SB_DH), lambda g, i: (i, g))
    full = pl.BlockSpec((tp, hb * SB_DH), lambda g, i: (0, g))
    r_ins = rider.ins if rider else []
    r_outs = rider.out_shapes if rider else []
    res = pl.pallas_call(
        body, name="sb_fwd", grid=grid, in_specs=[blk, full, full] + [_ANY] * len(r_ins),
        out_specs=[blk, blk] + [_ANY] * len(r_outs),
        out_shape=[jax.ShapeDtypeStruct((tp, SB_W), f32), jax.ShapeDtypeStruct((tp, SB_W), bf16)] + list(r_outs),
        scratch_shapes=rider.scratch if rider else [], compiler_params=_params(2),
    )(qs, ks, vs, *r_ins)
    return res[0], res[1], res[2:]


def sb_bwd(qs, ks, vs, o, do, rider=None):
    tp = qs.shape[0]
    nq = tp // SB_QB
    B, G, hb, QB = SB_BLOCK, SB_GROUP, SB_HEADS_PER_STEP, SB_QB
    assert tp >= G * B and tp % QB == 0 and QB % B == 0 and G * B >= QB
    grid = (HEADS // hb, nq)
    split, ride_first, ride_last = _ride(rider, 5, 3, grid)

    def body(*refs):
        (q_ref, k_ref, v_ref, o_ref, do_ref), (dq_ref, dk_ref, dv_ref), ride = split(refs)
        ride_first(ride)
        i = pl.program_id(1)

        @pl.when(i == 0)
        def _():
            dk_ref[...] = jnp.zeros_like(dk_ref)
            dv_ref[...] = jnp.zeros_like(dv_ref)

        R = range(hb)
        hs = [slice(h * SB_DH, (h + 1) * SB_DH) for h in R]
        q = [q_ref[:, hs[h]] for h in R]
        dob = [do_ref[:, hs[h]].astype(bf16) for h in R]
        et = [_rowsum(dob[h].astype(f32) * o_ref[:, hs[h]]) for h in R]
        us = (lax.broadcasted_iota(jnp.int32, (B, B), 0) > lax.broadcasted_iota(jnp.int32, (B, B), 1)).astype(bf16)

        def make_step(masked):
            def step(t, carry):
                dq, cs, ce = carry
                jlo, rows, kend = _sb_group(i, t)
                kcat = [k_ref[rows, hs[h]] for h in R]
                dwv = [_dot_nt(dob[h], v_ref[rows, hs[h]]) for h in R]
                lsp, vis, w, cs = _sb_weights(q, kcat, i, jlo, kend, cs, us, masked)
                wb = [w[h].astype(bf16) for h in R]
                ee = [dwv[h] * wb[h].astype(f32) for h in R]
                later = [_later_blocks(ee[h], G, ce[h]) for h in R]
                cum = [_cumsum_after(ee[h], G, us) for h in R]
                dz = []
                for h in R:
                    d = ee[h] - jnp.exp(lsp[h]) * (et[h] - (cum[h] + later[h][0]))
                    if masked:
                        d = jnp.where(vis, d, 0.0)
                    dz.append(d.astype(bf16))
                dkj = [_dot_tn(dz[h], q[h]) for h in R]
                dvj = [_dot_tn(wb[h], dob[h]) for h in R]
                dqj = [_dot(dz[h], kcat[h]) for h in R]
                for h in R:
                    dk_ref[rows, hs[h]] += dkj[h]
                    dv_ref[rows, hs[h]] += dvj[h]
                return tuple(dq[h] + dqj[h] for h in R), tuple(cs), tuple(later[h][1] for h in R)
            return step

        z0 = tuple(jnp.zeros((QB, 1), f32) for _ in R)
        dq, _, _ = _sb_loop(i, make_step, (tuple(jnp.zeros((QB, SB_DH), f32) for _ in R), z0, z0))
        for h in R:
            dq_ref[:, hs[h]] = dq[h] * (SB_DH ** -0.5)
        ride_last(ride)

    blk = pl.BlockSpec((QB, hb * SB_DH), lambda g, i: (i, g))
    full = pl.BlockSpec((tp, hb * SB_DH), lambda g, i: (0, g))
    r_ins = rider.ins if rider else []
    r_outs = rider.out_shapes if rider else []
    res = pl.pallas_call(
        body, name="sb_bwd", grid=grid, in_specs=[blk, full, full, blk, blk] + [_ANY] * len(r_ins),
        out_specs=[blk, full, full] + [_ANY] * len(r_outs),
        out_shape=[jax.ShapeDtypeStruct((tp, SB_W), f32)] * 3 + list(r_outs),
        scratch_shapes=rider.scratch if rider else [], compiler_params=_params(2),
    )(qs, ks, vs, o, do, *r_ins)
    return res[:3], res[3:]


def adamw(w, g, m, v, name, rider=None):
    r, c = w.shape
    rt = _tile(r, 128, SUB) if r % SUB == 0 else r
    blk = pl.BlockSpec((rt, c), lambda i: (i, 0))
    c1 = 1.0 - ADAM_B1 ** ADAM_STEP
    c2 = 1.0 - ADAM_B2 ** ADAM_STEP
    grid = (r // rt,)
    split, ride_first, ride_last = _ride(rider, 4, 3, grid)

    def body(*refs):
        (w_ref, g_ref, m_ref, v_ref), (d_ref, mo_ref, vo_ref), ride = split(refs)
        ride_first(ride)
        g_ = g_ref[...]
        m_ = ADAM_B1 * m_ref[...] + (1.0 - ADAM_B1) * g_
        v_ = ADAM_B2 * v_ref[...] + (1.0 - ADAM_B2) * (g_ * g_)
        mo_ref[...] = m_
        vo_ref[...] = v_
        d_ref[...] = -ADAM_LR * ((m_ / c1) / (jnp.sqrt(v_ / c2) + ADAM_EPS) + ADAM_WD * w_ref[...])
        ride_last(ride)

    r_ins = rider.ins if rider else []
    r_outs = rider.out_shapes if rider else []
    res = pl.pallas_call(
        body, name=name, grid=grid, in_specs=[blk] * 4 + [_ANY] * len(r_ins), out_specs=[blk] * 3 + [_ANY] * len(r_outs),
        out_shape=[jax.ShapeDtypeStruct((r, c), f32)] * 3 + list(r_outs),
        scratch_shapes=rider.scratch if rider else [], compiler_params=_params(1),
    )(w, g, m, v, *r_ins)
    return res[:3], res[3:]


def sum_slots(x, name):
    n, r, c = x.shape
    rt = _tile(r, 128, SUB) if r % SUB == 0 else r
    blk = pl.BlockSpec((n, rt, c), lambda i: (0, i, 0))

    def body(x_ref, o_ref):
        acc = x_ref[0].astype(f32)
        for s in range(1, n):
            acc = acc + x_ref[s].astype(f32)
        o_ref[...] = acc

    return pl.pallas_call(
        body, name=name, grid=(r // rt,), in_specs=[blk], out_specs=pl.BlockSpec((rt, c), lambda i: (i, 0)),
        out_shape=jax.ShapeDtypeStruct((r, c), f32), compiler_params=_params(1),
    )(x)


def sum_chip_parts(parts, slots, chip, name):
    n, r, c = slots.shape
    rt = _tile(r, 128, SUB) if r % SUB == 0 else r

    def body(chip_ref, own_ref, a_ref, b_ref, c_ref, o_ref):
        o_ref[...] = ((own_ref[0].astype(f32) + a_ref[0].astype(f32)) + b_ref[0].astype(f32)) + c_ref[0].astype(f32)

    def at(rel):
        return pl.BlockSpec((1, rt, c), lambda i, chip_ref: (jnp.bitwise_xor(chip_ref[0], rel), i, 0))

    return pl.pallas_call(
        body, name=name,
        grid_spec=pltpu.PrefetchScalarGridSpec(
            num_scalar_prefetch=1, grid=(r // rt,), in_specs=[at(0), at(1), at(2), at(3)],
            out_specs=pl.BlockSpec((rt, c), lambda i, chip_ref: (i, 0))),
        out_shape=jax.ShapeDtypeStruct((r, c), f32), compiler_params=_params(1),
    )(jnp.reshape(chip, (1,)).astype(jnp.int32), parts, slots, slots, slots)


def add2(a, b, name, out_dtype=f32):
    n, r, c = a.shape
    rt = _tile(r, 64, SUB) if r % SUB == 0 else r
    blk = pl.BlockSpec((n, rt, c), lambda i: (0, i, 0))

    def body(a_ref, b_ref, o_ref):
        o_ref[...] = (a_ref[...] + b_ref[...]).astype(out_dtype)

    return pl.pallas_call(
        body, name=name, grid=(r // rt,), in_specs=[blk, blk], out_specs=blk,
        out_shape=jax.ShapeDtypeStruct((n, r, c), out_dtype), compiler_params=_params(1),
    )(a, b)


_ANY = pl.BlockSpec(memory_space=pl.ANY)
_MESH = pl.DeviceIdType.MESH


def _coords():
    return lax.axis_index("x"), lax.axis_index("y"), lax.axis_index("c")


def _chip_peer(x, y, r):
    return x ^ (r >> 1), y ^ (r & 1)


class _Exchange:
    def __init__(self, ins, out_shapes, scratch, start, finish):
        self.ins, self.out_shapes, self.scratch, self.start, self.finish = ins, out_shapes, scratch, start, finish

    def split(self, refs):
        n, m = len(self.ins), len(self.out_shapes)
        return refs[:n], refs[n:n + m], refs[n + m:]


def run_exchange(ex, name):
    def body(*refs):
        ins, outs, sems = ex.split(refs)
        ex.start(ins, outs, sems)
        ex.finish(ins, outs, sems)

    return pl.pallas_call(body, name=name, in_specs=[_ANY] * len(ex.ins), out_specs=[_ANY] * len(ex.out_shapes),
                          out_shape=ex.out_shapes, scratch_shapes=ex.scratch)(*ex.ins)


def gather_chips(big, small):
    nb, n = len(big), len(big) + len(small)
    shards = list(big) + list(small)
    kb = nb * (N_CHIPS - 1)
    k = n * (N_CHIPS - 1)

    def copies(src, dst, sems):
        send, recv, fsend, frecv = sems
        x, y, c = _coords()
        sib = (x, y, 1 - c)
        peers = [_chip_peer(x, y, r) for r in range(1, N_CHIPS)]

        def direct(t, j, slot):
            s = t * (N_CHIPS - 1) + j
            if t < nb:
                return pltpu.make_async_remote_copy(src[t].at[c], dst[t].at[slot, c], send.at[s], recv.at[s],
                                                    device_id=(*peers[j], c), device_id_type=_MESH)
            return pltpu.make_async_remote_copy(src[t], dst[t].at[slot], send.at[s], recv.at[s],
                                                device_id=(*peers[j], c), device_id_type=_MESH)

        def passed(t, j, half):
            s = t * (N_CHIPS - 1) + j
            px, py = peers[j]
            part = dst[t].at[2 * px + py, half]
            return pltpu.make_async_remote_copy(part, part, fsend.at[s], frecv.at[s], device_id=sib, device_id_type=_MESH)

        return direct, passed, peers, 2 * x + y, c

    def start(src, dst, sems):
        direct, _, _, me, _ = copies(src, dst, sems)
        for t in range(n):
            for j in range(N_CHIPS - 1):
                direct(t, j, me).start()

    def finish(src, dst, sems):
        direct, passed, peers, me, c = copies(src, dst, sems)
        fwd = []
        for t in range(nb):
            for j in range(N_CHIPS - 1):
                px, py = peers[j]
                direct(t, j, 2 * px + py).wait_recv()
                fwd.append(passed(t, j, c))
                fwd[-1].start()
        for t in range(nb, n):
            for j in range(N_CHIPS - 1):
                px, py = peers[j]
                direct(t, j, 2 * px + py).wait_recv()
        for t in range(nb):
            for j in range(N_CHIPS - 1):
                passed(t, j, 1 - c).wait_recv()
        for t in range(n):
            for j in range(N_CHIPS - 1):
                direct(t, j, me).wait_send()
        for cp in fwd:
            cp.wait_send()

    return _Exchange(shards, [jax.ShapeDtypeStruct((N_CHIPS,) + s.shape, s.dtype) for s in shards],
                     [pltpu.SemaphoreType.DMA((k,)), pltpu.SemaphoreType.DMA((k,)),
                      pltpu.SemaphoreType.DMA((max(kb, 1),)), pltpu.SemaphoreType.DMA((max(kb, 1),))], start, finish)


def sibling_swap(grads):
    pairs = [(t, o) for t, g in enumerate(grads) for o in range(g.shape[0])]
    k = len(pairs)

    def copies(src, dst, sems):
        send, recv = sems
        x, y, c = _coords()
        return [pltpu.make_async_remote_copy(src[t].at[o, 1 - c], dst[t].at[o], send.at[s], recv.at[s],
                                             device_id=(x, y, 1 - c), device_id_type=_MESH)
                for s, (t, o) in enumerate(pairs)]

    def start(src, dst, sems):
        for cp in copies(src, dst, sems):
            cp.start()

    def finish(src, dst, sems):
        cps = copies(src, dst, sems)
        for cp in cps:
            cp.wait_recv()
        for cp in cps:
            cp.wait_send()

    return _Exchange(list(grads), [jax.ShapeDtypeStruct((g.shape[0],) + g.shape[2:], g.dtype) for g in grads],
                     [pltpu.SemaphoreType.DMA((k,)), pltpu.SemaphoreType.DMA((k,))], start, finish)


def scatter_chips(parts):
    n = len(parts)
    k = n * (N_CHIPS - 1)

    def copy(src, dst, sems, t, r, landing):
        send, recv = sems
        x, y, c = _coords()
        me = 2 * x + y
        px, py = _chip_peer(x, y, r)
        peer = 2 * px + py
        s = t * (N_CHIPS - 1) + r - 1
        return pltpu.make_async_remote_copy(src[t].at[me if landing else peer], dst[t].at[peer if landing else me],
                                            send.at[s], recv.at[s], device_id=(px, py, c), device_id_type=_MESH)

    def start(src, dst, sems):
        for t in range(n):
            for r in range(1, N_CHIPS):
                copy(src, dst, sems, t, r, False).start()

    def finish(src, dst, sems):
        for t in range(n):
            for r in range(1, N_CHIPS):
                copy(src, dst, sems, t, r, True).wait_recv()
        for t in range(n):
            for r in range(1, N_CHIPS):
                copy(src, dst, sems, t, r, False).wait_send()

    return _Exchange(list(parts), [jax.ShapeDtypeStruct(p.shape, p.dtype) for p in parts],
                     [pltpu.SemaphoreType.DMA((k,)), pltpu.SemaphoreType.DMA((k,))], start, finish)


def sibling_send(halves):
    n = len(halves)

    def copies(src, dst, sems):
        send, recv = sems
        x, y, c = _coords()
        return [pltpu.make_async_remote_copy(src[t], dst[t], send.at[t], recv.at[t],
                                             device_id=(x, y, 1 - c), device_id_type=_MESH) for t in range(n)]

    def start(src, dst, sems):
        for cp in copies(src, dst, sems):
            cp.start()

    def finish(src, dst, sems):
        cps = copies(src, dst, sems)
        for cp in cps:
            cp.wait_recv()
        for cp in cps:
            cp.wait_send()

    return _Exchange(list(halves), [jax.ShapeDtypeStruct(h.shape, h.dtype) for h in halves],
                     [pltpu.SemaphoreType.DMA((n,)), pltpu.SemaphoreType.DMA((n,))], start, finish)


def gather_all(block):
    def copies(src, dst, sems, landing):
        send, recv, loc = sems
        x, y, c = _coords()
        me = 4 * x + 2 * y + c
        mine = pltpu.make_async_copy(src[0], dst[0].at[me], loc)
        remote = []
        for r in range(1, N_DEV):
            px, py, pc = x ^ (r >> 2), y ^ ((r >> 1) & 1), c ^ (r & 1)
            slot = 4 * px + 2 * py + pc if landing else me
            remote.append(pltpu.make_async_remote_copy(src[0], dst[0].at[slot], send.at[r - 1], recv.at[r - 1],
                                                       device_id=(px, py, pc), device_id_type=_MESH))
        return mine, remote

    def start(src, dst, sems):
        mine, outs = copies(src, dst, sems, False)
        mine.start()
        for cp in outs:
            cp.start()

    def finish(src, dst, sems):
        mine, lands = copies(src, dst, sems, True)
        for cp in lands:
            cp.wait_recv()
        for cp in lands:
            cp.wait_send()
        mine.wait()

    return _Exchange([block], [jax.ShapeDtypeStruct((N_DEV,) + block.shape, block.dtype)],
                     [pltpu.SemaphoreType.DMA((N_DEV - 1,)), pltpu.SemaphoreType.DMA((N_DEV - 1,)),
                      pltpu.SemaphoreType.DMA(())], start, finish)


def _pad_lanes(v, n=LANE):
    return jnp.pad(v, ((0, 0), (0, n - v.shape[1])))


def _w_in_pieces():
    cs = (PROJ_BIG + 2 * HEADS) // N_CHIPS
    ab_end = AB_COL + 2 * HEADS
    out = []
    for o in range(N_CHIPS):
        lo, hi = o * cs, (o + 1) * cs
        cand = [("big", lo, min(hi, AB_COL), 0), ("ab", max(lo, AB_COL), min(hi, ab_end), AB_COL),
                ("big", max(lo, ab_end), hi, 2 * HEADS)]
        out.append([(s, a - off, b - off) for s, a, b, off in cand if a < b])
    return out


def _split_w_in(w4):
    big, ab = [], []
    for o, pieces in enumerate(_w_in_pieces()):
        at = 0
        for s, a, b in pieces:
            (big if s == "big" else ab).append(w4[o][:, at:at + b - a])
            at += b - a
    return jnp.concatenate(big, axis=1), _pad_lanes(jnp.concatenate(ab, axis=1))


def _join_w_in(big, ab):
    src = {"big": big, "ab": ab}
    return jnp.stack([jnp.concatenate([src[s][:, a:b] for s, a, b in pieces], axis=1) for pieces in _w_in_pieces()])


def _conv_w8(w):
    return jnp.pad(w, ((0, SUB - DN_CONV), (0, 0)))


def _row_layout(gc, tp):
    nc = tp // CHUNK
    g = gc[:, :, 0].reshape(HEADS, nc, 1, CHUNK)
    g = jnp.broadcast_to(g, (HEADS, nc, SUB, CHUNK))
    return jnp.pad(g, ((0, 0), (0, 0), (0, 0), (0, LANE - CHUNK))).reshape(HEADS, nc * SUB, LANE)


def _step(x, meta, W, target, late_weights=None, early_swap=None, early_grads=None, last_grads=None, final_send=None):
    W = dict(W)
    seq = x.shape[0]
    tp = P0 + N_META + seq
    h0 = jnp.concatenate([jnp.zeros((P0, D_MODEL), f32), meta, x], axis=0)
    w_big, w_ab = _split_w_in(W["w_in"])
    cq8, ck8, cv8 = _conv_w8(W["conv_q"]), _conv_w8(W["conv_k"]), _conv_w8(W["conv_v"])
    al, dtb = _pad_lanes(W["dn_a_log"]), _pad_lanes(W["dn_dt_bias"])

    n1 = rms_fwd(h0, W["norm_mix_gain"], "rms1_fwd")
    proj = matmul(n1, w_big, "nn", "proj_fwd")
    pab = matmul(n1, w_ab, "nn", "pab_fwd")
    qn = conv_fwd(proj, cq8, C_DQ * 8, 8, True, "conv_q_fwd")
    kn = conv_fwd(proj, ck8, C_DK * 8, 8, True, "conv_k_fwd")
    va = conv_fwd(proj, cv8, C_DV * 8, 16, False, "conv_v_fwd")
    gc, bc = ab_fwd(pab, al, dtb)
    grow = _row_layout(gc, tp)
    o_dn, states = gdn_fwd(qn, kn, va, gc, bc, grow)
    on = dn_out_fwd(o_dn, proj, W["dn_out_norm_gain"])
    qs, ks, vs = sb_prep_fwd(proj, W["sb_q_norm_gain"], W["sb_k_norm_gain"])
    o_sb, o_sb16, arrived = sb_fwd(qs, ks, vs, rider=late_weights[0] if late_weights else None)
    if late_weights:
        W.update(late_weights[1](arrived))
    ydn = matmul(on, W["w_branch_dn"], "nn", "ydn_fwd")
    ysb = matmul(o_sb16, W["w_branch_sb"], "nn", "ysb_fwd")
    merged = merge_fwd(proj, ydn, ysb)
    h1 = matmul(merged, W["w_out"], "nn", "wout_fwd", residual=h0)
    n2 = rms_fwd(h1, W["norm_ffn_gain"], "rms2_fwd")
    u = matmul(n2, W["w_ffn_in"], "nn", "ffn_in_fwd", tn_t=512)
    act = swiglu_fwd(u)
    y = matmul(act, W["w_ffn_out"], "nn", "ffn_out_fwd", residual=h1)
    dy, dy16, loss = loss_head(y, target)

    G = {}
    dact = matmul(dy16, W["w_ffn_out"], "nt", "ffn_out_dx", tn_t=1408)
    G["w_ffn_out"] = matmul(act, dy16, "tn", "ffn_out_dw", tm_t=1408)
    dgate, dup = swiglu_bwd(u, dact)
    du = jnp.concatenate([dgate, dup], axis=1)
    dn2 = matmul(du, W["w_ffn_in"], "nt", "ffn_in_dx", tk_t=512)
    G["w_ffn_in"] = matmul(n2, du, "tn", "ffn_in_dw", tn_t=512)
    (dh1, dh1_16, G["norm_ffn_gain"]), _ = rms_bwd(h1, W["norm_ffn_gain"], dn2, dy, "rms2_bwd")
    dmerged = matmul(dh1_16, W["w_out"], "nt", "wout_dx")
    G["w_out"] = matmul(merged, dh1_16, "tn", "wout_dw")
    dyd, dys, d_gates = merge_bwd(proj, ydn, ysb, dmerged)
    don = matmul(dyd, W["w_branch_dn"], "nt", "ydn_dx")
    G["w_branch_dn"] = matmul(on, dyd, "tn", "ydn_dw")
    do_sb = matmul(dys, W["w_branch_sb"], "nt", "ysb_dx")
    G["w_branch_sb"] = matmul(o_sb16, dys, "tn", "ysb_dw")
    do_dn, dproj, G["dn_out_norm_gain"] = dn_out_bwd(o_dn, proj, W["dn_out_norm_gain"], don)
    (dqn, dkn, dva, dgc, dbc), swapped = gdn_bwd(qn, kn, va, gc, bc, grow, states, do_dn,
                                                 rider=early_swap[0](G) if early_swap else None)
    if early_swap:
        early_swap[1](swapped)
    dpab, dal, ddt = ab_bwd(pab, al, dtb, dgc, dbc)
    G["dn_a_log"], G["dn_dt_bias"] = dal[:, :HEADS], ddt[:, :HEADS]
    dproj, dcq = conv_bwd(proj, cq8, dqn, dproj, C_DQ * 8, 8, True, "conv_q_bwd")
    dproj, dck = conv_bwd(proj, ck8, dkn, dproj, C_DK * 8, 8, True, "conv_k_bwd")
    dproj, dcv = conv_bwd(proj, cv8, dva, dproj, C_DV * 8, 16, False, "conv_v_bwd")
    G["conv_q"], G["conv_k"], G["conv_v"] = dcq[:DN_CONV], dck[:DN_CONV], dcv[:DN_CONV]
    (dqs, dks, dvs), delivered = sb_bwd(qs, ks, vs, o_sb, do_sb, rider=early_grads[0](G) if early_grads else None)
    if early_grads:
        early_grads[1](delivered)
    dproj, G["sb_q_norm_gain"], G["sb_k_norm_gain"] = sb_prep_bwd(
        proj, W["sb_q_norm_gain"], W["sb_k_norm_gain"], dqs, dks, dvs, dproj)
    dproj = lax.dynamic_update_slice(dproj, d_gates, (0, C_GDN * 1024))
    dw_big = matmul(n1, dproj, "tn", "proj_dw")
    dw_ab = matmul(n1, dpab, "tn", "pab_dw")
    G["w_in"] = (dw_big, dw_ab)
    if last_grads:
        dn1, delivered = matmul(dproj, w_big, "nt", "proj_dx", tk_t=1024, rider=last_grads[0](G))
        last_grads[1](delivered)
    else:
        dn1 = matmul(dproj, w_big, "nt", "proj_dx", tk_t=1024)
    dn1 = matmul(dpab, w_ab, "nt", "pab_dx", residual=dn1)
    (dh0, _, G["norm_mix_gain"]), sent = rms_bwd(h0, W["norm_mix_gain"], dn1, dh1, "rms1_bwd",
                                                 rider=final_send[0]() if final_send else None)
    if final_send:
        final_send[1](sent)
    G["meta_tokens"] = dh0[P0:P0 + N_META]
    return loss, dh0[P0 + N_META:], G


_BIG = ("w_in", "w_branch_dn", "w_branch_sb", "w_out", "w_ffn_in", "w_ffn_out")
_COL_SHARDED = ("w_in", "w_ffn_in", "meta_tokens", "conv_q", "conv_k", "conv_v")
_SMALL_REPL = ("norm_mix_gain", "norm_ffn_gain", "dn_a_log", "dn_dt_bias", "dn_out_norm_gain", "sb_q_norm_gain",
               "sb_k_norm_gain")
_SMALL_SHARD = ("meta_tokens", "conv_q", "conv_k", "conv_v")
_ORDER = ("meta_tokens", "norm_mix_gain", "w_in", "conv_q", "conv_k", "conv_v", "dn_a_log", "dn_dt_bias",
          "dn_out_norm_gain", "sb_q_norm_gain", "sb_k_norm_gain", "w_branch_dn", "w_branch_sb", "w_out",
          "norm_ffn_gain", "w_ffn_in", "w_ffn_out")


def _unshard(g4, name):
    if name in _COL_SHARDED:
        r, cs = g4.shape[1:]
        return jnp.transpose(g4, (1, 0, 2)).reshape(r, N_CHIPS * cs)
    return g4.reshape((-1,) + g4.shape[2:])


def _to_shards(full, name):
    if name in _COL_SHARDED:
        r, c = full.shape
        return jnp.transpose(full.reshape(r, N_CHIPS, c // N_CHIPS), (1, 0, 2))
    r, c = full.shape
    return full.reshape(N_CHIPS, r // N_CHIPS, c)


def _rows_1024(a):
    r, c = a.shape
    if c >= 1024:
        return a.reshape(r * (c // 1024), 1024)
    return jnp.pad(a, ((0, 0), (0, 1024 - c)))


def kernel(x, meta_tokens, norm_mix_gain, w_in, conv_q, conv_k, conv_v, dn_a_log, dn_dt_bias, dn_out_norm_gain, sb_q_norm_gain, sb_k_norm_gain, w_branch_dn, w_branch_sb, w_out, norm_ffn_gain, w_ffn_in, w_ffn_out, loss_target, m_meta_tokens, m_norm_mix_gain, m_w_in, m_conv_q, m_conv_k, m_conv_v, m_dn_a_log, m_dn_dt_bias, m_dn_out_norm_gain, m_sb_q_norm_gain, m_sb_k_norm_gain, m_w_branch_dn, m_w_branch_sb, m_w_out, m_norm_ffn_gain, m_w_ffn_in, m_w_ffn_out, v_meta_tokens, v_norm_mix_gain, v_w_in, v_conv_q, v_conv_k, v_conv_v, v_dn_a_log, v_dn_dt_bias, v_dn_out_norm_gain, v_sb_q_norm_gain, v_sb_k_norm_gain, v_w_branch_dn, v_w_branch_sb, v_w_out, v_norm_ffn_gain, v_w_ffn_in, v_w_ffn_out):
    Wl = dict(meta_tokens=meta_tokens, norm_mix_gain=norm_mix_gain, w_in=w_in[0], conv_q=conv_q[0], conv_k=conv_k[0],
              conv_v=conv_v[0], dn_a_log=dn_a_log, dn_dt_bias=dn_dt_bias, dn_out_norm_gain=dn_out_norm_gain,
              sb_q_norm_gain=sb_q_norm_gain, sb_k_norm_gain=sb_k_norm_gain, w_branch_dn=w_branch_dn[0],
              w_branch_sb=w_branch_sb[0], w_out=w_out[0], norm_ffn_gain=norm_ffn_gain, w_ffn_in=w_ffn_in[0],
              w_ffn_out=w_ffn_out[0])
    Ml = dict(meta_tokens=m_meta_tokens, norm_mix_gain=m_norm_mix_gain, w_in=m_w_in[0], conv_q=m_conv_q[0],
              conv_k=m_conv_k[0], conv_v=m_conv_v[0], dn_a_log=m_dn_a_log, dn_dt_bias=m_dn_dt_bias,
              dn_out_norm_gain=m_dn_out_norm_gain, sb_q_norm_gain=m_sb_q_norm_gain, sb_k_norm_gain=m_sb_k_norm_gain,
              w_branch_dn=m_w_branch_dn[0], w_branch_sb=m_w_branch_sb[0], w_out=m_w_out[0],
              norm_ffn_gain=m_norm_ffn_gain, w_ffn_in=m_w_ffn_in[0], w_ffn_out=m_w_ffn_out[0])
    Vl = dict(meta_tokens=v_meta_tokens, norm_mix_gain=v_norm_mix_gain, w_in=v_w_in[0], conv_q=v_conv_q[0],
              conv_k=v_conv_k[0], conv_v=v_conv_v[0], dn_a_log=v_dn_a_log, dn_dt_bias=v_dn_dt_bias,
              dn_out_norm_gain=v_dn_out_norm_gain, sb_q_norm_gain=v_sb_q_norm_gain, sb_k_norm_gain=v_sb_k_norm_gain,
              w_branch_dn=v_w_branch_dn[0], w_branch_sb=v_w_branch_sb[0], w_out=v_w_out[0],
              norm_ffn_gain=v_norm_ffn_gain, w_ffn_in=v_w_ffn_in[0], w_ffn_out=v_w_ffn_out[0])
    lead = {n: (1,) if (n in _BIG or n in ("conv_q", "conv_k", "conv_v")) else () for n in _ORDER}

    chip = 2 * lax.axis_index("x") + lax.axis_index("y")
    c = lax.axis_index("c")
    halved = {n: Wl[n].astype(bf16).reshape(2, Wl[n].shape[0] // 2, Wl[n].shape[1]) for n in _BIG}

    def gathered_weights(names, owns, outs):
        res = {}
        for n, own, g4 in zip(names, owns, outs):
            g4 = lax.dynamic_update_slice(g4, own[None], (chip,) + (0,) * own.ndim)
            if n in _BIG:
                g4 = g4.reshape(N_CHIPS, 2 * g4.shape[2], g4.shape[3])
            res[n] = g4 if n == "w_in" else _unshard(g4, n)
        return res

    first = ["w_in"] + list(_SMALL_SHARD)
    first_own = [halved["w_in"]] + [Wl[n] for n in _SMALL_SHARD]
    W = dict(Wl)
    W.update(gathered_weights(first, first_own, run_exchange(gather_chips(first_own[:1], first_own[1:]), "gather_w_in")))
    late = [n for n in _BIG if n != "w_in"]
    late_own = [halved[n] for n in late]
    for n in late:
        del W[n]

    def halves_of(names, G):
        g4 = [_to_shards(G[n], n) for n in names]
        return [g.reshape(N_CHIPS, 2, g.shape[1] // 2, g.shape[2]) for g in g4]

    def pair_added(g42, from_sib, tag, wire):
        mine = [lax.dynamic_index_in_dim(g, c, axis=1, keepdims=False) for g in g42]
        return [add2(a, b, "grad_pair_add_%s%d" % (tag, t), out_dtype=wire)
                for t, (a, b) in enumerate(zip(mine, from_sib))]

    def chip_reduced(parts, slots, tag):
        return [sum_chip_parts(p, s, chip, "grad_chip_sum_%s%d" % (tag, t)) for t, (p, s) in enumerate(zip(parts, slots))]

    early, last = {}, {}

    def early_swap_begin(G):
        early["g42"] = halves_of(late, G)
        return sibling_swap(early["g42"])

    def early_begin(G):
        early["parts"] = pair_added(early["g42"], early["from_sib"], "a", f32)
        return scatter_chips(early["parts"])

    def last_begin(G):
        g2 = [g.reshape(1, 2, g.shape[0] // 2, g.shape[1]) for g in G["w_in"]]
        added = pair_added(g2, run_exchange(sibling_swap(g2), "grad_sibling_swap_b"), "b", bf16)
        last["parts"] = [_join_w_in(added[0][0], added[1][0])]
        return scatter_chips(last["parts"])

    loss, grad_x, G = _step(
        x[0], W["meta_tokens"], W, loss_target[0],
        late_weights=(gather_chips(late_own, []), lambda outs: gathered_weights(late, late_own, outs)),
        early_swap=(early_swap_begin, lambda outs: early.update(from_sib=outs)),
        early_grads=(early_begin, lambda slots: early.update(halves=chip_reduced(early["parts"], slots, "a"))),
        last_grads=(last_begin, lambda slots: last.update(halves=chip_reduced(last["parts"], slots, "b"))),
        final_send=(lambda: sibling_send(last["halves"] + early["halves"]), lambda outs: last.update(theirs=outs)))
    Gs = {}
    for n, h, o in zip(["w_in"] + late, last["halves"] + early["halves"], last["theirs"]):
        Gs[n] = lax.dynamic_update_slice(jnp.concatenate([o, o], axis=0), h, (c * h.shape[0], 0))

    small_names = list(_SMALL_REPL) + list(_SMALL_SHARD)
    pieces = [_rows_1024(G[n]) for n in small_names] + [_rows_1024(loss)]
    counts = [p.shape[0] for p in pieces]
    pack = jnp.concatenate(pieces, axis=0)
    pad_rows = (-pack.shape[0]) % SUB
    pack = jnp.pad(pack, ((0, pad_rows), (0, 0)))
    adam = {"w_in": adamw(Wl["w_in"], Gs["w_in"], Ml["w_in"], Vl["w_in"], "adamw_w_in", rider=gather_all(pack))}
    total = sum_slots(adam["w_in"][1][0], "small_sum")
    row = 0
    for n, cnt in zip(small_names, counts[:-1]):
        blk = total[row:row + cnt]
        row += cnt
        full_shape = G[n].shape
        if full_shape[1] >= 1024:
            blk = blk.reshape(full_shape)
        else:
            blk = blk[:, :full_shape[1]]
        if n in _SMALL_SHARD:
            cs = full_shape[1] // N_CHIPS
            blk = lax.dynamic_slice_in_dim(blk, chip * cs, cs, axis=1)
        Gs[n] = blk
    loss_out = total[row, 0]

    grads, deltas, new_m, new_v = [], [], [], []
    for n in _ORDER:
        d, m2, v2 = (adam[n] if n in adam else adamw(Wl[n], Gs[n], Ml[n], Vl[n], "adamw_" + n))[0]
        shape = lead[n] + Wl[n].shape
        grads.append(Gs[n].reshape(shape))
        deltas.append(d.reshape(shape))
        new_m.append(m2.reshape(shape))
        new_v.append(v2.reshape(shape))
    return (loss_out, grad_x[None], *grads, *deltas, *new_m, *new_v)
```

```python
import jax
import jax.numpy as jnp
from jax import lax
from jax.experimental import pallas as pl
from jax.experimental.pallas import tpu as pltpu

f32 = jnp.float32
bf16 = jnp.bfloat16

D_MODEL = 1024
N_META = 16
CHUNK = 64
HEADS = 8
DN_DK = 128
DN_DV = 256
DN_CONV = 4
DN_QK = HEADS * DN_DK
DN_V = HEADS * DN_DV
SB_DH = 128
SB_W = HEADS * SB_DH
SB_BLOCK = 128
SB_QB = 384
SB_GROUP = 4
SB_HEADS_PER_STEP = 2
SB_FWD_HEADS_PER_STEP = 4
GDN_HEADS_PER_STEP = 8
CONV_W = 2048
D_FF = 2816
RMS_EPS = 1e-6
L2_EPS = 1e-6
ADAM_LR = 0.001
ADAM_B1 = 0.9
ADAM_B2 = 0.999
ADAM_EPS = 1e-08
ADAM_WD = 0.01
ADAM_STEP = 10

P0 = 112
LANE = 128
SUB = 8
VMEM_LIMIT = 48 * 1024 * 1024
N_CHIPS = 4
N_DEV = 8

C_DQ, C_DK, C_DV, C_DZ, C_SQ, C_SK, C_SV, C_GDN, C_GSB = 0, 1, 2, 4, 6, 7, 8, 9, 10
PROJ_BIG = 11 * 1024
AB_COL = 2 * DN_QK + 2 * DN_V


def _params(n_axes):
    return pltpu.CompilerParams(dimension_semantics=("arbitrary",) * n_axes, vmem_limit_bytes=VMEM_LIMIT)


def _tile(n, target, q=LANE):
    best = None
    for t in range(q, min(n, target) + 1, q):
        if n % t == 0:
            best = t
    return best if best is not None else n


def _dot(a, b):
    return jnp.dot(a.astype(bf16), b.astype(bf16), preferred_element_type=f32)


def _dot_nt(a, b):
    return lax.dot_general(a.astype(bf16), b.astype(bf16), (((1,), (1,)), ((), ())), preferred_element_type=f32)


def _dot_tn(a, b):
    return lax.dot_general(a.astype(bf16), b.astype(bf16), (((0,), (0,)), ((), ())), preferred_element_type=f32)


_HI = lax.Precision.HIGH


def _hdot(a, b):
    return jnp.dot(a, b, precision=_HI, preferred_element_type=f32)


def _hdot_nt(a, b):
    return lax.dot_general(a, b, (((1,), (1,)), ((), ())), precision=_HI, preferred_element_type=f32)


def _hdot_tn(a, b):
    return lax.dot_general(a, b, (((0,), (0,)), ((), ())), precision=_HI, preferred_element_type=f32)


def _sigmoid(x):
    return 0.5 * jnp.tanh(0.5 * x) + 0.5


def _log1p_small(e):
    return jnp.where(e < 1e-3, e * (1.0 - e * (0.5 - e * (1.0 / 3.0))), jnp.log(1.0 + e))


def _rowsum(x):
    return jnp.sum(x, axis=1, keepdims=True)


def _allsum(x):
    return jnp.sum(jnp.sum(x, axis=1, keepdims=True), axis=0, keepdims=True)


def matmul(a, b, mode, name, residual=None, out_dtype=f32, tm_t=1408, tn_t=1024, tk_t=1408, rider=None):
    if mode == "nn":
        (M, K), (K2, N) = a.shape, b.shape
    elif mode == "nt":
        (M, K), (N, K2) = a.shape, b.shape
    else:
        (K, M), (K2, N) = a.shape, b.shape
    assert K == K2, (a.shape, b.shape, mode)
    tm, tn, tk = _tile(M, tm_t), _tile(N, tn_t), _tile(K, tk_t)
    nk = K // tk
    if mode == "nn":
        a_spec = pl.BlockSpec((tm, tk), lambda i, j, k: (i, k))
        b_spec = pl.BlockSpec((tk, tn), lambda i, j, k: (k, j))
        dims = (((1,), (0,)), ((), ()))
    elif mode == "nt":
        a_spec = pl.BlockSpec((tm, tk), lambda i, j, k: (i, k))
        b_spec = pl.BlockSpec((tn, tk), lambda i, j, k: (j, k))
        dims = (((1,), (1,)), ((), ()))
    else:
        a_spec = pl.BlockSpec((tk, tm), lambda i, j, k: (k, i))
        b_spec = pl.BlockSpec((tk, tn), lambda i, j, k: (k, j))
        dims = (((0,), (0,)), ((), ()))
    o_spec = pl.BlockSpec((tm, tn), lambda i, j, k: (i, j))
    has_res = residual is not None
    grid = (M // tm, N // tn, nk)
    split, ride_first, ride_last = _ride(rider, 3 if has_res else 2, 1, grid)

    def body(*refs):
        ins_, (o_ref,), (rin, rout, rest) = split(refs)
        a_ref, b_ref = ins_[:2]
        r_ref = ins_[2] if has_res else None
        acc_ref, ride = rest[0], (rin, rout, rest[1:])
        ride_first(ride)
        k = pl.program_id(2)

        @pl.when(k == 0)
        def _():
            acc_ref[...] = jnp.zeros_like(acc_ref)

        acc_ref[...] += lax.dot_general(a_ref[...].astype(bf16), b_ref[...].astype(bf16), dims,
                                        preferred_element_type=f32)

        @pl.when(k == nk - 1)
        def _():
            r = acc_ref[...]
            if has_res:
                r = r + r_ref[...]
            o_ref[...] = r.astype(out_dtype)

        ride_last(ride)

    ins = [a, b] + ([residual] if has_res else [])
    specs = [a_spec, b_spec] + ([o_spec] if has_res else [])
    r_ins = rider.ins if rider else []
    r_outs = rider.out_shapes if rider else []
    res = pl.pallas_call(
        body, name=name, grid=grid, in_specs=specs + [_ANY] * len(r_ins), out_specs=[o_spec] + [_ANY] * len(r_outs),
        out_shape=[jax.ShapeDtypeStruct((M, N), out_dtype)] + list(r_outs),
        scratch_shapes=[pltpu.VMEM((tm, tn), f32)] + (rider.scratch if rider else []), compiler_params=_params(3),
    )(*ins, *r_ins)
    return (res[0], res[1:]) if rider else res[0]


def _row_tile(tp):
    return _tile(tp, 512)


def rms_fwd(h, gain, name):
    tp, d = h.shape
    rt = _row_tile(tp)

    def body(h_ref, g_ref, o_ref):
        x = h_ref[...]
        r = lax.rsqrt(jnp.mean(x * x, axis=-1, keepdims=True) + RMS_EPS)
        o_ref[...] = (x * r * g_ref[...]).astype(bf16)

    return pl.pallas_call(
        body, name=name, grid=(tp // rt,),
        in_specs=[pl.BlockSpec((rt, d), lambda i: (i, 0)), pl.BlockSpec((1, d), lambda i: (0, 0))],
        out_specs=pl.BlockSpec((rt, d), lambda i: (i, 0)),
        out_shape=jax.ShapeDtypeStruct((tp, d), bf16), compiler_params=_params(1),
    )(h, gain)


def rms_bwd(h, gain, dn, dres, name, rider=None):
    tp, d = h.shape
    rt = _row_tile(tp)
    grid = (tp // rt,)
    split, ride_first, ride_last = _ride(rider, 4, 3, grid)

    def body(*refs):
        (h_ref, g_ref, dn_ref, dr_ref), (dh_ref, dhb_ref, dg_ref), ride = split(refs)
        ride_first(ride)
        i = pl.program_id(0)
        x = h_ref[...]
        r = lax.rsqrt(jnp.mean(x * x, axis=-1, keepdims=True) + RMS_EPS)
        xh = x * r
        dn_ = dn_ref[...]
        dxh = dn_ * g_ref[...]
        dh = r * (dxh - xh * jnp.mean(dxh * xh, axis=-1, keepdims=True)) + dr_ref[...]
        dh_ref[...] = dh
        dhb_ref[...] = dh.astype(bf16)
        part = jnp.sum(dn_ * xh, axis=0, keepdims=True)

        @pl.when(i == 0)
        def _():
            dg_ref[...] = part

        @pl.when(i > 0)
        def _():
            dg_ref[...] += part

        ride_last(ride)

    row = pl.BlockSpec((rt, d), lambda i: (i, 0))
    vec = pl.BlockSpec((1, d), lambda i: (0, 0))
    r_ins = rider.ins if rider else []
    r_outs = rider.out_shapes if rider else []
    res = pl.pallas_call(
        body, name=name, grid=grid, in_specs=[row, vec, row, row] + [_ANY] * len(r_ins),
        out_specs=[row, row, vec] + [_ANY] * len(r_outs),
        out_shape=[jax.ShapeDtypeStruct((tp, d), f32), jax.ShapeDtypeStruct((tp, d), bf16),
                   jax.ShapeDtypeStruct((1, d), f32)] + list(r_outs),
        scratch_shapes=rider.scratch if rider else [], compiler_params=_params(1),
    )(h, gain, dn, dres, *r_ins)
    return res[:3], res[3:]


def loss_head(y, target):
    tp, d = y.shape
    lead = P0 + N_META
    rt = _row_tile(tp)
    ns = rt // lead
    assert lead == SB_BLOCK and rt % lead == 0 and target.shape == (tp - lead, d)
    last = target.shape[0] // lead - 1

    def body(*refs):
        y_ref, t_refs = refs[0], refs[1:1 + ns]
        dy_ref, dyb_ref, l_ref = refs[1 + ns:]
        i = pl.program_id(0)

        @pl.when(i == 0)
        def _():
            l_ref[...] = jnp.zeros_like(l_ref)

        part = jnp.zeros((1, 1), f32)
        for s in range(ns):
            rows = slice(s * lead, (s + 1) * lead)
            err = y_ref[rows, :] - t_refs[s][...]
            if s == 0:
                err = err * (i > 0).astype(f32)
            dy = err * (1.0 / d)
            dy_ref[rows, :] = dy
            dyb_ref[rows, :] = dy.astype(bf16)
            part = part + _allsum(err * err)
        l_ref[...] += jnp.broadcast_to(part * (0.5 / d), l_ref.shape)

    row = pl.BlockSpec((rt, d), lambda i: (i, 0))
    t_specs = [pl.BlockSpec((lead, d), lambda i, s=s: (jnp.clip(ns * i + s - 1, 0, last), 0)) for s in range(ns)]
    return pl.pallas_call(
        body, name="loss_head", grid=(tp // rt,), in_specs=[row] + t_specs,
        out_specs=[row, row, pl.BlockSpec((1, LANE), lambda i: (0, 0))],
        out_shape=[jax.ShapeDtypeStruct((tp, d), f32), jax.ShapeDtypeStruct((tp, d), bf16),
                   jax.ShapeDtypeStruct((1, LANE), f32)],
        compiler_params=_params(1),
    )(y, *([target] * ns))


def swiglu_fwd(u):
    tp = u.shape[0]
    rt, cb = _row_tile(tp), D_FF // 2
    nb = D_FF // cb

    def body(g_ref, u_ref, o_ref):
        g = g_ref[...]
        o_ref[...] = (g * _sigmoid(g) * u_ref[...]).astype(bf16)

    return pl.pallas_call(
        body, name="swiglu_fwd", grid=(tp // rt, nb),
        in_specs=[pl.BlockSpec((rt, cb), lambda i, j: (i, j)), pl.BlockSpec((rt, cb), lambda i, j: (i, j + nb))],
        out_specs=pl.BlockSpec((rt, cb), lambda i, j: (i, j)),
        out_shape=jax.ShapeDtypeStruct((tp, D_FF), bf16), compiler_params=_params(2),
    )(u, u)


def swiglu_bwd(u, dact):
    tp = u.shape[0]
    rt, cb = _row_tile(tp), D_FF // 2
    nb = D_FF // cb

    def body(g_ref, u_ref, da_ref, dg_ref, du_ref):
        g = g_ref[...]
        s = _sigmoid(g)
        da = da_ref[...]
        dg_ref[...] = (da * u_ref[...] * s * (1.0 + g * (1.0 - s))).astype(bf16)
        du_ref[...] = (da * g * s).astype(bf16)

    lo = pl.BlockSpec((rt, cb), lambda i, j: (i, j))
    hi = pl.BlockSpec((rt, cb), lambda i, j: (i, j + nb))
    dgate, dup = pl.pallas_call(
        body, name="swiglu_bwd", grid=(tp // rt, nb), in_specs=[lo, hi, lo], out_specs=[lo, lo],
        out_shape=[jax.ShapeDtypeStruct((tp, D_FF), bf16)] * 2, compiler_params=_params(2),
    )(u, u, dact)
    return dgate, dup


def merge_fwd(proj, ydn, ysb):
    tp = proj.shape[0]
    rt, d = _row_tile(tp), D_MODEL

    def body(gd_ref, gs_ref, yd_ref, ys_ref, o_ref):
        o_ref[...] = (_sigmoid(gd_ref[...]) * yd_ref[...] + _sigmoid(gs_ref[...]) * ys_ref[...]).astype(bf16)

    row = pl.BlockSpec((rt, d), lambda i: (i, 0))
    return pl.pallas_call(
        body, name="merge_fwd", grid=(tp // rt,),
        in_specs=[pl.BlockSpec((rt, d), lambda i: (i, C_GDN)), pl.BlockSpec((rt, d), lambda i: (i, C_GSB)), row, row],
        out_specs=row, out_shape=jax.ShapeDtypeStruct((tp, d), bf16), compiler_params=_params(1),
    )(proj, proj, ydn, ysb)


def merge_bwd(proj, ydn, ysb, dm):
    tp = proj.shape[0]
    rt, d = _row_tile(tp), D_MODEL

    def body(gd_ref, gs_ref, yd_ref, ys_ref, dm_ref, dyd_ref, dys_ref, dg_ref):
        dm_ = dm_ref[...]
        sd = _sigmoid(gd_ref[...])
        ss = _sigmoid(gs_ref[...])
        dyd_ref[...] = (dm_ * sd).astype(bf16)
        dys_ref[...] = (dm_ * ss).astype(bf16)
        dg_ref[:, :d] = (dm_ * yd_ref[...] * sd * (1.0 - sd)).astype(bf16)
        dg_ref[:, d:] = (dm_ * ys_ref[...] * ss * (1.0 - ss)).astype(bf16)

    row = pl.BlockSpec((rt, d), lambda i: (i, 0))
    return pl.pallas_call(
        body, name="merge_bwd", grid=(tp // rt,),
        in_specs=[pl.BlockSpec((rt, d), lambda i: (i, C_GDN)), pl.BlockSpec((rt, d), lambda i: (i, C_GSB)), row, row, row],
        out_specs=[row, row, pl.BlockSpec((rt, 2 * d), lambda i: (i, 0))],
        out_shape=[jax.ShapeDtypeStruct((tp, d), bf16)] * 2 + [jax.ShapeDtypeStruct((tp, 2 * d), bf16)],
        compiler_params=_params(1),
    )(proj, proj, ydn, ysb, dm)


def dn_out_fwd(o, proj, gain):
    tp = o.shape[0]
    rt, cb, wide = _row_tile(tp), DN_DV, 1024
    zb = C_DZ * 1024 // wide

    def body(o_ref, z_ref, g_ref, y_ref):
        for s in range(wide // cb):
            sl = slice(s * cb, (s + 1) * cb)
            x = o_ref[:, sl]
            r = lax.rsqrt(jnp.mean(x * x, axis=-1, keepdims=True) + RMS_EPS)
            z = z_ref[:, sl]
            y_ref[:, sl] = (x * r * g_ref[...] * (z * _sigmoid(z))).astype(bf16)

    blk = pl.BlockSpec((rt, wide), lambda i, j: (i, j))
    return pl.pallas_call(
        body, name="dn_out_fwd", grid=(tp // rt, DN_V // wide),
        in_specs=[blk, pl.BlockSpec((rt, wide), lambda i, j: (i, j + zb)), pl.BlockSpec((1, cb), lambda i, j: (0, 0))],
        out_specs=blk, out_shape=jax.ShapeDtypeStruct((tp, DN_V), bf16), compiler_params=_params(2),
    )(o, proj, gain)


def dn_out_bwd(o, proj, gain, dy):
    tp = o.shape[0]
    rt, cb, wide = _row_tile(tp), DN_DV, 1024
    zb = C_DZ * 1024 // wide

    def body(o_ref, z_ref, g_ref, dy_ref, do_ref, dz_ref, dg_ref):
        i, j = pl.program_id(0), pl.program_id(1)
        g = g_ref[...]
        part = jnp.zeros((1, cb), f32)
        for hh in range(wide // cb):
            sl = slice(hh * cb, (hh + 1) * cb)
            x = o_ref[:, sl]
            r = lax.rsqrt(jnp.mean(x * x, axis=-1, keepdims=True) + RMS_EPS)
            xh = x * r
            z = z_ref[:, sl]
            s = _sigmoid(z)
            dy_ = dy_ref[:, sl]
            drn = dy_ * (z * s)
            dz_ref[:, sl] = (dy_ * xh * g * s * (1.0 + z * (1.0 - s))).astype(bf16)
            dxh = drn * g
            do_ref[:, sl] = r * (dxh - xh * jnp.mean(dxh * xh, axis=-1, keepdims=True))
            part = part + jnp.sum(drn * xh, axis=0, keepdims=True)
        first = jnp.logical_and(i == 0, j == 0)

        @pl.when(first)
        def _():
            dg_ref[...] = part

        @pl.when(jnp.logical_not(first))
        def _():
            dg_ref[...] += part

    blk = pl.BlockSpec((rt, wide), lambda i, j: (i, j))
    vec = pl.BlockSpec((1, cb), lambda i, j: (0, 0))
    return pl.pallas_call(
        body, name="dn_out_bwd", grid=(tp // rt, DN_V // wide),
        in_specs=[blk, pl.BlockSpec((rt, wide), lambda i, j: (i, j + zb)), vec, blk],
        out_specs=[blk, pl.BlockSpec((rt, wide), lambda i, j: (i, j + zb)), vec],
        out_shape=[jax.ShapeDtypeStruct((tp, DN_V), f32), jax.ShapeDtypeStruct((tp, PROJ_BIG), bf16),
                   jax.ShapeDtypeStruct((1, cb), f32)],
        compiler_params=_params(2),
    )(o, proj, gain, dy)


def sb_prep_fwd(proj, gq, gk):
    tp = proj.shape[0]
    rt, cb = _row_tile(tp), SB_DH

    def body(q_ref, k_ref, v_ref, gq_ref, gk_ref, qo_ref, ko_ref, vo_ref):
        for x_ref, g_ref, o_ref in ((q_ref, gq_ref, qo_ref), (k_ref, gk_ref, ko_ref)):
            for h in range(HEADS):
                sl = slice(h * cb, (h + 1) * cb)
                x = x_ref[:, sl]
                r = lax.rsqrt(jnp.mean(x * x, axis=-1, keepdims=True) + RMS_EPS)
                o_ref[:, sl] = (x * r * g_ref[...]).astype(bf16)
        vo_ref[...] = v_ref[...].astype(bf16)

    blk = pl.BlockSpec((rt, SB_W), lambda i: (i, 0))
    vec = pl.BlockSpec((1, cb), lambda i: (0, 0))
    return pl.pallas_call(
        body, name="sb_prep_fwd", grid=(tp // rt,),
        in_specs=[pl.BlockSpec((rt, SB_W), lambda i: (i, C_SQ)), pl.BlockSpec((rt, SB_W), lambda i: (i, C_SK)),
                  pl.BlockSpec((rt, SB_W), lambda i: (i, C_SV)), vec, vec],
        out_specs=[blk] * 3, out_shape=[jax.ShapeDtypeStruct((tp, SB_W), bf16)] * 3, compiler_params=_params(1),
    )(proj, proj, proj, gq, gk)


def sb_prep_bwd(proj, gq, gk, dqs, dks, dvs, into):
    tp = proj.shape[0]
    rt, cb = _row_tile(tp), SB_DH
    assert (C_SQ * 1024) % (3 * SB_W) == 0 and (C_SQ + 1, C_SQ + 2) == (C_SK, C_SV)

    def body(q_ref, k_ref, gq_ref, gk_ref, dq_ref, dk_ref, dv_ref, into_ref, do_ref, dgq_ref, dgk_ref):
        first = pl.program_id(0) == 0
        do_ref[:, 2 * SB_W:] = dv_ref[...].astype(bf16)
        for x_ref, g_ref, dn_ref, at, dg_ref, mul in ((q_ref, gq_ref, dq_ref, 0, dgq_ref, None),
                                                      (k_ref, gk_ref, dk_ref, SB_W, dgk_ref, SB_DH ** -0.5)):
            part = jnp.zeros((1, cb), f32)
            for h in range(HEADS):
                sl = slice(h * cb, (h + 1) * cb)
                x = x_ref[:, sl]
                r = lax.rsqrt(jnp.mean(x * x, axis=-1, keepdims=True) + RMS_EPS)
                xh = x * r
                dn_ = dn_ref[:, sl] if mul is None else dn_ref[:, sl] * mul
                dxh = dn_ * g_ref[...]
                do_ref[:, at + h * cb:at + (h + 1) * cb] = (
                    r * (dxh - xh * jnp.mean(dxh * xh, axis=-1, keepdims=True))).astype(bf16)
                part = part + jnp.sum(dn_ * xh, axis=0, keepdims=True)

            @pl.when(first)
            def _(dg_ref=dg_ref, part=part):
                dg_ref[...] = part

            @pl.when(jnp.logical_not(first))
            def _(dg_ref=dg_ref, part=part):
                dg_ref[...] += part

    blk = pl.BlockSpec((rt, SB_W), lambda i: (i, 0))
    vec = pl.BlockSpec((1, cb), lambda i: (0, 0))
    return pl.pallas_call(
        body, name="sb_prep_bwd", grid=(tp // rt,),
        in_specs=[pl.BlockSpec((rt, SB_W), lambda i: (i, C_SQ)), pl.BlockSpec((rt, SB_W), lambda i: (i, C_SK)),
                  vec, vec, blk, blk, blk, pl.BlockSpec(memory_space=pl.ANY)],
        out_specs=[pl.BlockSpec((rt, 3 * SB_W), lambda i: (i, C_SQ * 1024 // (3 * SB_W))), vec, vec],
        out_shape=[jax.ShapeDtypeStruct(into.shape, into.dtype)] + [jax.ShapeDtypeStruct((1, cb), f32)] * 2,
        input_output_aliases={7: 0}, compiler_params=_params(1),
    )(proj, proj, gq, gk, dqs, dks, dvs, into)


def _conv_taps(ext, rt):
    taps = []
    for k in range(DN_CONV):
        s = DN_CONV - 1 - k
        taps.append((pltpu.roll(ext, s, axis=0) if s else ext)[SUB:SUB + rt])
    return taps


def _conv_act(taps, w, l2):
    y = taps[0] * w[0:1]
    for k in range(1, DN_CONV):
        y = y + taps[k] * w[k:k + 1]
    s = _sigmoid(y)
    a = y * s
    if l2:
        n = lax.rsqrt(jnp.sum(a * a, axis=-1, keepdims=True) + L2_EPS)
        return y, s, a, n
    return y, s, a, None


def conv_fwd(proj, w8, col_blk, ncb, l2, name):
    tp = proj.shape[0]
    rt = _row_tile(tp)
    hb = rt // SUB
    cw = min(CONV_W, ncb * LANE)
    cb0 = col_blk * LANE // cw

    def body(x_ref, h_ref, w_ref, o_ref):
        i = pl.program_id(1)
        first = (i > 0).astype(f32)
        for s in range(cw // LANE):
            sl = slice(s * LANE, (s + 1) * LANE)
            ext = jnp.concatenate([h_ref[:, sl] * first, x_ref[:, sl]], axis=0)
            _, _, a, n = _conv_act(_conv_taps(ext, rt), w_ref[:, sl], l2)
            o_ref[:, sl] = a * n if l2 else a

    return pl.pallas_call(
        body, name=name, grid=(ncb * LANE // cw, tp // rt),
        in_specs=[pl.BlockSpec((rt, cw), lambda j, i: (i, j + cb0)),
                  pl.BlockSpec((SUB, cw), lambda j, i: (jnp.maximum(i * hb - 1, 0), j + cb0)),
                  pl.BlockSpec((SUB, cw), lambda j, i: (0, j))],
        out_specs=pl.BlockSpec((rt, cw), lambda j, i: (i, j)),
        out_shape=jax.ShapeDtypeStruct((tp, ncb * LANE), f32), compiler_params=_params(2),
    )(proj, proj, w8)


def conv_bwd(proj, w8, dout, into, col_blk, ncb, l2, name):
    tp = proj.shape[0]
    rt = _row_tile(tp)
    hb = rt // SUB
    nr = tp // rt
    last8 = tp // SUB - 1
    cw = min(CONV_W, ncb * LANE)
    cb0 = col_blk * LANE // cw
    n = rt + SUB

    def body(x_ref, xb_ref, xf_ref, w_ref, d_ref, df_ref, into_ref, o_ref, dw_ref):
        i = pl.program_id(1)
        first = (i > 0).astype(f32)
        last = (i < nr - 1).astype(f32)
        rows = lax.broadcasted_iota(jnp.int32, (SUB, LANE), 0)
        for s in range(cw // LANE):
            sl = slice(s * LANE, (s + 1) * LANE)
            ext = jnp.concatenate([xb_ref[:, sl] * first, x_ref[:, sl], xf_ref[:, sl] * last], axis=0)
            taps = _conv_taps(ext, n)
            w = w_ref[:, sl]
            y, sg, a, nrm = _conv_act(taps, w, l2)
            da = jnp.concatenate([d_ref[:, sl], df_ref[:, sl] * last], axis=0)
            if l2:
                out = a * nrm
                da = nrm * (da - out * jnp.sum(da * out, axis=-1, keepdims=True))
            dy = da * sg * (1.0 + y * (1.0 - sg))
            part = jnp.zeros((SUB, LANE), f32)
            for k in range(DN_CONV):
                part = part + jnp.where(rows == k, jnp.sum(taps[k][0:rt] * dy[0:rt], axis=0, keepdims=True), 0.0)

            @pl.when(i == 0)
            def _(sl=sl, part=part):
                dw_ref[:, sl] = part

            @pl.when(i > 0)
            def _(sl=sl, part=part):
                dw_ref[:, sl] += part

            acc = None
            for k in range(DN_CONV):
                up = DN_CONV - 1 - k
                term = (pltpu.roll(dy, n - up, axis=0) if up else dy)[0:rt] * w[k:k + 1]
                acc = term if acc is None else acc + term
            o_ref[:, sl] = acc.astype(bf16)

    after = lambda j, i: (jnp.minimum((i + 1) * hb, last8), j)
    return pl.pallas_call(
        body, name=name, grid=(ncb * LANE // cw, nr),
        in_specs=[pl.BlockSpec((rt, cw), lambda j, i: (i, j + cb0)),
                  pl.BlockSpec((SUB, cw), lambda j, i: (jnp.maximum(i * hb - 1, 0), j + cb0)),
                  pl.BlockSpec((SUB, cw), lambda j, i: (jnp.minimum((i + 1) * hb, last8), j + cb0)),
                  pl.BlockSpec((SUB, cw), lambda j, i: (0, j)),
                  pl.BlockSpec((rt, cw), lambda j, i: (i, j)), pl.BlockSpec((SUB, cw), after),
                  pl.BlockSpec(memory_space=pl.ANY)],
        out_specs=[pl.BlockSpec((rt, cw), lambda j, i: (i, j + cb0)), pl.BlockSpec((SUB, cw), lambda j, i: (0, j))],
        out_shape=[jax.ShapeDtypeStruct(into.shape, into.dtype), jax.ShapeDtypeStruct((SUB, ncb * LANE), f32)],
        input_output_aliases={6: 0}, compiler_params=_params(2),
    )(proj, proj, proj, w8, dout, dout, into)


def _ab_common(p, al, dtb, r0):
    rows = r0 + lax.broadcasted_iota(jnp.int32, p.shape, 0)
    mask = (rows >= P0).astype(f32)
    xx = p + dtb
    sp = jnp.maximum(xx, 0.0) + _log1p_small(jnp.exp(-jnp.abs(xx)))
    ea = jnp.exp(al)
    g = -ea * sp * mask
    beta = _sigmoid(p) * mask
    return g, beta, _sigmoid(xx), ea, mask


def _chunk_tri(rt, later):
    r = lax.broadcasted_iota(jnp.int32, (rt, rt), 0)
    c = lax.broadcasted_iota(jnp.int32, (rt, rt), 1)
    shift = CHUNK.bit_length() - 1
    same = jnp.right_shift(r, shift) == jnp.right_shift(c, shift)
    return jnp.logical_and(same, c >= r if later else c <= r).astype(f32)


def ab_fwd(pab, al, dtb):
    tp = pab.shape[0]
    rt = _row_tile(tp)
    assert rt % CHUNK == 0

    def body(p_ref, al_ref, dt_ref, g_ref, b_ref):
        i = pl.program_id(0)
        g, beta, _, _, _ = _ab_common(p_ref[...], al_ref[...], dt_ref[...], i * rt)
        gam = _hdot(_chunk_tri(rt, False), g)
        for h in range(HEADS):
            g_ref[h] = jnp.broadcast_to(gam[:, h:h + 1], (rt, LANE))
            b_ref[h] = jnp.broadcast_to(beta[:, HEADS + h:HEADS + h + 1], (rt, LANE))

    vec = pl.BlockSpec((1, LANE), lambda i: (0, 0))
    out = pl.BlockSpec((HEADS, rt, LANE), lambda i: (0, i, 0))
    return pl.pallas_call(
        body, name="ab_fwd", grid=(tp // rt,), in_specs=[pl.BlockSpec((rt, LANE), lambda i: (i, 0)), vec, vec],
        out_specs=[out, out], out_shape=[jax.ShapeDtypeStruct((HEADS, tp, LANE), f32)] * 2, compiler_params=_params(1),
    )(pab, al, dtb)


def ab_bwd(pab, al, dtb, dg, db):
    tp = pab.shape[0]
    rt = _row_tile(tp)

    def body(p_ref, al_ref, dt_ref, dg_ref, db_ref, dp_ref, dal_ref, ddt_ref):
        i = pl.program_id(0)
        g, beta, sx, ea, mask = _ab_common(p_ref[...], al_ref[...], dt_ref[...], i * rt)
        lanes = lax.broadcasted_iota(jnp.int32, (rt, LANE), 1)
        dgl = jnp.zeros((rt, LANE), f32)
        dbl = jnp.zeros((rt, LANE), f32)
        for h in range(HEADS):
            dgl = dgl + jnp.where(lanes == h, dg_ref[h], 0.0)
            dbl = dbl + jnp.where(lanes == HEADS + h, db_ref[h], 0.0)
        dgl = _hdot(_chunk_tri(rt, True), dgl)
        dxx = dgl * (-ea) * sx * mask
        dp_ref[...] = (dxx + dbl * beta * (1.0 - beta)).astype(bf16)
        pal = jnp.sum(dgl * g, axis=0, keepdims=True)
        pdt = jnp.sum(dxx, axis=0, keepdims=True)

        @pl.when(i == 0)
        def _():
            dal_ref[...] = pal
            ddt_ref[...] = pdt

        @pl.when(i > 0)
        def _():
            dal_ref[...] += pal
            ddt_ref[...] += pdt

    vec = pl.BlockSpec((1, LANE), lambda i: (0, 0))
    row = pl.BlockSpec((rt, LANE), lambda i: (i, 0))
    big = pl.BlockSpec((HEADS, rt, LANE), lambda i: (0, i, 0))
    return pl.pallas_call(
        body, name="ab_bwd", grid=(tp // rt,), in_specs=[row, vec, vec, big, big], out_specs=[row, vec, vec],
        out_shape=[jax.ShapeDtypeStruct((tp, LANE), bf16), jax.ShapeDtypeStruct((1, LANE), f32),
                   jax.ShapeDtypeStruct((1, LANE), f32)],
        compiler_params=_params(1),
    )(pab, al, dtb, dg, db)


class _Chunk:
    pass


def _gdn_chunk(q, k, v, gcol, bcol, grow8):
    C = CHUNK
    R = range(len(q))
    X = _Chunk()
    ri = lax.broadcasted_iota(jnp.int32, (C, C), 0)
    ci = lax.broadcasted_iota(jnp.int32, (C, C), 1)
    eye = (ri == ci).astype(f32)
    gam = list(gcol)
    gam_row = [grow8[h][0:1, 0:C] for h in R]
    X.ri, X.ci = ri, ci
    X.Dm = [jnp.where(ri >= ci, jnp.exp(jnp.minimum(gam[h][:, 0:C] - gam_row[h], 0.0)), 0.0) for h in R]
    X.eg = [jnp.exp(gam[h]) for h in R]
    gl = [gam[h][C - 1:C, :] for h in R]
    X.egl = [jnp.exp(gl[h]) for h in R]
    X.kdec = [jnp.exp(gl[h] - gam[h]) for h in R]
    X.qs = [q[h] * (DN_DK ** -0.5) for h in R]
    X.kb = [k[h] * bcol[h] for h in R]
    kk = [_dot_nt(X.kb[h], k[h]) for h in R]
    qk = [_dot_nt(X.qs[h], k[h]) for h in R]
    X.A = [jnp.where(ri > ci, kk[h] * X.Dm[h], 0.0) for h in R]
    assert C == 64
    b16 = jnp.right_shift(ri, 4) == jnp.right_shift(ci, 4)
    b32 = jnp.right_shift(ri, 5) == jnp.right_shift(ci, 5)
    P = [jnp.where(b16, X.A[h], 0.0) for h in R]
    T = [eye - P[h] for h in R]
    for _ in range(3):
        P = [_hdot(P[h], P[h]) for h in R]
        T = [T[h] + _hdot(T[h], P[h]) for h in R]
    for off in (jnp.logical_and(b32, jnp.logical_not(b16)), jnp.logical_not(b32)):
        AT = [_hdot(jnp.where(off, X.A[h], 0.0), T[h]) for h in R]
        T = [T[h] - _hdot(T[h], AT[h]) for h in R]
    X.T = T
    X.b2 = [jnp.concatenate([bcol[h], bcol[h]], axis=-1) for h in R]
    X.u = [_hdot(T[h], v[h] * X.b2[h]) for h in R]
    X.w = [_hdot(T[h], X.kb[h] * X.eg[h]) for h in R]
    X.attn = [qk[h] * X.Dm[h] for h in R]
    X.qg = [X.qs[h] * X.eg[h] for h in R]
    X.kg = [k[h] * X.kdec[h] for h in R]
    return X


def gdn_fwd(q, k, v, gc, bc, grow):
    tp = q.shape[0]
    nc = tp // CHUNK
    hb = GDN_HEADS_PER_STEP

    def body(q_ref, k_ref, v_ref, gc_ref, bc_ref, gr_ref, o_ref, ss_ref, S_ref):
        c = pl.program_id(1)

        @pl.when(c == 0)
        def _():
            S_ref[...] = jnp.zeros_like(S_ref)

        R = range(hb)
        qc = [slice(h * DN_DK, (h + 1) * DN_DK) for h in R]
        vc = [slice(h * DN_DV, (h + 1) * DN_DV) for h in R]
        X = _gdn_chunk([q_ref[:, qc[h]] for h in R], [k_ref[:, qc[h]] for h in R], [v_ref[:, vc[h]] for h in R],
                       [gc_ref[h] for h in R], [bc_ref[h] for h in R], [gr_ref[h] for h in R])
        S = [S_ref[h] for h in R]
        for h in R:
            ss_ref[h, 0] = S[h]
        wS = [_dot(X.w[h], S[h]) for h in R]
        qS = [_dot(X.qg[h], S[h]) for h in R]
        vn = [X.u[h] - wS[h] for h in R]
        av = [_dot(X.attn[h], vn[h]) for h in R]
        kv = [_dot_tn(X.kg[h], vn[h]) for h in R]
        for h in R:
            o_ref[:, vc[h]] = qS[h] + av[h]
            S_ref[h] = S[h] * X.egl[h][:, 0:1] + kv[h]

    qk = pl.BlockSpec((CHUNK, hb * DN_DK), lambda g, c: (c, g))
    vv = pl.BlockSpec((CHUNK, hb * DN_DV), lambda g, c: (c, g))
    col = pl.BlockSpec((hb, CHUNK, LANE), lambda g, c: (g, c, 0))
    row = pl.BlockSpec((hb, SUB, LANE), lambda g, c: (g, c, 0))
    return pl.pallas_call(
        body, name="gdn_fwd", grid=(HEADS // hb, nc), in_specs=[qk, qk, vv, col, col, row],
        out_specs=[vv, pl.BlockSpec((hb, 1, DN_DK, DN_DV), lambda g, c: (g, c, 0, 0))],
        out_shape=[jax.ShapeDtypeStruct((tp, DN_V), f32), jax.ShapeDtypeStruct((HEADS, nc, DN_DK, DN_DV), f32)],
        scratch_shapes=[pltpu.VMEM((hb, DN_DK, DN_DV), f32)], compiler_params=_params(2),
    )(q, k, v, gc, bc, grow)


def gdn_bwd(q, k, v, gc, bc, grow, states, do, rider=None):
    tp = q.shape[0]
    nc = tp // CHUNK
    C = CHUNK
    hb = GDN_HEADS_PER_STEP
    grid = (HEADS // hb, nc)
    split, ride_first, ride_last = _ride(rider, 8, 5, grid)

    def body(*refs):
        ((q_ref, k_ref, v_ref, gc_ref, bc_ref, gr_ref, ss_ref, do_ref), (dq_ref, dk_ref, dv_ref, dg_ref, db_ref),
         (rin, rout, rest)) = split(refs)
        dS_ref, ride = rest[0], (rin, rout, rest[1:])
        ride_first(ride)
        c = pl.program_id(1)

        @pl.when(c == 0)
        def _():
            dS_ref[...] = jnp.zeros_like(dS_ref)

        R = range(hb)
        qc = [slice(h * DN_DK, (h + 1) * DN_DK) for h in R]
        vc = [slice(h * DN_DV, (h + 1) * DN_DV) for h in R]
        k_ = [k_ref[:, qc[h]] for h in R]
        v_ = [v_ref[:, vc[h]] for h in R]
        bcol = [bc_ref[h] for h in R]
        X = _gdn_chunk([q_ref[:, qc[h]] for h in R], k_, v_, [gc_ref[h] for h in R], bcol, [gr_ref[h] for h in R])
        ri, ci = X.ri, X.ci
        S = [ss_ref[h, 0] for h in R]
        do_ = [do_ref[:, vc[h]] for h in R]
        dSn = [dS_ref[h] for h in R]
        wS = [_dot(X.w[h], S[h]) for h in R]
        ado = [_dot_tn(X.attn[h], do_[h]) for h in R]
        kdS = [_dot(X.kg[h], dSn[h]) for h in R]
        d_qg = [_dot_nt(do_[h], S[h]) for h in R]
        qdo = [_dot_tn(X.qg[h], do_[h]) for h in R]
        vn = [X.u[h] - wS[h] for h in R]
        d_vn = [ado[h] + kdS[h] for h in R]
        dovn = [_dot_nt(do_[h], vn[h]) for h in R]
        d_kg = [_dot_nt(vn[h], dSn[h]) for h in R]
        wdv = [_dot_tn(X.w[h], d_vn[h]) for h in R]
        dw = [-_dot_nt(d_vn[h], S[h]) for h in R]
        for h in R:
            dS_ref[h] = qdo[h] + X.egl[h][:, 0:1] * dSn[h] - wdv[h]
        dattn = [jnp.where(ri >= ci, dovn[h], 0.0) for h in R]
        dRu = [_hdot_tn(X.T[h], d_vn[h]) for h in R]
        dRw = [_hdot_tn(X.T[h], dw[h]) for h in R]
        dAu = [_hdot_nt(dRu[h], X.u[h]) for h in R]
        dAw = [_hdot_nt(dRw[h], X.w[h]) for h in R]
        dA = [jnp.where(ri > ci, -(dAu[h] + dAw[h]), 0.0) for h in R]
        dKK = [dA[h] * X.Dm[h] for h in R]
        dQK = [dattn[h] * X.Dm[h] for h in R]
        E = [dA[h] * X.A[h] + dattn[h] * X.attn[h] for h in R]
        dkb = [_dot(dKK[h], k_[h]) + dRw[h] * X.eg[h] for h in R]
        dk1 = [_dot_tn(dKK[h], X.kb[h]) for h in R]
        dqs = [_dot(dQK[h], k_[h]) + d_qg[h] * X.eg[h] for h in R]
        dk2 = [_dot_tn(dQK[h], X.qs[h]) for h in R]
        ones = jnp.ones((C, LANE), f32)
        colE = [_hdot_tn(E[h], ones) for h in R]
        rows = lax.broadcasted_iota(jnp.int32, (C, LANE), 0)
        dgam = []
        for h in R:
            t = d_kg[h] * X.kg[h]
            dgl = _allsum(t) + X.egl[h][:, 0:1] * _allsum(S[h] * dSn[h])
            g = (_rowsum(E[h]) - colE[h] + _rowsum(dRw[h] * (X.kb[h] * X.eg[h])) + _rowsum(d_qg[h] * X.qg[h])
                 - _rowsum(t))
            dgam.append(g + jnp.where(rows == C - 1, dgl, 0.0))
        for h in R:
            dv_ref[:, vc[h]] = dRu[h] * X.b2[h]
            dbeta = _rowsum(dRu[h] * v_[h]) + _rowsum(dkb[h] * k_[h])
            dq_ref[:, qc[h]] = dqs[h] * (DN_DK ** -0.5)
            dk_ref[:, qc[h]] = dk1[h] + dk2[h] + dkb[h] * bcol[h] + d_kg[h] * X.kdec[h]
            dg_ref[h] = dgam[h]
            db_ref[h] = jnp.broadcast_to(dbeta, (C, LANE))
        ride_last(ride)

    rc = lambda c: nc - 1 - c
    qk = pl.BlockSpec((CHUNK, hb * DN_DK), lambda g, c: (rc(c), g))
    vv = pl.BlockSpec((CHUNK, hb * DN_DV), lambda g, c: (rc(c), g))
    col = pl.BlockSpec((hb, CHUNK, LANE), lambda g, c: (g, rc(c), 0))
    row = pl.BlockSpec((hb, SUB, LANE), lambda g, c: (g, rc(c), 0))
    st = pl.BlockSpec((hb, 1, DN_DK, DN_DV), lambda g, c: (g, rc(c), 0, 0))
    r_ins = rider.ins if rider else []
    r_outs = rider.out_shapes if rider else []
    res = pl.pallas_call(
        body, name="gdn_bwd", grid=grid, in_specs=[qk, qk, vv, col, col, row, st, vv] + [_ANY] * len(r_ins),
        out_specs=[qk, qk, vv, col, col] + [_ANY] * len(r_outs),
        out_shape=[jax.ShapeDtypeStruct((tp, DN_QK), f32), jax.ShapeDtypeStruct((tp, DN_QK), f32),
                   jax.ShapeDtypeStruct((tp, DN_V), f32), jax.ShapeDtypeStruct((HEADS, tp, LANE), f32),
                   jax.ShapeDtypeStruct((HEADS, tp, LANE), f32)] + list(r_outs),
        scratch_shapes=[pltpu.VMEM((hb, DN_DK, DN_DV), f32)] + (rider.scratch if rider else []),
        compiler_params=_params(2),
    )(q, k, v, gc, bc, grow, states, do, *r_ins)
    return res[:5], res[5:]


def _cumsum_after(x, nb, us, pieces=2):
    B, n = SB_BLOCK, x.shape[0]
    hi = x.astype(bf16)
    parts = (hi, (x - hi.astype(f32)).astype(bf16)) if pieces == 2 else (hi,)
    rows = [p[:, b * B:(b + 1) * B] for p in parts for b in range(nb)]
    r = jnp.dot(jnp.concatenate(rows, axis=0), us, preferred_element_type=f32)
    out = [r[b * n:(b + 1) * n] for b in range(nb)]
    if pieces == 2:
        out = [out[b] + r[(nb + b) * n:(nb + b + 1) * n] for b in range(nb)]
    return out[0] if nb == 1 else jnp.concatenate(out, axis=1)


def _later_blocks(x, nb, carry):
    B = SB_BLOCK
    tot = [_rowsum(x[:, b * B:(b + 1) * B]) for b in range(nb)]
    offs = [None] * nb
    run = carry
    for b in range(nb - 1, -1, -1):
        offs[b] = jnp.broadcast_to(run, (x.shape[0], B))
        run = run + tot[b]
    return (offs[0] if nb == 1 else jnp.concatenate(offs, axis=1)), run


def _sb_group(i, t):
    top = (i + 1) * (SB_QB // SB_BLOCK) - 1 - SB_GROUP * t
    jlo = jnp.maximum(top - SB_GROUP + 1, 0)
    rows = pl.ds(pl.multiple_of(jlo * SB_BLOCK, SB_BLOCK), SB_GROUP * SB_BLOCK)
    return jlo, rows, (top + 1) * SB_BLOCK


def _sb_weights(q, kcat, i, jlo, kend, cs, us, masked):
    B, nb = SB_BLOCK, SB_GROUP
    R = range(len(q))
    z = [_dot_nt(q[h], kcat[h]) * (SB_DH ** -0.5) for h in R]
    e = [jnp.exp(-jnp.abs(z[h])) for h in R]
    l1p = [jnp.log(1.0 + e[h]) for h in R]
    lsp = [jnp.minimum(z[h], 0.0) - l1p[h] for h in R]
    lk = [lsp[h] - z[h] for h in R]
    vis = None
    if masked:
        qpos = i * SB_QB + lax.broadcasted_iota(jnp.int32, (SB_QB, nb * B), 0)
        kpos = jlo * B + lax.broadcasted_iota(jnp.int32, (SB_QB, nb * B), 1)
        vis = jnp.logical_and(kpos < jnp.minimum(qpos, kend), kpos >= P0)
        lk = [jnp.where(vis, lk[h], 0.0) for h in R]
    later = [_later_blocks(lk[h], nb, cs[h]) for h in R]
    cum = [_cumsum_after(lk[h], nb, us) for h in R]
    w = [jnp.exp(lsp[h] + cum[h] + later[h][0]) for h in R]
    if masked:
        w = [jnp.where(vis, w[h], 0.0) for h in R]
    return lsp, vis, w, [later[h][1] for h in R]


def _sb_loop(i, step, carry):
    trips = ((i + 1) * (SB_QB // SB_BLOCK) - 1 + SB_GROUP) // SB_GROUP
    carry = step(True)(0, carry)
    carry = lax.fori_loop(1, trips - 1, step(False), carry)
    return lax.fori_loop(jnp.maximum(trips - 1, 1), trips, step(True), carry)


def _ride(rider, n_in, n_out, grid):
    n_rin = len(rider.ins) if rider else 0
    n_rout = len(rider.out_shapes) if rider else 0

    def split(refs):
        ins, rin = refs[:n_in], refs[n_in:n_in + n_rin]
        outs = refs[n_in + n_rin:n_in + n_rin + n_out]
        rout = refs[n_in + n_rin + n_out:n_in + n_rin + n_out + n_rout]
        return ins, outs, (rin, rout, refs[n_in + n_rin + n_out + n_rout:])

    def at(step, fn, r):
        if rider is None:
            return
        cond = None
        for a, g in enumerate(grid):
            c = pl.program_id(a) == (g - 1 if step == "last" else 0)
            cond = c if cond is None else jnp.logical_and(cond, c)

        @pl.when(cond)
        def _():
            fn(*r)

    first = lambda r: at("first", rider.start if rider else None, r)
    last = lambda r: at("last", rider.finish if rider else None, r)
    return split, first, last


def sb_fwd(qs, ks, vs, rider=None):
    tp = qs.shape[0]
    nq = tp // SB_QB
    B, G, hb, QB = SB_BLOCK, SB_GROUP, SB_FWD_HEADS_PER_STEP, SB_QB
    assert tp >= G * B and tp % QB == 0 and QB % B == 0 and G * B >= QB
    grid = (HEADS // hb, nq)
    split, ride_first, ride_last = _ride(rider, 3, 2, grid)

    def body(*refs):
        (q_ref, k_ref, v_ref), (o_ref, ob_ref), ride = split(refs)
        ride_first(ride)
        i = pl.program_id(1)
        R = range(hb)
        hs = [slice(h * SB_DH, (h + 1) * SB_DH) for h in R]
        q = [q_ref[:, hs[h]] for h in R]
        us = (lax.broadcasted_iota(jnp.int32, (B, B), 0) > lax.broadcasted_iota(jnp.int32, (B, B), 1)).astype(bf16)

        def make_step(masked):
            def step(t, carry):
                acc, cs = carry
                jlo, rows, kend = _sb_group(i, t)
                _, _, w, cs = _sb_weights(q, [k_ref[rows, hs[h]] for h in R], i, jlo, kend, cs, us, masked)
                pv = [_dot(w[h], v_ref[rows, hs[h]]) for h in R]
                return tuple(acc[h] + pv[h] for h in R), tuple(cs)
            return step

        carry = (tuple(jnp.zeros((QB, SB_DH), f32) for _ in R), tuple(jnp.zeros((QB, 1), f32) for _ in R))
        acc, _ = _sb_loop(i, make_step, carry)
        for h in R:
            o_ref[:, hs[h]] = acc[h]
            ob_ref[:, hs[h]] = acc[h].astype(bf16)
        ride_last(ride)

    blk = pl.BlockSpec((QB, hb * SB_DH), lambda g, i: (i, g))
    full = pl.BlockSpec((tp, hb * SB_DH), lambda g, i: (0, g))
    r_ins = rider.ins if rider else []
    r_outs = rider.out_shapes if rider else []
    res = pl.pallas_call(
        body, name="sb_fwd", grid=grid, in_specs=[blk, full, full] + [_ANY] * len(r_ins),
        out_specs=[blk, blk] + [_ANY] * len(r_outs),
        out_shape=[jax.ShapeDtypeStruct((tp, SB_W), f32), jax.ShapeDtypeStruct((tp, SB_W), bf16)] + list(r_outs),
        scratch_shapes=rider.scratch if rider else [], compiler_params=_params(2),
    )(qs, ks, vs, *r_ins)
    return res[0], res[1], res[2:]


def sb_bwd(qs, ks, vs, o, do, rider=None):
    tp = qs.shape[0]
    nq = tp // SB_QB
    B, G, hb, QB = SB_BLOCK, SB_GROUP, SB_HEADS_PER_STEP, SB_QB
    assert tp >= G * B and tp % QB == 0 and QB % B == 0 and G * B >= QB
    grid = (HEADS // hb, nq)
    split, ride_first, ride_last = _ride(rider, 5, 3, grid)

    def body(*refs):
        (q_ref, k_ref, v_ref, o_ref, do_ref), (dq_ref, dk_ref, dv_ref), ride = split(refs)
        ride_first(ride)
        i = pl.program_id(1)

        @pl.when(i == 0)
        def _():
            dk_ref[...] = jnp.zeros_like(dk_ref)
            dv_ref[...] = jnp.zeros_like(dv_ref)

        R = range(hb)
        hs = [slice(h * SB_DH, (h + 1) * SB_DH) for h in R]
        q = [q_ref[:, hs[h]] for h in R]
        dob = [do_ref[:, hs[h]].astype(bf16) for h in R]
        et = [_rowsum(dob[h].astype(f32) * o_ref[:, hs[h]]) for h in R]
        us = (lax.broadcasted_iota(jnp.int32, (B, B), 0) > lax.broadcasted_iota(jnp.int32, (B, B), 1)).astype(bf16)

        def make_step(masked):
            def step(t, carry):
                dq, cs, ce = carry
                jlo, rows, kend = _sb_group(i, t)
                kcat = [k_ref[rows, hs[h]] for h in R]
                dwv = [_dot_nt(dob[h], v_ref[rows, hs[h]]) for h in R]
                lsp, vis, w, cs = _sb_weights(q, kcat, i, jlo, kend, cs, us, masked)
                wb = [w[h].astype(bf16) for h in R]
                ee = [dwv[h] * wb[h].astype(f32) for h in R]
                later = [_later_blocks(ee[h], G, ce[h]) for h in R]
                cum = [_cumsum_after(ee[h], G, us) for h in R]
                dz = []
                for h in R:
                    d = ee[h] - jnp.exp(lsp[h]) * (et[h] - (cum[h] + later[h][0]))
                    if masked:
                        d = jnp.where(vis, d, 0.0)
                    dz.append(d.astype(bf16))
                dkj = [_dot_tn(dz[h], q[h]) for h in R]
                dvj = [_dot_tn(wb[h], dob[h]) for h in R]
                dqj = [_dot(dz[h], kcat[h]) for h in R]
                for h in R:
                    dk_ref[rows, hs[h]] += dkj[h]
                    dv_ref[rows, hs[h]] += dvj[h]
                return tuple(dq[h] + dqj[h] for h in R), tuple(cs), tuple(later[h][1] for h in R)
            return step

        z0 = tuple(jnp.zeros((QB, 1), f32) for _ in R)
        dq, _, _ = _sb_loop(i, make_step, (tuple(jnp.zeros((QB, SB_DH), f32) for _ in R), z0, z0))
        for h in R:
            dq_ref[:, hs[h]] = dq[h] * (SB_DH ** -0.5)
        ride_last(ride)

    blk = pl.BlockSpec((QB, hb * SB_DH), lambda g, i: (i, g))
    full = pl.BlockSpec((tp, hb * SB_DH), lambda g, i: (0, g))
    r_ins = rider.ins if rider else []
    r_outs = rider.out_shapes if rider else []
    res = pl.pallas_call(
        body, name="sb_bwd", grid=grid, in_specs=[blk, full, full, blk, blk] + [_ANY] * len(r_ins),
        out_specs=[blk, full, full] + [_ANY] * len(r_outs),
        out_shape=[jax.ShapeDtypeStruct((tp, SB_W), f32)] * 3 + list(r_outs),
        scratch_shapes=rider.scratch if rider else [], compiler_params=_params(2),
    )(qs, ks, vs, o, do, *r_ins)
    return res[:3], res[3:]


def adamw(w, g, m, v, name, rider=None):
    r, c = w.shape
    rt = _tile(r, 128, SUB) if r % SUB == 0 else r
    blk = pl.BlockSpec((rt, c), lambda i: (i, 0))
    c1 = 1.0 - ADAM_B1 ** ADAM_STEP
    c2 = 1.0 - ADAM_B2 ** ADAM_STEP
    grid = (r // rt,)
    split, ride_first, ride_last = _ride(rider, 4, 3, grid)

    def body(*refs):
        (w_ref, g_ref, m_ref, v_ref), (d_ref, mo_ref, vo_ref), ride = split(refs)
        ride_first(ride)
        g_ = g_ref[...]
        m_ = ADAM_B1 * m_ref[...] + (1.0 - ADAM_B1) * g_
        v_ = ADAM_B2 * v_ref[...] + (1.0 - ADAM_B2) * (g_ * g_)
        mo_ref[...] = m_
        vo_ref[...] = v_
        d_ref[...] = -ADAM_LR * ((m_ / c1) / (jnp.sqrt(v_ / c2) + ADAM_EPS) + ADAM_WD * w_ref[...])
        ride_last(ride)

    r_ins = rider.ins if rider else []
    r_outs = rider.out_shapes if rider else []
    res = pl.pallas_call(
        body, name=name, grid=grid, in_specs=[blk] * 4 + [_ANY] * len(r_ins), out_specs=[blk] * 3 + [_ANY] * len(r_outs),
        out_shape=[jax.ShapeDtypeStruct((r, c), f32)] * 3 + list(r_outs),
        scratch_shapes=rider.scratch if rider else [], compiler_params=_params(1),
    )(w, g, m, v, *r_ins)
    return res[:3], res[3:]


def sum_slots(x, name):
    n, r, c = x.shape
    rt = _tile(r, 128, SUB) if r % SUB == 0 else r
    blk = pl.BlockSpec((n, rt, c), lambda i: (0, i, 0))

    def body(x_ref, o_ref):
        acc = x_ref[0].astype(f32)
        for s in range(1, n):
            acc = acc + x_ref[s].astype(f32)
        o_ref[...] = acc

    return pl.pallas_call(
        body, name=name, grid=(r // rt,), in_specs=[blk], out_specs=pl.BlockSpec((rt, c), lambda i: (i, 0)),
        out_shape=jax.ShapeDtypeStruct((r, c), f32), compiler_params=_params(1),
    )(x)


def sum_chip_parts(parts, slots, chip, name):
    n, r, c = slots.shape
    rt = _tile(r, 128, SUB) if r % SUB == 0 else r

    def body(chip_ref, own_ref, a_ref, b_ref, c_ref, o_ref):
        o_ref[...] = ((own_ref[0].astype(f32) + a_ref[0].astype(f32)) + b_ref[0].astype(f32)) + c_ref[0].astype(f32)

    def at(rel):
        return pl.BlockSpec((1, rt, c), lambda i, chip_ref: (jnp.bitwise_xor(chip_ref[0], rel), i, 0))

    return pl.pallas_call(
        body, name=name,
        grid_spec=pltpu.PrefetchScalarGridSpec(
            num_scalar_prefetch=1, grid=(r // rt,), in_specs=[at(0), at(1), at(2), at(3)],
            out_specs=pl.BlockSpec((rt, c), lambda i, chip_ref: (i, 0))),
        out_shape=jax.ShapeDtypeStruct((r, c), f32), compiler_params=_params(1),
    )(jnp.reshape(chip, (1,)).astype(jnp.int32), parts, slots, slots, slots)


def add2(a, b, name, out_dtype=f32):
    n, r, c = a.shape
    rt = _tile(r, 64, SUB) if r % SUB == 0 else r
    blk = pl.BlockSpec((n, rt, c), lambda i: (0, i, 0))

    def body(a_ref, b_ref, o_ref):
        o_ref[...] = (a_ref[...] + b_ref[...]).astype(out_dtype)

    return pl.pallas_call(
        body, name=name, grid=(r // rt,), in_specs=[blk, blk], out_specs=blk,
        out_shape=jax.ShapeDtypeStruct((n, r, c), out_dtype), compiler_params=_params(1),
    )(a, b)


_ANY = pl.BlockSpec(memory_space=pl.ANY)
_MESH = pl.DeviceIdType.MESH


def _coords():
    return lax.axis_index("x"), lax.axis_index("y"), lax.axis_index("c")


def _chip_peer(x, y, r):
    return x ^ (r >> 1), y ^ (r & 1)


class _Exchange:
    def __init__(self, ins, out_shapes, scratch, start, finish):
        self.ins, self.out_shapes, self.scratch, self.start, self.finish = ins, out_shapes, scratch, start, finish

    def split(self, refs):
        n, m = len(self.ins), len(self.out_shapes)
        return refs[:n], refs[n:n + m], refs[n + m:]


def run_exchange(ex, name):
    def body(*refs):
        ins, outs, sems = ex.split(refs)
        ex.start(ins, outs, sems)
        ex.finish(ins, outs, sems)

    return pl.pallas_call(body, name=name, in_specs=[_ANY] * len(ex.ins), out_specs=[_ANY] * len(ex.out_shapes),
                          out_shape=ex.out_shapes, scratch_shapes=ex.scratch)(*ex.ins)


def gather_chips(big, small):
    nb, n = len(big), len(big) + len(small)
    shards = list(big) + list(small)
    kb = nb * (N_CHIPS - 1)
    k = n * (N_CHIPS - 1)

    def own(src, dst, sems, t):
        x, y, c = _coords()
        return pltpu.make_async_remote_copy(src[t], dst[t].at[2 * x + y], sems[4].at[t], sems[5].at[t],
                                            device_id=(x, y, 1 - c), device_id_type=_MESH)

    def copies(src, dst, sems):
        send, recv, fsend, frecv = sems[:4]
        x, y, c = _coords()
        sib = (x, y, 1 - c)
        peers = [_chip_peer(x, y, r) for r in range(1, N_CHIPS)]

        def direct(t, j, slot):
            s = t * (N_CHIPS - 1) + j
            if t < nb:
                return pltpu.make_async_remote_copy(src[t].at[c], dst[t].at[slot, c], send.at[s], recv.at[s],
                                                    device_id=(*peers[j], c), device_id_type=_MESH)
            return pltpu.make_async_remote_copy(src[t], dst[t].at[slot], send.at[s], recv.at[s],
                                                device_id=(*peers[j], c), device_id_type=_MESH)

        def passed(t, j, half):
            s = t * (N_CHIPS - 1) + j
            px, py = peers[j]
            part = dst[t].at[2 * px + py, half]
            return pltpu.make_async_remote_copy(part, part, fsend.at[s], frecv.at[s], device_id=sib, device_id_type=_MESH)

        return direct, passed, peers, 2 * x + y, c

    def start(src, dst, sems):
        direct, _, _, me, _ = copies(src, dst, sems)
        for t in range(n):
            for j in range(N_CHIPS - 1):
                direct(t, j, me).start()
        for t in range(n):
            own(src, dst, sems, t).start()

    def finish(src, dst, sems):
        direct, passed, peers, me, c = copies(src, dst, sems)
        for t in range(n):
            own(src, dst, sems, t).wait_recv()
        fwd = []
        for t in range(nb):
            for j in range(N_CHIPS - 1):
                px, py = peers[j]
                direct(t, j, 2 * px + py).wait_recv()
                fwd.append(passed(t, j, c))
                fwd[-1].start()
        for t in range(nb, n):
            for j in range(N_CHIPS - 1):
                px, py = peers[j]
                direct(t, j, 2 * px + py).wait_recv()
        for t in range(nb):
            for j in range(N_CHIPS - 1):
                passed(t, j, 1 - c).wait_recv()
        for t in range(n):
            for j in range(N_CHIPS - 1):
                direct(t, j, me).wait_send()
        for cp in fwd:
            cp.wait_send()
        for t in range(n):
            own(src, dst, sems, t).wait_send()

    return _Exchange(shards, [jax.ShapeDtypeStruct((N_CHIPS,) + s.shape, s.dtype) for s in shards],
                     [pltpu.SemaphoreType.DMA((k,)), pltpu.SemaphoreType.DMA((k,)),
                      pltpu.SemaphoreType.DMA((max(kb, 1),)), pltpu.SemaphoreType.DMA((max(kb, 1),)),
                      pltpu.SemaphoreType.DMA((n,)), pltpu.SemaphoreType.DMA((n,))], start, finish)


def sibling_swap(grads):
    pairs = [(t, o) for t, g in enumerate(grads) for o in range(g.shape[0])]
    k = len(pairs)

    def copies(src, dst, sems):
        send, recv = sems
        x, y, c = _coords()
        return [pltpu.make_async_remote_copy(src[t].at[o, 1 - c], dst[t].at[o], send.at[s], recv.at[s],
                                             device_id=(x, y, 1 - c), device_id_type=_MESH)
                for s, (t, o) in enumerate(pairs)]

    def start(src, dst, sems):
        for cp in copies(src, dst, sems):
            cp.start()

    def finish(src, dst, sems):
        cps = copies(src, dst, sems)
        for cp in cps:
            cp.wait_recv()
        for cp in cps:
            cp.wait_send()

    return _Exchange(list(grads), [jax.ShapeDtypeStruct((g.shape[0],) + g.shape[2:], g.dtype) for g in grads],
                     [pltpu.SemaphoreType.DMA((k,)), pltpu.SemaphoreType.DMA((k,))], start, finish)


def scatter_chips(parts):
    n = len(parts)
    k = n * (N_CHIPS - 1)

    def copy(src, dst, sems, t, r, landing):
        send, recv = sems
        x, y, c = _coords()
        me = 2 * x + y
        px, py = _chip_peer(x, y, r)
        peer = 2 * px + py
        s = t * (N_CHIPS - 1) + r - 1
        return pltpu.make_async_remote_copy(src[t].at[me if landing else peer], dst[t].at[peer if landing else me],
                                            send.at[s], recv.at[s], device_id=(px, py, c), device_id_type=_MESH)

    def start(src, dst, sems):
        for t in range(n):
            for r in range(1, N_CHIPS):
                copy(src, dst, sems, t, r, False).start()

    def finish(src, dst, sems):
        for t in range(n):
            for r in range(1, N_CHIPS):
                copy(src, dst, sems, t, r, True).wait_recv()
        for t in range(n):
            for r in range(1, N_CHIPS):
                copy(src, dst, sems, t, r, False).wait_send()

    return _Exchange(list(parts), [jax.ShapeDtypeStruct(p.shape, p.dtype) for p in parts],
                     [pltpu.SemaphoreType.DMA((k,)), pltpu.SemaphoreType.DMA((k,))], start, finish)


def sibling_send(halves):
    n = len(halves)

    def copies(src, dst, sems):
        send, recv = sems
        x, y, c = _coords()
        return [pltpu.make_async_remote_copy(src[t], dst[t], send.at[t], recv.at[t],
                                             device_id=(x, y, 1 - c), device_id_type=_MESH) for t in range(n)]

    def start(src, dst, sems):
        for cp in copies(src, dst, sems):
            cp.start()

    def finish(src, dst, sems):
        cps = copies(src, dst, sems)
        for cp in cps:
            cp.wait_recv()
        for cp in cps:
            cp.wait_send()

    return _Exchange(list(halves), [jax.ShapeDtypeStruct(h.shape, h.dtype) for h in halves],
                     [pltpu.SemaphoreType.DMA((n,)), pltpu.SemaphoreType.DMA((n,))], start, finish)


def gather_all(block):
    def copies(src, dst, sems, landing):
        send, recv, loc = sems
        x, y, c = _coords()
        me = 4 * x + 2 * y + c
        mine = pltpu.make_async_copy(src[0], dst[0].at[me], loc)
        remote = []
        for r in range(1, N_DEV):
            px, py, pc = x ^ (r >> 2), y ^ ((r >> 1) & 1), c ^ (r & 1)
            slot = 4 * px + 2 * py + pc if landing else me
            remote.append(pltpu.make_async_remote_copy(src[0], dst[0].at[slot], send.at[r - 1], recv.at[r - 1],
                                                       device_id=(px, py, pc), device_id_type=_MESH))
        return mine, remote

    def start(src, dst, sems):
        mine, outs = copies(src, dst, sems, False)
        mine.start()
        for cp in outs:
            cp.start()

    def finish(src, dst, sems):
        mine, lands = copies(src, dst, sems, True)
        for cp in lands:
            cp.wait_recv()
        for cp in lands:
            cp.wait_send()
        mine.wait()

    return _Exchange([block], [jax.ShapeDtypeStruct((N_DEV,) + block.shape, block.dtype)],
                     [pltpu.SemaphoreType.DMA((N_DEV - 1,)), pltpu.SemaphoreType.DMA((N_DEV - 1,)),
                      pltpu.SemaphoreType.DMA(())], start, finish)


def _pad_lanes(v, n=LANE):
    return jnp.pad(v, ((0, 0), (0, n - v.shape[1])))


def _w_in_pieces():
    cs = (PROJ_BIG + 2 * HEADS) // N_CHIPS
    ab_end = AB_COL + 2 * HEADS
    out = []
    for o in range(N_CHIPS):
        lo, hi = o * cs, (o + 1) * cs
        cand = [("big", lo, min(hi, AB_COL), 0), ("ab", max(lo, AB_COL), min(hi, ab_end), AB_COL),
                ("big", max(lo, ab_end), hi, 2 * HEADS)]
        out.append([(s, a - off, b - off) for s, a, b, off in cand if a < b])
    return out


def _split_w_in(w4):
    big, ab = [], []
    for o, pieces in enumerate(_w_in_pieces()):
        at = 0
        for s, a, b in pieces:
            (big if s == "big" else ab).append(w4[o][:, at:at + b - a])
            at += b - a
    return jnp.concatenate(big, axis=1), _pad_lanes(jnp.concatenate(ab, axis=1))


def _join_w_in(big, ab):
    src = {"big": big, "ab": ab}
    return jnp.stack([jnp.concatenate([src[s][:, a:b] for s, a, b in pieces], axis=1) for pieces in _w_in_pieces()])


def _conv_w8(w):
    return jnp.pad(w, ((0, SUB - DN_CONV), (0, 0)))


def _row_layout(gc, tp):
    nc = tp // CHUNK
    g = gc[:, :, 0].reshape(HEADS, nc, 1, CHUNK)
    g = jnp.broadcast_to(g, (HEADS, nc, SUB, CHUNK))
    return jnp.pad(g, ((0, 0), (0, 0), (0, 0), (0, LANE - CHUNK))).reshape(HEADS, nc * SUB, LANE)


def _step(x, meta, W, target, late_weights=None, early_swap=None, early_grads=None, last_grads=None, final_send=None):
    W = dict(W)
    seq = x.shape[0]
    tp = P0 + N_META + seq
    h0 = jnp.concatenate([jnp.zeros((P0, D_MODEL), f32), meta, x], axis=0)
    w_big, w_ab = _split_w_in(W["w_in"])
    cq8, ck8, cv8 = _conv_w8(W["conv_q"]), _conv_w8(W["conv_k"]), _conv_w8(W["conv_v"])
    al, dtb = _pad_lanes(W["dn_a_log"]), _pad_lanes(W["dn_dt_bias"])

    n1 = rms_fwd(h0, W["norm_mix_gain"], "rms1_fwd")
    proj = matmul(n1, w_big, "nn", "proj_fwd")
    pab = matmul(n1, w_ab, "nn", "pab_fwd")
    qn = conv_fwd(proj, cq8, C_DQ * 8, 8, True, "conv_q_fwd")
    kn = conv_fwd(proj, ck8, C_DK * 8, 8, True, "conv_k_fwd")
    va = conv_fwd(proj, cv8, C_DV * 8, 16, False, "conv_v_fwd")
    gc, bc = ab_fwd(pab, al, dtb)
    grow = _row_layout(gc, tp)
    o_dn, states = gdn_fwd(qn, kn, va, gc, bc, grow)
    on = dn_out_fwd(o_dn, proj, W["dn_out_norm_gain"])
    qs, ks, vs = sb_prep_fwd(proj, W["sb_q_norm_gain"], W["sb_k_norm_gain"])
    o_sb, o_sb16, arrived = sb_fwd(qs, ks, vs, rider=late_weights[0] if late_weights else None)
    if late_weights:
        W.update(late_weights[1](arrived))
    ydn = matmul(on, W["w_branch_dn"], "nn", "ydn_fwd")
    ysb = matmul(o_sb16, W["w_branch_sb"], "nn", "ysb_fwd")
    merged = merge_fwd(proj, ydn, ysb)
    h1 = matmul(merged, W["w_out"], "nn", "wout_fwd", residual=h0)
    n2 = rms_fwd(h1, W["norm_ffn_gain"], "rms2_fwd")
    u = matmul(n2, W["w_ffn_in"], "nn", "ffn_in_fwd", tn_t=512)
    act = swiglu_fwd(u)
    y = matmul(act, W["w_ffn_out"], "nn", "ffn_out_fwd", residual=h1)
    dy, dy16, loss = loss_head(y, target)

    G = {}
    dact = matmul(dy16, W["w_ffn_out"], "nt", "ffn_out_dx", tn_t=1408)
    G["w_ffn_out"] = matmul(act, dy16, "tn", "ffn_out_dw", tm_t=1408)
    dgate, dup = swiglu_bwd(u, dact)
    du = jnp.concatenate([dgate, dup], axis=1)
    dn2 = matmul(du, W["w_ffn_in"], "nt", "ffn_in_dx", tk_t=512)
    G["w_ffn_in"] = matmul(n2, du, "tn", "ffn_in_dw", tn_t=512)
    (dh1, dh1_16, G["norm_ffn_gain"]), _ = rms_bwd(h1, W["norm_ffn_gain"], dn2, dy, "rms2_bwd")
    dmerged = matmul(dh1_16, W["w_out"], "nt", "wout_dx")
    G["w_out"] = matmul(merged, dh1_16, "tn", "wout_dw")
    dyd, dys, d_gates = merge_bwd(proj, ydn, ysb, dmerged)
    don = matmul(dyd, W["w_branch_dn"], "nt", "ydn_dx")
    G["w_branch_dn"] = matmul(on, dyd, "tn", "ydn_dw")
    do_sb = matmul(dys, W["w_branch_sb"], "nt", "ysb_dx")
    G["w_branch_sb"] = matmul(o_sb16, dys, "tn", "ysb_dw")
    do_dn, dproj, G["dn_out_norm_gain"] = dn_out_bwd(o_dn, proj, W["dn_out_norm_gain"], don)
    (dqn, dkn, dva, dgc, dbc), swapped = gdn_bwd(qn, kn, va, gc, bc, grow, states, do_dn,
                                                 rider=early_swap[0](G) if early_swap else None)
    if early_swap:
        early_swap[1](swapped)
    dpab, dal, ddt = ab_bwd(pab, al, dtb, dgc, dbc)
    G["dn_a_log"], G["dn_dt_bias"] = dal[:, :HEADS], ddt[:, :HEADS]
    dproj, dcq = conv_bwd(proj, cq8, dqn, dproj, C_DQ * 8, 8, True, "conv_q_bwd")
    dproj, dck = conv_bwd(proj, ck8, dkn, dproj, C_DK * 8, 8, True, "conv_k_bwd")
    dproj, dcv = conv_bwd(proj, cv8, dva, dproj, C_DV * 8, 16, False, "conv_v_bwd")
    G["conv_q"], G["conv_k"], G["conv_v"] = dcq[:DN_CONV], dck[:DN_CONV], dcv[:DN_CONV]
    (dqs, dks, dvs), delivered = sb_bwd(qs, ks, vs, o_sb, do_sb, rider=early_grads[0](G) if early_grads else None)
    if early_grads:
        early_grads[1](delivered)
    dproj, G["sb_q_norm_gain"], G["sb_k_norm_gain"] = sb_prep_bwd(
        proj, W["sb_q_norm_gain"], W["sb_k_norm_gain"], dqs, dks, dvs, dproj)
    dproj = lax.dynamic_update_slice(dproj, d_gates, (0, C_GDN * 1024))
    dw_big = matmul(n1, dproj, "tn", "proj_dw")
    dw_ab = matmul(n1, dpab, "tn", "pab_dw")
    G["w_in"] = (dw_big, dw_ab)
    if last_grads:
        dn1, delivered = matmul(dproj, w_big, "nt", "proj_dx", tk_t=1024, rider=last_grads[0](G))
        last_grads[1](delivered)
    else:
        dn1 = matmul(dproj, w_big, "nt", "proj_dx", tk_t=1024)
    dn1 = matmul(dpab, w_ab, "nt", "pab_dx", residual=dn1)
    (dh0, _, G["norm_mix_gain"]), sent = rms_bwd(h0, W["norm_mix_gain"], dn1, dh1, "rms1_bwd",
                                                 rider=final_send[0]() if final_send else None)
    if final_send:
        final_send[1](sent)
    G["meta_tokens"] = dh0[P0:P0 + N_META]
    return loss, dh0[P0 + N_META:], G


_BIG = ("w_in", "w_branch_dn", "w_branch_sb", "w_out", "w_ffn_in", "w_ffn_out")
_COL_SHARDED = ("w_in", "w_ffn_in", "meta_tokens", "conv_q", "conv_k", "conv_v")
_SMALL_REPL = ("norm_mix_gain", "norm_ffn_gain", "dn_a_log", "dn_dt_bias", "dn_out_norm_gain", "sb_q_norm_gain",
               "sb_k_norm_gain")
_SMALL_SHARD = ("meta_tokens", "conv_q", "conv_k", "conv_v")
_ORDER = ("meta_tokens", "norm_mix_gain", "w_in", "conv_q", "conv_k", "conv_v", "dn_a_log", "dn_dt_bias",
          "dn_out_norm_gain", "sb_q_norm_gain", "sb_k_norm_gain", "w_branch_dn", "w_branch_sb", "w_out",
          "norm_ffn_gain", "w_ffn_in", "w_ffn_out")


def _unshard(g4, name):
    if name in _COL_SHARDED:
        r, cs = g4.shape[1:]
        return jnp.transpose(g4, (1, 0, 2)).reshape(r, N_CHIPS * cs)
    return g4.reshape((-1,) + g4.shape[2:])


def _to_shards(full, name):
    if name in _COL_SHARDED:
        r, c = full.shape
        return jnp.transpose(full.reshape(r, N_CHIPS, c // N_CHIPS), (1, 0, 2))
    r, c = full.shape
    return full.reshape(N_CHIPS, r // N_CHIPS, c)


def _rows_1024(a):
    r, c = a.shape
    if c >= 1024:
        return a.reshape(r * (c // 1024), 1024)
    return jnp.pad(a, ((0, 0), (0, 1024 - c)))


def kernel(x, meta_tokens, norm_mix_gain, w_in, conv_q, conv_k, conv_v, dn_a_log, dn_dt_bias, dn_out_norm_gain, sb_q_norm_gain, sb_k_norm_gain, w_branch_dn, w_branch_sb, w_out, norm_ffn_gain, w_ffn_in, w_ffn_out, loss_target, m_meta_tokens, m_norm_mix_gain, m_w_in, m_conv_q, m_conv_k, m_conv_v, m_dn_a_log, m_dn_dt_bias, m_dn_out_norm_gain, m_sb_q_norm_gain, m_sb_k_norm_gain, m_w_branch_dn, m_w_branch_sb, m_w_out, m_norm_ffn_gain, m_w_ffn_in, m_w_ffn_out, v_meta_tokens, v_norm_mix_gain, v_w_in, v_conv_q, v_conv_k, v_conv_v, v_dn_a_log, v_dn_dt_bias, v_dn_out_norm_gain, v_sb_q_norm_gain, v_sb_k_norm_gain, v_w_branch_dn, v_w_branch_sb, v_w_out, v_norm_ffn_gain, v_w_ffn_in, v_w_ffn_out):
    Wl = dict(meta_tokens=meta_tokens, norm_mix_gain=norm_mix_gain, w_in=w_in[0], conv_q=conv_q[0], conv_k=conv_k[0],
              conv_v=conv_v[0], dn_a_log=dn_a_log, dn_dt_bias=dn_dt_bias, dn_out_norm_gain=dn_out_norm_gain,
              sb_q_norm_gain=sb_q_norm_gain, sb_k_norm_gain=sb_k_norm_gain, w_branch_dn=w_branch_dn[0],
              w_branch_sb=w_branch_sb[0], w_out=w_out[0], norm_ffn_gain=norm_ffn_gain, w_ffn_in=w_ffn_in[0],
              w_ffn_out=w_ffn_out[0])
    Ml = dict(meta_tokens=m_meta_tokens, norm_mix_gain=m_norm_mix_gain, w_in=m_w_in[0], conv_q=m_conv_q[0],
              conv_k=m_conv_k[0], conv_v=m_conv_v[0], dn_a_log=m_dn_a_log, dn_dt_bias=m_dn_dt_bias,
              dn_out_norm_gain=m_dn_out_norm_gain, sb_q_norm_gain=m_sb_q_norm_gain, sb_k_norm_gain=m_sb_k_norm_gain,
              w_branch_dn=m_w_branch_dn[0], w_branch_sb=m_w_branch_sb[0], w_out=m_w_out[0],
              norm_ffn_gain=m_norm_ffn_gain, w_ffn_in=m_w_ffn_in[0], w_ffn_out=m_w_ffn_out[0])
    Vl = dict(meta_tokens=v_meta_tokens, norm_mix_gain=v_norm_mix_gain, w_in=v_w_in[0], conv_q=v_conv_q[0],
              conv_k=v_conv_k[0], conv_v=v_conv_v[0], dn_a_log=v_dn_a_log, dn_dt_bias=v_dn_dt_bias,
              dn_out_norm_gain=v_dn_out_norm_gain, sb_q_norm_gain=v_sb_q_norm_gain, sb_k_norm_gain=v_sb_k_norm_gain,
              w_branch_dn=v_w_branch_dn[0], w_branch_sb=v_w_branch_sb[0], w_out=v_w_out[0],
              norm_ffn_gain=v_norm_ffn_gain, w_ffn_in=v_w_ffn_in[0], w_ffn_out=v_w_ffn_out[0])
    lead = {n: (1,) if (n in _BIG or n in ("conv_q", "conv_k", "conv_v")) else () for n in _ORDER}

    chip = 2 * lax.axis_index("x") + lax.axis_index("y")
    c = lax.axis_index("c")
    halved = {n: Wl[n].astype(bf16).reshape(2, Wl[n].shape[0] // 2, Wl[n].shape[1]) for n in _BIG}

    def gathered_weights(names, owns, outs):
        res = {}
        for n, g4 in zip(names, outs):
            if n in _BIG:
                g4 = g4.reshape(N_CHIPS, 2 * g4.shape[2], g4.shape[3])
            res[n] = g4 if n == "w_in" else _unshard(g4, n)
        return res

    first = ["w_in"] + list(_SMALL_SHARD)
    first_own = [halved["w_in"]] + [Wl[n] for n in _SMALL_SHARD]
    W = dict(Wl)
    W.update(gathered_weights(first, first_own, run_exchange(gather_chips(first_own[:1], first_own[1:]), "gather_w_in")))
    late = [n for n in _BIG if n != "w_in"]
    late_own = [halved[n] for n in late]
    for n in late:
        del W[n]

    def halves_of(names, G):
        g4 = [_to_shards(G[n], n) for n in names]
        return [g.reshape(N_CHIPS, 2, g.shape[1] // 2, g.shape[2]) for g in g4]

    def pair_added(g42, from_sib, tag, wire):
        mine = [lax.dynamic_index_in_dim(g, c, axis=1, keepdims=False) for g in g42]
        return [add2(a, b, "grad_pair_add_%s%d" % (tag, t), out_dtype=wire)
                for t, (a, b) in enumerate(zip(mine, from_sib))]

    def chip_reduced(parts, slots, tag):
        return [sum_chip_parts(p, s, chip, "grad_chip_sum_%s%d" % (tag, t)) for t, (p, s) in enumerate(zip(parts, slots))]

    early, last = {}, {}

    def early_swap_begin(G):
        early["g42"] = halves_of(late, G)
        return sibling_swap(early["g42"])

    def early_begin(G):
        early["parts"] = pair_added(early["g42"], early["from_sib"], "a", f32)
        return scatter_chips(early["parts"])

    def last_begin(G):
        g2 = [g.reshape(1, 2, g.shape[0] // 2, g.shape[1]) for g in G["w_in"]]
        added = pair_added(g2, run_exchange(sibling_swap(g2), "grad_sibling_swap_b"), "b", bf16)
        last["parts"] = [_join_w_in(added[0][0], added[1][0])]
        return scatter_chips(last["parts"])

    loss, grad_x, G = _step(
        x[0], W["meta_tokens"], W, loss_target[0],
        late_weights=(gather_chips(late_own, []), lambda outs: gathered_weights(late, late_own, outs)),
        early_swap=(early_swap_begin, lambda outs: early.update(from_sib=outs)),
        early_grads=(early_begin, lambda slots: early.update(halves=chip_reduced(early["parts"], slots, "a"))),
        last_grads=(last_begin, lambda slots: last.update(halves=chip_reduced(last["parts"], slots, "b"))),
        final_send=(lambda: sibling_send(last["halves"] + early["halves"]), lambda outs: last.update(theirs=outs)))
    Gs = {}
    for n, h, o in zip(["w_in"] + late, last["halves"] + early["halves"], last["theirs"]):
        Gs[n] = lax.dynamic_update_slice(jnp.concatenate([o, o], axis=0), h, (c * h.shape[0], 0))

    small_names = list(_SMALL_REPL) + list(_SMALL_SHARD)
    pieces = [_rows_1024(G[n]) for n in small_names] + [_rows_1024(loss)]
    counts = [p.shape[0] for p in pieces]
    pack = jnp.concatenate(pieces, axis=0)
    pad_rows = (-pack.shape[0]) % SUB
    pack = jnp.pad(pack, ((0, pad_rows), (0, 0)))
    adam = {"w_in": adamw(Wl["w_in"], Gs["w_in"], Ml["w_in"], Vl["w_in"], "adamw_w_in", rider=gather_all(pack))}
    total = sum_slots(adam["w_in"][1][0], "small_sum")
    row = 0
    for n, cnt in zip(small_names, counts[:-1]):
        blk = total[row:row + cnt]
        row += cnt
        full_shape = G[n].shape
        if full_shape[1] >= 1024:
            blk = blk.reshape(full_shape)
        else:
            blk = blk[:, :full_shape[1]]
        if n in _SMALL_SHARD:
            cs = full_shape[1] // N_CHIPS
            blk = lax.dynamic_slice_in_dim(blk, chip * cs, cs, axis=1)
        Gs[n] = blk
    loss_out = total[row, 0]

    grads, deltas, new_m, new_v = [], [], [], []
    for n in _ORDER:
        d, m2, v2 = (adam[n] if n in adam else adamw(Wl[n], Gs[n], Ml[n], Vl[n], "adamw_" + n))[0]
        shape = lead[n] + Wl[n].shape
        grads.append(Gs[n].reshape(shape))
        deltas.append(d.reshape(shape))
        new_m.append(m2.reshape(shape))
        new_v.append(v2.reshape(shape))
    return (loss_out, grad_x[None], *grads, *deltas, *new_m, *new_v)
```

```python
import jax
import jax.numpy as jnp
from jax import lax
from jax.experimental import pallas as pl
from jax.experimental.pallas import tpu as pltpu

f32 = jnp.float32
bf16 = jnp.bfloat16

D_MODEL = 1024
N_META = 16
CHUNK = 64
HEADS = 8
DN_DK = 128
DN_DV = 256
DN_CONV = 4
DN_QK = HEADS * DN_DK
DN_V = HEADS * DN_DV
SB_DH = 128
SB_W = HEADS * SB_DH
SB_BLOCK = 128
SB_QB = 384
SB_GROUP = 4
SB_HEADS_PER_STEP = 2
SB_FWD_HEADS_PER_STEP = 4
GDN_HEADS_PER_STEP = 8
CONV_W = 2048
D_FF = 2816
RMS_EPS = 1e-6
L2_EPS = 1e-6
ADAM_LR = 0.001
ADAM_B1 = 0.9
ADAM_B2 = 0.999
ADAM_EPS = 1e-08
ADAM_WD = 0.01
ADAM_STEP = 10

P0 = 112
LANE = 128
SUB = 8
VMEM_LIMIT = 48 * 1024 * 1024
N_CHIPS = 4
N_DEV = 8

C_DQ, C_DK, C_DV, C_DZ, C_SQ, C_SK, C_SV, C_GDN, C_GSB = 0, 1, 2, 4, 6, 7, 8, 9, 10
PROJ_BIG = 11 * 1024
AB_COL = 2 * DN_QK + 2 * DN_V


def _params(n_axes):
    return pltpu.CompilerParams(dimension_semantics=("arbitrary",) * n_axes, vmem_limit_bytes=VMEM_LIMIT)


def _tile(n, target, q=LANE):
    best = None
    for t in range(q, min(n, target) + 1, q):
        if n % t == 0:
            best = t
    return best if best is not None else n


def _dot(a, b):
    return jnp.dot(a.astype(bf16), b.astype(bf16), preferred_element_type=f32)


def _dot_nt(a, b):
    return lax.dot_general(a.astype(bf16), b.astype(bf16), (((1,), (1,)), ((), ())), preferred_element_type=f32)


def _dot_tn(a, b):
    return lax.dot_general(a.astype(bf16), b.astype(bf16), (((0,), (0,)), ((), ())), preferred_element_type=f32)


_HI = lax.Precision.HIGH


def _hdot(a, b):
    return jnp.dot(a, b, precision=_HI, preferred_element_type=f32)


def _hdot_nt(a, b):
    return lax.dot_general(a, b, (((1,), (1,)), ((), ())), precision=_HI, preferred_element_type=f32)


def _hdot_tn(a, b):
    return lax.dot_general(a, b, (((0,), (0,)), ((), ())), precision=_HI, preferred_element_type=f32)


def _sigmoid(x):
    return 0.5 * jnp.tanh(0.5 * x) + 0.5


def _log1p_small(e):
    return jnp.where(e < 1e-3, e * (1.0 - e * (0.5 - e * (1.0 / 3.0))), jnp.log(1.0 + e))


def _rowsum(x):
    return jnp.sum(x, axis=1, keepdims=True)


def _allsum(x):
    return jnp.sum(jnp.sum(x, axis=1, keepdims=True), axis=0, keepdims=True)


def matmul(a, b, mode, name, residual=None, out_dtype=f32, tm_t=1408, tn_t=1024, tk_t=1408, rider=None):
    if mode == "nn":
        (M, K), (K2, N) = a.shape, b.shape
    elif mode == "nt":
        (M, K), (N, K2) = a.shape, b.shape
    else:
        (K, M), (K2, N) = a.shape, b.shape
    assert K == K2, (a.shape, b.shape, mode)
    tm, tn, tk = _tile(M, tm_t), _tile(N, tn_t), _tile(K, tk_t)
    nk = K // tk
    if mode == "nn":
        a_spec = pl.BlockSpec((tm, tk), lambda i, j, k: (i, k))
        b_spec = pl.BlockSpec((tk, tn), lambda i, j, k: (k, j))
        dims = (((1,), (0,)), ((), ()))
    elif mode == "nt":
        a_spec = pl.BlockSpec((tm, tk), lambda i, j, k: (i, k))
        b_spec = pl.BlockSpec((tn, tk), lambda i, j, k: (j, k))
        dims = (((1,), (1,)), ((), ()))
    else:
        a_spec = pl.BlockSpec((tk, tm), lambda i, j, k: (k, i))
        b_spec = pl.BlockSpec((tk, tn), lambda i, j, k: (k, j))
        dims = (((0,), (0,)), ((), ()))
    o_spec = pl.BlockSpec((tm, tn), lambda i, j, k: (i, j))
    has_res = residual is not None
    grid = (M // tm, N // tn, nk)
    split, ride_first, ride_last = _ride(rider, 3 if has_res else 2, 1, grid)

    def body(*refs):
        ins_, (o_ref,), (rin, rout, rest) = split(refs)
        a_ref, b_ref = ins_[:2]
        r_ref = ins_[2] if has_res else None
        acc_ref, ride = rest[0], (rin, rout, rest[1:])
        ride_first(ride)
        k = pl.program_id(2)

        @pl.when(k == 0)
        def _():
            acc_ref[...] = jnp.zeros_like(acc_ref)

        acc_ref[...] += lax.dot_general(a_ref[...].astype(bf16), b_ref[...].astype(bf16), dims,
                                        preferred_element_type=f32)

        @pl.when(k == nk - 1)
        def _():
            r = acc_ref[...]
            if has_res:
                r = r + r_ref[...]
            o_ref[...] = r.astype(out_dtype)

        ride_last(ride)

    ins = [a, b] + ([residual] if has_res else [])
    specs = [a_spec, b_spec] + ([o_spec] if has_res else [])
    r_ins = rider.ins if rider else []
    r_outs = rider.out_shapes if rider else []
    res = pl.pallas_call(
        body, name=name, grid=grid, in_specs=specs + [_ANY] * len(r_ins), out_specs=[o_spec] + [_ANY] * len(r_outs),
        out_shape=[jax.ShapeDtypeStruct((M, N), out_dtype)] + list(r_outs),
        scratch_shapes=[pltpu.VMEM((tm, tn), f32)] + (rider.scratch if rider else []), compiler_params=_params(3),
    )(*ins, *r_ins)
    return (res[0], res[1:]) if rider else res[0]


def _row_tile(tp):
    return _tile(tp, 512)


def rms_fwd(h, gain, name):
    tp, d = h.shape
    rt = _row_tile(tp)

    def body(h_ref, g_ref, o_ref):
        x = h_ref[...]
        r = lax.rsqrt(jnp.mean(x * x, axis=-1, keepdims=True) + RMS_EPS)
        o_ref[...] = (x * r * g_ref[...]).astype(bf16)

    return pl.pallas_call(
        body, name=name, grid=(tp // rt,),
        in_specs=[pl.BlockSpec((rt, d), lambda i: (i, 0)), pl.BlockSpec((1, d), lambda i: (0, 0))],
        out_specs=pl.BlockSpec((rt, d), lambda i: (i, 0)),
        out_shape=jax.ShapeDtypeStruct((tp, d), bf16), compiler_params=_params(1),
    )(h, gain)


def rms_bwd(h, gain, dn, dres, name, rider=None):
    tp, d = h.shape
    rt = _row_tile(tp)
    grid = (tp // rt,)
    split, ride_first, ride_last = _ride(rider, 4, 3, grid)

    def body(*refs):
        (h_ref, g_ref, dn_ref, dr_ref), (dh_ref, dhb_ref, dg_ref), ride = split(refs)
        ride_first(ride)
        i = pl.program_id(0)
        x = h_ref[...]
        r = lax.rsqrt(jnp.mean(x * x, axis=-1, keepdims=True) + RMS_EPS)
        xh = x * r
        dn_ = dn_ref[...]
        dxh = dn_ * g_ref[...]
        dh = r * (dxh - xh * jnp.mean(dxh * xh, axis=-1, keepdims=True)) + dr_ref[...]
        dh_ref[...] = dh
        dhb_ref[...] = dh.astype(bf16)
        part = jnp.sum(dn_ * xh, axis=0, keepdims=True)

        @pl.when(i == 0)
        def _():
            dg_ref[...] = part

        @pl.when(i > 0)
        def _():
            dg_ref[...] += part

        ride_last(ride)

    row = pl.BlockSpec((rt, d), lambda i: (i, 0))
    vec = pl.BlockSpec((1, d), lambda i: (0, 0))
    r_ins = rider.ins if rider else []
    r_outs = rider.out_shapes if rider else []
    res = pl.pallas_call(
        body, name=name, grid=grid, in_specs=[row, vec, row, row] + [_ANY] * len(r_ins),
        out_specs=[row, row, vec] + [_ANY] * len(r_outs),
        out_shape=[jax.ShapeDtypeStruct((tp, d), f32), jax.ShapeDtypeStruct((tp, d), bf16),
                   jax.ShapeDtypeStruct((1, d), f32)] + list(r_outs),
        scratch_shapes=rider.scratch if rider else [], compiler_params=_params(1),
    )(h, gain, dn, dres, *r_ins)
    return res[:3], res[3:]


def loss_head(y, target):
    tp, d = y.shape
    lead = P0 + N_META
    rt = _row_tile(tp)
    ns = rt // lead
    assert lead == SB_BLOCK and rt % lead == 0 and target.shape == (tp - lead, d)
    last = target.shape[0] // lead - 1

    def body(*refs):
        y_ref, t_refs = refs[0], refs[1:1 + ns]
        dy_ref, dyb_ref, l_ref = refs[1 + ns:]
        i = pl.program_id(0)

        @pl.when(i == 0)
        def _():
            l_ref[...] = jnp.zeros_like(l_ref)

        part = jnp.zeros((1, 1), f32)
        for s in range(ns):
            rows = slice(s * lead, (s + 1) * lead)
            err = y_ref[rows, :] - t_refs[s][...]
            if s == 0:
                err = err * (i > 0).astype(f32)
            dy = err * (1.0 / d)
            dy_ref[rows, :] = dy
            dyb_ref[rows, :] = dy.astype(bf16)
            part = part + _allsum(err * err)
        l_ref[...] += jnp.broadcast_to(part * (0.5 / d), l_ref.shape)

    row = pl.BlockSpec((rt, d), lambda i: (i, 0))
    t_specs = [pl.BlockSpec((lead, d), lambda i, s=s: (jnp.clip(ns * i + s - 1, 0, last), 0)) for s in range(ns)]
    return pl.pallas_call(
        body, name="loss_head", grid=(tp // rt,), in_specs=[row] + t_specs,
        out_specs=[row, row, pl.BlockSpec((1, LANE), lambda i: (0, 0))],
        out_shape=[jax.ShapeDtypeStruct((tp, d), f32), jax.ShapeDtypeStruct((tp, d), bf16),
                   jax.ShapeDtypeStruct((1, LANE), f32)],
        compiler_params=_params(1),
    )(y, *([target] * ns))


def swiglu_fwd(u):
    tp = u.shape[0]
    rt, cb = _row_tile(tp), D_FF // 2
    nb = D_FF // cb

    def body(g_ref, u_ref, o_ref):
        g = g_ref[...]
        o_ref[...] = (g * _sigmoid(g) * u_ref[...]).astype(bf16)

    return pl.pallas_call(
        body, name="swiglu_fwd", grid=(tp // rt, nb),
        in_specs=[pl.BlockSpec((rt, cb), lambda i, j: (i, j)), pl.BlockSpec((rt, cb), lambda i, j: (i, j + nb))],
        out_specs=pl.BlockSpec((rt, cb), lambda i, j: (i, j)),
        out_shape=jax.ShapeDtypeStruct((tp, D_FF), bf16), compiler_params=_params(2),
    )(u, u)


def swiglu_bwd(u, dact):
    tp = u.shape[0]
    rt, cb = _row_tile(tp), D_FF // 2
    nb = D_FF // cb

    def body(g_ref, u_ref, da_ref, dg_ref, du_ref):
        g = g_ref[...]
        s = _sigmoid(g)
        da = da_ref[...]
        dg_ref[...] = (da * u_ref[...] * s * (1.0 + g * (1.0 - s))).astype(bf16)
        du_ref[...] = (da * g * s).astype(bf16)

    lo = pl.BlockSpec((rt, cb), lambda i, j: (i, j))
    hi = pl.BlockSpec((rt, cb), lambda i, j: (i, j + nb))
    dgate, dup = pl.pallas_call(
        body, name="swiglu_bwd", grid=(tp // rt, nb), in_specs=[lo, hi, lo], out_specs=[lo, lo],
        out_shape=[jax.ShapeDtypeStruct((tp, D_FF), bf16)] * 2, compiler_params=_params(2),
    )(u, u, dact)
    return dgate, dup


def merge_fwd(proj, ydn, ysb):
    tp = proj.shape[0]
    rt, d = _row_tile(tp), D_MODEL

    def body(gd_ref, gs_ref, yd_ref, ys_ref, o_ref):
        o_ref[...] = (_sigmoid(gd_ref[...]) * yd_ref[...] + _sigmoid(gs_ref[...]) * ys_ref[...]).astype(bf16)

    row = pl.BlockSpec((rt, d), lambda i: (i, 0))
    return pl.pallas_call(
        body, name="merge_fwd", grid=(tp // rt,),
        in_specs=[pl.BlockSpec((rt, d), lambda i: (i, C_GDN)), pl.BlockSpec((rt, d), lambda i: (i, C_GSB)), row, row],
        out_specs=row, out_shape=jax.ShapeDtypeStruct((tp, d), bf16), compiler_params=_params(1),
    )(proj, proj, ydn, ysb)


def merge_bwd(proj, ydn, ysb, dm):
    tp = proj.shape[0]
    rt, d = _row_tile(tp), D_MODEL

    def body(gd_ref, gs_ref, yd_ref, ys_ref, dm_ref, dyd_ref, dys_ref, dg_ref):
        dm_ = dm_ref[...]
        sd = _sigmoid(gd_ref[...])
        ss = _sigmoid(gs_ref[...])
        dyd_ref[...] = (dm_ * sd).astype(bf16)
        dys_ref[...] = (dm_ * ss).astype(bf16)
        dg_ref[:, :d] = (dm_ * yd_ref[...] * sd * (1.0 - sd)).astype(bf16)
        dg_ref[:, d:] = (dm_ * ys_ref[...] * ss * (1.0 - ss)).astype(bf16)

    row = pl.BlockSpec((rt, d), lambda i: (i, 0))
    return pl.pallas_call(
        body, name="merge_bwd", grid=(tp // rt,),
        in_specs=[pl.BlockSpec((rt, d), lambda i: (i, C_GDN)), pl.BlockSpec((rt, d), lambda i: (i, C_GSB)), row, row, row],
        out_specs=[row, row, pl.BlockSpec((rt, 2 * d), lambda i: (i, 0))],
        out_shape=[jax.ShapeDtypeStruct((tp, d), bf16)] * 2 + [jax.ShapeDtypeStruct((tp, 2 * d), bf16)],
        compiler_params=_params(1),
    )(proj, proj, ydn, ysb, dm)


def dn_out_fwd(o, proj, gain):
    tp = o.shape[0]
    rt, cb, wide = _row_tile(tp), DN_DV, 1024
    zb = C_DZ * 1024 // wide

    def body(o_ref, z_ref, g_ref, y_ref):
        for s in range(wide // cb):
            sl = slice(s * cb, (s + 1) * cb)
            x = o_ref[:, sl]
            r = lax.rsqrt(jnp.mean(x * x, axis=-1, keepdims=True) + RMS_EPS)
            z = z_ref[:, sl]
            y_ref[:, sl] = (x * r * g_ref[...] * (z * _sigmoid(z))).astype(bf16)

    blk = pl.BlockSpec((rt, wide), lambda i, j: (i, j))
    return pl.pallas_call(
        body, name="dn_out_fwd", grid=(tp // rt, DN_V // wide),
        in_specs=[blk, pl.BlockSpec((rt, wide), lambda i, j: (i, j + zb)), pl.BlockSpec((1, cb), lambda i, j: (0, 0))],
        out_specs=blk, out_shape=jax.ShapeDtypeStruct((tp, DN_V), bf16), compiler_params=_params(2),
    )(o, proj, gain)


def dn_out_bwd(o, proj, gain, dy):
    tp = o.shape[0]
    rt, cb, wide = _row_tile(tp), DN_DV, 1024
    zb = C_DZ * 1024 // wide

    def body(o_ref, z_ref, g_ref, dy_ref, do_ref, dz_ref, dg_ref):
        i, j = pl.program_id(0), pl.program_id(1)
        g = g_ref[...]
        part = jnp.zeros((1, cb), f32)
        for hh in range(wide // cb):
            sl = slice(hh * cb, (hh + 1) * cb)
            x = o_ref[:, sl]
            r = lax.rsqrt(jnp.mean(x * x, axis=-1, keepdims=True) + RMS_EPS)
            xh = x * r
            z = z_ref[:, sl]
            s = _sigmoid(z)
            dy_ = dy_ref[:, sl]
            drn = dy_ * (z * s)
            dz_ref[:, sl] = (dy_ * xh * g * s * (1.0 + z * (1.0 - s))).astype(bf16)
            dxh = drn * g
            do_ref[:, sl] = r * (dxh - xh * jnp.mean(dxh * xh, axis=-1, keepdims=True))
            part = part + jnp.sum(drn * xh, axis=0, keepdims=True)
        first = jnp.logical_and(i == 0, j == 0)

        @pl.when(first)
        def _():
            dg_ref[...] = part

        @pl.when(jnp.logical_not(first))
        def _():
            dg_ref[...] += part

    blk = pl.BlockSpec((rt, wide), lambda i, j: (i, j))
    vec = pl.BlockSpec((1, cb), lambda i, j: (0, 0))
    return pl.pallas_call(
        body, name="dn_out_bwd", grid=(tp // rt, DN_V // wide),
        in_specs=[blk, pl.BlockSpec((rt, wide), lambda i, j: (i, j + zb)), vec, blk],
        out_specs=[blk, pl.BlockSpec((rt, wide), lambda i, j: (i, j + zb)), vec],
        out_shape=[jax.ShapeDtypeStruct((tp, DN_V), f32), jax.ShapeDtypeStruct((tp, PROJ_BIG), bf16),
                   jax.ShapeDtypeStruct((1, cb), f32)],
        compiler_params=_params(2),
    )(o, proj, gain, dy)


def sb_prep_fwd(proj, gq, gk):
    tp = proj.shape[0]
    rt, cb = _row_tile(tp), SB_DH

    def body(q_ref, k_ref, v_ref, gq_ref, gk_ref, qo_ref, ko_ref, vo_ref):
        for x_ref, g_ref, o_ref in ((q_ref, gq_ref, qo_ref), (k_ref, gk_ref, ko_ref)):
            for h in range(HEADS):
                sl = slice(h * cb, (h + 1) * cb)
                x = x_ref[:, sl]
                r = lax.rsqrt(jnp.mean(x * x, axis=-1, keepdims=True) + RMS_EPS)
                o_ref[:, sl] = (x * r * g_ref[...]).astype(bf16)
        vo_ref[...] = v_ref[...].astype(bf16)

    blk = pl.BlockSpec((rt, SB_W), lambda i: (i, 0))
    vec = pl.BlockSpec((1, cb), lambda i: (0, 0))
    return pl.pallas_call(
        body, name="sb_prep_fwd", grid=(tp // rt,),
        in_specs=[pl.BlockSpec((rt, SB_W), lambda i: (i, C_SQ)), pl.BlockSpec((rt, SB_W), lambda i: (i, C_SK)),
                  pl.BlockSpec((rt, SB_W), lambda i: (i, C_SV)), vec, vec],
        out_specs=[blk] * 3, out_shape=[jax.ShapeDtypeStruct((tp, SB_W), bf16)] * 3, compiler_params=_params(1),
    )(proj, proj, proj, gq, gk)


def sb_prep_bwd(proj, gq, gk, dqs, dks, dvs, into):
    tp = proj.shape[0]
    rt, cb = _row_tile(tp), SB_DH
    assert (C_SQ * 1024) % (3 * SB_W) == 0 and (C_SQ + 1, C_SQ + 2) == (C_SK, C_SV)

    def body(q_ref, k_ref, gq_ref, gk_ref, dq_ref, dk_ref, dv_ref, into_ref, do_ref, dgq_ref, dgk_ref):
        first = pl.program_id(0) == 0
        do_ref[:, 2 * SB_W:] = dv_ref[...].astype(bf16)
        for x_ref, g_ref, dn_ref, at, dg_ref, mul in ((q_ref, gq_ref, dq_ref, 0, dgq_ref, None),
                                                      (k_ref, gk_ref, dk_ref, SB_W, dgk_ref, SB_DH ** -0.5)):
            part = jnp.zeros((1, cb), f32)
            for h in range(HEADS):
                sl = slice(h * cb, (h + 1) * cb)
                x = x_ref[:, sl]
                r = lax.rsqrt(jnp.mean(x * x, axis=-1, keepdims=True) + RMS_EPS)
                xh = x * r
                dn_ = dn_ref[:, sl] if mul is None else dn_ref[:, sl] * mul
                dxh = dn_ * g_ref[...]
                do_ref[:, at + h * cb:at + (h + 1) * cb] = (
                    r * (dxh - xh * jnp.mean(dxh * xh, axis=-1, keepdims=True))).astype(bf16)
                part = part + jnp.sum(dn_ * xh, axis=0, keepdims=True)

            @pl.when(first)
            def _(dg_ref=dg_ref, part=part):
                dg_ref[...] = part

            @pl.when(jnp.logical_not(first))
            def _(dg_ref=dg_ref, part=part):
                dg_ref[...] += part

    blk = pl.BlockSpec((rt, SB_W), lambda i: (i, 0))
    vec = pl.BlockSpec((1, cb), lambda i: (0, 0))
    return pl.pallas_call(
        body, name="sb_prep_bwd", grid=(tp // rt,),
        in_specs=[pl.BlockSpec((rt, SB_W), lambda i: (i, C_SQ)), pl.BlockSpec((rt, SB_W), lambda i: (i, C_SK)),
                  vec, vec, blk, blk, blk, pl.BlockSpec(memory_space=pl.ANY)],
        out_specs=[pl.BlockSpec((rt, 3 * SB_W), lambda i: (i, C_SQ * 1024 // (3 * SB_W))), vec, vec],
        out_shape=[jax.ShapeDtypeStruct(into.shape, into.dtype)] + [jax.ShapeDtypeStruct((1, cb), f32)] * 2,
        input_output_aliases={7: 0}, compiler_params=_params(1),
    )(proj, proj, gq, gk, dqs, dks, dvs, into)


def _conv_taps(ext, rt):
    taps = []
    for k in range(DN_CONV):
        s = DN_CONV - 1 - k
        taps.append((pltpu.roll(ext, s, axis=0) if s else ext)[SUB:SUB + rt])
    return taps


def _conv_act(taps, w, l2):
    y = taps[0] * w[0:1]
    for k in range(1, DN_CONV):
        y = y + taps[k] * w[k:k + 1]
    s = _sigmoid(y)
    a = y * s
    if l2:
        n = lax.rsqrt(jnp.sum(a * a, axis=-1, keepdims=True) + L2_EPS)
        return y, s, a, n
    return y, s, a, None


def conv_fwd(proj, w8, col_blk, ncb, l2, name):
    tp = proj.shape[0]
    rt = _row_tile(tp)
    hb = rt // SUB
    cw = min(CONV_W, ncb * LANE)
    cb0 = col_blk * LANE // cw

    def body(x_ref, h_ref, w_ref, o_ref):
        i = pl.program_id(1)
        first = (i > 0).astype(f32)
        for s in range(cw // LANE):
            sl = slice(s * LANE, (s + 1) * LANE)
            ext = jnp.concatenate([h_ref[:, sl] * first, x_ref[:, sl]], axis=0)
            _, _, a, n = _conv_act(_conv_taps(ext, rt), w_ref[:, sl], l2)
            o_ref[:, sl] = a * n if l2 else a

    return pl.pallas_call(
        body, name=name, grid=(ncb * LANE // cw, tp // rt),
        in_specs=[pl.BlockSpec((rt, cw), lambda j, i: (i, j + cb0)),
                  pl.BlockSpec((SUB, cw), lambda j, i: (jnp.maximum(i * hb - 1, 0), j + cb0)),
                  pl.BlockSpec((SUB, cw), lambda j, i: (0, j))],
        out_specs=pl.BlockSpec((rt, cw), lambda j, i: (i, j)),
        out_shape=jax.ShapeDtypeStruct((tp, ncb * LANE), f32), compiler_params=_params(2),
    )(proj, proj, w8)


def conv_bwd(proj, w8, dout, into, col_blk, ncb, l2, name):
    tp = proj.shape[0]
    rt = _row_tile(tp)
    hb = rt // SUB
    nr = tp // rt
    last8 = tp // SUB - 1
    cw = min(CONV_W, ncb * LANE)
    cb0 = col_blk * LANE // cw
    n = rt + SUB

    def body(x_ref, xb_ref, xf_ref, w_ref, d_ref, df_ref, into_ref, o_ref, dw_ref):
        i = pl.program_id(1)
        first = (i > 0).astype(f32)
        last = (i < nr - 1).astype(f32)
        rows = lax.broadcasted_iota(jnp.int32, (SUB, LANE), 0)
        for s in range(cw // LANE):
            sl = slice(s * LANE, (s + 1) * LANE)
            ext = jnp.concatenate([xb_ref[:, sl] * first, x_ref[:, sl], xf_ref[:, sl] * last], axis=0)
            taps = _conv_taps(ext, n)
            w = w_ref[:, sl]
            y, sg, a, nrm = _conv_act(taps, w, l2)
            da = jnp.concatenate([d_ref[:, sl], df_ref[:, sl] * last], axis=0)
            if l2:
                out = a * nrm
                da = nrm * (da - out * jnp.sum(da * out, axis=-1, keepdims=True))
            dy = da * sg * (1.0 + y * (1.0 - sg))
            part = jnp.zeros((SUB, LANE), f32)
            for k in range(DN_CONV):
                part = part + jnp.where(rows == k, jnp.sum(taps[k][0:rt] * dy[0:rt], axis=0, keepdims=True), 0.0)

            @pl.when(i == 0)
            def _(sl=sl, part=part):
                dw_ref[:, sl] = part

            @pl.when(i > 0)
            def _(sl=sl, part=part):
                dw_ref[:, sl] += part

            acc = None
            for k in range(DN_CONV):
                up = DN_CONV - 1 - k
                term = (pltpu.roll(dy, n - up, axis=0) if up else dy)[0:rt] * w[k:k + 1]
                acc = term if acc is None else acc + term
            o_ref[:, sl] = acc.astype(bf16)

    after = lambda j, i: (jnp.minimum((i + 1) * hb, last8), j)
    return pl.pallas_call(
        body, name=name, grid=(ncb * LANE // cw, nr),
        in_specs=[pl.BlockSpec((rt, cw), lambda j, i: (i, j + cb0)),
                  pl.BlockSpec((SUB, cw), lambda j, i: (jnp.maximum(i * hb - 1, 0), j + cb0)),
                  pl.BlockSpec((SUB, cw), lambda j, i: (jnp.minimum((i + 1) * hb, last8), j + cb0)),
                  pl.BlockSpec((SUB, cw), lambda j, i: (0, j)),
                  pl.BlockSpec((rt, cw), lambda j, i: (i, j)), pl.BlockSpec((SUB, cw), after),
                  pl.BlockSpec(memory_space=pl.ANY)],
        out_specs=[pl.BlockSpec((rt, cw), lambda j, i: (i, j + cb0)), pl.BlockSpec((SUB, cw), lambda j, i: (0, j))],
        out_shape=[jax.ShapeDtypeStruct(into.shape, into.dtype), jax.ShapeDtypeStruct((SUB, ncb * LANE), f32)],
        input_output_aliases={6: 0}, compiler_params=_params(2),
    )(proj, proj, proj, w8, dout, dout, into)


def _ab_common(p, al, dtb, r0):
    rows = r0 + lax.broadcasted_iota(jnp.int32, p.shape, 0)
    mask = (rows >= P0).astype(f32)
    xx = p + dtb
    sp = jnp.maximum(xx, 0.0) + _log1p_small(jnp.exp(-jnp.abs(xx)))
    ea = jnp.exp(al)
    g = -ea * sp * mask
    beta = _sigmoid(p) * mask
    return g, beta, _sigmoid(xx), ea, mask


def _chunk_tri(rt, later):
    r = lax.broadcasted_iota(jnp.int32, (rt, rt), 0)
    c = lax.broadcasted_iota(jnp.int32, (rt, rt), 1)
    shift = CHUNK.bit_length() - 1
    same = jnp.right_shift(r, shift) == jnp.right_shift(c, shift)
    return jnp.logical_and(same, c >= r if later else c <= r).astype(f32)


def ab_fwd(pab, al, dtb):
    tp = pab.shape[0]
    rt = _row_tile(tp)
    assert rt % CHUNK == 0

    def body(p_ref, al_ref, dt_ref, g_ref, b_ref):
        i = pl.program_id(0)
        g, beta, _, _, _ = _ab_common(p_ref[...], al_ref[...], dt_ref[...], i * rt)
        gam = _hdot(_chunk_tri(rt, False), g)
        for h in range(HEADS):
            g_ref[h] = jnp.broadcast_to(gam[:, h:h + 1], (rt, LANE))
            b_ref[h] = jnp.broadcast_to(beta[:, HEADS + h:HEADS + h + 1], (rt, LANE))

    vec = pl.BlockSpec((1, LANE), lambda i: (0, 0))
    out = pl.BlockSpec((HEADS, rt, LANE), lambda i: (0, i, 0))
    return pl.pallas_call(
        body, name="ab_fwd", grid=(tp // rt,), in_specs=[pl.BlockSpec((rt, LANE), lambda i: (i, 0)), vec, vec],
        out_specs=[out, out], out_shape=[jax.ShapeDtypeStruct((HEADS, tp, LANE), f32)] * 2, compiler_params=_params(1),
    )(pab, al, dtb)


def ab_bwd(pab, al, dtb, dg, db):
    tp = pab.shape[0]
    rt = _row_tile(tp)

    def body(p_ref, al_ref, dt_ref, dg_ref, db_ref, dp_ref, dal_ref, ddt_ref):
        i = pl.program_id(0)
        g, beta, sx, ea, mask = _ab_common(p_ref[...], al_ref[...], dt_ref[...], i * rt)
        lanes = lax.broadcasted_iota(jnp.int32, (rt, LANE), 1)
        dgl = jnp.zeros((rt, LANE), f32)
        dbl = jnp.zeros((rt, LANE), f32)
        for h in range(HEADS):
            dgl = dgl + jnp.where(lanes == h, dg_ref[h], 0.0)
            dbl = dbl + jnp.where(lanes == HEADS + h, db_ref[h], 0.0)
        dgl = _hdot(_chunk_tri(rt, True), dgl)
        dxx = dgl * (-ea) * sx * mask
        dp_ref[...] = (dxx + dbl * beta * (1.0 - beta)).astype(bf16)
        pal = jnp.sum(dgl * g, axis=0, keepdims=True)
        pdt = jnp.sum(dxx, axis=0, keepdims=True)

        @pl.when(i == 0)
        def _():
            dal_ref[...] = pal
            ddt_ref[...] = pdt

        @pl.when(i > 0)
        def _():
            dal_ref[...] += pal
            ddt_ref[...] += pdt

    vec = pl.BlockSpec((1, LANE), lambda i: (0, 0))
    row = pl.BlockSpec((rt, LANE), lambda i: (i, 0))
    big = pl.BlockSpec((HEADS, rt, LANE), lambda i: (0, i, 0))
    return pl.pallas_call(
        body, name="ab_bwd", grid=(tp // rt,), in_specs=[row, vec, vec, big, big], out_specs=[row, vec, vec],
        out_shape=[jax.ShapeDtypeStruct((tp, LANE), bf16), jax.ShapeDtypeStruct((1, LANE), f32),
                   jax.ShapeDtypeStruct((1, LANE), f32)],
        compiler_params=_params(1),
    )(pab, al, dtb, dg, db)


class _Chunk:
    pass


def _gdn_chunk(q, k, v, gcol, bcol, grow8, saved=None):
    C = CHUNK
    R = range(len(q))
    X = _Chunk()
    ri = lax.broadcasted_iota(jnp.int32, (C, C), 0)
    ci = lax.broadcasted_iota(jnp.int32, (C, C), 1)
    eye = (ri == ci).astype(f32)
    gam = list(gcol)
    gam_row = [grow8[h][0:1, 0:C] for h in R]
    X.ri, X.ci = ri, ci
    X.Dm = [jnp.where(ri >= ci, jnp.exp(jnp.minimum(gam[h][:, 0:C] - gam_row[h], 0.0)), 0.0) for h in R]
    X.eg = [jnp.exp(gam[h]) for h in R]
    gl = [gam[h][C - 1:C, :] for h in R]
    X.egl = [jnp.exp(gl[h]) for h in R]
    X.kdec = [jnp.exp(gl[h] - gam[h]) for h in R]
    X.qs = [q[h] * (DN_DK ** -0.5) for h in R]
    X.kb = [k[h] * bcol[h] for h in R]
    kk = [_dot_nt(X.kb[h], k[h]) for h in R]
    qk = [_dot_nt(X.qs[h], k[h]) for h in R]
    X.A = [jnp.where(ri > ci, kk[h] * X.Dm[h], 0.0) for h in R]
    assert C == 64
    X.b2 = [jnp.concatenate([bcol[h], bcol[h]], axis=-1) for h in R]
    if saved is not None:
        X.T, X.u, X.w = saved
    else:
        b16 = jnp.right_shift(ri, 4) == jnp.right_shift(ci, 4)
        b32 = jnp.right_shift(ri, 5) == jnp.right_shift(ci, 5)
        P = [jnp.where(b16, X.A[h], 0.0) for h in R]
        T = [eye - P[h] for h in R]
        for _ in range(3):
            P = [_hdot(P[h], P[h]) for h in R]
            T = [T[h] + _hdot(T[h], P[h]) for h in R]
        for off in (jnp.logical_and(b32, jnp.logical_not(b16)), jnp.logical_not(b32)):
            AT = [_hdot(jnp.where(off, X.A[h], 0.0), T[h]) for h in R]
            T = [T[h] - _hdot(T[h], AT[h]) for h in R]
        X.T = T
        X.u = [_hdot(T[h], v[h] * X.b2[h]) for h in R]
        X.w = [_hdot(T[h], X.kb[h] * X.eg[h]) for h in R]
    X.attn = [qk[h] * X.Dm[h] for h in R]
    X.qg = [X.qs[h] * X.eg[h] for h in R]
    X.kg = [k[h] * X.kdec[h] for h in R]
    return X


def gdn_fwd(q, k, v, gc, bc, grow):
    tp = q.shape[0]
    nc = tp // CHUNK
    hb = GDN_HEADS_PER_STEP

    def body(q_ref, k_ref, v_ref, gc_ref, bc_ref, gr_ref, o_ref, ss_ref, t_ref, u_ref, w_ref, S_ref):
        c = pl.program_id(1)

        @pl.when(c == 0)
        def _():
            S_ref[...] = jnp.zeros_like(S_ref)

        R = range(hb)
        qc = [slice(h * DN_DK, (h + 1) * DN_DK) for h in R]
        vc = [slice(h * DN_DV, (h + 1) * DN_DV) for h in R]
        X = _gdn_chunk([q_ref[:, qc[h]] for h in R], [k_ref[:, qc[h]] for h in R], [v_ref[:, vc[h]] for h in R],
                       [gc_ref[h] for h in R], [bc_ref[h] for h in R], [gr_ref[h] for h in R])
        S = [S_ref[h] for h in R]
        for h in R:
            ss_ref[h, 0] = S[h]
            t_ref[h, 0] = X.T[h]
            u_ref[h, 0] = X.u[h]
            w_ref[h, 0] = X.w[h]
        wS = [_dot(X.w[h], S[h]) for h in R]
        qS = [_dot(X.qg[h], S[h]) for h in R]
        vn = [X.u[h] - wS[h] for h in R]
        av = [_dot(X.attn[h], vn[h]) for h in R]
        kv = [_dot_tn(X.kg[h], vn[h]) for h in R]
        for h in R:
            o_ref[:, vc[h]] = qS[h] + av[h]
            S_ref[h] = S[h] * X.egl[h][:, 0:1] + kv[h]

    qk = pl.BlockSpec((CHUNK, hb * DN_DK), lambda g, c: (c, g))
    vv = pl.BlockSpec((CHUNK, hb * DN_DV), lambda g, c: (c, g))
    col = pl.BlockSpec((hb, CHUNK, LANE), lambda g, c: (g, c, 0))
    row = pl.BlockSpec((hb, SUB, LANE), lambda g, c: (g, c, 0))
    per = lambda r, w: pl.BlockSpec((hb, 1, r, w), lambda g, c: (g, c, 0, 0))
    keep = lambda r, w: jax.ShapeDtypeStruct((HEADS, nc, r, w), f32)
    return pl.pallas_call(
        body, name="gdn_fwd", grid=(HEADS // hb, nc), in_specs=[qk, qk, vv, col, col, row],
        out_specs=[vv, per(DN_DK, DN_DV), per(CHUNK, CHUNK), per(CHUNK, DN_DV), per(CHUNK, DN_DK)],
        out_shape=[jax.ShapeDtypeStruct((tp, DN_V), f32), keep(DN_DK, DN_DV), keep(CHUNK, CHUNK), keep(CHUNK, DN_DV),
                   keep(CHUNK, DN_DK)],
        scratch_shapes=[pltpu.VMEM((hb, DN_DK, DN_DV), f32)], compiler_params=_params(2),
    )(q, k, v, gc, bc, grow)


def gdn_bwd(q, k, v, gc, bc, grow, states, kept, do, rider=None):
    tp = q.shape[0]
    nc = tp // CHUNK
    C = CHUNK
    hb = GDN_HEADS_PER_STEP
    grid = (HEADS // hb, nc)
    split, ride_first, ride_last = _ride(rider, 11, 5, grid)

    def body(*refs):
        ((q_ref, k_ref, v_ref, gc_ref, bc_ref, gr_ref, ss_ref, t_ref, u_ref, w_ref, do_ref),
         (dq_ref, dk_ref, dv_ref, dg_ref, db_ref), (rin, rout, rest)) = split(refs)
        dS_ref, ride = rest[0], (rin, rout, rest[1:])
        ride_first(ride)
        c = pl.program_id(1)

        @pl.when(c == 0)
        def _():
            dS_ref[...] = jnp.zeros_like(dS_ref)

        R = range(hb)
        qc = [slice(h * DN_DK, (h + 1) * DN_DK) for h in R]
        vc = [slice(h * DN_DV, (h + 1) * DN_DV) for h in R]
        k_ = [k_ref[:, qc[h]] for h in R]
        v_ = [v_ref[:, vc[h]] for h in R]
        bcol = [bc_ref[h] for h in R]
        X = _gdn_chunk([q_ref[:, qc[h]] for h in R], k_, v_, [gc_ref[h] for h in R], bcol, [gr_ref[h] for h in R],
                       saved=([t_ref[h, 0] for h in R], [u_ref[h, 0] for h in R], [w_ref[h, 0] for h in R]))
        ri, ci = X.ri, X.ci
        S = [ss_ref[h, 0] for h in R]
        do_ = [do_ref[:, vc[h]] for h in R]
        dSn = [dS_ref[h] for h in R]
        wS = [_dot(X.w[h], S[h]) for h in R]
        ado = [_dot_tn(X.attn[h], do_[h]) for h in R]
        kdS = [_dot(X.kg[h], dSn[h]) for h in R]
        d_qg = [_dot_nt(do_[h], S[h]) for h in R]
        qdo = [_dot_tn(X.qg[h], do_[h]) for h in R]
        vn = [X.u[h] - wS[h] for h in R]
        d_vn = [ado[h] + kdS[h] for h in R]
        dovn = [_dot_nt(do_[h], vn[h]) for h in R]
        d_kg = [_dot_nt(vn[h], dSn[h]) for h in R]
        wdv = [_dot_tn(X.w[h], d_vn[h]) for h in R]
        dw = [-_dot_nt(d_vn[h], S[h]) for h in R]
        for h in R:
            dS_ref[h] = qdo[h] + X.egl[h][:, 0:1] * dSn[h] - wdv[h]
        dattn = [jnp.where(ri >= ci, dovn[h], 0.0) for h in R]
        dRu = [_hdot_tn(X.T[h], d_vn[h]) for h in R]
        dRw = [_hdot_tn(X.T[h], dw[h]) for h in R]
        dAu = [_hdot_nt(dRu[h], X.u[h]) for h in R]
        dAw = [_hdot_nt(dRw[h], X.w[h]) for h in R]
        dA = [jnp.where(ri > ci, -(dAu[h] + dAw[h]), 0.0) for h in R]
        dKK = [dA[h] * X.Dm[h] for h in R]
        dQK = [dattn[h] * X.Dm[h] for h in R]
        E = [dA[h] * X.A[h] + dattn[h] * X.attn[h] for h in R]
        dkb = [_dot(dKK[h], k_[h]) + dRw[h] * X.eg[h] for h in R]
        dk1 = [_dot_tn(dKK[h], X.kb[h]) for h in R]
        dqs = [_dot(dQK[h], k_[h]) + d_qg[h] * X.eg[h] for h in R]
        dk2 = [_dot_tn(dQK[h], X.qs[h]) for h in R]
        ones = jnp.ones((C, LANE), f32)
        colE = [_hdot_tn(E[h], ones) for h in R]
        rows = lax.broadcasted_iota(jnp.int32, (C, LANE), 0)
        dgam = []
        for h in R:
            t = d_kg[h] * X.kg[h]
            dgl = _allsum(t) + X.egl[h][:, 0:1] * _allsum(S[h] * dSn[h])
            g = (_rowsum(E[h]) - colE[h] + _rowsum(dRw[h] * (X.kb[h] * X.eg[h])) + _rowsum(d_qg[h] * X.qg[h])
                 - _rowsum(t))
            dgam.append(g + jnp.where(rows == C - 1, dgl, 0.0))
        for h in R:
            dv_ref[:, vc[h]] = dRu[h] * X.b2[h]
            dbeta = _rowsum(dRu[h] * v_[h]) + _rowsum(dkb[h] * k_[h])
            dq_ref[:, qc[h]] = dqs[h] * (DN_DK ** -0.5)
            dk_ref[:, qc[h]] = dk1[h] + dk2[h] + dkb[h] * bcol[h] + d_kg[h] * X.kdec[h]
            dg_ref[h] = dgam[h]
            db_ref[h] = jnp.broadcast_to(dbeta, (C, LANE))
        ride_last(ride)

    rc = lambda c: nc - 1 - c
    qk = pl.BlockSpec((CHUNK, hb * DN_DK), lambda g, c: (rc(c), g))
    vv = pl.BlockSpec((CHUNK, hb * DN_DV), lambda g, c: (rc(c), g))
    col = pl.BlockSpec((hb, CHUNK, LANE), lambda g, c: (g, rc(c), 0))
    row = pl.BlockSpec((hb, SUB, LANE), lambda g, c: (g, rc(c), 0))
    st = pl.BlockSpec((hb, 1, DN_DK, DN_DV), lambda g, c: (g, rc(c), 0, 0))
    r_ins = rider.ins if rider else []
    r_outs = rider.out_shapes if rider else []
    per = lambda r, w: pl.BlockSpec((hb, 1, r, w), lambda g, c: (g, rc(c), 0, 0))
    res = pl.pallas_call(
        body, name="gdn_bwd", grid=grid,
        in_specs=[qk, qk, vv, col, col, row, st, per(C, C), per(C, DN_DV), per(C, DN_DK), vv] + [_ANY] * len(r_ins),
        out_specs=[qk, qk, vv, col, col] + [_ANY] * len(r_outs),
        out_shape=[jax.ShapeDtypeStruct((tp, DN_QK), f32), jax.ShapeDtypeStruct((tp, DN_QK), f32),
                   jax.ShapeDtypeStruct((tp, DN_V), f32), jax.ShapeDtypeStruct((HEADS, tp, LANE), f32),
                   jax.ShapeDtypeStruct((HEADS, tp, LANE), f32)] + list(r_outs),
        scratch_shapes=[pltpu.VMEM((hb, DN_DK, DN_DV), f32)] + (rider.scratch if rider else []),
        compiler_params=_params(2),
    )(q, k, v, gc, bc, grow, states, *kept, do, *r_ins)
    return res[:5], res[5:]


def _cumsum_after(x, nb, us, pieces=2):
    B, n = SB_BLOCK, x.shape[0]
    hi = x.astype(bf16)
    parts = (hi, (x - hi.astype(f32)).astype(bf16)) if pieces == 2 else (hi,)
    rows = [p[:, b * B:(b + 1) * B] for p in parts for b in range(nb)]
    r = jnp.dot(jnp.concatenate(rows, axis=0), us, preferred_element_type=f32)
    out = [r[b * n:(b + 1) * n] for b in range(nb)]
    if pieces == 2:
        out = [out[b] + r[(nb + b) * n:(nb + b + 1) * n] for b in range(nb)]
    return out[0] if nb == 1 else jnp.concatenate(out, axis=1)


def _later_blocks(x, nb, carry):
    B = SB_BLOCK
    tot = [_rowsum(x[:, b * B:(b + 1) * B]) for b in range(nb)]
    offs = [None] * nb
    run = carry
    for b in range(nb - 1, -1, -1):
        offs[b] = jnp.broadcast_to(run, (x.shape[0], B))
        run = run + tot[b]
    return (offs[0] if nb == 1 else jnp.concatenate(offs, axis=1)), run


def _sb_group(i, t):
    top = (i + 1) * (SB_QB // SB_BLOCK) - 1 - SB_GROUP * t
    jlo = jnp.maximum(top - SB_GROUP + 1, 0)
    rows = pl.ds(pl.multiple_of(jlo * SB_BLOCK, SB_BLOCK), SB_GROUP * SB_BLOCK)
    return jlo, rows, (top + 1) * SB_BLOCK


def _sb_weights(q, kcat, i, jlo, kend, cs, us, masked):
    B, nb = SB_BLOCK, SB_GROUP
    R = range(len(q))
    z = [_dot_nt(q[h], kcat[h]) * (SB_DH ** -0.5) for h in R]
    e = [jnp.exp(-jnp.abs(z[h])) for h in R]
    l1p = [jnp.log(1.0 + e[h]) for h in R]
    lsp = [jnp.minimum(z[h], 0.0) - l1p[h] for h in R]
    lk = [lsp[h] - z[h] for h in R]
    vis = None
    if masked:
        qpos = i * SB_QB + lax.broadcasted_iota(jnp.int32, (SB_QB, nb * B), 0)
        kpos = jlo * B + lax.broadcasted_iota(jnp.int32, (SB_QB, nb * B), 1)
        vis = jnp.logical_and(kpos < jnp.minimum(qpos, kend), kpos >= P0)
        lk = [jnp.where(vis, lk[h], 0.0) for h in R]
    later = [_later_blocks(lk[h], nb, cs[h]) for h in R]
    cum = [_cumsum_after(lk[h], nb, us) for h in R]
    w = [jnp.exp(lsp[h] + cum[h] + later[h][0]) for h in R]
    if masked:
        w = [jnp.where(vis, w[h], 0.0) for h in R]
    return lsp, vis, w, [later[h][1] for h in R]


def _sb_loop(i, step, carry):
    trips = ((i + 1) * (SB_QB // SB_BLOCK) - 1 + SB_GROUP) // SB_GROUP
    carry = step(True)(0, carry)
    carry = lax.fori_loop(1, trips - 1, step(False), carry)
    return lax.fori_loop(jnp.maximum(trips - 1, 1), trips, step(True), carry)


def _ride(rider, n_in, n_out, grid):
    n_rin = len(rider.ins) if rider else 0
    n_rout = len(rider.out_shapes) if rider else 0

    def split(refs):
        ins, rin = refs[:n_in], refs[n_in:n_in + n_rin]
        outs = refs[n_in + n_rin:n_in + n_rin + n_out]
        rout = refs[n_in + n_rin + n_out:n_in + n_rin + n_out + n_rout]
        return ins, outs, (rin, rout, refs[n_in + n_rin + n_out + n_rout:])

    def at(step, fn, r):
        if rider is None:
            return
        cond = None
        for a, g in enumerate(grid):
            c = pl.program_id(a) == (g - 1 if step == "last" else 0)
            cond = c if cond is None else jnp.logical_and(cond, c)

        @pl.when(cond)
        def _():
            fn(*r)

    first = lambda r: at("first", rider.start if rider else None, r)
    last = lambda r: at("last", rider.finish if rider else None, r)
    return split, first, last


def sb_fwd(qs, ks, vs, rider=None):
    tp = qs.shape[0]
    nq = tp // SB_QB
    B, G, hb, QB = SB_BLOCK, SB_GROUP, SB_FWD_HEADS_PER_STEP, SB_QB
    assert tp >= G * B and tp % QB == 0 and QB % B == 0 and G * B >= QB
    grid = (HEADS // hb, nq)
    split, ride_first, ride_last = _ride(rider, 3, 2, grid)

    def body(*refs):
        (q_ref, k_ref, v_ref), (o_ref, ob_ref), ride = split(refs)
        ride_first(ride)
        i = pl.program_id(1)
        R = range(hb)
        hs = [slice(h * SB_DH, (h + 1) * SB_DH) for h in R]
        q = [q_ref[:, hs[h]] for h in R]
        us = (lax.broadcasted_iota(jnp.int32, (B, B), 0) > lax.broadcasted_iota(jnp.int32, (B, B), 1)).astype(bf16)

        def make_step(masked):
            def step(t, carry):
                acc, cs = carry
                jlo, rows, kend = _sb_group(i, t)
                _, _, w, cs = _sb_weights(q, [k_ref[rows, hs[h]] for h in R], i, jlo, kend, cs, us, masked)
                pv = [_dot(w[h], v_ref[rows, hs[h]]) for h in R]
                return tuple(acc[h] + pv[h] for h in R), tuple(cs)
            return step

        carry = (tuple(jnp.zeros((QB, SB_DH), f32) for _ in R), tuple(jnp.zeros((QB, 1), f32) for _ in R))
        acc, _ = _sb_loop(i, make_step, carry)
        for h in R:
            o_ref[:, hs[h]] = acc[h]
            ob_ref[:, hs[h]] = acc[h].astype(bf16)
        ride_last(ride)

    blk = pl.BlockSpec((QB, hb * SB_DH), lambda g, i: (i, g))
    full = pl.BlockSpec((tp, hb * SB_DH), lambda g, i: (0, g))
    r_ins = rider.ins if rider else []
    r_outs = rider.out_shapes if rider else []
    res = pl.pallas_call(
        body, name="sb_fwd", grid=grid, in_specs=[blk, full, full] + [_ANY] * len(r_ins),
        out_specs=[blk, blk] + [_ANY] * len(r_outs),
        out_shape=[jax.ShapeDtypeStruct((tp, SB_W), f32), jax.ShapeDtypeStruct((tp, SB_W), bf16)] + list(r_outs),
        scratch_shapes=rider.scratch if rider else [], compiler_params=_params(2),
    )(qs, ks, vs, *r_ins)
    return res[0], res[1], res[2:]


def sb_bwd(qs, ks, vs, o, do, rider=None):
    tp = qs.shape[0]
    nq = tp // SB_QB
    B, G, hb, QB = SB_BLOCK, SB_GROUP, SB_HEADS_PER_STEP, SB_QB
    assert tp >= G * B and tp % QB == 0 and QB % B == 0 and G * B >= QB
    grid = (HEADS // hb, nq)
    split, ride_first, ride_last = _ride(rider, 5, 3, grid)

    def body(*refs):
        (q_ref, k_ref, v_ref, o_ref, do_ref), (dq_ref, dk_ref, dv_ref), ride = split(refs)
        ride_first(ride)
        i = pl.program_id(1)

        @pl.when(i == 0)
        def _():
            dk_ref[...] = jnp.zeros_like(dk_ref)
            dv_ref[...] = jnp.zeros_like(dv_ref)

        R = range(hb)
        hs = [slice(h * SB_DH, (h + 1) * SB_DH) for h in R]
        q = [q_ref[:, hs[h]] for h in R]
        dob = [do_ref[:, hs[h]].astype(bf16) for h in R]
        et = [_rowsum(dob[h].astype(f32) * o_ref[:, hs[h]]) for h in R]
        us = (lax.broadcasted_iota(jnp.int32, (B, B), 0) > lax.broadcasted_iota(jnp.int32, (B, B), 1)).astype(bf16)

        def make_step(masked):
            def step(t, carry):
                dq, cs, ce = carry
                jlo, rows, kend = _sb_group(i, t)
                kcat = [k_ref[rows, hs[h]] for h in R]
                dwv = [_dot_nt(dob[h], v_ref[rows, hs[h]]) for h in R]
                lsp, vis, w, cs = _sb_weights(q, kcat, i, jlo, kend, cs, us, masked)
                wb = [w[h].astype(bf16) for h in R]
                ee = [dwv[h] * wb[h].astype(f32) for h in R]
                later = [_later_blocks(ee[h], G, ce[h]) for h in R]
                cum = [_cumsum_after(ee[h], G, us) for h in R]
                dz = []
                for h in R:
                    d = ee[h] - jnp.exp(lsp[h]) * (et[h] - (cum[h] + later[h][0]))
                    if masked:
                        d = jnp.where(vis, d, 0.0)
                    dz.append(d.astype(bf16))
                dkj = [_dot_tn(dz[h], q[h]) for h in R]
                dvj = [_dot_tn(wb[h], dob[h]) for h in R]
                dqj = [_dot(dz[h], kcat[h]) for h in R]
                for h in R:
                    dk_ref[rows, hs[h]] += dkj[h]
                    dv_ref[rows, hs[h]] += dvj[h]
                return tuple(dq[h] + dqj[h] for h in R), tuple(cs), tuple(later[h][1] for h in R)
            return step

        z0 = tuple(jnp.zeros((QB, 1), f32) for _ in R)
        dq, _, _ = _sb_loop(i, make_step, (tuple(jnp.zeros((QB, SB_DH), f32) for _ in R), z0, z0))
        for h in R:
            dq_ref[:, hs[h]] = dq[h] * (SB_DH ** -0.5)
        ride_last(ride)

    blk = pl.BlockSpec((QB, hb * SB_DH), lambda g, i: (i, g))
    full = pl.BlockSpec((tp, hb * SB_DH), lambda g, i: (0, g))
    r_ins = rider.ins if rider else []
    r_outs = rider.out_shapes if rider else []
    res = pl.pallas_call(
        body, name="sb_bwd", grid=grid, in_specs=[blk, full, full, blk, blk] + [_ANY] * len(r_ins),
        out_specs=[blk, full, full] + [_ANY] * len(r_outs),
        out_shape=[jax.ShapeDtypeStruct((tp, SB_W), f32)] * 3 + list(r_outs),
        scratch_shapes=rider.scratch if rider else [], compiler_params=_params(2),
    )(qs, ks, vs, o, do, *r_ins)
    return res[:3], res[3:]


def adamw(w, g, m, v, name, rider=None):
    r, c = w.shape
    rt = _tile(r, 128, SUB) if r % SUB == 0 else r
    blk = pl.BlockSpec((rt, c), lambda i: (i, 0))
    c1 = 1.0 - ADAM_B1 ** ADAM_STEP
    c2 = 1.0 - ADAM_B2 ** ADAM_STEP
    grid = (r // rt,)
    split, ride_first, ride_last = _ride(rider, 4, 3, grid)

    def body(*refs):
        (w_ref, g_ref, m_ref, v_ref), (d_ref, mo_ref, vo_ref), ride = split(refs)
        ride_first(ride)
        g_ = g_ref[...]
        m_ = ADAM_B1 * m_ref[...] + (1.0 - ADAM_B1) * g_
        v_ = ADAM_B2 * v_ref[...] + (1.0 - ADAM_B2) * (g_ * g_)
        mo_ref[...] = m_
        vo_ref[...] = v_
        d_ref[...] = -ADAM_LR * ((m_ / c1) / (jnp.sqrt(v_ / c2) + ADAM_EPS) + ADAM_WD * w_ref[...])
        ride_last(ride)

    r_ins = rider.ins if rider else []
    r_outs = rider.out_shapes if rider else []
    res = pl.pallas_call(
        body, name=name, grid=grid, in_specs=[blk] * 4 + [_ANY] * len(r_ins), out_specs=[blk] * 3 + [_ANY] * len(r_outs),
        out_shape=[jax.ShapeDtypeStruct((r, c), f32)] * 3 + list(r_outs),
        scratch_shapes=rider.scratch if rider else [], compiler_params=_params(1),
    )(w, g, m, v, *r_ins)
    return res[:3], res[3:]


def sum_slots(x, name):
    n, r, c = x.shape
    rt = _tile(r, 128, SUB) if r % SUB == 0 else r
    blk = pl.BlockSpec((n, rt, c), lambda i: (0, i, 0))

    def body(x_ref, o_ref):
        acc = x_ref[0].astype(f32)
        for s in range(1, n):
            acc = acc + x_ref[s].astype(f32)
        o_ref[...] = acc

    return pl.pallas_call(
        body, name=name, grid=(r // rt,), in_specs=[blk], out_specs=pl.BlockSpec((rt, c), lambda i: (i, 0)),
        out_shape=jax.ShapeDtypeStruct((r, c), f32), compiler_params=_params(1),
    )(x)


def sum_chip_parts(parts, slots, chip, name):
    n, r, c = slots.shape
    rt = _tile(r, 128, SUB) if r % SUB == 0 else r

    def body(chip_ref, own_ref, a_ref, b_ref, c_ref, o_ref):
        o_ref[...] = ((own_ref[0].astype(f32) + a_ref[0].astype(f32)) + b_ref[0].astype(f32)) + c_ref[0].astype(f32)

    def at(rel):
        return pl.BlockSpec((1, rt, c), lambda i, chip_ref: (jnp.bitwise_xor(chip_ref[0], rel), i, 0))

    return pl.pallas_call(
        body, name=name,
        grid_spec=pltpu.PrefetchScalarGridSpec(
            num_scalar_prefetch=1, grid=(r // rt,), in_specs=[at(0), at(1), at(2), at(3)],
            out_specs=pl.BlockSpec((rt, c), lambda i, chip_ref: (i, 0))),
        out_shape=jax.ShapeDtypeStruct((r, c), f32), compiler_params=_params(1),
    )(jnp.reshape(chip, (1,)).astype(jnp.int32), parts, slots, slots, slots)


def add2(a, b, name, out_dtype=f32):
    n, r, c = a.shape
    rt = _tile(r, 64, SUB) if r % SUB == 0 else r
    blk = pl.BlockSpec((n, rt, c), lambda i: (0, i, 0))

    def body(a_ref, b_ref, o_ref):
        o_ref[...] = (a_ref[...] + b_ref[...]).astype(out_dtype)

    return pl.pallas_call(
        body, name=name, grid=(r // rt,), in_specs=[blk, blk], out_specs=blk,
        out_shape=jax.ShapeDtypeStruct((n, r, c), out_dtype), compiler_params=_params(1),
    )(a, b)


_ANY = pl.BlockSpec(memory_space=pl.ANY)
_MESH = pl.DeviceIdType.MESH


def _coords():
    return lax.axis_index("x"), lax.axis_index("y"), lax.axis_index("c")


def _chip_peer(x, y, r):
    return x ^ (r >> 1), y ^ (r & 1)


class _Exchange:
    def __init__(self, ins, out_shapes, scratch, start, finish):
        self.ins, self.out_shapes, self.scratch, self.start, self.finish = ins, out_shapes, scratch, start, finish

    def split(self, refs):
        n, m = len(self.ins), len(self.out_shapes)
        return refs[:n], refs[n:n + m], refs[n + m:]


def run_exchange(ex, name):
    def body(*refs):
        ins, outs, sems = ex.split(refs)
        ex.start(ins, outs, sems)
        ex.finish(ins, outs, sems)

    return pl.pallas_call(body, name=name, in_specs=[_ANY] * len(ex.ins), out_specs=[_ANY] * len(ex.out_shapes),
                          out_shape=ex.out_shapes, scratch_shapes=ex.scratch)(*ex.ins)


def gather_chips(big, small):
    nb, n = len(big), len(big) + len(small)
    shards = list(big) + list(small)
    kb = nb * (N_CHIPS - 1)
    k = n * (N_CHIPS - 1)

    def own(src, dst, sems, t):
        x, y, c = _coords()
        return pltpu.make_async_remote_copy(src[t], dst[t].at[2 * x + y], sems[4].at[t], sems[5].at[t],
                                            device_id=(x, y, 1 - c), device_id_type=_MESH)

    def copies(src, dst, sems):
        send, recv, fsend, frecv = sems[:4]
        x, y, c = _coords()
        sib = (x, y, 1 - c)
        peers = [_chip_peer(x, y, r) for r in range(1, N_CHIPS)]

        def direct(t, j, slot):
            s = t * (N_CHIPS - 1) + j
            if t < nb:
                return pltpu.make_async_remote_copy(src[t].at[c], dst[t].at[slot, c], send.at[s], recv.at[s],
                                                    device_id=(*peers[j], c), device_id_type=_MESH)
            return pltpu.make_async_remote_copy(src[t], dst[t].at[slot], send.at[s], recv.at[s],
                                                device_id=(*peers[j], c), device_id_type=_MESH)

        def passed(t, j, half):
            s = t * (N_CHIPS - 1) + j
            px, py = peers[j]
            part = dst[t].at[2 * px + py, half]
            return pltpu.make_async_remote_copy(part, part, fsend.at[s], frecv.at[s], device_id=sib, device_id_type=_MESH)

        return direct, passed, peers, 2 * x + y, c

    def start(src, dst, sems):
        direct, _, _, me, _ = copies(src, dst, sems)
        for t in range(n):
            for j in range(N_CHIPS - 1):
                direct(t, j, me).start()
        for t in range(n):
            own(src, dst, sems, t).start()

    def finish(src, dst, sems):
        direct, passed, peers, me, c = copies(src, dst, sems)
        for t in range(n):
            own(src, dst, sems, t).wait_recv()
        fwd = []
        for t in range(nb):
            for j in range(N_CHIPS - 1):
                px, py = peers[j]
                direct(t, j, 2 * px + py).wait_recv()
                fwd.append(passed(t, j, c))
                fwd[-1].start()
        for t in range(nb, n):
            for j in range(N_CHIPS - 1):
                px, py = peers[j]
                direct(t, j, 2 * px + py).wait_recv()
        for t in range(nb):
            for j in range(N_CHIPS - 1):
                passed(t, j, 1 - c).wait_recv()
        for t in range(n):
            for j in range(N_CHIPS - 1):
                direct(t, j, me).wait_send()
        for cp in fwd:
            cp.wait_send()
        for t in range(n):
            own(src, dst, sems, t).wait_send()

    return _Exchange(shards, [jax.ShapeDtypeStruct((N_CHIPS,) + s.shape, s.dtype) for s in shards],
                     [pltpu.SemaphoreType.DMA((k,)), pltpu.SemaphoreType.DMA((k,)),
                      pltpu.SemaphoreType.DMA((max(kb, 1),)), pltpu.SemaphoreType.DMA((max(kb, 1),)),
                      pltpu.SemaphoreType.DMA((n,)), pltpu.SemaphoreType.DMA((n,))], start, finish)


def sibling_swap(grads):
    pairs = [(t, o) for t, g in enumerate(grads) for o in range(g.shape[0])]
    k = len(pairs)

    def copies(src, dst, sems):
        send, recv = sems
        x, y, c = _coords()
        return [pltpu.make_async_remote_copy(src[t].at[o, 1 - c], dst[t].at[o], send.at[s], recv.at[s],
                                             device_id=(x, y, 1 - c), device_id_type=_MESH)
                for s, (t, o) in enumerate(pairs)]

    def start(src, dst, sems):
        for cp in copies(src, dst, sems):
            cp.start()

    def finish(src, dst, sems):
        cps = copies(src, dst, sems)
        for cp in cps:
            cp.wait_recv()
        for cp in cps:
            cp.wait_send()

    return _Exchange(list(grads), [jax.ShapeDtypeStruct((g.shape[0],) + g.shape[2:], g.dtype) for g in grads],
                     [pltpu.SemaphoreType.DMA((k,)), pltpu.SemaphoreType.DMA((k,))], start, finish)


def scatter_chips(parts):
    n = len(parts)
    k = n * (N_CHIPS - 1)

    def copy(src, dst, sems, t, r, landing):
        send, recv = sems
        x, y, c = _coords()
        me = 2 * x + y
        px, py = _chip_peer(x, y, r)
        peer = 2 * px + py
        s = t * (N_CHIPS - 1) + r - 1
        return pltpu.make_async_remote_copy(src[t].at[me if landing else peer], dst[t].at[peer if landing else me],
                                            send.at[s], recv.at[s], device_id=(px, py, c), device_id_type=_MESH)

    def start(src, dst, sems):
        for t in range(n):
            for r in range(1, N_CHIPS):
                copy(src, dst, sems, t, r, False).start()

    def finish(src, dst, sems):
        for t in range(n):
            for r in range(1, N_CHIPS):
                copy(src, dst, sems, t, r, True).wait_recv()
        for t in range(n):
            for r in range(1, N_CHIPS):
                copy(src, dst, sems, t, r, False).wait_send()

    return _Exchange(list(parts), [jax.ShapeDtypeStruct(p.shape, p.dtype) for p in parts],
                     [pltpu.SemaphoreType.DMA((k,)), pltpu.SemaphoreType.DMA((k,))], start, finish)


def sibling_send(halves):
    n = len(halves)

    def copies(src, dst, sems):
        send, recv = sems
        x, y, c = _coords()
        return [pltpu.make_async_remote_copy(src[t], dst[t], send.at[t], recv.at[t],
                                             device_id=(x, y, 1 - c), device_id_type=_MESH) for t in range(n)]

    def start(src, dst, sems):
        for cp in copies(src, dst, sems):
            cp.start()

    def finish(src, dst, sems):
        cps = copies(src, dst, sems)
        for cp in cps:
            cp.wait_recv()
        for cp in cps:
            cp.wait_send()

    return _Exchange(list(halves), [jax.ShapeDtypeStruct(h.shape, h.dtype) for h in halves],
                     [pltpu.SemaphoreType.DMA((n,)), pltpu.SemaphoreType.DMA((n,))], start, finish)


def gather_all(block):
    def copies(src, dst, sems, landing):
        send, recv, loc = sems
        x, y, c = _coords()
        me = 4 * x + 2 * y + c
        mine = pltpu.make_async_copy(src[0], dst[0].at[me], loc)
        remote = []
        for r in range(1, N_DEV):
            px, py, pc = x ^ (r >> 2), y ^ ((r >> 1) & 1), c ^ (r & 1)
            slot = 4 * px + 2 * py + pc if landing else me
            remote.append(pltpu.make_async_remote_copy(src[0], dst[0].at[slot], send.at[r - 1], recv.at[r - 1],
                                                       device_id=(px, py, pc), device_id_type=_MESH))
        return mine, remote

    def start(src, dst, sems):
        mine, outs = copies(src, dst, sems, False)
        mine.start()
        for cp in outs:
            cp.start()

    def finish(src, dst, sems):
        mine, lands = copies(src, dst, sems, True)
        for cp in lands:
            cp.wait_recv()
        for cp in lands:
            cp.wait_send()
        mine.wait()

    return _Exchange([block], [jax.ShapeDtypeStruct((N_DEV,) + block.shape, block.dtype)],
                     [pltpu.SemaphoreType.DMA((N_DEV - 1,)), pltpu.SemaphoreType.DMA((N_DEV - 1,)),
                      pltpu.SemaphoreType.DMA(())], start, finish)


def _pad_lanes(v, n=LANE):
    return jnp.pad(v, ((0, 0), (0, n - v.shape[1])))


def _w_in_pieces():
    cs = (PROJ_BIG + 2 * HEADS) // N_CHIPS
    ab_end = AB_COL + 2 * HEADS
    out = []
    for o in range(N_CHIPS):
        lo, hi = o * cs, (o + 1) * cs
        cand = [("big", lo, min(hi, AB_COL), 0), ("ab", max(lo, AB_COL), min(hi, ab_end), AB_COL),
                ("big", max(lo, ab_end), hi, 2 * HEADS)]
        out.append([(s, a - off, b - off) for s, a, b, off in cand if a < b])
    return out


def _split_w_in(w4):
    big, ab = [], []
    for o, pieces in enumerate(_w_in_pieces()):
        at = 0
        for s, a, b in pieces:
            (big if s == "big" else ab).append(w4[o][:, at:at + b - a])
            at += b - a
    return jnp.concatenate(big, axis=1), _pad_lanes(jnp.concatenate(ab, axis=1))


def _join_w_in(big, ab):
    src = {"big": big, "ab": ab}
    return jnp.stack([jnp.concatenate([src[s][:, a:b] for s, a, b in pieces], axis=1) for pieces in _w_in_pieces()])


def _conv_w8(w):
    return jnp.pad(w, ((0, SUB - DN_CONV), (0, 0)))


def _row_layout(gc, tp):
    nc = tp // CHUNK
    g = gc[:, :, 0].reshape(HEADS, nc, 1, CHUNK)
    g = jnp.broadcast_to(g, (HEADS, nc, SUB, CHUNK))
    return jnp.pad(g, ((0, 0), (0, 0), (0, 0), (0, LANE - CHUNK))).reshape(HEADS, nc * SUB, LANE)


def _step(x, meta, W, target, late_weights=None, early_swap=None, early_grads=None, last_grads=None, final_send=None):
    W = dict(W)
    seq = x.shape[0]
    tp = P0 + N_META + seq
    h0 = jnp.concatenate([jnp.zeros((P0, D_MODEL), f32), meta, x], axis=0)
    w_big, w_ab = _split_w_in(W["w_in"])
    cq8, ck8, cv8 = _conv_w8(W["conv_q"]), _conv_w8(W["conv_k"]), _conv_w8(W["conv_v"])
    al, dtb = _pad_lanes(W["dn_a_log"]), _pad_lanes(W["dn_dt_bias"])

    n1 = rms_fwd(h0, W["norm_mix_gain"], "rms1_fwd")
    proj = matmul(n1, w_big, "nn", "proj_fwd")
    pab = matmul(n1, w_ab, "nn", "pab_fwd")
    qn = conv_fwd(proj, cq8, C_DQ * 8, 8, True, "conv_q_fwd")
    kn = conv_fwd(proj, ck8, C_DK * 8, 8, True, "conv_k_fwd")
    va = conv_fwd(proj, cv8, C_DV * 8, 16, False, "conv_v_fwd")
    gc, bc = ab_fwd(pab, al, dtb)
    grow = _row_layout(gc, tp)
    o_dn, states, *kept = gdn_fwd(qn, kn, va, gc, bc, grow)
    on = dn_out_fwd(o_dn, proj, W["dn_out_norm_gain"])
    qs, ks, vs = sb_prep_fwd(proj, W["sb_q_norm_gain"], W["sb_k_norm_gain"])
    o_sb, o_sb16, arrived = sb_fwd(qs, ks, vs, rider=late_weights[0] if late_weights else None)
    if late_weights:
        W.update(late_weights[1](arrived))
    ydn = matmul(on, W["w_branch_dn"], "nn", "ydn_fwd")
    ysb = matmul(o_sb16, W["w_branch_sb"], "nn", "ysb_fwd")
    merged = merge_fwd(proj, ydn, ysb)
    h1 = matmul(merged, W["w_out"], "nn", "wout_fwd", residual=h0)
    n2 = rms_fwd(h1, W["norm_ffn_gain"], "rms2_fwd")
    u = matmul(n2, W["w_ffn_in"], "nn", "ffn_in_fwd", tn_t=512)
    act = swiglu_fwd(u)
    y = matmul(act, W["w_ffn_out"], "nn", "ffn_out_fwd", residual=h1)
    dy, dy16, loss = loss_head(y, target)

    G = {}
    dact = matmul(dy16, W["w_ffn_out"], "nt", "ffn_out_dx", tn_t=1408)
    G["w_ffn_out"] = matmul(act, dy16, "tn", "ffn_out_dw", tm_t=1408)
    dgate, dup = swiglu_bwd(u, dact)
    du = jnp.concatenate([dgate, dup], axis=1)
    dn2 = matmul(du, W["w_ffn_in"], "nt", "ffn_in_dx", tk_t=512)
    G["w_ffn_in"] = matmul(n2, du, "tn", "ffn_in_dw", tn_t=512)
    (dh1, dh1_16, G["norm_ffn_gain"]), _ = rms_bwd(h1, W["norm_ffn_gain"], dn2, dy, "rms2_bwd")
    dmerged = matmul(dh1_16, W["w_out"], "nt", "wout_dx")
    G["w_out"] = matmul(merged, dh1_16, "tn", "wout_dw")
    dyd, dys, d_gates = merge_bwd(proj, ydn, ysb, dmerged)
    don = matmul(dyd, W["w_branch_dn"], "nt", "ydn_dx")
    G["w_branch_dn"] = matmul(on, dyd, "tn", "ydn_dw")
    do_sb = matmul(dys, W["w_branch_sb"], "nt", "ysb_dx")
    G["w_branch_sb"] = matmul(o_sb16, dys, "tn", "ysb_dw")
    do_dn, dproj, G["dn_out_norm_gain"] = dn_out_bwd(o_dn, proj, W["dn_out_norm_gain"], don)
    (dqn, dkn, dva, dgc, dbc), swapped = gdn_bwd(qn, kn, va, gc, bc, grow, states, kept, do_dn,
                                                 rider=early_swap[0](G) if early_swap else None)
    if early_swap:
        early_swap[1](swapped)
    dpab, dal, ddt = ab_bwd(pab, al, dtb, dgc, dbc)
    G["dn_a_log"], G["dn_dt_bias"] = dal[:, :HEADS], ddt[:, :HEADS]
    dproj, dcq = conv_bwd(proj, cq8, dqn, dproj, C_DQ * 8, 8, True, "conv_q_bwd")
    dproj, dck = conv_bwd(proj, ck8, dkn, dproj, C_DK * 8, 8, True, "conv_k_bwd")
    dproj, dcv = conv_bwd(proj, cv8, dva, dproj, C_DV * 8, 16, False, "conv_v_bwd")
    G["conv_q"], G["conv_k"], G["conv_v"] = dcq[:DN_CONV], dck[:DN_CONV], dcv[:DN_CONV]
    (dqs, dks, dvs), delivered = sb_bwd(qs, ks, vs, o_sb, do_sb, rider=early_grads[0](G) if early_grads else None)
    if early_grads:
        early_grads[1](delivered)
    dproj, G["sb_q_norm_gain"], G["sb_k_norm_gain"] = sb_prep_bwd(
        proj, W["sb_q_norm_gain"], W["sb_k_norm_gain"], dqs, dks, dvs, dproj)
    dproj = lax.dynamic_update_slice(dproj, d_gates, (0, C_GDN * 1024))
    dw_big = matmul(n1, dproj, "tn", "proj_dw")
    dw_ab = matmul(n1, dpab, "tn", "pab_dw")
    G["w_in"] = (dw_big, dw_ab)
    if last_grads:
        dn1, delivered = matmul(dproj, w_big, "nt", "proj_dx", tk_t=1024, rider=last_grads[0](G))
        last_grads[1](delivered)
    else:
        dn1 = matmul(dproj, w_big, "nt", "proj_dx", tk_t=1024)
    dn1 = matmul(dpab, w_ab, "nt", "pab_dx", residual=dn1)
    (dh0, _, G["norm_mix_gain"]), sent = rms_bwd(h0, W["norm_mix_gain"], dn1, dh1, "rms1_bwd",
                                                 rider=final_send[0]() if final_send else None)
    if final_send:
        final_send[1](sent)
    G["meta_tokens"] = dh0[P0:P0 + N_META]
    return loss, dh0[P0 + N_META:], G


_BIG = ("w_in", "w_branch_dn", "w_branch_sb", "w_out", "w_ffn_in", "w_ffn_out")
_COL_SHARDED = ("w_in", "w_ffn_in", "meta_tokens", "conv_q", "conv_k", "conv_v")
_SMALL_REPL = ("norm_mix_gain", "norm_ffn_gain", "dn_a_log", "dn_dt_bias", "dn_out_norm_gain", "sb_q_norm_gain",
               "sb_k_norm_gain")
_SMALL_SHARD = ("meta_tokens", "conv_q", "conv_k", "conv_v")
_ORDER = ("meta_tokens", "norm_mix_gain", "w_in", "conv_q", "conv_k", "conv_v", "dn_a_log", "dn_dt_bias",
          "dn_out_norm_gain", "sb_q_norm_gain", "sb_k_norm_gain", "w_branch_dn", "w_branch_sb", "w_out",
          "norm_ffn_gain", "w_ffn_in", "w_ffn_out")


def _unshard(g4, name):
    if name in _COL_SHARDED:
        r, cs = g4.shape[1:]
        return jnp.transpose(g4, (1, 0, 2)).reshape(r, N_CHIPS * cs)
    return g4.reshape((-1,) + g4.shape[2:])


def _to_shards(full, name):
    if name in _COL_SHARDED:
        r, c = full.shape
        return jnp.transpose(full.reshape(r, N_CHIPS, c // N_CHIPS), (1, 0, 2))
    r, c = full.shape
    return full.reshape(N_CHIPS, r // N_CHIPS, c)


def _rows_1024(a):
    r, c = a.shape
    if c >= 1024:
        return a.reshape(r * (c // 1024), 1024)
    return jnp.pad(a, ((0, 0), (0, 1024 - c)))


def kernel(x, meta_tokens, norm_mix_gain, w_in, conv_q, conv_k, conv_v, dn_a_log, dn_dt_bias, dn_out_norm_gain, sb_q_norm_gain, sb_k_norm_gain, w_branch_dn, w_branch_sb, w_out, norm_ffn_gain, w_ffn_in, w_ffn_out, loss_target, m_meta_tokens, m_norm_mix_gain, m_w_in, m_conv_q, m_conv_k, m_conv_v, m_dn_a_log, m_dn_dt_bias, m_dn_out_norm_gain, m_sb_q_norm_gain, m_sb_k_norm_gain, m_w_branch_dn, m_w_branch_sb, m_w_out, m_norm_ffn_gain, m_w_ffn_in, m_w_ffn_out, v_meta_tokens, v_norm_mix_gain, v_w_in, v_conv_q, v_conv_k, v_conv_v, v_dn_a_log, v_dn_dt_bias, v_dn_out_norm_gain, v_sb_q_norm_gain, v_sb_k_norm_gain, v_w_branch_dn, v_w_branch_sb, v_w_out, v_norm_ffn_gain, v_w_ffn_in, v_w_ffn_out):
    Wl = dict(meta_tokens=meta_tokens, norm_mix_gain=norm_mix_gain, w_in=w_in[0], conv_q=conv_q[0], conv_k=conv_k[0],
              conv_v=conv_v[0], dn_a_log=dn_a_log, dn_dt_bias=dn_dt_bias, dn_out_norm_gain=dn_out_norm_gain,
              sb_q_norm_gain=sb_q_norm_gain, sb_k_norm_gain=sb_k_norm_gain, w_branch_dn=w_branch_dn[0],
              w_branch_sb=w_branch_sb[0], w_out=w_out[0], norm_ffn_gain=norm_ffn_gain, w_ffn_in=w_ffn_in[0],
              w_ffn_out=w_ffn_out[0])
    Ml = dict(meta_tokens=m_meta_tokens, norm_mix_gain=m_norm_mix_gain, w_in=m_w_in[0], conv_q=m_conv_q[0],
              conv_k=m_conv_k[0], conv_v=m_conv_v[0], dn_a_log=m_dn_a_log, dn_dt_bias=m_dn_dt_bias,
              dn_out_norm_gain=m_dn_out_norm_gain, sb_q_norm_gain=m_sb_q_norm_gain, sb_k_norm_gain=m_sb_k_norm_gain,
              w_branch_dn=m_w_branch_dn[0], w_branch_sb=m_w_branch_sb[0], w_out=m_w_out[0],
              norm_ffn_gain=m_norm_ffn_gain, w_ffn_in=m_w_ffn_in[0], w_ffn_out=m_w_ffn_out[0])
    Vl = dict(meta_tokens=v_meta_tokens, norm_mix_gain=v_norm_mix_gain, w_in=v_w_in[0], conv_q=v_conv_q[0],
              conv_k=v_conv_k[0], conv_v=v_conv_v[0], dn_a_log=v_dn_a_log, dn_dt_bias=v_dn_dt_bias,
              dn_out_norm_gain=v_dn_out_norm_gain, sb_q_norm_gain=v_sb_q_norm_gain, sb_k_norm_gain=v_sb_k_norm_gain,
              w_branch_dn=v_w_branch_dn[0], w_branch_sb=v_w_branch_sb[0], w_out=v_w_out[0],
              norm_ffn_gain=v_norm_ffn_gain, w_ffn_in=v_w_ffn_in[0], w_ffn_out=v_w_ffn_out[0])
    lead = {n: (1,) if (n in _BIG or n in ("conv_q", "conv_k", "conv_v")) else () for n in _ORDER}

    chip = 2 * lax.axis_index("x") + lax.axis_index("y")
    c = lax.axis_index("c")
    halved = {n: Wl[n].astype(bf16).reshape(2, Wl[n].shape[0] // 2, Wl[n].shape[1]) for n in _BIG}

    def gathered_weights(names, owns, outs):
        res = {}
        for n, g4 in zip(names, outs):
            if n in _BIG:
                g4 = g4.reshape(N_CHIPS, 2 * g4.shape[2], g4.shape[3])
            res[n] = g4 if n == "w_in" else _unshard(g4, n)
        return res

    first = ["w_in"] + list(_SMALL_SHARD)
    first_own = [halved["w_in"]] + [Wl[n] for n in _SMALL_SHARD]
    W = dict(Wl)
    W.update(gathered_weights(first, first_own, run_exchange(gather_chips(first_own[:1], first_own[1:]), "gather_w_in")))
    late = [n for n in _BIG if n != "w_in"]
    late_own = [halved[n] for n in late]
    for n in late:
        del W[n]

    def halves_of(names, G):
        g4 = [_to_shards(G[n], n) for n in names]
        return [g.reshape(N_CHIPS, 2, g.shape[1] // 2, g.shape[2]) for g in g4]

    def pair_added(g42, from_sib, tag, wire):
        mine = [lax.dynamic_index_in_dim(g, c, axis=1, keepdims=False) for g in g42]
        return [add2(a, b, "grad_pair_add_%s%d" % (tag, t), out_dtype=wire)
                for t, (a, b) in enumerate(zip(mine, from_sib))]

    def chip_reduced(parts, slots, tag):
        return [sum_chip_parts(p, s, chip, "grad_chip_sum_%s%d" % (tag, t)) for t, (p, s) in enumerate(zip(parts, slots))]

    early, last = {}, {}

    def early_swap_begin(G):
        early["g42"] = halves_of(late, G)
        return sibling_swap(early["g42"])

    def early_begin(G):
        early["parts"] = pair_added(early["g42"], early["from_sib"], "a", f32)
        return scatter_chips(early["parts"])

    def last_begin(G):
        g2 = [g.reshape(1, 2, g.shape[0] // 2, g.shape[1]) for g in G["w_in"]]
        added = pair_added(g2, run_exchange(sibling_swap(g2), "grad_sibling_swap_b"), "b", bf16)
        last["parts"] = [_join_w_in(added[0][0], added[1][0])]
        return scatter_chips(last["parts"])

    loss, grad_x, G = _step(
        x[0], W["meta_tokens"], W, loss_target[0],
        late_weights=(gather_chips(late_own, []), lambda outs: gathered_weights(late, late_own, outs)),
        early_swap=(early_swap_begin, lambda outs: early.update(from_sib=outs)),
        early_grads=(early_begin, lambda slots: early.update(halves=chip_reduced(early["parts"], slots, "a"))),
        last_grads=(last_begin, lambda slots: last.update(halves=chip_reduced(last["parts"], slots, "b"))),
        final_send=(lambda: sibling_send(last["halves"] + early["halves"]), lambda outs: last.update(theirs=outs)))
    Gs = {}
    for n, h, o in zip(["w_in"] + late, last["halves"] + early["halves"], last["theirs"]):
        Gs[n] = lax.dynamic_update_slice(jnp.concatenate([o, o], axis=0), h, (c * h.shape[0], 0))

    small_names = list(_SMALL_REPL) + list(_SMALL_SHARD)
    pieces = [_rows_1024(G[n]) for n in small_names] + [_rows_1024(loss)]
    counts = [p.shape[0] for p in pieces]
    pack = jnp.concatenate(pieces, axis=0)
    pad_rows = (-pack.shape[0]) % SUB
    pack = jnp.pad(pack, ((0, pad_rows), (0, 0)))
    adam = {"w_in": adamw(Wl["w_in"], Gs["w_in"], Ml["w_in"], Vl["w_in"], "adamw_w_in", rider=gather_all(pack))}
    total = sum_slots(adam["w_in"][1][0], "small_sum")
    row = 0
    for n, cnt in zip(small_names, counts[:-1]):
        blk = total[row:row + cnt]
        row += cnt
        full_shape = G[n].shape
        if full_shape[1] >= 1024:
            blk = blk.reshape(full_shape)
        else:
            blk = blk[:, :full_shape[1]]
        if n in _SMALL_SHARD:
            cs = full_shape[1] // N_CHIPS
            blk = lax.dynamic_slice_in_dim(blk, chip * cs, cs, axis=1)
        Gs[n] = blk
    loss_out = total[row, 0]

    grads, deltas, new_m, new_v = [], [], [], []
    for n in _ORDER:
        d, m2, v2 = (adam[n] if n in adam else adamw(Wl[n], Gs[n], Ml[n], Vl[n], "adamw_" + n))[0]
        shape = lead[n] + Wl[n].shape
        grads.append(Gs[n].reshape(shape))
        deltas.append(d.reshape(shape))
        new_m.append(m2.reshape(shape))
        new_v.append(v2.reshape(shape))
    return (loss_out, grad_x[None], *grads, *deltas, *new_m, *new_v)
```

```python
import jax
import jax.numpy as jnp
from jax import lax
from jax.experimental import pallas as pl
from jax.experimental.pallas import tpu as pltpu

f32 = jnp.float32
bf16 = jnp.bfloat16

D_MODEL = 1024
N_META = 16
CHUNK = 64
HEADS = 8
DN_DK = 128
DN_DV = 256
DN_CONV = 4
DN_QK = HEADS * DN_DK
DN_V = HEADS * DN_DV
SB_DH = 128
SB_W = HEADS * SB_DH
SB_BLOCK = 128
SB_QB = 384
SB_GROUP = 4
SB_HEADS_PER_STEP = 2
SB_FWD_HEADS_PER_STEP = 4
GDN_HEADS_PER_STEP = 8
CONV_W = 2048
D_FF = 2816
RMS_EPS = 1e-6
L2_EPS = 1e-6
ADAM_LR = 0.001
ADAM_B1 = 0.9
ADAM_B2 = 0.999
ADAM_EPS = 1e-08
ADAM_WD = 0.01
ADAM_STEP = 10

P0 = 112
LANE = 128
SUB = 8
VMEM_LIMIT = 48 * 1024 * 1024
N_CHIPS = 4
N_DEV = 8

C_DQ, C_DK, C_DV, C_DZ, C_SQ, C_SK, C_SV, C_GDN, C_GSB = 0, 1, 2, 4, 6, 7, 8, 9, 10
PROJ_BIG = 11 * 1024
AB_COL = 2 * DN_QK + 2 * DN_V


def _params(n_axes):
    return pltpu.CompilerParams(dimension_semantics=("arbitrary",) * n_axes, vmem_limit_bytes=VMEM_LIMIT)


def _tile(n, target, q=LANE):
    best = None
    for t in range(q, min(n, target) + 1, q):
        if n % t == 0:
            best = t
    return best if best is not None else n


def _dot(a, b):
    return jnp.dot(a.astype(bf16), b.astype(bf16), preferred_element_type=f32)


def _dot_nt(a, b):
    return lax.dot_general(a.astype(bf16), b.astype(bf16), (((1,), (1,)), ((), ())), preferred_element_type=f32)


def _dot_tn(a, b):
    return lax.dot_general(a.astype(bf16), b.astype(bf16), (((0,), (0,)), ((), ())), preferred_element_type=f32)


_HI = lax.Precision.HIGH


def _hdot(a, b):
    return jnp.dot(a, b, precision=_HI, preferred_element_type=f32)


def _hdot_nt(a, b):
    return lax.dot_general(a, b, (((1,), (1,)), ((), ())), precision=_HI, preferred_element_type=f32)


def _hdot_tn(a, b):
    return lax.dot_general(a, b, (((0,), (0,)), ((), ())), precision=_HI, preferred_element_type=f32)


def _sigmoid(x):
    return 0.5 * jnp.tanh(0.5 * x) + 0.5


def _log1p_small(e):
    return jnp.where(e < 1e-3, e * (1.0 - e * (0.5 - e * (1.0 / 3.0))), jnp.log(1.0 + e))


def _rowsum(x):
    return jnp.sum(x, axis=1, keepdims=True)


def _allsum(x):
    return jnp.sum(jnp.sum(x, axis=1, keepdims=True), axis=0, keepdims=True)


def matmul(a, b, mode, name, residual=None, out_dtype=f32, tm_t=1408, tn_t=1024, tk_t=1408, rider=None):
    if mode == "nn":
        (M, K), (K2, N) = a.shape, b.shape
    elif mode == "nt":
        (M, K), (N, K2) = a.shape, b.shape
    else:
        (K, M), (K2, N) = a.shape, b.shape
    assert K == K2, (a.shape, b.shape, mode)
    tm, tn, tk = _tile(M, tm_t), _tile(N, tn_t), _tile(K, tk_t)
    nk = K // tk
    if mode == "nn":
        a_spec = pl.BlockSpec((tm, tk), lambda i, j, k: (i, k))
        b_spec = pl.BlockSpec((tk, tn), lambda i, j, k: (k, j))
        dims = (((1,), (0,)), ((), ()))
    elif mode == "nt":
        a_spec = pl.BlockSpec((tm, tk), lambda i, j, k: (i, k))
        b_spec = pl.BlockSpec((tn, tk), lambda i, j, k: (j, k))
        dims = (((1,), (1,)), ((), ()))
    else:
        a_spec = pl.BlockSpec((tk, tm), lambda i, j, k: (k, i))
        b_spec = pl.BlockSpec((tk, tn), lambda i, j, k: (k, j))
        dims = (((0,), (0,)), ((), ()))
    o_spec = pl.BlockSpec((tm, tn), lambda i, j, k: (i, j))
    has_res = residual is not None
    grid = (M // tm, N // tn, nk)
    split, ride_first, ride_last = _ride(rider, 3 if has_res else 2, 1, grid)

    def body(*refs):
        ins_, (o_ref,), (rin, rout, rest) = split(refs)
        a_ref, b_ref = ins_[:2]
        r_ref = ins_[2] if has_res else None
        acc_ref, ride = rest[0], (rin, rout, rest[1:])
        ride_first(ride)
        k = pl.program_id(2)

        @pl.when(k == 0)
        def _():
            acc_ref[...] = jnp.zeros_like(acc_ref)

        acc_ref[...] += lax.dot_general(a_ref[...].astype(bf16), b_ref[...].astype(bf16), dims,
                                        preferred_element_type=f32)

        @pl.when(k == nk - 1)
        def _():
            r = acc_ref[...]
            if has_res:
                r = r + r_ref[...]
            o_ref[...] = r.astype(out_dtype)

        ride_last(ride)

    ins = [a, b] + ([residual] if has_res else [])
    specs = [a_spec, b_spec] + ([o_spec] if has_res else [])
    r_ins = rider.ins if rider else []
    r_outs = rider.out_shapes if rider else []
    res = pl.pallas_call(
        body, name=name, grid=grid, in_specs=specs + [_ANY] * len(r_ins), out_specs=[o_spec] + [_ANY] * len(r_outs),
        out_shape=[jax.ShapeDtypeStruct((M, N), out_dtype)] + list(r_outs),
        scratch_shapes=[pltpu.VMEM((tm, tn), f32)] + (rider.scratch if rider else []), compiler_params=_params(3),
    )(*ins, *r_ins)
    return (res[0], res[1:]) if rider else res[0]


def _row_tile(tp):
    return _tile(tp, 512)


def rms_fwd(h, gain, name):
    tp, d = h.shape
    rt = _row_tile(tp)

    def body(h_ref, g_ref, o_ref):
        x = h_ref[...]
        r = lax.rsqrt(jnp.mean(x * x, axis=-1, keepdims=True) + RMS_EPS)
        o_ref[...] = (x * r * g_ref[...]).astype(bf16)

    return pl.pallas_call(
        body, name=name, grid=(tp // rt,),
        in_specs=[pl.BlockSpec((rt, d), lambda i: (i, 0)), pl.BlockSpec((1, d), lambda i: (0, 0))],
        out_specs=pl.BlockSpec((rt, d), lambda i: (i, 0)),
        out_shape=jax.ShapeDtypeStruct((tp, d), bf16), compiler_params=_params(1),
    )(h, gain)


def rms_bwd(h, gain, dn, dres, name, rider=None):
    tp, d = h.shape
    rt = _row_tile(tp)
    grid = (tp // rt,)
    split, ride_first, ride_last = _ride(rider, 4, 3, grid)

    def body(*refs):
        (h_ref, g_ref, dn_ref, dr_ref), (dh_ref, dhb_ref, dg_ref), ride = split(refs)
        ride_first(ride)
        i = pl.program_id(0)
        x = h_ref[...]
        r = lax.rsqrt(jnp.mean(x * x, axis=-1, keepdims=True) + RMS_EPS)
        xh = x * r
        dn_ = dn_ref[...]
        dxh = dn_ * g_ref[...]
        dh = r * (dxh - xh * jnp.mean(dxh * xh, axis=-1, keepdims=True)) + dr_ref[...]
        dh_ref[...] = dh
        dhb_ref[...] = dh.astype(bf16)
        part = jnp.sum(dn_ * xh, axis=0, keepdims=True)

        @pl.when(i == 0)
        def _():
            dg_ref[...] = part

        @pl.when(i > 0)
        def _():
            dg_ref[...] += part

        ride_last(ride)

    row = pl.BlockSpec((rt, d), lambda i: (i, 0))
    vec = pl.BlockSpec((1, d), lambda i: (0, 0))
    r_ins = rider.ins if rider else []
    r_outs = rider.out_shapes if rider else []
    res = pl.pallas_call(
        body, name=name, grid=grid, in_specs=[row, vec, row, row] + [_ANY] * len(r_ins),
        out_specs=[row, row, vec] + [_ANY] * len(r_outs),
        out_shape=[jax.ShapeDtypeStruct((tp, d), f32), jax.ShapeDtypeStruct((tp, d), bf16),
                   jax.ShapeDtypeStruct((1, d), f32)] + list(r_outs),
        scratch_shapes=rider.scratch if rider else [], compiler_params=_params(1),
    )(h, gain, dn, dres, *r_ins)
    return res[:3], res[3:]


def loss_head(y, target):
    tp, d = y.shape
    lead = P0 + N_META
    rt = _row_tile(tp)
    ns = rt // lead
    assert lead == SB_BLOCK and rt % lead == 0 and target.shape == (tp - lead, d)
    last = target.shape[0] // lead - 1

    def body(*refs):
        y_ref, t_refs = refs[0], refs[1:1 + ns]
        dy_ref, dyb_ref, l_ref = refs[1 + ns:]
        i = pl.program_id(0)

        @pl.when(i == 0)
        def _():
            l_ref[...] = jnp.zeros_like(l_ref)

        part = jnp.zeros((1, 1), f32)
        for s in range(ns):
            rows = slice(s * lead, (s + 1) * lead)
            err = y_ref[rows, :] - t_refs[s][...]
            if s == 0:
                err = err * (i > 0).astype(f32)
            dy = err * (1.0 / d)
            dy_ref[rows, :] = dy
            dyb_ref[rows, :] = dy.astype(bf16)
            part = part + _allsum(err * err)
        l_ref[...] += jnp.broadcast_to(part * (0.5 / d), l_ref.shape)

    row = pl.BlockSpec((rt, d), lambda i: (i, 0))
    t_specs = [pl.BlockSpec((lead, d), lambda i, s=s: (jnp.clip(ns * i + s - 1, 0, last), 0)) for s in range(ns)]
    return pl.pallas_call(
        body, name="loss_head", grid=(tp // rt,), in_specs=[row] + t_specs,
        out_specs=[row, row, pl.BlockSpec((1, LANE), lambda i: (0, 0))],
        out_shape=[jax.ShapeDtypeStruct((tp, d), f32), jax.ShapeDtypeStruct((tp, d), bf16),
                   jax.ShapeDtypeStruct((1, LANE), f32)],
        compiler_params=_params(1),
    )(y, *([target] * ns))


def swiglu_fwd(u):
    tp = u.shape[0]
    rt, cb = _row_tile(tp), D_FF // 2
    nb = D_FF // cb

    def body(g_ref, u_ref, o_ref):
        g = g_ref[...]
        o_ref[...] = (g * _sigmoid(g) * u_ref[...]).astype(bf16)

    return pl.pallas_call(
        body, name="swiglu_fwd", grid=(tp // rt, nb),
        in_specs=[pl.BlockSpec((rt, cb), lambda i, j: (i, j)), pl.BlockSpec((rt, cb), lambda i, j: (i, j + nb))],
        out_specs=pl.BlockSpec((rt, cb), lambda i, j: (i, j)),
        out_shape=jax.ShapeDtypeStruct((tp, D_FF), bf16), compiler_params=_params(2),
    )(u, u)


def swiglu_bwd(u, dact):
    tp = u.shape[0]
    rt, cb = _row_tile(tp), D_FF // 2
    nb = D_FF // cb

    def body(g_ref, u_ref, da_ref, dg_ref, du_ref):
        g = g_ref[...]
        s = _sigmoid(g)
        da = da_ref[...]
        dg_ref[...] = (da * u_ref[...] * s * (1.0 + g * (1.0 - s))).astype(bf16)
        du_ref[...] = (da * g * s).astype(bf16)

    lo = pl.BlockSpec((rt, cb), lambda i, j: (i, j))
    hi = pl.BlockSpec((rt, cb), lambda i, j: (i, j + nb))
    dgate, dup = pl.pallas_call(
        body, name="swiglu_bwd", grid=(tp // rt, nb), in_specs=[lo, hi, lo], out_specs=[lo, lo],
        out_shape=[jax.ShapeDtypeStruct((tp, D_FF), bf16)] * 2, compiler_params=_params(2),
    )(u, u, dact)
    return dgate, dup


def merge_fwd(proj, ydn, ysb):
    tp = proj.shape[0]
    rt, d = _row_tile(tp), D_MODEL

    def body(gd_ref, gs_ref, yd_ref, ys_ref, o_ref):
        o_ref[...] = (_sigmoid(gd_ref[...]) * yd_ref[...] + _sigmoid(gs_ref[...]) * ys_ref[...]).astype(bf16)

    row = pl.BlockSpec((rt, d), lambda i: (i, 0))
    return pl.pallas_call(
        body, name="merge_fwd", grid=(tp // rt,),
        in_specs=[pl.BlockSpec((rt, d), lambda i: (i, C_GDN)), pl.BlockSpec((rt, d), lambda i: (i, C_GSB)), row, row],
        out_specs=row, out_shape=jax.ShapeDtypeStruct((tp, d), bf16), compiler_params=_params(1),
    )(proj, proj, ydn, ysb)


def merge_bwd(proj, ydn, ysb, dm):
    tp = proj.shape[0]
    rt, d = _row_tile(tp), D_MODEL

    def body(gd_ref, gs_ref, yd_ref, ys_ref, dm_ref, dyd_ref, dys_ref, dg_ref):
        dm_ = dm_ref[...]
        sd = _sigmoid(gd_ref[...])
        ss = _sigmoid(gs_ref[...])
        dyd_ref[...] = (dm_ * sd).astype(bf16)
        dys_ref[...] = (dm_ * ss).astype(bf16)
        dg_ref[:, :d] = (dm_ * yd_ref[...] * sd * (1.0 - sd)).astype(bf16)
        dg_ref[:, d:] = (dm_ * ys_ref[...] * ss * (1.0 - ss)).astype(bf16)

    row = pl.BlockSpec((rt, d), lambda i: (i, 0))
    return pl.pallas_call(
        body, name="merge_bwd", grid=(tp // rt,),
        in_specs=[pl.BlockSpec((rt, d), lambda i: (i, C_GDN)), pl.BlockSpec((rt, d), lambda i: (i, C_GSB)), row, row, row],
        out_specs=[row, row, pl.BlockSpec((rt, 2 * d), lambda i: (i, 0))],
        out_shape=[jax.ShapeDtypeStruct((tp, d), bf16)] * 2 + [jax.ShapeDtypeStruct((tp, 2 * d), bf16)],
        compiler_params=_params(1),
    )(proj, proj, ydn, ysb, dm)


def dn_out_fwd(o, proj, gain):
    tp = o.shape[0]
    rt, cb, wide = _row_tile(tp), DN_DV, 1024
    zb = C_DZ * 1024 // wide

    def body(o_ref, z_ref, g_ref, y_ref):
        for s in range(wide // cb):
            sl = slice(s * cb, (s + 1) * cb)
            x = o_ref[:, sl]
            r = lax.rsqrt(jnp.mean(x * x, axis=-1, keepdims=True) + RMS_EPS)
            z = z_ref[:, sl]
            y_ref[:, sl] = (x * r * g_ref[...] * (z * _sigmoid(z))).astype(bf16)

    blk = pl.BlockSpec((rt, wide), lambda i, j: (i, j))
    return pl.pallas_call(
        body, name="dn_out_fwd", grid=(tp // rt, DN_V // wide),
        in_specs=[blk, pl.BlockSpec((rt, wide), lambda i, j: (i, j + zb)), pl.BlockSpec((1, cb), lambda i, j: (0, 0))],
        out_specs=blk, out_shape=jax.ShapeDtypeStruct((tp, DN_V), bf16), compiler_params=_params(2),
    )(o, proj, gain)


def dn_out_bwd(o, proj, gain, dy):
    tp = o.shape[0]
    rt, cb, wide = _row_tile(tp), DN_DV, 1024
    zb = C_DZ * 1024 // wide

    def body(o_ref, z_ref, g_ref, dy_ref, do_ref, dz_ref, dg_ref):
        i, j = pl.program_id(0), pl.program_id(1)
        g = g_ref[...]
        part = jnp.zeros((1, cb), f32)
        for hh in range(wide // cb):
            sl = slice(hh * cb, (hh + 1) * cb)
            x = o_ref[:, sl]
            r = lax.rsqrt(jnp.mean(x * x, axis=-1, keepdims=True) + RMS_EPS)
            xh = x * r
            z = z_ref[:, sl]
            s = _sigmoid(z)
            dy_ = dy_ref[:, sl]
            drn = dy_ * (z * s)
            dz_ref[:, sl] = (dy_ * xh * g * s * (1.0 + z * (1.0 - s))).astype(bf16)
            dxh = drn * g
            do_ref[:, sl] = r * (dxh - xh * jnp.mean(dxh * xh, axis=-1, keepdims=True))
            part = part + jnp.sum(drn * xh, axis=0, keepdims=True)
        first = jnp.logical_and(i == 0, j == 0)

        @pl.when(first)
        def _():
            dg_ref[...] = part

        @pl.when(jnp.logical_not(first))
        def _():
            dg_ref[...] += part

    blk = pl.BlockSpec((rt, wide), lambda i, j: (i, j))
    vec = pl.BlockSpec((1, cb), lambda i, j: (0, 0))
    return pl.pallas_call(
        body, name="dn_out_bwd", grid=(tp // rt, DN_V // wide),
        in_specs=[blk, pl.BlockSpec((rt, wide), lambda i, j: (i, j + zb)), vec, blk],
        out_specs=[blk, pl.BlockSpec((rt, wide), lambda i, j: (i, j + zb)), vec],
        out_shape=[jax.ShapeDtypeStruct((tp, DN_V), f32), jax.ShapeDtypeStruct((tp, PROJ_BIG), bf16),
                   jax.ShapeDtypeStruct((1, cb), f32)],
        compiler_params=_params(2),
    )(o, proj, gain, dy)


def sb_prep_fwd(proj, gq, gk):
    tp = proj.shape[0]
    rt, cb = _row_tile(tp), SB_DH

    def body(q_ref, k_ref, v_ref, gq_ref, gk_ref, qo_ref, ko_ref, vo_ref):
        for x_ref, g_ref, o_ref in ((q_ref, gq_ref, qo_ref), (k_ref, gk_ref, ko_ref)):
            for h in range(HEADS):
                sl = slice(h * cb, (h + 1) * cb)
                x = x_ref[:, sl]
                r = lax.rsqrt(jnp.mean(x * x, axis=-1, keepdims=True) + RMS_EPS)
                o_ref[:, sl] = (x * r * g_ref[...]).astype(bf16)
        vo_ref[...] = v_ref[...].astype(bf16)

    blk = pl.BlockSpec((rt, SB_W), lambda i: (i, 0))
    vec = pl.BlockSpec((1, cb), lambda i: (0, 0))
    return pl.pallas_call(
        body, name="sb_prep_fwd", grid=(tp // rt,),
        in_specs=[pl.BlockSpec((rt, SB_W), lambda i: (i, C_SQ)), pl.BlockSpec((rt, SB_W), lambda i: (i, C_SK)),
                  pl.BlockSpec((rt, SB_W), lambda i: (i, C_SV)), vec, vec],
        out_specs=[blk] * 3, out_shape=[jax.ShapeDtypeStruct((tp, SB_W), bf16)] * 3, compiler_params=_params(1),
    )(proj, proj, proj, gq, gk)


def sb_prep_bwd(proj, gq, gk, dqs, dks, dvs, into):
    tp = proj.shape[0]
    rt, cb = _row_tile(tp), SB_DH
    assert (C_SQ * 1024) % (3 * SB_W) == 0 and (C_SQ + 1, C_SQ + 2) == (C_SK, C_SV)

    def body(q_ref, k_ref, gq_ref, gk_ref, dq_ref, dk_ref, dv_ref, into_ref, do_ref, dgq_ref, dgk_ref):
        first = pl.program_id(0) == 0
        do_ref[:, 2 * SB_W:] = dv_ref[...].astype(bf16)
        for x_ref, g_ref, dn_ref, at, dg_ref, mul in ((q_ref, gq_ref, dq_ref, 0, dgq_ref, None),
                                                      (k_ref, gk_ref, dk_ref, SB_W, dgk_ref, SB_DH ** -0.5)):
            part = jnp.zeros((1, cb), f32)
            for h in range(HEADS):
                sl = slice(h * cb, (h + 1) * cb)
                x = x_ref[:, sl]
                r = lax.rsqrt(jnp.mean(x * x, axis=-1, keepdims=True) + RMS_EPS)
                xh = x * r
                dn_ = dn_ref[:, sl] if mul is None else dn_ref[:, sl] * mul
                dxh = dn_ * g_ref[...]
                do_ref[:, at + h * cb:at + (h + 1) * cb] = (
                    r * (dxh - xh * jnp.mean(dxh * xh, axis=-1, keepdims=True))).astype(bf16)
                part = part + jnp.sum(dn_ * xh, axis=0, keepdims=True)

            @pl.when(first)
            def _(dg_ref=dg_ref, part=part):
                dg_ref[...] = part

            @pl.when(jnp.logical_not(first))
            def _(dg_ref=dg_ref, part=part):
                dg_ref[...] += part

    blk = pl.BlockSpec((rt, SB_W), lambda i: (i, 0))
    vec = pl.BlockSpec((1, cb), lambda i: (0, 0))
    return pl.pallas_call(
        body, name="sb_prep_bwd", grid=(tp // rt,),
        in_specs=[pl.BlockSpec((rt, SB_W), lambda i: (i, C_SQ)), pl.BlockSpec((rt, SB_W), lambda i: (i, C_SK)),
                  vec, vec, blk, blk, blk, pl.BlockSpec(memory_space=pl.ANY)],
        out_specs=[pl.BlockSpec((rt, 3 * SB_W), lambda i: (i, C_SQ * 1024 // (3 * SB_W))), vec, vec],
        out_shape=[jax.ShapeDtypeStruct(into.shape, into.dtype)] + [jax.ShapeDtypeStruct((1, cb), f32)] * 2,
        input_output_aliases={7: 0}, compiler_params=_params(1),
    )(proj, proj, gq, gk, dqs, dks, dvs, into)


def _conv_taps(ext, rt):
    taps = []
    for k in range(DN_CONV):
        s = DN_CONV - 1 - k
        taps.append((pltpu.roll(ext, s, axis=0) if s else ext)[SUB:SUB + rt])
    return taps


def _conv_act(taps, w, l2):
    y = taps[0] * w[0:1]
    for k in range(1, DN_CONV):
        y = y + taps[k] * w[k:k + 1]
    s = _sigmoid(y)
    a = y * s
    if l2:
        n = lax.rsqrt(jnp.sum(a * a, axis=-1, keepdims=True) + L2_EPS)
        return y, s, a, n
    return y, s, a, None


def conv_fwd(proj, w8, col_blk, ncb, l2, name):
    tp = proj.shape[0]
    rt = _row_tile(tp)
    hb = rt // SUB
    cw = min(CONV_W, ncb * LANE)
    cb0 = col_blk * LANE // cw

    def body(x_ref, h_ref, w_ref, o_ref):
        i = pl.program_id(1)
        first = (i > 0).astype(f32)
        for s in range(cw // LANE):
            sl = slice(s * LANE, (s + 1) * LANE)
            ext = jnp.concatenate([h_ref[:, sl] * first, x_ref[:, sl]], axis=0)
            _, _, a, n = _conv_act(_conv_taps(ext, rt), w_ref[:, sl], l2)
            o_ref[:, sl] = a * n if l2 else a

    return pl.pallas_call(
        body, name=name, grid=(ncb * LANE // cw, tp // rt),
        in_specs=[pl.BlockSpec((rt, cw), lambda j, i: (i, j + cb0)),
                  pl.BlockSpec((SUB, cw), lambda j, i: (jnp.maximum(i * hb - 1, 0), j + cb0)),
                  pl.BlockSpec((SUB, cw), lambda j, i: (0, j))],
        out_specs=pl.BlockSpec((rt, cw), lambda j, i: (i, j)),
        out_shape=jax.ShapeDtypeStruct((tp, ncb * LANE), f32), compiler_params=_params(2),
    )(proj, proj, w8)


def conv_bwd(proj, w8, dout, into, col_blk, ncb, l2, name):
    tp = proj.shape[0]
    rt = _row_tile(tp)
    hb = rt // SUB
    nr = tp // rt
    last8 = tp // SUB - 1
    cw = min(CONV_W, ncb * LANE)
    cb0 = col_blk * LANE // cw
    n = rt + SUB

    def body(x_ref, xb_ref, xf_ref, w_ref, d_ref, df_ref, into_ref, o_ref, dw_ref):
        i = pl.program_id(1)
        first = (i > 0).astype(f32)
        last = (i < nr - 1).astype(f32)
        rows = lax.broadcasted_iota(jnp.int32, (SUB, LANE), 0)
        for s in range(cw // LANE):
            sl = slice(s * LANE, (s + 1) * LANE)
            ext = jnp.concatenate([xb_ref[:, sl] * first, x_ref[:, sl], xf_ref[:, sl] * last], axis=0)
            taps = _conv_taps(ext, n)
            w = w_ref[:, sl]
            y, sg, a, nrm = _conv_act(taps, w, l2)
            da = jnp.concatenate([d_ref[:, sl], df_ref[:, sl] * last], axis=0)
            if l2:
                out = a * nrm
                da = nrm * (da - out * jnp.sum(da * out, axis=-1, keepdims=True))
            dy = da * sg * (1.0 + y * (1.0 - sg))
            part = jnp.zeros((SUB, LANE), f32)
            for k in range(DN_CONV):
                part = part + jnp.where(rows == k, jnp.sum(taps[k][0:rt] * dy[0:rt], axis=0, keepdims=True), 0.0)

            @pl.when(i == 0)
            def _(sl=sl, part=part):
                dw_ref[:, sl] = part

            @pl.when(i > 0)
            def _(sl=sl, part=part):
                dw_ref[:, sl] += part

            acc = None
            for k in range(DN_CONV):
                up = DN_CONV - 1 - k
                term = (pltpu.roll(dy, n - up, axis=0) if up else dy)[0:rt] * w[k:k + 1]
                acc = term if acc is None else acc + term
            o_ref[:, sl] = acc.astype(bf16)

    after = lambda j, i: (jnp.minimum((i + 1) * hb, last8), j)
    return pl.pallas_call(
        body, name=name, grid=(ncb * LANE // cw, nr),
        in_specs=[pl.BlockSpec((rt, cw), lambda j, i: (i, j + cb0)),
                  pl.BlockSpec((SUB, cw), lambda j, i: (jnp.maximum(i * hb - 1, 0), j + cb0)),
                  pl.BlockSpec((SUB, cw), lambda j, i: (jnp.minimum((i + 1) * hb, last8), j + cb0)),
                  pl.BlockSpec((SUB, cw), lambda j, i: (0, j)),
                  pl.BlockSpec((rt, cw), lambda j, i: (i, j)), pl.BlockSpec((SUB, cw), after),
                  pl.BlockSpec(memory_space=pl.ANY)],
        out_specs=[pl.BlockSpec((rt, cw), lambda j, i: (i, j + cb0)), pl.BlockSpec((SUB, cw), lambda j, i: (0, j))],
        out_shape=[jax.ShapeDtypeStruct(into.shape, into.dtype), jax.ShapeDtypeStruct((SUB, ncb * LANE), f32)],
        input_output_aliases={6: 0}, compiler_params=_params(2),
    )(proj, proj, proj, w8, dout, dout, into)


def _ab_common(p, al, dtb, r0):
    rows = r0 + lax.broadcasted_iota(jnp.int32, p.shape, 0)
    mask = (rows >= P0).astype(f32)
    xx = p + dtb
    sp = jnp.maximum(xx, 0.0) + _log1p_small(jnp.exp(-jnp.abs(xx)))
    ea = jnp.exp(al)
    g = -ea * sp * mask
    beta = _sigmoid(p) * mask
    return g, beta, _sigmoid(xx), ea, mask


def _chunk_tri(rt, later):
    r = lax.broadcasted_iota(jnp.int32, (rt, rt), 0)
    c = lax.broadcasted_iota(jnp.int32, (rt, rt), 1)
    shift = CHUNK.bit_length() - 1
    same = jnp.right_shift(r, shift) == jnp.right_shift(c, shift)
    return jnp.logical_and(same, c >= r if later else c <= r).astype(f32)


def ab_fwd(pab, al, dtb):
    tp = pab.shape[0]
    rt = _row_tile(tp)
    assert rt % CHUNK == 0

    def body(p_ref, al_ref, dt_ref, g_ref, b_ref):
        i = pl.program_id(0)
        g, beta, _, _, _ = _ab_common(p_ref[...], al_ref[...], dt_ref[...], i * rt)
        gam = _hdot(_chunk_tri(rt, False), g)
        for h in range(HEADS):
            g_ref[h] = jnp.broadcast_to(gam[:, h:h + 1], (rt, LANE))
            b_ref[h] = jnp.broadcast_to(beta[:, HEADS + h:HEADS + h + 1], (rt, LANE))

    vec = pl.BlockSpec((1, LANE), lambda i: (0, 0))
    out = pl.BlockSpec((HEADS, rt, LANE), lambda i: (0, i, 0))
    return pl.pallas_call(
        body, name="ab_fwd", grid=(tp // rt,), in_specs=[pl.BlockSpec((rt, LANE), lambda i: (i, 0)), vec, vec],
        out_specs=[out, out], out_shape=[jax.ShapeDtypeStruct((HEADS, tp, LANE), f32)] * 2, compiler_params=_params(1),
    )(pab, al, dtb)


def ab_bwd(pab, al, dtb, dg, db):
    tp = pab.shape[0]
    rt = _row_tile(tp)

    def body(p_ref, al_ref, dt_ref, dg_ref, db_ref, dp_ref, dal_ref, ddt_ref):
        i = pl.program_id(0)
        g, beta, sx, ea, mask = _ab_common(p_ref[...], al_ref[...], dt_ref[...], i * rt)
        lanes = lax.broadcasted_iota(jnp.int32, (rt, LANE), 1)
        dgl = jnp.zeros((rt, LANE), f32)
        dbl = jnp.zeros((rt, LANE), f32)
        for h in range(HEADS):
            dgl = dgl + jnp.where(lanes == h, dg_ref[h], 0.0)
            dbl = dbl + jnp.where(lanes == HEADS + h, db_ref[h], 0.0)
        dgl = _hdot(_chunk_tri(rt, True), dgl)
        dxx = dgl * (-ea) * sx * mask
        dp_ref[...] = (dxx + dbl * beta * (1.0 - beta)).astype(bf16)
        pal = jnp.sum(dgl * g, axis=0, keepdims=True)
        pdt = jnp.sum(dxx, axis=0, keepdims=True)

        @pl.when(i == 0)
        def _():
            dal_ref[...] = pal
            ddt_ref[...] = pdt

        @pl.when(i > 0)
        def _():
            dal_ref[...] += pal
            ddt_ref[...] += pdt

    vec = pl.BlockSpec((1, LANE), lambda i: (0, 0))
    row = pl.BlockSpec((rt, LANE), lambda i: (i, 0))
    big = pl.BlockSpec((HEADS, rt, LANE), lambda i: (0, i, 0))
    return pl.pallas_call(
        body, name="ab_bwd", grid=(tp // rt,), in_specs=[row, vec, vec, big, big], out_specs=[row, vec, vec],
        out_shape=[jax.ShapeDtypeStruct((tp, LANE), bf16), jax.ShapeDtypeStruct((1, LANE), f32),
                   jax.ShapeDtypeStruct((1, LANE), f32)],
        compiler_params=_params(1),
    )(pab, al, dtb, dg, db)


class _Chunk:
    pass


def _gdn_chunk(q, k, v, gcol, bcol, grow8, saved=None):
    C = CHUNK
    R = range(len(q))
    X = _Chunk()
    ri = lax.broadcasted_iota(jnp.int32, (C, C), 0)
    ci = lax.broadcasted_iota(jnp.int32, (C, C), 1)
    eye = (ri == ci).astype(f32)
    gam = list(gcol)
    gam_row = [grow8[h][0:1, 0:C] for h in R]
    X.ri, X.ci = ri, ci
    X.Dm = [jnp.where(ri >= ci, jnp.exp(jnp.minimum(gam[h][:, 0:C] - gam_row[h], 0.0)), 0.0) for h in R]
    X.eg = [jnp.exp(gam[h]) for h in R]
    gl = [gam[h][C - 1:C, :] for h in R]
    X.egl = [jnp.exp(gl[h]) for h in R]
    X.kdec = [jnp.exp(gl[h] - gam[h]) for h in R]
    X.qs = [q[h] * (DN_DK ** -0.5) for h in R]
    X.kb = [k[h] * bcol[h] for h in R]
    if saved is None:
        kk = [_dot_nt(X.kb[h], k[h]) for h in R]
        qk = [_dot_nt(X.qs[h], k[h]) for h in R]
        X.A = [jnp.where(ri > ci, kk[h] * X.Dm[h], 0.0) for h in R]
        X.attn = [qk[h] * X.Dm[h] for h in R]
    assert C == 64
    X.b2 = [jnp.concatenate([bcol[h], bcol[h]], axis=-1) for h in R]
    if saved is not None:
        X.T, X.u, X.w, X.A, X.attn = saved
    else:
        b16 = jnp.right_shift(ri, 4) == jnp.right_shift(ci, 4)
        b32 = jnp.right_shift(ri, 5) == jnp.right_shift(ci, 5)
        P = [jnp.where(b16, X.A[h], 0.0) for h in R]
        T = [eye - P[h] for h in R]
        for _ in range(3):
            P = [_hdot(P[h], P[h]) for h in R]
            T = [T[h] + _hdot(T[h], P[h]) for h in R]
        for off in (jnp.logical_and(b32, jnp.logical_not(b16)), jnp.logical_not(b32)):
            AT = [_hdot(jnp.where(off, X.A[h], 0.0), T[h]) for h in R]
            T = [T[h] - _hdot(T[h], AT[h]) for h in R]
        X.T = T
        X.u = [_hdot(T[h], v[h] * X.b2[h]) for h in R]
        X.w = [_hdot(T[h], X.kb[h] * X.eg[h]) for h in R]
    X.qg =[X.qs[h] * X.eg[h] for h in R]
    X.kg = [k[h] * X.kdec[h] for h in R]
    return X


def gdn_fwd(q, k, v, gc, bc, grow):
    tp = q.shape[0]
    nc = tp // CHUNK
    hb = GDN_HEADS_PER_STEP

    def body(q_ref, k_ref, v_ref, gc_ref, bc_ref, gr_ref, o_ref, ss_ref, t_ref, u_ref, w_ref, a_ref, at_ref, vn_ref,
             S_ref):
        c = pl.program_id(1)

        @pl.when(c == 0)
        def _():
            S_ref[...] = jnp.zeros_like(S_ref)

        R = range(hb)
        qc = [slice(h * DN_DK, (h + 1) * DN_DK) for h in R]
        vc = [slice(h * DN_DV, (h + 1) * DN_DV) for h in R]
        X = _gdn_chunk([q_ref[:, qc[h]] for h in R], [k_ref[:, qc[h]] for h in R], [v_ref[:, vc[h]] for h in R],
                       [gc_ref[h] for h in R], [bc_ref[h] for h in R], [gr_ref[h] for h in R])
        S = [S_ref[h] for h in R]
        for h in R:
            ss_ref[h, 0] = S[h]
            t_ref[h, 0] = X.T[h]
            u_ref[h, 0] = X.u[h]
            w_ref[h, 0] = X.w[h]
            a_ref[h, 0] = X.A[h]
            at_ref[h, 0] = X.attn[h]
        wS = [_dot(X.w[h], S[h]) for h in R]
        qS = [_dot(X.qg[h], S[h]) for h in R]
        vn = [X.u[h] - wS[h] for h in R]
        for h in R:
            vn_ref[h, 0] = vn[h]
        av = [_dot(X.attn[h], vn[h]) for h in R]
        kv = [_dot_tn(X.kg[h], vn[h]) for h in R]
        for h in R:
            o_ref[:, vc[h]] = qS[h] + av[h]
            S_ref[h] = S[h] * X.egl[h][:, 0:1] + kv[h]

    qk = pl.BlockSpec((CHUNK, hb * DN_DK), lambda g, c: (c, g))
    vv = pl.BlockSpec((CHUNK, hb * DN_DV), lambda g, c: (c, g))
    col = pl.BlockSpec((hb, CHUNK, LANE), lambda g, c: (g, c, 0))
    row = pl.BlockSpec((hb, SUB, LANE), lambda g, c: (g, c, 0))
    per = lambda r, w: pl.BlockSpec((hb, 1, r, w), lambda g, c: (g, c, 0, 0))
    keep = lambda r, w: jax.ShapeDtypeStruct((HEADS, nc, r, w), f32)
    return pl.pallas_call(
        body, name="gdn_fwd", grid=(HEADS // hb, nc), in_specs=[qk, qk, vv, col, col, row],
        out_specs=[vv, per(DN_DK, DN_DV), per(CHUNK, CHUNK), per(CHUNK, DN_DV), per(CHUNK, DN_DK), per(CHUNK, CHUNK),
                   per(CHUNK, CHUNK), per(CHUNK, DN_DV)],
        out_shape=[jax.ShapeDtypeStruct((tp, DN_V), f32), keep(DN_DK, DN_DV), keep(CHUNK, CHUNK), keep(CHUNK, DN_DV),
                   keep(CHUNK, DN_DK), keep(CHUNK, CHUNK), keep(CHUNK, CHUNK), keep(CHUNK, DN_DV)],
        scratch_shapes=[pltpu.VMEM((hb, DN_DK, DN_DV), f32)], compiler_params=_params(2),
    )(q, k, v, gc, bc, grow)


def gdn_bwd(q, k, v, gc, bc, grow, states, kept, do, rider=None):
    tp = q.shape[0]
    nc = tp // CHUNK
    C = CHUNK
    hb = GDN_HEADS_PER_STEP
    grid = (HEADS // hb, nc)
    split, ride_first, ride_last = _ride(rider, 14, 5, grid)

    def body(*refs):
        ((q_ref, k_ref, v_ref, gc_ref, bc_ref, gr_ref, ss_ref, t_ref, u_ref, w_ref, a_ref, at_ref, vn_ref, do_ref),
         (dq_ref, dk_ref, dv_ref, dg_ref, db_ref), (rin, rout, rest)) = split(refs)
        dS_ref, ride = rest[0], (rin, rout, rest[1:])
        ride_first(ride)
        c = pl.program_id(1)

        @pl.when(c == 0)
        def _():
            dS_ref[...] = jnp.zeros_like(dS_ref)

        R = range(hb)
        qc = [slice(h * DN_DK, (h + 1) * DN_DK) for h in R]
        vc = [slice(h * DN_DV, (h + 1) * DN_DV) for h in R]
        k_ = [k_ref[:, qc[h]] for h in R]
        v_ = [v_ref[:, vc[h]] for h in R]
        bcol = [bc_ref[h] for h in R]
        X = _gdn_chunk([q_ref[:, qc[h]] for h in R], k_, v_, [gc_ref[h] for h in R], bcol, [gr_ref[h] for h in R],
                       saved=([t_ref[h, 0] for h in R], [u_ref[h, 0] for h in R], [w_ref[h, 0] for h in R],
                              [a_ref[h, 0] for h in R], [at_ref[h, 0] for h in R]))
        ri, ci = X.ri, X.ci
        S = [ss_ref[h, 0] for h in R]
        do_ = [do_ref[:, vc[h]] for h in R]
        dSn = [dS_ref[h] for h in R]
        ado = [_dot_tn(X.attn[h], do_[h]) for h in R]
        kdS = [_dot(X.kg[h], dSn[h]) for h in R]
        d_qg = [_dot_nt(do_[h], S[h]) for h in R]
        qdo = [_dot_tn(X.qg[h], do_[h]) for h in R]
        vn = [vn_ref[h, 0] for h in R]
        d_vn = [ado[h] + kdS[h] for h in R]
        dovn = [_dot_nt(do_[h], vn[h]) for h in R]
        d_kg = [_dot_nt(vn[h], dSn[h]) for h in R]
        wdv = [_dot_tn(X.w[h], d_vn[h]) for h in R]
        dw = [-_dot_nt(d_vn[h], S[h]) for h in R]
        for h in R:
            dS_ref[h] = qdo[h] + X.egl[h][:, 0:1] * dSn[h] - wdv[h]
        dattn = [jnp.where(ri >= ci, dovn[h], 0.0) for h in R]
        dRu = [_hdot_tn(X.T[h], d_vn[h]) for h in R]
        dRw = [_hdot_tn(X.T[h], dw[h]) for h in R]
        dAu = [_hdot_nt(dRu[h], X.u[h]) for h in R]
        dAw = [_hdot_nt(dRw[h], X.w[h]) for h in R]
        dA = [jnp.where(ri > ci, -(dAu[h] + dAw[h]), 0.0) for h in R]
        dKK = [dA[h] * X.Dm[h] for h in R]
        dQK = [dattn[h] * X.Dm[h] for h in R]
        E = [dA[h] * X.A[h] + dattn[h] * X.attn[h] for h in R]
        dkb = [_dot(dKK[h], k_[h]) + dRw[h] * X.eg[h] for h in R]
        dk1 = [_dot_tn(dKK[h], X.kb[h]) for h in R]
        dqs = [_dot(dQK[h], k_[h]) + d_qg[h] * X.eg[h] for h in R]
        dk2 = [_dot_tn(dQK[h], X.qs[h]) for h in R]
        ones = jnp.ones((C, LANE), f32)
        colE = [_hdot_tn(E[h], ones) for h in R]
        rows = lax.broadcasted_iota(jnp.int32, (C, LANE), 0)
        dgam = []
        for h in R:
            t = d_kg[h] * X.kg[h]
            dgl = _allsum(t) + X.egl[h][:, 0:1] * _allsum(S[h] * dSn[h])
            g = (_rowsum(E[h]) - colE[h] + _rowsum(dRw[h] * (X.kb[h] * X.eg[h])) + _rowsum(d_qg[h] * X.qg[h])
                 - _rowsum(t))
            dgam.append(g + jnp.where(rows == C - 1, dgl, 0.0))
        for h in R:
            dv_ref[:, vc[h]] = dRu[h] * X.b2[h]
            dbeta = _rowsum(dRu[h] * v_[h]) + _rowsum(dkb[h] * k_[h])
            dq_ref[:, qc[h]] = dqs[h] * (DN_DK ** -0.5)
            dk_ref[:, qc[h]] = dk1[h] + dk2[h] + dkb[h] * bcol[h] + d_kg[h] * X.kdec[h]
            dg_ref[h] = dgam[h]
            db_ref[h] = jnp.broadcast_to(dbeta, (C, LANE))
        ride_last(ride)

    rc = lambda c: nc - 1 - c
    qk = pl.BlockSpec((CHUNK, hb * DN_DK), lambda g, c: (rc(c), g))
    vv = pl.BlockSpec((CHUNK, hb * DN_DV), lambda g, c: (rc(c), g))
    col = pl.BlockSpec((hb, CHUNK, LANE), lambda g, c: (g, rc(c), 0))
    row = pl.BlockSpec((hb, SUB, LANE), lambda g, c: (g, rc(c), 0))
    st = pl.BlockSpec((hb, 1, DN_DK, DN_DV), lambda g, c: (g, rc(c), 0, 0))
    r_ins = rider.ins if rider else []
    r_outs = rider.out_shapes if rider else []
    per = lambda r, w: pl.BlockSpec((hb, 1, r, w), lambda g, c: (g, rc(c), 0, 0))
    res = pl.pallas_call(
        body, name="gdn_bwd", grid=grid,
        in_specs=[qk, qk, vv, col, col, row, st, per(C, C), per(C, DN_DV), per(C, DN_DK), per(C, C), per(C, C),
                  per(C, DN_DV), vv] + [_ANY] * len(r_ins),
        out_specs=[qk, qk, vv, col, col] + [_ANY] * len(r_outs),
        out_shape=[jax.ShapeDtypeStruct((tp, DN_QK), f32), jax.ShapeDtypeStruct((tp, DN_QK), f32),
                   jax.ShapeDtypeStruct((tp, DN_V), f32), jax.ShapeDtypeStruct((HEADS, tp, LANE), f32),
                   jax.ShapeDtypeStruct((HEADS, tp, LANE), f32)] + list(r_outs),
        scratch_shapes=[pltpu.VMEM((hb, DN_DK, DN_DV), f32)] + (rider.scratch if rider else []),
        compiler_params=_params(2),
    )(q, k, v, gc, bc, grow, states, *kept, do, *r_ins)
    return res[:5], res[5:]


def _cumsum_after(x, nb, us, pieces=2):
    B, n = SB_BLOCK, x.shape[0]
    hi = x.astype(bf16)
    parts = (hi, (x - hi.astype(f32)).astype(bf16)) if pieces == 2 else (hi,)
    rows = [p[:, b * B:(b + 1) * B] for p in parts for b in range(nb)]
    r = jnp.dot(jnp.concatenate(rows, axis=0), us, preferred_element_type=f32)
    out = [r[b * n:(b + 1) * n] for b in range(nb)]
    if pieces == 2:
        out = [out[b] + r[(nb + b) * n:(nb + b + 1) * n] for b in range(nb)]
    return out[0] if nb == 1 else jnp.concatenate(out, axis=1)


def _later_blocks(x, nb, carry):
    B = SB_BLOCK
    tot = [_rowsum(x[:, b * B:(b + 1) * B]) for b in range(nb)]
    offs = [None] * nb
    run = carry
    for b in range(nb - 1, -1, -1):
        offs[b] = jnp.broadcast_to(run, (x.shape[0], B))
        run = run + tot[b]
    return (offs[0] if nb == 1 else jnp.concatenate(offs, axis=1)), run


def _sb_group(i, t):
    top = (i + 1) * (SB_QB // SB_BLOCK) - 1 - SB_GROUP * t
    jlo = jnp.maximum(top - SB_GROUP + 1, 0)
    rows = pl.ds(pl.multiple_of(jlo * SB_BLOCK, SB_BLOCK), SB_GROUP * SB_BLOCK)
    return jlo, rows, (top + 1) * SB_BLOCK


def _sb_weights(q, kcat, i, jlo, kend, cs, us, masked):
    B, nb = SB_BLOCK, SB_GROUP
    R = range(len(q))
    z = [_dot_nt(q[h], kcat[h]) * (SB_DH ** -0.5) for h in R]
    e = [jnp.exp(-jnp.abs(z[h])) for h in R]
    l1p = [jnp.log(1.0 + e[h]) for h in R]
    lsp = [jnp.minimum(z[h], 0.0) - l1p[h] for h in R]
    lk = [lsp[h] - z[h] for h in R]
    vis = None
    if masked:
        qpos = i * SB_QB + lax.broadcasted_iota(jnp.int32, (SB_QB, nb * B), 0)
        kpos = jlo * B + lax.broadcasted_iota(jnp.int32, (SB_QB, nb * B), 1)
        vis = jnp.logical_and(kpos < jnp.minimum(qpos, kend), kpos >= P0)
        lk = [jnp.where(vis, lk[h], 0.0) for h in R]
    later = [_later_blocks(lk[h], nb, cs[h]) for h in R]
    cum = [_cumsum_after(lk[h], nb, us) for h in R]
    w = [jnp.exp(lsp[h] + cum[h] + later[h][0]) for h in R]
    if masked:
        w = [jnp.where(vis, w[h], 0.0) for h in R]
    return lsp, vis, w, [later[h][1] for h in R]


def _sb_loop(i, step, carry):
    trips = ((i + 1) * (SB_QB // SB_BLOCK) - 1 + SB_GROUP) // SB_GROUP
    carry = step(True)(0, carry)
    carry = lax.fori_loop(1, trips - 1, step(False), carry)
    return lax.fori_loop(jnp.maximum(trips - 1, 1), trips, step(True), carry)


def _ride(rider, n_in, n_out, grid):
    n_rin = len(rider.ins) if rider else 0
    n_rout = len(rider.out_shapes) if rider else 0

    def split(refs):
        ins, rin = refs[:n_in], refs[n_in:n_in + n_rin]
        outs = refs[n_in + n_rin:n_in + n_rin + n_out]
        rout = refs[n_in + n_rin + n_out:n_in + n_rin + n_out + n_rout]
        return ins, outs, (rin, rout, refs[n_in + n_rin + n_out + n_rout:])

    def at(step, fn, r):
        if rider is None:
            return
        cond = None
        for a, g in enumerate(grid):
            c = pl.program_id(a) == (g - 1 if step == "last" else 0)
            cond = c if cond is None else jnp.logical_and(cond, c)

        @pl.when(cond)
        def _():
            fn(*r)

    first = lambda r: at("first", rider.start if rider else None, r)
    last = lambda r: at("last", rider.finish if rider else None, r)
    return split, first, last


def sb_fwd(qs, ks, vs, rider=None):
    tp = qs.shape[0]
    nq = tp // SB_QB
    B, G, hb, QB = SB_BLOCK, SB_GROUP, SB_FWD_HEADS_PER_STEP, SB_QB
    assert tp >= G * B and tp % QB == 0 and QB % B == 0 and G * B >= QB
    grid = (HEADS // hb, nq)
    split, ride_first, ride_last = _ride(rider, 3, 2, grid)

    def body(*refs):
        (q_ref, k_ref, v_ref), (o_ref, ob_ref), ride = split(refs)
        ride_first(ride)
        i = pl.program_id(1)
        R = range(hb)
        hs = [slice(h * SB_DH, (h + 1) * SB_DH) for h in R]
        q = [q_ref[:, hs[h]] for h in R]
        us = (lax.broadcasted_iota(jnp.int32, (B, B), 0) > lax.broadcasted_iota(jnp.int32, (B, B), 1)).astype(bf16)

        def make_step(masked):
            def step(t, carry):
                acc, cs = carry
                jlo, rows, kend = _sb_group(i, t)
                _, _, w, cs = _sb_weights(q, [k_ref[rows, hs[h]] for h in R], i, jlo, kend, cs, us, masked)
                pv = [_dot(w[h], v_ref[rows, hs[h]]) for h in R]
                return tuple(acc[h] + pv[h] for h in R), tuple(cs)
            return step

        carry = (tuple(jnp.zeros((QB, SB_DH), f32) for _ in R), tuple(jnp.zeros((QB, 1), f32) for _ in R))
        acc, _ = _sb_loop(i, make_step, carry)
        for h in R:
            o_ref[:, hs[h]] = acc[h]
            ob_ref[:, hs[h]] = acc[h].astype(bf16)
        ride_last(ride)

    blk = pl.BlockSpec((QB, hb * SB_DH), lambda g, i: (i, g))
    full = pl.BlockSpec((tp, hb * SB_DH), lambda g, i: (0, g))
    r_ins = rider.ins if rider else []
    r_outs = rider.out_shapes if rider else []
    res = pl.pallas_call(
        body, name="sb_fwd", grid=grid, in_specs=[blk, full, full] + [_ANY] * len(r_ins),
        out_specs=[blk, blk] + [_ANY] * len(r_outs),
        out_shape=[jax.ShapeDtypeStruct((tp, SB_W), f32), jax.ShapeDtypeStruct((tp, SB_W), bf16)] + list(r_outs),
        scratch_shapes=rider.scratch if rider else [], compiler_params=_params(2),
    )(qs, ks, vs, *r_ins)
    return res[0], res[1], res[2:]


def sb_bwd(qs, ks, vs, o, do, rider=None):
    tp = qs.shape[0]
    nq = tp // SB_QB
    B, G, hb, QB = SB_BLOCK, SB_GROUP, SB_HEADS_PER_STEP, SB_QB
    assert tp >= G * B and tp % QB == 0 and QB % B == 0 and G * B >= QB
    grid = (HEADS // hb, nq)
    split, ride_first, ride_last = _ride(rider, 5, 3, grid)

    def body(*refs):
        (q_ref, k_ref, v_ref, o_ref, do_ref), (dq_ref, dk_ref, dv_ref), ride = split(refs)
        ride_first(ride)
        i = pl.program_id(1)

        @pl.when(i == 0)
        def _():
            dk_ref[...] = jnp.zeros_like(dk_ref)
            dv_ref[...] = jnp.zeros_like(dv_ref)

        R = range(hb)
        hs = [slice(h * SB_DH, (h + 1) * SB_DH) for h in R]
        q = [q_ref[:, hs[h]] for h in R]
        dob = [do_ref[:, hs[h]].astype(bf16) for h in R]
        et = [_rowsum(dob[h].astype(f32) * o_ref[:, hs[h]]) for h in R]
        us = (lax.broadcasted_iota(jnp.int32, (B, B), 0) > lax.broadcasted_iota(jnp.int32, (B, B), 1)).astype(bf16)

        def make_step(masked):
            def step(t, carry):
                dq, cs, ce = carry
                jlo, rows, kend = _sb_group(i, t)
                kcat = [k_ref[rows, hs[h]] for h in R]
                dwv = [_dot_nt(dob[h], v_ref[rows, hs[h]]) for h in R]
                lsp, vis, w, cs = _sb_weights(q, kcat, i, jlo, kend, cs, us, masked)
                wb = [w[h].astype(bf16) for h in R]
                ee = [dwv[h] * wb[h].astype(f32) for h in R]
                later = [_later_blocks(ee[h], G, ce[h]) for h in R]
                cum = [_cumsum_after(ee[h], G, us) for h in R]
                dz = []
                for h in R:
                    d = ee[h] - jnp.exp(lsp[h]) * (et[h] - (cum[h] + later[h][0]))
                    if masked:
                        d = jnp.where(vis, d, 0.0)
                    dz.append(d.astype(bf16))
                dkj = [_dot_tn(dz[h], q[h]) for h in R]
                dvj = [_dot_tn(wb[h], dob[h]) for h in R]
                dqj = [_dot(dz[h], kcat[h]) for h in R]
                for h in R:
                    dk_ref[rows, hs[h]] += dkj[h]
                    dv_ref[rows, hs[h]] += dvj[h]
                return tuple(dq[h] + dqj[h] for h in R), tuple(cs), tuple(later[h][1] for h in R)
            return step

        z0 = tuple(jnp.zeros((QB, 1), f32) for _ in R)
        dq, _, _ = _sb_loop(i, make_step, (tuple(jnp.zeros((QB, SB_DH), f32) for _ in R), z0, z0))
        for h in R:
            dq_ref[:, hs[h]] = dq[h] * (SB_DH ** -0.5)
        ride_last(ride)

    blk = pl.BlockSpec((QB, hb * SB_DH), lambda g, i: (i, g))
    full = pl.BlockSpec((tp, hb * SB_DH), lambda g, i: (0, g))
    r_ins = rider.ins if rider else []
    r_outs = rider.out_shapes if rider else []
    res = pl.pallas_call(
        body, name="sb_bwd", grid=grid, in_specs=[blk, full, full, blk, blk] + [_ANY] * len(r_ins),
        out_specs=[blk, full, full] + [_ANY] * len(r_outs),
        out_shape=[jax.ShapeDtypeStruct((tp, SB_W), f32)] * 3 + list(r_outs),
        scratch_shapes=rider.scratch if rider else [], compiler_params=_params(2),
    )(qs, ks, vs, o, do, *r_ins)
    return res[:3], res[3:]


def adamw(w, g, m, v, name, rider=None):
    r, c = w.shape
    rt = _tile(r, 128, SUB) if r % SUB == 0 else r
    blk = pl.BlockSpec((rt, c), lambda i: (i, 0))
    c1 = 1.0 - ADAM_B1 ** ADAM_STEP
    c2 = 1.0 - ADAM_B2 ** ADAM_STEP
    grid = (r // rt,)
    split, ride_first, ride_last = _ride(rider, 4, 3, grid)

    def body(*refs):
        (w_ref, g_ref, m_ref, v_ref), (d_ref, mo_ref, vo_ref), ride = split(refs)
        ride_first(ride)
        g_ = g_ref[...]
        m_ = ADAM_B1 * m_ref[...] + (1.0 - ADAM_B1) * g_
        v_ = ADAM_B2 * v_ref[...] + (1.0 - ADAM_B2) * (g_ * g_)
        mo_ref[...] = m_
        vo_ref[...] = v_
        d_ref[...] = -ADAM_LR * ((m_ / c1) / (jnp.sqrt(v_ / c2) + ADAM_EPS) + ADAM_WD * w_ref[...])
        ride_last(ride)

    r_ins = rider.ins if rider else []
    r_outs = rider.out_shapes if rider else []
    res = pl.pallas_call(
        body, name=name, grid=grid, in_specs=[blk] * 4 + [_ANY] * len(r_ins), out_specs=[blk] * 3 + [_ANY] * len(r_outs),
        out_shape=[jax.ShapeDtypeStruct((r, c), f32)] * 3 + list(r_outs),
        scratch_shapes=rider.scratch if rider else [], compiler_params=_params(1),
    )(w, g, m, v, *r_ins)
    return res[:3], res[3:]


def sum_slots(x, name):
    n, r, c = x.shape
    rt = _tile(r, 128, SUB) if r % SUB == 0 else r
    blk = pl.BlockSpec((n, rt, c), lambda i: (0, i, 0))

    def body(x_ref, o_ref):
        acc = x_ref[0].astype(f32)
        for s in range(1, n):
            acc = acc + x_ref[s].astype(f32)
        o_ref[...] = acc

    return pl.pallas_call(
        body, name=name, grid=(r // rt,), in_specs=[blk], out_specs=pl.BlockSpec((rt, c), lambda i: (i, 0)),
        out_shape=jax.ShapeDtypeStruct((r, c), f32), compiler_params=_params(1),
    )(x)


def sum_chip_parts(parts, slots, chip, name):
    n, r, c = slots.shape
    rt = _tile(r, 128, SUB) if r % SUB == 0 else r

    def body(chip_ref, own_ref, a_ref, b_ref, c_ref, o_ref):
        o_ref[...] = ((own_ref[0].astype(f32) + a_ref[0].astype(f32)) + b_ref[0].astype(f32)) + c_ref[0].astype(f32)

    def at(rel):
        return pl.BlockSpec((1, rt, c), lambda i, chip_ref: (jnp.bitwise_xor(chip_ref[0], rel), i, 0))

    return pl.pallas_call(
        body, name=name,
        grid_spec=pltpu.PrefetchScalarGridSpec(
            num_scalar_prefetch=1, grid=(r // rt,), in_specs=[at(0), at(1), at(2), at(3)],
            out_specs=pl.BlockSpec((rt, c), lambda i, chip_ref: (i, 0))),
        out_shape=jax.ShapeDtypeStruct((r, c), f32), compiler_params=_params(1),
    )(jnp.reshape(chip, (1,)).astype(jnp.int32), parts, slots, slots, slots)


def add2(a, b, name, out_dtype=f32):
    n, r, c = a.shape
    rt = _tile(r, 64, SUB) if r % SUB == 0 else r
    blk = pl.BlockSpec((n, rt, c), lambda i: (0, i, 0))

    def body(a_ref, b_ref, o_ref):
        o_ref[...] = (a_ref[...] + b_ref[...]).astype(out_dtype)

    return pl.pallas_call(
        body, name=name, grid=(r // rt,), in_specs=[blk, blk], out_specs=blk,
        out_shape=jax.ShapeDtypeStruct((n, r, c), out_dtype), compiler_params=_params(1),
    )(a, b)


_ANY = pl.BlockSpec(memory_space=pl.ANY)
_MESH = pl.DeviceIdType.MESH


def _coords():
    return lax.axis_index("x"), lax.axis_index("y"), lax.axis_index("c")


def _chip_peer(x, y, r):
    return x ^ (r >> 1), y ^ (r & 1)


class _Exchange:
    def __init__(self, ins, out_shapes, scratch, start, finish):
        self.ins, self.out_shapes, self.scratch, self.start, self.finish = ins, out_shapes, scratch, start, finish

    def split(self, refs):
        n, m = len(self.ins), len(self.out_shapes)
        return refs[:n], refs[n:n + m], refs[n + m:]


def run_exchange(ex, name):
    def body(*refs):
        ins, outs, sems = ex.split(refs)
        ex.start(ins, outs, sems)
        ex.finish(ins, outs, sems)

    return pl.pallas_call(body, name=name, in_specs=[_ANY] * len(ex.ins), out_specs=[_ANY] * len(ex.out_shapes),
                          out_shape=ex.out_shapes, scratch_shapes=ex.scratch)(*ex.ins)


def gather_chips(big, small):
    nb, n = len(big), len(big) + len(small)
    shards = list(big) + list(small)
    kb = nb * (N_CHIPS - 1)
    k = n * (N_CHIPS - 1)

    def own(src, dst, sems, t):
        x, y, c = _coords()
        return pltpu.make_async_remote_copy(src[t], dst[t].at[2 * x + y], sems[4].at[t], sems[5].at[t],
                                            device_id=(x, y, 1 - c), device_id_type=_MESH)

    def copies(src, dst, sems):
        send, recv, fsend, frecv = sems[:4]
        x, y, c = _coords()
        sib = (x, y, 1 - c)
        peers = [_chip_peer(x, y, r) for r in range(1, N_CHIPS)]

        def direct(t, j, slot):
            s = t * (N_CHIPS - 1) + j
            if t < nb:
                return pltpu.make_async_remote_copy(src[t].at[c], dst[t].at[slot, c], send.at[s], recv.at[s],
                                                    device_id=(*peers[j], c), device_id_type=_MESH)
            return pltpu.make_async_remote_copy(src[t], dst[t].at[slot], send.at[s], recv.at[s],
                                                device_id=(*peers[j], c), device_id_type=_MESH)

        def passed(t, j, half):
            s = t * (N_CHIPS - 1) + j
            px, py = peers[j]
            part = dst[t].at[2 * px + py, half]
            return pltpu.make_async_remote_copy(part, part, fsend.at[s], frecv.at[s], device_id=sib, device_id_type=_MESH)

        return direct, passed, peers, 2 * x + y, c

    def start(src, dst, sems):
        direct, _, _, me, _ = copies(src, dst, sems)
        for t in range(n):
            for j in range(N_CHIPS - 1):
                direct(t, j, me).start()
        for t in range(n):
            own(src, dst, sems, t).start()

    def finish(src, dst, sems):
        direct, passed, peers, me, c = copies(src, dst, sems)
        for t in range(n):
            own(src, dst, sems, t).wait_recv()
        fwd = []
        for t in range(nb):
            for j in range(N_CHIPS - 1):
                px, py = peers[j]
                direct(t, j, 2 * px + py).wait_recv()
                fwd.append(passed(t, j, c))
                fwd[-1].start()
        for t in range(nb, n):
            for j in range(N_CHIPS - 1):
                px, py = peers[j]
                direct(t, j, 2 * px + py).wait_recv()
        for t in range(nb):
            for j in range(N_CHIPS - 1):
                passed(t, j, 1 - c).wait_recv()
        for t in range(n):
            for j in range(N_CHIPS - 1):
                direct(t, j, me).wait_send()
        for cp in fwd:
            cp.wait_send()
        for t in range(n):
            own(src, dst, sems, t).wait_send()

    return _Exchange(shards, [jax.ShapeDtypeStruct((N_CHIPS,) + s.shape, s.dtype) for s in shards],
                     [pltpu.SemaphoreType.DMA((k,)), pltpu.SemaphoreType.DMA((k,)),
                      pltpu.SemaphoreType.DMA((max(kb, 1),)), pltpu.SemaphoreType.DMA((max(kb, 1),)),
                      pltpu.SemaphoreType.DMA((n,)), pltpu.SemaphoreType.DMA((n,))], start, finish)


def sibling_swap(grads):
    pairs = [(t, o) for t, g in enumerate(grads) for o in range(g.shape[0])]
    k = len(pairs)

    def copies(src, dst, sems):
        send, recv = sems
        x, y, c = _coords()
        return [pltpu.make_async_remote_copy(src[t].at[o, 1 - c], dst[t].at[o], send.at[s], recv.at[s],
                                             device_id=(x, y, 1 - c), device_id_type=_MESH)
                for s, (t, o) in enumerate(pairs)]

    def start(src, dst, sems):
        for cp in copies(src, dst, sems):
            cp.start()

    def finish(src, dst, sems):
        cps = copies(src, dst, sems)
        for cp in cps:
            cp.wait_recv()
        for cp in cps:
            cp.wait_send()

    return _Exchange(list(grads), [jax.ShapeDtypeStruct((g.shape[0],) + g.shape[2:], g.dtype) for g in grads],
                     [pltpu.SemaphoreType.DMA((k,)), pltpu.SemaphoreType.DMA((k,))], start, finish)


def scatter_chips(parts):
    n = len(parts)
    k = n * (N_CHIPS - 1)

    def copy(src, dst, sems, t, r, landing):
        send, recv = sems
        x, y, c = _coords()
        me = 2 * x + y
        px, py = _chip_peer(x, y, r)
        peer = 2 * px + py
        s = t * (N_CHIPS - 1) + r - 1
        return pltpu.make_async_remote_copy(src[t].at[me if landing else peer], dst[t].at[peer if landing else me],
                                            send.at[s], recv.at[s], device_id=(px, py, c), device_id_type=_MESH)

    def start(src, dst, sems):
        for t in range(n):
            for r in range(1, N_CHIPS):
                copy(src, dst, sems, t, r, False).start()

    def finish(src, dst, sems):
        for t in range(n):
            for r in range(1, N_CHIPS):
                copy(src, dst, sems, t, r, True).wait_recv()
        for t in range(n):
            for r in range(1, N_CHIPS):
                copy(src, dst, sems, t, r, False).wait_send()

    return _Exchange(list(parts), [jax.ShapeDtypeStruct(p.shape, p.dtype) for p in parts],
                     [pltpu.SemaphoreType.DMA((k,)), pltpu.SemaphoreType.DMA((k,))], start, finish)


def sibling_send(halves):
    n = len(halves)

    def copies(src, dst, sems):
        send, recv = sems
        x, y, c = _coords()
        return [pltpu.make_async_remote_copy(src[t], dst[t], send.at[t], recv.at[t],
                                             device_id=(x, y, 1 - c), device_id_type=_MESH) for t in range(n)]

    def start(src, dst, sems):
        for cp in copies(src, dst, sems):
            cp.start()

    def finish(src, dst, sems):
        cps = copies(src, dst, sems)
        for cp in cps:
            cp.wait_recv()
        for cp in cps:
            cp.wait_send()

    return _Exchange(list(halves), [jax.ShapeDtypeStruct(h.shape, h.dtype) for h in halves],
                     [pltpu.SemaphoreType.DMA((n,)), pltpu.SemaphoreType.DMA((n,))], start, finish)


def gather_all(block):
    def copies(src, dst, sems, landing):
        send, recv, loc = sems
        x, y, c = _coords()
        me = 4 * x + 2 * y + c
        mine = pltpu.make_async_copy(src[0], dst[0].at[me], loc)
        remote = []
        for r in range(1, N_DEV):
            px, py, pc = x ^ (r >> 2), y ^ ((r >> 1) & 1), c ^ (r & 1)
            slot = 4 * px + 2 * py + pc if landing else me
            remote.append(pltpu.make_async_remote_copy(src[0], dst[0].at[slot], send.at[r - 1], recv.at[r - 1],
                                                       device_id=(px, py, pc), device_id_type=_MESH))
        return mine, remote

    def start(src, dst, sems):
        mine, outs = copies(src, dst, sems, False)
        mine.start()
        for cp in outs:
            cp.start()

    def finish(src, dst, sems):
        mine, lands = copies(src, dst, sems, True)
        for cp in lands:
            cp.wait_recv()
        for cp in lands:
            cp.wait_send()
        mine.wait()

    return _Exchange([block], [jax.ShapeDtypeStruct((N_DEV,) + block.shape, block.dtype)],
                     [pltpu.SemaphoreType.DMA((N_DEV - 1,)), pltpu.SemaphoreType.DMA((N_DEV - 1,)),
                      pltpu.SemaphoreType.DMA(())], start, finish)


def _pad_lanes(v, n=LANE):
    return jnp.pad(v, ((0, 0), (0, n - v.shape[1])))


def _w_in_pieces():
    cs = (PROJ_BIG + 2 * HEADS) // N_CHIPS
    ab_end = AB_COL + 2 * HEADS
    out = []
    for o in range(N_CHIPS):
        lo, hi = o * cs, (o + 1) * cs
        cand = [("big", lo, min(hi, AB_COL), 0), ("ab", max(lo, AB_COL), min(hi, ab_end), AB_COL),
                ("big", max(lo, ab_end), hi, 2 * HEADS)]
        out.append([(s, a - off, b - off) for s, a, b, off in cand if a < b])
    return out


def _split_w_in(w4):
    big, ab = [], []
    for o, pieces in enumerate(_w_in_pieces()):
        at = 0
        for s, a, b in pieces:
            (big if s == "big" else ab).append(w4[o][:, at:at + b - a])
            at += b - a
    return jnp.concatenate(big, axis=1), _pad_lanes(jnp.concatenate(ab, axis=1))


def _join_w_in(big, ab):
    src = {"big": big, "ab": ab}
    return jnp.stack([jnp.concatenate([src[s][:, a:b] for s, a, b in pieces], axis=1) for pieces in _w_in_pieces()])


def _conv_w8(w):
    return jnp.pad(w, ((0, SUB - DN_CONV), (0, 0)))


def _row_layout(gc, tp):
    nc = tp // CHUNK
    g = gc[:, :, 0].reshape(HEADS, nc, 1, CHUNK)
    g = jnp.broadcast_to(g, (HEADS, nc, SUB, CHUNK))
    return jnp.pad(g, ((0, 0), (0, 0), (0, 0), (0, LANE - CHUNK))).reshape(HEADS, nc * SUB, LANE)


def _step(x, meta, W, target, late_weights=None, early_swap=None, early_grads=None, last_grads=None, final_send=None):
    W = dict(W)
    seq = x.shape[0]
    tp = P0 + N_META + seq
    h0 = jnp.concatenate([jnp.zeros((P0, D_MODEL), f32), meta, x], axis=0)
    w_big, w_ab = _split_w_in(W["w_in"])
    cq8, ck8, cv8 = _conv_w8(W["conv_q"]), _conv_w8(W["conv_k"]), _conv_w8(W["conv_v"])
    al, dtb = _pad_lanes(W["dn_a_log"]), _pad_lanes(W["dn_dt_bias"])

    n1 = rms_fwd(h0, W["norm_mix_gain"], "rms1_fwd")
    proj = matmul(n1, w_big, "nn", "proj_fwd")
    pab = matmul(n1, w_ab, "nn", "pab_fwd")
    qn = conv_fwd(proj, cq8, C_DQ * 8, 8, True, "conv_q_fwd")
    kn = conv_fwd(proj, ck8, C_DK * 8, 8, True, "conv_k_fwd")
    va = conv_fwd(proj, cv8, C_DV * 8, 16, False, "conv_v_fwd")
    gc, bc = ab_fwd(pab, al, dtb)
    grow = _row_layout(gc, tp)
    o_dn, states, *kept = gdn_fwd(qn, kn, va, gc, bc, grow)
    on = dn_out_fwd(o_dn, proj, W["dn_out_norm_gain"])
    qs, ks, vs = sb_prep_fwd(proj, W["sb_q_norm_gain"], W["sb_k_norm_gain"])
    o_sb, o_sb16, arrived = sb_fwd(qs, ks, vs, rider=late_weights[0] if late_weights else None)
    if late_weights:
        W.update(late_weights[1](arrived))
    ydn = matmul(on, W["w_branch_dn"], "nn", "ydn_fwd")
    ysb = matmul(o_sb16, W["w_branch_sb"], "nn", "ysb_fwd")
    merged = merge_fwd(proj, ydn, ysb)
    h1 = matmul(merged, W["w_out"], "nn", "wout_fwd", residual=h0)
    n2 = rms_fwd(h1, W["norm_ffn_gain"], "rms2_fwd")
    u = matmul(n2, W["w_ffn_in"], "nn", "ffn_in_fwd", tn_t=512)
    act = swiglu_fwd(u)
    y = matmul(act, W["w_ffn_out"], "nn", "ffn_out_fwd", residual=h1)
    dy, dy16, loss = loss_head(y, target)

    G = {}
    dact = matmul(dy16, W["w_ffn_out"], "nt", "ffn_out_dx", tn_t=1408)
    G["w_ffn_out"] = matmul(act, dy16, "tn", "ffn_out_dw", tm_t=1408)
    dgate, dup = swiglu_bwd(u, dact)
    du = jnp.concatenate([dgate, dup], axis=1)
    dn2 = matmul(du, W["w_ffn_in"], "nt", "ffn_in_dx", tk_t=512)
    G["w_ffn_in"] = matmul(n2, du, "tn", "ffn_in_dw", tn_t=512)
    (dh1, dh1_16, G["norm_ffn_gain"]), _ = rms_bwd(h1, W["norm_ffn_gain"], dn2, dy, "rms2_bwd")
    dmerged = matmul(dh1_16, W["w_out"], "nt", "wout_dx")
    G["w_out"] = matmul(merged, dh1_16, "tn", "wout_dw")
    dyd, dys, d_gates = merge_bwd(proj, ydn, ysb, dmerged)
    don = matmul(dyd, W["w_branch_dn"], "nt", "ydn_dx")
    G["w_branch_dn"] = matmul(on, dyd, "tn", "ydn_dw")
    do_sb = matmul(dys, W["w_branch_sb"], "nt", "ysb_dx")
    G["w_branch_sb"] = matmul(o_sb16, dys, "tn", "ysb_dw")
    do_dn, dproj, G["dn_out_norm_gain"] = dn_out_bwd(o_dn, proj, W["dn_out_norm_gain"], don)
    (dqn, dkn, dva, dgc, dbc), swapped = gdn_bwd(qn, kn, va, gc, bc, grow, states, kept, do_dn,
                                                 rider=early_swap[0](G) if early_swap else None)
    if early_swap:
        early_swap[1](swapped)
    dpab, dal, ddt = ab_bwd(pab, al, dtb, dgc, dbc)
    G["dn_a_log"], G["dn_dt_bias"] = dal[:, :HEADS], ddt[:, :HEADS]
    dproj, dcq = conv_bwd(proj, cq8, dqn, dproj, C_DQ * 8, 8, True, "conv_q_bwd")
    dproj, dck = conv_bwd(proj, ck8, dkn, dproj, C_DK * 8, 8, True, "conv_k_bwd")
    dproj, dcv = conv_bwd(proj, cv8, dva, dproj, C_DV * 8, 16, False, "conv_v_bwd")
    G["conv_q"], G["conv_k"], G["conv_v"] = dcq[:DN_CONV], dck[:DN_CONV], dcv[:DN_CONV]
    (dqs, dks, dvs), delivered = sb_bwd(qs, ks, vs, o_sb, do_sb, rider=early_grads[0](G) if early_grads else None)
    if early_grads:
        early_grads[1](delivered)
    dproj, G["sb_q_norm_gain"], G["sb_k_norm_gain"] = sb_prep_bwd(
        proj, W["sb_q_norm_gain"], W["sb_k_norm_gain"], dqs, dks, dvs, dproj)
    dproj = lax.dynamic_update_slice(dproj, d_gates, (0, C_GDN * 1024))
    dw_big = matmul(n1, dproj, "tn", "proj_dw")
    dw_ab = matmul(n1, dpab, "tn", "pab_dw")
    G["w_in"] = (dw_big, dw_ab)
    if last_grads:
        dn1, delivered = matmul(dproj, w_big, "nt", "proj_dx", tk_t=1024, rider=last_grads[0](G))
        last_grads[1](delivered)
    else:
        dn1 = matmul(dproj, w_big, "nt", "proj_dx", tk_t=1024)
    dn1 = matmul(dpab, w_ab, "nt", "pab_dx", residual=dn1)
    (dh0, _, G["norm_mix_gain"]), sent = rms_bwd(h0, W["norm_mix_gain"], dn1, dh1, "rms1_bwd",
                                                 rider=final_send[0]() if final_send else None)
    if final_send:
        final_send[1](sent)
    G["meta_tokens"] = dh0[P0:P0 + N_META]
    return loss, dh0[P0 + N_META:], G


_BIG = ("w_in", "w_branch_dn", "w_branch_sb", "w_out", "w_ffn_in", "w_ffn_out")
_COL_SHARDED = ("w_in", "w_ffn_in", "meta_tokens", "conv_q", "conv_k", "conv_v")
_SMALL_REPL = ("norm_mix_gain", "norm_ffn_gain", "dn_a_log", "dn_dt_bias", "dn_out_norm_gain", "sb_q_norm_gain",
               "sb_k_norm_gain")
_SMALL_SHARD = ("meta_tokens", "conv_q", "conv_k", "conv_v")
_ORDER = ("meta_tokens", "norm_mix_gain", "w_in", "conv_q", "conv_k", "conv_v", "dn_a_log", "dn_dt_bias",
          "dn_out_norm_gain", "sb_q_norm_gain", "sb_k_norm_gain", "w_branch_dn", "w_branch_sb", "w_out",
          "norm_ffn_gain", "w_ffn_in", "w_ffn_out")


def _unshard(g4, name):
    if name in _COL_SHARDED:
        r, cs = g4.shape[1:]
        return jnp.transpose(g4, (1, 0, 2)).reshape(r, N_CHIPS * cs)
    return g4.reshape((-1,) + g4.shape[2:])


def _to_shards(full, name):
    if name in _COL_SHARDED:
        r, c = full.shape
        return jnp.transpose(full.reshape(r, N_CHIPS, c // N_CHIPS), (1, 0, 2))
    r, c = full.shape
    return full.reshape(N_CHIPS, r // N_CHIPS, c)


def _rows_1024(a):
    r, c = a.shape
    if c >= 1024:
        return a.reshape(r * (c // 1024), 1024)
    return jnp.pad(a, ((0, 0), (0, 1024 - c)))


def kernel(x, meta_tokens, norm_mix_gain, w_in, conv_q, conv_k, conv_v, dn_a_log, dn_dt_bias, dn_out_norm_gain, sb_q_norm_gain, sb_k_norm_gain, w_branch_dn, w_branch_sb, w_out, norm_ffn_gain, w_ffn_in, w_ffn_out, loss_target, m_meta_tokens, m_norm_mix_gain, m_w_in, m_conv_q, m_conv_k, m_conv_v, m_dn_a_log, m_dn_dt_bias, m_dn_out_norm_gain, m_sb_q_norm_gain, m_sb_k_norm_gain, m_w_branch_dn, m_w_branch_sb, m_w_out, m_norm_ffn_gain, m_w_ffn_in, m_w_ffn_out, v_meta_tokens, v_norm_mix_gain, v_w_in, v_conv_q, v_conv_k, v_conv_v, v_dn_a_log, v_dn_dt_bias, v_dn_out_norm_gain, v_sb_q_norm_gain, v_sb_k_norm_gain, v_w_branch_dn, v_w_branch_sb, v_w_out, v_norm_ffn_gain, v_w_ffn_in, v_w_ffn_out):
    Wl = dict(meta_tokens=meta_tokens, norm_mix_gain=norm_mix_gain, w_in=w_in[0], conv_q=conv_q[0], conv_k=conv_k[0],
              conv_v=conv_v[0], dn_a_log=dn_a_log, dn_dt_bias=dn_dt_bias, dn_out_norm_gain=dn_out_norm_gain,
              sb_q_norm_gain=sb_q_norm_gain, sb_k_norm_gain=sb_k_norm_gain, w_branch_dn=w_branch_dn[0],
              w_branch_sb=w_branch_sb[0], w_out=w_out[0], norm_ffn_gain=norm_ffn_gain, w_ffn_in=w_ffn_in[0],
              w_ffn_out=w_ffn_out[0])
    Ml = dict(meta_tokens=m_meta_tokens, norm_mix_gain=m_norm_mix_gain, w_in=m_w_in[0], conv_q=m_conv_q[0],
              conv_k=m_conv_k[0], conv_v=m_conv_v[0], dn_a_log=m_dn_a_log, dn_dt_bias=m_dn_dt_bias,
              dn_out_norm_gain=m_dn_out_norm_gain, sb_q_norm_gain=m_sb_q_norm_gain, sb_k_norm_gain=m_sb_k_norm_gain,
              w_branch_dn=m_w_branch_dn[0], w_branch_sb=m_w_branch_sb[0], w_out=m_w_out[0],
              norm_ffn_gain=m_norm_ffn_gain, w_ffn_in=m_w_ffn_in[0], w_ffn_out=m_w_ffn_out[0])
    Vl = dict(meta_tokens=v_meta_tokens, norm_mix_gain=v_norm_mix_gain, w_in=v_w_in[0], conv_q=v_conv_q[0],
              conv_k=v_conv_k[0], conv_v=v_conv_v[0], dn_a_log=v_dn_a_log, dn_dt_bias=v_dn_dt_bias,
              dn_out_norm_gain=v_dn_out_norm_gain, sb_q_norm_gain=v_sb_q_norm_gain, sb_k_norm_gain=v_sb_k_norm_gain,
              w_branch_dn=v_w_branch_dn[0], w_branch_sb=v_w_branch_sb[0], w_out=v_w_out[0],
              norm_ffn_gain=v_norm_ffn_gain, w_ffn_in=v_w_ffn_in[0], w_ffn_out=v_w_ffn_out[0])
    lead = {n: (1,) if (n in _BIG or n in ("conv_q", "conv_k", "conv_v")) else () for n in _ORDER}

    chip = 2 * lax.axis_index("x") + lax.axis_index("y")
    c = lax.axis_index("c")
    halved = {n: Wl[n].astype(bf16).reshape(2, Wl[n].shape[0] // 2, Wl[n].shape[1]) for n in _BIG}

    def gathered_weights(names, owns, outs):
        res = {}
        for n, g4 in zip(names, outs):
            if n in _BIG:
                g4 = g4.reshape(N_CHIPS, 2 * g4.shape[2], g4.shape[3])
            res[n] = g4 if n == "w_in" else _unshard(g4, n)
        return res

    first = ["w_in"] + list(_SMALL_SHARD)
    first_own = [halved["w_in"]] + [Wl[n] for n in _SMALL_SHARD]
    W = dict(Wl)
    W.update(gathered_weights(first, first_own, run_exchange(gather_chips(first_own[:1], first_own[1:]), "gather_w_in")))
    late = [n for n in _BIG if n != "w_in"]
    late_own = [halved[n] for n in late]
    for n in late:
        del W[n]

    def halves_of(names, G):
        g4 = [_to_shards(G[n], n) for n in names]
        return [g.reshape(N_CHIPS, 2, g.shape[1] // 2, g.shape[2]) for g in g4]

    def pair_added(g42, from_sib, tag, wire):
        mine = [lax.dynamic_index_in_dim(g, c, axis=1, keepdims=False) for g in g42]
        return [add2(a, b, "grad_pair_add_%s%d" % (tag, t), out_dtype=wire)
                for t, (a, b) in enumerate(zip(mine, from_sib))]

    def chip_reduced(parts, slots, tag):
        return [sum_chip_parts(p, s, chip, "grad_chip_sum_%s%d" % (tag, t)) for t, (p, s) in enumerate(zip(parts, slots))]

    early, last = {}, {}

    def early_swap_begin(G):
        early["g42"] = halves_of(late, G)
        return sibling_swap(early["g42"])

    def early_begin(G):
        early["parts"] = pair_added(early["g42"], early["from_sib"], "a", f32)
        return scatter_chips(early["parts"])

    def last_begin(G):
        g2 = [g.reshape(1, 2, g.shape[0] // 2, g.shape[1]) for g in G["w_in"]]
        added = pair_added(g2, run_exchange(sibling_swap(g2), "grad_sibling_swap_b"), "b", bf16)
        last["parts"] = [_join_w_in(added[0][0], added[1][0])]
        return scatter_chips(last["parts"])

    loss, grad_x, G = _step(
        x[0], W["meta_tokens"], W, loss_target[0],
        late_weights=(gather_chips(late_own, []), lambda outs: gathered_weights(late, late_own, outs)),
        early_swap=(early_swap_begin, lambda outs: early.update(from_sib=outs)),
        early_grads=(early_begin, lambda slots: early.update(halves=chip_reduced(early["parts"], slots, "a"))),
        last_grads=(last_begin, lambda slots: last.update(halves=chip_reduced(last["parts"], slots, "b"))),
        final_send=(lambda: sibling_send(last["halves"] + early["halves"]), lambda outs: last.update(theirs=outs)))
    Gs = {}
    for n, h, o in zip(["w_in"] + late, last["halves"] + early["halves"], last["theirs"]):
        Gs[n] = lax.dynamic_update_slice(jnp.concatenate([o, o], axis=0), h, (c * h.shape[0], 0))

    small_names = list(_SMALL_REPL) + list(_SMALL_SHARD)
    pieces = [_rows_1024(G[n]) for n in small_names] + [_rows_1024(loss)]
    counts = [p.shape[0] for p in pieces]
    pack = jnp.concatenate(pieces, axis=0)
    pad_rows = (-pack.shape[0]) % SUB
    pack = jnp.pad(pack, ((0, pad_rows), (0, 0)))
    adam = {"w_in": adamw(Wl["w_in"], Gs["w_in"], Ml["w_in"], Vl["w_in"], "adamw_w_in", rider=gather_all(pack))}
    total = sum_slots(adam["w_in"][1][0], "small_sum")
    row = 0
    for n, cnt in zip(small_names, counts[:-1]):
        blk = total[row:row + cnt]
        row += cnt
        full_shape = G[n].shape
        if full_shape[1] >= 1024:
            blk = blk.reshape(full_shape)
        else:
            blk = blk[:, :full_shape[1]]
        if n in _SMALL_SHARD:
            cs = full_shape[1] // N_CHIPS
            blk = lax.dynamic_slice_in_dim(blk, chip * cs, cs, axis=1)
        Gs[n] = blk
    loss_out = total[row, 0]

    grads, deltas, new_m, new_v = [], [], [], []
    for n in _ORDER:
        d, m2, v2 = (adam[n] if n in adam else adamw(Wl[n], Gs[n], Ml[n], Vl[n], "adamw_" + n))[0]
        shape = lead[n] + Wl[n].shape
        grads.append(Gs[n].reshape(shape))
        deltas.append(d.reshape(shape))
        new_m.append(m2.reshape(shape))
        new_v.append(v2.reshape(shape))
    return (loss_out, grad_x[None], *grads, *deltas, *new_m, *new_v)
```
